```python
import jax, jax.numpy as jnp
from jax import lax
import numpy as np

D_MODEL = 1024
BATCH = 16
SEQ = 2048
DEPTH = 2

N_A_LAYERS = DEPTH // 2
N_B_LAYERS = DEPTH - N_A_LAYERS

CHUNK = 64
EPS = 1e-6
NEG_INF = -1e30

HEADS_A = 16
HEAD_DIM_A = D_MODEL // HEADS_A
LEFT_CHUNKS = 8
BAND = (LEFT_CHUNKS + 1) * CHUNK
MAX_REL = 128

HEADS_B = D_MODEL // 128
NOPE_DIM = 128
ROPE_DIM = 64
V_DIM = 128
Q_LORA = 768
KV_LORA = 256
ROPE_THETA = 10000.0
Q_BLOCK = 128

D_FF = ((8 * D_MODEL // 3 + 127) // 128) * 128

kernel_name = "yoco_chunked_relpos_mla_macaron"


def rms_norm(x, g):
    xf = x.astype(jnp.float32)
    y = xf * lax.rsqrt(jnp.mean(xf * xf, axis=-1, keepdims=True) + EPS)
    return (y * g.astype(jnp.float32)).astype(x.dtype)


def swiglu(h, w_in, w_out):
    u = h @ w_in
    return (jax.nn.silu(u[..., :D_FF]) * u[..., D_FF:]) @ w_out


def rope_tables(seq_len):
    half = ROPE_DIM // 2
    freqs = ROPE_THETA ** (-jnp.arange(half, dtype=jnp.float32) / half)
    ang = jnp.arange(seq_len, dtype=jnp.float32)[:, None] * freqs[None, :]
    return jnp.cos(ang), jnp.sin(ang)


def apply_rope(x, cos, sin):
    half = ROPE_DIM // 2
    cos = cos.astype(x.dtype)
    sin = sin.astype(x.dtype)
    x1, x2 = x[..., :half], x[..., half:]
    return jnp.concatenate([x1 * cos - x2 * sin, x2 * cos + x1 * sin], axis=-1)


def chunked_relpos_attention(h, w_qkv, w_o, rel_table):
    B, S, _ = h.shape
    nc = S // CHUNK
    qkv = (h @ w_qkv).reshape(B, S, 3, HEADS_A, HEAD_DIM_A)
    q, k, v = qkv[:, :, 0], qkv[:, :, 1], qkv[:, :, 2]
    pad = LEFT_CHUNKS * CHUNK
    kp = jnp.pad(k, ((0, 0), (pad, 0), (0, 0), (0, 0)))
    vp = jnp.pad(v, ((0, 0), (pad, 0), (0, 0), (0, 0)))
    qi = jnp.arange(CHUNK)[:, None]
    kj = jnp.arange(BAND)[None, :]
    rel_idx = jnp.clip(pad + qi - kj, -MAX_REL, MAX_REL) + MAX_REL
    bias = rel_table[:, rel_idx].astype(jnp.float32)
    q_chunks = jnp.moveaxis(q.reshape(B, nc, CHUNK, HEADS_A, HEAD_DIM_A), 1, 0)
    scale = HEAD_DIM_A ** -0.5

    def one_chunk(args):
        qc, c = args
        start = c * CHUNK
        kb = lax.dynamic_slice_in_dim(kp, start, BAND, axis=1)
        vb = lax.dynamic_slice_in_dim(vp, start, BAND, axis=1)
        s = jnp.einsum('bqhd,bkhd->bhqk', qc, kb).astype(jnp.float32) * scale + bias
        valid = kj >= pad - start
        s = jnp.where(valid[None, None], s, NEG_INF)
        p = jax.nn.softmax(s, axis=-1).astype(vb.dtype)
        return jnp.einsum('bhqk,bkhd->bqhd', p, vb)

    out = lax.map(one_chunk, (q_chunks, jnp.arange(nc)))
    out = jnp.moveaxis(out, 0, 1).reshape(B, S, HEADS_A * HEAD_DIM_A)
    return out @ w_o


def mla_shared_kv(h_kv, w_down, latent_norm, w_up, cos, sin):
    B, S, _ = h_kv.shape
    ckr = h_kv @ w_down
    c_kv = rms_norm(ckr[..., :KV_LORA], latent_norm)
    k_rope = apply_rope(ckr[..., KV_LORA:], cos, sin)
    kv = (c_kv @ w_up).reshape(B, S, HEADS_B, NOPE_DIM + V_DIM)
    return kv[..., :NOPE_DIM], k_rope, kv[..., NOPE_DIM:]


def mla_attention(h, w_dq, q_norm, w_uq, w_o, k_nope, k_rope, v, cos, sin):
    B, S, _ = h.shape
    cq = rms_norm(h @ w_dq, q_norm)
    q = (cq @ w_uq).reshape(B, S, HEADS_B, NOPE_DIM + ROPE_DIM)
    q_nope = q[..., :NOPE_DIM]
    q_rope = apply_rope(q[..., NOPE_DIM:], cos[:, None], sin[:, None])
    nb = S // Q_BLOCK
    qn_blocks = jnp.moveaxis(q_nope.reshape(B, nb, Q_BLOCK, HEADS_B, NOPE_DIM), 1, 0)
    qr_blocks = jnp.moveaxis(q_rope.reshape(B, nb, Q_BLOCK, HEADS_B, ROPE_DIM), 1, 0)
    key_chunk = jnp.arange(S) // CHUNK
    scale = (NOPE_DIM + ROPE_DIM) ** -0.5

    def one_block(args):
        qn, qr, bidx = args
        s = (jnp.einsum('bqhd,bkhd->bhqk', qn, k_nope)
             + jnp.einsum('bqhr,bkr->bhqk', qr, k_rope)).astype(jnp.float32) * scale
        q_chunk = (bidx * Q_BLOCK + jnp.arange(Q_BLOCK)) // CHUNK
        mask = key_chunk[None, :] <= q_chunk[:, None]
        s = jnp.where(mask[None, None], s, NEG_INF)
        p = jax.nn.softmax(s, axis=-1).astype(v.dtype)
        return jnp.einsum('bhqk,bkhd->bqhd', p, v)

    out = lax.map(one_block, (qn_blocks, qr_blocks, jnp.arange(nb)))
    out = jnp.moveaxis(out, 0, 1).reshape(B, S, HEADS_B * V_DIM)
    return out @ w_o


def _fwd_setup_inputs(seed: int = 0) -> dict:
    key = jax.random.key(seed)
    ks = jax.random.split(key, 24)

    def w(k, shape, fan_in):
        return jax.random.normal(k, shape, jnp.float32) * fan_in ** -0.5

    def gain(k, shape):
        return 1.0 + 0.05 * jax.random.normal(k, shape, jnp.float32)

    return {
        "x": jax.random.normal(ks[0], (BATCH, SEQ, D_MODEL), jnp.float32),
        "ffn1_norm": gain(ks[1], (DEPTH, D_MODEL)),
        "ffn1_w_in": w(ks[2], (DEPTH, D_MODEL, 2 * D_FF), D_MODEL),
        "ffn1_w_out": w(ks[3], (DEPTH, D_FF, D_MODEL), D_FF),
        "mix_norm": gain(ks[4], (DEPTH, D_MODEL)),
        "ffn2_norm": gain(ks[5], (DEPTH, D_MODEL)),
        "ffn2_w_in": w(ks[6], (DEPTH, D_MODEL, 2 * D_FF), D_MODEL),
        "ffn2_w_out": w(ks[7], (DEPTH, D_FF, D_MODEL), D_FF),
        "a_w_qkv": w(ks[8], (N_A_LAYERS, D_MODEL, 3 * HEADS_A * HEAD_DIM_A), D_MODEL),
        "a_rel_bias": 0.5 * jax.random.normal(ks[9], (N_A_LAYERS, HEADS_A, 2 * MAX_REL + 1), jnp.float32),
        "a_w_o": w(ks[10], (N_A_LAYERS, HEADS_A * HEAD_DIM_A, D_MODEL), HEADS_A * HEAD_DIM_A),
        "kv_norm": gain(ks[11], (D_MODEL,)),
        "kv_w_down": w(ks[12], (D_MODEL, KV_LORA + ROPE_DIM), D_MODEL),
        "kv_latent_norm": gain(ks[13], (KV_LORA,)),
        "kv_w_up": w(ks[14], (KV_LORA, HEADS_B * (NOPE_DIM + V_DIM)), KV_LORA),
        "b_w_dq": w(ks[15], (N_B_LAYERS, D_MODEL, Q_LORA), D_MODEL),
        "b_q_norm": gain(ks[16], (N_B_LAYERS, Q_LORA)),
        "b_w_uq": w(ks[17], (N_B_LAYERS, Q_LORA, HEADS_B * (NOPE_DIM + ROPE_DIM)), Q_LORA),
        "b_w_o": w(ks[18], (N_B_LAYERS, HEADS_B * V_DIM, D_MODEL), HEADS_B * V_DIM),
        "final_norm": gain(ks[19], (D_MODEL,)),
    }


def _fwd_reference(x, ffn1_norm, ffn1_w_in, ffn1_w_out, mix_norm, ffn2_norm, ffn2_w_in,
              ffn2_w_out, a_w_qkv, a_rel_bias, a_w_o, kv_norm, kv_w_down,
              kv_latent_norm, kv_w_up, b_w_dq, b_q_norm, b_w_uq, b_w_o, final_norm):
    S = x.shape[1]
    cos, sin = rope_tables(S)
    h = x
    k_nope = k_rope = v_shared = None
    for layer in range(DEPTH):
        h = h + 0.5 * swiglu(rms_norm(h, ffn1_norm[layer]), ffn1_w_in[layer], ffn1_w_out[layer])
        hn = rms_norm(h, mix_norm[layer])
        if layer < N_A_LAYERS:
            h = h + chunked_relpos_attention(hn, a_w_qkv[layer], a_w_o[layer], a_rel_bias[layer])
        else:
            li = layer - N_A_LAYERS
            h = h + mla_attention(hn, b_w_dq[li], b_q_norm[li], b_w_uq[li], b_w_o[li],
                                  k_nope, k_rope, v_shared, cos, sin)
        h = h + 0.5 * swiglu(rms_norm(h, ffn2_norm[layer]), ffn2_w_in[layer], ffn2_w_out[layer])
        if layer == N_A_LAYERS - 1:
            k_nope, k_rope, v_shared = mla_shared_kv(rms_norm(h, kv_norm), kv_w_down,
                                                     kv_latent_norm, kv_w_up, cos, sin)
    return rms_norm(h, final_norm)


import jax as _jax
import jax.numpy as _jnp

TWIN_FORMAT = 'train_step'
FWD_PARAMS = ['x', 'ffn1_norm', 'ffn1_w_in', 'ffn1_w_out', 'mix_norm', 'ffn2_norm', 'ffn2_w_in', 'ffn2_w_out', 'a_w_qkv', 'a_rel_bias', 'a_w_o', 'kv_norm', 'kv_w_down', 'kv_latent_norm', 'kv_w_up', 'b_w_dq', 'b_q_norm', 'b_w_uq', 'b_w_o', 'final_norm']
TWIN_WEIGHTS = ['ffn1_norm', 'ffn1_w_in', 'ffn1_w_out', 'mix_norm', 'ffn2_norm', 'ffn2_w_in', 'ffn2_w_out', 'a_w_qkv', 'a_rel_bias', 'a_w_o', 'kv_norm', 'kv_w_down', 'kv_latent_norm', 'kv_w_up', 'b_w_dq', 'b_q_norm', 'b_w_uq', 'b_w_o', 'final_norm']
TWIN_DIFF_INPUT = 'x'
TWIN_INPUTS = ['x', 'ffn1_norm', 'ffn1_w_in', 'ffn1_w_out', 'mix_norm', 'ffn2_norm', 'ffn2_w_in', 'ffn2_w_out', 'a_w_qkv', 'a_rel_bias', 'a_w_o', 'kv_norm', 'kv_w_down', 'kv_latent_norm', 'kv_w_up', 'b_w_dq', 'b_q_norm', 'b_w_uq', 'b_w_o', 'final_norm', 'loss_target', 'm_ffn1_norm', 'm_ffn1_w_in', 'm_ffn1_w_out', 'm_mix_norm', 'm_ffn2_norm', 'm_ffn2_w_in', 'm_ffn2_w_out', 'm_a_w_qkv', 'm_a_rel_bias', 'm_a_w_o', 'm_kv_norm', 'm_kv_w_down', 'm_kv_latent_norm', 'm_kv_w_up', 'm_b_w_dq', 'm_b_q_norm', 'm_b_w_uq', 'm_b_w_o', 'm_final_norm', 'v_ffn1_norm', 'v_ffn1_w_in', 'v_ffn1_w_out', 'v_mix_norm', 'v_ffn2_norm', 'v_ffn2_w_in', 'v_ffn2_w_out', 'v_a_w_qkv', 'v_a_rel_bias', 'v_a_w_o', 'v_kv_norm', 'v_kv_w_down', 'v_kv_latent_norm', 'v_kv_w_up', 'v_b_w_dq', 'v_b_q_norm', 'v_b_w_uq', 'v_b_w_o', 'v_final_norm']
TWIN_OUTPUTS = ['loss', 'grad_x', 'grad_ffn1_norm', 'grad_ffn1_w_in', 'grad_ffn1_w_out', 'grad_mix_norm', 'grad_ffn2_norm', 'grad_ffn2_w_in', 'grad_ffn2_w_out', 'grad_a_w_qkv', 'grad_a_rel_bias', 'grad_a_w_o', 'grad_kv_norm', 'grad_kv_w_down', 'grad_kv_latent_norm', 'grad_kv_w_up', 'grad_b_w_dq', 'grad_b_q_norm', 'grad_b_w_uq', 'grad_b_w_o', 'grad_final_norm', 'delta_ffn1_norm', 'delta_ffn1_w_in', 'delta_ffn1_w_out', 'delta_mix_norm', 'delta_ffn2_norm', 'delta_ffn2_w_in', 'delta_ffn2_w_out', 'delta_a_w_qkv', 'delta_a_rel_bias', 'delta_a_w_o', 'delta_kv_norm', 'delta_kv_w_down', 'delta_kv_latent_norm', 'delta_kv_w_up', 'delta_b_w_dq', 'delta_b_q_norm', 'delta_b_w_uq', 'delta_b_w_o', 'delta_final_norm', 'new_m_ffn1_norm', 'new_m_ffn1_w_in', 'new_m_ffn1_w_out', 'new_m_mix_norm', 'new_m_ffn2_norm', 'new_m_ffn2_w_in', 'new_m_ffn2_w_out', 'new_m_a_w_qkv', 'new_m_a_rel_bias', 'new_m_a_w_o', 'new_m_kv_norm', 'new_m_kv_w_down', 'new_m_kv_latent_norm', 'new_m_kv_w_up', 'new_m_b_w_dq', 'new_m_b_q_norm', 'new_m_b_w_uq', 'new_m_b_w_o', 'new_m_final_norm', 'new_v_ffn1_norm', 'new_v_ffn1_w_in', 'new_v_ffn1_w_out', 'new_v_mix_norm', 'new_v_ffn2_norm', 'new_v_ffn2_w_in', 'new_v_ffn2_w_out', 'new_v_a_w_qkv', 'new_v_a_rel_bias', 'new_v_a_w_o', 'new_v_kv_norm', 'new_v_kv_w_down', 'new_v_kv_latent_norm', 'new_v_kv_w_up', 'new_v_b_w_dq', 'new_v_b_q_norm', 'new_v_b_w_uq', 'new_v_b_w_o', 'new_v_final_norm']
TWIN_LEAF_KINDS = {'loss': 'loss', 'grad_x': 'grad_x', 'grad_ffn1_norm': 'grad_w', 'grad_ffn1_w_in': 'grad_w', 'grad_ffn1_w_out': 'grad_w', 'grad_mix_norm': 'grad_w', 'grad_ffn2_norm': 'grad_w', 'grad_ffn2_w_in': 'grad_w', 'grad_ffn2_w_out': 'grad_w', 'grad_a_w_qkv': 'grad_w', 'grad_a_rel_bias': 'grad_w', 'grad_a_w_o': 'grad_w', 'grad_kv_norm': 'grad_w', 'grad_kv_w_down': 'grad_w', 'grad_kv_latent_norm': 'grad_w', 'grad_kv_w_up': 'grad_w', 'grad_b_w_dq': 'grad_w', 'grad_b_q_norm': 'grad_w', 'grad_b_w_uq': 'grad_w', 'grad_b_w_o': 'grad_w', 'grad_final_norm': 'grad_w', 'delta_ffn1_norm': 'delta_w', 'delta_ffn1_w_in': 'delta_w', 'delta_ffn1_w_out': 'delta_w', 'delta_mix_norm': 'delta_w', 'delta_ffn2_norm': 'delta_w', 'delta_ffn2_w_in': 'delta_w', 'delta_ffn2_w_out': 'delta_w', 'delta_a_w_qkv': 'delta_w', 'delta_a_rel_bias': 'delta_w', 'delta_a_w_o': 'delta_w', 'delta_kv_norm': 'delta_w', 'delta_kv_w_down': 'delta_w', 'delta_kv_latent_norm': 'delta_w', 'delta_kv_w_up': 'delta_w', 'delta_b_w_dq': 'delta_w', 'delta_b_q_norm': 'delta_w', 'delta_b_w_uq': 'delta_w', 'delta_b_w_o': 'delta_w', 'delta_final_norm': 'delta_w', 'new_m_ffn1_norm': 'new_m', 'new_m_ffn1_w_in': 'new_m', 'new_m_ffn1_w_out': 'new_m', 'new_m_mix_norm': 'new_m', 'new_m_ffn2_norm': 'new_m', 'new_m_ffn2_w_in': 'new_m', 'new_m_ffn2_w_out': 'new_m', 'new_m_a_w_qkv': 'new_m', 'new_m_a_rel_bias': 'new_m', 'new_m_a_w_o': 'new_m', 'new_m_kv_norm': 'new_m', 'new_m_kv_w_down': 'new_m', 'new_m_kv_latent_norm': 'new_m', 'new_m_kv_w_up': 'new_m', 'new_m_b_w_dq': 'new_m', 'new_m_b_q_norm': 'new_m', 'new_m_b_w_uq': 'new_m', 'new_m_b_w_o': 'new_m', 'new_m_final_norm': 'new_m', 'new_v_ffn1_norm': 'new_v', 'new_v_ffn1_w_in': 'new_v', 'new_v_ffn1_w_out': 'new_v', 'new_v_mix_norm': 'new_v', 'new_v_ffn2_norm': 'new_v', 'new_v_ffn2_w_in': 'new_v', 'new_v_ffn2_w_out': 'new_v', 'new_v_a_w_qkv': 'new_v', 'new_v_a_rel_bias': 'new_v', 'new_v_a_w_o': 'new_v', 'new_v_kv_norm': 'new_v', 'new_v_kv_w_down': 'new_v', 'new_v_kv_latent_norm': 'new_v', 'new_v_kv_w_up': 'new_v', 'new_v_b_w_dq': 'new_v', 'new_v_b_q_norm': 'new_v', 'new_v_b_w_uq': 'new_v', 'new_v_b_w_o': 'new_v', 'new_v_final_norm': 'new_v'}


def _forward(args):
    return _fwd_reference(*[args[k] for k in FWD_PARAMS])


def _output_shape():
    out = _jax.eval_shape(lambda: _forward(_fwd_setup_inputs(0)))
    return out.shape, out.dtype

N_MICROBATCH = 1
ADAM_LR = 0.001
ADAM_B1 = 0.9
ADAM_B2 = 0.999
ADAM_EPS = 1e-08
ADAM_WD = 0.01
ADAM_STEP = 10
PER_EXAMPLE_BATCH_AXIS = {'x': 0, 'loss_target': 0}
SHARED_INPUTS = []
_WEIGHT_DTYPES = {'ffn1_norm': _jnp.float32, 'ffn1_w_in': _jnp.float32, 'ffn1_w_out': _jnp.float32, 'mix_norm': _jnp.float32, 'ffn2_norm': _jnp.float32, 'ffn2_w_in': _jnp.float32, 'ffn2_w_out': _jnp.float32, 'a_w_qkv': _jnp.float32, 'a_rel_bias': _jnp.float32, 'a_w_o': _jnp.float32, 'kv_norm': _jnp.float32, 'kv_w_down': _jnp.float32, 'kv_latent_norm': _jnp.float32, 'kv_w_up': _jnp.float32, 'b_w_dq': _jnp.float32, 'b_q_norm': _jnp.float32, 'b_w_uq': _jnp.float32, 'b_w_o': _jnp.float32, 'final_norm': _jnp.float32}
MOMENT_SCALE = {'ffn1_norm': 7.711781e-02, 'ffn1_w_in': 3.290293e-02, 'ffn1_w_out': 5.355782e-02, 'mix_norm': 4.047687e-02, 'ffn2_norm': 7.402668e-02, 'ffn2_w_in': 3.030647e-02, 'ffn2_w_out': 4.961762e-02, 'a_w_qkv': 3.057967e-02, 'a_rel_bias': 1.406168e-02, 'a_w_o': 3.362213e-02, 'kv_norm': 3.678256e-02, 'kv_w_down': 6.381335e-02, 'kv_latent_norm': 6.725588e-02, 'kv_w_up': 2.460422e-02, 'b_w_dq': 2.734734e-02, 'b_q_norm': 2.745782e-02, 'b_w_uq': 1.931339e-02, 'b_w_o': 2.842557e-02, 'final_norm': 3.198547e+01}


def _to_microbatches(a, axis):
    t = _jnp.moveaxis(a, axis, 0)
    t = t.reshape((N_MICROBATCH, t.shape[0] // N_MICROBATCH) + t.shape[1:])
    return _jnp.moveaxis(t, 1, axis + 1)


def setup_inputs(seed: int = 0) -> dict:
    inp = _fwd_setup_inputs(seed)
    key = _jax.random.fold_in(_jax.random.key(seed), 7919)
    shape, _ = _output_shape()
    out = dict(inp)
    out["loss_target"] = _jax.random.normal(_jax.random.fold_in(key, 0), shape, _jnp.float32)
    for i, name in enumerate(TWIN_WEIGHTS):
        w = inp[name].astype(_jnp.float32)
        if MOMENT_SCALE is None:
            s = _jnp.sqrt(_jnp.mean(_jnp.square(w)) + 1e-30)
        else:
            s = MOMENT_SCALE[name]
        km, kv = _jax.random.split(_jax.random.fold_in(key, i + 1))
        out[name] = w
        out["m_" + name] = s * _jax.random.normal(km, w.shape, _jnp.float32)
        out["v_" + name] = (s * s) * _jax.random.uniform(kv, w.shape, _jnp.float32, 0.5, 1.5)
    if N_MICROBATCH > 1:
        for name, axis in PER_EXAMPLE_BATCH_AXIS.items():
            out[name] = _to_microbatches(out[name], axis)
    return {'x': out['x'], 'ffn1_norm': out['ffn1_norm'], 'ffn1_w_in': out['ffn1_w_in'], 'ffn1_w_out': out['ffn1_w_out'], 'mix_norm': out['mix_norm'], 'ffn2_norm': out['ffn2_norm'], 'ffn2_w_in': out['ffn2_w_in'], 'ffn2_w_out': out['ffn2_w_out'], 'a_w_qkv': out['a_w_qkv'], 'a_rel_bias': out['a_rel_bias'], 'a_w_o': out['a_w_o'], 'kv_norm': out['kv_norm'], 'kv_w_down': out['kv_w_down'], 'kv_latent_norm': out['kv_latent_norm'], 'kv_w_up': out['kv_w_up'], 'b_w_dq': out['b_w_dq'], 'b_q_norm': out['b_q_norm'], 'b_w_uq': out['b_w_uq'], 'b_w_o': out['b_w_o'], 'final_norm': out['final_norm'], 'loss_target': out['loss_target'], 'm_ffn1_norm': out['m_ffn1_norm'], 'm_ffn1_w_in': out['m_ffn1_w_in'], 'm_ffn1_w_out': out['m_ffn1_w_out'], 'm_mix_norm': out['m_mix_norm'], 'm_ffn2_norm': out['m_ffn2_norm'], 'm_ffn2_w_in': out['m_ffn2_w_in'], 'm_ffn2_w_out': out['m_ffn2_w_out'], 'm_a_w_qkv': out['m_a_w_qkv'], 'm_a_rel_bias': out['m_a_rel_bias'], 'm_a_w_o': out['m_a_w_o'], 'm_kv_norm': out['m_kv_norm'], 'm_kv_w_down': out['m_kv_w_down'], 'm_kv_latent_norm': out['m_kv_latent_norm'], 'm_kv_w_up': out['m_kv_w_up'], 'm_b_w_dq': out['m_b_w_dq'], 'm_b_q_norm': out['m_b_q_norm'], 'm_b_w_uq': out['m_b_w_uq'], 'm_b_w_o': out['m_b_w_o'], 'm_final_norm': out['m_final_norm'], 'v_ffn1_norm': out['v_ffn1_norm'], 'v_ffn1_w_in': out['v_ffn1_w_in'], 'v_ffn1_w_out': out['v_ffn1_w_out'], 'v_mix_norm': out['v_mix_norm'], 'v_ffn2_norm': out['v_ffn2_norm'], 'v_ffn2_w_in': out['v_ffn2_w_in'], 'v_ffn2_w_out': out['v_ffn2_w_out'], 'v_a_w_qkv': out['v_a_w_qkv'], 'v_a_rel_bias': out['v_a_rel_bias'], 'v_a_w_o': out['v_a_w_o'], 'v_kv_norm': out['v_kv_norm'], 'v_kv_w_down': out['v_kv_w_down'], 'v_kv_latent_norm': out['v_kv_latent_norm'], 'v_kv_w_up': out['v_kv_w_up'], 'v_b_w_dq': out['v_b_w_dq'], 'v_b_q_norm': out['v_b_q_norm'], 'v_b_w_uq': out['v_b_w_uq'], 'v_b_w_o': out['v_b_w_o'], 'v_final_norm': out['v_final_norm']}


def _loss(weights, diff, rest, loss_target):
    with _jax.named_scope("forward"):
        args = {**rest, TWIN_DIFF_INPUT: diff, **{k: w.astype(_WEIGHT_DTYPES[k]) for k, w in weights.items()}}
        y = _forward(args)
    with _jax.named_scope("loss_head"):
        err = _jnp.square(y.astype(_jnp.float32) - loss_target)
        return 0.5 * _jnp.sum(_jnp.mean(err, axis=-1)) if err.ndim else 0.5 * err


def _adamw(w, g, m, v):
    m = ADAM_B1 * m + (1.0 - ADAM_B1) * g
    v = ADAM_B2 * v + (1.0 - ADAM_B2) * _jnp.square(g)
    m_hat = m / (1.0 - ADAM_B1 ** ADAM_STEP)
    v_hat = v / (1.0 - ADAM_B2 ** ADAM_STEP)
    delta = -ADAM_LR * (m_hat / (_jnp.sqrt(v_hat) + ADAM_EPS) + ADAM_WD * w)
    return delta, m, v


def reference(x, ffn1_norm, ffn1_w_in, ffn1_w_out, mix_norm, ffn2_norm, ffn2_w_in, ffn2_w_out, a_w_qkv, a_rel_bias, a_w_o, kv_norm, kv_w_down, kv_latent_norm, kv_w_up, b_w_dq, b_q_norm, b_w_uq, b_w_o, final_norm, loss_target, m_ffn1_norm, m_ffn1_w_in, m_ffn1_w_out, m_mix_norm, m_ffn2_norm, m_ffn2_w_in, m_ffn2_w_out, m_a_w_qkv, m_a_rel_bias, m_a_w_o, m_kv_norm, m_kv_w_down, m_kv_latent_norm, m_kv_w_up, m_b_w_dq, m_b_q_norm, m_b_w_uq, m_b_w_o, m_final_norm, v_ffn1_norm, v_ffn1_w_in, v_ffn1_w_out, v_mix_norm, v_ffn2_norm, v_ffn2_w_in, v_ffn2_w_out, v_a_w_qkv, v_a_rel_bias, v_a_w_o, v_kv_norm, v_kv_w_down, v_kv_latent_norm, v_kv_w_up, v_b_w_dq, v_b_q_norm, v_b_w_uq, v_b_w_o, v_final_norm):
    given = dict(x=x, ffn1_norm=ffn1_norm, ffn1_w_in=ffn1_w_in, ffn1_w_out=ffn1_w_out, mix_norm=mix_norm, ffn2_norm=ffn2_norm, ffn2_w_in=ffn2_w_in, ffn2_w_out=ffn2_w_out, a_w_qkv=a_w_qkv, a_rel_bias=a_rel_bias, a_w_o=a_w_o, kv_norm=kv_norm, kv_w_down=kv_w_down, kv_latent_norm=kv_latent_norm, kv_w_up=kv_w_up, b_w_dq=b_w_dq, b_q_norm=b_q_norm, b_w_uq=b_w_uq, b_w_o=b_w_o, final_norm=final_norm, loss_target=loss_target, m_ffn1_norm=m_ffn1_norm, m_ffn1_w_in=m_ffn1_w_in, m_ffn1_w_out=m_ffn1_w_out, m_mix_norm=m_mix_norm, m_ffn2_norm=m_ffn2_norm, m_ffn2_w_in=m_ffn2_w_in, m_ffn2_w_out=m_ffn2_w_out, m_a_w_qkv=m_a_w_qkv, m_a_rel_bias=m_a_rel_bias, m_a_w_o=m_a_w_o, m_kv_norm=m_kv_norm, m_kv_w_down=m_kv_w_down, m_kv_latent_norm=m_kv_latent_norm, m_kv_w_up=m_kv_w_up, m_b_w_dq=m_b_w_dq, m_b_q_norm=m_b_q_norm, m_b_w_uq=m_b_w_uq, m_b_w_o=m_b_w_o, m_final_norm=m_final_norm, v_ffn1_norm=v_ffn1_norm, v_ffn1_w_in=v_ffn1_w_in, v_ffn1_w_out=v_ffn1_w_out, v_mix_norm=v_mix_norm, v_ffn2_norm=v_ffn2_norm, v_ffn2_w_in=v_ffn2_w_in, v_ffn2_w_out=v_ffn2_w_out, v_a_w_qkv=v_a_w_qkv, v_a_rel_bias=v_a_rel_bias, v_a_w_o=v_a_w_o, v_kv_norm=v_kv_norm, v_kv_w_down=v_kv_w_down, v_kv_latent_norm=v_kv_latent_norm, v_kv_w_up=v_kv_w_up, v_b_w_dq=v_b_w_dq, v_b_q_norm=v_b_q_norm, v_b_w_uq=v_b_w_uq, v_b_w_o=v_b_w_o, v_final_norm=v_final_norm)
    weights = {n: given[n] for n in TWIN_WEIGHTS}
    shared = {n: given[n] for n in SHARED_INPUTS}
    per_example = {n: given[n] for n in ['x']}
    grad_fn = _jax.value_and_grad(_loss, argnums=(0, 1))

    def one_microbatch(ex, loss_target):
        ex = dict(ex)
        diff = ex.pop(TWIN_DIFF_INPUT)
        return grad_fn(weights, diff, {**shared, **ex}, loss_target)

    if N_MICROBATCH == 1:
        loss, (grad_w, grad_x) = one_microbatch(per_example, given["loss_target"])
    else:
        def body(carry, xs):
            loss_sum, grad_sum = carry
            l_k, (gw_k, gx_k) = one_microbatch(xs[0], xs[1])
            with _jax.named_scope("update"):
                return (loss_sum + l_k, _jax.tree.map(_jnp.add, grad_sum, gw_k)), gx_k

        init = (_jnp.zeros((), _jnp.float32), _jax.tree.map(_jnp.zeros_like, weights))
        (loss, grad_w), grad_x = _jax.lax.scan(body, init, (per_example, given["loss_target"]))
    with _jax.named_scope("update"):
        delta_w, new_m, new_v = {}, {}, {}
        for n in TWIN_WEIGHTS:
            delta_w[n], new_m[n], new_v[n] = _adamw(weights[n], grad_w[n], given["m_" + n], given["v_" + n])
    return (loss, grad_x, *[grad_w[n] for n in TWIN_WEIGHTS], *[delta_w[n] for n in TWIN_WEIGHTS],
            *[new_m[n] for n in TWIN_WEIGHTS], *[new_v[n] for n in TWIN_WEIGHTS])
```

```python
import functools
import math

import jax
import jax.numpy as jnp
from jax import lax
from jax.experimental import pallas as pl
from jax.experimental.pallas import tpu as pltpu

F32 = jnp.float32
BF16 = jnp.bfloat16
I32 = jnp.int32

CHUNK = 64
CHUNK_SHIFT = 6
HEAD_DIM_A = 64
LEFT_CHUNKS = 8
MAX_REL = 128
REL_PAD = 384
WIN = (LEFT_CHUNKS + 2) * CHUNK
PADR = WIN - CHUNK
NOPE = 128
ROPE = 64
EPS = 1e-6
NEG_INF = -1e30
ROPE_THETA = 10000.0
ADAM_LR, ADAM_B1, ADAM_B2, ADAM_EPS, ADAM_WD, ADAM_STEP = 0.001, 0.9, 0.999, 1e-08, 0.01, 10
N_CHIPS = 4
LANE = 128
MESH = pl.DeviceIdType.MESH
VMEM_CAP_MB = 60

NN = (((1,), (0,)), ((), ()))
NT = (((1,), (1,)), ((), ()))
TN = (((0,), (0,)), ((), ()))


def _tile(n, pref, mult):
    t = (min(pref, n) // mult) * mult
    while t >= mult:
        if n % t == 0:
            return t
        t -= mult
    return n


def _nbytes(shape, dtype):
    return math.prod(shape) * jnp.dtype(dtype).itemsize


def _params(block_bytes, extra_bytes=0):
    need = 2 * block_bytes + extra_bytes
    mb = min(VMEM_CAP_MB, max(32, int(need * 1.25 / 2**20) + 8))
    return pltpu.CompilerParams(vmem_limit_bytes=mb * 2**20)


def _mm(name, kind, a, b, grid, a_spec, b_spec, o_spec, out_shape, out_dtype, blocks,
        red_axis=None, nred=1, alpha=1.0, res=None, res_spec=None):
    dims = {"nn": NN, "nt": NT, "tn": TN}[kind]
    has_res = res is not None
    acc_in_out = nred > 1 and out_dtype == F32 and not has_res and alpha == 1.0

    def body(*refs):
        a_ref, b_ref = refs[0], refs[1]
        r_ref = refs[2] if has_res else None
        o_ref = refs[3] if has_res else refs[2]
        p = lax.dot_general(a_ref[...].astype(BF16), b_ref[...].astype(BF16), dims,
                            preferred_element_type=F32)

        def finish(acc):
            y = acc if alpha == 1.0 else acc * alpha
            if has_res:
                y = r_ref[...] + y
            o_ref[...] = y.astype(o_ref.dtype)

        if nred == 1:
            finish(p)
            return
        k = pl.program_id(red_axis)
        tgt = o_ref if acc_in_out else refs[-1]

        @pl.when(k == 0)
        def _():
            tgt[...] = p

        @pl.when(k > 0)
        def _():
            tgt[...] += p

        if not acc_in_out:
            @pl.when(k == nred - 1)
            def _():
                finish(tgt[...])

    a_blk, b_blk, o_blk = blocks
    scratch = []
    extra = 0
    if nred > 1 and not acc_in_out:
        scratch = [pltpu.VMEM(o_blk, F32)]
        extra = _nbytes(o_blk, F32)
    blk = _nbytes(a_blk, a.dtype) + _nbytes(b_blk, b.dtype) + _nbytes(o_blk, out_dtype)
    ins, specs = [a, b], [a_spec, b_spec]
    if has_res:
        ins.append(res)
        specs.append(res_spec)
        blk += _nbytes(o_blk, res.dtype)
    extra += _nbytes(a_blk, BF16) + _nbytes(b_blk, BF16) + 2 * _nbytes(o_blk, F32)
    return pl.pallas_call(
        body, name=name, grid=grid, in_specs=specs, out_specs=o_spec,
        out_shape=jax.ShapeDtypeStruct(out_shape, out_dtype), scratch_shapes=scratch,
        compiler_params=_params(blk, extra),
    )(*ins)


def mm_colw(name, x, w3, out_dtype):
    T, K = x.shape
    _, _, nl = w3.shape
    tm = _tile(T, 512, 8)
    return _mm(name, "nn", x, w3, (N_CHIPS, T // tm),
               pl.BlockSpec((tm, K), lambda j, i: (i, 0)),
               pl.BlockSpec((None, K, nl), lambda j, i: (j, 0, 0)),
               pl.BlockSpec((tm, nl), lambda j, i: (i, j)),
               (T, N_CHIPS * nl), out_dtype, ((tm, K), (K, nl), (tm, nl)))


def mm_colw_t(name, dy, w3, out_dtype, res=None):
    T = dy.shape[0]
    _, K, nl = w3.shape
    tm = _tile(T, 512, 8)
    return _mm(name, "nt", dy, w3, (T // tm, N_CHIPS),
               pl.BlockSpec((tm, nl), lambda i, j: (i, j)),
               pl.BlockSpec((None, K, nl), lambda i, j: (j, 0, 0)),
               pl.BlockSpec((tm, K), lambda i, j: (i, 0)),
               (T, K), out_dtype, ((tm, nl), (K, nl), (tm, K)),
               red_axis=1, nred=N_CHIPS, res=res,
               res_spec=pl.BlockSpec((tm, K), lambda i, j: (i, 0)))


def mm_dcolw(name, x, dy):
    T, K = x.shape
    nl = dy.shape[1] // N_CHIPS
    tt = _tile(T, 512, 8)
    return _mm(name, "tn", x, dy, (N_CHIPS, T // tt),
               pl.BlockSpec((tt, K), lambda j, t: (t, 0)),
               pl.BlockSpec((tt, nl), lambda j, t: (t, j)),
               pl.BlockSpec((None, K, nl), lambda j, t: (j, 0, 0)),
               (N_CHIPS, K, nl), F32, ((tt, K), (tt, nl), (K, nl)),
               red_axis=1, nred=T // tt)


def mm_roww(name, x, w2, out_dtype, res=None, alpha=1.0):
    T, Kt = x.shape
    N = w2.shape[1]
    tm = _tile(T, 512, 8)
    return _mm(name, "nn", x, w2, (T // tm,),
               pl.BlockSpec((tm, Kt), lambda i: (i, 0)),
               pl.BlockSpec((Kt, N), lambda i: (0, 0)),
               pl.BlockSpec((tm, N), lambda i: (i, 0)),
               (T, N), out_dtype, ((tm, Kt), (Kt, N), (tm, N)),
               alpha=alpha, res=res, res_spec=pl.BlockSpec((tm, N), lambda i: (i, 0)))


def mm_roww_t(name, dy, w2, out_dtype, alpha=1.0):
    T, N = dy.shape
    Kt = w2.shape[0]
    tm = _tile(T, 512, 8)
    tk = _tile(Kt, 1408, LANE)
    return _mm(name, "nt", dy, w2, (Kt // tk, T // tm),
               pl.BlockSpec((tm, N), lambda j, i: (i, 0)),
               pl.BlockSpec((tk, N), lambda j, i: (j, 0)),
               pl.BlockSpec((tm, tk), lambda j, i: (i, j)),
               (T, Kt), out_dtype, ((tm, N), (tk, N), (tm, tk)), alpha=alpha)


def mm_droww(name, x, dy, alpha=1.0):
    T, Kt = x.shape
    N = dy.shape[1]
    tt = _tile(T, 512, 8)
    tk = _tile(Kt, 1408, LANE)
    return _mm(name, "tn", x, dy, (Kt // tk, T // tt),
               pl.BlockSpec((tt, tk), lambda j, t: (t, j)),
               pl.BlockSpec((tt, N), lambda j, t: (t, 0)),
               pl.BlockSpec((tk, N), lambda j, t: (j, 0)),
               (Kt, N), F32, ((tt, tk), (tt, N), (tk, N)),
               red_axis=1, nred=T // tt, alpha=alpha)


def rms_fwd(name, x, g):
    T, D = x.shape
    tm = _tile(T, 512, 8)

    def body(x_ref, g_ref, o_ref):
        xv = x_ref[...]
        r = lax.rsqrt(jnp.mean(xv * xv, axis=-1, keepdims=True) + EPS)
        o_ref[...] = (xv * r * g_ref[...]).astype(o_ref.dtype)

    return pl.pallas_call(
        body, name=name, grid=(T // tm,),
        in_specs=[pl.BlockSpec((tm, D), lambda i: (i, 0)), pl.BlockSpec((1, D), lambda i: (0, 0))],
        out_specs=pl.BlockSpec((tm, D), lambda i: (i, 0)),
        out_shape=jax.ShapeDtypeStruct((T, D), BF16),
        compiler_params=_params(_nbytes((tm, D), F32) * 2, 4 * _nbytes((tm, D), F32)),
    )(x, g.reshape(1, D))


def _rms_bwd_math(xv, gv, dy):
    r = lax.rsqrt(jnp.mean(xv * xv, axis=-1, keepdims=True) + EPS)
    xh = xv * r
    dyg = dy * gv
    dx = r * (dyg - xh * jnp.mean(dyg * xh, axis=-1, keepdims=True))
    dg = jnp.sum(dy * xh, axis=0, keepdims=True)
    return dx, dg


def rms_bwd(name, x, g, dy, dres=None):
    T, D = x.shape
    tm = _tile(T, 256, 8)
    has_res = dres is not None

    def body(*refs):
        x_ref, g_ref, dy_ref = refs[:3]
        r_ref = refs[3] if has_res else None
        dx_ref, dg_ref = refs[-2:]
        dx, dg = _rms_bwd_math(x_ref[...], g_ref[...], dy_ref[...].astype(F32))
        if has_res:
            dx = r_ref[...] + dx
        dx_ref[...] = dx

        @pl.when(pl.program_id(0) == 0)
        def _():
            dg_ref[...] = dg

        @pl.when(pl.program_id(0) > 0)
        def _():
            dg_ref[...] += dg

    row = pl.BlockSpec((tm, D), lambda i: (i, 0))
    vec = pl.BlockSpec((1, D), lambda i: (0, 0))
    ins, specs = [x, g.reshape(1, D), dy], [row, vec, row]
    if has_res:
        ins.append(dres)
        specs.append(row)
    dx, dg = pl.pallas_call(
        body, name=name, grid=(T // tm,), in_specs=specs, out_specs=[row, vec],
        out_shape=[jax.ShapeDtypeStruct((T, D), F32), jax.ShapeDtypeStruct((1, D), F32)],
        compiler_params=_params(_nbytes((tm, D), F32) * 4, 6 * _nbytes((tm, D), F32)),
    )(*ins)
    return dx, dg.reshape(D)


def swiglu_fwd(name, u):
    T, F2 = u.shape
    F = F2 // 2
    tm = _tile(T, 256, 8)

    def body(u_ref, o_ref):
        u1 = u_ref[:, :F]
        u2 = u_ref[:, F:]
        o_ref[...] = (u1 * jax.nn.sigmoid(u1) * u2).astype(o_ref.dtype)

    return pl.pallas_call(
        body, name=name, grid=(T // tm,),
        in_specs=[pl.BlockSpec((tm, F2), lambda i: (i, 0))],
        out_specs=pl.BlockSpec((tm, F), lambda i: (i, 0)),
        out_shape=jax.ShapeDtypeStruct((T, F), BF16),
        compiler_params=_params(_nbytes((tm, F2), F32) * 2, _nbytes((tm, F2), F32) * 2),
    )(u)


def swiglu_bwd(name, u, dact):
    T, F2 = u.shape
    F = F2 // 2
    tm = _tile(T, 256, 8)

    def body(u_ref, d_ref, o_ref):
        u1 = u_ref[:, :F]
        u2 = u_ref[:, F:]
        d = d_ref[...]
        sig = jax.nn.sigmoid(u1)
        silu = u1 * sig
        o_ref[:, :F] = (d * u2 * (sig * (1.0 + u1 * (1.0 - sig)))).astype(o_ref.dtype)
        o_ref[:, F:] = (d * silu).astype(o_ref.dtype)

    return pl.pallas_call(
        body, name=name, grid=(T // tm,),
        in_specs=[pl.BlockSpec((tm, F2), lambda i: (i, 0)), pl.BlockSpec((tm, F), lambda i: (i, 0))],
        out_specs=pl.BlockSpec((tm, F2), lambda i: (i, 0)),
        out_shape=jax.ShapeDtypeStruct((T, F2), BF16),
        compiler_params=_params(_nbytes((tm, F2), F32) * 2, _nbytes((tm, F2), F32) * 3),
    )(u, dact)


def loss_head(name, h, g, target):
    T, D = h.shape
    tm = _tile(T, 256, 8)

    def body(h_ref, g_ref, t_ref, dh_ref, dg_ref, loss_ref):
        xv = h_ref[...]
        gv = g_ref[...]
        r = lax.rsqrt(jnp.mean(xv * xv, axis=-1, keepdims=True) + EPS)
        err = xv * r * gv - t_ref[...]
        part = 0.5 * jnp.sum(jnp.mean(err * err, axis=-1, keepdims=True), axis=0, keepdims=True)
        dx, dg = _rms_bwd_math(xv, gv, err * (1.0 / D))
        dh_ref[...] = dx
        part = jnp.broadcast_to(part, (1, LANE))

        @pl.when(pl.program_id(0) == 0)
        def _():
            dg_ref[...] = dg
            loss_ref[...] = part

        @pl.when(pl.program_id(0) > 0)
        def _():
            dg_ref[...] += dg
            loss_ref[...] += part

    row = pl.BlockSpec((tm, D), lambda i: (i, 0))
    vec = pl.BlockSpec((1, D), lambda i: (0, 0))
    dh, dg, loss = pl.pallas_call(
        body, name=name, grid=(T // tm,), in_specs=[row, vec, row],
        out_specs=[row, vec, pl.BlockSpec((1, LANE), lambda i: (0, 0))],
        out_shape=[jax.ShapeDtypeStruct((T, D), F32), jax.ShapeDtypeStruct((1, D), F32),
                   jax.ShapeDtypeStruct((1, LANE), F32)],
        compiler_params=_params(_nbytes((tm, D), F32) * 3, 6 * _nbytes((tm, D), F32)),
    )(h, g.reshape(1, D), target)
    return dh, dg.reshape(D), loss


def rope_tables(S):
    half = ROPE // 2
    freqs = ROPE_THETA ** (-jnp.arange(half, dtype=F32) / half)
    ang = jnp.arange(S, dtype=F32)[:, None] * freqs[None, :]
    cos, sin = jnp.cos(ang), jnp.sin(ang)
    z = jnp.zeros_like(cos)
    ct = jnp.concatenate([cos, cos, z, z], axis=1)
    s1 = jnp.concatenate([-sin, z, z, z], axis=1)
    s2 = jnp.concatenate([z, sin, z, z], axis=1)
    return ct, s1, s2


def _rope_tile(t, ct, s1, s2):
    return t * ct + pltpu.roll(t, 96, 1) * s1 + pltpu.roll(t, 32, 1) * s2


def _rope_tile_bwd(d, ct, s1, s2):
    return d * ct + pltpu.roll(d * s1, 32, 1) + pltpu.roll(d * s2, 96, 1)


def qprep(name, q, tabs, B, S, bwd):
    T, W = q.shape
    nh = W // 256
    ts = _tile(S, 256, 8)
    fn = _rope_tile_bwd if bwd else _rope_tile

    def body(q_ref, ct_ref, s1_ref, s2_ref, o_ref):
        ct, s1, s2 = ct_ref[...], s1_ref[...], s2_ref[...]
        for h in range(nh):
            o_ref[0, :, 256 * h:256 * h + 128] = q_ref[0, :, 256 * h:256 * h + 128].astype(o_ref.dtype)
            t = q_ref[0, :, 256 * h + 128:256 * h + 256].astype(F32)
            o_ref[0, :, 256 * h + 128:256 * h + 256] = fn(t, ct, s1, s2).astype(o_ref.dtype)

    row = pl.BlockSpec((1, ts, W), lambda b, s: (b, s, 0))
    tab = pl.BlockSpec((ts, LANE), lambda b, s: (s, 0))
    out = pl.pallas_call(
        body, name=name, grid=(B, S // ts), in_specs=[row, tab, tab, tab], out_specs=row,
        out_shape=jax.ShapeDtypeStruct((B, S, W), BF16),
        compiler_params=_params(_nbytes((ts, W), F32) * 2, _nbytes((ts, W), F32) * 2),
    )(q.reshape(B, S, W), *tabs)
    return out.reshape(T, W)


def kvprep_fwd(name, ckr, g, tabs, B, S):
    T, W = ckr.shape
    KVL = W - LANE
    ts = _tile(S, 256, 8)

    def body(x_ref, g_ref, ct_ref, s1_ref, s2_ref, c_ref, k_ref):
        xv = x_ref[0, :, :KVL]
        r = lax.rsqrt(jnp.mean(xv * xv, axis=-1, keepdims=True) + EPS)
        c_ref[0] = (xv * r * g_ref[...]).astype(c_ref.dtype)
        k_ref[0] = _rope_tile(x_ref[0, :, KVL:], ct_ref[...], s1_ref[...], s2_ref[...]).astype(k_ref.dtype)

    tab = pl.BlockSpec((ts, LANE), lambda b, s: (s, 0))
    c, k = pl.pallas_call(
        body, name=name, grid=(B, S // ts),
        in_specs=[pl.BlockSpec((1, ts, W), lambda b, s: (b, s, 0)), pl.BlockSpec((1, KVL), lambda b, s: (0, 0)),
                  tab, tab, tab],
        out_specs=[pl.BlockSpec((1, ts, KVL), lambda b, s: (b, s, 0)),
                   pl.BlockSpec((1, ts, LANE), lambda b, s: (b, s, 0))],
        out_shape=[jax.ShapeDtypeStruct((B, S, KVL), BF16), jax.ShapeDtypeStruct((B, S, LANE), BF16)],
        compiler_params=_params(_nbytes((ts, W), F32) * 2, _nbytes((ts, W), F32) * 2),
    )(ckr.reshape(B, S, W), g.reshape(1, KVL), *tabs)
    return c.reshape(T, KVL), k


def kvprep_bwd(name, ckr, g, dc, dkr, tabs, B, S):
    T, W = ckr.shape
    KVL = W - LANE
    ts = _tile(S, 256, 8)

    def body(x_ref, g_ref, dc_ref, dk_ref, ct_ref, s1_ref, s2_ref, o_ref, dg_ref):
        dx, dg = _rms_bwd_math(x_ref[0, :, :KVL], g_ref[...], dc_ref[0])
        o_ref[0, :, :KVL] = dx
        o_ref[0, :, KVL:] = _rope_tile_bwd(dk_ref[0], ct_ref[...], s1_ref[...], s2_ref[...])
        first = (pl.program_id(0) == 0) & (pl.program_id(1) == 0)

        @pl.when(first)
        def _():
            dg_ref[...] = dg

        @pl.when(jnp.logical_not(first))
        def _():
            dg_ref[...] += dg

    tab = pl.BlockSpec((ts, LANE), lambda b, s: (s, 0))
    vec = pl.BlockSpec((1, KVL), lambda b, s: (0, 0))
    o, dg = pl.pallas_call(
        body, name=name, grid=(B, S // ts),
        in_specs=[pl.BlockSpec((1, ts, W), lambda b, s: (b, s, 0)), vec,
                  pl.BlockSpec((1, ts, KVL), lambda b, s: (b, s, 0)),
                  pl.BlockSpec((1, ts, LANE), lambda b, s: (b, s, 0)), tab, tab, tab],
        out_specs=[pl.BlockSpec((1, ts, W), lambda b, s: (b, s, 0)), vec],
        out_shape=[jax.ShapeDtypeStruct((B, S, W), F32), jax.ShapeDtypeStruct((1, KVL), F32)],
        compiler_params=_params(_nbytes((ts, W), F32) * 4, _nbytes((ts, W), F32) * 4),
    )(ckr.reshape(B, S, W), g.reshape(1, KVL), dc.reshape(B, S, KVL), dkr, *tabs)
    return o.reshape(T, W), dg.reshape(KVL)


def _rel_onehot(i):
    col = lax.broadcasted_iota(I32, (REL_PAD, WIN), 1)
    row = lax.broadcasted_iota(I32, (REL_PAD, WIN), 0)
    idx = jnp.clip(PADR + i - col, -MAX_REL, MAX_REL) + MAX_REL
    return (row == idx).astype(F32)


def rel_bias_tile(name, table):
    H = table.shape[0]
    tpad = jnp.pad(table, ((0, 0), (0, REL_PAD - table.shape[1])))

    def body(t_ref, o_ref):
        base = pl.program_id(0) * 8
        for ii in range(8):
            o_ref[:, ii, :] = lax.dot_general(t_ref[...], _rel_onehot(base + ii), NN,
                                              precision=lax.Precision.HIGHEST, preferred_element_type=F32)

    return pl.pallas_call(
        body, name=name, grid=(CHUNK // 8,),
        in_specs=[pl.BlockSpec((H, REL_PAD), lambda s: (0, 0))],
        out_specs=pl.BlockSpec((H, 8, WIN), lambda s: (0, s, 0)),
        out_shape=jax.ShapeDtypeStruct((H, CHUNK, WIN), F32),
    )(tpad)


def rel_bias_grad(name, dbias):
    H = dbias.shape[0]

    def body(d_ref, o_ref):
        base = pl.program_id(0) * 8
        acc = jnp.zeros((H, REL_PAD), F32)
        for ii in range(8):
            acc = acc + lax.dot_general(d_ref[:, ii, :], _rel_onehot(base + ii), NT,
                                        precision=lax.Precision.HIGHEST, preferred_element_type=F32)

        @pl.when(pl.program_id(0) == 0)
        def _():
            o_ref[...] = acc

        @pl.when(pl.program_id(0) > 0)
        def _():
            o_ref[...] += acc

    return pl.pallas_call(
        body, name=name, grid=(CHUNK // 8,),
        in_specs=[pl.BlockSpec((H, 8, WIN), lambda s: (0, s, 0))],
        out_specs=pl.BlockSpec((H, REL_PAD), lambda s: (0, 0)),
        out_shape=jax.ShapeDtypeStruct((H, REL_PAD), F32),
    )(dbias)


def _attn_a_load_kv(qkv_hbm, b, kpad, vpad, sem, S, D):
    kpad[0:PADR, :] = jnp.zeros((PADR, D), BF16)
    vpad[0:PADR, :] = jnp.zeros((PADR, D), BF16)
    ck = pltpu.make_async_copy(qkv_hbm.at[b, :, pl.ds(D, D)], kpad.at[pl.ds(PADR, S), :], sem.at[0])
    cv = pltpu.make_async_copy(qkv_hbm.at[b, :, pl.ds(2 * D, D)], vpad.at[pl.ds(PADR, S), :], sem.at[1])
    ck.start()
    cv.start()
    ck.wait()
    cv.wait()


def _attn_a_probs(qm, kp, bias, valid, scale):
    s = lax.dot_general(qm, kp, NT, preferred_element_type=F32) * scale + bias
    s = jnp.where(valid, s, NEG_INF)
    e = jnp.exp(s - jnp.max(s, axis=-1, keepdims=True))
    return e * (1.0 / jnp.sum(e, axis=-1, keepdims=True))


def attn_a_fwd(name, qkv, bias):
    B, S, D3 = qkv.shape
    D = D3 // 3
    H = D // HEAD_DIM_A
    nc = S // CHUNK
    scale = HEAD_DIM_A ** -0.5

    def body(q_ref, bias_ref, qkv_hbm, o_ref, kpad, vpad, sem):
        b, c = pl.program_id(0), pl.program_id(1)

        @pl.when(c == 0)
        def _():
            _attn_a_load_kv(qkv_hbm, b, kpad, vpad, sem, S, D)

        col = lax.broadcasted_iota(I32, (CHUNK, WIN), 1)
        valid = col >= jnp.maximum(CHUNK, (LEFT_CHUNKS + 1 - c) * CHUNK)
        lane = lax.broadcasted_iota(I32, (CHUNK, LANE), 1)
        w0 = pl.multiple_of(c * CHUNK, CHUNK)
        for p in range(H // 2):
            ls = slice(p * LANE, (p + 1) * LANE)
            qp = q_ref[0, :, ls]
            kp = kpad[pl.ds(w0, WIN), ls]
            vp = vpad[pl.ds(w0, WIN), ls]
            halves = []
            for hh in range(2):
                sel = (lane < HEAD_DIM_A) if hh == 0 else (lane >= HEAD_DIM_A)
                qm = jnp.where(sel, qp, jnp.zeros_like(qp))
                pr = _attn_a_probs(qm, kp, bias_ref[2 * p + hh], valid, scale)
                halves.append(jnp.dot(pr.astype(BF16), vp, preferred_element_type=F32))
            o_ref[0, :, ls] = jnp.where(lane < HEAD_DIM_A, halves[0], halves[1]).astype(o_ref.dtype)

    blk = _nbytes((CHUNK, D), BF16) * 2 + _nbytes((H, CHUNK, WIN), F32)
    return pl.pallas_call(
        body, name=name, grid=(B, nc),
        in_specs=[pl.BlockSpec((1, CHUNK, D), lambda b, c: (b, c, 0)),
                  pl.BlockSpec((H, CHUNK, WIN), lambda b, c: (0, 0, 0)),
                  pl.BlockSpec(memory_space=pl.ANY)],
        out_specs=pl.BlockSpec((1, CHUNK, D), lambda b, c: (b, c, 0)),
        out_shape=jax.ShapeDtypeStruct((B, S, D), BF16),
        scratch_shapes=[pltpu.VMEM((PADR + S, D), BF16), pltpu.VMEM((PADR + S, D), BF16),
                        pltpu.SemaphoreType.DMA((2,))],
        compiler_params=_params(blk, 2 * _nbytes((PADR + S, D), BF16)),
    )(qkv, bias, qkv)


def attn_a_bwd(name, qkv, do, bias):
    B, S, D3 = qkv.shape
    D = D3 // 3
    H = D // HEAD_DIM_A
    nc = S // CHUNK
    scale = HEAD_DIM_A ** -0.5

    def body(q_ref, do_ref, bias_ref, qkv_hbm, dq_ref, dkv_hbm, dbias_ref, kpad, vpad, dkacc, dvacc, sem):
        b, c = pl.program_id(0), pl.program_id(1)

        @pl.when(c == 0)
        def _():
            _attn_a_load_kv(qkv_hbm, b, kpad, vpad, sem, S, D)
            dkacc[...] = jnp.zeros_like(dkacc)
            dvacc[...] = jnp.zeros_like(dvacc)

        @pl.when((b == 0) & (c == 0))
        def _():
            dbias_ref[...] = jnp.zeros_like(dbias_ref)

        col = lax.broadcasted_iota(I32, (CHUNK, WIN), 1)
        valid = col >= jnp.maximum(CHUNK, (LEFT_CHUNKS + 1 - c) * CHUNK)
        lane = lax.broadcasted_iota(I32, (CHUNK, LANE), 1)
        w0 = pl.multiple_of(c * CHUNK, CHUNK)
        for p in range(H // 2):
            ls = slice(p * LANE, (p + 1) * LANE)
            qp = q_ref[0, :, ls]
            dop = do_ref[0, :, ls]
            kp = kpad[pl.ds(w0, WIN), ls]
            vp = vpad[pl.ds(w0, WIN), ls]
            dq_pair = jnp.zeros((CHUNK, LANE), F32)
            dk_pair = jnp.zeros((WIN, LANE), F32)
            dv_pair = jnp.zeros((WIN, LANE), F32)
            for hh in range(2):
                sel = (lane < HEAD_DIM_A) if hh == 0 else (lane >= HEAD_DIM_A)
                qm = jnp.where(sel, qp, jnp.zeros_like(qp))
                dom = jnp.where(sel, dop, jnp.zeros_like(dop))
                pr = _attn_a_probs(qm, kp, bias_ref[2 * p + hh], valid, scale)
                dp = lax.dot_general(dom, vp, NT, preferred_element_type=F32)
                ds = pr * (dp - jnp.sum(pr * dp, axis=-1, keepdims=True))
                dbias_ref[2 * p + hh] += ds
                dsb = (ds * scale).astype(BF16)
                dq_pair = jnp.where(sel, jnp.dot(dsb, kp, preferred_element_type=F32), dq_pair)
                dk_pair = dk_pair + lax.dot_general(dsb, qm, TN, preferred_element_type=F32)
                dv_pair = dv_pair + lax.dot_general(pr.astype(BF16), dom, TN, preferred_element_type=F32)
            dq_ref[0, :, ls] = dq_pair
            dkacc[pl.ds(w0, WIN), ls] += dk_pair
            dvacc[pl.ds(w0, WIN), ls] += dv_pair

        @pl.when(c == nc - 1)
        def _():
            ck = pltpu.make_async_copy(dkacc.at[pl.ds(PADR, S), :], dkv_hbm.at[b, :, pl.ds(0, D)], sem.at[0])
            cv = pltpu.make_async_copy(dvacc.at[pl.ds(PADR, S), :], dkv_hbm.at[b, :, pl.ds(D, D)], sem.at[1])
            ck.start()
            cv.start()
            ck.wait()
            cv.wait()

    blk = _nbytes((CHUNK, D), BF16) * 2 + _nbytes((CHUNK, D), F32) + 2 * _nbytes((H, CHUNK, WIN), F32)
    scr = 2 * _nbytes((PADR + S, D), BF16) + 2 * _nbytes((PADR + S, D), F32)
    return pl.pallas_call(
        body, name=name, grid=(B, nc),
        in_specs=[pl.BlockSpec((1, CHUNK, D), lambda b, c: (b, c, 0)),
                  pl.BlockSpec((1, CHUNK, D), lambda b, c: (b, c, 0)),
                  pl.BlockSpec((H, CHUNK, WIN), lambda b, c: (0, 0, 0)),
                  pl.BlockSpec(memory_space=pl.ANY)],
        out_specs=[pl.BlockSpec((1, CHUNK, D), lambda b, c: (b, c, 0)),
                   pl.BlockSpec(memory_space=pl.ANY),
                   pl.BlockSpec((H, CHUNK, WIN), lambda b, c: (0, 0, 0))],
        out_shape=[jax.ShapeDtypeStruct((B, S, D), F32), jax.ShapeDtypeStruct((B, S, 2 * D), F32),
                   jax.ShapeDtypeStruct((H, CHUNK, WIN), F32)],
        scratch_shapes=[pltpu.VMEM((PADR + S, D), BF16), pltpu.VMEM((PADR + S, D), BF16),
                        pltpu.VMEM((PADR + S, D), F32), pltpu.VMEM((PADR + S, D), F32),
                        pltpu.SemaphoreType.DMA((2,))],
        compiler_params=_params(blk, scr),
    )(qkv, do, bias, qkv)


def _mla_scores(q, kblk, krb, q0, k0, scale):
    s = (lax.dot_general(q[:, :NOPE], kblk[:, :NOPE], NT, preferred_element_type=F32)
         + lax.dot_general(q[:, NOPE:], krb, NT, preferred_element_type=F32)) * scale
    qc = jnp.right_shift(q0 + lax.broadcasted_iota(I32, s.shape, 0), CHUNK_SHIFT)
    kc = jnp.right_shift(k0 + lax.broadcasted_iota(I32, s.shape, 1), CHUNK_SHIFT)
    return s, kc <= qc


def mla_fwd(name, qf, kv, kr):
    B, S, W = qf.shape
    HB = W // 256
    QB = _tile(S, 256, CHUNK)
    nq = S // QB
    scale = (NOPE + ROPE) ** -0.5

    def body(q_ref, kv_ref, kr_ref, o_ref, lse_ref):
        qi = pl.program_id(2)
        q = q_ref[0]

        def step(kj, carry):
            m, l, acc = carry
            ks = pl.ds(pl.multiple_of(kj * QB, QB), QB)
            kblk = kv_ref[0, ks, :]
            s, mask = _mla_scores(q, kblk, kr_ref[0, ks, :], qi * QB, kj * QB, scale)
            s = jnp.where(mask, s, NEG_INF)
            m_new = jnp.maximum(m, jnp.max(s, axis=-1, keepdims=True))
            a = jnp.exp(m - m_new)
            p = jnp.exp(s - m_new)
            l = a * l + jnp.sum(p, axis=-1, keepdims=True)
            acc = a * acc + jnp.dot(p.astype(BF16), kblk[:, NOPE:], preferred_element_type=F32)
            return m_new, l, acc

        init = (jnp.full((QB, 1), NEG_INF, F32), jnp.zeros((QB, 1), F32), jnp.zeros((QB, NOPE), F32))
        m, l, acc = lax.fori_loop(0, qi + 1, step, init)
        o_ref[0] = acc * (1.0 / l)
        lse_ref[0, 0] = jnp.broadcast_to(m + jnp.log(l), (QB, LANE))

    blk = (_nbytes((QB, 256), BF16) + _nbytes((S, 256), BF16) + _nbytes((S, LANE), BF16)
           + 2 * _nbytes((QB, LANE), F32))
    return pl.pallas_call(
        body, name=name, grid=(B, HB, nq),
        in_specs=[pl.BlockSpec((1, QB, 256), lambda b, h, i: (b, i, h)),
                  pl.BlockSpec((1, S, 256), lambda b, h, i: (b, 0, h)),
                  pl.BlockSpec((1, S, LANE), lambda b, h, i: (b, 0, 0))],
        out_specs=[pl.BlockSpec((1, QB, LANE), lambda b, h, i: (b, i, h)),
                   pl.BlockSpec((1, 1, QB, LANE), lambda b, h, i: (b, h, i, 0))],
        out_shape=[jax.ShapeDtypeStruct((B, S, HB * LANE), F32), jax.ShapeDtypeStruct((B, HB, S, LANE), F32)],
        compiler_params=_params(blk, 8 * _nbytes((QB, QB), F32)),
    )(qf, kv, kr)


def mla_bwd(name, qf, kv, kr, do, o, lse):
    B, S, W = qf.shape
    HB = W // 256
    QB = _tile(S, 256, CHUNK)
    nq = S // QB
    scale = (NOPE + ROPE) ** -0.5

    def body(q_ref, kv_ref, kr_ref, do_ref, o_ref, lse_ref, dq_ref, dkv_ref, dkr_ref):
        h = pl.program_id(1)
        dq_ref[...] = jnp.zeros_like(dq_ref)
        dkv_ref[...] = jnp.zeros_like(dkv_ref)

        @pl.when(h == 0)
        def _():
            dkr_ref[...] = jnp.zeros_like(dkr_ref)

        for qi in range(nq):
            rows = slice(qi * QB, (qi + 1) * QB)
            q = q_ref[0, rows, :]
            dob = do_ref[0, rows, :]
            lse_q = lse_ref[0, 0, rows, :][:, :1]
            delta = jnp.sum(dob.astype(F32) * o_ref[0, rows, :], axis=-1, keepdims=True)

            def step(kj, carry, q=q, dob=dob, lse_q=lse_q, delta=delta, rows=rows, qi=qi):
                ks = pl.ds(pl.multiple_of(kj * QB, QB), QB)
                kblk = kv_ref[0, ks, :]
                krb = kr_ref[0, ks, :]
                s, mask = _mla_scores(q, kblk, krb, qi * QB, kj * QB, scale)
                p = jnp.where(mask, jnp.exp(s - lse_q), 0.0)
                dp = lax.dot_general(dob, kblk[:, NOPE:], NT, preferred_element_type=F32)
                dsb = (p * (dp - delta) * scale).astype(BF16)
                dkv_ref[0, ks, NOPE:] += lax.dot_general(p.astype(BF16), dob, TN, preferred_element_type=F32)
                dkv_ref[0, ks, :NOPE] += lax.dot_general(dsb, q[:, :NOPE], TN, preferred_element_type=F32)
                dkr_ref[0, ks, :] += lax.dot_general(dsb, q[:, NOPE:], TN, preferred_element_type=F32)
                dq_ref[0, rows, :NOPE] += jnp.dot(dsb, kblk[:, :NOPE], preferred_element_type=F32)
                dq_ref[0, rows, NOPE:] += jnp.dot(dsb, krb, preferred_element_type=F32)
                return carry

            lax.fori_loop(0, qi + 1, step, 0)

    head = lambda w: pl.BlockSpec((1, S, w), lambda b, h: (b, 0, h))
    shared = pl.BlockSpec((1, S, LANE), lambda b, h: (b, 0, 0))
    blk = (2 * _nbytes((S, 256), BF16) + 2 * _nbytes((S, LANE), BF16) + 2 * _nbytes((S, LANE), F32)
           + 2 * _nbytes((S, 256), F32) + _nbytes((S, LANE), F32))
    return pl.pallas_call(
        body, name=name, grid=(B, HB),
        in_specs=[head(256), head(256), shared, head(LANE), head(LANE),
                  pl.BlockSpec((1, 1, S, LANE), lambda b, h: (b, h, 0, 0))],
        out_specs=[head(256), head(256), shared],
        out_shape=[jax.ShapeDtypeStruct((B, S, W), F32), jax.ShapeDtypeStruct((B, S, W), F32),
                   jax.ShapeDtypeStruct((B, S, LANE), F32)],
        compiler_params=_params(blk, 10 * _nbytes((QB, QB), F32)),
    )(qf, kv, kr, do, o, lse)


def cast_bf16(name, w, layer):
    _, R, C = w.shape
    tr = _tile(R, 256, 16)

    def body(w_ref, o_ref):
        o_ref[...] = w_ref[...].astype(BF16)

    return pl.pallas_call(
        body, name=name, grid=(R // tr,),
        in_specs=[pl.BlockSpec((None, tr, C), lambda r: (layer, r, 0))],
        out_specs=pl.BlockSpec((tr, C), lambda r: (r, 0)),
        out_shape=jax.ShapeDtypeStruct((R, C), BF16),
    )(w)


def adamw(name, w, g, m, v):
    R, C = w.shape
    tr = _tile(R, max(8, (1 << 18) // C // 8 * 8), 8)
    c1 = 1.0 - ADAM_B1 ** ADAM_STEP
    c2 = 1.0 - ADAM_B2 ** ADAM_STEP

    def body(w_ref, g_ref, m_ref, v_ref, d_ref, mo_ref, vo_ref):
        gv = g_ref[...]
        mn = ADAM_B1 * m_ref[...] + (1.0 - ADAM_B1) * gv
        vn = ADAM_B2 * v_ref[...] + (1.0 - ADAM_B2) * (gv * gv)
        mo_ref[...] = mn
        vo_ref[...] = vn
        d_ref[...] = -ADAM_LR * ((mn / c1) / (jnp.sqrt(vn / c2) + ADAM_EPS) + ADAM_WD * w_ref[...])

    spec = pl.BlockSpec((tr, C), lambda r: (r, 0))
    return pl.pallas_call(
        body, name=name, grid=(R // tr,), in_specs=[spec] * 4, out_specs=[spec] * 3,
        out_shape=[jax.ShapeDtypeStruct((R, C), F32)] * 3,
        compiler_params=_params(7 * _nbytes((tr, C), F32), 4 * _nbytes((tr, C), F32)),
    )(w, g, m, v)


def half_sum(name, dw, landed, cidx):
    _, R, C = dw.shape
    hr = R // 2
    tr = _tile(hr, max(8, (1 << 18) // C // 8 * 8), 8)
    nb = hr // tr

    def body(c_ref, a_ref, b_ref, o_ref):
        o_ref[...] = a_ref[...] + b_ref[...]

    return pl.pallas_call(
        body, name=name,
        grid_spec=pltpu.PrefetchScalarGridSpec(
            num_scalar_prefetch=1, grid=(N_CHIPS, nb),
            in_specs=[pl.BlockSpec((None, tr, C), lambda k, r, c_ref: (k, c_ref[0] * nb + r, 0)),
                      pl.BlockSpec((None, tr, C), lambda k, r, c_ref: (k, r, 0))],
            out_specs=pl.BlockSpec((None, tr, C), lambda k, r, c_ref: (k, r, 0))),
        out_shape=jax.ShapeDtypeStruct((N_CHIPS, hr, C), F32),
    )(cidx, dw, landed)


def chip_sum(name, part, landed, kidx):
    _, hr, C = part.shape
    tr = _tile(hr, max(8, (1 << 18) // C // 8 * 8), 8)

    def body(k_ref, a_ref, b_ref, o_ref):
        o_ref[...] = ((a_ref[...] + b_ref[0]) + b_ref[1]) + b_ref[2]

    return pl.pallas_call(
        body, name=name,
        grid_spec=pltpu.PrefetchScalarGridSpec(
            num_scalar_prefetch=1, grid=(hr // tr,),
            in_specs=[pl.BlockSpec((None, tr, C), lambda r, k_ref: (k_ref[0], r, 0)),
                      pl.BlockSpec((3, tr, C), lambda r, k_ref: (0, r, 0))],
            out_specs=pl.BlockSpec((tr, C), lambda r, k_ref: (r, 0))),
        out_shape=jax.ShapeDtypeStruct((hr, C), F32),
    )(kidx, part, landed)


ANY = pl.BlockSpec(memory_space=pl.ANY)


def _place():
    x, y, c = lax.axis_index("x"), lax.axis_index("y"), lax.axis_index("c")
    chips = [(1 - x, y), (x, 1 - y), (1 - x, 1 - y)]
    return x, y, c, chips


def all_gather_weights(shards):
    n = len(shards)

    def body(*refs):
        ins, outs = refs[:n], refs[n:2 * n]
        send_sems, recv_sems, loc_sems = refs[2 * n:]
        x, y, c, chips = _place()
        kme = 2 * x + y
        sib = (x, y, 1 - c)

        def half(i, which):
            hr = ins[i].shape[0] // 2
            return pl.ds(pl.multiple_of(which * hr, 16), hr)

        def ici(i, j):
            return pltpu.make_async_remote_copy(
                src_ref=ins[i].at[half(i, c), :], dst_ref=outs[i].at[kme, half(i, c), :],
                send_sem=send_sems.at[6 * i + j], recv_sem=recv_sems.at[6 * i + j],
                device_id=(*chips[j], c), device_id_type=MESH)

        def landed(i, j, which):
            kj = 2 * chips[j][0] + chips[j][1]
            return outs[i].at[kj, half(i, which), :]

        def d2d(i, j, which):
            return pltpu.make_async_remote_copy(
                src_ref=landed(i, j, which), dst_ref=landed(i, j, which),
                send_sem=send_sems.at[6 * i + 3 + j], recv_sem=recv_sems.at[6 * i + 3 + j],
                device_id=sib, device_id_type=MESH)

        local = [pltpu.make_async_copy(ins[i], outs[i].at[kme], loc_sems.at[i]) for i in range(n)]
        for i in range(n):
            local[i].start()
            for j in range(3):
                ici(i, j).start()
        for i in range(n):
            for j in range(3):
                pltpu.make_async_remote_copy(
                    src_ref=ins[i].at[half(i, c), :], dst_ref=landed(i, j, c),
                    send_sem=send_sems.at[6 * i + j], recv_sem=recv_sems.at[6 * i + j],
                    device_id=(*chips[j], c), device_id_type=MESH).wait_recv()
                d2d(i, j, c).start()
        for i in range(n):
            for j in range(3):
                d2d(i, j, 1 - c).wait_recv()
        for i in range(n):
            for j in range(3):
                ici(i, j).wait_send()
                d2d(i, j, c).wait_send()
            local[i].wait()

    return pl.pallas_call(
        body, name="all_gather_weights", in_specs=[ANY] * n, out_specs=[ANY] * n,
        out_shape=[jax.ShapeDtypeStruct((N_CHIPS, *s.shape), BF16) for s in shards],
        scratch_shapes=[pltpu.SemaphoreType.DMA((6 * n,)), pltpu.SemaphoreType.DMA((6 * n,)),
                        pltpu.SemaphoreType.DMA((n,))],
    )(*shards)


def pair_exchange(dws):
    n = len(dws)

    def body(*refs):
        ins, outs = refs[:n], refs[n:2 * n]
        send_sems, recv_sems = refs[2 * n:]
        x, y, c, _ = _place()
        copies = []
        for i in range(n):
            hr = ins[i].shape[1] // 2
            other = pl.ds(pl.multiple_of((1 - c) * hr, 8), hr)
            copies.append(pltpu.make_async_remote_copy(
                src_ref=ins[i].at[:, other, :], dst_ref=outs[i],
                send_sem=send_sems.at[i], recv_sem=recv_sems.at[i],
                device_id=(x, y, 1 - c), device_id_type=MESH))
            copies[i].start()
        for cp in copies:
            cp.wait_recv()
        for cp in copies:
            cp.wait_send()

    return pl.pallas_call(
        body, name="grad_pair_exchange", in_specs=[ANY] * n, out_specs=[ANY] * n,
        out_shape=[jax.ShapeDtypeStruct((N_CHIPS, d.shape[1] // 2, d.shape[2]), F32) for d in dws],
        scratch_shapes=[pltpu.SemaphoreType.DMA((n,)), pltpu.SemaphoreType.DMA((n,))],
    )(*dws)


def chip_exchange(parts):
    n = len(parts)

    def body(*refs):
        ins, outs = refs[:n], refs[n:2 * n]
        send_sems, recv_sems = refs[2 * n:]
        x, y, c, chips = _place()
        copies = []
        for i in range(n):
            for j in range(3):
                kd = 2 * chips[j][0] + chips[j][1]
                copies.append(pltpu.make_async_remote_copy(
                    src_ref=ins[i].at[kd], dst_ref=outs[i].at[j],
                    send_sem=send_sems.at[3 * i + j], recv_sem=recv_sems.at[3 * i + j],
                    device_id=(*chips[j], c), device_id_type=MESH))
                copies[-1].start()
        for cp in copies:
            cp.wait_recv()
        for cp in copies:
            cp.wait_send()

    return pl.pallas_call(
        body, name="grad_chip_exchange", in_specs=[ANY] * n, out_specs=[ANY] * n,
        out_shape=[jax.ShapeDtypeStruct((3, *p.shape[1:]), F32) for p in parts],
        scratch_shapes=[pltpu.SemaphoreType.DMA((3 * n,)), pltpu.SemaphoreType.DMA((3 * n,))],
    )(*parts)


def pair_assemble(halves, layout, out_shapes):
    n = len(halves)
    no = len(out_shapes)

    def body(*refs):
        ins, outs = refs[:n], refs[n:n + no]
        send_sems, recv_sems, loc_sems = refs[n + no:]
        x, y, c, _ = _place()
        remote, local = [], []
        for i, (a, l) in enumerate(layout):
            hr = ins[i].shape[0]
            mine = pl.ds(pl.multiple_of(c * hr, 8), hr)
            local.append(pltpu.make_async_copy(ins[i], outs[a].at[l, mine, :], loc_sems.at[i]))
            remote.append(pltpu.make_async_remote_copy(
                src_ref=ins[i], dst_ref=outs[a].at[l, mine, :],
                send_sem=send_sems.at[i], recv_sem=recv_sems.at[i],
                device_id=(x, y, 1 - c), device_id_type=MESH))
            local[i].start()
            remote[i].start()
        for i in range(n):
            remote[i].wait_recv()
        for i in range(n):
            remote[i].wait_send()
            local[i].wait()

    return pl.pallas_call(
        body, name="grad_pair_assemble", in_specs=[ANY] * n, out_specs=[ANY] * no,
        out_shape=[jax.ShapeDtypeStruct(s, F32) for s in out_shapes],
        scratch_shapes=[pltpu.SemaphoreType.DMA((n,)), pltpu.SemaphoreType.DMA((n,)),
                        pltpu.SemaphoreType.DMA((n,))],
    )(*halves)


def all_reduce_small(vec):
    NR = vec.shape[0]
    flips = [(fx, fy, fc) for fx in (0, 1) for fy in (0, 1) for fc in (0, 1)][1:]

    def body(v_ref, o_ref, buf, send_sems, recv_sems):
        x, y, c, _ = _place()
        me = 4 * x + 2 * y + c
        buf[me] = v_ref[...]
        copies = []
        for j, (fx, fy, fc) in enumerate(flips):
            peer = (1 - x if fx else x, 1 - y if fy else y, 1 - c if fc else c)
            copies.append(pltpu.make_async_remote_copy(
                src_ref=v_ref, dst_ref=buf.at[me], send_sem=send_sems.at[j], recv_sem=recv_sems.at[j],
                device_id=peer, device_id_type=MESH))
            copies[j].start()
        for cp in copies:
            cp.wait_recv()
        for cp in copies:
            cp.wait_send()
        acc = buf[0]
        for d in range(1, 8):
            acc = acc + buf[d]
        o_ref[...] = acc

    return pl.pallas_call(
        body, name="all_reduce_small",
        in_specs=[pl.BlockSpec(memory_space=pltpu.VMEM)], out_specs=pl.BlockSpec(memory_space=pltpu.VMEM),
        out_shape=jax.ShapeDtypeStruct((NR, LANE), F32),
        scratch_shapes=[pltpu.VMEM((8, NR, LANE), F32), pltpu.SemaphoreType.DMA((7,)),
                        pltpu.SemaphoreType.DMA((7,))],
    )(vec)


def _pack(arrays):
    flat = jnp.concatenate([a.reshape(-1).astype(F32) for a in arrays])
    n = flat.shape[0]
    npad = -(-n // (8 * LANE)) * (8 * LANE)
    return jnp.pad(flat, (0, npad - n)).reshape(npad // LANE, LANE)


def _unpack(buf, like):
    flat = buf.reshape(-1)
    out, off = [], 0
    for a in like:
        out.append(flat[off:off + a.size].reshape(a.shape))
        off += a.size
    return out


def kernel(x, ffn1_norm, ffn1_w_in, ffn1_w_out, mix_norm, ffn2_norm, ffn2_w_in, ffn2_w_out, a_w_qkv, a_rel_bias, a_w_o, kv_norm, kv_w_down, kv_latent_norm, kv_w_up, b_w_dq, b_q_norm, b_w_uq, b_w_o, final_norm, loss_target, m_ffn1_norm, m_ffn1_w_in, m_ffn1_w_out, m_mix_norm, m_ffn2_norm, m_ffn2_w_in, m_ffn2_w_out, m_a_w_qkv, m_a_rel_bias, m_a_w_o, m_kv_norm, m_kv_w_down, m_kv_latent_norm, m_kv_w_up, m_b_w_dq, m_b_q_norm, m_b_w_uq, m_b_w_o, m_final_norm, v_ffn1_norm, v_ffn1_w_in, v_ffn1_w_out, v_mix_norm, v_ffn2_norm, v_ffn2_w_in, v_ffn2_w_out, v_a_w_qkv, v_a_rel_bias, v_a_w_o, v_kv_norm, v_kv_w_down, v_kv_latent_norm, v_kv_w_up, v_b_w_dq, v_b_q_norm, v_b_w_uq, v_b_w_o, v_final_norm):
    B, S, D = x.shape
    T = B * S
    HB = D // 128
    QL = b_q_norm.shape[-1]
    KVL = kv_latent_norm.shape[0]
    hpc = HB // N_CHIPS
    tabs = rope_tables(S)
    cidx = lax.axis_index("c").astype(I32).reshape(1)
    kidx = (2 * lax.axis_index("x") + lax.axis_index("y")).astype(I32).reshape(1)

    kv_w_down_p = jnp.pad(kv_w_down, ((0, 0), (0, LANE - ROPE)))[None]
    b_w_uq_p = jnp.pad(b_w_uq.reshape(1, QL, hpc, NOPE + ROPE),
                       ((0, 0), (0, 0), (0, 0), (0, LANE - ROPE))).reshape(1, QL, hpc * 256)
    sharded = [("ffn1_w_in", ffn1_w_in), ("ffn1_w_out", ffn1_w_out), ("ffn2_w_in", ffn2_w_in),
               ("ffn2_w_out", ffn2_w_out), ("a_w_qkv", a_w_qkv), ("a_w_o", a_w_o),
               ("kv_w_down", kv_w_down_p), ("kv_w_up", kv_w_up[None]), ("b_w_dq", b_w_dq),
               ("b_w_uq", b_w_uq_p), ("b_w_o", b_w_o)]
    pieces = [(a, l) for a, (_, w) in enumerate(sharded) for l in range(w.shape[0])]
    shards = [cast_bf16(f"cast_{sharded[a][0]}_{l}", sharded[a][1], l) for a, l in pieces]
    gathered = all_gather_weights(shards)
    W = {(sharded[a][0], l): g for (a, l), g in zip(pieces, gathered)}

    def col(nm, l=0):
        return W[(nm, l)]

    def row(nm, l=0):
        w = W[(nm, l)]
        return w.reshape(N_CHIPS * w.shape[1], w.shape[2])

    bias = rel_bias_tile("rel_bias_tile", a_rel_bias[0])

    def ffn_fwd(tag, h, g, w_in, w_out):
        xn = rms_fwd(f"{tag}_norm", h, g)
        u = mm_colw(f"{tag}_in", xn, w_in, F32)
        act = swiglu_fwd(f"{tag}_act", u)
        return mm_roww(f"{tag}_out", act, w_out, F32, res=h, alpha=0.5), (xn, u, act)

    h0 = x.reshape(T, D)
    h1, sv_f1a = ffn_fwd("l0f1", h0, ffn1_norm[0], col("ffn1_w_in", 0), row("ffn1_w_out", 0))
    hn_a = rms_fwd("l0mix_norm", h1, mix_norm[0])
    qkv = mm_colw("l0_qkv", hn_a, col("a_w_qkv"), BF16).reshape(B, S, 3 * D)
    o_a = attn_a_fwd("l0_attn", qkv, bias).reshape(T, D)
    h2 = mm_roww("l0_attn_out", o_a, row("a_w_o"), F32, res=h1)
    h3, sv_f2a = ffn_fwd("l0f2", h2, ffn2_norm[0], col("ffn2_w_in", 0), row("ffn2_w_out", 0))

    hkv = rms_fwd("kv_norm", h3, kv_norm)
    ckr = mm_roww("kv_down", hkv, row("kv_w_down"), F32)
    ckv, kr = kvprep_fwd("kv_prep", ckr, kv_latent_norm, tabs, B, S)
    kvb = mm_colw("kv_up", ckv, col("kv_w_up"), BF16).reshape(B, S, HB * 256)

    h4, sv_f1b = ffn_fwd("l1f1", h3, ffn1_norm[1], col("ffn1_w_in", 1), row("ffn1_w_out", 1))
    hn_b = rms_fwd("l1mix_norm", h4, mix_norm[1])
    cqp = mm_roww("l1_dq", hn_b, row("b_w_dq"), F32)
    cq = rms_fwd("l1_q_norm", cqp, b_q_norm[0])
    qpre = mm_colw("l1_uq", cq, col("b_w_uq"), F32)
    qf = qprep("l1_q_rope", qpre, tabs, B, S, bwd=False).reshape(B, S, HB * 256)
    o_b, lse = mla_fwd("l1_attn", qf, kvb, kr)
    h5 = mm_roww("l1_attn_out", o_b.reshape(T, HB * LANE), row("b_w_o"), F32, res=h4)
    h6, sv_f2b = ffn_fwd("l1f2", h5, ffn2_norm[1], col("ffn2_w_in", 1), row("ffn2_w_out", 1))

    dh, g_final, loss_part = loss_head("loss_head", h6, final_norm, loss_target.reshape(T, D))

    gw = {}

    def ffn_bwd(tag, dh, h_in, g, w_in, w_out, saved, key_in, key_out):
        xn, u, act = saved
        dact = mm_roww_t(f"{tag}_dact", dh, w_out, F32, alpha=0.5)
        dwo = mm_droww(f"{tag}_dwout", act, dh, alpha=0.5)
        gw[key_out] = dwo.reshape(N_CHIPS, dwo.shape[0] // N_CHIPS, dwo.shape[1])
        du = swiglu_bwd(f"{tag}_dswiglu", u, dact)
        gw[key_in] = mm_dcolw(f"{tag}_dwin", xn, du)
        dxn = mm_colw_t(f"{tag}_dxn", du, w_in, F32)
        return rms_bwd(f"{tag}_dnorm", h_in, g, dxn, dres=dh)

    def chip_major(dw):
        return dw.reshape(N_CHIPS, dw.shape[0] // N_CHIPS, dw.shape[1])

    dh, g_f2b = ffn_bwd("l1f2b", dh, h5, ffn2_norm[1], col("ffn2_w_in", 1), row("ffn2_w_out", 1), sv_f2b,
                        ("ffn2_w_in", 1), ("ffn2_w_out", 1))
    do_b = mm_roww_t("l1_attn_do", dh, row("b_w_o"), BF16).reshape(B, S, HB * LANE)
    gw[("b_w_o", 0)] = chip_major(mm_droww("l1_attn_dwo", o_b.reshape(T, HB * LANE), dh))
    dqf, dkv, dkr = mla_bwd("l1_attn_bwd", qf, kvb, kr, do_b, o_b, lse)
    dqpre = qprep("l1_q_rope_bwd", dqf.reshape(T, HB * 256), tabs, B, S, bwd=True)
    gw[("b_w_uq", 0)] = mm_dcolw("l1_dwuq", cq, dqpre)
    dcq = mm_colw_t("l1_dcq", dqpre, col("b_w_uq"), F32)
    dcqp, g_qn = rms_bwd("l1_dq_norm", cqp, b_q_norm[0], dcq)
    gw[("b_w_dq", 0)] = chip_major(mm_droww("l1_dwdq", hn_b, dcqp))
    dhn = mm_roww_t("l1_dhn", dcqp, row("b_w_dq"), F32)
    dh, g_mixb = rms_bwd("l1_dmix", h4, mix_norm[1], dhn, dres=dh)
    dh, g_f1b = ffn_bwd("l1f1b", dh, h3, ffn1_norm[1], col("ffn1_w_in", 1), row("ffn1_w_out", 1), sv_f1b,
                        ("ffn1_w_in", 1), ("ffn1_w_out", 1))
    dkv2 = dkv.reshape(T, HB * 256)
    gw[("kv_w_up", 0)] = mm_dcolw("kv_dwup", ckv, dkv2)
    dckv = mm_colw_t("kv_dckv", dkv2, col("kv_w_up"), F32)
    dckr, g_lat = kvprep_bwd("kv_prep_bwd", ckr, kv_latent_norm, dckv, dkr, tabs, B, S)
    gw[("kv_w_down", 0)] = chip_major(mm_droww("kv_dwdown", hkv, dckr))
    dhkv = mm_roww_t("kv_dhkv", dckr, row("kv_w_down"), F32)
    dh, g_kvn = rms_bwd("kv_dnorm", h3, kv_norm, dhkv, dres=dh)
    dh, g_f2a = ffn_bwd("l0f2b", dh, h2, ffn2_norm[0], col("ffn2_w_in", 0), row("ffn2_w_out", 0), sv_f2a,
                        ("ffn2_w_in", 0), ("ffn2_w_out", 0))
    do_a = mm_roww_t("l0_attn_do", dh, row("a_w_o"), BF16).reshape(B, S, D)
    gw[("a_w_o", 0)] = chip_major(mm_droww("l0_attn_dwo", o_a, dh))
    dq_a, dkv_a, dbias = attn_a_bwd("l0_attn_bwd", qkv, do_a, bias)
    dqkv = jnp.concatenate([dq_a.reshape(T, D), dkv_a.reshape(T, 2 * D)], axis=1)
    gw[("a_w_qkv", 0)] = mm_dcolw("l0_dwqkv", hn_a, dqkv)
    dhn = mm_colw_t("l0_dhn", dqkv, col("a_w_qkv"), F32)
    dh, g_mixa = rms_bwd("l0_dmix", h1, mix_norm[0], dhn, dres=dh)
    dh, g_f1a = ffn_bwd("l0f1b", dh, h0, ffn1_norm[0], col("ffn1_w_in", 0), row("ffn1_w_out", 0), sv_f1a,
                        ("ffn1_w_in", 0), ("ffn1_w_out", 0))
    grad_x = dh.reshape(B, S, D)
    g_rel = rel_bias_grad("rel_bias_grad", dbias)[:, :2 * MAX_REL + 1][None]

    dws = [gw[(sharded[a][0], l)] for a, l in pieces]
    landed1 = pair_exchange(dws)
    parts = [half_sum(f"half_sum_{i}", dws[i], landed1[i], cidx) for i in range(len(dws))]
    landed2 = chip_exchange(parts)
    halves = [chip_sum(f"chip_sum_{i}", parts[i], landed2[i], kidx) for i in range(len(dws))]
    full = pair_assemble(halves, pieces, [w.shape for _, w in sharded])
    G = {nm: g for (nm, _), g in zip(sharded, full)}
    G["kv_w_down"] = G["kv_w_down"][0, :, :KVL + ROPE]
    G["kv_w_up"] = G["kv_w_up"][0]
    G["b_w_uq"] = G["b_w_uq"].reshape(1, QL, hpc, 256)[..., :NOPE + ROPE].reshape(b_w_uq.shape)

    small = [("ffn1_norm", jnp.stack([g_f1a, g_f1b])), ("mix_norm", jnp.stack([g_mixa, g_mixb])),
             ("ffn2_norm", jnp.stack([g_f2a, g_f2b])), ("a_rel_bias", g_rel), ("kv_norm", g_kvn),
             ("kv_latent_norm", g_lat), ("b_q_norm", g_qn[None]), ("final_norm", g_final)]
    red = all_reduce_small(_pack([loss_part] + [g for _, g in small]))
    unpacked = _unpack(red, [loss_part] + [g for _, g in small])
    loss = unpacked[0][0, 0]
    for (nm, _), g in zip(small, unpacked[1:]):
        G[nm] = g

    given = dict(ffn1_norm=(ffn1_norm, m_ffn1_norm, v_ffn1_norm), ffn1_w_in=(ffn1_w_in, m_ffn1_w_in, v_ffn1_w_in),
                 ffn1_w_out=(ffn1_w_out, m_ffn1_w_out, v_ffn1_w_out), mix_norm=(mix_norm, m_mix_norm, v_mix_norm),
                 ffn2_norm=(ffn2_norm, m_ffn2_norm, v_ffn2_norm), ffn2_w_in=(ffn2_w_in, m_ffn2_w_in, v_ffn2_w_in),
                 ffn2_w_out=(ffn2_w_out, m_ffn2_w_out, v_ffn2_w_out), a_w_qkv=(a_w_qkv, m_a_w_qkv, v_a_w_qkv),
                 a_rel_bias=(a_rel_bias, m_a_rel_bias, v_a_rel_bias), a_w_o=(a_w_o, m_a_w_o, v_a_w_o),
                 kv_norm=(kv_norm, m_kv_norm, v_kv_norm), kv_w_down=(kv_w_down, m_kv_w_down, v_kv_w_down),
                 kv_latent_norm=(kv_latent_norm, m_kv_latent_norm, v_kv_latent_norm),
                 kv_w_up=(kv_w_up, m_kv_w_up, v_kv_w_up), b_w_dq=(b_w_dq, m_b_w_dq, v_b_w_dq),
                 b_q_norm=(b_q_norm, m_b_q_norm, v_b_q_norm), b_w_uq=(b_w_uq, m_b_w_uq, v_b_w_uq),
                 b_w_o=(b_w_o, m_b_w_o, v_b_w_o), final_norm=(final_norm, m_final_norm, v_final_norm))
    order = list(given)
    delta, new_m, new_v = {}, {}, {}
    small_names = [nm for nm, _ in small]
    packed = [_pack([given[nm][k] for nm in small_names]) for k in range(3)]
    outs = adamw("adamw_small", packed[0], _pack([G[nm] for nm in small_names]), packed[1], packed[2])
    for dst, buf in zip((delta, new_m, new_v), outs):
        for nm, a in zip(small_names, _unpack(buf, [given[nm][0] for nm in small_names])):
            dst[nm] = a
    for nm, _ in sharded:
        w, m, v = given[nm]
        g = G[nm].reshape(w.shape)
        G[nm] = g
        two = lambda a: a.reshape(-1, a.shape[-1])
        d_, m_, v_ = adamw(f"adamw_{nm}", two(w), two(g), two(m), two(v))
        delta[nm], new_m[nm], new_v[nm] = d_.reshape(w.shape), m_.reshape(w.shape), v_.reshape(w.shape)

    return (loss, grad_x, *[G[n] for n in order], *[delta[n] for n in order],
            *[new_m[n] for n in order], *[new_v[n] for n in order])
```

```python
import functools
import math

import jax
import jax.numpy as jnp
from jax import lax
from jax.experimental import pallas as pl
from jax.experimental.pallas import tpu as pltpu

F32 = jnp.float32
BF16 = jnp.bfloat16
I32 = jnp.int32

CHUNK = 64
CHUNK_SHIFT = 6
HEAD_DIM_A = 64
LEFT_CHUNKS = 8
MAX_REL = 128
REL_PAD = 384
WIN = (LEFT_CHUNKS + 2) * CHUNK
PADR = WIN - CHUNK
NOPE = 128
ROPE = 64
EPS = 1e-6
NEG_INF = -1e30
ROPE_THETA = 10000.0
ADAM_LR, ADAM_B1, ADAM_B2, ADAM_EPS, ADAM_WD, ADAM_STEP = 0.001, 0.9, 0.999, 1e-08, 0.01, 10
N_CHIPS = 4
LANE = 128
MESH = pl.DeviceIdType.MESH
VMEM_CAP_MB = 60

NN = (((1,), (0,)), ((), ()))
NT = (((1,), (1,)), ((), ()))
TN = (((0,), (0,)), ((), ()))


def _tile(n, pref, mult):
    t = (min(pref, n) // mult) * mult
    while t >= mult:
        if n % t == 0:
            return t
        t -= mult
    return n


def _nbytes(shape, dtype):
    return math.prod(shape) * jnp.dtype(dtype).itemsize


def _params(block_bytes, extra_bytes=0):
    need = 2 * block_bytes + extra_bytes
    mb = min(VMEM_CAP_MB, max(32, int(need * 1.25 / 2**20) + 8))
    return pltpu.CompilerParams(vmem_limit_bytes=mb * 2**20)


def _mm(name, kind, a, b, grid, a_spec, b_spec, o_spec, out_shape, out_dtype, blocks,
        red_axis=None, nred=1, alpha=1.0, res=None, res_spec=None):
    dims = {"nn": NN, "nt": NT, "tn": TN}[kind]
    has_res = res is not None
    acc_in_out = nred > 1 and out_dtype == F32 and not has_res and alpha == 1.0

    def body(*refs):
        a_ref, b_ref = refs[0], refs[1]
        r_ref = refs[2] if has_res else None
        o_ref = refs[3] if has_res else refs[2]
        p = lax.dot_general(a_ref[...].astype(BF16), b_ref[...].astype(BF16), dims,
                            preferred_element_type=F32)

        def finish(acc):
            y = acc if alpha == 1.0 else acc * alpha
            if has_res:
                y = r_ref[...] + y
            o_ref[...] = y.astype(o_ref.dtype)

        if nred == 1:
            finish(p)
            return
        k = pl.program_id(red_axis)
        tgt = o_ref if acc_in_out else refs[-1]

        @pl.when(k == 0)
        def _():
            tgt[...] = p

        @pl.when(k > 0)
        def _():
            tgt[...] += p

        if not acc_in_out:
            @pl.when(k == nred - 1)
            def _():
                finish(tgt[...])

    a_blk, b_blk, o_blk = blocks
    scratch = []
    extra = 0
    if nred > 1 and not acc_in_out:
        scratch = [pltpu.VMEM(o_blk, F32)]
        extra = _nbytes(o_blk, F32)
    blk = _nbytes(a_blk, a.dtype) + _nbytes(b_blk, b.dtype) + _nbytes(o_blk, out_dtype)
    ins, specs = [a, b], [a_spec, b_spec]
    if has_res:
        ins.append(res)
        specs.append(res_spec)
        blk += _nbytes(o_blk, res.dtype)
    extra += _nbytes(a_blk, BF16) + _nbytes(b_blk, BF16) + 2 * _nbytes(o_blk, F32)
    return pl.pallas_call(
        body, name=name, grid=grid, in_specs=specs, out_specs=o_spec,
        out_shape=jax.ShapeDtypeStruct(out_shape, out_dtype), scratch_shapes=scratch,
        compiler_params=_params(blk, extra),
    )(*ins)


def mm_colw(name, x, w3, out_dtype):
    T, K = x.shape
    _, _, nl = w3.shape
    tm = _tile(T, 512, 8)
    return _mm(name, "nn", x, w3, (N_CHIPS, T // tm),
               pl.BlockSpec((tm, K), lambda j, i: (i, 0)),
               pl.BlockSpec((None, K, nl), lambda j, i: (j, 0, 0)),
               pl.BlockSpec((tm, nl), lambda j, i: (i, j)),
               (T, N_CHIPS * nl), out_dtype, ((tm, K), (K, nl), (tm, nl)))


def mm_colw_t(name, dy, w3, out_dtype, res=None):
    T = dy.shape[0]
    _, K, nl = w3.shape
    tm = _tile(T, 512, 8)
    return _mm(name, "nt", dy, w3, (T // tm, N_CHIPS),
               pl.BlockSpec((tm, nl), lambda i, j: (i, j)),
               pl.BlockSpec((None, K, nl), lambda i, j: (j, 0, 0)),
               pl.BlockSpec((tm, K), lambda i, j: (i, 0)),
               (T, K), out_dtype, ((tm, nl), (K, nl), (tm, K)),
               red_axis=1, nred=N_CHIPS, res=res,
               res_spec=pl.BlockSpec((tm, K), lambda i, j: (i, 0)))


def mm_dcolw(name, x, dy):
    T, K = x.shape
    nl = dy.shape[1] // N_CHIPS
    tt = _tile(T, 512, 8)
    return _mm(name, "tn", x, dy, (N_CHIPS, T // tt),
               pl.BlockSpec((tt, K), lambda j, t: (t, 0)),
               pl.BlockSpec((tt, nl), lambda j, t: (t, j)),
               pl.BlockSpec((None, K, nl), lambda j, t: (j, 0, 0)),
               (N_CHIPS, K, nl), BF16, ((tt, K), (tt, nl), (K, nl)),
               red_axis=1, nred=T // tt)


def mm_roww(name, x, w2, out_dtype, res=None, alpha=1.0):
    T, Kt = x.shape
    N = w2.shape[1]
    tm = _tile(T, 512, 8)
    return _mm(name, "nn", x, w2, (T // tm,),
               pl.BlockSpec((tm, Kt), lambda i: (i, 0)),
               pl.BlockSpec((Kt, N), lambda i: (0, 0)),
               pl.BlockSpec((tm, N), lambda i: (i, 0)),
               (T, N), out_dtype, ((tm, Kt), (Kt, N), (tm, N)),
               alpha=alpha, res=res, res_spec=pl.BlockSpec((tm, N), lambda i: (i, 0)))


def mm_roww_t(name, dy, w2, out_dtype, alpha=1.0):
    T, N = dy.shape
    Kt = w2.shape[0]
    tm = _tile(T, 512, 8)
    tk = _tile(Kt, 1408, LANE)
    return _mm(name, "nt", dy, w2, (Kt // tk, T // tm),
               pl.BlockSpec((tm, N), lambda j, i: (i, 0)),
               pl.BlockSpec((tk, N), lambda j, i: (j, 0)),
               pl.BlockSpec((tm, tk), lambda j, i: (i, j)),
               (T, Kt), out_dtype, ((tm, N), (tk, N), (tm, tk)), alpha=alpha)


def mm_droww(name, x, dy, alpha=1.0):
    T, Kt = x.shape
    N = dy.shape[1]
    tt = _tile(T, 512, 8)
    tk = _tile(Kt, 1408, LANE)
    return _mm(name, "tn", x, dy, (Kt // tk, T // tt),
               pl.BlockSpec((tt, tk), lambda j, t: (t, j)),
               pl.BlockSpec((tt, N), lambda j, t: (t, 0)),
               pl.BlockSpec((tk, N), lambda j, t: (j, 0)),
               (Kt, N), BF16, ((tt, tk), (tt, N), (tk, N)),
               red_axis=1, nred=T // tt, alpha=alpha)


def rms_fwd(name, x, g):
    T, D = x.shape
    tm = _tile(T, 512, 8)

    def body(x_ref, g_ref, o_ref):
        xv = x_ref[...]
        r = lax.rsqrt(jnp.mean(xv * xv, axis=-1, keepdims=True) + EPS)
        o_ref[...] = (xv * r * g_ref[...]).astype(o_ref.dtype)

    return pl.pallas_call(
        body, name=name, grid=(T // tm,),
        in_specs=[pl.BlockSpec((tm, D), lambda i: (i, 0)), pl.BlockSpec((1, D), lambda i: (0, 0))],
        out_specs=pl.BlockSpec((tm, D), lambda i: (i, 0)),
        out_shape=jax.ShapeDtypeStruct((T, D), BF16),
        compiler_params=_params(_nbytes((tm, D), F32) * 2, 4 * _nbytes((tm, D), F32)),
    )(x, g.reshape(1, D))


def _rms_bwd_math(xv, gv, dy):
    r = lax.rsqrt(jnp.mean(xv * xv, axis=-1, keepdims=True) + EPS)
    xh = xv * r
    dyg = dy * gv
    dx = r * (dyg - xh * jnp.mean(dyg * xh, axis=-1, keepdims=True))
    dg = jnp.sum(dy * xh, axis=0, keepdims=True)
    return dx, dg


def rms_bwd(name, x, g, dy, dres=None):
    T, D = x.shape
    tm = _tile(T, 256, 8)
    has_res = dres is not None

    def body(*refs):
        x_ref, g_ref, dy_ref = refs[:3]
        r_ref = refs[3] if has_res else None
        dx_ref, dg_ref = refs[-2:]
        dx, dg = _rms_bwd_math(x_ref[...], g_ref[...], dy_ref[...].astype(F32))
        if has_res:
            dx = r_ref[...] + dx
        dx_ref[...] = dx

        @pl.when(pl.program_id(0) == 0)
        def _():
            dg_ref[...] = dg

        @pl.when(pl.program_id(0) > 0)
        def _():
            dg_ref[...] += dg

    row = pl.BlockSpec((tm, D), lambda i: (i, 0))
    vec = pl.BlockSpec((1, D), lambda i: (0, 0))
    ins, specs = [x, g.reshape(1, D), dy], [row, vec, row]
    if has_res:
        ins.append(dres)
        specs.append(row)
    dx, dg = pl.pallas_call(
        body, name=name, grid=(T // tm,), in_specs=specs, out_specs=[row, vec],
        out_shape=[jax.ShapeDtypeStruct((T, D), F32), jax.ShapeDtypeStruct((1, D), F32)],
        compiler_params=_params(_nbytes((tm, D), F32) * 4, 6 * _nbytes((tm, D), F32)),
    )(*ins)
    return dx, dg.reshape(D)


def swiglu_fwd(name, u):
    T, F2 = u.shape
    F = F2 // 2
    tm = _tile(T, 256, 8)

    def body(u_ref, o_ref):
        u1 = u_ref[:, :F]
        u2 = u_ref[:, F:]
        o_ref[...] = (u1 * jax.nn.sigmoid(u1) * u2).astype(o_ref.dtype)

    return pl.pallas_call(
        body, name=name, grid=(T // tm,),
        in_specs=[pl.BlockSpec((tm, F2), lambda i: (i, 0))],
        out_specs=pl.BlockSpec((tm, F), lambda i: (i, 0)),
        out_shape=jax.ShapeDtypeStruct((T, F), BF16),
        compiler_params=_params(_nbytes((tm, F2), F32) * 2, _nbytes((tm, F2), F32) * 2),
    )(u)


def swiglu_bwd(name, u, dact):
    T, F2 = u.shape
    F = F2 // 2
    tm = _tile(T, 256, 8)

    def body(u_ref, d_ref, o_ref):
        u1 = u_ref[:, :F]
        u2 = u_ref[:, F:]
        d = d_ref[...]
        sig = jax.nn.sigmoid(u1)
        silu = u1 * sig
        o_ref[:, :F] = (d * u2 * (sig * (1.0 + u1 * (1.0 - sig)))).astype(o_ref.dtype)
        o_ref[:, F:] = (d * silu).astype(o_ref.dtype)

    return pl.pallas_call(
        body, name=name, grid=(T // tm,),
        in_specs=[pl.BlockSpec((tm, F2), lambda i: (i, 0)), pl.BlockSpec((tm, F), lambda i: (i, 0))],
        out_specs=pl.BlockSpec((tm, F2), lambda i: (i, 0)),
        out_shape=jax.ShapeDtypeStruct((T, F2), BF16),
        compiler_params=_params(_nbytes((tm, F2), F32) * 2, _nbytes((tm, F2), F32) * 3),
    )(u, dact)


def loss_head(name, h, g, target):
    T, D = h.shape
    tm = _tile(T, 256, 8)

    def body(h_ref, g_ref, t_ref, dh_ref, dg_ref, loss_ref):
        xv = h_ref[...]
        gv = g_ref[...]
        r = lax.rsqrt(jnp.mean(xv * xv, axis=-1, keepdims=True) + EPS)
        err = xv * r * gv - t_ref[...]
        part = 0.5 * jnp.sum(jnp.mean(err * err, axis=-1, keepdims=True), axis=0, keepdims=True)
        dx, dg = _rms_bwd_math(xv, gv, err * (1.0 / D))
        dh_ref[...] = dx
        part = jnp.broadcast_to(part, (1, LANE))

        @pl.when(pl.program_id(0) == 0)
        def _():
            dg_ref[...] = dg
            loss_ref[...] = part

        @pl.when(pl.program_id(0) > 0)
        def _():
            dg_ref[...] += dg
            loss_ref[...] += part

    row = pl.BlockSpec((tm, D), lambda i: (i, 0))
    vec = pl.BlockSpec((1, D), lambda i: (0, 0))
    dh, dg, loss = pl.pallas_call(
        body, name=name, grid=(T // tm,), in_specs=[row, vec, row],
        out_specs=[row, vec, pl.BlockSpec((1, LANE), lambda i: (0, 0))],
        out_shape=[jax.ShapeDtypeStruct((T, D), F32), jax.ShapeDtypeStruct((1, D), F32),
                   jax.ShapeDtypeStruct((1, LANE), F32)],
        compiler_params=_params(_nbytes((tm, D), F32) * 3, 6 * _nbytes((tm, D), F32)),
    )(h, g.reshape(1, D), target)
    return dh, dg.reshape(D), loss


def rope_tables(S):
    half = ROPE // 2
    freqs = ROPE_THETA ** (-jnp.arange(half, dtype=F32) / half)
    ang = jnp.arange(S, dtype=F32)[:, None] * freqs[None, :]
    cos, sin = jnp.cos(ang), jnp.sin(ang)
    z = jnp.zeros_like(cos)
    ct = jnp.concatenate([cos, cos, z, z], axis=1)
    s1 = jnp.concatenate([-sin, z, z, z], axis=1)
    s2 = jnp.concatenate([z, sin, z, z], axis=1)
    return ct, s1, s2


def _rope_tile(t, ct, s1, s2):
    return t * ct + pltpu.roll(t, 96, 1) * s1 + pltpu.roll(t, 32, 1) * s2


def _rope_tile_bwd(d, ct, s1, s2):
    return d * ct + pltpu.roll(d * s1, 32, 1) + pltpu.roll(d * s2, 96, 1)


def qprep(name, q, tabs, B, S, bwd):
    T, W = q.shape
    nh = W // 256
    ts = _tile(S, 256, 8)
    fn = _rope_tile_bwd if bwd else _rope_tile

    def body(q_ref, ct_ref, s1_ref, s2_ref, o_ref):
        ct, s1, s2 = ct_ref[...], s1_ref[...], s2_ref[...]
        for h in range(nh):
            o_ref[0, :, 256 * h:256 * h + 128] = q_ref[0, :, 256 * h:256 * h + 128].astype(o_ref.dtype)
            t = q_ref[0, :, 256 * h + 128:256 * h + 256].astype(F32)
            o_ref[0, :, 256 * h + 128:256 * h + 256] = fn(t, ct, s1, s2).astype(o_ref.dtype)

    row = pl.BlockSpec((1, ts, W), lambda b, s: (b, s, 0))
    tab = pl.BlockSpec((ts, LANE), lambda b, s: (s, 0))
    out = pl.pallas_call(
        body, name=name, grid=(B, S // ts), in_specs=[row, tab, tab, tab], out_specs=row,
        out_shape=jax.ShapeDtypeStruct((B, S, W), BF16),
        compiler_params=_params(_nbytes((ts, W), F32) * 2, _nbytes((ts, W), F32) * 2),
    )(q.reshape(B, S, W), *tabs)
    return out.reshape(T, W)


def kvprep_fwd(name, ckr, g, tabs, B, S):
    T, W = ckr.shape
    KVL = W - LANE
    ts = _tile(S, 256, 8)

    def body(x_ref, g_ref, ct_ref, s1_ref, s2_ref, c_ref, k_ref):
        xv = x_ref[0, :, :KVL]
        r = lax.rsqrt(jnp.mean(xv * xv, axis=-1, keepdims=True) + EPS)
        c_ref[0] = (xv * r * g_ref[...]).astype(c_ref.dtype)
        k_ref[0] = _rope_tile(x_ref[0, :, KVL:], ct_ref[...], s1_ref[...], s2_ref[...]).astype(k_ref.dtype)

    tab = pl.BlockSpec((ts, LANE), lambda b, s: (s, 0))
    c, k = pl.pallas_call(
        body, name=name, grid=(B, S // ts),
        in_specs=[pl.BlockSpec((1, ts, W), lambda b, s: (b, s, 0)), pl.BlockSpec((1, KVL), lambda b, s: (0, 0)),
                  tab, tab, tab],
        out_specs=[pl.BlockSpec((1, ts, KVL), lambda b, s: (b, s, 0)),
                   pl.BlockSpec((1, ts, LANE), lambda b, s: (b, s, 0))],
        out_shape=[jax.ShapeDtypeStruct((B, S, KVL), BF16), jax.ShapeDtypeStruct((B, S, LANE), BF16)],
        compiler_params=_params(_nbytes((ts, W), F32) * 2, _nbytes((ts, W), F32) * 2),
    )(ckr.reshape(B, S, W), g.reshape(1, KVL), *tabs)
    return c.reshape(T, KVL), k


def kvprep_bwd(name, ckr, g, dc, dkr, tabs, B, S):
    T, W = ckr.shape
    KVL = W - LANE
    ts = _tile(S, 256, 8)

    def body(x_ref, g_ref, dc_ref, dk_ref, ct_ref, s1_ref, s2_ref, o_ref, dg_ref):
        dx, dg = _rms_bwd_math(x_ref[0, :, :KVL], g_ref[...], dc_ref[0])
        o_ref[0, :, :KVL] = dx
        o_ref[0, :, KVL:] = _rope_tile_bwd(dk_ref[0], ct_ref[...], s1_ref[...], s2_ref[...])
        first = (pl.program_id(0) == 0) & (pl.program_id(1) == 0)

        @pl.when(first)
        def _():
            dg_ref[...] = dg

        @pl.when(jnp.logical_not(first))
        def _():
            dg_ref[...] += dg

    tab = pl.BlockSpec((ts, LANE), lambda b, s: (s, 0))
    vec = pl.BlockSpec((1, KVL), lambda b, s: (0, 0))
    o, dg = pl.pallas_call(
        body, name=name, grid=(B, S // ts),
        in_specs=[pl.BlockSpec((1, ts, W), lambda b, s: (b, s, 0)), vec,
                  pl.BlockSpec((1, ts, KVL), lambda b, s: (b, s, 0)),
                  pl.BlockSpec((1, ts, LANE), lambda b, s: (b, s, 0)), tab, tab, tab],
        out_specs=[pl.BlockSpec((1, ts, W), lambda b, s: (b, s, 0)), vec],
        out_shape=[jax.ShapeDtypeStruct((B, S, W), F32), jax.ShapeDtypeStruct((1, KVL), F32)],
        compiler_params=_params(_nbytes((ts, W), F32) * 4, _nbytes((ts, W), F32) * 4),
    )(ckr.reshape(B, S, W), g.reshape(1, KVL), dc.reshape(B, S, KVL), dkr, *tabs)
    return o.reshape(T, W), dg.reshape(KVL)


def _rel_onehot(i):
    col = lax.broadcasted_iota(I32, (REL_PAD, WIN), 1)
    row = lax.broadcasted_iota(I32, (REL_PAD, WIN), 0)
    idx = jnp.clip(PADR + i - col, -MAX_REL, MAX_REL) + MAX_REL
    return (row == idx).astype(F32)


def rel_bias_tile(name, table):
    H = table.shape[0]
    tpad = jnp.pad(table, ((0, 0), (0, REL_PAD - table.shape[1])))

    def body(t_ref, o_ref):
        base = pl.program_id(0) * 8
        for ii in range(8):
            o_ref[:, ii, :] = lax.dot_general(t_ref[...], _rel_onehot(base + ii), NN,
                                              precision=lax.Precision.HIGHEST, preferred_element_type=F32)

    return pl.pallas_call(
        body, name=name, grid=(CHUNK // 8,),
        in_specs=[pl.BlockSpec((H, REL_PAD), lambda s: (0, 0))],
        out_specs=pl.BlockSpec((H, 8, WIN), lambda s: (0, s, 0)),
        out_shape=jax.ShapeDtypeStruct((H, CHUNK, WIN), F32),
    )(tpad)


def rel_bias_grad(name, dbias):
    H = dbias.shape[0]

    def body(d_ref, o_ref):
        base = pl.program_id(0) * 8
        acc = jnp.zeros((H, REL_PAD), F32)
        for ii in range(8):
            acc = acc + lax.dot_general(d_ref[:, ii, :], _rel_onehot(base + ii), NT,
                                        precision=lax.Precision.HIGHEST, preferred_element_type=F32)

        @pl.when(pl.program_id(0) == 0)
        def _():
            o_ref[...] = acc

        @pl.when(pl.program_id(0) > 0)
        def _():
            o_ref[...] += acc

    return pl.pallas_call(
        body, name=name, grid=(CHUNK // 8,),
        in_specs=[pl.BlockSpec((H, 8, WIN), lambda s: (0, s, 0))],
        out_specs=pl.BlockSpec((H, REL_PAD), lambda s: (0, 0)),
        out_shape=jax.ShapeDtypeStruct((H, REL_PAD), F32),
    )(dbias)


def _attn_a_load_kv(qkv_hbm, b, kpad, vpad, sem, S, D):
    kpad[0:PADR, :] = jnp.zeros((PADR, D), BF16)
    vpad[0:PADR, :] = jnp.zeros((PADR, D), BF16)
    ck = pltpu.make_async_copy(qkv_hbm.at[b, :, pl.ds(D, D)], kpad.at[pl.ds(PADR, S), :], sem.at[0])
    cv = pltpu.make_async_copy(qkv_hbm.at[b, :, pl.ds(2 * D, D)], vpad.at[pl.ds(PADR, S), :], sem.at[1])
    ck.start()
    cv.start()
    ck.wait()
    cv.wait()


def _attn_a_probs(qm, kp, bias, valid, scale):
    s = lax.dot_general(qm, kp, NT, preferred_element_type=F32) * scale + bias
    s = jnp.where(valid, s, NEG_INF)
    e = jnp.exp(s - jnp.max(s, axis=-1, keepdims=True))
    return e * (1.0 / jnp.sum(e, axis=-1, keepdims=True))


def attn_a_fwd(name, qkv, bias):
    B, S, D3 = qkv.shape
    D = D3 // 3
    H = D // HEAD_DIM_A
    nc = S // CHUNK
    scale = HEAD_DIM_A ** -0.5

    def body(q_ref, bias_ref, qkv_hbm, o_ref, kpad, vpad, sem):
        b, c = pl.program_id(0), pl.program_id(1)

        @pl.when(c == 0)
        def _():
            _attn_a_load_kv(qkv_hbm, b, kpad, vpad, sem, S, D)

        col = lax.broadcasted_iota(I32, (CHUNK, WIN), 1)
        valid = col >= jnp.maximum(CHUNK, (LEFT_CHUNKS + 1 - c) * CHUNK)
        lane = lax.broadcasted_iota(I32, (CHUNK, LANE), 1)
        w0 = pl.multiple_of(c * CHUNK, CHUNK)
        for p in range(H // 2):
            ls = slice(p * LANE, (p + 1) * LANE)
            qp = q_ref[0, :, ls]
            kp = kpad[pl.ds(w0, WIN), ls]
            vp = vpad[pl.ds(w0, WIN), ls]
            halves = []
            for hh in range(2):
                sel = (lane < HEAD_DIM_A) if hh == 0 else (lane >= HEAD_DIM_A)
                qm = jnp.where(sel, qp, jnp.zeros_like(qp))
                pr = _attn_a_probs(qm, kp, bias_ref[2 * p + hh], valid, scale)
                halves.append(jnp.dot(pr.astype(BF16), vp, preferred_element_type=F32))
            o_ref[0, :, ls] = jnp.where(lane < HEAD_DIM_A, halves[0], halves[1]).astype(o_ref.dtype)

    blk = _nbytes((CHUNK, D), BF16) * 2 + _nbytes((H, CHUNK, WIN), F32)
    return pl.pallas_call(
        body, name=name, grid=(B, nc),
        in_specs=[pl.BlockSpec((1, CHUNK, D), lambda b, c: (b, c, 0)),
                  pl.BlockSpec((H, CHUNK, WIN), lambda b, c: (0, 0, 0)),
                  pl.BlockSpec(memory_space=pl.ANY)],
        out_specs=pl.BlockSpec((1, CHUNK, D), lambda b, c: (b, c, 0)),
        out_shape=jax.ShapeDtypeStruct((B, S, D), BF16),
        scratch_shapes=[pltpu.VMEM((PADR + S, D), BF16), pltpu.VMEM((PADR + S, D), BF16),
                        pltpu.SemaphoreType.DMA((2,))],
        compiler_params=_params(blk, 2 * _nbytes((PADR + S, D), BF16)),
    )(qkv, bias, qkv)


def attn_a_bwd(name, qkv, do, bias):
    B, S, D3 = qkv.shape
    D = D3 // 3
    H = D // HEAD_DIM_A
    nc = S // CHUNK
    scale = HEAD_DIM_A ** -0.5

    def body(q_ref, do_ref, bias_ref, qkv_hbm, dq_ref, dkv_hbm, dbias_ref, kpad, vpad, dkacc, dvacc, sem):
        b, c = pl.program_id(0), pl.program_id(1)

        @pl.when(c == 0)
        def _():
            _attn_a_load_kv(qkv_hbm, b, kpad, vpad, sem, S, D)
            dkacc[...] = jnp.zeros_like(dkacc)
            dvacc[...] = jnp.zeros_like(dvacc)

        @pl.when((b == 0) & (c == 0))
        def _():
            dbias_ref[...] = jnp.zeros_like(dbias_ref)

        col = lax.broadcasted_iota(I32, (CHUNK, WIN), 1)
        valid = col >= jnp.maximum(CHUNK, (LEFT_CHUNKS + 1 - c) * CHUNK)
        lane = lax.broadcasted_iota(I32, (CHUNK, LANE), 1)
        w0 = pl.multiple_of(c * CHUNK, CHUNK)
        for p in range(H // 2):
            ls = slice(p * LANE, (p + 1) * LANE)
            qp = q_ref[0, :, ls]
            dop = do_ref[0, :, ls]
            kp = kpad[pl.ds(w0, WIN), ls]
            vp = vpad[pl.ds(w0, WIN), ls]
            dq_pair = jnp.zeros((CHUNK, LANE), F32)
            dk_pair = jnp.zeros((WIN, LANE), F32)
            dv_pair = jnp.zeros((WIN, LANE), F32)
            for hh in range(2):
                sel = (lane < HEAD_DIM_A) if hh == 0 else (lane >= HEAD_DIM_A)
                qm = jnp.where(sel, qp, jnp.zeros_like(qp))
                dom = jnp.where(sel, dop, jnp.zeros_like(dop))
                pr = _attn_a_probs(qm, kp, bias_ref[2 * p + hh], valid, scale)
                dp = lax.dot_general(dom, vp, NT, preferred_element_type=F32)
                ds = pr * (dp - jnp.sum(pr * dp, axis=-1, keepdims=True))
                dbias_ref[2 * p + hh] += ds
                dsb = (ds * scale).astype(BF16)
                dq_pair = jnp.where(sel, jnp.dot(dsb, kp, preferred_element_type=F32), dq_pair)
                dk_pair = dk_pair + lax.dot_general(dsb, qm, TN, preferred_element_type=F32)
                dv_pair = dv_pair + lax.dot_general(pr.astype(BF16), dom, TN, preferred_element_type=F32)
            dq_ref[0, :, ls] = dq_pair
            dkacc[pl.ds(w0, WIN), ls] += dk_pair
            dvacc[pl.ds(w0, WIN), ls] += dv_pair

        @pl.when(c == nc - 1)
        def _():
            ck = pltpu.make_async_copy(dkacc.at[pl.ds(PADR, S), :], dkv_hbm.at[b, :, pl.ds(0, D)], sem.at[0])
            cv = pltpu.make_async_copy(dvacc.at[pl.ds(PADR, S), :], dkv_hbm.at[b, :, pl.ds(D, D)], sem.at[1])
            ck.start()
            cv.start()
            ck.wait()
            cv.wait()

    blk = _nbytes((CHUNK, D), BF16) * 2 + _nbytes((CHUNK, D), F32) + 2 * _nbytes((H, CHUNK, WIN), F32)
    scr = 2 * _nbytes((PADR + S, D), BF16) + 2 * _nbytes((PADR + S, D), F32)
    return pl.pallas_call(
        body, name=name, grid=(B, nc),
        in_specs=[pl.BlockSpec((1, CHUNK, D), lambda b, c: (b, c, 0)),
                  pl.BlockSpec((1, CHUNK, D), lambda b, c: (b, c, 0)),
                  pl.BlockSpec((H, CHUNK, WIN), lambda b, c: (0, 0, 0)),
                  pl.BlockSpec(memory_space=pl.ANY)],
        out_specs=[pl.BlockSpec((1, CHUNK, D), lambda b, c: (b, c, 0)),
                   pl.BlockSpec(memory_space=pl.ANY),
                   pl.BlockSpec((H, CHUNK, WIN), lambda b, c: (0, 0, 0))],
        out_shape=[jax.ShapeDtypeStruct((B, S, D), F32), jax.ShapeDtypeStruct((B, S, 2 * D), F32),
                   jax.ShapeDtypeStruct((H, CHUNK, WIN), F32)],
        scratch_shapes=[pltpu.VMEM((PADR + S, D), BF16), pltpu.VMEM((PADR + S, D), BF16),
                        pltpu.VMEM((PADR + S, D), F32), pltpu.VMEM((PADR + S, D), F32),
                        pltpu.SemaphoreType.DMA((2,))],
        compiler_params=_params(blk, scr),
    )(qkv, do, bias, qkv)


def _mla_scores(q, kblk, krb, q0, k0, scale):
    s = (lax.dot_general(q[:, :NOPE], kblk[:, :NOPE], NT, preferred_element_type=F32)
         + lax.dot_general(q[:, NOPE:], krb, NT, preferred_element_type=F32)) * scale
    qc = jnp.right_shift(q0 + lax.broadcasted_iota(I32, s.shape, 0), CHUNK_SHIFT)
    kc = jnp.right_shift(k0 + lax.broadcasted_iota(I32, s.shape, 1), CHUNK_SHIFT)
    return s, kc <= qc


def mla_fwd(name, qf, kv, kr):
    B, S, W = qf.shape
    HB = W // 256
    QB = _tile(S, 256, CHUNK)
    nq = S // QB
    scale = (NOPE + ROPE) ** -0.5

    def body(q_ref, kv_ref, kr_ref, o_ref, lse_ref):
        qi = pl.program_id(2)
        q = q_ref[0]

        def step(kj, carry):
            m, l, acc = carry
            ks = pl.ds(pl.multiple_of(kj * QB, QB), QB)
            kblk = kv_ref[0, ks, :]
            s, mask = _mla_scores(q, kblk, kr_ref[0, ks, :], qi * QB, kj * QB, scale)
            s = jnp.where(mask, s, NEG_INF)
            m_new = jnp.maximum(m, jnp.max(s, axis=-1, keepdims=True))
            a = jnp.exp(m - m_new)
            p = jnp.exp(s - m_new)
            l = a * l + jnp.sum(p, axis=-1, keepdims=True)
            acc = a * acc + jnp.dot(p.astype(BF16), kblk[:, NOPE:], preferred_element_type=F32)
            return m_new, l, acc

        init = (jnp.full((QB, 1), NEG_INF, F32), jnp.zeros((QB, 1), F32), jnp.zeros((QB, NOPE), F32))
        m, l, acc = lax.fori_loop(0, qi + 1, step, init)
        o_ref[0] = acc * (1.0 / l)
        lse_ref[0, 0] = jnp.broadcast_to(m + jnp.log(l), (QB, LANE))

    blk = (_nbytes((QB, 256), BF16) + _nbytes((S, 256), BF16) + _nbytes((S, LANE), BF16)
           + 2 * _nbytes((QB, LANE), F32))
    return pl.pallas_call(
        body, name=name, grid=(B, HB, nq),
        in_specs=[pl.BlockSpec((1, QB, 256), lambda b, h, i: (b, i, h)),
                  pl.BlockSpec((1, S, 256), lambda b, h, i: (b, 0, h)),
                  pl.BlockSpec((1, S, LANE), lambda b, h, i: (b, 0, 0))],
        out_specs=[pl.BlockSpec((1, QB, LANE), lambda b, h, i: (b, i, h)),
                   pl.BlockSpec((1, 1, QB, LANE), lambda b, h, i: (b, h, i, 0))],
        out_shape=[jax.ShapeDtypeStruct((B, S, HB * LANE), F32), jax.ShapeDtypeStruct((B, HB, S, LANE), F32)],
        compiler_params=_params(blk, 8 * _nbytes((QB, QB), F32)),
    )(qf, kv, kr)


def mla_bwd(name, qf, kv, kr, do, o, lse):
    B, S, W = qf.shape
    HB = W // 256
    QB = _tile(S, 256, CHUNK)
    nq = S // QB
    scale = (NOPE + ROPE) ** -0.5

    def body(q_ref, kv_ref, kr_ref, do_ref, o_ref, lse_ref, dq_ref, dkv_ref, dkr_ref):
        h = pl.program_id(1)
        dq_ref[...] = jnp.zeros_like(dq_ref)
        dkv_ref[...] = jnp.zeros_like(dkv_ref)

        @pl.when(h == 0)
        def _():
            dkr_ref[...] = jnp.zeros_like(dkr_ref)

        for qi in range(nq):
            rows = slice(qi * QB, (qi + 1) * QB)
            q = q_ref[0, rows, :]
            dob = do_ref[0, rows, :]
            lse_q = lse_ref[0, 0, rows, :][:, :1]
            delta = jnp.sum(dob.astype(F32) * o_ref[0, rows, :], axis=-1, keepdims=True)

            def step(kj, carry, q=q, dob=dob, lse_q=lse_q, delta=delta, rows=rows, qi=qi):
                ks = pl.ds(pl.multiple_of(kj * QB, QB), QB)
                kblk = kv_ref[0, ks, :]
                krb = kr_ref[0, ks, :]
                s, mask = _mla_scores(q, kblk, krb, qi * QB, kj * QB, scale)
                p = jnp.where(mask, jnp.exp(s - lse_q), 0.0)
                dp = lax.dot_general(dob, kblk[:, NOPE:], NT, preferred_element_type=F32)
                dsb = (p * (dp - delta) * scale).astype(BF16)
                dkv_ref[0, ks, NOPE:] += lax.dot_general(p.astype(BF16), dob, TN, preferred_element_type=F32)
                dkv_ref[0, ks, :NOPE] += lax.dot_general(dsb, q[:, :NOPE], TN, preferred_element_type=F32)
                dkr_ref[0, ks, :] += lax.dot_general(dsb, q[:, NOPE:], TN, preferred_element_type=F32)
                dq_ref[0, rows, :NOPE] += jnp.dot(dsb, kblk[:, :NOPE], preferred_element_type=F32)
                dq_ref[0, rows, NOPE:] += jnp.dot(dsb, krb, preferred_element_type=F32)
                return carry

            lax.fori_loop(0, qi + 1, step, 0)

    head = lambda w: pl.BlockSpec((1, S, w), lambda b, h: (b, 0, h))
    shared = pl.BlockSpec((1, S, LANE), lambda b, h: (b, 0, 0))
    blk = (2 * _nbytes((S, 256), BF16) + 2 * _nbytes((S, LANE), BF16) + 2 * _nbytes((S, LANE), F32)
           + 2 * _nbytes((S, 256), F32) + _nbytes((S, LANE), F32))
    return pl.pallas_call(
        body, name=name, grid=(B, HB),
        in_specs=[head(256), head(256), shared, head(LANE), head(LANE),
                  pl.BlockSpec((1, 1, S, LANE), lambda b, h: (b, h, 0, 0))],
        out_specs=[head(256), head(256), shared],
        out_shape=[jax.ShapeDtypeStruct((B, S, W), F32), jax.ShapeDtypeStruct((B, S, W), F32),
                   jax.ShapeDtypeStruct((B, S, LANE), F32)],
        compiler_params=_params(blk, 10 * _nbytes((QB, QB), F32)),
    )(qf, kv, kr, do, o, lse)


def cast_bf16(name, w, layer, idx):
    _, R, C = w.shape
    tr = _tile(R, 256, 16)

    def body(k_ref, w_ref, o_ref):
        o_ref[...] = w_ref[...].astype(BF16)

    return pl.pallas_call(
        body, name=name,
        grid_spec=pltpu.PrefetchScalarGridSpec(
            num_scalar_prefetch=1, grid=(R // tr,),
            in_specs=[pl.BlockSpec((None, tr, C), lambda r, k_ref: (layer, r, 0))],
            out_specs=pl.BlockSpec((None, tr, C), lambda r, k_ref: (k_ref[0], r, 0))),
        out_shape=jax.ShapeDtypeStruct((N_CHIPS, R, C), BF16),
    )(idx, w)


def adamw(name, w, g, m, v):
    R, C = w.shape
    tr = _tile(R, max(8, (1 << 18) // C // 8 * 8), 8)
    c1 = 1.0 - ADAM_B1 ** ADAM_STEP
    c2 = 1.0 - ADAM_B2 ** ADAM_STEP

    def body(w_ref, g_ref, m_ref, v_ref, d_ref, mo_ref, vo_ref):
        gv = g_ref[...]
        mn = ADAM_B1 * m_ref[...] + (1.0 - ADAM_B1) * gv
        vn = ADAM_B2 * v_ref[...] + (1.0 - ADAM_B2) * (gv * gv)
        mo_ref[...] = mn
        vo_ref[...] = vn
        d_ref[...] = -ADAM_LR * ((mn / c1) / (jnp.sqrt(vn / c2) + ADAM_EPS) + ADAM_WD * w_ref[...])

    spec = pl.BlockSpec((tr, C), lambda r: (r, 0))
    return pl.pallas_call(
        body, name=name, grid=(R // tr,), in_specs=[spec] * 4, out_specs=[spec] * 3,
        out_shape=[jax.ShapeDtypeStruct((R, C), F32)] * 3,
        compiler_params=_params(7 * _nbytes((tr, C), F32), 4 * _nbytes((tr, C), F32)),
    )(w, g, m, v)


def half_sum(name, dw, landed, idx):
    _, _, hr, C = dw.shape
    tr = _tile(hr, max(16, (1 << 18) // C // 16 * 16), 16)

    def body(i_ref, a_ref, b_ref, o_ref):
        o_ref[...] = (a_ref[...].astype(F32) + b_ref[...].astype(F32)).astype(o_ref.dtype)

    return pl.pallas_call(
        body, name=name,
        grid_spec=pltpu.PrefetchScalarGridSpec(
            num_scalar_prefetch=1, grid=(N_CHIPS, hr // tr),
            in_specs=[pl.BlockSpec((None, None, tr, C), lambda k, r, i_ref: (k, i_ref[1], r, 0)),
                      pl.BlockSpec((None, tr, C), lambda k, r, i_ref: (k, r, 0))],
            out_specs=pl.BlockSpec((None, tr, C), lambda k, r, i_ref: (k, r, 0))),
        out_shape=jax.ShapeDtypeStruct((N_CHIPS, hr, C), BF16),
    )(idx, dw, landed)


def chip_sum(name, part, landed, gbuf, layer, idx):
    _, hr, C = part.shape
    tr = _tile(hr, max(16, (1 << 18) // C // 16 * 16), 16)

    def body(i_ref, a_ref, b_ref, g_ref, o_ref):
        o_ref[...] = ((a_ref[...].astype(F32) + b_ref[0].astype(F32)) + b_ref[1].astype(F32)) + b_ref[2].astype(F32)

    return pl.pallas_call(
        body, name=name,
        grid_spec=pltpu.PrefetchScalarGridSpec(
            num_scalar_prefetch=1, grid=(hr // tr,),
            in_specs=[pl.BlockSpec((None, tr, C), lambda r, i_ref: (i_ref[0], r, 0)),
                      pl.BlockSpec((3, tr, C), lambda r, i_ref: (0, r, 0)),
                      pl.BlockSpec(memory_space=pl.ANY)],
            out_specs=pl.BlockSpec((None, None, tr, C), lambda r, i_ref: (layer, i_ref[1], r, 0))),
        out_shape=jax.ShapeDtypeStruct(gbuf.shape, F32),
        input_output_aliases={3: 0},
    )(idx, part, landed, gbuf)


ANY = pl.BlockSpec(memory_space=pl.ANY)


def _place():
    x, y, c = lax.axis_index("x"), lax.axis_index("y"), lax.axis_index("c")
    chips = [(1 - x, y), (x, 1 - y), (1 - x, 1 - y)]
    return x, y, c, chips


def all_gather_weights(parts):
    n = len(parts)

    def body(*refs):
        bufs = refs[n:2 * n]
        send_sems, recv_sems = refs[2 * n:]
        x, y, c, chips = _place()
        kme = 2 * x + y
        sib = (x, y, 1 - c)

        def ici(i, j, k):
            return pltpu.make_async_remote_copy(
                src_ref=bufs[i].at[k, c], dst_ref=bufs[i].at[k, c],
                send_sem=send_sems.at[6 * i + j], recv_sem=recv_sems.at[6 * i + j],
                device_id=(*chips[j], c), device_id_type=MESH)

        def d2d(i, j, which):
            kj = 2 * chips[j][0] + chips[j][1]
            return pltpu.make_async_remote_copy(
                src_ref=bufs[i].at[kj, which], dst_ref=bufs[i].at[kj, which],
                send_sem=send_sems.at[6 * i + 3 + j], recv_sem=recv_sems.at[6 * i + 3 + j],
                device_id=sib, device_id_type=MESH)

        for i in range(n):
            for j in range(3):
                ici(i, j, kme).start()
        for i in range(n):
            for j in range(3):
                ici(i, j, 2 * chips[j][0] + chips[j][1]).wait_recv()
                d2d(i, j, c).start()
        for i in range(n):
            for j in range(3):
                d2d(i, j, 1 - c).wait_recv()
        for i in range(n):
            for j in range(3):
                ici(i, j, kme).wait_send()
                d2d(i, j, c).wait_send()

    return pl.pallas_call(
        body, name="all_gather_weights", in_specs=[ANY] * n, out_specs=[ANY] * n,
        out_shape=[jax.ShapeDtypeStruct(p.shape, p.dtype) for p in parts],
        input_output_aliases={i: i for i in range(n)},
        scratch_shapes=[pltpu.SemaphoreType.DMA((6 * n,)), pltpu.SemaphoreType.DMA((6 * n,))],
    )(*parts)


def pair_exchange(dws):
    n = len(dws)

    def body(*refs):
        ins, outs = refs[:n], refs[n:2 * n]
        send_sems, recv_sems = refs[2 * n:]
        x, y, c, _ = _place()
        copies = []
        for i in range(n):
            copies.append(pltpu.make_async_remote_copy(
                src_ref=ins[i].at[:, 1 - c], dst_ref=outs[i],
                send_sem=send_sems.at[i], recv_sem=recv_sems.at[i],
                device_id=(x, y, 1 - c), device_id_type=MESH))
            copies[i].start()
        for cp in copies:
            cp.wait_recv()
        for cp in copies:
            cp.wait_send()

    return pl.pallas_call(
        body, name="grad_pair_exchange", in_specs=[ANY] * n, out_specs=[ANY] * n,
        out_shape=[jax.ShapeDtypeStruct((N_CHIPS, *d.shape[2:]), d.dtype) for d in dws],
        scratch_shapes=[pltpu.SemaphoreType.DMA((n,)), pltpu.SemaphoreType.DMA((n,))],
    )(*dws)


def chip_exchange(parts):
    n = len(parts)

    def body(*refs):
        ins, outs = refs[:n], refs[n:2 * n]
        send_sems, recv_sems = refs[2 * n:]
        x, y, c, chips = _place()
        copies = []
        for i in range(n):
            for j in range(3):
                kd = 2 * chips[j][0] + chips[j][1]
                copies.append(pltpu.make_async_remote_copy(
                    src_ref=ins[i].at[kd], dst_ref=outs[i].at[j],
                    send_sem=send_sems.at[3 * i + j], recv_sem=recv_sems.at[3 * i + j],
                    device_id=(*chips[j], c), device_id_type=MESH))
                copies[-1].start()
        for cp in copies:
            cp.wait_recv()
        for cp in copies:
            cp.wait_send()

    return pl.pallas_call(
        body, name="grad_chip_exchange", in_specs=[ANY] * n, out_specs=[ANY] * n,
        out_shape=[jax.ShapeDtypeStruct((3, *p.shape[1:]), p.dtype) for p in parts],
        scratch_shapes=[pltpu.SemaphoreType.DMA((3 * n,)), pltpu.SemaphoreType.DMA((3 * n,))],
    )(*parts)


def pair_assemble(gbufs):
    n = len(gbufs)

    def body(*refs):
        bufs = refs[n:2 * n]
        send_sems, recv_sems = refs[2 * n:]
        x, y, c, _ = _place()
        copies = []
        for i in range(n):
            copies.append(pltpu.make_async_remote_copy(
                src_ref=bufs[i].at[:, c], dst_ref=bufs[i].at[:, c],
                send_sem=send_sems.at[i], recv_sem=recv_sems.at[i],
                device_id=(x, y, 1 - c), device_id_type=MESH))
            copies[i].start()
        for i in range(n):
            pltpu.make_async_remote_copy(
                src_ref=bufs[i].at[:, 1 - c], dst_ref=bufs[i].at[:, 1 - c],
                send_sem=send_sems.at[i], recv_sem=recv_sems.at[i],
                device_id=(x, y, 1 - c), device_id_type=MESH).wait_recv()
        for cp in copies:
            cp.wait_send()

    return pl.pallas_call(
        body, name="grad_pair_assemble", in_specs=[ANY] * n, out_specs=[ANY] * n,
        out_shape=[jax.ShapeDtypeStruct(g.shape, g.dtype) for g in gbufs],
        input_output_aliases={i: i for i in range(n)},
        scratch_shapes=[pltpu.SemaphoreType.DMA((n,)), pltpu.SemaphoreType.DMA((n,))],
    )(*gbufs)


def all_reduce_small(vec):
    NR = vec.shape[0]
    flips = [(fx, fy, fc) for fx in (0, 1) for fy in (0, 1) for fc in (0, 1)][1:]

    def body(v_ref, o_ref, buf, send_sems, recv_sems):
        x, y, c, _ = _place()
        me = 4 * x + 2 * y + c
        buf[me] = v_ref[...]
        copies = []
        for j, (fx, fy, fc) in enumerate(flips):
            peer = (1 - x if fx else x, 1 - y if fy else y, 1 - c if fc else c)
            copies.append(pltpu.make_async_remote_copy(
                src_ref=v_ref, dst_ref=buf.at[me], send_sem=send_sems.at[j], recv_sem=recv_sems.at[j],
                device_id=peer, device_id_type=MESH))
            copies[j].start()
        for cp in copies:
            cp.wait_recv()
        for cp in copies:
            cp.wait_send()
        acc = buf[0]
        for d in range(1, 8):
            acc = acc + buf[d]
        o_ref[...] = acc

    return pl.pallas_call(
        body, name="all_reduce_small",
        in_specs=[pl.BlockSpec(memory_space=pltpu.VMEM)], out_specs=pl.BlockSpec(memory_space=pltpu.VMEM),
        out_shape=jax.ShapeDtypeStruct((NR, LANE), F32),
        scratch_shapes=[pltpu.VMEM((8, NR, LANE), F32), pltpu.SemaphoreType.DMA((7,)),
                        pltpu.SemaphoreType.DMA((7,))],
    )(vec)


def _pack(arrays):
    flat = jnp.concatenate([a.reshape(-1).astype(F32) for a in arrays])
    n = flat.shape[0]
    npad = -(-n // (8 * LANE)) * (8 * LANE)
    return jnp.pad(flat, (0, npad - n)).reshape(npad // LANE, LANE)


def _unpack(buf, like):
    flat = buf.reshape(-1)
    out, off = [], 0
    for a in like:
        out.append(flat[off:off + a.size].reshape(a.shape))
        off += a.size
    return out


def kernel(x, ffn1_norm, ffn1_w_in, ffn1_w_out, mix_norm, ffn2_norm, ffn2_w_in, ffn2_w_out, a_w_qkv, a_rel_bias, a_w_o, kv_norm, kv_w_down, kv_latent_norm, kv_w_up, b_w_dq, b_q_norm, b_w_uq, b_w_o, final_norm, loss_target, m_ffn1_norm, m_ffn1_w_in, m_ffn1_w_out, m_mix_norm, m_ffn2_norm, m_ffn2_w_in, m_ffn2_w_out, m_a_w_qkv, m_a_rel_bias, m_a_w_o, m_kv_norm, m_kv_w_down, m_kv_latent_norm, m_kv_w_up, m_b_w_dq, m_b_q_norm, m_b_w_uq, m_b_w_o, m_final_norm, v_ffn1_norm, v_ffn1_w_in, v_ffn1_w_out, v_mix_norm, v_ffn2_norm, v_ffn2_w_in, v_ffn2_w_out, v_a_w_qkv, v_a_rel_bias, v_a_w_o, v_kv_norm, v_kv_w_down, v_kv_latent_norm, v_kv_w_up, v_b_w_dq, v_b_q_norm, v_b_w_uq, v_b_w_o, v_final_norm):
    B, S, D = x.shape
    T = B * S
    HB = D // 128
    QL = b_q_norm.shape[-1]
    KVL = kv_latent_norm.shape[0]
    hpc = HB // N_CHIPS
    tabs = rope_tables(S)
    idx = jnp.stack([2 * lax.axis_index("x") + lax.axis_index("y"), lax.axis_index("c")]).astype(I32)

    def halves(a):
        return a.reshape(*a.shape[:-2], 2, a.shape[-2] // 2, a.shape[-1])

    def whole(a):
        return a.reshape(*a.shape[:-3], 2 * a.shape[-2], a.shape[-1])

    kv_w_down_p = jnp.pad(kv_w_down, ((0, 0), (0, LANE - ROPE)))[None]
    b_w_uq_p = jnp.pad(b_w_uq.reshape(1, QL, hpc, NOPE + ROPE),
                       ((0, 0), (0, 0), (0, 0), (0, LANE - ROPE))).reshape(1, QL, hpc * 256)
    sharded = [("ffn1_w_in", ffn1_w_in), ("ffn1_w_out", ffn1_w_out), ("ffn2_w_in", ffn2_w_in),
               ("ffn2_w_out", ffn2_w_out), ("a_w_qkv", a_w_qkv), ("a_w_o", a_w_o),
               ("kv_w_down", kv_w_down_p), ("kv_w_up", kv_w_up[None]), ("b_w_dq", b_w_dq),
               ("b_w_uq", b_w_uq_p), ("b_w_o", b_w_o)]
    pieces = [(a, l) for a, (_, w) in enumerate(sharded) for l in range(w.shape[0])]
    own = [cast_bf16(f"cast_{sharded[a][0]}_{l}", sharded[a][1], l, idx) for a, l in pieces]
    gathered = [whole(g) for g in all_gather_weights([halves(p) for p in own])]
    W = {(sharded[a][0], l): g for (a, l), g in zip(pieces, gathered)}

    def col(nm, l=0):
        return W[(nm, l)]

    def row(nm, l=0):
        w = W[(nm, l)]
        return w.reshape(N_CHIPS * w.shape[1], w.shape[2])

    bias = rel_bias_tile("rel_bias_tile", a_rel_bias[0])

    def ffn_fwd(tag, h, g, w_in, w_out):
        xn = rms_fwd(f"{tag}_norm", h, g)
        u = mm_colw(f"{tag}_in", xn, w_in, F32)
        act = swiglu_fwd(f"{tag}_act", u)
        return mm_roww(f"{tag}_out", act, w_out, F32, res=h, alpha=0.5), (xn, u, act)

    h0 = x.reshape(T, D)
    h1, sv_f1a = ffn_fwd("l0f1", h0, ffn1_norm[0], col("ffn1_w_in", 0), row("ffn1_w_out", 0))
    hn_a = rms_fwd("l0mix_norm", h1, mix_norm[0])
    qkv = mm_colw("l0_qkv", hn_a, col("a_w_qkv"), BF16).reshape(B, S, 3 * D)
    o_a = attn_a_fwd("l0_attn", qkv, bias).reshape(T, D)
    h2 = mm_roww("l0_attn_out", o_a, row("a_w_o"), F32, res=h1)
    h3, sv_f2a = ffn_fwd("l0f2", h2, ffn2_norm[0], col("ffn2_w_in", 0), row("ffn2_w_out", 0))

    hkv = rms_fwd("kv_norm", h3, kv_norm)
    ckr = mm_roww("kv_down", hkv, row("kv_w_down"), F32)
    ckv, kr = kvprep_fwd("kv_prep", ckr, kv_latent_norm, tabs, B, S)
    kvb = mm_colw("kv_up", ckv, col("kv_w_up"), BF16).reshape(B, S, HB * 256)

    h4, sv_f1b = ffn_fwd("l1f1", h3, ffn1_norm[1], col("ffn1_w_in", 1), row("ffn1_w_out", 1))
    hn_b = rms_fwd("l1mix_norm", h4, mix_norm[1])
    cqp = mm_roww("l1_dq", hn_b, row("b_w_dq"), F32)
    cq = rms_fwd("l1_q_norm", cqp, b_q_norm[0])
    qpre = mm_colw("l1_uq", cq, col("b_w_uq"), F32)
    qf = qprep("l1_q_rope", qpre, tabs, B, S, bwd=False).reshape(B, S, HB * 256)
    o_b, lse = mla_fwd("l1_attn", qf, kvb, kr)
    h5 = mm_roww("l1_attn_out", o_b.reshape(T, HB * LANE), row("b_w_o"), F32, res=h4)
    h6, sv_f2b = ffn_fwd("l1f2", h5, ffn2_norm[1], col("ffn2_w_in", 1), row("ffn2_w_out", 1))

    dh, g_final, loss_part = loss_head("loss_head", h6, final_norm, loss_target.reshape(T, D))

    gw = {}

    def ffn_bwd(tag, dh, h_in, g, w_in, w_out, saved, key_in, key_out):
        xn, u, act = saved
        dact = mm_roww_t(f"{tag}_dact", dh, w_out, F32, alpha=0.5)
        dwo = mm_droww(f"{tag}_dwout", act, dh, alpha=0.5)
        gw[key_out] = dwo.reshape(N_CHIPS, dwo.shape[0] // N_CHIPS, dwo.shape[1])
        du = swiglu_bwd(f"{tag}_dswiglu", u, dact)
        gw[key_in] = mm_dcolw(f"{tag}_dwin", xn, du)
        dxn = mm_colw_t(f"{tag}_dxn", du, w_in, F32)
        return rms_bwd(f"{tag}_dnorm", h_in, g, dxn, dres=dh)

    def chip_major(dw):
        return dw.reshape(N_CHIPS, dw.shape[0] // N_CHIPS, dw.shape[1])

    dh, g_f2b = ffn_bwd("l1f2b", dh, h5, ffn2_norm[1], col("ffn2_w_in", 1), row("ffn2_w_out", 1), sv_f2b,
                        ("ffn2_w_in", 1), ("ffn2_w_out", 1))
    do_b = mm_roww_t("l1_attn_do", dh, row("b_w_o"), BF16).reshape(B, S, HB * LANE)
    gw[("b_w_o", 0)] = chip_major(mm_droww("l1_attn_dwo", o_b.reshape(T, HB * LANE), dh))
    dqf, dkv, dkr = mla_bwd("l1_attn_bwd", qf, kvb, kr, do_b, o_b, lse)
    dqpre = qprep("l1_q_rope_bwd", dqf.reshape(T, HB * 256), tabs, B, S, bwd=True)
    gw[("b_w_uq", 0)] = mm_dcolw("l1_dwuq", cq, dqpre)
    dcq = mm_colw_t("l1_dcq", dqpre, col("b_w_uq"), F32)
    dcqp, g_qn = rms_bwd("l1_dq_norm", cqp, b_q_norm[0], dcq)
    gw[("b_w_dq", 0)] = chip_major(mm_droww("l1_dwdq", hn_b, dcqp))
    dhn = mm_roww_t("l1_dhn", dcqp, row("b_w_dq"), F32)
    dh, g_mixb = rms_bwd("l1_dmix", h4, mix_norm[1], dhn, dres=dh)
    dh, g_f1b = ffn_bwd("l1f1b", dh, h3, ffn1_norm[1], col("ffn1_w_in", 1), row("ffn1_w_out", 1), sv_f1b,
                        ("ffn1_w_in", 1), ("ffn1_w_out", 1))
    dkv2 = dkv.reshape(T, HB * 256)
    gw[("kv_w_up", 0)] = mm_dcolw("kv_dwup", ckv, dkv2)
    dckv = mm_colw_t("kv_dckv", dkv2, col("kv_w_up"), F32)
    dckr, g_lat = kvprep_bwd("kv_prep_bwd", ckr, kv_latent_norm, dckv, dkr, tabs, B, S)
    gw[("kv_w_down", 0)] = chip_major(mm_droww("kv_dwdown", hkv, dckr))
    dhkv = mm_roww_t("kv_dhkv", dckr, row("kv_w_down"), F32)
    dh, g_kvn = rms_bwd("kv_dnorm", h3, kv_norm, dhkv, dres=dh)
    dh, g_f2a = ffn_bwd("l0f2b", dh, h2, ffn2_norm[0], col("ffn2_w_in", 0), row("ffn2_w_out", 0), sv_f2a,
                        ("ffn2_w_in", 0), ("ffn2_w_out", 0))
    do_a = mm_roww_t("l0_attn_do", dh, row("a_w_o"), BF16).reshape(B, S, D)
    gw[("a_w_o", 0)] = chip_major(mm_droww("l0_attn_dwo", o_a, dh))
    dq_a, dkv_a, dbias = attn_a_bwd("l0_attn_bwd", qkv, do_a, bias)
    dqkv = jnp.concatenate([dq_a.reshape(T, D), dkv_a.reshape(T, 2 * D)], axis=1)
    gw[("a_w_qkv", 0)] = mm_dcolw("l0_dwqkv", hn_a, dqkv)
    dhn = mm_colw_t("l0_dhn", dqkv, col("a_w_qkv"), F32)
    dh, g_mixa = rms_bwd("l0_dmix", h1, mix_norm[0], dhn, dres=dh)
    dh, g_f1a = ffn_bwd("l0f1b", dh, h0, ffn1_norm[0], col("ffn1_w_in", 0), row("ffn1_w_out", 0), sv_f1a,
                        ("ffn1_w_in", 0), ("ffn1_w_out", 0))
    grad_x = dh.reshape(B, S, D)
    g_rel = rel_bias_grad("rel_bias_grad", dbias)[:, :2 * MAX_REL + 1][None]

    dws = [halves(gw[(sharded[a][0], l)]) for a, l in pieces]
    landed1 = pair_exchange(dws)
    parts = [half_sum(f"half_sum_{i}", dws[i], landed1[i], idx) for i in range(len(dws))]
    landed2 = chip_exchange(parts)
    gbufs = [lax.empty(halves(w).shape, F32) for _, w in sharded]
    for i, (a, l) in enumerate(pieces):
        gbufs[a] = chip_sum(f"chip_sum_{i}", parts[i], landed2[i], gbufs[a], l, idx)
    full = [whole(g) for g in pair_assemble(gbufs)]
    G = {nm: g for (nm, _), g in zip(sharded, full)}
    G["kv_w_down"] = G["kv_w_down"][0, :, :KVL + ROPE]
    G["kv_w_up"] = G["kv_w_up"][0]
    G["b_w_uq"] = G["b_w_uq"].reshape(1, QL, hpc, 256)[..., :NOPE + ROPE].reshape(b_w_uq.shape)

    small = [("ffn1_norm", jnp.stack([g_f1a, g_f1b])), ("mix_norm", jnp.stack([g_mixa, g_mixb])),
             ("ffn2_norm", jnp.stack([g_f2a, g_f2b])), ("a_rel_bias", g_rel), ("kv_norm", g_kvn),
             ("kv_latent_norm", g_lat), ("b_q_norm", g_qn[None]), ("final_norm", g_final)]
    red = all_reduce_small(_pack([loss_part] + [g for _, g in small]))
    unpacked = _unpack(red, [loss_part] + [g for _, g in small])
    loss = unpacked[0][0, 0]
    for (nm, _), g in zip(small, unpacked[1:]):
        G[nm] = g

    given = dict(ffn1_norm=(ffn1_norm, m_ffn1_norm, v_ffn1_norm), ffn1_w_in=(ffn1_w_in, m_ffn1_w_in, v_ffn1_w_in),
                 ffn1_w_out=(ffn1_w_out, m_ffn1_w_out, v_ffn1_w_out), mix_norm=(mix_norm, m_mix_norm, v_mix_norm),
                 ffn2_norm=(ffn2_norm, m_ffn2_norm, v_ffn2_norm), ffn2_w_in=(ffn2_w_in, m_ffn2_w_in, v_ffn2_w_in),
                 ffn2_w_out=(ffn2_w_out, m_ffn2_w_out, v_ffn2_w_out), a_w_qkv=(a_w_qkv, m_a_w_qkv, v_a_w_qkv),
                 a_rel_bias=(a_rel_bias, m_a_rel_bias, v_a_rel_bias), a_w_o=(a_w_o, m_a_w_o, v_a_w_o),
                 kv_norm=(kv_norm, m_kv_norm, v_kv_norm), kv_w_down=(kv_w_down, m_kv_w_down, v_kv_w_down),
                 kv_latent_norm=(kv_latent_norm, m_kv_latent_norm, v_kv_latent_norm),
                 kv_w_up=(kv_w_up, m_kv_w_up, v_kv_w_up), b_w_dq=(b_w_dq, m_b_w_dq, v_b_w_dq),
                 b_q_norm=(b_q_norm, m_b_q_norm, v_b_q_norm), b_w_uq=(b_w_uq, m_b_w_uq, v_b_w_uq),
                 b_w_o=(b_w_o, m_b_w_o, v_b_w_o), final_norm=(final_norm, m_final_norm, v_final_norm))
    order = list(given)
    delta, new_m, new_v = {}, {}, {}
    small_names = [nm for nm, _ in small]
    packed = [_pack([given[nm][k] for nm in small_names]) for k in range(3)]
    outs = adamw("adamw_small", packed[0], _pack([G[nm] for nm in small_names]), packed[1], packed[2])
    for dst, buf in zip((delta, new_m, new_v), outs):
        for nm, a in zip(small_names, _unpack(buf, [given[nm][0] for nm in small_names])):
            dst[nm] = a
    for nm, _ in sharded:
        w, m, v = given[nm]
        g = G[nm].reshape(w.shape)
        G[nm] = g
        two = lambda a: a.reshape(-1, a.shape[-1])
        d_, m_, v_ = adamw(f"adamw_{nm}", two(w), two(g), two(m), two(v))
        delta[nm], new_m[nm], new_v[nm] = d_.reshape(w.shape), m_.reshape(w.shape), v_.reshape(w.shape)

    return (loss, grad_x, *[G[n] for n in order], *[delta[n] for n in order],
            *[new_m[n] for n in order], *[new_v[n] for n in order])
```

```python
import functools
import math

import jax
import jax.numpy as jnp
from jax import lax
from jax.experimental import pallas as pl
from jax.experimental.pallas import tpu as pltpu

F32 = jnp.float32
BF16 = jnp.bfloat16
I32 = jnp.int32

CHUNK = 64
CHUNK_SHIFT = 6
HEAD_DIM_A = 64
LEFT_CHUNKS = 8
MAX_REL = 128
REL_PAD = 384
QROWS = 2 * CHUNK
WIN = (LEFT_CHUNKS + 2) * CHUNK
PADR = LEFT_CHUNKS * CHUNK
NOPE = 128
ROPE = 64
EPS = 1e-6
NEG_INF = -1e30
ROPE_THETA = 10000.0
ADAM_LR, ADAM_B1, ADAM_B2, ADAM_EPS, ADAM_WD, ADAM_STEP = 0.001, 0.9, 0.999, 1e-08, 0.01, 10
N_CHIPS = 4
LANE = 128
MESH = pl.DeviceIdType.MESH
VMEM_CAP_MB = 60

NN = (((1,), (0,)), ((), ()))
NT = (((1,), (1,)), ((), ()))
TN = (((0,), (0,)), ((), ()))


def _tile(n, pref, mult):
    t = (min(pref, n) // mult) * mult
    while t >= mult:
        if n % t == 0:
            return t
        t -= mult
    return n


def _nbytes(shape, dtype):
    return math.prod(shape) * jnp.dtype(dtype).itemsize


def _params(block_bytes, extra_bytes=0):
    need = 2 * block_bytes + extra_bytes
    mb = min(VMEM_CAP_MB, max(32, int(need * 1.25 / 2**20) + 8))
    return pltpu.CompilerParams(vmem_limit_bytes=mb * 2**20)


def _mm(name, kind, a, b, grid, a_spec, b_spec, o_spec, out_shape, out_dtype, blocks,
        red_axis=None, nred=1, alpha=1.0, res=None, res_spec=None):
    dims = {"nn": NN, "nt": NT, "tn": TN}[kind]
    has_res = res is not None
    acc_in_out = nred > 1 and out_dtype == F32 and not has_res and alpha == 1.0

    def body(*refs):
        a_ref, b_ref = refs[0], refs[1]
        r_ref = refs[2] if has_res else None
        o_ref = refs[3] if has_res else refs[2]
        p = lax.dot_general(a_ref[...].astype(BF16), b_ref[...].astype(BF16), dims,
                            preferred_element_type=F32)

        def finish(acc):
            y = acc if alpha == 1.0 else acc * alpha
            if has_res:
                y = r_ref[...] + y
            o_ref[...] = y.astype(o_ref.dtype)

        if nred == 1:
            finish(p)
            return
        k = pl.program_id(red_axis)
        tgt = o_ref if acc_in_out else refs[-1]

        @pl.when(k == 0)
        def _():
            tgt[...] = p

        @pl.when(k > 0)
        def _():
            tgt[...] += p

        if not acc_in_out:
            @pl.when(k == nred - 1)
            def _():
                finish(tgt[...])

    a_blk, b_blk, o_blk = blocks
    scratch = []
    extra = 0
    if nred > 1 and not acc_in_out:
        scratch = [pltpu.VMEM(o_blk, F32)]
        extra = _nbytes(o_blk, F32)
    blk = _nbytes(a_blk, a.dtype) + _nbytes(b_blk, b.dtype) + _nbytes(o_blk, out_dtype)
    ins, specs = [a, b], [a_spec, b_spec]
    if has_res:
        ins.append(res)
        specs.append(res_spec)
        blk += _nbytes(o_blk, res.dtype)
    extra += _nbytes(a_blk, BF16) + _nbytes(b_blk, BF16) + 2 * _nbytes(o_blk, F32)
    return pl.pallas_call(
        body, name=name, grid=grid, in_specs=specs, out_specs=o_spec,
        out_shape=jax.ShapeDtypeStruct(out_shape, out_dtype), scratch_shapes=scratch,
        compiler_params=_params(blk, extra),
    )(*ins)


def mm_colw(name, x, w3, out_dtype):
    T, K = x.shape
    _, _, nl = w3.shape
    tm = _tile(T, 512, 8)
    return _mm(name, "nn", x, w3, (N_CHIPS, T // tm),
               pl.BlockSpec((tm, K), lambda j, i: (i, 0)),
               pl.BlockSpec((None, K, nl), lambda j, i: (j, 0, 0)),
               pl.BlockSpec((tm, nl), lambda j, i: (i, j)),
               (T, N_CHIPS * nl), out_dtype, ((tm, K), (K, nl), (tm, nl)))


def mm_colw_t(name, dy, w3, out_dtype, res=None):
    T = dy.shape[0]
    _, K, nl = w3.shape
    tm = _tile(T, 512, 8)
    return _mm(name, "nt", dy, w3, (T // tm, N_CHIPS),
               pl.BlockSpec((tm, nl), lambda i, j: (i, j)),
               pl.BlockSpec((None, K, nl), lambda i, j: (j, 0, 0)),
               pl.BlockSpec((tm, K), lambda i, j: (i, 0)),
               (T, K), out_dtype, ((tm, nl), (K, nl), (tm, K)),
               red_axis=1, nred=N_CHIPS, res=res,
               res_spec=pl.BlockSpec((tm, K), lambda i, j: (i, 0)))


def mm_dcolw(name, x, dy):
    T, K = x.shape
    nl = dy.shape[1] // N_CHIPS
    tt = _tile(T, 512, 8)
    return _mm(name, "tn", x, dy, (N_CHIPS, T // tt),
               pl.BlockSpec((tt, K), lambda j, t: (t, 0)),
               pl.BlockSpec((tt, nl), lambda j, t: (t, j)),
               pl.BlockSpec((None, K, nl), lambda j, t: (j, 0, 0)),
               (N_CHIPS, K, nl), BF16, ((tt, K), (tt, nl), (K, nl)),
               red_axis=1, nred=T // tt)


def mm_roww(name, x, w2, out_dtype, res=None, alpha=1.0):
    T, Kt = x.shape
    N = w2.shape[1]
    tm = _tile(T, 512, 8)
    return _mm(name, "nn", x, w2, (T // tm,),
               pl.BlockSpec((tm, Kt), lambda i: (i, 0)),
               pl.BlockSpec((Kt, N), lambda i: (0, 0)),
               pl.BlockSpec((tm, N), lambda i: (i, 0)),
               (T, N), out_dtype, ((tm, Kt), (Kt, N), (tm, N)),
               alpha=alpha, res=res, res_spec=pl.BlockSpec((tm, N), lambda i: (i, 0)))


def mm_roww_t(name, dy, w2, out_dtype, alpha=1.0):
    T, N = dy.shape
    Kt = w2.shape[0]
    tm = _tile(T, 512, 8)
    tk = _tile(Kt, 1408, LANE)
    return _mm(name, "nt", dy, w2, (Kt // tk, T // tm),
               pl.BlockSpec((tm, N), lambda j, i: (i, 0)),
               pl.BlockSpec((tk, N), lambda j, i: (j, 0)),
               pl.BlockSpec((tm, tk), lambda j, i: (i, j)),
               (T, Kt), out_dtype, ((tm, N), (tk, N), (tm, tk)), alpha=alpha)


def mm_droww(name, x, dy, alpha=1.0):
    T, Kt = x.shape
    N = dy.shape[1]
    tt = _tile(T, 512, 8)
    tk = _tile(Kt, 1408, LANE)
    return _mm(name, "tn", x, dy, (Kt // tk, T // tt),
               pl.BlockSpec((tt, tk), lambda j, t: (t, j)),
               pl.BlockSpec((tt, N), lambda j, t: (t, 0)),
               pl.BlockSpec((tk, N), lambda j, t: (j, 0)),
               (Kt, N), BF16, ((tt, tk), (tt, N), (tk, N)),
               red_axis=1, nred=T // tt, alpha=alpha)


def rms_fwd(name, x, g):
    T, D = x.shape
    tm = _tile(T, 512, 8)

    def body(x_ref, g_ref, o_ref):
        xv = x_ref[...]
        r = lax.rsqrt(jnp.mean(xv * xv, axis=-1, keepdims=True) + EPS)
        o_ref[...] = (xv * r * g_ref[...]).astype(o_ref.dtype)

    return pl.pallas_call(
        body, name=name, grid=(T // tm,),
        in_specs=[pl.BlockSpec((tm, D), lambda i: (i, 0)), pl.BlockSpec((1, D), lambda i: (0, 0))],
        out_specs=pl.BlockSpec((tm, D), lambda i: (i, 0)),
        out_shape=jax.ShapeDtypeStruct((T, D), BF16),
        compiler_params=_params(_nbytes((tm, D), F32) * 2, 4 * _nbytes((tm, D), F32)),
    )(x, g.reshape(1, D))


def _rms_bwd_math(xv, gv, dy):
    r = lax.rsqrt(jnp.mean(xv * xv, axis=-1, keepdims=True) + EPS)
    xh = xv * r
    dyg = dy * gv
    dx = r * (dyg - xh * jnp.mean(dyg * xh, axis=-1, keepdims=True))
    dg = jnp.sum(dy * xh, axis=0, keepdims=True)
    return dx, dg


def rms_bwd(name, x, g, dy, dres=None):
    T, D = x.shape
    tm = _tile(T, 256, 8)
    has_res = dres is not None

    def body(*refs):
        x_ref, g_ref, dy_ref = refs[:3]
        r_ref = refs[3] if has_res else None
        dx_ref, dg_ref = refs[-2:]
        dx, dg = _rms_bwd_math(x_ref[...], g_ref[...], dy_ref[...].astype(F32))
        if has_res:
            dx = r_ref[...] + dx
        dx_ref[...] = dx

        @pl.when(pl.program_id(0) == 0)
        def _():
            dg_ref[...] = dg

        @pl.when(pl.program_id(0) > 0)
        def _():
            dg_ref[...] += dg

    row = pl.BlockSpec((tm, D), lambda i: (i, 0))
    vec = pl.BlockSpec((1, D), lambda i: (0, 0))
    ins, specs = [x, g.reshape(1, D), dy], [row, vec, row]
    if has_res:
        ins.append(dres)
        specs.append(row)
    dx, dg = pl.pallas_call(
        body, name=name, grid=(T // tm,), in_specs=specs, out_specs=[row, vec],
        out_shape=[jax.ShapeDtypeStruct((T, D), F32), jax.ShapeDtypeStruct((1, D), F32)],
        compiler_params=_params(_nbytes((tm, D), F32) * 4, 6 * _nbytes((tm, D), F32)),
    )(*ins)
    return dx, dg.reshape(D)


def swiglu_fwd(name, u):
    T, F2 = u.shape
    F = F2 // 2
    tm = _tile(T, 256, 8)

    def body(u_ref, o_ref):
        u1 = u_ref[:, :F]
        u2 = u_ref[:, F:]
        o_ref[...] = (u1 * jax.nn.sigmoid(u1) * u2).astype(o_ref.dtype)

    return pl.pallas_call(
        body, name=name, grid=(T // tm,),
        in_specs=[pl.BlockSpec((tm, F2), lambda i: (i, 0))],
        out_specs=pl.BlockSpec((tm, F), lambda i: (i, 0)),
        out_shape=jax.ShapeDtypeStruct((T, F), BF16),
        compiler_params=_params(_nbytes((tm, F2), F32) * 2, _nbytes((tm, F2), F32) * 2),
    )(u)


def swiglu_bwd(name, u, dact):
    T, F2 = u.shape
    F = F2 // 2
    tm = _tile(T, 256, 8)

    def body(u_ref, d_ref, o_ref):
        u1 = u_ref[:, :F]
        u2 = u_ref[:, F:]
        d = d_ref[...]
        sig = jax.nn.sigmoid(u1)
        silu = u1 * sig
        o_ref[:, :F] = (d * u2 * (sig * (1.0 + u1 * (1.0 - sig)))).astype(o_ref.dtype)
        o_ref[:, F:] = (d * silu).astype(o_ref.dtype)

    return pl.pallas_call(
        body, name=name, grid=(T // tm,),
        in_specs=[pl.BlockSpec((tm, F2), lambda i: (i, 0)), pl.BlockSpec((tm, F), lambda i: (i, 0))],
        out_specs=pl.BlockSpec((tm, F2), lambda i: (i, 0)),
        out_shape=jax.ShapeDtypeStruct((T, F2), BF16),
        compiler_params=_params(_nbytes((tm, F2), F32) * 2, _nbytes((tm, F2), F32) * 3),
    )(u, dact)


def loss_head(name, h, g, target):
    T, D = h.shape
    tm = _tile(T, 256, 8)

    def body(h_ref, g_ref, t_ref, dh_ref, dg_ref, loss_ref):
        xv = h_ref[...]
        gv = g_ref[...]
        r = lax.rsqrt(jnp.mean(xv * xv, axis=-1, keepdims=True) + EPS)
        err = xv * r * gv - t_ref[...]
        part = 0.5 * jnp.sum(jnp.mean(err * err, axis=-1, keepdims=True), axis=0, keepdims=True)
        dx, dg = _rms_bwd_math(xv, gv, err * (1.0 / D))
        dh_ref[...] = dx
        part = jnp.broadcast_to(part, (1, LANE))

        @pl.when(pl.program_id(0) == 0)
        def _():
            dg_ref[...] = dg
            loss_ref[...] = part

        @pl.when(pl.program_id(0) > 0)
        def _():
            dg_ref[...] += dg
            loss_ref[...] += part

    row = pl.BlockSpec((tm, D), lambda i: (i, 0))
    vec = pl.BlockSpec((1, D), lambda i: (0, 0))
    dh, dg, loss = pl.pallas_call(
        body, name=name, grid=(T // tm,), in_specs=[row, vec, row],
        out_specs=[row, vec, pl.BlockSpec((1, LANE), lambda i: (0, 0))],
        out_shape=[jax.ShapeDtypeStruct((T, D), F32), jax.ShapeDtypeStruct((1, D), F32),
                   jax.ShapeDtypeStruct((1, LANE), F32)],
        compiler_params=_params(_nbytes((tm, D), F32) * 3, 6 * _nbytes((tm, D), F32)),
    )(h, g.reshape(1, D), target)
    return dh, dg.reshape(D), loss


def rope_tables(S):
    half = ROPE // 2
    freqs = ROPE_THETA ** (-jnp.arange(half, dtype=F32) / half)
    ang = jnp.arange(S, dtype=F32)[:, None] * freqs[None, :]
    cos, sin = jnp.cos(ang), jnp.sin(ang)
    z = jnp.zeros_like(cos)
    ct = jnp.concatenate([cos, cos, z, z], axis=1)
    s1 = jnp.concatenate([-sin, z, z, z], axis=1)
    s2 = jnp.concatenate([z, sin, z, z], axis=1)
    return ct, s1, s2


def _rope_tile(t, ct, s1, s2):
    return t * ct + pltpu.roll(t, 96, 1) * s1 + pltpu.roll(t, 32, 1) * s2


def _rope_tile_bwd(d, ct, s1, s2):
    return d * ct + pltpu.roll(d * s1, 32, 1) + pltpu.roll(d * s2, 96, 1)


def qprep(name, q, tabs, B, S, bwd):
    T, W = q.shape
    nh = W // 256
    ts = _tile(S, 256, 8)
    fn = _rope_tile_bwd if bwd else _rope_tile

    def body(q_ref, ct_ref, s1_ref, s2_ref, o_ref):
        ct, s1, s2 = ct_ref[...], s1_ref[...], s2_ref[...]
        for h in range(nh):
            o_ref[0, :, 256 * h:256 * h + 128] = q_ref[0, :, 256 * h:256 * h + 128].astype(o_ref.dtype)
            t = q_ref[0, :, 256 * h + 128:256 * h + 256].astype(F32)
            o_ref[0, :, 256 * h + 128:256 * h + 256] = fn(t, ct, s1, s2).astype(o_ref.dtype)

    row = pl.BlockSpec((1, ts, W), lambda b, s: (b, s, 0))
    tab = pl.BlockSpec((ts, LANE), lambda b, s: (s, 0))
    out = pl.pallas_call(
        body, name=name, grid=(B, S // ts), in_specs=[row, tab, tab, tab], out_specs=row,
        out_shape=jax.ShapeDtypeStruct((B, S, W), BF16),
        compiler_params=_params(_nbytes((ts, W), F32) * 2, _nbytes((ts, W), F32) * 2),
    )(q.reshape(B, S, W), *tabs)
    return out.reshape(T, W)


def kvprep_fwd(name, ckr, g, tabs, B, S):
    T, W = ckr.shape
    KVL = W - LANE
    ts = _tile(S, 256, 8)

    def body(x_ref, g_ref, ct_ref, s1_ref, s2_ref, c_ref, k_ref):
        xv = x_ref[0, :, :KVL]
        r = lax.rsqrt(jnp.mean(xv * xv, axis=-1, keepdims=True) + EPS)
        c_ref[0] = (xv * r * g_ref[...]).astype(c_ref.dtype)
        k_ref[0] = _rope_tile(x_ref[0, :, KVL:], ct_ref[...], s1_ref[...], s2_ref[...]).astype(k_ref.dtype)

    tab = pl.BlockSpec((ts, LANE), lambda b, s: (s, 0))
    c, k = pl.pallas_call(
        body, name=name, grid=(B, S // ts),
        in_specs=[pl.BlockSpec((1, ts, W), lambda b, s: (b, s, 0)), pl.BlockSpec((1, KVL), lambda b, s: (0, 0)),
                  tab, tab, tab],
        out_specs=[pl.BlockSpec((1, ts, KVL), lambda b, s: (b, s, 0)),
                   pl.BlockSpec((1, ts, LANE), lambda b, s: (b, s, 0))],
        out_shape=[jax.ShapeDtypeStruct((B, S, KVL), BF16), jax.ShapeDtypeStruct((B, S, LANE), BF16)],
        compiler_params=_params(_nbytes((ts, W), F32) * 2, _nbytes((ts, W), F32) * 2),
    )(ckr.reshape(B, S, W), g.reshape(1, KVL), *tabs)
    return c.reshape(T, KVL), k


def kvprep_bwd(name, ckr, g, dc, dkr, tabs, B, S):
    T, W = ckr.shape
    KVL = W - LANE
    ts = _tile(S, 256, 8)

    def body(x_ref, g_ref, dc_ref, dk_ref, ct_ref, s1_ref, s2_ref, o_ref, dg_ref):
        dx, dg = _rms_bwd_math(x_ref[0, :, :KVL], g_ref[...], dc_ref[0])
        o_ref[0, :, :KVL] = dx
        o_ref[0, :, KVL:] = _rope_tile_bwd(dk_ref[0], ct_ref[...], s1_ref[...], s2_ref[...])
        first = (pl.program_id(0) == 0) & (pl.program_id(1) == 0)

        @pl.when(first)
        def _():
            dg_ref[...] = dg

        @pl.when(jnp.logical_not(first))
        def _():
            dg_ref[...] += dg

    tab = pl.BlockSpec((ts, LANE), lambda b, s: (s, 0))
    vec = pl.BlockSpec((1, KVL), lambda b, s: (0, 0))
    o, dg = pl.pallas_call(
        body, name=name, grid=(B, S // ts),
        in_specs=[pl.BlockSpec((1, ts, W), lambda b, s: (b, s, 0)), vec,
                  pl.BlockSpec((1, ts, KVL), lambda b, s: (b, s, 0)),
                  pl.BlockSpec((1, ts, LANE), lambda b, s: (b, s, 0)), tab, tab, tab],
        out_specs=[pl.BlockSpec((1, ts, W), lambda b, s: (b, s, 0)), vec],
        out_shape=[jax.ShapeDtypeStruct((B, S, W), F32), jax.ShapeDtypeStruct((1, KVL), F32)],
        compiler_params=_params(_nbytes((ts, W), F32) * 4, _nbytes((ts, W), F32) * 4),
    )(ckr.reshape(B, S, W), g.reshape(1, KVL), dc.reshape(B, S, KVL), dkr, *tabs)
    return o.reshape(T, W), dg.reshape(KVL)


def _rel_onehot(r):
    col = lax.broadcasted_iota(I32, (REL_PAD, WIN), 1)
    row = lax.broadcasted_iota(I32, (REL_PAD, WIN), 0)
    idx = jnp.clip(PADR + r - col, -MAX_REL, MAX_REL) + MAX_REL
    return (row == idx).astype(F32)


def rel_bias_tile(name, table):
    H = table.shape[0]
    tpad = jnp.pad(table, ((0, 0), (0, REL_PAD - table.shape[1]))).reshape(H // 2, 2, REL_PAD).transpose(1, 0, 2)
    nblk = QROWS // 8

    def body(t_ref, o_ref):
        base = pl.program_id(1) * 8
        for ii in range(8):
            o_ref[:, ii, :] = lax.dot_general(t_ref[...], _rel_onehot(base + ii), NN,
                                              precision=lax.Precision.HIGHEST, preferred_element_type=F32)

    return pl.pallas_call(
        body, name=name, grid=(2, nblk),
        in_specs=[pl.BlockSpec((None, H // 2, REL_PAD), lambda hh, s: (hh, 0, 0))],
        out_specs=pl.BlockSpec((H // 2, 8, WIN), lambda hh, s: (0, hh * nblk + s, 0)),
        out_shape=jax.ShapeDtypeStruct((H // 2, 2 * QROWS, WIN), F32),
    )(tpad)


def rel_bias_grad(name, dbias):
    P = dbias.shape[0]
    nblk = QROWS // 8

    def body(d_ref, o_ref):
        base = pl.program_id(1) * 8
        acc = jnp.zeros((P, REL_PAD), F32)
        for ii in range(8):
            acc = acc + lax.dot_general(d_ref[:, ii, :], _rel_onehot(base + ii), NT,
                                        precision=lax.Precision.HIGHEST, preferred_element_type=F32)

        @pl.when(pl.program_id(1) == 0)
        def _():
            o_ref[...] = acc

        @pl.when(pl.program_id(1) > 0)
        def _():
            o_ref[...] += acc

    out = pl.pallas_call(
        body, name=name, grid=(2, nblk),
        in_specs=[pl.BlockSpec((P, 8, WIN), lambda hh, s: (0, hh * nblk + s, 0))],
        out_specs=pl.BlockSpec((None, P, REL_PAD), lambda hh, s: (hh, 0, 0)),
        out_shape=jax.ShapeDtypeStruct((2, P, REL_PAD), F32),
    )(dbias)
    return out.transpose(1, 0, 2).reshape(2 * P, REL_PAD)


def _stack_pair(xp):
    lane = lax.broadcasted_iota(I32, xp.shape, 1)
    z = jnp.zeros_like(xp)
    return jnp.concatenate([jnp.where(lane < HEAD_DIM_A, xp, z), jnp.where(lane >= HEAD_DIM_A, xp, z)], axis=0)


def _unstack_pair(y):
    lane = lax.broadcasted_iota(I32, (QROWS, LANE), 1)
    return jnp.where(lane < HEAD_DIM_A, y[:QROWS], y[QROWS:])


def _attn_a_mask(j):
    r = lax.broadcasted_iota(I32, (2 * QROWS, WIN), 0)
    w = lax.broadcasted_iota(I32, (2 * QROWS, WIN), 1)
    qc = jnp.right_shift(jnp.bitwise_and(r, QROWS - 1), CHUNK_SHIFT)
    kc = jnp.right_shift(w, CHUNK_SHIFT)
    return (kc >= qc) & (kc <= qc + LEFT_CHUNKS) & (w >= PADR - QROWS * j)


def _attn_a_load_bias(bias_hbm, bias_v, sem):
    cp = pltpu.make_async_copy(bias_hbm, bias_v, sem)
    cp.start()
    cp.wait()


def _attn_a_load_kv(qkv_hbm, b, kpad, vpad, sem, S, D):
    kpad[0:PADR, :] = jnp.zeros((PADR, D), BF16)
    vpad[0:PADR, :] = jnp.zeros((PADR, D), BF16)
    ck = pltpu.make_async_copy(qkv_hbm.at[b, :, pl.ds(D, D)], kpad.at[pl.ds(PADR, S), :], sem.at[0])
    cv = pltpu.make_async_copy(qkv_hbm.at[b, :, pl.ds(2 * D, D)], vpad.at[pl.ds(PADR, S), :], sem.at[1])
    ck.start()
    cv.start()
    ck.wait()
    cv.wait()


def _attn_a_probs(qm, kp, bias, valid, scale):
    s = lax.dot_general(qm, kp, NT, preferred_element_type=F32) * scale + bias
    s = jnp.where(valid, s, NEG_INF)
    e = jnp.exp(s - jnp.max(s, axis=-1, keepdims=True))
    return e * (1.0 / jnp.sum(e, axis=-1, keepdims=True))


def attn_a_fwd(name, qkv, bias):
    B, S, D3 = qkv.shape
    D = D3 // 3
    H = D // HEAD_DIM_A
    nb = S // QROWS
    scale = HEAD_DIM_A ** -0.5

    def body(q_ref, bias_hbm, qkv_hbm, o_ref, kpad, vpad, bias_v, sem):
        b, j = pl.program_id(0), pl.program_id(1)

        @pl.when((b == 0) & (j == 0))
        def _():
            _attn_a_load_bias(bias_hbm, bias_v, sem.at[2])

        @pl.when(j == 0)
        def _():
            _attn_a_load_kv(qkv_hbm, b, kpad, vpad, sem, S, D)

        mask = _attn_a_mask(j)
        w0 = pl.multiple_of(j * QROWS, QROWS)
        for p in range(H // 2):
            ls = slice(p * LANE, (p + 1) * LANE)
            pr = _attn_a_probs(_stack_pair(q_ref[0, :, ls]), kpad[pl.ds(w0, WIN), ls], bias_v[p], mask, scale)
            o2 = jnp.dot(pr.astype(BF16), vpad[pl.ds(w0, WIN), ls], preferred_element_type=F32)
            o_ref[0, :, ls] = _unstack_pair(o2).astype(o_ref.dtype)

    scr = 2 * _nbytes((PADR + S, D), BF16) + _nbytes(bias.shape, F32) + 8 * _nbytes((2 * QROWS, WIN), F32)
    return pl.pallas_call(
        body, name=name, grid=(B, nb),
        in_specs=[pl.BlockSpec((1, QROWS, D), lambda b, j: (b, j, 0)),
                  pl.BlockSpec(memory_space=pl.ANY), pl.BlockSpec(memory_space=pl.ANY)],
        out_specs=pl.BlockSpec((1, QROWS, D), lambda b, j: (b, j, 0)),
        out_shape=jax.ShapeDtypeStruct((B, S, D), BF16),
        scratch_shapes=[pltpu.VMEM((PADR + S, D), BF16), pltpu.VMEM((PADR + S, D), BF16),
                        pltpu.VMEM(bias.shape, F32), pltpu.SemaphoreType.DMA((3,))],
        compiler_params=_params(2 * _nbytes((QROWS, D), BF16), scr),
    )(qkv, bias, qkv)


def attn_a_bwd(name, qkv, do, bias):
    B, S, D3 = qkv.shape
    D = D3 // 3
    H = D // HEAD_DIM_A
    nb = S // QROWS
    scale = HEAD_DIM_A ** -0.5

    def body(q_ref, do_ref, bias_hbm, qkv_hbm, dq_ref, dkv_hbm, dbias_hbm, kpad, vpad, dkacc, dvacc, bias_v, dbias_v, sem):
        b, j = pl.program_id(0), pl.program_id(1)

        @pl.when((b == 0) & (j == 0))
        def _():
            _attn_a_load_bias(bias_hbm, bias_v, sem.at[2])
            dbias_v[...] = jnp.zeros_like(dbias_v)

        @pl.when(j == 0)
        def _():
            _attn_a_load_kv(qkv_hbm, b, kpad, vpad, sem, S, D)
            dkacc[...] = jnp.zeros_like(dkacc)
            dvacc[...] = jnp.zeros_like(dvacc)

        mask = _attn_a_mask(j)
        w0 = pl.multiple_of(j * QROWS, QROWS)
        for p in range(H // 2):
            ls = slice(p * LANE, (p + 1) * LANE)
            q2 = _stack_pair(q_ref[0, :, ls])
            do2 = _stack_pair(do_ref[0, :, ls])
            kp = kpad[pl.ds(w0, WIN), ls]
            vp = vpad[pl.ds(w0, WIN), ls]
            pr = _attn_a_probs(q2, kp, bias_v[p], mask, scale)
            dp = lax.dot_general(do2, vp, NT, preferred_element_type=F32)
            ds = pr * (dp - jnp.sum(pr * dp, axis=-1, keepdims=True))
            dbias_v[p] += ds
            dsb = (ds * scale).astype(BF16)
            dq_ref[0, :, ls] = _unstack_pair(jnp.dot(dsb, kp, preferred_element_type=F32))
            dkacc[pl.ds(w0, WIN), ls] += lax.dot_general(dsb, q2, TN, preferred_element_type=F32)
            dvacc[pl.ds(w0, WIN), ls] += lax.dot_general(pr.astype(BF16), do2, TN, preferred_element_type=F32)

        @pl.when(j == nb - 1)
        def _():
            ck = pltpu.make_async_copy(dkacc.at[pl.ds(PADR, S), :], dkv_hbm.at[b, :, pl.ds(0, D)], sem.at[0])
            cv = pltpu.make_async_copy(dvacc.at[pl.ds(PADR, S), :], dkv_hbm.at[b, :, pl.ds(D, D)], sem.at[1])
            ck.start()
            cv.start()
            ck.wait()
            cv.wait()

        @pl.when((b == B - 1) & (j == nb - 1))
        def _():
            cb = pltpu.make_async_copy(dbias_v, dbias_hbm, sem.at[2])
            cb.start()
            cb.wait()

    blk = _nbytes((QROWS, D), BF16) * 2 + _nbytes((QROWS, D), F32)
    scr = (2 * _nbytes((PADR + S, D), BF16) + 2 * _nbytes((PADR + S, D), F32) + 2 * _nbytes(bias.shape, F32)
           + 8 * _nbytes((2 * QROWS, WIN), F32))
    return pl.pallas_call(
        body, name=name, grid=(B, nb),
        in_specs=[pl.BlockSpec((1, QROWS, D), lambda b, j: (b, j, 0)),
                  pl.BlockSpec((1, QROWS, D), lambda b, j: (b, j, 0)),
                  pl.BlockSpec(memory_space=pl.ANY), pl.BlockSpec(memory_space=pl.ANY)],
        out_specs=[pl.BlockSpec((1, QROWS, D), lambda b, j: (b, j, 0)),
                   pl.BlockSpec(memory_space=pl.ANY), pl.BlockSpec(memory_space=pl.ANY)],
        out_shape=[jax.ShapeDtypeStruct((B, S, D), F32), jax.ShapeDtypeStruct((B, S, 2 * D), F32),
                   jax.ShapeDtypeStruct(bias.shape, F32)],
        scratch_shapes=[pltpu.VMEM((PADR + S, D), BF16), pltpu.VMEM((PADR + S, D), BF16),
                        pltpu.VMEM((PADR + S, D), F32), pltpu.VMEM((PADR + S, D), F32),
                        pltpu.VMEM(bias.shape, F32), pltpu.VMEM(bias.shape, F32),
                        pltpu.SemaphoreType.DMA((3,))],
        compiler_params=_params(blk, scr),
    )(qkv, do, bias, qkv)


def _mla_scores(q, kblk, krb, q0, k0, scale):
    s = (lax.dot_general(q[:, :NOPE], kblk[:, :NOPE], NT, preferred_element_type=F32)
         + lax.dot_general(q[:, NOPE:], krb, NT, preferred_element_type=F32)) * scale
    qc = jnp.right_shift(q0 + lax.broadcasted_iota(I32, s.shape, 0), CHUNK_SHIFT)
    kc = jnp.right_shift(k0 + lax.broadcasted_iota(I32, s.shape, 1), CHUNK_SHIFT)
    return s, kc <= qc


def mla_fwd(name, qf, kv, kr):
    B, S, W = qf.shape
    HB = W // 256
    QB = _tile(S, 256, CHUNK)
    nq = S // QB
    scale = (NOPE + ROPE) ** -0.5

    def body(q_ref, kv_ref, kr_ref, o_ref, lse_ref):
        qi = pl.program_id(2)
        q = q_ref[0]

        def step(kj, carry):
            m, l, acc = carry
            ks = pl.ds(pl.multiple_of(kj * QB, QB), QB)
            kblk = kv_ref[0, ks, :]
            s, mask = _mla_scores(q, kblk, kr_ref[0, ks, :], qi * QB, kj * QB, scale)
            s = jnp.where(mask, s, NEG_INF)
            m_new = jnp.maximum(m, jnp.max(s, axis=-1, keepdims=True))
            a = jnp.exp(m - m_new)
            p = jnp.exp(s - m_new)
            l = a * l + jnp.sum(p, axis=-1, keepdims=True)
            acc = a * acc + jnp.dot(p.astype(BF16), kblk[:, NOPE:], preferred_element_type=F32)
            return m_new, l, acc

        init = (jnp.full((QB, 1), NEG_INF, F32), jnp.zeros((QB, 1), F32), jnp.zeros((QB, NOPE), F32))
        m, l, acc = lax.fori_loop(0, qi + 1, step, init)
        o_ref[0] = acc * (1.0 / l)
        lse_ref[0, 0] = jnp.broadcast_to(m + jnp.log(l), (QB, LANE))

    blk = (_nbytes((QB, 256), BF16) + _nbytes((S, 256), BF16) + _nbytes((S, LANE), BF16)
           + 2 * _nbytes((QB, LANE), F32))
    return pl.pallas_call(
        body, name=name, grid=(B, HB, nq),
        in_specs=[pl.BlockSpec((1, QB, 256), lambda b, h, i: (b, i, h)),
                  pl.BlockSpec((1, S, 256), lambda b, h, i: (b, 0, h)),
                  pl.BlockSpec((1, S, LANE), lambda b, h, i: (b, 0, 0))],
        out_specs=[pl.BlockSpec((1, QB, LANE), lambda b, h, i: (b, i, h)),
                   pl.BlockSpec((1, 1, QB, LANE), lambda b, h, i: (b, h, i, 0))],
        out_shape=[jax.ShapeDtypeStruct((B, S, HB * LANE), F32), jax.ShapeDtypeStruct((B, HB, S, LANE), F32)],
        compiler_params=_params(blk, 8 * _nbytes((QB, QB), F32)),
    )(qf, kv, kr)


def mla_bwd(name, qf, kv, kr, do, o, lse):
    B, S, W = qf.shape
    HB = W // 256
    QB = _tile(S, 256, CHUNK)
    nq = S // QB
    scale = (NOPE + ROPE) ** -0.5

    def body(q_ref, kv_ref, kr_ref, do_ref, o_ref, lse_ref, dq_ref, dkv_ref, dkr_ref):
        h = pl.program_id(1)
        dq_ref[...] = jnp.zeros_like(dq_ref)
        dkv_ref[...] = jnp.zeros_like(dkv_ref)

        @pl.when(h == 0)
        def _():
            dkr_ref[...] = jnp.zeros_like(dkr_ref)

        for qi in range(nq):
            rows = slice(qi * QB, (qi + 1) * QB)
            q = q_ref[0, rows, :]
            dob = do_ref[0, rows, :]
            lse_q = lse_ref[0, 0, rows, :][:, :1]
            delta = jnp.sum(dob.astype(F32) * o_ref[0, rows, :], axis=-1, keepdims=True)

            def step(kj, carry, q=q, dob=dob, lse_q=lse_q, delta=delta, rows=rows, qi=qi):
                ks = pl.ds(pl.multiple_of(kj * QB, QB), QB)
                kblk = kv_ref[0, ks, :]
                krb = kr_ref[0, ks, :]
                s, mask = _mla_scores(q, kblk, krb, qi * QB, kj * QB, scale)
                p = jnp.where(mask, jnp.exp(s - lse_q), 0.0)
                dp = lax.dot_general(dob, kblk[:, NOPE:], NT, preferred_element_type=F32)
                dsb = (p * (dp - delta) * scale).astype(BF16)
                dkv_ref[0, ks, NOPE:] += lax.dot_general(p.astype(BF16), dob, TN, preferred_element_type=F32)
                dkv_ref[0, ks, :NOPE] += lax.dot_general(dsb, q[:, :NOPE], TN, preferred_element_type=F32)
                dkr_ref[0, ks, :] += lax.dot_general(dsb, q[:, NOPE:], TN, preferred_element_type=F32)
                dq_ref[0, rows, :NOPE] += jnp.dot(dsb, kblk[:, :NOPE], preferred_element_type=F32)
                dq_ref[0, rows, NOPE:] += jnp.dot(dsb, krb, preferred_element_type=F32)
                return carry

            lax.fori_loop(0, qi + 1, step, 0)

    head = lambda w: pl.BlockSpec((1, S, w), lambda b, h: (b, 0, h))
    shared = pl.BlockSpec((1, S, LANE), lambda b, h: (b, 0, 0))
    blk = (2 * _nbytes((S, 256), BF16) + 2 * _nbytes((S, LANE), BF16) + 2 * _nbytes((S, LANE), F32)
           + 2 * _nbytes((S, 256), F32) + _nbytes((S, LANE), F32))
    return pl.pallas_call(
        body, name=name, grid=(B, HB),
        in_specs=[head(256), head(256), shared, head(LANE), head(LANE),
                  pl.BlockSpec((1, 1, S, LANE), lambda b, h: (b, h, 0, 0))],
        out_specs=[head(256), head(256), shared],
        out_shape=[jax.ShapeDtypeStruct((B, S, W), F32), jax.ShapeDtypeStruct((B, S, W), F32),
                   jax.ShapeDtypeStruct((B, S, LANE), F32)],
        compiler_params=_params(blk, 10 * _nbytes((QB, QB), F32)),
    )(qf, kv, kr, do, o, lse)


def cast_bf16(name, w, layer, idx):
    _, R, C = w.shape
    tr = _tile(R, 256, 16)

    def body(k_ref, w_ref, o_ref):
        o_ref[...] = w_ref[...].astype(BF16)

    return pl.pallas_call(
        body, name=name,
        grid_spec=pltpu.PrefetchScalarGridSpec(
            num_scalar_prefetch=1, grid=(R // tr,),
            in_specs=[pl.BlockSpec((None, tr, C), lambda r, k_ref: (layer, r, 0))],
            out_specs=pl.BlockSpec((None, tr, C), lambda r, k_ref: (k_ref[0], r, 0))),
        out_shape=jax.ShapeDtypeStruct((N_CHIPS, R, C), BF16),
    )(idx, w)


def adamw(name, w, g, m, v):
    R, C = w.shape
    tr = _tile(R, max(8, (1 << 18) // C // 8 * 8), 8)
    c1 = 1.0 - ADAM_B1 ** ADAM_STEP
    c2 = 1.0 - ADAM_B2 ** ADAM_STEP

    def body(w_ref, g_ref, m_ref, v_ref, d_ref, mo_ref, vo_ref):
        gv = g_ref[...]
        mn = ADAM_B1 * m_ref[...] + (1.0 - ADAM_B1) * gv
        vn = ADAM_B2 * v_ref[...] + (1.0 - ADAM_B2) * (gv * gv)
        mo_ref[...] = mn
        vo_ref[...] = vn
        d_ref[...] = -ADAM_LR * ((mn / c1) / (jnp.sqrt(vn / c2) + ADAM_EPS) + ADAM_WD * w_ref[...])

    spec = pl.BlockSpec((tr, C), lambda r: (r, 0))
    return pl.pallas_call(
        body, name=name, grid=(R // tr,), in_specs=[spec] * 4, out_specs=[spec] * 3,
        out_shape=[jax.ShapeDtypeStruct((R, C), F32)] * 3,
        compiler_params=_params(7 * _nbytes((tr, C), F32), 4 * _nbytes((tr, C), F32)),
    )(w, g, m, v)


def half_sum(name, dw, landed, idx):
    _, _, hr, C = dw.shape
    tr = _tile(hr, max(16, (1 << 18) // C // 16 * 16), 16)

    def body(i_ref, a_ref, b_ref, o_ref):
        o_ref[...] = (a_ref[...].astype(F32) + b_ref[...].astype(F32)).astype(o_ref.dtype)

    return pl.pallas_call(
        body, name=name,
        grid_spec=pltpu.PrefetchScalarGridSpec(
            num_scalar_prefetch=1, grid=(N_CHIPS, hr // tr),
            in_specs=[pl.BlockSpec((None, None, tr, C), lambda k, r, i_ref: (k, i_ref[1], r, 0)),
                      pl.BlockSpec((None, tr, C), lambda k, r, i_ref: (k, r, 0))],
            out_specs=pl.BlockSpec((None, tr, C), lambda k, r, i_ref: (k, r, 0))),
        out_shape=jax.ShapeDtypeStruct((N_CHIPS, hr, C), BF16),
    )(idx, dw, landed)


def chip_sum(name, part, landed, gbuf, layer, idx):
    _, hr, C = part.shape
    tr = _tile(hr, max(16, (1 << 18) // C // 16 * 16), 16)

    def body(i_ref, a_ref, b_ref, g_ref, o_ref):
        o_ref[...] = ((a_ref[...].astype(F32) + b_ref[0].astype(F32)) + b_ref[1].astype(F32)) + b_ref[2].astype(F32)

    return pl.pallas_call(
        body, name=name,
        grid_spec=pltpu.PrefetchScalarGridSpec(
            num_scalar_prefetch=1, grid=(hr // tr,),
            in_specs=[pl.BlockSpec((None, tr, C), lambda r, i_ref: (i_ref[0], r, 0)),
                      pl.BlockSpec((3, tr, C), lambda r, i_ref: (0, r, 0)),
                      pl.BlockSpec(memory_space=pl.ANY)],
            out_specs=pl.BlockSpec((None, None, tr, C), lambda r, i_ref: (layer, i_ref[1], r, 0))),
        out_shape=jax.ShapeDtypeStruct(gbuf.shape, F32),
        input_output_aliases={3: 0},
    )(idx, part, landed, gbuf)


ANY = pl.BlockSpec(memory_space=pl.ANY)


def _place():
    x, y, c = lax.axis_index("x"), lax.axis_index("y"), lax.axis_index("c")
    chips = [(1 - x, y), (x, 1 - y), (1 - x, 1 - y)]
    return x, y, c, chips


def all_gather_weights(parts):
    n = len(parts)

    def body(*refs):
        bufs = refs[n:2 * n]
        send_sems, recv_sems = refs[2 * n:]
        x, y, c, chips = _place()
        kme = 2 * x + y
        sib = (x, y, 1 - c)

        def ici(i, j, k):
            return pltpu.make_async_remote_copy(
                src_ref=bufs[i].at[k, c], dst_ref=bufs[i].at[k, c],
                send_sem=send_sems.at[6 * i + j], recv_sem=recv_sems.at[6 * i + j],
                device_id=(*chips[j], c), device_id_type=MESH)

        def d2d(i, j, which):
            kj = 2 * chips[j][0] + chips[j][1]
            return pltpu.make_async_remote_copy(
                src_ref=bufs[i].at[kj, which], dst_ref=bufs[i].at[kj, which],
                send_sem=send_sems.at[6 * i + 3 + j], recv_sem=recv_sems.at[6 * i + 3 + j],
                device_id=sib, device_id_type=MESH)

        for i in range(n):
            for j in range(3):
                ici(i, j, kme).start()
        for i in range(n):
            for j in range(3):
                ici(i, j, 2 * chips[j][0] + chips[j][1]).wait_recv()
                d2d(i, j, c).start()
        for i in range(n):
            for j in range(3):
                d2d(i, j, 1 - c).wait_recv()
        for i in range(n):
            for j in range(3):
                ici(i, j, kme).wait_send()
                d2d(i, j, c).wait_send()

    return pl.pallas_call(
        body, name="all_gather_weights", in_specs=[ANY] * n, out_specs=[ANY] * n,
        out_shape=[jax.ShapeDtypeStruct(p.shape, p.dtype) for p in parts],
        input_output_aliases={i: i for i in range(n)},
        scratch_shapes=[pltpu.SemaphoreType.DMA((6 * n,)), pltpu.SemaphoreType.DMA((6 * n,))],
    )(*parts)


def pair_exchange(dws):
    n = len(dws)

    def body(*refs):
        ins, outs = refs[:n], refs[n:2 * n]
        send_sems, recv_sems = refs[2 * n:]
        x, y, c, _ = _place()
        copies = []
        for i in range(n):
            copies.append(pltpu.make_async_remote_copy(
                src_ref=ins[i].at[:, 1 - c], dst_ref=outs[i],
                send_sem=send_sems.at[i], recv_sem=recv_sems.at[i],
                device_id=(x, y, 1 - c), device_id_type=MESH))
            copies[i].start()
        for cp in copies:
            cp.wait_recv()
        for cp in copies:
            cp.wait_send()

    return pl.pallas_call(
        body, name="grad_pair_exchange", in_specs=[ANY] * n, out_specs=[ANY] * n,
        out_shape=[jax.ShapeDtypeStruct((N_CHIPS, *d.shape[2:]), d.dtype) for d in dws],
        scratch_shapes=[pltpu.SemaphoreType.DMA((n,)), pltpu.SemaphoreType.DMA((n,))],
    )(*dws)


def chip_exchange(parts):
    n = len(parts)

    def body(*refs):
        ins, outs = refs[:n], refs[n:2 * n]
        send_sems, recv_sems = refs[2 * n:]
        x, y, c, chips = _place()
        copies = []
        for i in range(n):
            for j in range(3):
                kd = 2 * chips[j][0] + chips[j][1]
                copies.append(pltpu.make_async_remote_copy(
                    src_ref=ins[i].at[kd], dst_ref=outs[i].at[j],
                    send_sem=send_sems.at[3 * i + j], recv_sem=recv_sems.at[3 * i + j],
                    device_id=(*chips[j], c), device_id_type=MESH))
                copies[-1].start()
        for cp in copies:
            cp.wait_recv()
        for cp in copies:
            cp.wait_send()

    return pl.pallas_call(
        body, name="grad_chip_exchange", in_specs=[ANY] * n, out_specs=[ANY] * n,
        out_shape=[jax.ShapeDtypeStruct((3, *p.shape[1:]), p.dtype) for p in parts],
        scratch_shapes=[pltpu.SemaphoreType.DMA((3 * n,)), pltpu.SemaphoreType.DMA((3 * n,))],
    )(*parts)


def pair_assemble(gbufs):
    n = len(gbufs)

    def body(*refs):
        bufs = refs[n:2 * n]
        send_sems, recv_sems = refs[2 * n:]
        x, y, c, _ = _place()
        copies = []
        for i in range(n):
            copies.append(pltpu.make_async_remote_copy(
                src_ref=bufs[i].at[:, c], dst_ref=bufs[i].at[:, c],
                send_sem=send_sems.at[i], recv_sem=recv_sems.at[i],
                device_id=(x, y, 1 - c), device_id_type=MESH))
            copies[i].start()
        for i in range(n):
            pltpu.make_async_remote_copy(
                src_ref=bufs[i].at[:, 1 - c], dst_ref=bufs[i].at[:, 1 - c],
                send_sem=send_sems.at[i], recv_sem=recv_sems.at[i],
                device_id=(x, y, 1 - c), device_id_type=MESH).wait_recv()
        for cp in copies:
            cp.wait_send()

    return pl.pallas_call(
        body, name="grad_pair_assemble", in_specs=[ANY] * n, out_specs=[ANY] * n,
        out_shape=[jax.ShapeDtypeStruct(g.shape, g.dtype) for g in gbufs],
        input_output_aliases={i: i for i in range(n)},
        scratch_shapes=[pltpu.SemaphoreType.DMA((n,)), pltpu.SemaphoreType.DMA((n,))],
    )(*gbufs)


def all_reduce_small(vec):
    NR = vec.shape[0]
    flips = [(fx, fy, fc) for fx in (0, 1) for fy in (0, 1) for fc in (0, 1)][1:]

    def body(v_ref, o_ref, buf, send_sems, recv_sems):
        x, y, c, _ = _place()
        me = 4 * x + 2 * y + c
        buf[me] = v_ref[...]
        copies = []
        for j, (fx, fy, fc) in enumerate(flips):
            peer = (1 - x if fx else x, 1 - y if fy else y, 1 - c if fc else c)
            copies.append(pltpu.make_async_remote_copy(
                src_ref=v_ref, dst_ref=buf.at[me], send_sem=send_sems.at[j], recv_sem=recv_sems.at[j],
                device_id=peer, device_id_type=MESH))
            copies[j].start()
        for cp in copies:
            cp.wait_recv()
        for cp in copies:
            cp.wait_send()
        acc = buf[0]
        for d in range(1, 8):
            acc = acc + buf[d]
        o_ref[...] = acc

    return pl.pallas_call(
        body, name="all_reduce_small",
        in_specs=[pl.BlockSpec(memory_space=pltpu.VMEM)], out_specs=pl.BlockSpec(memory_space=pltpu.VMEM),
        out_shape=jax.ShapeDtypeStruct((NR, LANE), F32),
        scratch_shapes=[pltpu.VMEM((8, NR, LANE), F32), pltpu.SemaphoreType.DMA((7,)),
                        pltpu.SemaphoreType.DMA((7,))],
    )(vec)


def _pack(arrays):
    flat = jnp.concatenate([a.reshape(-1).astype(F32) for a in arrays])
    n = flat.shape[0]
    npad = -(-n // (8 * LANE)) * (8 * LANE)
    return jnp.pad(flat, (0, npad - n)).reshape(npad // LANE, LANE)


def _unpack(buf, like):
    flat = buf.reshape(-1)
    out, off = [], 0
    for a in like:
        out.append(flat[off:off + a.size].reshape(a.shape))
        off += a.size
    return out


def kernel(x, ffn1_norm, ffn1_w_in, ffn1_w_out, mix_norm, ffn2_norm, ffn2_w_in, ffn2_w_out, a_w_qkv, a_rel_bias, a_w_o, kv_norm, kv_w_down, kv_latent_norm, kv_w_up, b_w_dq, b_q_norm, b_w_uq, b_w_o, final_norm, loss_target, m_ffn1_norm, m_ffn1_w_in, m_ffn1_w_out, m_mix_norm, m_ffn2_norm, m_ffn2_w_in, m_ffn2_w_out, m_a_w_qkv, m_a_rel_bias, m_a_w_o, m_kv_norm, m_kv_w_down, m_kv_latent_norm, m_kv_w_up, m_b_w_dq, m_b_q_norm, m_b_w_uq, m_b_w_o, m_final_norm, v_ffn1_norm, v_ffn1_w_in, v_ffn1_w_out, v_mix_norm, v_ffn2_norm, v_ffn2_w_in, v_ffn2_w_out, v_a_w_qkv, v_a_rel_bias, v_a_w_o, v_kv_norm, v_kv_w_down, v_kv_latent_norm, v_kv_w_up, v_b_w_dq, v_b_q_norm, v_b_w_uq, v_b_w_o, v_final_norm):
    B, S, D = x.shape
    T = B * S
    HB = D // 128
    QL = b_q_norm.shape[-1]
    KVL = kv_latent_norm.shape[0]
    hpc = HB // N_CHIPS
    tabs = rope_tables(S)
    idx = jnp.stack([2 * lax.axis_index("x") + lax.axis_index("y"), lax.axis_index("c")]).astype(I32)

    def halves(a):
        return a.reshape(*a.shape[:-2], 2, a.shape[-2] // 2, a.shape[-1])

    def whole(a):
        return a.reshape(*a.shape[:-3], 2 * a.shape[-2], a.shape[-1])

    kv_w_down_p = jnp.pad(kv_w_down, ((0, 0), (0, LANE - ROPE)))[None]
    b_w_uq_p = jnp.pad(b_w_uq.reshape(1, QL, hpc, NOPE + ROPE),
                       ((0, 0), (0, 0), (0, 0), (0, LANE - ROPE))).reshape(1, QL, hpc * 256)
    sharded = [("ffn1_w_in", ffn1_w_in), ("ffn1_w_out", ffn1_w_out), ("ffn2_w_in", ffn2_w_in),
               ("ffn2_w_out", ffn2_w_out), ("a_w_qkv", a_w_qkv), ("a_w_o", a_w_o),
               ("kv_w_down", kv_w_down_p), ("kv_w_up", kv_w_up[None]), ("b_w_dq", b_w_dq),
               ("b_w_uq", b_w_uq_p), ("b_w_o", b_w_o)]
    pieces = [(a, l) for a, (_, w) in enumerate(sharded) for l in range(w.shape[0])]
    own = [cast_bf16(f"cast_{sharded[a][0]}_{l}", sharded[a][1], l, idx) for a, l in pieces]
    gathered = [whole(g) for g in all_gather_weights([halves(p) for p in own])]
    W = {(sharded[a][0], l): g for (a, l), g in zip(pieces, gathered)}

    def col(nm, l=0):
        return W[(nm, l)]

    def row(nm, l=0):
        w = W[(nm, l)]
        return w.reshape(N_CHIPS * w.shape[1], w.shape[2])

    bias = rel_bias_tile("rel_bias_tile", a_rel_bias[0])

    def ffn_fwd(tag, h, g, w_in, w_out):
        xn = rms_fwd(f"{tag}_norm", h, g)
        u = mm_colw(f"{tag}_in", xn, w_in, F32)
        act = swiglu_fwd(f"{tag}_act", u)
        return mm_roww(f"{tag}_out", act, w_out, F32, res=h, alpha=0.5), (xn, u, act)

    h0 = x.reshape(T, D)
    h1, sv_f1a = ffn_fwd("l0f1", h0, ffn1_norm[0], col("ffn1_w_in", 0), row("ffn1_w_out", 0))
    hn_a = rms_fwd("l0mix_norm", h1, mix_norm[0])
    qkv = mm_colw("l0_qkv", hn_a, col("a_w_qkv"), BF16).reshape(B, S, 3 * D)
    o_a = attn_a_fwd("l0_attn", qkv, bias).reshape(T, D)
    h2 = mm_roww("l0_attn_out", o_a, row("a_w_o"), F32, res=h1)
    h3, sv_f2a = ffn_fwd("l0f2", h2, ffn2_norm[0], col("ffn2_w_in", 0), row("ffn2_w_out", 0))

    hkv = rms_fwd("kv_norm", h3, kv_norm)
    ckr = mm_roww("kv_down", hkv, row("kv_w_down"), F32)
    ckv, kr = kvprep_fwd("kv_prep", ckr, kv_latent_norm, tabs, B, S)
    kvb = mm_colw("kv_up", ckv, col("kv_w_up"), BF16).reshape(B, S, HB * 256)

    h4, sv_f1b = ffn_fwd("l1f1", h3, ffn1_norm[1], col("ffn1_w_in", 1), row("ffn1_w_out", 1))
    hn_b = rms_fwd("l1mix_norm", h4, mix_norm[1])
    cqp = mm_roww("l1_dq", hn_b, row("b_w_dq"), F32)
    cq = rms_fwd("l1_q_norm", cqp, b_q_norm[0])
    qpre = mm_colw("l1_uq", cq, col("b_w_uq"), F32)
    qf = qprep("l1_q_rope", qpre, tabs, B, S, bwd=False).reshape(B, S, HB * 256)
    o_b, lse = mla_fwd("l1_attn", qf, kvb, kr)
    h5 = mm_roww("l1_attn_out", o_b.reshape(T, HB * LANE), row("b_w_o"), F32, res=h4)
    h6, sv_f2b = ffn_fwd("l1f2", h5, ffn2_norm[1], col("ffn2_w_in", 1), row("ffn2_w_out", 1))

    dh, g_final, loss_part = loss_head("loss_head", h6, final_norm, loss_target.reshape(T, D))

    gw = {}

    def ffn_bwd(tag, dh, h_in, g, w_in, w_out, saved, key_in, key_out):
        xn, u, act = saved
        dact = mm_roww_t(f"{tag}_dact", dh, w_out, F32, alpha=0.5)
        dwo = mm_droww(f"{tag}_dwout", act, dh, alpha=0.5)
        gw[key_out] = dwo.reshape(N_CHIPS, dwo.shape[0] // N_CHIPS, dwo.shape[1])
        du = swiglu_bwd(f"{tag}_dswiglu", u, dact)
        gw[key_in] = mm_dcolw(f"{tag}_dwin", xn, du)
        dxn = mm_colw_t(f"{tag}_dxn", du, w_in, F32)
        return rms_bwd(f"{tag}_dnorm", h_in, g, dxn, dres=dh)

    def chip_major(dw):
        return dw.reshape(N_CHIPS, dw.shape[0] // N_CHIPS, dw.shape[1])

    dh, g_f2b = ffn_bwd("l1f2b", dh, h5, ffn2_norm[1], col("ffn2_w_in", 1), row("ffn2_w_out", 1), sv_f2b,
                        ("ffn2_w_in", 1), ("ffn2_w_out", 1))
    do_b = mm_roww_t("l1_attn_do", dh, row("b_w_o"), BF16).reshape(B, S, HB * LANE)
    gw[("b_w_o", 0)] = chip_major(mm_droww("l1_attn_dwo", o_b.reshape(T, HB * LANE), dh))
    dqf, dkv, dkr = mla_bwd("l1_attn_bwd", qf, kvb, kr, do_b, o_b, lse)
    dqpre = qprep("l1_q_rope_bwd", dqf.reshape(T, HB * 256), tabs, B, S, bwd=True)
    gw[("b_w_uq", 0)] = mm_dcolw("l1_dwuq", cq, dqpre)
    dcq = mm_colw_t("l1_dcq", dqpre, col("b_w_uq"), F32)
    dcqp, g_qn = rms_bwd("l1_dq_norm", cqp, b_q_norm[0], dcq)
    gw[("b_w_dq", 0)] = chip_major(mm_droww("l1_dwdq", hn_b, dcqp))
    dhn = mm_roww_t("l1_dhn", dcqp, row("b_w_dq"), F32)
    dh, g_mixb = rms_bwd("l1_dmix", h4, mix_norm[1], dhn, dres=dh)
    dh, g_f1b = ffn_bwd("l1f1b", dh, h3, ffn1_norm[1], col("ffn1_w_in", 1), row("ffn1_w_out", 1), sv_f1b,
                        ("ffn1_w_in", 1), ("ffn1_w_out", 1))
    dkv2 = dkv.reshape(T, HB * 256)
    gw[("kv_w_up", 0)] = mm_dcolw("kv_dwup", ckv, dkv2)
    dckv = mm_colw_t("kv_dckv", dkv2, col("kv_w_up"), F32)
    dckr, g_lat = kvprep_bwd("kv_prep_bwd", ckr, kv_latent_norm, dckv, dkr, tabs, B, S)
    gw[("kv_w_down", 0)] = chip_major(mm_droww("kv_dwdown", hkv, dckr))
    dhkv = mm_roww_t("kv_dhkv", dckr, row("kv_w_down"), F32)
    dh, g_kvn = rms_bwd("kv_dnorm", h3, kv_norm, dhkv, dres=dh)
    dh, g_f2a = ffn_bwd("l0f2b", dh, h2, ffn2_norm[0], col("ffn2_w_in", 0), row("ffn2_w_out", 0), sv_f2a,
                        ("ffn2_w_in", 0), ("ffn2_w_out", 0))
    do_a = mm_roww_t("l0_attn_do", dh, row("a_w_o"), BF16).reshape(B, S, D)
    gw[("a_w_o", 0)] = chip_major(mm_droww("l0_attn_dwo", o_a, dh))
    dq_a, dkv_a, dbias = attn_a_bwd("l0_attn_bwd", qkv, do_a, bias)
    dqkv = jnp.concatenate([dq_a.reshape(T, D), dkv_a.reshape(T, 2 * D)], axis=1)
    gw[("a_w_qkv", 0)] = mm_dcolw("l0_dwqkv", hn_a, dqkv)
    dhn = mm_colw_t("l0_dhn", dqkv, col("a_w_qkv"), F32)
    dh, g_mixa = rms_bwd("l0_dmix", h1, mix_norm[0], dhn, dres=dh)
    dh, g_f1a = ffn_bwd("l0f1b", dh, h0, ffn1_norm[0], col("ffn1_w_in", 0), row("ffn1_w_out", 0), sv_f1a,
                        ("ffn1_w_in", 0), ("ffn1_w_out", 0))
    grad_x = dh.reshape(B, S, D)
    g_rel = rel_bias_grad("rel_bias_grad", dbias)[:, :2 * MAX_REL + 1][None]

    dws = [halves(gw[(sharded[a][0], l)]) for a, l in pieces]
    landed1 = pair_exchange(dws)
    parts = [half_sum(f"half_sum_{i}", dws[i], landed1[i], idx) for i in range(len(dws))]
    landed2 = chip_exchange(parts)
    gbufs = [lax.empty(halves(w).shape, F32) for _, w in sharded]
    for i, (a, l) in enumerate(pieces):
        gbufs[a] = chip_sum(f"chip_sum_{i}", parts[i], landed2[i], gbufs[a], l, idx)
    full = [whole(g) for g in pair_assemble(gbufs)]
    G = {nm: g for (nm, _), g in zip(sharded, full)}
    G["kv_w_down"] = G["kv_w_down"][0, :, :KVL + ROPE]
    G["kv_w_up"] = G["kv_w_up"][0]
    G["b_w_uq"] = G["b_w_uq"].reshape(1, QL, hpc, 256)[..., :NOPE + ROPE].reshape(b_w_uq.shape)

    small = [("ffn1_norm", jnp.stack([g_f1a, g_f1b])), ("mix_norm", jnp.stack([g_mixa, g_mixb])),
             ("ffn2_norm", jnp.stack([g_f2a, g_f2b])), ("a_rel_bias", g_rel), ("kv_norm", g_kvn),
             ("kv_latent_norm", g_lat), ("b_q_norm", g_qn[None]), ("final_norm", g_final)]
    red = all_reduce_small(_pack([loss_part] + [g for _, g in small]))
    unpacked = _unpack(red, [loss_part] + [g for _, g in small])
    loss = unpacked[0][0, 0]
    for (nm, _), g in zip(small, unpacked[1:]):
        G[nm] = g

    given = dict(ffn1_norm=(ffn1_norm, m_ffn1_norm, v_ffn1_norm), ffn1_w_in=(ffn1_w_in, m_ffn1_w_in, v_ffn1_w_in),
                 ffn1_w_out=(ffn1_w_out, m_ffn1_w_out, v_ffn1_w_out), mix_norm=(mix_norm, m_mix_norm, v_mix_norm),
                 ffn2_norm=(ffn2_norm, m_ffn2_norm, v_ffn2_norm), ffn2_w_in=(ffn2_w_in, m_ffn2_w_in, v_ffn2_w_in),
                 ffn2_w_out=(ffn2_w_out, m_ffn2_w_out, v_ffn2_w_out), a_w_qkv=(a_w_qkv, m_a_w_qkv, v_a_w_qkv),
                 a_rel_bias=(a_rel_bias, m_a_rel_bias, v_a_rel_bias), a_w_o=(a_w_o, m_a_w_o, v_a_w_o),
                 kv_norm=(kv_norm, m_kv_norm, v_kv_norm), kv_w_down=(kv_w_down, m_kv_w_down, v_kv_w_down),
                 kv_latent_norm=(kv_latent_norm, m_kv_latent_norm, v_kv_latent_norm),
                 kv_w_up=(kv_w_up, m_kv_w_up, v_kv_w_up), b_w_dq=(b_w_dq, m_b_w_dq, v_b_w_dq),
                 b_q_norm=(b_q_norm, m_b_q_norm, v_b_q_norm), b_w_uq=(b_w_uq, m_b_w_uq, v_b_w_uq),
                 b_w_o=(b_w_o, m_b_w_o, v_b_w_o), final_norm=(final_norm, m_final_norm, v_final_norm))
    order = list(given)
    delta, new_m, new_v = {}, {}, {}
    small_names = [nm for nm, _ in small]
    packed = [_pack([given[nm][k] for nm in small_names]) for k in range(3)]
    outs = adamw("adamw_small", packed[0], _pack([G[nm] for nm in small_names]), packed[1], packed[2])
    for dst, buf in zip((delta, new_m, new_v), outs):
        for nm, a in zip(small_names, _unpack(buf, [given[nm][0] for nm in small_names])):
            dst[nm] = a
    for nm, _ in sharded:
        w, m, v = given[nm]
        g = G[nm].reshape(w.shape)
        G[nm] = g
        two = lambda a: a.reshape(-1, a.shape[-1])
        d_, m_, v_ = adamw(f"adamw_{nm}", two(w), two(g), two(m), two(v))
        delta[nm], new_m[nm], new_v[nm] = d_.reshape(w.shape), m_.reshape(w.shape), v_.reshape(w.shape)

    return (loss, grad_x, *[G[n] for n in order], *[delta[n] for n in order],
            *[new_m[n] for n in order], *[new_v[n] for n in order])
```

```python
import functools
import math

import jax
import jax.numpy as jnp
from jax import lax
from jax.experimental import pallas as pl
from jax.experimental.pallas import tpu as pltpu

F32 = jnp.float32
BF16 = jnp.bfloat16
I32 = jnp.int32

CHUNK = 64
CHUNK_SHIFT = 6
HEAD_DIM_A = 64
LEFT_CHUNKS = 8
MAX_REL = 128
REL_PAD = 384
QROWS = 2 * CHUNK
WIN = (LEFT_CHUNKS + 2) * CHUNK
PADR = LEFT_CHUNKS * CHUNK
NOPE = 128
ROPE = 64
EPS = 1e-6
NEG_INF = -1e30
ROPE_THETA = 10000.0
ADAM_LR, ADAM_B1, ADAM_B2, ADAM_EPS, ADAM_WD, ADAM_STEP = 0.001, 0.9, 0.999, 1e-08, 0.01, 10
N_CHIPS = 4
LANE = 128
MESH = pl.DeviceIdType.MESH
VMEM_CAP_MB = 60

NN = (((1,), (0,)), ((), ()))
NT = (((1,), (1,)), ((), ()))
TN = (((0,), (0,)), ((), ()))


def _tile(n, pref, mult):
    t = (min(pref, n) // mult) * mult
    while t >= mult:
        if n % t == 0:
            return t
        t -= mult
    return n


def _nbytes(shape, dtype):
    return math.prod(shape) * jnp.dtype(dtype).itemsize


def _params(block_bytes, extra_bytes=0):
    need = 2 * block_bytes + extra_bytes
    mb = min(VMEM_CAP_MB, max(32, int(need * 1.25 / 2**20) + 8))
    return pltpu.CompilerParams(vmem_limit_bytes=mb * 2**20)


def _mm(name, kind, a, b, grid, a_spec, b_spec, o_spec, out_shape, out_dtype, blocks,
        red_axis=None, nred=1, alpha=1.0, res=None, res_spec=None):
    dims = {"nn": NN, "nt": NT, "tn": TN}[kind]
    has_res = res is not None
    acc_in_out = nred > 1 and out_dtype == F32 and not has_res and alpha == 1.0

    def body(*refs):
        a_ref, b_ref = refs[0], refs[1]
        r_ref = refs[2] if has_res else None
        o_ref = refs[3] if has_res else refs[2]
        p = lax.dot_general(a_ref[...].astype(BF16), b_ref[...].astype(BF16), dims,
                            preferred_element_type=F32)

        def finish(acc):
            y = acc if alpha == 1.0 else acc * alpha
            if has_res:
                y = r_ref[...] + y
            o_ref[...] = y.astype(o_ref.dtype)

        if nred == 1:
            finish(p)
            return
        k = pl.program_id(red_axis)
        tgt = o_ref if acc_in_out else refs[-1]

        @pl.when(k == 0)
        def _():
            tgt[...] = p

        @pl.when(k > 0)
        def _():
            tgt[...] += p

        if not acc_in_out:
            @pl.when(k == nred - 1)
            def _():
                finish(tgt[...])

    a_blk, b_blk, o_blk = blocks
    scratch = []
    extra = 0
    if nred > 1 and not acc_in_out:
        scratch = [pltpu.VMEM(o_blk, F32)]
        extra = _nbytes(o_blk, F32)
    blk = _nbytes(a_blk, a.dtype) + _nbytes(b_blk, b.dtype) + _nbytes(o_blk, out_dtype)
    ins, specs = [a, b], [a_spec, b_spec]
    if has_res:
        ins.append(res)
        specs.append(res_spec)
        blk += _nbytes(o_blk, res.dtype)
    extra += _nbytes(a_blk, BF16) + _nbytes(b_blk, BF16) + 2 * _nbytes(o_blk, F32)
    return pl.pallas_call(
        body, name=name, grid=grid, in_specs=specs, out_specs=o_spec,
        out_shape=jax.ShapeDtypeStruct(out_shape, out_dtype), scratch_shapes=scratch,
        compiler_params=_params(blk, extra),
    )(*ins)


def mm_colw(name, x, w3, out_dtype):
    T, K = x.shape
    _, _, nl = w3.shape
    tm = _tile(T, 512, 8)
    return _mm(name, "nn", x, w3, (N_CHIPS, T // tm),
               pl.BlockSpec((tm, K), lambda j, i: (i, 0)),
               pl.BlockSpec((None, K, nl), lambda j, i: (j, 0, 0)),
               pl.BlockSpec((tm, nl), lambda j, i: (i, j)),
               (T, N_CHIPS * nl), out_dtype, ((tm, K), (K, nl), (tm, nl)))


def mm_colw_t(name, dy, w3, out_dtype, res=None):
    T = dy.shape[0]
    _, K, nl = w3.shape
    tm = _tile(T, 512, 8)
    return _mm(name, "nt", dy, w3, (T // tm, N_CHIPS),
               pl.BlockSpec((tm, nl), lambda i, j: (i, j)),
               pl.BlockSpec((None, K, nl), lambda i, j: (j, 0, 0)),
               pl.BlockSpec((tm, K), lambda i, j: (i, 0)),
               (T, K), out_dtype, ((tm, nl), (K, nl), (tm, K)),
               red_axis=1, nred=N_CHIPS, res=res,
               res_spec=pl.BlockSpec((tm, K), lambda i, j: (i, 0)))


def mm_dcolw(name, x, dy):
    T, K = x.shape
    nl = dy.shape[1] // N_CHIPS
    tt = _tile(T, 512, 8)
    return _mm(name, "tn", x, dy, (N_CHIPS, T // tt),
               pl.BlockSpec((tt, K), lambda j, t: (t, 0)),
               pl.BlockSpec((tt, nl), lambda j, t: (t, j)),
               pl.BlockSpec((None, K, nl), lambda j, t: (j, 0, 0)),
               (N_CHIPS, K, nl), BF16, ((tt, K), (tt, nl), (K, nl)),
               red_axis=1, nred=T // tt)


def mm_roww(name, x, w2, out_dtype, res=None, alpha=1.0):
    T, Kt = x.shape
    N = w2.shape[1]
    tm = _tile(T, 512, 8)
    return _mm(name, "nn", x, w2, (T // tm,),
               pl.BlockSpec((tm, Kt), lambda i: (i, 0)),
               pl.BlockSpec((Kt, N), lambda i: (0, 0)),
               pl.BlockSpec((tm, N), lambda i: (i, 0)),
               (T, N), out_dtype, ((tm, Kt), (Kt, N), (tm, N)),
               alpha=alpha, res=res, res_spec=pl.BlockSpec((tm, N), lambda i: (i, 0)))


def mm_roww_t(name, dy, w2, out_dtype, alpha=1.0):
    T, N = dy.shape
    Kt = w2.shape[0]
    tm = _tile(T, 512, 8)
    tk = _tile(Kt, 1408, LANE)
    return _mm(name, "nt", dy, w2, (Kt // tk, T // tm),
               pl.BlockSpec((tm, N), lambda j, i: (i, 0)),
               pl.BlockSpec((tk, N), lambda j, i: (j, 0)),
               pl.BlockSpec((tm, tk), lambda j, i: (i, j)),
               (T, Kt), out_dtype, ((tm, N), (tk, N), (tm, tk)), alpha=alpha)


def mm_droww(name, x, dy, alpha=1.0):
    T, Kt = x.shape
    N = dy.shape[1]
    tt = _tile(T, 512, 8)
    tk = _tile(Kt, 1408, LANE)
    return _mm(name, "tn", x, dy, (Kt // tk, T // tt),
               pl.BlockSpec((tt, tk), lambda j, t: (t, j)),
               pl.BlockSpec((tt, N), lambda j, t: (t, 0)),
               pl.BlockSpec((tk, N), lambda j, t: (j, 0)),
               (Kt, N), BF16, ((tt, tk), (tt, N), (tk, N)),
               red_axis=1, nred=T // tt, alpha=alpha)


def rms_fwd(name, x, g):
    T, D = x.shape
    tm = _tile(T, 512, 8)

    def body(x_ref, g_ref, o_ref):
        xv = x_ref[...]
        r = lax.rsqrt(jnp.mean(xv * xv, axis=-1, keepdims=True) + EPS)
        o_ref[...] = (xv * r * g_ref[...]).astype(o_ref.dtype)

    return pl.pallas_call(
        body, name=name, grid=(T // tm,),
        in_specs=[pl.BlockSpec((tm, D), lambda i: (i, 0)), pl.BlockSpec((1, D), lambda i: (0, 0))],
        out_specs=pl.BlockSpec((tm, D), lambda i: (i, 0)),
        out_shape=jax.ShapeDtypeStruct((T, D), BF16),
        compiler_params=_params(_nbytes((tm, D), F32) * 2, 4 * _nbytes((tm, D), F32)),
    )(x, g.reshape(1, D))


def _rms_bwd_math(xv, gv, dy):
    r = lax.rsqrt(jnp.mean(xv * xv, axis=-1, keepdims=True) + EPS)
    xh = xv * r
    dyg = dy * gv
    dx = r * (dyg - xh * jnp.mean(dyg * xh, axis=-1, keepdims=True))
    dg = jnp.sum(dy * xh, axis=0, keepdims=True)
    return dx, dg


def rms_bwd(name, x, g, dy, dres=None):
    T, D = x.shape
    tm = _tile(T, 256, 8)
    has_res = dres is not None

    def body(*refs):
        x_ref, g_ref, dy_ref = refs[:3]
        r_ref = refs[3] if has_res else None
        dx_ref, dg_ref = refs[-2:]
        dx, dg = _rms_bwd_math(x_ref[...], g_ref[...], dy_ref[...].astype(F32))
        if has_res:
            dx = r_ref[...] + dx
        dx_ref[...] = dx

        @pl.when(pl.program_id(0) == 0)
        def _():
            dg_ref[...] = dg

        @pl.when(pl.program_id(0) > 0)
        def _():
            dg_ref[...] += dg

    row = pl.BlockSpec((tm, D), lambda i: (i, 0))
    vec = pl.BlockSpec((1, D), lambda i: (0, 0))
    ins, specs = [x, g.reshape(1, D), dy], [row, vec, row]
    if has_res:
        ins.append(dres)
        specs.append(row)
    dx, dg = pl.pallas_call(
        body, name=name, grid=(T // tm,), in_specs=specs, out_specs=[row, vec],
        out_shape=[jax.ShapeDtypeStruct((T, D), F32), jax.ShapeDtypeStruct((1, D), F32)],
        compiler_params=_params(_nbytes((tm, D), F32) * 4, 6 * _nbytes((tm, D), F32)),
    )(*ins)
    return dx, dg.reshape(D)


def swiglu_fwd(name, u):
    T, F2 = u.shape
    F = F2 // 2
    tm = _tile(T, 256, 8)

    def body(u_ref, o_ref):
        u1 = u_ref[:, :F]
        u2 = u_ref[:, F:]
        o_ref[...] = (u1 * jax.nn.sigmoid(u1) * u2).astype(o_ref.dtype)

    return pl.pallas_call(
        body, name=name, grid=(T // tm,),
        in_specs=[pl.BlockSpec((tm, F2), lambda i: (i, 0))],
        out_specs=pl.BlockSpec((tm, F), lambda i: (i, 0)),
        out_shape=jax.ShapeDtypeStruct((T, F), BF16),
        compiler_params=_params(_nbytes((tm, F2), F32) * 2, _nbytes((tm, F2), F32) * 2),
    )(u)


def swiglu_bwd(name, u, dact):
    T, F2 = u.shape
    F = F2 // 2
    tm = _tile(T, 256, 8)

    def body(u_ref, d_ref, o_ref):
        u1 = u_ref[:, :F]
        u2 = u_ref[:, F:]
        d = d_ref[...]
        sig = jax.nn.sigmoid(u1)
        silu = u1 * sig
        o_ref[:, :F] = (d * u2 * (sig * (1.0 + u1 * (1.0 - sig)))).astype(o_ref.dtype)
        o_ref[:, F:] = (d * silu).astype(o_ref.dtype)

    return pl.pallas_call(
        body, name=name, grid=(T // tm,),
        in_specs=[pl.BlockSpec((tm, F2), lambda i: (i, 0)), pl.BlockSpec((tm, F), lambda i: (i, 0))],
        out_specs=pl.BlockSpec((tm, F2), lambda i: (i, 0)),
        out_shape=jax.ShapeDtypeStruct((T, F2), BF16),
        compiler_params=_params(_nbytes((tm, F2), F32) * 2, _nbytes((tm, F2), F32) * 3),
    )(u, dact)


def loss_head(name, h, g, target):
    T, D = h.shape
    tm = _tile(T, 256, 8)

    def body(h_ref, g_ref, t_ref, dh_ref, dg_ref, loss_ref):
        xv = h_ref[...]
        gv = g_ref[...]
        r = lax.rsqrt(jnp.mean(xv * xv, axis=-1, keepdims=True) + EPS)
        err = xv * r * gv - t_ref[...]
        part = 0.5 * jnp.sum(jnp.mean(err * err, axis=-1, keepdims=True), axis=0, keepdims=True)
        dx, dg = _rms_bwd_math(xv, gv, err * (1.0 / D))
        dh_ref[...] = dx
        part = jnp.broadcast_to(part, (1, LANE))

        @pl.when(pl.program_id(0) == 0)
        def _():
            dg_ref[...] = dg
            loss_ref[...] = part

        @pl.when(pl.program_id(0) > 0)
        def _():
            dg_ref[...] += dg
            loss_ref[...] += part

    row = pl.BlockSpec((tm, D), lambda i: (i, 0))
    vec = pl.BlockSpec((1, D), lambda i: (0, 0))
    dh, dg, loss = pl.pallas_call(
        body, name=name, grid=(T // tm,), in_specs=[row, vec, row],
        out_specs=[row, vec, pl.BlockSpec((1, LANE), lambda i: (0, 0))],
        out_shape=[jax.ShapeDtypeStruct((T, D), F32), jax.ShapeDtypeStruct((1, D), F32),
                   jax.ShapeDtypeStruct((1, LANE), F32)],
        compiler_params=_params(_nbytes((tm, D), F32) * 3, 6 * _nbytes((tm, D), F32)),
    )(h, g.reshape(1, D), target)
    return dh, dg.reshape(D), loss


def rope_tables(S):
    half = ROPE // 2
    freqs = ROPE_THETA ** (-jnp.arange(half, dtype=F32) / half)
    ang = jnp.arange(S, dtype=F32)[:, None] * freqs[None, :]
    cos, sin = jnp.cos(ang), jnp.sin(ang)
    z = jnp.zeros_like(cos)
    ct = jnp.concatenate([cos, cos, z, z], axis=1)
    s1 = jnp.concatenate([-sin, z, z, z], axis=1)
    s2 = jnp.concatenate([z, sin, z, z], axis=1)
    return ct, s1, s2


def _rope_tile(t, ct, s1, s2):
    return t * ct + pltpu.roll(t, 96, 1) * s1 + pltpu.roll(t, 32, 1) * s2


def _rope_tile_bwd(d, ct, s1, s2):
    return d * ct + pltpu.roll(d * s1, 32, 1) + pltpu.roll(d * s2, 96, 1)


def qprep(name, q, tabs, B, S, bwd):
    T, W = q.shape
    nh = W // 256
    ts = _tile(S, 256, 8)
    fn = _rope_tile_bwd if bwd else _rope_tile

    def body(q_ref, ct_ref, s1_ref, s2_ref, o_ref):
        ct, s1, s2 = ct_ref[...], s1_ref[...], s2_ref[...]
        for h in range(nh):
            o_ref[0, :, 256 * h:256 * h + 128] = q_ref[0, :, 256 * h:256 * h + 128].astype(o_ref.dtype)
            t = q_ref[0, :, 256 * h + 128:256 * h + 256].astype(F32)
            o_ref[0, :, 256 * h + 128:256 * h + 256] = fn(t, ct, s1, s2).astype(o_ref.dtype)

    row = pl.BlockSpec((1, ts, W), lambda b, s: (b, s, 0))
    tab = pl.BlockSpec((ts, LANE), lambda b, s: (s, 0))
    out = pl.pallas_call(
        body, name=name, grid=(B, S // ts), in_specs=[row, tab, tab, tab], out_specs=row,
        out_shape=jax.ShapeDtypeStruct((B, S, W), BF16),
        compiler_params=_params(_nbytes((ts, W), F32) * 2, _nbytes((ts, W), F32) * 2),
    )(q.reshape(B, S, W), *tabs)
    return out.reshape(T, W)


def kvprep_fwd(name, ckr, g, tabs, B, S):
    T, W = ckr.shape
    KVL = W - LANE
    ts = _tile(S, 256, 8)

    def body(x_ref, g_ref, ct_ref, s1_ref, s2_ref, c_ref, k_ref):
        xv = x_ref[0, :, :KVL]
        r = lax.rsqrt(jnp.mean(xv * xv, axis=-1, keepdims=True) + EPS)
        c_ref[0] = (xv * r * g_ref[...]).astype(c_ref.dtype)
        k_ref[0] = _rope_tile(x_ref[0, :, KVL:], ct_ref[...], s1_ref[...], s2_ref[...]).astype(k_ref.dtype)

    tab = pl.BlockSpec((ts, LANE), lambda b, s: (s, 0))
    c, k = pl.pallas_call(
        body, name=name, grid=(B, S // ts),
        in_specs=[pl.BlockSpec((1, ts, W), lambda b, s: (b, s, 0)), pl.BlockSpec((1, KVL), lambda b, s: (0, 0)),
                  tab, tab, tab],
        out_specs=[pl.BlockSpec((1, ts, KVL), lambda b, s: (b, s, 0)),
                   pl.BlockSpec((1, ts, LANE), lambda b, s: (b, s, 0))],
        out_shape=[jax.ShapeDtypeStruct((B, S, KVL), BF16), jax.ShapeDtypeStruct((B, S, LANE), BF16)],
        compiler_params=_params(_nbytes((ts, W), F32) * 2, _nbytes((ts, W), F32) * 2),
    )(ckr.reshape(B, S, W), g.reshape(1, KVL), *tabs)
    return c.reshape(T, KVL), k


def kvprep_bwd(name, ckr, g, dc, dkr, tabs, B, S):
    T, W = ckr.shape
    KVL = W - LANE
    ts = _tile(S, 256, 8)

    def body(x_ref, g_ref, dc_ref, dk_ref, ct_ref, s1_ref, s2_ref, o_ref, dg_ref):
        dx, dg = _rms_bwd_math(x_ref[0, :, :KVL], g_ref[...], dc_ref[0])
        o_ref[0, :, :KVL] = dx
        o_ref[0, :, KVL:] = _rope_tile_bwd(dk_ref[0], ct_ref[...], s1_ref[...], s2_ref[...])
        first = (pl.program_id(0) == 0) & (pl.program_id(1) == 0)

        @pl.when(first)
        def _():
            dg_ref[...] = dg

        @pl.when(jnp.logical_not(first))
        def _():
            dg_ref[...] += dg

    tab = pl.BlockSpec((ts, LANE), lambda b, s: (s, 0))
    vec = pl.BlockSpec((1, KVL), lambda b, s: (0, 0))
    o, dg = pl.pallas_call(
        body, name=name, grid=(B, S // ts),
        in_specs=[pl.BlockSpec((1, ts, W), lambda b, s: (b, s, 0)), vec,
                  pl.BlockSpec((1, ts, KVL), lambda b, s: (b, s, 0)),
                  pl.BlockSpec((1, ts, LANE), lambda b, s: (b, s, 0)), tab, tab, tab],
        out_specs=[pl.BlockSpec((1, ts, W), lambda b, s: (b, s, 0)), vec],
        out_shape=[jax.ShapeDtypeStruct((B, S, W), F32), jax.ShapeDtypeStruct((1, KVL), F32)],
        compiler_params=_params(_nbytes((ts, W), F32) * 4, _nbytes((ts, W), F32) * 4),
    )(ckr.reshape(B, S, W), g.reshape(1, KVL), dc.reshape(B, S, KVL), dkr, *tabs)
    return o.reshape(T, W), dg.reshape(KVL)


DIAGS = 768


def _diag_onehot():
    col = lax.broadcasted_iota(I32, (REL_PAD, DIAGS), 1)
    row = lax.broadcasted_iota(I32, (REL_PAD, DIAGS), 0)
    idx = jnp.clip(PADR + QROWS - 1 - col, -MAX_REL, MAX_REL) + MAX_REL
    return (row == idx).astype(F32)


def rel_bias_tile(name, table):
    H = table.shape[0]
    tpad = jnp.pad(table, ((0, 0), (0, REL_PAD - table.shape[1])))

    def body(t_ref, o_ref):
        g = lax.dot_general(t_ref[...], _diag_onehot(), NN, precision=lax.Precision.HIGHEST,
                            preferred_element_type=F32)
        for h in range(H):
            gb = jnp.broadcast_to(g[h:h + 1, :], (QROWS, DIAGS))
            tile = pltpu.roll(gb, DIAGS - (QROWS - 1), 1, stride=1, stride_axis=0)
            o_ref[h // 2, (h % 2) * QROWS:(h % 2 + 1) * QROWS, :] = tile[:, :WIN]

    return pl.pallas_call(
        body, name=name, out_shape=jax.ShapeDtypeStruct((H // 2, 2 * QROWS, WIN), F32),
        compiler_params=_params(0, 2 * _nbytes((H // 2, 2 * QROWS, WIN), F32)),
    )(tpad)


def rel_bias_grad(name, dbias):
    H = 2 * dbias.shape[0]

    def body(d_ref, o_ref):
        flip = (lax.broadcasted_iota(I32, (QROWS, QROWS), 0) + lax.broadcasted_iota(I32, (QROWS, QROWS), 1)
                == QROWS - 1).astype(F32)
        rows = []
        for h in range(H):
            x = d_ref[h // 2, (h % 2) * QROWS:(h % 2 + 1) * QROWS, :]
            xr = lax.dot_general(flip, x, NN, precision=lax.Precision.HIGHEST, preferred_element_type=F32)
            xp = jnp.concatenate([xr, jnp.zeros((QROWS, DIAGS - WIN), F32)], axis=1)
            y = pltpu.roll(xp, 0, 1, stride=1, stride_axis=0)
            rows.append(jnp.sum(y, axis=0, keepdims=True))
        o_ref[...] = lax.dot_general(jnp.concatenate(rows, axis=0), _diag_onehot(), NT,
                                     precision=lax.Precision.HIGHEST, preferred_element_type=F32)

    return pl.pallas_call(
        body, name=name, out_shape=jax.ShapeDtypeStruct((H, REL_PAD), F32),
        compiler_params=_params(0, 2 * _nbytes(dbias.shape, F32)),
    )(dbias)


def _stack_pair(xp):
    lane = lax.broadcasted_iota(I32, xp.shape, 1)
    z = jnp.zeros_like(xp)
    return jnp.concatenate([jnp.where(lane < HEAD_DIM_A, xp, z), jnp.where(lane >= HEAD_DIM_A, xp, z)], axis=0)


def _unstack_pair(y):
    lane = lax.broadcasted_iota(I32, (QROWS, LANE), 1)
    return jnp.where(lane < HEAD_DIM_A, y[:QROWS], y[QROWS:])


def _attn_a_mask(j):
    r = lax.broadcasted_iota(I32, (2 * QROWS, WIN), 0)
    w = lax.broadcasted_iota(I32, (2 * QROWS, WIN), 1)
    qc = jnp.right_shift(jnp.bitwise_and(r, QROWS - 1), CHUNK_SHIFT)
    kc = jnp.right_shift(w, CHUNK_SHIFT)
    return (kc >= qc) & (kc <= qc + LEFT_CHUNKS) & (w >= PADR - QROWS * j)


def _attn_a_load_bias(bias_hbm, bias_v, sem):
    cp = pltpu.make_async_copy(bias_hbm, bias_v, sem)
    cp.start()
    cp.wait()


def _attn_a_load_kv(qkv_hbm, b, kpad, vpad, sem, S, D):
    kpad[0:PADR, :] = jnp.zeros((PADR, D), BF16)
    vpad[0:PADR, :] = jnp.zeros((PADR, D), BF16)
    ck = pltpu.make_async_copy(qkv_hbm.at[b, :, pl.ds(D, D)], kpad.at[pl.ds(PADR, S), :], sem.at[0])
    cv = pltpu.make_async_copy(qkv_hbm.at[b, :, pl.ds(2 * D, D)], vpad.at[pl.ds(PADR, S), :], sem.at[1])
    ck.start()
    cv.start()
    ck.wait()
    cv.wait()


def _attn_a_probs(qm, kp, bias, valid, scale):
    s = lax.dot_general(qm, kp, NT, preferred_element_type=F32) * scale + bias
    s = jnp.where(valid, s, NEG_INF)
    e = jnp.exp(s - jnp.max(s, axis=-1, keepdims=True))
    return e * (1.0 / jnp.sum(e, axis=-1, keepdims=True))


def attn_a_fwd(name, qkv, bias):
    B, S, D3 = qkv.shape
    D = D3 // 3
    H = D // HEAD_DIM_A
    nb = S // QROWS
    scale = HEAD_DIM_A ** -0.5

    def body(q_ref, bias_hbm, qkv_hbm, o_ref, kpad, vpad, bias_v, sem):
        b, j = pl.program_id(0), pl.program_id(1)

        @pl.when((b == 0) & (j == 0))
        def _():
            _attn_a_load_bias(bias_hbm, bias_v, sem.at[2])

        @pl.when(j == 0)
        def _():
            _attn_a_load_kv(qkv_hbm, b, kpad, vpad, sem, S, D)

        mask = _attn_a_mask(j)
        w0 = pl.multiple_of(j * QROWS, QROWS)
        for p in range(H // 2):
            ls = slice(p * LANE, (p + 1) * LANE)
            pr = _attn_a_probs(_stack_pair(q_ref[0, :, ls]), kpad[pl.ds(w0, WIN), ls], bias_v[p], mask, scale)
            o2 = jnp.dot(pr.astype(BF16), vpad[pl.ds(w0, WIN), ls], preferred_element_type=F32)
            o_ref[0, :, ls] = _unstack_pair(o2).astype(o_ref.dtype)

    scr = 2 * _nbytes((PADR + S, D), BF16) + _nbytes(bias.shape, F32) + 8 * _nbytes((2 * QROWS, WIN), F32)
    return pl.pallas_call(
        body, name=name, grid=(B, nb),
        in_specs=[pl.BlockSpec((1, QROWS, D), lambda b, j: (b, j, 0)),
                  pl.BlockSpec(memory_space=pl.ANY), pl.BlockSpec(memory_space=pl.ANY)],
        out_specs=pl.BlockSpec((1, QROWS, D), lambda b, j: (b, j, 0)),
        out_shape=jax.ShapeDtypeStruct((B, S, D), BF16),
        scratch_shapes=[pltpu.VMEM((PADR + S, D), BF16), pltpu.VMEM((PADR + S, D), BF16),
                        pltpu.VMEM(bias.shape, F32), pltpu.SemaphoreType.DMA((3,))],
        compiler_params=_params(2 * _nbytes((QROWS, D), BF16), scr),
    )(qkv, bias, qkv)


def attn_a_bwd(name, qkv, do, bias):
    B, S, D3 = qkv.shape
    D = D3 // 3
    H = D // HEAD_DIM_A
    nb = S // QROWS
    scale = HEAD_DIM_A ** -0.5

    def body(q_ref, do_ref, bias_hbm, qkv_hbm, dq_ref, dkv_hbm, dbias_hbm, kpad, vpad, dkacc, dvacc, bias_v, dbias_v, sem):
        b, j = pl.program_id(0), pl.program_id(1)

        @pl.when((b == 0) & (j == 0))
        def _():
            _attn_a_load_bias(bias_hbm, bias_v, sem.at[2])
            dbias_v[...] = jnp.zeros_like(dbias_v)

        @pl.when(j == 0)
        def _():
            _attn_a_load_kv(qkv_hbm, b, kpad, vpad, sem, S, D)
            dkacc[...] = jnp.zeros_like(dkacc)
            dvacc[...] = jnp.zeros_like(dvacc)

        mask = _attn_a_mask(j)
        w0 = pl.multiple_of(j * QROWS, QROWS)
        for p in range(H // 2):
            ls = slice(p * LANE, (p + 1) * LANE)
            q2 = _stack_pair(q_ref[0, :, ls])
            do2 = _stack_pair(do_ref[0, :, ls])
            kp = kpad[pl.ds(w0, WIN), ls]
            vp = vpad[pl.ds(w0, WIN), ls]
            pr = _attn_a_probs(q2, kp, bias_v[p], mask, scale)
            dp = lax.dot_general(do2, vp, NT, preferred_element_type=F32)
            ds = pr * (dp - jnp.sum(pr * dp, axis=-1, keepdims=True))
            dbias_v[p] += ds
            dsb = (ds * scale).astype(BF16)
            dq_ref[0, :, ls] = _unstack_pair(jnp.dot(dsb, kp, preferred_element_type=F32))
            dkacc[pl.ds(w0, WIN), ls] += lax.dot_general(dsb, q2, TN, preferred_element_type=F32)
            dvacc[pl.ds(w0, WIN), ls] += lax.dot_general(pr.astype(BF16), do2, TN, preferred_element_type=F32)

        @pl.when(j == nb - 1)
        def _():
            ck = pltpu.make_async_copy(dkacc.at[pl.ds(PADR, S), :], dkv_hbm.at[b, :, pl.ds(0, D)], sem.at[0])
            cv = pltpu.make_async_copy(dvacc.at[pl.ds(PADR, S), :], dkv_hbm.at[b, :, pl.ds(D, D)], sem.at[1])
            ck.start()
            cv.start()
            ck.wait()
            cv.wait()

        @pl.when((b == B - 1) & (j == nb - 1))
        def _():
            cb = pltpu.make_async_copy(dbias_v, dbias_hbm, sem.at[2])
            cb.start()
            cb.wait()

    blk = _nbytes((QROWS, D), BF16) * 2 + _nbytes((QROWS, D), F32)
    scr = (2 * _nbytes((PADR + S, D), BF16) + 2 * _nbytes((PADR + S, D), F32) + 2 * _nbytes(bias.shape, F32)
           + 8 * _nbytes((2 * QROWS, WIN), F32))
    return pl.pallas_call(
        body, name=name, grid=(B, nb),
        in_specs=[pl.BlockSpec((1, QROWS, D), lambda b, j: (b, j, 0)),
                  pl.BlockSpec((1, QROWS, D), lambda b, j: (b, j, 0)),
                  pl.BlockSpec(memory_space=pl.ANY), pl.BlockSpec(memory_space=pl.ANY)],
        out_specs=[pl.BlockSpec((1, QROWS, D), lambda b, j: (b, j, 0)),
                   pl.BlockSpec(memory_space=pl.ANY), pl.BlockSpec(memory_space=pl.ANY)],
        out_shape=[jax.ShapeDtypeStruct((B, S, D), F32), jax.ShapeDtypeStruct((B, S, 2 * D), F32),
                   jax.ShapeDtypeStruct(bias.shape, F32)],
        scratch_shapes=[pltpu.VMEM((PADR + S, D), BF16), pltpu.VMEM((PADR + S, D), BF16),
                        pltpu.VMEM((PADR + S, D), F32), pltpu.VMEM((PADR + S, D), F32),
                        pltpu.VMEM(bias.shape, F32), pltpu.VMEM(bias.shape, F32),
                        pltpu.SemaphoreType.DMA((3,))],
        compiler_params=_params(blk, scr),
    )(qkv, do, bias, qkv)


def _mla_scores_t(k2blk, q, q0, k0, scale):
    st = lax.dot_general(k2blk, q, NT, preferred_element_type=F32) * scale
    kc = jnp.right_shift(k0 + lax.broadcasted_iota(I32, st.shape, 0), CHUNK_SHIFT)
    qc = jnp.right_shift(q0 + lax.broadcasted_iota(I32, st.shape, 1), CHUNK_SHIFT)
    return st, kc <= qc


def _mla_fill_keys(kv_ref, kr_ref, k2):
    k2[:, :NOPE] = kv_ref[0, :, :NOPE]
    k2[:, NOPE:] = kr_ref[0]


def _t(x):
    return x.astype(F32).T


def mla_fwd(name, qf, kv, kr):
    B, S, W = qf.shape
    HB = W // 256
    QB = _tile(S, 256, CHUNK)
    nq = S // QB
    scale = (NOPE + ROPE) ** -0.5

    def body(q_ref, kv_ref, kr_ref, o_ref, lse_ref, k2, vt):
        qi = pl.program_id(2)

        @pl.when(qi == 0)
        def _():
            _mla_fill_keys(kv_ref, kr_ref, k2)
            for kj in range(nq):
                vt[kj] = _t(kv_ref[0, kj * QB:(kj + 1) * QB, NOPE:]).astype(BF16)

        q = q_ref[0]

        def step(kj, carry):
            m, l, acc = carry
            ks = pl.ds(pl.multiple_of(kj * QB, QB), QB)
            st, mask = _mla_scores_t(k2[ks, :], q, qi * QB, kj * QB, scale)
            st = jnp.where(mask, st, NEG_INF)
            m_new = jnp.maximum(m, jnp.max(st, axis=0, keepdims=True))
            a = jnp.exp(m - m_new)
            pt = jnp.exp(st - m_new)
            l = a * l + jnp.sum(pt, axis=0, keepdims=True)
            acc = a * acc + jnp.dot(vt[kj], pt.astype(BF16), preferred_element_type=F32)
            return m_new, l, acc

        init = (jnp.full((1, QB), NEG_INF, F32), jnp.zeros((1, QB), F32), jnp.zeros((NOPE, QB), F32))
        m, l, acc = lax.fori_loop(0, qi + 1, step, init)
        o_ref[0] = (acc * (1.0 / l)).T
        lse_ref[0, 0] = m + jnp.log(l)

    blk = (_nbytes((QB, 256), BF16) + _nbytes((S, 256), BF16) + _nbytes((S, LANE), BF16)
           + _nbytes((QB, LANE), F32))
    return pl.pallas_call(
        body, name=name, grid=(B, HB, nq),
        in_specs=[pl.BlockSpec((1, QB, 256), lambda b, h, i: (b, i, h)),
                  pl.BlockSpec((1, S, 256), lambda b, h, i: (b, 0, h)),
                  pl.BlockSpec((1, S, LANE), lambda b, h, i: (b, 0, 0))],
        out_specs=[pl.BlockSpec((1, QB, LANE), lambda b, h, i: (b, i, h)),
                   pl.BlockSpec((1, 1, 1, QB), lambda b, h, i: (b, h, 0, i))],
        out_shape=[jax.ShapeDtypeStruct((B, S, HB * LANE), F32), jax.ShapeDtypeStruct((B, HB, 1, S), F32)],
        scratch_shapes=[pltpu.VMEM((S, 256), BF16), pltpu.VMEM((nq, NOPE, QB), BF16)],
        compiler_params=_params(blk, 2 * _nbytes((S, 256), BF16) + 8 * _nbytes((QB, QB), F32)),
    )(qf, kv, kr)


def mla_bwd(name, qf, kv, kr, do, o, lse):
    B, S, W = qf.shape
    HB = W // 256
    QB = _tile(S, 256, CHUNK)
    nq = S // QB
    scale = (NOPE + ROPE) ** -0.5

    def body(q_ref, kv_ref, kr_ref, do_ref, o_ref, lse_ref, dq_ref, dkv_ref, dkr_ref, k2, kt, dot_, delta, dqt):
        h = pl.program_id(1)
        dkv_ref[...] = jnp.zeros_like(dkv_ref)

        @pl.when(h == 0)
        def _():
            dkr_ref[...] = jnp.zeros_like(dkr_ref)

        _mla_fill_keys(kv_ref, kr_ref, k2)
        for i in range(nq):
            rows = slice(i * QB, (i + 1) * QB)
            kt[i] = _t(k2[rows, :]).astype(BF16)
            dot32 = _t(do_ref[0, rows, :])
            delta[i] = jnp.sum(dot32 * o_ref[0, rows, :].T, axis=0, keepdims=True)
            dot_[i] = dot32.astype(BF16)

        for qi in range(nq):
            rows = slice(qi * QB, (qi + 1) * QB)
            q = q_ref[0, rows, :]
            dob = do_ref[0, rows, :]
            lse_q = lse_ref[0, 0, :, rows]
            delta_q = delta[qi]
            dqt[...] = jnp.zeros_like(dqt)

            def step(kj, carry, q=q, dob=dob, lse_q=lse_q, delta_q=delta_q, qi=qi):
                ks = pl.ds(pl.multiple_of(kj * QB, QB), QB)
                st, mask = _mla_scores_t(k2[ks, :], q, qi * QB, kj * QB, scale)
                pt = jnp.where(mask, jnp.exp(st - lse_q), 0.0)
                dpt = jnp.dot(kv_ref[0, ks, NOPE:], dot_[qi], preferred_element_type=F32)
                dst = (pt * (dpt - delta_q) * scale).astype(BF16)
                dkv_ref[0, ks, NOPE:] += jnp.dot(pt.astype(BF16), dob, preferred_element_type=F32)
                dk2 = jnp.dot(dst, q, preferred_element_type=F32)
                dkv_ref[0, ks, :NOPE] += dk2[:, :NOPE]
                dkr_ref[0, ks, :] += dk2[:, NOPE:]
                dqt[...] += jnp.dot(kt[kj], dst, preferred_element_type=F32)
                return carry

            lax.fori_loop(0, qi + 1, step, 0)
            dq_ref[0, rows, :] = dqt[...].T

    head = lambda w: pl.BlockSpec((1, S, w), lambda b, h: (b, 0, h))
    shared = pl.BlockSpec((1, S, LANE), lambda b, h: (b, 0, 0))
    blk = (2 * _nbytes((S, 256), BF16) + 2 * _nbytes((S, LANE), BF16) + _nbytes((S, LANE), F32)
           + 2 * _nbytes((S, 256), F32) + _nbytes((S, LANE), F32))
    scr = 3 * _nbytes((S, 256), BF16) + 10 * _nbytes((QB, QB), F32)
    return pl.pallas_call(
        body, name=name, grid=(B, HB),
        in_specs=[head(256), head(256), shared, head(LANE), head(LANE),
                  pl.BlockSpec((1, 1, 1, S), lambda b, h: (b, h, 0, 0))],
        out_specs=[head(256), head(256), shared],
        out_shape=[jax.ShapeDtypeStruct((B, S, W), F32), jax.ShapeDtypeStruct((B, S, W), F32),
                   jax.ShapeDtypeStruct((B, S, LANE), F32)],
        scratch_shapes=[pltpu.VMEM((S, 256), BF16), pltpu.VMEM((nq, 256, QB), BF16),
                        pltpu.VMEM((nq, NOPE, QB), BF16), pltpu.VMEM((nq, 1, QB), F32),
                        pltpu.VMEM((256, QB), F32)],
        compiler_params=_params(blk, scr),
    )(qf, kv, kr, do, o, lse)


def cast_bf16(name, w, layer, idx):
    _, R, C = w.shape
    tr = _tile(R, 256, 16)

    def body(k_ref, w_ref, o_ref):
        o_ref[...] = w_ref[...].astype(BF16)

    return pl.pallas_call(
        body, name=name,
        grid_spec=pltpu.PrefetchScalarGridSpec(
            num_scalar_prefetch=1, grid=(R // tr,),
            in_specs=[pl.BlockSpec((None, tr, C), lambda r, k_ref: (layer, r, 0))],
            out_specs=pl.BlockSpec((None, tr, C), lambda r, k_ref: (k_ref[0], r, 0))),
        out_shape=jax.ShapeDtypeStruct((N_CHIPS, R, C), BF16),
    )(idx, w)


def adamw(name, w, g, m, v):
    R, C = w.shape
    tr = _tile(R, max(8, (1 << 18) // C // 8 * 8), 8)
    c1 = 1.0 - ADAM_B1 ** ADAM_STEP
    c2 = 1.0 - ADAM_B2 ** ADAM_STEP

    def body(w_ref, g_ref, m_ref, v_ref, d_ref, mo_ref, vo_ref):
        gv = g_ref[...]
        mn = ADAM_B1 * m_ref[...] + (1.0 - ADAM_B1) * gv
        vn = ADAM_B2 * v_ref[...] + (1.0 - ADAM_B2) * (gv * gv)
        mo_ref[...] = mn
        vo_ref[...] = vn
        d_ref[...] = -ADAM_LR * ((mn / c1) / (jnp.sqrt(vn / c2) + ADAM_EPS) + ADAM_WD * w_ref[...])

    spec = pl.BlockSpec((tr, C), lambda r: (r, 0))
    return pl.pallas_call(
        body, name=name, grid=(R // tr,), in_specs=[spec] * 4, out_specs=[spec] * 3,
        out_shape=[jax.ShapeDtypeStruct((R, C), F32)] * 3,
        compiler_params=_params(7 * _nbytes((tr, C), F32), 4 * _nbytes((tr, C), F32)),
    )(w, g, m, v)


def half_sum(name, dw, landed, idx):
    _, _, hr, C = dw.shape
    tr = _tile(hr, max(16, (1 << 18) // C // 16 * 16), 16)

    def body(i_ref, a_ref, b_ref, o_ref):
        o_ref[...] = (a_ref[...].astype(F32) + b_ref[...].astype(F32)).astype(o_ref.dtype)

    return pl.pallas_call(
        body, name=name,
        grid_spec=pltpu.PrefetchScalarGridSpec(
            num_scalar_prefetch=1, grid=(N_CHIPS, hr // tr),
            in_specs=[pl.BlockSpec((None, None, tr, C), lambda k, r, i_ref: (k, i_ref[1], r, 0)),
                      pl.BlockSpec((None, tr, C), lambda k, r, i_ref: (k, r, 0))],
            out_specs=pl.BlockSpec((None, tr, C), lambda k, r, i_ref: (k, r, 0))),
        out_shape=jax.ShapeDtypeStruct((N_CHIPS, hr, C), BF16),
    )(idx, dw, landed)


def chip_sum(name, part, landed, gbuf, layer, idx):
    _, hr, C = part.shape
    tr = _tile(hr, max(16, (1 << 18) // C // 16 * 16), 16)

    def body(i_ref, a_ref, b_ref, g_ref, o_ref):
        o_ref[...] = ((a_ref[...].astype(F32) + b_ref[0].astype(F32)) + b_ref[1].astype(F32)) + b_ref[2].astype(F32)

    return pl.pallas_call(
        body, name=name,
        grid_spec=pltpu.PrefetchScalarGridSpec(
            num_scalar_prefetch=1, grid=(hr // tr,),
            in_specs=[pl.BlockSpec((None, tr, C), lambda r, i_ref: (i_ref[0], r, 0)),
                      pl.BlockSpec((3, tr, C), lambda r, i_ref: (0, r, 0)),
                      pl.BlockSpec(memory_space=pl.ANY)],
            out_specs=pl.BlockSpec((None, None, tr, C), lambda r, i_ref: (layer, i_ref[1], r, 0))),
        out_shape=jax.ShapeDtypeStruct(gbuf.shape, F32),
        input_output_aliases={3: 0},
    )(idx, part, landed, gbuf)


ANY = pl.BlockSpec(memory_space=pl.ANY)


def _place():
    x, y, c = lax.axis_index("x"), lax.axis_index("y"), lax.axis_index("c")
    chips = [(1 - x, y), (x, 1 - y), (1 - x, 1 - y)]
    return x, y, c, chips


def all_gather_weights(parts):
    n = len(parts)

    def body(*refs):
        bufs = refs[n:2 * n]
        send_sems, recv_sems = refs[2 * n:]
        x, y, c, chips = _place()
        kme = 2 * x + y
        sib = (x, y, 1 - c)

        def ici(i, j, k):
            return pltpu.make_async_remote_copy(
                src_ref=bufs[i].at[k, c], dst_ref=bufs[i].at[k, c],
                send_sem=send_sems.at[6 * i + j], recv_sem=recv_sems.at[6 * i + j],
                device_id=(*chips[j], c), device_id_type=MESH)

        def d2d(i, j, which):
            kj = 2 * chips[j][0] + chips[j][1]
            return pltpu.make_async_remote_copy(
                src_ref=bufs[i].at[kj, which], dst_ref=bufs[i].at[kj, which],
                send_sem=send_sems.at[6 * i + 3 + j], recv_sem=recv_sems.at[6 * i + 3 + j],
                device_id=sib, device_id_type=MESH)

        for i in range(n):
            for j in range(3):
                ici(i, j, kme).start()
        for i in range(n):
            for j in range(3):
                ici(i, j, 2 * chips[j][0] + chips[j][1]).wait_recv()
                d2d(i, j, c).start()
        for i in range(n):
            for j in range(3):
                d2d(i, j, 1 - c).wait_recv()
        for i in range(n):
            for j in range(3):
                ici(i, j, kme).wait_send()
                d2d(i, j, c).wait_send()

    return pl.pallas_call(
        body, name="all_gather_weights", in_specs=[ANY] * n, out_specs=[ANY] * n,
        out_shape=[jax.ShapeDtypeStruct(p.shape, p.dtype) for p in parts],
        input_output_aliases={i: i for i in range(n)},
        scratch_shapes=[pltpu.SemaphoreType.DMA((6 * n,)), pltpu.SemaphoreType.DMA((6 * n,))],
    )(*parts)


def pair_exchange(dws):
    n = len(dws)

    def body(*refs):
        ins, outs = refs[:n], refs[n:2 * n]
        send_sems, recv_sems = refs[2 * n:]
        x, y, c, _ = _place()
        copies = []
        for i in range(n):
            copies.append(pltpu.make_async_remote_copy(
                src_ref=ins[i].at[:, 1 - c], dst_ref=outs[i],
                send_sem=send_sems.at[i], recv_sem=recv_sems.at[i],
                device_id=(x, y, 1 - c), device_id_type=MESH))
            copies[i].start()
        for cp in copies:
            cp.wait_recv()
        for cp in copies:
            cp.wait_send()

    return pl.pallas_call(
        body, name="grad_pair_exchange", in_specs=[ANY] * n, out_specs=[ANY] * n,
        out_shape=[jax.ShapeDtypeStruct((N_CHIPS, *d.shape[2:]), d.dtype) for d in dws],
        scratch_shapes=[pltpu.SemaphoreType.DMA((n,)), pltpu.SemaphoreType.DMA((n,))],
    )(*dws)


def chip_exchange(parts):
    n = len(parts)

    def body(*refs):
        ins, outs = refs[:n], refs[n:2 * n]
        send_sems, recv_sems = refs[2 * n:]
        x, y, c, chips = _place()
        copies = []
        for i in range(n):
            for j in range(3):
                kd = 2 * chips[j][0] + chips[j][1]
                copies.append(pltpu.make_async_remote_copy(
                    src_ref=ins[i].at[kd], dst_ref=outs[i].at[j],
                    send_sem=send_sems.at[3 * i + j], recv_sem=recv_sems.at[3 * i + j],
                    device_id=(*chips[j], c), device_id_type=MESH))
                copies[-1].start()
        for cp in copies:
            cp.wait_recv()
        for cp in copies:
            cp.wait_send()

    return pl.pallas_call(
        body, name="grad_chip_exchange", in_specs=[ANY] * n, out_specs=[ANY] * n,
        out_shape=[jax.ShapeDtypeStruct((3, *p.shape[1:]), p.dtype) for p in parts],
        scratch_shapes=[pltpu.SemaphoreType.DMA((3 * n,)), pltpu.SemaphoreType.DMA((3 * n,))],
    )(*parts)


def pair_assemble(gbufs):
    n = len(gbufs)

    def body(*refs):
        bufs = refs[n:2 * n]
        send_sems, recv_sems = refs[2 * n:]
        x, y, c, _ = _place()
        copies = []
        for i in range(n):
            copies.append(pltpu.make_async_remote_copy(
                src_ref=bufs[i].at[:, c], dst_ref=bufs[i].at[:, c],
                send_sem=send_sems.at[i], recv_sem=recv_sems.at[i],
                device_id=(x, y, 1 - c), device_id_type=MESH))
            copies[i].start()
        for i in range(n):
            pltpu.make_async_remote_copy(
                src_ref=bufs[i].at[:, 1 - c], dst_ref=bufs[i].at[:, 1 - c],
                send_sem=send_sems.at[i], recv_sem=recv_sems.at[i],
                device_id=(x, y, 1 - c), device_id_type=MESH).wait_recv()
        for cp in copies:
            cp.wait_send()

    return pl.pallas_call(
        body, name="grad_pair_assemble", in_specs=[ANY] * n, out_specs=[ANY] * n,
        out_shape=[jax.ShapeDtypeStruct(g.shape, g.dtype) for g in gbufs],
        input_output_aliases={i: i for i in range(n)},
        scratch_shapes=[pltpu.SemaphoreType.DMA((n,)), pltpu.SemaphoreType.DMA((n,))],
    )(*gbufs)


def all_reduce_small(vec):
    NR = vec.shape[0]
    flips = [(fx, fy, fc) for fx in (0, 1) for fy in (0, 1) for fc in (0, 1)][1:]

    def body(v_ref, o_ref, buf, send_sems, recv_sems):
        x, y, c, _ = _place()
        me = 4 * x + 2 * y + c
        buf[me] = v_ref[...]
        copies = []
        for j, (fx, fy, fc) in enumerate(flips):
            peer = (1 - x if fx else x, 1 - y if fy else y, 1 - c if fc else c)
            copies.append(pltpu.make_async_remote_copy(
                src_ref=v_ref, dst_ref=buf.at[me], send_sem=send_sems.at[j], recv_sem=recv_sems.at[j],
                device_id=peer, device_id_type=MESH))
            copies[j].start()
        for cp in copies:
            cp.wait_recv()
        for cp in copies:
            cp.wait_send()
        acc = buf[0]
        for d in range(1, 8):
            acc = acc + buf[d]
        o_ref[...] = acc

    return pl.pallas_call(
        body, name="all_reduce_small",
        in_specs=[pl.BlockSpec(memory_space=pltpu.VMEM)], out_specs=pl.BlockSpec(memory_space=pltpu.VMEM),
        out_shape=jax.ShapeDtypeStruct((NR, LANE), F32),
        scratch_shapes=[pltpu.VMEM((8, NR, LANE), F32), pltpu.SemaphoreType.DMA((7,)),
                        pltpu.SemaphoreType.DMA((7,))],
    )(vec)


def _pack(arrays):
    flat = jnp.concatenate([a.reshape(-1).astype(F32) for a in arrays])
    n = flat.shape[0]
    npad = -(-n // (8 * LANE)) * (8 * LANE)
    return jnp.pad(flat, (0, npad - n)).reshape(npad // LANE, LANE)


def _unpack(buf, like):
    flat = buf.reshape(-1)
    out, off = [], 0
    for a in like:
        out.append(flat[off:off + a.size].reshape(a.shape))
        off += a.size
    return out


def kernel(x, ffn1_norm, ffn1_w_in, ffn1_w_out, mix_norm, ffn2_norm, ffn2_w_in, ffn2_w_out, a_w_qkv, a_rel_bias, a_w_o, kv_norm, kv_w_down, kv_latent_norm, kv_w_up, b_w_dq, b_q_norm, b_w_uq, b_w_o, final_norm, loss_target, m_ffn1_norm, m_ffn1_w_in, m_ffn1_w_out, m_mix_norm, m_ffn2_norm, m_ffn2_w_in, m_ffn2_w_out, m_a_w_qkv, m_a_rel_bias, m_a_w_o, m_kv_norm, m_kv_w_down, m_kv_latent_norm, m_kv_w_up, m_b_w_dq, m_b_q_norm, m_b_w_uq, m_b_w_o, m_final_norm, v_ffn1_norm, v_ffn1_w_in, v_ffn1_w_out, v_mix_norm, v_ffn2_norm, v_ffn2_w_in, v_ffn2_w_out, v_a_w_qkv, v_a_rel_bias, v_a_w_o, v_kv_norm, v_kv_w_down, v_kv_latent_norm, v_kv_w_up, v_b_w_dq, v_b_q_norm, v_b_w_uq, v_b_w_o, v_final_norm):
    B, S, D = x.shape
    T = B * S
    HB = D // 128
    QL = b_q_norm.shape[-1]
    KVL = kv_latent_norm.shape[0]
    hpc = HB // N_CHIPS
    tabs = rope_tables(S)
    idx = jnp.stack([2 * lax.axis_index("x") + lax.axis_index("y"), lax.axis_index("c")]).astype(I32)

    def halves(a):
        return a.reshape(*a.shape[:-2], 2, a.shape[-2] // 2, a.shape[-1])

    def whole(a):
        return a.reshape(*a.shape[:-3], 2 * a.shape[-2], a.shape[-1])

    kv_w_down_p = jnp.pad(kv_w_down, ((0, 0), (0, LANE - ROPE)))[None]
    b_w_uq_p = jnp.pad(b_w_uq.reshape(1, QL, hpc, NOPE + ROPE),
                       ((0, 0), (0, 0), (0, 0), (0, LANE - ROPE))).reshape(1, QL, hpc * 256)
    sharded = [("ffn1_w_in", ffn1_w_in), ("ffn1_w_out", ffn1_w_out), ("ffn2_w_in", ffn2_w_in),
               ("ffn2_w_out", ffn2_w_out), ("a_w_qkv", a_w_qkv), ("a_w_o", a_w_o),
               ("kv_w_down", kv_w_down_p), ("kv_w_up", kv_w_up[None]), ("b_w_dq", b_w_dq),
               ("b_w_uq", b_w_uq_p), ("b_w_o", b_w_o)]
    pieces = [(a, l) for a, (_, w) in enumerate(sharded) for l in range(w.shape[0])]
    own = [cast_bf16(f"cast_{sharded[a][0]}_{l}", sharded[a][1], l, idx) for a, l in pieces]
    gathered = [whole(g) for g in all_gather_weights([halves(p) for p in own])]
    W = {(sharded[a][0], l): g for (a, l), g in zip(pieces, gathered)}

    def col(nm, l=0):
        return W[(nm, l)]

    def row(nm, l=0):
        w = W[(nm, l)]
        return w.reshape(N_CHIPS * w.shape[1], w.shape[2])

    bias = rel_bias_tile("rel_bias_tile", a_rel_bias[0])

    def ffn_fwd(tag, h, g, w_in, w_out):
        xn = rms_fwd(f"{tag}_norm", h, g)
        u = mm_colw(f"{tag}_in", xn, w_in, F32)
        act = swiglu_fwd(f"{tag}_act", u)
        return mm_roww(f"{tag}_out", act, w_out, F32, res=h, alpha=0.5), (xn, u, act)

    h0 = x.reshape(T, D)
    h1, sv_f1a = ffn_fwd("l0f1", h0, ffn1_norm[0], col("ffn1_w_in", 0), row("ffn1_w_out", 0))
    hn_a = rms_fwd("l0mix_norm", h1, mix_norm[0])
    qkv = mm_colw("l0_qkv", hn_a, col("a_w_qkv"), BF16).reshape(B, S, 3 * D)
    o_a = attn_a_fwd("l0_attn", qkv, bias).reshape(T, D)
    h2 = mm_roww("l0_attn_out", o_a, row("a_w_o"), F32, res=h1)
    h3, sv_f2a = ffn_fwd("l0f2", h2, ffn2_norm[0], col("ffn2_w_in", 0), row("ffn2_w_out", 0))

    hkv = rms_fwd("kv_norm", h3, kv_norm)
    ckr = mm_roww("kv_down", hkv, row("kv_w_down"), F32)
    ckv, kr = kvprep_fwd("kv_prep", ckr, kv_latent_norm, tabs, B, S)
    kvb = mm_colw("kv_up", ckv, col("kv_w_up"), BF16).reshape(B, S, HB * 256)

    h4, sv_f1b = ffn_fwd("l1f1", h3, ffn1_norm[1], col("ffn1_w_in", 1), row("ffn1_w_out", 1))
    hn_b = rms_fwd("l1mix_norm", h4, mix_norm[1])
    cqp = mm_roww("l1_dq", hn_b, row("b_w_dq"), F32)
    cq = rms_fwd("l1_q_norm", cqp, b_q_norm[0])
    qpre = mm_colw("l1_uq", cq, col("b_w_uq"), F32)
    qf = qprep("l1_q_rope", qpre, tabs, B, S, bwd=False).reshape(B, S, HB * 256)
    o_b, lse = mla_fwd("l1_attn", qf, kvb, kr)
    h5 = mm_roww("l1_attn_out", o_b.reshape(T, HB * LANE), row("b_w_o"), F32, res=h4)
    h6, sv_f2b = ffn_fwd("l1f2", h5, ffn2_norm[1], col("ffn2_w_in", 1), row("ffn2_w_out", 1))

    dh, g_final, loss_part = loss_head("loss_head", h6, final_norm, loss_target.reshape(T, D))

    gw = {}

    def ffn_bwd(tag, dh, h_in, g, w_in, w_out, saved, key_in, key_out):
        xn, u, act = saved
        dact = mm_roww_t(f"{tag}_dact", dh, w_out, F32, alpha=0.5)
        dwo = mm_droww(f"{tag}_dwout", act, dh, alpha=0.5)
        gw[key_out] = dwo.reshape(N_CHIPS, dwo.shape[0] // N_CHIPS, dwo.shape[1])
        du = swiglu_bwd(f"{tag}_dswiglu", u, dact)
        gw[key_in] = mm_dcolw(f"{tag}_dwin", xn, du)
        dxn = mm_colw_t(f"{tag}_dxn", du, w_in, F32)
        return rms_bwd(f"{tag}_dnorm", h_in, g, dxn, dres=dh)

    def chip_major(dw):
        return dw.reshape(N_CHIPS, dw.shape[0] // N_CHIPS, dw.shape[1])

    dh, g_f2b = ffn_bwd("l1f2b", dh, h5, ffn2_norm[1], col("ffn2_w_in", 1), row("ffn2_w_out", 1), sv_f2b,
                        ("ffn2_w_in", 1), ("ffn2_w_out", 1))
    do_b = mm_roww_t("l1_attn_do", dh, row("b_w_o"), BF16).reshape(B, S, HB * LANE)
    gw[("b_w_o", 0)] = chip_major(mm_droww("l1_attn_dwo", o_b.reshape(T, HB * LANE), dh))
    dqf, dkv, dkr = mla_bwd("l1_attn_bwd", qf, kvb, kr, do_b, o_b, lse)
    dqpre = qprep("l1_q_rope_bwd", dqf.reshape(T, HB * 256), tabs, B, S, bwd=True)
    gw[("b_w_uq", 0)] = mm_dcolw("l1_dwuq", cq, dqpre)
    dcq = mm_colw_t("l1_dcq", dqpre, col("b_w_uq"), F32)
    dcqp, g_qn = rms_bwd("l1_dq_norm", cqp, b_q_norm[0], dcq)
    gw[("b_w_dq", 0)] = chip_major(mm_droww("l1_dwdq", hn_b, dcqp))
    dhn = mm_roww_t("l1_dhn", dcqp, row("b_w_dq"), F32)
    dh, g_mixb = rms_bwd("l1_dmix", h4, mix_norm[1], dhn, dres=dh)
    dh, g_f1b = ffn_bwd("l1f1b", dh, h3, ffn1_norm[1], col("ffn1_w_in", 1), row("ffn1_w_out", 1), sv_f1b,
                        ("ffn1_w_in", 1), ("ffn1_w_out", 1))
    dkv2 = dkv.reshape(T, HB * 256)
    gw[("kv_w_up", 0)] = mm_dcolw("kv_dwup", ckv, dkv2)
    dckv = mm_colw_t("kv_dckv", dkv2, col("kv_w_up"), F32)
    dckr, g_lat = kvprep_bwd("kv_prep_bwd", ckr, kv_latent_norm, dckv, dkr, tabs, B, S)
    gw[("kv_w_down", 0)] = chip_major(mm_droww("kv_dwdown", hkv, dckr))
    dhkv = mm_roww_t("kv_dhkv", dckr, row("kv_w_down"), F32)
    dh, g_kvn = rms_bwd("kv_dnorm", h3, kv_norm, dhkv, dres=dh)
    dh, g_f2a = ffn_bwd("l0f2b", dh, h2, ffn2_norm[0], col("ffn2_w_in", 0), row("ffn2_w_out", 0), sv_f2a,
                        ("ffn2_w_in", 0), ("ffn2_w_out", 0))
    do_a = mm_roww_t("l0_attn_do", dh, row("a_w_o"), BF16).reshape(B, S, D)
    gw[("a_w_o", 0)] = chip_major(mm_droww("l0_attn_dwo", o_a, dh))
    dq_a, dkv_a, dbias = attn_a_bwd("l0_attn_bwd", qkv, do_a, bias)
    dqkv = jnp.concatenate([dq_a.reshape(T, D), dkv_a.reshape(T, 2 * D)], axis=1)
    gw[("a_w_qkv", 0)] = mm_dcolw("l0_dwqkv", hn_a, dqkv)
    dhn = mm_colw_t("l0_dhn", dqkv, col("a_w_qkv"), F32)
    dh, g_mixa = rms_bwd("l0_dmix", h1, mix_norm[0], dhn, dres=dh)
    dh, g_f1a = ffn_bwd("l0f1b", dh, h0, ffn1_norm[0], col("ffn1_w_in", 0), row("ffn1_w_out", 0), sv_f1a,
                        ("ffn1_w_in", 0), ("ffn1_w_out", 0))
    grad_x = dh.reshape(B, S, D)
    g_rel = rel_bias_grad("rel_bias_grad", dbias)[:, :2 * MAX_REL + 1][None]

    dws = [halves(gw[(sharded[a][0], l)]) for a, l in pieces]
    landed1 = pair_exchange(dws)
    parts = [half_sum(f"half_sum_{i}", dws[i], landed1[i], idx) for i in range(len(dws))]
    landed2 = chip_exchange(parts)
    gbufs = [lax.empty(halves(w).shape, F32) for _, w in sharded]
    for i, (a, l) in enumerate(pieces):
        gbufs[a] = chip_sum(f"chip_sum_{i}", parts[i], landed2[i], gbufs[a], l, idx)
    full = [whole(g) for g in pair_assemble(gbufs)]
    G = {nm: g for (nm, _), g in zip(sharded, full)}
    G["kv_w_down"] = G["kv_w_down"][0, :, :KVL + ROPE]
    G["kv_w_up"] = G["kv_w_up"][0]
    G["b_w_uq"] = G["b_w_uq"].reshape(1, QL, hpc, 256)[..., :NOPE + ROPE].reshape(b_w_uq.shape)

    small = [("ffn1_norm", jnp.stack([g_f1a, g_f1b])), ("mix_norm", jnp.stack([g_mixa, g_mixb])),
             ("ffn2_norm", jnp.stack([g_f2a, g_f2b])), ("a_rel_bias", g_rel), ("kv_norm", g_kvn),
             ("kv_latent_norm", g_lat), ("b_q_norm", g_qn[None]), ("final_norm", g_final)]
    red = all_reduce_small(_pack([loss_part] + [g for _, g in small]))
    unpacked = _unpack(red, [loss_part] + [g for _, g in small])
    loss = unpacked[0][0, 0]
    for (nm, _), g in zip(small, unpacked[1:]):
        G[nm] = g

    given = dict(ffn1_norm=(ffn1_norm, m_ffn1_norm, v_ffn1_norm), ffn1_w_in=(ffn1_w_in, m_ffn1_w_in, v_ffn1_w_in),
                 ffn1_w_out=(ffn1_w_out, m_ffn1_w_out, v_ffn1_w_out), mix_norm=(mix_norm, m_mix_norm, v_mix_norm),
                 ffn2_norm=(ffn2_norm, m_ffn2_norm, v_ffn2_norm), ffn2_w_in=(ffn2_w_in, m_ffn2_w_in, v_ffn2_w_in),
                 ffn2_w_out=(ffn2_w_out, m_ffn2_w_out, v_ffn2_w_out), a_w_qkv=(a_w_qkv, m_a_w_qkv, v_a_w_qkv),
                 a_rel_bias=(a_rel_bias, m_a_rel_bias, v_a_rel_bias), a_w_o=(a_w_o, m_a_w_o, v_a_w_o),
                 kv_norm=(kv_norm, m_kv_norm, v_kv_norm), kv_w_down=(kv_w_down, m_kv_w_down, v_kv_w_down),
                 kv_latent_norm=(kv_latent_norm, m_kv_latent_norm, v_kv_latent_norm),
                 kv_w_up=(kv_w_up, m_kv_w_up, v_kv_w_up), b_w_dq=(b_w_dq, m_b_w_dq, v_b_w_dq),
                 b_q_norm=(b_q_norm, m_b_q_norm, v_b_q_norm), b_w_uq=(b_w_uq, m_b_w_uq, v_b_w_uq),
                 b_w_o=(b_w_o, m_b_w_o, v_b_w_o), final_norm=(final_norm, m_final_norm, v_final_norm))
    order = list(given)
    delta, new_m, new_v = {}, {}, {}
    small_names = [nm for nm, _ in small]
    packed = [_pack([given[nm][k] for nm in small_names]) for k in range(3)]
    outs = adamw("adamw_small", packed[0], _pack([G[nm] for nm in small_names]), packed[1], packed[2])
    for dst, buf in zip((delta, new_m, new_v), outs):
        for nm, a in zip(small_names, _unpack(buf, [given[nm][0] for nm in small_names])):
            dst[nm] = a
    for nm, _ in sharded:
        w, m, v = given[nm]
        g = G[nm].reshape(w.shape)
        G[nm] = g
        two = lambda a: a.reshape(-1, a.shape[-1])
        d_, m_, v_ = adamw(f"adamw_{nm}", two(w), two(g), two(m), two(v))
        delta[nm], new_m[nm], new_v[nm] = d_.reshape(w.shape), m_.reshape(w.shape), v_.reshape(w.shape)

    return (loss, grad_x, *[G[n] for n in order], *[delta[n] for n in order],
            *[new_m[n] for n in order], *[new_v[n] for n in order])
```

```python
import functools
import math

import jax
import jax.numpy as jnp
from jax import lax
from jax.experimental import pallas as pl
from jax.experimental.pallas import tpu as pltpu

F32 = jnp.float32
BF16 = jnp.bfloat16
I32 = jnp.int32

CHUNK = 64
CHUNK_SHIFT = 6
HEAD_DIM_A = 64
LEFT_CHUNKS = 8
MAX_REL = 128
REL_PAD = 384
QROWS = 2 * CHUNK
WIN = (LEFT_CHUNKS + 2) * CHUNK
PADR = LEFT_CHUNKS * CHUNK
NOPE = 128
ROPE = 64
EPS = 1e-6
NEG_INF = -1e30
ROPE_THETA = 10000.0
ADAM_LR, ADAM_B1, ADAM_B2, ADAM_EPS, ADAM_WD, ADAM_STEP = 0.001, 0.9, 0.999, 1e-08, 0.01, 10
N_CHIPS = 4
LANE = 128
MESH = pl.DeviceIdType.MESH
VMEM_CAP_MB = 60

NN = (((1,), (0,)), ((), ()))
NT = (((1,), (1,)), ((), ()))
TN = (((0,), (0,)), ((), ()))


def _tile(n, pref, mult):
    t = (min(pref, n) // mult) * mult
    while t >= mult:
        if n % t == 0:
            return t
        t -= mult
    return n


def _nbytes(shape, dtype):
    return math.prod(shape) * jnp.dtype(dtype).itemsize


def _params(block_bytes, extra_bytes=0):
    need = 2 * block_bytes + extra_bytes
    mb = min(VMEM_CAP_MB, max(32, int(need * 1.25 / 2**20) + 8))
    return pltpu.CompilerParams(vmem_limit_bytes=mb * 2**20)


def _mm(name, kind, a, b, grid, a_spec, b_spec, o_spec, out_shape, out_dtype, blocks,
        red_axis=None, nred=1, alpha=1.0, res=None, res_spec=None, after=None):
    dims = {"nn": NN, "nt": NT, "tn": TN}[kind]
    has_res = res is not None
    acc_in_out = nred > 1 and out_dtype == F32 and not has_res and alpha == 1.0
    n_in = 2 + has_res + (after is not None)

    def body(*refs):
        a_ref, b_ref = refs[0], refs[1]
        r_ref = refs[2] if has_res else None
        o_ref = refs[n_in]
        p = lax.dot_general(a_ref[...].astype(BF16), b_ref[...].astype(BF16), dims,
                            preferred_element_type=F32)

        def finish(acc):
            y = acc if alpha == 1.0 else acc * alpha
            if has_res:
                y = r_ref[...] + y
            o_ref[...] = y.astype(o_ref.dtype)

        if nred == 1:
            finish(p)
            return
        k = pl.program_id(red_axis)
        tgt = o_ref if acc_in_out else refs[-1]

        @pl.when(k == 0)
        def _():
            tgt[...] = p

        @pl.when(k > 0)
        def _():
            tgt[...] += p

        if not acc_in_out:
            @pl.when(k == nred - 1)
            def _():
                finish(tgt[...])

    a_blk, b_blk, o_blk = blocks
    scratch = []
    extra = 0
    if nred > 1 and not acc_in_out:
        scratch = [pltpu.VMEM(o_blk, F32)]
        extra = _nbytes(o_blk, F32)
    blk = _nbytes(a_blk, a.dtype) + _nbytes(b_blk, b.dtype) + _nbytes(o_blk, out_dtype)
    ins, specs = [a, b], [a_spec, b_spec]
    if has_res:
        ins.append(res)
        specs.append(res_spec)
        blk += _nbytes(o_blk, res.dtype)
    if after is not None:
        ins.append(after)
        specs.append(pl.BlockSpec(memory_space=pl.ANY))
    extra += _nbytes(a_blk, BF16) + _nbytes(b_blk, BF16) + 2 * _nbytes(o_blk, F32)
    return pl.pallas_call(
        body, name=name, grid=grid, in_specs=specs, out_specs=o_spec,
        out_shape=jax.ShapeDtypeStruct(out_shape, out_dtype), scratch_shapes=scratch,
        compiler_params=_params(blk, extra),
    )(*ins)


def mm_colw(name, x, w3, out_dtype):
    T, K = x.shape
    _, _, nl = w3.shape
    tm = _tile(T, 512, 8)
    return _mm(name, "nn", x, w3, (N_CHIPS, T // tm),
               pl.BlockSpec((tm, K), lambda j, i: (i, 0)),
               pl.BlockSpec((None, K, nl), lambda j, i: (j, 0, 0)),
               pl.BlockSpec((tm, nl), lambda j, i: (i, j)),
               (T, N_CHIPS * nl), out_dtype, ((tm, K), (K, nl), (tm, nl)))


def mm_colw_t(name, dy, w3, out_dtype, res=None):
    T = dy.shape[0]
    _, K, nl = w3.shape
    tm = _tile(T, 512, 8)
    return _mm(name, "nt", dy, w3, (T // tm, N_CHIPS),
               pl.BlockSpec((tm, nl), lambda i, j: (i, j)),
               pl.BlockSpec((None, K, nl), lambda i, j: (j, 0, 0)),
               pl.BlockSpec((tm, K), lambda i, j: (i, 0)),
               (T, K), out_dtype, ((tm, nl), (K, nl), (tm, K)),
               red_axis=1, nred=N_CHIPS, res=res,
               res_spec=pl.BlockSpec((tm, K), lambda i, j: (i, 0)))


def mm_dcolw(name, x, dy, after=None):
    T, K = x.shape
    nl = dy.shape[1] // N_CHIPS
    tt = _tile(T, 512, 8)
    return _mm(name, "tn", x, dy, (N_CHIPS, T // tt),
               pl.BlockSpec((tt, K), lambda j, t: (t, 0)),
               pl.BlockSpec((tt, nl), lambda j, t: (t, j)),
               pl.BlockSpec((None, K, nl), lambda j, t: (j, 0, 0)),
               (N_CHIPS, K, nl), BF16, ((tt, K), (tt, nl), (K, nl)),
               red_axis=1, nred=T // tt, after=after)


def mm_roww(name, x, w2, out_dtype, res=None, alpha=1.0):
    T, Kt = x.shape
    N = w2.shape[1]
    tm = _tile(T, 512, 8)
    return _mm(name, "nn", x, w2, (T // tm,),
               pl.BlockSpec((tm, Kt), lambda i: (i, 0)),
               pl.BlockSpec((Kt, N), lambda i: (0, 0)),
               pl.BlockSpec((tm, N), lambda i: (i, 0)),
               (T, N), out_dtype, ((tm, Kt), (Kt, N), (tm, N)),
               alpha=alpha, res=res, res_spec=pl.BlockSpec((tm, N), lambda i: (i, 0)))


def mm_roww_t(name, dy, w2, out_dtype, alpha=1.0, after=None):
    T, N = dy.shape
    Kt = w2.shape[0]
    tm = _tile(T, 512, 8)
    tk = _tile(Kt, 1408, LANE)
    return _mm(name, "nt", dy, w2, (Kt // tk, T // tm),
               pl.BlockSpec((tm, N), lambda j, i: (i, 0)),
               pl.BlockSpec((tk, N), lambda j, i: (j, 0)),
               pl.BlockSpec((tm, tk), lambda j, i: (i, j)),
               (T, Kt), out_dtype, ((tm, N), (tk, N), (tm, tk)), alpha=alpha, after=after)


def mm_droww(name, x, dy, alpha=1.0):
    T, Kt = x.shape
    N = dy.shape[1]
    tt = _tile(T, 512, 8)
    tk = _tile(Kt, 1408, LANE)
    return _mm(name, "tn", x, dy, (Kt // tk, T // tt),
               pl.BlockSpec((tt, tk), lambda j, t: (t, j)),
               pl.BlockSpec((tt, N), lambda j, t: (t, 0)),
               pl.BlockSpec((tk, N), lambda j, t: (j, 0)),
               (Kt, N), BF16, ((tt, tk), (tt, N), (tk, N)),
               red_axis=1, nred=T // tt, alpha=alpha)


def rms_fwd(name, x, g):
    T, D = x.shape
    tm = _tile(T, 512, 8)

    def body(x_ref, g_ref, o_ref):
        xv = x_ref[...]
        r = lax.rsqrt(jnp.mean(xv * xv, axis=-1, keepdims=True) + EPS)
        o_ref[...] = (xv * r * g_ref[...]).astype(o_ref.dtype)

    return pl.pallas_call(
        body, name=name, grid=(T // tm,),
        in_specs=[pl.BlockSpec((tm, D), lambda i: (i, 0)), pl.BlockSpec((1, D), lambda i: (0, 0))],
        out_specs=pl.BlockSpec((tm, D), lambda i: (i, 0)),
        out_shape=jax.ShapeDtypeStruct((T, D), BF16),
        compiler_params=_params(_nbytes((tm, D), F32) * 2, 4 * _nbytes((tm, D), F32)),
    )(x, g.reshape(1, D))


def _rms_bwd_math(xv, gv, dy):
    r = lax.rsqrt(jnp.mean(xv * xv, axis=-1, keepdims=True) + EPS)
    xh = xv * r
    dyg = dy * gv
    dx = r * (dyg - xh * jnp.mean(dyg * xh, axis=-1, keepdims=True))
    dg = jnp.sum(dy * xh, axis=0, keepdims=True)
    return dx, dg


def rms_bwd(name, x, g, dy, dres=None):
    T, D = x.shape
    tm = _tile(T, 256, 8)
    has_res = dres is not None

    def body(*refs):
        x_ref, g_ref, dy_ref = refs[:3]
        r_ref = refs[3] if has_res else None
        dx_ref, dg_ref = refs[-2:]
        dx, dg = _rms_bwd_math(x_ref[...], g_ref[...], dy_ref[...].astype(F32))
        if has_res:
            dx = r_ref[...] + dx
        dx_ref[...] = dx

        @pl.when(pl.program_id(0) == 0)
        def _():
            dg_ref[...] = dg

        @pl.when(pl.program_id(0) > 0)
        def _():
            dg_ref[...] += dg

    row = pl.BlockSpec((tm, D), lambda i: (i, 0))
    vec = pl.BlockSpec((1, D), lambda i: (0, 0))
    ins, specs = [x, g.reshape(1, D), dy], [row, vec, row]
    if has_res:
        ins.append(dres)
        specs.append(row)
    dx, dg = pl.pallas_call(
        body, name=name, grid=(T // tm,), in_specs=specs, out_specs=[row, vec],
        out_shape=[jax.ShapeDtypeStruct((T, D), F32), jax.ShapeDtypeStruct((1, D), F32)],
        compiler_params=_params(_nbytes((tm, D), F32) * 4, 6 * _nbytes((tm, D), F32)),
    )(*ins)
    return dx, dg.reshape(D)


def swiglu_fwd(name, u):
    T, F2 = u.shape
    F = F2 // 2
    tm = _tile(T, 256, 8)

    def body(u_ref, o_ref):
        u1 = u_ref[:, :F]
        u2 = u_ref[:, F:]
        o_ref[...] = (u1 * jax.nn.sigmoid(u1) * u2).astype(o_ref.dtype)

    return pl.pallas_call(
        body, name=name, grid=(T // tm,),
        in_specs=[pl.BlockSpec((tm, F2), lambda i: (i, 0))],
        out_specs=pl.BlockSpec((tm, F), lambda i: (i, 0)),
        out_shape=jax.ShapeDtypeStruct((T, F), BF16),
        compiler_params=_params(_nbytes((tm, F2), F32) * 2, _nbytes((tm, F2), F32) * 2),
    )(u)


def swiglu_bwd(name, u, dact):
    T, F2 = u.shape
    F = F2 // 2
    tm = _tile(T, 256, 8)

    def body(u_ref, d_ref, o_ref):
        u1 = u_ref[:, :F]
        u2 = u_ref[:, F:]
        d = d_ref[...]
        sig = jax.nn.sigmoid(u1)
        silu = u1 * sig
        o_ref[:, :F] = (d * u2 * (sig * (1.0 + u1 * (1.0 - sig)))).astype(o_ref.dtype)
        o_ref[:, F:] = (d * silu).astype(o_ref.dtype)

    return pl.pallas_call(
        body, name=name, grid=(T // tm,),
        in_specs=[pl.BlockSpec((tm, F2), lambda i: (i, 0)), pl.BlockSpec((tm, F), lambda i: (i, 0))],
        out_specs=pl.BlockSpec((tm, F2), lambda i: (i, 0)),
        out_shape=jax.ShapeDtypeStruct((T, F2), BF16),
        compiler_params=_params(_nbytes((tm, F2), F32) * 2, _nbytes((tm, F2), F32) * 3),
    )(u, dact)


def loss_head(name, h, g, target):
    T, D = h.shape
    tm = _tile(T, 256, 8)

    def body(h_ref, g_ref, t_ref, dh_ref, dg_ref, loss_ref):
        xv = h_ref[...]
        gv = g_ref[...]
        r = lax.rsqrt(jnp.mean(xv * xv, axis=-1, keepdims=True) + EPS)
        err = xv * r * gv - t_ref[...]
        part = 0.5 * jnp.sum(jnp.mean(err * err, axis=-1, keepdims=True), axis=0, keepdims=True)
        dx, dg = _rms_bwd_math(xv, gv, err * (1.0 / D))
        dh_ref[...] = dx
        part = jnp.broadcast_to(part, (1, LANE))

        @pl.when(pl.program_id(0) == 0)
        def _():
            dg_ref[...] = dg
            loss_ref[...] = part

        @pl.when(pl.program_id(0) > 0)
        def _():
            dg_ref[...] += dg
            loss_ref[...] += part

    row = pl.BlockSpec((tm, D), lambda i: (i, 0))
    vec = pl.BlockSpec((1, D), lambda i: (0, 0))
    dh, dg, loss = pl.pallas_call(
        body, name=name, grid=(T // tm,), in_specs=[row, vec, row],
        out_specs=[row, vec, pl.BlockSpec((1, LANE), lambda i: (0, 0))],
        out_shape=[jax.ShapeDtypeStruct((T, D), F32), jax.ShapeDtypeStruct((1, D), F32),
                   jax.ShapeDtypeStruct((1, LANE), F32)],
        compiler_params=_params(_nbytes((tm, D), F32) * 3, 6 * _nbytes((tm, D), F32)),
    )(h, g.reshape(1, D), target)
    return dh, dg.reshape(D), loss


def rope_tables(S):
    half = ROPE // 2
    freqs = ROPE_THETA ** (-jnp.arange(half, dtype=F32) / half)
    ang = jnp.arange(S, dtype=F32)[:, None] * freqs[None, :]
    cos, sin = jnp.cos(ang), jnp.sin(ang)
    z = jnp.zeros_like(cos)
    ct = jnp.concatenate([cos, cos, z, z], axis=1)
    s1 = jnp.concatenate([-sin, z, z, z], axis=1)
    s2 = jnp.concatenate([z, sin, z, z], axis=1)
    return ct, s1, s2


def _rope_tile(t, ct, s1, s2):
    return t * ct + pltpu.roll(t, 96, 1) * s1 + pltpu.roll(t, 32, 1) * s2


def _rope_tile_bwd(d, ct, s1, s2):
    return d * ct + pltpu.roll(d * s1, 32, 1) + pltpu.roll(d * s2, 96, 1)


def qprep(name, q, tabs, B, S, bwd):
    T, W = q.shape
    nh = W // 256
    ts = _tile(S, 256, 8)
    fn = _rope_tile_bwd if bwd else _rope_tile

    def body(q_ref, ct_ref, s1_ref, s2_ref, o_ref):
        ct, s1, s2 = ct_ref[...], s1_ref[...], s2_ref[...]
        for h in range(nh):
            o_ref[0, :, 256 * h:256 * h + 128] = q_ref[0, :, 256 * h:256 * h + 128].astype(o_ref.dtype)
            t = q_ref[0, :, 256 * h + 128:256 * h + 256].astype(F32)
            o_ref[0, :, 256 * h + 128:256 * h + 256] = fn(t, ct, s1, s2).astype(o_ref.dtype)

    row = pl.BlockSpec((1, ts, W), lambda b, s: (b, s, 0))
    tab = pl.BlockSpec((ts, LANE), lambda b, s: (s, 0))
    out = pl.pallas_call(
        body, name=name, grid=(B, S // ts), in_specs=[row, tab, tab, tab], out_specs=row,
        out_shape=jax.ShapeDtypeStruct((B, S, W), BF16),
        compiler_params=_params(_nbytes((ts, W), F32) * 2, _nbytes((ts, W), F32) * 2),
    )(q.reshape(B, S, W), *tabs)
    return out.reshape(T, W)


def kvprep_fwd(name, ckr, g, tabs, B, S):
    T, W = ckr.shape
    KVL = W - LANE
    ts = _tile(S, 256, 8)

    def body(x_ref, g_ref, ct_ref, s1_ref, s2_ref, c_ref, k_ref):
        xv = x_ref[0, :, :KVL]
        r = lax.rsqrt(jnp.mean(xv * xv, axis=-1, keepdims=True) + EPS)
        c_ref[0] = (xv * r * g_ref[...]).astype(c_ref.dtype)
        k_ref[0] = _rope_tile(x_ref[0, :, KVL:], ct_ref[...], s1_ref[...], s2_ref[...]).astype(k_ref.dtype)

    tab = pl.BlockSpec((ts, LANE), lambda b, s: (s, 0))
    c, k = pl.pallas_call(
        body, name=name, grid=(B, S // ts),
        in_specs=[pl.BlockSpec((1, ts, W), lambda b, s: (b, s, 0)), pl.BlockSpec((1, KVL), lambda b, s: (0, 0)),
                  tab, tab, tab],
        out_specs=[pl.BlockSpec((1, ts, KVL), lambda b, s: (b, s, 0)),
                   pl.BlockSpec((1, ts, LANE), lambda b, s: (b, s, 0))],
        out_shape=[jax.ShapeDtypeStruct((B, S, KVL), BF16), jax.ShapeDtypeStruct((B, S, LANE), BF16)],
        compiler_params=_params(_nbytes((ts, W), F32) * 2, _nbytes((ts, W), F32) * 2),
    )(ckr.reshape(B, S, W), g.reshape(1, KVL), *tabs)
    return c.reshape(T, KVL), k


def kvprep_bwd(name, ckr, g, dc, dkr, tabs, B, S):
    T, W = ckr.shape
    KVL = W - LANE
    ts = _tile(S, 256, 8)

    def body(x_ref, g_ref, dc_ref, dk_ref, ct_ref, s1_ref, s2_ref, o_ref, dg_ref):
        dx, dg = _rms_bwd_math(x_ref[0, :, :KVL], g_ref[...], dc_ref[0])
        o_ref[0, :, :KVL] = dx
        o_ref[0, :, KVL:] = _rope_tile_bwd(dk_ref[0], ct_ref[...], s1_ref[...], s2_ref[...])
        first = (pl.program_id(0) == 0) & (pl.program_id(1) == 0)

        @pl.when(first)
        def _():
            dg_ref[...] = dg

        @pl.when(jnp.logical_not(first))
        def _():
            dg_ref[...] += dg

    tab = pl.BlockSpec((ts, LANE), lambda b, s: (s, 0))
    vec = pl.BlockSpec((1, KVL), lambda b, s: (0, 0))
    o, dg = pl.pallas_call(
        body, name=name, grid=(B, S // ts),
        in_specs=[pl.BlockSpec((1, ts, W), lambda b, s: (b, s, 0)), vec,
                  pl.BlockSpec((1, ts, KVL), lambda b, s: (b, s, 0)),
                  pl.BlockSpec((1, ts, LANE), lambda b, s: (b, s, 0)), tab, tab, tab],
        out_specs=[pl.BlockSpec((1, ts, W), lambda b, s: (b, s, 0)), vec],
        out_shape=[jax.ShapeDtypeStruct((B, S, W), F32), jax.ShapeDtypeStruct((1, KVL), F32)],
        compiler_params=_params(_nbytes((ts, W), F32) * 4, _nbytes((ts, W), F32) * 4),
    )(ckr.reshape(B, S, W), g.reshape(1, KVL), dc.reshape(B, S, KVL), dkr, *tabs)
    return o.reshape(T, W), dg.reshape(KVL)


DIAGS = 768


def _diag_onehot():
    col = lax.broadcasted_iota(I32, (REL_PAD, DIAGS), 1)
    row = lax.broadcasted_iota(I32, (REL_PAD, DIAGS), 0)
    idx = jnp.clip(PADR + QROWS - 1 - col, -MAX_REL, MAX_REL) + MAX_REL
    return (row == idx).astype(F32)


def rel_bias_tile(name, table):
    H = table.shape[0]
    tpad = jnp.pad(table, ((0, 0), (0, REL_PAD - table.shape[1])))

    def body(t_ref, o_ref):
        g = lax.dot_general(t_ref[...], _diag_onehot(), NN, precision=lax.Precision.HIGHEST,
                            preferred_element_type=F32)
        for h in range(H):
            gb = jnp.broadcast_to(g[h:h + 1, :], (QROWS, DIAGS))
            tile = pltpu.roll(gb, DIAGS - (QROWS - 1), 1, stride=1, stride_axis=0)
            o_ref[h // 2, (h % 2) * QROWS:(h % 2 + 1) * QROWS, :] = tile[:, :WIN]

    return pl.pallas_call(
        body, name=name, out_shape=jax.ShapeDtypeStruct((H // 2, 2 * QROWS, WIN), F32),
        compiler_params=_params(0, 2 * _nbytes((H // 2, 2 * QROWS, WIN), F32)),
    )(tpad)


def rel_bias_grad(name, dbias):
    H = 2 * dbias.shape[0]

    def body(d_ref, o_ref):
        flip = (lax.broadcasted_iota(I32, (QROWS, QROWS), 0) + lax.broadcasted_iota(I32, (QROWS, QROWS), 1)
                == QROWS - 1).astype(F32)
        rows = []
        for h in range(H):
            x = d_ref[h // 2, (h % 2) * QROWS:(h % 2 + 1) * QROWS, :]
            xr = lax.dot_general(flip, x, NN, precision=lax.Precision.HIGHEST, preferred_element_type=F32)
            xp = jnp.concatenate([xr, jnp.zeros((QROWS, DIAGS - WIN), F32)], axis=1)
            y = pltpu.roll(xp, 0, 1, stride=1, stride_axis=0)
            rows.append(jnp.sum(y, axis=0, keepdims=True))
        o_ref[...] = lax.dot_general(jnp.concatenate(rows, axis=0), _diag_onehot(), NT,
                                     precision=lax.Precision.HIGHEST, preferred_element_type=F32)

    return pl.pallas_call(
        body, name=name, out_shape=jax.ShapeDtypeStruct((H, REL_PAD), F32),
        compiler_params=_params(0, 2 * _nbytes(dbias.shape, F32)),
    )(dbias)


def _stack_pair(xp):
    lane = lax.broadcasted_iota(I32, xp.shape, 1)
    z = jnp.zeros_like(xp)
    return jnp.concatenate([jnp.where(lane < HEAD_DIM_A, xp, z), jnp.where(lane >= HEAD_DIM_A, xp, z)], axis=0)


def _unstack_pair(y):
    lane = lax.broadcasted_iota(I32, (QROWS, LANE), 1)
    return jnp.where(lane < HEAD_DIM_A, y[:QROWS], y[QROWS:])


def _attn_a_mask(j):
    r = lax.broadcasted_iota(I32, (2 * QROWS, WIN), 0)
    w = lax.broadcasted_iota(I32, (2 * QROWS, WIN), 1)
    qc = jnp.right_shift(jnp.bitwise_and(r, QROWS - 1), CHUNK_SHIFT)
    kc = jnp.right_shift(w, CHUNK_SHIFT)
    return (kc >= qc) & (kc <= qc + LEFT_CHUNKS) & (w >= PADR - QROWS * j)


def _attn_a_load_bias(bias_hbm, bias_v, sem):
    cp = pltpu.make_async_copy(bias_hbm, bias_v, sem)
    cp.start()
    cp.wait()


def _attn_a_load_kv(qkv_hbm, b, kpad, vpad, sem, S, D):
    kpad[0:PADR, :] = jnp.zeros((PADR, D), BF16)
    vpad[0:PADR, :] = jnp.zeros((PADR, D), BF16)
    ck = pltpu.make_async_copy(qkv_hbm.at[b, :, pl.ds(D, D)], kpad.at[pl.ds(PADR, S), :], sem.at[0])
    cv = pltpu.make_async_copy(qkv_hbm.at[b, :, pl.ds(2 * D, D)], vpad.at[pl.ds(PADR, S), :], sem.at[1])
    ck.start()
    cv.start()
    ck.wait()
    cv.wait()


def _attn_a_probs(qm, kp, bias, valid, scale):
    s = lax.dot_general(qm, kp, NT, preferred_element_type=F32) * scale + bias
    s = jnp.where(valid, s, NEG_INF)
    e = jnp.exp(s - jnp.max(s, axis=-1, keepdims=True))
    return e * (1.0 / jnp.sum(e, axis=-1, keepdims=True))


def attn_a_fwd(name, qkv, bias):
    B, S, D3 = qkv.shape
    D = D3 // 3
    H = D // HEAD_DIM_A
    nb = S // QROWS
    scale = HEAD_DIM_A ** -0.5

    def body(q_ref, bias_hbm, qkv_hbm, o_ref, kpad, vpad, bias_v, sem):
        b, j = pl.program_id(0), pl.program_id(1)

        @pl.when((b == 0) & (j == 0))
        def _():
            _attn_a_load_bias(bias_hbm, bias_v, sem.at[2])

        @pl.when(j == 0)
        def _():
            _attn_a_load_kv(qkv_hbm, b, kpad, vpad, sem, S, D)

        mask = _attn_a_mask(j)
        w0 = pl.multiple_of(j * QROWS, QROWS)
        for p in range(H // 2):
            ls = slice(p * LANE, (p + 1) * LANE)
            pr = _attn_a_probs(_stack_pair(q_ref[0, :, ls]), kpad[pl.ds(w0, WIN), ls], bias_v[p], mask, scale)
            o2 = jnp.dot(pr.astype(BF16), vpad[pl.ds(w0, WIN), ls], preferred_element_type=F32)
            o_ref[0, :, ls] = _unstack_pair(o2).astype(o_ref.dtype)

    scr = 2 * _nbytes((PADR + S, D), BF16) + _nbytes(bias.shape, F32) + 8 * _nbytes((2 * QROWS, WIN), F32)
    return pl.pallas_call(
        body, name=name, grid=(B, nb),
        in_specs=[pl.BlockSpec((1, QROWS, D), lambda b, j: (b, j, 0)),
                  pl.BlockSpec(memory_space=pl.ANY), pl.BlockSpec(memory_space=pl.ANY)],
        out_specs=pl.BlockSpec((1, QROWS, D), lambda b, j: (b, j, 0)),
        out_shape=jax.ShapeDtypeStruct((B, S, D), BF16),
        scratch_shapes=[pltpu.VMEM((PADR + S, D), BF16), pltpu.VMEM((PADR + S, D), BF16),
                        pltpu.VMEM(bias.shape, F32), pltpu.SemaphoreType.DMA((3,))],
        compiler_params=_params(2 * _nbytes((QROWS, D), BF16), scr),
    )(qkv, bias, qkv)


def attn_a_bwd(name, qkv, do, bias):
    B, S, D3 = qkv.shape
    D = D3 // 3
    H = D // HEAD_DIM_A
    nb = S // QROWS
    scale = HEAD_DIM_A ** -0.5

    def body(q_ref, do_ref, bias_hbm, qkv_hbm, dq_ref, dkv_hbm, dbias_hbm, kpad, vpad, dkacc, dvacc, bias_v, dbias_v, sem):
        b, j = pl.program_id(0), pl.program_id(1)

        @pl.when((b == 0) & (j == 0))
        def _():
            _attn_a_load_bias(bias_hbm, bias_v, sem.at[2])
            dbias_v[...] = jnp.zeros_like(dbias_v)

        @pl.when(j == 0)
        def _():
            _attn_a_load_kv(qkv_hbm, b, kpad, vpad, sem, S, D)
            dkacc[...] = jnp.zeros_like(dkacc)
            dvacc[...] = jnp.zeros_like(dvacc)

        mask = _attn_a_mask(j)
        w0 = pl.multiple_of(j * QROWS, QROWS)
        for p in range(H // 2):
            ls = slice(p * LANE, (p + 1) * LANE)
            q2 = _stack_pair(q_ref[0, :, ls])
            do2 = _stack_pair(do_ref[0, :, ls])
            kp = kpad[pl.ds(w0, WIN), ls]
            vp = vpad[pl.ds(w0, WIN), ls]
            pr = _attn_a_probs(q2, kp, bias_v[p], mask, scale)
            dp = lax.dot_general(do2, vp, NT, preferred_element_type=F32)
            ds = pr * (dp - jnp.sum(pr * dp, axis=-1, keepdims=True))
            dbias_v[p] += ds
            dsb = (ds * scale).astype(BF16)
            dq_ref[0, :, ls] = _unstack_pair(jnp.dot(dsb, kp, preferred_element_type=F32))
            dkacc[pl.ds(w0, WIN), ls] += lax.dot_general(dsb, q2, TN, preferred_element_type=F32)
            dvacc[pl.ds(w0, WIN), ls] += lax.dot_general(pr.astype(BF16), do2, TN, preferred_element_type=F32)

        @pl.when(j == nb - 1)
        def _():
            ck = pltpu.make_async_copy(dkacc.at[pl.ds(PADR, S), :], dkv_hbm.at[b, :, pl.ds(0, D)], sem.at[0])
            cv = pltpu.make_async_copy(dvacc.at[pl.ds(PADR, S), :], dkv_hbm.at[b, :, pl.ds(D, D)], sem.at[1])
            ck.start()
            cv.start()
            ck.wait()
            cv.wait()

        @pl.when((b == B - 1) & (j == nb - 1))
        def _():
            cb = pltpu.make_async_copy(dbias_v, dbias_hbm, sem.at[2])
            cb.start()
            cb.wait()

    blk = _nbytes((QROWS, D), BF16) * 2 + _nbytes((QROWS, D), F32)
    scr = (2 * _nbytes((PADR + S, D), BF16) + 2 * _nbytes((PADR + S, D), F32) + 2 * _nbytes(bias.shape, F32)
           + 8 * _nbytes((2 * QROWS, WIN), F32))
    return pl.pallas_call(
        body, name=name, grid=(B, nb),
        in_specs=[pl.BlockSpec((1, QROWS, D), lambda b, j: (b, j, 0)),
                  pl.BlockSpec((1, QROWS, D), lambda b, j: (b, j, 0)),
                  pl.BlockSpec(memory_space=pl.ANY), pl.BlockSpec(memory_space=pl.ANY)],
        out_specs=[pl.BlockSpec((1, QROWS, D), lambda b, j: (b, j, 0)),
                   pl.BlockSpec(memory_space=pl.ANY), pl.BlockSpec(memory_space=pl.ANY)],
        out_shape=[jax.ShapeDtypeStruct((B, S, D), F32), jax.ShapeDtypeStruct((B, S, 2 * D), F32),
                   jax.ShapeDtypeStruct(bias.shape, F32)],
        scratch_shapes=[pltpu.VMEM((PADR + S, D), BF16), pltpu.VMEM((PADR + S, D), BF16),
                        pltpu.VMEM((PADR + S, D), F32), pltpu.VMEM((PADR + S, D), F32),
                        pltpu.VMEM(bias.shape, F32), pltpu.VMEM(bias.shape, F32),
                        pltpu.SemaphoreType.DMA((3,))],
        compiler_params=_params(blk, scr),
    )(qkv, do, bias, qkv)


def _mla_scores_t(k2blk, q, q0, k0, scale):
    st = lax.dot_general(k2blk, q, NT, preferred_element_type=F32) * scale
    kc = jnp.right_shift(k0 + lax.broadcasted_iota(I32, st.shape, 0), CHUNK_SHIFT)
    qc = jnp.right_shift(q0 + lax.broadcasted_iota(I32, st.shape, 1), CHUNK_SHIFT)
    return st, kc <= qc


def _mla_fill_keys(kv_ref, kr_ref, k2):
    k2[:, :NOPE] = kv_ref[0, :, :NOPE]
    k2[:, NOPE:] = kr_ref[0]


def _t(x):
    return x.astype(F32).T


def mla_fwd(name, qf, kv, kr):
    B, S, W = qf.shape
    HB = W // 256
    QB = _tile(S, 256, CHUNK)
    nq = S // QB
    scale = (NOPE + ROPE) ** -0.5

    def body(q_ref, kv_ref, kr_ref, o_ref, lse_ref, k2, vt):
        qi = pl.program_id(2)

        @pl.when(qi == 0)
        def _():
            _mla_fill_keys(kv_ref, kr_ref, k2)
            for kj in range(nq):
                vt[kj] = _t(kv_ref[0, kj * QB:(kj + 1) * QB, NOPE:]).astype(BF16)

        q = q_ref[0]

        def step(kj, carry):
            m, l, acc = carry
            ks = pl.ds(pl.multiple_of(kj * QB, QB), QB)
            st, mask = _mla_scores_t(k2[ks, :], q, qi * QB, kj * QB, scale)
            st = jnp.where(mask, st, NEG_INF)
            m_new = jnp.maximum(m, jnp.max(st, axis=0, keepdims=True))
            a = jnp.exp(m - m_new)
            pt = jnp.exp(st - m_new)
            l = a * l + jnp.sum(pt, axis=0, keepdims=True)
            acc = a * acc + jnp.dot(vt[kj], pt.astype(BF16), preferred_element_type=F32)
            return m_new, l, acc

        init = (jnp.full((1, QB), NEG_INF, F32), jnp.zeros((1, QB), F32), jnp.zeros((NOPE, QB), F32))
        m, l, acc = lax.fori_loop(0, qi + 1, step, init)
        o_ref[0] = (acc * (1.0 / l)).T
        lse_ref[0, 0] = m + jnp.log(l)

    blk = (_nbytes((QB, 256), BF16) + _nbytes((S, 256), BF16) + _nbytes((S, LANE), BF16)
           + _nbytes((QB, LANE), F32))
    return pl.pallas_call(
        body, name=name, grid=(B, HB, nq),
        in_specs=[pl.BlockSpec((1, QB, 256), lambda b, h, i: (b, i, h)),
                  pl.BlockSpec((1, S, 256), lambda b, h, i: (b, 0, h)),
                  pl.BlockSpec((1, S, LANE), lambda b, h, i: (b, 0, 0))],
        out_specs=[pl.BlockSpec((1, QB, LANE), lambda b, h, i: (b, i, h)),
                   pl.BlockSpec((1, 1, 1, QB), lambda b, h, i: (b, h, 0, i))],
        out_shape=[jax.ShapeDtypeStruct((B, S, HB * LANE), F32), jax.ShapeDtypeStruct((B, HB, 1, S), F32)],
        scratch_shapes=[pltpu.VMEM((S, 256), BF16), pltpu.VMEM((nq, NOPE, QB), BF16)],
        compiler_params=_params(blk, 2 * _nbytes((S, 256), BF16) + 8 * _nbytes((QB, QB), F32)),
    )(qf, kv, kr)


def mla_bwd(name, qf, kv, kr, do, o, lse):
    B, S, W = qf.shape
    HB = W // 256
    QB = _tile(S, 256, CHUNK)
    nq = S // QB
    scale = (NOPE + ROPE) ** -0.5

    def body(q_ref, kv_ref, kr_ref, do_ref, o_ref, lse_ref, dq_ref, dkv_ref, dkr_ref, k2, kt, dot_, delta, dqt):
        h = pl.program_id(1)
        dkv_ref[...] = jnp.zeros_like(dkv_ref)

        @pl.when(h == 0)
        def _():
            dkr_ref[...] = jnp.zeros_like(dkr_ref)

        _mla_fill_keys(kv_ref, kr_ref, k2)
        for i in range(nq):
            rows = slice(i * QB, (i + 1) * QB)
            kt[i] = _t(k2[rows, :]).astype(BF16)
            dot32 = _t(do_ref[0, rows, :])
            delta[i] = jnp.sum(dot32 * o_ref[0, rows, :].T, axis=0, keepdims=True)
            dot_[i] = dot32.astype(BF16)

        for qi in range(nq):
            rows = slice(qi * QB, (qi + 1) * QB)
            q = q_ref[0, rows, :]
            dob = do_ref[0, rows, :]
            lse_q = lse_ref[0, 0, :, rows]
            delta_q = delta[qi]
            dqt[...] = jnp.zeros_like(dqt)

            def step(kj, carry, q=q, dob=dob, lse_q=lse_q, delta_q=delta_q, qi=qi):
                ks = pl.ds(pl.multiple_of(kj * QB, QB), QB)
                st, mask = _mla_scores_t(k2[ks, :], q, qi * QB, kj * QB, scale)
                pt = jnp.where(mask, jnp.exp(st - lse_q), 0.0)
                dpt = jnp.dot(kv_ref[0, ks, NOPE:], dot_[qi], preferred_element_type=F32)
                dst = (pt * (dpt - delta_q) * scale).astype(BF16)
                dkv_ref[0, ks, NOPE:] += jnp.dot(pt.astype(BF16), dob, preferred_element_type=F32)
                dk2 = jnp.dot(dst, q, preferred_element_type=F32)
                dkv_ref[0, ks, :NOPE] += dk2[:, :NOPE]
                dkr_ref[0, ks, :] += dk2[:, NOPE:]
                dqt[...] += jnp.dot(kt[kj], dst, preferred_element_type=F32)
                return carry

            lax.fori_loop(0, qi + 1, step, 0)
            dq_ref[0, rows, :] = dqt[...].T

    head = lambda w: pl.BlockSpec((1, S, w), lambda b, h: (b, 0, h))
    shared = pl.BlockSpec((1, S, LANE), lambda b, h: (b, 0, 0))
    blk = (2 * _nbytes((S, 256), BF16) + 2 * _nbytes((S, LANE), BF16) + _nbytes((S, LANE), F32)
           + 2 * _nbytes((S, 256), F32) + _nbytes((S, LANE), F32))
    scr = 3 * _nbytes((S, 256), BF16) + 10 * _nbytes((QB, QB), F32)
    return pl.pallas_call(
        body, name=name, grid=(B, HB),
        in_specs=[head(256), head(256), shared, head(LANE), head(LANE),
                  pl.BlockSpec((1, 1, 1, S), lambda b, h: (b, h, 0, 0))],
        out_specs=[head(256), head(256), shared],
        out_shape=[jax.ShapeDtypeStruct((B, S, W), F32), jax.ShapeDtypeStruct((B, S, W), F32),
                   jax.ShapeDtypeStruct((B, S, LANE), F32)],
        scratch_shapes=[pltpu.VMEM((S, 256), BF16), pltpu.VMEM((nq, 256, QB), BF16),
                        pltpu.VMEM((nq, NOPE, QB), BF16), pltpu.VMEM((nq, 1, QB), F32),
                        pltpu.VMEM((256, QB), F32)],
        compiler_params=_params(blk, scr),
    )(qf, kv, kr, do, o, lse)


def cast_bf16(name, w, layer, idx):
    _, R, C = w.shape
    tr = _tile(R, 256, 16)

    def body(k_ref, w_ref, o_ref):
        o_ref[...] = w_ref[...].astype(BF16)

    return pl.pallas_call(
        body, name=name,
        grid_spec=pltpu.PrefetchScalarGridSpec(
            num_scalar_prefetch=1, grid=(R // tr,),
            in_specs=[pl.BlockSpec((None, tr, C), lambda r, k_ref: (layer, r, 0))],
            out_specs=pl.BlockSpec((None, tr, C), lambda r, k_ref: (k_ref[0], r, 0))),
        out_shape=jax.ShapeDtypeStruct((N_CHIPS, R, C), BF16),
    )(idx, w)


def adamw(name, w, g, m, v):
    R, C = w.shape
    tr = _tile(R, max(8, (1 << 18) // C // 8 * 8), 8)
    c1 = 1.0 - ADAM_B1 ** ADAM_STEP
    c2 = 1.0 - ADAM_B2 ** ADAM_STEP

    def body(w_ref, g_ref, m_ref, v_ref, d_ref, mo_ref, vo_ref):
        gv = g_ref[...]
        mn = ADAM_B1 * m_ref[...] + (1.0 - ADAM_B1) * gv
        vn = ADAM_B2 * v_ref[...] + (1.0 - ADAM_B2) * (gv * gv)
        mo_ref[...] = mn
        vo_ref[...] = vn
        d_ref[...] = -ADAM_LR * ((mn / c1) / (jnp.sqrt(vn / c2) + ADAM_EPS) + ADAM_WD * w_ref[...])

    spec = pl.BlockSpec((tr, C), lambda r: (r, 0))
    return pl.pallas_call(
        body, name=name, grid=(R // tr,), in_specs=[spec] * 4, out_specs=[spec] * 3,
        out_shape=[jax.ShapeDtypeStruct((R, C), F32)] * 3,
        compiler_params=_params(7 * _nbytes((tr, C), F32), 4 * _nbytes((tr, C), F32)),
    )(w, g, m, v)


def half_sum(name, dw, landed, idx):
    _, _, hr, C = dw.shape
    tr = _tile(hr, max(16, (1 << 18) // C // 16 * 16), 16)

    def body(i_ref, a_ref, b_ref, o_ref):
        o_ref[...] = (a_ref[...].astype(F32) + b_ref[...].astype(F32)).astype(o_ref.dtype)

    return pl.pallas_call(
        body, name=name,
        grid_spec=pltpu.PrefetchScalarGridSpec(
            num_scalar_prefetch=1, grid=(N_CHIPS, hr // tr),
            in_specs=[pl.BlockSpec((None, None, tr, C), lambda k, r, i_ref: (k, i_ref[1], r, 0)),
                      pl.BlockSpec((None, tr, C), lambda k, r, i_ref: (k, r, 0))],
            out_specs=pl.BlockSpec((None, tr, C), lambda k, r, i_ref: (k, r, 0))),
        out_shape=jax.ShapeDtypeStruct((N_CHIPS, hr, C), BF16),
    )(idx, dw, landed)


def chip_sum(name, part, landed, gbuf, layer, idx):
    _, hr, C = part.shape
    tr = _tile(hr, max(16, (1 << 18) // C // 16 * 16), 16)

    def body(i_ref, a_ref, b_ref, g_ref, o_ref):
        o_ref[...] = ((a_ref[...].astype(F32) + b_ref[0].astype(F32)) + b_ref[1].astype(F32)) + b_ref[2].astype(F32)

    return pl.pallas_call(
        body, name=name,
        grid_spec=pltpu.PrefetchScalarGridSpec(
            num_scalar_prefetch=1, grid=(hr // tr,),
            in_specs=[pl.BlockSpec((None, tr, C), lambda r, i_ref: (i_ref[0], r, 0)),
                      pl.BlockSpec((3, tr, C), lambda r, i_ref: (0, r, 0)),
                      pl.BlockSpec(memory_space=pl.ANY)],
            out_specs=pl.BlockSpec((None, None, tr, C), lambda r, i_ref: (layer, i_ref[1], r, 0))),
        out_shape=jax.ShapeDtypeStruct(gbuf.shape, F32),
        input_output_aliases={3: 0},
    )(idx, part, landed, gbuf)


ANY = pl.BlockSpec(memory_space=pl.ANY)


def _place():
    x, y, c = lax.axis_index("x"), lax.axis_index("y"), lax.axis_index("c")
    chips = [(1 - x, y), (x, 1 - y), (1 - x, 1 - y)]
    return x, y, c, chips


HBM = pl.BlockSpec(memory_space=pltpu.HBM)
SEM = pl.BlockSpec(memory_space=pltpu.SEMAPHORE)
EFFECT = pltpu.SideEffectType.DATAFLOW_SIDE_EFFECTING


def _in_hbm(a):
    return pltpu.with_memory_space_constraint(a, pltpu.HBM)


def _ici_copy(src, dst, send_sems, recv_sems, k, peer):
    return pltpu.make_async_remote_copy(src_ref=src, dst_ref=dst, send_sem=send_sems.at[k], recv_sem=recv_sems.at[k],
                                        device_id=peer, device_id_type=MESH)


def ici_start(name, bufs, lands, after, gather):
    n, nl = len(bufs), len(lands)

    def body(*refs):
        b_in = refs[:n]
        send_sems, recv_sems = refs[n + nl + 1], refs[n + nl + 2]
        b_out = refs[n + nl + 3:2 * n + nl + 3]
        l_out = refs[2 * n + nl + 3:2 * n + 2 * nl + 3]
        token = refs[-1]
        x, y, c, chips = _place()
        kme = 2 * x + y
        for i in range(n):
            for j in range(3):
                peer = (*chips[j], c)
                if gather:
                    _ici_copy(b_out[i].at[kme, c], b_out[i].at[kme, c], send_sems, recv_sems, 3 * i + j, peer).start()
                else:
                    kd = 2 * chips[j][0] + chips[j][1]
                    _ici_copy(b_out[i].at[kd], l_out[i].at[j], send_sems, recv_sems, 3 * i + j, peer).start()
        token[...] = jnp.zeros_like(token)

    arrays = [*bufs, *lands]
    outs = pl.pallas_call(
        body, name=name,
        in_specs=[HBM] * (n + nl) + [ANY],
        out_specs=(SEM, SEM, *[HBM] * (n + nl), pl.BlockSpec(memory_space=pltpu.VMEM)),
        out_shape=(pltpu.SemaphoreType.DMA((3 * n,)), pltpu.SemaphoreType.DMA((3 * n,)),
                   *[pltpu.HBM(a.shape, a.dtype) for a in arrays], jax.ShapeDtypeStruct((8, LANE), F32)),
        input_output_aliases={i: 2 + i for i in range(n + nl)},
        compiler_params=pltpu.CompilerParams(has_side_effects=EFFECT),
    )(*[_in_hbm(a) for a in arrays], after)
    return outs[0], outs[1], list(outs[2:2 + n]), list(outs[2 + n:2 + n + nl]), outs[-1]


def ici_wait(name, send_sems, recv_sems, bufs, lands, after, gather):
    n, nl = len(bufs), len(lands)

    def body(*refs):
        b_in, l_in = refs[:n], refs[n:n + nl]
        send_sems, recv_sems = refs[n + nl], refs[n + nl + 1]
        x, y, c, chips = _place()
        kme = 2 * x + y
        for i in range(n):
            for j in range(3):
                peer = (*chips[j], c)
                kj = 2 * chips[j][0] + chips[j][1]
                if gather:
                    _ici_copy(b_in[i].at[kme, c], b_in[i].at[kme, c], send_sems, recv_sems, 3 * i + j, peer).wait_send()
                    _ici_copy(b_in[i].at[kj, c], b_in[i].at[kj, c], send_sems, recv_sems, 3 * i + j, peer).wait_recv()
                else:
                    _ici_copy(b_in[i].at[kj], l_in[i].at[j], send_sems, recv_sems, 3 * i + j, peer).wait_send()
                    _ici_copy(b_in[i].at[kj], l_in[i].at[j], send_sems, recv_sems, 3 * i + j, peer).wait_recv()

    arrays = [*bufs, *lands]
    outs = pl.pallas_call(
        body, name=name,
        in_specs=[HBM] * (n + nl) + [SEM, SEM, ANY],
        out_specs=tuple([HBM] * (n + nl)),
        out_shape=tuple(pltpu.HBM(a.shape, a.dtype) for a in arrays),
        input_output_aliases={i: i for i in range(n + nl)},
        compiler_params=pltpu.CompilerParams(has_side_effects=EFFECT),
    )(*arrays, send_sems, recv_sems, after)
    return list(outs[:n]), list(outs[n:])


def gather_pair_pass(name, bufs):
    n = len(bufs)

    def body(*refs):
        b = refs[n:2 * n]
        send_sems, recv_sems = refs[2 * n:]
        x, y, c, chips = _place()
        sib = (x, y, 1 - c)

        def d2d(i, j, which):
            kj = 2 * chips[j][0] + chips[j][1]
            return _ici_copy(b[i].at[kj, which], b[i].at[kj, which], send_sems, recv_sems, 3 * i + j, sib)

        for i in range(n):
            for j in range(3):
                d2d(i, j, c).start()
        for i in range(n):
            for j in range(3):
                d2d(i, j, 1 - c).wait_recv()
        for i in range(n):
            for j in range(3):
                d2d(i, j, c).wait_send()

    return pl.pallas_call(
        body, name=name, in_specs=[ANY] * n, out_specs=[ANY] * n,
        out_shape=[jax.ShapeDtypeStruct(a.shape, a.dtype) for a in bufs],
        input_output_aliases={i: i for i in range(n)},
        scratch_shapes=[pltpu.SemaphoreType.DMA((3 * n,)), pltpu.SemaphoreType.DMA((3 * n,))],
    )(*bufs)


def pair_exchange(name, dws):
    n = len(dws)

    def body(*refs):
        ins, outs = refs[:n], refs[n:2 * n]
        send_sems, recv_sems = refs[2 * n:]
        x, y, c, _ = _place()
        copies = []
        for i in range(n):
            copies.append(pltpu.make_async_remote_copy(
                src_ref=ins[i].at[:, 1 - c], dst_ref=outs[i],
                send_sem=send_sems.at[i], recv_sem=recv_sems.at[i],
                device_id=(x, y, 1 - c), device_id_type=MESH))
            copies[i].start()
        for cp in copies:
            cp.wait_recv()
        for cp in copies:
            cp.wait_send()

    return pl.pallas_call(
        body, name=name, in_specs=[ANY] * n, out_specs=[ANY] * n,
        out_shape=[jax.ShapeDtypeStruct((N_CHIPS, *d.shape[2:]), d.dtype) for d in dws],
        scratch_shapes=[pltpu.SemaphoreType.DMA((n,)), pltpu.SemaphoreType.DMA((n,))],
    )(*dws)


def pair_assemble(gbufs):
    n = len(gbufs)

    def body(*refs):
        bufs = refs[n:2 * n]
        send_sems, recv_sems = refs[2 * n:]
        x, y, c, _ = _place()
        copies = []
        for i in range(n):
            copies.append(pltpu.make_async_remote_copy(
                src_ref=bufs[i].at[:, c], dst_ref=bufs[i].at[:, c],
                send_sem=send_sems.at[i], recv_sem=recv_sems.at[i],
                device_id=(x, y, 1 - c), device_id_type=MESH))
            copies[i].start()
        for i in range(n):
            pltpu.make_async_remote_copy(
                src_ref=bufs[i].at[:, 1 - c], dst_ref=bufs[i].at[:, 1 - c],
                send_sem=send_sems.at[i], recv_sem=recv_sems.at[i],
                device_id=(x, y, 1 - c), device_id_type=MESH).wait_recv()
        for cp in copies:
            cp.wait_send()

    return pl.pallas_call(
        body, name="grad_pair_assemble", in_specs=[ANY] * n, out_specs=[ANY] * n,
        out_shape=[jax.ShapeDtypeStruct(g.shape, g.dtype) for g in gbufs],
        input_output_aliases={i: i for i in range(n)},
        scratch_shapes=[pltpu.SemaphoreType.DMA((n,)), pltpu.SemaphoreType.DMA((n,))],
    )(*gbufs)


def all_reduce_small(vec):
    NR = vec.shape[0]
    flips = [(fx, fy, fc) for fx in (0, 1) for fy in (0, 1) for fc in (0, 1)][1:]

    def body(v_ref, o_ref, buf, send_sems, recv_sems):
        x, y, c, _ = _place()
        me = 4 * x + 2 * y + c
        buf[me] = v_ref[...]
        copies = []
        for j, (fx, fy, fc) in enumerate(flips):
            peer = (1 - x if fx else x, 1 - y if fy else y, 1 - c if fc else c)
            copies.append(pltpu.make_async_remote_copy(
                src_ref=v_ref, dst_ref=buf.at[me], send_sem=send_sems.at[j], recv_sem=recv_sems.at[j],
                device_id=peer, device_id_type=MESH))
            copies[j].start()
        for cp in copies:
            cp.wait_recv()
        for cp in copies:
            cp.wait_send()
        acc = buf[0]
        for d in range(1, 8):
            acc = acc + buf[d]
        o_ref[...] = acc

    return pl.pallas_call(
        body, name="all_reduce_small",
        in_specs=[pl.BlockSpec(memory_space=pltpu.VMEM)], out_specs=pl.BlockSpec(memory_space=pltpu.VMEM),
        out_shape=jax.ShapeDtypeStruct((NR, LANE), F32),
        scratch_shapes=[pltpu.VMEM((8, NR, LANE), F32), pltpu.SemaphoreType.DMA((7,)),
                        pltpu.SemaphoreType.DMA((7,))],
    )(vec)


def _pack(arrays):
    flat = jnp.concatenate([a.reshape(-1).astype(F32) for a in arrays])
    n = flat.shape[0]
    npad = -(-n // (8 * LANE)) * (8 * LANE)
    return jnp.pad(flat, (0, npad - n)).reshape(npad // LANE, LANE)


def _unpack(buf, like):
    flat = buf.reshape(-1)
    out, off = [], 0
    for a in like:
        out.append(flat[off:off + a.size].reshape(a.shape))
        off += a.size
    return out


def kernel(x, ffn1_norm, ffn1_w_in, ffn1_w_out, mix_norm, ffn2_norm, ffn2_w_in, ffn2_w_out, a_w_qkv, a_rel_bias, a_w_o, kv_norm, kv_w_down, kv_latent_norm, kv_w_up, b_w_dq, b_q_norm, b_w_uq, b_w_o, final_norm, loss_target, m_ffn1_norm, m_ffn1_w_in, m_ffn1_w_out, m_mix_norm, m_ffn2_norm, m_ffn2_w_in, m_ffn2_w_out, m_a_w_qkv, m_a_rel_bias, m_a_w_o, m_kv_norm, m_kv_w_down, m_kv_latent_norm, m_kv_w_up, m_b_w_dq, m_b_q_norm, m_b_w_uq, m_b_w_o, m_final_norm, v_ffn1_norm, v_ffn1_w_in, v_ffn1_w_out, v_mix_norm, v_ffn2_norm, v_ffn2_w_in, v_ffn2_w_out, v_a_w_qkv, v_a_rel_bias, v_a_w_o, v_kv_norm, v_kv_w_down, v_kv_latent_norm, v_kv_w_up, v_b_w_dq, v_b_q_norm, v_b_w_uq, v_b_w_o, v_final_norm):
    B, S, D = x.shape
    T = B * S
    HB = D // 128
    QL = b_q_norm.shape[-1]
    KVL = kv_latent_norm.shape[0]
    hpc = HB // N_CHIPS
    tabs = rope_tables(S)
    idx = jnp.stack([2 * lax.axis_index("x") + lax.axis_index("y"), lax.axis_index("c")]).astype(I32)

    def halves(a):
        return a.reshape(*a.shape[:-2], 2, a.shape[-2] // 2, a.shape[-1])

    def whole(a):
        return a.reshape(*a.shape[:-3], 2 * a.shape[-2], a.shape[-1])

    kv_w_down_p = jnp.pad(kv_w_down, ((0, 0), (0, LANE - ROPE)))[None]
    b_w_uq_p = jnp.pad(b_w_uq.reshape(1, QL, hpc, NOPE + ROPE),
                       ((0, 0), (0, 0), (0, 0), (0, LANE - ROPE))).reshape(1, QL, hpc * 256)
    sharded = [("ffn1_w_in", ffn1_w_in), ("ffn1_w_out", ffn1_w_out), ("ffn2_w_in", ffn2_w_in),
               ("ffn2_w_out", ffn2_w_out), ("a_w_qkv", a_w_qkv), ("a_w_o", a_w_o),
               ("kv_w_down", kv_w_down_p), ("kv_w_up", kv_w_up[None]), ("b_w_dq", b_w_dq),
               ("b_w_uq", b_w_uq_p), ("b_w_o", b_w_o)]
    pieces = [(a, l) for a, (_, w) in enumerate(sharded) for l in range(w.shape[0])]
    names = [nm for nm, _ in sharded]
    own = {(names[a], l): cast_bf16(f"cast_{names[a]}_{l}", sharded[a][1], l, idx) for a, l in pieces}
    W = {}

    gather_groups = [
        [("ffn1_w_in", 0), ("ffn1_w_out", 0)],
        [("a_w_qkv", 0), ("a_w_o", 0), ("ffn2_w_in", 0), ("ffn2_w_out", 0), ("kv_w_down", 0), ("kv_w_up", 0)],
        [("ffn1_w_in", 1), ("ffn1_w_out", 1), ("b_w_dq", 0), ("b_w_uq", 0), ("b_w_o", 0), ("ffn2_w_in", 1),
         ("ffn2_w_out", 1)]]

    def gather_start(g, after):
        keys = gather_groups[g]
        ss, rs, bufs, _, token = ici_start(f"gather_start_{g}", [halves(own[k]) for k in keys], [], after, True)
        return (g, ss, rs, bufs), token

    def gather_finish(state, after):
        g, ss, rs, bufs = state
        bufs, _ = ici_wait(f"gather_wait_{g}", ss, rs, bufs, [], after, True)
        full = gather_pair_pass(f"gather_pair_{g}", bufs)
        for k, w in zip(gather_groups[g], full):
            W[k] = whole(w)
        return full[0]

    def tied(a, token):
        return a + token[0, 0]

    def col(nm, l=0):
        return W[(nm, l)]

    def row(nm, l=0):
        w = W[(nm, l)]
        return w.reshape(N_CHIPS * w.shape[1], w.shape[2])

    bias = rel_bias_tile("rel_bias_tile", a_rel_bias[0])

    def ffn_fwd(tag, h, g, w_in, w_out):
        xn = rms_fwd(f"{tag}_norm", h, g)
        u = mm_colw(f"{tag}_in", xn, w_in, F32)
        act = swiglu_fwd(f"{tag}_act", u)
        return mm_roww(f"{tag}_out", act, w_out, F32, res=h, alpha=0.5), (xn, u, act)

    h0 = x.reshape(T, D)
    st0, tok0 = gather_start(0, h0)
    done0 = gather_finish(st0, tok0)
    st1, tok1 = gather_start(1, done0)
    h1, sv_f1a = ffn_fwd("l0f1", h0, tied(ffn1_norm[0], tok1), col("ffn1_w_in", 0), row("ffn1_w_out", 0))
    done1 = gather_finish(st1, h1)
    st2, tok2 = gather_start(2, done1)
    hn_a = rms_fwd("l0mix_norm", h1, tied(mix_norm[0], tok2))
    qkv = mm_colw("l0_qkv", hn_a, col("a_w_qkv"), BF16).reshape(B, S, 3 * D)
    o_a = attn_a_fwd("l0_attn", qkv, bias).reshape(T, D)
    h2 = mm_roww("l0_attn_out", o_a, row("a_w_o"), F32, res=h1)
    h3, sv_f2a = ffn_fwd("l0f2", h2, ffn2_norm[0], col("ffn2_w_in", 0), row("ffn2_w_out", 0))

    hkv = rms_fwd("kv_norm", h3, kv_norm)
    ckr = mm_roww("kv_down", hkv, row("kv_w_down"), F32)
    ckv, kr = kvprep_fwd("kv_prep", ckr, kv_latent_norm, tabs, B, S)
    kvb = mm_colw("kv_up", ckv, col("kv_w_up"), BF16).reshape(B, S, HB * 256)
    gather_finish(st2, kvb)

    h4, sv_f1b = ffn_fwd("l1f1", h3, ffn1_norm[1], col("ffn1_w_in", 1), row("ffn1_w_out", 1))
    hn_b = rms_fwd("l1mix_norm", h4, mix_norm[1])
    cqp = mm_roww("l1_dq", hn_b, row("b_w_dq"), F32)
    cq = rms_fwd("l1_q_norm", cqp, b_q_norm[0])
    qpre = mm_colw("l1_uq", cq, col("b_w_uq"), F32)
    qf = qprep("l1_q_rope", qpre, tabs, B, S, bwd=False).reshape(B, S, HB * 256)
    o_b, lse = mla_fwd("l1_attn", qf, kvb, kr)
    h5 = mm_roww("l1_attn_out", o_b.reshape(T, HB * LANE), row("b_w_o"), F32, res=h4)
    h6, sv_f2b = ffn_fwd("l1f2", h5, ffn2_norm[1], col("ffn2_w_in", 1), row("ffn2_w_out", 1))

    dh, g_final, loss_part = loss_head("loss_head", h6, final_norm, loss_target.reshape(T, D))

    gw = {}
    gbufs = {nm: lax.empty(halves(w).shape, F32) for nm, w in sharded}

    def reduce_start(r, keys, after):
        dws = [halves(gw[k]) for k in keys]
        landed = pair_exchange(f"grad_pair_exchange_{r}", dws)
        parts = [half_sum(f"half_sum_{r}_{i}", dws[i], landed[i], idx) for i in range(len(keys))]
        lands = [lax.empty((3, *p.shape[1:]), p.dtype) for p in parts]
        ss, rs, parts, lands, token = ici_start(f"reduce_start_{r}", parts, lands, after, False)
        return (r, keys, ss, rs, parts, lands), token

    def reduce_finish(state, after):
        r, keys, ss, rs, parts, lands = state
        parts, lands = ici_wait(f"reduce_wait_{r}", ss, rs, parts, lands, after, False)
        for i, (nm, l) in enumerate(keys):
            gbufs[nm] = chip_sum(f"chip_sum_{r}_{i}", parts[i], lands[i], gbufs[nm], l, idx)
        return gbufs[keys[0][0]]

    def ffn_bwd(tag, dh, h_in, g, w_in, w_out, saved, key_in, key_out, after=None):
        xn, u, act = saved
        dact = mm_roww_t(f"{tag}_dact", dh, w_out, F32, alpha=0.5, after=after)
        dwo = mm_droww(f"{tag}_dwout", act, dh, alpha=0.5)
        gw[key_out] = dwo.reshape(N_CHIPS, dwo.shape[0] // N_CHIPS, dwo.shape[1])
        du = swiglu_bwd(f"{tag}_dswiglu", u, dact)
        gw[key_in] = mm_dcolw(f"{tag}_dwin", xn, du)
        dxn = mm_colw_t(f"{tag}_dxn", du, w_in, F32)
        return rms_bwd(f"{tag}_dnorm", h_in, g, dxn, dres=dh)

    def chip_major(dw):
        return dw.reshape(N_CHIPS, dw.shape[0] // N_CHIPS, dw.shape[1])

    dh, g_f2b = ffn_bwd("l1f2b", dh, h5, ffn2_norm[1], col("ffn2_w_in", 1), row("ffn2_w_out", 1), sv_f2b,
                        ("ffn2_w_in", 1), ("ffn2_w_out", 1))
    red0, rtok0 = reduce_start(0, [("ffn2_w_in", 1), ("ffn2_w_out", 1)], dh)
    do_b = mm_roww_t("l1_attn_do", dh, row("b_w_o"), BF16, after=rtok0).reshape(B, S, HB * LANE)
    gw[("b_w_o", 0)] = chip_major(mm_droww("l1_attn_dwo", o_b.reshape(T, HB * LANE), dh))
    dqf, dkv, dkr = mla_bwd("l1_attn_bwd", qf, kvb, kr, do_b, o_b, lse)
    dqpre = qprep("l1_q_rope_bwd", dqf.reshape(T, HB * 256), tabs, B, S, bwd=True)
    gw[("b_w_uq", 0)] = mm_dcolw("l1_dwuq", cq, dqpre)
    dcq = mm_colw_t("l1_dcq", dqpre, col("b_w_uq"), F32)
    dcqp, g_qn = rms_bwd("l1_dq_norm", cqp, b_q_norm[0], dcq)
    gw[("b_w_dq", 0)] = chip_major(mm_droww("l1_dwdq", hn_b, dcqp))
    dhn = mm_roww_t("l1_dhn", dcqp, row("b_w_dq"), F32)
    dh, g_mixb = rms_bwd("l1_dmix", h4, mix_norm[1], dhn, dres=dh)
    dh, g_f1b = ffn_bwd("l1f1b", dh, h3, ffn1_norm[1], col("ffn1_w_in", 1), row("ffn1_w_out", 1), sv_f1b,
                        ("ffn1_w_in", 1), ("ffn1_w_out", 1))
    fin0 = reduce_finish(red0, dh)
    red1, rtok1 = reduce_start(1, [("b_w_o", 0), ("b_w_uq", 0), ("b_w_dq", 0), ("ffn1_w_in", 1), ("ffn1_w_out", 1)], fin0)
    dkv2 = dkv.reshape(T, HB * 256)
    gw[("kv_w_up", 0)] = mm_dcolw("kv_dwup", ckv, dkv2, after=rtok1)
    dckv = mm_colw_t("kv_dckv", dkv2, col("kv_w_up"), F32)
    dckr, g_lat = kvprep_bwd("kv_prep_bwd", ckr, kv_latent_norm, dckv, dkr, tabs, B, S)
    gw[("kv_w_down", 0)] = chip_major(mm_droww("kv_dwdown", hkv, dckr))
    dhkv = mm_roww_t("kv_dhkv", dckr, row("kv_w_down"), F32)
    dh, g_kvn = rms_bwd("kv_dnorm", h3, kv_norm, dhkv, dres=dh)
    dh, g_f2a = ffn_bwd("l0f2b", dh, h2, ffn2_norm[0], col("ffn2_w_in", 0), row("ffn2_w_out", 0), sv_f2a,
                        ("ffn2_w_in", 0), ("ffn2_w_out", 0))
    do_a = mm_roww_t("l0_attn_do", dh, row("a_w_o"), BF16).reshape(B, S, D)
    gw[("a_w_o", 0)] = chip_major(mm_droww("l0_attn_dwo", o_a, dh))
    dq_a, dkv_a, dbias = attn_a_bwd("l0_attn_bwd", qkv, do_a, bias)
    dqkv = jnp.concatenate([dq_a.reshape(T, D), dkv_a.reshape(T, 2 * D)], axis=1)
    gw[("a_w_qkv", 0)] = mm_dcolw("l0_dwqkv", hn_a, dqkv)
    dhn = mm_colw_t("l0_dhn", dqkv, col("a_w_qkv"), F32)
    dh, g_mixa = rms_bwd("l0_dmix", h1, mix_norm[0], dhn, dres=dh)
    fin1 = reduce_finish(red1, dh)
    red2, rtok2 = reduce_start(2, [("kv_w_up", 0), ("kv_w_down", 0), ("ffn2_w_in", 0), ("ffn2_w_out", 0),
                                   ("a_w_o", 0), ("a_w_qkv", 0)], fin1)
    dh, g_f1a = ffn_bwd("l0f1b", dh, h0, ffn1_norm[0], col("ffn1_w_in", 0), row("ffn1_w_out", 0), sv_f1a,
                        ("ffn1_w_in", 0), ("ffn1_w_out", 0), after=rtok2)
    grad_x = dh.reshape(B, S, D)
    g_rel = rel_bias_grad("rel_bias_grad", dbias)[:, :2 * MAX_REL + 1][None]
    fin2 = reduce_finish(red2, dh)
    red3, rtok3 = reduce_start(3, [("ffn1_w_in", 0), ("ffn1_w_out", 0)], fin2)
    reduce_finish(red3, rtok3)

    full = [whole(g) for g in pair_assemble([gbufs[nm] for nm in names])]
    G = {nm: g for (nm, _), g in zip(sharded, full)}
    G["kv_w_down"] = G["kv_w_down"][0, :, :KVL + ROPE]
    G["kv_w_up"] = G["kv_w_up"][0]
    G["b_w_uq"] = G["b_w_uq"].reshape(1, QL, hpc, 256)[..., :NOPE + ROPE].reshape(b_w_uq.shape)

    small = [("ffn1_norm", jnp.stack([g_f1a, g_f1b])), ("mix_norm", jnp.stack([g_mixa, g_mixb])),
             ("ffn2_norm", jnp.stack([g_f2a, g_f2b])), ("a_rel_bias", g_rel), ("kv_norm", g_kvn),
             ("kv_latent_norm", g_lat), ("b_q_norm", g_qn[None]), ("final_norm", g_final)]
    red = all_reduce_small(_pack([loss_part] + [g for _, g in small]))
    unpacked = _unpack(red, [loss_part] + [g for _, g in small])
    loss = unpacked[0][0, 0]
    for (nm, _), g in zip(small, unpacked[1:]):
        G[nm] = g

    given = dict(ffn1_norm=(ffn1_norm, m_ffn1_norm, v_ffn1_norm), ffn1_w_in=(ffn1_w_in, m_ffn1_w_in, v_ffn1_w_in),
                 ffn1_w_out=(ffn1_w_out, m_ffn1_w_out, v_ffn1_w_out), mix_norm=(mix_norm, m_mix_norm, v_mix_norm),
                 ffn2_norm=(ffn2_norm, m_ffn2_norm, v_ffn2_norm), ffn2_w_in=(ffn2_w_in, m_ffn2_w_in, v_ffn2_w_in),
                 ffn2_w_out=(ffn2_w_out, m_ffn2_w_out, v_ffn2_w_out), a_w_qkv=(a_w_qkv, m_a_w_qkv, v_a_w_qkv),
                 a_rel_bias=(a_rel_bias, m_a_rel_bias, v_a_rel_bias), a_w_o=(a_w_o, m_a_w_o, v_a_w_o),
                 kv_norm=(kv_norm, m_kv_norm, v_kv_norm), kv_w_down=(kv_w_down, m_kv_w_down, v_kv_w_down),
                 kv_latent_norm=(kv_latent_norm, m_kv_latent_norm, v_kv_latent_norm),
                 kv_w_up=(kv_w_up, m_kv_w_up, v_kv_w_up), b_w_dq=(b_w_dq, m_b_w_dq, v_b_w_dq),
                 b_q_norm=(b_q_norm, m_b_q_norm, v_b_q_norm), b_w_uq=(b_w_uq, m_b_w_uq, v_b_w_uq),
                 b_w_o=(b_w_o, m_b_w_o, v_b_w_o), final_norm=(final_norm, m_final_norm, v_final_norm))
    order = list(given)
    delta, new_m, new_v = {}, {}, {}
    small_names = [nm for nm, _ in small]
    packed = [_pack([given[nm][k] for nm in small_names]) for k in range(3)]
    outs = adamw("adamw_small", packed[0], _pack([G[nm] for nm in small_names]), packed[1], packed[2])
    for dst, buf in zip((delta, new_m, new_v), outs):
        for nm, a in zip(small_names, _unpack(buf, [given[nm][0] for nm in small_names])):
            dst[nm] = a
    for nm, _ in sharded:
        w, m, v = given[nm]
        g = G[nm].reshape(w.shape)
        G[nm] = g
        two = lambda a: a.reshape(-1, a.shape[-1])
        d_, m_, v_ = adamw(f"adamw_{nm}", two(w), two(g), two(m), two(v))
        delta[nm], new_m[nm], new_v[nm] = d_.reshape(w.shape), m_.reshape(w.shape), v_.reshape(w.shape)

    return (loss, grad_x, *[G[n] for n in order], *[delta[n] for n in order],
            *[new_m[n] for n in order], *[new_v[n] for n in order])
```

```python
import functools
import math

import jax
import jax.numpy as jnp
from jax import lax
from jax.experimental import pallas as pl
from jax.experimental.pallas import tpu as pltpu

F32 = jnp.float32
BF16 = jnp.bfloat16
I32 = jnp.int32

CHUNK = 64
CHUNK_SHIFT = 6
HEAD_DIM_A = 64
LEFT_CHUNKS = 8
MAX_REL = 128
REL_PAD = 384
QROWS = 2 * CHUNK
WIN = (LEFT_CHUNKS + 2) * CHUNK
PADR = LEFT_CHUNKS * CHUNK
NOPE = 128
ROPE = 64
EPS = 1e-6
NEG_INF = -1e30
ROPE_THETA = 10000.0
ADAM_LR, ADAM_B1, ADAM_B2, ADAM_EPS, ADAM_WD, ADAM_STEP = 0.001, 0.9, 0.999, 1e-08, 0.01, 10
N_CHIPS = 4
LANE = 128
MESH = pl.DeviceIdType.MESH
VMEM_CAP_MB = 60

NN = (((1,), (0,)), ((), ()))
NT = (((1,), (1,)), ((), ()))
TN = (((0,), (0,)), ((), ()))


def _tile(n, pref, mult):
    t = (min(pref, n) // mult) * mult
    while t >= mult:
        if n % t == 0:
            return t
        t -= mult
    return n


def _nbytes(shape, dtype):
    return math.prod(shape) * jnp.dtype(dtype).itemsize


def _params(block_bytes, extra_bytes=0):
    need = 2 * block_bytes + extra_bytes
    mb = min(VMEM_CAP_MB, max(32, int(need * 1.25 / 2**20) + 8))
    return pltpu.CompilerParams(vmem_limit_bytes=mb * 2**20)


def _mm(name, kind, a, b, grid, a_spec, b_spec, o_spec, out_shape, out_dtype, blocks,
        red_axis=None, nred=1, alpha=1.0, res=None, res_spec=None, after=None):
    dims = {"nn": NN, "nt": NT, "tn": TN}[kind]
    has_res = res is not None
    acc_in_out = nred > 1 and out_dtype == F32 and not has_res and alpha == 1.0
    n_in = 2 + has_res + (after is not None)

    def body(*refs):
        a_ref, b_ref = refs[0], refs[1]
        r_ref = refs[2] if has_res else None
        o_ref = refs[n_in]
        p = lax.dot_general(a_ref[...].astype(BF16), b_ref[...].astype(BF16), dims,
                            preferred_element_type=F32)

        def finish(acc):
            y = acc if alpha == 1.0 else acc * alpha
            if has_res:
                y = r_ref[...] + y
            o_ref[...] = y.astype(o_ref.dtype)

        if nred == 1:
            finish(p)
            return
        k = pl.program_id(red_axis)
        tgt = o_ref if acc_in_out else refs[-1]

        @pl.when(k == 0)
        def _():
            tgt[...] = p

        @pl.when(k > 0)
        def _():
            tgt[...] += p

        if not acc_in_out:
            @pl.when(k == nred - 1)
            def _():
                finish(tgt[...])

    a_blk, b_blk, o_blk = blocks
    scratch = []
    extra = 0
    if nred > 1 and not acc_in_out:
        scratch = [pltpu.VMEM(o_blk, F32)]
        extra = _nbytes(o_blk, F32)
    blk = _nbytes(a_blk, a.dtype) + _nbytes(b_blk, b.dtype) + _nbytes(o_blk, out_dtype)
    ins, specs = [a, b], [a_spec, b_spec]
    if has_res:
        ins.append(res)
        specs.append(res_spec)
        blk += _nbytes(o_blk, res.dtype)
    if after is not None:
        ins.append(after)
        specs.append(pl.BlockSpec(memory_space=pl.ANY))
    extra += _nbytes(a_blk, BF16) + _nbytes(b_blk, BF16) + 2 * _nbytes(o_blk, F32)
    return pl.pallas_call(
        body, name=name, grid=grid, in_specs=specs, out_specs=o_spec,
        out_shape=jax.ShapeDtypeStruct(out_shape, out_dtype), scratch_shapes=scratch,
        compiler_params=_params(blk, extra),
    )(*ins)


def mm_colw(name, x, w3, out_dtype):
    T, K = x.shape
    _, _, nl = w3.shape
    tm = _tile(T, 512, 8)
    return _mm(name, "nn", x, w3, (N_CHIPS, T // tm),
               pl.BlockSpec((tm, K), lambda j, i: (i, 0)),
               pl.BlockSpec((None, K, nl), lambda j, i: (j, 0, 0)),
               pl.BlockSpec((tm, nl), lambda j, i: (i, j)),
               (T, N_CHIPS * nl), out_dtype, ((tm, K), (K, nl), (tm, nl)))


def _pair_chip(j):
    return (j % 2) * 2 + j // 2


def mm_colw_t(name, dy, w3, out_dtype, res=None, after=None, pair_layout=False):
    T = dy.shape[0]
    _, K, nl = w3.shape
    tm = _tile(T, 512, 8)
    chip = _pair_chip if pair_layout else (lambda j: j)
    return _mm(name, "nt", dy, w3, (T // tm, N_CHIPS),
               pl.BlockSpec((tm, nl), lambda i, j: (i, j)),
               pl.BlockSpec((None, K, nl), lambda i, j: (chip(j), 0, 0)),
               pl.BlockSpec((tm, K), lambda i, j: (i, 0)),
               (T, K), out_dtype, ((tm, nl), (K, nl), (tm, K)),
               red_axis=1, nred=N_CHIPS, res=res,
               res_spec=pl.BlockSpec((tm, K), lambda i, j: (i, 0)), after=after)


def mm_dcolw(name, x, dy, after=None, pair_layout=False):
    T, K = x.shape
    nl = dy.shape[1] // N_CHIPS
    tt = _tile(T, 512, 8)
    chip = _pair_chip if pair_layout else (lambda j: j)
    return _mm(name, "tn", x, dy, (N_CHIPS, T // tt),
               pl.BlockSpec((tt, K), lambda j, t: (t, 0)),
               pl.BlockSpec((tt, nl), lambda j, t: (t, j)),
               pl.BlockSpec((None, K, nl), lambda j, t: (chip(j), 0, 0)),
               (N_CHIPS, K, nl), BF16, ((tt, K), (tt, nl), (K, nl)),
               red_axis=1, nred=T // tt, after=after)


def mm_roww(name, x, w2, out_dtype, res=None, alpha=1.0):
    T, Kt = x.shape
    N = w2.shape[1]
    tm = _tile(T, 512, 8)
    return _mm(name, "nn", x, w2, (T // tm,),
               pl.BlockSpec((tm, Kt), lambda i: (i, 0)),
               pl.BlockSpec((Kt, N), lambda i: (0, 0)),
               pl.BlockSpec((tm, N), lambda i: (i, 0)),
               (T, N), out_dtype, ((tm, Kt), (Kt, N), (tm, N)),
               alpha=alpha, res=res, res_spec=pl.BlockSpec((tm, N), lambda i: (i, 0)))


def mm_roww_t(name, dy, w2, out_dtype, alpha=1.0, after=None):
    T, N = dy.shape
    Kt = w2.shape[0]
    tm = _tile(T, 512, 8)
    tk = _tile(Kt, 1408, LANE)
    return _mm(name, "nt", dy, w2, (Kt // tk, T // tm),
               pl.BlockSpec((tm, N), lambda j, i: (i, 0)),
               pl.BlockSpec((tk, N), lambda j, i: (j, 0)),
               pl.BlockSpec((tm, tk), lambda j, i: (i, j)),
               (T, Kt), out_dtype, ((tm, N), (tk, N), (tm, tk)), alpha=alpha, after=after)


def mm_droww(name, x, dy, alpha=1.0):
    T, Kt = x.shape
    N = dy.shape[1]
    tt = _tile(T, 512, 8)
    tk = _tile(Kt, 1408, LANE)
    return _mm(name, "tn", x, dy, (Kt // tk, T // tt),
               pl.BlockSpec((tt, tk), lambda j, t: (t, j)),
               pl.BlockSpec((tt, N), lambda j, t: (t, 0)),
               pl.BlockSpec((tk, N), lambda j, t: (j, 0)),
               (Kt, N), BF16, ((tt, tk), (tt, N), (tk, N)),
               red_axis=1, nred=T // tt, alpha=alpha)


def rms_fwd(name, x, g):
    T, D = x.shape
    tm = _tile(T, 512, 8)

    def body(x_ref, g_ref, o_ref):
        xv = x_ref[...]
        r = lax.rsqrt(jnp.mean(xv * xv, axis=-1, keepdims=True) + EPS)
        o_ref[...] = (xv * r * g_ref[...]).astype(o_ref.dtype)

    return pl.pallas_call(
        body, name=name, grid=(T // tm,),
        in_specs=[pl.BlockSpec((tm, D), lambda i: (i, 0)), pl.BlockSpec((1, D), lambda i: (0, 0))],
        out_specs=pl.BlockSpec((tm, D), lambda i: (i, 0)),
        out_shape=jax.ShapeDtypeStruct((T, D), BF16),
        compiler_params=_params(_nbytes((tm, D), F32) * 2, 4 * _nbytes((tm, D), F32)),
    )(x, g.reshape(1, D))


def _rms_bwd_math(xv, gv, dy):
    r = lax.rsqrt(jnp.mean(xv * xv, axis=-1, keepdims=True) + EPS)
    xh = xv * r
    dyg = dy * gv
    dx = r * (dyg - xh * jnp.mean(dyg * xh, axis=-1, keepdims=True))
    dg = jnp.sum(dy * xh, axis=0, keepdims=True)
    return dx, dg


def rms_bwd(name, x, g, dy, dres=None):
    T, D = x.shape
    tm = _tile(T, 256, 8)
    has_res = dres is not None

    def body(*refs):
        x_ref, g_ref, dy_ref = refs[:3]
        r_ref = refs[3] if has_res else None
        dx_ref, dg_ref = refs[-2:]
        dx, dg = _rms_bwd_math(x_ref[...], g_ref[...], dy_ref[...].astype(F32))
        if has_res:
            dx = r_ref[...] + dx
        dx_ref[...] = dx

        @pl.when(pl.program_id(0) == 0)
        def _():
            dg_ref[...] = dg

        @pl.when(pl.program_id(0) > 0)
        def _():
            dg_ref[...] += dg

    row = pl.BlockSpec((tm, D), lambda i: (i, 0))
    vec = pl.BlockSpec((1, D), lambda i: (0, 0))
    ins, specs = [x, g.reshape(1, D), dy], [row, vec, row]
    if has_res:
        ins.append(dres)
        specs.append(row)
    dx, dg = pl.pallas_call(
        body, name=name, grid=(T // tm,), in_specs=specs, out_specs=[row, vec],
        out_shape=[jax.ShapeDtypeStruct((T, D), F32), jax.ShapeDtypeStruct((1, D), F32)],
        compiler_params=_params(_nbytes((tm, D), F32) * 4, 6 * _nbytes((tm, D), F32)),
    )(*ins)
    return dx, dg.reshape(D)


def ffn_in_act(name, x, w3):
    T, K = x.shape
    _, _, nl = w3.shape
    tm = _tile(T, 512, 8)

    def body(x_ref, wg_ref, wu_ref, u_ref, a_ref):
        xv = x_ref[...]
        g = jnp.dot(xv, wg_ref[...], preferred_element_type=F32)
        up = jnp.dot(xv, wu_ref[...], preferred_element_type=F32)
        u_ref[:, :nl] = g.astype(u_ref.dtype)
        u_ref[:, nl:] = up.astype(u_ref.dtype)
        a_ref[...] = (g * jax.nn.sigmoid(g) * up).astype(a_ref.dtype)

    blk = _nbytes((tm, K), BF16) + 2 * _nbytes((K, nl), BF16) + _nbytes((tm, 3 * nl), BF16)
    return pl.pallas_call(
        body, name=name, grid=(2, T // tm),
        in_specs=[pl.BlockSpec((tm, K), lambda p, i: (i, 0)),
                  pl.BlockSpec((None, K, nl), lambda p, i: (p, 0, 0)),
                  pl.BlockSpec((None, K, nl), lambda p, i: (p + 2, 0, 0))],
        out_specs=[pl.BlockSpec((tm, 2 * nl), lambda p, i: (i, p)), pl.BlockSpec((tm, nl), lambda p, i: (i, p))],
        out_shape=[jax.ShapeDtypeStruct((T, 4 * nl), BF16), jax.ShapeDtypeStruct((T, 2 * nl), BF16)],
        compiler_params=_params(blk, 4 * _nbytes((tm, nl), F32)),
    )(x, w3, w3)


def ffn_dact(name, dh, w_out, u, after=None):
    T, N = dh.shape
    F = w_out.shape[0]
    nl = F // 2
    tm = _tile(T, 512, 8)

    def body(*refs):
        d_ref, w_ref, u_ref = refs[:3]
        o_ref = refs[-1]
        dact = 0.5 * lax.dot_general(d_ref[...].astype(BF16), w_ref[...], NT, preferred_element_type=F32)
        g = u_ref[:, :nl].astype(F32)
        up = u_ref[:, nl:].astype(F32)
        sig = jax.nn.sigmoid(g)
        o_ref[:, :nl] = (dact * up * (sig * (1.0 + g * (1.0 - sig)))).astype(o_ref.dtype)
        o_ref[:, nl:] = (dact * (g * sig)).astype(o_ref.dtype)

    ins = [dh, w_out, u]
    specs = [pl.BlockSpec((tm, N), lambda p, i: (i, 0)), pl.BlockSpec((nl, N), lambda p, i: (p, 0)),
             pl.BlockSpec((tm, 2 * nl), lambda p, i: (i, p))]
    if after is not None:
        ins.append(after)
        specs.append(pl.BlockSpec(memory_space=pl.ANY))
    blk = _nbytes((tm, N), F32) + _nbytes((nl, N), BF16) + 2 * _nbytes((tm, 2 * nl), BF16)
    return pl.pallas_call(
        body, name=name, grid=(2, T // tm), in_specs=specs,
        out_specs=pl.BlockSpec((tm, 2 * nl), lambda p, i: (i, p)),
        out_shape=jax.ShapeDtypeStruct((T, 2 * F), BF16),
        compiler_params=_params(blk, 6 * _nbytes((tm, nl), F32)),
    )(*ins)


def loss_head(name, h, g, target):
    T, D = h.shape
    tm = _tile(T, 256, 8)

    def body(h_ref, g_ref, t_ref, dh_ref, dg_ref, loss_ref):
        xv = h_ref[...]
        gv = g_ref[...]
        r = lax.rsqrt(jnp.mean(xv * xv, axis=-1, keepdims=True) + EPS)
        err = xv * r * gv - t_ref[...]
        part = 0.5 * jnp.sum(jnp.mean(err * err, axis=-1, keepdims=True), axis=0, keepdims=True)
        dx, dg = _rms_bwd_math(xv, gv, err * (1.0 / D))
        dh_ref[...] = dx
        part = jnp.broadcast_to(part, (1, LANE))

        @pl.when(pl.program_id(0) == 0)
        def _():
            dg_ref[...] = dg
            loss_ref[...] = part

        @pl.when(pl.program_id(0) > 0)
        def _():
            dg_ref[...] += dg
            loss_ref[...] += part

    row = pl.BlockSpec((tm, D), lambda i: (i, 0))
    vec = pl.BlockSpec((1, D), lambda i: (0, 0))
    dh, dg, loss = pl.pallas_call(
        body, name=name, grid=(T // tm,), in_specs=[row, vec, row],
        out_specs=[row, vec, pl.BlockSpec((1, LANE), lambda i: (0, 0))],
        out_shape=[jax.ShapeDtypeStruct((T, D), F32), jax.ShapeDtypeStruct((1, D), F32),
                   jax.ShapeDtypeStruct((1, LANE), F32)],
        compiler_params=_params(_nbytes((tm, D), F32) * 3, 6 * _nbytes((tm, D), F32)),
    )(h, g.reshape(1, D), target)
    return dh, dg.reshape(D), loss


def rope_tables(S):
    half = ROPE // 2
    freqs = ROPE_THETA ** (-jnp.arange(half, dtype=F32) / half)
    ang = jnp.arange(S, dtype=F32)[:, None] * freqs[None, :]
    cos, sin = jnp.cos(ang), jnp.sin(ang)
    z = jnp.zeros_like(cos)
    ct = jnp.concatenate([cos, cos, z, z], axis=1)
    s1 = jnp.concatenate([-sin, z, z, z], axis=1)
    s2 = jnp.concatenate([z, sin, z, z], axis=1)
    return ct, s1, s2


def _rope_tile(t, ct, s1, s2):
    return t * ct + pltpu.roll(t, 96, 1) * s1 + pltpu.roll(t, 32, 1) * s2


def _rope_tile_bwd(d, ct, s1, s2):
    return d * ct + pltpu.roll(d * s1, 32, 1) + pltpu.roll(d * s2, 96, 1)


def qprep(name, q, tabs, B, S, bwd):
    T, W = q.shape
    nh = W // 256
    ts = _tile(S, 256, 8)
    fn = _rope_tile_bwd if bwd else _rope_tile

    def body(q_ref, ct_ref, s1_ref, s2_ref, o_ref):
        ct, s1, s2 = ct_ref[...], s1_ref[...], s2_ref[...]
        for h in range(nh):
            o_ref[0, :, 256 * h:256 * h + 128] = q_ref[0, :, 256 * h:256 * h + 128].astype(o_ref.dtype)
            t = q_ref[0, :, 256 * h + 128:256 * h + 256].astype(F32)
            o_ref[0, :, 256 * h + 128:256 * h + 256] = fn(t, ct, s1, s2).astype(o_ref.dtype)

    row = pl.BlockSpec((1, ts, W), lambda b, s: (b, s, 0))
    tab = pl.BlockSpec((ts, LANE), lambda b, s: (s, 0))
    out = pl.pallas_call(
        body, name=name, grid=(B, S // ts), in_specs=[row, tab, tab, tab], out_specs=row,
        out_shape=jax.ShapeDtypeStruct((B, S, W), BF16),
        compiler_params=_params(_nbytes((ts, W), F32) * 2, _nbytes((ts, W), F32) * 2),
    )(q.reshape(B, S, W), *tabs)
    return out.reshape(T, W)


def kvprep_fwd(name, ckr, g, tabs, B, S):
    T, W = ckr.shape
    KVL = W - LANE
    ts = _tile(S, 256, 8)

    def body(x_ref, g_ref, ct_ref, s1_ref, s2_ref, c_ref, k_ref):
        xv = x_ref[0, :, :KVL]
        r = lax.rsqrt(jnp.mean(xv * xv, axis=-1, keepdims=True) + EPS)
        c_ref[0] = (xv * r * g_ref[...]).astype(c_ref.dtype)
        k_ref[0] = _rope_tile(x_ref[0, :, KVL:], ct_ref[...], s1_ref[...], s2_ref[...]).astype(k_ref.dtype)

    tab = pl.BlockSpec((ts, LANE), lambda b, s: (s, 0))
    c, k = pl.pallas_call(
        body, name=name, grid=(B, S // ts),
        in_specs=[pl.BlockSpec((1, ts, W), lambda b, s: (b, s, 0)), pl.BlockSpec((1, KVL), lambda b, s: (0, 0)),
                  tab, tab, tab],
        out_specs=[pl.BlockSpec((1, ts, KVL), lambda b, s: (b, s, 0)),
                   pl.BlockSpec((1, ts, LANE), lambda b, s: (b, s, 0))],
        out_shape=[jax.ShapeDtypeStruct((B, S, KVL), BF16), jax.ShapeDtypeStruct((B, S, LANE), BF16)],
        compiler_params=_params(_nbytes((ts, W), F32) * 2, _nbytes((ts, W), F32) * 2),
    )(ckr.reshape(B, S, W), g.reshape(1, KVL), *tabs)
    return c.reshape(T, KVL), k


def kvprep_bwd(name, ckr, g, dc, dkr, tabs, B, S):
    T, W = ckr.shape
    KVL = W - LANE
    ts = _tile(S, 256, 8)

    def body(x_ref, g_ref, dc_ref, dk_ref, ct_ref, s1_ref, s2_ref, o_ref, dg_ref):
        dx, dg = _rms_bwd_math(x_ref[0, :, :KVL], g_ref[...], dc_ref[0])
        o_ref[0, :, :KVL] = dx
        o_ref[0, :, KVL:] = _rope_tile_bwd(dk_ref[0], ct_ref[...], s1_ref[...], s2_ref[...])
        first = (pl.program_id(0) == 0) & (pl.program_id(1) == 0)

        @pl.when(first)
        def _():
            dg_ref[...] = dg

        @pl.when(jnp.logical_not(first))
        def _():
            dg_ref[...] += dg

    tab = pl.BlockSpec((ts, LANE), lambda b, s: (s, 0))
    vec = pl.BlockSpec((1, KVL), lambda b, s: (0, 0))
    o, dg = pl.pallas_call(
        body, name=name, grid=(B, S // ts),
        in_specs=[pl.BlockSpec((1, ts, W), lambda b, s: (b, s, 0)), vec,
                  pl.BlockSpec((1, ts, KVL), lambda b, s: (b, s, 0)),
                  pl.BlockSpec((1, ts, LANE), lambda b, s: (b, s, 0)), tab, tab, tab],
        out_specs=[pl.BlockSpec((1, ts, W), lambda b, s: (b, s, 0)), vec],
        out_shape=[jax.ShapeDtypeStruct((B, S, W), F32), jax.ShapeDtypeStruct((1, KVL), F32)],
        compiler_params=_params(_nbytes((ts, W), F32) * 4, _nbytes((ts, W), F32) * 4),
    )(ckr.reshape(B, S, W), g.reshape(1, KVL), dc.reshape(B, S, KVL), dkr, *tabs)
    return o.reshape(T, W), dg.reshape(KVL)


DIAGS = 768


def _diag_onehot():
    col = lax.broadcasted_iota(I32, (REL_PAD, DIAGS), 1)
    row = lax.broadcasted_iota(I32, (REL_PAD, DIAGS), 0)
    idx = jnp.clip(PADR + QROWS - 1 - col, -MAX_REL, MAX_REL) + MAX_REL
    return (row == idx).astype(F32)


def rel_bias_tile(name, table):
    H = table.shape[0]
    tpad = jnp.pad(table, ((0, 0), (0, REL_PAD - table.shape[1])))

    def body(t_ref, o_ref):
        g = lax.dot_general(t_ref[...], _diag_onehot(), NN, precision=lax.Precision.HIGHEST,
                            preferred_element_type=F32)
        for h in range(H):
            gb = jnp.broadcast_to(g[h:h + 1, :], (QROWS, DIAGS))
            tile = pltpu.roll(gb, DIAGS - (QROWS - 1), 1, stride=1, stride_axis=0)
            o_ref[h // 2, (h % 2) * QROWS:(h % 2 + 1) * QROWS, :] = tile[:, :WIN]

    return pl.pallas_call(
        body, name=name, out_shape=jax.ShapeDtypeStruct((H // 2, 2 * QROWS, WIN), F32),
        compiler_params=_params(0, 2 * _nbytes((H // 2, 2 * QROWS, WIN), F32)),
    )(tpad)


def rel_bias_grad(name, dbias):
    H = 2 * dbias.shape[0]

    def body(d_ref, o_ref):
        flip = (lax.broadcasted_iota(I32, (QROWS, QROWS), 0) + lax.broadcasted_iota(I32, (QROWS, QROWS), 1)
                == QROWS - 1).astype(F32)
        rows = []
        for h in range(H):
            x = d_ref[h // 2, (h % 2) * QROWS:(h % 2 + 1) * QROWS, :]
            xr = lax.dot_general(flip, x, NN, precision=lax.Precision.HIGHEST, preferred_element_type=F32)
            xp = jnp.concatenate([xr, jnp.zeros((QROWS, DIAGS - WIN), F32)], axis=1)
            y = pltpu.roll(xp, 0, 1, stride=1, stride_axis=0)
            rows.append(jnp.sum(y, axis=0, keepdims=True))
        o_ref[...] = lax.dot_general(jnp.concatenate(rows, axis=0), _diag_onehot(), NT,
                                     precision=lax.Precision.HIGHEST, preferred_element_type=F32)

    return pl.pallas_call(
        body, name=name, out_shape=jax.ShapeDtypeStruct((H, REL_PAD), F32),
        compiler_params=_params(0, 2 * _nbytes(dbias.shape, F32)),
    )(dbias)


def _stack_pair(xp):
    lane = lax.broadcasted_iota(I32, xp.shape, 1)
    z = jnp.zeros_like(xp)
    return jnp.concatenate([jnp.where(lane < HEAD_DIM_A, xp, z), jnp.where(lane >= HEAD_DIM_A, xp, z)], axis=0)


def _unstack_pair(y):
    lane = lax.broadcasted_iota(I32, (QROWS, LANE), 1)
    return jnp.where(lane < HEAD_DIM_A, y[:QROWS], y[QROWS:])


def _attn_a_mask(j):
    r = lax.broadcasted_iota(I32, (2 * QROWS, WIN), 0)
    w = lax.broadcasted_iota(I32, (2 * QROWS, WIN), 1)
    qc = jnp.right_shift(jnp.bitwise_and(r, QROWS - 1), CHUNK_SHIFT)
    kc = jnp.right_shift(w, CHUNK_SHIFT)
    return (kc >= qc) & (kc <= qc + LEFT_CHUNKS) & (w >= PADR - QROWS * j)


def _attn_a_load_bias(bias_hbm, bias_v, sem):
    cp = pltpu.make_async_copy(bias_hbm, bias_v, sem)
    cp.start()
    cp.wait()


def _attn_a_load_kv(qkv_hbm, b, kpad, vpad, sem, S, D):
    kpad[0:PADR, :] = jnp.zeros((PADR, D), BF16)
    vpad[0:PADR, :] = jnp.zeros((PADR, D), BF16)
    ck = pltpu.make_async_copy(qkv_hbm.at[b, :, pl.ds(D, D)], kpad.at[pl.ds(PADR, S), :], sem.at[0])
    cv = pltpu.make_async_copy(qkv_hbm.at[b, :, pl.ds(2 * D, D)], vpad.at[pl.ds(PADR, S), :], sem.at[1])
    ck.start()
    cv.start()
    ck.wait()
    cv.wait()


def _attn_a_probs(qm, kp, bias, valid, scale):
    s = lax.dot_general(qm, kp, NT, preferred_element_type=F32) * scale + bias
    s = jnp.where(valid, s, NEG_INF)
    e = jnp.exp(s - jnp.max(s, axis=-1, keepdims=True))
    return e * (1.0 / jnp.sum(e, axis=-1, keepdims=True))


def attn_a_fwd(name, qkv, bias):
    B, S, D3 = qkv.shape
    D = D3 // 3
    H = D // HEAD_DIM_A
    nb = S // QROWS
    scale = HEAD_DIM_A ** -0.5

    def body(q_ref, bias_hbm, qkv_hbm, o_ref, kpad, vpad, bias_v, sem):
        b, j = pl.program_id(0), pl.program_id(1)

        @pl.when((b == 0) & (j == 0))
        def _():
            _attn_a_load_bias(bias_hbm, bias_v, sem.at[2])

        @pl.when(j == 0)
        def _():
            _attn_a_load_kv(qkv_hbm, b, kpad, vpad, sem, S, D)

        mask = _attn_a_mask(j)
        w0 = pl.multiple_of(j * QROWS, QROWS)
        for p in range(H // 2):
            ls = slice(p * LANE, (p + 1) * LANE)
            pr = _attn_a_probs(_stack_pair(q_ref[0, :, ls]), kpad[pl.ds(w0, WIN), ls], bias_v[p], mask, scale)
            o2 = jnp.dot(pr.astype(BF16), vpad[pl.ds(w0, WIN), ls], preferred_element_type=F32)
            o_ref[0, :, ls] = _unstack_pair(o2).astype(o_ref.dtype)

    scr = 2 * _nbytes((PADR + S, D), BF16) + _nbytes(bias.shape, F32) + 8 * _nbytes((2 * QROWS, WIN), F32)
    return pl.pallas_call(
        body, name=name, grid=(B, nb),
        in_specs=[pl.BlockSpec((1, QROWS, D), lambda b, j: (b, j, 0)),
                  pl.BlockSpec(memory_space=pl.ANY), pl.BlockSpec(memory_space=pl.ANY)],
        out_specs=pl.BlockSpec((1, QROWS, D), lambda b, j: (b, j, 0)),
        out_shape=jax.ShapeDtypeStruct((B, S, D), BF16),
        scratch_shapes=[pltpu.VMEM((PADR + S, D), BF16), pltpu.VMEM((PADR + S, D), BF16),
                        pltpu.VMEM(bias.shape, F32), pltpu.SemaphoreType.DMA((3,))],
        compiler_params=_params(2 * _nbytes((QROWS, D), BF16), scr),
    )(qkv, bias, qkv)


def attn_a_bwd(name, qkv, do, bias):
    B, S, D3 = qkv.shape
    D = D3 // 3
    H = D // HEAD_DIM_A
    nb = S // QROWS
    scale = HEAD_DIM_A ** -0.5

    def body(q_ref, do_ref, bias_hbm, qkv_hbm, dq_ref, dkv_hbm, dbias_hbm, kpad, vpad, dkacc, dvacc, bias_v, dbias_v, sem):
        b, j = pl.program_id(0), pl.program_id(1)

        @pl.when((b == 0) & (j == 0))
        def _():
            _attn_a_load_bias(bias_hbm, bias_v, sem.at[2])
            dbias_v[...] = jnp.zeros_like(dbias_v)

        @pl.when(j == 0)
        def _():
            _attn_a_load_kv(qkv_hbm, b, kpad, vpad, sem, S, D)
            dkacc[...] = jnp.zeros_like(dkacc)
            dvacc[...] = jnp.zeros_like(dvacc)

        mask = _attn_a_mask(j)
        w0 = pl.multiple_of(j * QROWS, QROWS)
        for p in range(H // 2):
            ls = slice(p * LANE, (p + 1) * LANE)
            q2 = _stack_pair(q_ref[0, :, ls])
            do2 = _stack_pair(do_ref[0, :, ls])
            kp = kpad[pl.ds(w0, WIN), ls]
            vp = vpad[pl.ds(w0, WIN), ls]
            pr = _attn_a_probs(q2, kp, bias_v[p], mask, scale)
            dp = lax.dot_general(do2, vp, NT, preferred_element_type=F32)
            ds = pr * (dp - jnp.sum(pr * dp, axis=-1, keepdims=True))
            dbias_v[p] += ds
            dsb = (ds * scale).astype(BF16)
            dq_ref[0, :, ls] = _unstack_pair(jnp.dot(dsb, kp, preferred_element_type=F32))
            dkacc[pl.ds(w0, WIN), ls] += lax.dot_general(dsb, q2, TN, preferred_element_type=F32)
            dvacc[pl.ds(w0, WIN), ls] += lax.dot_general(pr.astype(BF16), do2, TN, preferred_element_type=F32)

        @pl.when(j == nb - 1)
        def _():
            ck = pltpu.make_async_copy(dkacc.at[pl.ds(PADR, S), :], dkv_hbm.at[b, :, pl.ds(0, D)], sem.at[0])
            cv = pltpu.make_async_copy(dvacc.at[pl.ds(PADR, S), :], dkv_hbm.at[b, :, pl.ds(D, D)], sem.at[1])
            ck.start()
            cv.start()
            ck.wait()
            cv.wait()

        @pl.when((b == B - 1) & (j == nb - 1))
        def _():
            cb = pltpu.make_async_copy(dbias_v, dbias_hbm, sem.at[2])
            cb.start()
            cb.wait()

    blk = _nbytes((QROWS, D), BF16) * 2 + _nbytes((QROWS, D), F32)
    scr = (2 * _nbytes((PADR + S, D), BF16) + 2 * _nbytes((PADR + S, D), F32) + 2 * _nbytes(bias.shape, F32)
           + 8 * _nbytes((2 * QROWS, WIN), F32))
    return pl.pallas_call(
        body, name=name, grid=(B, nb),
        in_specs=[pl.BlockSpec((1, QROWS, D), lambda b, j: (b, j, 0)),
                  pl.BlockSpec((1, QROWS, D), lambda b, j: (b, j, 0)),
                  pl.BlockSpec(memory_space=pl.ANY), pl.BlockSpec(memory_space=pl.ANY)],
        out_specs=[pl.BlockSpec((1, QROWS, D), lambda b, j: (b, j, 0)),
                   pl.BlockSpec(memory_space=pl.ANY), pl.BlockSpec(memory_space=pl.ANY)],
        out_shape=[jax.ShapeDtypeStruct((B, S, D), F32), jax.ShapeDtypeStruct((B, S, 2 * D), F32),
                   jax.ShapeDtypeStruct(bias.shape, F32)],
        scratch_shapes=[pltpu.VMEM((PADR + S, D), BF16), pltpu.VMEM((PADR + S, D), BF16),
                        pltpu.VMEM((PADR + S, D), F32), pltpu.VMEM((PADR + S, D), F32),
                        pltpu.VMEM(bias.shape, F32), pltpu.VMEM(bias.shape, F32),
                        pltpu.SemaphoreType.DMA((3,))],
        compiler_params=_params(blk, scr),
    )(qkv, do, bias, qkv)


def _mla_scores_t(k2blk, q, q0, k0, scale):
    st = lax.dot_general(k2blk, q, NT, preferred_element_type=F32) * scale
    kc = jnp.right_shift(k0 + lax.broadcasted_iota(I32, st.shape, 0), CHUNK_SHIFT)
    qc = jnp.right_shift(q0 + lax.broadcasted_iota(I32, st.shape, 1), CHUNK_SHIFT)
    return st, kc <= qc


def _mla_fill_keys(kv_ref, kr_ref, k2):
    k2[:, :NOPE] = kv_ref[0, :, :NOPE]
    k2[:, NOPE:] = kr_ref[0]


def _t(x):
    return x.astype(F32).T


def mla_fwd(name, qf, kv, kr):
    B, S, W = qf.shape
    HB = W // 256
    QB = _tile(S, 256, CHUNK)
    nq = S // QB
    scale = (NOPE + ROPE) ** -0.5

    def body(q_ref, kv_ref, kr_ref, o_ref, lse_ref, k2, vt):
        qi = pl.program_id(2)

        @pl.when(qi == 0)
        def _():
            _mla_fill_keys(kv_ref, kr_ref, k2)
            for kj in range(nq):
                vt[kj] = _t(kv_ref[0, kj * QB:(kj + 1) * QB, NOPE:]).astype(BF16)

        q = q_ref[0]

        def step(kj, carry):
            m, l, acc = carry
            ks = pl.ds(pl.multiple_of(kj * QB, QB), QB)
            st, mask = _mla_scores_t(k2[ks, :], q, qi * QB, kj * QB, scale)
            st = jnp.where(mask, st, NEG_INF)
            m_new = jnp.maximum(m, jnp.max(st, axis=0, keepdims=True))
            a = jnp.exp(m - m_new)
            pt = jnp.exp(st - m_new)
            l = a * l + jnp.sum(pt, axis=0, keepdims=True)
            acc = a * acc + jnp.dot(vt[kj], pt.astype(BF16), preferred_element_type=F32)
            return m_new, l, acc

        init = (jnp.full((1, QB), NEG_INF, F32), jnp.zeros((1, QB), F32), jnp.zeros((NOPE, QB), F32))
        m, l, acc = lax.fori_loop(0, qi + 1, step, init)
        o_ref[0] = (acc * (1.0 / l)).T
        lse_ref[0, 0] = m + jnp.log(l)

    blk = (_nbytes((QB, 256), BF16) + _nbytes((S, 256), BF16) + _nbytes((S, LANE), BF16)
           + _nbytes((QB, LANE), F32))
    return pl.pallas_call(
        body, name=name, grid=(B, HB, nq),
        in_specs=[pl.BlockSpec((1, QB, 256), lambda b, h, i: (b, i, h)),
                  pl.BlockSpec((1, S, 256), lambda b, h, i: (b, 0, h)),
                  pl.BlockSpec((1, S, LANE), lambda b, h, i: (b, 0, 0))],
        out_specs=[pl.BlockSpec((1, QB, LANE), lambda b, h, i: (b, i, h)),
                   pl.BlockSpec((1, 1, 1, QB), lambda b, h, i: (b, h, 0, i))],
        out_shape=[jax.ShapeDtypeStruct((B, S, HB * LANE), F32), jax.ShapeDtypeStruct((B, HB, 1, S), F32)],
        scratch_shapes=[pltpu.VMEM((S, 256), BF16), pltpu.VMEM((nq, NOPE, QB), BF16)],
        compiler_params=_params(blk, 2 * _nbytes((S, 256), BF16) + 8 * _nbytes((QB, QB), F32)),
    )(qf, kv, kr)


def mla_bwd(name, qf, kv, kr, do, o, lse):
    B, S, W = qf.shape
    HB = W // 256
    QB = _tile(S, 256, CHUNK)
    nq = S // QB
    scale = (NOPE + ROPE) ** -0.5

    def body(q_ref, kv_ref, kr_ref, do_ref, o_ref, lse_ref, dq_ref, dkv_ref, dkr_ref, k2, kt, dot_, delta, dqt):
        h = pl.program_id(1)
        dkv_ref[...] = jnp.zeros_like(dkv_ref)

        @pl.when(h == 0)
        def _():
            dkr_ref[...] = jnp.zeros_like(dkr_ref)

        _mla_fill_keys(kv_ref, kr_ref, k2)
        for i in range(nq):
            rows = slice(i * QB, (i + 1) * QB)
            kt[i] = _t(k2[rows, :]).astype(BF16)
            dot32 = _t(do_ref[0, rows, :])
            delta[i] = jnp.sum(dot32 * o_ref[0, rows, :].T, axis=0, keepdims=True)
            dot_[i] = dot32.astype(BF16)

        for qi in range(nq):
            rows = slice(qi * QB, (qi + 1) * QB)
            q = q_ref[0, rows, :]
            dob = do_ref[0, rows, :]
            lse_q = lse_ref[0, 0, :, rows]
            delta_q = delta[qi]
            dqt[...] = jnp.zeros_like(dqt)

            def step(kj, carry, q=q, dob=dob, lse_q=lse_q, delta_q=delta_q, qi=qi):
                ks = pl.ds(pl.multiple_of(kj * QB, QB), QB)
                st, mask = _mla_scores_t(k2[ks, :], q, qi * QB, kj * QB, scale)
                pt = jnp.where(mask, jnp.exp(st - lse_q), 0.0)
                dpt = jnp.dot(kv_ref[0, ks, NOPE:], dot_[qi], preferred_element_type=F32)
                dst = (pt * (dpt - delta_q) * scale).astype(BF16)
                dkv_ref[0, ks, NOPE:] += jnp.dot(pt.astype(BF16), dob, preferred_element_type=F32)
                dk2 = jnp.dot(dst, q, preferred_element_type=F32)
                dkv_ref[0, ks, :NOPE] += dk2[:, :NOPE]
                dkr_ref[0, ks, :] += dk2[:, NOPE:]
                dqt[...] += jnp.dot(kt[kj], dst, preferred_element_type=F32)
                return carry

            lax.fori_loop(0, qi + 1, step, 0)
            dq_ref[0, rows, :] = dqt[...].T

    head = lambda w: pl.BlockSpec((1, S, w), lambda b, h: (b, 0, h))
    shared = pl.BlockSpec((1, S, LANE), lambda b, h: (b, 0, 0))
    blk = (2 * _nbytes((S, 256), BF16) + 2 * _nbytes((S, LANE), BF16) + _nbytes((S, LANE), F32)
           + 2 * _nbytes((S, 256), F32) + _nbytes((S, LANE), F32))
    scr = 3 * _nbytes((S, 256), BF16) + 10 * _nbytes((QB, QB), F32)
    return pl.pallas_call(
        body, name=name, grid=(B, HB),
        in_specs=[head(256), head(256), shared, head(LANE), head(LANE),
                  pl.BlockSpec((1, 1, 1, S), lambda b, h: (b, h, 0, 0))],
        out_specs=[head(256), head(256), shared],
        out_shape=[jax.ShapeDtypeStruct((B, S, W), F32), jax.ShapeDtypeStruct((B, S, W), F32),
                   jax.ShapeDtypeStruct((B, S, LANE), F32)],
        scratch_shapes=[pltpu.VMEM((S, 256), BF16), pltpu.VMEM((nq, 256, QB), BF16),
                        pltpu.VMEM((nq, NOPE, QB), BF16), pltpu.VMEM((nq, 1, QB), F32),
                        pltpu.VMEM((256, QB), F32)],
        compiler_params=_params(blk, scr),
    )(qf, kv, kr, do, o, lse)


def cast_bf16(name, w, layer, idx):
    _, R, C = w.shape
    tr = _tile(R, 256, 16)

    def body(k_ref, w_ref, o_ref):
        o_ref[...] = w_ref[...].astype(BF16)

    return pl.pallas_call(
        body, name=name,
        grid_spec=pltpu.PrefetchScalarGridSpec(
            num_scalar_prefetch=1, grid=(R // tr,),
            in_specs=[pl.BlockSpec((None, tr, C), lambda r, k_ref: (layer, r, 0))],
            out_specs=pl.BlockSpec((None, tr, C), lambda r, k_ref: (k_ref[0], r, 0))),
        out_shape=jax.ShapeDtypeStruct((N_CHIPS, R, C), BF16),
    )(idx, w)


def adamw(name, w, g, m, v):
    R, C = w.shape
    tr = _tile(R, max(8, (1 << 18) // C // 8 * 8), 8)
    c1 = 1.0 - ADAM_B1 ** ADAM_STEP
    c2 = 1.0 - ADAM_B2 ** ADAM_STEP

    def body(w_ref, g_ref, m_ref, v_ref, d_ref, mo_ref, vo_ref):
        gv = g_ref[...]
        mn = ADAM_B1 * m_ref[...] + (1.0 - ADAM_B1) * gv
        vn = ADAM_B2 * v_ref[...] + (1.0 - ADAM_B2) * (gv * gv)
        mo_ref[...] = mn
        vo_ref[...] = vn
        d_ref[...] = -ADAM_LR * ((mn / c1) / (jnp.sqrt(vn / c2) + ADAM_EPS) + ADAM_WD * w_ref[...])

    spec = pl.BlockSpec((tr, C), lambda r: (r, 0))
    return pl.pallas_call(
        body, name=name, grid=(R // tr,), in_specs=[spec] * 4, out_specs=[spec] * 3,
        out_shape=[jax.ShapeDtypeStruct((R, C), F32)] * 3,
        compiler_params=_params(7 * _nbytes((tr, C), F32), 4 * _nbytes((tr, C), F32)),
    )(w, g, m, v)


def half_sum(name, dw, landed, idx):
    _, _, hr, C = dw.shape
    tr = _tile(hr, max(16, (1 << 18) // C // 16 * 16), 16)

    def body(i_ref, a_ref, b_ref, o_ref):
        o_ref[...] = (a_ref[...].astype(F32) + b_ref[...].astype(F32)).astype(o_ref.dtype)

    return pl.pallas_call(
        body, name=name,
        grid_spec=pltpu.PrefetchScalarGridSpec(
            num_scalar_prefetch=1, grid=(N_CHIPS, hr // tr),
            in_specs=[pl.BlockSpec((None, None, tr, C), lambda k, r, i_ref: (k, i_ref[1], r, 0)),
                      pl.BlockSpec((None, tr, C), lambda k, r, i_ref: (k, r, 0))],
            out_specs=pl.BlockSpec((None, tr, C), lambda k, r, i_ref: (k, r, 0))),
        out_shape=jax.ShapeDtypeStruct((N_CHIPS, hr, C), BF16),
    )(idx, dw, landed)


def chip_sum(name, part, landed, gbuf, layer, idx):
    _, hr, C = part.shape
    tr = _tile(hr, max(16, (1 << 18) // C // 16 * 16), 16)

    def body(i_ref, a_ref, b_ref, g_ref, o_ref):
        o_ref[...] = ((a_ref[...].astype(F32) + b_ref[0].astype(F32)) + b_ref[1].astype(F32)) + b_ref[2].astype(F32)

    return pl.pallas_call(
        body, name=name,
        grid_spec=pltpu.PrefetchScalarGridSpec(
            num_scalar_prefetch=1, grid=(hr // tr,),
            in_specs=[pl.BlockSpec((None, tr, C), lambda r, i_ref: (i_ref[0], r, 0)),
                      pl.BlockSpec((3, tr, C), lambda r, i_ref: (0, r, 0)),
                      pl.BlockSpec(memory_space=pl.ANY)],
            out_specs=pl.BlockSpec((None, None, tr, C), lambda r, i_ref: (layer, i_ref[1], r, 0))),
        out_shape=jax.ShapeDtypeStruct(gbuf.shape, F32),
        input_output_aliases={3: 0},
    )(idx, part, landed, gbuf)


ANY = pl.BlockSpec(memory_space=pl.ANY)


def _place():
    x, y, c = lax.axis_index("x"), lax.axis_index("y"), lax.axis_index("c")
    chips = [(1 - x, y), (x, 1 - y), (1 - x, 1 - y)]
    return x, y, c, chips


HBM = pl.BlockSpec(memory_space=pltpu.HBM)
SEM = pl.BlockSpec(memory_space=pltpu.SEMAPHORE)
EFFECT = pltpu.SideEffectType.DATAFLOW_SIDE_EFFECTING


def _in_hbm(a):
    return pltpu.with_memory_space_constraint(a, pltpu.HBM)


def _ici_copy(src, dst, send_sems, recv_sems, k, peer):
    return pltpu.make_async_remote_copy(src_ref=src, dst_ref=dst, send_sem=send_sems.at[k], recv_sem=recv_sems.at[k],
                                        device_id=peer, device_id_type=MESH)


def ici_start(name, bufs, lands, after, gather):
    n, nl = len(bufs), len(lands)

    def body(*refs):
        b_in = refs[:n]
        send_sems, recv_sems = refs[n + nl + 1], refs[n + nl + 2]
        b_out = refs[n + nl + 3:2 * n + nl + 3]
        l_out = refs[2 * n + nl + 3:2 * n + 2 * nl + 3]
        token = refs[-1]
        x, y, c, chips = _place()
        kme = 2 * x + y
        for i in range(n):
            for j in range(3):
                peer = (*chips[j], c)
                if gather:
                    _ici_copy(b_out[i].at[kme, c], b_out[i].at[kme, c], send_sems, recv_sems, 3 * i + j, peer).start()
                else:
                    kd = 2 * chips[j][0] + chips[j][1]
                    _ici_copy(b_out[i].at[kd], l_out[i].at[j], send_sems, recv_sems, 3 * i + j, peer).start()
        token[...] = jnp.zeros_like(token)

    arrays = [*bufs, *lands]
    outs = pl.pallas_call(
        body, name=name,
        in_specs=[HBM] * (n + nl) + [ANY],
        out_specs=(SEM, SEM, *[HBM] * (n + nl), pl.BlockSpec(memory_space=pltpu.VMEM)),
        out_shape=(pltpu.SemaphoreType.DMA((3 * n,)), pltpu.SemaphoreType.DMA((3 * n,)),
                   *[pltpu.HBM(a.shape, a.dtype) for a in arrays], jax.ShapeDtypeStruct((8, LANE), F32)),
        input_output_aliases={i: 2 + i for i in range(n + nl)},
        compiler_params=pltpu.CompilerParams(has_side_effects=EFFECT),
    )(*[_in_hbm(a) for a in arrays], after)
    return outs[0], outs[1], list(outs[2:2 + n]), list(outs[2 + n:2 + n + nl]), outs[-1]


def ici_wait(name, send_sems, recv_sems, bufs, lands, after, gather):
    n, nl = len(bufs), len(lands)

    def body(*refs):
        b_in, l_in = refs[:n], refs[n:n + nl]
        send_sems, recv_sems = refs[n + nl], refs[n + nl + 1]
        x, y, c, chips = _place()
        kme = 2 * x + y
        for i in range(n):
            for j in range(3):
                peer = (*chips[j], c)
                kj = 2 * chips[j][0] + chips[j][1]
                if gather:
                    _ici_copy(b_in[i].at[kme, c], b_in[i].at[kme, c], send_sems, recv_sems, 3 * i + j, peer).wait_send()
                    _ici_copy(b_in[i].at[kj, c], b_in[i].at[kj, c], send_sems, recv_sems, 3 * i + j, peer).wait_recv()
                else:
                    _ici_copy(b_in[i].at[kj], l_in[i].at[j], send_sems, recv_sems, 3 * i + j, peer).wait_send()
                    _ici_copy(b_in[i].at[kj], l_in[i].at[j], send_sems, recv_sems, 3 * i + j, peer).wait_recv()

    arrays = [*bufs, *lands]
    outs = pl.pallas_call(
        body, name=name,
        in_specs=[HBM] * (n + nl) + [SEM, SEM, ANY],
        out_specs=tuple([HBM] * (n + nl)),
        out_shape=tuple(pltpu.HBM(a.shape, a.dtype) for a in arrays),
        input_output_aliases={i: i for i in range(n + nl)},
        compiler_params=pltpu.CompilerParams(has_side_effects=EFFECT),
    )(*arrays, send_sems, recv_sems, after)
    return list(outs[:n]), list(outs[n:])


def gather_pair_pass(name, bufs):
    n = len(bufs)

    def body(*refs):
        b = refs[n:2 * n]
        send_sems, recv_sems = refs[2 * n:]
        x, y, c, chips = _place()
        sib = (x, y, 1 - c)

        def d2d(i, j, which):
            kj = 2 * chips[j][0] + chips[j][1]
            return _ici_copy(b[i].at[kj, which], b[i].at[kj, which], send_sems, recv_sems, 3 * i + j, sib)

        for i in range(n):
            for j in range(3):
                d2d(i, j, c).start()
        for i in range(n):
            for j in range(3):
                d2d(i, j, 1 - c).wait_recv()
        for i in range(n):
            for j in range(3):
                d2d(i, j, c).wait_send()

    return pl.pallas_call(
        body, name=name, in_specs=[ANY] * n, out_specs=[ANY] * n,
        out_shape=[jax.ShapeDtypeStruct(a.shape, a.dtype) for a in bufs],
        input_output_aliases={i: i for i in range(n)},
        scratch_shapes=[pltpu.SemaphoreType.DMA((3 * n,)), pltpu.SemaphoreType.DMA((3 * n,))],
    )(*bufs)


def pair_exchange(name, dws):
    n = len(dws)

    def body(*refs):
        ins, outs = refs[:n], refs[n:2 * n]
        send_sems, recv_sems = refs[2 * n:]
        x, y, c, _ = _place()
        copies = []
        for i in range(n):
            copies.append(pltpu.make_async_remote_copy(
                src_ref=ins[i].at[:, 1 - c], dst_ref=outs[i],
                send_sem=send_sems.at[i], recv_sem=recv_sems.at[i],
                device_id=(x, y, 1 - c), device_id_type=MESH))
            copies[i].start()
        for cp in copies:
            cp.wait_recv()
        for cp in copies:
            cp.wait_send()

    return pl.pallas_call(
        body, name=name, in_specs=[ANY] * n, out_specs=[ANY] * n,
        out_shape=[jax.ShapeDtypeStruct((N_CHIPS, *d.shape[2:]), d.dtype) for d in dws],
        scratch_shapes=[pltpu.SemaphoreType.DMA((n,)), pltpu.SemaphoreType.DMA((n,))],
    )(*dws)


def pair_assemble(gbufs):
    n = len(gbufs)

    def body(*refs):
        bufs = refs[n:2 * n]
        send_sems, recv_sems = refs[2 * n:]
        x, y, c, _ = _place()
        copies = []
        for i in range(n):
            copies.append(pltpu.make_async_remote_copy(
                src_ref=bufs[i].at[:, c], dst_ref=bufs[i].at[:, c],
                send_sem=send_sems.at[i], recv_sem=recv_sems.at[i],
                device_id=(x, y, 1 - c), device_id_type=MESH))
            copies[i].start()
        for i in range(n):
            pltpu.make_async_remote_copy(
                src_ref=bufs[i].at[:, 1 - c], dst_ref=bufs[i].at[:, 1 - c],
                send_sem=send_sems.at[i], recv_sem=recv_sems.at[i],
                device_id=(x, y, 1 - c), device_id_type=MESH).wait_recv()
        for cp in copies:
            cp.wait_send()

    return pl.pallas_call(
        body, name="grad_pair_assemble", in_specs=[ANY] * n, out_specs=[ANY] * n,
        out_shape=[jax.ShapeDtypeStruct(g.shape, g.dtype) for g in gbufs],
        input_output_aliases={i: i for i in range(n)},
        scratch_shapes=[pltpu.SemaphoreType.DMA((n,)), pltpu.SemaphoreType.DMA((n,))],
    )(*gbufs)


def all_reduce_small(vec):
    NR = vec.shape[0]
    flips = [(fx, fy, fc) for fx in (0, 1) for fy in (0, 1) for fc in (0, 1)][1:]

    def body(v_ref, o_ref, buf, send_sems, recv_sems):
        x, y, c, _ = _place()
        me = 4 * x + 2 * y + c
        buf[me] = v_ref[...]
        copies = []
        for j, (fx, fy, fc) in enumerate(flips):
            peer = (1 - x if fx else x, 1 - y if fy else y, 1 - c if fc else c)
            copies.append(pltpu.make_async_remote_copy(
                src_ref=v_ref, dst_ref=buf.at[me], send_sem=send_sems.at[j], recv_sem=recv_sems.at[j],
                device_id=peer, device_id_type=MESH))
            copies[j].start()
        for cp in copies:
            cp.wait_recv()
        for cp in copies:
            cp.wait_send()
        acc = buf[0]
        for d in range(1, 8):
            acc = acc + buf[d]
        o_ref[...] = acc

    return pl.pallas_call(
        body, name="all_reduce_small",
        in_specs=[pl.BlockSpec(memory_space=pltpu.VMEM)], out_specs=pl.BlockSpec(memory_space=pltpu.VMEM),
        out_shape=jax.ShapeDtypeStruct((NR, LANE), F32),
        scratch_shapes=[pltpu.VMEM((8, NR, LANE), F32), pltpu.SemaphoreType.DMA((7,)),
                        pltpu.SemaphoreType.DMA((7,))],
    )(vec)


def _pack(arrays):
    flat = jnp.concatenate([a.reshape(-1).astype(F32) for a in arrays])
    n = flat.shape[0]
    npad = -(-n // (8 * LANE)) * (8 * LANE)
    return jnp.pad(flat, (0, npad - n)).reshape(npad // LANE, LANE)


def _unpack(buf, like):
    flat = buf.reshape(-1)
    out, off = [], 0
    for a in like:
        out.append(flat[off:off + a.size].reshape(a.shape))
        off += a.size
    return out


def kernel(x, ffn1_norm, ffn1_w_in, ffn1_w_out, mix_norm, ffn2_norm, ffn2_w_in, ffn2_w_out, a_w_qkv, a_rel_bias, a_w_o, kv_norm, kv_w_down, kv_latent_norm, kv_w_up, b_w_dq, b_q_norm, b_w_uq, b_w_o, final_norm, loss_target, m_ffn1_norm, m_ffn1_w_in, m_ffn1_w_out, m_mix_norm, m_ffn2_norm, m_ffn2_w_in, m_ffn2_w_out, m_a_w_qkv, m_a_rel_bias, m_a_w_o, m_kv_norm, m_kv_w_down, m_kv_latent_norm, m_kv_w_up, m_b_w_dq, m_b_q_norm, m_b_w_uq, m_b_w_o, m_final_norm, v_ffn1_norm, v_ffn1_w_in, v_ffn1_w_out, v_mix_norm, v_ffn2_norm, v_ffn2_w_in, v_ffn2_w_out, v_a_w_qkv, v_a_rel_bias, v_a_w_o, v_kv_norm, v_kv_w_down, v_kv_latent_norm, v_kv_w_up, v_b_w_dq, v_b_q_norm, v_b_w_uq, v_b_w_o, v_final_norm):
    B, S, D = x.shape
    T = B * S
    HB = D // 128
    QL = b_q_norm.shape[-1]
    KVL = kv_latent_norm.shape[0]
    hpc = HB // N_CHIPS
    tabs = rope_tables(S)
    idx = jnp.stack([2 * lax.axis_index("x") + lax.axis_index("y"), lax.axis_index("c")]).astype(I32)

    def halves(a):
        return a.reshape(*a.shape[:-2], 2, a.shape[-2] // 2, a.shape[-1])

    def whole(a):
        return a.reshape(*a.shape[:-3], 2 * a.shape[-2], a.shape[-1])

    kv_w_down_p = jnp.pad(kv_w_down, ((0, 0), (0, LANE - ROPE)))[None]
    b_w_uq_p = jnp.pad(b_w_uq.reshape(1, QL, hpc, NOPE + ROPE),
                       ((0, 0), (0, 0), (0, 0), (0, LANE - ROPE))).reshape(1, QL, hpc * 256)
    sharded = [("ffn1_w_in", ffn1_w_in), ("ffn1_w_out", ffn1_w_out), ("ffn2_w_in", ffn2_w_in),
               ("ffn2_w_out", ffn2_w_out), ("a_w_qkv", a_w_qkv), ("a_w_o", a_w_o),
               ("kv_w_down", kv_w_down_p), ("kv_w_up", kv_w_up[None]), ("b_w_dq", b_w_dq),
               ("b_w_uq", b_w_uq_p), ("b_w_o", b_w_o)]
    pieces = [(a, l) for a, (_, w) in enumerate(sharded) for l in range(w.shape[0])]
    names = [nm for nm, _ in sharded]
    own = {(names[a], l): cast_bf16(f"cast_{names[a]}_{l}", sharded[a][1], l, idx) for a, l in pieces}
    W = {}

    gather_groups = [
        [("ffn1_w_in", 0), ("ffn1_w_out", 0)],
        [("a_w_qkv", 0), ("a_w_o", 0), ("ffn2_w_in", 0), ("ffn2_w_out", 0), ("kv_w_down", 0), ("kv_w_up", 0)],
        [("ffn1_w_in", 1), ("ffn1_w_out", 1), ("b_w_dq", 0), ("b_w_uq", 0), ("b_w_o", 0), ("ffn2_w_in", 1),
         ("ffn2_w_out", 1)]]

    def gather_start(g, after):
        keys = gather_groups[g]
        ss, rs, bufs, _, token = ici_start(f"gather_start_{g}", [halves(own[k]) for k in keys], [], after, True)
        return (g, ss, rs, bufs), token

    def gather_finish(state, after):
        g, ss, rs, bufs = state
        bufs, _ = ici_wait(f"gather_wait_{g}", ss, rs, bufs, [], after, True)
        full = gather_pair_pass(f"gather_pair_{g}", bufs)
        for k, w in zip(gather_groups[g], full):
            W[k] = whole(w)
        return full[0]

    def tied(a, token):
        return a + token[0, 0]

    def col(nm, l=0):
        return W[(nm, l)]

    def row(nm, l=0):
        w = W[(nm, l)]
        return w.reshape(N_CHIPS * w.shape[1], w.shape[2])

    bias = rel_bias_tile("rel_bias_tile", a_rel_bias[0])

    def ffn_fwd(tag, h, g, w_in, w_out):
        xn = rms_fwd(f"{tag}_norm", h, g)
        u, act = ffn_in_act(f"{tag}_in", xn, w_in)
        return mm_roww(f"{tag}_out", act, w_out, F32, res=h, alpha=0.5), (xn, u, act)

    h0 = x.reshape(T, D)
    st0, tok0 = gather_start(0, h0)
    done0 = gather_finish(st0, tok0)
    st1, tok1 = gather_start(1, done0)
    h1, sv_f1a = ffn_fwd("l0f1", h0, tied(ffn1_norm[0], tok1), col("ffn1_w_in", 0), row("ffn1_w_out", 0))
    done1 = gather_finish(st1, h1)
    st2, tok2 = gather_start(2, done1)
    hn_a = rms_fwd("l0mix_norm", h1, tied(mix_norm[0], tok2))
    qkv = mm_colw("l0_qkv", hn_a, col("a_w_qkv"), BF16).reshape(B, S, 3 * D)
    o_a = attn_a_fwd("l0_attn", qkv, bias).reshape(T, D)
    h2 = mm_roww("l0_attn_out", o_a, row("a_w_o"), F32, res=h1)
    h3, sv_f2a = ffn_fwd("l0f2", h2, ffn2_norm[0], col("ffn2_w_in", 0), row("ffn2_w_out", 0))

    hkv = rms_fwd("kv_norm", h3, kv_norm)
    ckr = mm_roww("kv_down", hkv, row("kv_w_down"), F32)
    ckv, kr = kvprep_fwd("kv_prep", ckr, kv_latent_norm, tabs, B, S)
    kvb = mm_colw("kv_up", ckv, col("kv_w_up"), BF16).reshape(B, S, HB * 256)
    gather_finish(st2, kvb)

    h4, sv_f1b = ffn_fwd("l1f1", h3, ffn1_norm[1], col("ffn1_w_in", 1), row("ffn1_w_out", 1))
    hn_b = rms_fwd("l1mix_norm", h4, mix_norm[1])
    cqp = mm_roww("l1_dq", hn_b, row("b_w_dq"), F32)
    cq = rms_fwd("l1_q_norm", cqp, b_q_norm[0])
    qpre = mm_colw("l1_uq", cq, col("b_w_uq"), F32)
    qf = qprep("l1_q_rope", qpre, tabs, B, S, bwd=False).reshape(B, S, HB * 256)
    o_b, lse = mla_fwd("l1_attn", qf, kvb, kr)
    h5 = mm_roww("l1_attn_out", o_b.reshape(T, HB * LANE), row("b_w_o"), F32, res=h4)
    h6, sv_f2b = ffn_fwd("l1f2", h5, ffn2_norm[1], col("ffn2_w_in", 1), row("ffn2_w_out", 1))

    dh, g_final, loss_part = loss_head("loss_head", h6, final_norm, loss_target.reshape(T, D))

    gw = {}
    gbufs = {nm: lax.empty(halves(w).shape, F32) for nm, w in sharded}

    def reduce_start(r, keys, after):
        dws = [halves(gw[k]) for k in keys]
        landed = pair_exchange(f"grad_pair_exchange_{r}", dws)
        parts = [half_sum(f"half_sum_{r}_{i}", dws[i], landed[i], idx) for i in range(len(keys))]
        lands = [lax.empty((3, *p.shape[1:]), p.dtype) for p in parts]
        ss, rs, parts, lands, token = ici_start(f"reduce_start_{r}", parts, lands, after, False)
        return (r, keys, ss, rs, parts, lands), token

    def reduce_finish(state, after):
        r, keys, ss, rs, parts, lands = state
        parts, lands = ici_wait(f"reduce_wait_{r}", ss, rs, parts, lands, after, False)
        for i, (nm, l) in enumerate(keys):
            gbufs[nm] = chip_sum(f"chip_sum_{r}_{i}", parts[i], lands[i], gbufs[nm], l, idx)
        return gbufs[keys[0][0]]

    def ffn_bwd(tag, dh, h_in, g, w_in, w_out, saved, key_in, key_out, after=None):
        xn, u, act = saved
        du = ffn_dact(f"{tag}_dact", dh, w_out, u, after=after)
        dwo = mm_droww(f"{tag}_dwout", act, dh, alpha=0.5)
        gw[key_out] = dwo.reshape(N_CHIPS, dwo.shape[0] // N_CHIPS, dwo.shape[1])
        gw[key_in] = mm_dcolw(f"{tag}_dwin", xn, du, pair_layout=True)
        dxn = mm_colw_t(f"{tag}_dxn", du, w_in, F32, pair_layout=True)
        return rms_bwd(f"{tag}_dnorm", h_in, g, dxn, dres=dh)

    def chip_major(dw):
        return dw.reshape(N_CHIPS, dw.shape[0] // N_CHIPS, dw.shape[1])

    dh, g_f2b = ffn_bwd("l1f2b", dh, h5, ffn2_norm[1], col("ffn2_w_in", 1), row("ffn2_w_out", 1), sv_f2b,
                        ("ffn2_w_in", 1), ("ffn2_w_out", 1))
    red0, rtok0 = reduce_start(0, [("ffn2_w_in", 1), ("ffn2_w_out", 1)], dh)
    do_b = mm_roww_t("l1_attn_do", dh, row("b_w_o"), BF16, after=rtok0).reshape(B, S, HB * LANE)
    gw[("b_w_o", 0)] = chip_major(mm_droww("l1_attn_dwo", o_b.reshape(T, HB * LANE), dh))
    dqf, dkv, dkr = mla_bwd("l1_attn_bwd", qf, kvb, kr, do_b, o_b, lse)
    dqpre = qprep("l1_q_rope_bwd", dqf.reshape(T, HB * 256), tabs, B, S, bwd=True)
    gw[("b_w_uq", 0)] = mm_dcolw("l1_dwuq", cq, dqpre)
    dcq = mm_colw_t("l1_dcq", dqpre, col("b_w_uq"), F32)
    dcqp, g_qn = rms_bwd("l1_dq_norm", cqp, b_q_norm[0], dcq)
    gw[("b_w_dq", 0)] = chip_major(mm_droww("l1_dwdq", hn_b, dcqp))
    dhn = mm_roww_t("l1_dhn", dcqp, row("b_w_dq"), F32)
    dh, g_mixb = rms_bwd("l1_dmix", h4, mix_norm[1], dhn, dres=dh)
    dh, g_f1b = ffn_bwd("l1f1b", dh, h3, ffn1_norm[1], col("ffn1_w_in", 1), row("ffn1_w_out", 1), sv_f1b,
                        ("ffn1_w_in", 1), ("ffn1_w_out", 1))
    fin0 = reduce_finish(red0, dh)
    red1, rtok1 = reduce_start(1, [("b_w_o", 0), ("b_w_uq", 0), ("b_w_dq", 0), ("ffn1_w_in", 1), ("ffn1_w_out", 1)], fin0)
    dkv2 = dkv.reshape(T, HB * 256)
    gw[("kv_w_up", 0)] = mm_dcolw("kv_dwup", ckv, dkv2, after=rtok1)
    dckv = mm_colw_t("kv_dckv", dkv2, col("kv_w_up"), F32, after=rtok1)
    dckr, g_lat = kvprep_bwd("kv_prep_bwd", ckr, kv_latent_norm, dckv, dkr, tabs, B, S)
    gw[("kv_w_down", 0)] = chip_major(mm_droww("kv_dwdown", hkv, dckr))
    dhkv = mm_roww_t("kv_dhkv", dckr, row("kv_w_down"), F32)
    dh, g_kvn = rms_bwd("kv_dnorm", h3, kv_norm, dhkv, dres=dh)
    dh, g_f2a = ffn_bwd("l0f2b", dh, h2, ffn2_norm[0], col("ffn2_w_in", 0), row("ffn2_w_out", 0), sv_f2a,
                        ("ffn2_w_in", 0), ("ffn2_w_out", 0))
    do_a = mm_roww_t("l0_attn_do", dh, row("a_w_o"), BF16).reshape(B, S, D)
    gw[("a_w_o", 0)] = chip_major(mm_droww("l0_attn_dwo", o_a, dh))
    dq_a, dkv_a, dbias = attn_a_bwd("l0_attn_bwd", qkv, do_a, bias)
    dqkv = jnp.concatenate([dq_a.reshape(T, D), dkv_a.reshape(T, 2 * D)], axis=1)
    gw[("a_w_qkv", 0)] = mm_dcolw("l0_dwqkv", hn_a, dqkv)
    dhn = mm_colw_t("l0_dhn", dqkv, col("a_w_qkv"), F32)
    dh, g_mixa = rms_bwd("l0_dmix", h1, mix_norm[0], dhn, dres=dh)
    fin1 = reduce_finish(red1, dh)
    red2, rtok2 = reduce_start(2, [("kv_w_up", 0), ("kv_w_down", 0), ("ffn2_w_in", 0), ("ffn2_w_out", 0),
                                   ("a_w_o", 0), ("a_w_qkv", 0)], fin1)
    dh, g_f1a = ffn_bwd("l0f1b", dh, h0, ffn1_norm[0], col("ffn1_w_in", 0), row("ffn1_w_out", 0), sv_f1a,
                        ("ffn1_w_in", 0), ("ffn1_w_out", 0), after=rtok2)
    grad_x = dh.reshape(B, S, D)
    g_rel = rel_bias_grad("rel_bias_grad", dbias)[:, :2 * MAX_REL + 1][None]
    fin2 = reduce_finish(red2, dh)
    red3, rtok3 = reduce_start(3, [("ffn1_w_in", 0), ("ffn1_w_out", 0)], fin2)
    reduce_finish(red3, rtok3)

    full = [whole(g) for g in pair_assemble([gbufs[nm] for nm in names])]
    G = {nm: g for (nm, _), g in zip(sharded, full)}
    G["kv_w_down"] = G["kv_w_down"][0, :, :KVL + ROPE]
    G["kv_w_up"] = G["kv_w_up"][0]
    G["b_w_uq"] = G["b_w_uq"].reshape(1, QL, hpc, 256)[..., :NOPE + ROPE].reshape(b_w_uq.shape)

    small = [("ffn1_norm", jnp.stack([g_f1a, g_f1b])), ("mix_norm", jnp.stack([g_mixa, g_mixb])),
             ("ffn2_norm", jnp.stack([g_f2a, g_f2b])), ("a_rel_bias", g_rel), ("kv_norm", g_kvn),
             ("kv_latent_norm", g_lat), ("b_q_norm", g_qn[None]), ("final_norm", g_final)]
    red = all_reduce_small(_pack([loss_part] + [g for _, g in small]))
    unpacked = _unpack(red, [loss_part] + [g for _, g in small])
    loss = unpacked[0][0, 0]
    for (nm, _), g in zip(small, unpacked[1:]):
        G[nm] = g

    given = dict(ffn1_norm=(ffn1_norm, m_ffn1_norm, v_ffn1_norm), ffn1_w_in=(ffn1_w_in, m_ffn1_w_in, v_ffn1_w_in),
                 ffn1_w_out=(ffn1_w_out, m_ffn1_w_out, v_ffn1_w_out), mix_norm=(mix_norm, m_mix_norm, v_mix_norm),
                 ffn2_norm=(ffn2_norm, m_ffn2_norm, v_ffn2_norm), ffn2_w_in=(ffn2_w_in, m_ffn2_w_in, v_ffn2_w_in),
                 ffn2_w_out=(ffn2_w_out, m_ffn2_w_out, v_ffn2_w_out), a_w_qkv=(a_w_qkv, m_a_w_qkv, v_a_w_qkv),
                 a_rel_bias=(a_rel_bias, m_a_rel_bias, v_a_rel_bias), a_w_o=(a_w_o, m_a_w_o, v_a_w_o),
                 kv_norm=(kv_norm, m_kv_norm, v_kv_norm), kv_w_down=(kv_w_down, m_kv_w_down, v_kv_w_down),
                 kv_latent_norm=(kv_latent_norm, m_kv_latent_norm, v_kv_latent_norm),
                 kv_w_up=(kv_w_up, m_kv_w_up, v_kv_w_up), b_w_dq=(b_w_dq, m_b_w_dq, v_b_w_dq),
                 b_q_norm=(b_q_norm, m_b_q_norm, v_b_q_norm), b_w_uq=(b_w_uq, m_b_w_uq, v_b_w_uq),
                 b_w_o=(b_w_o, m_b_w_o, v_b_w_o), final_norm=(final_norm, m_final_norm, v_final_norm))
    order = list(given)
    delta, new_m, new_v = {}, {}, {}
    small_names = [nm for nm, _ in small]
    packed = [_pack([given[nm][k] for nm in small_names]) for k in range(3)]
    outs = adamw("adamw_small", packed[0], _pack([G[nm] for nm in small_names]), packed[1], packed[2])
    for dst, buf in zip((delta, new_m, new_v), outs):
        for nm, a in zip(small_names, _unpack(buf, [given[nm][0] for nm in small_names])):
            dst[nm] = a
    for nm, _ in sharded:
        w, m, v = given[nm]
        g = G[nm].reshape(w.shape)
        G[nm] = g
        two = lambda a: a.reshape(-1, a.shape[-1])
        d_, m_, v_ = adamw(f"adamw_{nm}", two(w), two(g), two(m), two(v))
        delta[nm], new_m[nm], new_v[nm] = d_.reshape(w.shape), m_.reshape(w.shape), v_.reshape(w.shape)

    return (loss, grad_x, *[G[n] for n in order], *[delta[n] for n in order],
            *[new_m[n] for n in order], *[new_v[n] for n in order])
```

```python
import functools
import math

import jax
import jax.numpy as jnp
from jax import lax
from jax.experimental import pallas as pl
from jax.experimental.pallas import tpu as pltpu

F32 = jnp.float32
BF16 = jnp.bfloat16
I32 = jnp.int32

CHUNK = 64
CHUNK_SHIFT = 6
HEAD_DIM_A = 64
LEFT_CHUNKS = 8
MAX_REL = 128
REL_PAD = 384
QROWS = 2 * CHUNK
WIN = (LEFT_CHUNKS + 2) * CHUNK
PADR = LEFT_CHUNKS * CHUNK
NOPE = 128
ROPE = 64
EPS = 1e-6
NEG_INF = -1e30
ROPE_THETA = 10000.0
ADAM_LR, ADAM_B1, ADAM_B2, ADAM_EPS, ADAM_WD, ADAM_STEP = 0.001, 0.9, 0.999, 1e-08, 0.01, 10
N_CHIPS = 4
LANE = 128
MESH = pl.DeviceIdType.MESH
VMEM_CAP_MB = 60

NN = (((1,), (0,)), ((), ()))
NT = (((1,), (1,)), ((), ()))
TN = (((0,), (0,)), ((), ()))


def _tile(n, pref, mult):
    t = (min(pref, n) // mult) * mult
    while t >= mult:
        if n % t == 0:
            return t
        t -= mult
    return n


def _nbytes(shape, dtype):
    return math.prod(shape) * jnp.dtype(dtype).itemsize


def hbm_call(body, *, out_shape, **kw):
    multi = isinstance(out_shape, (list, tuple))
    shapes = [pltpu.HBM(s.shape, s.dtype) for s in (out_shape if multi else [out_shape])]
    call = pl.pallas_call(body, out_shape=shapes if multi else shapes[0], **kw)

    def run(*ins):
        return call(*[pltpu.with_memory_space_constraint(v, pltpu.HBM) if jnp.issubdtype(v.dtype, jnp.floating)
                      else v for v in ins])

    return run


def _params(block_bytes, extra_bytes=0):
    need = 2 * block_bytes + extra_bytes
    mb = min(VMEM_CAP_MB, max(32, int(need * 1.25 / 2**20) + 8))
    return pltpu.CompilerParams(vmem_limit_bytes=mb * 2**20)


def _mm(name, kind, a, b, grid, a_spec, b_spec, o_spec, out_shape, out_dtype, blocks,
        red_axis=None, nred=1, alpha=1.0, res=None, res_spec=None, after=None):
    dims = {"nn": NN, "nt": NT, "tn": TN}[kind]
    has_res = res is not None
    acc_in_out = nred > 1 and out_dtype == F32 and not has_res and alpha == 1.0
    n_in = 2 + has_res + (after is not None)

    def body(*refs):
        a_ref, b_ref = refs[0], refs[1]
        r_ref = refs[2] if has_res else None
        o_ref = refs[n_in]
        p = lax.dot_general(a_ref[...].astype(BF16), b_ref[...].astype(BF16), dims,
                            preferred_element_type=F32)

        def finish(acc):
            y = acc if alpha == 1.0 else acc * alpha
            if has_res:
                y = r_ref[...] + y
            o_ref[...] = y.astype(o_ref.dtype)

        if nred == 1:
            finish(p)
            return
        k = pl.program_id(red_axis)
        tgt = o_ref if acc_in_out else refs[-1]

        @pl.when(k == 0)
        def _():
            tgt[...] = p

        @pl.when(k > 0)
        def _():
            tgt[...] += p

        if not acc_in_out:
            @pl.when(k == nred - 1)
            def _():
                finish(tgt[...])

    a_blk, b_blk, o_blk = blocks
    scratch = []
    extra = 0
    if nred > 1 and not acc_in_out:
        scratch = [pltpu.VMEM(o_blk, F32)]
        extra = _nbytes(o_blk, F32)
    blk = _nbytes(a_blk, a.dtype) + _nbytes(b_blk, b.dtype) + _nbytes(o_blk, out_dtype)
    ins, specs = [a, b], [a_spec, b_spec]
    if has_res:
        ins.append(res)
        specs.append(res_spec)
        blk += _nbytes(o_blk, res.dtype)
    if after is not None:
        ins.append(after)
        specs.append(pl.BlockSpec(memory_space=pl.ANY))
    extra += _nbytes(a_blk, BF16) + _nbytes(b_blk, BF16) + 2 * _nbytes(o_blk, F32)
    return hbm_call(
        body, name=name, grid=grid, in_specs=specs, out_specs=o_spec,
        out_shape=jax.ShapeDtypeStruct(out_shape, out_dtype), scratch_shapes=scratch,
        compiler_params=_params(blk, extra),
    )(*ins)


def mm_colw(name, x, w3, out_dtype):
    T, K = x.shape
    _, _, nl = w3.shape
    tm = _tile(T, 512, 8)
    return _mm(name, "nn", x, w3, (N_CHIPS, T // tm),
               pl.BlockSpec((tm, K), lambda j, i: (i, 0)),
               pl.BlockSpec((None, K, nl), lambda j, i: (j, 0, 0)),
               pl.BlockSpec((tm, nl), lambda j, i: (i, j)),
               (T, N_CHIPS * nl), out_dtype, ((tm, K), (K, nl), (tm, nl)))


def _pair_chip(j):
    return (j % 2) * 2 + j // 2


def mm_colw_t(name, dy, w3, out_dtype, res=None, after=None, pair_layout=False):
    T = dy.shape[0]
    _, K, nl = w3.shape
    tm = _tile(T, 512, 8)
    chip = _pair_chip if pair_layout else (lambda j: j)
    return _mm(name, "nt", dy, w3, (T // tm, N_CHIPS),
               pl.BlockSpec((tm, nl), lambda i, j: (i, j)),
               pl.BlockSpec((None, K, nl), lambda i, j: (chip(j), 0, 0)),
               pl.BlockSpec((tm, K), lambda i, j: (i, 0)),
               (T, K), out_dtype, ((tm, nl), (K, nl), (tm, K)),
               red_axis=1, nred=N_CHIPS, res=res,
               res_spec=pl.BlockSpec((tm, K), lambda i, j: (i, 0)), after=after)


def mm_dcolw(name, x, dy, after=None, pair_layout=False):
    T, K = x.shape
    nl = dy.shape[1] // N_CHIPS
    tt = _tile(T, 512, 8)
    chip = _pair_chip if pair_layout else (lambda j: j)
    return _mm(name, "tn", x, dy, (N_CHIPS, T // tt),
               pl.BlockSpec((tt, K), lambda j, t: (t, 0)),
               pl.BlockSpec((tt, nl), lambda j, t: (t, j)),
               pl.BlockSpec((None, K, nl), lambda j, t: (chip(j), 0, 0)),
               (N_CHIPS, K, nl), BF16, ((tt, K), (tt, nl), (K, nl)),
               red_axis=1, nred=T // tt, after=after)


def mm_roww(name, x, w2, out_dtype, res=None, alpha=1.0):
    T, Kt = x.shape
    N = w2.shape[1]
    tm = _tile(T, 512, 8)
    return _mm(name, "nn", x, w2, (T // tm,),
               pl.BlockSpec((tm, Kt), lambda i: (i, 0)),
               pl.BlockSpec((Kt, N), lambda i: (0, 0)),
               pl.BlockSpec((tm, N), lambda i: (i, 0)),
               (T, N), out_dtype, ((tm, Kt), (Kt, N), (tm, N)),
               alpha=alpha, res=res, res_spec=pl.BlockSpec((tm, N), lambda i: (i, 0)))


def mm_roww_t(name, dy, w2, out_dtype, alpha=1.0, after=None):
    T, N = dy.shape
    Kt = w2.shape[0]
    tm = _tile(T, 512, 8)
    tk = _tile(Kt, 1408, LANE)
    return _mm(name, "nt", dy, w2, (Kt // tk, T // tm),
               pl.BlockSpec((tm, N), lambda j, i: (i, 0)),
               pl.BlockSpec((tk, N), lambda j, i: (j, 0)),
               pl.BlockSpec((tm, tk), lambda j, i: (i, j)),
               (T, Kt), out_dtype, ((tm, N), (tk, N), (tm, tk)), alpha=alpha, after=after)


def mm_droww(name, x, dy, alpha=1.0):
    T, Kt = x.shape
    N = dy.shape[1]
    tt = _tile(T, 512, 8)
    tk = _tile(Kt, 1408, LANE)
    return _mm(name, "tn", x, dy, (Kt // tk, T // tt),
               pl.BlockSpec((tt, tk), lambda j, t: (t, j)),
               pl.BlockSpec((tt, N), lambda j, t: (t, 0)),
               pl.BlockSpec((tk, N), lambda j, t: (j, 0)),
               (Kt, N), BF16, ((tt, tk), (tt, N), (tk, N)),
               red_axis=1, nred=T // tt, alpha=alpha)


def rms_fwd(name, x, g):
    T, D = x.shape
    tm = _tile(T, 512, 8)

    def body(x_ref, g_ref, o_ref):
        xv = x_ref[...]
        r = lax.rsqrt(jnp.mean(xv * xv, axis=-1, keepdims=True) + EPS)
        o_ref[...] = (xv * r * g_ref[...]).astype(o_ref.dtype)

    return hbm_call(
        body, name=name, grid=(T // tm,),
        in_specs=[pl.BlockSpec((tm, D), lambda i: (i, 0)), pl.BlockSpec((1, D), lambda i: (0, 0))],
        out_specs=pl.BlockSpec((tm, D), lambda i: (i, 0)),
        out_shape=jax.ShapeDtypeStruct((T, D), BF16),
        compiler_params=_params(_nbytes((tm, D), F32) * 2, 4 * _nbytes((tm, D), F32)),
    )(x, g.reshape(1, D))


def _rms_bwd_math(xv, gv, dy):
    r = lax.rsqrt(jnp.mean(xv * xv, axis=-1, keepdims=True) + EPS)
    xh = xv * r
    dyg = dy * gv
    dx = r * (dyg - xh * jnp.mean(dyg * xh, axis=-1, keepdims=True))
    dg = jnp.sum(dy * xh, axis=0, keepdims=True)
    return dx, dg


def rms_bwd(name, x, g, dy, dres=None):
    T, D = x.shape
    tm = _tile(T, 256, 8)
    has_res = dres is not None

    def body(*refs):
        x_ref, g_ref, dy_ref = refs[:3]
        r_ref = refs[3] if has_res else None
        dx_ref, dg_ref = refs[-2:]
        dx, dg = _rms_bwd_math(x_ref[...], g_ref[...], dy_ref[...].astype(F32))
        if has_res:
            dx = r_ref[...] + dx
        dx_ref[...] = dx

        @pl.when(pl.program_id(0) == 0)
        def _():
            dg_ref[...] = dg

        @pl.when(pl.program_id(0) > 0)
        def _():
            dg_ref[...] += dg

    row = pl.BlockSpec((tm, D), lambda i: (i, 0))
    vec = pl.BlockSpec((1, D), lambda i: (0, 0))
    ins, specs = [x, g.reshape(1, D), dy], [row, vec, row]
    if has_res:
        ins.append(dres)
        specs.append(row)
    dx, dg = hbm_call(
        body, name=name, grid=(T // tm,), in_specs=specs, out_specs=[row, vec],
        out_shape=[jax.ShapeDtypeStruct((T, D), F32), jax.ShapeDtypeStruct((1, D), F32)],
        compiler_params=_params(_nbytes((tm, D), F32) * 4, 6 * _nbytes((tm, D), F32)),
    )(*ins)
    return dx, dg.reshape(D)


def ffn_in_act(name, x, w3):
    T, K = x.shape
    _, _, nl = w3.shape
    tm = _tile(T, 512, 8)

    def body(x_ref, wg_ref, wu_ref, u_ref, a_ref):
        xv = x_ref[...]
        g = jnp.dot(xv, wg_ref[...], preferred_element_type=F32)
        up = jnp.dot(xv, wu_ref[...], preferred_element_type=F32)
        u_ref[:, :nl] = g.astype(u_ref.dtype)
        u_ref[:, nl:] = up.astype(u_ref.dtype)
        a_ref[...] = (g * jax.nn.sigmoid(g) * up).astype(a_ref.dtype)

    blk = _nbytes((tm, K), BF16) + 2 * _nbytes((K, nl), BF16) + _nbytes((tm, 3 * nl), BF16)
    return hbm_call(
        body, name=name, grid=(2, T // tm),
        in_specs=[pl.BlockSpec((tm, K), lambda p, i: (i, 0)),
                  pl.BlockSpec((None, K, nl), lambda p, i: (p, 0, 0)),
                  pl.BlockSpec((None, K, nl), lambda p, i: (p + 2, 0, 0))],
        out_specs=[pl.BlockSpec((tm, 2 * nl), lambda p, i: (i, p)), pl.BlockSpec((tm, nl), lambda p, i: (i, p))],
        out_shape=[jax.ShapeDtypeStruct((T, 4 * nl), BF16), jax.ShapeDtypeStruct((T, 2 * nl), BF16)],
        compiler_params=_params(blk, 4 * _nbytes((tm, nl), F32)),
    )(x, w3, w3)


def ffn_dact(name, dh, w_out, u, after=None):
    T, N = dh.shape
    F = w_out.shape[0]
    nl = F // 2
    tm = _tile(T, 512, 8)

    def body(*refs):
        d_ref, w_ref, u_ref = refs[:3]
        o_ref = refs[-1]
        dact = 0.5 * lax.dot_general(d_ref[...].astype(BF16), w_ref[...], NT, preferred_element_type=F32)
        g = u_ref[:, :nl].astype(F32)
        up = u_ref[:, nl:].astype(F32)
        sig = jax.nn.sigmoid(g)
        o_ref[:, :nl] = (dact * up * (sig * (1.0 + g * (1.0 - sig)))).astype(o_ref.dtype)
        o_ref[:, nl:] = (dact * (g * sig)).astype(o_ref.dtype)

    ins = [dh, w_out, u]
    specs = [pl.BlockSpec((tm, N), lambda p, i: (i, 0)), pl.BlockSpec((nl, N), lambda p, i: (p, 0)),
             pl.BlockSpec((tm, 2 * nl), lambda p, i: (i, p))]
    if after is not None:
        ins.append(after)
        specs.append(pl.BlockSpec(memory_space=pl.ANY))
    blk = _nbytes((tm, N), F32) + _nbytes((nl, N), BF16) + 2 * _nbytes((tm, 2 * nl), BF16)
    return hbm_call(
        body, name=name, grid=(2, T // tm), in_specs=specs,
        out_specs=pl.BlockSpec((tm, 2 * nl), lambda p, i: (i, p)),
        out_shape=jax.ShapeDtypeStruct((T, 2 * F), BF16),
        compiler_params=_params(blk, 6 * _nbytes((tm, nl), F32)),
    )(*ins)


def loss_head(name, h, g, target):
    T, D = h.shape
    tm = _tile(T, 256, 8)

    def body(h_ref, g_ref, t_ref, dh_ref, dg_ref, loss_ref):
        xv = h_ref[...]
        gv = g_ref[...]
        r = lax.rsqrt(jnp.mean(xv * xv, axis=-1, keepdims=True) + EPS)
        err = xv * r * gv - t_ref[...]
        part = 0.5 * jnp.sum(jnp.mean(err * err, axis=-1, keepdims=True), axis=0, keepdims=True)
        dx, dg = _rms_bwd_math(xv, gv, err * (1.0 / D))
        dh_ref[...] = dx
        part = jnp.broadcast_to(part, (1, LANE))

        @pl.when(pl.program_id(0) == 0)
        def _():
            dg_ref[...] = dg
            loss_ref[...] = part

        @pl.when(pl.program_id(0) > 0)
        def _():
            dg_ref[...] += dg
            loss_ref[...] += part

    row = pl.BlockSpec((tm, D), lambda i: (i, 0))
    vec = pl.BlockSpec((1, D), lambda i: (0, 0))
    dh, dg, loss = hbm_call(
        body, name=name, grid=(T // tm,), in_specs=[row, vec, row],
        out_specs=[row, vec, pl.BlockSpec((1, LANE), lambda i: (0, 0))],
        out_shape=[jax.ShapeDtypeStruct((T, D), F32), jax.ShapeDtypeStruct((1, D), F32),
                   jax.ShapeDtypeStruct((1, LANE), F32)],
        compiler_params=_params(_nbytes((tm, D), F32) * 3, 6 * _nbytes((tm, D), F32)),
    )(h, g.reshape(1, D), target)
    return dh, dg.reshape(D), loss


def rope_tables(S):
    half = ROPE // 2
    freqs = ROPE_THETA ** (-jnp.arange(half, dtype=F32) / half)
    ang = jnp.arange(S, dtype=F32)[:, None] * freqs[None, :]
    cos, sin = jnp.cos(ang), jnp.sin(ang)
    z = jnp.zeros_like(cos)
    ct = jnp.concatenate([cos, cos, z, z], axis=1)
    s1 = jnp.concatenate([-sin, z, z, z], axis=1)
    s2 = jnp.concatenate([z, sin, z, z], axis=1)
    return ct, s1, s2


def _rope_tile(t, ct, s1, s2):
    return t * ct + pltpu.roll(t, 96, 1) * s1 + pltpu.roll(t, 32, 1) * s2


def _rope_tile_bwd(d, ct, s1, s2):
    return d * ct + pltpu.roll(d * s1, 32, 1) + pltpu.roll(d * s2, 96, 1)


def qprep(name, q, tabs, B, S, bwd):
    T, W = q.shape
    nh = W // 256
    ts = _tile(S, 256, 8)
    fn = _rope_tile_bwd if bwd else _rope_tile

    def body(q_ref, ct_ref, s1_ref, s2_ref, o_ref):
        ct, s1, s2 = ct_ref[...], s1_ref[...], s2_ref[...]
        for h in range(nh):
            o_ref[0, :, 256 * h:256 * h + 128] = q_ref[0, :, 256 * h:256 * h + 128].astype(o_ref.dtype)
            t = q_ref[0, :, 256 * h + 128:256 * h + 256].astype(F32)
            o_ref[0, :, 256 * h + 128:256 * h + 256] = fn(t, ct, s1, s2).astype(o_ref.dtype)

    row = pl.BlockSpec((1, ts, W), lambda b, s: (b, s, 0))
    tab = pl.BlockSpec((ts, LANE), lambda b, s: (s, 0))
    out = hbm_call(
        body, name=name, grid=(B, S // ts), in_specs=[row, tab, tab, tab], out_specs=row,
        out_shape=jax.ShapeDtypeStruct((B, S, W), BF16),
        compiler_params=_params(_nbytes((ts, W), F32) * 2, _nbytes((ts, W), F32) * 2),
    )(q.reshape(B, S, W), *tabs)
    return out.reshape(T, W)


def kvprep_fwd(name, ckr, g, tabs, B, S):
    T, W = ckr.shape
    KVL = W - LANE
    ts = _tile(S, 256, 8)

    def body(x_ref, g_ref, ct_ref, s1_ref, s2_ref, c_ref, k_ref):
        xv = x_ref[0, :, :KVL]
        r = lax.rsqrt(jnp.mean(xv * xv, axis=-1, keepdims=True) + EPS)
        c_ref[0] = (xv * r * g_ref[...]).astype(c_ref.dtype)
        k_ref[0] = _rope_tile(x_ref[0, :, KVL:], ct_ref[...], s1_ref[...], s2_ref[...]).astype(k_ref.dtype)

    tab = pl.BlockSpec((ts, LANE), lambda b, s: (s, 0))
    c, k = hbm_call(
        body, name=name, grid=(B, S // ts),
        in_specs=[pl.BlockSpec((1, ts, W), lambda b, s: (b, s, 0)), pl.BlockSpec((1, KVL), lambda b, s: (0, 0)),
                  tab, tab, tab],
        out_specs=[pl.BlockSpec((1, ts, KVL), lambda b, s: (b, s, 0)),
                   pl.BlockSpec((1, ts, LANE), lambda b, s: (b, s, 0))],
        out_shape=[jax.ShapeDtypeStruct((B, S, KVL), BF16), jax.ShapeDtypeStruct((B, S, LANE), BF16)],
        compiler_params=_params(_nbytes((ts, W), F32) * 2, _nbytes((ts, W), F32) * 2),
    )(ckr.reshape(B, S, W), g.reshape(1, KVL), *tabs)
    return c.reshape(T, KVL), k


def kvprep_bwd(name, ckr, g, dc, dkr, tabs, B, S):
    T, W = ckr.shape
    KVL = W - LANE
    ts = _tile(S, 256, 8)

    def body(x_ref, g_ref, dc_ref, dk_ref, ct_ref, s1_ref, s2_ref, o_ref, dg_ref):
        dx, dg = _rms_bwd_math(x_ref[0, :, :KVL], g_ref[...], dc_ref[0])
        o_ref[0, :, :KVL] = dx
        o_ref[0, :, KVL:] = _rope_tile_bwd(dk_ref[0], ct_ref[...], s1_ref[...], s2_ref[...])
        first = (pl.program_id(0) == 0) & (pl.program_id(1) == 0)

        @pl.when(first)
        def _():
            dg_ref[...] = dg

        @pl.when(jnp.logical_not(first))
        def _():
            dg_ref[...] += dg

    tab = pl.BlockSpec((ts, LANE), lambda b, s: (s, 0))
    vec = pl.BlockSpec((1, KVL), lambda b, s: (0, 0))
    o, dg = hbm_call(
        body, name=name, grid=(B, S // ts),
        in_specs=[pl.BlockSpec((1, ts, W), lambda b, s: (b, s, 0)), vec,
                  pl.BlockSpec((1, ts, KVL), lambda b, s: (b, s, 0)),
                  pl.BlockSpec((1, ts, LANE), lambda b, s: (b, s, 0)), tab, tab, tab],
        out_specs=[pl.BlockSpec((1, ts, W), lambda b, s: (b, s, 0)), vec],
        out_shape=[jax.ShapeDtypeStruct((B, S, W), F32), jax.ShapeDtypeStruct((1, KVL), F32)],
        compiler_params=_params(_nbytes((ts, W), F32) * 4, _nbytes((ts, W), F32) * 4),
    )(ckr.reshape(B, S, W), g.reshape(1, KVL), dc.reshape(B, S, KVL), dkr, *tabs)
    return o.reshape(T, W), dg.reshape(KVL)


DIAGS = 768


def _diag_onehot():
    col = lax.broadcasted_iota(I32, (REL_PAD, DIAGS), 1)
    row = lax.broadcasted_iota(I32, (REL_PAD, DIAGS), 0)
    idx = jnp.clip(PADR + QROWS - 1 - col, -MAX_REL, MAX_REL) + MAX_REL
    return (row == idx).astype(F32)


def rel_bias_tile(name, table):
    H = table.shape[0]
    tpad = jnp.pad(table, ((0, 0), (0, REL_PAD - table.shape[1])))

    def body(t_ref, o_ref):
        g = lax.dot_general(t_ref[...], _diag_onehot(), NN, precision=lax.Precision.HIGHEST,
                            preferred_element_type=F32)
        for h in range(H):
            gb = jnp.broadcast_to(g[h:h + 1, :], (QROWS, DIAGS))
            tile = pltpu.roll(gb, DIAGS - (QROWS - 1), 1, stride=1, stride_axis=0)
            o_ref[h // 2, (h % 2) * QROWS:(h % 2 + 1) * QROWS, :] = tile[:, :WIN]

    return hbm_call(
        body, name=name, out_shape=jax.ShapeDtypeStruct((H // 2, 2 * QROWS, WIN), F32),
        compiler_params=_params(0, 2 * _nbytes((H // 2, 2 * QROWS, WIN), F32)),
    )(tpad)


def rel_bias_grad(name, dbias):
    H = 2 * dbias.shape[0]

    def body(d_ref, o_ref):
        flip = (lax.broadcasted_iota(I32, (QROWS, QROWS), 0) + lax.broadcasted_iota(I32, (QROWS, QROWS), 1)
                == QROWS - 1).astype(F32)
        rows = []
        for h in range(H):
            x = d_ref[h // 2, (h % 2) * QROWS:(h % 2 + 1) * QROWS, :]
            xr = lax.dot_general(flip, x, NN, precision=lax.Precision.HIGHEST, preferred_element_type=F32)
            xp = jnp.concatenate([xr, jnp.zeros((QROWS, DIAGS - WIN), F32)], axis=1)
            y = pltpu.roll(xp, 0, 1, stride=1, stride_axis=0)
            rows.append(jnp.sum(y, axis=0, keepdims=True))
        o_ref[...] = lax.dot_general(jnp.concatenate(rows, axis=0), _diag_onehot(), NT,
                                     precision=lax.Precision.HIGHEST, preferred_element_type=F32)

    return hbm_call(
        body, name=name, out_shape=jax.ShapeDtypeStruct((H, REL_PAD), F32),
        compiler_params=_params(0, 2 * _nbytes(dbias.shape, F32)),
    )(dbias)


def _stack_pair(xp):
    lane = lax.broadcasted_iota(I32, xp.shape, 1)
    z = jnp.zeros_like(xp)
    return jnp.concatenate([jnp.where(lane < HEAD_DIM_A, xp, z), jnp.where(lane >= HEAD_DIM_A, xp, z)], axis=0)


def _unstack_pair(y):
    lane = lax.broadcasted_iota(I32, (QROWS, LANE), 1)
    return jnp.where(lane < HEAD_DIM_A, y[:QROWS], y[QROWS:])


def _attn_a_mask(j):
    r = lax.broadcasted_iota(I32, (2 * QROWS, WIN), 0)
    w = lax.broadcasted_iota(I32, (2 * QROWS, WIN), 1)
    qc = jnp.right_shift(jnp.bitwise_and(r, QROWS - 1), CHUNK_SHIFT)
    kc = jnp.right_shift(w, CHUNK_SHIFT)
    return (kc >= qc) & (kc <= qc + LEFT_CHUNKS) & (w >= PADR - QROWS * j)


def _attn_a_load_bias(bias_hbm, bias_v, sem):
    cp = pltpu.make_async_copy(bias_hbm, bias_v, sem)
    cp.start()
    cp.wait()


def _attn_a_load_kv(qkv_hbm, b, kpad, vpad, sem, S, D):
    kpad[0:PADR, :] = jnp.zeros((PADR, D), BF16)
    vpad[0:PADR, :] = jnp.zeros((PADR, D), BF16)
    ck = pltpu.make_async_copy(qkv_hbm.at[b, :, pl.ds(D, D)], kpad.at[pl.ds(PADR, S), :], sem.at[0])
    cv = pltpu.make_async_copy(qkv_hbm.at[b, :, pl.ds(2 * D, D)], vpad.at[pl.ds(PADR, S), :], sem.at[1])
    ck.start()
    cv.start()
    ck.wait()
    cv.wait()


def _attn_a_probs(qm, kp, bias, valid, scale):
    s = lax.dot_general(qm, kp, NT, preferred_element_type=F32) * scale + bias
    s = jnp.where(valid, s, NEG_INF)
    e = jnp.exp(s - jnp.max(s, axis=-1, keepdims=True))
    return e * (1.0 / jnp.sum(e, axis=-1, keepdims=True))


def attn_a_fwd(name, qkv, bias):
    B, S, D3 = qkv.shape
    D = D3 // 3
    H = D // HEAD_DIM_A
    nb = S // QROWS
    scale = HEAD_DIM_A ** -0.5

    def body(q_ref, bias_hbm, qkv_hbm, o_ref, kpad, vpad, bias_v, sem):
        b, j = pl.program_id(0), pl.program_id(1)

        @pl.when((b == 0) & (j == 0))
        def _():
            _attn_a_load_bias(bias_hbm, bias_v, sem.at[2])

        @pl.when(j == 0)
        def _():
            _attn_a_load_kv(qkv_hbm, b, kpad, vpad, sem, S, D)

        mask = _attn_a_mask(j)
        w0 = pl.multiple_of(j * QROWS, QROWS)
        for p in range(H // 2):
            ls = slice(p * LANE, (p + 1) * LANE)
            pr = _attn_a_probs(_stack_pair(q_ref[0, :, ls]), kpad[pl.ds(w0, WIN), ls], bias_v[p], mask, scale)
            o2 = jnp.dot(pr.astype(BF16), vpad[pl.ds(w0, WIN), ls], preferred_element_type=F32)
            o_ref[0, :, ls] = _unstack_pair(o2).astype(o_ref.dtype)

    scr = 2 * _nbytes((PADR + S, D), BF16) + _nbytes(bias.shape, F32) + 8 * _nbytes((2 * QROWS, WIN), F32)
    return hbm_call(
        body, name=name, grid=(B, nb),
        in_specs=[pl.BlockSpec((1, QROWS, D), lambda b, j: (b, j, 0)),
                  pl.BlockSpec(memory_space=pl.ANY), pl.BlockSpec(memory_space=pl.ANY)],
        out_specs=pl.BlockSpec((1, QROWS, D), lambda b, j: (b, j, 0)),
        out_shape=jax.ShapeDtypeStruct((B, S, D), BF16),
        scratch_shapes=[pltpu.VMEM((PADR + S, D), BF16), pltpu.VMEM((PADR + S, D), BF16),
                        pltpu.VMEM(bias.shape, F32), pltpu.SemaphoreType.DMA((3,))],
        compiler_params=_params(2 * _nbytes((QROWS, D), BF16), scr),
    )(qkv, bias, qkv)


def attn_a_bwd(name, qkv, do, bias):
    B, S, D3 = qkv.shape
    D = D3 // 3
    H = D // HEAD_DIM_A
    nb = S // QROWS
    scale = HEAD_DIM_A ** -0.5

    def body(q_ref, do_ref, bias_hbm, qkv_hbm, dq_ref, dkv_hbm, dbias_hbm, kpad, vpad, dkacc, dvacc, bias_v, dbias_v, sem):
        b, j = pl.program_id(0), pl.program_id(1)

        @pl.when((b == 0) & (j == 0))
        def _():
            _attn_a_load_bias(bias_hbm, bias_v, sem.at[2])
            dbias_v[...] = jnp.zeros_like(dbias_v)

        @pl.when(j == 0)
        def _():
            _attn_a_load_kv(qkv_hbm, b, kpad, vpad, sem, S, D)
            dkacc[...] = jnp.zeros_like(dkacc)
            dvacc[...] = jnp.zeros_like(dvacc)

        mask = _attn_a_mask(j)
        w0 = pl.multiple_of(j * QROWS, QROWS)
        for p in range(H // 2):
            ls = slice(p * LANE, (p + 1) * LANE)
            q2 = _stack_pair(q_ref[0, :, ls])
            do2 = _stack_pair(do_ref[0, :, ls])
            kp = kpad[pl.ds(w0, WIN), ls]
            vp = vpad[pl.ds(w0, WIN), ls]
            pr = _attn_a_probs(q2, kp, bias_v[p], mask, scale)
            dp = lax.dot_general(do2, vp, NT, preferred_element_type=F32)
            ds = pr * (dp - jnp.sum(pr * dp, axis=-1, keepdims=True))
            dbias_v[p] += ds
            dsb = (ds * scale).astype(BF16)
            dq_ref[0, :, ls] = _unstack_pair(jnp.dot(dsb, kp, preferred_element_type=F32))
            dkacc[pl.ds(w0, WIN), ls] += lax.dot_general(dsb, q2, TN, preferred_element_type=F32)
            dvacc[pl.ds(w0, WIN), ls] += lax.dot_general(pr.astype(BF16), do2, TN, preferred_element_type=F32)

        @pl.when(j == nb - 1)
        def _():
            ck = pltpu.make_async_copy(dkacc.at[pl.ds(PADR, S), :], dkv_hbm.at[b, :, pl.ds(0, D)], sem.at[0])
            cv = pltpu.make_async_copy(dvacc.at[pl.ds(PADR, S), :], dkv_hbm.at[b, :, pl.ds(D, D)], sem.at[1])
            ck.start()
            cv.start()
            ck.wait()
            cv.wait()

        @pl.when((b == B - 1) & (j == nb - 1))
        def _():
            cb = pltpu.make_async_copy(dbias_v, dbias_hbm, sem.at[2])
            cb.start()
            cb.wait()

    blk = _nbytes((QROWS, D), BF16) * 2 + _nbytes((QROWS, D), F32)
    scr = (2 * _nbytes((PADR + S, D), BF16) + 2 * _nbytes((PADR + S, D), F32) + 2 * _nbytes(bias.shape, F32)
           + 8 * _nbytes((2 * QROWS, WIN), F32))
    return hbm_call(
        body, name=name, grid=(B, nb),
        in_specs=[pl.BlockSpec((1, QROWS, D), lambda b, j: (b, j, 0)),
                  pl.BlockSpec((1, QROWS, D), lambda b, j: (b, j, 0)),
                  pl.BlockSpec(memory_space=pl.ANY), pl.BlockSpec(memory_space=pl.ANY)],
        out_specs=[pl.BlockSpec((1, QROWS, D), lambda b, j: (b, j, 0)),
                   pl.BlockSpec(memory_space=pl.ANY), pl.BlockSpec(memory_space=pl.ANY)],
        out_shape=[jax.ShapeDtypeStruct((B, S, D), F32), jax.ShapeDtypeStruct((B, S, 2 * D), F32),
                   jax.ShapeDtypeStruct(bias.shape, F32)],
        scratch_shapes=[pltpu.VMEM((PADR + S, D), BF16), pltpu.VMEM((PADR + S, D), BF16),
                        pltpu.VMEM((PADR + S, D), F32), pltpu.VMEM((PADR + S, D), F32),
                        pltpu.VMEM(bias.shape, F32), pltpu.VMEM(bias.shape, F32),
                        pltpu.SemaphoreType.DMA((3,))],
        compiler_params=_params(blk, scr),
    )(qkv, do, bias, qkv)


def _mla_raw_t(k2, kj, q, QB):
    return lax.dot_general(k2[_blk(kj, QB), :], q, NT, preferred_element_type=F32)


def _blk(kj, QB):
    return pl.ds(kj * QB, QB) if isinstance(kj, int) else pl.ds(pl.multiple_of(kj * QB, QB), QB)


def _mla_mask_t(shape, q0, k0):
    kc = jnp.right_shift(k0 + lax.broadcasted_iota(I32, shape, 0), CHUNK_SHIFT)
    qc = jnp.right_shift(q0 + lax.broadcasted_iota(I32, shape, 1), CHUNK_SHIFT)
    return kc <= qc


def _mla_fill_keys(kv_ref, kr_ref, k2):
    k2[:, :NOPE] = kv_ref[0, :, :NOPE]
    k2[:, NOPE:] = kr_ref[0]


def _t(x):
    return x.astype(F32).T


def mla_fwd(name, qf, kv, kr):
    B, S, W = qf.shape
    HB = W // 256
    QB = _tile(S, 256, CHUNK)
    nq = S // QB
    scale = (NOPE + ROPE) ** -0.5

    def body(q_ref, kv_ref, kr_ref, o_ref, lse_ref, k2, vt, st_buf):
        qi = pl.program_id(2)

        @pl.when(qi == 0)
        def _():
            _mla_fill_keys(kv_ref, kr_ref, k2)
            for kj in range(nq):
                vt[kj] = _t(kv_ref[0, kj * QB:(kj + 1) * QB, NOPE:]).astype(BF16)

        q = q_ref[0]
        st_buf[0] = _mla_raw_t(k2, 0, q, QB)

        def step(kj, carry):
            m, l, acc = carry
            cur = lax.rem(kj, 2)
            st_raw = st_buf[cur]
            st_buf[1 - cur] = _mla_raw_t(k2, jnp.minimum(kj + 1, qi), q, QB)
            st = jnp.where(_mla_mask_t((QB, QB), qi * QB, kj * QB), st_raw * scale, NEG_INF)
            m_new = jnp.maximum(m, jnp.max(st, axis=0, keepdims=True))
            a = jnp.exp(m - m_new)
            pt = jnp.exp(st - m_new)
            l = a * l + jnp.sum(pt, axis=0, keepdims=True)
            acc = a * acc + jnp.dot(vt[kj], pt.astype(BF16), preferred_element_type=F32)
            return m_new, l, acc

        init = (jnp.full((1, QB), NEG_INF, F32), jnp.zeros((1, QB), F32), jnp.zeros((NOPE, QB), F32))
        m, l, acc = lax.fori_loop(0, qi + 1, step, init)
        o_ref[0] = (acc * (1.0 / l)).T
        lse_ref[0, 0] = m + jnp.log(l)

    blk = (_nbytes((QB, 256), BF16) + _nbytes((S, 256), BF16) + _nbytes((S, LANE), BF16)
           + _nbytes((QB, LANE), F32))
    return hbm_call(
        body, name=name, grid=(B, HB, nq),
        in_specs=[pl.BlockSpec((1, QB, 256), lambda b, h, i: (b, i, h)),
                  pl.BlockSpec((1, S, 256), lambda b, h, i: (b, 0, h)),
                  pl.BlockSpec((1, S, LANE), lambda b, h, i: (b, 0, 0))],
        out_specs=[pl.BlockSpec((1, QB, LANE), lambda b, h, i: (b, i, h)),
                   pl.BlockSpec((1, 1, 1, QB), lambda b, h, i: (b, h, 0, i))],
        out_shape=[jax.ShapeDtypeStruct((B, S, HB * LANE), F32), jax.ShapeDtypeStruct((B, HB, 1, S), F32)],
        scratch_shapes=[pltpu.VMEM((S, 256), BF16), pltpu.VMEM((nq, NOPE, QB), BF16),
                        pltpu.VMEM((2, QB, QB), F32)],
        compiler_params=_params(blk, 2 * _nbytes((S, 256), BF16) + 10 * _nbytes((QB, QB), F32)),
    )(qf, kv, kr)


def mla_bwd(name, qf, kv, kr, do, o, lse):
    B, S, W = qf.shape
    HB = W // 256
    QB = _tile(S, 256, CHUNK)
    nq = S // QB
    scale = (NOPE + ROPE) ** -0.5

    def body(q_ref, kv_ref, kr_ref, do_ref, o_ref, lse_ref, dq_ref, dkv_ref, dkr_ref, k2, kt, dot_, delta, dqt,
             st_buf, dp_buf):
        h = pl.program_id(1)
        dkv_ref[...] = jnp.zeros_like(dkv_ref)

        @pl.when(h == 0)
        def _():
            dkr_ref[...] = jnp.zeros_like(dkr_ref)

        _mla_fill_keys(kv_ref, kr_ref, k2)
        for i in range(nq):
            rows = slice(i * QB, (i + 1) * QB)
            kt[i] = _t(k2[rows, :]).astype(BF16)
            dot32 = _t(do_ref[0, rows, :])
            delta[i] = jnp.sum(dot32 * o_ref[0, rows, :].T, axis=0, keepdims=True)
            dot_[i] = dot32.astype(BF16)

        for qi in range(nq):
            rows = slice(qi * QB, (qi + 1) * QB)
            q = q_ref[0, rows, :]
            dob = do_ref[0, rows, :]
            lse_q = lse_ref[0, 0, :, rows]
            delta_q = delta[qi]
            dqt[...] = jnp.zeros_like(dqt)

            def raw(kj, slot, q=q, qi=qi):
                st_buf[slot] = _mla_raw_t(k2, kj, q, QB)
                dp_buf[slot] = jnp.dot(kv_ref[0, _blk(kj, QB), NOPE:], dot_[qi], preferred_element_type=F32)

            raw(0, 0)

            def step(kj, carry, q=q, dob=dob, lse_q=lse_q, delta_q=delta_q, qi=qi, raw=raw):
                ks = pl.ds(pl.multiple_of(kj * QB, QB), QB)
                cur = lax.rem(kj, 2)
                st_raw, dp_raw = st_buf[cur], dp_buf[cur]
                raw(jnp.minimum(kj + 1, qi), 1 - cur)
                mask = _mla_mask_t((QB, QB), qi * QB, kj * QB)
                pt = jnp.where(mask, jnp.exp(st_raw * scale - lse_q), 0.0)
                dst = (pt * (dp_raw - delta_q) * scale).astype(BF16)
                dkv_ref[0, ks, NOPE:] += jnp.dot(pt.astype(BF16), dob, preferred_element_type=F32)
                dk2 = jnp.dot(dst, q, preferred_element_type=F32)
                dkv_ref[0, ks, :NOPE] += dk2[:, :NOPE]
                dkr_ref[0, ks, :] += dk2[:, NOPE:]
                dqt[...] += jnp.dot(kt[kj], dst, preferred_element_type=F32)
                return carry

            lax.fori_loop(0, qi + 1, step, 0)
            dq_ref[0, rows, :] = dqt[...].T

    head = lambda w: pl.BlockSpec((1, S, w), lambda b, h: (b, 0, h))
    shared = pl.BlockSpec((1, S, LANE), lambda b, h: (b, 0, 0))
    blk = (2 * _nbytes((S, 256), BF16) + 2 * _nbytes((S, LANE), BF16) + _nbytes((S, LANE), F32)
           + 2 * _nbytes((S, 256), F32) + _nbytes((S, LANE), F32))
    scr = 3 * _nbytes((S, 256), BF16) + 14 * _nbytes((QB, QB), F32)
    return hbm_call(
        body, name=name, grid=(B, HB),
        in_specs=[head(256), head(256), shared, head(LANE), head(LANE),
                  pl.BlockSpec((1, 1, 1, S), lambda b, h: (b, h, 0, 0))],
        out_specs=[head(256), head(256), shared],
        out_shape=[jax.ShapeDtypeStruct((B, S, W), F32), jax.ShapeDtypeStruct((B, S, W), F32),
                   jax.ShapeDtypeStruct((B, S, LANE), F32)],
        scratch_shapes=[pltpu.VMEM((S, 256), BF16), pltpu.VMEM((nq, 256, QB), BF16),
                        pltpu.VMEM((nq, NOPE, QB), BF16), pltpu.VMEM((nq, 1, QB), F32),
                        pltpu.VMEM((256, QB), F32), pltpu.VMEM((2, QB, QB), F32), pltpu.VMEM((2, QB, QB), F32)],
        compiler_params=_params(blk, scr),
    )(qf, kv, kr, do, o, lse)


def cast_bf16(name, w, layer, idx):
    _, R, C = w.shape
    tr = _tile(R, 256, 16)

    def body(k_ref, w_ref, o_ref):
        o_ref[...] = w_ref[...].astype(BF16)

    return hbm_call(
        body, name=name,
        grid_spec=pltpu.PrefetchScalarGridSpec(
            num_scalar_prefetch=1, grid=(R // tr,),
            in_specs=[pl.BlockSpec((None, tr, C), lambda r, k_ref: (layer, r, 0))],
            out_specs=pl.BlockSpec((None, tr, C), lambda r, k_ref: (k_ref[0], r, 0))),
        out_shape=jax.ShapeDtypeStruct((N_CHIPS, R, C), BF16),
    )(idx, w)


def adamw(name, w, g, m, v):
    R, C = w.shape
    tr = _tile(R, max(8, (1 << 18) // C // 8 * 8), 8)
    c1 = 1.0 - ADAM_B1 ** ADAM_STEP
    c2 = 1.0 - ADAM_B2 ** ADAM_STEP

    def body(w_ref, g_ref, m_ref, v_ref, d_ref, mo_ref, vo_ref):
        gv = g_ref[...]
        mn = ADAM_B1 * m_ref[...] + (1.0 - ADAM_B1) * gv
        vn = ADAM_B2 * v_ref[...] + (1.0 - ADAM_B2) * (gv * gv)
        mo_ref[...] = mn
        vo_ref[...] = vn
        d_ref[...] = -ADAM_LR * ((mn / c1) / (jnp.sqrt(vn / c2) + ADAM_EPS) + ADAM_WD * w_ref[...])

    spec = pl.BlockSpec((tr, C), lambda r: (r, 0))
    return hbm_call(
        body, name=name, grid=(R // tr,), in_specs=[spec] * 4, out_specs=[spec] * 3,
        out_shape=[jax.ShapeDtypeStruct((R, C), F32)] * 3,
        compiler_params=_params(7 * _nbytes((tr, C), F32), 4 * _nbytes((tr, C), F32)),
    )(w, g, m, v)


def half_sum(name, dw, landed, idx):
    _, _, hr, C = dw.shape
    tr = _tile(hr, max(16, (1 << 18) // C // 16 * 16), 16)

    def body(i_ref, a_ref, b_ref, o_ref):
        o_ref[...] = (a_ref[...].astype(F32) + b_ref[...].astype(F32)).astype(o_ref.dtype)

    return hbm_call(
        body, name=name,
        grid_spec=pltpu.PrefetchScalarGridSpec(
            num_scalar_prefetch=1, grid=(N_CHIPS, hr // tr),
            in_specs=[pl.BlockSpec((None, None, tr, C), lambda k, r, i_ref: (k, i_ref[1], r, 0)),
                      pl.BlockSpec((None, tr, C), lambda k, r, i_ref: (k, r, 0))],
            out_specs=pl.BlockSpec((None, tr, C), lambda k, r, i_ref: (k, r, 0))),
        out_shape=jax.ShapeDtypeStruct((N_CHIPS, hr, C), BF16),
    )(idx, dw, landed)


def chip_sum(name, part, landed, gbuf, layer, idx):
    _, hr, C = part.shape
    tr = _tile(hr, max(16, (1 << 18) // C // 16 * 16), 16)

    def body(i_ref, a_ref, b_ref, g_ref, o_ref):
        o_ref[...] = ((a_ref[...].astype(F32) + b_ref[0].astype(F32)) + b_ref[1].astype(F32)) + b_ref[2].astype(F32)

    return hbm_call(
        body, name=name,
        grid_spec=pltpu.PrefetchScalarGridSpec(
            num_scalar_prefetch=1, grid=(hr // tr,),
            in_specs=[pl.BlockSpec((None, tr, C), lambda r, i_ref: (i_ref[0], r, 0)),
                      pl.BlockSpec((3, tr, C), lambda r, i_ref: (0, r, 0)),
                      pl.BlockSpec(memory_space=pl.ANY)],
            out_specs=pl.BlockSpec((None, None, tr, C), lambda r, i_ref: (layer, i_ref[1], r, 0))),
        out_shape=jax.ShapeDtypeStruct(gbuf.shape, F32),
        input_output_aliases={3: 0},
    )(idx, part, landed, gbuf)


ANY = pl.BlockSpec(memory_space=pl.ANY)


def _place():
    x, y, c = lax.axis_index("x"), lax.axis_index("y"), lax.axis_index("c")
    chips = [(1 - x, y), (x, 1 - y), (1 - x, 1 - y)]
    return x, y, c, chips


HBM = pl.BlockSpec(memory_space=pltpu.HBM)
SEM = pl.BlockSpec(memory_space=pltpu.SEMAPHORE)
EFFECT = pltpu.SideEffectType.DATAFLOW_SIDE_EFFECTING


def _in_hbm(a):
    return pltpu.with_memory_space_constraint(a, pltpu.HBM)


def _ici_copy(src, dst, send_sems, recv_sems, k, peer):
    return pltpu.make_async_remote_copy(src_ref=src, dst_ref=dst, send_sem=send_sems.at[k], recv_sem=recv_sems.at[k],
                                        device_id=peer, device_id_type=MESH)


def ici_start(name, bufs, lands, after, gather):
    n, nl = len(bufs), len(lands)

    def body(*refs):
        b_in = refs[:n]
        send_sems, recv_sems = refs[n + nl + 1], refs[n + nl + 2]
        b_out = refs[n + nl + 3:2 * n + nl + 3]
        l_out = refs[2 * n + nl + 3:2 * n + 2 * nl + 3]
        token = refs[-1]
        x, y, c, chips = _place()
        kme = 2 * x + y
        for i in range(n):
            for j in range(3):
                peer = (*chips[j], c)
                if gather:
                    _ici_copy(b_out[i].at[kme, c], b_out[i].at[kme, c], send_sems, recv_sems, 3 * i + j, peer).start()
                else:
                    kd = 2 * chips[j][0] + chips[j][1]
                    _ici_copy(b_out[i].at[kd], l_out[i].at[j], send_sems, recv_sems, 3 * i + j, peer).start()
        token[...] = jnp.zeros_like(token)

    arrays = [*bufs, *lands]
    outs = pl.pallas_call(
        body, name=name,
        in_specs=[HBM] * (n + nl) + [ANY],
        out_specs=(SEM, SEM, *[HBM] * (n + nl), pl.BlockSpec(memory_space=pltpu.VMEM)),
        out_shape=(pltpu.SemaphoreType.DMA((3 * n,)), pltpu.SemaphoreType.DMA((3 * n,)),
                   *[pltpu.HBM(a.shape, a.dtype) for a in arrays], jax.ShapeDtypeStruct((8, LANE), F32)),
        input_output_aliases={i: 2 + i for i in range(n + nl)},
        compiler_params=pltpu.CompilerParams(has_side_effects=EFFECT),
    )(*[_in_hbm(a) for a in arrays], after)
    return outs[0], outs[1], list(outs[2:2 + n]), list(outs[2 + n:2 + n + nl]), outs[-1]


def ici_wait(name, send_sems, recv_sems, bufs, lands, after, gather):
    n, nl = len(bufs), len(lands)

    def body(*refs):
        b_in, l_in = refs[:n], refs[n:n + nl]
        send_sems, recv_sems = refs[n + nl], refs[n + nl + 1]
        x, y, c, chips = _place()
        kme = 2 * x + y
        for i in range(n):
            for j in range(3):
                peer = (*chips[j], c)
                kj = 2 * chips[j][0] + chips[j][1]
                if gather:
                    _ici_copy(b_in[i].at[kme, c], b_in[i].at[kme, c], send_sems, recv_sems, 3 * i + j, peer).wait_send()
                    _ici_copy(b_in[i].at[kj, c], b_in[i].at[kj, c], send_sems, recv_sems, 3 * i + j, peer).wait_recv()
                else:
                    _ici_copy(b_in[i].at[kj], l_in[i].at[j], send_sems, recv_sems, 3 * i + j, peer).wait_send()
                    _ici_copy(b_in[i].at[kj], l_in[i].at[j], send_sems, recv_sems, 3 * i + j, peer).wait_recv()

    arrays = [*bufs, *lands]
    outs = pl.pallas_call(
        body, name=name,
        in_specs=[HBM] * (n + nl) + [SEM, SEM, ANY],
        out_specs=tuple([HBM] * (n + nl)),
        out_shape=tuple(pltpu.HBM(a.shape, a.dtype) for a in arrays),
        input_output_aliases={i: i for i in range(n + nl)},
        compiler_params=pltpu.CompilerParams(has_side_effects=EFFECT),
    )(*arrays, send_sems, recv_sems, after)
    return list(outs[:n]), list(outs[n:])


def gather_pair_pass(name, bufs):
    n = len(bufs)

    def body(*refs):
        b = refs[n:2 * n]
        send_sems, recv_sems = refs[2 * n:]
        x, y, c, chips = _place()
        sib = (x, y, 1 - c)

        def d2d(i, j, which):
            kj = 2 * chips[j][0] + chips[j][1]
            return _ici_copy(b[i].at[kj, which], b[i].at[kj, which], send_sems, recv_sems, 3 * i + j, sib)

        for i in range(n):
            for j in range(3):
                d2d(i, j, c).start()
        for i in range(n):
            for j in range(3):
                d2d(i, j, 1 - c).wait_recv()
        for i in range(n):
            for j in range(3):
                d2d(i, j, c).wait_send()

    return pl.pallas_call(
        body, name=name, in_specs=[ANY] * n, out_specs=[ANY] * n,
        out_shape=[jax.ShapeDtypeStruct(a.shape, a.dtype) for a in bufs],
        input_output_aliases={i: i for i in range(n)},
        scratch_shapes=[pltpu.SemaphoreType.DMA((3 * n,)), pltpu.SemaphoreType.DMA((3 * n,))],
    )(*bufs)


def pair_exchange(name, dws):
    n = len(dws)

    def body(*refs):
        ins, outs = refs[:n], refs[n:2 * n]
        send_sems, recv_sems = refs[2 * n:]
        x, y, c, _ = _place()
        copies = []
        for i in range(n):
            copies.append(pltpu.make_async_remote_copy(
                src_ref=ins[i].at[:, 1 - c], dst_ref=outs[i],
                send_sem=send_sems.at[i], recv_sem=recv_sems.at[i],
                device_id=(x, y, 1 - c), device_id_type=MESH))
            copies[i].start()
        for cp in copies:
            cp.wait_recv()
        for cp in copies:
            cp.wait_send()

    return pl.pallas_call(
        body, name=name, in_specs=[ANY] * n, out_specs=[ANY] * n,
        out_shape=[jax.ShapeDtypeStruct((N_CHIPS, *d.shape[2:]), d.dtype) for d in dws],
        scratch_shapes=[pltpu.SemaphoreType.DMA((n,)), pltpu.SemaphoreType.DMA((n,))],
    )(*dws)


def pair_assemble(gbufs):
    n = len(gbufs)

    def body(*refs):
        bufs = refs[n:2 * n]
        send_sems, recv_sems = refs[2 * n:]
        x, y, c, _ = _place()
        copies = []
        for i in range(n):
            copies.append(pltpu.make_async_remote_copy(
                src_ref=bufs[i].at[:, c], dst_ref=bufs[i].at[:, c],
                send_sem=send_sems.at[i], recv_sem=recv_sems.at[i],
                device_id=(x, y, 1 - c), device_id_type=MESH))
            copies[i].start()
        for i in range(n):
            pltpu.make_async_remote_copy(
                src_ref=bufs[i].at[:, 1 - c], dst_ref=bufs[i].at[:, 1 - c],
                send_sem=send_sems.at[i], recv_sem=recv_sems.at[i],
                device_id=(x, y, 1 - c), device_id_type=MESH).wait_recv()
        for cp in copies:
            cp.wait_send()

    return pl.pallas_call(
        body, name="grad_pair_assemble", in_specs=[ANY] * n, out_specs=[ANY] * n,
        out_shape=[jax.ShapeDtypeStruct(g.shape, g.dtype) for g in gbufs],
        input_output_aliases={i: i for i in range(n)},
        scratch_shapes=[pltpu.SemaphoreType.DMA((n,)), pltpu.SemaphoreType.DMA((n,))],
    )(*gbufs)


def all_reduce_small(vec):
    NR = vec.shape[0]
    flips = [(fx, fy, fc) for fx in (0, 1) for fy in (0, 1) for fc in (0, 1)][1:]

    def body(v_ref, o_ref, buf, send_sems, recv_sems):
        x, y, c, _ = _place()
        me = 4 * x + 2 * y + c
        buf[me] = v_ref[...]
        copies = []
        for j, (fx, fy, fc) in enumerate(flips):
            peer = (1 - x if fx else x, 1 - y if fy else y, 1 - c if fc else c)
            copies.append(pltpu.make_async_remote_copy(
                src_ref=v_ref, dst_ref=buf.at[me], send_sem=send_sems.at[j], recv_sem=recv_sems.at[j],
                device_id=peer, device_id_type=MESH))
            copies[j].start()
        for cp in copies:
            cp.wait_recv()
        for cp in copies:
            cp.wait_send()
        acc = buf[0]
        for d in range(1, 8):
            acc = acc + buf[d]
        o_ref[...] = acc

    return pl.pallas_call(
        body, name="all_reduce_small",
        in_specs=[pl.BlockSpec(memory_space=pltpu.VMEM)], out_specs=pl.BlockSpec(memory_space=pltpu.VMEM),
        out_shape=jax.ShapeDtypeStruct((NR, LANE), F32),
        scratch_shapes=[pltpu.VMEM((8, NR, LANE), F32), pltpu.SemaphoreType.DMA((7,)),
                        pltpu.SemaphoreType.DMA((7,))],
    )(vec)


def _pack(arrays):
    flat = jnp.concatenate([a.reshape(-1).astype(F32) for a in arrays])
    n = flat.shape[0]
    npad = -(-n // (8 * LANE)) * (8 * LANE)
    return jnp.pad(flat, (0, npad - n)).reshape(npad // LANE, LANE)


def _unpack(buf, like):
    flat = buf.reshape(-1)
    out, off = [], 0
    for a in like:
        out.append(flat[off:off + a.size].reshape(a.shape))
        off += a.size
    return out


def kernel(x, ffn1_norm, ffn1_w_in, ffn1_w_out, mix_norm, ffn2_norm, ffn2_w_in, ffn2_w_out, a_w_qkv, a_rel_bias, a_w_o, kv_norm, kv_w_down, kv_latent_norm, kv_w_up, b_w_dq, b_q_norm, b_w_uq, b_w_o, final_norm, loss_target, m_ffn1_norm, m_ffn1_w_in, m_ffn1_w_out, m_mix_norm, m_ffn2_norm, m_ffn2_w_in, m_ffn2_w_out, m_a_w_qkv, m_a_rel_bias, m_a_w_o, m_kv_norm, m_kv_w_down, m_kv_latent_norm, m_kv_w_up, m_b_w_dq, m_b_q_norm, m_b_w_uq, m_b_w_o, m_final_norm, v_ffn1_norm, v_ffn1_w_in, v_ffn1_w_out, v_mix_norm, v_ffn2_norm, v_ffn2_w_in, v_ffn2_w_out, v_a_w_qkv, v_a_rel_bias, v_a_w_o, v_kv_norm, v_kv_w_down, v_kv_latent_norm, v_kv_w_up, v_b_w_dq, v_b_q_norm, v_b_w_uq, v_b_w_o, v_final_norm):
    B, S, D = x.shape
    T = B * S
    HB = D // 128
    QL = b_q_norm.shape[-1]
    KVL = kv_latent_norm.shape[0]
    hpc = HB // N_CHIPS
    tabs = rope_tables(S)
    idx = jnp.stack([2 * lax.axis_index("x") + lax.axis_index("y"), lax.axis_index("c")]).astype(I32)

    def halves(a):
        return a.reshape(*a.shape[:-2], 2, a.shape[-2] // 2, a.shape[-1])

    def whole(a):
        return a.reshape(*a.shape[:-3], 2 * a.shape[-2], a.shape[-1])

    kv_w_down_p = jnp.pad(kv_w_down, ((0, 0), (0, LANE - ROPE)))[None]
    b_w_uq_p = jnp.pad(b_w_uq.reshape(1, QL, hpc, NOPE + ROPE),
                       ((0, 0), (0, 0), (0, 0), (0, LANE - ROPE))).reshape(1, QL, hpc * 256)
    sharded = [("ffn1_w_in", ffn1_w_in), ("ffn1_w_out", ffn1_w_out), ("ffn2_w_in", ffn2_w_in),
               ("ffn2_w_out", ffn2_w_out), ("a_w_qkv", a_w_qkv), ("a_w_o", a_w_o),
               ("kv_w_down", kv_w_down_p), ("kv_w_up", kv_w_up[None]), ("b_w_dq", b_w_dq),
               ("b_w_uq", b_w_uq_p), ("b_w_o", b_w_o)]
    pieces = [(a, l) for a, (_, w) in enumerate(sharded) for l in range(w.shape[0])]
    names = [nm for nm, _ in sharded]
    own = {(names[a], l): cast_bf16(f"cast_{names[a]}_{l}", sharded[a][1], l, idx) for a, l in pieces}
    W = {}

    gather_groups = [
        [("ffn1_w_in", 0), ("ffn1_w_out", 0)],
        [("a_w_qkv", 0), ("a_w_o", 0), ("ffn2_w_in", 0), ("ffn2_w_out", 0), ("kv_w_down", 0), ("kv_w_up", 0)],
        [("ffn1_w_in", 1), ("ffn1_w_out", 1), ("b_w_dq", 0), ("b_w_uq", 0), ("b_w_o", 0), ("ffn2_w_in", 1),
         ("ffn2_w_out", 1)]]

    def gather_start(g, after):
        keys = gather_groups[g]
        ss, rs, bufs, _, token = ici_start(f"gather_start_{g}", [halves(own[k]) for k in keys], [], after, True)
        return (g, ss, rs, bufs), token

    def gather_finish(state, after):
        g, ss, rs, bufs = state
        bufs, _ = ici_wait(f"gather_wait_{g}", ss, rs, bufs, [], after, True)
        full = gather_pair_pass(f"gather_pair_{g}", bufs)
        for k, w in zip(gather_groups[g], full):
            W[k] = whole(w)
        return full[0]

    def tied(a, token):
        return a + token[0, 0]

    def col(nm, l=0):
        return W[(nm, l)]

    def row(nm, l=0):
        w = W[(nm, l)]
        return w.reshape(N_CHIPS * w.shape[1], w.shape[2])

    bias = rel_bias_tile("rel_bias_tile", a_rel_bias[0])

    def ffn_fwd(tag, h, g, w_in, w_out):
        xn = rms_fwd(f"{tag}_norm", h, g)
        u, act = ffn_in_act(f"{tag}_in", xn, w_in)
        return mm_roww(f"{tag}_out", act, w_out, F32, res=h, alpha=0.5), (xn, u, act)

    h0 = x.reshape(T, D)
    st0, tok0 = gather_start(0, h0)
    done0 = gather_finish(st0, tok0)
    st1, tok1 = gather_start(1, done0)
    h1, sv_f1a = ffn_fwd("l0f1", h0, tied(ffn1_norm[0], tok1), col("ffn1_w_in", 0), row("ffn1_w_out", 0))
    done1 = gather_finish(st1, h1)
    st2, tok2 = gather_start(2, done1)
    hn_a = rms_fwd("l0mix_norm", h1, tied(mix_norm[0], tok2))
    qkv = mm_colw("l0_qkv", hn_a, col("a_w_qkv"), BF16).reshape(B, S, 3 * D)
    o_a = attn_a_fwd("l0_attn", qkv, bias).reshape(T, D)
    h2 = mm_roww("l0_attn_out", o_a, row("a_w_o"), F32, res=h1)
    h3, sv_f2a = ffn_fwd("l0f2", h2, ffn2_norm[0], col("ffn2_w_in", 0), row("ffn2_w_out", 0))

    hkv = rms_fwd("kv_norm", h3, kv_norm)
    ckr = mm_roww("kv_down", hkv, row("kv_w_down"), F32)
    ckv, kr = kvprep_fwd("kv_prep", ckr, kv_latent_norm, tabs, B, S)
    kvb = mm_colw("kv_up", ckv, col("kv_w_up"), BF16).reshape(B, S, HB * 256)
    gather_finish(st2, kvb)

    h4, sv_f1b = ffn_fwd("l1f1", h3, ffn1_norm[1], col("ffn1_w_in", 1), row("ffn1_w_out", 1))
    hn_b = rms_fwd("l1mix_norm", h4, mix_norm[1])
    cqp = mm_roww("l1_dq", hn_b, row("b_w_dq"), F32)
    cq = rms_fwd("l1_q_norm", cqp, b_q_norm[0])
    qpre = mm_colw("l1_uq", cq, col("b_w_uq"), F32)
    qf = qprep("l1_q_rope", qpre, tabs, B, S, bwd=False).reshape(B, S, HB * 256)
    o_b, lse = mla_fwd("l1_attn", qf, kvb, kr)
    h5 = mm_roww("l1_attn_out", o_b.reshape(T, HB * LANE), row("b_w_o"), F32, res=h4)
    h6, sv_f2b = ffn_fwd("l1f2", h5, ffn2_norm[1], col("ffn2_w_in", 1), row("ffn2_w_out", 1))

    dh, g_final, loss_part = loss_head("loss_head", h6, final_norm, loss_target.reshape(T, D))

    gw = {}
    gbufs = {nm: lax.empty(halves(w).shape, F32) for nm, w in sharded}

    def reduce_start(r, keys, after):
        dws = [halves(gw[k]) for k in keys]
        landed = pair_exchange(f"grad_pair_exchange_{r}", dws)
        parts = [half_sum(f"half_sum_{r}_{i}", dws[i], landed[i], idx) for i in range(len(keys))]
        lands = [lax.empty((3, *p.shape[1:]), p.dtype) for p in parts]
        ss, rs, parts, lands, token = ici_start(f"reduce_start_{r}", parts, lands, after, False)
        return (r, keys, ss, rs, parts, lands), token

    def reduce_finish(state, after):
        r, keys, ss, rs, parts, lands = state
        parts, lands = ici_wait(f"reduce_wait_{r}", ss, rs, parts, lands, after, False)
        for i, (nm, l) in enumerate(keys):
            gbufs[nm] = chip_sum(f"chip_sum_{r}_{i}", parts[i], lands[i], gbufs[nm], l, idx)
        return gbufs[keys[0][0]]

    def ffn_bwd(tag, dh, h_in, g, w_in, w_out, saved, key_in, key_out, after=None):
        xn, u, act = saved
        du = ffn_dact(f"{tag}_dact", dh, w_out, u, after=after)
        dwo = mm_droww(f"{tag}_dwout", act, dh, alpha=0.5)
        gw[key_out] = dwo.reshape(N_CHIPS, dwo.shape[0] // N_CHIPS, dwo.shape[1])
        gw[key_in] = mm_dcolw(f"{tag}_dwin", xn, du, pair_layout=True)
        dxn = mm_colw_t(f"{tag}_dxn", du, w_in, F32, pair_layout=True)
        return rms_bwd(f"{tag}_dnorm", h_in, g, dxn, dres=dh)

    def chip_major(dw):
        return dw.reshape(N_CHIPS, dw.shape[0] // N_CHIPS, dw.shape[1])

    dh, g_f2b = ffn_bwd("l1f2b", dh, h5, ffn2_norm[1], col("ffn2_w_in", 1), row("ffn2_w_out", 1), sv_f2b,
                        ("ffn2_w_in", 1), ("ffn2_w_out", 1))
    red0, rtok0 = reduce_start(0, [("ffn2_w_in", 1), ("ffn2_w_out", 1)], dh)
    do_b = mm_roww_t("l1_attn_do", dh, row("b_w_o"), BF16, after=rtok0).reshape(B, S, HB * LANE)
    gw[("b_w_o", 0)] = chip_major(mm_droww("l1_attn_dwo", o_b.reshape(T, HB * LANE), dh))
    dqf, dkv, dkr = mla_bwd("l1_attn_bwd", qf, kvb, kr, do_b, o_b, lse)
    dqpre = qprep("l1_q_rope_bwd", dqf.reshape(T, HB * 256), tabs, B, S, bwd=True)
    gw[("b_w_uq", 0)] = mm_dcolw("l1_dwuq", cq, dqpre)
    dcq = mm_colw_t("l1_dcq", dqpre, col("b_w_uq"), F32)
    dcqp, g_qn = rms_bwd("l1_dq_norm", cqp, b_q_norm[0], dcq)
    gw[("b_w_dq", 0)] = chip_major(mm_droww("l1_dwdq", hn_b, dcqp))
    dhn = mm_roww_t("l1_dhn", dcqp, row("b_w_dq"), F32)
    dh, g_mixb = rms_bwd("l1_dmix", h4, mix_norm[1], dhn, dres=dh)
    dh, g_f1b = ffn_bwd("l1f1b", dh, h3, ffn1_norm[1], col("ffn1_w_in", 1), row("ffn1_w_out", 1), sv_f1b,
                        ("ffn1_w_in", 1), ("ffn1_w_out", 1))
    fin0 = reduce_finish(red0, dh)
    red1, rtok1 = reduce_start(1, [("b_w_o", 0), ("b_w_uq", 0), ("b_w_dq", 0), ("ffn1_w_in", 1), ("ffn1_w_out", 1)], fin0)
    dkv2 = dkv.reshape(T, HB * 256)
    gw[("kv_w_up", 0)] = mm_dcolw("kv_dwup", ckv, dkv2, after=rtok1)
    dckv = mm_colw_t("kv_dckv", dkv2, col("kv_w_up"), F32, after=rtok1)
    dckr, g_lat = kvprep_bwd("kv_prep_bwd", ckr, kv_latent_norm, dckv, dkr, tabs, B, S)
    gw[("kv_w_down", 0)] = chip_major(mm_droww("kv_dwdown", hkv, dckr))
    dhkv = mm_roww_t("kv_dhkv", dckr, row("kv_w_down"), F32)
    dh, g_kvn = rms_bwd("kv_dnorm", h3, kv_norm, dhkv, dres=dh)
    dh, g_f2a = ffn_bwd("l0f2b", dh, h2, ffn2_norm[0], col("ffn2_w_in", 0), row("ffn2_w_out", 0), sv_f2a,
                        ("ffn2_w_in", 0), ("ffn2_w_out", 0))
    do_a = mm_roww_t("l0_attn_do", dh, row("a_w_o"), BF16).reshape(B, S, D)
    gw[("a_w_o", 0)] = chip_major(mm_droww("l0_attn_dwo", o_a, dh))
    dq_a, dkv_a, dbias = attn_a_bwd("l0_attn_bwd", qkv, do_a, bias)
    dqkv = jnp.concatenate([dq_a.reshape(T, D), dkv_a.reshape(T, 2 * D)], axis=1)
    gw[("a_w_qkv", 0)] = mm_dcolw("l0_dwqkv", hn_a, dqkv)
    dhn = mm_colw_t("l0_dhn", dqkv, col("a_w_qkv"), F32)
    dh, g_mixa = rms_bwd("l0_dmix", h1, mix_norm[0], dhn, dres=dh)
    fin1 = reduce_finish(red1, dh)
    red2, rtok2 = reduce_start(2, [("kv_w_up", 0), ("kv_w_down", 0), ("ffn2_w_in", 0), ("ffn2_w_out", 0),
                                   ("a_w_o", 0), ("a_w_qkv", 0)], fin1)
    dh, g_f1a = ffn_bwd("l0f1b", dh, h0, ffn1_norm[0], col("ffn1_w_in", 0), row("ffn1_w_out", 0), sv_f1a,
                        ("ffn1_w_in", 0), ("ffn1_w_out", 0), after=rtok2)
    grad_x = dh.reshape(B, S, D)
    g_rel = rel_bias_grad("rel_bias_grad", dbias)[:, :2 * MAX_REL + 1][None]
    fin2 = reduce_finish(red2, dh)
    red3, rtok3 = reduce_start(3, [("ffn1_w_in", 0), ("ffn1_w_out", 0)], fin2)
    reduce_finish(red3, rtok3)

    full = [whole(g) for g in pair_assemble([gbufs[nm] for nm in names])]
    G = {nm: g for (nm, _), g in zip(sharded, full)}
    G["kv_w_down"] = G["kv_w_down"][0, :, :KVL + ROPE]
    G["kv_w_up"] = G["kv_w_up"][0]
    G["b_w_uq"] = G["b_w_uq"].reshape(1, QL, hpc, 256)[..., :NOPE + ROPE].reshape(b_w_uq.shape)

    small = [("ffn1_norm", jnp.stack([g_f1a, g_f1b])), ("mix_norm", jnp.stack([g_mixa, g_mixb])),
             ("ffn2_norm", jnp.stack([g_f2a, g_f2b])), ("a_rel_bias", g_rel), ("kv_norm", g_kvn),
             ("kv_latent_norm", g_lat), ("b_q_norm", g_qn[None]), ("final_norm", g_final)]
    red = all_reduce_small(_pack([loss_part] + [g for _, g in small]))
    unpacked = _unpack(red, [loss_part] + [g for _, g in small])
    loss = unpacked[0][0, 0]
    for (nm, _), g in zip(small, unpacked[1:]):
        G[nm] = g

    given = dict(ffn1_norm=(ffn1_norm, m_ffn1_norm, v_ffn1_norm), ffn1_w_in=(ffn1_w_in, m_ffn1_w_in, v_ffn1_w_in),
                 ffn1_w_out=(ffn1_w_out, m_ffn1_w_out, v_ffn1_w_out), mix_norm=(mix_norm, m_mix_norm, v_mix_norm),
                 ffn2_norm=(ffn2_norm, m_ffn2_norm, v_ffn2_norm), ffn2_w_in=(ffn2_w_in, m_ffn2_w_in, v_ffn2_w_in),
                 ffn2_w_out=(ffn2_w_out, m_ffn2_w_out, v_ffn2_w_out), a_w_qkv=(a_w_qkv, m_a_w_qkv, v_a_w_qkv),
                 a_rel_bias=(a_rel_bias, m_a_rel_bias, v_a_rel_bias), a_w_o=(a_w_o, m_a_w_o, v_a_w_o),
                 kv_norm=(kv_norm, m_kv_norm, v_kv_norm), kv_w_down=(kv_w_down, m_kv_w_down, v_kv_w_down),
                 kv_latent_norm=(kv_latent_norm, m_kv_latent_norm, v_kv_latent_norm),
                 kv_w_up=(kv_w_up, m_kv_w_up, v_kv_w_up), b_w_dq=(b_w_dq, m_b_w_dq, v_b_w_dq),
                 b_q_norm=(b_q_norm, m_b_q_norm, v_b_q_norm), b_w_uq=(b_w_uq, m_b_w_uq, v_b_w_uq),
                 b_w_o=(b_w_o, m_b_w_o, v_b_w_o), final_norm=(final_norm, m_final_norm, v_final_norm))
    order = list(given)
    delta, new_m, new_v = {}, {}, {}
    small_names = [nm for nm, _ in small]
    packed = [_pack([given[nm][k] for nm in small_names]) for k in range(3)]
    outs = adamw("adamw_small", packed[0], _pack([G[nm] for nm in small_names]), packed[1], packed[2])
    for dst, buf in zip((delta, new_m, new_v), outs):
        for nm, a in zip(small_names, _unpack(buf, [given[nm][0] for nm in small_names])):
            dst[nm] = a
    for nm, _ in sharded:
        w, m, v = given[nm]
        g = G[nm].reshape(w.shape)
        G[nm] = g
        two = lambda a: a.reshape(-1, a.shape[-1])
        d_, m_, v_ = adamw(f"adamw_{nm}", two(w), two(g), two(m), two(v))
        delta[nm], new_m[nm], new_v[nm] = d_.reshape(w.shape), m_.reshape(w.shape), v_.reshape(w.shape)

    return (loss, grad_x, *[G[n] for n in order], *[delta[n] for n in order],
            *[new_m[n] for n in order], *[new_v[n] for n in order])
```

```python
import functools
import math

import jax
import jax.numpy as jnp
from jax import lax
from jax.experimental import pallas as pl
from jax.experimental.pallas import tpu as pltpu

F32 = jnp.float32
BF16 = jnp.bfloat16
I32 = jnp.int32

CHUNK = 64
CHUNK_SHIFT = 6
HEAD_DIM_A = 64
LEFT_CHUNKS = 8
MAX_REL = 128
REL_PAD = 384
QROWS = 2 * CHUNK
WIN = (LEFT_CHUNKS + 2) * CHUNK
PADR = LEFT_CHUNKS * CHUNK
NOPE = 128
ROPE = 64
EPS = 1e-6
NEG_INF = -1e30
ROPE_THETA = 10000.0
ADAM_LR, ADAM_B1, ADAM_B2, ADAM_EPS, ADAM_WD, ADAM_STEP = 0.001, 0.9, 0.999, 1e-08, 0.01, 10
N_CHIPS = 4
LANE = 128
MESH = pl.DeviceIdType.MESH
VMEM_CAP_MB = 60

NN = (((1,), (0,)), ((), ()))
NT = (((1,), (1,)), ((), ()))
TN = (((0,), (0,)), ((), ()))


def _tile(n, pref, mult):
    t = (min(pref, n) // mult) * mult
    while t >= mult:
        if n % t == 0:
            return t
        t -= mult
    return n


def _nbytes(shape, dtype):
    return math.prod(shape) * jnp.dtype(dtype).itemsize


def _params(block_bytes, extra_bytes=0):
    need = 2 * block_bytes + extra_bytes
    mb = min(VMEM_CAP_MB, max(32, int(need * 1.25 / 2**20) + 8))
    return pltpu.CompilerParams(vmem_limit_bytes=mb * 2**20)


def _mm(name, kind, a, b, grid, a_spec, b_spec, o_spec, out_shape, out_dtype, blocks,
        red_axis=None, nred=1, alpha=1.0, res=None, res_spec=None, after=None):
    dims = {"nn": NN, "nt": NT, "tn": TN}[kind]
    has_res = res is not None
    acc_in_out = nred > 1 and out_dtype == F32 and not has_res and alpha == 1.0
    n_in = 2 + has_res + (after is not None)

    def body(*refs):
        a_ref, b_ref = refs[0], refs[1]
        r_ref = refs[2] if has_res else None
        o_ref = refs[n_in]
        p = lax.dot_general(a_ref[...].astype(BF16), b_ref[...].astype(BF16), dims,
                            preferred_element_type=F32)

        def finish(acc):
            y = acc if alpha == 1.0 else acc * alpha
            if has_res:
                y = r_ref[...] + y
            o_ref[...] = y.astype(o_ref.dtype)

        if nred == 1:
            finish(p)
            return
        k = pl.program_id(red_axis)
        tgt = o_ref if acc_in_out else refs[-1]

        @pl.when(k == 0)
        def _():
            tgt[...] = p

        @pl.when(k > 0)
        def _():
            tgt[...] += p

        if not acc_in_out:
            @pl.when(k == nred - 1)
            def _():
                finish(tgt[...])

    a_blk, b_blk, o_blk = blocks
    scratch = []
    extra = 0
    if nred > 1 and not acc_in_out:
        scratch = [pltpu.VMEM(o_blk, F32)]
        extra = _nbytes(o_blk, F32)
    blk = _nbytes(a_blk, a.dtype) + _nbytes(b_blk, b.dtype) + _nbytes(o_blk, out_dtype)
    ins, specs = [a, b], [a_spec, b_spec]
    if has_res:
        ins.append(res)
        specs.append(res_spec)
        blk += _nbytes(o_blk, res.dtype)
    if after is not None:
        ins.append(after)
        specs.append(pl.BlockSpec(memory_space=pl.ANY))
    extra += _nbytes(a_blk, BF16) + _nbytes(b_blk, BF16) + 2 * _nbytes(o_blk, F32)
    return pl.pallas_call(
        body, name=name, grid=grid, in_specs=specs, out_specs=o_spec,
        out_shape=jax.ShapeDtypeStruct(out_shape, out_dtype), scratch_shapes=scratch,
        compiler_params=_params(blk, extra),
    )(*ins)


def mm_colw(name, x, w3, out_dtype):
    T, K = x.shape
    _, _, nl = w3.shape
    tm = _tile(T, 512, 8)
    return _mm(name, "nn", x, w3, (N_CHIPS, T // tm),
               pl.BlockSpec((tm, K), lambda j, i: (i, 0)),
               pl.BlockSpec((None, K, nl), lambda j, i: (j, 0, 0)),
               pl.BlockSpec((tm, nl), lambda j, i: (i, j)),
               (T, N_CHIPS * nl), out_dtype, ((tm, K), (K, nl), (tm, nl)))


def _pair_chip(j):
    return (j % 2) * 2 + j // 2


def mm_colw_t(name, dy, w3, out_dtype, res=None, after=None, pair_layout=False):
    T = dy.shape[0]
    _, K, nl = w3.shape
    tm = _tile(T, 512, 8)
    chip = _pair_chip if pair_layout else (lambda j: j)
    return _mm(name, "nt", dy, w3, (T // tm, N_CHIPS),
               pl.BlockSpec((tm, nl), lambda i, j: (i, j)),
               pl.BlockSpec((None, K, nl), lambda i, j: (chip(j), 0, 0)),
               pl.BlockSpec((tm, K), lambda i, j: (i, 0)),
               (T, K), out_dtype, ((tm, nl), (K, nl), (tm, K)),
               red_axis=1, nred=N_CHIPS, res=res,
               res_spec=pl.BlockSpec((tm, K), lambda i, j: (i, 0)), after=after)


def mm_dcolw(name, x, dy, after=None, pair_layout=False):
    T, K = x.shape
    nl = dy.shape[1] // N_CHIPS
    tt = _tile(T, 1024, 8)
    chip = _pair_chip if pair_layout else (lambda j: j)
    return _mm(name, "tn", x, dy, (N_CHIPS, T // tt),
               pl.BlockSpec((tt, K), lambda j, t: (t, 0)),
               pl.BlockSpec((tt, nl), lambda j, t: (t, j)),
               pl.BlockSpec((None, K, nl), lambda j, t: (chip(j), 0, 0)),
               (N_CHIPS, K, nl), BF16, ((tt, K), (tt, nl), (K, nl)),
               red_axis=1, nred=T // tt, after=after)


def mm_roww(name, x, w2, out_dtype, res=None, alpha=1.0):
    T, Kt = x.shape
    N = w2.shape[1]
    tm = _tile(T, 512, 8)
    return _mm(name, "nn", x, w2, (T // tm,),
               pl.BlockSpec((tm, Kt), lambda i: (i, 0)),
               pl.BlockSpec((Kt, N), lambda i: (0, 0)),
               pl.BlockSpec((tm, N), lambda i: (i, 0)),
               (T, N), out_dtype, ((tm, Kt), (Kt, N), (tm, N)),
               alpha=alpha, res=res, res_spec=pl.BlockSpec((tm, N), lambda i: (i, 0)))


def mm_roww_t(name, dy, w2, out_dtype, alpha=1.0, after=None):
    T, N = dy.shape
    Kt = w2.shape[0]
    tm = _tile(T, 512, 8)
    tk = _tile(Kt, 1408, LANE)
    return _mm(name, "nt", dy, w2, (Kt // tk, T // tm),
               pl.BlockSpec((tm, N), lambda j, i: (i, 0)),
               pl.BlockSpec((tk, N), lambda j, i: (j, 0)),
               pl.BlockSpec((tm, tk), lambda j, i: (i, j)),
               (T, Kt), out_dtype, ((tm, N), (tk, N), (tm, tk)), alpha=alpha, after=after)


def mm_droww(name, x, dy, alpha=1.0):
    T, Kt = x.shape
    N = dy.shape[1]
    tt = _tile(T, 1024, 8)
    tk = _tile(Kt, 1408, LANE)
    return _mm(name, "tn", x, dy, (Kt // tk, T // tt),
               pl.BlockSpec((tt, tk), lambda j, t: (t, j)),
               pl.BlockSpec((tt, N), lambda j, t: (t, 0)),
               pl.BlockSpec((tk, N), lambda j, t: (j, 0)),
               (Kt, N), BF16, ((tt, tk), (tt, N), (tk, N)),
               red_axis=1, nred=T // tt, alpha=alpha)


def rms_fwd(name, x, g):
    T, D = x.shape
    tm = _tile(T, 512, 8)

    def body(x_ref, g_ref, o_ref):
        xv = x_ref[...]
        r = lax.rsqrt(jnp.mean(xv * xv, axis=-1, keepdims=True) + EPS)
        o_ref[...] = (xv * r * g_ref[...]).astype(o_ref.dtype)

    return pl.pallas_call(
        body, name=name, grid=(T // tm,),
        in_specs=[pl.BlockSpec((tm, D), lambda i: (i, 0)), pl.BlockSpec((1, D), lambda i: (0, 0))],
        out_specs=pl.BlockSpec((tm, D), lambda i: (i, 0)),
        out_shape=jax.ShapeDtypeStruct((T, D), BF16),
        compiler_params=_params(_nbytes((tm, D), F32) * 2, 4 * _nbytes((tm, D), F32)),
    )(x, g.reshape(1, D))


def _rms_bwd_math(xv, gv, dy):
    r = lax.rsqrt(jnp.mean(xv * xv, axis=-1, keepdims=True) + EPS)
    xh = xv * r
    dyg = dy * gv
    dx = r * (dyg - xh * jnp.mean(dyg * xh, axis=-1, keepdims=True))
    dg = jnp.sum(dy * xh, axis=0, keepdims=True)
    return dx, dg


def rms_bwd(name, x, g, dy, dres=None):
    T, D = x.shape
    tm = _tile(T, 256, 8)
    has_res = dres is not None

    def body(*refs):
        x_ref, g_ref, dy_ref = refs[:3]
        r_ref = refs[3] if has_res else None
        dx_ref, dg_ref = refs[-2:]
        dx, dg = _rms_bwd_math(x_ref[...], g_ref[...], dy_ref[...].astype(F32))
        if has_res:
            dx = r_ref[...] + dx
        dx_ref[...] = dx

        @pl.when(pl.program_id(0) == 0)
        def _():
            dg_ref[...] = dg

        @pl.when(pl.program_id(0) > 0)
        def _():
            dg_ref[...] += dg

    row = pl.BlockSpec((tm, D), lambda i: (i, 0))
    vec = pl.BlockSpec((1, D), lambda i: (0, 0))
    ins, specs = [x, g.reshape(1, D), dy], [row, vec, row]
    if has_res:
        ins.append(dres)
        specs.append(row)
    dx, dg = pl.pallas_call(
        body, name=name, grid=(T // tm,), in_specs=specs, out_specs=[row, vec],
        out_shape=[jax.ShapeDtypeStruct((T, D), F32), jax.ShapeDtypeStruct((1, D), F32)],
        compiler_params=_params(_nbytes((tm, D), F32) * 4, 6 * _nbytes((tm, D), F32)),
    )(*ins)
    return dx, dg.reshape(D)


def ffn_in_act(name, x, w3):
    T, K = x.shape
    _, _, nl = w3.shape
    tm = _tile(T, 512, 8)

    def body(x_ref, wg_ref, wu_ref, u_ref, a_ref):
        xv = x_ref[...]
        g = jnp.dot(xv, wg_ref[...], preferred_element_type=F32)
        up = jnp.dot(xv, wu_ref[...], preferred_element_type=F32)
        u_ref[:, :nl] = g.astype(u_ref.dtype)
        u_ref[:, nl:] = up.astype(u_ref.dtype)
        a_ref[...] = (g * jax.nn.sigmoid(g) * up).astype(a_ref.dtype)

    blk = _nbytes((tm, K), BF16) + 2 * _nbytes((K, nl), BF16) + _nbytes((tm, 3 * nl), BF16)
    return pl.pallas_call(
        body, name=name, grid=(2, T // tm),
        in_specs=[pl.BlockSpec((tm, K), lambda p, i: (i, 0)),
                  pl.BlockSpec((None, K, nl), lambda p, i: (p, 0, 0)),
                  pl.BlockSpec((None, K, nl), lambda p, i: (p + 2, 0, 0))],
        out_specs=[pl.BlockSpec((tm, 2 * nl), lambda p, i: (i, p)), pl.BlockSpec((tm, nl), lambda p, i: (i, p))],
        out_shape=[jax.ShapeDtypeStruct((T, 4 * nl), BF16), jax.ShapeDtypeStruct((T, 2 * nl), BF16)],
        compiler_params=_params(blk, 4 * _nbytes((tm, nl), F32)),
    )(x, w3, w3)


def ffn_dact(name, dh, w_out, u, after=None):
    T, N = dh.shape
    F = w_out.shape[0]
    nl = F // 2
    tm = _tile(T, 512, 8)

    def body(*refs):
        d_ref, w_ref, u_ref = refs[:3]
        o_ref = refs[-1]
        dact = 0.5 * lax.dot_general(d_ref[...].astype(BF16), w_ref[...], NT, preferred_element_type=F32)
        g = u_ref[:, :nl].astype(F32)
        up = u_ref[:, nl:].astype(F32)
        sig = jax.nn.sigmoid(g)
        o_ref[:, :nl] = (dact * up * (sig * (1.0 + g * (1.0 - sig)))).astype(o_ref.dtype)
        o_ref[:, nl:] = (dact * (g * sig)).astype(o_ref.dtype)

    ins = [dh, w_out, u]
    specs = [pl.BlockSpec((tm, N), lambda p, i: (i, 0)), pl.BlockSpec((nl, N), lambda p, i: (p, 0)),
             pl.BlockSpec((tm, 2 * nl), lambda p, i: (i, p))]
    if after is not None:
        ins.append(after)
        specs.append(pl.BlockSpec(memory_space=pl.ANY))
    blk = _nbytes((tm, N), F32) + _nbytes((nl, N), BF16) + 2 * _nbytes((tm, 2 * nl), BF16)
    return pl.pallas_call(
        body, name=name, grid=(2, T // tm), in_specs=specs,
        out_specs=pl.BlockSpec((tm, 2 * nl), lambda p, i: (i, p)),
        out_shape=jax.ShapeDtypeStruct((T, 2 * F), BF16),
        compiler_params=_params(blk, 6 * _nbytes((tm, nl), F32)),
    )(*ins)


def loss_head(name, h, g, target):
    T, D = h.shape
    tm = _tile(T, 256, 8)

    def body(h_ref, g_ref, t_ref, dh_ref, dg_ref, loss_ref):
        xv = h_ref[...]
        gv = g_ref[...]
        r = lax.rsqrt(jnp.mean(xv * xv, axis=-1, keepdims=True) + EPS)
        err = xv * r * gv - t_ref[...]
        part = 0.5 * jnp.sum(jnp.mean(err * err, axis=-1, keepdims=True), axis=0, keepdims=True)
        dx, dg = _rms_bwd_math(xv, gv, err * (1.0 / D))
        dh_ref[...] = dx
        part = jnp.broadcast_to(part, (1, LANE))

        @pl.when(pl.program_id(0) == 0)
        def _():
            dg_ref[...] = dg
            loss_ref[...] = part

        @pl.when(pl.program_id(0) > 0)
        def _():
            dg_ref[...] += dg
            loss_ref[...] += part

    row = pl.BlockSpec((tm, D), lambda i: (i, 0))
    vec = pl.BlockSpec((1, D), lambda i: (0, 0))
    dh, dg, loss = pl.pallas_call(
        body, name=name, grid=(T // tm,), in_specs=[row, vec, row],
        out_specs=[row, vec, pl.BlockSpec((1, LANE), lambda i: (0, 0))],
        out_shape=[jax.ShapeDtypeStruct((T, D), F32), jax.ShapeDtypeStruct((1, D), F32),
                   jax.ShapeDtypeStruct((1, LANE), F32)],
        compiler_params=_params(_nbytes((tm, D), F32) * 3, 6 * _nbytes((tm, D), F32)),
    )(h, g.reshape(1, D), target)
    return dh, dg.reshape(D), loss


def rope_tables(S):
    half = ROPE // 2
    freqs = ROPE_THETA ** (-jnp.arange(half, dtype=F32) / half)
    ang = jnp.arange(S, dtype=F32)[:, None] * freqs[None, :]
    cos, sin = jnp.cos(ang), jnp.sin(ang)
    z = jnp.zeros_like(cos)
    ct = jnp.concatenate([cos, cos, z, z], axis=1)
    s1 = jnp.concatenate([-sin, z, z, z], axis=1)
    s2 = jnp.concatenate([z, sin, z, z], axis=1)
    return ct, s1, s2


def _rope_tile(t, ct, s1, s2):
    return t * ct + pltpu.roll(t, 96, 1) * s1 + pltpu.roll(t, 32, 1) * s2


def _rope_tile_bwd(d, ct, s1, s2):
    return d * ct + pltpu.roll(d * s1, 32, 1) + pltpu.roll(d * s2, 96, 1)


def qprep(name, q, tabs, B, S, bwd):
    T, W = q.shape
    nh = W // 256
    ts = _tile(S, 256, 8)
    fn = _rope_tile_bwd if bwd else _rope_tile

    def body(q_ref, ct_ref, s1_ref, s2_ref, o_ref):
        ct, s1, s2 = ct_ref[...], s1_ref[...], s2_ref[...]
        for h in range(nh):
            o_ref[0, :, 256 * h:256 * h + 128] = q_ref[0, :, 256 * h:256 * h + 128].astype(o_ref.dtype)
            t = q_ref[0, :, 256 * h + 128:256 * h + 256].astype(F32)
            o_ref[0, :, 256 * h + 128:256 * h + 256] = fn(t, ct, s1, s2).astype(o_ref.dtype)

    row = pl.BlockSpec((1, ts, W), lambda b, s: (b, s, 0))
    tab = pl.BlockSpec((ts, LANE), lambda b, s: (s, 0))
    out = pl.pallas_call(
        body, name=name, grid=(B, S // ts), in_specs=[row, tab, tab, tab], out_specs=row,
        out_shape=jax.ShapeDtypeStruct((B, S, W), BF16),
        compiler_params=_params(_nbytes((ts, W), F32) * 2, _nbytes((ts, W), F32) * 2),
    )(q.reshape(B, S, W), *tabs)
    return out.reshape(T, W)


def kvprep_fwd(name, ckr, g, tabs, B, S):
    T, W = ckr.shape
    KVL = W - LANE
    ts = _tile(S, 256, 8)

    def body(x_ref, g_ref, ct_ref, s1_ref, s2_ref, c_ref, k_ref):
        xv = x_ref[0, :, :KVL]
        r = lax.rsqrt(jnp.mean(xv * xv, axis=-1, keepdims=True) + EPS)
        c_ref[0] = (xv * r * g_ref[...]).astype(c_ref.dtype)
        k_ref[0] = _rope_tile(x_ref[0, :, KVL:], ct_ref[...], s1_ref[...], s2_ref[...]).astype(k_ref.dtype)

    tab = pl.BlockSpec((ts, LANE), lambda b, s: (s, 0))
    c, k = pl.pallas_call(
        body, name=name, grid=(B, S // ts),
        in_specs=[pl.BlockSpec((1, ts, W), lambda b, s: (b, s, 0)), pl.BlockSpec((1, KVL), lambda b, s: (0, 0)),
                  tab, tab, tab],
        out_specs=[pl.BlockSpec((1, ts, KVL), lambda b, s: (b, s, 0)),
                   pl.BlockSpec((1, ts, LANE), lambda b, s: (b, s, 0))],
        out_shape=[jax.ShapeDtypeStruct((B, S, KVL), BF16), jax.ShapeDtypeStruct((B, S, LANE), BF16)],
        compiler_params=_params(_nbytes((ts, W), F32) * 2, _nbytes((ts, W), F32) * 2),
    )(ckr.reshape(B, S, W), g.reshape(1, KVL), *tabs)
    return c.reshape(T, KVL), k


def kvprep_bwd(name, ckr, g, dc, dkr, tabs, B, S):
    T, W = ckr.shape
    KVL = W - LANE
    ts = _tile(S, 256, 8)

    def body(x_ref, g_ref, dc_ref, dk_ref, ct_ref, s1_ref, s2_ref, o_ref, dg_ref):
        dx, dg = _rms_bwd_math(x_ref[0, :, :KVL], g_ref[...], dc_ref[0])
        o_ref[0, :, :KVL] = dx
        o_ref[0, :, KVL:] = _rope_tile_bwd(dk_ref[0], ct_ref[...], s1_ref[...], s2_ref[...])
        first = (pl.program_id(0) == 0) & (pl.program_id(1) == 0)

        @pl.when(first)
        def _():
            dg_ref[...] = dg

        @pl.when(jnp.logical_not(first))
        def _():
            dg_ref[...] += dg

    tab = pl.BlockSpec((ts, LANE), lambda b, s: (s, 0))
    vec = pl.BlockSpec((1, KVL), lambda b, s: (0, 0))
    o, dg = pl.pallas_call(
        body, name=name, grid=(B, S // ts),
        in_specs=[pl.BlockSpec((1, ts, W), lambda b, s: (b, s, 0)), vec,
                  pl.BlockSpec((1, ts, KVL), lambda b, s: (b, s, 0)),
                  pl.BlockSpec((1, ts, LANE), lambda b, s: (b, s, 0)), tab, tab, tab],
        out_specs=[pl.BlockSpec((1, ts, W), lambda b, s: (b, s, 0)), vec],
        out_shape=[jax.ShapeDtypeStruct((B, S, W), F32), jax.ShapeDtypeStruct((1, KVL), F32)],
        compiler_params=_params(_nbytes((ts, W), F32) * 4, _nbytes((ts, W), F32) * 4),
    )(ckr.reshape(B, S, W), g.reshape(1, KVL), dc.reshape(B, S, KVL), dkr, *tabs)
    return o.reshape(T, W), dg.reshape(KVL)


DIAGS = 768


def _diag_onehot():
    col = lax.broadcasted_iota(I32, (REL_PAD, DIAGS), 1)
    row = lax.broadcasted_iota(I32, (REL_PAD, DIAGS), 0)
    idx = jnp.clip(PADR + QROWS - 1 - col, -MAX_REL, MAX_REL) + MAX_REL
    return (row == idx).astype(F32)


def rel_bias_tile(name, table):
    H = table.shape[0]
    tpad = jnp.pad(table, ((0, 0), (0, REL_PAD - table.shape[1])))

    def body(t_ref, o_ref):
        g = lax.dot_general(t_ref[...], _diag_onehot(), NN, precision=lax.Precision.HIGHEST,
                            preferred_element_type=F32)
        for h in range(H):
            gb = jnp.broadcast_to(g[h:h + 1, :], (QROWS, DIAGS))
            tile = pltpu.roll(gb, DIAGS - (QROWS - 1), 1, stride=1, stride_axis=0)
            o_ref[h // 2, (h % 2) * QROWS:(h % 2 + 1) * QROWS, :] = tile[:, :WIN]

    return pl.pallas_call(
        body, name=name, out_shape=jax.ShapeDtypeStruct((H // 2, 2 * QROWS, WIN), F32),
        compiler_params=_params(0, 2 * _nbytes((H // 2, 2 * QROWS, WIN), F32)),
    )(tpad)


def rel_bias_grad(name, dbias):
    H = 2 * dbias.shape[0]

    def body(d_ref, o_ref):
        flip = (lax.broadcasted_iota(I32, (QROWS, QROWS), 0) + lax.broadcasted_iota(I32, (QROWS, QROWS), 1)
                == QROWS - 1).astype(F32)
        rows = []
        for h in range(H):
            x = d_ref[h // 2, (h % 2) * QROWS:(h % 2 + 1) * QROWS, :]
            xr = lax.dot_general(flip, x, NN, precision=lax.Precision.HIGHEST, preferred_element_type=F32)
            xp = jnp.concatenate([xr, jnp.zeros((QROWS, DIAGS - WIN), F32)], axis=1)
            y = pltpu.roll(xp, 0, 1, stride=1, stride_axis=0)
            rows.append(jnp.sum(y, axis=0, keepdims=True))
        o_ref[...] = lax.dot_general(jnp.concatenate(rows, axis=0), _diag_onehot(), NT,
                                     precision=lax.Precision.HIGHEST, preferred_element_type=F32)

    return pl.pallas_call(
        body, name=name, out_shape=jax.ShapeDtypeStruct((H, REL_PAD), F32),
        compiler_params=_params(0, 2 * _nbytes(dbias.shape, F32)),
    )(dbias)


def _stack_pair(xp):
    lane = lax.broadcasted_iota(I32, xp.shape, 1)
    z = jnp.zeros_like(xp)
    return jnp.concatenate([jnp.where(lane < HEAD_DIM_A, xp, z), jnp.where(lane >= HEAD_DIM_A, xp, z)], axis=0)


def _unstack_pair(y):
    lane = lax.broadcasted_iota(I32, (QROWS, LANE), 1)
    return jnp.where(lane < HEAD_DIM_A, y[:QROWS], y[QROWS:])


def _attn_a_mask(j):
    r = lax.broadcasted_iota(I32, (2 * QROWS, WIN), 0)
    w = lax.broadcasted_iota(I32, (2 * QROWS, WIN), 1)
    qc = jnp.right_shift(jnp.bitwise_and(r, QROWS - 1), CHUNK_SHIFT)
    kc = jnp.right_shift(w, CHUNK_SHIFT)
    return (kc >= qc) & (kc <= qc + LEFT_CHUNKS) & (w >= PADR - QROWS * j)


def _attn_a_load_bias(bias_hbm, bias_v, sem):
    cp = pltpu.make_async_copy(bias_hbm, bias_v, sem)
    cp.start()
    cp.wait()


def _attn_a_load_kv(qkv_hbm, b, kpad, vpad, sem, S, D):
    kpad[0:PADR, :] = jnp.zeros((PADR, D), BF16)
    vpad[0:PADR, :] = jnp.zeros((PADR, D), BF16)
    ck = pltpu.make_async_copy(qkv_hbm.at[b, :, pl.ds(D, D)], kpad.at[pl.ds(PADR, S), :], sem.at[0])
    cv = pltpu.make_async_copy(qkv_hbm.at[b, :, pl.ds(2 * D, D)], vpad.at[pl.ds(PADR, S), :], sem.at[1])
    ck.start()
    cv.start()
    ck.wait()
    cv.wait()


def _attn_a_probs(qm, kp, bias, valid, scale):
    s = lax.dot_general(qm, kp, NT, preferred_element_type=F32) * scale + bias
    s = jnp.where(valid, s, NEG_INF)
    e = jnp.exp(s - jnp.max(s, axis=-1, keepdims=True))
    return e * (1.0 / jnp.sum(e, axis=-1, keepdims=True))


def attn_a_fwd(name, qkv, bias):
    B, S, D3 = qkv.shape
    D = D3 // 3
    H = D // HEAD_DIM_A
    nb = S // QROWS
    scale = HEAD_DIM_A ** -0.5

    def body(q_ref, bias_hbm, qkv_hbm, o_ref, kpad, vpad, bias_v, sem):
        b, j = pl.program_id(0), pl.program_id(1)

        @pl.when((b == 0) & (j == 0))
        def _():
            _attn_a_load_bias(bias_hbm, bias_v, sem.at[2])

        @pl.when(j == 0)
        def _():
            _attn_a_load_kv(qkv_hbm, b, kpad, vpad, sem, S, D)

        mask = _attn_a_mask(j)
        w0 = pl.multiple_of(j * QROWS, QROWS)
        for p in range(H // 2):
            ls = slice(p * LANE, (p + 1) * LANE)
            pr = _attn_a_probs(_stack_pair(q_ref[0, :, ls]), kpad[pl.ds(w0, WIN), ls], bias_v[p], mask, scale)
            o2 = jnp.dot(pr.astype(BF16), vpad[pl.ds(w0, WIN), ls], preferred_element_type=F32)
            o_ref[0, :, ls] = _unstack_pair(o2).astype(o_ref.dtype)

    scr = 2 * _nbytes((PADR + S, D), BF16) + _nbytes(bias.shape, F32) + 8 * _nbytes((2 * QROWS, WIN), F32)
    return pl.pallas_call(
        body, name=name, grid=(B, nb),
        in_specs=[pl.BlockSpec((1, QROWS, D), lambda b, j: (b, j, 0)),
                  pl.BlockSpec(memory_space=pl.ANY), pl.BlockSpec(memory_space=pl.ANY)],
        out_specs=pl.BlockSpec((1, QROWS, D), lambda b, j: (b, j, 0)),
        out_shape=jax.ShapeDtypeStruct((B, S, D), BF16),
        scratch_shapes=[pltpu.VMEM((PADR + S, D), BF16), pltpu.VMEM((PADR + S, D), BF16),
                        pltpu.VMEM(bias.shape, F32), pltpu.SemaphoreType.DMA((3,))],
        compiler_params=_params(2 * _nbytes((QROWS, D), BF16), scr),
    )(qkv, bias, qkv)


def attn_a_bwd(name, qkv, do, bias):
    B, S, D3 = qkv.shape
    D = D3 // 3
    H = D // HEAD_DIM_A
    nb = S // QROWS
    scale = HEAD_DIM_A ** -0.5

    def body(q_ref, do_ref, bias_hbm, qkv_hbm, dq_ref, dkv_hbm, dbias_hbm, kpad, vpad, dkacc, dvacc, bias_v, dbias_v, sem):
        b, j = pl.program_id(0), pl.program_id(1)

        @pl.when((b == 0) & (j == 0))
        def _():
            _attn_a_load_bias(bias_hbm, bias_v, sem.at[2])
            dbias_v[...] = jnp.zeros_like(dbias_v)

        @pl.when(j == 0)
        def _():
            _attn_a_load_kv(qkv_hbm, b, kpad, vpad, sem, S, D)
            dkacc[...] = jnp.zeros_like(dkacc)
            dvacc[...] = jnp.zeros_like(dvacc)

        mask = _attn_a_mask(j)
        w0 = pl.multiple_of(j * QROWS, QROWS)
        for p in range(H // 2):
            ls = slice(p * LANE, (p + 1) * LANE)
            q2 = _stack_pair(q_ref[0, :, ls])
            do2 = _stack_pair(do_ref[0, :, ls])
            kp = kpad[pl.ds(w0, WIN), ls]
            vp = vpad[pl.ds(w0, WIN), ls]
            pr = _attn_a_probs(q2, kp, bias_v[p], mask, scale)
            dp = lax.dot_general(do2, vp, NT, preferred_element_type=F32)
            ds = pr * (dp - jnp.sum(pr * dp, axis=-1, keepdims=True))
            dbias_v[p] += ds
            dsb = (ds * scale).astype(BF16)
            dq_ref[0, :, ls] = _unstack_pair(jnp.dot(dsb, kp, preferred_element_type=F32))
            dkacc[pl.ds(w0, WIN), ls] += lax.dot_general(dsb, q2, TN, preferred_element_type=F32)
            dvacc[pl.ds(w0, WIN), ls] += lax.dot_general(pr.astype(BF16), do2, TN, preferred_element_type=F32)

        @pl.when(j == nb - 1)
        def _():
            ck = pltpu.make_async_copy(dkacc.at[pl.ds(PADR, S), :], dkv_hbm.at[b, :, pl.ds(0, D)], sem.at[0])
            cv = pltpu.make_async_copy(dvacc.at[pl.ds(PADR, S), :], dkv_hbm.at[b, :, pl.ds(D, D)], sem.at[1])
            ck.start()
            cv.start()
            ck.wait()
            cv.wait()

        @pl.when((b == B - 1) & (j == nb - 1))
        def _():
            cb = pltpu.make_async_copy(dbias_v, dbias_hbm, sem.at[2])
            cb.start()
            cb.wait()

    blk = _nbytes((QROWS, D), BF16) * 2 + _nbytes((QROWS, D), F32)
    scr = (2 * _nbytes((PADR + S, D), BF16) + 2 * _nbytes((PADR + S, D), F32) + 2 * _nbytes(bias.shape, F32)
           + 8 * _nbytes((2 * QROWS, WIN), F32))
    return pl.pallas_call(
        body, name=name, grid=(B, nb),
        in_specs=[pl.BlockSpec((1, QROWS, D), lambda b, j: (b, j, 0)),
                  pl.BlockSpec((1, QROWS, D), lambda b, j: (b, j, 0)),
                  pl.BlockSpec(memory_space=pl.ANY), pl.BlockSpec(memory_space=pl.ANY)],
        out_specs=[pl.BlockSpec((1, QROWS, D), lambda b, j: (b, j, 0)),
                   pl.BlockSpec(memory_space=pl.ANY), pl.BlockSpec(memory_space=pl.ANY)],
        out_shape=[jax.ShapeDtypeStruct((B, S, D), F32), jax.ShapeDtypeStruct((B, S, 2 * D), F32),
                   jax.ShapeDtypeStruct(bias.shape, F32)],
        scratch_shapes=[pltpu.VMEM((PADR + S, D), BF16), pltpu.VMEM((PADR + S, D), BF16),
                        pltpu.VMEM((PADR + S, D), F32), pltpu.VMEM((PADR + S, D), F32),
                        pltpu.VMEM(bias.shape, F32), pltpu.VMEM(bias.shape, F32),
                        pltpu.SemaphoreType.DMA((3,))],
        compiler_params=_params(blk, scr),
    )(qkv, do, bias, qkv)


def _mla_raw_t(k2, kj, q, QB):
    return lax.dot_general(k2[_blk(kj, QB), :], q, NT, preferred_element_type=F32)


def _blk(kj, QB):
    return pl.ds(kj * QB, QB) if isinstance(kj, int) else pl.ds(pl.multiple_of(kj * QB, QB), QB)


def _mla_mask_t(shape, q0, k0):
    kc = jnp.right_shift(k0 + lax.broadcasted_iota(I32, shape, 0), CHUNK_SHIFT)
    qc = jnp.right_shift(q0 + lax.broadcasted_iota(I32, shape, 1), CHUNK_SHIFT)
    return kc <= qc


def _mla_fill_keys(kv_ref, kr_ref, k2):
    k2[:, :NOPE] = kv_ref[0, :, :NOPE]
    k2[:, NOPE:] = kr_ref[0]


def _t(x):
    return x.astype(F32).T


def mla_fwd(name, qf, kv, kr):
    B, S, W = qf.shape
    HB = W // 256
    QB = _tile(S, 256, CHUNK)
    nq = S // QB
    scale = (NOPE + ROPE) ** -0.5

    def body(q_ref, kv_ref, kr_ref, o_ref, lse_ref, k2, vt, st_buf):
        qi = pl.program_id(2)

        @pl.when(qi == 0)
        def _():
            _mla_fill_keys(kv_ref, kr_ref, k2)
            for kj in range(nq):
                vt[kj] = _t(kv_ref[0, kj * QB:(kj + 1) * QB, NOPE:]).astype(BF16)

        q = q_ref[0]
        st_buf[0] = _mla_raw_t(k2, 0, q, QB)

        def step(kj, carry):
            m, l, acc = carry
            cur = lax.rem(kj, 2)
            st_raw = st_buf[cur]
            st_buf[1 - cur] = _mla_raw_t(k2, jnp.minimum(kj + 1, qi), q, QB)
            st = jnp.where(_mla_mask_t((QB, QB), qi * QB, kj * QB), st_raw * scale, NEG_INF)
            m_new = jnp.maximum(m, jnp.max(st, axis=0, keepdims=True))
            a = jnp.exp(m - m_new)
            pt = jnp.exp(st - m_new)
            l = a * l + jnp.sum(pt, axis=0, keepdims=True)
            acc = a * acc + jnp.dot(vt[kj], pt.astype(BF16), preferred_element_type=F32)
            return m_new, l, acc

        init = (jnp.full((1, QB), NEG_INF, F32), jnp.zeros((1, QB), F32), jnp.zeros((NOPE, QB), F32))
        m, l, acc = lax.fori_loop(0, qi + 1, step, init)
        o_ref[0] = (acc * (1.0 / l)).T
        lse_ref[0, 0] = m + jnp.log(l)

    blk = (_nbytes((QB, 256), BF16) + _nbytes((S, 256), BF16) + _nbytes((S, LANE), BF16)
           + _nbytes((QB, LANE), F32))
    return pl.pallas_call(
        body, name=name, grid=(B, HB, nq),
        in_specs=[pl.BlockSpec((1, QB, 256), lambda b, h, i: (b, i, h)),
                  pl.BlockSpec((1, S, 256), lambda b, h, i: (b, 0, h)),
                  pl.BlockSpec((1, S, LANE), lambda b, h, i: (b, 0, 0))],
        out_specs=[pl.BlockSpec((1, QB, LANE), lambda b, h, i: (b, i, h)),
                   pl.BlockSpec((1, 1, 1, QB), lambda b, h, i: (b, h, 0, i))],
        out_shape=[jax.ShapeDtypeStruct((B, S, HB * LANE), F32), jax.ShapeDtypeStruct((B, HB, 1, S), F32)],
        scratch_shapes=[pltpu.VMEM((S, 256), BF16), pltpu.VMEM((nq, NOPE, QB), BF16),
                        pltpu.VMEM((2, QB, QB), F32)],
        compiler_params=_params(blk, 2 * _nbytes((S, 256), BF16) + 10 * _nbytes((QB, QB), F32)),
    )(qf, kv, kr)


def mla_bwd(name, qf, kv, kr, do, o, lse):
    B, S, W = qf.shape
    HB = W // 256
    QB = _tile(S, 256, CHUNK)
    nq = S // QB
    scale = (NOPE + ROPE) ** -0.5

    def body(q_ref, kv_ref, kr_ref, do_ref, o_ref, lse_ref, dq_ref, dkv_ref, dkr_ref, k2, kt, dot_, delta, dqt,
             st_buf, dp_buf):
        h = pl.program_id(1)
        dkv_ref[...] = jnp.zeros_like(dkv_ref)

        @pl.when(h == 0)
        def _():
            dkr_ref[...] = jnp.zeros_like(dkr_ref)

        _mla_fill_keys(kv_ref, kr_ref, k2)
        for i in range(nq):
            rows = slice(i * QB, (i + 1) * QB)
            kt[i] = _t(k2[rows, :]).astype(BF16)
            dot32 = _t(do_ref[0, rows, :])
            delta[i] = jnp.sum(dot32 * o_ref[0, rows, :].T, axis=0, keepdims=True)
            dot_[i] = dot32.astype(BF16)

        for qi in range(nq):
            rows = slice(qi * QB, (qi + 1) * QB)
            q = q_ref[0, rows, :]
            dob = do_ref[0, rows, :]
            lse_q = lse_ref[0, 0, :, rows]
            delta_q = delta[qi]
            dqt[...] = jnp.zeros_like(dqt)

            def raw(kj, slot, q=q, qi=qi):
                st_buf[slot] = _mla_raw_t(k2, kj, q, QB)
                dp_buf[slot] = jnp.dot(kv_ref[0, _blk(kj, QB), NOPE:], dot_[qi], preferred_element_type=F32)

            raw(0, 0)

            def step(kj, carry, q=q, dob=dob, lse_q=lse_q, delta_q=delta_q, qi=qi, raw=raw):
                ks = pl.ds(pl.multiple_of(kj * QB, QB), QB)
                cur = lax.rem(kj, 2)
                st_raw, dp_raw = st_buf[cur], dp_buf[cur]
                raw(jnp.minimum(kj + 1, qi), 1 - cur)
                mask = _mla_mask_t((QB, QB), qi * QB, kj * QB)
                pt = jnp.where(mask, jnp.exp(st_raw * scale - lse_q), 0.0)
                dst = (pt * (dp_raw - delta_q) * scale).astype(BF16)
                dkv_ref[0, ks, NOPE:] += jnp.dot(pt.astype(BF16), dob, preferred_element_type=F32)
                dk2 = jnp.dot(dst, q, preferred_element_type=F32)
                dkv_ref[0, ks, :NOPE] += dk2[:, :NOPE]
                dkr_ref[0, ks, :] += dk2[:, NOPE:]
                dqt[...] += jnp.dot(kt[kj], dst, preferred_element_type=F32)
                return carry

            lax.fori_loop(0, qi + 1, step, 0)
            dq_ref[0, rows, :] = dqt[...].T

    head = lambda w: pl.BlockSpec((1, S, w), lambda b, h: (b, 0, h))
    shared = pl.BlockSpec((1, S, LANE), lambda b, h: (b, 0, 0))
    blk = (2 * _nbytes((S, 256), BF16) + 2 * _nbytes((S, LANE), BF16) + _nbytes((S, LANE), F32)
           + 2 * _nbytes((S, 256), F32) + _nbytes((S, LANE), F32))
    scr = 3 * _nbytes((S, 256), BF16) + 14 * _nbytes((QB, QB), F32)
    return pl.pallas_call(
        body, name=name, grid=(B, HB),
        in_specs=[head(256), head(256), shared, head(LANE), head(LANE),
                  pl.BlockSpec((1, 1, 1, S), lambda b, h: (b, h, 0, 0))],
        out_specs=[head(256), head(256), shared],
        out_shape=[jax.ShapeDtypeStruct((B, S, W), F32), jax.ShapeDtypeStruct((B, S, W), F32),
                   jax.ShapeDtypeStruct((B, S, LANE), F32)],
        scratch_shapes=[pltpu.VMEM((S, 256), BF16), pltpu.VMEM((nq, 256, QB), BF16),
                        pltpu.VMEM((nq, NOPE, QB), BF16), pltpu.VMEM((nq, 1, QB), F32),
                        pltpu.VMEM((256, QB), F32), pltpu.VMEM((2, QB, QB), F32), pltpu.VMEM((2, QB, QB), F32)],
        compiler_params=_params(blk, scr),
    )(qf, kv, kr, do, o, lse)


def cast_bf16(name, w, layer, idx):
    _, R, C = w.shape
    tr = _tile(R, 256, 16)

    def body(k_ref, w_ref, o_ref):
        o_ref[...] = w_ref[...].astype(BF16)

    return pl.pallas_call(
        body, name=name,
        grid_spec=pltpu.PrefetchScalarGridSpec(
            num_scalar_prefetch=1, grid=(R // tr,),
            in_specs=[pl.BlockSpec((None, tr, C), lambda r, k_ref: (layer, r, 0))],
            out_specs=pl.BlockSpec((None, tr, C), lambda r, k_ref: (k_ref[0], r, 0))),
        out_shape=jax.ShapeDtypeStruct((N_CHIPS, R, C), BF16),
    )(idx, w)


def adamw(name, w, g, m, v):
    R, C = w.shape
    tr = _tile(R, max(8, (1 << 18) // C // 8 * 8), 8)
    c1 = 1.0 - ADAM_B1 ** ADAM_STEP
    c2 = 1.0 - ADAM_B2 ** ADAM_STEP

    def body(w_ref, g_ref, m_ref, v_ref, d_ref, mo_ref, vo_ref):
        gv = g_ref[...]
        mn = ADAM_B1 * m_ref[...] + (1.0 - ADAM_B1) * gv
        vn = ADAM_B2 * v_ref[...] + (1.0 - ADAM_B2) * (gv * gv)
        mo_ref[...] = mn
        vo_ref[...] = vn
        d_ref[...] = -ADAM_LR * ((mn / c1) / (jnp.sqrt(vn / c2) + ADAM_EPS) + ADAM_WD * w_ref[...])

    spec = pl.BlockSpec((tr, C), lambda r: (r, 0))
    return pl.pallas_call(
        body, name=name, grid=(R // tr,), in_specs=[spec] * 4, out_specs=[spec] * 3,
        out_shape=[jax.ShapeDtypeStruct((R, C), F32)] * 3,
        compiler_params=_params(7 * _nbytes((tr, C), F32), 4 * _nbytes((tr, C), F32)),
    )(w, g, m, v)


def half_sum(name, dw, landed, idx):
    _, _, hr, C = dw.shape
    tr = _tile(hr, max(16, (1 << 18) // C // 16 * 16), 16)

    def body(i_ref, a_ref, b_ref, o_ref):
        o_ref[...] = (a_ref[...].astype(F32) + b_ref[...].astype(F32)).astype(o_ref.dtype)

    return pl.pallas_call(
        body, name=name,
        grid_spec=pltpu.PrefetchScalarGridSpec(
            num_scalar_prefetch=1, grid=(N_CHIPS, hr // tr),
            in_specs=[pl.BlockSpec((None, None, tr, C), lambda k, r, i_ref: (k, i_ref[1], r, 0)),
                      pl.BlockSpec((None, tr, C), lambda k, r, i_ref: (k, r, 0))],
            out_specs=pl.BlockSpec((None, tr, C), lambda k, r, i_ref: (k, r, 0))),
        out_shape=jax.ShapeDtypeStruct((N_CHIPS, hr, C), BF16),
    )(idx, dw, landed)


def chip_sum(name, part, landed, gbuf, layer, idx):
    _, hr, C = part.shape
    tr = _tile(hr, max(16, (1 << 18) // C // 16 * 16), 16)

    def body(i_ref, a_ref, b_ref, g_ref, o_ref):
        o_ref[...] = ((a_ref[...].astype(F32) + b_ref[0].astype(F32)) + b_ref[1].astype(F32)) + b_ref[2].astype(F32)

    return pl.pallas_call(
        body, name=name,
        grid_spec=pltpu.PrefetchScalarGridSpec(
            num_scalar_prefetch=1, grid=(hr // tr,),
            in_specs=[pl.BlockSpec((None, tr, C), lambda r, i_ref: (i_ref[0], r, 0)),
                      pl.BlockSpec((3, tr, C), lambda r, i_ref: (0, r, 0)),
                      pl.BlockSpec(memory_space=pl.ANY)],
            out_specs=pl.BlockSpec((None, None, tr, C), lambda r, i_ref: (layer, i_ref[1], r, 0))),
        out_shape=jax.ShapeDtypeStruct(gbuf.shape, F32),
        input_output_aliases={3: 0},
    )(idx, part, landed, gbuf)


ANY = pl.BlockSpec(memory_space=pl.ANY)


def _place():
    x, y, c = lax.axis_index("x"), lax.axis_index("y"), lax.axis_index("c")
    chips = [(1 - x, y), (x, 1 - y), (1 - x, 1 - y)]
    return x, y, c, chips


HBM = pl.BlockSpec(memory_space=pltpu.HBM)
SEM = pl.BlockSpec(memory_space=pltpu.SEMAPHORE)
EFFECT = pltpu.SideEffectType.DATAFLOW_SIDE_EFFECTING


def _in_hbm(a):
    return pltpu.with_memory_space_constraint(a, pltpu.HBM)


def _ici_copy(src, dst, send_sems, recv_sems, k, peer):
    return pltpu.make_async_remote_copy(src_ref=src, dst_ref=dst, send_sem=send_sems.at[k], recv_sem=recv_sems.at[k],
                                        device_id=peer, device_id_type=MESH)


def ici_start(name, bufs, lands, after, gather):
    n, nl = len(bufs), len(lands)

    def body(*refs):
        b_in = refs[:n]
        send_sems, recv_sems = refs[n + nl + 1], refs[n + nl + 2]
        b_out = refs[n + nl + 3:2 * n + nl + 3]
        l_out = refs[2 * n + nl + 3:2 * n + 2 * nl + 3]
        token = refs[-1]
        x, y, c, chips = _place()
        kme = 2 * x + y
        for i in range(n):
            for j in range(3):
                peer = (*chips[j], c)
                if gather:
                    _ici_copy(b_out[i].at[kme, c], b_out[i].at[kme, c], send_sems, recv_sems, 3 * i + j, peer).start()
                else:
                    kd = 2 * chips[j][0] + chips[j][1]
                    _ici_copy(b_out[i].at[kd], l_out[i].at[j], send_sems, recv_sems, 3 * i + j, peer).start()
        token[...] = jnp.zeros_like(token)

    arrays = [*bufs, *lands]
    outs = pl.pallas_call(
        body, name=name,
        in_specs=[HBM] * (n + nl) + [ANY],
        out_specs=(SEM, SEM, *[HBM] * (n + nl), pl.BlockSpec(memory_space=pltpu.VMEM)),
        out_shape=(pltpu.SemaphoreType.DMA((3 * n,)), pltpu.SemaphoreType.DMA((3 * n,)),
                   *[pltpu.HBM(a.shape, a.dtype) for a in arrays], jax.ShapeDtypeStruct((8, LANE), F32)),
        input_output_aliases={i: 2 + i for i in range(n + nl)},
        compiler_params=pltpu.CompilerParams(has_side_effects=EFFECT),
    )(*[_in_hbm(a) for a in arrays], after)
    return outs[0], outs[1], list(outs[2:2 + n]), list(outs[2 + n:2 + n + nl]), outs[-1]


def ici_wait(name, send_sems, recv_sems, bufs, lands, after, gather):
    n, nl = len(bufs), len(lands)

    def body(*refs):
        b_in, l_in = refs[:n], refs[n:n + nl]
        send_sems, recv_sems = refs[n + nl], refs[n + nl + 1]
        x, y, c, chips = _place()
        kme = 2 * x + y
        for i in range(n):
            for j in range(3):
                peer = (*chips[j], c)
                kj = 2 * chips[j][0] + chips[j][1]
                if gather:
                    _ici_copy(b_in[i].at[kme, c], b_in[i].at[kme, c], send_sems, recv_sems, 3 * i + j, peer).wait_send()
                    _ici_copy(b_in[i].at[kj, c], b_in[i].at[kj, c], send_sems, recv_sems, 3 * i + j, peer).wait_recv()
                else:
                    _ici_copy(b_in[i].at[kj], l_in[i].at[j], send_sems, recv_sems, 3 * i + j, peer).wait_send()
                    _ici_copy(b_in[i].at[kj], l_in[i].at[j], send_sems, recv_sems, 3 * i + j, peer).wait_recv()

    arrays = [*bufs, *lands]
    outs = pl.pallas_call(
        body, name=name,
        in_specs=[HBM] * (n + nl) + [SEM, SEM, ANY],
        out_specs=tuple([HBM] * (n + nl)),
        out_shape=tuple(pltpu.HBM(a.shape, a.dtype) for a in arrays),
        input_output_aliases={i: i for i in range(n + nl)},
        compiler_params=pltpu.CompilerParams(has_side_effects=EFFECT),
    )(*arrays, send_sems, recv_sems, after)
    return list(outs[:n]), list(outs[n:])


def gather_pair_pass(name, bufs):
    n = len(bufs)

    def body(*refs):
        b = refs[n:2 * n]
        send_sems, recv_sems = refs[2 * n:]
        x, y, c, chips = _place()
        sib = (x, y, 1 - c)

        def d2d(i, j, which):
            kj = 2 * chips[j][0] + chips[j][1]
            return _ici_copy(b[i].at[kj, which], b[i].at[kj, which], send_sems, recv_sems, 3 * i + j, sib)

        for i in range(n):
            for j in range(3):
                d2d(i, j, c).start()
        for i in range(n):
            for j in range(3):
                d2d(i, j, 1 - c).wait_recv()
        for i in range(n):
            for j in range(3):
                d2d(i, j, c).wait_send()

    return pl.pallas_call(
        body, name=name, in_specs=[ANY] * n, out_specs=[ANY] * n,
        out_shape=[jax.ShapeDtypeStruct(a.shape, a.dtype) for a in bufs],
        input_output_aliases={i: i for i in range(n)},
        scratch_shapes=[pltpu.SemaphoreType.DMA((3 * n,)), pltpu.SemaphoreType.DMA((3 * n,))],
    )(*bufs)


def pair_exchange(name, dws):
    n = len(dws)

    def body(*refs):
        ins, outs = refs[:n], refs[n:2 * n]
        send_sems, recv_sems = refs[2 * n:]
        x, y, c, _ = _place()
        copies = []
        for i in range(n):
            copies.append(pltpu.make_async_remote_copy(
                src_ref=ins[i].at[:, 1 - c], dst_ref=outs[i],
                send_sem=send_sems.at[i], recv_sem=recv_sems.at[i],
                device_id=(x, y, 1 - c), device_id_type=MESH))
            copies[i].start()
        for cp in copies:
            cp.wait_recv()
        for cp in copies:
            cp.wait_send()

    return pl.pallas_call(
        body, name=name, in_specs=[ANY] * n, out_specs=[ANY] * n,
        out_shape=[jax.ShapeDtypeStruct((N_CHIPS, *d.shape[2:]), d.dtype) for d in dws],
        scratch_shapes=[pltpu.SemaphoreType.DMA((n,)), pltpu.SemaphoreType.DMA((n,))],
    )(*dws)


def pair_assemble(gbufs):
    n = len(gbufs)

    def body(*refs):
        bufs = refs[n:2 * n]
        send_sems, recv_sems = refs[2 * n:]
        x, y, c, _ = _place()
        copies = []
        for i in range(n):
            copies.append(pltpu.make_async_remote_copy(
                src_ref=bufs[i].at[:, c], dst_ref=bufs[i].at[:, c],
                send_sem=send_sems.at[i], recv_sem=recv_sems.at[i],
                device_id=(x, y, 1 - c), device_id_type=MESH))
            copies[i].start()
        for i in range(n):
            pltpu.make_async_remote_copy(
                src_ref=bufs[i].at[:, 1 - c], dst_ref=bufs[i].at[:, 1 - c],
                send_sem=send_sems.at[i], recv_sem=recv_sems.at[i],
                device_id=(x, y, 1 - c), device_id_type=MESH).wait_recv()
        for cp in copies:
            cp.wait_send()

    return pl.pallas_call(
        body, name="grad_pair_assemble", in_specs=[ANY] * n, out_specs=[ANY] * n,
        out_shape=[jax.ShapeDtypeStruct(g.shape, g.dtype) for g in gbufs],
        input_output_aliases={i: i for i in range(n)},
        scratch_shapes=[pltpu.SemaphoreType.DMA((n,)), pltpu.SemaphoreType.DMA((n,))],
    )(*gbufs)


def all_reduce_small(vec):
    NR = vec.shape[0]
    flips = [(fx, fy, fc) for fx in (0, 1) for fy in (0, 1) for fc in (0, 1)][1:]

    def body(v_ref, o_ref, buf, send_sems, recv_sems):
        x, y, c, _ = _place()
        me = 4 * x + 2 * y + c
        buf[me] = v_ref[...]
        copies = []
        for j, (fx, fy, fc) in enumerate(flips):
            peer = (1 - x if fx else x, 1 - y if fy else y, 1 - c if fc else c)
            copies.append(pltpu.make_async_remote_copy(
                src_ref=v_ref, dst_ref=buf.at[me], send_sem=send_sems.at[j], recv_sem=recv_sems.at[j],
                device_id=peer, device_id_type=MESH))
            copies[j].start()
        for cp in copies:
            cp.wait_recv()
        for cp in copies:
            cp.wait_send()
        acc = buf[0]
        for d in range(1, 8):
            acc = acc + buf[d]
        o_ref[...] = acc

    return pl.pallas_call(
        body, name="all_reduce_small",
        in_specs=[pl.BlockSpec(memory_space=pltpu.VMEM)], out_specs=pl.BlockSpec(memory_space=pltpu.VMEM),
        out_shape=jax.ShapeDtypeStruct((NR, LANE), F32),
        scratch_shapes=[pltpu.VMEM((8, NR, LANE), F32), pltpu.SemaphoreType.DMA((7,)),
                        pltpu.SemaphoreType.DMA((7,))],
    )(vec)


def _pack(arrays):
    flat = jnp.concatenate([a.reshape(-1).astype(F32) for a in arrays])
    n = flat.shape[0]
    npad = -(-n // (8 * LANE)) * (8 * LANE)
    return jnp.pad(flat, (0, npad - n)).reshape(npad // LANE, LANE)


def _unpack(buf, like):
    flat = buf.reshape(-1)
    out, off = [], 0
    for a in like:
        out.append(flat[off:off + a.size].reshape(a.shape))
        off += a.size
    return out


def kernel(x, ffn1_norm, ffn1_w_in, ffn1_w_out, mix_norm, ffn2_norm, ffn2_w_in, ffn2_w_out, a_w_qkv, a_rel_bias, a_w_o, kv_norm, kv_w_down, kv_latent_norm, kv_w_up, b_w_dq, b_q_norm, b_w_uq, b_w_o, final_norm, loss_target, m_ffn1_norm, m_ffn1_w_in, m_ffn1_w_out, m_mix_norm, m_ffn2_norm, m_ffn2_w_in, m_ffn2_w_out, m_a_w_qkv, m_a_rel_bias, m_a_w_o, m_kv_norm, m_kv_w_down, m_kv_latent_norm, m_kv_w_up, m_b_w_dq, m_b_q_norm, m_b_w_uq, m_b_w_o, m_final_norm, v_ffn1_norm, v_ffn1_w_in, v_ffn1_w_out, v_mix_norm, v_ffn2_norm, v_ffn2_w_in, v_ffn2_w_out, v_a_w_qkv, v_a_rel_bias, v_a_w_o, v_kv_norm, v_kv_w_down, v_kv_latent_norm, v_kv_w_up, v_b_w_dq, v_b_q_norm, v_b_w_uq, v_b_w_o, v_final_norm):
    B, S, D = x.shape
    T = B * S
    HB = D // 128
    QL = b_q_norm.shape[-1]
    KVL = kv_latent_norm.shape[0]
    hpc = HB // N_CHIPS
    tabs = rope_tables(S)
    idx = jnp.stack([2 * lax.axis_index("x") + lax.axis_index("y"), lax.axis_index("c")]).astype(I32)

    def halves(a):
        return a.reshape(*a.shape[:-2], 2, a.shape[-2] // 2, a.shape[-1])

    def whole(a):
        return a.reshape(*a.shape[:-3], 2 * a.shape[-2], a.shape[-1])

    kv_w_down_p = jnp.pad(kv_w_down, ((0, 0), (0, LANE - ROPE)))[None]
    b_w_uq_p = jnp.pad(b_w_uq.reshape(1, QL, hpc, NOPE + ROPE),
                       ((0, 0), (0, 0), (0, 0), (0, LANE - ROPE))).reshape(1, QL, hpc * 256)
    sharded = [("ffn1_w_in", ffn1_w_in), ("ffn1_w_out", ffn1_w_out), ("ffn2_w_in", ffn2_w_in),
               ("ffn2_w_out", ffn2_w_out), ("a_w_qkv", a_w_qkv), ("a_w_o", a_w_o),
               ("kv_w_down", kv_w_down_p), ("kv_w_up", kv_w_up[None]), ("b_w_dq", b_w_dq),
               ("b_w_uq", b_w_uq_p), ("b_w_o", b_w_o)]
    pieces = [(a, l) for a, (_, w) in enumerate(sharded) for l in range(w.shape[0])]
    names = [nm for nm, _ in sharded]
    own = {(names[a], l): cast_bf16(f"cast_{names[a]}_{l}", sharded[a][1], l, idx) for a, l in pieces}
    W = {}

    gather_groups = [
        [("ffn1_w_in", 0), ("ffn1_w_out", 0)],
        [("a_w_qkv", 0), ("a_w_o", 0), ("ffn2_w_in", 0), ("ffn2_w_out", 0), ("kv_w_down", 0), ("kv_w_up", 0)],
        [("ffn1_w_in", 1), ("ffn1_w_out", 1), ("b_w_dq", 0), ("b_w_uq", 0), ("b_w_o", 0), ("ffn2_w_in", 1),
         ("ffn2_w_out", 1)]]

    def gather_start(g, after):
        keys = gather_groups[g]
        ss, rs, bufs, _, token = ici_start(f"gather_start_{g}", [halves(own[k]) for k in keys], [], after, True)
        return (g, ss, rs, bufs), token

    def gather_finish(state, after):
        g, ss, rs, bufs = state
        bufs, _ = ici_wait(f"gather_wait_{g}", ss, rs, bufs, [], after, True)
        full = gather_pair_pass(f"gather_pair_{g}", bufs)
        for k, w in zip(gather_groups[g], full):
            W[k] = whole(w)
        return full[0]

    def tied(a, token):
        return a + token[0, 0]

    def col(nm, l=0):
        return W[(nm, l)]

    def row(nm, l=0):
        w = W[(nm, l)]
        return w.reshape(N_CHIPS * w.shape[1], w.shape[2])

    bias = rel_bias_tile("rel_bias_tile", a_rel_bias[0])

    def ffn_fwd(tag, h, g, w_in, w_out):
        xn = rms_fwd(f"{tag}_norm", h, g)
        u, act = ffn_in_act(f"{tag}_in", xn, w_in)
        return mm_roww(f"{tag}_out", act, w_out, F32, res=h, alpha=0.5), (xn, u, act)

    h0 = x.reshape(T, D)
    st0, tok0 = gather_start(0, h0)
    done0 = gather_finish(st0, tok0)
    st1, tok1 = gather_start(1, done0)
    h1, sv_f1a = ffn_fwd("l0f1", h0, tied(ffn1_norm[0], tok1), col("ffn1_w_in", 0), row("ffn1_w_out", 0))
    done1 = gather_finish(st1, h1)
    st2, tok2 = gather_start(2, done1)
    hn_a = rms_fwd("l0mix_norm", h1, tied(mix_norm[0], tok2))
    qkv = mm_colw("l0_qkv", hn_a, col("a_w_qkv"), BF16).reshape(B, S, 3 * D)
    o_a = attn_a_fwd("l0_attn", qkv, bias).reshape(T, D)
    h2 = mm_roww("l0_attn_out", o_a, row("a_w_o"), F32, res=h1)
    h3, sv_f2a = ffn_fwd("l0f2", h2, ffn2_norm[0], col("ffn2_w_in", 0), row("ffn2_w_out", 0))

    hkv = rms_fwd("kv_norm", h3, kv_norm)
    ckr = mm_roww("kv_down", hkv, row("kv_w_down"), F32)
    ckv, kr = kvprep_fwd("kv_prep", ckr, kv_latent_norm, tabs, B, S)
    kvb = mm_colw("kv_up", ckv, col("kv_w_up"), BF16).reshape(B, S, HB * 256)
    gather_finish(st2, kvb)

    h4, sv_f1b = ffn_fwd("l1f1", h3, ffn1_norm[1], col("ffn1_w_in", 1), row("ffn1_w_out", 1))
    hn_b = rms_fwd("l1mix_norm", h4, mix_norm[1])
    cqp = mm_roww("l1_dq", hn_b, row("b_w_dq"), F32)
    cq = rms_fwd("l1_q_norm", cqp, b_q_norm[0])
    qpre = mm_colw("l1_uq", cq, col("b_w_uq"), F32)
    qf = qprep("l1_q_rope", qpre, tabs, B, S, bwd=False).reshape(B, S, HB * 256)
    o_b, lse = mla_fwd("l1_attn", qf, kvb, kr)
    h5 = mm_roww("l1_attn_out", o_b.reshape(T, HB * LANE), row("b_w_o"), F32, res=h4)
    h6, sv_f2b = ffn_fwd("l1f2", h5, ffn2_norm[1], col("ffn2_w_in", 1), row("ffn2_w_out", 1))

    dh, g_final, loss_part = loss_head("loss_head", h6, final_norm, loss_target.reshape(T, D))

    gw = {}
    gbufs = {nm: lax.empty(halves(w).shape, F32) for nm, w in sharded}

    def reduce_start(r, keys, after):
        dws = [halves(gw[k]) for k in keys]
        landed = pair_exchange(f"grad_pair_exchange_{r}", dws)
        parts = [half_sum(f"half_sum_{r}_{i}", dws[i], landed[i], idx) for i in range(len(keys))]
        lands = [lax.empty((3, *p.shape[1:]), p.dtype) for p in parts]
        ss, rs, parts, lands, token = ici_start(f"reduce_start_{r}", parts, lands, after, False)
        return (r, keys, ss, rs, parts, lands), token

    def reduce_finish(state, after):
        r, keys, ss, rs, parts, lands = state
        parts, lands = ici_wait(f"reduce_wait_{r}", ss, rs, parts, lands, after, False)
        for i, (nm, l) in enumerate(keys):
            gbufs[nm] = chip_sum(f"chip_sum_{r}_{i}", parts[i], lands[i], gbufs[nm], l, idx)
        return gbufs[keys[0][0]]

    def ffn_bwd(tag, dh, h_in, g, w_in, w_out, saved, key_in, key_out, after=None):
        xn, u, act = saved
        du = ffn_dact(f"{tag}_dact", dh, w_out, u, after=after)
        dwo = mm_droww(f"{tag}_dwout", act, dh, alpha=0.5)
        gw[key_out] = dwo.reshape(N_CHIPS, dwo.shape[0] // N_CHIPS, dwo.shape[1])
        gw[key_in] = mm_dcolw(f"{tag}_dwin", xn, du, pair_layout=True)
        dxn = mm_colw_t(f"{tag}_dxn", du, w_in, F32, pair_layout=True)
        return rms_bwd(f"{tag}_dnorm", h_in, g, dxn, dres=dh)

    def chip_major(dw):
        return dw.reshape(N_CHIPS, dw.shape[0] // N_CHIPS, dw.shape[1])

    dh, g_f2b = ffn_bwd("l1f2b", dh, h5, ffn2_norm[1], col("ffn2_w_in", 1), row("ffn2_w_out", 1), sv_f2b,
                        ("ffn2_w_in", 1), ("ffn2_w_out", 1))
    red0, rtok0 = reduce_start(0, [("ffn2_w_in", 1), ("ffn2_w_out", 1)], dh)
    do_b = mm_roww_t("l1_attn_do", dh, row("b_w_o"), BF16, after=rtok0).reshape(B, S, HB * LANE)
    gw[("b_w_o", 0)] = chip_major(mm_droww("l1_attn_dwo", o_b.reshape(T, HB * LANE), dh))
    dqf, dkv, dkr = mla_bwd("l1_attn_bwd", qf, kvb, kr, do_b, o_b, lse)
    dqpre = qprep("l1_q_rope_bwd", dqf.reshape(T, HB * 256), tabs, B, S, bwd=True)
    gw[("b_w_uq", 0)] = mm_dcolw("l1_dwuq", cq, dqpre)
    dcq = mm_colw_t("l1_dcq", dqpre, col("b_w_uq"), F32)
    dcqp, g_qn = rms_bwd("l1_dq_norm", cqp, b_q_norm[0], dcq)
    gw[("b_w_dq", 0)] = chip_major(mm_droww("l1_dwdq", hn_b, dcqp))
    dhn = mm_roww_t("l1_dhn", dcqp, row("b_w_dq"), F32)
    dh, g_mixb = rms_bwd("l1_dmix", h4, mix_norm[1], dhn, dres=dh)
    dh, g_f1b = ffn_bwd("l1f1b", dh, h3, ffn1_norm[1], col("ffn1_w_in", 1), row("ffn1_w_out", 1), sv_f1b,
                        ("ffn1_w_in", 1), ("ffn1_w_out", 1))
    fin0 = reduce_finish(red0, dh)
    red1, rtok1 = reduce_start(1, [("b_w_o", 0), ("b_w_uq", 0), ("b_w_dq", 0), ("ffn1_w_in", 1), ("ffn1_w_out", 1)], fin0)
    dkv2 = dkv.reshape(T, HB * 256)
    gw[("kv_w_up", 0)] = mm_dcolw("kv_dwup", ckv, dkv2, after=rtok1)
    dckv = mm_colw_t("kv_dckv", dkv2, col("kv_w_up"), F32, after=rtok1)
    dckr, g_lat = kvprep_bwd("kv_prep_bwd", ckr, kv_latent_norm, dckv, dkr, tabs, B, S)
    gw[("kv_w_down", 0)] = chip_major(mm_droww("kv_dwdown", hkv, dckr))
    dhkv = mm_roww_t("kv_dhkv", dckr, row("kv_w_down"), F32)
    dh, g_kvn = rms_bwd("kv_dnorm", h3, kv_norm, dhkv, dres=dh)
    dh, g_f2a = ffn_bwd("l0f2b", dh, h2, ffn2_norm[0], col("ffn2_w_in", 0), row("ffn2_w_out", 0), sv_f2a,
                        ("ffn2_w_in", 0), ("ffn2_w_out", 0))
    do_a = mm_roww_t("l0_attn_do", dh, row("a_w_o"), BF16).reshape(B, S, D)
    gw[("a_w_o", 0)] = chip_major(mm_droww("l0_attn_dwo", o_a, dh))
    dq_a, dkv_a, dbias = attn_a_bwd("l0_attn_bwd", qkv, do_a, bias)
    dqkv = jnp.concatenate([dq_a.reshape(T, D), dkv_a.reshape(T, 2 * D)], axis=1)
    gw[("a_w_qkv", 0)] = mm_dcolw("l0_dwqkv", hn_a, dqkv)
    dhn = mm_colw_t("l0_dhn", dqkv, col("a_w_qkv"), F32)
    dh, g_mixa = rms_bwd("l0_dmix", h1, mix_norm[0], dhn, dres=dh)
    fin1 = reduce_finish(red1, dh)
    red2, rtok2 = reduce_start(2, [("kv_w_up", 0), ("kv_w_down", 0), ("ffn2_w_in", 0), ("ffn2_w_out", 0),
                                   ("a_w_o", 0), ("a_w_qkv", 0)], fin1)
    dh, g_f1a = ffn_bwd("l0f1b", dh, h0, ffn1_norm[0], col("ffn1_w_in", 0), row("ffn1_w_out", 0), sv_f1a,
                        ("ffn1_w_in", 0), ("ffn1_w_out", 0), after=rtok2)
    grad_x = dh.reshape(B, S, D)
    g_rel = rel_bias_grad("rel_bias_grad", dbias)[:, :2 * MAX_REL + 1][None]
    fin2 = reduce_finish(red2, dh)
    red3, rtok3 = reduce_start(3, [("ffn1_w_in", 0), ("ffn1_w_out", 0)], fin2)
    reduce_finish(red3, rtok3)

    full = [whole(g) for g in pair_assemble([gbufs[nm] for nm in names])]
    G = {nm: g for (nm, _), g in zip(sharded, full)}
    G["kv_w_down"] = G["kv_w_down"][0, :, :KVL + ROPE]
    G["kv_w_up"] = G["kv_w_up"][0]
    G["b_w_uq"] = G["b_w_uq"].reshape(1, QL, hpc, 256)[..., :NOPE + ROPE].reshape(b_w_uq.shape)

    small = [("ffn1_norm", jnp.stack([g_f1a, g_f1b])), ("mix_norm", jnp.stack([g_mixa, g_mixb])),
             ("ffn2_norm", jnp.stack([g_f2a, g_f2b])), ("a_rel_bias", g_rel), ("kv_norm", g_kvn),
             ("kv_latent_norm", g_lat), ("b_q_norm", g_qn[None]), ("final_norm", g_final)]
    red = all_reduce_small(_pack([loss_part] + [g for _, g in small]))
    unpacked = _unpack(red, [loss_part] + [g for _, g in small])
    loss = unpacked[0][0, 0]
    for (nm, _), g in zip(small, unpacked[1:]):
        G[nm] = g

    given = dict(ffn1_norm=(ffn1_norm, m_ffn1_norm, v_ffn1_norm), ffn1_w_in=(ffn1_w_in, m_ffn1_w_in, v_ffn1_w_in),
                 ffn1_w_out=(ffn1_w_out, m_ffn1_w_out, v_ffn1_w_out), mix_norm=(mix_norm, m_mix_norm, v_mix_norm),
                 ffn2_norm=(ffn2_norm, m_ffn2_norm, v_ffn2_norm), ffn2_w_in=(ffn2_w_in, m_ffn2_w_in, v_ffn2_w_in),
                 ffn2_w_out=(ffn2_w_out, m_ffn2_w_out, v_ffn2_w_out), a_w_qkv=(a_w_qkv, m_a_w_qkv, v_a_w_qkv),
                 a_rel_bias=(a_rel_bias, m_a_rel_bias, v_a_rel_bias), a_w_o=(a_w_o, m_a_w_o, v_a_w_o),
                 kv_norm=(kv_norm, m_kv_norm, v_kv_norm), kv_w_down=(kv_w_down, m_kv_w_down, v_kv_w_down),
                 kv_latent_norm=(kv_latent_norm, m_kv_latent_norm, v_kv_latent_norm),
                 kv_w_up=(kv_w_up, m_kv_w_up, v_kv_w_up), b_w_dq=(b_w_dq, m_b_w_dq, v_b_w_dq),
                 b_q_norm=(b_q_norm, m_b_q_norm, v_b_q_norm), b_w_uq=(b_w_uq, m_b_w_uq, v_b_w_uq),
                 b_w_o=(b_w_o, m_b_w_o, v_b_w_o), final_norm=(final_norm, m_final_norm, v_final_norm))
    order = list(given)
    delta, new_m, new_v = {}, {}, {}
    small_names = [nm for nm, _ in small]
    packed = [_pack([given[nm][k] for nm in small_names]) for k in range(3)]
    outs = adamw("adamw_small", packed[0], _pack([G[nm] for nm in small_names]), packed[1], packed[2])
    for dst, buf in zip((delta, new_m, new_v), outs):
        for nm, a in zip(small_names, _unpack(buf, [given[nm][0] for nm in small_names])):
            dst[nm] = a
    for nm, _ in sharded:
        w, m, v = given[nm]
        g = G[nm].reshape(w.shape)
        G[nm] = g
        two = lambda a: a.reshape(-1, a.shape[-1])
        d_, m_, v_ = adamw(f"adamw_{nm}", two(w), two(g), two(m), two(v))
        delta[nm], new_m[nm], new_v[nm] = d_.reshape(w.shape), m_.reshape(w.shape), v_.reshape(w.shape)

    return (loss, grad_x, *[G[n] for n in order], *[delta[n] for n in order],
            *[new_m[n] for n in order], *[new_v[n] for n in order])
```

```python
import functools
import math

import jax
import jax.numpy as jnp
from jax import lax
from jax.experimental import pallas as pl
from jax.experimental.pallas import tpu as pltpu

F32 = jnp.float32
BF16 = jnp.bfloat16
I32 = jnp.int32

CHUNK = 64
CHUNK_SHIFT = 6
HEAD_DIM_A = 64
LEFT_CHUNKS = 8
MAX_REL = 128
REL_PAD = 384
QROWS = 2 * CHUNK
WIN = (LEFT_CHUNKS + 2) * CHUNK
PADR = LEFT_CHUNKS * CHUNK
NOPE = 128
ROPE = 64
EPS = 1e-6
NEG_INF = -1e30
ROPE_THETA = 10000.0
ADAM_LR, ADAM_B1, ADAM_B2, ADAM_EPS, ADAM_WD, ADAM_STEP = 0.001, 0.9, 0.999, 1e-08, 0.01, 10
N_CHIPS = 4
LANE = 128
MESH = pl.DeviceIdType.MESH
VMEM_CAP_MB = 60

NN = (((1,), (0,)), ((), ()))
NT = (((1,), (1,)), ((), ()))
TN = (((0,), (0,)), ((), ()))


def _tile(n, pref, mult):
    t = (min(pref, n) // mult) * mult
    while t >= mult:
        if n % t == 0:
            return t
        t -= mult
    return n


def _nbytes(shape, dtype):
    return math.prod(shape) * jnp.dtype(dtype).itemsize


def _params(block_bytes, extra_bytes=0):
    need = 2 * block_bytes + extra_bytes
    mb = min(VMEM_CAP_MB, max(32, int(need * 1.25 / 2**20) + 8))
    return pltpu.CompilerParams(vmem_limit_bytes=mb * 2**20)


def _mm(name, kind, a, b, grid, a_spec, b_spec, o_spec, out_shape, out_dtype, blocks,
        red_axis=None, nred=1, alpha=1.0, res=None, res_spec=None, after=None):
    dims = {"nn": NN, "nt": NT, "tn": TN}[kind]
    has_res = res is not None
    acc_in_out = nred > 1 and out_dtype == F32 and not has_res and alpha == 1.0
    n_in = 2 + has_res + (after is not None)

    def body(*refs):
        a_ref, b_ref = refs[0], refs[1]
        r_ref = refs[2] if has_res else None
        o_ref = refs[n_in]
        p = lax.dot_general(a_ref[...].astype(BF16), b_ref[...].astype(BF16), dims,
                            preferred_element_type=F32)

        def finish(acc):
            y = acc if alpha == 1.0 else acc * alpha
            if has_res:
                y = r_ref[...] + y
            o_ref[...] = y.astype(o_ref.dtype)

        if nred == 1:
            finish(p)
            return
        k = pl.program_id(red_axis)
        tgt = o_ref if acc_in_out else refs[-1]

        @pl.when(k == 0)
        def _():
            tgt[...] = p

        @pl.when(k > 0)
        def _():
            tgt[...] += p

        if not acc_in_out:
            @pl.when(k == nred - 1)
            def _():
                finish(tgt[...])

    a_blk, b_blk, o_blk = blocks
    scratch = []
    extra = 0
    if nred > 1 and not acc_in_out:
        scratch = [pltpu.VMEM(o_blk, F32)]
        extra = _nbytes(o_blk, F32)
    blk = _nbytes(a_blk, a.dtype) + _nbytes(b_blk, b.dtype) + _nbytes(o_blk, out_dtype)
    ins, specs = [a, b], [a_spec, b_spec]
    if has_res:
        ins.append(res)
        specs.append(res_spec)
        blk += _nbytes(o_blk, res.dtype)
    if after is not None:
        ins.append(after)
        specs.append(pl.BlockSpec(memory_space=pl.ANY))
    extra += _nbytes(a_blk, BF16) + _nbytes(b_blk, BF16) + 2 * _nbytes(o_blk, F32)
    return pl.pallas_call(
        body, name=name, grid=grid, in_specs=specs, out_specs=o_spec,
        out_shape=jax.ShapeDtypeStruct(out_shape, out_dtype), scratch_shapes=scratch,
        compiler_params=_params(blk, extra),
    )(*ins)


def mm_colw(name, x, w3, out_dtype):
    T, K = x.shape
    _, _, nl = w3.shape
    tm = _tile(T, 512, 8)
    return _mm(name, "nn", x, w3, (N_CHIPS, T // tm),
               pl.BlockSpec((tm, K), lambda j, i: (i, 0)),
               pl.BlockSpec((None, K, nl), lambda j, i: (j, 0, 0)),
               pl.BlockSpec((tm, nl), lambda j, i: (i, j)),
               (T, N_CHIPS * nl), out_dtype, ((tm, K), (K, nl), (tm, nl)))


def _pair_chip(j):
    return (j % 2) * 2 + j // 2


def mm_colw_t(name, dy, w3, out_dtype, res=None, after=None, pair_layout=False):
    T = dy.shape[0]
    _, K, nl = w3.shape
    tm = _tile(T, 1024, 8)
    chip = _pair_chip if pair_layout else (lambda j: j)
    return _mm(name, "nt", dy, w3, (T // tm, N_CHIPS),
               pl.BlockSpec((tm, nl), lambda i, j: (i, j)),
               pl.BlockSpec((None, K, nl), lambda i, j: (chip(j), 0, 0)),
               pl.BlockSpec((tm, K), lambda i, j: (i, 0)),
               (T, K), out_dtype, ((tm, nl), (K, nl), (tm, K)),
               red_axis=1, nred=N_CHIPS, res=res,
               res_spec=pl.BlockSpec((tm, K), lambda i, j: (i, 0)), after=after)


def mm_dcolw(name, x, dy, after=None, pair_layout=False):
    T, K = x.shape
    nl = dy.shape[1] // N_CHIPS
    tt = _tile(T, 2048, 8)
    chip = _pair_chip if pair_layout else (lambda j: j)
    return _mm(name, "tn", x, dy, (N_CHIPS, T // tt),
               pl.BlockSpec((tt, K), lambda j, t: (t, 0)),
               pl.BlockSpec((tt, nl), lambda j, t: (t, j)),
               pl.BlockSpec((None, K, nl), lambda j, t: (chip(j), 0, 0)),
               (N_CHIPS, K, nl), BF16, ((tt, K), (tt, nl), (K, nl)),
               red_axis=1, nred=T // tt, after=after)


def mm_roww(name, x, w2, out_dtype, res=None, alpha=1.0):
    T, Kt = x.shape
    N = w2.shape[1]
    tm = _tile(T, 512, 8)
    return _mm(name, "nn", x, w2, (T // tm,),
               pl.BlockSpec((tm, Kt), lambda i: (i, 0)),
               pl.BlockSpec((Kt, N), lambda i: (0, 0)),
               pl.BlockSpec((tm, N), lambda i: (i, 0)),
               (T, N), out_dtype, ((tm, Kt), (Kt, N), (tm, N)),
               alpha=alpha, res=res, res_spec=pl.BlockSpec((tm, N), lambda i: (i, 0)))


def mm_roww_t(name, dy, w2, out_dtype, alpha=1.0, after=None):
    T, N = dy.shape
    Kt = w2.shape[0]
    tm = _tile(T, 512, 8)
    tk = _tile(Kt, 1408, LANE)
    return _mm(name, "nt", dy, w2, (Kt // tk, T // tm),
               pl.BlockSpec((tm, N), lambda j, i: (i, 0)),
               pl.BlockSpec((tk, N), lambda j, i: (j, 0)),
               pl.BlockSpec((tm, tk), lambda j, i: (i, j)),
               (T, Kt), out_dtype, ((tm, N), (tk, N), (tm, tk)), alpha=alpha, after=after)


def mm_droww(name, x, dy, alpha=1.0):
    T, Kt = x.shape
    N = dy.shape[1]
    tt = _tile(T, 2048, 8)
    tk = _tile(Kt, 1408, LANE)
    return _mm(name, "tn", x, dy, (Kt // tk, T // tt),
               pl.BlockSpec((tt, tk), lambda j, t: (t, j)),
               pl.BlockSpec((tt, N), lambda j, t: (t, 0)),
               pl.BlockSpec((tk, N), lambda j, t: (j, 0)),
               (Kt, N), BF16, ((tt, tk), (tt, N), (tk, N)),
               red_axis=1, nred=T // tt, alpha=alpha)


def rms_fwd(name, x, g):
    T, D = x.shape
    tm = _tile(T, 512, 8)

    def body(x_ref, g_ref, o_ref):
        xv = x_ref[...]
        r = lax.rsqrt(jnp.mean(xv * xv, axis=-1, keepdims=True) + EPS)
        o_ref[...] = (xv * r * g_ref[...]).astype(o_ref.dtype)

    return pl.pallas_call(
        body, name=name, grid=(T // tm,),
        in_specs=[pl.BlockSpec((tm, D), lambda i: (i, 0)), pl.BlockSpec((1, D), lambda i: (0, 0))],
        out_specs=pl.BlockSpec((tm, D), lambda i: (i, 0)),
        out_shape=jax.ShapeDtypeStruct((T, D), BF16),
        compiler_params=_params(_nbytes((tm, D), F32) * 2, 4 * _nbytes((tm, D), F32)),
    )(x, g.reshape(1, D))


def _rms_bwd_math(xv, gv, dy):
    r = lax.rsqrt(jnp.mean(xv * xv, axis=-1, keepdims=True) + EPS)
    xh = xv * r
    dyg = dy * gv
    dx = r * (dyg - xh * jnp.mean(dyg * xh, axis=-1, keepdims=True))
    dg = jnp.sum(dy * xh, axis=0, keepdims=True)
    return dx, dg


def rms_bwd(name, x, g, dy, dres=None):
    T, D = x.shape
    tm = _tile(T, 256, 8)
    has_res = dres is not None

    def body(*refs):
        x_ref, g_ref, dy_ref = refs[:3]
        r_ref = refs[3] if has_res else None
        dx_ref, dg_ref = refs[-2:]
        dx, dg = _rms_bwd_math(x_ref[...], g_ref[...], dy_ref[...].astype(F32))
        if has_res:
            dx = r_ref[...] + dx
        dx_ref[...] = dx

        @pl.when(pl.program_id(0) == 0)
        def _():
            dg_ref[...] = dg

        @pl.when(pl.program_id(0) > 0)
        def _():
            dg_ref[...] += dg

    row = pl.BlockSpec((tm, D), lambda i: (i, 0))
    vec = pl.BlockSpec((1, D), lambda i: (0, 0))
    ins, specs = [x, g.reshape(1, D), dy], [row, vec, row]
    if has_res:
        ins.append(dres)
        specs.append(row)
    dx, dg = pl.pallas_call(
        body, name=name, grid=(T // tm,), in_specs=specs, out_specs=[row, vec],
        out_shape=[jax.ShapeDtypeStruct((T, D), F32), jax.ShapeDtypeStruct((1, D), F32)],
        compiler_params=_params(_nbytes((tm, D), F32) * 4, 6 * _nbytes((tm, D), F32)),
    )(*ins)
    return dx, dg.reshape(D)


def ffn_in_act(name, x, w3):
    T, K = x.shape
    _, _, nl = w3.shape
    tm = _tile(T, 512, 8)

    def body(x_ref, wg_ref, wu_ref, u_ref, a_ref):
        xv = x_ref[...]
        g = jnp.dot(xv, wg_ref[...], preferred_element_type=F32)
        up = jnp.dot(xv, wu_ref[...], preferred_element_type=F32)
        u_ref[:, :nl] = g.astype(u_ref.dtype)
        u_ref[:, nl:] = up.astype(u_ref.dtype)
        a_ref[...] = (g * jax.nn.sigmoid(g) * up).astype(a_ref.dtype)

    blk = _nbytes((tm, K), BF16) + 2 * _nbytes((K, nl), BF16) + _nbytes((tm, 3 * nl), BF16)
    return pl.pallas_call(
        body, name=name, grid=(2, T // tm),
        in_specs=[pl.BlockSpec((tm, K), lambda p, i: (i, 0)),
                  pl.BlockSpec((None, K, nl), lambda p, i: (p, 0, 0)),
                  pl.BlockSpec((None, K, nl), lambda p, i: (p + 2, 0, 0))],
        out_specs=[pl.BlockSpec((tm, 2 * nl), lambda p, i: (i, p)), pl.BlockSpec((tm, nl), lambda p, i: (i, p))],
        out_shape=[jax.ShapeDtypeStruct((T, 4 * nl), BF16), jax.ShapeDtypeStruct((T, 2 * nl), BF16)],
        compiler_params=_params(blk, 4 * _nbytes((tm, nl), F32)),
    )(x, w3, w3)


def ffn_dact(name, dh, w_out, u, after=None):
    T, N = dh.shape
    F = w_out.shape[0]
    nl = F // 2
    tm = _tile(T, 512, 8)

    def body(*refs):
        d_ref, w_ref, u_ref = refs[:3]
        o_ref = refs[-1]
        dact = 0.5 * lax.dot_general(d_ref[...].astype(BF16), w_ref[...], NT, preferred_element_type=F32)
        g = u_ref[:, :nl].astype(F32)
        up = u_ref[:, nl:].astype(F32)
        sig = jax.nn.sigmoid(g)
        o_ref[:, :nl] = (dact * up * (sig * (1.0 + g * (1.0 - sig)))).astype(o_ref.dtype)
        o_ref[:, nl:] = (dact * (g * sig)).astype(o_ref.dtype)

    ins = [dh, w_out, u]
    specs = [pl.BlockSpec((tm, N), lambda p, i: (i, 0)), pl.BlockSpec((nl, N), lambda p, i: (p, 0)),
             pl.BlockSpec((tm, 2 * nl), lambda p, i: (i, p))]
    if after is not None:
        ins.append(after)
        specs.append(pl.BlockSpec(memory_space=pl.ANY))
    blk = _nbytes((tm, N), F32) + _nbytes((nl, N), BF16) + 2 * _nbytes((tm, 2 * nl), BF16)
    return pl.pallas_call(
        body, name=name, grid=(2, T // tm), in_specs=specs,
        out_specs=pl.BlockSpec((tm, 2 * nl), lambda p, i: (i, p)),
        out_shape=jax.ShapeDtypeStruct((T, 2 * F), BF16),
        compiler_params=_params(blk, 6 * _nbytes((tm, nl), F32)),
    )(*ins)


def loss_head(name, h, g, target):
    T, D = h.shape
    tm = _tile(T, 256, 8)

    def body(h_ref, g_ref, t_ref, dh_ref, dg_ref, loss_ref):
        xv = h_ref[...]
        gv = g_ref[...]
        r = lax.rsqrt(jnp.mean(xv * xv, axis=-1, keepdims=True) + EPS)
        err = xv * r * gv - t_ref[...]
        part = 0.5 * jnp.sum(jnp.mean(err * err, axis=-1, keepdims=True), axis=0, keepdims=True)
        dx, dg = _rms_bwd_math(xv, gv, err * (1.0 / D))
        dh_ref[...] = dx
        part = jnp.broadcast_to(part, (1, LANE))

        @pl.when(pl.program_id(0) == 0)
        def _():
            dg_ref[...] = dg
            loss_ref[...] = part

        @pl.when(pl.program_id(0) > 0)
        def _():
            dg_ref[...] += dg
            loss_ref[...] += part

    row = pl.BlockSpec((tm, D), lambda i: (i, 0))
    vec = pl.BlockSpec((1, D), lambda i: (0, 0))
    dh, dg, loss = pl.pallas_call(
        body, name=name, grid=(T // tm,), in_specs=[row, vec, row],
        out_specs=[row, vec, pl.BlockSpec((1, LANE), lambda i: (0, 0))],
        out_shape=[jax.ShapeDtypeStruct((T, D), F32), jax.ShapeDtypeStruct((1, D), F32),
                   jax.ShapeDtypeStruct((1, LANE), F32)],
        compiler_params=_params(_nbytes((tm, D), F32) * 3, 6 * _nbytes((tm, D), F32)),
    )(h, g.reshape(1, D), target)
    return dh, dg.reshape(D), loss


def rope_tables(S):
    half = ROPE // 2
    freqs = ROPE_THETA ** (-jnp.arange(half, dtype=F32) / half)
    ang = jnp.arange(S, dtype=F32)[:, None] * freqs[None, :]
    cos, sin = jnp.cos(ang), jnp.sin(ang)
    z = jnp.zeros_like(cos)
    ct = jnp.concatenate([cos, cos, z, z], axis=1)
    s1 = jnp.concatenate([-sin, z, z, z], axis=1)
    s2 = jnp.concatenate([z, sin, z, z], axis=1)
    return ct, s1, s2


def _rope_tile(t, ct, s1, s2):
    return t * ct + pltpu.roll(t, 96, 1) * s1 + pltpu.roll(t, 32, 1) * s2


def _rope_tile_bwd(d, ct, s1, s2):
    return d * ct + pltpu.roll(d * s1, 32, 1) + pltpu.roll(d * s2, 96, 1)


def qprep(name, q, tabs, B, S, bwd):
    T, W = q.shape
    nh = W // 256
    ts = _tile(S, 256, 8)
    fn = _rope_tile_bwd if bwd else _rope_tile

    def body(q_ref, ct_ref, s1_ref, s2_ref, o_ref):
        ct, s1, s2 = ct_ref[...], s1_ref[...], s2_ref[...]
        for h in range(nh):
            o_ref[0, :, 256 * h:256 * h + 128] = q_ref[0, :, 256 * h:256 * h + 128].astype(o_ref.dtype)
            t = q_ref[0, :, 256 * h + 128:256 * h + 256].astype(F32)
            o_ref[0, :, 256 * h + 128:256 * h + 256] = fn(t, ct, s1, s2).astype(o_ref.dtype)

    row = pl.BlockSpec((1, ts, W), lambda b, s: (b, s, 0))
    tab = pl.BlockSpec((ts, LANE), lambda b, s: (s, 0))
    out = pl.pallas_call(
        body, name=name, grid=(B, S // ts), in_specs=[row, tab, tab, tab], out_specs=row,
        out_shape=jax.ShapeDtypeStruct((B, S, W), BF16),
        compiler_params=_params(_nbytes((ts, W), F32) * 2, _nbytes((ts, W), F32) * 2),
    )(q.reshape(B, S, W), *tabs)
    return out.reshape(T, W)


def kvprep_fwd(name, ckr, g, tabs, B, S):
    T, W = ckr.shape
    KVL = W - LANE
    ts = _tile(S, 256, 8)

    def body(x_ref, g_ref, ct_ref, s1_ref, s2_ref, c_ref, k_ref):
        xv = x_ref[0, :, :KVL]
        r = lax.rsqrt(jnp.mean(xv * xv, axis=-1, keepdims=True) + EPS)
        c_ref[0] = (xv * r * g_ref[...]).astype(c_ref.dtype)
        k_ref[0] = _rope_tile(x_ref[0, :, KVL:], ct_ref[...], s1_ref[...], s2_ref[...]).astype(k_ref.dtype)

    tab = pl.BlockSpec((ts, LANE), lambda b, s: (s, 0))
    c, k = pl.pallas_call(
        body, name=name, grid=(B, S // ts),
        in_specs=[pl.BlockSpec((1, ts, W), lambda b, s: (b, s, 0)), pl.BlockSpec((1, KVL), lambda b, s: (0, 0)),
                  tab, tab, tab],
        out_specs=[pl.BlockSpec((1, ts, KVL), lambda b, s: (b, s, 0)),
                   pl.BlockSpec((1, ts, LANE), lambda b, s: (b, s, 0))],
        out_shape=[jax.ShapeDtypeStruct((B, S, KVL), BF16), jax.ShapeDtypeStruct((B, S, LANE), BF16)],
        compiler_params=_params(_nbytes((ts, W), F32) * 2, _nbytes((ts, W), F32) * 2),
    )(ckr.reshape(B, S, W), g.reshape(1, KVL), *tabs)
    return c.reshape(T, KVL), k


def kvprep_bwd(name, ckr, g, dc, dkr, tabs, B, S):
    T, W = ckr.shape
    KVL = W - LANE
    ts = _tile(S, 256, 8)

    def body(x_ref, g_ref, dc_ref, dk_ref, ct_ref, s1_ref, s2_ref, o_ref, dg_ref):
        dx, dg = _rms_bwd_math(x_ref[0, :, :KVL], g_ref[...], dc_ref[0])
        o_ref[0, :, :KVL] = dx
        o_ref[0, :, KVL:] = _rope_tile_bwd(dk_ref[0], ct_ref[...], s1_ref[...], s2_ref[...])
        first = (pl.program_id(0) == 0) & (pl.program_id(1) == 0)

        @pl.when(first)
        def _():
            dg_ref[...] = dg

        @pl.when(jnp.logical_not(first))
        def _():
            dg_ref[...] += dg

    tab = pl.BlockSpec((ts, LANE), lambda b, s: (s, 0))
    vec = pl.BlockSpec((1, KVL), lambda b, s: (0, 0))
    o, dg = pl.pallas_call(
        body, name=name, grid=(B, S // ts),
        in_specs=[pl.BlockSpec((1, ts, W), lambda b, s: (b, s, 0)), vec,
                  pl.BlockSpec((1, ts, KVL), lambda b, s: (b, s, 0)),
                  pl.BlockSpec((1, ts, LANE), lambda b, s: (b, s, 0)), tab, tab, tab],
        out_specs=[pl.BlockSpec((1, ts, W), lambda b, s: (b, s, 0)), vec],
        out_shape=[jax.ShapeDtypeStruct((B, S, W), F32), jax.ShapeDtypeStruct((1, KVL), F32)],
        compiler_params=_params(_nbytes((ts, W), F32) * 4, _nbytes((ts, W), F32) * 4),
    )(ckr.reshape(B, S, W), g.reshape(1, KVL), dc.reshape(B, S, KVL), dkr, *tabs)
    return o.reshape(T, W), dg.reshape(KVL)


DIAGS = 768


def _diag_onehot():
    col = lax.broadcasted_iota(I32, (REL_PAD, DIAGS), 1)
    row = lax.broadcasted_iota(I32, (REL_PAD, DIAGS), 0)
    idx = jnp.clip(PADR + QROWS - 1 - col, -MAX_REL, MAX_REL) + MAX_REL
    return (row == idx).astype(F32)


def rel_bias_tile(name, table):
    H = table.shape[0]
    tpad = jnp.pad(table, ((0, 0), (0, REL_PAD - table.shape[1])))

    def body(t_ref, o_ref):
        g = lax.dot_general(t_ref[...], _diag_onehot(), NN, precision=lax.Precision.HIGHEST,
                            preferred_element_type=F32)
        qc = jnp.right_shift(lax.broadcasted_iota(I32, (QROWS, WIN), 0), CHUNK_SHIFT)
        kc = jnp.right_shift(lax.broadcasted_iota(I32, (QROWS, WIN), 1), CHUNK_SHIFT)
        band = (kc >= qc) & (kc <= qc + LEFT_CHUNKS)
        for h in range(H):
            gb = jnp.broadcast_to(g[h:h + 1, :], (QROWS, DIAGS))
            tile = pltpu.roll(gb, DIAGS - (QROWS - 1), 1, stride=1, stride_axis=0)
            o_ref[h // 2, (h % 2) * QROWS:(h % 2 + 1) * QROWS, :] = jnp.where(band, tile[:, :WIN], NEG_INF)

    return pl.pallas_call(
        body, name=name, out_shape=jax.ShapeDtypeStruct((H // 2, 2 * QROWS, WIN), F32),
        compiler_params=_params(0, 2 * _nbytes((H // 2, 2 * QROWS, WIN), F32)),
    )(tpad)


def rel_bias_grad(name, dbias):
    H = 2 * dbias.shape[0]

    def body(d_ref, o_ref):
        flip = (lax.broadcasted_iota(I32, (QROWS, QROWS), 0) + lax.broadcasted_iota(I32, (QROWS, QROWS), 1)
                == QROWS - 1).astype(F32)
        rows = []
        for h in range(H):
            x = d_ref[h // 2, (h % 2) * QROWS:(h % 2 + 1) * QROWS, :]
            xr = lax.dot_general(flip, x, NN, precision=lax.Precision.HIGHEST, preferred_element_type=F32)
            xp = jnp.concatenate([xr, jnp.zeros((QROWS, DIAGS - WIN), F32)], axis=1)
            y = pltpu.roll(xp, 0, 1, stride=1, stride_axis=0)
            rows.append(jnp.sum(y, axis=0, keepdims=True))
        o_ref[...] = lax.dot_general(jnp.concatenate(rows, axis=0), _diag_onehot(), NT,
                                     precision=lax.Precision.HIGHEST, preferred_element_type=F32)

    return pl.pallas_call(
        body, name=name, out_shape=jax.ShapeDtypeStruct((H, REL_PAD), F32),
        compiler_params=_params(0, 2 * _nbytes(dbias.shape, F32)),
    )(dbias)


def _stack_pair(xp):
    lane = lax.broadcasted_iota(I32, xp.shape, 1)
    z = jnp.zeros_like(xp)
    return jnp.concatenate([jnp.where(lane < HEAD_DIM_A, xp, z), jnp.where(lane >= HEAD_DIM_A, xp, z)], axis=0)


def _unstack_pair(y):
    lane = lax.broadcasted_iota(I32, (QROWS, LANE), 1)
    return jnp.where(lane < HEAD_DIM_A, y[:QROWS], y[QROWS:])


def _attn_a_rowpen(j):
    w = lax.broadcasted_iota(I32, (1, WIN), 1)
    return jnp.where(w >= PADR - QROWS * j, 0.0, NEG_INF).astype(F32)


def _attn_a_load_bias(bias_hbm, bias_v, sem):
    cp = pltpu.make_async_copy(bias_hbm, bias_v, sem)
    cp.start()
    cp.wait()


def _attn_a_load_kv(qkv_hbm, b, kpad, vpad, sem, S, D):
    kpad[0:PADR, :] = jnp.zeros((PADR, D), BF16)
    vpad[0:PADR, :] = jnp.zeros((PADR, D), BF16)
    ck = pltpu.make_async_copy(qkv_hbm.at[b, :, pl.ds(D, D)], kpad.at[pl.ds(PADR, S), :], sem.at[0])
    cv = pltpu.make_async_copy(qkv_hbm.at[b, :, pl.ds(2 * D, D)], vpad.at[pl.ds(PADR, S), :], sem.at[1])
    ck.start()
    cv.start()
    ck.wait()
    cv.wait()


def _attn_a_exp(q2s, kp, bias, pen):
    s = lax.dot_general(q2s, kp, NT, preferred_element_type=F32) + bias + pen
    e = jnp.exp(s - jnp.max(s, axis=-1, keepdims=True))
    return e, 1.0 / jnp.sum(e, axis=-1, keepdims=True)


def attn_a_fwd(name, qkv, bias):
    B, S, D3 = qkv.shape
    D = D3 // 3
    H = D // HEAD_DIM_A
    nb = S // QROWS
    scale = HEAD_DIM_A ** -0.5

    def body(q_ref, bias_hbm, qkv_hbm, o_ref, kpad, vpad, bias_v, sem):
        b, j = pl.program_id(0), pl.program_id(1)

        @pl.when((b == 0) & (j == 0))
        def _():
            _attn_a_load_bias(bias_hbm, bias_v, sem.at[2])

        @pl.when(j == 0)
        def _():
            _attn_a_load_kv(qkv_hbm, b, kpad, vpad, sem, S, D)

        pen = _attn_a_rowpen(j)
        w0 = pl.multiple_of(j * QROWS, QROWS)
        for p in range(H // 2):
            ls = slice(p * LANE, (p + 1) * LANE)
            e, rl = _attn_a_exp(_stack_pair(q_ref[0, :, ls] * scale), kpad[pl.ds(w0, WIN), ls], bias_v[p], pen)
            o2 = jnp.dot(e.astype(BF16), vpad[pl.ds(w0, WIN), ls], preferred_element_type=F32) * rl
            o_ref[0, :, ls] = _unstack_pair(o2).astype(o_ref.dtype)

    scr = 2 * _nbytes((PADR + S, D), BF16) + _nbytes(bias.shape, F32) + 8 * _nbytes((2 * QROWS, WIN), F32)
    return pl.pallas_call(
        body, name=name, grid=(B, nb),
        in_specs=[pl.BlockSpec((1, QROWS, D), lambda b, j: (b, j, 0)),
                  pl.BlockSpec(memory_space=pl.ANY), pl.BlockSpec(memory_space=pl.ANY)],
        out_specs=pl.BlockSpec((1, QROWS, D), lambda b, j: (b, j, 0)),
        out_shape=jax.ShapeDtypeStruct((B, S, D), BF16),
        scratch_shapes=[pltpu.VMEM((PADR + S, D), BF16), pltpu.VMEM((PADR + S, D), BF16),
                        pltpu.VMEM(bias.shape, F32), pltpu.SemaphoreType.DMA((3,))],
        compiler_params=_params(2 * _nbytes((QROWS, D), BF16), scr),
    )(qkv, bias, qkv)


def attn_a_bwd(name, qkv, do, bias):
    B, S, D3 = qkv.shape
    D = D3 // 3
    H = D // HEAD_DIM_A
    nb = S // QROWS
    scale = HEAD_DIM_A ** -0.5

    def body(q_ref, do_ref, bias_hbm, qkv_hbm, dq_ref, dkv_hbm, dbias_hbm, kpad, vpad, dkacc, dvacc, bias_v, dbias_v, sem):
        b, j = pl.program_id(0), pl.program_id(1)

        @pl.when((b == 0) & (j == 0))
        def _():
            _attn_a_load_bias(bias_hbm, bias_v, sem.at[2])
            dbias_v[...] = jnp.zeros_like(dbias_v)

        @pl.when(j == 0)
        def _():
            _attn_a_load_kv(qkv_hbm, b, kpad, vpad, sem, S, D)
            dkacc[...] = jnp.zeros_like(dkacc)
            dvacc[...] = jnp.zeros_like(dvacc)

        pen = _attn_a_rowpen(j)
        w0 = pl.multiple_of(j * QROWS, QROWS)
        for p in range(H // 2):
            ls = slice(p * LANE, (p + 1) * LANE)
            q2s = _stack_pair(q_ref[0, :, ls] * scale)
            do2 = _stack_pair(do_ref[0, :, ls])
            kp = kpad[pl.ds(w0, WIN), ls]
            vp = vpad[pl.ds(w0, WIN), ls]
            e, rl = _attn_a_exp(q2s, kp, bias_v[p], pen)
            pr = e * rl
            dp = lax.dot_general(do2, vp, NT, preferred_element_type=F32)
            ds = pr * (dp - jnp.sum(pr * dp, axis=-1, keepdims=True))
            dbias_v[p] += ds
            dsb = ds.astype(BF16)
            dq_ref[0, :, ls] = _unstack_pair(jnp.dot(dsb, kp, preferred_element_type=F32)) * scale
            dkacc[pl.ds(w0, WIN), ls] += lax.dot_general(dsb, q2s, TN, preferred_element_type=F32)
            dvacc[pl.ds(w0, WIN), ls] += lax.dot_general(pr.astype(BF16), do2, TN, preferred_element_type=F32)

        @pl.when(j == nb - 1)
        def _():
            ck = pltpu.make_async_copy(dkacc.at[pl.ds(PADR, S), :], dkv_hbm.at[b, :, pl.ds(0, D)], sem.at[0])
            cv = pltpu.make_async_copy(dvacc.at[pl.ds(PADR, S), :], dkv_hbm.at[b, :, pl.ds(D, D)], sem.at[1])
            ck.start()
            cv.start()
            ck.wait()
            cv.wait()

        @pl.when((b == B - 1) & (j == nb - 1))
        def _():
            cb = pltpu.make_async_copy(dbias_v, dbias_hbm, sem.at[2])
            cb.start()
            cb.wait()

    blk = _nbytes((QROWS, D), BF16) * 2 + _nbytes((QROWS, D), F32)
    scr = (2 * _nbytes((PADR + S, D), BF16) + 2 * _nbytes((PADR + S, D), F32) + 2 * _nbytes(bias.shape, F32)
           + 8 * _nbytes((2 * QROWS, WIN), F32))
    return pl.pallas_call(
        body, name=name, grid=(B, nb),
        in_specs=[pl.BlockSpec((1, QROWS, D), lambda b, j: (b, j, 0)),
                  pl.BlockSpec((1, QROWS, D), lambda b, j: (b, j, 0)),
                  pl.BlockSpec(memory_space=pl.ANY), pl.BlockSpec(memory_space=pl.ANY)],
        out_specs=[pl.BlockSpec((1, QROWS, D), lambda b, j: (b, j, 0)),
                   pl.BlockSpec(memory_space=pl.ANY), pl.BlockSpec(memory_space=pl.ANY)],
        out_shape=[jax.ShapeDtypeStruct((B, S, D), F32), jax.ShapeDtypeStruct((B, S, 2 * D), F32),
                   jax.ShapeDtypeStruct(bias.shape, F32)],
        scratch_shapes=[pltpu.VMEM((PADR + S, D), BF16), pltpu.VMEM((PADR + S, D), BF16),
                        pltpu.VMEM((PADR + S, D), F32), pltpu.VMEM((PADR + S, D), F32),
                        pltpu.VMEM(bias.shape, F32), pltpu.VMEM(bias.shape, F32),
                        pltpu.SemaphoreType.DMA((3,))],
        compiler_params=_params(blk, scr),
    )(qkv, do, bias, qkv)


def _mla_raw_t(k2, kj, q, QB):
    return lax.dot_general(k2[_blk(kj, QB), :], q, NT, preferred_element_type=F32)


def _blk(kj, QB):
    return pl.ds(kj * QB, QB) if isinstance(kj, int) else pl.ds(pl.multiple_of(kj * QB, QB), QB)


def _mla_diag_pen(QB):
    kc = jnp.right_shift(lax.broadcasted_iota(I32, (QB, QB), 0), CHUNK_SHIFT)
    qc = jnp.right_shift(lax.broadcasted_iota(I32, (QB, QB), 1), CHUNK_SHIFT)
    return jnp.where(kc <= qc, 0.0, NEG_INF).astype(F32)


def _mla_fill_keys(kv_ref, kr_ref, k2):
    k2[:, :NOPE] = kv_ref[0, :, :NOPE]
    k2[:, NOPE:] = kr_ref[0]


def _t(x):
    return x.astype(F32).T


def mla_fwd(name, qf, kv, kr):
    B, S, W = qf.shape
    HB = W // 256
    QB = _tile(S, 256, CHUNK)
    nq = S // QB
    scale = (NOPE + ROPE) ** -0.5

    def body(q_ref, kv_ref, kr_ref, o_ref, lse_ref, k2, vt, st_buf, pen):
        qi = pl.program_id(2)

        @pl.when(qi == 0)
        def _():
            pen[...] = _mla_diag_pen(QB)
            _mla_fill_keys(kv_ref, kr_ref, k2)
            for kj in range(nq):
                vt[kj] = _t(kv_ref[0, kj * QB:(kj + 1) * QB, NOPE:]).astype(BF16)

        q = q_ref[0]
        st_buf[0] = _mla_raw_t(k2, 0, q, QB)

        def step(kj, carry):
            m, l, acc = carry
            cur = lax.rem(kj, 2)
            st_raw = st_buf[cur]
            st_buf[1 - cur] = _mla_raw_t(k2, jnp.minimum(kj + 1, qi), q, QB)
            st = st_raw * scale + jnp.where(kj == qi, pen[...], 0.0)
            m_new = jnp.maximum(m, jnp.max(st, axis=0, keepdims=True))
            a = jnp.exp(m - m_new)
            pt = jnp.exp(st - m_new)
            l = a * l + jnp.sum(pt, axis=0, keepdims=True)
            acc = a * acc + jnp.dot(vt[kj], pt.astype(BF16), preferred_element_type=F32)
            return m_new, l, acc

        init = (jnp.full((1, QB), NEG_INF, F32), jnp.zeros((1, QB), F32), jnp.zeros((NOPE, QB), F32))
        m, l, acc = lax.fori_loop(0, qi + 1, step, init)
        o_ref[0] = (acc * (1.0 / l)).T
        lse_ref[0, 0] = m + jnp.log(l)

    blk = (_nbytes((QB, 256), BF16) + _nbytes((S, 256), BF16) + _nbytes((S, LANE), BF16)
           + _nbytes((QB, LANE), F32))
    return pl.pallas_call(
        body, name=name, grid=(B, HB, nq),
        in_specs=[pl.BlockSpec((1, QB, 256), lambda b, h, i: (b, i, h)),
                  pl.BlockSpec((1, S, 256), lambda b, h, i: (b, 0, h)),
                  pl.BlockSpec((1, S, LANE), lambda b, h, i: (b, 0, 0))],
        out_specs=[pl.BlockSpec((1, QB, LANE), lambda b, h, i: (b, i, h)),
                   pl.BlockSpec((1, 1, 1, QB), lambda b, h, i: (b, h, 0, i))],
        out_shape=[jax.ShapeDtypeStruct((B, S, HB * LANE), F32), jax.ShapeDtypeStruct((B, HB, 1, S), F32)],
        scratch_shapes=[pltpu.VMEM((S, 256), BF16), pltpu.VMEM((nq, NOPE, QB), BF16),
                        pltpu.VMEM((2, QB, QB), F32), pltpu.VMEM((QB, QB), F32)],
        compiler_params=_params(blk, 2 * _nbytes((S, 256), BF16) + 10 * _nbytes((QB, QB), F32)),
    )(qf, kv, kr)


def mla_bwd(name, qf, kv, kr, do, o, lse):
    B, S, W = qf.shape
    HB = W // 256
    QB = _tile(S, 256, CHUNK)
    nq = S // QB
    scale = (NOPE + ROPE) ** -0.5

    def body(q_ref, kv_ref, kr_ref, do_ref, o_ref, lse_ref, dq_ref, dkv_ref, dkr_ref, k2, kt, dot_, delta, dqt,
             st_buf, dp_buf, pen):
        h = pl.program_id(1)
        pen[...] = _mla_diag_pen(QB)
        dkv_ref[...] = jnp.zeros_like(dkv_ref)

        @pl.when(h == 0)
        def _():
            dkr_ref[...] = jnp.zeros_like(dkr_ref)

        _mla_fill_keys(kv_ref, kr_ref, k2)
        for i in range(nq):
            rows = slice(i * QB, (i + 1) * QB)
            kt[i] = _t(k2[rows, :]).astype(BF16)
            dot32 = _t(do_ref[0, rows, :])
            delta[i] = jnp.sum(dot32 * o_ref[0, rows, :].T, axis=0, keepdims=True)
            dot_[i] = dot32.astype(BF16)

        for qi in range(nq):
            rows = slice(qi * QB, (qi + 1) * QB)
            q = q_ref[0, rows, :]
            dob = do_ref[0, rows, :]
            lse_q = lse_ref[0, 0, :, rows]
            delta_q = delta[qi]
            dqt[...] = jnp.zeros_like(dqt)

            def raw(kj, slot, q=q, qi=qi):
                st_buf[slot] = _mla_raw_t(k2, kj, q, QB)
                dp_buf[slot] = jnp.dot(kv_ref[0, _blk(kj, QB), NOPE:], dot_[qi], preferred_element_type=F32)

            raw(0, 0)

            def step(kj, carry, q=q, dob=dob, lse_q=lse_q, delta_q=delta_q, qi=qi, raw=raw):
                ks = pl.ds(pl.multiple_of(kj * QB, QB), QB)
                cur = lax.rem(kj, 2)
                st_raw, dp_raw = st_buf[cur], dp_buf[cur]
                raw(jnp.minimum(kj + 1, qi), 1 - cur)
                pt = jnp.exp(st_raw * scale + jnp.where(kj == qi, pen[...], 0.0) - lse_q)
                dst = (pt * (dp_raw - delta_q) * scale).astype(BF16)
                dkv_ref[0, ks, NOPE:] += jnp.dot(pt.astype(BF16), dob, preferred_element_type=F32)
                dk2 = jnp.dot(dst, q, preferred_element_type=F32)
                dkv_ref[0, ks, :NOPE] += dk2[:, :NOPE]
                dkr_ref[0, ks, :] += dk2[:, NOPE:]
                dqt[...] += jnp.dot(kt[kj], dst, preferred_element_type=F32)
                return carry

            lax.fori_loop(0, qi + 1, step, 0)
            dq_ref[0, rows, :] = dqt[...].T

    head = lambda w: pl.BlockSpec((1, S, w), lambda b, h: (b, 0, h))
    shared = pl.BlockSpec((1, S, LANE), lambda b, h: (b, 0, 0))
    blk = (2 * _nbytes((S, 256), BF16) + 2 * _nbytes((S, LANE), BF16) + _nbytes((S, LANE), F32)
           + 2 * _nbytes((S, 256), F32) + _nbytes((S, LANE), F32))
    scr = 3 * _nbytes((S, 256), BF16) + 14 * _nbytes((QB, QB), F32)
    return pl.pallas_call(
        body, name=name, grid=(B, HB),
        in_specs=[head(256), head(256), shared, head(LANE), head(LANE),
                  pl.BlockSpec((1, 1, 1, S), lambda b, h: (b, h, 0, 0))],
        out_specs=[head(256), head(256), shared],
        out_shape=[jax.ShapeDtypeStruct((B, S, W), F32), jax.ShapeDtypeStruct((B, S, W), F32),
                   jax.ShapeDtypeStruct((B, S, LANE), F32)],
        scratch_shapes=[pltpu.VMEM((S, 256), BF16), pltpu.VMEM((nq, 256, QB), BF16),
                        pltpu.VMEM((nq, NOPE, QB), BF16), pltpu.VMEM((nq, 1, QB), F32),
                        pltpu.VMEM((256, QB), F32), pltpu.VMEM((2, QB, QB), F32), pltpu.VMEM((2, QB, QB), F32),
                        pltpu.VMEM((QB, QB), F32)],
        compiler_params=_params(blk, scr),
    )(qf, kv, kr, do, o, lse)


def cast_bf16(name, w, layer, idx):
    _, R, C = w.shape
    tr = _tile(R, 256, 16)

    def body(k_ref, w_ref, o_ref):
        o_ref[...] = w_ref[...].astype(BF16)

    return pl.pallas_call(
        body, name=name,
        grid_spec=pltpu.PrefetchScalarGridSpec(
            num_scalar_prefetch=1, grid=(R // tr,),
            in_specs=[pl.BlockSpec((None, tr, C), lambda r, k_ref: (layer, r, 0))],
            out_specs=pl.BlockSpec((None, tr, C), lambda r, k_ref: (k_ref[0], r, 0))),
        out_shape=jax.ShapeDtypeStruct((N_CHIPS, R, C), BF16),
    )(idx, w)


def adamw(name, w, g, m, v):
    R, C = w.shape
    tr = _tile(R, max(8, (1 << 18) // C // 8 * 8), 8)
    c1 = 1.0 - ADAM_B1 ** ADAM_STEP
    c2 = 1.0 - ADAM_B2 ** ADAM_STEP

    def body(w_ref, g_ref, m_ref, v_ref, d_ref, mo_ref, vo_ref):
        gv = g_ref[...]
        mn = ADAM_B1 * m_ref[...] + (1.0 - ADAM_B1) * gv
        vn = ADAM_B2 * v_ref[...] + (1.0 - ADAM_B2) * (gv * gv)
        mo_ref[...] = mn
        vo_ref[...] = vn
        d_ref[...] = -ADAM_LR * ((mn / c1) / (jnp.sqrt(vn / c2) + ADAM_EPS) + ADAM_WD * w_ref[...])

    spec = pl.BlockSpec((tr, C), lambda r: (r, 0))
    return pl.pallas_call(
        body, name=name, grid=(R // tr,), in_specs=[spec] * 4, out_specs=[spec] * 3,
        out_shape=[jax.ShapeDtypeStruct((R, C), F32)] * 3,
        compiler_params=_params(7 * _nbytes((tr, C), F32), 4 * _nbytes((tr, C), F32)),
    )(w, g, m, v)


def half_sum(name, dw, landed, idx):
    _, _, hr, C = dw.shape
    tr = _tile(hr, max(16, (1 << 18) // C // 16 * 16), 16)

    def body(i_ref, a_ref, b_ref, o_ref):
        o_ref[...] = (a_ref[...].astype(F32) + b_ref[...].astype(F32)).astype(o_ref.dtype)

    return pl.pallas_call(
        body, name=name,
        grid_spec=pltpu.PrefetchScalarGridSpec(
            num_scalar_prefetch=1, grid=(N_CHIPS, hr // tr),
            in_specs=[pl.BlockSpec((None, None, tr, C), lambda k, r, i_ref: (k, i_ref[1], r, 0)),
                      pl.BlockSpec((None, tr, C), lambda k, r, i_ref: (k, r, 0))],
            out_specs=pl.BlockSpec((None, tr, C), lambda k, r, i_ref: (k, r, 0))),
        out_shape=jax.ShapeDtypeStruct((N_CHIPS, hr, C), BF16),
    )(idx, dw, landed)


def chip_sum(name, part, landed, gbuf, layer, idx):
    _, hr, C = part.shape
    tr = _tile(hr, max(16, (1 << 18) // C // 16 * 16), 16)

    def body(i_ref, a_ref, b_ref, g_ref, o_ref):
        o_ref[...] = ((a_ref[...].astype(F32) + b_ref[0].astype(F32)) + b_ref[1].astype(F32)) + b_ref[2].astype(F32)

    return pl.pallas_call(
        body, name=name,
        grid_spec=pltpu.PrefetchScalarGridSpec(
            num_scalar_prefetch=1, grid=(hr // tr,),
            in_specs=[pl.BlockSpec((None, tr, C), lambda r, i_ref: (i_ref[0], r, 0)),
                      pl.BlockSpec((3, tr, C), lambda r, i_ref: (0, r, 0)),
                      pl.BlockSpec(memory_space=pl.ANY)],
            out_specs=pl.BlockSpec((None, None, tr, C), lambda r, i_ref: (layer, i_ref[1], r, 0))),
        out_shape=jax.ShapeDtypeStruct(gbuf.shape, F32),
        input_output_aliases={3: 0},
    )(idx, part, landed, gbuf)


ANY = pl.BlockSpec(memory_space=pl.ANY)


def _place():
    x, y, c = lax.axis_index("x"), lax.axis_index("y"), lax.axis_index("c")
    chips = [(1 - x, y), (x, 1 - y), (1 - x, 1 - y)]
    return x, y, c, chips


HBM = pl.BlockSpec(memory_space=pltpu.HBM)
SEM = pl.BlockSpec(memory_space=pltpu.SEMAPHORE)
EFFECT = pltpu.SideEffectType.DATAFLOW_SIDE_EFFECTING


def _in_hbm(a):
    return pltpu.with_memory_space_constraint(a, pltpu.HBM)


def _ici_copy(src, dst, send_sems, recv_sems, k, peer):
    return pltpu.make_async_remote_copy(src_ref=src, dst_ref=dst, send_sem=send_sems.at[k], recv_sem=recv_sems.at[k],
                                        device_id=peer, device_id_type=MESH)


def ici_start(name, bufs, lands, after, gather):
    n, nl = len(bufs), len(lands)

    def body(*refs):
        b_in = refs[:n]
        send_sems, recv_sems = refs[n + nl + 1], refs[n + nl + 2]
        b_out = refs[n + nl + 3:2 * n + nl + 3]
        l_out = refs[2 * n + nl + 3:2 * n + 2 * nl + 3]
        token = refs[-1]
        x, y, c, chips = _place()
        kme = 2 * x + y
        for i in range(n):
            for j in range(3):
                peer = (*chips[j], c)
                if gather:
                    _ici_copy(b_out[i].at[kme, c], b_out[i].at[kme, c], send_sems, recv_sems, 3 * i + j, peer).start()
                else:
                    kd = 2 * chips[j][0] + chips[j][1]
                    _ici_copy(b_out[i].at[kd], l_out[i].at[j], send_sems, recv_sems, 3 * i + j, peer).start()
        token[...] = jnp.zeros_like(token)

    arrays = [*bufs, *lands]
    outs = pl.pallas_call(
        body, name=name,
        in_specs=[HBM] * (n + nl) + [ANY],
        out_specs=(SEM, SEM, *[HBM] * (n + nl), pl.BlockSpec(memory_space=pltpu.VMEM)),
        out_shape=(pltpu.SemaphoreType.DMA((3 * n,)), pltpu.SemaphoreType.DMA((3 * n,)),
                   *[pltpu.HBM(a.shape, a.dtype) for a in arrays], jax.ShapeDtypeStruct((8, LANE), F32)),
        input_output_aliases={i: 2 + i for i in range(n + nl)},
        compiler_params=pltpu.CompilerParams(has_side_effects=EFFECT),
    )(*[_in_hbm(a) for a in arrays], after)
    return outs[0], outs[1], list(outs[2:2 + n]), list(outs[2 + n:2 + n + nl]), outs[-1]


def ici_wait(name, send_sems, recv_sems, bufs, lands, after, gather):
    n, nl = len(bufs), len(lands)

    def body(*refs):
        b_in, l_in = refs[:n], refs[n:n + nl]
        send_sems, recv_sems = refs[n + nl], refs[n + nl + 1]
        x, y, c, chips = _place()
        kme = 2 * x + y
        for i in range(n):
            for j in range(3):
                peer = (*chips[j], c)
                kj = 2 * chips[j][0] + chips[j][1]
                if gather:
                    _ici_copy(b_in[i].at[kme, c], b_in[i].at[kme, c], send_sems, recv_sems, 3 * i + j, peer).wait_send()
                    _ici_copy(b_in[i].at[kj, c], b_in[i].at[kj, c], send_sems, recv_sems, 3 * i + j, peer).wait_recv()
                else:
                    _ici_copy(b_in[i].at[kj], l_in[i].at[j], send_sems, recv_sems, 3 * i + j, peer).wait_send()
                    _ici_copy(b_in[i].at[kj], l_in[i].at[j], send_sems, recv_sems, 3 * i + j, peer).wait_recv()

    arrays = [*bufs, *lands]
    outs = pl.pallas_call(
        body, name=name,
        in_specs=[HBM] * (n + nl) + [SEM, SEM, ANY],
        out_specs=tuple([HBM] * (n + nl)),
        out_shape=tuple(pltpu.HBM(a.shape, a.dtype) for a in arrays),
        input_output_aliases={i: i for i in range(n + nl)},
        compiler_params=pltpu.CompilerParams(has_side_effects=EFFECT),
    )(*arrays, send_sems, recv_sems, after)
    return list(outs[:n]), list(outs[n:])


def gather_pair_pass(name, bufs):
    n = len(bufs)

    def body(*refs):
        b = refs[n:2 * n]
        send_sems, recv_sems = refs[2 * n:]
        x, y, c, chips = _place()
        sib = (x, y, 1 - c)

        def d2d(i, j, which):
            kj = 2 * chips[j][0] + chips[j][1]
            return _ici_copy(b[i].at[kj, which], b[i].at[kj, which], send_sems, recv_sems, 3 * i + j, sib)

        for i in range(n):
            for j in range(3):
                d2d(i, j, c).start()
        for i in range(n):
            for j in range(3):
                d2d(i, j, 1 - c).wait_recv()
        for i in range(n):
            for j in range(3):
                d2d(i, j, c).wait_send()

    return pl.pallas_call(
        body, name=name, in_specs=[ANY] * n, out_specs=[ANY] * n,
        out_shape=[jax.ShapeDtypeStruct(a.shape, a.dtype) for a in bufs],
        input_output_aliases={i: i for i in range(n)},
        scratch_shapes=[pltpu.SemaphoreType.DMA((3 * n,)), pltpu.SemaphoreType.DMA((3 * n,))],
    )(*bufs)


def pair_exchange(name, dws):
    n = len(dws)

    def body(*refs):
        ins, outs = refs[:n], refs[n:2 * n]
        send_sems, recv_sems = refs[2 * n:]
        x, y, c, _ = _place()
        copies = []
        for i in range(n):
            copies.append(pltpu.make_async_remote_copy(
                src_ref=ins[i].at[:, 1 - c], dst_ref=outs[i],
                send_sem=send_sems.at[i], recv_sem=recv_sems.at[i],
                device_id=(x, y, 1 - c), device_id_type=MESH))
            copies[i].start()
        for cp in copies:
            cp.wait_recv()
        for cp in copies:
            cp.wait_send()

    return pl.pallas_call(
        body, name=name, in_specs=[ANY] * n, out_specs=[ANY] * n,
        out_shape=[jax.ShapeDtypeStruct((N_CHIPS, *d.shape[2:]), d.dtype) for d in dws],
        scratch_shapes=[pltpu.SemaphoreType.DMA((n,)), pltpu.SemaphoreType.DMA((n,))],
    )(*dws)


def pair_assemble(gbufs):
    n = len(gbufs)

    def body(*refs):
        bufs = refs[n:2 * n]
        send_sems, recv_sems = refs[2 * n:]
        x, y, c, _ = _place()
        copies = []
        for i in range(n):
            copies.append(pltpu.make_async_remote_copy(
                src_ref=bufs[i].at[:, c], dst_ref=bufs[i].at[:, c],
                send_sem=send_sems.at[i], recv_sem=recv_sems.at[i],
                device_id=(x, y, 1 - c), device_id_type=MESH))
            copies[i].start()
        for i in range(n):
            pltpu.make_async_remote_copy(
                src_ref=bufs[i].at[:, 1 - c], dst_ref=bufs[i].at[:, 1 - c],
                send_sem=send_sems.at[i], recv_sem=recv_sems.at[i],
                device_id=(x, y, 1 - c), device_id_type=MESH).wait_recv()
        for cp in copies:
            cp.wait_send()

    return pl.pallas_call(
        body, name="grad_pair_assemble", in_specs=[ANY] * n, out_specs=[ANY] * n,
        out_shape=[jax.ShapeDtypeStruct(g.shape, g.dtype) for g in gbufs],
        input_output_aliases={i: i for i in range(n)},
        scratch_shapes=[pltpu.SemaphoreType.DMA((n,)), pltpu.SemaphoreType.DMA((n,))],
    )(*gbufs)


def all_reduce_small(vec):
    NR = vec.shape[0]
    flips = [(fx, fy, fc) for fx in (0, 1) for fy in (0, 1) for fc in (0, 1)][1:]

    def body(v_ref, o_ref, buf, send_sems, recv_sems):
        x, y, c, _ = _place()
        me = 4 * x + 2 * y + c
        buf[me] = v_ref[...]
        copies = []
        for j, (fx, fy, fc) in enumerate(flips):
            peer = (1 - x if fx else x, 1 - y if fy else y, 1 - c if fc else c)
            copies.append(pltpu.make_async_remote_copy(
                src_ref=v_ref, dst_ref=buf.at[me], send_sem=send_sems.at[j], recv_sem=recv_sems.at[j],
                device_id=peer, device_id_type=MESH))
            copies[j].start()
        for cp in copies:
            cp.wait_recv()
        for cp in copies:
            cp.wait_send()
        acc = buf[0]
        for d in range(1, 8):
            acc = acc + buf[d]
        o_ref[...] = acc

    return pl.pallas_call(
        body, name="all_reduce_small",
        in_specs=[pl.BlockSpec(memory_space=pltpu.VMEM)], out_specs=pl.BlockSpec(memory_space=pltpu.VMEM),
        out_shape=jax.ShapeDtypeStruct((NR, LANE), F32),
        scratch_shapes=[pltpu.VMEM((8, NR, LANE), F32), pltpu.SemaphoreType.DMA((7,)),
                        pltpu.SemaphoreType.DMA((7,))],
    )(vec)


def _pack(arrays):
    flat = jnp.concatenate([a.reshape(-1).astype(F32) for a in arrays])
    n = flat.shape[0]
    npad = -(-n // (8 * LANE)) * (8 * LANE)
    return jnp.pad(flat, (0, npad - n)).reshape(npad // LANE, LANE)


def _unpack(buf, like):
    flat = buf.reshape(-1)
    out, off = [], 0
    for a in like:
        out.append(flat[off:off + a.size].reshape(a.shape))
        off += a.size
    return out


def kernel(x, ffn1_norm, ffn1_w_in, ffn1_w_out, mix_norm, ffn2_norm, ffn2_w_in, ffn2_w_out, a_w_qkv, a_rel_bias, a_w_o, kv_norm, kv_w_down, kv_latent_norm, kv_w_up, b_w_dq, b_q_norm, b_w_uq, b_w_o, final_norm, loss_target, m_ffn1_norm, m_ffn1_w_in, m_ffn1_w_out, m_mix_norm, m_ffn2_norm, m_ffn2_w_in, m_ffn2_w_out, m_a_w_qkv, m_a_rel_bias, m_a_w_o, m_kv_norm, m_kv_w_down, m_kv_latent_norm, m_kv_w_up, m_b_w_dq, m_b_q_norm, m_b_w_uq, m_b_w_o, m_final_norm, v_ffn1_norm, v_ffn1_w_in, v_ffn1_w_out, v_mix_norm, v_ffn2_norm, v_ffn2_w_in, v_ffn2_w_out, v_a_w_qkv, v_a_rel_bias, v_a_w_o, v_kv_norm, v_kv_w_down, v_kv_latent_norm, v_kv_w_up, v_b_w_dq, v_b_q_norm, v_b_w_uq, v_b_w_o, v_final_norm):
    B, S, D = x.shape
    T = B * S
    HB = D // 128
    QL = b_q_norm.shape[-1]
    KVL = kv_latent_norm.shape[0]
    hpc = HB // N_CHIPS
    tabs = rope_tables(S)
    idx = jnp.stack([2 * lax.axis_index("x") + lax.axis_index("y"), lax.axis_index("c")]).astype(I32)

    def halves(a):
        return a.reshape(*a.shape[:-2], 2, a.shape[-2] // 2, a.shape[-1])

    def whole(a):
        return a.reshape(*a.shape[:-3], 2 * a.shape[-2], a.shape[-1])

    kv_w_down_p = jnp.pad(kv_w_down, ((0, 0), (0, LANE - ROPE)))[None]
    b_w_uq_p = jnp.pad(b_w_uq.reshape(1, QL, hpc, NOPE + ROPE),
                       ((0, 0), (0, 0), (0, 0), (0, LANE - ROPE))).reshape(1, QL, hpc * 256)
    sharded = [("ffn1_w_in", ffn1_w_in), ("ffn1_w_out", ffn1_w_out), ("ffn2_w_in", ffn2_w_in),
               ("ffn2_w_out", ffn2_w_out), ("a_w_qkv", a_w_qkv), ("a_w_o", a_w_o),
               ("kv_w_down", kv_w_down_p), ("kv_w_up", kv_w_up[None]), ("b_w_dq", b_w_dq),
               ("b_w_uq", b_w_uq_p), ("b_w_o", b_w_o)]
    pieces = [(a, l) for a, (_, w) in enumerate(sharded) for l in range(w.shape[0])]
    names = [nm for nm, _ in sharded]
    own = {(names[a], l): cast_bf16(f"cast_{names[a]}_{l}", sharded[a][1], l, idx) for a, l in pieces}
    W = {}

    gather_groups = [
        [("ffn1_w_in", 0), ("ffn1_w_out", 0)],
        [("a_w_qkv", 0), ("a_w_o", 0), ("ffn2_w_in", 0), ("ffn2_w_out", 0), ("kv_w_down", 0), ("kv_w_up", 0)],
        [("ffn1_w_in", 1), ("ffn1_w_out", 1), ("b_w_dq", 0), ("b_w_uq", 0), ("b_w_o", 0), ("ffn2_w_in", 1),
         ("ffn2_w_out", 1)]]

    def gather_start(g, after):
        keys = gather_groups[g]
        ss, rs, bufs, _, token = ici_start(f"gather_start_{g}", [halves(own[k]) for k in keys], [], after, True)
        return (g, ss, rs, bufs), token

    def gather_finish(state, after):
        g, ss, rs, bufs = state
        bufs, _ = ici_wait(f"gather_wait_{g}", ss, rs, bufs, [], after, True)
        full = gather_pair_pass(f"gather_pair_{g}", bufs)
        for k, w in zip(gather_groups[g], full):
            W[k] = whole(w)
        return full[0]

    def tied(a, token):
        return a + token[0, 0]

    def col(nm, l=0):
        return W[(nm, l)]

    def row(nm, l=0):
        w = W[(nm, l)]
        return w.reshape(N_CHIPS * w.shape[1], w.shape[2])

    bias = rel_bias_tile("rel_bias_tile", a_rel_bias[0])

    def ffn_fwd(tag, h, g, w_in, w_out):
        xn = rms_fwd(f"{tag}_norm", h, g)
        u, act = ffn_in_act(f"{tag}_in", xn, w_in)
        return mm_roww(f"{tag}_out", act, w_out, F32, res=h, alpha=0.5), (xn, u, act)

    h0 = x.reshape(T, D)
    st0, tok0 = gather_start(0, h0)
    done0 = gather_finish(st0, tok0)
    st1, tok1 = gather_start(1, done0)
    h1, sv_f1a = ffn_fwd("l0f1", h0, tied(ffn1_norm[0], tok1), col("ffn1_w_in", 0), row("ffn1_w_out", 0))
    done1 = gather_finish(st1, h1)
    st2, tok2 = gather_start(2, done1)
    hn_a = rms_fwd("l0mix_norm", h1, tied(mix_norm[0], tok2))
    qkv = mm_colw("l0_qkv", hn_a, col("a_w_qkv"), BF16).reshape(B, S, 3 * D)
    o_a = attn_a_fwd("l0_attn", qkv, bias).reshape(T, D)
    h2 = mm_roww("l0_attn_out", o_a, row("a_w_o"), F32, res=h1)
    h3, sv_f2a = ffn_fwd("l0f2", h2, ffn2_norm[0], col("ffn2_w_in", 0), row("ffn2_w_out", 0))

    hkv = rms_fwd("kv_norm", h3, kv_norm)
    ckr = mm_roww("kv_down", hkv, row("kv_w_down"), F32)
    ckv, kr = kvprep_fwd("kv_prep", ckr, kv_latent_norm, tabs, B, S)
    kvb = mm_colw("kv_up", ckv, col("kv_w_up"), BF16).reshape(B, S, HB * 256)
    gather_finish(st2, kvb)

    h4, sv_f1b = ffn_fwd("l1f1", h3, ffn1_norm[1], col("ffn1_w_in", 1), row("ffn1_w_out", 1))
    hn_b = rms_fwd("l1mix_norm", h4, mix_norm[1])
    cqp = mm_roww("l1_dq", hn_b, row("b_w_dq"), F32)
    cq = rms_fwd("l1_q_norm", cqp, b_q_norm[0])
    qpre = mm_colw("l1_uq", cq, col("b_w_uq"), F32)
    qf = qprep("l1_q_rope", qpre, tabs, B, S, bwd=False).reshape(B, S, HB * 256)
    o_b, lse = mla_fwd("l1_attn", qf, kvb, kr)
    h5 = mm_roww("l1_attn_out", o_b.reshape(T, HB * LANE), row("b_w_o"), F32, res=h4)
    h6, sv_f2b = ffn_fwd("l1f2", h5, ffn2_norm[1], col("ffn2_w_in", 1), row("ffn2_w_out", 1))

    dh, g_final, loss_part = loss_head("loss_head", h6, final_norm, loss_target.reshape(T, D))

    gw = {}
    gbufs = {nm: lax.empty(halves(w).shape, F32) for nm, w in sharded}

    def reduce_start(r, keys, after):
        dws = [halves(gw[k]) for k in keys]
        landed = pair_exchange(f"grad_pair_exchange_{r}", dws)
        parts = [half_sum(f"half_sum_{r}_{i}", dws[i], landed[i], idx) for i in range(len(keys))]
        lands = [lax.empty((3, *p.shape[1:]), p.dtype) for p in parts]
        ss, rs, parts, lands, token = ici_start(f"reduce_start_{r}", parts, lands, after, False)
        return (r, keys, ss, rs, parts, lands), token

    def reduce_finish(state, after):
        r, keys, ss, rs, parts, lands = state
        parts, lands = ici_wait(f"reduce_wait_{r}", ss, rs, parts, lands, after, False)
        for i, (nm, l) in enumerate(keys):
            gbufs[nm] = chip_sum(f"chip_sum_{r}_{i}", parts[i], lands[i], gbufs[nm], l, idx)
        return gbufs[keys[0][0]]

    def ffn_bwd(tag, dh, h_in, g, w_in, w_out, saved, key_in, key_out, after=None):
        xn, u, act = saved
        du = ffn_dact(f"{tag}_dact", dh, w_out, u, after=after)
        dwo = mm_droww(f"{tag}_dwout", act, dh, alpha=0.5)
        gw[key_out] = dwo.reshape(N_CHIPS, dwo.shape[0] // N_CHIPS, dwo.shape[1])
        gw[key_in] = mm_dcolw(f"{tag}_dwin", xn, du, pair_layout=True)
        dxn = mm_colw_t(f"{tag}_dxn", du, w_in, F32, pair_layout=True)
        return rms_bwd(f"{tag}_dnorm", h_in, g, dxn, dres=dh)

    def chip_major(dw):
        return dw.reshape(N_CHIPS, dw.shape[0] // N_CHIPS, dw.shape[1])

    dh, g_f2b = ffn_bwd("l1f2b", dh, h5, ffn2_norm[1], col("ffn2_w_in", 1), row("ffn2_w_out", 1), sv_f2b,
                        ("ffn2_w_in", 1), ("ffn2_w_out", 1))
    red0, rtok0 = reduce_start(0, [("ffn2_w_in", 1), ("ffn2_w_out", 1)], dh)
    do_b = mm_roww_t("l1_attn_do", dh, row("b_w_o"), BF16, after=rtok0).reshape(B, S, HB * LANE)
    gw[("b_w_o", 0)] = chip_major(mm_droww("l1_attn_dwo", o_b.reshape(T, HB * LANE), dh))
    dqf, dkv, dkr = mla_bwd("l1_attn_bwd", qf, kvb, kr, do_b, o_b, lse)
    dqpre = qprep("l1_q_rope_bwd", dqf.reshape(T, HB * 256), tabs, B, S, bwd=True)
    gw[("b_w_uq", 0)] = mm_dcolw("l1_dwuq", cq, dqpre)
    dcq = mm_colw_t("l1_dcq", dqpre, col("b_w_uq"), F32)
    dcqp, g_qn = rms_bwd("l1_dq_norm", cqp, b_q_norm[0], dcq)
    gw[("b_w_dq", 0)] = chip_major(mm_droww("l1_dwdq", hn_b, dcqp))
    dhn = mm_roww_t("l1_dhn", dcqp, row("b_w_dq"), F32)
    dh, g_mixb = rms_bwd("l1_dmix", h4, mix_norm[1], dhn, dres=dh)
    dh, g_f1b = ffn_bwd("l1f1b", dh, h3, ffn1_norm[1], col("ffn1_w_in", 1), row("ffn1_w_out", 1), sv_f1b,
                        ("ffn1_w_in", 1), ("ffn1_w_out", 1))
    fin0 = reduce_finish(red0, dh)
    red1, rtok1 = reduce_start(1, [("b_w_o", 0), ("b_w_uq", 0), ("b_w_dq", 0), ("ffn1_w_in", 1), ("ffn1_w_out", 1)], fin0)
    dkv2 = dkv.reshape(T, HB * 256)
    gw[("kv_w_up", 0)] = mm_dcolw("kv_dwup", ckv, dkv2, after=rtok1)
    dckv = mm_colw_t("kv_dckv", dkv2, col("kv_w_up"), F32, after=rtok1)
    dckr, g_lat = kvprep_bwd("kv_prep_bwd", ckr, kv_latent_norm, dckv, dkr, tabs, B, S)
    gw[("kv_w_down", 0)] = chip_major(mm_droww("kv_dwdown", hkv, dckr))
    dhkv = mm_roww_t("kv_dhkv", dckr, row("kv_w_down"), F32)
    dh, g_kvn = rms_bwd("kv_dnorm", h3, kv_norm, dhkv, dres=dh)
    dh, g_f2a = ffn_bwd("l0f2b", dh, h2, ffn2_norm[0], col("ffn2_w_in", 0), row("ffn2_w_out", 0), sv_f2a,
                        ("ffn2_w_in", 0), ("ffn2_w_out", 0))
    do_a = mm_roww_t("l0_attn_do", dh, row("a_w_o"), BF16).reshape(B, S, D)
    gw[("a_w_o", 0)] = chip_major(mm_droww("l0_attn_dwo", o_a, dh))
    dq_a, dkv_a, dbias = attn_a_bwd("l0_attn_bwd", qkv, do_a, bias)
    dqkv = jnp.concatenate([dq_a.reshape(T, D), dkv_a.reshape(T, 2 * D)], axis=1)
    gw[("a_w_qkv", 0)] = mm_dcolw("l0_dwqkv", hn_a, dqkv)
    dhn = mm_colw_t("l0_dhn", dqkv, col("a_w_qkv"), F32)
    dh, g_mixa = rms_bwd("l0_dmix", h1, mix_norm[0], dhn, dres=dh)
    fin1 = reduce_finish(red1, dh)
    red2, rtok2 = reduce_start(2, [("kv_w_up", 0), ("kv_w_down", 0), ("ffn2_w_in", 0), ("ffn2_w_out", 0),
                                   ("a_w_o", 0), ("a_w_qkv", 0)], fin1)
    dh, g_f1a = ffn_bwd("l0f1b", dh, h0, ffn1_norm[0], col("ffn1_w_in", 0), row("ffn1_w_out", 0), sv_f1a,
                        ("ffn1_w_in", 0), ("ffn1_w_out", 0), after=rtok2)
    grad_x = dh.reshape(B, S, D)
    g_rel = rel_bias_grad("rel_bias_grad", dbias)[:, :2 * MAX_REL + 1][None]
    fin2 = reduce_finish(red2, dh)
    red3, rtok3 = reduce_start(3, [("ffn1_w_in", 0), ("ffn1_w_out", 0)], fin2)
    reduce_finish(red3, rtok3)

    full = [whole(g) for g in pair_assemble([gbufs[nm] for nm in names])]
    G = {nm: g for (nm, _), g in zip(sharded, full)}
    G["kv_w_down"] = G["kv_w_down"][0, :, :KVL + ROPE]
    G["kv_w_up"] = G["kv_w_up"][0]
    G["b_w_uq"] = G["b_w_uq"].reshape(1, QL, hpc, 256)[..., :NOPE + ROPE].reshape(b_w_uq.shape)

    small = [("ffn1_norm", jnp.stack([g_f1a, g_f1b])), ("mix_norm", jnp.stack([g_mixa, g_mixb])),
             ("ffn2_norm", jnp.stack([g_f2a, g_f2b])), ("a_rel_bias", g_rel), ("kv_norm", g_kvn),
             ("kv_latent_norm", g_lat), ("b_q_norm", g_qn[None]), ("final_norm", g_final)]
    red = all_reduce_small(_pack([loss_part] + [g for _, g in small]))
    unpacked = _unpack(red, [loss_part] + [g for _, g in small])
    loss = unpacked[0][0, 0]
    for (nm, _), g in zip(small, unpacked[1:]):
        G[nm] = g

    given = dict(ffn1_norm=(ffn1_norm, m_ffn1_norm, v_ffn1_norm), ffn1_w_in=(ffn1_w_in, m_ffn1_w_in, v_ffn1_w_in),
                 ffn1_w_out=(ffn1_w_out, m_ffn1_w_out, v_ffn1_w_out), mix_norm=(mix_norm, m_mix_norm, v_mix_norm),
                 ffn2_norm=(ffn2_norm, m_ffn2_norm, v_ffn2_norm), ffn2_w_in=(ffn2_w_in, m_ffn2_w_in, v_ffn2_w_in),
                 ffn2_w_out=(ffn2_w_out, m_ffn2_w_out, v_ffn2_w_out), a_w_qkv=(a_w_qkv, m_a_w_qkv, v_a_w_qkv),
                 a_rel_bias=(a_rel_bias, m_a_rel_bias, v_a_rel_bias), a_w_o=(a_w_o, m_a_w_o, v_a_w_o),
                 kv_norm=(kv_norm, m_kv_norm, v_kv_norm), kv_w_down=(kv_w_down, m_kv_w_down, v_kv_w_down),
                 kv_latent_norm=(kv_latent_norm, m_kv_latent_norm, v_kv_latent_norm),
                 kv_w_up=(kv_w_up, m_kv_w_up, v_kv_w_up), b_w_dq=(b_w_dq, m_b_w_dq, v_b_w_dq),
                 b_q_norm=(b_q_norm, m_b_q_norm, v_b_q_norm), b_w_uq=(b_w_uq, m_b_w_uq, v_b_w_uq),
                 b_w_o=(b_w_o, m_b_w_o, v_b_w_o), final_norm=(final_norm, m_final_norm, v_final_norm))
    order = list(given)
    delta, new_m, new_v = {}, {}, {}
    small_names = [nm for nm, _ in small]
    packed = [_pack([given[nm][k] for nm in small_names]) for k in range(3)]
    outs = adamw("adamw_small", packed[0], _pack([G[nm] for nm in small_names]), packed[1], packed[2])
    for dst, buf in zip((delta, new_m, new_v), outs):
        for nm, a in zip(small_names, _unpack(buf, [given[nm][0] for nm in small_names])):
            dst[nm] = a
    for nm, _ in sharded:
        w, m, v = given[nm]
        g = G[nm].reshape(w.shape)
        G[nm] = g
        two = lambda a: a.reshape(-1, a.shape[-1])
        d_, m_, v_ = adamw(f"adamw_{nm}", two(w), two(g), two(m), two(v))
        delta[nm], new_m[nm], new_v[nm] = d_.reshape(w.shape), m_.reshape(w.shape), v_.reshape(w.shape)

    return (loss, grad_x, *[G[n] for n in order], *[delta[n] for n in order],
            *[new_m[n] for n in order], *[new_v[n] for n in order])
```

```python
import functools
import math

import jax
import jax.numpy as jnp
from jax import lax
from jax.experimental import pallas as pl
from jax.experimental.pallas import tpu as pltpu

F32 = jnp.float32
BF16 = jnp.bfloat16
I32 = jnp.int32

CHUNK = 64
CHUNK_SHIFT = 6
HEAD_DIM_A = 64
LEFT_CHUNKS = 8
MAX_REL = 128
REL_PAD = 384
QROWS = 2 * CHUNK
WIN = (LEFT_CHUNKS + 2) * CHUNK
PADR = LEFT_CHUNKS * CHUNK
NOPE = 128
ROPE = 64
EPS = 1e-6
NEG_INF = -1e30
ROPE_THETA = 10000.0
ADAM_LR, ADAM_B1, ADAM_B2, ADAM_EPS, ADAM_WD, ADAM_STEP = 0.001, 0.9, 0.999, 1e-08, 0.01, 10
N_CHIPS = 4
LANE = 128
MESH = pl.DeviceIdType.MESH
VMEM_CAP_MB = 60

NN = (((1,), (0,)), ((), ()))
NT = (((1,), (1,)), ((), ()))
TN = (((0,), (0,)), ((), ()))


def _tile(n, pref, mult):
    t = (min(pref, n) // mult) * mult
    while t >= mult:
        if n % t == 0:
            return t
        t -= mult
    return n


def _nbytes(shape, dtype):
    return math.prod(shape) * jnp.dtype(dtype).itemsize


def _params(block_bytes, extra_bytes=0):
    need = 2 * block_bytes + extra_bytes
    mb = min(VMEM_CAP_MB, max(32, int(need * 1.25 / 2**20) + 8))
    return pltpu.CompilerParams(vmem_limit_bytes=mb * 2**20)


def _mm(name, kind, a, b, grid, a_spec, b_spec, o_spec, out_shape, out_dtype, blocks,
        red_axis=None, nred=1, alpha=1.0, res=None, res_spec=None, after=None):
    dims = {"nn": NN, "nt": NT, "tn": TN}[kind]
    has_res = res is not None
    acc_in_out = nred > 1 and out_dtype == F32 and not has_res and alpha == 1.0
    n_in = 2 + has_res + (after is not None)

    def body(*refs):
        a_ref, b_ref = refs[0], refs[1]
        r_ref = refs[2] if has_res else None
        o_ref = refs[n_in]
        p = lax.dot_general(a_ref[...].astype(BF16), b_ref[...].astype(BF16), dims,
                            preferred_element_type=F32)

        def finish(acc):
            y = acc if alpha == 1.0 else acc * alpha
            if has_res:
                y = r_ref[...] + y
            o_ref[...] = y.astype(o_ref.dtype)

        if nred == 1:
            finish(p)
            return
        k = pl.program_id(red_axis)
        tgt = o_ref if acc_in_out else refs[-1]

        @pl.when(k == 0)
        def _():
            tgt[...] = p

        @pl.when(k > 0)
        def _():
            tgt[...] += p

        if not acc_in_out:
            @pl.when(k == nred - 1)
            def _():
                finish(tgt[...])

    a_blk, b_blk, o_blk = blocks
    scratch = []
    extra = 0
    if nred > 1 and not acc_in_out:
        scratch = [pltpu.VMEM(o_blk, F32)]
        extra = _nbytes(o_blk, F32)
    blk = _nbytes(a_blk, a.dtype) + _nbytes(b_blk, b.dtype) + _nbytes(o_blk, out_dtype)
    ins, specs = [a, b], [a_spec, b_spec]
    if has_res:
        ins.append(res)
        specs.append(res_spec)
        blk += _nbytes(o_blk, res.dtype)
    if after is not None:
        ins.append(after)
        specs.append(pl.BlockSpec(memory_space=pl.ANY))
    extra += _nbytes(a_blk, BF16) + _nbytes(b_blk, BF16) + 2 * _nbytes(o_blk, F32)
    return pl.pallas_call(
        body, name=name, grid=grid, in_specs=specs, out_specs=o_spec,
        out_shape=jax.ShapeDtypeStruct(out_shape, out_dtype), scratch_shapes=scratch,
        compiler_params=_params(blk, extra),
    )(*ins)


def mm_colw(name, x, w3, out_dtype):
    T, K = x.shape
    _, _, nl = w3.shape
    tm = _tile(T, 512, 8)
    return _mm(name, "nn", x, w3, (N_CHIPS, T // tm),
               pl.BlockSpec((tm, K), lambda j, i: (i, 0)),
               pl.BlockSpec((None, K, nl), lambda j, i: (j, 0, 0)),
               pl.BlockSpec((tm, nl), lambda j, i: (i, j)),
               (T, N_CHIPS * nl), out_dtype, ((tm, K), (K, nl), (tm, nl)))


def _pair_chip(j):
    return (j % 2) * 2 + j // 2


def mm_colw_t(name, dy, w3, out_dtype, res=None, after=None, pair_layout=False):
    T = dy.shape[0]
    _, K, nl = w3.shape
    tm = _tile(T, 1024, 8)
    chip = _pair_chip if pair_layout else (lambda j: j)
    return _mm(name, "nt", dy, w3, (T // tm, N_CHIPS),
               pl.BlockSpec((tm, nl), lambda i, j: (i, j)),
               pl.BlockSpec((None, K, nl), lambda i, j: (chip(j), 0, 0)),
               pl.BlockSpec((tm, K), lambda i, j: (i, 0)),
               (T, K), out_dtype, ((tm, nl), (K, nl), (tm, K)),
               red_axis=1, nred=N_CHIPS, res=res,
               res_spec=pl.BlockSpec((tm, K), lambda i, j: (i, 0)), after=after)


def mm_dcolw(name, x, dy, after=None, pair_layout=False):
    T, K = x.shape
    nl = dy.shape[1] // N_CHIPS
    tt = _tile(T, 2048, 8)
    chip = _pair_chip if pair_layout else (lambda j: j)
    return _mm(name, "tn", x, dy, (N_CHIPS, T // tt),
               pl.BlockSpec((tt, K), lambda j, t: (t, 0)),
               pl.BlockSpec((tt, nl), lambda j, t: (t, j)),
               pl.BlockSpec((None, K, nl), lambda j, t: (chip(j), 0, 0)),
               (N_CHIPS, K, nl), BF16, ((tt, K), (tt, nl), (K, nl)),
               red_axis=1, nred=T // tt, after=after)


def mm_roww(name, x, w2, out_dtype, res=None, alpha=1.0):
    T, Kt = x.shape
    N = w2.shape[1]
    tm = _tile(T, 512, 8)
    return _mm(name, "nn", x, w2, (T // tm,),
               pl.BlockSpec((tm, Kt), lambda i: (i, 0)),
               pl.BlockSpec((Kt, N), lambda i: (0, 0)),
               pl.BlockSpec((tm, N), lambda i: (i, 0)),
               (T, N), out_dtype, ((tm, Kt), (Kt, N), (tm, N)),
               alpha=alpha, res=res, res_spec=pl.BlockSpec((tm, N), lambda i: (i, 0)))


def mm_roww_t(name, dy, w2, out_dtype, alpha=1.0, after=None):
    T, N = dy.shape
    Kt = w2.shape[0]
    tm = _tile(T, 512, 8)
    tk = _tile(Kt, 1408, LANE)
    return _mm(name, "nt", dy, w2, (Kt // tk, T // tm),
               pl.BlockSpec((tm, N), lambda j, i: (i, 0)),
               pl.BlockSpec((tk, N), lambda j, i: (j, 0)),
               pl.BlockSpec((tm, tk), lambda j, i: (i, j)),
               (T, Kt), out_dtype, ((tm, N), (tk, N), (tm, tk)), alpha=alpha, after=after)


def mm_droww(name, x, dy, alpha=1.0):
    T, Kt = x.shape
    N = dy.shape[1]
    tt = _tile(T, 2048, 8)
    tk = _tile(Kt, 1408, LANE)
    return _mm(name, "tn", x, dy, (Kt // tk, T // tt),
               pl.BlockSpec((tt, tk), lambda j, t: (t, j)),
               pl.BlockSpec((tt, N), lambda j, t: (t, 0)),
               pl.BlockSpec((tk, N), lambda j, t: (j, 0)),
               (Kt, N), BF16, ((tt, tk), (tt, N), (tk, N)),
               red_axis=1, nred=T // tt, alpha=alpha)


def rms_fwd(name, x, g):
    T, D = x.shape
    tm = _tile(T, 512, 8)

    def body(x_ref, g_ref, o_ref):
        xv = x_ref[...]
        r = lax.rsqrt(jnp.mean(xv * xv, axis=-1, keepdims=True) + EPS)
        o_ref[...] = (xv * r * g_ref[...]).astype(o_ref.dtype)

    return pl.pallas_call(
        body, name=name, grid=(T // tm,),
        in_specs=[pl.BlockSpec((tm, D), lambda i: (i, 0)), pl.BlockSpec((1, D), lambda i: (0, 0))],
        out_specs=pl.BlockSpec((tm, D), lambda i: (i, 0)),
        out_shape=jax.ShapeDtypeStruct((T, D), BF16),
        compiler_params=_params(_nbytes((tm, D), F32) * 2, 4 * _nbytes((tm, D), F32)),
    )(x, g.reshape(1, D))


def _rms_bwd_math(xv, gv, dy):
    r = lax.rsqrt(jnp.mean(xv * xv, axis=-1, keepdims=True) + EPS)
    xh = xv * r
    dyg = dy * gv
    dx = r * (dyg - xh * jnp.mean(dyg * xh, axis=-1, keepdims=True))
    dg = jnp.sum(dy * xh, axis=0, keepdims=True)
    return dx, dg


def rms_bwd(name, x, g, dy, dres=None):
    T, D = x.shape
    tm = _tile(T, 256, 8)
    has_res = dres is not None

    def body(*refs):
        x_ref, g_ref, dy_ref = refs[:3]
        r_ref = refs[3] if has_res else None
        dx_ref, dg_ref = refs[-2:]
        dx, dg = _rms_bwd_math(x_ref[...], g_ref[...], dy_ref[...].astype(F32))
        if has_res:
            dx = r_ref[...] + dx
        dx_ref[...] = dx

        @pl.when(pl.program_id(0) == 0)
        def _():
            dg_ref[...] = dg

        @pl.when(pl.program_id(0) > 0)
        def _():
            dg_ref[...] += dg

    row = pl.BlockSpec((tm, D), lambda i: (i, 0))
    vec = pl.BlockSpec((1, D), lambda i: (0, 0))
    ins, specs = [x, g.reshape(1, D), dy], [row, vec, row]
    if has_res:
        ins.append(dres)
        specs.append(row)
    dx, dg = pl.pallas_call(
        body, name=name, grid=(T // tm,), in_specs=specs, out_specs=[row, vec],
        out_shape=[jax.ShapeDtypeStruct((T, D), F32), jax.ShapeDtypeStruct((1, D), F32)],
        compiler_params=_params(_nbytes((tm, D), F32) * 4, 6 * _nbytes((tm, D), F32)),
    )(*ins)
    return dx, dg.reshape(D)


def dx_norm_bwd(name, dy, w3, x, g, dres=None, pair_layout=False, after=None):
    T = dy.shape[0]
    _, K, nl = w3.shape
    tm = _tile(T, 512, 8)
    chip = _pair_chip if pair_layout else (lambda j: j)
    has_res = dres is not None

    def body(*refs):
        dy_ref, w_ref, x_ref, g_ref = refs[:4]
        r_ref = refs[4] if has_res else None
        dx_ref, dg_ref, acc = refs[-3:]
        i, k = pl.program_id(0), pl.program_id(1)
        p = lax.dot_general(dy_ref[...].astype(BF16), w_ref[...], NT, preferred_element_type=F32)

        @pl.when(k == 0)
        def _():
            acc[...] = p

        @pl.when(k > 0)
        def _():
            acc[...] += p

        @pl.when(k == N_CHIPS - 1)
        def _():
            dx, dg = _rms_bwd_math(x_ref[...], g_ref[...], acc[...])
            dx_ref[...] = r_ref[...] + dx if has_res else dx

            @pl.when(i == 0)
            def _():
                dg_ref[...] = dg

            @pl.when(i > 0)
            def _():
                dg_ref[...] += dg

    row = pl.BlockSpec((tm, K), lambda i, j: (i, 0))
    vec = pl.BlockSpec((1, K), lambda i, j: (0, 0))
    ins = [dy, w3, x, g.reshape(1, K)]
    specs = [pl.BlockSpec((tm, nl), lambda i, j: (i, j)),
             pl.BlockSpec((None, K, nl), lambda i, j: (chip(j), 0, 0)), row, vec]
    if has_res:
        ins.append(dres)
        specs.append(row)
    if after is not None:
        ins.append(after)
        specs.append(pl.BlockSpec(memory_space=pl.ANY))
    blk = _nbytes((tm, nl), dy.dtype) + _nbytes((K, nl), BF16) + (2 + has_res) * _nbytes((tm, K), F32)
    dx, dg = pl.pallas_call(
        body, name=name, grid=(T // tm, N_CHIPS), in_specs=specs, out_specs=[row, vec],
        out_shape=[jax.ShapeDtypeStruct((T, K), F32), jax.ShapeDtypeStruct((1, K), F32)],
        scratch_shapes=[pltpu.VMEM((tm, K), F32)],
        compiler_params=_params(blk, 8 * _nbytes((tm, K), F32)),
    )(*ins)
    return dx, dg.reshape(K)


def ffn_in_act(name, x, w3):
    T, K = x.shape
    _, _, nl = w3.shape
    tm = _tile(T, 512, 8)

    def body(x_ref, wg_ref, wu_ref, u_ref, a_ref):
        xv = x_ref[...]
        g = jnp.dot(xv, wg_ref[...], preferred_element_type=F32)
        up = jnp.dot(xv, wu_ref[...], preferred_element_type=F32)
        u_ref[:, :nl] = g.astype(u_ref.dtype)
        u_ref[:, nl:] = up.astype(u_ref.dtype)
        a_ref[...] = (g * jax.nn.sigmoid(g) * up).astype(a_ref.dtype)

    blk = _nbytes((tm, K), BF16) + 2 * _nbytes((K, nl), BF16) + _nbytes((tm, 3 * nl), BF16)
    return pl.pallas_call(
        body, name=name, grid=(2, T // tm),
        in_specs=[pl.BlockSpec((tm, K), lambda p, i: (i, 0)),
                  pl.BlockSpec((None, K, nl), lambda p, i: (p, 0, 0)),
                  pl.BlockSpec((None, K, nl), lambda p, i: (p + 2, 0, 0))],
        out_specs=[pl.BlockSpec((tm, 2 * nl), lambda p, i: (i, p)), pl.BlockSpec((tm, nl), lambda p, i: (i, p))],
        out_shape=[jax.ShapeDtypeStruct((T, 4 * nl), BF16), jax.ShapeDtypeStruct((T, 2 * nl), BF16)],
        compiler_params=_params(blk, 4 * _nbytes((tm, nl), F32)),
    )(x, w3, w3)


def ffn_dact(name, dh, w_out, u, after=None):
    T, N = dh.shape
    F = w_out.shape[0]
    nl = F // 2
    tm = _tile(T, 512, 8)

    def body(*refs):
        d_ref, w_ref, u_ref = refs[:3]
        o_ref = refs[-1]
        dact = 0.5 * lax.dot_general(d_ref[...].astype(BF16), w_ref[...], NT, preferred_element_type=F32)
        g = u_ref[:, :nl].astype(F32)
        up = u_ref[:, nl:].astype(F32)
        sig = jax.nn.sigmoid(g)
        o_ref[:, :nl] = (dact * up * (sig * (1.0 + g * (1.0 - sig)))).astype(o_ref.dtype)
        o_ref[:, nl:] = (dact * (g * sig)).astype(o_ref.dtype)

    ins = [dh, w_out, u]
    specs = [pl.BlockSpec((tm, N), lambda p, i: (i, 0)), pl.BlockSpec((nl, N), lambda p, i: (p, 0)),
             pl.BlockSpec((tm, 2 * nl), lambda p, i: (i, p))]
    if after is not None:
        ins.append(after)
        specs.append(pl.BlockSpec(memory_space=pl.ANY))
    blk = _nbytes((tm, N), F32) + _nbytes((nl, N), BF16) + 2 * _nbytes((tm, 2 * nl), BF16)
    return pl.pallas_call(
        body, name=name, grid=(2, T // tm), in_specs=specs,
        out_specs=pl.BlockSpec((tm, 2 * nl), lambda p, i: (i, p)),
        out_shape=jax.ShapeDtypeStruct((T, 2 * F), BF16),
        compiler_params=_params(blk, 6 * _nbytes((tm, nl), F32)),
    )(*ins)


def loss_head(name, h, g, target):
    T, D = h.shape
    tm = _tile(T, 256, 8)

    def body(h_ref, g_ref, t_ref, dh_ref, dg_ref, loss_ref):
        xv = h_ref[...]
        gv = g_ref[...]
        r = lax.rsqrt(jnp.mean(xv * xv, axis=-1, keepdims=True) + EPS)
        err = xv * r * gv - t_ref[...]
        part = 0.5 * jnp.sum(jnp.mean(err * err, axis=-1, keepdims=True), axis=0, keepdims=True)
        dx, dg = _rms_bwd_math(xv, gv, err * (1.0 / D))
        dh_ref[...] = dx
        part = jnp.broadcast_to(part, (1, LANE))

        @pl.when(pl.program_id(0) == 0)
        def _():
            dg_ref[...] = dg
            loss_ref[...] = part

        @pl.when(pl.program_id(0) > 0)
        def _():
            dg_ref[...] += dg
            loss_ref[...] += part

    row = pl.BlockSpec((tm, D), lambda i: (i, 0))
    vec = pl.BlockSpec((1, D), lambda i: (0, 0))
    dh, dg, loss = pl.pallas_call(
        body, name=name, grid=(T // tm,), in_specs=[row, vec, row],
        out_specs=[row, vec, pl.BlockSpec((1, LANE), lambda i: (0, 0))],
        out_shape=[jax.ShapeDtypeStruct((T, D), F32), jax.ShapeDtypeStruct((1, D), F32),
                   jax.ShapeDtypeStruct((1, LANE), F32)],
        compiler_params=_params(_nbytes((tm, D), F32) * 3, 6 * _nbytes((tm, D), F32)),
    )(h, g.reshape(1, D), target)
    return dh, dg.reshape(D), loss


def rope_tables(S):
    half = ROPE // 2
    freqs = ROPE_THETA ** (-jnp.arange(half, dtype=F32) / half)
    ang = jnp.arange(S, dtype=F32)[:, None] * freqs[None, :]
    cos, sin = jnp.cos(ang), jnp.sin(ang)
    z = jnp.zeros_like(cos)
    ct = jnp.concatenate([cos, cos, z, z], axis=1)
    s1 = jnp.concatenate([-sin, z, z, z], axis=1)
    s2 = jnp.concatenate([z, sin, z, z], axis=1)
    return ct, s1, s2


def _rope_tile(t, ct, s1, s2):
    return t * ct + pltpu.roll(t, 96, 1) * s1 + pltpu.roll(t, 32, 1) * s2


def _rope_tile_bwd(d, ct, s1, s2):
    return d * ct + pltpu.roll(d * s1, 32, 1) + pltpu.roll(d * s2, 96, 1)


def qprep(name, q, tabs, B, S, bwd):
    T, W = q.shape
    nh = W // 256
    ts = _tile(S, 256, 8)
    fn = _rope_tile_bwd if bwd else _rope_tile

    def body(q_ref, ct_ref, s1_ref, s2_ref, o_ref):
        ct, s1, s2 = ct_ref[...], s1_ref[...], s2_ref[...]
        for h in range(nh):
            o_ref[0, :, 256 * h:256 * h + 128] = q_ref[0, :, 256 * h:256 * h + 128].astype(o_ref.dtype)
            t = q_ref[0, :, 256 * h + 128:256 * h + 256].astype(F32)
            o_ref[0, :, 256 * h + 128:256 * h + 256] = fn(t, ct, s1, s2).astype(o_ref.dtype)

    row = pl.BlockSpec((1, ts, W), lambda b, s: (b, s, 0))
    tab = pl.BlockSpec((ts, LANE), lambda b, s: (s, 0))
    out = pl.pallas_call(
        body, name=name, grid=(B, S // ts), in_specs=[row, tab, tab, tab], out_specs=row,
        out_shape=jax.ShapeDtypeStruct((B, S, W), BF16),
        compiler_params=_params(_nbytes((ts, W), F32) * 2, _nbytes((ts, W), F32) * 2),
    )(q.reshape(B, S, W), *tabs)
    return out.reshape(T, W)


def kvprep_fwd(name, ckr, g, tabs, B, S):
    T, W = ckr.shape
    KVL = W - LANE
    ts = _tile(S, 256, 8)

    def body(x_ref, g_ref, ct_ref, s1_ref, s2_ref, c_ref, k_ref):
        xv = x_ref[0, :, :KVL]
        r = lax.rsqrt(jnp.mean(xv * xv, axis=-1, keepdims=True) + EPS)
        c_ref[0] = (xv * r * g_ref[...]).astype(c_ref.dtype)
        k_ref[0] = _rope_tile(x_ref[0, :, KVL:], ct_ref[...], s1_ref[...], s2_ref[...]).astype(k_ref.dtype)

    tab = pl.BlockSpec((ts, LANE), lambda b, s: (s, 0))
    c, k = pl.pallas_call(
        body, name=name, grid=(B, S // ts),
        in_specs=[pl.BlockSpec((1, ts, W), lambda b, s: (b, s, 0)), pl.BlockSpec((1, KVL), lambda b, s: (0, 0)),
                  tab, tab, tab],
        out_specs=[pl.BlockSpec((1, ts, KVL), lambda b, s: (b, s, 0)),
                   pl.BlockSpec((1, ts, LANE), lambda b, s: (b, s, 0))],
        out_shape=[jax.ShapeDtypeStruct((B, S, KVL), BF16), jax.ShapeDtypeStruct((B, S, LANE), BF16)],
        compiler_params=_params(_nbytes((ts, W), F32) * 2, _nbytes((ts, W), F32) * 2),
    )(ckr.reshape(B, S, W), g.reshape(1, KVL), *tabs)
    return c.reshape(T, KVL), k


def kvprep_bwd(name, ckr, g, dc, dkr, tabs, B, S):
    T, W = ckr.shape
    KVL = W - LANE
    ts = _tile(S, 256, 8)

    def body(x_ref, g_ref, dc_ref, dk_ref, ct_ref, s1_ref, s2_ref, o_ref, dg_ref):
        dx, dg = _rms_bwd_math(x_ref[0, :, :KVL], g_ref[...], dc_ref[0])
        o_ref[0, :, :KVL] = dx
        o_ref[0, :, KVL:] = _rope_tile_bwd(dk_ref[0], ct_ref[...], s1_ref[...], s2_ref[...])
        first = (pl.program_id(0) == 0) & (pl.program_id(1) == 0)

        @pl.when(first)
        def _():
            dg_ref[...] = dg

        @pl.when(jnp.logical_not(first))
        def _():
            dg_ref[...] += dg

    tab = pl.BlockSpec((ts, LANE), lambda b, s: (s, 0))
    vec = pl.BlockSpec((1, KVL), lambda b, s: (0, 0))
    o, dg = pl.pallas_call(
        body, name=name, grid=(B, S // ts),
        in_specs=[pl.BlockSpec((1, ts, W), lambda b, s: (b, s, 0)), vec,
                  pl.BlockSpec((1, ts, KVL), lambda b, s: (b, s, 0)),
                  pl.BlockSpec((1, ts, LANE), lambda b, s: (b, s, 0)), tab, tab, tab],
        out_specs=[pl.BlockSpec((1, ts, W), lambda b, s: (b, s, 0)), vec],
        out_shape=[jax.ShapeDtypeStruct((B, S, W), F32), jax.ShapeDtypeStruct((1, KVL), F32)],
        compiler_params=_params(_nbytes((ts, W), F32) * 4, _nbytes((ts, W), F32) * 4),
    )(ckr.reshape(B, S, W), g.reshape(1, KVL), dc.reshape(B, S, KVL), dkr, *tabs)
    return o.reshape(T, W), dg.reshape(KVL)


DIAGS = 768


def _diag_onehot():
    col = lax.broadcasted_iota(I32, (REL_PAD, DIAGS), 1)
    row = lax.broadcasted_iota(I32, (REL_PAD, DIAGS), 0)
    idx = jnp.clip(PADR + QROWS - 1 - col, -MAX_REL, MAX_REL) + MAX_REL
    return (row == idx).astype(F32)


def rel_bias_tile(name, table):
    H = table.shape[0]
    tpad = jnp.pad(table, ((0, 0), (0, REL_PAD - table.shape[1])))

    def body(t_ref, o_ref):
        g = lax.dot_general(t_ref[...], _diag_onehot(), NN, precision=lax.Precision.HIGHEST,
                            preferred_element_type=F32)
        qc = jnp.right_shift(lax.broadcasted_iota(I32, (QROWS, WIN), 0), CHUNK_SHIFT)
        kc = jnp.right_shift(lax.broadcasted_iota(I32, (QROWS, WIN), 1), CHUNK_SHIFT)
        band = (kc >= qc) & (kc <= qc + LEFT_CHUNKS)
        for h in range(H):
            gb = jnp.broadcast_to(g[h:h + 1, :], (QROWS, DIAGS))
            tile = pltpu.roll(gb, DIAGS - (QROWS - 1), 1, stride=1, stride_axis=0)
            o_ref[h // 2, (h % 2) * QROWS:(h % 2 + 1) * QROWS, :] = jnp.where(band, tile[:, :WIN], NEG_INF)

    return pl.pallas_call(
        body, name=name, out_shape=jax.ShapeDtypeStruct((H // 2, 2 * QROWS, WIN), F32),
        compiler_params=_params(0, 2 * _nbytes((H // 2, 2 * QROWS, WIN), F32)),
    )(tpad)


def rel_bias_grad(name, dbias):
    H = 2 * dbias.shape[0]

    def body(d_ref, o_ref):
        flip = (lax.broadcasted_iota(I32, (QROWS, QROWS), 0) + lax.broadcasted_iota(I32, (QROWS, QROWS), 1)
                == QROWS - 1).astype(F32)
        rows = []
        for h in range(H):
            x = d_ref[h // 2, (h % 2) * QROWS:(h % 2 + 1) * QROWS, :]
            xr = lax.dot_general(flip, x, NN, precision=lax.Precision.HIGHEST, preferred_element_type=F32)
            xp = jnp.concatenate([xr, jnp.zeros((QROWS, DIAGS - WIN), F32)], axis=1)
            y = pltpu.roll(xp, 0, 1, stride=1, stride_axis=0)
            rows.append(jnp.sum(y, axis=0, keepdims=True))
        o_ref[...] = lax.dot_general(jnp.concatenate(rows, axis=0), _diag_onehot(), NT,
                                     precision=lax.Precision.HIGHEST, preferred_element_type=F32)

    return pl.pallas_call(
        body, name=name, out_shape=jax.ShapeDtypeStruct((H, REL_PAD), F32),
        compiler_params=_params(0, 2 * _nbytes(dbias.shape, F32)),
    )(dbias)


def _stack_pair(xp):
    lane = lax.broadcasted_iota(I32, xp.shape, 1)
    z = jnp.zeros_like(xp)
    return jnp.concatenate([jnp.where(lane < HEAD_DIM_A, xp, z), jnp.where(lane >= HEAD_DIM_A, xp, z)], axis=0)


def _unstack_pair(y):
    lane = lax.broadcasted_iota(I32, (QROWS, LANE), 1)
    return jnp.where(lane < HEAD_DIM_A, y[:QROWS], y[QROWS:])


def _attn_a_rowpen(j):
    w = lax.broadcasted_iota(I32, (1, WIN), 1)
    return jnp.where(w >= PADR - QROWS * j, 0.0, NEG_INF).astype(F32)


def _attn_a_load_bias(bias_hbm, bias_v, sem):
    cp = pltpu.make_async_copy(bias_hbm, bias_v, sem)
    cp.start()
    cp.wait()


def _attn_a_load_kv(qkv_hbm, b, kpad, vpad, sem, S, D):
    kpad[0:PADR, :] = jnp.zeros((PADR, D), BF16)
    vpad[0:PADR, :] = jnp.zeros((PADR, D), BF16)
    ck = pltpu.make_async_copy(qkv_hbm.at[b, :, pl.ds(D, D)], kpad.at[pl.ds(PADR, S), :], sem.at[0])
    cv = pltpu.make_async_copy(qkv_hbm.at[b, :, pl.ds(2 * D, D)], vpad.at[pl.ds(PADR, S), :], sem.at[1])
    ck.start()
    cv.start()
    ck.wait()
    cv.wait()


def _attn_a_exp(q2s, kp, bias, pen):
    s = lax.dot_general(q2s, kp, NT, preferred_element_type=F32) + bias + pen
    e = jnp.exp(s - jnp.max(s, axis=-1, keepdims=True))
    return e, 1.0 / jnp.sum(e, axis=-1, keepdims=True)


def attn_a_fwd(name, qkv, bias):
    B, S, D3 = qkv.shape
    D = D3 // 3
    H = D // HEAD_DIM_A
    nb = S // QROWS
    scale = HEAD_DIM_A ** -0.5

    def body(q_ref, bias_hbm, qkv_hbm, o_ref, kpad, vpad, bias_v, sem):
        b, j = pl.program_id(0), pl.program_id(1)

        @pl.when((b == 0) & (j == 0))
        def _():
            _attn_a_load_bias(bias_hbm, bias_v, sem.at[2])

        @pl.when(j == 0)
        def _():
            _attn_a_load_kv(qkv_hbm, b, kpad, vpad, sem, S, D)

        pen = _attn_a_rowpen(j)
        w0 = pl.multiple_of(j * QROWS, QROWS)
        for p in range(H // 2):
            ls = slice(p * LANE, (p + 1) * LANE)
            e, rl = _attn_a_exp(_stack_pair(q_ref[0, :, ls] * scale), kpad[pl.ds(w0, WIN), ls], bias_v[p], pen)
            o2 = jnp.dot(e.astype(BF16), vpad[pl.ds(w0, WIN), ls], preferred_element_type=F32) * rl
            o_ref[0, :, ls] = _unstack_pair(o2).astype(o_ref.dtype)

    scr = 2 * _nbytes((PADR + S, D), BF16) + _nbytes(bias.shape, F32) + 8 * _nbytes((2 * QROWS, WIN), F32)
    return pl.pallas_call(
        body, name=name, grid=(B, nb),
        in_specs=[pl.BlockSpec((1, QROWS, D), lambda b, j: (b, j, 0)),
                  pl.BlockSpec(memory_space=pl.ANY), pl.BlockSpec(memory_space=pl.ANY)],
        out_specs=pl.BlockSpec((1, QROWS, D), lambda b, j: (b, j, 0)),
        out_shape=jax.ShapeDtypeStruct((B, S, D), BF16),
        scratch_shapes=[pltpu.VMEM((PADR + S, D), BF16), pltpu.VMEM((PADR + S, D), BF16),
                        pltpu.VMEM(bias.shape, F32), pltpu.SemaphoreType.DMA((3,))],
        compiler_params=_params(2 * _nbytes((QROWS, D), BF16), scr),
    )(qkv, bias, qkv)


def attn_a_bwd(name, qkv, do, bias):
    B, S, D3 = qkv.shape
    D = D3 // 3
    H = D // HEAD_DIM_A
    nb = S // QROWS
    scale = HEAD_DIM_A ** -0.5

    def body(q_ref, do_ref, bias_hbm, qkv_hbm, dq_ref, dkv_hbm, dbias_hbm, kpad, vpad, dkacc, dvacc, bias_v, dbias_v, sem):
        b, j = pl.program_id(0), pl.program_id(1)

        @pl.when((b == 0) & (j == 0))
        def _():
            _attn_a_load_bias(bias_hbm, bias_v, sem.at[2])
            dbias_v[...] = jnp.zeros_like(dbias_v)

        @pl.when(j == 0)
        def _():
            _attn_a_load_kv(qkv_hbm, b, kpad, vpad, sem, S, D)
            dkacc[...] = jnp.zeros_like(dkacc)
            dvacc[...] = jnp.zeros_like(dvacc)

        pen = _attn_a_rowpen(j)
        w0 = pl.multiple_of(j * QROWS, QROWS)
        for p in range(H // 2):
            ls = slice(p * LANE, (p + 1) * LANE)
            q2s = _stack_pair(q_ref[0, :, ls] * scale)
            do2 = _stack_pair(do_ref[0, :, ls])
            kp = kpad[pl.ds(w0, WIN), ls]
            vp = vpad[pl.ds(w0, WIN), ls]
            e, rl = _attn_a_exp(q2s, kp, bias_v[p], pen)
            pr = e * rl
            dp = lax.dot_general(do2, vp, NT, preferred_element_type=F32)
            ds = pr * (dp - jnp.sum(pr * dp, axis=-1, keepdims=True))
            dbias_v[p] += ds
            dsb = ds.astype(BF16)
            dq_ref[0, :, ls] = _unstack_pair(jnp.dot(dsb, kp, preferred_element_type=F32)) * scale
            dkacc[pl.ds(w0, WIN), ls] += lax.dot_general(dsb, q2s, TN, preferred_element_type=F32)
            dvacc[pl.ds(w0, WIN), ls] += lax.dot_general(pr.astype(BF16), do2, TN, preferred_element_type=F32)

        @pl.when(j == nb - 1)
        def _():
            ck = pltpu.make_async_copy(dkacc.at[pl.ds(PADR, S), :], dkv_hbm.at[b, :, pl.ds(0, D)], sem.at[0])
            cv = pltpu.make_async_copy(dvacc.at[pl.ds(PADR, S), :], dkv_hbm.at[b, :, pl.ds(D, D)], sem.at[1])
            ck.start()
            cv.start()
            ck.wait()
            cv.wait()

        @pl.when((b == B - 1) & (j == nb - 1))
        def _():
            cb = pltpu.make_async_copy(dbias_v, dbias_hbm, sem.at[2])
            cb.start()
            cb.wait()

    blk = _nbytes((QROWS, D), BF16) * 2 + _nbytes((QROWS, D), F32)
    scr = (2 * _nbytes((PADR + S, D), BF16) + 2 * _nbytes((PADR + S, D), F32) + 2 * _nbytes(bias.shape, F32)
           + 8 * _nbytes((2 * QROWS, WIN), F32))
    return pl.pallas_call(
        body, name=name, grid=(B, nb),
        in_specs=[pl.BlockSpec((1, QROWS, D), lambda b, j: (b, j, 0)),
                  pl.BlockSpec((1, QROWS, D), lambda b, j: (b, j, 0)),
                  pl.BlockSpec(memory_space=pl.ANY), pl.BlockSpec(memory_space=pl.ANY)],
        out_specs=[pl.BlockSpec((1, QROWS, D), lambda b, j: (b, j, 0)),
                   pl.BlockSpec(memory_space=pl.ANY), pl.BlockSpec(memory_space=pl.ANY)],
        out_shape=[jax.ShapeDtypeStruct((B, S, D), F32), jax.ShapeDtypeStruct((B, S, 2 * D), F32),
                   jax.ShapeDtypeStruct(bias.shape, F32)],
        scratch_shapes=[pltpu.VMEM((PADR + S, D), BF16), pltpu.VMEM((PADR + S, D), BF16),
                        pltpu.VMEM((PADR + S, D), F32), pltpu.VMEM((PADR + S, D), F32),
                        pltpu.VMEM(bias.shape, F32), pltpu.VMEM(bias.shape, F32),
                        pltpu.SemaphoreType.DMA((3,))],
        compiler_params=_params(blk, scr),
    )(qkv, do, bias, qkv)


def _mla_raw_t(k2, kj, q, QB):
    return lax.dot_general(k2[_blk(kj, QB), :], q, NT, preferred_element_type=F32)


def _blk(kj, QB):
    return pl.ds(kj * QB, QB) if isinstance(kj, int) else pl.ds(pl.multiple_of(kj * QB, QB), QB)


def _mla_diag_pen(QB):
    kc = jnp.right_shift(lax.broadcasted_iota(I32, (QB, QB), 0), CHUNK_SHIFT)
    qc = jnp.right_shift(lax.broadcasted_iota(I32, (QB, QB), 1), CHUNK_SHIFT)
    return jnp.where(kc <= qc, 0.0, NEG_INF).astype(F32)


def _mla_fill_keys(kv_ref, kr_ref, k2):
    k2[:, :NOPE] = kv_ref[0, :, :NOPE]
    k2[:, NOPE:] = kr_ref[0]


def _t(x):
    return x.astype(F32).T


def mla_fwd(name, qf, kv, kr):
    B, S, W = qf.shape
    HB = W // 256
    QB = _tile(S, 256, CHUNK)
    nq = S // QB
    scale = (NOPE + ROPE) ** -0.5

    def body(q_ref, kv_ref, kr_ref, o_ref, lse_ref, k2, vt, st_buf, pen):
        qi = pl.program_id(2)

        @pl.when(qi == 0)
        def _():
            pen[...] = _mla_diag_pen(QB)
            _mla_fill_keys(kv_ref, kr_ref, k2)
            for kj in range(nq):
                vt[kj] = _t(kv_ref[0, kj * QB:(kj + 1) * QB, NOPE:]).astype(BF16)

        q = q_ref[0]
        st_buf[0] = _mla_raw_t(k2, 0, q, QB)

        def step(kj, carry):
            m, l, acc = carry
            cur = lax.rem(kj, 2)
            st_raw = st_buf[cur]
            st_buf[1 - cur] = _mla_raw_t(k2, jnp.minimum(kj + 1, qi), q, QB)
            st = st_raw * scale + jnp.where(kj == qi, pen[...], 0.0)
            m_new = jnp.maximum(m, jnp.max(st, axis=0, keepdims=True))
            a = jnp.exp(m - m_new)
            pt = jnp.exp(st - m_new)
            l = a * l + jnp.sum(pt, axis=0, keepdims=True)
            acc = a * acc + jnp.dot(vt[kj], pt.astype(BF16), preferred_element_type=F32)
            return m_new, l, acc

        init = (jnp.full((1, QB), NEG_INF, F32), jnp.zeros((1, QB), F32), jnp.zeros((NOPE, QB), F32))
        m, l, acc = lax.fori_loop(0, qi + 1, step, init)
        o_ref[0] = (acc * (1.0 / l)).T
        lse_ref[0, 0] = m + jnp.log(l)

    blk = (_nbytes((QB, 256), BF16) + _nbytes((S, 256), BF16) + _nbytes((S, LANE), BF16)
           + _nbytes((QB, LANE), F32))
    return pl.pallas_call(
        body, name=name, grid=(B, HB, nq),
        in_specs=[pl.BlockSpec((1, QB, 256), lambda b, h, i: (b, i, h)),
                  pl.BlockSpec((1, S, 256), lambda b, h, i: (b, 0, h)),
                  pl.BlockSpec((1, S, LANE), lambda b, h, i: (b, 0, 0))],
        out_specs=[pl.BlockSpec((1, QB, LANE), lambda b, h, i: (b, i, h)),
                   pl.BlockSpec((1, 1, 1, QB), lambda b, h, i: (b, h, 0, i))],
        out_shape=[jax.ShapeDtypeStruct((B, S, HB * LANE), F32), jax.ShapeDtypeStruct((B, HB, 1, S), F32)],
        scratch_shapes=[pltpu.VMEM((S, 256), BF16), pltpu.VMEM((nq, NOPE, QB), BF16),
                        pltpu.VMEM((2, QB, QB), F32), pltpu.VMEM((QB, QB), F32)],
        compiler_params=_params(blk, 2 * _nbytes((S, 256), BF16) + 10 * _nbytes((QB, QB), F32)),
    )(qf, kv, kr)


def mla_bwd(name, qf, kv, kr, do, o, lse):
    B, S, W = qf.shape
    HB = W // 256
    QB = _tile(S, 256, CHUNK)
    nq = S // QB
    scale = (NOPE + ROPE) ** -0.5

    def body(q_ref, kv_ref, kr_ref, do_ref, o_ref, lse_ref, dq_ref, dkv_ref, dkr_ref, k2, kt, dot_, delta, dqt,
             st_buf, dp_buf, pen):
        h = pl.program_id(1)
        pen[...] = _mla_diag_pen(QB)
        dkv_ref[...] = jnp.zeros_like(dkv_ref)

        @pl.when(h == 0)
        def _():
            dkr_ref[...] = jnp.zeros_like(dkr_ref)

        _mla_fill_keys(kv_ref, kr_ref, k2)
        for i in range(nq):
            rows = slice(i * QB, (i + 1) * QB)
            kt[i] = _t(k2[rows, :]).astype(BF16)
            dot32 = _t(do_ref[0, rows, :])
            delta[i] = jnp.sum(dot32 * o_ref[0, rows, :].T, axis=0, keepdims=True)
            dot_[i] = dot32.astype(BF16)

        for qi in range(nq):
            rows = slice(qi * QB, (qi + 1) * QB)
            q = q_ref[0, rows, :]
            dob = do_ref[0, rows, :]
            lse_q = lse_ref[0, 0, :, rows]
            delta_q = delta[qi]
            dqt[...] = jnp.zeros_like(dqt)

            def raw(kj, slot, q=q, qi=qi):
                st_buf[slot] = _mla_raw_t(k2, kj, q, QB)
                dp_buf[slot] = jnp.dot(kv_ref[0, _blk(kj, QB), NOPE:], dot_[qi], preferred_element_type=F32)

            raw(0, 0)

            def step(kj, carry, q=q, dob=dob, lse_q=lse_q, delta_q=delta_q, qi=qi, raw=raw):
                ks = pl.ds(pl.multiple_of(kj * QB, QB), QB)
                cur = lax.rem(kj, 2)
                st_raw, dp_raw = st_buf[cur], dp_buf[cur]
                raw(jnp.minimum(kj + 1, qi), 1 - cur)
                pt = jnp.exp(st_raw * scale + jnp.where(kj == qi, pen[...], 0.0) - lse_q)
                dst = (pt * (dp_raw - delta_q) * scale).astype(BF16)
                dkv_ref[0, ks, NOPE:] += jnp.dot(pt.astype(BF16), dob, preferred_element_type=F32)
                dk2 = jnp.dot(dst, q, preferred_element_type=F32)
                dkv_ref[0, ks, :NOPE] += dk2[:, :NOPE]
                dkr_ref[0, ks, :] += dk2[:, NOPE:]
                dqt[...] += jnp.dot(kt[kj], dst, preferred_element_type=F32)
                return carry

            lax.fori_loop(0, qi + 1, step, 0)
            dq_ref[0, rows, :] = dqt[...].T

    head = lambda w: pl.BlockSpec((1, S, w), lambda b, h: (b, 0, h))
    shared = pl.BlockSpec((1, S, LANE), lambda b, h: (b, 0, 0))
    blk = (2 * _nbytes((S, 256), BF16) + 2 * _nbytes((S, LANE), BF16) + _nbytes((S, LANE), F32)
           + 2 * _nbytes((S, 256), F32) + _nbytes((S, LANE), F32))
    scr = 3 * _nbytes((S, 256), BF16) + 14 * _nbytes((QB, QB), F32)
    return pl.pallas_call(
        body, name=name, grid=(B, HB),
        in_specs=[head(256), head(256), shared, head(LANE), head(LANE),
                  pl.BlockSpec((1, 1, 1, S), lambda b, h: (b, h, 0, 0))],
        out_specs=[head(256), head(256), shared],
        out_shape=[jax.ShapeDtypeStruct((B, S, W), F32), jax.ShapeDtypeStruct((B, S, W), F32),
                   jax.ShapeDtypeStruct((B, S, LANE), F32)],
        scratch_shapes=[pltpu.VMEM((S, 256), BF16), pltpu.VMEM((nq, 256, QB), BF16),
                        pltpu.VMEM((nq, NOPE, QB), BF16), pltpu.VMEM((nq, 1, QB), F32),
                        pltpu.VMEM((256, QB), F32), pltpu.VMEM((2, QB, QB), F32), pltpu.VMEM((2, QB, QB), F32),
                        pltpu.VMEM((QB, QB), F32)],
        compiler_params=_params(blk, scr),
    )(qf, kv, kr, do, o, lse)


GROUP_STEPS = 4


def cast_group(name, ws, layers, idx):
    n = len(ws)

    def body(k_ref, *refs):
        for i in range(n):
            refs[n + i][...] = refs[i][...].astype(BF16)

    def spec_in(w, layer):
        return pl.BlockSpec((None, w.shape[1] // GROUP_STEPS, w.shape[2]), lambda r, k_ref: (layer, r, 0))

    def spec_out(w):
        return pl.BlockSpec((None, w.shape[1] // GROUP_STEPS, w.shape[2]), lambda r, k_ref: (k_ref[0], r, 0))

    return pl.pallas_call(
        body, name=name,
        grid_spec=pltpu.PrefetchScalarGridSpec(
            num_scalar_prefetch=1, grid=(GROUP_STEPS,),
            in_specs=[spec_in(w, l) for w, l in zip(ws, layers)], out_specs=[spec_out(w) for w in ws]),
        out_shape=[jax.ShapeDtypeStruct((N_CHIPS, *w.shape[1:]), BF16) for w in ws],
        compiler_params=_params(sum(_nbytes(w.shape[1:], F32) * 3 // 2 for w in ws) // GROUP_STEPS),
    )(idx, *ws)


def adamw(name, w, g, m, v):
    R, C = w.shape
    tr = _tile(R, max(8, (1 << 18) // C // 8 * 8), 8)
    c1 = 1.0 - ADAM_B1 ** ADAM_STEP
    c2 = 1.0 - ADAM_B2 ** ADAM_STEP

    def body(w_ref, g_ref, m_ref, v_ref, d_ref, mo_ref, vo_ref):
        gv = g_ref[...]
        mn = ADAM_B1 * m_ref[...] + (1.0 - ADAM_B1) * gv
        vn = ADAM_B2 * v_ref[...] + (1.0 - ADAM_B2) * (gv * gv)
        mo_ref[...] = mn
        vo_ref[...] = vn
        d_ref[...] = -ADAM_LR * ((mn / c1) / (jnp.sqrt(vn / c2) + ADAM_EPS) + ADAM_WD * w_ref[...])

    spec = pl.BlockSpec((tr, C), lambda r: (r, 0))
    return pl.pallas_call(
        body, name=name, grid=(R // tr,), in_specs=[spec] * 4, out_specs=[spec] * 3,
        out_shape=[jax.ShapeDtypeStruct((R, C), F32)] * 3,
        compiler_params=_params(7 * _nbytes((tr, C), F32), 4 * _nbytes((tr, C), F32)),
    )(w, g, m, v)


def half_sum_group(name, dws, landed, idx):
    n = len(dws)
    steps = GROUP_STEPS // 2

    def body(i_ref, *refs):
        for i in range(n):
            refs[2 * n + i][...] = (refs[i][...].astype(F32) + refs[n + i][...].astype(F32)).astype(BF16)

    def own(d):
        return pl.BlockSpec((None, None, d.shape[2] // steps, d.shape[3]), lambda k, r, i_ref: (k, i_ref[1], r, 0))

    def flat(d):
        return pl.BlockSpec((None, d.shape[2] // steps, d.shape[3]), lambda k, r, i_ref: (k, r, 0))

    return pl.pallas_call(
        body, name=name,
        grid_spec=pltpu.PrefetchScalarGridSpec(
            num_scalar_prefetch=1, grid=(N_CHIPS, steps),
            in_specs=[own(d) for d in dws] + [flat(d) for d in dws], out_specs=[flat(d) for d in dws]),
        out_shape=[jax.ShapeDtypeStruct((N_CHIPS, *d.shape[2:]), BF16) for d in dws],
        compiler_params=_params(sum(3 * _nbytes(d.shape[2:], BF16) for d in dws) // steps),
    )(idx, *dws, *landed)


def chip_sum_group(name, parts, landed, gbufs, layers, idx):
    n = len(parts)
    steps = GROUP_STEPS // 2

    def body(i_ref, *refs):
        for i in range(n):
            a, b = refs[i], refs[n + i]
            refs[3 * n + i][...] = ((a[...].astype(F32) + b[0].astype(F32)) + b[1].astype(F32)) + b[2].astype(F32)

    def mine(p):
        return pl.BlockSpec((None, p.shape[1] // steps, p.shape[2]), lambda r, i_ref: (i_ref[0], r, 0))

    def three(p):
        return pl.BlockSpec((3, p.shape[1] // steps, p.shape[2]), lambda r, i_ref: (0, r, 0))

    def out(p, layer):
        return pl.BlockSpec((None, None, p.shape[1] // steps, p.shape[2]), lambda r, i_ref: (layer, i_ref[1], r, 0))

    return pl.pallas_call(
        body, name=name,
        grid_spec=pltpu.PrefetchScalarGridSpec(
            num_scalar_prefetch=1, grid=(steps,),
            in_specs=[mine(p) for p in parts] + [three(p) for p in parts] + [pl.BlockSpec(memory_space=pl.ANY)] * n,
            out_specs=[out(p, l) for p, l in zip(parts, layers)]),
        out_shape=[jax.ShapeDtypeStruct(g.shape, F32) for g in gbufs],
        input_output_aliases={1 + 2 * n + i: i for i in range(n)},
        compiler_params=_params(sum(6 * _nbytes(p.shape[1:], BF16) for p in parts) // steps),
    )(idx, *parts, *landed, *gbufs)


ANY = pl.BlockSpec(memory_space=pl.ANY)


def _place():
    x, y, c = lax.axis_index("x"), lax.axis_index("y"), lax.axis_index("c")
    chips = [(1 - x, y), (x, 1 - y), (1 - x, 1 - y)]
    return x, y, c, chips


HBM = pl.BlockSpec(memory_space=pltpu.HBM)
SEM = pl.BlockSpec(memory_space=pltpu.SEMAPHORE)
EFFECT = pltpu.SideEffectType.DATAFLOW_SIDE_EFFECTING


def _in_hbm(a):
    return pltpu.with_memory_space_constraint(a, pltpu.HBM)


def _ici_copy(src, dst, send_sems, recv_sems, k, peer):
    return pltpu.make_async_remote_copy(src_ref=src, dst_ref=dst, send_sem=send_sems.at[k], recv_sem=recv_sems.at[k],
                                        device_id=peer, device_id_type=MESH)


def ici_start(name, bufs, lands, after, gather):
    n, nl = len(bufs), len(lands)

    def body(*refs):
        b_in = refs[:n]
        send_sems, recv_sems = refs[n + nl + 1], refs[n + nl + 2]
        b_out = refs[n + nl + 3:2 * n + nl + 3]
        l_out = refs[2 * n + nl + 3:2 * n + 2 * nl + 3]
        token = refs[-1]
        x, y, c, chips = _place()
        kme = 2 * x + y
        for i in range(n):
            for j in range(3):
                peer = (*chips[j], c)
                if gather:
                    _ici_copy(b_out[i].at[kme, c], b_out[i].at[kme, c], send_sems, recv_sems, 3 * i + j, peer).start()
                else:
                    kd = 2 * chips[j][0] + chips[j][1]
                    _ici_copy(b_out[i].at[kd], l_out[i].at[j], send_sems, recv_sems, 3 * i + j, peer).start()
        token[...] = jnp.zeros_like(token)

    arrays = [*bufs, *lands]
    outs = pl.pallas_call(
        body, name=name,
        in_specs=[HBM] * (n + nl) + [ANY],
        out_specs=(SEM, SEM, *[HBM] * (n + nl), pl.BlockSpec(memory_space=pltpu.VMEM)),
        out_shape=(pltpu.SemaphoreType.DMA((3 * n,)), pltpu.SemaphoreType.DMA((3 * n,)),
                   *[pltpu.HBM(a.shape, a.dtype) for a in arrays], jax.ShapeDtypeStruct((8, LANE), F32)),
        input_output_aliases={i: 2 + i for i in range(n + nl)},
        compiler_params=pltpu.CompilerParams(has_side_effects=EFFECT),
    )(*[_in_hbm(a) for a in arrays], after)
    return outs[0], outs[1], list(outs[2:2 + n]), list(outs[2 + n:2 + n + nl]), outs[-1]


def ici_wait(name, send_sems, recv_sems, bufs, lands, after, gather):
    n, nl = len(bufs), len(lands)

    def body(*refs):
        b_in, l_in = refs[:n], refs[n:n + nl]
        send_sems, recv_sems = refs[n + nl], refs[n + nl + 1]
        x, y, c, chips = _place()
        kme = 2 * x + y
        for i in range(n):
            for j in range(3):
                peer = (*chips[j], c)
                kj = 2 * chips[j][0] + chips[j][1]
                if gather:
                    _ici_copy(b_in[i].at[kme, c], b_in[i].at[kme, c], send_sems, recv_sems, 3 * i + j, peer).wait_send()
                    _ici_copy(b_in[i].at[kj, c], b_in[i].at[kj, c], send_sems, recv_sems, 3 * i + j, peer).wait_recv()
                else:
                    _ici_copy(b_in[i].at[kj], l_in[i].at[j], send_sems, recv_sems, 3 * i + j, peer).wait_send()
                    _ici_copy(b_in[i].at[kj], l_in[i].at[j], send_sems, recv_sems, 3 * i + j, peer).wait_recv()

    arrays = [*bufs, *lands]
    outs = pl.pallas_call(
        body, name=name,
        in_specs=[HBM] * (n + nl) + [SEM, SEM, ANY],
        out_specs=tuple([HBM] * (n + nl)),
        out_shape=tuple(pltpu.HBM(a.shape, a.dtype) for a in arrays),
        input_output_aliases={i: i for i in range(n + nl)},
        compiler_params=pltpu.CompilerParams(has_side_effects=EFFECT),
    )(*arrays, send_sems, recv_sems, after)
    return list(outs[:n]), list(outs[n:])


def gather_pair_pass(name, bufs):
    n = len(bufs)

    def body(*refs):
        b = refs[n:2 * n]
        send_sems, recv_sems = refs[2 * n:]
        x, y, c, chips = _place()
        sib = (x, y, 1 - c)

        def d2d(i, j, which):
            kj = 2 * chips[j][0] + chips[j][1]
            return _ici_copy(b[i].at[kj, which], b[i].at[kj, which], send_sems, recv_sems, 3 * i + j, sib)

        for i in range(n):
            for j in range(3):
                d2d(i, j, c).start()
        for i in range(n):
            for j in range(3):
                d2d(i, j, 1 - c).wait_recv()
        for i in range(n):
            for j in range(3):
                d2d(i, j, c).wait_send()

    return pl.pallas_call(
        body, name=name, in_specs=[ANY] * n, out_specs=[ANY] * n,
        out_shape=[jax.ShapeDtypeStruct(a.shape, a.dtype) for a in bufs],
        input_output_aliases={i: i for i in range(n)},
        scratch_shapes=[pltpu.SemaphoreType.DMA((3 * n,)), pltpu.SemaphoreType.DMA((3 * n,))],
    )(*bufs)


def pair_exchange(name, dws):
    n = len(dws)

    def body(*refs):
        ins, outs = refs[:n], refs[n:2 * n]
        send_sems, recv_sems = refs[2 * n:]
        x, y, c, _ = _place()
        copies = []
        for i in range(n):
            copies.append(pltpu.make_async_remote_copy(
                src_ref=ins[i].at[:, 1 - c], dst_ref=outs[i],
                send_sem=send_sems.at[i], recv_sem=recv_sems.at[i],
                device_id=(x, y, 1 - c), device_id_type=MESH))
            copies[i].start()
        for cp in copies:
            cp.wait_recv()
        for cp in copies:
            cp.wait_send()

    return pl.pallas_call(
        body, name=name, in_specs=[ANY] * n, out_specs=[ANY] * n,
        out_shape=[jax.ShapeDtypeStruct((N_CHIPS, *d.shape[2:]), d.dtype) for d in dws],
        scratch_shapes=[pltpu.SemaphoreType.DMA((n,)), pltpu.SemaphoreType.DMA((n,))],
    )(*dws)


def pair_assemble(gbufs):
    n = len(gbufs)

    def body(*refs):
        bufs = refs[n:2 * n]
        send_sems, recv_sems = refs[2 * n:]
        x, y, c, _ = _place()
        copies = []
        for i in range(n):
            copies.append(pltpu.make_async_remote_copy(
                src_ref=bufs[i].at[:, c], dst_ref=bufs[i].at[:, c],
                send_sem=send_sems.at[i], recv_sem=recv_sems.at[i],
                device_id=(x, y, 1 - c), device_id_type=MESH))
            copies[i].start()
        for i in range(n):
            pltpu.make_async_remote_copy(
                src_ref=bufs[i].at[:, 1 - c], dst_ref=bufs[i].at[:, 1 - c],
                send_sem=send_sems.at[i], recv_sem=recv_sems.at[i],
                device_id=(x, y, 1 - c), device_id_type=MESH).wait_recv()
        for cp in copies:
            cp.wait_send()

    return pl.pallas_call(
        body, name="grad_pair_assemble", in_specs=[ANY] * n, out_specs=[ANY] * n,
        out_shape=[jax.ShapeDtypeStruct(g.shape, g.dtype) for g in gbufs],
        input_output_aliases={i: i for i in range(n)},
        scratch_shapes=[pltpu.SemaphoreType.DMA((n,)), pltpu.SemaphoreType.DMA((n,))],
    )(*gbufs)


def all_reduce_small(vec):
    NR = vec.shape[0]
    flips = [(fx, fy, fc) for fx in (0, 1) for fy in (0, 1) for fc in (0, 1)][1:]

    def body(v_ref, o_ref, buf, send_sems, recv_sems):
        x, y, c, _ = _place()
        me = 4 * x + 2 * y + c
        buf[me] = v_ref[...]
        copies = []
        for j, (fx, fy, fc) in enumerate(flips):
            peer = (1 - x if fx else x, 1 - y if fy else y, 1 - c if fc else c)
            copies.append(pltpu.make_async_remote_copy(
                src_ref=v_ref, dst_ref=buf.at[me], send_sem=send_sems.at[j], recv_sem=recv_sems.at[j],
                device_id=peer, device_id_type=MESH))
            copies[j].start()
        for cp in copies:
            cp.wait_recv()
        for cp in copies:
            cp.wait_send()
        acc = buf[0]
        for d in range(1, 8):
            acc = acc + buf[d]
        o_ref[...] = acc

    return pl.pallas_call(
        body, name="all_reduce_small",
        in_specs=[pl.BlockSpec(memory_space=pltpu.VMEM)], out_specs=pl.BlockSpec(memory_space=pltpu.VMEM),
        out_shape=jax.ShapeDtypeStruct((NR, LANE), F32),
        scratch_shapes=[pltpu.VMEM((8, NR, LANE), F32), pltpu.SemaphoreType.DMA((7,)),
                        pltpu.SemaphoreType.DMA((7,))],
    )(vec)


def _pack(arrays):
    flat = jnp.concatenate([a.reshape(-1).astype(F32) for a in arrays])
    n = flat.shape[0]
    npad = -(-n // (8 * LANE)) * (8 * LANE)
    return jnp.pad(flat, (0, npad - n)).reshape(npad // LANE, LANE)


def _unpack(buf, like):
    flat = buf.reshape(-1)
    out, off = [], 0
    for a in like:
        out.append(flat[off:off + a.size].reshape(a.shape))
        off += a.size
    return out


def kernel(x, ffn1_norm, ffn1_w_in, ffn1_w_out, mix_norm, ffn2_norm, ffn2_w_in, ffn2_w_out, a_w_qkv, a_rel_bias, a_w_o, kv_norm, kv_w_down, kv_latent_norm, kv_w_up, b_w_dq, b_q_norm, b_w_uq, b_w_o, final_norm, loss_target, m_ffn1_norm, m_ffn1_w_in, m_ffn1_w_out, m_mix_norm, m_ffn2_norm, m_ffn2_w_in, m_ffn2_w_out, m_a_w_qkv, m_a_rel_bias, m_a_w_o, m_kv_norm, m_kv_w_down, m_kv_latent_norm, m_kv_w_up, m_b_w_dq, m_b_q_norm, m_b_w_uq, m_b_w_o, m_final_norm, v_ffn1_norm, v_ffn1_w_in, v_ffn1_w_out, v_mix_norm, v_ffn2_norm, v_ffn2_w_in, v_ffn2_w_out, v_a_w_qkv, v_a_rel_bias, v_a_w_o, v_kv_norm, v_kv_w_down, v_kv_latent_norm, v_kv_w_up, v_b_w_dq, v_b_q_norm, v_b_w_uq, v_b_w_o, v_final_norm):
    B, S, D = x.shape
    T = B * S
    HB = D // 128
    QL = b_q_norm.shape[-1]
    KVL = kv_latent_norm.shape[0]
    hpc = HB // N_CHIPS
    tabs = rope_tables(S)
    idx = jnp.stack([2 * lax.axis_index("x") + lax.axis_index("y"), lax.axis_index("c")]).astype(I32)

    def halves(a):
        return a.reshape(*a.shape[:-2], 2, a.shape[-2] // 2, a.shape[-1])

    def whole(a):
        return a.reshape(*a.shape[:-3], 2 * a.shape[-2], a.shape[-1])

    kv_w_down_p = jnp.pad(kv_w_down, ((0, 0), (0, LANE - ROPE)))[None]
    b_w_uq_p = jnp.pad(b_w_uq.reshape(1, QL, hpc, NOPE + ROPE),
                       ((0, 0), (0, 0), (0, 0), (0, LANE - ROPE))).reshape(1, QL, hpc * 256)
    sharded = [("ffn1_w_in", ffn1_w_in), ("ffn1_w_out", ffn1_w_out), ("ffn2_w_in", ffn2_w_in),
               ("ffn2_w_out", ffn2_w_out), ("a_w_qkv", a_w_qkv), ("a_w_o", a_w_o),
               ("kv_w_down", kv_w_down_p), ("kv_w_up", kv_w_up[None]), ("b_w_dq", b_w_dq),
               ("b_w_uq", b_w_uq_p), ("b_w_o", b_w_o)]
    pieces = [(a, l) for a, (_, w) in enumerate(sharded) for l in range(w.shape[0])]
    names = [nm for nm, _ in sharded]
    shard_of = dict(sharded)
    W = {}

    gather_groups = [
        [("ffn1_w_in", 0), ("ffn1_w_out", 0)],
        [("a_w_qkv", 0), ("a_w_o", 0), ("ffn2_w_in", 0), ("ffn2_w_out", 0), ("kv_w_down", 0), ("kv_w_up", 0)],
        [("ffn1_w_in", 1), ("ffn1_w_out", 1), ("b_w_dq", 0), ("b_w_uq", 0), ("b_w_o", 0), ("ffn2_w_in", 1),
         ("ffn2_w_out", 1)]]

    own = {}
    for g, keys in enumerate(gather_groups):
        cast = cast_group(f"cast_group_{g}", [shard_of[nm] for nm, _ in keys], [l for _, l in keys], idx)
        own.update(zip(keys, cast))

    def gather_start(g, after):
        keys = gather_groups[g]
        ss, rs, bufs, _, token = ici_start(f"gather_start_{g}", [halves(own[k]) for k in keys], [], after, True)
        return (g, ss, rs, bufs), token

    def gather_finish(state, after):
        g, ss, rs, bufs = state
        bufs, _ = ici_wait(f"gather_wait_{g}", ss, rs, bufs, [], after, True)
        full = gather_pair_pass(f"gather_pair_{g}", bufs)
        for k, w in zip(gather_groups[g], full):
            W[k] = whole(w)
        return full[0]

    def tied(a, token):
        return a + token[0, 0]

    def col(nm, l=0):
        return W[(nm, l)]

    def row(nm, l=0):
        w = W[(nm, l)]
        return w.reshape(N_CHIPS * w.shape[1], w.shape[2])

    bias = rel_bias_tile("rel_bias_tile", a_rel_bias[0])

    def ffn_fwd(tag, h, g, w_in, w_out):
        xn = rms_fwd(f"{tag}_norm", h, g)
        u, act = ffn_in_act(f"{tag}_in", xn, w_in)
        return mm_roww(f"{tag}_out", act, w_out, F32, res=h, alpha=0.5), (xn, u, act)

    h0 = x.reshape(T, D)
    st0, tok0 = gather_start(0, h0)
    done0 = gather_finish(st0, tok0)
    st1, tok1 = gather_start(1, done0)
    h1, sv_f1a = ffn_fwd("l0f1", h0, tied(ffn1_norm[0], tok1), col("ffn1_w_in", 0), row("ffn1_w_out", 0))
    done1 = gather_finish(st1, h1)
    st2, tok2 = gather_start(2, done1)
    hn_a = rms_fwd("l0mix_norm", h1, tied(mix_norm[0], tok2))
    qkv = mm_colw("l0_qkv", hn_a, col("a_w_qkv"), BF16).reshape(B, S, 3 * D)
    o_a = attn_a_fwd("l0_attn", qkv, bias).reshape(T, D)
    h2 = mm_roww("l0_attn_out", o_a, row("a_w_o"), F32, res=h1)
    h3, sv_f2a = ffn_fwd("l0f2", h2, ffn2_norm[0], col("ffn2_w_in", 0), row("ffn2_w_out", 0))

    hkv = rms_fwd("kv_norm", h3, kv_norm)
    ckr = mm_roww("kv_down", hkv, row("kv_w_down"), F32)
    ckv, kr = kvprep_fwd("kv_prep", ckr, kv_latent_norm, tabs, B, S)
    kvb = mm_colw("kv_up", ckv, col("kv_w_up"), BF16).reshape(B, S, HB * 256)
    gather_finish(st2, kvb)

    h4, sv_f1b = ffn_fwd("l1f1", h3, ffn1_norm[1], col("ffn1_w_in", 1), row("ffn1_w_out", 1))
    hn_b = rms_fwd("l1mix_norm", h4, mix_norm[1])
    cqp = mm_roww("l1_dq", hn_b, row("b_w_dq"), F32)
    cq = rms_fwd("l1_q_norm", cqp, b_q_norm[0])
    qpre = mm_colw("l1_uq", cq, col("b_w_uq"), F32)
    qf = qprep("l1_q_rope", qpre, tabs, B, S, bwd=False).reshape(B, S, HB * 256)
    o_b, lse = mla_fwd("l1_attn", qf, kvb, kr)
    h5 = mm_roww("l1_attn_out", o_b.reshape(T, HB * LANE), row("b_w_o"), F32, res=h4)
    h6, sv_f2b = ffn_fwd("l1f2", h5, ffn2_norm[1], col("ffn2_w_in", 1), row("ffn2_w_out", 1))

    dh, g_final, loss_part = loss_head("loss_head", h6, final_norm, loss_target.reshape(T, D))

    gw = {}
    gbufs = {nm: lax.empty(halves(w).shape, F32) for nm, w in sharded}

    def reduce_start(r, keys, after):
        dws = [halves(gw[k]) for k in keys]
        landed = pair_exchange(f"grad_pair_exchange_{r}", dws)
        parts = half_sum_group(f"half_sum_{r}", dws, landed, idx)
        lands = [lax.empty((3, *p.shape[1:]), p.dtype) for p in parts]
        ss, rs, parts, lands, token = ici_start(f"reduce_start_{r}", parts, lands, after, False)
        return (r, keys, ss, rs, parts, lands), token

    def reduce_finish(state, after):
        r, keys, ss, rs, parts, lands = state
        parts, lands = ici_wait(f"reduce_wait_{r}", ss, rs, parts, lands, after, False)
        done = chip_sum_group(f"chip_sum_{r}", parts, lands, [gbufs[nm] for nm, _ in keys], [l for _, l in keys], idx)
        gbufs.update(zip([nm for nm, _ in keys], done))
        return done[0]

    def ffn_bwd(tag, dh, h_in, g, w_in, w_out, saved, key_in, key_out, after=None):
        xn, u, act = saved
        du = ffn_dact(f"{tag}_dact", dh, w_out, u, after=after)
        dwo = mm_droww(f"{tag}_dwout", act, dh, alpha=0.5)
        gw[key_out] = dwo.reshape(N_CHIPS, dwo.shape[0] // N_CHIPS, dwo.shape[1])
        gw[key_in] = mm_dcolw(f"{tag}_dwin", xn, du, pair_layout=True)
        return dx_norm_bwd(f"{tag}_dxn", du, w_in, h_in, g, dres=dh, pair_layout=True)

    def chip_major(dw):
        return dw.reshape(N_CHIPS, dw.shape[0] // N_CHIPS, dw.shape[1])

    dh, g_f2b = ffn_bwd("l1f2b", dh, h5, ffn2_norm[1], col("ffn2_w_in", 1), row("ffn2_w_out", 1), sv_f2b,
                        ("ffn2_w_in", 1), ("ffn2_w_out", 1))
    red0, rtok0 = reduce_start(0, [("ffn2_w_in", 1), ("ffn2_w_out", 1)], dh)
    do_b = mm_roww_t("l1_attn_do", dh, row("b_w_o"), BF16, after=rtok0).reshape(B, S, HB * LANE)
    gw[("b_w_o", 0)] = chip_major(mm_droww("l1_attn_dwo", o_b.reshape(T, HB * LANE), dh))
    dqf, dkv, dkr = mla_bwd("l1_attn_bwd", qf, kvb, kr, do_b, o_b, lse)
    dqpre = qprep("l1_q_rope_bwd", dqf.reshape(T, HB * 256), tabs, B, S, bwd=True)
    gw[("b_w_uq", 0)] = mm_dcolw("l1_dwuq", cq, dqpre)
    dcqp, g_qn = dx_norm_bwd("l1_dcq", dqpre, col("b_w_uq"), cqp, b_q_norm[0])
    gw[("b_w_dq", 0)] = chip_major(mm_droww("l1_dwdq", hn_b, dcqp))
    dhn = mm_roww_t("l1_dhn", dcqp, row("b_w_dq"), F32)
    dh, g_mixb = rms_bwd("l1_dmix", h4, mix_norm[1], dhn, dres=dh)
    dh, g_f1b = ffn_bwd("l1f1b", dh, h3, ffn1_norm[1], col("ffn1_w_in", 1), row("ffn1_w_out", 1), sv_f1b,
                        ("ffn1_w_in", 1), ("ffn1_w_out", 1))
    fin0 = reduce_finish(red0, dh)
    red1, rtok1 = reduce_start(1, [("b_w_o", 0), ("b_w_uq", 0), ("b_w_dq", 0), ("ffn1_w_in", 1), ("ffn1_w_out", 1)], fin0)
    dkv2 = dkv.reshape(T, HB * 256)
    gw[("kv_w_up", 0)] = mm_dcolw("kv_dwup", ckv, dkv2, after=rtok1)
    dckv = mm_colw_t("kv_dckv", dkv2, col("kv_w_up"), F32, after=rtok1)
    dckr, g_lat = kvprep_bwd("kv_prep_bwd", ckr, kv_latent_norm, dckv, dkr, tabs, B, S)
    gw[("kv_w_down", 0)] = chip_major(mm_droww("kv_dwdown", hkv, dckr))
    dhkv = mm_roww_t("kv_dhkv", dckr, row("kv_w_down"), F32)
    dh, g_kvn = rms_bwd("kv_dnorm", h3, kv_norm, dhkv, dres=dh)
    dh, g_f2a = ffn_bwd("l0f2b", dh, h2, ffn2_norm[0], col("ffn2_w_in", 0), row("ffn2_w_out", 0), sv_f2a,
                        ("ffn2_w_in", 0), ("ffn2_w_out", 0))
    do_a = mm_roww_t("l0_attn_do", dh, row("a_w_o"), BF16).reshape(B, S, D)
    gw[("a_w_o", 0)] = chip_major(mm_droww("l0_attn_dwo", o_a, dh))
    dq_a, dkv_a, dbias = attn_a_bwd("l0_attn_bwd", qkv, do_a, bias)
    dqkv = jnp.concatenate([dq_a.reshape(T, D), dkv_a.reshape(T, 2 * D)], axis=1)
    gw[("a_w_qkv", 0)] = mm_dcolw("l0_dwqkv", hn_a, dqkv)
    dh, g_mixa = dx_norm_bwd("l0_dhn", dqkv, col("a_w_qkv"), h1, mix_norm[0], dres=dh)
    fin1 = reduce_finish(red1, dh)
    red2, rtok2 = reduce_start(2, [("kv_w_up", 0), ("kv_w_down", 0), ("ffn2_w_in", 0), ("ffn2_w_out", 0),
                                   ("a_w_o", 0), ("a_w_qkv", 0)], fin1)
    dh, g_f1a = ffn_bwd("l0f1b", dh, h0, ffn1_norm[0], col("ffn1_w_in", 0), row("ffn1_w_out", 0), sv_f1a,
                        ("ffn1_w_in", 0), ("ffn1_w_out", 0), after=rtok2)
    grad_x = dh.reshape(B, S, D)
    g_rel = rel_bias_grad("rel_bias_grad", dbias)[:, :2 * MAX_REL + 1][None]
    fin2 = reduce_finish(red2, dh)
    red3, rtok3 = reduce_start(3, [("ffn1_w_in", 0), ("ffn1_w_out", 0)], fin2)
    reduce_finish(red3, rtok3)

    full = [whole(g) for g in pair_assemble([gbufs[nm] for nm in names])]
    G = {nm: g for (nm, _), g in zip(sharded, full)}
    G["kv_w_down"] = G["kv_w_down"][0, :, :KVL + ROPE]
    G["kv_w_up"] = G["kv_w_up"][0]
    G["b_w_uq"] = G["b_w_uq"].reshape(1, QL, hpc, 256)[..., :NOPE + ROPE].reshape(b_w_uq.shape)

    small = [("ffn1_norm", jnp.stack([g_f1a, g_f1b])), ("mix_norm", jnp.stack([g_mixa, g_mixb])),
             ("ffn2_norm", jnp.stack([g_f2a, g_f2b])), ("a_rel_bias", g_rel), ("kv_norm", g_kvn),
             ("kv_latent_norm", g_lat), ("b_q_norm", g_qn[None]), ("final_norm", g_final)]
    red = all_reduce_small(_pack([loss_part] + [g for _, g in small]))
    unpacked = _unpack(red, [loss_part] + [g for _, g in small])
    loss = unpacked[0][0, 0]
    for (nm, _), g in zip(small, unpacked[1:]):
        G[nm] = g

    given = dict(ffn1_norm=(ffn1_norm, m_ffn1_norm, v_ffn1_norm), ffn1_w_in=(ffn1_w_in, m_ffn1_w_in, v_ffn1_w_in),
                 ffn1_w_out=(ffn1_w_out, m_ffn1_w_out, v_ffn1_w_out), mix_norm=(mix_norm, m_mix_norm, v_mix_norm),
                 ffn2_norm=(ffn2_norm, m_ffn2_norm, v_ffn2_norm), ffn2_w_in=(ffn2_w_in, m_ffn2_w_in, v_ffn2_w_in),
                 ffn2_w_out=(ffn2_w_out, m_ffn2_w_out, v_ffn2_w_out), a_w_qkv=(a_w_qkv, m_a_w_qkv, v_a_w_qkv),
                 a_rel_bias=(a_rel_bias, m_a_rel_bias, v_a_rel_bias), a_w_o=(a_w_o, m_a_w_o, v_a_w_o),
                 kv_norm=(kv_norm, m_kv_norm, v_kv_norm), kv_w_down=(kv_w_down, m_kv_w_down, v_kv_w_down),
                 kv_latent_norm=(kv_latent_norm, m_kv_latent_norm, v_kv_latent_norm),
                 kv_w_up=(kv_w_up, m_kv_w_up, v_kv_w_up), b_w_dq=(b_w_dq, m_b_w_dq, v_b_w_dq),
                 b_q_norm=(b_q_norm, m_b_q_norm, v_b_q_norm), b_w_uq=(b_w_uq, m_b_w_uq, v_b_w_uq),
                 b_w_o=(b_w_o, m_b_w_o, v_b_w_o), final_norm=(final_norm, m_final_norm, v_final_norm))
    order = list(given)
    delta, new_m, new_v = {}, {}, {}
    small_names = [nm for nm, _ in small]
    packed = [_pack([given[nm][k] for nm in small_names]) for k in range(3)]
    outs = adamw("adamw_small", packed[0], _pack([G[nm] for nm in small_names]), packed[1], packed[2])
    for dst, buf in zip((delta, new_m, new_v), outs):
        for nm, a in zip(small_names, _unpack(buf, [given[nm][0] for nm in small_names])):
            dst[nm] = a
    for nm, _ in sharded:
        w, m, v = given[nm]
        g = G[nm].reshape(w.shape)
        G[nm] = g
        two = lambda a: a.reshape(-1, a.shape[-1])
        d_, m_, v_ = adamw(f"adamw_{nm}", two(w), two(g), two(m), two(v))
        delta[nm], new_m[nm], new_v[nm] = d_.reshape(w.shape), m_.reshape(w.shape), v_.reshape(w.shape)

    return (loss, grad_x, *[G[n] for n in order], *[delta[n] for n in order],
            *[new_m[n] for n in order], *[new_v[n] for n in order])
```

```python
import functools
import math

import jax
import jax.numpy as jnp
from jax import lax
from jax.experimental import pallas as pl
from jax.experimental.pallas import tpu as pltpu

F32 = jnp.float32
BF16 = jnp.bfloat16
I32 = jnp.int32

CHUNK = 64
CHUNK_SHIFT = 6
HEAD_DIM_A = 64
LEFT_CHUNKS = 8
MAX_REL = 128
REL_PAD = 384
QROWS = 2 * CHUNK
WIN = (LEFT_CHUNKS + 2) * CHUNK
PADR = LEFT_CHUNKS * CHUNK
NOPE = 128
ROPE = 64
EPS = 1e-6
NEG_INF = -1e30
ROPE_THETA = 10000.0
ADAM_LR, ADAM_B1, ADAM_B2, ADAM_EPS, ADAM_WD, ADAM_STEP = 0.001, 0.9, 0.999, 1e-08, 0.01, 10
N_CHIPS = 4
LANE = 128
MESH = pl.DeviceIdType.MESH
VMEM_CAP_MB = 60

NN = (((1,), (0,)), ((), ()))
NT = (((1,), (1,)), ((), ()))
TN = (((0,), (0,)), ((), ()))


def _tile(n, pref, mult):
    t = (min(pref, n) // mult) * mult
    while t >= mult:
        if n % t == 0:
            return t
        t -= mult
    return n


def _nbytes(shape, dtype):
    return math.prod(shape) * jnp.dtype(dtype).itemsize


def _small(a):
    return pltpu.with_memory_space_constraint(a, pltpu.HBM)


def _params(block_bytes, extra_bytes=0):
    need = 2 * block_bytes + extra_bytes
    mb = min(VMEM_CAP_MB, max(32, int(need * 1.25 / 2**20) + 8))
    return pltpu.CompilerParams(vmem_limit_bytes=mb * 2**20)


def _mm(name, kind, a, b, grid, a_spec, b_spec, o_spec, out_shape, out_dtype, blocks,
        red_axis=None, nred=1, alpha=1.0, res=None, res_spec=None, after=None):
    dims = {"nn": NN, "nt": NT, "tn": TN}[kind]
    has_res = res is not None
    acc_in_out = nred > 1 and out_dtype == F32 and not has_res and alpha == 1.0
    n_in = 2 + has_res + (after is not None)

    def body(*refs):
        a_ref, b_ref = refs[0], refs[1]
        r_ref = refs[2] if has_res else None
        o_ref = refs[n_in]
        p = lax.dot_general(a_ref[...].astype(BF16), b_ref[...].astype(BF16), dims,
                            preferred_element_type=F32)

        def finish(acc):
            y = acc if alpha == 1.0 else acc * alpha
            if has_res:
                y = r_ref[...] + y
            o_ref[...] = y.astype(o_ref.dtype)

        if nred == 1:
            finish(p)
            return
        k = pl.program_id(red_axis)
        tgt = o_ref if acc_in_out else refs[-1]

        @pl.when(k == 0)
        def _():
            tgt[...] = p

        @pl.when(k > 0)
        def _():
            tgt[...] += p

        if not acc_in_out:
            @pl.when(k == nred - 1)
            def _():
                finish(tgt[...])

    a_blk, b_blk, o_blk = blocks
    scratch = []
    extra = 0
    if nred > 1 and not acc_in_out:
        scratch = [pltpu.VMEM(o_blk, F32)]
        extra = _nbytes(o_blk, F32)
    blk = _nbytes(a_blk, a.dtype) + _nbytes(b_blk, b.dtype) + _nbytes(o_blk, out_dtype)
    ins, specs = [a, b], [a_spec, b_spec]
    if has_res:
        ins.append(res)
        specs.append(res_spec)
        blk += _nbytes(o_blk, res.dtype)
    if after is not None:
        ins.append(after)
        specs.append(pl.BlockSpec(memory_space=pl.ANY))
    extra += _nbytes(a_blk, BF16) + _nbytes(b_blk, BF16) + 2 * _nbytes(o_blk, F32)
    return pl.pallas_call(
        body, name=name, grid=grid, in_specs=specs, out_specs=o_spec,
        out_shape=jax.ShapeDtypeStruct(out_shape, out_dtype), scratch_shapes=scratch,
        compiler_params=_params(blk, extra),
    )(*ins)


def mm_colw(name, x, w3, out_dtype):
    T, K = x.shape
    _, _, nl = w3.shape
    tm = _tile(T, 512, 8)
    return _mm(name, "nn", x, w3, (N_CHIPS, T // tm),
               pl.BlockSpec((tm, K), lambda j, i: (i, 0)),
               pl.BlockSpec((None, K, nl), lambda j, i: (j, 0, 0)),
               pl.BlockSpec((tm, nl), lambda j, i: (i, j)),
               (T, N_CHIPS * nl), out_dtype, ((tm, K), (K, nl), (tm, nl)))


def _pair_chip(j):
    return (j % 2) * 2 + j // 2


def mm_colw_t(name, dy, w3, out_dtype, res=None, after=None, pair_layout=False):
    T = dy.shape[0]
    _, K, nl = w3.shape
    tm = _tile(T, 1024, 8)
    chip = _pair_chip if pair_layout else (lambda j: j)
    return _mm(name, "nt", dy, w3, (T // tm, N_CHIPS),
               pl.BlockSpec((tm, nl), lambda i, j: (i, j)),
               pl.BlockSpec((None, K, nl), lambda i, j: (chip(j), 0, 0)),
               pl.BlockSpec((tm, K), lambda i, j: (i, 0)),
               (T, K), out_dtype, ((tm, nl), (K, nl), (tm, K)),
               red_axis=1, nred=N_CHIPS, res=res,
               res_spec=pl.BlockSpec((tm, K), lambda i, j: (i, 0)), after=after)


def mm_dcolw(name, x, dy, after=None, pair_layout=False):
    T, K = x.shape
    nl = dy.shape[1] // N_CHIPS
    tt = _tile(T, 2048, 8)
    chip = _pair_chip if pair_layout else (lambda j: j)
    return _mm(name, "tn", x, dy, (N_CHIPS, T // tt),
               pl.BlockSpec((tt, K), lambda j, t: (t, 0)),
               pl.BlockSpec((tt, nl), lambda j, t: (t, j)),
               pl.BlockSpec((None, K, nl), lambda j, t: (chip(j), 0, 0)),
               (N_CHIPS, K, nl), BF16, ((tt, K), (tt, nl), (K, nl)),
               red_axis=1, nred=T // tt, after=after)


def mm_roww(name, x, w2, out_dtype, res=None, alpha=1.0):
    T, Kt = x.shape
    N = w2.shape[1]
    tm = _tile(T, 512, 8)
    return _mm(name, "nn", x, w2, (T // tm,),
               pl.BlockSpec((tm, Kt), lambda i: (i, 0)),
               pl.BlockSpec((Kt, N), lambda i: (0, 0)),
               pl.BlockSpec((tm, N), lambda i: (i, 0)),
               (T, N), out_dtype, ((tm, Kt), (Kt, N), (tm, N)),
               alpha=alpha, res=res, res_spec=pl.BlockSpec((tm, N), lambda i: (i, 0)))


def mm_roww_t(name, dy, w2, out_dtype, alpha=1.0, after=None):
    T, N = dy.shape
    Kt = w2.shape[0]
    tm = _tile(T, 512, 8)
    tk = _tile(Kt, 1408, LANE)
    return _mm(name, "nt", dy, w2, (Kt // tk, T // tm),
               pl.BlockSpec((tm, N), lambda j, i: (i, 0)),
               pl.BlockSpec((tk, N), lambda j, i: (j, 0)),
               pl.BlockSpec((tm, tk), lambda j, i: (i, j)),
               (T, Kt), out_dtype, ((tm, N), (tk, N), (tm, tk)), alpha=alpha, after=after)


def mm_droww(name, x, dy, alpha=1.0):
    T, Kt = x.shape
    N = dy.shape[1]
    tt = _tile(T, 2048, 8)
    tk = _tile(Kt, 1408, LANE)
    return _mm(name, "tn", x, dy, (Kt // tk, T // tt),
               pl.BlockSpec((tt, tk), lambda j, t: (t, j)),
               pl.BlockSpec((tt, N), lambda j, t: (t, 0)),
               pl.BlockSpec((tk, N), lambda j, t: (j, 0)),
               (Kt, N), BF16, ((tt, tk), (tt, N), (tk, N)),
               red_axis=1, nred=T // tt, alpha=alpha)


def rms_fwd(name, x, g):
    T, D = x.shape
    tm = _tile(T, 512, 8)

    def body(x_ref, g_ref, o_ref):
        xv = x_ref[...]
        r = lax.rsqrt(jnp.mean(xv * xv, axis=-1, keepdims=True) + EPS)
        o_ref[...] = (xv * r * g_ref[...]).astype(o_ref.dtype)

    return pl.pallas_call(
        body, name=name, grid=(T // tm,),
        in_specs=[pl.BlockSpec((tm, D), lambda i: (i, 0)), pl.BlockSpec((1, D), lambda i: (0, 0))],
        out_specs=pl.BlockSpec((tm, D), lambda i: (i, 0)),
        out_shape=pltpu.HBM((T, D), BF16),
        compiler_params=_params(_nbytes((tm, D), F32) * 2, 4 * _nbytes((tm, D), F32)),
    )(x, _small(g.reshape(1, D)))


def _rms_bwd_math(xv, gv, dy):
    r = lax.rsqrt(jnp.mean(xv * xv, axis=-1, keepdims=True) + EPS)
    xh = xv * r
    dyg = dy * gv
    dx = r * (dyg - xh * jnp.mean(dyg * xh, axis=-1, keepdims=True))
    dg = jnp.sum(dy * xh, axis=0, keepdims=True)
    return dx, dg


def rms_bwd(name, x, g, dy, dres=None):
    T, D = x.shape
    tm = _tile(T, 256, 8)
    has_res = dres is not None

    def body(*refs):
        x_ref, g_ref, dy_ref = refs[:3]
        r_ref = refs[3] if has_res else None
        dx_ref, dg_ref = refs[-2:]
        dx, dg = _rms_bwd_math(x_ref[...], g_ref[...], dy_ref[...].astype(F32))
        if has_res:
            dx = r_ref[...] + dx
        dx_ref[...] = dx

        @pl.when(pl.program_id(0) == 0)
        def _():
            dg_ref[...] = dg

        @pl.when(pl.program_id(0) > 0)
        def _():
            dg_ref[...] += dg

    row = pl.BlockSpec((tm, D), lambda i: (i, 0))
    vec = pl.BlockSpec((1, D), lambda i: (0, 0))
    ins, specs = [x, _small(g.reshape(1, D)), dy], [row, vec, row]
    if has_res:
        ins.append(dres)
        specs.append(row)
    dx, dg = pl.pallas_call(
        body, name=name, grid=(T // tm,), in_specs=specs, out_specs=[row, vec],
        out_shape=[jax.ShapeDtypeStruct((T, D), F32), pltpu.HBM((1, D), F32)],
        compiler_params=_params(_nbytes((tm, D), F32) * 4, 6 * _nbytes((tm, D), F32)),
    )(*ins)
    return dx, dg.reshape(D)


def dx_norm_bwd(name, dy, w3, x, g, dres=None, pair_layout=False, after=None):
    T = dy.shape[0]
    _, K, nl = w3.shape
    tm = _tile(T, 512, 8)
    chip = _pair_chip if pair_layout else (lambda j: j)
    has_res = dres is not None

    def body(*refs):
        dy_ref, w_ref, x_ref, g_ref = refs[:4]
        r_ref = refs[4] if has_res else None
        dx_ref, dg_ref, acc = refs[-3:]
        i, k = pl.program_id(0), pl.program_id(1)
        p = lax.dot_general(dy_ref[...].astype(BF16), w_ref[...], NT, preferred_element_type=F32)

        @pl.when(k == 0)
        def _():
            acc[...] = p

        @pl.when(k > 0)
        def _():
            acc[...] += p

        @pl.when(k == N_CHIPS - 1)
        def _():
            dx, dg = _rms_bwd_math(x_ref[...], g_ref[...], acc[...])
            dx_ref[...] = r_ref[...] + dx if has_res else dx

            @pl.when(i == 0)
            def _():
                dg_ref[...] = dg

            @pl.when(i > 0)
            def _():
                dg_ref[...] += dg

    row = pl.BlockSpec((tm, K), lambda i, j: (i, 0))
    vec = pl.BlockSpec((1, K), lambda i, j: (0, 0))
    ins = [dy, w3, x, _small(g.reshape(1, K))]
    specs = [pl.BlockSpec((tm, nl), lambda i, j: (i, j)),
             pl.BlockSpec((None, K, nl), lambda i, j: (chip(j), 0, 0)), row, vec]
    if has_res:
        ins.append(dres)
        specs.append(row)
    if after is not None:
        ins.append(after)
        specs.append(pl.BlockSpec(memory_space=pl.ANY))
    blk = _nbytes((tm, nl), dy.dtype) + _nbytes((K, nl), BF16) + (2 + has_res) * _nbytes((tm, K), F32)
    dx, dg = pl.pallas_call(
        body, name=name, grid=(T // tm, N_CHIPS), in_specs=specs, out_specs=[row, vec],
        out_shape=[jax.ShapeDtypeStruct((T, K), F32), pltpu.HBM((1, K), F32)],
        scratch_shapes=[pltpu.VMEM((tm, K), F32)],
        compiler_params=_params(blk, 8 * _nbytes((tm, K), F32)),
    )(*ins)
    return dx, dg.reshape(K)


def ffn_in_act(name, x, w3):
    T, K = x.shape
    _, _, nl = w3.shape
    tm = _tile(T, 512, 8)

    def body(x_ref, wg_ref, wu_ref, u_ref, a_ref):
        xv = x_ref[...]
        g = jnp.dot(xv, wg_ref[...], preferred_element_type=F32)
        up = jnp.dot(xv, wu_ref[...], preferred_element_type=F32)
        u_ref[:, :nl] = g.astype(u_ref.dtype)
        u_ref[:, nl:] = up.astype(u_ref.dtype)
        a_ref[...] = (g * jax.nn.sigmoid(g) * up).astype(a_ref.dtype)

    blk = _nbytes((tm, K), BF16) + 2 * _nbytes((K, nl), BF16) + _nbytes((tm, 3 * nl), BF16)
    return pl.pallas_call(
        body, name=name, grid=(2, T // tm),
        in_specs=[pl.BlockSpec((tm, K), lambda p, i: (i, 0)),
                  pl.BlockSpec((None, K, nl), lambda p, i: (p, 0, 0)),
                  pl.BlockSpec((None, K, nl), lambda p, i: (p + 2, 0, 0))],
        out_specs=[pl.BlockSpec((tm, 2 * nl), lambda p, i: (i, p)), pl.BlockSpec((tm, nl), lambda p, i: (i, p))],
        out_shape=[jax.ShapeDtypeStruct((T, 4 * nl), BF16), jax.ShapeDtypeStruct((T, 2 * nl), BF16)],
        compiler_params=_params(blk, 4 * _nbytes((tm, nl), F32)),
    )(x, w3, w3)


def ffn_dact(name, dh, w_out, u, after=None):
    T, N = dh.shape
    F = w_out.shape[0]
    nl = F // 2
    tm = _tile(T, 512, 8)

    def body(*refs):
        d_ref, w_ref, u_ref = refs[:3]
        o_ref = refs[-1]
        dact = 0.5 * lax.dot_general(d_ref[...].astype(BF16), w_ref[...], NT, preferred_element_type=F32)
        g = u_ref[:, :nl].astype(F32)
        up = u_ref[:, nl:].astype(F32)
        sig = jax.nn.sigmoid(g)
        o_ref[:, :nl] = (dact * up * (sig * (1.0 + g * (1.0 - sig)))).astype(o_ref.dtype)
        o_ref[:, nl:] = (dact * (g * sig)).astype(o_ref.dtype)

    ins = [dh, w_out, u]
    specs = [pl.BlockSpec((tm, N), lambda p, i: (i, 0)), pl.BlockSpec((nl, N), lambda p, i: (p, 0)),
             pl.BlockSpec((tm, 2 * nl), lambda p, i: (i, p))]
    if after is not None:
        ins.append(after)
        specs.append(pl.BlockSpec(memory_space=pl.ANY))
    blk = _nbytes((tm, N), F32) + _nbytes((nl, N), BF16) + 2 * _nbytes((tm, 2 * nl), BF16)
    return pl.pallas_call(
        body, name=name, grid=(2, T // tm), in_specs=specs,
        out_specs=pl.BlockSpec((tm, 2 * nl), lambda p, i: (i, p)),
        out_shape=jax.ShapeDtypeStruct((T, 2 * F), BF16),
        compiler_params=_params(blk, 6 * _nbytes((tm, nl), F32)),
    )(*ins)


def loss_head(name, h, g, target):
    T, D = h.shape
    tm = _tile(T, 256, 8)

    def body(h_ref, g_ref, t_ref, dh_ref, dg_ref, loss_ref):
        xv = h_ref[...]
        gv = g_ref[...]
        r = lax.rsqrt(jnp.mean(xv * xv, axis=-1, keepdims=True) + EPS)
        err = xv * r * gv - t_ref[...]
        part = 0.5 * jnp.sum(jnp.mean(err * err, axis=-1, keepdims=True), axis=0, keepdims=True)
        dx, dg = _rms_bwd_math(xv, gv, err * (1.0 / D))
        dh_ref[...] = dx
        part = jnp.broadcast_to(part, (1, LANE))

        @pl.when(pl.program_id(0) == 0)
        def _():
            dg_ref[...] = dg
            loss_ref[...] = part

        @pl.when(pl.program_id(0) > 0)
        def _():
            dg_ref[...] += dg
            loss_ref[...] += part

    row = pl.BlockSpec((tm, D), lambda i: (i, 0))
    vec = pl.BlockSpec((1, D), lambda i: (0, 0))
    dh, dg, loss = pl.pallas_call(
        body, name=name, grid=(T // tm,), in_specs=[row, vec, row],
        out_specs=[row, vec, pl.BlockSpec((1, LANE), lambda i: (0, 0))],
        out_shape=[jax.ShapeDtypeStruct((T, D), F32), pltpu.HBM((1, D), F32), pltpu.HBM((1, LANE), F32)],
        compiler_params=_params(_nbytes((tm, D), F32) * 3, 6 * _nbytes((tm, D), F32)),
    )(h, _small(g.reshape(1, D)), target)
    return dh, dg.reshape(D), loss


def rope_tables(S):
    half = ROPE // 2
    freqs = ROPE_THETA ** (-jnp.arange(half, dtype=F32) / half)
    ang = jnp.arange(S, dtype=F32)[:, None] * freqs[None, :]
    cos, sin = jnp.cos(ang), jnp.sin(ang)
    z = jnp.zeros_like(cos)
    ct = jnp.concatenate([cos, cos, z, z], axis=1)
    s1 = jnp.concatenate([-sin, z, z, z], axis=1)
    s2 = jnp.concatenate([z, sin, z, z], axis=1)
    return ct, s1, s2


def _rope_tile(t, ct, s1, s2):
    return t * ct + pltpu.roll(t, 96, 1) * s1 + pltpu.roll(t, 32, 1) * s2


def _rope_tile_bwd(d, ct, s1, s2):
    return d * ct + pltpu.roll(d * s1, 32, 1) + pltpu.roll(d * s2, 96, 1)


def qprep(name, q, tabs, B, S, bwd):
    T, W = q.shape
    nh = W // 256
    ts = _tile(S, 256, 8)
    fn = _rope_tile_bwd if bwd else _rope_tile

    def body(q_ref, ct_ref, s1_ref, s2_ref, o_ref):
        ct, s1, s2 = ct_ref[...], s1_ref[...], s2_ref[...]
        for h in range(nh):
            o_ref[0, :, 256 * h:256 * h + 128] = q_ref[0, :, 256 * h:256 * h + 128].astype(o_ref.dtype)
            t = q_ref[0, :, 256 * h + 128:256 * h + 256].astype(F32)
            o_ref[0, :, 256 * h + 128:256 * h + 256] = fn(t, ct, s1, s2).astype(o_ref.dtype)

    row = pl.BlockSpec((1, ts, W), lambda b, s: (b, s, 0))
    tab = pl.BlockSpec((ts, LANE), lambda b, s: (s, 0))
    out = pl.pallas_call(
        body, name=name, grid=(B, S // ts), in_specs=[row, tab, tab, tab], out_specs=row,
        out_shape=jax.ShapeDtypeStruct((B, S, W), BF16),
        compiler_params=_params(_nbytes((ts, W), F32) * 2, _nbytes((ts, W), F32) * 2),
    )(q.reshape(B, S, W), *[_small(t) for t in tabs])
    return out.reshape(T, W)


def kvprep_fwd(name, ckr, g, tabs, B, S):
    T, W = ckr.shape
    KVL = W - LANE
    ts = _tile(S, 256, 8)

    def body(x_ref, g_ref, ct_ref, s1_ref, s2_ref, c_ref, k_ref):
        xv = x_ref[0, :, :KVL]
        r = lax.rsqrt(jnp.mean(xv * xv, axis=-1, keepdims=True) + EPS)
        c_ref[0] = (xv * r * g_ref[...]).astype(c_ref.dtype)
        k_ref[0] = _rope_tile(x_ref[0, :, KVL:], ct_ref[...], s1_ref[...], s2_ref[...]).astype(k_ref.dtype)

    tab = pl.BlockSpec((ts, LANE), lambda b, s: (s, 0))
    c, k = pl.pallas_call(
        body, name=name, grid=(B, S // ts),
        in_specs=[pl.BlockSpec((1, ts, W), lambda b, s: (b, s, 0)), pl.BlockSpec((1, KVL), lambda b, s: (0, 0)),
                  tab, tab, tab],
        out_specs=[pl.BlockSpec((1, ts, KVL), lambda b, s: (b, s, 0)),
                   pl.BlockSpec((1, ts, LANE), lambda b, s: (b, s, 0))],
        out_shape=[jax.ShapeDtypeStruct((B, S, KVL), BF16), jax.ShapeDtypeStruct((B, S, LANE), BF16)],
        compiler_params=_params(_nbytes((ts, W), F32) * 2, _nbytes((ts, W), F32) * 2),
    )(ckr.reshape(B, S, W), _small(g.reshape(1, KVL)), *[_small(t) for t in tabs])
    return c.reshape(T, KVL), k


def kvprep_bwd(name, ckr, g, dc, dkr, tabs, B, S):
    T, W = ckr.shape
    KVL = W - LANE
    ts = _tile(S, 256, 8)

    def body(x_ref, g_ref, dc_ref, dk_ref, ct_ref, s1_ref, s2_ref, o_ref, dg_ref):
        dx, dg = _rms_bwd_math(x_ref[0, :, :KVL], g_ref[...], dc_ref[0])
        o_ref[0, :, :KVL] = dx
        o_ref[0, :, KVL:] = _rope_tile_bwd(dk_ref[0], ct_ref[...], s1_ref[...], s2_ref[...])
        first = (pl.program_id(0) == 0) & (pl.program_id(1) == 0)

        @pl.when(first)
        def _():
            dg_ref[...] = dg

        @pl.when(jnp.logical_not(first))
        def _():
            dg_ref[...] += dg

    tab = pl.BlockSpec((ts, LANE), lambda b, s: (s, 0))
    vec = pl.BlockSpec((1, KVL), lambda b, s: (0, 0))
    o, dg = pl.pallas_call(
        body, name=name, grid=(B, S // ts),
        in_specs=[pl.BlockSpec((1, ts, W), lambda b, s: (b, s, 0)), vec,
                  pl.BlockSpec((1, ts, KVL), lambda b, s: (b, s, 0)),
                  pl.BlockSpec((1, ts, LANE), lambda b, s: (b, s, 0)), tab, tab, tab],
        out_specs=[pl.BlockSpec((1, ts, W), lambda b, s: (b, s, 0)), vec],
        out_shape=[jax.ShapeDtypeStruct((B, S, W), F32), pltpu.HBM((1, KVL), F32)],
        compiler_params=_params(_nbytes((ts, W), F32) * 4, _nbytes((ts, W), F32) * 4),
    )(ckr.reshape(B, S, W), _small(g.reshape(1, KVL)), dc.reshape(B, S, KVL), dkr, *[_small(t) for t in tabs])
    return o.reshape(T, W), dg.reshape(KVL)


DIAGS = 768


def _diag_onehot():
    col = lax.broadcasted_iota(I32, (REL_PAD, DIAGS), 1)
    row = lax.broadcasted_iota(I32, (REL_PAD, DIAGS), 0)
    idx = jnp.clip(PADR + QROWS - 1 - col, -MAX_REL, MAX_REL) + MAX_REL
    return (row == idx).astype(F32)


def rel_bias_tile(name, table):
    H = table.shape[0]
    tpad = jnp.pad(table, ((0, 0), (0, REL_PAD - table.shape[1])))

    def body(t_ref, o_ref):
        g = lax.dot_general(t_ref[...], _diag_onehot(), NN, precision=lax.Precision.HIGHEST,
                            preferred_element_type=F32)
        qc = jnp.right_shift(lax.broadcasted_iota(I32, (QROWS, WIN), 0), CHUNK_SHIFT)
        kc = jnp.right_shift(lax.broadcasted_iota(I32, (QROWS, WIN), 1), CHUNK_SHIFT)
        band = (kc >= qc) & (kc <= qc + LEFT_CHUNKS)
        for h in range(H):
            gb = jnp.broadcast_to(g[h:h + 1, :], (QROWS, DIAGS))
            tile = pltpu.roll(gb, DIAGS - (QROWS - 1), 1, stride=1, stride_axis=0)
            o_ref[h // 2, (h % 2) * QROWS:(h % 2 + 1) * QROWS, :] = jnp.where(band, tile[:, :WIN], NEG_INF)

    return pl.pallas_call(
        body, name=name, out_shape=jax.ShapeDtypeStruct((H // 2, 2 * QROWS, WIN), F32),
        compiler_params=_params(0, 2 * _nbytes((H // 2, 2 * QROWS, WIN), F32)),
    )(tpad)


def rel_bias_grad(name, dbias):
    H = 2 * dbias.shape[0]

    def body(d_ref, o_ref):
        flip = (lax.broadcasted_iota(I32, (QROWS, QROWS), 0) + lax.broadcasted_iota(I32, (QROWS, QROWS), 1)
                == QROWS - 1).astype(F32)
        rows = []
        for h in range(H):
            x = d_ref[h // 2, (h % 2) * QROWS:(h % 2 + 1) * QROWS, :]
            xr = lax.dot_general(flip, x, NN, precision=lax.Precision.HIGHEST, preferred_element_type=F32)
            xp = jnp.concatenate([xr, jnp.zeros((QROWS, DIAGS - WIN), F32)], axis=1)
            y = pltpu.roll(xp, 0, 1, stride=1, stride_axis=0)
            rows.append(jnp.sum(y, axis=0, keepdims=True))
        o_ref[...] = lax.dot_general(jnp.concatenate(rows, axis=0), _diag_onehot(), NT,
                                     precision=lax.Precision.HIGHEST, preferred_element_type=F32)

    return pl.pallas_call(
        body, name=name, out_shape=jax.ShapeDtypeStruct((H, REL_PAD), F32),
        compiler_params=_params(0, 2 * _nbytes(dbias.shape, F32)),
    )(dbias)


def _stack_pair(xp):
    lane = lax.broadcasted_iota(I32, xp.shape, 1)
    z = jnp.zeros_like(xp)
    return jnp.concatenate([jnp.where(lane < HEAD_DIM_A, xp, z), jnp.where(lane >= HEAD_DIM_A, xp, z)], axis=0)


def _unstack_pair(y):
    lane = lax.broadcasted_iota(I32, (QROWS, LANE), 1)
    return jnp.where(lane < HEAD_DIM_A, y[:QROWS], y[QROWS:])


def _attn_a_rowpen(j):
    w = lax.broadcasted_iota(I32, (1, WIN), 1)
    return jnp.where(w >= PADR - QROWS * j, 0.0, NEG_INF).astype(F32)


def _attn_a_load_bias(bias_hbm, bias_v, sem):
    cp = pltpu.make_async_copy(bias_hbm, bias_v, sem)
    cp.start()
    cp.wait()


def _attn_a_load_kv(qkv_hbm, b, kpad, vpad, sem, S, D):
    kpad[0:PADR, :] = jnp.zeros((PADR, D), BF16)
    vpad[0:PADR, :] = jnp.zeros((PADR, D), BF16)
    ck = pltpu.make_async_copy(qkv_hbm.at[b, :, pl.ds(D, D)], kpad.at[pl.ds(PADR, S), :], sem.at[0])
    cv = pltpu.make_async_copy(qkv_hbm.at[b, :, pl.ds(2 * D, D)], vpad.at[pl.ds(PADR, S), :], sem.at[1])
    ck.start()
    cv.start()
    ck.wait()
    cv.wait()


def _attn_a_exp(q2s, kp, bias, pen):
    s = lax.dot_general(q2s, kp, NT, preferred_element_type=F32) + bias + pen
    e = jnp.exp(s - jnp.max(s, axis=-1, keepdims=True))
    return e, 1.0 / jnp.sum(e, axis=-1, keepdims=True)


def attn_a_fwd(name, qkv, bias):
    B, S, D3 = qkv.shape
    D = D3 // 3
    H = D // HEAD_DIM_A
    nb = S // QROWS
    scale = HEAD_DIM_A ** -0.5

    def body(q_ref, bias_hbm, qkv_hbm, o_ref, kpad, vpad, bias_v, sem):
        b, j = pl.program_id(0), pl.program_id(1)

        @pl.when((b == 0) & (j == 0))
        def _():
            _attn_a_load_bias(bias_hbm, bias_v, sem.at[2])

        @pl.when(j == 0)
        def _():
            _attn_a_load_kv(qkv_hbm, b, kpad, vpad, sem, S, D)

        pen = _attn_a_rowpen(j)
        w0 = pl.multiple_of(j * QROWS, QROWS)
        for p in range(H // 2):
            ls = slice(p * LANE, (p + 1) * LANE)
            e, rl = _attn_a_exp(_stack_pair(q_ref[0, :, ls] * scale), kpad[pl.ds(w0, WIN), ls], bias_v[p], pen)
            o2 = jnp.dot(e.astype(BF16), vpad[pl.ds(w0, WIN), ls], preferred_element_type=F32) * rl
            o_ref[0, :, ls] = _unstack_pair(o2).astype(o_ref.dtype)

    scr = 2 * _nbytes((PADR + S, D), BF16) + _nbytes(bias.shape, F32) + 8 * _nbytes((2 * QROWS, WIN), F32)
    return pl.pallas_call(
        body, name=name, grid=(B, nb),
        in_specs=[pl.BlockSpec((1, QROWS, D), lambda b, j: (b, j, 0)),
                  pl.BlockSpec(memory_space=pl.ANY), pl.BlockSpec(memory_space=pl.ANY)],
        out_specs=pl.BlockSpec((1, QROWS, D), lambda b, j: (b, j, 0)),
        out_shape=jax.ShapeDtypeStruct((B, S, D), BF16),
        scratch_shapes=[pltpu.VMEM((PADR + S, D), BF16), pltpu.VMEM((PADR + S, D), BF16),
                        pltpu.VMEM(bias.shape, F32), pltpu.SemaphoreType.DMA((3,))],
        compiler_params=_params(2 * _nbytes((QROWS, D), BF16), scr),
    )(qkv, bias, qkv)


def attn_a_bwd(name, qkv, do, bias):
    B, S, D3 = qkv.shape
    D = D3 // 3
    H = D // HEAD_DIM_A
    nb = S // QROWS
    scale = HEAD_DIM_A ** -0.5

    def body(q_ref, do_ref, bias_hbm, qkv_hbm, dq_ref, dkv_hbm, dbias_hbm, kpad, vpad, dkacc, dvacc, bias_v, dbias_v, sem):
        b, j = pl.program_id(0), pl.program_id(1)

        @pl.when((b == 0) & (j == 0))
        def _():
            _attn_a_load_bias(bias_hbm, bias_v, sem.at[2])
            dbias_v[...] = jnp.zeros_like(dbias_v)

        @pl.when(j == 0)
        def _():
            _attn_a_load_kv(qkv_hbm, b, kpad, vpad, sem, S, D)
            dkacc[...] = jnp.zeros_like(dkacc)
            dvacc[...] = jnp.zeros_like(dvacc)

        pen = _attn_a_rowpen(j)
        w0 = pl.multiple_of(j * QROWS, QROWS)
        for p in range(H // 2):
            ls = slice(p * LANE, (p + 1) * LANE)
            q2s = _stack_pair(q_ref[0, :, ls] * scale)
            do2 = _stack_pair(do_ref[0, :, ls])
            kp = kpad[pl.ds(w0, WIN), ls]
            vp = vpad[pl.ds(w0, WIN), ls]
            e, rl = _attn_a_exp(q2s, kp, bias_v[p], pen)
            pr = e * rl
            dp = lax.dot_general(do2, vp, NT, preferred_element_type=F32)
            ds = pr * (dp - jnp.sum(pr * dp, axis=-1, keepdims=True))
            dbias_v[p] += ds
            dsb = ds.astype(BF16)
            dq_ref[0, :, ls] = _unstack_pair(jnp.dot(dsb, kp, preferred_element_type=F32)) * scale
            dkacc[pl.ds(w0, WIN), ls] += lax.dot_general(dsb, q2s, TN, preferred_element_type=F32)
            dvacc[pl.ds(w0, WIN), ls] += lax.dot_general(pr.astype(BF16), do2, TN, preferred_element_type=F32)

        @pl.when(j == nb - 1)
        def _():
            ck = pltpu.make_async_copy(dkacc.at[pl.ds(PADR, S), :], dkv_hbm.at[b, :, pl.ds(0, D)], sem.at[0])
            cv = pltpu.make_async_copy(dvacc.at[pl.ds(PADR, S), :], dkv_hbm.at[b, :, pl.ds(D, D)], sem.at[1])
            ck.start()
            cv.start()
            ck.wait()
            cv.wait()

        @pl.when((b == B - 1) & (j == nb - 1))
        def _():
            cb = pltpu.make_async_copy(dbias_v, dbias_hbm, sem.at[2])
            cb.start()
            cb.wait()

    blk = _nbytes((QROWS, D), BF16) * 2 + _nbytes((QROWS, D), F32)
    scr = (2 * _nbytes((PADR + S, D), BF16) + 2 * _nbytes((PADR + S, D), F32) + 2 * _nbytes(bias.shape, F32)
           + 8 * _nbytes((2 * QROWS, WIN), F32))
    return pl.pallas_call(
        body, name=name, grid=(B, nb),
        in_specs=[pl.BlockSpec((1, QROWS, D), lambda b, j: (b, j, 0)),
                  pl.BlockSpec((1, QROWS, D), lambda b, j: (b, j, 0)),
                  pl.BlockSpec(memory_space=pl.ANY), pl.BlockSpec(memory_space=pl.ANY)],
        out_specs=[pl.BlockSpec((1, QROWS, D), lambda b, j: (b, j, 0)),
                   pl.BlockSpec(memory_space=pl.ANY), pl.BlockSpec(memory_space=pl.ANY)],
        out_shape=[jax.ShapeDtypeStruct((B, S, D), F32), jax.ShapeDtypeStruct((B, S, 2 * D), F32),
                   jax.ShapeDtypeStruct(bias.shape, F32)],
        scratch_shapes=[pltpu.VMEM((PADR + S, D), BF16), pltpu.VMEM((PADR + S, D), BF16),
                        pltpu.VMEM((PADR + S, D), F32), pltpu.VMEM((PADR + S, D), F32),
                        pltpu.VMEM(bias.shape, F32), pltpu.VMEM(bias.shape, F32),
                        pltpu.SemaphoreType.DMA((3,))],
        compiler_params=_params(blk, scr),
    )(qkv, do, bias, qkv)


def _mla_raw_t(k2, kj, q, QB):
    return lax.dot_general(k2[_blk(kj, QB), :], q, NT, preferred_element_type=F32)


def _blk(kj, QB):
    return pl.ds(kj * QB, QB) if isinstance(kj, int) else pl.ds(pl.multiple_of(kj * QB, QB), QB)


def _mla_diag_pen(QB):
    kc = jnp.right_shift(lax.broadcasted_iota(I32, (QB, QB), 0), CHUNK_SHIFT)
    qc = jnp.right_shift(lax.broadcasted_iota(I32, (QB, QB), 1), CHUNK_SHIFT)
    return jnp.where(kc <= qc, 0.0, NEG_INF).astype(F32)


def _mla_fill_keys(kv_ref, kr_ref, k2):
    k2[:, :NOPE] = kv_ref[0, :, :NOPE]
    k2[:, NOPE:] = kr_ref[0]


def _t(x):
    return x.astype(F32).T


def mla_fwd(name, qf, kv, kr):
    B, S, W = qf.shape
    HB = W // 256
    QB = _tile(S, 256, CHUNK)
    nq = S // QB
    scale = (NOPE + ROPE) ** -0.5

    def body(q_ref, kv_ref, kr_ref, o_ref, lse_ref, k2, vt, st_buf, pen):
        qi = pl.program_id(2)

        @pl.when(qi == 0)
        def _():
            pen[...] = _mla_diag_pen(QB)
            _mla_fill_keys(kv_ref, kr_ref, k2)
            for kj in range(nq):
                vt[kj] = _t(kv_ref[0, kj * QB:(kj + 1) * QB, NOPE:]).astype(BF16)

        q = q_ref[0]
        st_buf[0] = _mla_raw_t(k2, 0, q, QB)

        def step(kj, carry):
            m, l, acc = carry
            cur = lax.rem(kj, 2)
            st_raw = st_buf[cur]
            st_buf[1 - cur] = _mla_raw_t(k2, jnp.minimum(kj + 1, qi), q, QB)
            st = st_raw * scale + jnp.where(kj == qi, pen[...], 0.0)
            m_new = jnp.maximum(m, jnp.max(st, axis=0, keepdims=True))
            a = jnp.exp(m - m_new)
            pt = jnp.exp(st - m_new)
            l = a * l + jnp.sum(pt, axis=0, keepdims=True)
            acc = a * acc + jnp.dot(vt[kj], pt.astype(BF16), preferred_element_type=F32)
            return m_new, l, acc

        init = (jnp.full((1, QB), NEG_INF, F32), jnp.zeros((1, QB), F32), jnp.zeros((NOPE, QB), F32))
        m, l, acc = lax.fori_loop(0, qi + 1, step, init)
        o_ref[0] = (acc * (1.0 / l)).T
        lse_ref[0, 0] = m + jnp.log(l)

    blk = (_nbytes((QB, 256), BF16) + _nbytes((S, 256), BF16) + _nbytes((S, LANE), BF16)
           + _nbytes((QB, LANE), F32))
    return pl.pallas_call(
        body, name=name, grid=(B, HB, nq),
        in_specs=[pl.BlockSpec((1, QB, 256), lambda b, h, i: (b, i, h)),
                  pl.BlockSpec((1, S, 256), lambda b, h, i: (b, 0, h)),
                  pl.BlockSpec((1, S, LANE), lambda b, h, i: (b, 0, 0))],
        out_specs=[pl.BlockSpec((1, QB, LANE), lambda b, h, i: (b, i, h)),
                   pl.BlockSpec((1, 1, 1, QB), lambda b, h, i: (b, h, 0, i))],
        out_shape=[jax.ShapeDtypeStruct((B, S, HB * LANE), F32), jax.ShapeDtypeStruct((B, HB, 1, S), F32)],
        scratch_shapes=[pltpu.VMEM((S, 256), BF16), pltpu.VMEM((nq, NOPE, QB), BF16),
                        pltpu.VMEM((2, QB, QB), F32), pltpu.VMEM((QB, QB), F32)],
        compiler_params=_params(blk, 2 * _nbytes((S, 256), BF16) + 10 * _nbytes((QB, QB), F32)),
    )(qf, kv, kr)


def mla_bwd(name, qf, kv, kr, do, o, lse):
    B, S, W = qf.shape
    HB = W // 256
    QB = _tile(S, 256, CHUNK)
    nq = S // QB
    scale = (NOPE + ROPE) ** -0.5

    def body(q_ref, kv_ref, kr_ref, do_ref, o_ref, lse_ref, dq_ref, dkv_ref, dkr_ref, k2, kt, dot_, delta, dqt,
             st_buf, dp_buf, pen):
        h = pl.program_id(1)
        pen[...] = _mla_diag_pen(QB)
        dkv_ref[...] = jnp.zeros_like(dkv_ref)

        @pl.when(h == 0)
        def _():
            dkr_ref[...] = jnp.zeros_like(dkr_ref)

        _mla_fill_keys(kv_ref, kr_ref, k2)
        for i in range(nq):
            rows = slice(i * QB, (i + 1) * QB)
            kt[i] = _t(k2[rows, :]).astype(BF16)
            dot32 = _t(do_ref[0, rows, :])
            delta[i] = jnp.sum(dot32 * o_ref[0, rows, :].T, axis=0, keepdims=True)
            dot_[i] = dot32.astype(BF16)

        for qi in range(nq):
            rows = slice(qi * QB, (qi + 1) * QB)
            q = q_ref[0, rows, :]
            dob = do_ref[0, rows, :]
            lse_q = lse_ref[0, 0, :, rows]
            delta_q = delta[qi]
            dqt[...] = jnp.zeros_like(dqt)

            def raw(kj, slot, q=q, qi=qi):
                st_buf[slot] = _mla_raw_t(k2, kj, q, QB)
                dp_buf[slot] = jnp.dot(kv_ref[0, _blk(kj, QB), NOPE:], dot_[qi], preferred_element_type=F32)

            raw(0, 0)

            def step(kj, carry, q=q, dob=dob, lse_q=lse_q, delta_q=delta_q, qi=qi, raw=raw):
                ks = pl.ds(pl.multiple_of(kj * QB, QB), QB)
                cur = lax.rem(kj, 2)
                st_raw, dp_raw = st_buf[cur], dp_buf[cur]
                raw(jnp.minimum(kj + 1, qi), 1 - cur)
                pt = jnp.exp(st_raw * scale + jnp.where(kj == qi, pen[...], 0.0) - lse_q)
                dst = (pt * (dp_raw - delta_q) * scale).astype(BF16)
                dkv_ref[0, ks, NOPE:] += jnp.dot(pt.astype(BF16), dob, preferred_element_type=F32)
                dk2 = jnp.dot(dst, q, preferred_element_type=F32)
                dkv_ref[0, ks, :NOPE] += dk2[:, :NOPE]
                dkr_ref[0, ks, :] += dk2[:, NOPE:]
                dqt[...] += jnp.dot(kt[kj], dst, preferred_element_type=F32)
                return carry

            lax.fori_loop(0, qi + 1, step, 0)
            dq_ref[0, rows, :] = dqt[...].T

    head = lambda w: pl.BlockSpec((1, S, w), lambda b, h: (b, 0, h))
    shared = pl.BlockSpec((1, S, LANE), lambda b, h: (b, 0, 0))
    blk = (2 * _nbytes((S, 256), BF16) + 2 * _nbytes((S, LANE), BF16) + _nbytes((S, LANE), F32)
           + 2 * _nbytes((S, 256), F32) + _nbytes((S, LANE), F32))
    scr = 3 * _nbytes((S, 256), BF16) + 14 * _nbytes((QB, QB), F32)
    return pl.pallas_call(
        body, name=name, grid=(B, HB),
        in_specs=[head(256), head(256), shared, head(LANE), head(LANE),
                  pl.BlockSpec((1, 1, 1, S), lambda b, h: (b, h, 0, 0))],
        out_specs=[head(256), head(256), shared],
        out_shape=[jax.ShapeDtypeStruct((B, S, W), F32), jax.ShapeDtypeStruct((B, S, W), F32),
                   jax.ShapeDtypeStruct((B, S, LANE), F32)],
        scratch_shapes=[pltpu.VMEM((S, 256), BF16), pltpu.VMEM((nq, 256, QB), BF16),
                        pltpu.VMEM((nq, NOPE, QB), BF16), pltpu.VMEM((nq, 1, QB), F32),
                        pltpu.VMEM((256, QB), F32), pltpu.VMEM((2, QB, QB), F32), pltpu.VMEM((2, QB, QB), F32),
                        pltpu.VMEM((QB, QB), F32)],
        compiler_params=_params(blk, scr),
    )(qf, kv, kr, do, o, lse)


GROUP_STEPS = 4


def cast_group(name, ws, layers, idx):
    n = len(ws)

    def body(k_ref, *refs):
        for i in range(n):
            refs[n + i][...] = refs[i][...].astype(BF16)

    def spec_in(w, layer):
        return pl.BlockSpec((None, w.shape[1] // GROUP_STEPS, w.shape[2]), lambda r, k_ref: (layer, r, 0))

    def spec_out(w):
        return pl.BlockSpec((None, w.shape[1] // GROUP_STEPS, w.shape[2]), lambda r, k_ref: (k_ref[0], r, 0))

    return pl.pallas_call(
        body, name=name,
        grid_spec=pltpu.PrefetchScalarGridSpec(
            num_scalar_prefetch=1, grid=(GROUP_STEPS,),
            in_specs=[spec_in(w, l) for w, l in zip(ws, layers)], out_specs=[spec_out(w) for w in ws]),
        out_shape=[jax.ShapeDtypeStruct((N_CHIPS, *w.shape[1:]), BF16) for w in ws],
        compiler_params=_params(sum(_nbytes(w.shape[1:], F32) * 3 // 2 for w in ws) // GROUP_STEPS),
    )(idx, *ws)


def adamw(name, w, g, m, v):
    R, C = w.shape
    tr = _tile(R, max(8, (1 << 18) // C // 8 * 8), 8)
    c1 = 1.0 - ADAM_B1 ** ADAM_STEP
    c2 = 1.0 - ADAM_B2 ** ADAM_STEP

    def body(w_ref, g_ref, m_ref, v_ref, d_ref, mo_ref, vo_ref):
        gv = g_ref[...]
        mn = ADAM_B1 * m_ref[...] + (1.0 - ADAM_B1) * gv
        vn = ADAM_B2 * v_ref[...] + (1.0 - ADAM_B2) * (gv * gv)
        mo_ref[...] = mn
        vo_ref[...] = vn
        d_ref[...] = -ADAM_LR * ((mn / c1) / (jnp.sqrt(vn / c2) + ADAM_EPS) + ADAM_WD * w_ref[...])

    spec = pl.BlockSpec((tr, C), lambda r: (r, 0))
    return pl.pallas_call(
        body, name=name, grid=(R // tr,), in_specs=[spec] * 4, out_specs=[spec] * 3,
        out_shape=[jax.ShapeDtypeStruct((R, C), F32)] * 3,
        compiler_params=_params(7 * _nbytes((tr, C), F32), 4 * _nbytes((tr, C), F32)),
    )(w, g, m, v)


def half_sum_group(name, dws, landed, idx):
    n = len(dws)
    steps = GROUP_STEPS // 2

    def body(i_ref, *refs):
        for i in range(n):
            refs[2 * n + i][...] = (refs[i][...].astype(F32) + refs[n + i][...].astype(F32)).astype(BF16)

    def own(d):
        return pl.BlockSpec((None, None, d.shape[2] // steps, d.shape[3]), lambda k, r, i_ref: (k, i_ref[1], r, 0))

    def flat(d):
        return pl.BlockSpec((None, d.shape[2] // steps, d.shape[3]), lambda k, r, i_ref: (k, r, 0))

    return pl.pallas_call(
        body, name=name,
        grid_spec=pltpu.PrefetchScalarGridSpec(
            num_scalar_prefetch=1, grid=(N_CHIPS, steps),
            in_specs=[own(d) for d in dws] + [flat(d) for d in dws], out_specs=[flat(d) for d in dws]),
        out_shape=[jax.ShapeDtypeStruct((N_CHIPS, *d.shape[2:]), BF16) for d in dws],
        compiler_params=_params(sum(3 * _nbytes(d.shape[2:], BF16) for d in dws) // steps),
    )(idx, *dws, *landed)


def chip_sum_group(name, parts, landed, gbufs, layers, idx):
    n = len(parts)
    steps = GROUP_STEPS // 2

    def body(i_ref, *refs):
        for i in range(n):
            a, b = refs[i], refs[n + i]
            refs[3 * n + i][...] = ((a[...].astype(F32) + b[0].astype(F32)) + b[1].astype(F32)) + b[2].astype(F32)

    def mine(p):
        return pl.BlockSpec((None, p.shape[1] // steps, p.shape[2]), lambda r, i_ref: (i_ref[0], r, 0))

    def three(p):
        return pl.BlockSpec((3, p.shape[1] // steps, p.shape[2]), lambda r, i_ref: (0, r, 0))

    def out(p, layer):
        return pl.BlockSpec((None, None, p.shape[1] // steps, p.shape[2]), lambda r, i_ref: (layer, i_ref[1], r, 0))

    return pl.pallas_call(
        body, name=name,
        grid_spec=pltpu.PrefetchScalarGridSpec(
            num_scalar_prefetch=1, grid=(steps,),
            in_specs=[mine(p) for p in parts] + [three(p) for p in parts] + [pl.BlockSpec(memory_space=pl.ANY)] * n,
            out_specs=[out(p, l) for p, l in zip(parts, layers)]),
        out_shape=[jax.ShapeDtypeStruct(g.shape, F32) for g in gbufs],
        input_output_aliases={1 + 2 * n + i: i for i in range(n)},
        compiler_params=_params(sum(6 * _nbytes(p.shape[1:], BF16) for p in parts) // steps),
    )(idx, *parts, *landed, *gbufs)


ANY = pl.BlockSpec(memory_space=pl.ANY)


def _place():
    x, y, c = lax.axis_index("x"), lax.axis_index("y"), lax.axis_index("c")
    chips = [(1 - x, y), (x, 1 - y), (1 - x, 1 - y)]
    return x, y, c, chips


HBM = pl.BlockSpec(memory_space=pltpu.HBM)
SEM = pl.BlockSpec(memory_space=pltpu.SEMAPHORE)
EFFECT = pltpu.SideEffectType.DATAFLOW_SIDE_EFFECTING


def _in_hbm(a):
    return pltpu.with_memory_space_constraint(a, pltpu.HBM)


def _ici_copy(src, dst, send_sems, recv_sems, k, peer):
    return pltpu.make_async_remote_copy(src_ref=src, dst_ref=dst, send_sem=send_sems.at[k], recv_sem=recv_sems.at[k],
                                        device_id=peer, device_id_type=MESH)


def ici_start(name, bufs, lands, after, gather):
    n, nl = len(bufs), len(lands)

    def body(*refs):
        b_in = refs[:n]
        send_sems, recv_sems = refs[n + nl + 1], refs[n + nl + 2]
        b_out = refs[n + nl + 3:2 * n + nl + 3]
        l_out = refs[2 * n + nl + 3:2 * n + 2 * nl + 3]
        token = refs[-1]
        x, y, c, chips = _place()
        kme = 2 * x + y
        for i in range(n):
            for j in range(3):
                peer = (*chips[j], c)
                if gather:
                    _ici_copy(b_out[i].at[kme, c], b_out[i].at[kme, c], send_sems, recv_sems, 3 * i + j, peer).start()
                else:
                    kd = 2 * chips[j][0] + chips[j][1]
                    _ici_copy(b_out[i].at[kd], l_out[i].at[j], send_sems, recv_sems, 3 * i + j, peer).start()
        token[...] = jnp.zeros_like(token)

    arrays = [*bufs, *lands]
    outs = pl.pallas_call(
        body, name=name,
        in_specs=[HBM] * (n + nl) + [ANY],
        out_specs=(SEM, SEM, *[HBM] * (n + nl), pl.BlockSpec(memory_space=pltpu.VMEM)),
        out_shape=(pltpu.SemaphoreType.DMA((3 * n,)), pltpu.SemaphoreType.DMA((3 * n,)),
                   *[pltpu.HBM(a.shape, a.dtype) for a in arrays], jax.ShapeDtypeStruct((8, LANE), F32)),
        input_output_aliases={i: 2 + i for i in range(n + nl)},
        compiler_params=pltpu.CompilerParams(has_side_effects=EFFECT),
    )(*[_in_hbm(a) for a in arrays], after)
    return outs[0], outs[1], list(outs[2:2 + n]), list(outs[2 + n:2 + n + nl]), outs[-1]


def ici_wait(name, send_sems, recv_sems, bufs, lands, after, gather):
    n, nl = len(bufs), len(lands)

    def body(*refs):
        b_in, l_in = refs[:n], refs[n:n + nl]
        send_sems, recv_sems = refs[n + nl], refs[n + nl + 1]
        x, y, c, chips = _place()
        kme = 2 * x + y
        for i in range(n):
            for j in range(3):
                peer = (*chips[j], c)
                kj = 2 * chips[j][0] + chips[j][1]
                if gather:
                    _ici_copy(b_in[i].at[kme, c], b_in[i].at[kme, c], send_sems, recv_sems, 3 * i + j, peer).wait_send()
                    _ici_copy(b_in[i].at[kj, c], b_in[i].at[kj, c], send_sems, recv_sems, 3 * i + j, peer).wait_recv()
                else:
                    _ici_copy(b_in[i].at[kj], l_in[i].at[j], send_sems, recv_sems, 3 * i + j, peer).wait_send()
                    _ici_copy(b_in[i].at[kj], l_in[i].at[j], send_sems, recv_sems, 3 * i + j, peer).wait_recv()

    arrays = [*bufs, *lands]
    outs = pl.pallas_call(
        body, name=name,
        in_specs=[HBM] * (n + nl) + [SEM, SEM, ANY],
        out_specs=tuple([HBM] * (n + nl)),
        out_shape=tuple(pltpu.HBM(a.shape, a.dtype) for a in arrays),
        input_output_aliases={i: i for i in range(n + nl)},
        compiler_params=pltpu.CompilerParams(has_side_effects=EFFECT),
    )(*arrays, send_sems, recv_sems, after)
    return list(outs[:n]), list(outs[n:])


def gather_pair_pass(name, bufs):
    n = len(bufs)

    def body(*refs):
        b = refs[n:2 * n]
        send_sems, recv_sems = refs[2 * n:]
        x, y, c, chips = _place()
        sib = (x, y, 1 - c)

        def d2d(i, j, which):
            kj = 2 * chips[j][0] + chips[j][1]
            return _ici_copy(b[i].at[kj, which], b[i].at[kj, which], send_sems, recv_sems, 3 * i + j, sib)

        for i in range(n):
            for j in range(3):
                d2d(i, j, c).start()
        for i in range(n):
            for j in range(3):
                d2d(i, j, 1 - c).wait_recv()
        for i in range(n):
            for j in range(3):
                d2d(i, j, c).wait_send()

    return pl.pallas_call(
        body, name=name, in_specs=[ANY] * n, out_specs=[ANY] * n,
        out_shape=[jax.ShapeDtypeStruct(a.shape, a.dtype) for a in bufs],
        input_output_aliases={i: i for i in range(n)},
        scratch_shapes=[pltpu.SemaphoreType.DMA((3 * n,)), pltpu.SemaphoreType.DMA((3 * n,))],
    )(*bufs)


def pair_exchange(name, dws):
    n = len(dws)

    def body(*refs):
        ins, outs = refs[:n], refs[n:2 * n]
        send_sems, recv_sems = refs[2 * n:]
        x, y, c, _ = _place()
        copies = []
        for i in range(n):
            copies.append(pltpu.make_async_remote_copy(
                src_ref=ins[i].at[:, 1 - c], dst_ref=outs[i],
                send_sem=send_sems.at[i], recv_sem=recv_sems.at[i],
                device_id=(x, y, 1 - c), device_id_type=MESH))
            copies[i].start()
        for cp in copies:
            cp.wait_recv()
        for cp in copies:
            cp.wait_send()

    return pl.pallas_call(
        body, name=name, in_specs=[ANY] * n, out_specs=[ANY] * n,
        out_shape=[jax.ShapeDtypeStruct((N_CHIPS, *d.shape[2:]), d.dtype) for d in dws],
        scratch_shapes=[pltpu.SemaphoreType.DMA((n,)), pltpu.SemaphoreType.DMA((n,))],
    )(*dws)


def pair_assemble(gbufs):
    n = len(gbufs)

    def body(*refs):
        bufs = refs[n:2 * n]
        send_sems, recv_sems = refs[2 * n:]
        x, y, c, _ = _place()
        copies = []
        for i in range(n):
            copies.append(pltpu.make_async_remote_copy(
                src_ref=bufs[i].at[:, c], dst_ref=bufs[i].at[:, c],
                send_sem=send_sems.at[i], recv_sem=recv_sems.at[i],
                device_id=(x, y, 1 - c), device_id_type=MESH))
            copies[i].start()
        for i in range(n):
            pltpu.make_async_remote_copy(
                src_ref=bufs[i].at[:, 1 - c], dst_ref=bufs[i].at[:, 1 - c],
                send_sem=send_sems.at[i], recv_sem=recv_sems.at[i],
                device_id=(x, y, 1 - c), device_id_type=MESH).wait_recv()
        for cp in copies:
            cp.wait_send()

    return pl.pallas_call(
        body, name="grad_pair_assemble", in_specs=[ANY] * n, out_specs=[ANY] * n,
        out_shape=[jax.ShapeDtypeStruct(g.shape, g.dtype) for g in gbufs],
        input_output_aliases={i: i for i in range(n)},
        scratch_shapes=[pltpu.SemaphoreType.DMA((n,)), pltpu.SemaphoreType.DMA((n,))],
    )(*gbufs)


def all_reduce_small(vec):
    NR = vec.shape[0]
    flips = [(fx, fy, fc) for fx in (0, 1) for fy in (0, 1) for fc in (0, 1)][1:]

    def body(v_ref, o_ref, buf, send_sems, recv_sems):
        x, y, c, _ = _place()
        me = 4 * x + 2 * y + c
        buf[me] = v_ref[...]
        copies = []
        for j, (fx, fy, fc) in enumerate(flips):
            peer = (1 - x if fx else x, 1 - y if fy else y, 1 - c if fc else c)
            copies.append(pltpu.make_async_remote_copy(
                src_ref=v_ref, dst_ref=buf.at[me], send_sem=send_sems.at[j], recv_sem=recv_sems.at[j],
                device_id=peer, device_id_type=MESH))
            copies[j].start()
        for cp in copies:
            cp.wait_recv()
        for cp in copies:
            cp.wait_send()
        acc = buf[0]
        for d in range(1, 8):
            acc = acc + buf[d]
        o_ref[...] = acc

    return pl.pallas_call(
        body, name="all_reduce_small",
        in_specs=[pl.BlockSpec(memory_space=pltpu.VMEM)], out_specs=pl.BlockSpec(memory_space=pltpu.VMEM),
        out_shape=jax.ShapeDtypeStruct((NR, LANE), F32),
        scratch_shapes=[pltpu.VMEM((8, NR, LANE), F32), pltpu.SemaphoreType.DMA((7,)),
                        pltpu.SemaphoreType.DMA((7,))],
    )(vec)


def _pack(arrays):
    flat = jnp.concatenate([a.reshape(-1).astype(F32) for a in arrays])
    n = flat.shape[0]
    npad = -(-n // (8 * LANE)) * (8 * LANE)
    return jnp.pad(flat, (0, npad - n)).reshape(npad // LANE, LANE)


def _unpack(buf, like):
    flat = buf.reshape(-1)
    out, off = [], 0
    for a in like:
        out.append(flat[off:off + a.size].reshape(a.shape))
        off += a.size
    return out


def kernel(x, ffn1_norm, ffn1_w_in, ffn1_w_out, mix_norm, ffn2_norm, ffn2_w_in, ffn2_w_out, a_w_qkv, a_rel_bias, a_w_o, kv_norm, kv_w_down, kv_latent_norm, kv_w_up, b_w_dq, b_q_norm, b_w_uq, b_w_o, final_norm, loss_target, m_ffn1_norm, m_ffn1_w_in, m_ffn1_w_out, m_mix_norm, m_ffn2_norm, m_ffn2_w_in, m_ffn2_w_out, m_a_w_qkv, m_a_rel_bias, m_a_w_o, m_kv_norm, m_kv_w_down, m_kv_latent_norm, m_kv_w_up, m_b_w_dq, m_b_q_norm, m_b_w_uq, m_b_w_o, m_final_norm, v_ffn1_norm, v_ffn1_w_in, v_ffn1_w_out, v_mix_norm, v_ffn2_norm, v_ffn2_w_in, v_ffn2_w_out, v_a_w_qkv, v_a_rel_bias, v_a_w_o, v_kv_norm, v_kv_w_down, v_kv_latent_norm, v_kv_w_up, v_b_w_dq, v_b_q_norm, v_b_w_uq, v_b_w_o, v_final_norm):
    B, S, D = x.shape
    T = B * S
    HB = D // 128
    QL = b_q_norm.shape[-1]
    KVL = kv_latent_norm.shape[0]
    hpc = HB // N_CHIPS
    tabs = rope_tables(S)
    idx = jnp.stack([2 * lax.axis_index("x") + lax.axis_index("y"), lax.axis_index("c")]).astype(I32)

    def halves(a):
        return a.reshape(*a.shape[:-2], 2, a.shape[-2] // 2, a.shape[-1])

    def whole(a):
        return a.reshape(*a.shape[:-3], 2 * a.shape[-2], a.shape[-1])

    kv_w_down_p = jnp.pad(kv_w_down, ((0, 0), (0, LANE - ROPE)))[None]
    b_w_uq_p = jnp.pad(b_w_uq.reshape(1, QL, hpc, NOPE + ROPE),
                       ((0, 0), (0, 0), (0, 0), (0, LANE - ROPE))).reshape(1, QL, hpc * 256)
    sharded = [("ffn1_w_in", ffn1_w_in), ("ffn1_w_out", ffn1_w_out), ("ffn2_w_in", ffn2_w_in),
               ("ffn2_w_out", ffn2_w_out), ("a_w_qkv", a_w_qkv), ("a_w_o", a_w_o),
               ("kv_w_down", kv_w_down_p), ("kv_w_up", kv_w_up[None]), ("b_w_dq", b_w_dq),
               ("b_w_uq", b_w_uq_p), ("b_w_o", b_w_o)]
    pieces = [(a, l) for a, (_, w) in enumerate(sharded) for l in range(w.shape[0])]
    names = [nm for nm, _ in sharded]
    shard_of = dict(sharded)
    W = {}

    gather_groups = [
        [("ffn1_w_in", 0), ("ffn1_w_out", 0)],
        [("a_w_qkv", 0), ("a_w_o", 0), ("ffn2_w_in", 0), ("ffn2_w_out", 0), ("kv_w_down", 0), ("kv_w_up", 0)],
        [("ffn1_w_in", 1), ("ffn1_w_out", 1), ("b_w_dq", 0), ("b_w_uq", 0), ("b_w_o", 0), ("ffn2_w_in", 1),
         ("ffn2_w_out", 1)]]

    own = {}
    for g, keys in enumerate(gather_groups):
        cast = cast_group(f"cast_group_{g}", [shard_of[nm] for nm, _ in keys], [l for _, l in keys], idx)
        own.update(zip(keys, cast))

    def gather_start(g, after):
        keys = gather_groups[g]
        ss, rs, bufs, _, token = ici_start(f"gather_start_{g}", [halves(own[k]) for k in keys], [], after, True)
        return (g, ss, rs, bufs), token

    def gather_finish(state, after):
        g, ss, rs, bufs = state
        bufs, _ = ici_wait(f"gather_wait_{g}", ss, rs, bufs, [], after, True)
        full = gather_pair_pass(f"gather_pair_{g}", bufs)
        for k, w in zip(gather_groups[g], full):
            W[k] = whole(w)
        return full[0]

    def tied(a, token):
        return a + token[0, 0]

    def col(nm, l=0):
        return W[(nm, l)]

    def row(nm, l=0):
        w = W[(nm, l)]
        return w.reshape(N_CHIPS * w.shape[1], w.shape[2])

    bias = rel_bias_tile("rel_bias_tile", a_rel_bias[0])

    def ffn_fwd(tag, h, g, w_in, w_out):
        xn = rms_fwd(f"{tag}_norm", h, g)
        u, act = ffn_in_act(f"{tag}_in", xn, w_in)
        return mm_roww(f"{tag}_out", act, w_out, F32, res=h, alpha=0.5), (xn, u, act)

    h0 = x.reshape(T, D)
    st0, tok0 = gather_start(0, h0)
    done0 = gather_finish(st0, tok0)
    st1, tok1 = gather_start(1, done0)
    h1, sv_f1a = ffn_fwd("l0f1", h0, tied(ffn1_norm[0], tok1), col("ffn1_w_in", 0), row("ffn1_w_out", 0))
    done1 = gather_finish(st1, h1)
    st2, tok2 = gather_start(2, done1)
    hn_a = rms_fwd("l0mix_norm", h1, tied(mix_norm[0], tok2))
    qkv = mm_colw("l0_qkv", hn_a, col("a_w_qkv"), BF16).reshape(B, S, 3 * D)
    o_a = attn_a_fwd("l0_attn", qkv, bias).reshape(T, D)
    h2 = mm_roww("l0_attn_out", o_a, row("a_w_o"), F32, res=h1)
    h3, sv_f2a = ffn_fwd("l0f2", h2, ffn2_norm[0], col("ffn2_w_in", 0), row("ffn2_w_out", 0))

    hkv = rms_fwd("kv_norm", h3, kv_norm)
    ckr = mm_roww("kv_down", hkv, row("kv_w_down"), F32)
    ckv, kr = kvprep_fwd("kv_prep", ckr, kv_latent_norm, tabs, B, S)
    kvb = mm_colw("kv_up", ckv, col("kv_w_up"), BF16).reshape(B, S, HB * 256)
    gather_finish(st2, kvb)

    h4, sv_f1b = ffn_fwd("l1f1", h3, ffn1_norm[1], col("ffn1_w_in", 1), row("ffn1_w_out", 1))
    hn_b = rms_fwd("l1mix_norm", h4, mix_norm[1])
    cqp = mm_roww("l1_dq", hn_b, row("b_w_dq"), F32)
    cq = rms_fwd("l1_q_norm", cqp, b_q_norm[0])
    qpre = mm_colw("l1_uq", cq, col("b_w_uq"), F32)
    qf = qprep("l1_q_rope", qpre, tabs, B, S, bwd=False).reshape(B, S, HB * 256)
    o_b, lse = mla_fwd("l1_attn", qf, kvb, kr)
    h5 = mm_roww("l1_attn_out", o_b.reshape(T, HB * LANE), row("b_w_o"), F32, res=h4)
    h6, sv_f2b = ffn_fwd("l1f2", h5, ffn2_norm[1], col("ffn2_w_in", 1), row("ffn2_w_out", 1))

    dh, g_final, loss_part = loss_head("loss_head", h6, final_norm, loss_target.reshape(T, D))

    gw = {}
    gbufs = {nm: lax.empty(halves(w).shape, F32) for nm, w in sharded}

    def reduce_start(r, keys, after):
        dws = [halves(gw[k]) for k in keys]
        landed = pair_exchange(f"grad_pair_exchange_{r}", dws)
        parts = half_sum_group(f"half_sum_{r}", dws, landed, idx)
        lands = [lax.empty((3, *p.shape[1:]), p.dtype) for p in parts]
        ss, rs, parts, lands, token = ici_start(f"reduce_start_{r}", parts, lands, after, False)
        return (r, keys, ss, rs, parts, lands), token

    def reduce_finish(state, after):
        r, keys, ss, rs, parts, lands = state
        parts, lands = ici_wait(f"reduce_wait_{r}", ss, rs, parts, lands, after, False)
        done = chip_sum_group(f"chip_sum_{r}", parts, lands, [gbufs[nm] for nm, _ in keys], [l for _, l in keys], idx)
        gbufs.update(zip([nm for nm, _ in keys], done))
        return done[0]

    def ffn_bwd(tag, dh, h_in, g, w_in, w_out, saved, key_in, key_out, after=None):
        xn, u, act = saved
        du = ffn_dact(f"{tag}_dact", dh, w_out, u, after=after)
        dwo = mm_droww(f"{tag}_dwout", act, dh, alpha=0.5)
        gw[key_out] = dwo.reshape(N_CHIPS, dwo.shape[0] // N_CHIPS, dwo.shape[1])
        gw[key_in] = mm_dcolw(f"{tag}_dwin", xn, du, pair_layout=True)
        return dx_norm_bwd(f"{tag}_dxn", du, w_in, h_in, g, dres=dh, pair_layout=True)

    def chip_major(dw):
        return dw.reshape(N_CHIPS, dw.shape[0] // N_CHIPS, dw.shape[1])

    dh, g_f2b = ffn_bwd("l1f2b", dh, h5, ffn2_norm[1], col("ffn2_w_in", 1), row("ffn2_w_out", 1), sv_f2b,
                        ("ffn2_w_in", 1), ("ffn2_w_out", 1))
    red0, rtok0 = reduce_start(0, [("ffn2_w_in", 1), ("ffn2_w_out", 1)], dh)
    do_b = mm_roww_t("l1_attn_do", dh, row("b_w_o"), BF16, after=rtok0).reshape(B, S, HB * LANE)
    gw[("b_w_o", 0)] = chip_major(mm_droww("l1_attn_dwo", o_b.reshape(T, HB * LANE), dh))
    dqf, dkv, dkr = mla_bwd("l1_attn_bwd", qf, kvb, kr, do_b, o_b, lse)
    dqpre = qprep("l1_q_rope_bwd", dqf.reshape(T, HB * 256), tabs, B, S, bwd=True)
    gw[("b_w_uq", 0)] = mm_dcolw("l1_dwuq", cq, dqpre)
    dcqp, g_qn = dx_norm_bwd("l1_dcq", dqpre, col("b_w_uq"), cqp, b_q_norm[0])
    gw[("b_w_dq", 0)] = chip_major(mm_droww("l1_dwdq", hn_b, dcqp))
    dhn = mm_roww_t("l1_dhn", dcqp, row("b_w_dq"), F32)
    dh, g_mixb = rms_bwd("l1_dmix", h4, mix_norm[1], dhn, dres=dh)
    dh, g_f1b = ffn_bwd("l1f1b", dh, h3, ffn1_norm[1], col("ffn1_w_in", 1), row("ffn1_w_out", 1), sv_f1b,
                        ("ffn1_w_in", 1), ("ffn1_w_out", 1))
    fin0 = reduce_finish(red0, dh)
    red1, rtok1 = reduce_start(1, [("b_w_o", 0), ("b_w_uq", 0), ("b_w_dq", 0), ("ffn1_w_in", 1), ("ffn1_w_out", 1)], fin0)
    dkv2 = dkv.reshape(T, HB * 256)
    gw[("kv_w_up", 0)] = mm_dcolw("kv_dwup", ckv, dkv2, after=rtok1)
    dckv = mm_colw_t("kv_dckv", dkv2, col("kv_w_up"), F32, after=rtok1)
    dckr, g_lat = kvprep_bwd("kv_prep_bwd", ckr, kv_latent_norm, dckv, dkr, tabs, B, S)
    gw[("kv_w_down", 0)] = chip_major(mm_droww("kv_dwdown", hkv, dckr))
    dhkv = mm_roww_t("kv_dhkv", dckr, row("kv_w_down"), F32)
    dh, g_kvn = rms_bwd("kv_dnorm", h3, kv_norm, dhkv, dres=dh)
    dh, g_f2a = ffn_bwd("l0f2b", dh, h2, ffn2_norm[0], col("ffn2_w_in", 0), row("ffn2_w_out", 0), sv_f2a,
                        ("ffn2_w_in", 0), ("ffn2_w_out", 0))
    do_a = mm_roww_t("l0_attn_do", dh, row("a_w_o"), BF16).reshape(B, S, D)
    gw[("a_w_o", 0)] = chip_major(mm_droww("l0_attn_dwo", o_a, dh))
    dq_a, dkv_a, dbias = attn_a_bwd("l0_attn_bwd", qkv, do_a, bias)
    dqkv = jnp.concatenate([dq_a.reshape(T, D), dkv_a.reshape(T, 2 * D)], axis=1)
    gw[("a_w_qkv", 0)] = mm_dcolw("l0_dwqkv", hn_a, dqkv)
    dh, g_mixa = dx_norm_bwd("l0_dhn", dqkv, col("a_w_qkv"), h1, mix_norm[0], dres=dh)
    fin1 = reduce_finish(red1, dh)
    red2, rtok2 = reduce_start(2, [("kv_w_up", 0), ("kv_w_down", 0), ("ffn2_w_in", 0), ("ffn2_w_out", 0),
                                   ("a_w_o", 0), ("a_w_qkv", 0)], fin1)
    dh, g_f1a = ffn_bwd("l0f1b", dh, h0, ffn1_norm[0], col("ffn1_w_in", 0), row("ffn1_w_out", 0), sv_f1a,
                        ("ffn1_w_in", 0), ("ffn1_w_out", 0), after=rtok2)
    grad_x = dh.reshape(B, S, D)
    g_rel = rel_bias_grad("rel_bias_grad", dbias)[:, :2 * MAX_REL + 1][None]
    fin2 = reduce_finish(red2, dh)
    red3, rtok3 = reduce_start(3, [("ffn1_w_in", 0), ("ffn1_w_out", 0)], fin2)
    reduce_finish(red3, rtok3)

    full = [whole(g) for g in pair_assemble([gbufs[nm] for nm in names])]
    G = {nm: g for (nm, _), g in zip(sharded, full)}
    G["kv_w_down"] = G["kv_w_down"][0, :, :KVL + ROPE]
    G["kv_w_up"] = G["kv_w_up"][0]
    G["b_w_uq"] = G["b_w_uq"].reshape(1, QL, hpc, 256)[..., :NOPE + ROPE].reshape(b_w_uq.shape)

    small = [("ffn1_norm", jnp.stack([g_f1a, g_f1b])), ("mix_norm", jnp.stack([g_mixa, g_mixb])),
             ("ffn2_norm", jnp.stack([g_f2a, g_f2b])), ("a_rel_bias", g_rel), ("kv_norm", g_kvn),
             ("kv_latent_norm", g_lat), ("b_q_norm", g_qn[None]), ("final_norm", g_final)]
    red = all_reduce_small(_pack([loss_part] + [g for _, g in small]))
    unpacked = _unpack(red, [loss_part] + [g for _, g in small])
    loss = unpacked[0][0, 0]
    for (nm, _), g in zip(small, unpacked[1:]):
        G[nm] = g

    given = dict(ffn1_norm=(ffn1_norm, m_ffn1_norm, v_ffn1_norm), ffn1_w_in=(ffn1_w_in, m_ffn1_w_in, v_ffn1_w_in),
                 ffn1_w_out=(ffn1_w_out, m_ffn1_w_out, v_ffn1_w_out), mix_norm=(mix_norm, m_mix_norm, v_mix_norm),
                 ffn2_norm=(ffn2_norm, m_ffn2_norm, v_ffn2_norm), ffn2_w_in=(ffn2_w_in, m_ffn2_w_in, v_ffn2_w_in),
                 ffn2_w_out=(ffn2_w_out, m_ffn2_w_out, v_ffn2_w_out), a_w_qkv=(a_w_qkv, m_a_w_qkv, v_a_w_qkv),
                 a_rel_bias=(a_rel_bias, m_a_rel_bias, v_a_rel_bias), a_w_o=(a_w_o, m_a_w_o, v_a_w_o),
                 kv_norm=(kv_norm, m_kv_norm, v_kv_norm), kv_w_down=(kv_w_down, m_kv_w_down, v_kv_w_down),
                 kv_latent_norm=(kv_latent_norm, m_kv_latent_norm, v_kv_latent_norm),
                 kv_w_up=(kv_w_up, m_kv_w_up, v_kv_w_up), b_w_dq=(b_w_dq, m_b_w_dq, v_b_w_dq),
                 b_q_norm=(b_q_norm, m_b_q_norm, v_b_q_norm), b_w_uq=(b_w_uq, m_b_w_uq, v_b_w_uq),
                 b_w_o=(b_w_o, m_b_w_o, v_b_w_o), final_norm=(final_norm, m_final_norm, v_final_norm))
    order = list(given)
    delta, new_m, new_v = {}, {}, {}
    small_names = [nm for nm, _ in small]
    packed = [_pack([given[nm][k] for nm in small_names]) for k in range(3)]
    outs = adamw("adamw_small", packed[0], _pack([G[nm] for nm in small_names]), packed[1], packed[2])
    for dst, buf in zip((delta, new_m, new_v), outs):
        for nm, a in zip(small_names, _unpack(buf, [given[nm][0] for nm in small_names])):
            dst[nm] = a
    for nm, _ in sharded:
        w, m, v = given[nm]
        g = G[nm].reshape(w.shape)
        G[nm] = g
        two = lambda a: a.reshape(-1, a.shape[-1])
        d_, m_, v_ = adamw(f"adamw_{nm}", two(w), two(g), two(m), two(v))
        delta[nm], new_m[nm], new_v[nm] = d_.reshape(w.shape), m_.reshape(w.shape), v_.reshape(w.shape)

    return (loss, grad_x, *[G[n] for n in order], *[delta[n] for n in order],
            *[new_m[n] for n in order], *[new_v[n] for n in order])
```

```python
import functools
import math

import jax
import jax.numpy as jnp
from jax import lax
from jax.experimental import pallas as pl
from jax.experimental.pallas import tpu as pltpu

F32 = jnp.float32
BF16 = jnp.bfloat16
I32 = jnp.int32

CHUNK = 64
CHUNK_SHIFT = 6
HEAD_DIM_A = 64
LEFT_CHUNKS = 8
MAX_REL = 128
REL_PAD = 384
QROWS = 2 * CHUNK
WIN = (LEFT_CHUNKS + 2) * CHUNK
PADR = LEFT_CHUNKS * CHUNK
NOPE = 128
ROPE = 64
EPS = 1e-6
NEG_INF = -1e30
ROPE_THETA = 10000.0
ADAM_LR, ADAM_B1, ADAM_B2, ADAM_EPS, ADAM_WD, ADAM_STEP = 0.001, 0.9, 0.999, 1e-08, 0.01, 10
N_CHIPS = 4
LANE = 128
MESH = pl.DeviceIdType.MESH
VMEM_CAP_MB = 60

NN = (((1,), (0,)), ((), ()))
NT = (((1,), (1,)), ((), ()))
TN = (((0,), (0,)), ((), ()))


def _tile(n, pref, mult):
    t = (min(pref, n) // mult) * mult
    while t >= mult:
        if n % t == 0:
            return t
        t -= mult
    return n


def _nbytes(shape, dtype):
    return math.prod(shape) * jnp.dtype(dtype).itemsize


def _params(block_bytes, extra_bytes=0):
    need = 2 * block_bytes + extra_bytes
    mb = min(VMEM_CAP_MB, max(32, int(need * 1.25 / 2**20) + 8))
    return pltpu.CompilerParams(vmem_limit_bytes=mb * 2**20)


def _mm(name, kind, a, b, grid, a_spec, b_spec, o_spec, out_shape, out_dtype, blocks,
        red_axis=None, nred=1, alpha=1.0, res=None, res_spec=None, after=None):
    dims = {"nn": NN, "nt": NT, "tn": TN}[kind]
    has_res = res is not None
    acc_in_out = nred > 1 and out_dtype == F32 and not has_res and alpha == 1.0
    n_in = 2 + has_res + (after is not None)

    def body(*refs):
        a_ref, b_ref = refs[0], refs[1]
        r_ref = refs[2] if has_res else None
        o_ref = refs[n_in]
        p = lax.dot_general(a_ref[...].astype(BF16), b_ref[...].astype(BF16), dims,
                            preferred_element_type=F32)

        def finish(acc):
            y = acc if alpha == 1.0 else acc * alpha
            if has_res:
                y = r_ref[...] + y
            o_ref[...] = y.astype(o_ref.dtype)

        if nred == 1:
            finish(p)
            return
        k = pl.program_id(red_axis)
        tgt = o_ref if acc_in_out else refs[-1]

        @pl.when(k == 0)
        def _():
            tgt[...] = p

        @pl.when(k > 0)
        def _():
            tgt[...] += p

        if not acc_in_out:
            @pl.when(k == nred - 1)
            def _():
                finish(tgt[...])

    a_blk, b_blk, o_blk = blocks
    scratch = []
    extra = 0
    if nred > 1 and not acc_in_out:
        scratch = [pltpu.VMEM(o_blk, F32)]
        extra = _nbytes(o_blk, F32)
    blk = _nbytes(a_blk, a.dtype) + _nbytes(b_blk, b.dtype) + _nbytes(o_blk, out_dtype)
    ins, specs = [a, b], [a_spec, b_spec]
    if has_res:
        ins.append(res)
        specs.append(res_spec)
        blk += _nbytes(o_blk, res.dtype)
    if after is not None:
        ins.append(after)
        specs.append(pl.BlockSpec(memory_space=pl.ANY))
    extra += _nbytes(a_blk, BF16) + _nbytes(b_blk, BF16) + 2 * _nbytes(o_blk, F32)
    return pl.pallas_call(
        body, name=name, grid=grid, in_specs=specs, out_specs=o_spec,
        out_shape=jax.ShapeDtypeStruct(out_shape, out_dtype), scratch_shapes=scratch,
        compiler_params=_params(blk, extra),
    )(*ins)


def mm_colw(name, x, w3, out_dtype):
    T, K = x.shape
    _, _, nl = w3.shape
    tm = _tile(T, 512, 8)
    return _mm(name, "nn", x, w3, (N_CHIPS, T // tm),
               pl.BlockSpec((tm, K), lambda j, i: (i, 0)),
               pl.BlockSpec((None, K, nl), lambda j, i: (j, 0, 0)),
               pl.BlockSpec((tm, nl), lambda j, i: (i, j)),
               (T, N_CHIPS * nl), out_dtype, ((tm, K), (K, nl), (tm, nl)))


def _pair_chip(j):
    return (j % 2) * 2 + j // 2


def mm_colw_t(name, dy, w3, out_dtype, res=None, after=None, pair_layout=False):
    T = dy.shape[0]
    _, K, nl = w3.shape
    tm = _tile(T, 1024, 8)
    chip = _pair_chip if pair_layout else (lambda j: j)
    return _mm(name, "nt", dy, w3, (T // tm, N_CHIPS),
               pl.BlockSpec((tm, nl), lambda i, j: (i, j)),
               pl.BlockSpec((None, K, nl), lambda i, j: (chip(j), 0, 0)),
               pl.BlockSpec((tm, K), lambda i, j: (i, 0)),
               (T, K), out_dtype, ((tm, nl), (K, nl), (tm, K)),
               red_axis=1, nred=N_CHIPS, res=res,
               res_spec=pl.BlockSpec((tm, K), lambda i, j: (i, 0)), after=after)


def mm_dcolw(name, x, dy, after=None, pair_layout=False):
    T, K = x.shape
    nl = dy.shape[1] // N_CHIPS
    tt = _tile(T, 2048, 8)
    chip = _pair_chip if pair_layout else (lambda j: j)
    return _mm(name, "tn", x, dy, (N_CHIPS, T // tt),
               pl.BlockSpec((tt, K), lambda j, t: (t, 0)),
               pl.BlockSpec((tt, nl), lambda j, t: (t, j)),
               pl.BlockSpec((None, K, nl), lambda j, t: (chip(j), 0, 0)),
               (N_CHIPS, K, nl), BF16, ((tt, K), (tt, nl), (K, nl)),
               red_axis=1, nred=T // tt, after=after)


def mm_roww(name, x, w2, out_dtype, res=None, alpha=1.0):
    T, Kt = x.shape
    N = w2.shape[1]
    tm = _tile(T, 512, 8)
    return _mm(name, "nn", x, w2, (T // tm,),
               pl.BlockSpec((tm, Kt), lambda i: (i, 0)),
               pl.BlockSpec((Kt, N), lambda i: (0, 0)),
               pl.BlockSpec((tm, N), lambda i: (i, 0)),
               (T, N), out_dtype, ((tm, Kt), (Kt, N), (tm, N)),
               alpha=alpha, res=res, res_spec=pl.BlockSpec((tm, N), lambda i: (i, 0)))


def mm_roww_t(name, dy, w2, out_dtype, alpha=1.0, after=None):
    T, N = dy.shape
    Kt = w2.shape[0]
    tm = _tile(T, 512, 8)
    tk = _tile(Kt, 1408, LANE)
    return _mm(name, "nt", dy, w2, (Kt // tk, T // tm),
               pl.BlockSpec((tm, N), lambda j, i: (i, 0)),
               pl.BlockSpec((tk, N), lambda j, i: (j, 0)),
               pl.BlockSpec((tm, tk), lambda j, i: (i, j)),
               (T, Kt), out_dtype, ((tm, N), (tk, N), (tm, tk)), alpha=alpha, after=after)


def mm_droww(name, x, dy, alpha=1.0):
    T, Kt = x.shape
    N = dy.shape[1]
    tt = _tile(T, 2048, 8)
    tk = _tile(Kt, 1408, LANE)
    return _mm(name, "tn", x, dy, (Kt // tk, T // tt),
               pl.BlockSpec((tt, tk), lambda j, t: (t, j)),
               pl.BlockSpec((tt, N), lambda j, t: (t, 0)),
               pl.BlockSpec((tk, N), lambda j, t: (j, 0)),
               (Kt, N), BF16, ((tt, tk), (tt, N), (tk, N)),
               red_axis=1, nred=T // tt, alpha=alpha)


def rms_fwd(name, x, g):
    T, D = x.shape
    tm = _tile(T, 512, 8)

    def body(x_ref, g_ref, o_ref):
        xv = x_ref[...]
        r = lax.rsqrt(jnp.mean(xv * xv, axis=-1, keepdims=True) + EPS)
        o_ref[...] = (xv * r * g_ref[...]).astype(o_ref.dtype)

    return pl.pallas_call(
        body, name=name, grid=(T // tm,),
        in_specs=[pl.BlockSpec((tm, D), lambda i: (i, 0)), pl.BlockSpec((1, D), lambda i: (0, 0))],
        out_specs=pl.BlockSpec((tm, D), lambda i: (i, 0)),
        out_shape=jax.ShapeDtypeStruct((T, D), BF16),
        compiler_params=_params(_nbytes((tm, D), F32) * 2, 4 * _nbytes((tm, D), F32)),
    )(x, g.reshape(1, D))


def _rms_bwd_math(xv, gv, dy):
    r = lax.rsqrt(jnp.mean(xv * xv, axis=-1, keepdims=True) + EPS)
    xh = xv * r
    dyg = dy * gv
    dx = r * (dyg - xh * jnp.mean(dyg * xh, axis=-1, keepdims=True))
    dg = jnp.sum(dy * xh, axis=0, keepdims=True)
    return dx, dg


def rms_bwd(name, x, g, dy, dres=None):
    T, D = x.shape
    tm = _tile(T, 256, 8)
    has_res = dres is not None

    def body(*refs):
        x_ref, g_ref, dy_ref = refs[:3]
        r_ref = refs[3] if has_res else None
        dx_ref, dg_ref = refs[-2:]
        dx, dg = _rms_bwd_math(x_ref[...], g_ref[...], dy_ref[...].astype(F32))
        if has_res:
            dx = r_ref[...] + dx
        dx_ref[...] = dx

        @pl.when(pl.program_id(0) == 0)
        def _():
            dg_ref[...] = dg

        @pl.when(pl.program_id(0) > 0)
        def _():
            dg_ref[...] += dg

    row = pl.BlockSpec((tm, D), lambda i: (i, 0))
    vec = pl.BlockSpec((1, D), lambda i: (0, 0))
    ins, specs = [x, g.reshape(1, D), dy], [row, vec, row]
    if has_res:
        ins.append(dres)
        specs.append(row)
    dx, dg = pl.pallas_call(
        body, name=name, grid=(T // tm,), in_specs=specs, out_specs=[row, vec],
        out_shape=[jax.ShapeDtypeStruct((T, D), F32), jax.ShapeDtypeStruct((1, D), F32)],
        compiler_params=_params(_nbytes((tm, D), F32) * 4, 6 * _nbytes((tm, D), F32)),
    )(*ins)
    return dx, dg.reshape(D)


def dx_norm_bwd(name, dy, w3, x, g, dres=None, pair_layout=False, after=None):
    T = dy.shape[0]
    _, K, nl = w3.shape
    tm = _tile(T, 512, 8)
    chip = _pair_chip if pair_layout else (lambda j: j)
    has_res = dres is not None

    def body(*refs):
        dy_ref, w_ref, x_ref, g_ref = refs[:4]
        r_ref = refs[4] if has_res else None
        dx_ref, dg_ref, acc = refs[-3:]
        i, k = pl.program_id(0), pl.program_id(1)
        p = lax.dot_general(dy_ref[...].astype(BF16), w_ref[...], NT, preferred_element_type=F32)

        @pl.when(k == 0)
        def _():
            acc[...] = p

        @pl.when(k > 0)
        def _():
            acc[...] += p

        @pl.when(k == N_CHIPS - 1)
        def _():
            dx, dg = _rms_bwd_math(x_ref[...], g_ref[...], acc[...])
            dx_ref[...] = r_ref[...] + dx if has_res else dx

            @pl.when(i == 0)
            def _():
                dg_ref[...] = dg

            @pl.when(i > 0)
            def _():
                dg_ref[...] += dg

    row = pl.BlockSpec((tm, K), lambda i, j: (i, 0))
    vec = pl.BlockSpec((1, K), lambda i, j: (0, 0))
    ins = [dy, w3, x, g.reshape(1, K)]
    specs = [pl.BlockSpec((tm, nl), lambda i, j: (i, j)),
             pl.BlockSpec((None, K, nl), lambda i, j: (chip(j), 0, 0)), row, vec]
    if has_res:
        ins.append(dres)
        specs.append(row)
    if after is not None:
        ins.append(after)
        specs.append(pl.BlockSpec(memory_space=pl.ANY))
    blk = _nbytes((tm, nl), dy.dtype) + _nbytes((K, nl), BF16) + (2 + has_res) * _nbytes((tm, K), F32)
    dx, dg = pl.pallas_call(
        body, name=name, grid=(T // tm, N_CHIPS), in_specs=specs, out_specs=[row, vec],
        out_shape=[jax.ShapeDtypeStruct((T, K), F32), jax.ShapeDtypeStruct((1, K), F32)],
        scratch_shapes=[pltpu.VMEM((tm, K), F32)],
        compiler_params=_params(blk, 8 * _nbytes((tm, K), F32)),
    )(*ins)
    return dx, dg.reshape(K)


def ffn_in_act(name, x, w3, after=None):
    T, K = x.shape
    _, _, nl = w3.shape
    tm = _tile(T, 512, 8)

    def body(*refs):
        x_ref, wg_ref, wu_ref = refs[:3]
        u_ref, a_ref = refs[-2:]
        xv = x_ref[...]
        g = jnp.dot(xv, wg_ref[...], preferred_element_type=F32)
        up = jnp.dot(xv, wu_ref[...], preferred_element_type=F32)
        u_ref[:, :nl] = g.astype(u_ref.dtype)
        u_ref[:, nl:] = up.astype(u_ref.dtype)
        a_ref[...] = (g * jax.nn.sigmoid(g) * up).astype(a_ref.dtype)

    blk = _nbytes((tm, K), BF16) + 2 * _nbytes((K, nl), BF16) + _nbytes((tm, 3 * nl), BF16)
    return pl.pallas_call(
        body, name=name, grid=(2, T // tm),
        in_specs=[pl.BlockSpec((tm, K), lambda p, i: (i, 0)),
                  pl.BlockSpec((None, K, nl), lambda p, i: (p, 0, 0)),
                  pl.BlockSpec((None, K, nl), lambda p, i: (p + 2, 0, 0))]
        + [pl.BlockSpec(memory_space=pl.ANY)] * (after is not None),
        out_specs=[pl.BlockSpec((tm, 2 * nl), lambda p, i: (i, p)), pl.BlockSpec((tm, nl), lambda p, i: (i, p))],
        out_shape=[jax.ShapeDtypeStruct((T, 4 * nl), BF16), jax.ShapeDtypeStruct((T, 2 * nl), BF16)],
        compiler_params=_params(blk, 4 * _nbytes((tm, nl), F32)),
    )(x, w3, w3, *([] if after is None else [after]))


def ffn_dact(name, dh, w_out, u, after=None):
    T, N = dh.shape
    F = w_out.shape[0]
    nl = F // 2
    tm = _tile(T, 512, 8)

    def body(*refs):
        d_ref, w_ref, u_ref = refs[:3]
        o_ref = refs[-1]
        dact = 0.5 * lax.dot_general(d_ref[...].astype(BF16), w_ref[...], NT, preferred_element_type=F32)
        g = u_ref[:, :nl].astype(F32)
        up = u_ref[:, nl:].astype(F32)
        sig = jax.nn.sigmoid(g)
        o_ref[:, :nl] = (dact * up * (sig * (1.0 + g * (1.0 - sig)))).astype(o_ref.dtype)
        o_ref[:, nl:] = (dact * (g * sig)).astype(o_ref.dtype)

    ins = [dh, w_out, u]
    specs = [pl.BlockSpec((tm, N), lambda p, i: (i, 0)), pl.BlockSpec((nl, N), lambda p, i: (p, 0)),
             pl.BlockSpec((tm, 2 * nl), lambda p, i: (i, p))]
    if after is not None:
        ins.append(after)
        specs.append(pl.BlockSpec(memory_space=pl.ANY))
    blk = _nbytes((tm, N), F32) + _nbytes((nl, N), BF16) + 2 * _nbytes((tm, 2 * nl), BF16)
    return pl.pallas_call(
        body, name=name, grid=(2, T // tm), in_specs=specs,
        out_specs=pl.BlockSpec((tm, 2 * nl), lambda p, i: (i, p)),
        out_shape=jax.ShapeDtypeStruct((T, 2 * F), BF16),
        compiler_params=_params(blk, 6 * _nbytes((tm, nl), F32)),
    )(*ins)


def loss_head(name, h, g, target):
    T, D = h.shape
    tm = _tile(T, 256, 8)

    def body(h_ref, g_ref, t_ref, dh_ref, dg_ref, loss_ref):
        xv = h_ref[...]
        gv = g_ref[...]
        r = lax.rsqrt(jnp.mean(xv * xv, axis=-1, keepdims=True) + EPS)
        err = xv * r * gv - t_ref[...]
        part = 0.5 * jnp.sum(jnp.mean(err * err, axis=-1, keepdims=True), axis=0, keepdims=True)
        dx, dg = _rms_bwd_math(xv, gv, err * (1.0 / D))
        dh_ref[...] = dx
        part = jnp.broadcast_to(part, (1, LANE))

        @pl.when(pl.program_id(0) == 0)
        def _():
            dg_ref[...] = dg
            loss_ref[...] = part

        @pl.when(pl.program_id(0) > 0)
        def _():
            dg_ref[...] += dg
            loss_ref[...] += part

    row = pl.BlockSpec((tm, D), lambda i: (i, 0))
    vec = pl.BlockSpec((1, D), lambda i: (0, 0))
    dh, dg, loss = pl.pallas_call(
        body, name=name, grid=(T // tm,), in_specs=[row, vec, row],
        out_specs=[row, vec, pl.BlockSpec((1, LANE), lambda i: (0, 0))],
        out_shape=[jax.ShapeDtypeStruct((T, D), F32), jax.ShapeDtypeStruct((1, D), F32),
                   jax.ShapeDtypeStruct((1, LANE), F32)],
        compiler_params=_params(_nbytes((tm, D), F32) * 3, 6 * _nbytes((tm, D), F32)),
    )(h, g.reshape(1, D), target)
    return dh, dg.reshape(D), loss


def rope_tables(S):
    half = ROPE // 2
    freqs = ROPE_THETA ** (-jnp.arange(half, dtype=F32) / half)
    ang = jnp.arange(S, dtype=F32)[:, None] * freqs[None, :]
    cos, sin = jnp.cos(ang), jnp.sin(ang)
    z = jnp.zeros_like(cos)
    ct = jnp.concatenate([cos, cos, z, z], axis=1)
    s1 = jnp.concatenate([-sin, z, z, z], axis=1)
    s2 = jnp.concatenate([z, sin, z, z], axis=1)
    return ct, s1, s2


def _rope_tile(t, ct, s1, s2):
    return t * ct + pltpu.roll(t, 96, 1) * s1 + pltpu.roll(t, 32, 1) * s2


def _rope_tile_bwd(d, ct, s1, s2):
    return d * ct + pltpu.roll(d * s1, 32, 1) + pltpu.roll(d * s2, 96, 1)


def qprep(name, q, tabs, B, S, bwd):
    T, W = q.shape
    nh = W // 256
    ts = _tile(S, 256, 8)
    fn = _rope_tile_bwd if bwd else _rope_tile

    def body(q_ref, ct_ref, s1_ref, s2_ref, o_ref):
        ct, s1, s2 = ct_ref[...], s1_ref[...], s2_ref[...]
        for h in range(nh):
            o_ref[0, :, 256 * h:256 * h + 128] = q_ref[0, :, 256 * h:256 * h + 128].astype(o_ref.dtype)
            t = q_ref[0, :, 256 * h + 128:256 * h + 256].astype(F32)
            o_ref[0, :, 256 * h + 128:256 * h + 256] = fn(t, ct, s1, s2).astype(o_ref.dtype)

    row = pl.BlockSpec((1, ts, W), lambda b, s: (b, s, 0))
    tab = pl.BlockSpec((ts, LANE), lambda b, s: (s, 0))
    out = pl.pallas_call(
        body, name=name, grid=(B, S // ts), in_specs=[row, tab, tab, tab], out_specs=row,
        out_shape=jax.ShapeDtypeStruct((B, S, W), BF16),
        compiler_params=_params(_nbytes((ts, W), F32) * 2, _nbytes((ts, W), F32) * 2),
    )(q.reshape(B, S, W), *tabs)
    return out.reshape(T, W)


def kvprep_fwd(name, ckr, g, tabs, B, S):
    T, W = ckr.shape
    KVL = W - LANE
    ts = _tile(S, 256, 8)

    def body(x_ref, g_ref, ct_ref, s1_ref, s2_ref, c_ref, k_ref):
        xv = x_ref[0, :, :KVL]
        r = lax.rsqrt(jnp.mean(xv * xv, axis=-1, keepdims=True) + EPS)
        c_ref[0] = (xv * r * g_ref[...]).astype(c_ref.dtype)
        k_ref[0] = _rope_tile(x_ref[0, :, KVL:], ct_ref[...], s1_ref[...], s2_ref[...]).astype(k_ref.dtype)

    tab = pl.BlockSpec((ts, LANE), lambda b, s: (s, 0))
    c, k = pl.pallas_call(
        body, name=name, grid=(B, S // ts),
        in_specs=[pl.BlockSpec((1, ts, W), lambda b, s: (b, s, 0)), pl.BlockSpec((1, KVL), lambda b, s: (0, 0)),
                  tab, tab, tab],
        out_specs=[pl.BlockSpec((1, ts, KVL), lambda b, s: (b, s, 0)),
                   pl.BlockSpec((1, ts, LANE), lambda b, s: (b, s, 0))],
        out_shape=[jax.ShapeDtypeStruct((B, S, KVL), BF16), jax.ShapeDtypeStruct((B, S, LANE), BF16)],
        compiler_params=_params(_nbytes((ts, W), F32) * 2, _nbytes((ts, W), F32) * 2),
    )(ckr.reshape(B, S, W), g.reshape(1, KVL), *tabs)
    return c.reshape(T, KVL), k


def kvprep_bwd(name, ckr, g, dc, dkr, tabs, B, S):
    T, W = ckr.shape
    KVL = W - LANE
    ts = _tile(S, 256, 8)

    def body(x_ref, g_ref, dc_ref, dk_ref, ct_ref, s1_ref, s2_ref, o_ref, dg_ref):
        dx, dg = _rms_bwd_math(x_ref[0, :, :KVL], g_ref[...], dc_ref[0])
        o_ref[0, :, :KVL] = dx
        o_ref[0, :, KVL:] = _rope_tile_bwd(dk_ref[0], ct_ref[...], s1_ref[...], s2_ref[...])
        first = (pl.program_id(0) == 0) & (pl.program_id(1) == 0)

        @pl.when(first)
        def _():
            dg_ref[...] = dg

        @pl.when(jnp.logical_not(first))
        def _():
            dg_ref[...] += dg

    tab = pl.BlockSpec((ts, LANE), lambda b, s: (s, 0))
    vec = pl.BlockSpec((1, KVL), lambda b, s: (0, 0))
    o, dg = pl.pallas_call(
        body, name=name, grid=(B, S // ts),
        in_specs=[pl.BlockSpec((1, ts, W), lambda b, s: (b, s, 0)), vec,
                  pl.BlockSpec((1, ts, KVL), lambda b, s: (b, s, 0)),
                  pl.BlockSpec((1, ts, LANE), lambda b, s: (b, s, 0)), tab, tab, tab],
        out_specs=[pl.BlockSpec((1, ts, W), lambda b, s: (b, s, 0)), vec],
        out_shape=[jax.ShapeDtypeStruct((B, S, W), F32), jax.ShapeDtypeStruct((1, KVL), F32)],
        compiler_params=_params(_nbytes((ts, W), F32) * 4, _nbytes((ts, W), F32) * 4),
    )(ckr.reshape(B, S, W), g.reshape(1, KVL), dc.reshape(B, S, KVL), dkr, *tabs)
    return o.reshape(T, W), dg.reshape(KVL)


DIAGS = 768


def _diag_onehot():
    col = lax.broadcasted_iota(I32, (REL_PAD, DIAGS), 1)
    row = lax.broadcasted_iota(I32, (REL_PAD, DIAGS), 0)
    idx = jnp.clip(PADR + QROWS - 1 - col, -MAX_REL, MAX_REL) + MAX_REL
    return (row == idx).astype(F32)


def rel_bias_tile(name, table):
    H = table.shape[0]
    tpad = jnp.pad(table, ((0, 0), (0, REL_PAD - table.shape[1])))

    def body(t_ref, o_ref):
        g = lax.dot_general(t_ref[...], _diag_onehot(), NN, precision=lax.Precision.HIGHEST,
                            preferred_element_type=F32)
        qc = jnp.right_shift(lax.broadcasted_iota(I32, (QROWS, WIN), 0), CHUNK_SHIFT)
        kc = jnp.right_shift(lax.broadcasted_iota(I32, (QROWS, WIN), 1), CHUNK_SHIFT)
        band = (kc >= qc) & (kc <= qc + LEFT_CHUNKS)
        for h in range(H):
            gb = jnp.broadcast_to(g[h:h + 1, :], (QROWS, DIAGS))
            tile = pltpu.roll(gb, DIAGS - (QROWS - 1), 1, stride=1, stride_axis=0)
            o_ref[h // 2, (h % 2) * QROWS:(h % 2 + 1) * QROWS, :] = jnp.where(band, tile[:, :WIN], NEG_INF)

    return pl.pallas_call(
        body, name=name, out_shape=jax.ShapeDtypeStruct((H // 2, 2 * QROWS, WIN), F32),
        compiler_params=_params(0, 2 * _nbytes((H // 2, 2 * QROWS, WIN), F32)),
    )(tpad)


def rel_bias_grad(name, dbias):
    H = 2 * dbias.shape[0]

    def body(d_ref, o_ref):
        flip = (lax.broadcasted_iota(I32, (QROWS, QROWS), 0) + lax.broadcasted_iota(I32, (QROWS, QROWS), 1)
                == QROWS - 1).astype(F32)
        rows = []
        for h in range(H):
            x = d_ref[h // 2, (h % 2) * QROWS:(h % 2 + 1) * QROWS, :]
            xr = lax.dot_general(flip, x, NN, precision=lax.Precision.HIGHEST, preferred_element_type=F32)
            xp = jnp.concatenate([xr, jnp.zeros((QROWS, DIAGS - WIN), F32)], axis=1)
            y = pltpu.roll(xp, 0, 1, stride=1, stride_axis=0)
            rows.append(jnp.sum(y, axis=0, keepdims=True))
        o_ref[...] = lax.dot_general(jnp.concatenate(rows, axis=0), _diag_onehot(), NT,
                                     precision=lax.Precision.HIGHEST, preferred_element_type=F32)

    return pl.pallas_call(
        body, name=name, out_shape=jax.ShapeDtypeStruct((H, REL_PAD), F32),
        compiler_params=_params(0, 2 * _nbytes(dbias.shape, F32)),
    )(dbias)


def _stack_pair(xp):
    lane = lax.broadcasted_iota(I32, xp.shape, 1)
    z = jnp.zeros_like(xp)
    return jnp.concatenate([jnp.where(lane < HEAD_DIM_A, xp, z), jnp.where(lane >= HEAD_DIM_A, xp, z)], axis=0)


def _unstack_pair(y):
    lane = lax.broadcasted_iota(I32, (QROWS, LANE), 1)
    return jnp.where(lane < HEAD_DIM_A, y[:QROWS], y[QROWS:])


def _attn_a_rowpen(j):
    w = lax.broadcasted_iota(I32, (1, WIN), 1)
    return jnp.where(w >= PADR - QROWS * j, 0.0, NEG_INF).astype(F32)


def _attn_a_load_bias(bias_hbm, bias_v, sem):
    cp = pltpu.make_async_copy(bias_hbm, bias_v, sem)
    cp.start()
    cp.wait()


def _attn_a_load_kv(qkv_hbm, b, kpad, vpad, sem, S, D):
    kpad[0:PADR, :] = jnp.zeros((PADR, D), BF16)
    vpad[0:PADR, :] = jnp.zeros((PADR, D), BF16)
    ck = pltpu.make_async_copy(qkv_hbm.at[b, :, pl.ds(D, D)], kpad.at[pl.ds(PADR, S), :], sem.at[0])
    cv = pltpu.make_async_copy(qkv_hbm.at[b, :, pl.ds(2 * D, D)], vpad.at[pl.ds(PADR, S), :], sem.at[1])
    ck.start()
    cv.start()
    ck.wait()
    cv.wait()


def _attn_a_exp(q2s, kp, bias, pen):
    s = lax.dot_general(q2s, kp, NT, preferred_element_type=F32) + bias + pen
    e = jnp.exp(s - jnp.max(s, axis=-1, keepdims=True))
    return e, 1.0 / jnp.sum(e, axis=-1, keepdims=True)


def attn_a_fwd(name, qkv, bias):
    B, S, D3 = qkv.shape
    D = D3 // 3
    H = D // HEAD_DIM_A
    nb = S // QROWS
    scale = HEAD_DIM_A ** -0.5

    def body(q_ref, bias_hbm, qkv_hbm, o_ref, kpad, vpad, bias_v, sem):
        b, j = pl.program_id(0), pl.program_id(1)

        @pl.when((b == 0) & (j == 0))
        def _():
            _attn_a_load_bias(bias_hbm, bias_v, sem.at[2])

        @pl.when(j == 0)
        def _():
            _attn_a_load_kv(qkv_hbm, b, kpad, vpad, sem, S, D)

        pen = _attn_a_rowpen(j)
        w0 = pl.multiple_of(j * QROWS, QROWS)
        for p in range(H // 2):
            ls = slice(p * LANE, (p + 1) * LANE)
            e, rl = _attn_a_exp(_stack_pair(q_ref[0, :, ls] * scale), kpad[pl.ds(w0, WIN), ls], bias_v[p], pen)
            o2 = jnp.dot(e.astype(BF16), vpad[pl.ds(w0, WIN), ls], preferred_element_type=F32) * rl
            o_ref[0, :, ls] = _unstack_pair(o2).astype(o_ref.dtype)

    scr = 2 * _nbytes((PADR + S, D), BF16) + _nbytes(bias.shape, F32) + 8 * _nbytes((2 * QROWS, WIN), F32)
    return pl.pallas_call(
        body, name=name, grid=(B, nb),
        in_specs=[pl.BlockSpec((1, QROWS, D), lambda b, j: (b, j, 0)),
                  pl.BlockSpec(memory_space=pl.ANY), pl.BlockSpec(memory_space=pl.ANY)],
        out_specs=pl.BlockSpec((1, QROWS, D), lambda b, j: (b, j, 0)),
        out_shape=jax.ShapeDtypeStruct((B, S, D), BF16),
        scratch_shapes=[pltpu.VMEM((PADR + S, D), BF16), pltpu.VMEM((PADR + S, D), BF16),
                        pltpu.VMEM(bias.shape, F32), pltpu.SemaphoreType.DMA((3,))],
        compiler_params=_params(2 * _nbytes((QROWS, D), BF16), scr),
    )(qkv, bias, qkv)


def attn_a_bwd(name, qkv, do, bias):
    B, S, D3 = qkv.shape
    D = D3 // 3
    H = D // HEAD_DIM_A
    nb = S // QROWS
    scale = HEAD_DIM_A ** -0.5

    def body(q_ref, do_ref, bias_hbm, qkv_hbm, dqkv_hbm, dbias_hbm, kpad, vpad, dkacc, dvacc, bias_v, dbias_v,
             dq_stage, sem):
        b, j = pl.program_id(0), pl.program_id(1)
        step = b * nb + j
        slot = lax.rem(step, 2)

        def dq_out(s):
            return pltpu.make_async_copy(dq_stage.at[s], dqkv_hbm.at[b, pl.ds(j * QROWS, QROWS), pl.ds(0, D)],
                                         sem.at[3 + s])

        @pl.when(step >= 2)
        def _():
            dq_out(slot).wait()

        @pl.when((b == 0) & (j == 0))
        def _():
            _attn_a_load_bias(bias_hbm, bias_v, sem.at[2])
            dbias_v[...] = jnp.zeros_like(dbias_v)

        @pl.when(j == 0)
        def _():
            _attn_a_load_kv(qkv_hbm, b, kpad, vpad, sem, S, D)
            dkacc[...] = jnp.zeros_like(dkacc)
            dvacc[...] = jnp.zeros_like(dvacc)

        pen = _attn_a_rowpen(j)
        w0 = pl.multiple_of(j * QROWS, QROWS)
        for p in range(H // 2):
            ls = slice(p * LANE, (p + 1) * LANE)
            q2s = _stack_pair(q_ref[0, :, ls] * scale)
            do2 = _stack_pair(do_ref[0, :, ls])
            kp = kpad[pl.ds(w0, WIN), ls]
            vp = vpad[pl.ds(w0, WIN), ls]
            e, rl = _attn_a_exp(q2s, kp, bias_v[p], pen)
            pr = e * rl
            dp = lax.dot_general(do2, vp, NT, preferred_element_type=F32)
            ds = pr * (dp - jnp.sum(pr * dp, axis=-1, keepdims=True))
            dbias_v[p] += ds
            dsb = ds.astype(BF16)
            dq_stage[slot, :, ls] = _unstack_pair(jnp.dot(dsb, kp, preferred_element_type=F32)) * scale
            dkacc[pl.ds(w0, WIN), ls] += lax.dot_general(dsb, q2s, TN, preferred_element_type=F32)
            dvacc[pl.ds(w0, WIN), ls] += lax.dot_general(pr.astype(BF16), do2, TN, preferred_element_type=F32)

        dq_out(slot).start()

        @pl.when(j == nb - 1)
        def _():
            ck = pltpu.make_async_copy(dkacc.at[pl.ds(PADR, S), :], dqkv_hbm.at[b, :, pl.ds(D, D)], sem.at[0])
            cv = pltpu.make_async_copy(dvacc.at[pl.ds(PADR, S), :], dqkv_hbm.at[b, :, pl.ds(2 * D, D)], sem.at[1])
            ck.start()
            cv.start()
            ck.wait()
            cv.wait()

        @pl.when((b == B - 1) & (j == nb - 1))
        def _():
            cb = pltpu.make_async_copy(dbias_v, dbias_hbm, sem.at[2])
            cb.start()
            dq_out(0).wait()
            dq_out(1).wait()
            cb.wait()

    blk = _nbytes((QROWS, D), BF16) * 2
    scr = (2 * _nbytes((PADR + S, D), BF16) + 2 * _nbytes((PADR + S, D), F32) + 2 * _nbytes(bias.shape, F32)
           + 8 * _nbytes((2 * QROWS, WIN), F32) + 2 * _nbytes((QROWS, D), F32))
    return pl.pallas_call(
        body, name=name, grid=(B, nb),
        in_specs=[pl.BlockSpec((1, QROWS, D), lambda b, j: (b, j, 0)),
                  pl.BlockSpec((1, QROWS, D), lambda b, j: (b, j, 0)),
                  pl.BlockSpec(memory_space=pl.ANY), pl.BlockSpec(memory_space=pl.ANY)],
        out_specs=[pl.BlockSpec(memory_space=pl.ANY), pl.BlockSpec(memory_space=pl.ANY)],
        out_shape=[jax.ShapeDtypeStruct((B, S, 3 * D), F32), jax.ShapeDtypeStruct(bias.shape, F32)],
        scratch_shapes=[pltpu.VMEM((PADR + S, D), BF16), pltpu.VMEM((PADR + S, D), BF16),
                        pltpu.VMEM((PADR + S, D), F32), pltpu.VMEM((PADR + S, D), F32),
                        pltpu.VMEM(bias.shape, F32), pltpu.VMEM(bias.shape, F32),
                        pltpu.VMEM((2, QROWS, D), F32), pltpu.SemaphoreType.DMA((5,))],
        compiler_params=_params(blk, scr),
    )(qkv, do, bias, qkv)


def _mla_raw_t(k2, kj, q, QB):
    return lax.dot_general(k2[_blk(kj, QB), :], q, NT, preferred_element_type=F32)


def _blk(kj, QB):
    return pl.ds(kj * QB, QB) if isinstance(kj, int) else pl.ds(pl.multiple_of(kj * QB, QB), QB)


def _mla_diag_pen(QB):
    kc = jnp.right_shift(lax.broadcasted_iota(I32, (QB, QB), 0), CHUNK_SHIFT)
    qc = jnp.right_shift(lax.broadcasted_iota(I32, (QB, QB), 1), CHUNK_SHIFT)
    return jnp.where(kc <= qc, 0.0, NEG_INF).astype(F32)


def _mla_fill_keys(kv_ref, kr_ref, k2):
    k2[:, :NOPE] = kv_ref[0, :, :NOPE]
    k2[:, NOPE:] = kr_ref[0]


def _t(x):
    return x.astype(F32).T


def mla_fwd(name, qf, kv, kr):
    B, S, W = qf.shape
    HB = W // 256
    QB = _tile(S, 256, CHUNK)
    nq = S // QB
    scale = (NOPE + ROPE) ** -0.5

    def body(q_ref, kv_ref, kr_ref, o_ref, lse_ref, k2, vt, st_buf, pen):
        qi = pl.program_id(2)

        @pl.when(qi == 0)
        def _():
            pen[...] = _mla_diag_pen(QB)
            _mla_fill_keys(kv_ref, kr_ref, k2)
            for kj in range(nq):
                vt[kj] = _t(kv_ref[0, kj * QB:(kj + 1) * QB, NOPE:]).astype(BF16)

        q = q_ref[0]
        st_buf[0] = _mla_raw_t(k2, 0, q, QB)

        def step(kj, carry):
            m, l, acc = carry
            cur = lax.rem(kj, 2)
            st_raw = st_buf[cur]
            st_buf[1 - cur] = _mla_raw_t(k2, jnp.minimum(kj + 1, qi), q, QB)
            st = st_raw * scale + jnp.where(kj == qi, pen[...], 0.0)
            m_new = jnp.maximum(m, jnp.max(st, axis=0, keepdims=True))
            a = jnp.exp(m - m_new)
            pt = jnp.exp(st - m_new)
            l = a * l + jnp.sum(pt, axis=0, keepdims=True)
            acc = a * acc + jnp.dot(vt[kj], pt.astype(BF16), preferred_element_type=F32)
            return m_new, l, acc

        init = (jnp.full((1, QB), NEG_INF, F32), jnp.zeros((1, QB), F32), jnp.zeros((NOPE, QB), F32))
        m, l, acc = lax.fori_loop(0, qi + 1, step, init)
        o_ref[0] = (acc * (1.0 / l)).T
        lse_ref[0, 0] = m + jnp.log(l)

    blk = (_nbytes((QB, 256), BF16) + _nbytes((S, 256), BF16) + _nbytes((S, LANE), BF16)
           + _nbytes((QB, LANE), F32))
    return pl.pallas_call(
        body, name=name, grid=(B, HB, nq),
        in_specs=[pl.BlockSpec((1, QB, 256), lambda b, h, i: (b, i, h)),
                  pl.BlockSpec((1, S, 256), lambda b, h, i: (b, 0, h)),
                  pl.BlockSpec((1, S, LANE), lambda b, h, i: (b, 0, 0))],
        out_specs=[pl.BlockSpec((1, QB, LANE), lambda b, h, i: (b, i, h)),
                   pl.BlockSpec((1, 1, 1, QB), lambda b, h, i: (b, h, 0, i))],
        out_shape=[jax.ShapeDtypeStruct((B, S, HB * LANE), F32), jax.ShapeDtypeStruct((B, HB, 1, S), F32)],
        scratch_shapes=[pltpu.VMEM((S, 256), BF16), pltpu.VMEM((nq, NOPE, QB), BF16),
                        pltpu.VMEM((2, QB, QB), F32), pltpu.VMEM((QB, QB), F32)],
        compiler_params=_params(blk, 2 * _nbytes((S, 256), BF16) + 10 * _nbytes((QB, QB), F32)),
    )(qf, kv, kr)


def mla_bwd(name, qf, kv, kr, do, o, lse):
    B, S, W = qf.shape
    HB = W // 256
    QB = _tile(S, 256, CHUNK)
    nq = S // QB
    scale = (NOPE + ROPE) ** -0.5

    def body(q_ref, kv_ref, kr_ref, do_ref, o_ref, lse_ref, dq_ref, dkv_ref, dkr_ref, k2, kt, dot_, delta, dqt,
             st_buf, dp_buf, pen):
        h = pl.program_id(1)
        pen[...] = _mla_diag_pen(QB)
        dkv_ref[...] = jnp.zeros_like(dkv_ref)

        @pl.when(h == 0)
        def _():
            dkr_ref[...] = jnp.zeros_like(dkr_ref)

        _mla_fill_keys(kv_ref, kr_ref, k2)
        for i in range(nq):
            rows = slice(i * QB, (i + 1) * QB)
            kt[i] = _t(k2[rows, :]).astype(BF16)
            dot32 = _t(do_ref[0, rows, :])
            delta[i] = jnp.sum(dot32 * o_ref[0, rows, :].T, axis=0, keepdims=True)
            dot_[i] = dot32.astype(BF16)

        for qi in range(nq):
            rows = slice(qi * QB, (qi + 1) * QB)
            q = q_ref[0, rows, :]
            dob = do_ref[0, rows, :]
            lse_q = lse_ref[0, 0, :, rows]
            delta_q = delta[qi]
            dqt[...] = jnp.zeros_like(dqt)

            def raw(kj, slot, q=q, qi=qi):
                st_buf[slot] = _mla_raw_t(k2, kj, q, QB)
                dp_buf[slot] = jnp.dot(kv_ref[0, _blk(kj, QB), NOPE:], dot_[qi], preferred_element_type=F32)

            raw(0, 0)

            def step(kj, carry, q=q, dob=dob, lse_q=lse_q, delta_q=delta_q, qi=qi, raw=raw):
                ks = pl.ds(pl.multiple_of(kj * QB, QB), QB)
                cur = lax.rem(kj, 2)
                st_raw, dp_raw = st_buf[cur], dp_buf[cur]
                raw(jnp.minimum(kj + 1, qi), 1 - cur)
                pt = jnp.exp(st_raw * scale + jnp.where(kj == qi, pen[...], 0.0) - lse_q)
                dst = (pt * (dp_raw - delta_q) * scale).astype(BF16)
                dkv_ref[0, ks, NOPE:] += jnp.dot(pt.astype(BF16), dob, preferred_element_type=F32)
                dk2 = jnp.dot(dst, q, preferred_element_type=F32)
                dkv_ref[0, ks, :NOPE] += dk2[:, :NOPE]
                dkr_ref[0, ks, :] += dk2[:, NOPE:]
                dqt[...] += jnp.dot(kt[kj], dst, preferred_element_type=F32)
                return carry

            lax.fori_loop(0, qi + 1, step, 0)
            dq_ref[0, rows, :] = dqt[...].T

    head = lambda w: pl.BlockSpec((1, S, w), lambda b, h: (b, 0, h))
    shared = pl.BlockSpec((1, S, LANE), lambda b, h: (b, 0, 0))
    blk = (2 * _nbytes((S, 256), BF16) + 2 * _nbytes((S, LANE), BF16) + _nbytes((S, LANE), F32)
           + 2 * _nbytes((S, 256), F32) + _nbytes((S, LANE), F32))
    scr = 3 * _nbytes((S, 256), BF16) + 14 * _nbytes((QB, QB), F32)
    return pl.pallas_call(
        body, name=name, grid=(B, HB),
        in_specs=[head(256), head(256), shared, head(LANE), head(LANE),
                  pl.BlockSpec((1, 1, 1, S), lambda b, h: (b, h, 0, 0))],
        out_specs=[head(256), head(256), shared],
        out_shape=[jax.ShapeDtypeStruct((B, S, W), F32), jax.ShapeDtypeStruct((B, S, W), F32),
                   jax.ShapeDtypeStruct((B, S, LANE), F32)],
        scratch_shapes=[pltpu.VMEM((S, 256), BF16), pltpu.VMEM((nq, 256, QB), BF16),
                        pltpu.VMEM((nq, NOPE, QB), BF16), pltpu.VMEM((nq, 1, QB), F32),
                        pltpu.VMEM((256, QB), F32), pltpu.VMEM((2, QB, QB), F32), pltpu.VMEM((2, QB, QB), F32),
                        pltpu.VMEM((QB, QB), F32)],
        compiler_params=_params(blk, scr),
    )(qf, kv, kr, do, o, lse)


GROUP_STEPS = 4


def cast_group(name, ws, layers, idx, after=None):
    n = len(ws)
    n_in = n + (after is not None)

    def body(k_ref, *refs):
        for i in range(n):
            refs[n_in + i][...] = refs[i][...].astype(BF16)

    def spec_in(w, layer):
        return pl.BlockSpec((None, w.shape[1] // GROUP_STEPS, w.shape[2]), lambda r, k_ref: (layer, r, 0))

    def spec_out(w):
        return pl.BlockSpec((None, w.shape[1] // GROUP_STEPS, w.shape[2]), lambda r, k_ref: (k_ref[0], r, 0))

    return pl.pallas_call(
        body, name=name,
        grid_spec=pltpu.PrefetchScalarGridSpec(
            num_scalar_prefetch=1, grid=(GROUP_STEPS,),
            in_specs=([spec_in(w, l) for w, l in zip(ws, layers)]
                      + [pl.BlockSpec(memory_space=pl.ANY)] * (after is not None)),
            out_specs=[spec_out(w) for w in ws]),
        out_shape=[jax.ShapeDtypeStruct((N_CHIPS, *w.shape[1:]), BF16) for w in ws],
        compiler_params=_params(sum(_nbytes(w.shape[1:], F32) * 3 // 2 for w in ws) // GROUP_STEPS),
    )(idx, *ws, *([] if after is None else [after]))


def adamw(name, w, g, m, v):
    R, C = w.shape
    tr = _tile(R, max(8, (1 << 18) // C // 8 * 8), 8)
    c1 = 1.0 - ADAM_B1 ** ADAM_STEP
    c2 = 1.0 - ADAM_B2 ** ADAM_STEP

    def body(w_ref, g_ref, m_ref, v_ref, d_ref, mo_ref, vo_ref):
        gv = g_ref[...]
        mn = ADAM_B1 * m_ref[...] + (1.0 - ADAM_B1) * gv
        vn = ADAM_B2 * v_ref[...] + (1.0 - ADAM_B2) * (gv * gv)
        mo_ref[...] = mn
        vo_ref[...] = vn
        d_ref[...] = -ADAM_LR * ((mn / c1) / (jnp.sqrt(vn / c2) + ADAM_EPS) + ADAM_WD * w_ref[...])

    spec = pl.BlockSpec((tr, C), lambda r: (r, 0))
    return pl.pallas_call(
        body, name=name, grid=(R // tr,), in_specs=[spec] * 4, out_specs=[spec] * 3,
        out_shape=[jax.ShapeDtypeStruct((R, C), F32)] * 3,
        compiler_params=_params(7 * _nbytes((tr, C), F32), 4 * _nbytes((tr, C), F32)),
    )(w, g, m, v)


def half_sum_group(name, dws, landed, idx):
    n = len(dws)
    steps = GROUP_STEPS // 2

    def body(i_ref, *refs):
        for i in range(n):
            refs[2 * n + i][...] = (refs[i][...].astype(F32) + refs[n + i][...].astype(F32)).astype(BF16)

    def own(d):
        return pl.BlockSpec((None, None, d.shape[2] // steps, d.shape[3]), lambda k, r, i_ref: (k, i_ref[1], r, 0))

    def flat(d):
        return pl.BlockSpec((None, d.shape[2] // steps, d.shape[3]), lambda k, r, i_ref: (k, r, 0))

    return pl.pallas_call(
        body, name=name,
        grid_spec=pltpu.PrefetchScalarGridSpec(
            num_scalar_prefetch=1, grid=(N_CHIPS, steps),
            in_specs=[own(d) for d in dws] + [flat(d) for d in dws], out_specs=[flat(d) for d in dws]),
        out_shape=[jax.ShapeDtypeStruct((N_CHIPS, *d.shape[2:]), BF16) for d in dws],
        compiler_params=_params(sum(3 * _nbytes(d.shape[2:], BF16) for d in dws) // steps),
    )(idx, *dws, *landed)


def chip_sum_group(name, parts, landed, gbufs, layers, idx):
    n = len(parts)
    steps = GROUP_STEPS // 2

    def body(i_ref, *refs):
        for i in range(n):
            a, b = refs[i], refs[n + i]
            refs[3 * n + i][...] = ((a[...].astype(F32) + b[0].astype(F32)) + b[1].astype(F32)) + b[2].astype(F32)

    def mine(p):
        return pl.BlockSpec((None, p.shape[1] // steps, p.shape[2]), lambda r, i_ref: (i_ref[0], r, 0))

    def three(p):
        return pl.BlockSpec((3, p.shape[1] // steps, p.shape[2]), lambda r, i_ref: (0, r, 0))

    def out(p, layer):
        return pl.BlockSpec((None, None, p.shape[1] // steps, p.shape[2]), lambda r, i_ref: (layer, i_ref[1], r, 0))

    return pl.pallas_call(
        body, name=name,
        grid_spec=pltpu.PrefetchScalarGridSpec(
            num_scalar_prefetch=1, grid=(steps,),
            in_specs=[mine(p) for p in parts] + [three(p) for p in parts] + [pl.BlockSpec(memory_space=pl.ANY)] * n,
            out_specs=[out(p, l) for p, l in zip(parts, layers)]),
        out_shape=[jax.ShapeDtypeStruct(g.shape, F32) for g in gbufs],
        input_output_aliases={1 + 2 * n + i: i for i in range(n)},
        compiler_params=_params(sum(6 * _nbytes(p.shape[1:], BF16) for p in parts) // steps),
    )(idx, *parts, *landed, *gbufs)


ANY = pl.BlockSpec(memory_space=pl.ANY)


def _place():
    x, y, c = lax.axis_index("x"), lax.axis_index("y"), lax.axis_index("c")
    chips = [(1 - x, y), (x, 1 - y), (1 - x, 1 - y)]
    return x, y, c, chips


HBM = pl.BlockSpec(memory_space=pltpu.HBM)
SEM = pl.BlockSpec(memory_space=pltpu.SEMAPHORE)
EFFECT = pltpu.SideEffectType.DATAFLOW_SIDE_EFFECTING


def _in_hbm(a):
    return pltpu.with_memory_space_constraint(a, pltpu.HBM)


def _ici_copy(src, dst, send_sems, recv_sems, k, peer):
    return pltpu.make_async_remote_copy(src_ref=src, dst_ref=dst, send_sem=send_sems.at[k], recv_sem=recv_sems.at[k],
                                        device_id=peer, device_id_type=MESH)


def ici_start(name, bufs, lands, after, gather):
    n, nl = len(bufs), len(lands)

    def body(*refs):
        b_in = refs[:n]
        send_sems, recv_sems = refs[n + nl + 1], refs[n + nl + 2]
        b_out = refs[n + nl + 3:2 * n + nl + 3]
        l_out = refs[2 * n + nl + 3:2 * n + 2 * nl + 3]
        token = refs[-1]
        x, y, c, chips = _place()
        kme = 2 * x + y
        for i in range(n):
            for j in range(3):
                peer = (*chips[j], c)
                if gather:
                    _ici_copy(b_out[i].at[kme, c], b_out[i].at[kme, c], send_sems, recv_sems, 3 * i + j, peer).start()
                else:
                    kd = 2 * chips[j][0] + chips[j][1]
                    _ici_copy(b_out[i].at[kd], l_out[i].at[j], send_sems, recv_sems, 3 * i + j, peer).start()
        token[...] = jnp.zeros_like(token)

    arrays = [*bufs, *lands]
    outs = pl.pallas_call(
        body, name=name,
        in_specs=[HBM] * (n + nl) + [ANY],
        out_specs=(SEM, SEM, *[HBM] * (n + nl), pl.BlockSpec(memory_space=pltpu.VMEM)),
        out_shape=(pltpu.SemaphoreType.DMA((3 * n,)), pltpu.SemaphoreType.DMA((3 * n,)),
                   *[pltpu.HBM(a.shape, a.dtype) for a in arrays], jax.ShapeDtypeStruct((8, LANE), F32)),
        input_output_aliases={i: 2 + i for i in range(n + nl)},
        compiler_params=pltpu.CompilerParams(has_side_effects=EFFECT),
    )(*[_in_hbm(a) for a in arrays], after)
    return outs[0], outs[1], list(outs[2:2 + n]), list(outs[2 + n:2 + n + nl]), outs[-1]


def ici_wait(name, send_sems, recv_sems, bufs, lands, after, gather):
    n, nl = len(bufs), len(lands)

    def body(*refs):
        b_in, l_in = refs[:n], refs[n:n + nl]
        send_sems, recv_sems = refs[n + nl], refs[n + nl + 1]
        x, y, c, chips = _place()
        kme = 2 * x + y
        for i in range(n):
            for j in range(3):
                peer = (*chips[j], c)
                kj = 2 * chips[j][0] + chips[j][1]
                if gather:
                    _ici_copy(b_in[i].at[kme, c], b_in[i].at[kme, c], send_sems, recv_sems, 3 * i + j, peer).wait_send()
                    _ici_copy(b_in[i].at[kj, c], b_in[i].at[kj, c], send_sems, recv_sems, 3 * i + j, peer).wait_recv()
                else:
                    _ici_copy(b_in[i].at[kj], l_in[i].at[j], send_sems, recv_sems, 3 * i + j, peer).wait_send()
                    _ici_copy(b_in[i].at[kj], l_in[i].at[j], send_sems, recv_sems, 3 * i + j, peer).wait_recv()

    arrays = [*bufs, *lands]
    outs = pl.pallas_call(
        body, name=name,
        in_specs=[HBM] * (n + nl) + [SEM, SEM, ANY],
        out_specs=tuple([HBM] * (n + nl)),
        out_shape=tuple(pltpu.HBM(a.shape, a.dtype) for a in arrays),
        input_output_aliases={i: i for i in range(n + nl)},
        compiler_params=pltpu.CompilerParams(has_side_effects=EFFECT),
    )(*arrays, send_sems, recv_sems, after)
    return list(outs[:n]), list(outs[n:])


def gather_pair_pass(name, bufs):
    n = len(bufs)

    def body(*refs):
        b = refs[n:2 * n]
        send_sems, recv_sems = refs[2 * n:]
        x, y, c, chips = _place()
        sib = (x, y, 1 - c)

        def d2d(i, j, which):
            kj = 2 * chips[j][0] + chips[j][1]
            return _ici_copy(b[i].at[kj, which], b[i].at[kj, which], send_sems, recv_sems, 3 * i + j, sib)

        for i in range(n):
            for j in range(3):
                d2d(i, j, c).start()
        for i in range(n):
            for j in range(3):
                d2d(i, j, 1 - c).wait_recv()
        for i in range(n):
            for j in range(3):
                d2d(i, j, c).wait_send()

    return pl.pallas_call(
        body, name=name, in_specs=[ANY] * n, out_specs=[ANY] * n,
        out_shape=[jax.ShapeDtypeStruct(a.shape, a.dtype) for a in bufs],
        input_output_aliases={i: i for i in range(n)},
        scratch_shapes=[pltpu.SemaphoreType.DMA((3 * n,)), pltpu.SemaphoreType.DMA((3 * n,))],
    )(*bufs)


def pair_exchange(name, dws):
    n = len(dws)

    def body(*refs):
        ins, outs = refs[:n], refs[n:2 * n]
        send_sems, recv_sems = refs[2 * n:]
        x, y, c, _ = _place()
        copies = []
        for i in range(n):
            copies.append(pltpu.make_async_remote_copy(
                src_ref=ins[i].at[:, 1 - c], dst_ref=outs[i],
                send_sem=send_sems.at[i], recv_sem=recv_sems.at[i],
                device_id=(x, y, 1 - c), device_id_type=MESH))
            copies[i].start()
        for cp in copies:
            cp.wait_recv()
        for cp in copies:
            cp.wait_send()

    return pl.pallas_call(
        body, name=name, in_specs=[ANY] * n, out_specs=[ANY] * n,
        out_shape=[jax.ShapeDtypeStruct((N_CHIPS, *d.shape[2:]), d.dtype) for d in dws],
        scratch_shapes=[pltpu.SemaphoreType.DMA((n,)), pltpu.SemaphoreType.DMA((n,))],
    )(*dws)


def pair_assemble(gbufs):
    n = len(gbufs)

    def body(*refs):
        bufs = refs[n:2 * n]
        send_sems, recv_sems = refs[2 * n:]
        x, y, c, _ = _place()
        copies = []
        for i in range(n):
            copies.append(pltpu.make_async_remote_copy(
                src_ref=bufs[i].at[:, c], dst_ref=bufs[i].at[:, c],
                send_sem=send_sems.at[i], recv_sem=recv_sems.at[i],
                device_id=(x, y, 1 - c), device_id_type=MESH))
            copies[i].start()
        for i in range(n):
            pltpu.make_async_remote_copy(
                src_ref=bufs[i].at[:, 1 - c], dst_ref=bufs[i].at[:, 1 - c],
                send_sem=send_sems.at[i], recv_sem=recv_sems.at[i],
                device_id=(x, y, 1 - c), device_id_type=MESH).wait_recv()
        for cp in copies:
            cp.wait_send()

    return pl.pallas_call(
        body, name="grad_pair_assemble", in_specs=[ANY] * n, out_specs=[ANY] * n,
        out_shape=[jax.ShapeDtypeStruct(g.shape, g.dtype) for g in gbufs],
        input_output_aliases={i: i for i in range(n)},
        scratch_shapes=[pltpu.SemaphoreType.DMA((n,)), pltpu.SemaphoreType.DMA((n,))],
    )(*gbufs)


def all_reduce_small(vec):
    NR = vec.shape[0]
    flips = [(fx, fy, fc) for fx in (0, 1) for fy in (0, 1) for fc in (0, 1)][1:]

    def body(v_ref, o_ref, buf, send_sems, recv_sems):
        x, y, c, _ = _place()
        me = 4 * x + 2 * y + c
        buf[me] = v_ref[...]
        copies = []
        for j, (fx, fy, fc) in enumerate(flips):
            peer = (1 - x if fx else x, 1 - y if fy else y, 1 - c if fc else c)
            copies.append(pltpu.make_async_remote_copy(
                src_ref=v_ref, dst_ref=buf.at[me], send_sem=send_sems.at[j], recv_sem=recv_sems.at[j],
                device_id=peer, device_id_type=MESH))
            copies[j].start()
        for cp in copies:
            cp.wait_recv()
        for cp in copies:
            cp.wait_send()
        acc = buf[0]
        for d in range(1, 8):
            acc = acc + buf[d]
        o_ref[...] = acc

    return pl.pallas_call(
        body, name="all_reduce_small",
        in_specs=[pl.BlockSpec(memory_space=pltpu.VMEM)], out_specs=pl.BlockSpec(memory_space=pltpu.VMEM),
        out_shape=jax.ShapeDtypeStruct((NR, LANE), F32),
        scratch_shapes=[pltpu.VMEM((8, NR, LANE), F32), pltpu.SemaphoreType.DMA((7,)),
                        pltpu.SemaphoreType.DMA((7,))],
    )(vec)


def _pack(arrays):
    flat = jnp.concatenate([a.reshape(-1).astype(F32) for a in arrays])
    n = flat.shape[0]
    npad = -(-n // (8 * LANE)) * (8 * LANE)
    return jnp.pad(flat, (0, npad - n)).reshape(npad // LANE, LANE)


def _unpack(buf, like):
    flat = buf.reshape(-1)
    out, off = [], 0
    for a in like:
        out.append(flat[off:off + a.size].reshape(a.shape))
        off += a.size
    return out


def kernel(x, ffn1_norm, ffn1_w_in, ffn1_w_out, mix_norm, ffn2_norm, ffn2_w_in, ffn2_w_out, a_w_qkv, a_rel_bias, a_w_o, kv_norm, kv_w_down, kv_latent_norm, kv_w_up, b_w_dq, b_q_norm, b_w_uq, b_w_o, final_norm, loss_target, m_ffn1_norm, m_ffn1_w_in, m_ffn1_w_out, m_mix_norm, m_ffn2_norm, m_ffn2_w_in, m_ffn2_w_out, m_a_w_qkv, m_a_rel_bias, m_a_w_o, m_kv_norm, m_kv_w_down, m_kv_latent_norm, m_kv_w_up, m_b_w_dq, m_b_q_norm, m_b_w_uq, m_b_w_o, m_final_norm, v_ffn1_norm, v_ffn1_w_in, v_ffn1_w_out, v_mix_norm, v_ffn2_norm, v_ffn2_w_in, v_ffn2_w_out, v_a_w_qkv, v_a_rel_bias, v_a_w_o, v_kv_norm, v_kv_w_down, v_kv_latent_norm, v_kv_w_up, v_b_w_dq, v_b_q_norm, v_b_w_uq, v_b_w_o, v_final_norm):
    B, S, D = x.shape
    T = B * S
    HB = D // 128
    QL = b_q_norm.shape[-1]
    KVL = kv_latent_norm.shape[0]
    hpc = HB // N_CHIPS
    tabs = rope_tables(S)
    idx = jnp.stack([2 * lax.axis_index("x") + lax.axis_index("y"), lax.axis_index("c")]).astype(I32)

    def halves(a):
        return a.reshape(*a.shape[:-2], 2, a.shape[-2] // 2, a.shape[-1])

    def whole(a):
        return a.reshape(*a.shape[:-3], 2 * a.shape[-2], a.shape[-1])

    kv_w_down_p = jnp.pad(kv_w_down, ((0, 0), (0, LANE - ROPE)))[None]
    b_w_uq_p = jnp.pad(b_w_uq.reshape(1, QL, hpc, NOPE + ROPE),
                       ((0, 0), (0, 0), (0, 0), (0, LANE - ROPE))).reshape(1, QL, hpc * 256)
    sharded = [("ffn1_w_in", ffn1_w_in), ("ffn1_w_out", ffn1_w_out), ("ffn2_w_in", ffn2_w_in),
               ("ffn2_w_out", ffn2_w_out), ("a_w_qkv", a_w_qkv), ("a_w_o", a_w_o),
               ("kv_w_down", kv_w_down_p), ("kv_w_up", kv_w_up[None]), ("b_w_dq", b_w_dq),
               ("b_w_uq", b_w_uq_p), ("b_w_o", b_w_o)]
    pieces = [(a, l) for a, (_, w) in enumerate(sharded) for l in range(w.shape[0])]
    names = [nm for nm, _ in sharded]
    shard_of = dict(sharded)
    W = {}

    gather_groups = [
        [("ffn1_w_in", 0), ("ffn1_w_out", 0)],
        [("a_w_qkv", 0), ("a_w_o", 0), ("ffn2_w_in", 0), ("ffn2_w_out", 0), ("kv_w_down", 0), ("kv_w_up", 0)],
        [("ffn1_w_in", 1), ("ffn1_w_out", 1), ("b_w_dq", 0), ("b_w_uq", 0), ("b_w_o", 0), ("ffn2_w_in", 1),
         ("ffn2_w_out", 1)]]

    own = {}

    def cast(g, after=None):
        keys = gather_groups[g]
        own.update(zip(keys, cast_group(f"cast_group_{g}", [shard_of[nm] for nm, _ in keys], [l for _, l in keys],
                                        idx, after=after)))

    def gather_start(g, after):
        keys = gather_groups[g]
        ss, rs, bufs, _, token = ici_start(f"gather_start_{g}", [halves(own[k]) for k in keys], [], after, True)
        return (g, ss, rs, bufs), token

    def gather_finish(state, after):
        g, ss, rs, bufs = state
        bufs, _ = ici_wait(f"gather_wait_{g}", ss, rs, bufs, [], after, True)
        full = gather_pair_pass(f"gather_pair_{g}", bufs)
        for k, w in zip(gather_groups[g], full):
            W[k] = whole(w)
        return full[0]

    def tied(a, token):
        return a + token[0, 0]

    def col(nm, l=0):
        return W[(nm, l)]

    def row(nm, l=0):
        w = W[(nm, l)]
        return w.reshape(N_CHIPS * w.shape[1], w.shape[2])

    bias = rel_bias_tile("rel_bias_tile", a_rel_bias[0])

    def ffn_fwd(tag, h, g, w_in, w_out, xn=None, after=None):
        if xn is None:
            xn = rms_fwd(f"{tag}_norm", h, g)
        u, act = ffn_in_act(f"{tag}_in", xn, w_in, after=after)
        return mm_roww(f"{tag}_out", act, w_out, F32, res=h, alpha=0.5), (xn, u, act)

    h0 = x.reshape(T, D)
    cast(0)
    st0, tok0 = gather_start(0, h0)
    cast(1, tok0)
    cast(2, tok0)
    xn0 = rms_fwd("l0f1_norm", h0, tied(ffn1_norm[0], tok0))
    done0 = gather_finish(st0, xn0)
    st1, tok1 = gather_start(1, done0)
    h1, sv_f1a = ffn_fwd("l0f1", h0, None, col("ffn1_w_in", 0), row("ffn1_w_out", 0), xn=xn0, after=tok1)
    done1 = gather_finish(st1, h1)
    st2, tok2 = gather_start(2, done1)
    hn_a = rms_fwd("l0mix_norm", h1, tied(mix_norm[0], tok2))
    qkv = mm_colw("l0_qkv", hn_a, col("a_w_qkv"), BF16).reshape(B, S, 3 * D)
    o_a = attn_a_fwd("l0_attn", qkv, bias).reshape(T, D)
    h2 = mm_roww("l0_attn_out", o_a, row("a_w_o"), F32, res=h1)
    h3, sv_f2a = ffn_fwd("l0f2", h2, ffn2_norm[0], col("ffn2_w_in", 0), row("ffn2_w_out", 0))

    hkv = rms_fwd("kv_norm", h3, kv_norm)
    ckr = mm_roww("kv_down", hkv, row("kv_w_down"), F32)
    ckv, kr = kvprep_fwd("kv_prep", ckr, kv_latent_norm, tabs, B, S)
    kvb = mm_colw("kv_up", ckv, col("kv_w_up"), BF16).reshape(B, S, HB * 256)
    gather_finish(st2, kvb)

    h4, sv_f1b = ffn_fwd("l1f1", h3, ffn1_norm[1], col("ffn1_w_in", 1), row("ffn1_w_out", 1))
    hn_b = rms_fwd("l1mix_norm", h4, mix_norm[1])
    cqp = mm_roww("l1_dq", hn_b, row("b_w_dq"), F32)
    cq = rms_fwd("l1_q_norm", cqp, b_q_norm[0])
    qpre = mm_colw("l1_uq", cq, col("b_w_uq"), F32)
    qf = qprep("l1_q_rope", qpre, tabs, B, S, bwd=False).reshape(B, S, HB * 256)
    o_b, lse = mla_fwd("l1_attn", qf, kvb, kr)
    h5 = mm_roww("l1_attn_out", o_b.reshape(T, HB * LANE), row("b_w_o"), F32, res=h4)
    h6, sv_f2b = ffn_fwd("l1f2", h5, ffn2_norm[1], col("ffn2_w_in", 1), row("ffn2_w_out", 1))

    dh, g_final, loss_part = loss_head("loss_head", h6, final_norm, loss_target.reshape(T, D))

    gw = {}
    gbufs = {nm: lax.empty(halves(w).shape, F32) for nm, w in sharded}

    def reduce_start(r, keys, after):
        dws = [halves(gw[k]) for k in keys]
        landed = pair_exchange(f"grad_pair_exchange_{r}", dws)
        parts = half_sum_group(f"half_sum_{r}", dws, landed, idx)
        lands = [lax.empty((3, *p.shape[1:]), p.dtype) for p in parts]
        ss, rs, parts, lands, token = ici_start(f"reduce_start_{r}", parts, lands, after, False)
        return (r, keys, ss, rs, parts, lands), token

    def reduce_finish(state, after):
        r, keys, ss, rs, parts, lands = state
        parts, lands = ici_wait(f"reduce_wait_{r}", ss, rs, parts, lands, after, False)
        done = chip_sum_group(f"chip_sum_{r}", parts, lands, [gbufs[nm] for nm, _ in keys], [l for _, l in keys], idx)
        gbufs.update(zip([nm for nm, _ in keys], done))
        return done[0]

    def ffn_bwd(tag, dh, h_in, g, w_in, w_out, saved, key_in, key_out, after=None, then=None):
        xn, u, act = saved
        du = ffn_dact(f"{tag}_dact", dh, w_out, u, after=after)
        dwo = mm_droww(f"{tag}_dwout", act, dh, alpha=0.5)
        gw[key_out] = dwo.reshape(N_CHIPS, dwo.shape[0] // N_CHIPS, dwo.shape[1])
        gw[key_in] = mm_dcolw(f"{tag}_dwin", xn, du, pair_layout=True)
        token = then(du) if then is not None else None
        return dx_norm_bwd(f"{tag}_dxn", du, w_in, h_in, g, dres=dh, pair_layout=True, after=token)

    def chip_major(dw):
        return dw.reshape(N_CHIPS, dw.shape[0] // N_CHIPS, dw.shape[1])

    dh, g_f2b = ffn_bwd("l1f2b", dh, h5, ffn2_norm[1], col("ffn2_w_in", 1), row("ffn2_w_out", 1), sv_f2b,
                        ("ffn2_w_in", 1), ("ffn2_w_out", 1))
    red0, rtok0 = reduce_start(0, [("ffn2_w_in", 1), ("ffn2_w_out", 1)], dh)
    do_b = mm_roww_t("l1_attn_do", dh, row("b_w_o"), BF16, after=rtok0).reshape(B, S, HB * LANE)
    gw[("b_w_o", 0)] = chip_major(mm_droww("l1_attn_dwo", o_b.reshape(T, HB * LANE), dh))
    dqf, dkv, dkr = mla_bwd("l1_attn_bwd", qf, kvb, kr, do_b, o_b, lse)
    dqpre = qprep("l1_q_rope_bwd", dqf.reshape(T, HB * 256), tabs, B, S, bwd=True)
    gw[("b_w_uq", 0)] = mm_dcolw("l1_dwuq", cq, dqpre)
    dcqp, g_qn = dx_norm_bwd("l1_dcq", dqpre, col("b_w_uq"), cqp, b_q_norm[0])
    gw[("b_w_dq", 0)] = chip_major(mm_droww("l1_dwdq", hn_b, dcqp))
    dhn = mm_roww_t("l1_dhn", dcqp, row("b_w_dq"), F32)
    dh, g_mixb = rms_bwd("l1_dmix", h4, mix_norm[1], dhn, dres=dh)
    dh, g_f1b = ffn_bwd("l1f1b", dh, h3, ffn1_norm[1], col("ffn1_w_in", 1), row("ffn1_w_out", 1), sv_f1b,
                        ("ffn1_w_in", 1), ("ffn1_w_out", 1))
    fin0 = reduce_finish(red0, dh)
    red1, rtok1 = reduce_start(1, [("b_w_o", 0), ("b_w_uq", 0), ("b_w_dq", 0), ("ffn1_w_in", 1), ("ffn1_w_out", 1)], fin0)
    dkv2 = dkv.reshape(T, HB * 256)
    gw[("kv_w_up", 0)] = mm_dcolw("kv_dwup", ckv, dkv2, after=rtok1)
    dckv = mm_colw_t("kv_dckv", dkv2, col("kv_w_up"), F32, after=rtok1)
    dckr, g_lat = kvprep_bwd("kv_prep_bwd", ckr, kv_latent_norm, dckv, dkr, tabs, B, S)
    gw[("kv_w_down", 0)] = chip_major(mm_droww("kv_dwdown", hkv, dckr))
    dhkv = mm_roww_t("kv_dhkv", dckr, row("kv_w_down"), F32)
    dh, g_kvn = rms_bwd("kv_dnorm", h3, kv_norm, dhkv, dres=dh)
    dh, g_f2a = ffn_bwd("l0f2b", dh, h2, ffn2_norm[0], col("ffn2_w_in", 0), row("ffn2_w_out", 0), sv_f2a,
                        ("ffn2_w_in", 0), ("ffn2_w_out", 0))
    do_a = mm_roww_t("l0_attn_do", dh, row("a_w_o"), BF16).reshape(B, S, D)
    gw[("a_w_o", 0)] = chip_major(mm_droww("l0_attn_dwo", o_a, dh))
    dqkv, dbias = attn_a_bwd("l0_attn_bwd", qkv, do_a, bias)
    dqkv = dqkv.reshape(T, 3 * D)
    gw[("a_w_qkv", 0)] = mm_dcolw("l0_dwqkv", hn_a, dqkv)
    dh, g_mixa = dx_norm_bwd("l0_dhn", dqkv, col("a_w_qkv"), h1, mix_norm[0], dres=dh)
    fin1 = reduce_finish(red1, dh)
    red2, rtok2 = reduce_start(2, [("kv_w_up", 0), ("kv_w_down", 0), ("ffn2_w_in", 0), ("ffn2_w_out", 0),
                                   ("a_w_o", 0), ("a_w_qkv", 0)], fin1)
    last = {}

    def last_group(du):
        fin2 = reduce_finish(red2, du)
        last["red"], token = reduce_start(3, [("ffn1_w_in", 0), ("ffn1_w_out", 0)], fin2)
        return token

    dh, g_f1a = ffn_bwd("l0f1b", dh, h0, ffn1_norm[0], col("ffn1_w_in", 0), row("ffn1_w_out", 0), sv_f1a,
                        ("ffn1_w_in", 0), ("ffn1_w_out", 0), after=rtok2, then=last_group)
    grad_x = dh.reshape(B, S, D)
    g_rel = rel_bias_grad("rel_bias_grad", dbias)[:, :2 * MAX_REL + 1][None]
    reduce_finish(last["red"], dh)

    full = [whole(g) for g in pair_assemble([gbufs[nm] for nm in names])]
    G = {nm: g for (nm, _), g in zip(sharded, full)}
    G["kv_w_down"] = G["kv_w_down"][0, :, :KVL + ROPE]
    G["kv_w_up"] = G["kv_w_up"][0]
    G["b_w_uq"] = G["b_w_uq"].reshape(1, QL, hpc, 256)[..., :NOPE + ROPE].reshape(b_w_uq.shape)

    small = [("ffn1_norm", jnp.stack([g_f1a, g_f1b])), ("mix_norm", jnp.stack([g_mixa, g_mixb])),
             ("ffn2_norm", jnp.stack([g_f2a, g_f2b])), ("a_rel_bias", g_rel), ("kv_norm", g_kvn),
             ("kv_latent_norm", g_lat), ("b_q_norm", g_qn[None]), ("final_norm", g_final)]
    red = all_reduce_small(_pack([loss_part] + [g for _, g in small]))
    unpacked = _unpack(red, [loss_part] + [g for _, g in small])
    loss = unpacked[0][0, 0]
    for (nm, _), g in zip(small, unpacked[1:]):
        G[nm] = g

    given = dict(ffn1_norm=(ffn1_norm, m_ffn1_norm, v_ffn1_norm), ffn1_w_in=(ffn1_w_in, m_ffn1_w_in, v_ffn1_w_in),
                 ffn1_w_out=(ffn1_w_out, m_ffn1_w_out, v_ffn1_w_out), mix_norm=(mix_norm, m_mix_norm, v_mix_norm),
                 ffn2_norm=(ffn2_norm, m_ffn2_norm, v_ffn2_norm), ffn2_w_in=(ffn2_w_in, m_ffn2_w_in, v_ffn2_w_in),
                 ffn2_w_out=(ffn2_w_out, m_ffn2_w_out, v_ffn2_w_out), a_w_qkv=(a_w_qkv, m_a_w_qkv, v_a_w_qkv),
                 a_rel_bias=(a_rel_bias, m_a_rel_bias, v_a_rel_bias), a_w_o=(a_w_o, m_a_w_o, v_a_w_o),
                 kv_norm=(kv_norm, m_kv_norm, v_kv_norm), kv_w_down=(kv_w_down, m_kv_w_down, v_kv_w_down),
                 kv_latent_norm=(kv_latent_norm, m_kv_latent_norm, v_kv_latent_norm),
                 kv_w_up=(kv_w_up, m_kv_w_up, v_kv_w_up), b_w_dq=(b_w_dq, m_b_w_dq, v_b_w_dq),
                 b_q_norm=(b_q_norm, m_b_q_norm, v_b_q_norm), b_w_uq=(b_w_uq, m_b_w_uq, v_b_w_uq),
                 b_w_o=(b_w_o, m_b_w_o, v_b_w_o), final_norm=(final_norm, m_final_norm, v_final_norm))
    order = list(given)
    delta, new_m, new_v = {}, {}, {}
    small_names = [nm for nm, _ in small]
    packed = [_pack([given[nm][k] for nm in small_names]) for k in range(3)]
    outs = adamw("adamw_small", packed[0], _pack([G[nm] for nm in small_names]), packed[1], packed[2])
    for dst, buf in zip((delta, new_m, new_v), outs):
        for nm, a in zip(small_names, _unpack(buf, [given[nm][0] for nm in small_names])):
            dst[nm] = a
    for nm, _ in sharded:
        w, m, v = given[nm]
        g = G[nm].reshape(w.shape)
        G[nm] = g
        two = lambda a: a.reshape(-1, a.shape[-1])
        d_, m_, v_ = adamw(f"adamw_{nm}", two(w), two(g), two(m), two(v))
        delta[nm], new_m[nm], new_v[nm] = d_.reshape(w.shape), m_.reshape(w.shape), v_.reshape(w.shape)

    return (loss, grad_x, *[G[n] for n in order], *[delta[n] for n in order],
            *[new_m[n] for n in order], *[new_v[n] for n in order])
```

```python
import functools
import math

import jax
import jax.numpy as jnp
from jax import lax
from jax.experimental import pallas as pl
from jax.experimental.pallas import tpu as pltpu

F32 = jnp.float32
BF16 = jnp.bfloat16
I32 = jnp.int32

CHUNK = 64
CHUNK_SHIFT = 6
HEAD_DIM_A = 64
LEFT_CHUNKS = 8
MAX_REL = 128
REL_PAD = 384
QROWS = 2 * CHUNK
WIN = (LEFT_CHUNKS + 2) * CHUNK
PADR = LEFT_CHUNKS * CHUNK
NOPE = 128
ROPE = 64
EPS = 1e-6
NEG_INF = -1e30
ROPE_THETA = 10000.0
ADAM_LR, ADAM_B1, ADAM_B2, ADAM_EPS, ADAM_WD, ADAM_STEP = 0.001, 0.9, 0.999, 1e-08, 0.01, 10
N_CHIPS = 4
LANE = 128
MESH = pl.DeviceIdType.MESH
VMEM_CAP_MB = 60

NN = (((1,), (0,)), ((), ()))
NT = (((1,), (1,)), ((), ()))
TN = (((0,), (0,)), ((), ()))


def _tile(n, pref, mult):
    t = (min(pref, n) // mult) * mult
    while t >= mult:
        if n % t == 0:
            return t
        t -= mult
    return n


def _nbytes(shape, dtype):
    return math.prod(shape) * jnp.dtype(dtype).itemsize


def _params(block_bytes, extra_bytes=0):
    need = 2 * block_bytes + extra_bytes
    mb = min(VMEM_CAP_MB, max(32, int(need * 1.25 / 2**20) + 8))
    return pltpu.CompilerParams(vmem_limit_bytes=mb * 2**20)


def _mm(name, kind, a, b, grid, a_spec, b_spec, o_spec, out_shape, out_dtype, blocks,
        red_axis=None, nred=1, alpha=1.0, res=None, res_spec=None, after=None):
    dims = {"nn": NN, "nt": NT, "tn": TN}[kind]
    has_res = res is not None
    acc_in_out = nred > 1 and out_dtype == F32 and not has_res and alpha == 1.0
    n_in = 2 + has_res + (after is not None)

    def body(*refs):
        a_ref, b_ref = refs[0], refs[1]
        r_ref = refs[2] if has_res else None
        o_ref = refs[n_in]
        p = lax.dot_general(a_ref[...].astype(BF16), b_ref[...].astype(BF16), dims,
                            preferred_element_type=F32)

        def finish(acc):
            y = acc if alpha == 1.0 else acc * alpha
            if has_res:
                y = r_ref[...] + y
            o_ref[...] = y.astype(o_ref.dtype)

        if nred == 1:
            finish(p)
            return
        k = pl.program_id(red_axis)
        tgt = o_ref if acc_in_out else refs[-1]

        @pl.when(k == 0)
        def _():
            tgt[...] = p

        @pl.when(k > 0)
        def _():
            tgt[...] += p

        if not acc_in_out:
            @pl.when(k == nred - 1)
            def _():
                finish(tgt[...])

    a_blk, b_blk, o_blk = blocks
    scratch = []
    extra = 0
    if nred > 1 and not acc_in_out:
        scratch = [pltpu.VMEM(o_blk, F32)]
        extra = _nbytes(o_blk, F32)
    blk = _nbytes(a_blk, a.dtype) + _nbytes(b_blk, b.dtype) + _nbytes(o_blk, out_dtype)
    ins, specs = [a, b], [a_spec, b_spec]
    if has_res:
        ins.append(res)
        specs.append(res_spec)
        blk += _nbytes(o_blk, res.dtype)
    if after is not None:
        ins.append(after)
        specs.append(pl.BlockSpec(memory_space=pl.ANY))
    extra += _nbytes(a_blk, BF16) + _nbytes(b_blk, BF16) + 2 * _nbytes(o_blk, F32)
    return pl.pallas_call(
        body, name=name, grid=grid, in_specs=specs, out_specs=o_spec,
        out_shape=jax.ShapeDtypeStruct(out_shape, out_dtype), scratch_shapes=scratch,
        compiler_params=_params(blk, extra),
    )(*ins)


def mm_colw(name, x, w3, out_dtype):
    T, K = x.shape
    _, _, nl = w3.shape
    tm = _tile(T, 512, 8)
    return _mm(name, "nn", x, w3, (N_CHIPS, T // tm),
               pl.BlockSpec((tm, K), lambda j, i: (i, 0)),
               pl.BlockSpec((None, K, nl), lambda j, i: (j, 0, 0)),
               pl.BlockSpec((tm, nl), lambda j, i: (i, j)),
               (T, N_CHIPS * nl), out_dtype, ((tm, K), (K, nl), (tm, nl)))


def _pair_chip(j):
    return (j % 2) * 2 + j // 2


def mm_colw_t(name, dy, w3, out_dtype, res=None, after=None, pair_layout=False):
    T = dy.shape[0]
    _, K, nl = w3.shape
    tm = _tile(T, 1024, 8)
    chip = _pair_chip if pair_layout else (lambda j: j)
    return _mm(name, "nt", dy, w3, (T // tm, N_CHIPS),
               pl.BlockSpec((tm, nl), lambda i, j: (i, j)),
               pl.BlockSpec((None, K, nl), lambda i, j: (chip(j), 0, 0)),
               pl.BlockSpec((tm, K), lambda i, j: (i, 0)),
               (T, K), out_dtype, ((tm, nl), (K, nl), (tm, K)),
               red_axis=1, nred=N_CHIPS, res=res,
               res_spec=pl.BlockSpec((tm, K), lambda i, j: (i, 0)), after=after)


def mm_dcolw(name, x, dy, after=None, pair_layout=False):
    T, K = x.shape
    nl = dy.shape[1] // N_CHIPS
    tt = _tile(T, 2048, 8)
    chip = _pair_chip if pair_layout else (lambda j: j)
    return _mm(name, "tn", x, dy, (N_CHIPS, T // tt),
               pl.BlockSpec((tt, K), lambda j, t: (t, 0)),
               pl.BlockSpec((tt, nl), lambda j, t: (t, j)),
               pl.BlockSpec((None, K, nl), lambda j, t: (chip(j), 0, 0)),
               (N_CHIPS, K, nl), BF16, ((tt, K), (tt, nl), (K, nl)),
               red_axis=1, nred=T // tt, after=after)


def mm_roww(name, x, w2, out_dtype, res=None, alpha=1.0):
    T, Kt = x.shape
    N = w2.shape[1]
    tm = _tile(T, 512, 8)
    return _mm(name, "nn", x, w2, (T // tm,),
               pl.BlockSpec((tm, Kt), lambda i: (i, 0)),
               pl.BlockSpec((Kt, N), lambda i: (0, 0)),
               pl.BlockSpec((tm, N), lambda i: (i, 0)),
               (T, N), out_dtype, ((tm, Kt), (Kt, N), (tm, N)),
               alpha=alpha, res=res, res_spec=pl.BlockSpec((tm, N), lambda i: (i, 0)))


def mm_roww_t(name, dy, w2, out_dtype, alpha=1.0, after=None):
    T, N = dy.shape
    Kt = w2.shape[0]
    tm = _tile(T, 512, 8)
    tk = _tile(Kt, 1408, LANE)
    return _mm(name, "nt", dy, w2, (Kt // tk, T // tm),
               pl.BlockSpec((tm, N), lambda j, i: (i, 0)),
               pl.BlockSpec((tk, N), lambda j, i: (j, 0)),
               pl.BlockSpec((tm, tk), lambda j, i: (i, j)),
               (T, Kt), out_dtype, ((tm, N), (tk, N), (tm, tk)), alpha=alpha, after=after)


def mm_droww(name, x, dy, alpha=1.0):
    T, Kt = x.shape
    N = dy.shape[1]
    tt = _tile(T, 2048, 8)
    tk = _tile(Kt, 1408, LANE)
    return _mm(name, "tn", x, dy, (Kt // tk, T // tt),
               pl.BlockSpec((tt, tk), lambda j, t: (t, j)),
               pl.BlockSpec((tt, N), lambda j, t: (t, 0)),
               pl.BlockSpec((tk, N), lambda j, t: (j, 0)),
               (Kt, N), BF16, ((tt, tk), (tt, N), (tk, N)),
               red_axis=1, nred=T // tt, alpha=alpha)


def rms_fwd(name, x, g):
    T, D = x.shape
    tm = _tile(T, 512, 8)

    def body(x_ref, g_ref, o_ref):
        xv = x_ref[...]
        r = lax.rsqrt(jnp.mean(xv * xv, axis=-1, keepdims=True) + EPS)
        o_ref[...] = (xv * r * g_ref[...]).astype(o_ref.dtype)

    return pl.pallas_call(
        body, name=name, grid=(T // tm,),
        in_specs=[pl.BlockSpec((tm, D), lambda i: (i, 0)), pl.BlockSpec((1, D), lambda i: (0, 0))],
        out_specs=pl.BlockSpec((tm, D), lambda i: (i, 0)),
        out_shape=jax.ShapeDtypeStruct((T, D), BF16),
        compiler_params=_params(_nbytes((tm, D), F32) * 2, 4 * _nbytes((tm, D), F32)),
    )(x, g.reshape(1, D))


def _rms_bwd_math(xv, gv, dy):
    r = lax.rsqrt(jnp.mean(xv * xv, axis=-1, keepdims=True) + EPS)
    xh = xv * r
    dyg = dy * gv
    dx = r * (dyg - xh * jnp.mean(dyg * xh, axis=-1, keepdims=True))
    dg = jnp.sum(dy * xh, axis=0, keepdims=True)
    return dx, dg


def rms_bwd(name, x, g, dy, dres=None):
    T, D = x.shape
    tm = _tile(T, 256, 8)
    has_res = dres is not None

    def body(*refs):
        x_ref, g_ref, dy_ref = refs[:3]
        r_ref = refs[3] if has_res else None
        dx_ref, dg_ref = refs[-2:]
        dx, dg = _rms_bwd_math(x_ref[...], g_ref[...], dy_ref[...].astype(F32))
        if has_res:
            dx = r_ref[...] + dx
        dx_ref[...] = dx

        @pl.when(pl.program_id(0) == 0)
        def _():
            dg_ref[...] = dg

        @pl.when(pl.program_id(0) > 0)
        def _():
            dg_ref[...] += dg

    row = pl.BlockSpec((tm, D), lambda i: (i, 0))
    vec = pl.BlockSpec((1, D), lambda i: (0, 0))
    ins, specs = [x, g.reshape(1, D), dy], [row, vec, row]
    if has_res:
        ins.append(dres)
        specs.append(row)
    dx, dg = pl.pallas_call(
        body, name=name, grid=(T // tm,), in_specs=specs, out_specs=[row, vec],
        out_shape=[jax.ShapeDtypeStruct((T, D), F32), jax.ShapeDtypeStruct((1, D), F32)],
        compiler_params=_params(_nbytes((tm, D), F32) * 4, 6 * _nbytes((tm, D), F32)),
    )(*ins)
    return dx, dg.reshape(D)


def dx_norm_bwd(name, dy, w3, x, g, dres=None, pair_layout=False, after=None):
    T = dy.shape[0]
    _, K, nl = w3.shape
    tm = _tile(T, 512, 8)
    chip = _pair_chip if pair_layout else (lambda j: j)
    has_res = dres is not None

    def body(*refs):
        dy_ref, w_ref, x_ref, g_ref = refs[:4]
        r_ref = refs[4] if has_res else None
        dx_ref, dg_ref, acc = refs[-3:]
        i, k = pl.program_id(0), pl.program_id(1)
        p = lax.dot_general(dy_ref[...].astype(BF16), w_ref[...], NT, preferred_element_type=F32)

        @pl.when(k == 0)
        def _():
            acc[...] = p

        @pl.when(k > 0)
        def _():
            acc[...] += p

        @pl.when(k == N_CHIPS - 1)
        def _():
            dx, dg = _rms_bwd_math(x_ref[...], g_ref[...], acc[...])
            dx_ref[...] = r_ref[...] + dx if has_res else dx

            @pl.when(i == 0)
            def _():
                dg_ref[...] = dg

            @pl.when(i > 0)
            def _():
                dg_ref[...] += dg

    row = pl.BlockSpec((tm, K), lambda i, j: (i, 0))
    vec = pl.BlockSpec((1, K), lambda i, j: (0, 0))
    ins = [dy, w3, x, g.reshape(1, K)]
    specs = [pl.BlockSpec((tm, nl), lambda i, j: (i, j)),
             pl.BlockSpec((None, K, nl), lambda i, j: (chip(j), 0, 0)), row, vec]
    if has_res:
        ins.append(dres)
        specs.append(row)
    if after is not None:
        ins.append(after)
        specs.append(pl.BlockSpec(memory_space=pl.ANY))
    blk = _nbytes((tm, nl), dy.dtype) + _nbytes((K, nl), BF16) + (2 + has_res) * _nbytes((tm, K), F32)
    dx, dg = pl.pallas_call(
        body, name=name, grid=(T // tm, N_CHIPS), in_specs=specs, out_specs=[row, vec],
        out_shape=[jax.ShapeDtypeStruct((T, K), F32), jax.ShapeDtypeStruct((1, K), F32)],
        scratch_shapes=[pltpu.VMEM((tm, K), F32)],
        compiler_params=_params(blk, 8 * _nbytes((tm, K), F32)),
    )(*ins)
    return dx, dg.reshape(K)


def ffn_in_act(name, x, w3, after=None):
    T, K = x.shape
    _, _, nl = w3.shape
    tm = _tile(T, 512, 8)

    def body(*refs):
        x_ref, wg_ref, wu_ref = refs[:3]
        u_ref, a_ref = refs[-2:]
        xv = x_ref[...]
        g = jnp.dot(xv, wg_ref[...], preferred_element_type=F32)
        up = jnp.dot(xv, wu_ref[...], preferred_element_type=F32)
        u_ref[:, :nl] = g.astype(u_ref.dtype)
        u_ref[:, nl:] = up.astype(u_ref.dtype)
        a_ref[...] = (g * jax.nn.sigmoid(g) * up).astype(a_ref.dtype)

    blk = _nbytes((tm, K), BF16) + 2 * _nbytes((K, nl), BF16) + _nbytes((tm, 3 * nl), BF16)
    return pl.pallas_call(
        body, name=name, grid=(2, T // tm),
        in_specs=[pl.BlockSpec((tm, K), lambda p, i: (i, 0)),
                  pl.BlockSpec((None, K, nl), lambda p, i: (p, 0, 0)),
                  pl.BlockSpec((None, K, nl), lambda p, i: (p + 2, 0, 0))]
        + [pl.BlockSpec(memory_space=pl.ANY)] * (after is not None),
        out_specs=[pl.BlockSpec((tm, 2 * nl), lambda p, i: (i, p)), pl.BlockSpec((tm, nl), lambda p, i: (i, p))],
        out_shape=[jax.ShapeDtypeStruct((T, 4 * nl), BF16), jax.ShapeDtypeStruct((T, 2 * nl), BF16)],
        compiler_params=_params(blk, 4 * _nbytes((tm, nl), F32)),
    )(x, w3, w3, *([] if after is None else [after]))


def ffn_dact(name, dh, w_out, u, after=None):
    T, N = dh.shape
    F = w_out.shape[0]
    nl = F // 2
    tm = _tile(T, 512, 8)

    def body(*refs):
        d_ref, w_ref, u_ref = refs[:3]
        o_ref = refs[-1]
        dact = 0.5 * lax.dot_general(d_ref[...].astype(BF16), w_ref[...], NT, preferred_element_type=F32)
        g = u_ref[:, :nl].astype(F32)
        up = u_ref[:, nl:].astype(F32)
        sig = jax.nn.sigmoid(g)
        o_ref[:, :nl] = (dact * up * (sig * (1.0 + g * (1.0 - sig)))).astype(o_ref.dtype)
        o_ref[:, nl:] = (dact * (g * sig)).astype(o_ref.dtype)

    ins = [dh, w_out, u]
    specs = [pl.BlockSpec((tm, N), lambda p, i: (i, 0)), pl.BlockSpec((nl, N), lambda p, i: (p, 0)),
             pl.BlockSpec((tm, 2 * nl), lambda p, i: (i, p))]
    if after is not None:
        ins.append(after)
        specs.append(pl.BlockSpec(memory_space=pl.ANY))
    blk = _nbytes((tm, N), F32) + _nbytes((nl, N), BF16) + 2 * _nbytes((tm, 2 * nl), BF16)
    return pl.pallas_call(
        body, name=name, grid=(2, T // tm), in_specs=specs,
        out_specs=pl.BlockSpec((tm, 2 * nl), lambda p, i: (i, p)),
        out_shape=jax.ShapeDtypeStruct((T, 2 * F), BF16),
        compiler_params=_params(blk, 6 * _nbytes((tm, nl), F32)),
    )(*ins)


def loss_head(name, h, g, target):
    T, D = h.shape
    tm = _tile(T, 256, 8)

    def body(h_ref, g_ref, t_ref, dh_ref, dg_ref, loss_ref):
        xv = h_ref[...]
        gv = g_ref[...]
        r = lax.rsqrt(jnp.mean(xv * xv, axis=-1, keepdims=True) + EPS)
        err = xv * r * gv - t_ref[...]
        part = 0.5 * jnp.sum(jnp.mean(err * err, axis=-1, keepdims=True), axis=0, keepdims=True)
        dx, dg = _rms_bwd_math(xv, gv, err * (1.0 / D))
        dh_ref[...] = dx
        part = jnp.broadcast_to(part, (1, LANE))

        @pl.when(pl.program_id(0) == 0)
        def _():
            dg_ref[...] = dg
            loss_ref[...] = part

        @pl.when(pl.program_id(0) > 0)
        def _():
            dg_ref[...] += dg
            loss_ref[...] += part

    row = pl.BlockSpec((tm, D), lambda i: (i, 0))
    vec = pl.BlockSpec((1, D), lambda i: (0, 0))
    dh, dg, loss = pl.pallas_call(
        body, name=name, grid=(T // tm,), in_specs=[row, vec, row],
        out_specs=[row, vec, pl.BlockSpec((1, LANE), lambda i: (0, 0))],
        out_shape=[jax.ShapeDtypeStruct((T, D), F32), jax.ShapeDtypeStruct((1, D), F32),
                   jax.ShapeDtypeStruct((1, LANE), F32)],
        compiler_params=_params(_nbytes((tm, D), F32) * 3, 6 * _nbytes((tm, D), F32)),
    )(h, g.reshape(1, D), target)
    return dh, dg.reshape(D), loss


def rope_tables(S):
    half = ROPE // 2
    freqs = ROPE_THETA ** (-jnp.arange(half, dtype=F32) / half)
    ang = jnp.arange(S, dtype=F32)[:, None] * freqs[None, :]
    cos, sin = jnp.cos(ang), jnp.sin(ang)
    z = jnp.zeros_like(cos)
    ct = jnp.concatenate([cos, cos, z, z], axis=1)
    s1 = jnp.concatenate([-sin, z, z, z], axis=1)
    s2 = jnp.concatenate([z, sin, z, z], axis=1)
    return ct, s1, s2


def _rope_tile(t, ct, s1, s2):
    return t * ct + pltpu.roll(t, 96, 1) * s1 + pltpu.roll(t, 32, 1) * s2


def _rope_tile_bwd(d, ct, s1, s2):
    return d * ct + pltpu.roll(d * s1, 32, 1) + pltpu.roll(d * s2, 96, 1)


def qprep(name, q, tabs, B, S, bwd):
    T, W = q.shape
    nh = W // 256
    ts = _tile(S, 256, 8)
    fn = _rope_tile_bwd if bwd else _rope_tile

    def body(q_ref, ct_ref, s1_ref, s2_ref, o_ref):
        ct, s1, s2 = ct_ref[...], s1_ref[...], s2_ref[...]
        for h in range(nh):
            o_ref[0, :, 256 * h:256 * h + 128] = q_ref[0, :, 256 * h:256 * h + 128].astype(o_ref.dtype)
            t = q_ref[0, :, 256 * h + 128:256 * h + 256].astype(F32)
            o_ref[0, :, 256 * h + 128:256 * h + 256] = fn(t, ct, s1, s2).astype(o_ref.dtype)

    row = pl.BlockSpec((1, ts, W), lambda b, s: (b, s, 0))
    tab = pl.BlockSpec((ts, LANE), lambda b, s: (s, 0))
    out = pl.pallas_call(
        body, name=name, grid=(B, S // ts), in_specs=[row, tab, tab, tab], out_specs=row,
        out_shape=jax.ShapeDtypeStruct((B, S, W), BF16),
        compiler_params=_params(_nbytes((ts, W), F32) * 2, _nbytes((ts, W), F32) * 2),
    )(q.reshape(B, S, W), *tabs)
    return out.reshape(T, W)


def kvprep_fwd(name, ckr, g, tabs, B, S):
    T, W = ckr.shape
    KVL = W - LANE
    ts = _tile(S, 256, 8)

    def body(x_ref, g_ref, ct_ref, s1_ref, s2_ref, c_ref, k_ref):
        xv = x_ref[0, :, :KVL]
        r = lax.rsqrt(jnp.mean(xv * xv, axis=-1, keepdims=True) + EPS)
        c_ref[0] = (xv * r * g_ref[...]).astype(c_ref.dtype)
        k_ref[0] = _rope_tile(x_ref[0, :, KVL:], ct_ref[...], s1_ref[...], s2_ref[...]).astype(k_ref.dtype)

    tab = pl.BlockSpec((ts, LANE), lambda b, s: (s, 0))
    c, k = pl.pallas_call(
        body, name=name, grid=(B, S // ts),
        in_specs=[pl.BlockSpec((1, ts, W), lambda b, s: (b, s, 0)), pl.BlockSpec((1, KVL), lambda b, s: (0, 0)),
                  tab, tab, tab],
        out_specs=[pl.BlockSpec((1, ts, KVL), lambda b, s: (b, s, 0)),
                   pl.BlockSpec((1, ts, LANE), lambda b, s: (b, s, 0))],
        out_shape=[jax.ShapeDtypeStruct((B, S, KVL), BF16), jax.ShapeDtypeStruct((B, S, LANE), BF16)],
        compiler_params=_params(_nbytes((ts, W), F32) * 2, _nbytes((ts, W), F32) * 2),
    )(ckr.reshape(B, S, W), g.reshape(1, KVL), *tabs)
    return c.reshape(T, KVL), k


def kvprep_bwd(name, ckr, g, dc, dkr, tabs, B, S):
    T, W = ckr.shape
    KVL = W - LANE
    ts = _tile(S, 256, 8)

    def body(x_ref, g_ref, dc_ref, dk_ref, ct_ref, s1_ref, s2_ref, o_ref, dg_ref):
        dx, dg = _rms_bwd_math(x_ref[0, :, :KVL], g_ref[...], dc_ref[0])
        o_ref[0, :, :KVL] = dx
        o_ref[0, :, KVL:] = _rope_tile_bwd(dk_ref[0], ct_ref[...], s1_ref[...], s2_ref[...])
        first = (pl.program_id(0) == 0) & (pl.program_id(1) == 0)

        @pl.when(first)
        def _():
            dg_ref[...] = dg

        @pl.when(jnp.logical_not(first))
        def _():
            dg_ref[...] += dg

    tab = pl.BlockSpec((ts, LANE), lambda b, s: (s, 0))
    vec = pl.BlockSpec((1, KVL), lambda b, s: (0, 0))
    o, dg = pl.pallas_call(
        body, name=name, grid=(B, S // ts),
        in_specs=[pl.BlockSpec((1, ts, W), lambda b, s: (b, s, 0)), vec,
                  pl.BlockSpec((1, ts, KVL), lambda b, s: (b, s, 0)),
                  pl.BlockSpec((1, ts, LANE), lambda b, s: (b, s, 0)), tab, tab, tab],
        out_specs=[pl.BlockSpec((1, ts, W), lambda b, s: (b, s, 0)), vec],
        out_shape=[jax.ShapeDtypeStruct((B, S, W), F32), jax.ShapeDtypeStruct((1, KVL), F32)],
        compiler_params=_params(_nbytes((ts, W), F32) * 4, _nbytes((ts, W), F32) * 4),
    )(ckr.reshape(B, S, W), g.reshape(1, KVL), dc.reshape(B, S, KVL), dkr, *tabs)
    return o.reshape(T, W), dg.reshape(KVL)


DIAGS = 768


def _diag_onehot():
    col = lax.broadcasted_iota(I32, (REL_PAD, DIAGS), 1)
    row = lax.broadcasted_iota(I32, (REL_PAD, DIAGS), 0)
    idx = jnp.clip(PADR + QROWS - 1 - col, -MAX_REL, MAX_REL) + MAX_REL
    return (row == idx).astype(F32)


def rel_bias_tile(name, table):
    H = table.shape[0]
    tpad = jnp.pad(table, ((0, 0), (0, REL_PAD - table.shape[1])))

    def body(t_ref, o_ref):
        g = lax.dot_general(t_ref[...], _diag_onehot(), NN, precision=lax.Precision.HIGHEST,
                            preferred_element_type=F32)
        qc = jnp.right_shift(lax.broadcasted_iota(I32, (QROWS, WIN), 0), CHUNK_SHIFT)
        kc = jnp.right_shift(lax.broadcasted_iota(I32, (QROWS, WIN), 1), CHUNK_SHIFT)
        band = (kc >= qc) & (kc <= qc + LEFT_CHUNKS)
        for h in range(H):
            gb = jnp.broadcast_to(g[h:h + 1, :], (QROWS, DIAGS))
            tile = pltpu.roll(gb, DIAGS - (QROWS - 1), 1, stride=1, stride_axis=0)
            o_ref[h // 2, (h % 2) * QROWS:(h % 2 + 1) * QROWS, :] = jnp.where(band, tile[:, :WIN], NEG_INF)

    return pl.pallas_call(
        body, name=name, out_shape=jax.ShapeDtypeStruct((H // 2, 2 * QROWS, WIN), F32),
        compiler_params=_params(0, 2 * _nbytes((H // 2, 2 * QROWS, WIN), F32)),
    )(tpad)


def rel_bias_grad(name, dbias):
    H = 2 * dbias.shape[0]

    def body(d_ref, o_ref):
        flip = (lax.broadcasted_iota(I32, (QROWS, QROWS), 0) + lax.broadcasted_iota(I32, (QROWS, QROWS), 1)
                == QROWS - 1).astype(F32)
        rows = []
        for h in range(H):
            x = d_ref[h // 2, (h % 2) * QROWS:(h % 2 + 1) * QROWS, :]
            xr = lax.dot_general(flip, x, NN, precision=lax.Precision.HIGHEST, preferred_element_type=F32)
            xp = jnp.concatenate([xr, jnp.zeros((QROWS, DIAGS - WIN), F32)], axis=1)
            y = pltpu.roll(xp, 0, 1, stride=1, stride_axis=0)
            rows.append(jnp.sum(y, axis=0, keepdims=True))
        o_ref[...] = lax.dot_general(jnp.concatenate(rows, axis=0), _diag_onehot(), NT,
                                     precision=lax.Precision.HIGHEST, preferred_element_type=F32)

    return pl.pallas_call(
        body, name=name, out_shape=jax.ShapeDtypeStruct((H, REL_PAD), F32),
        compiler_params=_params(0, 2 * _nbytes(dbias.shape, F32)),
    )(dbias)


def _stack_pair(xp):
    lane = lax.broadcasted_iota(I32, xp.shape, 1)
    z = jnp.zeros_like(xp)
    return jnp.concatenate([jnp.where(lane < HEAD_DIM_A, xp, z), jnp.where(lane >= HEAD_DIM_A, xp, z)], axis=0)


def _unstack_pair(y):
    lane = lax.broadcasted_iota(I32, (QROWS, LANE), 1)
    return jnp.where(lane < HEAD_DIM_A, y[:QROWS], y[QROWS:])


def _attn_a_rowpen(j):
    w = lax.broadcasted_iota(I32, (1, WIN), 1)
    return jnp.where(w >= PADR - QROWS * j, 0.0, NEG_INF).astype(F32)


def _attn_a_load_bias(bias_hbm, bias_v, sem):
    cp = pltpu.make_async_copy(bias_hbm, bias_v, sem)
    cp.start()
    cp.wait()


def _attn_a_load_kv(qkv_hbm, b, kpad, vpad, sem, S, D):
    kpad[0:PADR, :] = jnp.zeros((PADR, D), BF16)
    vpad[0:PADR, :] = jnp.zeros((PADR, D), BF16)
    ck = pltpu.make_async_copy(qkv_hbm.at[b, :, pl.ds(D, D)], kpad.at[pl.ds(PADR, S), :], sem.at[0])
    cv = pltpu.make_async_copy(qkv_hbm.at[b, :, pl.ds(2 * D, D)], vpad.at[pl.ds(PADR, S), :], sem.at[1])
    ck.start()
    cv.start()
    ck.wait()
    cv.wait()


def _attn_a_exp(q2s, kp, bias, pen):
    s = lax.dot_general(q2s, kp, NT, preferred_element_type=F32) + bias + pen
    e = jnp.exp(s - jnp.max(s, axis=-1, keepdims=True))
    return e, 1.0 / jnp.sum(e, axis=-1, keepdims=True)


def attn_a_fwd(name, qkv, bias):
    B, S, D3 = qkv.shape
    D = D3 // 3
    H = D // HEAD_DIM_A
    nb = S // QROWS
    scale = HEAD_DIM_A ** -0.5

    def body(q_ref, bias_hbm, qkv_hbm, o_ref, kpad, vpad, bias_v, sem):
        b, j = pl.program_id(0), pl.program_id(1)

        @pl.when((b == 0) & (j == 0))
        def _():
            _attn_a_load_bias(bias_hbm, bias_v, sem.at[2])

        @pl.when(j == 0)
        def _():
            _attn_a_load_kv(qkv_hbm, b, kpad, vpad, sem, S, D)

        pen = _attn_a_rowpen(j)
        w0 = pl.multiple_of(j * QROWS, QROWS)
        for p in range(H // 2):
            ls = slice(p * LANE, (p + 1) * LANE)
            e, rl = _attn_a_exp(_stack_pair(q_ref[0, :, ls] * scale), kpad[pl.ds(w0, WIN), ls], bias_v[p], pen)
            o2 = jnp.dot(e.astype(BF16), vpad[pl.ds(w0, WIN), ls], preferred_element_type=F32) * rl
            o_ref[0, :, ls] = _unstack_pair(o2).astype(o_ref.dtype)

    scr = 2 * _nbytes((PADR + S, D), BF16) + _nbytes(bias.shape, F32) + 8 * _nbytes((2 * QROWS, WIN), F32)
    return pl.pallas_call(
        body, name=name, grid=(B, nb),
        in_specs=[pl.BlockSpec((1, QROWS, D), lambda b, j: (b, j, 0)),
                  pl.BlockSpec(memory_space=pl.ANY), pl.BlockSpec(memory_space=pl.ANY)],
        out_specs=pl.BlockSpec((1, QROWS, D), lambda b, j: (b, j, 0)),
        out_shape=jax.ShapeDtypeStruct((B, S, D), BF16),
        scratch_shapes=[pltpu.VMEM((PADR + S, D), BF16), pltpu.VMEM((PADR + S, D), BF16),
                        pltpu.VMEM(bias.shape, F32), pltpu.SemaphoreType.DMA((3,))],
        compiler_params=_params(2 * _nbytes((QROWS, D), BF16), scr),
    )(qkv, bias, qkv)


def attn_a_bwd(name, qkv, do, bias):
    B, S, D3 = qkv.shape
    D = D3 // 3
    H = D // HEAD_DIM_A
    nb = S // QROWS
    scale = HEAD_DIM_A ** -0.5

    def body(q_ref, do_ref, bias_hbm, qkv_hbm, dqkv_hbm, dbias_hbm, kpad, vpad, dkacc, dvacc, bias_v, dbias_v,
             dq_stage, sem):
        b, j = pl.program_id(0), pl.program_id(1)
        step = b * nb + j
        slot = lax.rem(step, 2)

        def dq_out(s):
            return pltpu.make_async_copy(dq_stage.at[s], dqkv_hbm.at[b, pl.ds(j * QROWS, QROWS), pl.ds(0, D)],
                                         sem.at[3 + s])

        @pl.when(step >= 2)
        def _():
            dq_out(slot).wait()

        @pl.when((b == 0) & (j == 0))
        def _():
            _attn_a_load_bias(bias_hbm, bias_v, sem.at[2])
            dbias_v[...] = jnp.zeros_like(dbias_v)

        @pl.when(j == 0)
        def _():
            _attn_a_load_kv(qkv_hbm, b, kpad, vpad, sem, S, D)
            dkacc[...] = jnp.zeros_like(dkacc)
            dvacc[...] = jnp.zeros_like(dvacc)

        pen = _attn_a_rowpen(j)
        w0 = pl.multiple_of(j * QROWS, QROWS)
        for p in range(H // 2):
            ls = slice(p * LANE, (p + 1) * LANE)
            q2s = _stack_pair(q_ref[0, :, ls] * scale)
            do2 = _stack_pair(do_ref[0, :, ls])
            kp = kpad[pl.ds(w0, WIN), ls]
            vp = vpad[pl.ds(w0, WIN), ls]
            e, rl = _attn_a_exp(q2s, kp, bias_v[p], pen)
            pr = e * rl
            dp = lax.dot_general(do2, vp, NT, preferred_element_type=F32)
            ds = pr * (dp - jnp.sum(pr * dp, axis=-1, keepdims=True))
            dbias_v[p] += ds
            dsb = ds.astype(BF16)
            dq_stage[slot, :, ls] = _unstack_pair(jnp.dot(dsb, kp, preferred_element_type=F32)) * scale
            dkacc[pl.ds(w0, WIN), ls] += lax.dot_general(dsb, q2s, TN, preferred_element_type=F32)
            dvacc[pl.ds(w0, WIN), ls] += lax.dot_general(pr.astype(BF16), do2, TN, preferred_element_type=F32)

        dq_out(slot).start()

        @pl.when(j == nb - 1)
        def _():
            ck = pltpu.make_async_copy(dkacc.at[pl.ds(PADR, S), :], dqkv_hbm.at[b, :, pl.ds(D, D)], sem.at[0])
            cv = pltpu.make_async_copy(dvacc.at[pl.ds(PADR, S), :], dqkv_hbm.at[b, :, pl.ds(2 * D, D)], sem.at[1])
            ck.start()
            cv.start()
            ck.wait()
            cv.wait()

        @pl.when((b == B - 1) & (j == nb - 1))
        def _():
            cb = pltpu.make_async_copy(dbias_v, dbias_hbm, sem.at[2])
            cb.start()
            dq_out(0).wait()
            dq_out(1).wait()
            cb.wait()

    blk = _nbytes((QROWS, D), BF16) * 2
    scr = (2 * _nbytes((PADR + S, D), BF16) + 2 * _nbytes((PADR + S, D), F32) + 2 * _nbytes(bias.shape, F32)
           + 8 * _nbytes((2 * QROWS, WIN), F32) + 2 * _nbytes((QROWS, D), F32))
    return pl.pallas_call(
        body, name=name, grid=(B, nb),
        in_specs=[pl.BlockSpec((1, QROWS, D), lambda b, j: (b, j, 0)),
                  pl.BlockSpec((1, QROWS, D), lambda b, j: (b, j, 0)),
                  pl.BlockSpec(memory_space=pl.ANY), pl.BlockSpec(memory_space=pl.ANY)],
        out_specs=[pl.BlockSpec(memory_space=pl.ANY), pl.BlockSpec(memory_space=pl.ANY)],
        out_shape=[jax.ShapeDtypeStruct((B, S, 3 * D), F32), jax.ShapeDtypeStruct(bias.shape, F32)],
        scratch_shapes=[pltpu.VMEM((PADR + S, D), BF16), pltpu.VMEM((PADR + S, D), BF16),
                        pltpu.VMEM((PADR + S, D), F32), pltpu.VMEM((PADR + S, D), F32),
                        pltpu.VMEM(bias.shape, F32), pltpu.VMEM(bias.shape, F32),
                        pltpu.VMEM((2, QROWS, D), F32), pltpu.SemaphoreType.DMA((5,))],
        compiler_params=_params(blk, scr),
    )(qkv, do, bias, qkv)


def _mla_raw_t(k2, kj, q, QB):
    return lax.dot_general(k2[_blk(kj, QB), :], q, NT, preferred_element_type=F32)


def _blk(kj, QB):
    return pl.ds(kj * QB, QB) if isinstance(kj, int) else pl.ds(pl.multiple_of(kj * QB, QB), QB)


def _mla_diag_pen(QB):
    kc = jnp.right_shift(lax.broadcasted_iota(I32, (QB, QB), 0), CHUNK_SHIFT)
    qc = jnp.right_shift(lax.broadcasted_iota(I32, (QB, QB), 1), CHUNK_SHIFT)
    return jnp.where(kc <= qc, 0.0, NEG_INF).astype(F32)


def _mla_fill_keys(kv_ref, kr_ref, k2):
    k2[:, :NOPE] = kv_ref[0, :, :NOPE]
    k2[:, NOPE:] = kr_ref[0]


def _t(x):
    return x.astype(F32).T


def mla_fwd(name, qf, kv, kr):
    B, S, W = qf.shape
    HB = W // 256
    QB = _tile(S, 256, CHUNK)
    nq = S // QB
    scale = (NOPE + ROPE) ** -0.5

    def body(q_ref, kv_ref, kr_ref, o_ref, lse_ref, k2, vt, st_buf, pen):
        qi = pl.program_id(2)

        @pl.when(qi == 0)
        def _():
            pen[...] = _mla_diag_pen(QB)
            _mla_fill_keys(kv_ref, kr_ref, k2)
            for kj in range(nq):
                vt[kj] = _t(kv_ref[0, kj * QB:(kj + 1) * QB, NOPE:]).astype(BF16)

        q = q_ref[0]
        st_buf[0] = _mla_raw_t(k2, 0, q, QB)

        def step(kj, carry):
            m, l, acc = carry
            cur = lax.rem(kj, 2)
            st_raw = st_buf[cur]
            st_buf[1 - cur] = _mla_raw_t(k2, jnp.minimum(kj + 1, qi), q, QB)
            st = st_raw * scale + jnp.where(kj == qi, pen[...], 0.0)
            m_new = jnp.maximum(m, jnp.max(st, axis=0, keepdims=True))
            a = jnp.exp(m - m_new)
            pt = jnp.exp(st - m_new)
            l = a * l + jnp.sum(pt, axis=0, keepdims=True)
            acc = a * acc + jnp.dot(vt[kj], pt.astype(BF16), preferred_element_type=F32)
            return m_new, l, acc

        init = (jnp.full((1, QB), NEG_INF, F32), jnp.zeros((1, QB), F32), jnp.zeros((NOPE, QB), F32))
        m, l, acc = lax.fori_loop(0, qi + 1, step, init)
        o_ref[0] = (acc * (1.0 / l)).T
        lse_ref[0, 0] = m + jnp.log(l)

    blk = (_nbytes((QB, 256), BF16) + _nbytes((S, 256), BF16) + _nbytes((S, LANE), BF16)
           + _nbytes((QB, LANE), F32))
    return pl.pallas_call(
        body, name=name, grid=(B, HB, nq),
        in_specs=[pl.BlockSpec((1, QB, 256), lambda b, h, i: (b, i, h)),
                  pl.BlockSpec((1, S, 256), lambda b, h, i: (b, 0, h)),
                  pl.BlockSpec((1, S, LANE), lambda b, h, i: (b, 0, 0))],
        out_specs=[pl.BlockSpec((1, QB, LANE), lambda b, h, i: (b, i, h)),
                   pl.BlockSpec((1, 1, 1, QB), lambda b, h, i: (b, h, 0, i))],
        out_shape=[jax.ShapeDtypeStruct((B, S, HB * LANE), F32), jax.ShapeDtypeStruct((B, HB, 1, S), F32)],
        scratch_shapes=[pltpu.VMEM((S, 256), BF16), pltpu.VMEM((nq, NOPE, QB), BF16),
                        pltpu.VMEM((2, QB, QB), F32), pltpu.VMEM((QB, QB), F32)],
        compiler_params=_params(blk, 2 * _nbytes((S, 256), BF16) + 10 * _nbytes((QB, QB), F32)),
    )(qf, kv, kr)


def mla_bwd(name, qf, kv, kr, do, o, lse):
    B, S, W = qf.shape
    HB = W // 256
    QB = _tile(S, 256, CHUNK)
    nq = S // QB
    scale = (NOPE + ROPE) ** -0.5

    def body(q_ref, kv_ref, kr_ref, do_ref, o_ref, lse_ref, dq_ref, dkv_ref, dkr_ref, k2, kt, dot_, delta, dqt,
             st_buf, dp_buf, pen):
        h = pl.program_id(1)
        pen[...] = _mla_diag_pen(QB)
        dkv_ref[...] = jnp.zeros_like(dkv_ref)

        @pl.when(h == 0)
        def _():
            dkr_ref[...] = jnp.zeros_like(dkr_ref)

        _mla_fill_keys(kv_ref, kr_ref, k2)
        for i in range(nq):
            rows = slice(i * QB, (i + 1) * QB)
            kt[i] = _t(k2[rows, :]).astype(BF16)
            dot32 = _t(do_ref[0, rows, :])
            delta[i] = jnp.sum(dot32 * o_ref[0, rows, :].T, axis=0, keepdims=True)
            dot_[i] = dot32.astype(BF16)

        for qi in range(nq):
            rows = slice(qi * QB, (qi + 1) * QB)
            q = q_ref[0, rows, :]
            dob = do_ref[0, rows, :]
            lse_q = lse_ref[0, 0, :, rows]
            delta_q = delta[qi]
            dqt[...] = jnp.zeros_like(dqt)

            def raw(kj, slot, q=q, qi=qi):
                st_buf[slot] = _mla_raw_t(k2, kj, q, QB)
                dp_buf[slot] = jnp.dot(kv_ref[0, _blk(kj, QB), NOPE:], dot_[qi], preferred_element_type=F32)

            raw(0, 0)

            def step(kj, carry, q=q, dob=dob, lse_q=lse_q, delta_q=delta_q, qi=qi, raw=raw):
                ks = pl.ds(pl.multiple_of(kj * QB, QB), QB)
                cur = lax.rem(kj, 2)
                st_raw, dp_raw = st_buf[cur], dp_buf[cur]
                raw(jnp.minimum(kj + 1, qi), 1 - cur)
                pt = jnp.exp(st_raw * scale + jnp.where(kj == qi, pen[...], 0.0) - lse_q)
                dst = (pt * (dp_raw - delta_q) * scale).astype(BF16)
                dkv_ref[0, ks, NOPE:] += jnp.dot(pt.astype(BF16), dob, preferred_element_type=F32)
                dk2 = jnp.dot(dst, q, preferred_element_type=F32)
                dkv_ref[0, ks, :NOPE] += dk2[:, :NOPE]
                dkr_ref[0, ks, :] += dk2[:, NOPE:]
                dqt[...] += jnp.dot(kt[kj], dst, preferred_element_type=F32)
                return carry

            lax.fori_loop(0, qi + 1, step, 0)
            dq_ref[0, rows, :] = dqt[...].T

    head = lambda w: pl.BlockSpec((1, S, w), lambda b, h: (b, 0, h))
    shared = pl.BlockSpec((1, S, LANE), lambda b, h: (b, 0, 0))
    blk = (2 * _nbytes((S, 256), BF16) + 2 * _nbytes((S, LANE), BF16) + _nbytes((S, LANE), F32)
           + 2 * _nbytes((S, 256), F32) + _nbytes((S, LANE), F32))
    scr = 3 * _nbytes((S, 256), BF16) + 14 * _nbytes((QB, QB), F32)
    return pl.pallas_call(
        body, name=name, grid=(B, HB),
        in_specs=[head(256), head(256), shared, head(LANE), head(LANE),
                  pl.BlockSpec((1, 1, 1, S), lambda b, h: (b, h, 0, 0))],
        out_specs=[head(256), head(256), shared],
        out_shape=[jax.ShapeDtypeStruct((B, S, W), F32), jax.ShapeDtypeStruct((B, S, W), F32),
                   jax.ShapeDtypeStruct((B, S, LANE), F32)],
        scratch_shapes=[pltpu.VMEM((S, 256), BF16), pltpu.VMEM((nq, 256, QB), BF16),
                        pltpu.VMEM((nq, NOPE, QB), BF16), pltpu.VMEM((nq, 1, QB), F32),
                        pltpu.VMEM((256, QB), F32), pltpu.VMEM((2, QB, QB), F32), pltpu.VMEM((2, QB, QB), F32),
                        pltpu.VMEM((QB, QB), F32)],
        compiler_params=_params(blk, scr),
    )(qf, kv, kr, do, o, lse)


GROUP_STEPS = 4


def cast_group(name, ws, layers, idx, after=None):
    n = len(ws)
    n_in = n + (after is not None)

    def body(k_ref, *refs):
        for i in range(n):
            refs[n_in + i][...] = refs[i][...].astype(BF16)

    def spec_in(w, layer):
        return pl.BlockSpec((None, w.shape[1] // GROUP_STEPS, w.shape[2]), lambda r, k_ref: (layer, r, 0))

    def spec_out(w):
        return pl.BlockSpec((None, w.shape[1] // GROUP_STEPS, w.shape[2]), lambda r, k_ref: (k_ref[0], r, 0))

    return pl.pallas_call(
        body, name=name,
        grid_spec=pltpu.PrefetchScalarGridSpec(
            num_scalar_prefetch=1, grid=(GROUP_STEPS,),
            in_specs=([spec_in(w, l) for w, l in zip(ws, layers)]
                      + [pl.BlockSpec(memory_space=pl.ANY)] * (after is not None)),
            out_specs=[spec_out(w) for w in ws]),
        out_shape=[jax.ShapeDtypeStruct((N_CHIPS, *w.shape[1:]), BF16) for w in ws],
        compiler_params=_params(sum(_nbytes(w.shape[1:], F32) * 3 // 2 for w in ws) // GROUP_STEPS),
    )(idx, *ws, *([] if after is None else [after]))


def adamw(name, w, g, m, v):
    R, C = w.shape
    tr = _tile(R, max(8, (1 << 18) // C // 8 * 8), 8)
    c1 = 1.0 - ADAM_B1 ** ADAM_STEP
    c2 = 1.0 - ADAM_B2 ** ADAM_STEP

    def body(w_ref, g_ref, m_ref, v_ref, d_ref, mo_ref, vo_ref):
        gv = g_ref[...]
        mn = ADAM_B1 * m_ref[...] + (1.0 - ADAM_B1) * gv
        vn = ADAM_B2 * v_ref[...] + (1.0 - ADAM_B2) * (gv * gv)
        mo_ref[...] = mn
        vo_ref[...] = vn
        d_ref[...] = -ADAM_LR * ((mn / c1) / (jnp.sqrt(vn / c2) + ADAM_EPS) + ADAM_WD * w_ref[...])

    spec = pl.BlockSpec((tr, C), lambda r: (r, 0))
    return pl.pallas_call(
        body, name=name, grid=(R // tr,), in_specs=[spec] * 4, out_specs=[spec] * 3,
        out_shape=[jax.ShapeDtypeStruct((R, C), F32)] * 3,
        compiler_params=_params(7 * _nbytes((tr, C), F32), 4 * _nbytes((tr, C), F32)),
    )(w, g, m, v)


def half_sum_group(name, dws, landed, idx):
    n = len(dws)
    steps = GROUP_STEPS // 2

    def body(i_ref, *refs):
        for i in range(n):
            refs[2 * n + i][...] = (refs[i][...].astype(F32) + refs[n + i][...].astype(F32)).astype(BF16)

    def own(d):
        return pl.BlockSpec((None, None, d.shape[2] // steps, d.shape[3]), lambda k, r, i_ref: (k, i_ref[1], r, 0))

    def flat(d):
        return pl.BlockSpec((None, d.shape[2] // steps, d.shape[3]), lambda k, r, i_ref: (k, r, 0))

    return pl.pallas_call(
        body, name=name,
        grid_spec=pltpu.PrefetchScalarGridSpec(
            num_scalar_prefetch=1, grid=(N_CHIPS, steps),
            in_specs=[own(d) for d in dws] + [flat(d) for d in dws], out_specs=[flat(d) for d in dws]),
        out_shape=[jax.ShapeDtypeStruct((N_CHIPS, *d.shape[2:]), BF16) for d in dws],
        compiler_params=_params(sum(3 * _nbytes(d.shape[2:], BF16) for d in dws) // steps),
    )(idx, *dws, *landed)


def chip_sum_group(name, parts, landed, gbufs, layers, idx):
    n = len(parts)
    steps = GROUP_STEPS // 2

    def body(i_ref, *refs):
        for i in range(n):
            a, b = refs[i], refs[n + i]
            refs[3 * n + i][...] = ((a[...].astype(F32) + b[0].astype(F32)) + b[1].astype(F32)) + b[2].astype(F32)

    def mine(p):
        return pl.BlockSpec((None, p.shape[1] // steps, p.shape[2]), lambda r, i_ref: (i_ref[0], r, 0))

    def three(p):
        return pl.BlockSpec((3, p.shape[1] // steps, p.shape[2]), lambda r, i_ref: (0, r, 0))

    def out(p, layer):
        return pl.BlockSpec((None, None, p.shape[1] // steps, p.shape[2]), lambda r, i_ref: (layer, i_ref[1], r, 0))

    return pl.pallas_call(
        body, name=name,
        grid_spec=pltpu.PrefetchScalarGridSpec(
            num_scalar_prefetch=1, grid=(steps,),
            in_specs=[mine(p) for p in parts] + [three(p) for p in parts] + [pl.BlockSpec(memory_space=pl.ANY)] * n,
            out_specs=[out(p, l) for p, l in zip(parts, layers)]),
        out_shape=[jax.ShapeDtypeStruct(g.shape, F32) for g in gbufs],
        input_output_aliases={1 + 2 * n + i: i for i in range(n)},
        compiler_params=_params(sum(6 * _nbytes(p.shape[1:], BF16) for p in parts) // steps),
    )(idx, *parts, *landed, *gbufs)


ANY = pl.BlockSpec(memory_space=pl.ANY)


def _place():
    x, y, c = lax.axis_index("x"), lax.axis_index("y"), lax.axis_index("c")
    chips = [(1 - x, y), (x, 1 - y), (1 - x, 1 - y)]
    return x, y, c, chips


HBM = pl.BlockSpec(memory_space=pltpu.HBM)
SEM = pl.BlockSpec(memory_space=pltpu.SEMAPHORE)
EFFECT = pltpu.SideEffectType.DATAFLOW_SIDE_EFFECTING


def _in_hbm(a):
    return pltpu.with_memory_space_constraint(a, pltpu.HBM)


def _ici_copy(src, dst, send_sems, recv_sems, k, peer):
    return pltpu.make_async_remote_copy(src_ref=src, dst_ref=dst, send_sem=send_sems.at[k], recv_sem=recv_sems.at[k],
                                        device_id=peer, device_id_type=MESH)


def ici_start(name, bufs, lands, after, gather):
    n, nl = len(bufs), len(lands)

    def body(*refs):
        b_in = refs[:n]
        send_sems, recv_sems = refs[n + nl + 1], refs[n + nl + 2]
        b_out = refs[n + nl + 3:2 * n + nl + 3]
        l_out = refs[2 * n + nl + 3:2 * n + 2 * nl + 3]
        token = refs[-1]
        x, y, c, chips = _place()
        kme = 2 * x + y
        for i in range(n):
            for j in range(3):
                peer = (*chips[j], c)
                if gather:
                    _ici_copy(b_out[i].at[kme, c], b_out[i].at[kme, c], send_sems, recv_sems, 3 * i + j, peer).start()
                else:
                    kd = 2 * chips[j][0] + chips[j][1]
                    _ici_copy(b_out[i].at[kd], l_out[i].at[j], send_sems, recv_sems, 3 * i + j, peer).start()
        token[...] = jnp.zeros_like(token)

    arrays = [*bufs, *lands]
    outs = pl.pallas_call(
        body, name=name,
        in_specs=[HBM] * (n + nl) + [ANY],
        out_specs=(SEM, SEM, *[HBM] * (n + nl), pl.BlockSpec(memory_space=pltpu.VMEM)),
        out_shape=(pltpu.SemaphoreType.DMA((3 * n,)), pltpu.SemaphoreType.DMA((3 * n,)),
                   *[pltpu.HBM(a.shape, a.dtype) for a in arrays], jax.ShapeDtypeStruct((8, LANE), F32)),
        input_output_aliases={i: 2 + i for i in range(n + nl)},
        compiler_params=pltpu.CompilerParams(has_side_effects=EFFECT),
    )(*[_in_hbm(a) for a in arrays], after)
    return outs[0], outs[1], list(outs[2:2 + n]), list(outs[2 + n:2 + n + nl]), outs[-1]


def ici_wait(name, send_sems, recv_sems, bufs, lands, after, gather):
    n, nl = len(bufs), len(lands)

    def body(*refs):
        b_in, l_in = refs[:n], refs[n:n + nl]
        send_sems, recv_sems = refs[n + nl], refs[n + nl + 1]
        x, y, c, chips = _place()
        kme = 2 * x + y
        for i in range(n):
            for j in range(3):
                peer = (*chips[j], c)
                kj = 2 * chips[j][0] + chips[j][1]
                if gather:
                    _ici_copy(b_in[i].at[kme, c], b_in[i].at[kme, c], send_sems, recv_sems, 3 * i + j, peer).wait_send()
                    _ici_copy(b_in[i].at[kj, c], b_in[i].at[kj, c], send_sems, recv_sems, 3 * i + j, peer).wait_recv()
                else:
                    _ici_copy(b_in[i].at[kj], l_in[i].at[j], send_sems, recv_sems, 3 * i + j, peer).wait_send()
                    _ici_copy(b_in[i].at[kj], l_in[i].at[j], send_sems, recv_sems, 3 * i + j, peer).wait_recv()

    arrays = [*bufs, *lands]
    outs = pl.pallas_call(
        body, name=name,
        in_specs=[HBM] * (n + nl) + [SEM, SEM, ANY],
        out_specs=tuple([HBM] * (n + nl)),
        out_shape=tuple(pltpu.HBM(a.shape, a.dtype) for a in arrays),
        input_output_aliases={i: i for i in range(n + nl)},
        compiler_params=pltpu.CompilerParams(has_side_effects=EFFECT),
    )(*arrays, send_sems, recv_sems, after)
    return list(outs[:n]), list(outs[n:])


def gather_pair_pass(name, bufs):
    n = len(bufs)

    def body(*refs):
        b = refs[n:2 * n]
        send_sems, recv_sems = refs[2 * n:]
        x, y, c, chips = _place()
        sib = (x, y, 1 - c)

        def d2d(i, j, which):
            kj = 2 * chips[j][0] + chips[j][1]
            return _ici_copy(b[i].at[kj, which], b[i].at[kj, which], send_sems, recv_sems, 3 * i + j, sib)

        for i in range(n):
            for j in range(3):
                d2d(i, j, c).start()
        for i in range(n):
            for j in range(3):
                d2d(i, j, 1 - c).wait_recv()
        for i in range(n):
            for j in range(3):
                d2d(i, j, c).wait_send()

    return pl.pallas_call(
        body, name=name, in_specs=[ANY] * n, out_specs=[ANY] * n,
        out_shape=[jax.ShapeDtypeStruct(a.shape, a.dtype) for a in bufs],
        input_output_aliases={i: i for i in range(n)},
        scratch_shapes=[pltpu.SemaphoreType.DMA((3 * n,)), pltpu.SemaphoreType.DMA((3 * n,))],
    )(*bufs)


def pair_exchange(name, dws):
    n = len(dws)

    def body(*refs):
        ins, outs = refs[:n], refs[n:2 * n]
        send_sems, recv_sems = refs[2 * n:]
        x, y, c, _ = _place()
        copies = []
        for i in range(n):
            copies.append(pltpu.make_async_remote_copy(
                src_ref=ins[i].at[:, 1 - c], dst_ref=outs[i],
                send_sem=send_sems.at[i], recv_sem=recv_sems.at[i],
                device_id=(x, y, 1 - c), device_id_type=MESH))
            copies[i].start()
        for cp in copies:
            cp.wait_recv()
        for cp in copies:
            cp.wait_send()

    return pl.pallas_call(
        body, name=name, in_specs=[ANY] * n, out_specs=[ANY] * n,
        out_shape=[jax.ShapeDtypeStruct((N_CHIPS, *d.shape[2:]), d.dtype) for d in dws],
        scratch_shapes=[pltpu.SemaphoreType.DMA((n,)), pltpu.SemaphoreType.DMA((n,))],
    )(*dws)


def pair_assemble(gbufs):
    n = len(gbufs)

    def body(*refs):
        bufs = refs[n:2 * n]
        send_sems, recv_sems = refs[2 * n:]
        x, y, c, _ = _place()
        copies = []
        for i in range(n):
            copies.append(pltpu.make_async_remote_copy(
                src_ref=bufs[i].at[:, c], dst_ref=bufs[i].at[:, c],
                send_sem=send_sems.at[i], recv_sem=recv_sems.at[i],
                device_id=(x, y, 1 - c), device_id_type=MESH))
            copies[i].start()
        for i in range(n):
            pltpu.make_async_remote_copy(
                src_ref=bufs[i].at[:, 1 - c], dst_ref=bufs[i].at[:, 1 - c],
                send_sem=send_sems.at[i], recv_sem=recv_sems.at[i],
                device_id=(x, y, 1 - c), device_id_type=MESH).wait_recv()
        for cp in copies:
            cp.wait_send()

    return pl.pallas_call(
        body, name="grad_pair_assemble", in_specs=[ANY] * n, out_specs=[ANY] * n,
        out_shape=[jax.ShapeDtypeStruct(g.shape, g.dtype) for g in gbufs],
        input_output_aliases={i: i for i in range(n)},
        scratch_shapes=[pltpu.SemaphoreType.DMA((n,)), pltpu.SemaphoreType.DMA((n,))],
    )(*gbufs)


def all_reduce_small(vec):
    NR = vec.shape[0]
    flips = [(fx, fy, fc) for fx in (0, 1) for fy in (0, 1) for fc in (0, 1)][1:]

    def body(v_ref, o_ref, buf, send_sems, recv_sems):
        x, y, c, _ = _place()
        me = 4 * x + 2 * y + c
        buf[me] = v_ref[...]
        copies = []
        for j, (fx, fy, fc) in enumerate(flips):
            peer = (1 - x if fx else x, 1 - y if fy else y, 1 - c if fc else c)
            copies.append(pltpu.make_async_remote_copy(
                src_ref=v_ref, dst_ref=buf.at[me], send_sem=send_sems.at[j], recv_sem=recv_sems.at[j],
                device_id=peer, device_id_type=MESH))
            copies[j].start()
        for cp in copies:
            cp.wait_recv()
        for cp in copies:
            cp.wait_send()
        acc = buf[0]
        for d in range(1, 8):
            acc = acc + buf[d]
        o_ref[...] = acc

    return pl.pallas_call(
        body, name="all_reduce_small",
        in_specs=[pl.BlockSpec(memory_space=pltpu.VMEM)], out_specs=pl.BlockSpec(memory_space=pltpu.VMEM),
        out_shape=jax.ShapeDtypeStruct((NR, LANE), F32),
        scratch_shapes=[pltpu.VMEM((8, NR, LANE), F32), pltpu.SemaphoreType.DMA((7,)),
                        pltpu.SemaphoreType.DMA((7,))],
    )(vec)


def _pack(arrays):
    flat = jnp.concatenate([a.reshape(-1).astype(F32) for a in arrays])
    n = flat.shape[0]
    npad = -(-n // (8 * LANE)) * (8 * LANE)
    return jnp.pad(flat, (0, npad - n)).reshape(npad // LANE, LANE)


def _unpack(buf, like):
    flat = buf.reshape(-1)
    out, off = [], 0
    for a in like:
        out.append(flat[off:off + a.size].reshape(a.shape))
        off += a.size
    return out


def kernel(x, ffn1_norm, ffn1_w_in, ffn1_w_out, mix_norm, ffn2_norm, ffn2_w_in, ffn2_w_out, a_w_qkv, a_rel_bias, a_w_o, kv_norm, kv_w_down, kv_latent_norm, kv_w_up, b_w_dq, b_q_norm, b_w_uq, b_w_o, final_norm, loss_target, m_ffn1_norm, m_ffn1_w_in, m_ffn1_w_out, m_mix_norm, m_ffn2_norm, m_ffn2_w_in, m_ffn2_w_out, m_a_w_qkv, m_a_rel_bias, m_a_w_o, m_kv_norm, m_kv_w_down, m_kv_latent_norm, m_kv_w_up, m_b_w_dq, m_b_q_norm, m_b_w_uq, m_b_w_o, m_final_norm, v_ffn1_norm, v_ffn1_w_in, v_ffn1_w_out, v_mix_norm, v_ffn2_norm, v_ffn2_w_in, v_ffn2_w_out, v_a_w_qkv, v_a_rel_bias, v_a_w_o, v_kv_norm, v_kv_w_down, v_kv_latent_norm, v_kv_w_up, v_b_w_dq, v_b_q_norm, v_b_w_uq, v_b_w_o, v_final_norm):
    B, S, D = x.shape
    T = B * S
    HB = D // 128
    QL = b_q_norm.shape[-1]
    KVL = kv_latent_norm.shape[0]
    hpc = HB // N_CHIPS
    tabs = rope_tables(S)
    idx = jnp.stack([2 * lax.axis_index("x") + lax.axis_index("y"), lax.axis_index("c")]).astype(I32)

    def halves(a):
        return a.reshape(*a.shape[:-2], 2, a.shape[-2] // 2, a.shape[-1])

    def whole(a):
        return a.reshape(*a.shape[:-3], 2 * a.shape[-2], a.shape[-1])

    kv_w_down_p = jnp.pad(kv_w_down, ((0, 0), (0, LANE - ROPE)))[None]
    b_w_uq_p = jnp.pad(b_w_uq.reshape(1, QL, hpc, NOPE + ROPE),
                       ((0, 0), (0, 0), (0, 0), (0, LANE - ROPE))).reshape(1, QL, hpc * 256)
    sharded = [("ffn1_w_in", ffn1_w_in), ("ffn1_w_out", ffn1_w_out), ("ffn2_w_in", ffn2_w_in),
               ("ffn2_w_out", ffn2_w_out), ("a_w_qkv", a_w_qkv), ("a_w_o", a_w_o),
               ("kv_w_down", kv_w_down_p), ("kv_w_up", kv_w_up[None]), ("b_w_dq", b_w_dq),
               ("b_w_uq", b_w_uq_p), ("b_w_o", b_w_o)]
    pieces = [(a, l) for a, (_, w) in enumerate(sharded) for l in range(w.shape[0])]
    names = [nm for nm, _ in sharded]
    shard_of = dict(sharded)
    W = {}

    gather_groups = [
        [("ffn1_w_in", 0), ("ffn1_w_out", 0)],
        [("a_w_qkv", 0), ("a_w_o", 0)],
        [("ffn2_w_in", 0), ("ffn2_w_out", 0), ("kv_w_down", 0), ("kv_w_up", 0)],
        [("ffn1_w_in", 1), ("ffn1_w_out", 1), ("b_w_dq", 0), ("b_w_uq", 0), ("b_w_o", 0), ("ffn2_w_in", 1),
         ("ffn2_w_out", 1)]]

    own = {}

    def cast(g, after=None):
        keys = gather_groups[g]
        own.update(zip(keys, cast_group(f"cast_group_{g}", [shard_of[nm] for nm, _ in keys], [l for _, l in keys],
                                        idx, after=after)))

    def gather_start(g, after):
        keys = gather_groups[g]
        ss, rs, bufs, _, token = ici_start(f"gather_start_{g}", [halves(own[k]) for k in keys], [], after, True)
        return (g, ss, rs, bufs), token

    def gather_finish(state, after):
        g, ss, rs, bufs = state
        bufs, _ = ici_wait(f"gather_wait_{g}", ss, rs, bufs, [], after, True)
        full = gather_pair_pass(f"gather_pair_{g}", bufs)
        for k, w in zip(gather_groups[g], full):
            W[k] = whole(w)
        return full[0]

    def tied(a, token):
        return a + token[0, 0]

    def col(nm, l=0):
        return W[(nm, l)]

    def row(nm, l=0):
        w = W[(nm, l)]
        return w.reshape(N_CHIPS * w.shape[1], w.shape[2])

    bias = rel_bias_tile("rel_bias_tile", a_rel_bias[0])

    def ffn_fwd(tag, h, g, w_in, w_out, xn=None, after=None):
        if xn is None:
            xn = rms_fwd(f"{tag}_norm", h, g)
        u, act = ffn_in_act(f"{tag}_in", xn, w_in, after=after)
        return mm_roww(f"{tag}_out", act, w_out, F32, res=h, alpha=0.5), (xn, u, act)

    h0 = x.reshape(T, D)
    cast(0)
    st0, tok0 = gather_start(0, h0)
    for g in range(1, len(gather_groups)):
        cast(g, tok0)
    xn0 = rms_fwd("l0f1_norm", h0, tied(ffn1_norm[0], tok0))
    done0 = gather_finish(st0, xn0)
    st1, tok1 = gather_start(1, done0)
    h1, sv_f1a = ffn_fwd("l0f1", h0, None, col("ffn1_w_in", 0), row("ffn1_w_out", 0), xn=xn0, after=tok1)
    done1 = gather_finish(st1, h1)
    st2, tok2 = gather_start(2, done1)
    st3, tok3 = gather_start(3, tok2)
    hn_a = rms_fwd("l0mix_norm", h1, tied(mix_norm[0], tok3))
    qkv = mm_colw("l0_qkv", hn_a, col("a_w_qkv"), BF16).reshape(B, S, 3 * D)
    o_a = attn_a_fwd("l0_attn", qkv, bias).reshape(T, D)
    h2 = mm_roww("l0_attn_out", o_a, row("a_w_o"), F32, res=h1)
    gather_finish(st2, h2)
    h3, sv_f2a = ffn_fwd("l0f2", h2, ffn2_norm[0], col("ffn2_w_in", 0), row("ffn2_w_out", 0))

    hkv = rms_fwd("kv_norm", h3, kv_norm)
    ckr = mm_roww("kv_down", hkv, row("kv_w_down"), F32)
    ckv, kr = kvprep_fwd("kv_prep", ckr, kv_latent_norm, tabs, B, S)
    kvb = mm_colw("kv_up", ckv, col("kv_w_up"), BF16).reshape(B, S, HB * 256)
    gather_finish(st3, kvb)

    h4, sv_f1b = ffn_fwd("l1f1", h3, ffn1_norm[1], col("ffn1_w_in", 1), row("ffn1_w_out", 1))
    hn_b = rms_fwd("l1mix_norm", h4, mix_norm[1])
    cqp = mm_roww("l1_dq", hn_b, row("b_w_dq"), F32)
    cq = rms_fwd("l1_q_norm", cqp, b_q_norm[0])
    qpre = mm_colw("l1_uq", cq, col("b_w_uq"), F32)
    qf = qprep("l1_q_rope", qpre, tabs, B, S, bwd=False).reshape(B, S, HB * 256)
    o_b, lse = mla_fwd("l1_attn", qf, kvb, kr)
    h5 = mm_roww("l1_attn_out", o_b.reshape(T, HB * LANE), row("b_w_o"), F32, res=h4)
    h6, sv_f2b = ffn_fwd("l1f2", h5, ffn2_norm[1], col("ffn2_w_in", 1), row("ffn2_w_out", 1))

    dh, g_final, loss_part = loss_head("loss_head", h6, final_norm, loss_target.reshape(T, D))

    gw = {}
    gbufs = {nm: lax.empty(halves(w).shape, F32) for nm, w in sharded}

    def reduce_start(r, keys, after):
        dws = [halves(gw[k]) for k in keys]
        landed = pair_exchange(f"grad_pair_exchange_{r}", dws)
        parts = half_sum_group(f"half_sum_{r}", dws, landed, idx)
        lands = [lax.empty((3, *p.shape[1:]), p.dtype) for p in parts]
        ss, rs, parts, lands, token = ici_start(f"reduce_start_{r}", parts, lands, after, False)
        return (r, keys, ss, rs, parts, lands), token

    def reduce_finish(state, after):
        r, keys, ss, rs, parts, lands = state
        parts, lands = ici_wait(f"reduce_wait_{r}", ss, rs, parts, lands, after, False)
        done = chip_sum_group(f"chip_sum_{r}", parts, lands, [gbufs[nm] for nm, _ in keys], [l for _, l in keys], idx)
        gbufs.update(zip([nm for nm, _ in keys], done))
        return done[0]

    def ffn_bwd(tag, dh, h_in, g, w_in, w_out, saved, key_in, key_out, after=None, then=None):
        xn, u, act = saved
        du = ffn_dact(f"{tag}_dact", dh, w_out, u, after=after)
        dwo = mm_droww(f"{tag}_dwout", act, dh, alpha=0.5)
        gw[key_out] = dwo.reshape(N_CHIPS, dwo.shape[0] // N_CHIPS, dwo.shape[1])
        gw[key_in] = mm_dcolw(f"{tag}_dwin", xn, du, pair_layout=True)
        token = then(du) if then is not None else None
        return dx_norm_bwd(f"{tag}_dxn", du, w_in, h_in, g, dres=dh, pair_layout=True, after=token)

    def chip_major(dw):
        return dw.reshape(N_CHIPS, dw.shape[0] // N_CHIPS, dw.shape[1])

    dh, g_f2b = ffn_bwd("l1f2b", dh, h5, ffn2_norm[1], col("ffn2_w_in", 1), row("ffn2_w_out", 1), sv_f2b,
                        ("ffn2_w_in", 1), ("ffn2_w_out", 1))
    red0, rtok0 = reduce_start(0, [("ffn2_w_in", 1), ("ffn2_w_out", 1)], dh)
    do_b = mm_roww_t("l1_attn_do", dh, row("b_w_o"), BF16, after=rtok0).reshape(B, S, HB * LANE)
    gw[("b_w_o", 0)] = chip_major(mm_droww("l1_attn_dwo", o_b.reshape(T, HB * LANE), dh))
    dqf, dkv, dkr = mla_bwd("l1_attn_bwd", qf, kvb, kr, do_b, o_b, lse)
    dqpre = qprep("l1_q_rope_bwd", dqf.reshape(T, HB * 256), tabs, B, S, bwd=True)
    gw[("b_w_uq", 0)] = mm_dcolw("l1_dwuq", cq, dqpre)
    dcqp, g_qn = dx_norm_bwd("l1_dcq", dqpre, col("b_w_uq"), cqp, b_q_norm[0])
    gw[("b_w_dq", 0)] = chip_major(mm_droww("l1_dwdq", hn_b, dcqp))
    dhn = mm_roww_t("l1_dhn", dcqp, row("b_w_dq"), F32)
    dh, g_mixb = rms_bwd("l1_dmix", h4, mix_norm[1], dhn, dres=dh)
    dh, g_f1b = ffn_bwd("l1f1b", dh, h3, ffn1_norm[1], col("ffn1_w_in", 1), row("ffn1_w_out", 1), sv_f1b,
                        ("ffn1_w_in", 1), ("ffn1_w_out", 1))
    fin0 = reduce_finish(red0, dh)
    red1, rtok1 = reduce_start(1, [("b_w_o", 0), ("b_w_uq", 0), ("b_w_dq", 0), ("ffn1_w_in", 1), ("ffn1_w_out", 1)], fin0)
    dkv2 = dkv.reshape(T, HB * 256)
    gw[("kv_w_up", 0)] = mm_dcolw("kv_dwup", ckv, dkv2, after=rtok1)
    dckv = mm_colw_t("kv_dckv", dkv2, col("kv_w_up"), F32, after=rtok1)
    dckr, g_lat = kvprep_bwd("kv_prep_bwd", ckr, kv_latent_norm, dckv, dkr, tabs, B, S)
    gw[("kv_w_down", 0)] = chip_major(mm_droww("kv_dwdown", hkv, dckr))
    dhkv = mm_roww_t("kv_dhkv", dckr, row("kv_w_down"), F32)
    dh, g_kvn = rms_bwd("kv_dnorm", h3, kv_norm, dhkv, dres=dh)
    dh, g_f2a = ffn_bwd("l0f2b", dh, h2, ffn2_norm[0], col("ffn2_w_in", 0), row("ffn2_w_out", 0), sv_f2a,
                        ("ffn2_w_in", 0), ("ffn2_w_out", 0))
    do_a = mm_roww_t("l0_attn_do", dh, row("a_w_o"), BF16).reshape(B, S, D)
    gw[("a_w_o", 0)] = chip_major(mm_droww("l0_attn_dwo", o_a, dh))
    dqkv, dbias = attn_a_bwd("l0_attn_bwd", qkv, do_a, bias)
    dqkv = dqkv.reshape(T, 3 * D)
    gw[("a_w_qkv", 0)] = mm_dcolw("l0_dwqkv", hn_a, dqkv)
    dh, g_mixa = dx_norm_bwd("l0_dhn", dqkv, col("a_w_qkv"), h1, mix_norm[0], dres=dh)
    fin1 = reduce_finish(red1, dh)
    red2, rtok2 = reduce_start(2, [("kv_w_up", 0), ("kv_w_down", 0), ("ffn2_w_in", 0), ("ffn2_w_out", 0),
                                   ("a_w_o", 0), ("a_w_qkv", 0)], fin1)
    last = {}

    def last_group(du):
        fin2 = reduce_finish(red2, gw[("ffn1_w_in", 0)])
        last["red"], token = reduce_start(3, [("ffn1_w_in", 0), ("ffn1_w_out", 0)], fin2)
        return token

    dh, g_f1a = ffn_bwd("l0f1b", dh, h0, ffn1_norm[0], col("ffn1_w_in", 0), row("ffn1_w_out", 0), sv_f1a,
                        ("ffn1_w_in", 0), ("ffn1_w_out", 0), after=rtok2, then=last_group)
    grad_x = dh.reshape(B, S, D)
    g_rel = rel_bias_grad("rel_bias_grad", dbias)[:, :2 * MAX_REL + 1][None]
    reduce_finish(last["red"], dh)

    full = [whole(g) for g in pair_assemble([gbufs[nm] for nm in names])]
    G = {nm: g for (nm, _), g in zip(sharded, full)}
    G["kv_w_down"] = G["kv_w_down"][0, :, :KVL + ROPE]
    G["kv_w_up"] = G["kv_w_up"][0]
    G["b_w_uq"] = G["b_w_uq"].reshape(1, QL, hpc, 256)[..., :NOPE + ROPE].reshape(b_w_uq.shape)

    small = [("ffn1_norm", jnp.stack([g_f1a, g_f1b])), ("mix_norm", jnp.stack([g_mixa, g_mixb])),
             ("ffn2_norm", jnp.stack([g_f2a, g_f2b])), ("a_rel_bias", g_rel), ("kv_norm", g_kvn),
             ("kv_latent_norm", g_lat), ("b_q_norm", g_qn[None]), ("final_norm", g_final)]
    red = all_reduce_small(_pack([loss_part] + [g for _, g in small]))
    unpacked = _unpack(red, [loss_part] + [g for _, g in small])
    loss = unpacked[0][0, 0]
    for (nm, _), g in zip(small, unpacked[1:]):
        G[nm] = g

    given = dict(ffn1_norm=(ffn1_norm, m_ffn1_norm, v_ffn1_norm), ffn1_w_in=(ffn1_w_in, m_ffn1_w_in, v_ffn1_w_in),
                 ffn1_w_out=(ffn1_w_out, m_ffn1_w_out, v_ffn1_w_out), mix_norm=(mix_norm, m_mix_norm, v_mix_norm),
                 ffn2_norm=(ffn2_norm, m_ffn2_norm, v_ffn2_norm), ffn2_w_in=(ffn2_w_in, m_ffn2_w_in, v_ffn2_w_in),
                 ffn2_w_out=(ffn2_w_out, m_ffn2_w_out, v_ffn2_w_out), a_w_qkv=(a_w_qkv, m_a_w_qkv, v_a_w_qkv),
                 a_rel_bias=(a_rel_bias, m_a_rel_bias, v_a_rel_bias), a_w_o=(a_w_o, m_a_w_o, v_a_w_o),
                 kv_norm=(kv_norm, m_kv_norm, v_kv_norm), kv_w_down=(kv_w_down, m_kv_w_down, v_kv_w_down),
                 kv_latent_norm=(kv_latent_norm, m_kv_latent_norm, v_kv_latent_norm),
                 kv_w_up=(kv_w_up, m_kv_w_up, v_kv_w_up), b_w_dq=(b_w_dq, m_b_w_dq, v_b_w_dq),
                 b_q_norm=(b_q_norm, m_b_q_norm, v_b_q_norm), b_w_uq=(b_w_uq, m_b_w_uq, v_b_w_uq),
                 b_w_o=(b_w_o, m_b_w_o, v_b_w_o), final_norm=(final_norm, m_final_norm, v_final_norm))
    order = list(given)
    delta, new_m, new_v = {}, {}, {}
    small_names = [nm for nm, _ in small]
    packed = [_pack([given[nm][k] for nm in small_names]) for k in range(3)]
    outs = adamw("adamw_small", packed[0], _pack([G[nm] for nm in small_names]), packed[1], packed[2])
    for dst, buf in zip((delta, new_m, new_v), outs):
        for nm, a in zip(small_names, _unpack(buf, [given[nm][0] for nm in small_names])):
            dst[nm] = a
    for nm, _ in sharded:
        w, m, v = given[nm]
        g = G[nm].reshape(w.shape)
        G[nm] = g
        two = lambda a: a.reshape(-1, a.shape[-1])
        d_, m_, v_ = adamw(f"adamw_{nm}", two(w), two(g), two(m), two(v))
        delta[nm], new_m[nm], new_v[nm] = d_.reshape(w.shape), m_.reshape(w.shape), v_.reshape(w.shape)

    return (loss, grad_x, *[G[n] for n in order], *[delta[n] for n in order],
            *[new_m[n] for n in order], *[new_v[n] for n in order])
```

```python
import math

import jax
import jax.numpy as jnp
from jax import lax
from jax.experimental import pallas as pl
from jax.experimental.pallas import tpu as pltpu

F32 = jnp.float32
BF16 = jnp.bfloat16
I32 = jnp.int32

CHUNK = 64
CHUNK_SHIFT = 6
HEAD_DIM_A = 64
LEFT_CHUNKS = 8
MAX_REL = 128
REL_PAD = 384
QROWS = 2 * CHUNK
WIN = (LEFT_CHUNKS + 2) * CHUNK
PADR = LEFT_CHUNKS * CHUNK
NOPE = 128
ROPE = 64
EPS = 1e-6
NEG_INF = -1e30
ROPE_THETA = 10000.0
ADAM_LR, ADAM_B1, ADAM_B2, ADAM_EPS, ADAM_WD, ADAM_STEP = 0.001, 0.9, 0.999, 1e-08, 0.01, 10
N_CHIPS = 4
LANE = 128
MESH = pl.DeviceIdType.MESH
VMEM_CAP_MB = 60

NN = (((1,), (0,)), ((), ()))
NT = (((1,), (1,)), ((), ()))
TN = (((0,), (0,)), ((), ()))


def _tile(n, pref, mult):
    t = (min(pref, n) // mult) * mult
    while t >= mult:
        if n % t == 0:
            return t
        t -= mult
    return n


def _nbytes(shape, dtype):
    return math.prod(shape) * jnp.dtype(dtype).itemsize


def _params(block_bytes, extra_bytes=0):
    need = 2 * block_bytes + extra_bytes
    mb = min(VMEM_CAP_MB, max(32, int(need * 1.25 / 2**20) + 8))
    return pltpu.CompilerParams(vmem_limit_bytes=mb * 2**20)


def _mm(name, kind, a, b, grid, a_spec, b_spec, o_spec, out_shape, out_dtype, blocks,
        red_axis=None, nred=1, alpha=1.0, res=None, res_spec=None, after=None):
    dims = {"nn": NN, "nt": NT, "tn": TN}[kind]
    has_res = res is not None
    acc_in_out = nred > 1 and out_dtype == F32 and not has_res and alpha == 1.0
    n_in = 2 + has_res + (after is not None)

    def body(*refs):
        a_ref, b_ref = refs[0], refs[1]
        r_ref = refs[2] if has_res else None
        o_ref = refs[n_in]
        p = lax.dot_general(a_ref[...].astype(BF16), b_ref[...].astype(BF16), dims,
                            preferred_element_type=F32)

        def finish(acc):
            y = acc if alpha == 1.0 else acc * alpha
            if has_res:
                y = r_ref[...] + y
            o_ref[...] = y.astype(o_ref.dtype)

        if nred == 1:
            finish(p)
            return
        k = pl.program_id(red_axis)
        tgt = o_ref if acc_in_out else refs[-1]

        @pl.when(k == 0)
        def _():
            tgt[...] = p

        @pl.when(k > 0)
        def _():
            tgt[...] += p

        if not acc_in_out:
            @pl.when(k == nred - 1)
            def _():
                finish(tgt[...])

    a_blk, b_blk, o_blk = blocks
    scratch = []
    extra = 0
    if nred > 1 and not acc_in_out:
        scratch = [pltpu.VMEM(o_blk, F32)]
        extra = _nbytes(o_blk, F32)
    blk = _nbytes(a_blk, a.dtype) + _nbytes(b_blk, b.dtype) + _nbytes(o_blk, out_dtype)
    ins, specs = [a, b], [a_spec, b_spec]
    if has_res:
        ins.append(res)
        specs.append(res_spec)
        blk += _nbytes(o_blk, res.dtype)
    if after is not None:
        ins.append(after)
        specs.append(pl.BlockSpec(memory_space=pl.ANY))
    extra += _nbytes(a_blk, BF16) + _nbytes(b_blk, BF16) + 2 * _nbytes(o_blk, F32)
    return pl.pallas_call(
        body, name=name, grid=grid, in_specs=specs, out_specs=o_spec,
        out_shape=jax.ShapeDtypeStruct(out_shape, out_dtype), scratch_shapes=scratch,
        compiler_params=_params(blk, extra),
    )(*ins)


def mm_colw(name, x, w3, out_dtype):
    T, K = x.shape
    _, _, nl = w3.shape
    tm = _tile(T, 512, 8)
    return _mm(name, "nn", x, w3, (N_CHIPS, T // tm),
               pl.BlockSpec((tm, K), lambda j, i: (i, 0)),
               pl.BlockSpec((None, K, nl), lambda j, i: (j, 0, 0)),
               pl.BlockSpec((tm, nl), lambda j, i: (i, j)),
               (T, N_CHIPS * nl), out_dtype, ((tm, K), (K, nl), (tm, nl)))


def _pair_chip(j):
    return (j % 2) * 2 + j // 2


def mm_colw_t(name, dy, w3, out_dtype, res=None, after=None, pair_layout=False):
    T = dy.shape[0]
    _, K, nl = w3.shape
    tm = _tile(T, 1024, 8)
    chip = _pair_chip if pair_layout else (lambda j: j)
    return _mm(name, "nt", dy, w3, (T // tm, N_CHIPS),
               pl.BlockSpec((tm, nl), lambda i, j: (i, j)),
               pl.BlockSpec((None, K, nl), lambda i, j: (chip(j), 0, 0)),
               pl.BlockSpec((tm, K), lambda i, j: (i, 0)),
               (T, K), out_dtype, ((tm, nl), (K, nl), (tm, K)),
               red_axis=1, nred=N_CHIPS, res=res,
               res_spec=pl.BlockSpec((tm, K), lambda i, j: (i, 0)), after=after)


def mm_dcolw(name, x, dy, after=None, pair_layout=False):
    T, K = x.shape
    nl = dy.shape[1] // N_CHIPS
    tt = _tile(T, 2048, 8)
    chip = _pair_chip if pair_layout else (lambda j: j)
    return _mm(name, "tn", x, dy, (N_CHIPS, T // tt),
               pl.BlockSpec((tt, K), lambda j, t: (t, 0)),
               pl.BlockSpec((tt, nl), lambda j, t: (t, j)),
               pl.BlockSpec((None, K, nl), lambda j, t: (chip(j), 0, 0)),
               (N_CHIPS, K, nl), BF16, ((tt, K), (tt, nl), (K, nl)),
               red_axis=1, nred=T // tt, after=after)


def mm_roww(name, x, w2, out_dtype, res=None, alpha=1.0):
    T, Kt = x.shape
    N = w2.shape[1]
    tm = _tile(T, 512, 8)
    return _mm(name, "nn", x, w2, (T // tm,),
               pl.BlockSpec((tm, Kt), lambda i: (i, 0)),
               pl.BlockSpec((Kt, N), lambda i: (0, 0)),
               pl.BlockSpec((tm, N), lambda i: (i, 0)),
               (T, N), out_dtype, ((tm, Kt), (Kt, N), (tm, N)),
               alpha=alpha, res=res, res_spec=pl.BlockSpec((tm, N), lambda i: (i, 0)))


def mm_roww_t(name, dy, w2, out_dtype, alpha=1.0, after=None):
    T, N = dy.shape
    Kt = w2.shape[0]
    tm = _tile(T, 512, 8)
    tk = _tile(Kt, 1408, LANE)
    return _mm(name, "nt", dy, w2, (Kt // tk, T // tm),
               pl.BlockSpec((tm, N), lambda j, i: (i, 0)),
               pl.BlockSpec((tk, N), lambda j, i: (j, 0)),
               pl.BlockSpec((tm, tk), lambda j, i: (i, j)),
               (T, Kt), out_dtype, ((tm, N), (tk, N), (tm, tk)), alpha=alpha, after=after)


def mm_droww(name, x, dy, alpha=1.0):
    T, Kt = x.shape
    N = dy.shape[1]
    tt = _tile(T, 2048, 8)
    tk = _tile(Kt, 1408, LANE)
    return _mm(name, "tn", x, dy, (Kt // tk, T // tt),
               pl.BlockSpec((tt, tk), lambda j, t: (t, j)),
               pl.BlockSpec((tt, N), lambda j, t: (t, 0)),
               pl.BlockSpec((tk, N), lambda j, t: (j, 0)),
               (Kt, N), BF16, ((tt, tk), (tt, N), (tk, N)),
               red_axis=1, nred=T // tt, alpha=alpha)


def rms_fwd(name, x, g):
    T, D = x.shape
    tm = _tile(T, 512, 8)

    def body(x_ref, g_ref, o_ref):
        xv = x_ref[...]
        r = lax.rsqrt(jnp.mean(xv * xv, axis=-1, keepdims=True) + EPS)
        o_ref[...] = (xv * r * g_ref[...]).astype(o_ref.dtype)

    return pl.pallas_call(
        body, name=name, grid=(T // tm,),
        in_specs=[pl.BlockSpec((tm, D), lambda i: (i, 0)), pl.BlockSpec((1, D), lambda i: (0, 0))],
        out_specs=pl.BlockSpec((tm, D), lambda i: (i, 0)),
        out_shape=jax.ShapeDtypeStruct((T, D), BF16),
        compiler_params=_params(_nbytes((tm, D), F32) * 2, 4 * _nbytes((tm, D), F32)),
    )(x, g.reshape(1, D))


def _rms_bwd_math(xv, gv, dy):
    r = lax.rsqrt(jnp.mean(xv * xv, axis=-1, keepdims=True) + EPS)
    xh = xv * r
    dyg = dy * gv
    dx = r * (dyg - xh * jnp.mean(dyg * xh, axis=-1, keepdims=True))
    dg = jnp.sum(dy * xh, axis=0, keepdims=True)
    return dx, dg


def rms_bwd(name, x, g, dy, dres=None):
    T, D = x.shape
    tm = _tile(T, 256, 8)
    has_res = dres is not None

    def body(*refs):
        x_ref, g_ref, dy_ref = refs[:3]
        r_ref = refs[3] if has_res else None
        dx_ref, dg_ref = refs[-2:]
        dx, dg = _rms_bwd_math(x_ref[...], g_ref[...], dy_ref[...].astype(F32))
        if has_res:
            dx = r_ref[...] + dx
        dx_ref[...] = dx

        @pl.when(pl.program_id(0) == 0)
        def _():
            dg_ref[...] = dg

        @pl.when(pl.program_id(0) > 0)
        def _():
            dg_ref[...] += dg

    row = pl.BlockSpec((tm, D), lambda i: (i, 0))
    vec = pl.BlockSpec((1, D), lambda i: (0, 0))
    ins, specs = [x, g.reshape(1, D), dy], [row, vec, row]
    if has_res:
        ins.append(dres)
        specs.append(row)
    dx, dg = pl.pallas_call(
        body, name=name, grid=(T // tm,), in_specs=specs, out_specs=[row, vec],
        out_shape=[jax.ShapeDtypeStruct((T, D), F32), jax.ShapeDtypeStruct((1, D), F32)],
        compiler_params=_params(_nbytes((tm, D), F32) * 4, 6 * _nbytes((tm, D), F32)),
    )(*ins)
    return dx, dg.reshape(D)


def dx_norm_bwd(name, dy, w3, x, g, dres=None, pair_layout=False, after=None):
    T = dy.shape[0]
    _, K, nl = w3.shape
    tm = _tile(T, 512, 8)
    chip = _pair_chip if pair_layout else (lambda j: j)
    has_res = dres is not None

    def body(*refs):
        dy_ref, w_ref, x_ref, g_ref = refs[:4]
        r_ref = refs[4] if has_res else None
        dx_ref, dg_ref, acc = refs[-3:]
        i, k = pl.program_id(0), pl.program_id(1)
        p = lax.dot_general(dy_ref[...].astype(BF16), w_ref[...], NT, preferred_element_type=F32)

        @pl.when(k == 0)
        def _():
            acc[...] = p

        @pl.when(k > 0)
        def _():
            acc[...] += p

        @pl.when(k == N_CHIPS - 1)
        def _():
            dx, dg = _rms_bwd_math(x_ref[...], g_ref[...], acc[...])
            dx_ref[...] = r_ref[...] + dx if has_res else dx

            @pl.when(i == 0)
            def _():
                dg_ref[...] = dg

            @pl.when(i > 0)
            def _():
                dg_ref[...] += dg

    row = pl.BlockSpec((tm, K), lambda i, j: (i, 0))
    vec = pl.BlockSpec((1, K), lambda i, j: (0, 0))
    ins = [dy, w3, x, g.reshape(1, K)]
    specs = [pl.BlockSpec((tm, nl), lambda i, j: (i, j)),
             pl.BlockSpec((None, K, nl), lambda i, j: (chip(j), 0, 0)), row, vec]
    if has_res:
        ins.append(dres)
        specs.append(row)
    if after is not None:
        ins.append(after)
        specs.append(pl.BlockSpec(memory_space=pl.ANY))
    blk = _nbytes((tm, nl), dy.dtype) + _nbytes((K, nl), BF16) + (2 + has_res) * _nbytes((tm, K), F32)
    dx, dg = pl.pallas_call(
        body, name=name, grid=(T // tm, N_CHIPS), in_specs=specs, out_specs=[row, vec],
        out_shape=[jax.ShapeDtypeStruct((T, K), F32), jax.ShapeDtypeStruct((1, K), F32)],
        scratch_shapes=[pltpu.VMEM((tm, K), F32)],
        compiler_params=_params(blk, 8 * _nbytes((tm, K), F32)),
    )(*ins)
    return dx, dg.reshape(K)


def ffn_in_act(name, x, w3, after=None):
    T, K = x.shape
    _, _, nl = w3.shape
    tm = _tile(T, 512, 8)

    def body(*refs):
        x_ref, wg_ref, wu_ref = refs[:3]
        u_ref, a_ref = refs[-2:]
        xv = x_ref[...]
        g = jnp.dot(xv, wg_ref[...], preferred_element_type=F32)
        up = jnp.dot(xv, wu_ref[...], preferred_element_type=F32)
        u_ref[:, :nl] = g.astype(u_ref.dtype)
        u_ref[:, nl:] = up.astype(u_ref.dtype)
        a_ref[...] = (g * jax.nn.sigmoid(g) * up).astype(a_ref.dtype)

    blk = _nbytes((tm, K), BF16) + 2 * _nbytes((K, nl), BF16) + _nbytes((tm, 3 * nl), BF16)
    return pl.pallas_call(
        body, name=name, grid=(2, T // tm),
        in_specs=[pl.BlockSpec((tm, K), lambda p, i: (i, 0)),
                  pl.BlockSpec((None, K, nl), lambda p, i: (p, 0, 0)),
                  pl.BlockSpec((None, K, nl), lambda p, i: (p + 2, 0, 0))]
        + [pl.BlockSpec(memory_space=pl.ANY)] * (after is not None),
        out_specs=[pl.BlockSpec((tm, 2 * nl), lambda p, i: (i, p)), pl.BlockSpec((tm, nl), lambda p, i: (i, p))],
        out_shape=[jax.ShapeDtypeStruct((T, 4 * nl), BF16), jax.ShapeDtypeStruct((T, 2 * nl), BF16)],
        compiler_params=_params(blk, 4 * _nbytes((tm, nl), F32)),
    )(x, w3, w3, *([] if after is None else [after]))


def ffn_dact(name, dh, w_out, u, after=None):
    T, N = dh.shape
    F = w_out.shape[0]
    nl = F // 2
    tm = _tile(T, 512, 8)

    def body(*refs):
        d_ref, w_ref, u_ref = refs[:3]
        o_ref = refs[-1]
        dact = 0.5 * lax.dot_general(d_ref[...].astype(BF16), w_ref[...], NT, preferred_element_type=F32)
        g = u_ref[:, :nl].astype(F32)
        up = u_ref[:, nl:].astype(F32)
        sig = jax.nn.sigmoid(g)
        o_ref[:, :nl] = (dact * up * (sig * (1.0 + g * (1.0 - sig)))).astype(o_ref.dtype)
        o_ref[:, nl:] = (dact * (g * sig)).astype(o_ref.dtype)

    ins = [dh, w_out, u]
    specs = [pl.BlockSpec((tm, N), lambda p, i: (i, 0)), pl.BlockSpec((nl, N), lambda p, i: (p, 0)),
             pl.BlockSpec((tm, 2 * nl), lambda p, i: (i, p))]
    if after is not None:
        ins.append(after)
        specs.append(pl.BlockSpec(memory_space=pl.ANY))
    blk = _nbytes((tm, N), F32) + _nbytes((nl, N), BF16) + 2 * _nbytes((tm, 2 * nl), BF16)
    return pl.pallas_call(
        body, name=name, grid=(2, T // tm), in_specs=specs,
        out_specs=pl.BlockSpec((tm, 2 * nl), lambda p, i: (i, p)),
        out_shape=jax.ShapeDtypeStruct((T, 2 * F), BF16),
        compiler_params=_params(blk, 6 * _nbytes((tm, nl), F32)),
    )(*ins)


def loss_head(name, h, g, target):
    T, D = h.shape
    tm = _tile(T, 256, 8)

    def body(h_ref, g_ref, t_ref, dh_ref, dg_ref, loss_ref):
        xv = h_ref[...]
        gv = g_ref[...]
        r = lax.rsqrt(jnp.mean(xv * xv, axis=-1, keepdims=True) + EPS)
        err = xv * r * gv - t_ref[...]
        part = 0.5 * jnp.sum(jnp.mean(err * err, axis=-1, keepdims=True), axis=0, keepdims=True)
        dx, dg = _rms_bwd_math(xv, gv, err * (1.0 / D))
        dh_ref[...] = dx
        part = jnp.broadcast_to(part, (1, LANE))

        @pl.when(pl.program_id(0) == 0)
        def _():
            dg_ref[...] = dg
            loss_ref[...] = part

        @pl.when(pl.program_id(0) > 0)
        def _():
            dg_ref[...] += dg
            loss_ref[...] += part

    row = pl.BlockSpec((tm, D), lambda i: (i, 0))
    vec = pl.BlockSpec((1, D), lambda i: (0, 0))
    dh, dg, loss = pl.pallas_call(
        body, name=name, grid=(T // tm,), in_specs=[row, vec, row],
        out_specs=[row, vec, pl.BlockSpec((1, LANE), lambda i: (0, 0))],
        out_shape=[jax.ShapeDtypeStruct((T, D), F32), jax.ShapeDtypeStruct((1, D), F32),
                   jax.ShapeDtypeStruct((1, LANE), F32)],
        compiler_params=_params(_nbytes((tm, D), F32) * 3, 6 * _nbytes((tm, D), F32)),
    )(h, g.reshape(1, D), target)
    return dh, dg.reshape(D), loss


def rope_tables(S):
    half = ROPE // 2
    freqs = ROPE_THETA ** (-jnp.arange(half, dtype=F32) / half)
    ang = jnp.arange(S, dtype=F32)[:, None] * freqs[None, :]
    cos, sin = jnp.cos(ang), jnp.sin(ang)
    z = jnp.zeros_like(cos)
    ct = jnp.concatenate([cos, cos, z, z], axis=1)
    s1 = jnp.concatenate([-sin, z, z, z], axis=1)
    s2 = jnp.concatenate([z, sin, z, z], axis=1)
    return ct, s1, s2


def _rope_tile(t, ct, s1, s2):
    return t * ct + pltpu.roll(t, 96, 1) * s1 + pltpu.roll(t, 32, 1) * s2


def _rope_tile_bwd(d, ct, s1, s2):
    return d * ct + pltpu.roll(d * s1, 32, 1) + pltpu.roll(d * s2, 96, 1)


def qprep(name, q, tabs, B, S, bwd):
    T, W = q.shape
    nh = W // 256
    ts = _tile(S, 256, 8)
    fn = _rope_tile_bwd if bwd else _rope_tile

    def body(q_ref, ct_ref, s1_ref, s2_ref, o_ref):
        ct, s1, s2 = ct_ref[...], s1_ref[...], s2_ref[...]
        for h in range(nh):
            o_ref[0, :, 256 * h:256 * h + 128] = q_ref[0, :, 256 * h:256 * h + 128].astype(o_ref.dtype)
            t = q_ref[0, :, 256 * h + 128:256 * h + 256].astype(F32)
            o_ref[0, :, 256 * h + 128:256 * h + 256] = fn(t, ct, s1, s2).astype(o_ref.dtype)

    row = pl.BlockSpec((1, ts, W), lambda b, s: (b, s, 0))
    tab = pl.BlockSpec((ts, LANE), lambda b, s: (s, 0))
    out = pl.pallas_call(
        body, name=name, grid=(B, S // ts), in_specs=[row, tab, tab, tab], out_specs=row,
        out_shape=jax.ShapeDtypeStruct((B, S, W), BF16),
        compiler_params=_params(_nbytes((ts, W), F32) * 2, _nbytes((ts, W), F32) * 2),
    )(q.reshape(B, S, W), *tabs)
    return out.reshape(T, W)


def kvprep_fwd(name, ckr, g, tabs, B, S):
    T, W = ckr.shape
    KVL = W - LANE
    ts = _tile(S, 256, 8)

    def body(x_ref, g_ref, ct_ref, s1_ref, s2_ref, c_ref, k_ref):
        xv = x_ref[0, :, :KVL]
        r = lax.rsqrt(jnp.mean(xv * xv, axis=-1, keepdims=True) + EPS)
        c_ref[0] = (xv * r * g_ref[...]).astype(c_ref.dtype)
        k_ref[0] = _rope_tile(x_ref[0, :, KVL:], ct_ref[...], s1_ref[...], s2_ref[...]).astype(k_ref.dtype)

    tab = pl.BlockSpec((ts, LANE), lambda b, s: (s, 0))
    c, k = pl.pallas_call(
        body, name=name, grid=(B, S // ts),
        in_specs=[pl.BlockSpec((1, ts, W), lambda b, s: (b, s, 0)), pl.BlockSpec((1, KVL), lambda b, s: (0, 0)),
                  tab, tab, tab],
        out_specs=[pl.BlockSpec((1, ts, KVL), lambda b, s: (b, s, 0)),
                   pl.BlockSpec((1, ts, LANE), lambda b, s: (b, s, 0))],
        out_shape=[jax.ShapeDtypeStruct((B, S, KVL), BF16), jax.ShapeDtypeStruct((B, S, LANE), BF16)],
        compiler_params=_params(_nbytes((ts, W), F32) * 2, _nbytes((ts, W), F32) * 2),
    )(ckr.reshape(B, S, W), g.reshape(1, KVL), *tabs)
    return c.reshape(T, KVL), k


def kvprep_bwd(name, ckr, g, dc, dkr, tabs, B, S):
    T, W = ckr.shape
    KVL = W - LANE
    ts = _tile(S, 256, 8)

    def body(x_ref, g_ref, dc_ref, dk_ref, ct_ref, s1_ref, s2_ref, o_ref, dg_ref):
        dx, dg = _rms_bwd_math(x_ref[0, :, :KVL], g_ref[...], dc_ref[0])
        o_ref[0, :, :KVL] = dx
        o_ref[0, :, KVL:] = _rope_tile_bwd(dk_ref[0], ct_ref[...], s1_ref[...], s2_ref[...])
        first = (pl.program_id(0) == 0) & (pl.program_id(1) == 0)

        @pl.when(first)
        def _():
            dg_ref[...] = dg

        @pl.when(jnp.logical_not(first))
        def _():
            dg_ref[...] += dg

    tab = pl.BlockSpec((ts, LANE), lambda b, s: (s, 0))
    vec = pl.BlockSpec((1, KVL), lambda b, s: (0, 0))
    o, dg = pl.pallas_call(
        body, name=name, grid=(B, S // ts),
        in_specs=[pl.BlockSpec((1, ts, W), lambda b, s: (b, s, 0)), vec,
                  pl.BlockSpec((1, ts, KVL), lambda b, s: (b, s, 0)),
                  pl.BlockSpec((1, ts, LANE), lambda b, s: (b, s, 0)), tab, tab, tab],
        out_specs=[pl.BlockSpec((1, ts, W), lambda b, s: (b, s, 0)), vec],
        out_shape=[jax.ShapeDtypeStruct((B, S, W), F32), jax.ShapeDtypeStruct((1, KVL), F32)],
        compiler_params=_params(_nbytes((ts, W), F32) * 4, _nbytes((ts, W), F32) * 4),
    )(ckr.reshape(B, S, W), g.reshape(1, KVL), dc.reshape(B, S, KVL), dkr, *tabs)
    return o.reshape(T, W), dg.reshape(KVL)


DIAGS = 768


def _diag_onehot():
    col = lax.broadcasted_iota(I32, (REL_PAD, DIAGS), 1)
    row = lax.broadcasted_iota(I32, (REL_PAD, DIAGS), 0)
    idx = jnp.clip(PADR + QROWS - 1 - col, -MAX_REL, MAX_REL) + MAX_REL
    return (row == idx).astype(F32)


def rel_bias_tile(name, table):
    H = table.shape[0]
    tpad = jnp.pad(table, ((0, 0), (0, REL_PAD - table.shape[1])))

    def body(t_ref, o_ref):
        g = lax.dot_general(t_ref[...], _diag_onehot(), NN, precision=lax.Precision.HIGHEST,
                            preferred_element_type=F32)
        qc = jnp.right_shift(lax.broadcasted_iota(I32, (QROWS, WIN), 0), CHUNK_SHIFT)
        kc = jnp.right_shift(lax.broadcasted_iota(I32, (QROWS, WIN), 1), CHUNK_SHIFT)
        band = (kc >= qc) & (kc <= qc + LEFT_CHUNKS)
        for h in range(H):
            gb = jnp.broadcast_to(g[h:h + 1, :], (QROWS, DIAGS))
            tile = pltpu.roll(gb, DIAGS - (QROWS - 1), 1, stride=1, stride_axis=0)
            o_ref[h // 2, (h % 2) * QROWS:(h % 2 + 1) * QROWS, :] = jnp.where(band, tile[:, :WIN], NEG_INF)

    return pl.pallas_call(
        body, name=name, out_shape=jax.ShapeDtypeStruct((H // 2, 2 * QROWS, WIN), F32),
        compiler_params=_params(0, 2 * _nbytes((H // 2, 2 * QROWS, WIN), F32)),
    )(tpad)


def rel_bias_grad(name, dbias):
    H = 2 * dbias.shape[0]

    def body(d_ref, o_ref):
        flip = (lax.broadcasted_iota(I32, (QROWS, QROWS), 0) + lax.broadcasted_iota(I32, (QROWS, QROWS), 1)
                == QROWS - 1).astype(F32)
        rows = []
        for h in range(H):
            x = d_ref[h // 2, (h % 2) * QROWS:(h % 2 + 1) * QROWS, :]
            xr = lax.dot_general(flip, x, NN, precision=lax.Precision.HIGHEST, preferred_element_type=F32)
            xp = jnp.concatenate([xr, jnp.zeros((QROWS, DIAGS - WIN), F32)], axis=1)
            y = pltpu.roll(xp, 0, 1, stride=1, stride_axis=0)
            rows.append(jnp.sum(y, axis=0, keepdims=True))
        o_ref[...] = lax.dot_general(jnp.concatenate(rows, axis=0), _diag_onehot(), NT,
                                     precision=lax.Precision.HIGHEST, preferred_element_type=F32)

    return pl.pallas_call(
        body, name=name, out_shape=jax.ShapeDtypeStruct((H, REL_PAD), F32),
        compiler_params=_params(0, 2 * _nbytes(dbias.shape, F32)),
    )(dbias)


def _stack_pair(xp):
    lane = lax.broadcasted_iota(I32, xp.shape, 1)
    z = jnp.zeros_like(xp)
    return jnp.concatenate([jnp.where(lane < HEAD_DIM_A, xp, z), jnp.where(lane >= HEAD_DIM_A, xp, z)], axis=0)


def _unstack_pair(y):
    lane = lax.broadcasted_iota(I32, (QROWS, LANE), 1)
    return jnp.where(lane < HEAD_DIM_A, y[:QROWS], y[QROWS:])


def _attn_a_rowpen(j):
    w = lax.broadcasted_iota(I32, (1, WIN), 1)
    return jnp.where(w >= PADR - QROWS * j, 0.0, NEG_INF).astype(F32)


def _attn_a_load_bias(bias_hbm, bias_v, sem):
    cp = pltpu.make_async_copy(bias_hbm, bias_v, sem)
    cp.start()
    cp.wait()


def _attn_a_load_kv(qkv_hbm, b, kpad, vpad, sem, S, D):
    kpad[0:PADR, :] = jnp.zeros((PADR, D), BF16)
    vpad[0:PADR, :] = jnp.zeros((PADR, D), BF16)
    ck = pltpu.make_async_copy(qkv_hbm.at[b, :, pl.ds(D, D)], kpad.at[pl.ds(PADR, S), :], sem.at[0])
    cv = pltpu.make_async_copy(qkv_hbm.at[b, :, pl.ds(2 * D, D)], vpad.at[pl.ds(PADR, S), :], sem.at[1])
    ck.start()
    cv.start()
    ck.wait()
    cv.wait()


def _attn_a_exp(q2s, kp, bias, pen):
    s = lax.dot_general(q2s, kp, NT, preferred_element_type=F32) + bias + pen
    e = jnp.exp(s - jnp.max(s, axis=-1, keepdims=True))
    return e, 1.0 / jnp.sum(e, axis=-1, keepdims=True)


def attn_a_fwd(name, qkv, bias):
    B, S, D3 = qkv.shape
    D = D3 // 3
    H = D // HEAD_DIM_A
    nb = S // QROWS
    scale = HEAD_DIM_A ** -0.5

    def body(q_ref, bias_hbm, qkv_hbm, o_ref, kpad, vpad, bias_v, sem):
        b, j = pl.program_id(0), pl.program_id(1)

        @pl.when((b == 0) & (j == 0))
        def _():
            _attn_a_load_bias(bias_hbm, bias_v, sem.at[2])

        @pl.when(j == 0)
        def _():
            _attn_a_load_kv(qkv_hbm, b, kpad, vpad, sem, S, D)

        pen = _attn_a_rowpen(j)
        w0 = pl.multiple_of(j * QROWS, QROWS)
        for p in range(H // 2):
            ls = slice(p * LANE, (p + 1) * LANE)
            e, rl = _attn_a_exp(_stack_pair(q_ref[0, :, ls] * scale), kpad[pl.ds(w0, WIN), ls], bias_v[p], pen)
            o2 = jnp.dot(e.astype(BF16), vpad[pl.ds(w0, WIN), ls], preferred_element_type=F32) * rl
            o_ref[0, :, ls] = _unstack_pair(o2).astype(o_ref.dtype)

    scr = 2 * _nbytes((PADR + S, D), BF16) + _nbytes(bias.shape, F32) + 8 * _nbytes((2 * QROWS, WIN), F32)
    return pl.pallas_call(
        body, name=name, grid=(B, nb),
        in_specs=[pl.BlockSpec((1, QROWS, D), lambda b, j: (b, j, 0)),
                  pl.BlockSpec(memory_space=pl.ANY), pl.BlockSpec(memory_space=pl.ANY)],
        out_specs=pl.BlockSpec((1, QROWS, D), lambda b, j: (b, j, 0)),
        out_shape=jax.ShapeDtypeStruct((B, S, D), BF16),
        scratch_shapes=[pltpu.VMEM((PADR + S, D), BF16), pltpu.VMEM((PADR + S, D), BF16),
                        pltpu.VMEM(bias.shape, F32), pltpu.SemaphoreType.DMA((3,))],
        compiler_params=_params(2 * _nbytes((QROWS, D), BF16), scr),
    )(qkv, bias, qkv)


def attn_a_bwd(name, qkv, do, bias):
    B, S, D3 = qkv.shape
    D = D3 // 3
    H = D // HEAD_DIM_A
    nb = S // QROWS
    scale = HEAD_DIM_A ** -0.5

    def body(q_ref, do_ref, bias_hbm, qkv_hbm, dqkv_hbm, dbias_hbm, kpad, vpad, dkacc, dvacc, bias_v, dbias_v,
             dq_stage, sem):
        b, j = pl.program_id(0), pl.program_id(1)
        step = b * nb + j
        slot = lax.rem(step, 2)

        def dq_out(s):
            return pltpu.make_async_copy(dq_stage.at[s], dqkv_hbm.at[b, pl.ds(j * QROWS, QROWS), pl.ds(0, D)],
                                         sem.at[3 + s])

        @pl.when(step >= 2)
        def _():
            dq_out(slot).wait()

        @pl.when((b == 0) & (j == 0))
        def _():
            _attn_a_load_bias(bias_hbm, bias_v, sem.at[2])
            dbias_v[...] = jnp.zeros_like(dbias_v)

        @pl.when(j == 0)
        def _():
            _attn_a_load_kv(qkv_hbm, b, kpad, vpad, sem, S, D)
            dkacc[...] = jnp.zeros_like(dkacc)
            dvacc[...] = jnp.zeros_like(dvacc)

        pen = _attn_a_rowpen(j)
        w0 = pl.multiple_of(j * QROWS, QROWS)
        for p in range(H // 2):
            ls = slice(p * LANE, (p + 1) * LANE)
            q2s = _stack_pair(q_ref[0, :, ls] * scale)
            do2 = _stack_pair(do_ref[0, :, ls])
            kp = kpad[pl.ds(w0, WIN), ls]
            vp = vpad[pl.ds(w0, WIN), ls]
            e, rl = _attn_a_exp(q2s, kp, bias_v[p], pen)
            pr = e * rl
            dp = lax.dot_general(do2, vp, NT, preferred_element_type=F32)
            ds = pr * (dp - jnp.sum(pr * dp, axis=-1, keepdims=True))
            dbias_v[p] += ds
            dsb = ds.astype(BF16)
            dq_stage[slot, :, ls] = _unstack_pair(jnp.dot(dsb, kp, preferred_element_type=F32)) * scale
            dkacc[pl.ds(w0, WIN), ls] += lax.dot_general(dsb, q2s, TN, preferred_element_type=F32)
            dvacc[pl.ds(w0, WIN), ls] += lax.dot_general(pr.astype(BF16), do2, TN, preferred_element_type=F32)

        dq_out(slot).start()

        @pl.when(j == nb - 1)
        def _():
            ck = pltpu.make_async_copy(dkacc.at[pl.ds(PADR, S), :], dqkv_hbm.at[b, :, pl.ds(D, D)], sem.at[0])
            cv = pltpu.make_async_copy(dvacc.at[pl.ds(PADR, S), :], dqkv_hbm.at[b, :, pl.ds(2 * D, D)], sem.at[1])
            ck.start()
            cv.start()
            ck.wait()
            cv.wait()

        @pl.when((b == B - 1) & (j == nb - 1))
        def _():
            cb = pltpu.make_async_copy(dbias_v, dbias_hbm, sem.at[2])
            cb.start()
            dq_out(0).wait()
            dq_out(1).wait()
            cb.wait()

    blk = _nbytes((QROWS, D), BF16) * 2
    scr = (2 * _nbytes((PADR + S, D), BF16) + 2 * _nbytes((PADR + S, D), F32) + 2 * _nbytes(bias.shape, F32)
           + 8 * _nbytes((2 * QROWS, WIN), F32) + 2 * _nbytes((QROWS, D), F32))
    return pl.pallas_call(
        body, name=name, grid=(B, nb),
        in_specs=[pl.BlockSpec((1, QROWS, D), lambda b, j: (b, j, 0)),
                  pl.BlockSpec((1, QROWS, D), lambda b, j: (b, j, 0)),
                  pl.BlockSpec(memory_space=pl.ANY), pl.BlockSpec(memory_space=pl.ANY)],
        out_specs=[pl.BlockSpec(memory_space=pl.ANY), pl.BlockSpec(memory_space=pl.ANY)],
        out_shape=[jax.ShapeDtypeStruct((B, S, 3 * D), F32), jax.ShapeDtypeStruct(bias.shape, F32)],
        scratch_shapes=[pltpu.VMEM((PADR + S, D), BF16), pltpu.VMEM((PADR + S, D), BF16),
                        pltpu.VMEM((PADR + S, D), F32), pltpu.VMEM((PADR + S, D), F32),
                        pltpu.VMEM(bias.shape, F32), pltpu.VMEM(bias.shape, F32),
                        pltpu.VMEM((2, QROWS, D), F32), pltpu.SemaphoreType.DMA((5,))],
        compiler_params=_params(blk, scr),
    )(qkv, do, bias, qkv)


def _mla_raw_t(k2, kj, q, QB):
    return lax.dot_general(k2[_blk(kj, QB), :], q, NT, preferred_element_type=F32)


def _blk(kj, QB):
    return pl.ds(kj * QB, QB) if isinstance(kj, int) else pl.ds(pl.multiple_of(kj * QB, QB), QB)


def _mla_diag_pen(QB):
    kc = jnp.right_shift(lax.broadcasted_iota(I32, (QB, QB), 0), CHUNK_SHIFT)
    qc = jnp.right_shift(lax.broadcasted_iota(I32, (QB, QB), 1), CHUNK_SHIFT)
    return jnp.where(kc <= qc, 0.0, NEG_INF).astype(F32)


def _mla_fill_keys(kv_ref, kr_ref, k2):
    k2[:, :NOPE] = kv_ref[0, :, :NOPE]
    k2[:, NOPE:] = kr_ref[0]


def _t(x):
    return x.astype(F32).T


def mla_fwd(name, qf, kv, kr):
    B, S, W = qf.shape
    HB = W // 256
    QB = _tile(S, 256, CHUNK)
    nq = S // QB
    scale = (NOPE + ROPE) ** -0.5

    def body(q_ref, kv_ref, kr_ref, o_ref, lse_ref, k2, vt, st_buf, pen):
        qi = pl.program_id(2)

        @pl.when(qi == 0)
        def _():
            pen[...] = _mla_diag_pen(QB)
            _mla_fill_keys(kv_ref, kr_ref, k2)
            for kj in range(nq):
                vt[kj] = _t(kv_ref[0, kj * QB:(kj + 1) * QB, NOPE:]).astype(BF16)

        q = q_ref[0]
        st_buf[0] = _mla_raw_t(k2, 0, q, QB)

        def step(kj, carry):
            m, l, acc = carry
            cur = lax.rem(kj, 2)
            st_raw = st_buf[cur]
            st_buf[1 - cur] = _mla_raw_t(k2, jnp.minimum(kj + 1, qi), q, QB)
            st = st_raw * scale + jnp.where(kj == qi, pen[...], 0.0)
            m_new = jnp.maximum(m, jnp.max(st, axis=0, keepdims=True))
            a = jnp.exp(m - m_new)
            pt = jnp.exp(st - m_new)
            l = a * l + jnp.sum(pt, axis=0, keepdims=True)
            acc = a * acc + jnp.dot(vt[kj], pt.astype(BF16), preferred_element_type=F32)
            return m_new, l, acc

        init = (jnp.full((1, QB), NEG_INF, F32), jnp.zeros((1, QB), F32), jnp.zeros((NOPE, QB), F32))
        m, l, acc = lax.fori_loop(0, qi + 1, step, init)
        o_ref[0] = (acc * (1.0 / l)).T
        lse_ref[0, 0] = m + jnp.log(l)

    blk = (_nbytes((QB, 256), BF16) + _nbytes((S, 256), BF16) + _nbytes((S, LANE), BF16)
           + _nbytes((QB, LANE), F32))
    return pl.pallas_call(
        body, name=name, grid=(B, HB, nq),
        in_specs=[pl.BlockSpec((1, QB, 256), lambda b, h, i: (b, i, h)),
                  pl.BlockSpec((1, S, 256), lambda b, h, i: (b, 0, h)),
                  pl.BlockSpec((1, S, LANE), lambda b, h, i: (b, 0, 0))],
        out_specs=[pl.BlockSpec((1, QB, LANE), lambda b, h, i: (b, i, h)),
                   pl.BlockSpec((1, 1, 1, QB), lambda b, h, i: (b, h, 0, i))],
        out_shape=[jax.ShapeDtypeStruct((B, S, HB * LANE), F32), jax.ShapeDtypeStruct((B, HB, 1, S), F32)],
        scratch_shapes=[pltpu.VMEM((S, 256), BF16), pltpu.VMEM((nq, NOPE, QB), BF16),
                        pltpu.VMEM((2, QB, QB), F32), pltpu.VMEM((QB, QB), F32)],
        compiler_params=_params(blk, 2 * _nbytes((S, 256), BF16) + 10 * _nbytes((QB, QB), F32)),
    )(qf, kv, kr)


def mla_bwd(name, qf, kv, kr, do, o, lse):
    B, S, W = qf.shape
    HB = W // 256
    QB = _tile(S, 256, CHUNK)
    nq = S // QB
    scale = (NOPE + ROPE) ** -0.5

    def body(q_ref, kv_ref, kr_ref, do_ref, o_ref, lse_ref, dq_ref, dkv_ref, dkr_ref, k2, kt, dot_, delta, dqt,
             st_buf, dp_buf, pen):
        h = pl.program_id(1)
        pen[...] = _mla_diag_pen(QB)
        dkv_ref[...] = jnp.zeros_like(dkv_ref)

        @pl.when(h == 0)
        def _():
            dkr_ref[...] = jnp.zeros_like(dkr_ref)

        _mla_fill_keys(kv_ref, kr_ref, k2)
        for i in range(nq):
            rows = slice(i * QB, (i + 1) * QB)
            kt[i] = _t(k2[rows, :]).astype(BF16)
            dot32 = _t(do_ref[0, rows, :])
            delta[i] = jnp.sum(dot32 * o_ref[0, rows, :].T, axis=0, keepdims=True)
            dot_[i] = dot32.astype(BF16)

        for qi in range(nq):
            rows = slice(qi * QB, (qi + 1) * QB)
            q = q_ref[0, rows, :]
            dob = do_ref[0, rows, :]
            lse_q = lse_ref[0, 0, :, rows]
            delta_q = delta[qi]
            dqt[...] = jnp.zeros_like(dqt)

            def raw(kj, slot, q=q, qi=qi):
                st_buf[slot] = _mla_raw_t(k2, kj, q, QB)
                dp_buf[slot] = jnp.dot(kv_ref[0, _blk(kj, QB), NOPE:], dot_[qi], preferred_element_type=F32)

            raw(0, 0)

            def step(kj, carry, q=q, dob=dob, lse_q=lse_q, delta_q=delta_q, qi=qi, raw=raw):
                ks = pl.ds(pl.multiple_of(kj * QB, QB), QB)
                cur = lax.rem(kj, 2)
                st_raw, dp_raw = st_buf[cur], dp_buf[cur]
                raw(jnp.minimum(kj + 1, qi), 1 - cur)
                pt = jnp.exp(st_raw * scale + jnp.where(kj == qi, pen[...], 0.0) - lse_q)
                dst = (pt * (dp_raw - delta_q) * scale).astype(BF16)
                dkv_ref[0, ks, NOPE:] += jnp.dot(pt.astype(BF16), dob, preferred_element_type=F32)
                dk2 = jnp.dot(dst, q, preferred_element_type=F32)
                dkv_ref[0, ks, :NOPE] += dk2[:, :NOPE]
                dkr_ref[0, ks, :] += dk2[:, NOPE:]
                dqt[...] += jnp.dot(kt[kj], dst, preferred_element_type=F32)
                return carry

            lax.fori_loop(0, qi + 1, step, 0)
            dq_ref[0, rows, :] = dqt[...].T

    head = lambda w: pl.BlockSpec((1, S, w), lambda b, h: (b, 0, h))
    shared = pl.BlockSpec((1, S, LANE), lambda b, h: (b, 0, 0))
    blk = (2 * _nbytes((S, 256), BF16) + 2 * _nbytes((S, LANE), BF16) + _nbytes((S, LANE), F32)
           + 2 * _nbytes((S, 256), F32) + _nbytes((S, LANE), F32))
    scr = 3 * _nbytes((S, 256), BF16) + 14 * _nbytes((QB, QB), F32)
    return pl.pallas_call(
        body, name=name, grid=(B, HB),
        in_specs=[head(256), head(256), shared, head(LANE), head(LANE),
                  pl.BlockSpec((1, 1, 1, S), lambda b, h: (b, h, 0, 0))],
        out_specs=[head(256), head(256), shared],
        out_shape=[jax.ShapeDtypeStruct((B, S, W), F32), jax.ShapeDtypeStruct((B, S, W), F32),
                   jax.ShapeDtypeStruct((B, S, LANE), F32)],
        scratch_shapes=[pltpu.VMEM((S, 256), BF16), pltpu.VMEM((nq, 256, QB), BF16),
                        pltpu.VMEM((nq, NOPE, QB), BF16), pltpu.VMEM((nq, 1, QB), F32),
                        pltpu.VMEM((256, QB), F32), pltpu.VMEM((2, QB, QB), F32), pltpu.VMEM((2, QB, QB), F32),
                        pltpu.VMEM((QB, QB), F32)],
        compiler_params=_params(blk, scr),
    )(qf, kv, kr, do, o, lse)


GROUP_STEPS = 4


def cast_group(name, ws, layers, idx, after=None):
    n = len(ws)
    n_in = n + (after is not None)

    def body(k_ref, *refs):
        for i in range(n):
            refs[n_in + i][...] = refs[i][...].astype(BF16)

    def spec_in(w, layer):
        return pl.BlockSpec((None, w.shape[1] // GROUP_STEPS, w.shape[2]), lambda r, k_ref: (layer, r, 0))

    def spec_out(w):
        return pl.BlockSpec((None, w.shape[1] // GROUP_STEPS, w.shape[2]), lambda r, k_ref: (k_ref[0], r, 0))

    return pl.pallas_call(
        body, name=name,
        grid_spec=pltpu.PrefetchScalarGridSpec(
            num_scalar_prefetch=1, grid=(GROUP_STEPS,),
            in_specs=([spec_in(w, l) for w, l in zip(ws, layers)]
                      + [pl.BlockSpec(memory_space=pl.ANY)] * (after is not None)),
            out_specs=[spec_out(w) for w in ws]),
        out_shape=[jax.ShapeDtypeStruct((N_CHIPS, *w.shape[1:]), BF16) for w in ws],
        compiler_params=_params(sum(_nbytes(w.shape[1:], F32) * 3 // 2 for w in ws) // GROUP_STEPS),
    )(idx, *ws, *([] if after is None else [after]))


def adamw(name, w, g, m, v):
    R, C = w.shape
    tr = _tile(R, max(8, (1 << 18) // C // 8 * 8), 8)
    c1 = 1.0 - ADAM_B1 ** ADAM_STEP
    c2 = 1.0 - ADAM_B2 ** ADAM_STEP

    def body(w_ref, g_ref, m_ref, v_ref, d_ref, mo_ref, vo_ref):
        gv = g_ref[...]
        mn = ADAM_B1 * m_ref[...] + (1.0 - ADAM_B1) * gv
        vn = ADAM_B2 * v_ref[...] + (1.0 - ADAM_B2) * (gv * gv)
        mo_ref[...] = mn
        vo_ref[...] = vn
        d_ref[...] = -ADAM_LR * ((mn / c1) / (jnp.sqrt(vn / c2) + ADAM_EPS) + ADAM_WD * w_ref[...])

    spec = pl.BlockSpec((tr, C), lambda r: (r, 0))
    return pl.pallas_call(
        body, name=name, grid=(R // tr,), in_specs=[spec] * 4, out_specs=[spec] * 3,
        out_shape=[jax.ShapeDtypeStruct((R, C), F32)] * 3,
        compiler_params=_params(7 * _nbytes((tr, C), F32), 4 * _nbytes((tr, C), F32)),
    )(w, g, m, v)


def half_sum_group(name, dws, landed, idx):
    n = len(dws)
    steps = GROUP_STEPS // 2

    def body(i_ref, *refs):
        for i in range(n):
            refs[2 * n + i][...] = (refs[i][...].astype(F32) + refs[n + i][...].astype(F32)).astype(BF16)

    def own(d):
        return pl.BlockSpec((None, None, d.shape[2] // steps, d.shape[3]), lambda k, r, i_ref: (k, i_ref[1], r, 0))

    def flat(d):
        return pl.BlockSpec((None, d.shape[2] // steps, d.shape[3]), lambda k, r, i_ref: (k, r, 0))

    return pl.pallas_call(
        body, name=name,
        grid_spec=pltpu.PrefetchScalarGridSpec(
            num_scalar_prefetch=1, grid=(N_CHIPS, steps),
            in_specs=[own(d) for d in dws] + [flat(d) for d in dws], out_specs=[flat(d) for d in dws]),
        out_shape=[jax.ShapeDtypeStruct((N_CHIPS, *d.shape[2:]), BF16) for d in dws],
        compiler_params=_params(sum(3 * _nbytes(d.shape[2:], BF16) for d in dws) // steps),
    )(idx, *dws, *landed)


def chip_sum_group(name, parts, landed, gbufs, layers, idx):
    n = len(parts)
    steps = GROUP_STEPS // 2

    def body(i_ref, *refs):
        for i in range(n):
            a, b = refs[i], refs[n + i]
            refs[3 * n + i][...] = ((a[...].astype(F32) + b[0].astype(F32)) + b[1].astype(F32)) + b[2].astype(F32)

    def mine(p):
        return pl.BlockSpec((None, p.shape[1] // steps, p.shape[2]), lambda r, i_ref: (i_ref[0], r, 0))

    def three(p):
        return pl.BlockSpec((3, p.shape[1] // steps, p.shape[2]), lambda r, i_ref: (0, r, 0))

    def out(p, layer):
        return pl.BlockSpec((None, None, p.shape[1] // steps, p.shape[2]), lambda r, i_ref: (layer, i_ref[1], r, 0))

    return pl.pallas_call(
        body, name=name,
        grid_spec=pltpu.PrefetchScalarGridSpec(
            num_scalar_prefetch=1, grid=(steps,),
            in_specs=[mine(p) for p in parts] + [three(p) for p in parts] + [pl.BlockSpec(memory_space=pl.ANY)] * n,
            out_specs=[out(p, l) for p, l in zip(parts, layers)]),
        out_shape=[jax.ShapeDtypeStruct(g.shape, F32) for g in gbufs],
        input_output_aliases={1 + 2 * n + i: i for i in range(n)},
        compiler_params=_params(sum(6 * _nbytes(p.shape[1:], BF16) for p in parts) // steps),
    )(idx, *parts, *landed, *gbufs)


ANY = pl.BlockSpec(memory_space=pl.ANY)


def _place():
    x, y, c = lax.axis_index("x"), lax.axis_index("y"), lax.axis_index("c")
    chips = [(1 - x, y), (x, 1 - y), (1 - x, 1 - y)]
    return x, y, c, chips


HBM = pl.BlockSpec(memory_space=pltpu.HBM)
SEM = pl.BlockSpec(memory_space=pltpu.SEMAPHORE)
EFFECT = pltpu.SideEffectType.DATAFLOW_SIDE_EFFECTING


def _in_hbm(a):
    return pltpu.with_memory_space_constraint(a, pltpu.HBM)


def _ici_copy(src, dst, send_sems, recv_sems, k, peer):
    return pltpu.make_async_remote_copy(src_ref=src, dst_ref=dst, send_sem=send_sems.at[k], recv_sem=recv_sems.at[k],
                                        device_id=peer, device_id_type=MESH)


def ici_start(name, bufs, lands, after, gather):
    n, nl = len(bufs), len(lands)

    def body(*refs):
        b_in = refs[:n]
        send_sems, recv_sems = refs[n + nl + 1], refs[n + nl + 2]
        b_out = refs[n + nl + 3:2 * n + nl + 3]
        l_out = refs[2 * n + nl + 3:2 * n + 2 * nl + 3]
        token = refs[-1]
        x, y, c, chips = _place()
        kme = 2 * x + y
        for i in range(n):
            for j in range(3):
                peer = (*chips[j], c)
                if gather:
                    _ici_copy(b_out[i].at[kme, c], b_out[i].at[kme, c], send_sems, recv_sems, 3 * i + j, peer).start()
                else:
                    kd = 2 * chips[j][0] + chips[j][1]
                    _ici_copy(b_out[i].at[kd], l_out[i].at[j], send_sems, recv_sems, 3 * i + j, peer).start()
        token[...] = jnp.zeros_like(token)

    arrays = [*bufs, *lands]
    outs = pl.pallas_call(
        body, name=name,
        in_specs=[HBM] * (n + nl) + [ANY],
        out_specs=(SEM, SEM, *[HBM] * (n + nl), pl.BlockSpec(memory_space=pltpu.VMEM)),
        out_shape=(pltpu.SemaphoreType.DMA((3 * n,)), pltpu.SemaphoreType.DMA((3 * n,)),
                   *[pltpu.HBM(a.shape, a.dtype) for a in arrays], jax.ShapeDtypeStruct((8, LANE), F32)),
        input_output_aliases={i: 2 + i for i in range(n + nl)},
        compiler_params=pltpu.CompilerParams(has_side_effects=EFFECT),
    )(*[_in_hbm(a) for a in arrays], after)
    return outs[0], outs[1], list(outs[2:2 + n]), list(outs[2 + n:2 + n + nl]), outs[-1]


def ici_wait(name, send_sems, recv_sems, bufs, lands, after, gather):
    n, nl = len(bufs), len(lands)

    def body(*refs):
        b_in, l_in = refs[:n], refs[n:n + nl]
        send_sems, recv_sems = refs[n + nl], refs[n + nl + 1]
        x, y, c, chips = _place()
        kme = 2 * x + y
        for i in range(n):
            for j in range(3):
                peer = (*chips[j], c)
                kj = 2 * chips[j][0] + chips[j][1]
                if gather:
                    _ici_copy(b_in[i].at[kme, c], b_in[i].at[kme, c], send_sems, recv_sems, 3 * i + j, peer).wait_send()
                    _ici_copy(b_in[i].at[kj, c], b_in[i].at[kj, c], send_sems, recv_sems, 3 * i + j, peer).wait_recv()
                else:
                    _ici_copy(b_in[i].at[kj], l_in[i].at[j], send_sems, recv_sems, 3 * i + j, peer).wait_send()
                    _ici_copy(b_in[i].at[kj], l_in[i].at[j], send_sems, recv_sems, 3 * i + j, peer).wait_recv()

    arrays = [*bufs, *lands]
    outs = pl.pallas_call(
        body, name=name,
        in_specs=[HBM] * (n + nl) + [SEM, SEM, ANY],
        out_specs=tuple([HBM] * (n + nl)),
        out_shape=tuple(pltpu.HBM(a.shape, a.dtype) for a in arrays),
        input_output_aliases={i: i for i in range(n + nl)},
        compiler_params=pltpu.CompilerParams(has_side_effects=EFFECT),
    )(*arrays, send_sems, recv_sems, after)
    return list(outs[:n]), list(outs[n:])


def gather_pair_pass(name, bufs):
    n = len(bufs)

    def body(*refs):
        b = refs[n:2 * n]
        send_sems, recv_sems = refs[2 * n:]
        x, y, c, chips = _place()
        sib = (x, y, 1 - c)

        def d2d(i, j, which):
            kj = 2 * chips[j][0] + chips[j][1]
            return _ici_copy(b[i].at[kj, which], b[i].at[kj, which], send_sems, recv_sems, 3 * i + j, sib)

        for i in range(n):
            for j in range(3):
                d2d(i, j, c).start()
        for i in range(n):
            for j in range(3):
                d2d(i, j, 1 - c).wait_recv()
        for i in range(n):
            for j in range(3):
                d2d(i, j, c).wait_send()

    return pl.pallas_call(
        body, name=name, in_specs=[ANY] * n, out_specs=[ANY] * n,
        out_shape=[jax.ShapeDtypeStruct(a.shape, a.dtype) for a in bufs],
        input_output_aliases={i: i for i in range(n)},
        scratch_shapes=[pltpu.SemaphoreType.DMA((3 * n,)), pltpu.SemaphoreType.DMA((3 * n,))],
    )(*bufs)


def pair_exchange(name, dws):
    n = len(dws)

    def body(*refs):
        ins, outs = refs[:n], refs[n:2 * n]
        send_sems, recv_sems = refs[2 * n:]
        x, y, c, _ = _place()
        copies = []
        for i in range(n):
            copies.append(pltpu.make_async_remote_copy(
                src_ref=ins[i].at[:, 1 - c], dst_ref=outs[i],
                send_sem=send_sems.at[i], recv_sem=recv_sems.at[i],
                device_id=(x, y, 1 - c), device_id_type=MESH))
            copies[i].start()
        for cp in copies:
            cp.wait_recv()
        for cp in copies:
            cp.wait_send()

    return pl.pallas_call(
        body, name=name, in_specs=[ANY] * n, out_specs=[ANY] * n,
        out_shape=[jax.ShapeDtypeStruct((N_CHIPS, *d.shape[2:]), d.dtype) for d in dws],
        scratch_shapes=[pltpu.SemaphoreType.DMA((n,)), pltpu.SemaphoreType.DMA((n,))],
    )(*dws)


def pair_assemble(gbufs):
    n = len(gbufs)

    def body(*refs):
        bufs = refs[n:2 * n]
        send_sems, recv_sems = refs[2 * n:]
        x, y, c, _ = _place()
        copies = []
        for i in range(n):
            copies.append(pltpu.make_async_remote_copy(
                src_ref=bufs[i].at[:, c], dst_ref=bufs[i].at[:, c],
                send_sem=send_sems.at[i], recv_sem=recv_sems.at[i],
                device_id=(x, y, 1 - c), device_id_type=MESH))
            copies[i].start()
        for i in range(n):
            pltpu.make_async_remote_copy(
                src_ref=bufs[i].at[:, 1 - c], dst_ref=bufs[i].at[:, 1 - c],
                send_sem=send_sems.at[i], recv_sem=recv_sems.at[i],
                device_id=(x, y, 1 - c), device_id_type=MESH).wait_recv()
        for cp in copies:
            cp.wait_send()

    return pl.pallas_call(
        body, name="grad_pair_assemble", in_specs=[ANY] * n, out_specs=[ANY] * n,
        out_shape=[jax.ShapeDtypeStruct(g.shape, g.dtype) for g in gbufs],
        input_output_aliases={i: i for i in range(n)},
        scratch_shapes=[pltpu.SemaphoreType.DMA((n,)), pltpu.SemaphoreType.DMA((n,))],
    )(*gbufs)


def all_reduce_small(vec):
    NR = vec.shape[0]
    flips = [(fx, fy, fc) for fx in (0, 1) for fy in (0, 1) for fc in (0, 1)][1:]

    def body(v_ref, o_ref, buf, send_sems, recv_sems):
        x, y, c, _ = _place()
        me = 4 * x + 2 * y + c
        buf[me] = v_ref[...]
        copies = []
        for j, (fx, fy, fc) in enumerate(flips):
            peer = (1 - x if fx else x, 1 - y if fy else y, 1 - c if fc else c)
            copies.append(pltpu.make_async_remote_copy(
                src_ref=v_ref, dst_ref=buf.at[me], send_sem=send_sems.at[j], recv_sem=recv_sems.at[j],
                device_id=peer, device_id_type=MESH))
            copies[j].start()
        for cp in copies:
            cp.wait_recv()
        for cp in copies:
            cp.wait_send()
        acc = buf[0]
        for d in range(1, 8):
            acc = acc + buf[d]
        o_ref[...] = acc

    return pl.pallas_call(
        body, name="all_reduce_small",
        in_specs=[pl.BlockSpec(memory_space=pltpu.VMEM)], out_specs=pl.BlockSpec(memory_space=pltpu.VMEM),
        out_shape=jax.ShapeDtypeStruct((NR, LANE), F32),
        scratch_shapes=[pltpu.VMEM((8, NR, LANE), F32), pltpu.SemaphoreType.DMA((7,)),
                        pltpu.SemaphoreType.DMA((7,))],
    )(vec)


def _pack(arrays):
    flat = jnp.concatenate([a.reshape(-1).astype(F32) for a in arrays])
    n = flat.shape[0]
    npad = -(-n // (8 * LANE)) * (8 * LANE)
    return jnp.pad(flat, (0, npad - n)).reshape(npad // LANE, LANE)


def _unpack(buf, like):
    flat = buf.reshape(-1)
    out, off = [], 0
    for a in like:
        out.append(flat[off:off + a.size].reshape(a.shape))
        off += a.size
    return out


def kernel(x, ffn1_norm, ffn1_w_in, ffn1_w_out, mix_norm, ffn2_norm, ffn2_w_in, ffn2_w_out, a_w_qkv, a_rel_bias, a_w_o, kv_norm, kv_w_down, kv_latent_norm, kv_w_up, b_w_dq, b_q_norm, b_w_uq, b_w_o, final_norm, loss_target, m_ffn1_norm, m_ffn1_w_in, m_ffn1_w_out, m_mix_norm, m_ffn2_norm, m_ffn2_w_in, m_ffn2_w_out, m_a_w_qkv, m_a_rel_bias, m_a_w_o, m_kv_norm, m_kv_w_down, m_kv_latent_norm, m_kv_w_up, m_b_w_dq, m_b_q_norm, m_b_w_uq, m_b_w_o, m_final_norm, v_ffn1_norm, v_ffn1_w_in, v_ffn1_w_out, v_mix_norm, v_ffn2_norm, v_ffn2_w_in, v_ffn2_w_out, v_a_w_qkv, v_a_rel_bias, v_a_w_o, v_kv_norm, v_kv_w_down, v_kv_latent_norm, v_kv_w_up, v_b_w_dq, v_b_q_norm, v_b_w_uq, v_b_w_o, v_final_norm):
    B, S, D = x.shape
    T = B * S
    HB = D // 128
    QL = b_q_norm.shape[-1]
    KVL = kv_latent_norm.shape[0]
    hpc = HB // N_CHIPS
    tabs = rope_tables(S)
    idx = jnp.stack([2 * lax.axis_index("x") + lax.axis_index("y"), lax.axis_index("c")]).astype(I32)

    def halves(a):
        return a.reshape(*a.shape[:-2], 2, a.shape[-2] // 2, a.shape[-1])

    def whole(a):
        return a.reshape(*a.shape[:-3], 2 * a.shape[-2], a.shape[-1])

    kv_w_down_p = jnp.pad(kv_w_down, ((0, 0), (0, LANE - ROPE)))[None]
    b_w_uq_p = jnp.pad(b_w_uq.reshape(1, QL, hpc, NOPE + ROPE),
                       ((0, 0), (0, 0), (0, 0), (0, LANE - ROPE))).reshape(1, QL, hpc * 256)
    sharded = [("ffn1_w_in", ffn1_w_in), ("ffn1_w_out", ffn1_w_out), ("ffn2_w_in", ffn2_w_in),
               ("ffn2_w_out", ffn2_w_out), ("a_w_qkv", a_w_qkv), ("a_w_o", a_w_o),
               ("kv_w_down", kv_w_down_p), ("kv_w_up", kv_w_up[None]), ("b_w_dq", b_w_dq),
               ("b_w_uq", b_w_uq_p), ("b_w_o", b_w_o)]
    names = [nm for nm, _ in sharded]
    shard_of = dict(sharded)
    W = {}

    gather_groups = [
        [("ffn1_w_in", 0)],
        [("ffn1_w_out", 0)],
        [("a_w_qkv", 0), ("a_w_o", 0)],
        [("ffn2_w_in", 0), ("ffn2_w_out", 0), ("kv_w_down", 0), ("kv_w_up", 0)],
        [("ffn1_w_in", 1), ("ffn1_w_out", 1), ("b_w_dq", 0), ("b_w_uq", 0), ("b_w_o", 0), ("ffn2_w_in", 1),
         ("ffn2_w_out", 1)]]

    own = {}

    def cast(g, after=None):
        keys = gather_groups[g]
        own.update(zip(keys, cast_group(f"cast_group_{g}", [shard_of[nm] for nm, _ in keys], [l for _, l in keys],
                                        idx, after=after)))

    def gather_start(g, after):
        keys = gather_groups[g]
        ss, rs, bufs, _, token = ici_start(f"gather_start_{g}", [halves(own[k]) for k in keys], [], after, True)
        return (g, ss, rs, bufs), token

    def gather_finish(state, after):
        g, ss, rs, bufs = state
        bufs, _ = ici_wait(f"gather_wait_{g}", ss, rs, bufs, [], after, True)
        full = gather_pair_pass(f"gather_pair_{g}", bufs)
        for k, w in zip(gather_groups[g], full):
            W[k] = whole(w)
        return full[0]

    def tied(a, token):
        return a + token[0, 0]

    def col(nm, l=0):
        return W[(nm, l)]

    def row(nm, l=0):
        w = W[(nm, l)]
        return w.reshape(N_CHIPS * w.shape[1], w.shape[2])

    bias = rel_bias_tile("rel_bias_tile", a_rel_bias[0])

    def ffn_fwd(tag, h, g, w_in, w_out):
        xn = rms_fwd(f"{tag}_norm", h, g)
        u, act = ffn_in_act(f"{tag}_in", xn, w_in)
        return mm_roww(f"{tag}_out", act, w_out, F32, res=h, alpha=0.5), (xn, u, act)

    h0 = x.reshape(T, D)
    for g in range(3):
        cast(g)
    st0, tok0 = gather_start(0, h0)
    st1, tok1 = gather_start(1, tok0)
    st2, tok2 = gather_start(2, tok1)
    for g in range(3, len(gather_groups)):
        cast(g, tok2)
    xn0 = rms_fwd("l0f1_norm", h0, tied(ffn1_norm[0], tok2))
    gather_finish(st0, xn0)
    u0, act0 = ffn_in_act("l0f1_in", xn0, col("ffn1_w_in", 0))
    gather_finish(st1, u0)
    h1 = mm_roww("l0f1_out", act0, row("ffn1_w_out", 0), F32, res=h0, alpha=0.5)
    sv_f1a = (xn0, u0, act0)
    done2 = gather_finish(st2, h1)
    st3, tok3 = gather_start(3, done2)
    st4, tok4 = gather_start(4, tok3)
    hn_a = rms_fwd("l0mix_norm", h1, tied(mix_norm[0], tok4))
    qkv = mm_colw("l0_qkv", hn_a, col("a_w_qkv"), BF16).reshape(B, S, 3 * D)
    o_a = attn_a_fwd("l0_attn", qkv, bias).reshape(T, D)
    h2 = mm_roww("l0_attn_out", o_a, row("a_w_o"), F32, res=h1)
    gather_finish(st3, h2)
    h3, sv_f2a = ffn_fwd("l0f2", h2, ffn2_norm[0], col("ffn2_w_in", 0), row("ffn2_w_out", 0))

    hkv = rms_fwd("kv_norm", h3, kv_norm)
    ckr = mm_roww("kv_down", hkv, row("kv_w_down"), F32)
    ckv, kr = kvprep_fwd("kv_prep", ckr, kv_latent_norm, tabs, B, S)
    kvb = mm_colw("kv_up", ckv, col("kv_w_up"), BF16).reshape(B, S, HB * 256)
    gather_finish(st4, kvb)

    h4, sv_f1b = ffn_fwd("l1f1", h3, ffn1_norm[1], col("ffn1_w_in", 1), row("ffn1_w_out", 1))
    hn_b = rms_fwd("l1mix_norm", h4, mix_norm[1])
    cqp = mm_roww("l1_dq", hn_b, row("b_w_dq"), F32)
    cq = rms_fwd("l1_q_norm", cqp, b_q_norm[0])
    qpre = mm_colw("l1_uq", cq, col("b_w_uq"), F32)
    qf = qprep("l1_q_rope", qpre, tabs, B, S, bwd=False).reshape(B, S, HB * 256)
    o_b, lse = mla_fwd("l1_attn", qf, kvb, kr)
    h5 = mm_roww("l1_attn_out", o_b.reshape(T, HB * LANE), row("b_w_o"), F32, res=h4)
    h6, sv_f2b = ffn_fwd("l1f2", h5, ffn2_norm[1], col("ffn2_w_in", 1), row("ffn2_w_out", 1))

    dh, g_final, loss_part = loss_head("loss_head", h6, final_norm, loss_target.reshape(T, D))

    gw = {}
    gbufs = {nm: lax.empty(halves(w).shape, F32) for nm, w in sharded}

    def reduce_start(r, keys, after):
        dws = [halves(gw[k]) for k in keys]
        landed = pair_exchange(f"grad_pair_exchange_{r}", dws)
        parts = half_sum_group(f"half_sum_{r}", dws, landed, idx)
        lands = [lax.empty((3, *p.shape[1:]), p.dtype) for p in parts]
        ss, rs, parts, lands, token = ici_start(f"reduce_start_{r}", parts, lands, after, False)
        return (r, keys, ss, rs, parts, lands), token

    def reduce_finish(state, after):
        r, keys, ss, rs, parts, lands = state
        parts, lands = ici_wait(f"reduce_wait_{r}", ss, rs, parts, lands, after, False)
        done = chip_sum_group(f"chip_sum_{r}", parts, lands, [gbufs[nm] for nm, _ in keys], [l for _, l in keys], idx)
        gbufs.update(zip([nm for nm, _ in keys], done))
        return done[0]

    def ffn_bwd(tag, dh, h_in, g, w_in, w_out, saved, key_in, key_out, after=None, then=None):
        xn, u, act = saved
        du = ffn_dact(f"{tag}_dact", dh, w_out, u, after=after)
        dwo = mm_droww(f"{tag}_dwout", act, dh, alpha=0.5)
        gw[key_out] = dwo.reshape(N_CHIPS, dwo.shape[0] // N_CHIPS, dwo.shape[1])
        gw[key_in] = mm_dcolw(f"{tag}_dwin", xn, du, pair_layout=True)
        token = then(du) if then is not None else None
        return dx_norm_bwd(f"{tag}_dxn", du, w_in, h_in, g, dres=dh, pair_layout=True, after=token)

    def chip_major(dw):
        return dw.reshape(N_CHIPS, dw.shape[0] // N_CHIPS, dw.shape[1])

    dh, g_f2b = ffn_bwd("l1f2b", dh, h5, ffn2_norm[1], col("ffn2_w_in", 1), row("ffn2_w_out", 1), sv_f2b,
                        ("ffn2_w_in", 1), ("ffn2_w_out", 1))
    red0, rtok0 = reduce_start(0, [("ffn2_w_in", 1), ("ffn2_w_out", 1)], dh)
    do_b = mm_roww_t("l1_attn_do", dh, row("b_w_o"), BF16, after=rtok0).reshape(B, S, HB * LANE)
    gw[("b_w_o", 0)] = chip_major(mm_droww("l1_attn_dwo", o_b.reshape(T, HB * LANE), dh))
    dqf, dkv, dkr = mla_bwd("l1_attn_bwd", qf, kvb, kr, do_b, o_b, lse)
    dqpre = qprep("l1_q_rope_bwd", dqf.reshape(T, HB * 256), tabs, B, S, bwd=True)
    gw[("b_w_uq", 0)] = mm_dcolw("l1_dwuq", cq, dqpre)
    dcqp, g_qn = dx_norm_bwd("l1_dcq", dqpre, col("b_w_uq"), cqp, b_q_norm[0])
    gw[("b_w_dq", 0)] = chip_major(mm_droww("l1_dwdq", hn_b, dcqp))
    dhn = mm_roww_t("l1_dhn", dcqp, row("b_w_dq"), F32)
    dh, g_mixb = rms_bwd("l1_dmix", h4, mix_norm[1], dhn, dres=dh)
    dh, g_f1b = ffn_bwd("l1f1b", dh, h3, ffn1_norm[1], col("ffn1_w_in", 1), row("ffn1_w_out", 1), sv_f1b,
                        ("ffn1_w_in", 1), ("ffn1_w_out", 1))
    fin0 = reduce_finish(red0, dh)
    red1, rtok1 = reduce_start(1, [("b_w_o", 0), ("b_w_uq", 0), ("b_w_dq", 0), ("ffn1_w_in", 1), ("ffn1_w_out", 1)], fin0)
    dkv2 = dkv.reshape(T, HB * 256)
    gw[("kv_w_up", 0)] = mm_dcolw("kv_dwup", ckv, dkv2, after=rtok1)
    dckv = mm_colw_t("kv_dckv", dkv2, col("kv_w_up"), F32, after=rtok1)
    dckr, g_lat = kvprep_bwd("kv_prep_bwd", ckr, kv_latent_norm, dckv, dkr, tabs, B, S)
    gw[("kv_w_down", 0)] = chip_major(mm_droww("kv_dwdown", hkv, dckr))
    dhkv = mm_roww_t("kv_dhkv", dckr, row("kv_w_down"), F32)
    dh, g_kvn = rms_bwd("kv_dnorm", h3, kv_norm, dhkv, dres=dh)
    dh, g_f2a = ffn_bwd("l0f2b", dh, h2, ffn2_norm[0], col("ffn2_w_in", 0), row("ffn2_w_out", 0), sv_f2a,
                        ("ffn2_w_in", 0), ("ffn2_w_out", 0))
    do_a = mm_roww_t("l0_attn_do", dh, row("a_w_o"), BF16).reshape(B, S, D)
    gw[("a_w_o", 0)] = chip_major(mm_droww("l0_attn_dwo", o_a, dh))
    dqkv, dbias = attn_a_bwd("l0_attn_bwd", qkv, do_a, bias)
    dqkv = dqkv.reshape(T, 3 * D)
    gw[("a_w_qkv", 0)] = mm_dcolw("l0_dwqkv", hn_a, dqkv)
    dh, g_mixa = dx_norm_bwd("l0_dhn", dqkv, col("a_w_qkv"), h1, mix_norm[0], dres=dh)
    fin1 = reduce_finish(red1, dh)
    red2, rtok2 = reduce_start(2, [("kv_w_up", 0), ("kv_w_down", 0), ("ffn2_w_in", 0), ("ffn2_w_out", 0),
                                   ("a_w_o", 0), ("a_w_qkv", 0)], fin1)
    last = {}

    def last_group(du):
        fin2 = reduce_finish(red2, gw[("ffn1_w_in", 0)])
        last["red"], token = reduce_start(3, [("ffn1_w_in", 0), ("ffn1_w_out", 0)], fin2)
        return token

    dh, g_f1a = ffn_bwd("l0f1b", dh, h0, ffn1_norm[0], col("ffn1_w_in", 0), row("ffn1_w_out", 0), sv_f1a,
                        ("ffn1_w_in", 0), ("ffn1_w_out", 0), after=rtok2, then=last_group)
    grad_x = dh.reshape(B, S, D)
    g_rel = rel_bias_grad("rel_bias_grad", dbias)[:, :2 * MAX_REL + 1][None]
    reduce_finish(last["red"], dh)

    full = [whole(g) for g in pair_assemble([gbufs[nm] for nm in names])]
    G = {nm: g for (nm, _), g in zip(sharded, full)}
    G["kv_w_down"] = G["kv_w_down"][0, :, :KVL + ROPE]
    G["kv_w_up"] = G["kv_w_up"][0]
    G["b_w_uq"] = G["b_w_uq"].reshape(1, QL, hpc, 256)[..., :NOPE + ROPE].reshape(b_w_uq.shape)

    small = [("ffn1_norm", jnp.stack([g_f1a, g_f1b])), ("mix_norm", jnp.stack([g_mixa, g_mixb])),
             ("ffn2_norm", jnp.stack([g_f2a, g_f2b])), ("a_rel_bias", g_rel), ("kv_norm", g_kvn),
             ("kv_latent_norm", g_lat), ("b_q_norm", g_qn[None]), ("final_norm", g_final)]
    red = all_reduce_small(_pack([loss_part] + [g for _, g in small]))
    unpacked = _unpack(red, [loss_part] + [g for _, g in small])
    loss = unpacked[0][0, 0]
    for (nm, _), g in zip(small, unpacked[1:]):
        G[nm] = g

    given = dict(ffn1_norm=(ffn1_norm, m_ffn1_norm, v_ffn1_norm), ffn1_w_in=(ffn1_w_in, m_ffn1_w_in, v_ffn1_w_in),
                 ffn1_w_out=(ffn1_w_out, m_ffn1_w_out, v_ffn1_w_out), mix_norm=(mix_norm, m_mix_norm, v_mix_norm),
                 ffn2_norm=(ffn2_norm, m_ffn2_norm, v_ffn2_norm), ffn2_w_in=(ffn2_w_in, m_ffn2_w_in, v_ffn2_w_in),
                 ffn2_w_out=(ffn2_w_out, m_ffn2_w_out, v_ffn2_w_out), a_w_qkv=(a_w_qkv, m_a_w_qkv, v_a_w_qkv),
                 a_rel_bias=(a_rel_bias, m_a_rel_bias, v_a_rel_bias), a_w_o=(a_w_o, m_a_w_o, v_a_w_o),
                 kv_norm=(kv_norm, m_kv_norm, v_kv_norm), kv_w_down=(kv_w_down, m_kv_w_down, v_kv_w_down),
                 kv_latent_norm=(kv_latent_norm, m_kv_latent_norm, v_kv_latent_norm),
                 kv_w_up=(kv_w_up, m_kv_w_up, v_kv_w_up), b_w_dq=(b_w_dq, m_b_w_dq, v_b_w_dq),
                 b_q_norm=(b_q_norm, m_b_q_norm, v_b_q_norm), b_w_uq=(b_w_uq, m_b_w_uq, v_b_w_uq),
                 b_w_o=(b_w_o, m_b_w_o, v_b_w_o), final_norm=(final_norm, m_final_norm, v_final_norm))
    order = list(given)
    delta, new_m, new_v = {}, {}, {}
    small_names = [nm for nm, _ in small]
    packed = [_pack([given[nm][k] for nm in small_names]) for k in range(3)]
    outs = adamw("adamw_small", packed[0], _pack([G[nm] for nm in small_names]), packed[1], packed[2])
    for dst, buf in zip((delta, new_m, new_v), outs):
        for nm, a in zip(small_names, _unpack(buf, [given[nm][0] for nm in small_names])):
            dst[nm] = a
    for nm, _ in sharded:
        w, m, v = given[nm]
        g = G[nm].reshape(w.shape)
        G[nm] = g
        two = lambda a: a.reshape(-1, a.shape[-1])
        d_, m_, v_ = adamw(f"adamw_{nm}", two(w), two(g), two(m), two(v))
        delta[nm], new_m[nm], new_v[nm] = d_.reshape(w.shape), m_.reshape(w.shape), v_.reshape(w.shape)

    return (loss, grad_x, *[G[n] for n in order], *[delta[n] for n in order],
            *[new_m[n] for n in order], *[new_v[n] for n in order])
```

```python
import math

import jax
import jax.numpy as jnp
from jax import lax
from jax.experimental import pallas as pl
from jax.experimental.pallas import tpu as pltpu

F32 = jnp.float32
BF16 = jnp.bfloat16
I32 = jnp.int32

CHUNK = 64
CHUNK_SHIFT = 6
HEAD_DIM_A = 64
LEFT_CHUNKS = 8
MAX_REL = 128
REL_PAD = 384
QROWS = 2 * CHUNK
WIN = (LEFT_CHUNKS + 2) * CHUNK
PADR = LEFT_CHUNKS * CHUNK
NOPE = 128
ROPE = 64
EPS = 1e-6
NEG_INF = -1e30
ROPE_THETA = 10000.0
ADAM_LR, ADAM_B1, ADAM_B2, ADAM_EPS, ADAM_WD, ADAM_STEP = 0.001, 0.9, 0.999, 1e-08, 0.01, 10
N_CHIPS = 4
LANE = 128
MESH = pl.DeviceIdType.MESH
VMEM_CAP_MB = 60

NN = (((1,), (0,)), ((), ()))
NT = (((1,), (1,)), ((), ()))
TN = (((0,), (0,)), ((), ()))


def _tile(n, pref, mult):
    t = (min(pref, n) // mult) * mult
    while t >= mult:
        if n % t == 0:
            return t
        t -= mult
    return n


def _nbytes(shape, dtype):
    return math.prod(shape) * jnp.dtype(dtype).itemsize


def _params(block_bytes, extra_bytes=0):
    need = 2 * block_bytes + extra_bytes
    mb = min(VMEM_CAP_MB, max(32, int(need * 1.25 / 2**20) + 8))
    return pltpu.CompilerParams(vmem_limit_bytes=mb * 2**20)


def _mm(name, kind, a, b, grid, a_spec, b_spec, o_spec, out_shape, out_dtype, blocks,
        red_axis=None, nred=1, alpha=1.0, res=None, res_spec=None, after=None):
    dims = {"nn": NN, "nt": NT, "tn": TN}[kind]
    has_res = res is not None
    acc_in_out = nred > 1 and out_dtype == F32 and not has_res and alpha == 1.0
    n_in = 2 + has_res + (after is not None)

    def body(*refs):
        a_ref, b_ref = refs[0], refs[1]
        r_ref = refs[2] if has_res else None
        o_ref = refs[n_in]
        p = lax.dot_general(a_ref[...].astype(BF16), b_ref[...].astype(BF16), dims,
                            preferred_element_type=F32)

        def finish(acc):
            y = acc if alpha == 1.0 else acc * alpha
            if has_res:
                y = r_ref[...] + y
            o_ref[...] = y.astype(o_ref.dtype)

        if nred == 1:
            finish(p)
            return
        k = pl.program_id(red_axis)
        tgt = o_ref if acc_in_out else refs[-1]

        @pl.when(k == 0)
        def _():
            tgt[...] = p

        @pl.when(k > 0)
        def _():
            tgt[...] += p

        if not acc_in_out:
            @pl.when(k == nred - 1)
            def _():
                finish(tgt[...])

    a_blk, b_blk, o_blk = blocks
    scratch = []
    extra = 0
    if nred > 1 and not acc_in_out:
        scratch = [pltpu.VMEM(o_blk, F32)]
        extra = _nbytes(o_blk, F32)
    blk = _nbytes(a_blk, a.dtype) + _nbytes(b_blk, b.dtype) + _nbytes(o_blk, out_dtype)
    ins, specs = [a, b], [a_spec, b_spec]
    if has_res:
        ins.append(res)
        specs.append(res_spec)
        blk += _nbytes(o_blk, res.dtype)
    if after is not None:
        ins.append(after)
        specs.append(pl.BlockSpec(memory_space=pl.ANY))
    extra += _nbytes(a_blk, BF16) + _nbytes(b_blk, BF16) + 2 * _nbytes(o_blk, F32)
    return pl.pallas_call(
        body, name=name, grid=grid, in_specs=specs, out_specs=o_spec,
        out_shape=jax.ShapeDtypeStruct(out_shape, out_dtype), scratch_shapes=scratch,
        compiler_params=_params(blk, extra),
    )(*ins)


def mm_colw(name, x, w3, out_dtype):
    T, K = x.shape
    _, _, nl = w3.shape
    tm = _tile(T, 512, 8)
    return _mm(name, "nn", x, w3, (N_CHIPS, T // tm),
               pl.BlockSpec((tm, K), lambda j, i: (i, 0)),
               pl.BlockSpec((None, K, nl), lambda j, i: (j, 0, 0)),
               pl.BlockSpec((tm, nl), lambda j, i: (i, j)),
               (T, N_CHIPS * nl), out_dtype, ((tm, K), (K, nl), (tm, nl)))


def _pair_chip(j):
    return (j % 2) * 2 + j // 2


def mm_colw_t(name, dy, w3, out_dtype, res=None, after=None, pair_layout=False):
    T = dy.shape[0]
    _, K, nl = w3.shape
    tm = _tile(T, 1024, 8)
    chip = _pair_chip if pair_layout else (lambda j: j)
    return _mm(name, "nt", dy, w3, (T // tm, N_CHIPS),
               pl.BlockSpec((tm, nl), lambda i, j: (i, j)),
               pl.BlockSpec((None, K, nl), lambda i, j: (chip(j), 0, 0)),
               pl.BlockSpec((tm, K), lambda i, j: (i, 0)),
               (T, K), out_dtype, ((tm, nl), (K, nl), (tm, K)),
               red_axis=1, nred=N_CHIPS, res=res,
               res_spec=pl.BlockSpec((tm, K), lambda i, j: (i, 0)), after=after)


def mm_dcolw(name, x, dy, after=None, pair_layout=False):
    T, K = x.shape
    nl = dy.shape[1] // N_CHIPS
    tt = _tile(T, 2048, 8)
    chip = _pair_chip if pair_layout else (lambda j: j)
    return _mm(name, "tn", x, dy, (N_CHIPS, T // tt),
               pl.BlockSpec((tt, K), lambda j, t: (t, 0)),
               pl.BlockSpec((tt, nl), lambda j, t: (t, j)),
               pl.BlockSpec((None, K, nl), lambda j, t: (chip(j), 0, 0)),
               (N_CHIPS, K, nl), BF16, ((tt, K), (tt, nl), (K, nl)),
               red_axis=1, nred=T // tt, after=after)


def mm_roww(name, x, w2, out_dtype, res=None, alpha=1.0):
    T, Kt = x.shape
    N = w2.shape[1]
    tm = _tile(T, 512, 8)
    return _mm(name, "nn", x, w2, (T // tm,),
               pl.BlockSpec((tm, Kt), lambda i: (i, 0)),
               pl.BlockSpec((Kt, N), lambda i: (0, 0)),
               pl.BlockSpec((tm, N), lambda i: (i, 0)),
               (T, N), out_dtype, ((tm, Kt), (Kt, N), (tm, N)),
               alpha=alpha, res=res, res_spec=pl.BlockSpec((tm, N), lambda i: (i, 0)))


def mm_roww_t(name, dy, w2, out_dtype, alpha=1.0, after=None):
    T, N = dy.shape
    Kt = w2.shape[0]
    tm = _tile(T, 512, 8)
    tk = _tile(Kt, 1408, LANE)
    return _mm(name, "nt", dy, w2, (Kt // tk, T // tm),
               pl.BlockSpec((tm, N), lambda j, i: (i, 0)),
               pl.BlockSpec((tk, N), lambda j, i: (j, 0)),
               pl.BlockSpec((tm, tk), lambda j, i: (i, j)),
               (T, Kt), out_dtype, ((tm, N), (tk, N), (tm, tk)), alpha=alpha, after=after)


def mm_droww(name, x, dy, alpha=1.0):
    T, Kt = x.shape
    N = dy.shape[1]
    tt = _tile(T, 2048, 8)
    tk = _tile(Kt, 1408, LANE)
    return _mm(name, "tn", x, dy, (Kt // tk, T // tt),
               pl.BlockSpec((tt, tk), lambda j, t: (t, j)),
               pl.BlockSpec((tt, N), lambda j, t: (t, 0)),
               pl.BlockSpec((tk, N), lambda j, t: (j, 0)),
               (Kt, N), BF16, ((tt, tk), (tt, N), (tk, N)),
               red_axis=1, nred=T // tt, alpha=alpha)


def rms_fwd(name, x, g):
    T, D = x.shape
    tm = _tile(T, 512, 8)

    def body(x_ref, g_ref, o_ref):
        xv = x_ref[...]
        r = lax.rsqrt(jnp.mean(xv * xv, axis=-1, keepdims=True) + EPS)
        o_ref[...] = (xv * r * g_ref[...]).astype(o_ref.dtype)

    return pl.pallas_call(
        body, name=name, grid=(T // tm,),
        in_specs=[pl.BlockSpec((tm, D), lambda i: (i, 0)), pl.BlockSpec((1, D), lambda i: (0, 0))],
        out_specs=pl.BlockSpec((tm, D), lambda i: (i, 0)),
        out_shape=jax.ShapeDtypeStruct((T, D), BF16),
        compiler_params=_params(_nbytes((tm, D), F32) * 2, 4 * _nbytes((tm, D), F32)),
    )(x, g.reshape(1, D))


def _rms_bwd_math(xv, gv, dy):
    r = lax.rsqrt(jnp.mean(xv * xv, axis=-1, keepdims=True) + EPS)
    xh = xv * r
    dyg = dy * gv
    dx = r * (dyg - xh * jnp.mean(dyg * xh, axis=-1, keepdims=True))
    dg = jnp.sum(dy * xh, axis=0, keepdims=True)
    return dx, dg


def rms_bwd(name, x, g, dy, dres=None):
    T, D = x.shape
    tm = _tile(T, 256, 8)
    has_res = dres is not None

    def body(*refs):
        x_ref, g_ref, dy_ref = refs[:3]
        r_ref = refs[3] if has_res else None
        dx_ref, dg_ref = refs[-2:]
        dx, dg = _rms_bwd_math(x_ref[...], g_ref[...], dy_ref[...].astype(F32))
        if has_res:
            dx = r_ref[...] + dx
        dx_ref[...] = dx

        @pl.when(pl.program_id(0) == 0)
        def _():
            dg_ref[...] = dg

        @pl.when(pl.program_id(0) > 0)
        def _():
            dg_ref[...] += dg

    row = pl.BlockSpec((tm, D), lambda i: (i, 0))
    vec = pl.BlockSpec((1, D), lambda i: (0, 0))
    ins, specs = [x, g.reshape(1, D), dy], [row, vec, row]
    if has_res:
        ins.append(dres)
        specs.append(row)
    dx, dg = pl.pallas_call(
        body, name=name, grid=(T // tm,), in_specs=specs, out_specs=[row, vec],
        out_shape=[jax.ShapeDtypeStruct((T, D), F32), jax.ShapeDtypeStruct((1, D), F32)],
        compiler_params=_params(_nbytes((tm, D), F32) * 4, 6 * _nbytes((tm, D), F32)),
    )(*ins)
    return dx, dg.reshape(D)


def dx_norm_bwd(name, dy, w3, x, g, dres=None, pair_layout=False, after=None):
    T = dy.shape[0]
    _, K, nl = w3.shape
    tm = _tile(T, 512, 8)
    chip = _pair_chip if pair_layout else (lambda j: j)
    has_res = dres is not None

    def body(*refs):
        dy_ref, w_ref, x_ref, g_ref = refs[:4]
        r_ref = refs[4] if has_res else None
        dx_ref, dg_ref, acc = refs[-3:]
        i, k = pl.program_id(0), pl.program_id(1)
        p = lax.dot_general(dy_ref[...].astype(BF16), w_ref[...], NT, preferred_element_type=F32)

        @pl.when(k == 0)
        def _():
            acc[...] = p

        @pl.when(k > 0)
        def _():
            acc[...] += p

        @pl.when(k == N_CHIPS - 1)
        def _():
            dx, dg = _rms_bwd_math(x_ref[...], g_ref[...], acc[...])
            dx_ref[...] = r_ref[...] + dx if has_res else dx

            @pl.when(i == 0)
            def _():
                dg_ref[...] = dg

            @pl.when(i > 0)
            def _():
                dg_ref[...] += dg

    row = pl.BlockSpec((tm, K), lambda i, j: (i, 0))
    vec = pl.BlockSpec((1, K), lambda i, j: (0, 0))
    ins = [dy, w3, x, g.reshape(1, K)]
    specs = [pl.BlockSpec((tm, nl), lambda i, j: (i, j)),
             pl.BlockSpec((None, K, nl), lambda i, j: (chip(j), 0, 0)), row, vec]
    if has_res:
        ins.append(dres)
        specs.append(row)
    if after is not None:
        ins.append(after)
        specs.append(pl.BlockSpec(memory_space=pl.ANY))
    blk = _nbytes((tm, nl), dy.dtype) + _nbytes((K, nl), BF16) + (2 + has_res) * _nbytes((tm, K), F32)
    dx, dg = pl.pallas_call(
        body, name=name, grid=(T // tm, N_CHIPS), in_specs=specs, out_specs=[row, vec],
        out_shape=[jax.ShapeDtypeStruct((T, K), F32), jax.ShapeDtypeStruct((1, K), F32)],
        scratch_shapes=[pltpu.VMEM((tm, K), F32)],
        compiler_params=_params(blk, 8 * _nbytes((tm, K), F32)),
    )(*ins)
    return dx, dg.reshape(K)


def ffn_in_act(name, x, w3):
    T, K = x.shape
    _, _, nl = w3.shape
    tm = _tile(T, 512, 8)

    def body(*refs):
        x_ref, wg_ref, wu_ref = refs[:3]
        u_ref, a_ref = refs[-2:]
        xv = x_ref[...]
        g = jnp.dot(xv, wg_ref[...], preferred_element_type=F32)
        up = jnp.dot(xv, wu_ref[...], preferred_element_type=F32)
        u_ref[:, :nl] = g.astype(u_ref.dtype)
        u_ref[:, nl:] = up.astype(u_ref.dtype)
        a_ref[...] = (g * jax.nn.sigmoid(g) * up).astype(a_ref.dtype)

    blk = _nbytes((tm, K), BF16) + 2 * _nbytes((K, nl), BF16) + _nbytes((tm, 3 * nl), BF16)
    return pl.pallas_call(
        body, name=name, grid=(2, T // tm),
        in_specs=[pl.BlockSpec((tm, K), lambda p, i: (i, 0)),
                  pl.BlockSpec((None, K, nl), lambda p, i: (p, 0, 0)),
                  pl.BlockSpec((None, K, nl), lambda p, i: (p + 2, 0, 0))],
        out_specs=[pl.BlockSpec((tm, 2 * nl), lambda p, i: (i, p)), pl.BlockSpec((tm, nl), lambda p, i: (i, p))],
        out_shape=[jax.ShapeDtypeStruct((T, 4 * nl), BF16), jax.ShapeDtypeStruct((T, 2 * nl), BF16)],
        compiler_params=_params(blk, 4 * _nbytes((tm, nl), F32)),
    )(x, w3, w3)


def ffn_dact(name, dh, w_out, u, after=None):
    T, N = dh.shape
    F = w_out.shape[0]
    nl = F // 2
    tm = _tile(T, 512, 8)

    def body(*refs):
        d_ref, w_ref, u_ref = refs[:3]
        o_ref = refs[-1]
        dact = 0.5 * lax.dot_general(d_ref[...].astype(BF16), w_ref[...], NT, preferred_element_type=F32)
        g = u_ref[:, :nl].astype(F32)
        up = u_ref[:, nl:].astype(F32)
        sig = jax.nn.sigmoid(g)
        o_ref[:, :nl] = (dact * up * (sig * (1.0 + g * (1.0 - sig)))).astype(o_ref.dtype)
        o_ref[:, nl:] = (dact * (g * sig)).astype(o_ref.dtype)

    ins = [dh, w_out, u]
    specs = [pl.BlockSpec((tm, N), lambda p, i: (i, 0)), pl.BlockSpec((nl, N), lambda p, i: (p, 0)),
             pl.BlockSpec((tm, 2 * nl), lambda p, i: (i, p))]
    if after is not None:
        ins.append(after)
        specs.append(pl.BlockSpec(memory_space=pl.ANY))
    blk = _nbytes((tm, N), F32) + _nbytes((nl, N), BF16) + 2 * _nbytes((tm, 2 * nl), BF16)
    return pl.pallas_call(
        body, name=name, grid=(2, T // tm), in_specs=specs,
        out_specs=pl.BlockSpec((tm, 2 * nl), lambda p, i: (i, p)),
        out_shape=jax.ShapeDtypeStruct((T, 2 * F), BF16),
        compiler_params=_params(blk, 6 * _nbytes((tm, nl), F32)),
    )(*ins)


def loss_head(name, h, g, target):
    T, D = h.shape
    tm = _tile(T, 256, 8)

    def body(h_ref, g_ref, t_ref, dh_ref, dg_ref, loss_ref):
        xv = h_ref[...]
        gv = g_ref[...]
        r = lax.rsqrt(jnp.mean(xv * xv, axis=-1, keepdims=True) + EPS)
        err = xv * r * gv - t_ref[...]
        part = 0.5 * jnp.sum(jnp.mean(err * err, axis=-1, keepdims=True), axis=0, keepdims=True)
        dx, dg = _rms_bwd_math(xv, gv, err * (1.0 / D))
        dh_ref[...] = dx
        part = jnp.broadcast_to(part, (1, LANE))

        @pl.when(pl.program_id(0) == 0)
        def _():
            dg_ref[...] = dg
            loss_ref[...] = part

        @pl.when(pl.program_id(0) > 0)
        def _():
            dg_ref[...] += dg
            loss_ref[...] += part

    row = pl.BlockSpec((tm, D), lambda i: (i, 0))
    vec = pl.BlockSpec((1, D), lambda i: (0, 0))
    dh, dg, loss = pl.pallas_call(
        body, name=name, grid=(T // tm,), in_specs=[row, vec, row],
        out_specs=[row, vec, pl.BlockSpec((1, LANE), lambda i: (0, 0))],
        out_shape=[jax.ShapeDtypeStruct((T, D), F32), jax.ShapeDtypeStruct((1, D), F32),
                   jax.ShapeDtypeStruct((1, LANE), F32)],
        compiler_params=_params(_nbytes((tm, D), F32) * 3, 6 * _nbytes((tm, D), F32)),
    )(h, g.reshape(1, D), target)
    return dh, dg.reshape(D), loss


def rope_tables(S):
    half = ROPE // 2
    freqs = ROPE_THETA ** (-jnp.arange(half, dtype=F32) / half)
    ang = jnp.arange(S, dtype=F32)[:, None] * freqs[None, :]
    cos, sin = jnp.cos(ang), jnp.sin(ang)
    z = jnp.zeros_like(cos)
    ct = jnp.concatenate([cos, cos, z, z], axis=1)
    s1 = jnp.concatenate([-sin, z, z, z], axis=1)
    s2 = jnp.concatenate([z, sin, z, z], axis=1)
    return ct, s1, s2


def _rope_tile(t, ct, s1, s2):
    return t * ct + pltpu.roll(t, 96, 1) * s1 + pltpu.roll(t, 32, 1) * s2


def _rope_tile_bwd(d, ct, s1, s2):
    return d * ct + pltpu.roll(d * s1, 32, 1) + pltpu.roll(d * s2, 96, 1)


def uq_rope(name, x, w3, tabs, S):
    T, K = x.shape
    _, _, nl = w3.shape
    tm = _tile(S, 512, 8)
    nt = S // tm

    def body(x_ref, w_ref, ct_ref, s1_ref, s2_ref, o_ref):
        q = jnp.dot(x_ref[...], w_ref[...], preferred_element_type=F32)
        ct, s1, s2 = ct_ref[...], s1_ref[...], s2_ref[...]
        for h in range(nl // 256):
            o_ref[:, 256 * h:256 * h + 128] = q[:, 256 * h:256 * h + 128].astype(o_ref.dtype)
            o_ref[:, 256 * h + 128:256 * h + 256] = _rope_tile(q[:, 256 * h + 128:256 * h + 256],
                                                               ct, s1, s2).astype(o_ref.dtype)

    tab = pl.BlockSpec((tm, LANE), lambda j, i: (i % nt, 0))
    blk = _nbytes((tm, K), BF16) + _nbytes((K, nl), BF16) + _nbytes((tm, nl), BF16) + 3 * _nbytes((tm, LANE), F32)
    return pl.pallas_call(
        body, name=name, grid=(N_CHIPS, T // tm),
        in_specs=[pl.BlockSpec((tm, K), lambda j, i: (i, 0)), pl.BlockSpec((None, K, nl), lambda j, i: (j, 0, 0)),
                  tab, tab, tab],
        out_specs=pl.BlockSpec((tm, nl), lambda j, i: (i, j)),
        out_shape=jax.ShapeDtypeStruct((T, N_CHIPS * nl), BF16),
        compiler_params=_params(blk, 4 * _nbytes((tm, nl), F32)),
    )(x, w3, *tabs)


def kvprep_fwd(name, ckr, g, tabs, B, S):
    T, W = ckr.shape
    KVL = W - LANE
    ts = _tile(S, 256, 8)

    def body(x_ref, g_ref, ct_ref, s1_ref, s2_ref, c_ref, k_ref):
        xv = x_ref[0, :, :KVL]
        r = lax.rsqrt(jnp.mean(xv * xv, axis=-1, keepdims=True) + EPS)
        c_ref[0] = (xv * r * g_ref[...]).astype(c_ref.dtype)
        k_ref[0] = _rope_tile(x_ref[0, :, KVL:], ct_ref[...], s1_ref[...], s2_ref[...]).astype(k_ref.dtype)

    tab = pl.BlockSpec((ts, LANE), lambda b, s: (s, 0))
    c, k = pl.pallas_call(
        body, name=name, grid=(B, S // ts),
        in_specs=[pl.BlockSpec((1, ts, W), lambda b, s: (b, s, 0)), pl.BlockSpec((1, KVL), lambda b, s: (0, 0)),
                  tab, tab, tab],
        out_specs=[pl.BlockSpec((1, ts, KVL), lambda b, s: (b, s, 0)),
                   pl.BlockSpec((1, ts, LANE), lambda b, s: (b, s, 0))],
        out_shape=[jax.ShapeDtypeStruct((B, S, KVL), BF16), jax.ShapeDtypeStruct((B, S, LANE), BF16)],
        compiler_params=_params(_nbytes((ts, W), F32) * 2, _nbytes((ts, W), F32) * 2),
    )(ckr.reshape(B, S, W), g.reshape(1, KVL), *tabs)
    return c.reshape(T, KVL), k


def kvprep_bwd(name, ckr, g, dc, dkr, tabs, B, S):
    T, W = ckr.shape
    KVL = W - LANE
    ts = _tile(S, 256, 8)

    def body(x_ref, g_ref, dc_ref, dk_ref, ct_ref, s1_ref, s2_ref, o_ref, dg_ref):
        dx, dg = _rms_bwd_math(x_ref[0, :, :KVL], g_ref[...], dc_ref[0])
        o_ref[0, :, :KVL] = dx
        o_ref[0, :, KVL:] = _rope_tile_bwd(dk_ref[0], ct_ref[...], s1_ref[...], s2_ref[...])
        first = (pl.program_id(0) == 0) & (pl.program_id(1) == 0)

        @pl.when(first)
        def _():
            dg_ref[...] = dg

        @pl.when(jnp.logical_not(first))
        def _():
            dg_ref[...] += dg

    tab = pl.BlockSpec((ts, LANE), lambda b, s: (s, 0))
    vec = pl.BlockSpec((1, KVL), lambda b, s: (0, 0))
    o, dg = pl.pallas_call(
        body, name=name, grid=(B, S // ts),
        in_specs=[pl.BlockSpec((1, ts, W), lambda b, s: (b, s, 0)), vec,
                  pl.BlockSpec((1, ts, KVL), lambda b, s: (b, s, 0)),
                  pl.BlockSpec((1, ts, LANE), lambda b, s: (b, s, 0)), tab, tab, tab],
        out_specs=[pl.BlockSpec((1, ts, W), lambda b, s: (b, s, 0)), vec],
        out_shape=[jax.ShapeDtypeStruct((B, S, W), F32), jax.ShapeDtypeStruct((1, KVL), F32)],
        compiler_params=_params(_nbytes((ts, W), F32) * 4, _nbytes((ts, W), F32) * 4),
    )(ckr.reshape(B, S, W), g.reshape(1, KVL), dc.reshape(B, S, KVL), dkr, *tabs)
    return o.reshape(T, W), dg.reshape(KVL)


DIAGS = 768


def _diag_onehot():
    col = lax.broadcasted_iota(I32, (REL_PAD, DIAGS), 1)
    row = lax.broadcasted_iota(I32, (REL_PAD, DIAGS), 0)
    idx = jnp.clip(PADR + QROWS - 1 - col, -MAX_REL, MAX_REL) + MAX_REL
    return (row == idx).astype(F32)


def rel_bias_tile(name, table):
    H = table.shape[0]
    tpad = jnp.pad(table, ((0, 0), (0, REL_PAD - table.shape[1])))

    def body(t_ref, o_ref):
        g = lax.dot_general(t_ref[...], _diag_onehot(), NN, precision=lax.Precision.HIGHEST,
                            preferred_element_type=F32)
        qc = jnp.right_shift(lax.broadcasted_iota(I32, (QROWS, WIN), 0), CHUNK_SHIFT)
        kc = jnp.right_shift(lax.broadcasted_iota(I32, (QROWS, WIN), 1), CHUNK_SHIFT)
        band = (kc >= qc) & (kc <= qc + LEFT_CHUNKS)
        for h in range(H):
            gb = jnp.broadcast_to(g[h:h + 1, :], (QROWS, DIAGS))
            tile = pltpu.roll(gb, DIAGS - (QROWS - 1), 1, stride=1, stride_axis=0)
            o_ref[h // 2, (h % 2) * QROWS:(h % 2 + 1) * QROWS, :] = jnp.where(band, tile[:, :WIN], NEG_INF)

    return pl.pallas_call(
        body, name=name, out_shape=jax.ShapeDtypeStruct((H // 2, 2 * QROWS, WIN), F32),
        compiler_params=_params(0, 2 * _nbytes((H // 2, 2 * QROWS, WIN), F32)),
    )(tpad)


def rel_bias_grad(name, dbias):
    H = 2 * dbias.shape[0]

    def body(d_ref, o_ref):
        flip = (lax.broadcasted_iota(I32, (QROWS, QROWS), 0) + lax.broadcasted_iota(I32, (QROWS, QROWS), 1)
                == QROWS - 1).astype(F32)
        rows = []
        for h in range(H):
            x = d_ref[h // 2, (h % 2) * QROWS:(h % 2 + 1) * QROWS, :]
            xr = lax.dot_general(flip, x, NN, precision=lax.Precision.HIGHEST, preferred_element_type=F32)
            xp = jnp.concatenate([xr, jnp.zeros((QROWS, DIAGS - WIN), F32)], axis=1)
            y = pltpu.roll(xp, 0, 1, stride=1, stride_axis=0)
            rows.append(jnp.sum(y, axis=0, keepdims=True))
        o_ref[...] = lax.dot_general(jnp.concatenate(rows, axis=0), _diag_onehot(), NT,
                                     precision=lax.Precision.HIGHEST, preferred_element_type=F32)

    return pl.pallas_call(
        body, name=name, out_shape=jax.ShapeDtypeStruct((H, REL_PAD), F32),
        compiler_params=_params(0, 2 * _nbytes(dbias.shape, F32)),
    )(dbias)


def _stack_pair(xp):
    lane = lax.broadcasted_iota(I32, xp.shape, 1)
    z = jnp.zeros_like(xp)
    return jnp.concatenate([jnp.where(lane < HEAD_DIM_A, xp, z), jnp.where(lane >= HEAD_DIM_A, xp, z)], axis=0)


def _unstack_pair(y):
    lane = lax.broadcasted_iota(I32, (QROWS, LANE), 1)
    return jnp.where(lane < HEAD_DIM_A, y[:QROWS], y[QROWS:])


def _attn_a_rowpen(j):
    w = lax.broadcasted_iota(I32, (1, WIN), 1)
    return jnp.where(w >= PADR - QROWS * j, 0.0, NEG_INF).astype(F32)


def _attn_a_load_bias(bias_hbm, bias_v, sem):
    cp = pltpu.make_async_copy(bias_hbm, bias_v, sem)
    cp.start()
    cp.wait()


def _attn_a_load_kv(qkv_hbm, b, kpad, vpad, sem, S, D):
    kpad[0:PADR, :] = jnp.zeros((PADR, D), BF16)
    vpad[0:PADR, :] = jnp.zeros((PADR, D), BF16)
    ck = pltpu.make_async_copy(qkv_hbm.at[b, :, pl.ds(D, D)], kpad.at[pl.ds(PADR, S), :], sem.at[0])
    cv = pltpu.make_async_copy(qkv_hbm.at[b, :, pl.ds(2 * D, D)], vpad.at[pl.ds(PADR, S), :], sem.at[1])
    ck.start()
    cv.start()
    ck.wait()
    cv.wait()


def _attn_a_exp(q2s, kp, bias, pen):
    s = lax.dot_general(q2s, kp, NT, preferred_element_type=F32) + bias + pen
    e = jnp.exp(s - jnp.max(s, axis=-1, keepdims=True))
    return e, 1.0 / jnp.sum(e, axis=-1, keepdims=True)


def attn_a_fwd(name, qkv, bias):
    B, S, D3 = qkv.shape
    D = D3 // 3
    H = D // HEAD_DIM_A
    nb = S // QROWS
    scale = HEAD_DIM_A ** -0.5

    def body(q_ref, bias_hbm, qkv_hbm, o_ref, kpad, vpad, bias_v, sem):
        b, j = pl.program_id(0), pl.program_id(1)

        @pl.when((b == 0) & (j == 0))
        def _():
            _attn_a_load_bias(bias_hbm, bias_v, sem.at[2])

        @pl.when(j == 0)
        def _():
            _attn_a_load_kv(qkv_hbm, b, kpad, vpad, sem, S, D)

        pen = _attn_a_rowpen(j)
        w0 = pl.multiple_of(j * QROWS, QROWS)
        for p in range(H // 2):
            ls = slice(p * LANE, (p + 1) * LANE)
            e, rl = _attn_a_exp(_stack_pair(q_ref[0, :, ls] * scale), kpad[pl.ds(w0, WIN), ls], bias_v[p], pen)
            o2 = jnp.dot(e.astype(BF16), vpad[pl.ds(w0, WIN), ls], preferred_element_type=F32) * rl
            o_ref[0, :, ls] = _unstack_pair(o2).astype(o_ref.dtype)

    scr = 2 * _nbytes((PADR + S, D), BF16) + _nbytes(bias.shape, F32) + 8 * _nbytes((2 * QROWS, WIN), F32)
    return pl.pallas_call(
        body, name=name, grid=(B, nb),
        in_specs=[pl.BlockSpec((1, QROWS, D), lambda b, j: (b, j, 0)),
                  pl.BlockSpec(memory_space=pl.ANY), pl.BlockSpec(memory_space=pl.ANY)],
        out_specs=pl.BlockSpec((1, QROWS, D), lambda b, j: (b, j, 0)),
        out_shape=jax.ShapeDtypeStruct((B, S, D), BF16),
        scratch_shapes=[pltpu.VMEM((PADR + S, D), BF16), pltpu.VMEM((PADR + S, D), BF16),
                        pltpu.VMEM(bias.shape, F32), pltpu.SemaphoreType.DMA((3,))],
        compiler_params=_params(2 * _nbytes((QROWS, D), BF16), scr),
    )(qkv, bias, qkv)


def attn_a_bwd(name, qkv, do, bias):
    B, S, D3 = qkv.shape
    D = D3 // 3
    H = D // HEAD_DIM_A
    nb = S // QROWS
    scale = HEAD_DIM_A ** -0.5

    def body(q_ref, do_ref, bias_hbm, qkv_hbm, dqkv_hbm, dbias_hbm, kpad, vpad, dkacc, dvacc, bias_v, dbias_v,
             dq_stage, sem):
        b, j = pl.program_id(0), pl.program_id(1)
        step = b * nb + j
        slot = lax.rem(step, 2)

        def dq_out(s):
            return pltpu.make_async_copy(dq_stage.at[s], dqkv_hbm.at[b, pl.ds(j * QROWS, QROWS), pl.ds(0, D)],
                                         sem.at[3 + s])

        @pl.when(step >= 2)
        def _():
            dq_out(slot).wait()

        @pl.when((b == 0) & (j == 0))
        def _():
            _attn_a_load_bias(bias_hbm, bias_v, sem.at[2])
            dbias_v[...] = jnp.zeros_like(dbias_v)

        @pl.when(j == 0)
        def _():
            _attn_a_load_kv(qkv_hbm, b, kpad, vpad, sem, S, D)
            dkacc[...] = jnp.zeros_like(dkacc)
            dvacc[...] = jnp.zeros_like(dvacc)

        pen = _attn_a_rowpen(j)
        w0 = pl.multiple_of(j * QROWS, QROWS)
        for p in range(H // 2):
            ls = slice(p * LANE, (p + 1) * LANE)
            q2s = _stack_pair(q_ref[0, :, ls] * scale)
            do2 = _stack_pair(do_ref[0, :, ls])
            kp = kpad[pl.ds(w0, WIN), ls]
            vp = vpad[pl.ds(w0, WIN), ls]
            e, rl = _attn_a_exp(q2s, kp, bias_v[p], pen)
            pr = e * rl
            dp = lax.dot_general(do2, vp, NT, preferred_element_type=F32)
            ds = pr * (dp - jnp.sum(pr * dp, axis=-1, keepdims=True))
            dbias_v[p] += ds
            dsb = ds.astype(BF16)
            dq_stage[slot, :, ls] = _unstack_pair(jnp.dot(dsb, kp, preferred_element_type=F32)) * scale
            dkacc[pl.ds(w0, WIN), ls] += lax.dot_general(dsb, q2s, TN, preferred_element_type=F32)
            dvacc[pl.ds(w0, WIN), ls] += lax.dot_general(pr.astype(BF16), do2, TN, preferred_element_type=F32)

        dq_out(slot).start()

        @pl.when(j == nb - 1)
        def _():
            ck = pltpu.make_async_copy(dkacc.at[pl.ds(PADR, S), :], dqkv_hbm.at[b, :, pl.ds(D, D)], sem.at[0])
            cv = pltpu.make_async_copy(dvacc.at[pl.ds(PADR, S), :], dqkv_hbm.at[b, :, pl.ds(2 * D, D)], sem.at[1])
            ck.start()
            cv.start()
            ck.wait()
            cv.wait()

        @pl.when((b == B - 1) & (j == nb - 1))
        def _():
            cb = pltpu.make_async_copy(dbias_v, dbias_hbm, sem.at[2])
            cb.start()
            dq_out(0).wait()
            dq_out(1).wait()
            cb.wait()

    blk = _nbytes((QROWS, D), BF16) * 2
    scr = (2 * _nbytes((PADR + S, D), BF16) + 2 * _nbytes((PADR + S, D), F32) + 2 * _nbytes(bias.shape, F32)
           + 8 * _nbytes((2 * QROWS, WIN), F32) + 2 * _nbytes((QROWS, D), F32))
    return pl.pallas_call(
        body, name=name, grid=(B, nb),
        in_specs=[pl.BlockSpec((1, QROWS, D), lambda b, j: (b, j, 0)),
                  pl.BlockSpec((1, QROWS, D), lambda b, j: (b, j, 0)),
                  pl.BlockSpec(memory_space=pl.ANY), pl.BlockSpec(memory_space=pl.ANY)],
        out_specs=[pl.BlockSpec(memory_space=pl.ANY), pl.BlockSpec(memory_space=pl.ANY)],
        out_shape=[jax.ShapeDtypeStruct((B, S, 3 * D), F32), jax.ShapeDtypeStruct(bias.shape, F32)],
        scratch_shapes=[pltpu.VMEM((PADR + S, D), BF16), pltpu.VMEM((PADR + S, D), BF16),
                        pltpu.VMEM((PADR + S, D), F32), pltpu.VMEM((PADR + S, D), F32),
                        pltpu.VMEM(bias.shape, F32), pltpu.VMEM(bias.shape, F32),
                        pltpu.VMEM((2, QROWS, D), F32), pltpu.SemaphoreType.DMA((5,))],
        compiler_params=_params(blk, scr),
    )(qkv, do, bias, qkv)


def _mla_raw_t(k2, kj, q, QB):
    return lax.dot_general(k2[_blk(kj, QB), :], q, NT, preferred_element_type=F32)


def _blk(kj, QB):
    return pl.ds(kj * QB, QB) if isinstance(kj, int) else pl.ds(pl.multiple_of(kj * QB, QB), QB)


def _mla_diag_pen(QB):
    kc = jnp.right_shift(lax.broadcasted_iota(I32, (QB, QB), 0), CHUNK_SHIFT)
    qc = jnp.right_shift(lax.broadcasted_iota(I32, (QB, QB), 1), CHUNK_SHIFT)
    return jnp.where(kc <= qc, 0.0, NEG_INF).astype(F32)


def _mla_fill_keys(kv_ref, kr_ref, k2):
    k2[:, :NOPE] = kv_ref[0, :, :NOPE]
    k2[:, NOPE:] = kr_ref[0]


def _t(x):
    return x.astype(F32).T


def mla_fwd(name, qf, kv, kr):
    B, S, W = qf.shape
    HB = W // 256
    QB = _tile(S, 256, CHUNK)
    nq = S // QB
    scale = (NOPE + ROPE) ** -0.5

    def body(q_ref, kv_ref, kr_ref, o_ref, lse_ref, k2, vt, st_buf, pen):
        qi = pl.program_id(2)

        @pl.when(qi == 0)
        def _():
            pen[...] = _mla_diag_pen(QB)
            _mla_fill_keys(kv_ref, kr_ref, k2)
            for kj in range(nq):
                vt[kj] = _t(kv_ref[0, kj * QB:(kj + 1) * QB, NOPE:]).astype(BF16)

        q = q_ref[0]
        st_buf[0] = _mla_raw_t(k2, 0, q, QB)

        def step(kj, carry):
            m, l, acc = carry
            cur = lax.rem(kj, 2)
            st_raw = st_buf[cur]
            st_buf[1 - cur] = _mla_raw_t(k2, jnp.minimum(kj + 1, qi), q, QB)
            st = st_raw * scale + jnp.where(kj == qi, pen[...], 0.0)
            m_new = jnp.maximum(m, jnp.max(st, axis=0, keepdims=True))
            a = jnp.exp(m - m_new)
            pt = jnp.exp(st - m_new)
            l = a * l + jnp.sum(pt, axis=0, keepdims=True)
            acc = a * acc + jnp.dot(vt[kj], pt.astype(BF16), preferred_element_type=F32)
            return m_new, l, acc

        init = (jnp.full((1, QB), NEG_INF, F32), jnp.zeros((1, QB), F32), jnp.zeros((NOPE, QB), F32))
        m, l, acc = lax.fori_loop(0, qi + 1, step, init)
        o_ref[0] = (acc * (1.0 / l)).T
        lse_ref[0, 0] = m + jnp.log(l)

    blk = (_nbytes((QB, 256), BF16) + _nbytes((S, 256), BF16) + _nbytes((S, LANE), BF16)
           + _nbytes((QB, LANE), F32))
    return pl.pallas_call(
        body, name=name, grid=(B, HB, nq),
        in_specs=[pl.BlockSpec((1, QB, 256), lambda b, h, i: (b, i, h)),
                  pl.BlockSpec((1, S, 256), lambda b, h, i: (b, 0, h)),
                  pl.BlockSpec((1, S, LANE), lambda b, h, i: (b, 0, 0))],
        out_specs=[pl.BlockSpec((1, QB, LANE), lambda b, h, i: (b, i, h)),
                   pl.BlockSpec((1, 1, 1, QB), lambda b, h, i: (b, h, 0, i))],
        out_shape=[jax.ShapeDtypeStruct((B, S, HB * LANE), F32), jax.ShapeDtypeStruct((B, HB, 1, S), F32)],
        scratch_shapes=[pltpu.VMEM((S, 256), BF16), pltpu.VMEM((nq, NOPE, QB), BF16),
                        pltpu.VMEM((2, QB, QB), F32), pltpu.VMEM((QB, QB), F32)],
        compiler_params=_params(blk, 2 * _nbytes((S, 256), BF16) + 10 * _nbytes((QB, QB), F32)),
    )(qf, kv, kr)


def mla_bwd(name, qf, kv, kr, do, o, lse, tabs):
    B, S, W = qf.shape
    HB = W // 256
    QB = _tile(S, 256, CHUNK)
    nq = S // QB
    scale = (NOPE + ROPE) ** -0.5

    def body(q_ref, kv_ref, kr_ref, do_ref, o_ref, lse_ref, ct_ref, s1_ref, s2_ref, dq_ref, dkv_ref, dkr_ref,
             k2, kt, dot_, delta, dqt, st_buf, dp_buf, pen):
        h = pl.program_id(1)
        pen[...] = _mla_diag_pen(QB)
        dkv_ref[...] = jnp.zeros_like(dkv_ref)

        @pl.when(h == 0)
        def _():
            dkr_ref[...] = jnp.zeros_like(dkr_ref)

        _mla_fill_keys(kv_ref, kr_ref, k2)
        for i in range(nq):
            rows = slice(i * QB, (i + 1) * QB)
            kt[i] = _t(k2[rows, :]).astype(BF16)
            dot32 = _t(do_ref[0, rows, :])
            delta[i] = jnp.sum(dot32 * o_ref[0, rows, :].T, axis=0, keepdims=True)
            dot_[i] = dot32.astype(BF16)

        for qi in range(nq):
            rows = slice(qi * QB, (qi + 1) * QB)
            q = q_ref[0, rows, :]
            dob = do_ref[0, rows, :]
            lse_q = lse_ref[0, 0, :, rows]
            delta_q = delta[qi]
            dqt[...] = jnp.zeros_like(dqt)

            def raw(kj, slot, q=q, qi=qi):
                st_buf[slot] = _mla_raw_t(k2, kj, q, QB)
                dp_buf[slot] = jnp.dot(kv_ref[0, _blk(kj, QB), NOPE:], dot_[qi], preferred_element_type=F32)

            raw(0, 0)

            def step(kj, carry, q=q, dob=dob, lse_q=lse_q, delta_q=delta_q, qi=qi, raw=raw):
                ks = pl.ds(pl.multiple_of(kj * QB, QB), QB)
                cur = lax.rem(kj, 2)
                st_raw, dp_raw = st_buf[cur], dp_buf[cur]
                raw(jnp.minimum(kj + 1, qi), 1 - cur)
                pt = jnp.exp(st_raw * scale + jnp.where(kj == qi, pen[...], 0.0) - lse_q)
                dst = (pt * (dp_raw - delta_q) * scale).astype(BF16)
                dkv_ref[0, ks, NOPE:] += jnp.dot(pt.astype(BF16), dob, preferred_element_type=F32)
                dk2 = jnp.dot(dst, q, preferred_element_type=F32)
                dkv_ref[0, ks, :NOPE] += dk2[:, :NOPE]
                dkr_ref[0, ks, :] += dk2[:, NOPE:]
                dqt[...] += jnp.dot(kt[kj], dst, preferred_element_type=F32)
                return carry

            lax.fori_loop(0, qi + 1, step, 0)
            dq = dqt[...].T
            dq_ref[0, rows, :NOPE] = dq[:, :NOPE].astype(dq_ref.dtype)
            dq_ref[0, rows, NOPE:] = _rope_tile_bwd(dq[:, NOPE:], ct_ref[rows, :], s1_ref[rows, :],
                                                    s2_ref[rows, :]).astype(dq_ref.dtype)

    head = lambda w: pl.BlockSpec((1, S, w), lambda b, h: (b, 0, h))
    shared = pl.BlockSpec((1, S, LANE), lambda b, h: (b, 0, 0))
    blk = (2 * _nbytes((S, 256), BF16) + 2 * _nbytes((S, LANE), BF16) + _nbytes((S, LANE), F32)
           + 2 * _nbytes((S, 256), F32) + _nbytes((S, LANE), F32))
    scr = 3 * _nbytes((S, 256), BF16) + 14 * _nbytes((QB, QB), F32)
    return pl.pallas_call(
        body, name=name, grid=(B, HB),
        in_specs=[head(256), head(256), shared, head(LANE), head(LANE),
                  pl.BlockSpec((1, 1, 1, S), lambda b, h: (b, h, 0, 0))]
        + [pl.BlockSpec((S, LANE), lambda b, h: (0, 0))] * 3,
        out_specs=[head(256), head(256), shared],
        out_shape=[jax.ShapeDtypeStruct((B, S, W), BF16), jax.ShapeDtypeStruct((B, S, W), F32),
                   jax.ShapeDtypeStruct((B, S, LANE), F32)],
        scratch_shapes=[pltpu.VMEM((S, 256), BF16), pltpu.VMEM((nq, 256, QB), BF16),
                        pltpu.VMEM((nq, NOPE, QB), BF16), pltpu.VMEM((nq, 1, QB), F32),
                        pltpu.VMEM((256, QB), F32), pltpu.VMEM((2, QB, QB), F32), pltpu.VMEM((2, QB, QB), F32),
                        pltpu.VMEM((QB, QB), F32)],
        compiler_params=_params(blk, scr),
    )(qf, kv, kr, do, o, lse, *tabs)


GROUP_STEPS = 4


def cast_group(name, ws, layers, idx, after=None):
    n = len(ws)
    n_in = n + (after is not None)

    def body(k_ref, *refs):
        for i in range(n):
            refs[n_in + i][...] = refs[i][...].astype(BF16)

    def spec_in(w, layer):
        return pl.BlockSpec((None, w.shape[1] // GROUP_STEPS, w.shape[2]), lambda r, k_ref: (layer, r, 0))

    def spec_out(w):
        return pl.BlockSpec((None, w.shape[1] // GROUP_STEPS, w.shape[2]), lambda r, k_ref: (k_ref[0], r, 0))

    return pl.pallas_call(
        body, name=name,
        grid_spec=pltpu.PrefetchScalarGridSpec(
            num_scalar_prefetch=1, grid=(GROUP_STEPS,),
            in_specs=([spec_in(w, l) for w, l in zip(ws, layers)]
                      + [pl.BlockSpec(memory_space=pl.ANY)] * (after is not None)),
            out_specs=[spec_out(w) for w in ws]),
        out_shape=[jax.ShapeDtypeStruct((N_CHIPS, *w.shape[1:]), BF16) for w in ws],
        compiler_params=_params(sum(_nbytes(w.shape[1:], F32) * 3 // 2 for w in ws) // GROUP_STEPS),
    )(idx, *ws, *([] if after is None else [after]))


def adamw(name, w, g, m, v):
    R, C = w.shape
    tr = _tile(R, max(8, (1 << 18) // C // 8 * 8), 8)
    c1 = 1.0 - ADAM_B1 ** ADAM_STEP
    c2 = 1.0 - ADAM_B2 ** ADAM_STEP

    def body(w_ref, g_ref, m_ref, v_ref, d_ref, mo_ref, vo_ref):
        gv = g_ref[...]
        mn = ADAM_B1 * m_ref[...] + (1.0 - ADAM_B1) * gv
        vn = ADAM_B2 * v_ref[...] + (1.0 - ADAM_B2) * (gv * gv)
        mo_ref[...] = mn
        vo_ref[...] = vn
        d_ref[...] = -ADAM_LR * ((mn / c1) / (jnp.sqrt(vn / c2) + ADAM_EPS) + ADAM_WD * w_ref[...])

    spec = pl.BlockSpec((tr, C), lambda r: (r, 0))
    return pl.pallas_call(
        body, name=name, grid=(R // tr,), in_specs=[spec] * 4, out_specs=[spec] * 3,
        out_shape=[jax.ShapeDtypeStruct((R, C), F32)] * 3,
        compiler_params=_params(7 * _nbytes((tr, C), F32), 4 * _nbytes((tr, C), F32)),
    )(w, g, m, v)


def half_sum_group(name, dws, landed, idx):
    n = len(dws)
    steps = GROUP_STEPS // 2

    def body(i_ref, *refs):
        for i in range(n):
            refs[2 * n + i][...] = (refs[i][...].astype(F32) + refs[n + i][...].astype(F32)).astype(BF16)

    def own(d):
        return pl.BlockSpec((None, None, d.shape[2] // steps, d.shape[3]), lambda k, r, i_ref: (k, i_ref[1], r, 0))

    def flat(d):
        return pl.BlockSpec((None, d.shape[2] // steps, d.shape[3]), lambda k, r, i_ref: (k, r, 0))

    return pl.pallas_call(
        body, name=name,
        grid_spec=pltpu.PrefetchScalarGridSpec(
            num_scalar_prefetch=1, grid=(N_CHIPS, steps),
            in_specs=[own(d) for d in dws] + [flat(d) for d in dws], out_specs=[flat(d) for d in dws]),
        out_shape=[jax.ShapeDtypeStruct((N_CHIPS, *d.shape[2:]), BF16) for d in dws],
        compiler_params=_params(sum(3 * _nbytes(d.shape[2:], BF16) for d in dws) // steps),
    )(idx, *dws, *landed)


def chip_sum_group(name, parts, landed, gbufs, layers, idx):
    n = len(parts)
    steps = GROUP_STEPS // 2

    def body(i_ref, *refs):
        for i in range(n):
            a, b = refs[i], refs[n + i]
            refs[3 * n + i][...] = ((a[...].astype(F32) + b[0].astype(F32)) + b[1].astype(F32)) + b[2].astype(F32)

    def mine(p):
        return pl.BlockSpec((None, p.shape[1] // steps, p.shape[2]), lambda r, i_ref: (i_ref[0], r, 0))

    def three(p):
        return pl.BlockSpec((3, p.shape[1] // steps, p.shape[2]), lambda r, i_ref: (0, r, 0))

    def out(p, layer):
        return pl.BlockSpec((None, None, p.shape[1] // steps, p.shape[2]), lambda r, i_ref: (layer, i_ref[1], r, 0))

    return pl.pallas_call(
        body, name=name,
        grid_spec=pltpu.PrefetchScalarGridSpec(
            num_scalar_prefetch=1, grid=(steps,),
            in_specs=[mine(p) for p in parts] + [three(p) for p in parts] + [pl.BlockSpec(memory_space=pl.ANY)] * n,
            out_specs=[out(p, l) for p, l in zip(parts, layers)]),
        out_shape=[jax.ShapeDtypeStruct(g.shape, F32) for g in gbufs],
        input_output_aliases={1 + 2 * n + i: i for i in range(n)},
        compiler_params=_params(sum(6 * _nbytes(p.shape[1:], BF16) for p in parts) // steps),
    )(idx, *parts, *landed, *gbufs)


ANY = pl.BlockSpec(memory_space=pl.ANY)


def _place():
    x, y, c = lax.axis_index("x"), lax.axis_index("y"), lax.axis_index("c")
    chips = [(1 - x, y), (x, 1 - y), (1 - x, 1 - y)]
    return x, y, c, chips


HBM = pl.BlockSpec(memory_space=pltpu.HBM)
SEM = pl.BlockSpec(memory_space=pltpu.SEMAPHORE)
EFFECT = pltpu.SideEffectType.DATAFLOW_SIDE_EFFECTING


def _in_hbm(a):
    return pltpu.with_memory_space_constraint(a, pltpu.HBM)


def _ici_copy(src, dst, send_sems, recv_sems, k, peer):
    return pltpu.make_async_remote_copy(src_ref=src, dst_ref=dst, send_sem=send_sems.at[k], recv_sem=recv_sems.at[k],
                                        device_id=peer, device_id_type=MESH)


def ici_start(name, bufs, lands, after, gather):
    n, nl = len(bufs), len(lands)

    def body(*refs):
        b_in = refs[:n]
        send_sems, recv_sems = refs[n + nl + 1], refs[n + nl + 2]
        b_out = refs[n + nl + 3:2 * n + nl + 3]
        l_out = refs[2 * n + nl + 3:2 * n + 2 * nl + 3]
        token = refs[-1]
        x, y, c, chips = _place()
        kme = 2 * x + y
        for i in range(n):
            for j in range(3):
                peer = (*chips[j], c)
                if gather:
                    _ici_copy(b_out[i].at[kme, c], b_out[i].at[kme, c], send_sems, recv_sems, 3 * i + j, peer).start()
                else:
                    kd = 2 * chips[j][0] + chips[j][1]
                    _ici_copy(b_out[i].at[kd], l_out[i].at[j], send_sems, recv_sems, 3 * i + j, peer).start()
        token[...] = jnp.zeros_like(token)

    arrays = [*bufs, *lands]
    outs = pl.pallas_call(
        body, name=name,
        in_specs=[HBM] * (n + nl) + [ANY],
        out_specs=(SEM, SEM, *[HBM] * (n + nl), pl.BlockSpec(memory_space=pltpu.VMEM)),
        out_shape=(pltpu.SemaphoreType.DMA((3 * n,)), pltpu.SemaphoreType.DMA((3 * n,)),
                   *[pltpu.HBM(a.shape, a.dtype) for a in arrays], jax.ShapeDtypeStruct((8, LANE), F32)),
        input_output_aliases={i: 2 + i for i in range(n + nl)},
        compiler_params=pltpu.CompilerParams(has_side_effects=EFFECT),
    )(*[_in_hbm(a) for a in arrays], after)
    return outs[0], outs[1], list(outs[2:2 + n]), list(outs[2 + n:2 + n + nl]), outs[-1]


def ici_wait(name, send_sems, recv_sems, bufs, lands, after, gather):
    n, nl = len(bufs), len(lands)

    def body(*refs):
        b_in, l_in = refs[:n], refs[n:n + nl]
        send_sems, recv_sems = refs[n + nl], refs[n + nl + 1]
        x, y, c, chips = _place()
        kme = 2 * x + y
        for i in range(n):
            for j in range(3):
                peer = (*chips[j], c)
                kj = 2 * chips[j][0] + chips[j][1]
                if gather:
                    _ici_copy(b_in[i].at[kme, c], b_in[i].at[kme, c], send_sems, recv_sems, 3 * i + j, peer).wait_send()
                    _ici_copy(b_in[i].at[kj, c], b_in[i].at[kj, c], send_sems, recv_sems, 3 * i + j, peer).wait_recv()
                else:
                    _ici_copy(b_in[i].at[kj], l_in[i].at[j], send_sems, recv_sems, 3 * i + j, peer).wait_send()
                    _ici_copy(b_in[i].at[kj], l_in[i].at[j], send_sems, recv_sems, 3 * i + j, peer).wait_recv()

    arrays = [*bufs, *lands]
    outs = pl.pallas_call(
        body, name=name,
        in_specs=[HBM] * (n + nl) + [SEM, SEM, ANY],
        out_specs=tuple([HBM] * (n + nl)),
        out_shape=tuple(pltpu.HBM(a.shape, a.dtype) for a in arrays),
        input_output_aliases={i: i for i in range(n + nl)},
        compiler_params=pltpu.CompilerParams(has_side_effects=EFFECT),
    )(*arrays, send_sems, recv_sems, after)
    return list(outs[:n]), list(outs[n:])


def gather_pair_pass(name, bufs):
    n = len(bufs)

    def body(*refs):
        b = refs[n:2 * n]
        send_sems, recv_sems = refs[2 * n:]
        x, y, c, chips = _place()
        sib = (x, y, 1 - c)

        def d2d(i, j, which):
            kj = 2 * chips[j][0] + chips[j][1]
            return _ici_copy(b[i].at[kj, which], b[i].at[kj, which], send_sems, recv_sems, 3 * i + j, sib)

        for i in range(n):
            for j in range(3):
                d2d(i, j, c).start()
        for i in range(n):
            for j in range(3):
                d2d(i, j, 1 - c).wait_recv()
        for i in range(n):
            for j in range(3):
                d2d(i, j, c).wait_send()

    return pl.pallas_call(
        body, name=name, in_specs=[ANY] * n, out_specs=[ANY] * n,
        out_shape=[jax.ShapeDtypeStruct(a.shape, a.dtype) for a in bufs],
        input_output_aliases={i: i for i in range(n)},
        scratch_shapes=[pltpu.SemaphoreType.DMA((3 * n,)), pltpu.SemaphoreType.DMA((3 * n,))],
    )(*bufs)


def pair_exchange(name, dws):
    n = len(dws)

    def body(*refs):
        ins, outs = refs[:n], refs[n:2 * n]
        send_sems, recv_sems = refs[2 * n:]
        x, y, c, _ = _place()
        copies = []
        for i in range(n):
            copies.append(pltpu.make_async_remote_copy(
                src_ref=ins[i].at[:, 1 - c], dst_ref=outs[i],
                send_sem=send_sems.at[i], recv_sem=recv_sems.at[i],
                device_id=(x, y, 1 - c), device_id_type=MESH))
            copies[i].start()
        for cp in copies:
            cp.wait_recv()
        for cp in copies:
            cp.wait_send()

    return pl.pallas_call(
        body, name=name, in_specs=[ANY] * n, out_specs=[ANY] * n,
        out_shape=[jax.ShapeDtypeStruct((N_CHIPS, *d.shape[2:]), d.dtype) for d in dws],
        scratch_shapes=[pltpu.SemaphoreType.DMA((n,)), pltpu.SemaphoreType.DMA((n,))],
    )(*dws)


def pair_assemble(gbufs):
    n = len(gbufs)

    def body(*refs):
        bufs = refs[n:2 * n]
        send_sems, recv_sems = refs[2 * n:]
        x, y, c, _ = _place()
        copies = []
        for i in range(n):
            copies.append(pltpu.make_async_remote_copy(
                src_ref=bufs[i].at[:, c], dst_ref=bufs[i].at[:, c],
                send_sem=send_sems.at[i], recv_sem=recv_sems.at[i],
                device_id=(x, y, 1 - c), device_id_type=MESH))
            copies[i].start()
        for i in range(n):
            pltpu.make_async_remote_copy(
                src_ref=bufs[i].at[:, 1 - c], dst_ref=bufs[i].at[:, 1 - c],
                send_sem=send_sems.at[i], recv_sem=recv_sems.at[i],
                device_id=(x, y, 1 - c), device_id_type=MESH).wait_recv()
        for cp in copies:
            cp.wait_send()

    return pl.pallas_call(
        body, name="grad_pair_assemble", in_specs=[ANY] * n, out_specs=[ANY] * n,
        out_shape=[jax.ShapeDtypeStruct(g.shape, g.dtype) for g in gbufs],
        input_output_aliases={i: i for i in range(n)},
        scratch_shapes=[pltpu.SemaphoreType.DMA((n,)), pltpu.SemaphoreType.DMA((n,))],
    )(*gbufs)


def all_reduce_small(vec):
    NR = vec.shape[0]
    flips = [(fx, fy, fc) for fx in (0, 1) for fy in (0, 1) for fc in (0, 1)][1:]

    def body(v_ref, o_ref, buf, send_sems, recv_sems):
        x, y, c, _ = _place()
        me = 4 * x + 2 * y + c
        buf[me] = v_ref[...]
        copies = []
        for j, (fx, fy, fc) in enumerate(flips):
            peer = (1 - x if fx else x, 1 - y if fy else y, 1 - c if fc else c)
            copies.append(pltpu.make_async_remote_copy(
                src_ref=v_ref, dst_ref=buf.at[me], send_sem=send_sems.at[j], recv_sem=recv_sems.at[j],
                device_id=peer, device_id_type=MESH))
            copies[j].start()
        for cp in copies:
            cp.wait_recv()
        for cp in copies:
            cp.wait_send()
        acc = buf[0]
        for d in range(1, 8):
            acc = acc + buf[d]
        o_ref[...] = acc

    return pl.pallas_call(
        body, name="all_reduce_small",
        in_specs=[pl.BlockSpec(memory_space=pltpu.VMEM)], out_specs=pl.BlockSpec(memory_space=pltpu.VMEM),
        out_shape=jax.ShapeDtypeStruct((NR, LANE), F32),
        scratch_shapes=[pltpu.VMEM((8, NR, LANE), F32), pltpu.SemaphoreType.DMA((7,)),
                        pltpu.SemaphoreType.DMA((7,))],
    )(vec)


def _pack(arrays):
    flat = jnp.concatenate([a.reshape(-1).astype(F32) for a in arrays])
    n = flat.shape[0]
    npad = -(-n // (8 * LANE)) * (8 * LANE)
    return jnp.pad(flat, (0, npad - n)).reshape(npad // LANE, LANE)


def _unpack(buf, like):
    flat = buf.reshape(-1)
    out, off = [], 0
    for a in like:
        out.append(flat[off:off + a.size].reshape(a.shape))
        off += a.size
    return out


def kernel(x, ffn1_norm, ffn1_w_in, ffn1_w_out, mix_norm, ffn2_norm, ffn2_w_in, ffn2_w_out, a_w_qkv, a_rel_bias, a_w_o, kv_norm, kv_w_down, kv_latent_norm, kv_w_up, b_w_dq, b_q_norm, b_w_uq, b_w_o, final_norm, loss_target, m_ffn1_norm, m_ffn1_w_in, m_ffn1_w_out, m_mix_norm, m_ffn2_norm, m_ffn2_w_in, m_ffn2_w_out, m_a_w_qkv, m_a_rel_bias, m_a_w_o, m_kv_norm, m_kv_w_down, m_kv_latent_norm, m_kv_w_up, m_b_w_dq, m_b_q_norm, m_b_w_uq, m_b_w_o, m_final_norm, v_ffn1_norm, v_ffn1_w_in, v_ffn1_w_out, v_mix_norm, v_ffn2_norm, v_ffn2_w_in, v_ffn2_w_out, v_a_w_qkv, v_a_rel_bias, v_a_w_o, v_kv_norm, v_kv_w_down, v_kv_latent_norm, v_kv_w_up, v_b_w_dq, v_b_q_norm, v_b_w_uq, v_b_w_o, v_final_norm):
    B, S, D = x.shape
    T = B * S
    HB = D // 128
    QL = b_q_norm.shape[-1]
    KVL = kv_latent_norm.shape[0]
    hpc = HB // N_CHIPS
    tabs = rope_tables(S)
    idx = jnp.stack([2 * lax.axis_index("x") + lax.axis_index("y"), lax.axis_index("c")]).astype(I32)

    def halves(a):
        return a.reshape(*a.shape[:-2], 2, a.shape[-2] // 2, a.shape[-1])

    def whole(a):
        return a.reshape(*a.shape[:-3], 2 * a.shape[-2], a.shape[-1])

    kv_w_down_p = jnp.pad(kv_w_down, ((0, 0), (0, LANE - ROPE)))[None]
    b_w_uq_p = jnp.pad(b_w_uq.reshape(1, QL, hpc, NOPE + ROPE),
                       ((0, 0), (0, 0), (0, 0), (0, LANE - ROPE))).reshape(1, QL, hpc * 256)
    sharded = [("ffn1_w_in", ffn1_w_in), ("ffn1_w_out", ffn1_w_out), ("ffn2_w_in", ffn2_w_in),
               ("ffn2_w_out", ffn2_w_out), ("a_w_qkv", a_w_qkv), ("a_w_o", a_w_o),
               ("kv_w_down", kv_w_down_p), ("kv_w_up", kv_w_up[None]), ("b_w_dq", b_w_dq),
               ("b_w_uq", b_w_uq_p), ("b_w_o", b_w_o)]
    names = [nm for nm, _ in sharded]
    shard_of = dict(sharded)
    W = {}

    gather_groups = [
        [("ffn1_w_in", 0)],
        [("ffn1_w_out", 0)],
        [("a_w_qkv", 0), ("a_w_o", 0)],
        [("ffn2_w_in", 0), ("ffn2_w_out", 0), ("kv_w_down", 0), ("kv_w_up", 0)],
        [("ffn1_w_in", 1), ("ffn1_w_out", 1), ("b_w_dq", 0), ("b_w_uq", 0), ("b_w_o", 0), ("ffn2_w_in", 1),
         ("ffn2_w_out", 1)]]

    own = {}

    def cast(g, after=None):
        keys = gather_groups[g]
        own.update(zip(keys, cast_group(f"cast_group_{g}", [shard_of[nm] for nm, _ in keys], [l for _, l in keys],
                                        idx, after=after)))

    def gather_start(g, after):
        keys = gather_groups[g]
        ss, rs, bufs, _, token = ici_start(f"gather_start_{g}", [halves(own[k]) for k in keys], [], after, True)
        return (g, ss, rs, bufs), token

    def gather_finish(state, after):
        g, ss, rs, bufs = state
        bufs, _ = ici_wait(f"gather_wait_{g}", ss, rs, bufs, [], after, True)
        full = gather_pair_pass(f"gather_pair_{g}", bufs)
        for k, w in zip(gather_groups[g], full):
            W[k] = whole(w)
        return full[0]

    def tied(a, token):
        return a + token[0, 0]

    def col(nm, l=0):
        return W[(nm, l)]

    def row(nm, l=0):
        w = W[(nm, l)]
        return w.reshape(N_CHIPS * w.shape[1], w.shape[2])

    bias = rel_bias_tile("rel_bias_tile", a_rel_bias[0])

    def ffn_fwd(tag, h, g, w_in, w_out):
        xn = rms_fwd(f"{tag}_norm", h, g)
        u, act = ffn_in_act(f"{tag}_in", xn, w_in)
        return mm_roww(f"{tag}_out", act, w_out, F32, res=h, alpha=0.5), (xn, u, act)

    h0 = x.reshape(T, D)
    for g in range(3):
        cast(g)
    st0, tok0 = gather_start(0, h0)
    st1, tok1 = gather_start(1, tok0)
    st2, tok2 = gather_start(2, tok1)
    for g in range(3, len(gather_groups)):
        cast(g, tok2)
    xn0 = rms_fwd("l0f1_norm", h0, tied(ffn1_norm[0], tok2))
    gather_finish(st0, xn0)
    u0, act0 = ffn_in_act("l0f1_in", xn0, col("ffn1_w_in", 0))
    gather_finish(st1, u0)
    h1 = mm_roww("l0f1_out", act0, row("ffn1_w_out", 0), F32, res=h0, alpha=0.5)
    sv_f1a = (xn0, u0, act0)
    done2 = gather_finish(st2, h1)
    st3, tok3 = gather_start(3, done2)
    st4, tok4 = gather_start(4, tok3)
    hn_a = rms_fwd("l0mix_norm", h1, tied(mix_norm[0], tok4))
    qkv = mm_colw("l0_qkv", hn_a, col("a_w_qkv"), BF16).reshape(B, S, 3 * D)
    o_a = attn_a_fwd("l0_attn", qkv, bias).reshape(T, D)
    h2 = mm_roww("l0_attn_out", o_a, row("a_w_o"), F32, res=h1)
    gather_finish(st3, h2)
    h3, sv_f2a = ffn_fwd("l0f2", h2, ffn2_norm[0], col("ffn2_w_in", 0), row("ffn2_w_out", 0))

    hkv = rms_fwd("kv_norm", h3, kv_norm)
    ckr = mm_roww("kv_down", hkv, row("kv_w_down"), F32)
    ckv, kr = kvprep_fwd("kv_prep", ckr, kv_latent_norm, tabs, B, S)
    kvb = mm_colw("kv_up", ckv, col("kv_w_up"), BF16).reshape(B, S, HB * 256)
    gather_finish(st4, kvb)

    h4, sv_f1b = ffn_fwd("l1f1", h3, ffn1_norm[1], col("ffn1_w_in", 1), row("ffn1_w_out", 1))
    hn_b = rms_fwd("l1mix_norm", h4, mix_norm[1])
    cqp = mm_roww("l1_dq", hn_b, row("b_w_dq"), F32)
    cq = rms_fwd("l1_q_norm", cqp, b_q_norm[0])
    qf = uq_rope("l1_uq", cq, col("b_w_uq"), tabs, S).reshape(B, S, HB * 256)
    o_b, lse = mla_fwd("l1_attn", qf, kvb, kr)
    h5 = mm_roww("l1_attn_out", o_b.reshape(T, HB * LANE), row("b_w_o"), F32, res=h4)
    h6, sv_f2b = ffn_fwd("l1f2", h5, ffn2_norm[1], col("ffn2_w_in", 1), row("ffn2_w_out", 1))

    dh, g_final, loss_part = loss_head("loss_head", h6, final_norm, loss_target.reshape(T, D))

    gw = {}
    gbufs = {nm: lax.empty(halves(w).shape, F32) for nm, w in sharded}

    def reduce_start(r, keys, after):
        dws = [halves(gw[k]) for k in keys]
        landed = pair_exchange(f"grad_pair_exchange_{r}", dws)
        parts = half_sum_group(f"half_sum_{r}", dws, landed, idx)
        lands = [lax.empty((3, *p.shape[1:]), p.dtype) for p in parts]
        ss, rs, parts, lands, token = ici_start(f"reduce_start_{r}", parts, lands, after, False)
        return (r, keys, ss, rs, parts, lands), token

    def reduce_finish(state, after):
        r, keys, ss, rs, parts, lands = state
        parts, lands = ici_wait(f"reduce_wait_{r}", ss, rs, parts, lands, after, False)
        done = chip_sum_group(f"chip_sum_{r}", parts, lands, [gbufs[nm] for nm, _ in keys], [l for _, l in keys], idx)
        gbufs.update(zip([nm for nm, _ in keys], done))
        return done[0]

    def ffn_bwd(tag, dh, h_in, g, w_in, w_out, saved, key_in, key_out, after=None, then=None):
        xn, u, act = saved
        du = ffn_dact(f"{tag}_dact", dh, w_out, u, after=after)
        dwo = mm_droww(f"{tag}_dwout", act, dh, alpha=0.5)
        gw[key_out] = dwo.reshape(N_CHIPS, dwo.shape[0] // N_CHIPS, dwo.shape[1])
        gw[key_in] = mm_dcolw(f"{tag}_dwin", xn, du, pair_layout=True)
        token = then(du) if then is not None else None
        return dx_norm_bwd(f"{tag}_dxn", du, w_in, h_in, g, dres=dh, pair_layout=True, after=token)

    def chip_major(dw):
        return dw.reshape(N_CHIPS, dw.shape[0] // N_CHIPS, dw.shape[1])

    dh, g_f2b = ffn_bwd("l1f2b", dh, h5, ffn2_norm[1], col("ffn2_w_in", 1), row("ffn2_w_out", 1), sv_f2b,
                        ("ffn2_w_in", 1), ("ffn2_w_out", 1))
    red0, rtok0 = reduce_start(0, [("ffn2_w_in", 1), ("ffn2_w_out", 1)], dh)
    do_b = mm_roww_t("l1_attn_do", dh, row("b_w_o"), BF16, after=rtok0).reshape(B, S, HB * LANE)
    gw[("b_w_o", 0)] = chip_major(mm_droww("l1_attn_dwo", o_b.reshape(T, HB * LANE), dh))
    dqpre, dkv, dkr = mla_bwd("l1_attn_bwd", qf, kvb, kr, do_b, o_b, lse, tabs)
    dqpre = dqpre.reshape(T, HB * 256)
    gw[("b_w_uq", 0)] = mm_dcolw("l1_dwuq", cq, dqpre)
    dcqp, g_qn = dx_norm_bwd("l1_dcq", dqpre, col("b_w_uq"), cqp, b_q_norm[0])
    gw[("b_w_dq", 0)] = chip_major(mm_droww("l1_dwdq", hn_b, dcqp))
    dhn = mm_roww_t("l1_dhn", dcqp, row("b_w_dq"), F32)
    dh, g_mixb = rms_bwd("l1_dmix", h4, mix_norm[1], dhn, dres=dh)
    dh, g_f1b = ffn_bwd("l1f1b", dh, h3, ffn1_norm[1], col("ffn1_w_in", 1), row("ffn1_w_out", 1), sv_f1b,
                        ("ffn1_w_in", 1), ("ffn1_w_out", 1))
    fin0 = reduce_finish(red0, dh)
    red1, rtok1 = reduce_start(1, [("b_w_o", 0), ("b_w_uq", 0), ("b_w_dq", 0), ("ffn1_w_in", 1), ("ffn1_w_out", 1)], fin0)
    dkv2 = dkv.reshape(T, HB * 256)
    gw[("kv_w_up", 0)] = mm_dcolw("kv_dwup", ckv, dkv2, after=rtok1)
    dckv = mm_colw_t("kv_dckv", dkv2, col("kv_w_up"), F32, after=rtok1)
    dckr, g_lat = kvprep_bwd("kv_prep_bwd", ckr, kv_latent_norm, dckv, dkr, tabs, B, S)
    gw[("kv_w_down", 0)] = chip_major(mm_droww("kv_dwdown", hkv, dckr))
    dhkv = mm_roww_t("kv_dhkv", dckr, row("kv_w_down"), F32)
    dh, g_kvn = rms_bwd("kv_dnorm", h3, kv_norm, dhkv, dres=dh)
    dh, g_f2a = ffn_bwd("l0f2b", dh, h2, ffn2_norm[0], col("ffn2_w_in", 0), row("ffn2_w_out", 0), sv_f2a,
                        ("ffn2_w_in", 0), ("ffn2_w_out", 0))
    do_a = mm_roww_t("l0_attn_do", dh, row("a_w_o"), BF16).reshape(B, S, D)
    gw[("a_w_o", 0)] = chip_major(mm_droww("l0_attn_dwo", o_a, dh))
    dqkv, dbias = attn_a_bwd("l0_attn_bwd", qkv, do_a, bias)
    dqkv = dqkv.reshape(T, 3 * D)
    gw[("a_w_qkv", 0)] = mm_dcolw("l0_dwqkv", hn_a, dqkv)
    dh, g_mixa = dx_norm_bwd("l0_dhn", dqkv, col("a_w_qkv"), h1, mix_norm[0], dres=dh)
    fin1 = reduce_finish(red1, dh)
    red2, rtok2 = reduce_start(2, [("kv_w_up", 0), ("kv_w_down", 0), ("ffn2_w_in", 0), ("ffn2_w_out", 0),
                                   ("a_w_o", 0), ("a_w_qkv", 0)], fin1)
    last = {}

    def last_group(du):
        fin2 = reduce_finish(red2, gw[("ffn1_w_in", 0)])
        last["red"], token = reduce_start(3, [("ffn1_w_in", 0), ("ffn1_w_out", 0)], fin2)
        return token

    dh, g_f1a = ffn_bwd("l0f1b", dh, h0, ffn1_norm[0], col("ffn1_w_in", 0), row("ffn1_w_out", 0), sv_f1a,
                        ("ffn1_w_in", 0), ("ffn1_w_out", 0), after=rtok2, then=last_group)
    grad_x = dh.reshape(B, S, D)
    g_rel = rel_bias_grad("rel_bias_grad", dbias)[:, :2 * MAX_REL + 1][None]
    reduce_finish(last["red"], dh)

    full = [whole(g) for g in pair_assemble([gbufs[nm] for nm in names])]
    G = {nm: g for (nm, _), g in zip(sharded, full)}
    G["kv_w_down"] = G["kv_w_down"][0, :, :KVL + ROPE]
    G["kv_w_up"] = G["kv_w_up"][0]
    G["b_w_uq"] = G["b_w_uq"].reshape(1, QL, hpc, 256)[..., :NOPE + ROPE].reshape(b_w_uq.shape)

    small = [("ffn1_norm", jnp.stack([g_f1a, g_f1b])), ("mix_norm", jnp.stack([g_mixa, g_mixb])),
             ("ffn2_norm", jnp.stack([g_f2a, g_f2b])), ("a_rel_bias", g_rel), ("kv_norm", g_kvn),
             ("kv_latent_norm", g_lat), ("b_q_norm", g_qn[None]), ("final_norm", g_final)]
    red = all_reduce_small(_pack([loss_part] + [g for _, g in small]))
    unpacked = _unpack(red, [loss_part] + [g for _, g in small])
    loss = unpacked[0][0, 0]
    for (nm, _), g in zip(small, unpacked[1:]):
        G[nm] = g

    given = dict(ffn1_norm=(ffn1_norm, m_ffn1_norm, v_ffn1_norm), ffn1_w_in=(ffn1_w_in, m_ffn1_w_in, v_ffn1_w_in),
                 ffn1_w_out=(ffn1_w_out, m_ffn1_w_out, v_ffn1_w_out), mix_norm=(mix_norm, m_mix_norm, v_mix_norm),
                 ffn2_norm=(ffn2_norm, m_ffn2_norm, v_ffn2_norm), ffn2_w_in=(ffn2_w_in, m_ffn2_w_in, v_ffn2_w_in),
                 ffn2_w_out=(ffn2_w_out, m_ffn2_w_out, v_ffn2_w_out), a_w_qkv=(a_w_qkv, m_a_w_qkv, v_a_w_qkv),
                 a_rel_bias=(a_rel_bias, m_a_rel_bias, v_a_rel_bias), a_w_o=(a_w_o, m_a_w_o, v_a_w_o),
                 kv_norm=(kv_norm, m_kv_norm, v_kv_norm), kv_w_down=(kv_w_down, m_kv_w_down, v_kv_w_down),
                 kv_latent_norm=(kv_latent_norm, m_kv_latent_norm, v_kv_latent_norm),
                 kv_w_up=(kv_w_up, m_kv_w_up, v_kv_w_up), b_w_dq=(b_w_dq, m_b_w_dq, v_b_w_dq),
                 b_q_norm=(b_q_norm, m_b_q_norm, v_b_q_norm), b_w_uq=(b_w_uq, m_b_w_uq, v_b_w_uq),
                 b_w_o=(b_w_o, m_b_w_o, v_b_w_o), final_norm=(final_norm, m_final_norm, v_final_norm))
    order = list(given)
    delta, new_m, new_v = {}, {}, {}
    small_names = [nm for nm, _ in small]
    packed = [_pack([given[nm][k] for nm in small_names]) for k in range(3)]
    outs = adamw("adamw_small", packed[0], _pack([G[nm] for nm in small_names]), packed[1], packed[2])
    for dst, buf in zip((delta, new_m, new_v), outs):
        for nm, a in zip(small_names, _unpack(buf, [given[nm][0] for nm in small_names])):
            dst[nm] = a
    for nm, _ in sharded:
        w, m, v = given[nm]
        g = G[nm].reshape(w.shape)
        G[nm] = g
        two = lambda a: a.reshape(-1, a.shape[-1])
        d_, m_, v_ = adamw(f"adamw_{nm}", two(w), two(g), two(m), two(v))
        delta[nm], new_m[nm], new_v[nm] = d_.reshape(w.shape), m_.reshape(w.shape), v_.reshape(w.shape)

    return (loss, grad_x, *[G[n] for n in order], *[delta[n] for n in order],
            *[new_m[n] for n in order], *[new_v[n] for n in order])
```

```python
import math

import jax
import jax.numpy as jnp
from jax import lax
from jax.experimental import pallas as pl
from jax.experimental.pallas import tpu as pltpu

F32 = jnp.float32
BF16 = jnp.bfloat16
I32 = jnp.int32

CHUNK = 64
CHUNK_SHIFT = 6
HEAD_DIM_A = 64
LEFT_CHUNKS = 8
MAX_REL = 128
REL_PAD = 384
QROWS = 2 * CHUNK
WIN = (LEFT_CHUNKS + 2) * CHUNK
PADR = LEFT_CHUNKS * CHUNK
NOPE = 128
ROPE = 64
EPS = 1e-6
NEG_INF = -1e30
ROPE_THETA = 10000.0
ADAM_LR, ADAM_B1, ADAM_B2, ADAM_EPS, ADAM_WD, ADAM_STEP = 0.001, 0.9, 0.999, 1e-08, 0.01, 10
N_CHIPS = 4
LANE = 128
MESH = pl.DeviceIdType.MESH
VMEM_CAP_MB = 60

NN = (((1,), (0,)), ((), ()))
NT = (((1,), (1,)), ((), ()))
TN = (((0,), (0,)), ((), ()))


def _tile(n, pref, mult):
    t = (min(pref, n) // mult) * mult
    while t >= mult:
        if n % t == 0:
            return t
        t -= mult
    return n


def _nbytes(shape, dtype):
    return math.prod(shape) * jnp.dtype(dtype).itemsize


def _params(block_bytes, extra_bytes=0):
    need = 2 * block_bytes + extra_bytes
    mb = min(VMEM_CAP_MB, max(32, int(need * 1.25 / 2**20) + 8))
    return pltpu.CompilerParams(vmem_limit_bytes=mb * 2**20)


def _mm(name, kind, a, b, grid, a_spec, b_spec, o_spec, out_shape, out_dtype, blocks,
        red_axis=None, nred=1, alpha=1.0, res=None, res_spec=None, after=None):
    dims = {"nn": NN, "nt": NT, "tn": TN}[kind]
    has_res = res is not None
    acc_in_out = nred > 1 and out_dtype == F32 and not has_res and alpha == 1.0
    n_in = 2 + has_res + (after is not None)

    def body(*refs):
        a_ref, b_ref = refs[0], refs[1]
        r_ref = refs[2] if has_res else None
        o_ref = refs[n_in]
        p = lax.dot_general(a_ref[...].astype(BF16), b_ref[...].astype(BF16), dims,
                            preferred_element_type=F32)

        def finish(acc):
            y = acc if alpha == 1.0 else acc * alpha
            if has_res:
                y = r_ref[...] + y
            o_ref[...] = y.astype(o_ref.dtype)

        if nred == 1:
            finish(p)
            return
        k = pl.program_id(red_axis)
        tgt = o_ref if acc_in_out else refs[-1]

        @pl.when(k == 0)
        def _():
            tgt[...] = p

        @pl.when(k > 0)
        def _():
            tgt[...] += p

        if not acc_in_out:
            @pl.when(k == nred - 1)
            def _():
                finish(tgt[...])

    a_blk, b_blk, o_blk = blocks
    scratch = []
    extra = 0
    if nred > 1 and not acc_in_out:
        scratch = [pltpu.VMEM(o_blk, F32)]
        extra = _nbytes(o_blk, F32)
    blk = _nbytes(a_blk, a.dtype) + _nbytes(b_blk, b.dtype) + _nbytes(o_blk, out_dtype)
    ins, specs = [a, b], [a_spec, b_spec]
    if has_res:
        ins.append(res)
        specs.append(res_spec)
        blk += _nbytes(o_blk, res.dtype)
    if after is not None:
        ins.append(after)
        specs.append(pl.BlockSpec(memory_space=pl.ANY))
    extra += _nbytes(a_blk, BF16) + _nbytes(b_blk, BF16) + 2 * _nbytes(o_blk, F32)
    return pl.pallas_call(
        body, name=name, grid=grid, in_specs=specs, out_specs=o_spec,
        out_shape=jax.ShapeDtypeStruct(out_shape, out_dtype), scratch_shapes=scratch,
        compiler_params=_params(blk, extra),
    )(*ins)


def mm_colw(name, x, w3, out_dtype):
    T, K = x.shape
    _, _, nl = w3.shape
    tm = _tile(T, 512, 8)
    return _mm(name, "nn", x, w3, (N_CHIPS, T // tm),
               pl.BlockSpec((tm, K), lambda j, i: (i, 0)),
               pl.BlockSpec((None, K, nl), lambda j, i: (j, 0, 0)),
               pl.BlockSpec((tm, nl), lambda j, i: (i, j)),
               (T, N_CHIPS * nl), out_dtype, ((tm, K), (K, nl), (tm, nl)))


def _pair_chip(j):
    return (j % 2) * 2 + j // 2


def mm_colw_t(name, dy, w3, out_dtype, res=None, after=None, pair_layout=False):
    T = dy.shape[0]
    _, K, nl = w3.shape
    tm = _tile(T, 1024, 8)
    chip = _pair_chip if pair_layout else (lambda j: j)
    return _mm(name, "nt", dy, w3, (T // tm, N_CHIPS),
               pl.BlockSpec((tm, nl), lambda i, j: (i, j)),
               pl.BlockSpec((None, K, nl), lambda i, j: (chip(j), 0, 0)),
               pl.BlockSpec((tm, K), lambda i, j: (i, 0)),
               (T, K), out_dtype, ((tm, nl), (K, nl), (tm, K)),
               red_axis=1, nred=N_CHIPS, res=res,
               res_spec=pl.BlockSpec((tm, K), lambda i, j: (i, 0)), after=after)


def mm_dcolw(name, x, dy, after=None, pair_layout=False):
    T, K = x.shape
    nl = dy.shape[1] // N_CHIPS
    tt = _tile(T, 2048, 8)
    chip = _pair_chip if pair_layout else (lambda j: j)
    return _mm(name, "tn", x, dy, (N_CHIPS, T // tt),
               pl.BlockSpec((tt, K), lambda j, t: (t, 0)),
               pl.BlockSpec((tt, nl), lambda j, t: (t, j)),
               pl.BlockSpec((None, K, nl), lambda j, t: (chip(j), 0, 0)),
               (N_CHIPS, K, nl), BF16, ((tt, K), (tt, nl), (K, nl)),
               red_axis=1, nred=T // tt, after=after)


def mm_roww(name, x, w2, out_dtype, res=None, alpha=1.0):
    T, Kt = x.shape
    N = w2.shape[1]
    tm = _tile(T, 512, 8)
    return _mm(name, "nn", x, w2, (T // tm,),
               pl.BlockSpec((tm, Kt), lambda i: (i, 0)),
               pl.BlockSpec((Kt, N), lambda i: (0, 0)),
               pl.BlockSpec((tm, N), lambda i: (i, 0)),
               (T, N), out_dtype, ((tm, Kt), (Kt, N), (tm, N)),
               alpha=alpha, res=res, res_spec=pl.BlockSpec((tm, N), lambda i: (i, 0)))


def mm_roww_t(name, dy, w2, out_dtype, alpha=1.0, after=None):
    T, N = dy.shape
    Kt = w2.shape[0]
    tm = _tile(T, 512, 8)
    tk = _tile(Kt, 1408, LANE)
    return _mm(name, "nt", dy, w2, (Kt // tk, T // tm),
               pl.BlockSpec((tm, N), lambda j, i: (i, 0)),
               pl.BlockSpec((tk, N), lambda j, i: (j, 0)),
               pl.BlockSpec((tm, tk), lambda j, i: (i, j)),
               (T, Kt), out_dtype, ((tm, N), (tk, N), (tm, tk)), alpha=alpha, after=after)


def mm_droww(name, x, dy, alpha=1.0):
    T, Kt = x.shape
    N = dy.shape[1]
    tt = _tile(T, 2048, 8)
    tk = _tile(Kt, 1408, LANE)
    return _mm(name, "tn", x, dy, (Kt // tk, T // tt),
               pl.BlockSpec((tt, tk), lambda j, t: (t, j)),
               pl.BlockSpec((tt, N), lambda j, t: (t, 0)),
               pl.BlockSpec((tk, N), lambda j, t: (j, 0)),
               (Kt, N), BF16, ((tt, tk), (tt, N), (tk, N)),
               red_axis=1, nred=T // tt, alpha=alpha)


def rms_fwd(name, x, g):
    T, D = x.shape
    tm = _tile(T, 512, 8)

    def body(x_ref, g_ref, o_ref):
        xv = x_ref[...]
        r = lax.rsqrt(jnp.mean(xv * xv, axis=-1, keepdims=True) + EPS)
        o_ref[...] = (xv * r * g_ref[...]).astype(o_ref.dtype)

    return pl.pallas_call(
        body, name=name, grid=(T // tm,),
        in_specs=[pl.BlockSpec((tm, D), lambda i: (i, 0)), pl.BlockSpec((1, D), lambda i: (0, 0))],
        out_specs=pl.BlockSpec((tm, D), lambda i: (i, 0)),
        out_shape=jax.ShapeDtypeStruct((T, D), BF16),
        compiler_params=_params(_nbytes((tm, D), F32) * 2, 4 * _nbytes((tm, D), F32)),
    )(x, g.reshape(1, D))


def _rms_bwd_math(xv, gv, dy):
    r = lax.rsqrt(jnp.mean(xv * xv, axis=-1, keepdims=True) + EPS)
    xh = xv * r
    dyg = dy * gv
    dx = r * (dyg - xh * jnp.mean(dyg * xh, axis=-1, keepdims=True))
    dg = jnp.sum(dy * xh, axis=0, keepdims=True)
    return dx, dg


def rms_bwd(name, x, g, dy, dres=None):
    T, D = x.shape
    tm = _tile(T, 256, 8)
    has_res = dres is not None

    def body(*refs):
        x_ref, g_ref, dy_ref = refs[:3]
        r_ref = refs[3] if has_res else None
        dx_ref, dg_ref = refs[-2:]
        dx, dg = _rms_bwd_math(x_ref[...], g_ref[...], dy_ref[...].astype(F32))
        if has_res:
            dx = r_ref[...] + dx
        dx_ref[...] = dx

        @pl.when(pl.program_id(0) == 0)
        def _():
            dg_ref[...] = dg

        @pl.when(pl.program_id(0) > 0)
        def _():
            dg_ref[...] += dg

    row = pl.BlockSpec((tm, D), lambda i: (i, 0))
    vec = pl.BlockSpec((1, D), lambda i: (0, 0))
    ins, specs = [x, g.reshape(1, D), dy], [row, vec, row]
    if has_res:
        ins.append(dres)
        specs.append(row)
    dx, dg = pl.pallas_call(
        body, name=name, grid=(T // tm,), in_specs=specs, out_specs=[row, vec],
        out_shape=[jax.ShapeDtypeStruct((T, D), F32), jax.ShapeDtypeStruct((1, D), F32)],
        compiler_params=_params(_nbytes((tm, D), F32) * 4, 6 * _nbytes((tm, D), F32)),
    )(*ins)
    return dx, dg.reshape(D)


def dx_norm_bwd(name, dy, w3, x, g, dres=None, pair_layout=False, after=None):
    T = dy.shape[0]
    _, K, nl = w3.shape
    tm = _tile(T, 512, 8)
    chip = _pair_chip if pair_layout else (lambda j: j)
    has_res = dres is not None

    def body(*refs):
        dy_ref, w_ref, x_ref, g_ref = refs[:4]
        r_ref = refs[4] if has_res else None
        dx_ref, dg_ref, acc = refs[-3:]
        i, k = pl.program_id(0), pl.program_id(1)
        p = lax.dot_general(dy_ref[...].astype(BF16), w_ref[...], NT, preferred_element_type=F32)

        @pl.when(k == 0)
        def _():
            acc[...] = p

        @pl.when(k > 0)
        def _():
            acc[...] += p

        @pl.when(k == N_CHIPS - 1)
        def _():
            dx, dg = _rms_bwd_math(x_ref[...], g_ref[...], acc[...])
            dx_ref[...] = r_ref[...] + dx if has_res else dx

            @pl.when(i == 0)
            def _():
                dg_ref[...] = dg

            @pl.when(i > 0)
            def _():
                dg_ref[...] += dg

    row = pl.BlockSpec((tm, K), lambda i, j: (i, 0))
    vec = pl.BlockSpec((1, K), lambda i, j: (0, 0))
    ins = [dy, w3, x, g.reshape(1, K)]
    specs = [pl.BlockSpec((tm, nl), lambda i, j: (i, j)),
             pl.BlockSpec((None, K, nl), lambda i, j: (chip(j), 0, 0)), row, vec]
    if has_res:
        ins.append(dres)
        specs.append(row)
    if after is not None:
        ins.append(after)
        specs.append(pl.BlockSpec(memory_space=pl.ANY))
    blk = _nbytes((tm, nl), dy.dtype) + _nbytes((K, nl), BF16) + (2 + has_res) * _nbytes((tm, K), F32)
    dx, dg = pl.pallas_call(
        body, name=name, grid=(T // tm, N_CHIPS), in_specs=specs, out_specs=[row, vec],
        out_shape=[jax.ShapeDtypeStruct((T, K), F32), jax.ShapeDtypeStruct((1, K), F32)],
        scratch_shapes=[pltpu.VMEM((tm, K), F32)],
        compiler_params=_params(blk, 8 * _nbytes((tm, K), F32)),
    )(*ins)
    return dx, dg.reshape(K)


def ffn_in_act(name, x, w3):
    T, K = x.shape
    _, _, nl = w3.shape
    tm = _tile(T, 512, 8)

    def body(*refs):
        x_ref, wg_ref, wu_ref = refs[:3]
        u_ref, a_ref = refs[-2:]
        xv = x_ref[...]
        g = jnp.dot(xv, wg_ref[...], preferred_element_type=F32)
        up = jnp.dot(xv, wu_ref[...], preferred_element_type=F32)
        u_ref[:, :nl] = g.astype(u_ref.dtype)
        u_ref[:, nl:] = up.astype(u_ref.dtype)
        a_ref[...] = (g * jax.nn.sigmoid(g) * up).astype(a_ref.dtype)

    blk = _nbytes((tm, K), BF16) + 2 * _nbytes((K, nl), BF16) + _nbytes((tm, 3 * nl), BF16)
    return pl.pallas_call(
        body, name=name, grid=(2, T // tm),
        in_specs=[pl.BlockSpec((tm, K), lambda p, i: (i, 0)),
                  pl.BlockSpec((None, K, nl), lambda p, i: (p, 0, 0)),
                  pl.BlockSpec((None, K, nl), lambda p, i: (p + 2, 0, 0))],
        out_specs=[pl.BlockSpec((tm, 2 * nl), lambda p, i: (i, p)), pl.BlockSpec((tm, nl), lambda p, i: (i, p))],
        out_shape=[jax.ShapeDtypeStruct((T, 4 * nl), BF16), jax.ShapeDtypeStruct((T, 2 * nl), BF16)],
        compiler_params=_params(blk, 4 * _nbytes((tm, nl), F32)),
    )(x, w3, w3)


def ffn_dact(name, dh, w_out, u, after=None):
    T, N = dh.shape
    F = w_out.shape[0]
    nl = F // 2
    tm = _tile(T, 512, 8)

    def body(*refs):
        d_ref, w_ref, u_ref = refs[:3]
        o_ref = refs[-1]
        dact = 0.5 * lax.dot_general(d_ref[...].astype(BF16), w_ref[...], NT, preferred_element_type=F32)
        g = u_ref[:, :nl].astype(F32)
        up = u_ref[:, nl:].astype(F32)
        sig = jax.nn.sigmoid(g)
        o_ref[:, :nl] = (dact * up * (sig * (1.0 + g * (1.0 - sig)))).astype(o_ref.dtype)
        o_ref[:, nl:] = (dact * (g * sig)).astype(o_ref.dtype)

    ins = [dh, w_out, u]
    specs = [pl.BlockSpec((tm, N), lambda p, i: (i, 0)), pl.BlockSpec((nl, N), lambda p, i: (p, 0)),
             pl.BlockSpec((tm, 2 * nl), lambda p, i: (i, p))]
    if after is not None:
        ins.append(after)
        specs.append(pl.BlockSpec(memory_space=pl.ANY))
    blk = _nbytes((tm, N), F32) + _nbytes((nl, N), BF16) + 2 * _nbytes((tm, 2 * nl), BF16)
    return pl.pallas_call(
        body, name=name, grid=(2, T // tm), in_specs=specs,
        out_specs=pl.BlockSpec((tm, 2 * nl), lambda p, i: (i, p)),
        out_shape=jax.ShapeDtypeStruct((T, 2 * F), BF16),
        compiler_params=_params(blk, 6 * _nbytes((tm, nl), F32)),
    )(*ins)


def loss_head(name, h, g, target):
    T, D = h.shape
    tm = _tile(T, 256, 8)

    def body(h_ref, g_ref, t_ref, dh_ref, dg_ref, loss_ref):
        xv = h_ref[...]
        gv = g_ref[...]
        r = lax.rsqrt(jnp.mean(xv * xv, axis=-1, keepdims=True) + EPS)
        err = xv * r * gv - t_ref[...]
        part = 0.5 * jnp.sum(jnp.mean(err * err, axis=-1, keepdims=True), axis=0, keepdims=True)
        dx, dg = _rms_bwd_math(xv, gv, err * (1.0 / D))
        dh_ref[...] = dx
        part = jnp.broadcast_to(part, (1, LANE))

        @pl.when(pl.program_id(0) == 0)
        def _():
            dg_ref[...] = dg
            loss_ref[...] = part

        @pl.when(pl.program_id(0) > 0)
        def _():
            dg_ref[...] += dg
            loss_ref[...] += part

    row = pl.BlockSpec((tm, D), lambda i: (i, 0))
    vec = pl.BlockSpec((1, D), lambda i: (0, 0))
    dh, dg, loss = pl.pallas_call(
        body, name=name, grid=(T // tm,), in_specs=[row, vec, row],
        out_specs=[row, vec, pl.BlockSpec((1, LANE), lambda i: (0, 0))],
        out_shape=[jax.ShapeDtypeStruct((T, D), F32), jax.ShapeDtypeStruct((1, D), F32),
                   jax.ShapeDtypeStruct((1, LANE), F32)],
        compiler_params=_params(_nbytes((tm, D), F32) * 3, 6 * _nbytes((tm, D), F32)),
    )(h, g.reshape(1, D), target)
    return dh, dg.reshape(D), loss


def rope_tables(S):
    half = ROPE // 2
    freqs = ROPE_THETA ** (-jnp.arange(half, dtype=F32) / half)
    ang = jnp.arange(S, dtype=F32)[:, None] * freqs[None, :]
    cos, sin = jnp.cos(ang), jnp.sin(ang)
    z = jnp.zeros_like(cos)
    ct = jnp.concatenate([cos, cos, z, z], axis=1)
    s1 = jnp.concatenate([-sin, z, z, z], axis=1)
    s2 = jnp.concatenate([z, sin, z, z], axis=1)
    return ct, s1, s2


def _rope_tile(t, ct, s1, s2):
    return t * ct + pltpu.roll(t, 96, 1) * s1 + pltpu.roll(t, 32, 1) * s2


def _rope_tile_bwd(d, ct, s1, s2):
    return d * ct + pltpu.roll(d * s1, 32, 1) + pltpu.roll(d * s2, 96, 1)


def uq_rope(name, x, w3, tabs, S):
    T, K = x.shape
    _, _, nl = w3.shape
    tm = _tile(S, 512, 8)
    nt = S // tm

    def body(x_ref, w_ref, ct_ref, s1_ref, s2_ref, o_ref):
        q = jnp.dot(x_ref[...], w_ref[...], preferred_element_type=F32)
        ct, s1, s2 = ct_ref[...], s1_ref[...], s2_ref[...]
        for h in range(nl // 256):
            o_ref[:, 256 * h:256 * h + 128] = q[:, 256 * h:256 * h + 128].astype(o_ref.dtype)
            o_ref[:, 256 * h + 128:256 * h + 256] = _rope_tile(q[:, 256 * h + 128:256 * h + 256],
                                                               ct, s1, s2).astype(o_ref.dtype)

    tab = pl.BlockSpec((tm, LANE), lambda j, i: (i % nt, 0))
    blk = _nbytes((tm, K), BF16) + _nbytes((K, nl), BF16) + _nbytes((tm, nl), BF16) + 3 * _nbytes((tm, LANE), F32)
    return pl.pallas_call(
        body, name=name, grid=(N_CHIPS, T // tm),
        in_specs=[pl.BlockSpec((tm, K), lambda j, i: (i, 0)), pl.BlockSpec((None, K, nl), lambda j, i: (j, 0, 0)),
                  tab, tab, tab],
        out_specs=pl.BlockSpec((tm, nl), lambda j, i: (i, j)),
        out_shape=jax.ShapeDtypeStruct((T, N_CHIPS * nl), BF16),
        compiler_params=_params(blk, 4 * _nbytes((tm, nl), F32)),
    )(x, w3, *tabs)


def kvprep_fwd(name, ckr, g, tabs, B, S):
    T, W = ckr.shape
    KVL = W - LANE
    ts = _tile(S, 256, 8)

    def body(x_ref, g_ref, ct_ref, s1_ref, s2_ref, c_ref, k_ref):
        xv = x_ref[0, :, :KVL]
        r = lax.rsqrt(jnp.mean(xv * xv, axis=-1, keepdims=True) + EPS)
        c_ref[0] = (xv * r * g_ref[...]).astype(c_ref.dtype)
        k_ref[0] = _rope_tile(x_ref[0, :, KVL:], ct_ref[...], s1_ref[...], s2_ref[...]).astype(k_ref.dtype)

    tab = pl.BlockSpec((ts, LANE), lambda b, s: (s, 0))
    c, k = pl.pallas_call(
        body, name=name, grid=(B, S // ts),
        in_specs=[pl.BlockSpec((1, ts, W), lambda b, s: (b, s, 0)), pl.BlockSpec((1, KVL), lambda b, s: (0, 0)),
                  tab, tab, tab],
        out_specs=[pl.BlockSpec((1, ts, KVL), lambda b, s: (b, s, 0)),
                   pl.BlockSpec((1, ts, LANE), lambda b, s: (b, s, 0))],
        out_shape=[jax.ShapeDtypeStruct((B, S, KVL), BF16), jax.ShapeDtypeStruct((B, S, LANE), BF16)],
        compiler_params=_params(_nbytes((ts, W), F32) * 2, _nbytes((ts, W), F32) * 2),
    )(ckr.reshape(B, S, W), g.reshape(1, KVL), *tabs)
    return c.reshape(T, KVL), k


def kvprep_bwd(name, ckr, g, dc, dkr, tabs, B, S):
    T, W = ckr.shape
    KVL = W - LANE
    ts = _tile(S, 256, 8)

    def body(x_ref, g_ref, dc_ref, dk_ref, ct_ref, s1_ref, s2_ref, o_ref, dg_ref):
        dx, dg = _rms_bwd_math(x_ref[0, :, :KVL], g_ref[...], dc_ref[0])
        o_ref[0, :, :KVL] = dx
        o_ref[0, :, KVL:] = _rope_tile_bwd(dk_ref[0], ct_ref[...], s1_ref[...], s2_ref[...])
        first = (pl.program_id(0) == 0) & (pl.program_id(1) == 0)

        @pl.when(first)
        def _():
            dg_ref[...] = dg

        @pl.when(jnp.logical_not(first))
        def _():
            dg_ref[...] += dg

    tab = pl.BlockSpec((ts, LANE), lambda b, s: (s, 0))
    vec = pl.BlockSpec((1, KVL), lambda b, s: (0, 0))
    o, dg = pl.pallas_call(
        body, name=name, grid=(B, S // ts),
        in_specs=[pl.BlockSpec((1, ts, W), lambda b, s: (b, s, 0)), vec,
                  pl.BlockSpec((1, ts, KVL), lambda b, s: (b, s, 0)),
                  pl.BlockSpec((1, ts, LANE), lambda b, s: (b, s, 0)), tab, tab, tab],
        out_specs=[pl.BlockSpec((1, ts, W), lambda b, s: (b, s, 0)), vec],
        out_shape=[jax.ShapeDtypeStruct((B, S, W), F32), jax.ShapeDtypeStruct((1, KVL), F32)],
        compiler_params=_params(_nbytes((ts, W), F32) * 4, _nbytes((ts, W), F32) * 4),
    )(ckr.reshape(B, S, W), g.reshape(1, KVL), dc.reshape(B, S, KVL), dkr, *tabs)
    return o.reshape(T, W), dg.reshape(KVL)


DIAGS = 768


def _diag_onehot():
    col = lax.broadcasted_iota(I32, (REL_PAD, DIAGS), 1)
    row = lax.broadcasted_iota(I32, (REL_PAD, DIAGS), 0)
    idx = jnp.clip(PADR + QROWS - 1 - col, -MAX_REL, MAX_REL) + MAX_REL
    return (row == idx).astype(F32)


def rel_bias_tile(name, table):
    H = table.shape[0]
    tpad = jnp.pad(table, ((0, 0), (0, REL_PAD - table.shape[1])))

    def body(t_ref, o_ref):
        g = lax.dot_general(t_ref[...], _diag_onehot(), NN, precision=lax.Precision.HIGHEST,
                            preferred_element_type=F32)
        qc = jnp.right_shift(lax.broadcasted_iota(I32, (QROWS, WIN), 0), CHUNK_SHIFT)
        kc = jnp.right_shift(lax.broadcasted_iota(I32, (QROWS, WIN), 1), CHUNK_SHIFT)
        band = (kc >= qc) & (kc <= qc + LEFT_CHUNKS)
        for h in range(H):
            gb = jnp.broadcast_to(g[h:h + 1, :], (QROWS, DIAGS))
            tile = pltpu.roll(gb, DIAGS - (QROWS - 1), 1, stride=1, stride_axis=0)
            o_ref[h // 2, (h % 2) * QROWS:(h % 2 + 1) * QROWS, :] = jnp.where(band, tile[:, :WIN], NEG_INF)

    return pl.pallas_call(
        body, name=name, out_shape=jax.ShapeDtypeStruct((H // 2, 2 * QROWS, WIN), F32),
        compiler_params=_params(0, 2 * _nbytes((H // 2, 2 * QROWS, WIN), F32)),
    )(tpad)


def rel_bias_grad(name, dbias):
    H = 2 * dbias.shape[0]

    def body(d_ref, o_ref):
        flip = (lax.broadcasted_iota(I32, (QROWS, QROWS), 0) + lax.broadcasted_iota(I32, (QROWS, QROWS), 1)
                == QROWS - 1).astype(F32)
        rows = []
        for h in range(H):
            x = d_ref[h // 2, (h % 2) * QROWS:(h % 2 + 1) * QROWS, :]
            xr = lax.dot_general(flip, x, NN, precision=lax.Precision.HIGHEST, preferred_element_type=F32)
            xp = jnp.concatenate([xr, jnp.zeros((QROWS, DIAGS - WIN), F32)], axis=1)
            y = pltpu.roll(xp, 0, 1, stride=1, stride_axis=0)
            rows.append(jnp.sum(y, axis=0, keepdims=True))
        o_ref[...] = lax.dot_general(jnp.concatenate(rows, axis=0), _diag_onehot(), NT,
                                     precision=lax.Precision.HIGHEST, preferred_element_type=F32)

    return pl.pallas_call(
        body, name=name, out_shape=jax.ShapeDtypeStruct((H, REL_PAD), F32),
        compiler_params=_params(0, 2 * _nbytes(dbias.shape, F32)),
    )(dbias)


def _stack_pair(xp):
    lane = lax.broadcasted_iota(I32, xp.shape, 1)
    z = jnp.zeros_like(xp)
    return jnp.concatenate([jnp.where(lane < HEAD_DIM_A, xp, z), jnp.where(lane >= HEAD_DIM_A, xp, z)], axis=0)


def _unstack_pair(y):
    lane = lax.broadcasted_iota(I32, (QROWS, LANE), 1)
    return jnp.where(lane < HEAD_DIM_A, y[:QROWS], y[QROWS:])


def _attn_a_rowpen(j):
    w = lax.broadcasted_iota(I32, (1, WIN), 1)
    return jnp.where(w >= PADR - QROWS * j, 0.0, NEG_INF).astype(F32)


def _attn_a_load_bias(bias_hbm, bias_v, sem):
    cp = pltpu.make_async_copy(bias_hbm, bias_v, sem)
    cp.start()
    cp.wait()


def _attn_a_load_kv(qkv_hbm, b, kpad, vpad, sem, S, D):
    kpad[0:PADR, :] = jnp.zeros((PADR, D), BF16)
    vpad[0:PADR, :] = jnp.zeros((PADR, D), BF16)
    ck = pltpu.make_async_copy(qkv_hbm.at[b, :, pl.ds(D, D)], kpad.at[pl.ds(PADR, S), :], sem.at[0])
    cv = pltpu.make_async_copy(qkv_hbm.at[b, :, pl.ds(2 * D, D)], vpad.at[pl.ds(PADR, S), :], sem.at[1])
    ck.start()
    cv.start()
    ck.wait()
    cv.wait()


def _attn_a_exp(q2s, kp, bias, pen):
    s = lax.dot_general(q2s, kp, NT, preferred_element_type=F32) + bias + pen
    e = jnp.exp(s - jnp.max(s, axis=-1, keepdims=True))
    return e, 1.0 / jnp.sum(e, axis=-1, keepdims=True)


def attn_a_fwd(name, qkv, bias):
    B, S, D3 = qkv.shape
    D = D3 // 3
    H = D // HEAD_DIM_A
    nb = S // QROWS
    scale = HEAD_DIM_A ** -0.5

    def body(q_ref, bias_hbm, qkv_hbm, o_ref, kpad, vpad, bias_v, sem):
        b, j = pl.program_id(0), pl.program_id(1)

        @pl.when((b == 0) & (j == 0))
        def _():
            _attn_a_load_bias(bias_hbm, bias_v, sem.at[2])

        @pl.when(j == 0)
        def _():
            _attn_a_load_kv(qkv_hbm, b, kpad, vpad, sem, S, D)

        pen = _attn_a_rowpen(j)
        w0 = pl.multiple_of(j * QROWS, QROWS)
        for p in range(H // 2):
            ls = slice(p * LANE, (p + 1) * LANE)
            e, rl = _attn_a_exp(_stack_pair(q_ref[0, :, ls] * scale), kpad[pl.ds(w0, WIN), ls], bias_v[p], pen)
            o2 = jnp.dot(e.astype(BF16), vpad[pl.ds(w0, WIN), ls], preferred_element_type=F32) * rl
            o_ref[0, :, ls] = _unstack_pair(o2).astype(o_ref.dtype)

    scr = 2 * _nbytes((PADR + S, D), BF16) + _nbytes(bias.shape, F32) + 8 * _nbytes((2 * QROWS, WIN), F32)
    return pl.pallas_call(
        body, name=name, grid=(B, nb),
        in_specs=[pl.BlockSpec((1, QROWS, D), lambda b, j: (b, j, 0)),
                  pl.BlockSpec(memory_space=pl.ANY), pl.BlockSpec(memory_space=pl.ANY)],
        out_specs=pl.BlockSpec((1, QROWS, D), lambda b, j: (b, j, 0)),
        out_shape=jax.ShapeDtypeStruct((B, S, D), BF16),
        scratch_shapes=[pltpu.VMEM((PADR + S, D), BF16), pltpu.VMEM((PADR + S, D), BF16),
                        pltpu.VMEM(bias.shape, F32), pltpu.SemaphoreType.DMA((3,))],
        compiler_params=_params(2 * _nbytes((QROWS, D), BF16), scr),
    )(qkv, bias, qkv)


def attn_a_bwd(name, qkv, do, bias):
    B, S, D3 = qkv.shape
    D = D3 // 3
    H = D // HEAD_DIM_A
    nb = S // QROWS
    scale = HEAD_DIM_A ** -0.5

    def body(q_ref, do_ref, bias_hbm, qkv_hbm, dqkv_hbm, dbias_hbm, kpad, vpad, dkacc, dvacc, bias_v, dbias_v,
             dq_stage, sem):
        b, j = pl.program_id(0), pl.program_id(1)
        step = b * nb + j
        slot = lax.rem(step, 2)

        def dq_out(s):
            return pltpu.make_async_copy(dq_stage.at[s], dqkv_hbm.at[b, pl.ds(j * QROWS, QROWS), pl.ds(0, D)],
                                         sem.at[3 + s])

        @pl.when(step >= 2)
        def _():
            dq_out(slot).wait()

        @pl.when((b == 0) & (j == 0))
        def _():
            _attn_a_load_bias(bias_hbm, bias_v, sem.at[2])
            dbias_v[...] = jnp.zeros_like(dbias_v)

        @pl.when(j == 0)
        def _():
            _attn_a_load_kv(qkv_hbm, b, kpad, vpad, sem, S, D)
            dkacc[...] = jnp.zeros_like(dkacc)
            dvacc[...] = jnp.zeros_like(dvacc)

        pen = _attn_a_rowpen(j)
        w0 = pl.multiple_of(j * QROWS, QROWS)
        for p in range(H // 2):
            ls = slice(p * LANE, (p + 1) * LANE)
            q2s = _stack_pair(q_ref[0, :, ls] * scale)
            do2 = _stack_pair(do_ref[0, :, ls])
            kp = kpad[pl.ds(w0, WIN), ls]
            vp = vpad[pl.ds(w0, WIN), ls]
            e, rl = _attn_a_exp(q2s, kp, bias_v[p], pen)
            pr = e * rl
            dp = lax.dot_general(do2, vp, NT, preferred_element_type=F32)
            ds = pr * (dp - jnp.sum(pr * dp, axis=-1, keepdims=True))
            dbias_v[p] += ds
            dsb = ds.astype(BF16)
            dq_stage[slot, :, ls] = (_unstack_pair(jnp.dot(dsb, kp, preferred_element_type=F32))
                                     * scale).astype(dq_stage.dtype)
            dkacc[pl.ds(w0, WIN), ls] += lax.dot_general(dsb, q2s, TN, preferred_element_type=F32)
            dvacc[pl.ds(w0, WIN), ls] += lax.dot_general(pr.astype(BF16), do2, TN, preferred_element_type=F32)

        dq_out(slot).start()

        @pl.when(j == nb - 1)
        def _():
            kpad[pl.ds(PADR, S), :] = dkacc[pl.ds(PADR, S), :].astype(BF16)
            vpad[pl.ds(PADR, S), :] = dvacc[pl.ds(PADR, S), :].astype(BF16)
            ck = pltpu.make_async_copy(kpad.at[pl.ds(PADR, S), :], dqkv_hbm.at[b, :, pl.ds(D, D)], sem.at[0])
            cv = pltpu.make_async_copy(vpad.at[pl.ds(PADR, S), :], dqkv_hbm.at[b, :, pl.ds(2 * D, D)], sem.at[1])
            ck.start()
            cv.start()
            ck.wait()
            cv.wait()

        @pl.when((b == B - 1) & (j == nb - 1))
        def _():
            cb = pltpu.make_async_copy(dbias_v, dbias_hbm, sem.at[2])
            cb.start()
            dq_out(0).wait()
            dq_out(1).wait()
            cb.wait()

    blk = _nbytes((QROWS, D), BF16) * 2
    scr = (2 * _nbytes((PADR + S, D), BF16) + 2 * _nbytes((PADR + S, D), F32) + 2 * _nbytes(bias.shape, F32)
           + 8 * _nbytes((2 * QROWS, WIN), F32) + 2 * _nbytes((QROWS, D), F32))
    return pl.pallas_call(
        body, name=name, grid=(B, nb),
        in_specs=[pl.BlockSpec((1, QROWS, D), lambda b, j: (b, j, 0)),
                  pl.BlockSpec((1, QROWS, D), lambda b, j: (b, j, 0)),
                  pl.BlockSpec(memory_space=pl.ANY), pl.BlockSpec(memory_space=pl.ANY)],
        out_specs=[pl.BlockSpec(memory_space=pl.ANY), pl.BlockSpec(memory_space=pl.ANY)],
        out_shape=[jax.ShapeDtypeStruct((B, S, 3 * D), BF16), jax.ShapeDtypeStruct(bias.shape, F32)],
        scratch_shapes=[pltpu.VMEM((PADR + S, D), BF16), pltpu.VMEM((PADR + S, D), BF16),
                        pltpu.VMEM((PADR + S, D), F32), pltpu.VMEM((PADR + S, D), F32),
                        pltpu.VMEM(bias.shape, F32), pltpu.VMEM(bias.shape, F32),
                        pltpu.VMEM((2, QROWS, D), BF16), pltpu.SemaphoreType.DMA((5,))],
        compiler_params=_params(blk, scr),
    )(qkv, do, bias, qkv)


def _mla_raw_t(k2, kj, q, QB):
    return lax.dot_general(k2[_blk(kj, QB), :], q, NT, preferred_element_type=F32)


def _blk(kj, QB):
    return pl.ds(kj * QB, QB) if isinstance(kj, int) else pl.ds(pl.multiple_of(kj * QB, QB), QB)


def _mla_diag_pen(QB):
    kc = jnp.right_shift(lax.broadcasted_iota(I32, (QB, QB), 0), CHUNK_SHIFT)
    qc = jnp.right_shift(lax.broadcasted_iota(I32, (QB, QB), 1), CHUNK_SHIFT)
    return jnp.where(kc <= qc, 0.0, NEG_INF).astype(F32)


def _mla_fill_keys(kv_ref, kr_ref, k2):
    k2[:, :NOPE] = kv_ref[0, :, :NOPE]
    k2[:, NOPE:] = kr_ref[0]


def _t(x):
    return x.astype(F32).T


def mla_fwd(name, qf, kv, kr):
    B, S, W = qf.shape
    HB = W // 256
    QB = _tile(S, 256, CHUNK)
    nq = S // QB
    scale = (NOPE + ROPE) ** -0.5

    def body(q_ref, kv_ref, kr_ref, o_ref, lse_ref, k2, vt, st_buf, pen):
        qi = pl.program_id(2)

        @pl.when(qi == 0)
        def _():
            pen[...] = _mla_diag_pen(QB)
            _mla_fill_keys(kv_ref, kr_ref, k2)
            for kj in range(nq):
                vt[kj] = _t(kv_ref[0, kj * QB:(kj + 1) * QB, NOPE:]).astype(BF16)

        q = q_ref[0]
        st_buf[0] = _mla_raw_t(k2, 0, q, QB)

        def step(kj, carry):
            m, l, acc = carry
            cur = lax.rem(kj, 2)
            st_raw = st_buf[cur]
            st_buf[1 - cur] = _mla_raw_t(k2, jnp.minimum(kj + 1, qi), q, QB)
            st = st_raw * scale + jnp.where(kj == qi, pen[...], 0.0)
            m_new = jnp.maximum(m, jnp.max(st, axis=0, keepdims=True))
            a = jnp.exp(m - m_new)
            pt = jnp.exp(st - m_new)
            l = a * l + jnp.sum(pt, axis=0, keepdims=True)
            acc = a * acc + jnp.dot(vt[kj], pt.astype(BF16), preferred_element_type=F32)
            return m_new, l, acc

        init = (jnp.full((1, QB), NEG_INF, F32), jnp.zeros((1, QB), F32), jnp.zeros((NOPE, QB), F32))
        m, l, acc = lax.fori_loop(0, qi + 1, step, init)
        o_ref[0] = (acc * (1.0 / l)).T
        lse_ref[0, 0] = m + jnp.log(l)

    blk = (_nbytes((QB, 256), BF16) + _nbytes((S, 256), BF16) + _nbytes((S, LANE), BF16)
           + _nbytes((QB, LANE), F32))
    return pl.pallas_call(
        body, name=name, grid=(B, HB, nq),
        in_specs=[pl.BlockSpec((1, QB, 256), lambda b, h, i: (b, i, h)),
                  pl.BlockSpec((1, S, 256), lambda b, h, i: (b, 0, h)),
                  pl.BlockSpec((1, S, LANE), lambda b, h, i: (b, 0, 0))],
        out_specs=[pl.BlockSpec((1, QB, LANE), lambda b, h, i: (b, i, h)),
                   pl.BlockSpec((1, 1, 1, QB), lambda b, h, i: (b, h, 0, i))],
        out_shape=[jax.ShapeDtypeStruct((B, S, HB * LANE), F32), jax.ShapeDtypeStruct((B, HB, 1, S), F32)],
        scratch_shapes=[pltpu.VMEM((S, 256), BF16), pltpu.VMEM((nq, NOPE, QB), BF16),
                        pltpu.VMEM((2, QB, QB), F32), pltpu.VMEM((QB, QB), F32)],
        compiler_params=_params(blk, 2 * _nbytes((S, 256), BF16) + 10 * _nbytes((QB, QB), F32)),
    )(qf, kv, kr)


def mla_bwd(name, qf, kv, kr, do, o, lse, tabs):
    B, S, W = qf.shape
    HB = W // 256
    QB = _tile(S, 256, CHUNK)
    nq = S // QB
    scale = (NOPE + ROPE) ** -0.5

    def body(q_ref, kv_ref, kr_ref, do_ref, o_ref, lse_ref, ct_ref, s1_ref, s2_ref, dq_ref, dkv_ref, dkr_ref,
             k2, kt, dot_, delta, dqt, st_buf, dp_buf, pen, dkv_acc):
        h = pl.program_id(1)
        pen[...] = _mla_diag_pen(QB)
        dkv_acc[...] = jnp.zeros_like(dkv_acc)

        @pl.when(h == 0)
        def _():
            dkr_ref[...] = jnp.zeros_like(dkr_ref)

        _mla_fill_keys(kv_ref, kr_ref, k2)
        for i in range(nq):
            rows = slice(i * QB, (i + 1) * QB)
            kt[i] = _t(k2[rows, :]).astype(BF16)
            dot32 = _t(do_ref[0, rows, :])
            delta[i] = jnp.sum(dot32 * o_ref[0, rows, :].T, axis=0, keepdims=True)
            dot_[i] = dot32.astype(BF16)

        for qi in range(nq):
            rows = slice(qi * QB, (qi + 1) * QB)
            q = q_ref[0, rows, :]
            dob = do_ref[0, rows, :]
            lse_q = lse_ref[0, 0, :, rows]
            delta_q = delta[qi]
            dqt[...] = jnp.zeros_like(dqt)

            def raw(kj, slot, q=q, qi=qi):
                st_buf[slot] = _mla_raw_t(k2, kj, q, QB)
                dp_buf[slot] = jnp.dot(kv_ref[0, _blk(kj, QB), NOPE:], dot_[qi], preferred_element_type=F32)

            raw(0, 0)

            def step(kj, carry, q=q, dob=dob, lse_q=lse_q, delta_q=delta_q, qi=qi, raw=raw):
                ks = pl.ds(pl.multiple_of(kj * QB, QB), QB)
                cur = lax.rem(kj, 2)
                st_raw, dp_raw = st_buf[cur], dp_buf[cur]
                raw(jnp.minimum(kj + 1, qi), 1 - cur)
                pt = jnp.exp(st_raw * scale + jnp.where(kj == qi, pen[...], 0.0) - lse_q)
                dst = (pt * (dp_raw - delta_q) * scale).astype(BF16)
                dkv_acc[ks, NOPE:] += jnp.dot(pt.astype(BF16), dob, preferred_element_type=F32)
                dk2 = jnp.dot(dst, q, preferred_element_type=F32)
                dkv_acc[ks, :NOPE] += dk2[:, :NOPE]
                dkr_ref[0, ks, :] += dk2[:, NOPE:]
                dqt[...] += jnp.dot(kt[kj], dst, preferred_element_type=F32)
                return carry

            lax.fori_loop(0, qi + 1, step, 0)
            dq = dqt[...].T
            dq_ref[0, rows, :NOPE] = dq[:, :NOPE].astype(dq_ref.dtype)
            dq_ref[0, rows, NOPE:] = _rope_tile_bwd(dq[:, NOPE:], ct_ref[rows, :], s1_ref[rows, :],
                                                    s2_ref[rows, :]).astype(dq_ref.dtype)

        dkv_ref[0] = dkv_acc[...].astype(dkv_ref.dtype)

    head = lambda w: pl.BlockSpec((1, S, w), lambda b, h: (b, 0, h))
    shared = pl.BlockSpec((1, S, LANE), lambda b, h: (b, 0, 0))
    blk = (2 * _nbytes((S, 256), BF16) + 2 * _nbytes((S, LANE), BF16) + _nbytes((S, LANE), F32)
           + 2 * _nbytes((S, 256), F32) + _nbytes((S, LANE), F32))
    scr = 3 * _nbytes((S, 256), BF16) + 14 * _nbytes((QB, QB), F32)
    return pl.pallas_call(
        body, name=name, grid=(B, HB),
        in_specs=[head(256), head(256), shared, head(LANE), head(LANE),
                  pl.BlockSpec((1, 1, 1, S), lambda b, h: (b, h, 0, 0))]
        + [pl.BlockSpec((S, LANE), lambda b, h: (0, 0))] * 3,
        out_specs=[head(256), head(256), shared],
        out_shape=[jax.ShapeDtypeStruct((B, S, W), BF16), jax.ShapeDtypeStruct((B, S, W), BF16),
                   jax.ShapeDtypeStruct((B, S, LANE), F32)],
        scratch_shapes=[pltpu.VMEM((S, 256), BF16), pltpu.VMEM((nq, 256, QB), BF16),
                        pltpu.VMEM((nq, NOPE, QB), BF16), pltpu.VMEM((nq, 1, QB), F32),
                        pltpu.VMEM((256, QB), F32), pltpu.VMEM((2, QB, QB), F32), pltpu.VMEM((2, QB, QB), F32),
                        pltpu.VMEM((QB, QB), F32), pltpu.VMEM((S, 256), F32)],
        compiler_params=_params(blk, scr),
    )(qf, kv, kr, do, o, lse, *tabs)


GROUP_STEPS = 4


def cast_group(name, ws, layers, idx, after=None):
    n = len(ws)
    n_in = n + (after is not None)

    def body(k_ref, *refs):
        for i in range(n):
            refs[n_in + i][...] = refs[i][...].astype(BF16)

    def spec_in(w, layer):
        return pl.BlockSpec((None, w.shape[1] // GROUP_STEPS, w.shape[2]), lambda r, k_ref: (layer, r, 0))

    def spec_out(w):
        return pl.BlockSpec((None, w.shape[1] // GROUP_STEPS, w.shape[2]), lambda r, k_ref: (k_ref[0], r, 0))

    return pl.pallas_call(
        body, name=name,
        grid_spec=pltpu.PrefetchScalarGridSpec(
            num_scalar_prefetch=1, grid=(GROUP_STEPS,),
            in_specs=([spec_in(w, l) for w, l in zip(ws, layers)]
                      + [pl.BlockSpec(memory_space=pl.ANY)] * (after is not None)),
            out_specs=[spec_out(w) for w in ws]),
        out_shape=[jax.ShapeDtypeStruct((N_CHIPS, *w.shape[1:]), BF16) for w in ws],
        compiler_params=_params(sum(_nbytes(w.shape[1:], F32) * 3 // 2 for w in ws) // GROUP_STEPS),
    )(idx, *ws, *([] if after is None else [after]))


def adamw(name, w, g, m, v):
    R, C = w.shape
    tr = _tile(R, max(8, (1 << 18) // C // 8 * 8), 8)
    c1 = 1.0 - ADAM_B1 ** ADAM_STEP
    c2 = 1.0 - ADAM_B2 ** ADAM_STEP

    def body(w_ref, g_ref, m_ref, v_ref, d_ref, mo_ref, vo_ref):
        gv = g_ref[...]
        mn = ADAM_B1 * m_ref[...] + (1.0 - ADAM_B1) * gv
        vn = ADAM_B2 * v_ref[...] + (1.0 - ADAM_B2) * (gv * gv)
        mo_ref[...] = mn
        vo_ref[...] = vn
        d_ref[...] = -ADAM_LR * ((mn / c1) / (jnp.sqrt(vn / c2) + ADAM_EPS) + ADAM_WD * w_ref[...])

    spec = pl.BlockSpec((tr, C), lambda r: (r, 0))
    return pl.pallas_call(
        body, name=name, grid=(R // tr,), in_specs=[spec] * 4, out_specs=[spec] * 3,
        out_shape=[jax.ShapeDtypeStruct((R, C), F32)] * 3,
        compiler_params=_params(7 * _nbytes((tr, C), F32), 4 * _nbytes((tr, C), F32)),
    )(w, g, m, v)


def half_sum_group(name, dws, landed, idx):
    n = len(dws)
    steps = GROUP_STEPS // 2

    def body(i_ref, *refs):
        for i in range(n):
            refs[2 * n + i][...] = (refs[i][...].astype(F32) + refs[n + i][...].astype(F32)).astype(BF16)

    def own(d):
        return pl.BlockSpec((None, None, d.shape[2] // steps, d.shape[3]), lambda k, r, i_ref: (k, i_ref[1], r, 0))

    def flat(d):
        return pl.BlockSpec((None, d.shape[2] // steps, d.shape[3]), lambda k, r, i_ref: (k, r, 0))

    return pl.pallas_call(
        body, name=name,
        grid_spec=pltpu.PrefetchScalarGridSpec(
            num_scalar_prefetch=1, grid=(N_CHIPS, steps),
            in_specs=[own(d) for d in dws] + [flat(d) for d in dws], out_specs=[flat(d) for d in dws]),
        out_shape=[jax.ShapeDtypeStruct((N_CHIPS, *d.shape[2:]), BF16) for d in dws],
        compiler_params=_params(sum(3 * _nbytes(d.shape[2:], BF16) for d in dws) // steps),
    )(idx, *dws, *landed)


def chip_sum_group(name, parts, landed, gbufs, layers, idx):
    n = len(parts)
    steps = GROUP_STEPS // 2

    def body(i_ref, *refs):
        for i in range(n):
            a, b = refs[i], refs[n + i]
            refs[3 * n + i][...] = ((a[...].astype(F32) + b[0].astype(F32)) + b[1].astype(F32)) + b[2].astype(F32)

    def mine(p):
        return pl.BlockSpec((None, p.shape[1] // steps, p.shape[2]), lambda r, i_ref: (i_ref[0], r, 0))

    def three(p):
        return pl.BlockSpec((3, p.shape[1] // steps, p.shape[2]), lambda r, i_ref: (0, r, 0))

    def out(p, layer):
        return pl.BlockSpec((None, None, p.shape[1] // steps, p.shape[2]), lambda r, i_ref: (layer, i_ref[1], r, 0))

    return pl.pallas_call(
        body, name=name,
        grid_spec=pltpu.PrefetchScalarGridSpec(
            num_scalar_prefetch=1, grid=(steps,),
            in_specs=[mine(p) for p in parts] + [three(p) for p in parts] + [pl.BlockSpec(memory_space=pl.ANY)] * n,
            out_specs=[out(p, l) for p, l in zip(parts, layers)]),
        out_shape=[jax.ShapeDtypeStruct(g.shape, F32) for g in gbufs],
        input_output_aliases={1 + 2 * n + i: i for i in range(n)},
        compiler_params=_params(sum(6 * _nbytes(p.shape[1:], BF16) for p in parts) // steps),
    )(idx, *parts, *landed, *gbufs)


ANY = pl.BlockSpec(memory_space=pl.ANY)


def _place():
    x, y, c = lax.axis_index("x"), lax.axis_index("y"), lax.axis_index("c")
    chips = [(1 - x, y), (x, 1 - y), (1 - x, 1 - y)]
    return x, y, c, chips


HBM = pl.BlockSpec(memory_space=pltpu.HBM)
SEM = pl.BlockSpec(memory_space=pltpu.SEMAPHORE)
EFFECT = pltpu.SideEffectType.DATAFLOW_SIDE_EFFECTING


def _in_hbm(a):
    return pltpu.with_memory_space_constraint(a, pltpu.HBM)


def _ici_copy(src, dst, send_sems, recv_sems, k, peer):
    return pltpu.make_async_remote_copy(src_ref=src, dst_ref=dst, send_sem=send_sems.at[k], recv_sem=recv_sems.at[k],
                                        device_id=peer, device_id_type=MESH)


def ici_start(name, bufs, lands, after, gather):
    n, nl = len(bufs), len(lands)

    def body(*refs):
        b_in = refs[:n]
        send_sems, recv_sems = refs[n + nl + 1], refs[n + nl + 2]
        b_out = refs[n + nl + 3:2 * n + nl + 3]
        l_out = refs[2 * n + nl + 3:2 * n + 2 * nl + 3]
        token = refs[-1]
        x, y, c, chips = _place()
        kme = 2 * x + y
        for i in range(n):
            for j in range(3):
                peer = (*chips[j], c)
                if gather:
                    _ici_copy(b_out[i].at[kme, c], b_out[i].at[kme, c], send_sems, recv_sems, 3 * i + j, peer).start()
                else:
                    kd = 2 * chips[j][0] + chips[j][1]
                    _ici_copy(b_out[i].at[kd], l_out[i].at[j], send_sems, recv_sems, 3 * i + j, peer).start()
        token[...] = jnp.zeros_like(token)

    arrays = [*bufs, *lands]
    outs = pl.pallas_call(
        body, name=name,
        in_specs=[HBM] * (n + nl) + [ANY],
        out_specs=(SEM, SEM, *[HBM] * (n + nl), pl.BlockSpec(memory_space=pltpu.VMEM)),
        out_shape=(pltpu.SemaphoreType.DMA((3 * n,)), pltpu.SemaphoreType.DMA((3 * n,)),
                   *[pltpu.HBM(a.shape, a.dtype) for a in arrays], jax.ShapeDtypeStruct((8, LANE), F32)),
        input_output_aliases={i: 2 + i for i in range(n + nl)},
        compiler_params=pltpu.CompilerParams(has_side_effects=EFFECT),
    )(*[_in_hbm(a) for a in arrays], after)
    return outs[0], outs[1], list(outs[2:2 + n]), list(outs[2 + n:2 + n + nl]), outs[-1]


def ici_wait(name, send_sems, recv_sems, bufs, lands, after, gather):
    n, nl = len(bufs), len(lands)

    def body(*refs):
        b_in, l_in = refs[:n], refs[n:n + nl]
        send_sems, recv_sems = refs[n + nl], refs[n + nl + 1]
        x, y, c, chips = _place()
        kme = 2 * x + y
        for i in range(n):
            for j in range(3):
                peer = (*chips[j], c)
                kj = 2 * chips[j][0] + chips[j][1]
                if gather:
                    _ici_copy(b_in[i].at[kme, c], b_in[i].at[kme, c], send_sems, recv_sems, 3 * i + j, peer).wait_send()
                    _ici_copy(b_in[i].at[kj, c], b_in[i].at[kj, c], send_sems, recv_sems, 3 * i + j, peer).wait_recv()
                else:
                    _ici_copy(b_in[i].at[kj], l_in[i].at[j], send_sems, recv_sems, 3 * i + j, peer).wait_send()
                    _ici_copy(b_in[i].at[kj], l_in[i].at[j], send_sems, recv_sems, 3 * i + j, peer).wait_recv()

    arrays = [*bufs, *lands]
    outs = pl.pallas_call(
        body, name=name,
        in_specs=[HBM] * (n + nl) + [SEM, SEM, ANY],
        out_specs=tuple([HBM] * (n + nl)),
        out_shape=tuple(pltpu.HBM(a.shape, a.dtype) for a in arrays),
        input_output_aliases={i: i for i in range(n + nl)},
        compiler_params=pltpu.CompilerParams(has_side_effects=EFFECT),
    )(*arrays, send_sems, recv_sems, after)
    return list(outs[:n]), list(outs[n:])


def gather_pair_pass(name, bufs):
    n = len(bufs)

    def body(*refs):
        b = refs[n:2 * n]
        send_sems, recv_sems = refs[2 * n:]
        x, y, c, chips = _place()
        sib = (x, y, 1 - c)

        def d2d(i, j, which):
            kj = 2 * chips[j][0] + chips[j][1]
            return _ici_copy(b[i].at[kj, which], b[i].at[kj, which], send_sems, recv_sems, 3 * i + j, sib)

        for i in range(n):
            for j in range(3):
                d2d(i, j, c).start()
        for i in range(n):
            for j in range(3):
                d2d(i, j, 1 - c).wait_recv()
        for i in range(n):
            for j in range(3):
                d2d(i, j, c).wait_send()

    return pl.pallas_call(
        body, name=name, in_specs=[ANY] * n, out_specs=[ANY] * n,
        out_shape=[jax.ShapeDtypeStruct(a.shape, a.dtype) for a in bufs],
        input_output_aliases={i: i for i in range(n)},
        scratch_shapes=[pltpu.SemaphoreType.DMA((3 * n,)), pltpu.SemaphoreType.DMA((3 * n,))],
    )(*bufs)


def pair_exchange(name, dws):
    n = len(dws)

    def body(*refs):
        ins, outs = refs[:n], refs[n:2 * n]
        send_sems, recv_sems = refs[2 * n:]
        x, y, c, _ = _place()
        copies = []
        for i in range(n):
            copies.append(pltpu.make_async_remote_copy(
                src_ref=ins[i].at[:, 1 - c], dst_ref=outs[i],
                send_sem=send_sems.at[i], recv_sem=recv_sems.at[i],
                device_id=(x, y, 1 - c), device_id_type=MESH))
            copies[i].start()
        for cp in copies:
            cp.wait_recv()
        for cp in copies:
            cp.wait_send()

    return pl.pallas_call(
        body, name=name, in_specs=[ANY] * n, out_specs=[ANY] * n,
        out_shape=[jax.ShapeDtypeStruct((N_CHIPS, *d.shape[2:]), d.dtype) for d in dws],
        scratch_shapes=[pltpu.SemaphoreType.DMA((n,)), pltpu.SemaphoreType.DMA((n,))],
    )(*dws)


def pair_assemble(gbufs):
    n = len(gbufs)

    def body(*refs):
        bufs = refs[n:2 * n]
        send_sems, recv_sems = refs[2 * n:]
        x, y, c, _ = _place()
        copies = []
        for i in range(n):
            copies.append(pltpu.make_async_remote_copy(
                src_ref=bufs[i].at[:, c], dst_ref=bufs[i].at[:, c],
                send_sem=send_sems.at[i], recv_sem=recv_sems.at[i],
                device_id=(x, y, 1 - c), device_id_type=MESH))
            copies[i].start()
        for i in range(n):
            pltpu.make_async_remote_copy(
                src_ref=bufs[i].at[:, 1 - c], dst_ref=bufs[i].at[:, 1 - c],
                send_sem=send_sems.at[i], recv_sem=recv_sems.at[i],
                device_id=(x, y, 1 - c), device_id_type=MESH).wait_recv()
        for cp in copies:
            cp.wait_send()

    return pl.pallas_call(
        body, name="grad_pair_assemble", in_specs=[ANY] * n, out_specs=[ANY] * n,
        out_shape=[jax.ShapeDtypeStruct(g.shape, g.dtype) for g in gbufs],
        input_output_aliases={i: i for i in range(n)},
        scratch_shapes=[pltpu.SemaphoreType.DMA((n,)), pltpu.SemaphoreType.DMA((n,))],
    )(*gbufs)


def all_reduce_small(vec):
    NR = vec.shape[0]
    flips = [(fx, fy, fc) for fx in (0, 1) for fy in (0, 1) for fc in (0, 1)][1:]

    def body(v_ref, o_ref, buf, send_sems, recv_sems):
        x, y, c, _ = _place()
        me = 4 * x + 2 * y + c
        buf[me] = v_ref[...]
        copies = []
        for j, (fx, fy, fc) in enumerate(flips):
            peer = (1 - x if fx else x, 1 - y if fy else y, 1 - c if fc else c)
            copies.append(pltpu.make_async_remote_copy(
                src_ref=v_ref, dst_ref=buf.at[me], send_sem=send_sems.at[j], recv_sem=recv_sems.at[j],
                device_id=peer, device_id_type=MESH))
            copies[j].start()
        for cp in copies:
            cp.wait_recv()
        for cp in copies:
            cp.wait_send()
        acc = buf[0]
        for d in range(1, 8):
            acc = acc + buf[d]
        o_ref[...] = acc

    return pl.pallas_call(
        body, name="all_reduce_small",
        in_specs=[pl.BlockSpec(memory_space=pltpu.VMEM)], out_specs=pl.BlockSpec(memory_space=pltpu.VMEM),
        out_shape=jax.ShapeDtypeStruct((NR, LANE), F32),
        scratch_shapes=[pltpu.VMEM((8, NR, LANE), F32), pltpu.SemaphoreType.DMA((7,)),
                        pltpu.SemaphoreType.DMA((7,))],
    )(vec)


def _pack(arrays):
    flat = jnp.concatenate([a.reshape(-1).astype(F32) for a in arrays])
    n = flat.shape[0]
    npad = -(-n // (8 * LANE)) * (8 * LANE)
    return jnp.pad(flat, (0, npad - n)).reshape(npad // LANE, LANE)


def _unpack(buf, like):
    flat = buf.reshape(-1)
    out, off = [], 0
    for a in like:
        out.append(flat[off:off + a.size].reshape(a.shape))
        off += a.size
    return out


def kernel(x, ffn1_norm, ffn1_w_in, ffn1_w_out, mix_norm, ffn2_norm, ffn2_w_in, ffn2_w_out, a_w_qkv, a_rel_bias, a_w_o, kv_norm, kv_w_down, kv_latent_norm, kv_w_up, b_w_dq, b_q_norm, b_w_uq, b_w_o, final_norm, loss_target, m_ffn1_norm, m_ffn1_w_in, m_ffn1_w_out, m_mix_norm, m_ffn2_norm, m_ffn2_w_in, m_ffn2_w_out, m_a_w_qkv, m_a_rel_bias, m_a_w_o, m_kv_norm, m_kv_w_down, m_kv_latent_norm, m_kv_w_up, m_b_w_dq, m_b_q_norm, m_b_w_uq, m_b_w_o, m_final_norm, v_ffn1_norm, v_ffn1_w_in, v_ffn1_w_out, v_mix_norm, v_ffn2_norm, v_ffn2_w_in, v_ffn2_w_out, v_a_w_qkv, v_a_rel_bias, v_a_w_o, v_kv_norm, v_kv_w_down, v_kv_latent_norm, v_kv_w_up, v_b_w_dq, v_b_q_norm, v_b_w_uq, v_b_w_o, v_final_norm):
    B, S, D = x.shape
    T = B * S
    HB = D // 128
    QL = b_q_norm.shape[-1]
    KVL = kv_latent_norm.shape[0]
    hpc = HB // N_CHIPS
    tabs = rope_tables(S)
    idx = jnp.stack([2 * lax.axis_index("x") + lax.axis_index("y"), lax.axis_index("c")]).astype(I32)

    def halves(a):
        return a.reshape(*a.shape[:-2], 2, a.shape[-2] // 2, a.shape[-1])

    def whole(a):
        return a.reshape(*a.shape[:-3], 2 * a.shape[-2], a.shape[-1])

    kv_w_down_p = jnp.pad(kv_w_down, ((0, 0), (0, LANE - ROPE)))[None]
    b_w_uq_p = jnp.pad(b_w_uq.reshape(1, QL, hpc, NOPE + ROPE),
                       ((0, 0), (0, 0), (0, 0), (0, LANE - ROPE))).reshape(1, QL, hpc * 256)
    sharded = [("ffn1_w_in", ffn1_w_in), ("ffn1_w_out", ffn1_w_out), ("ffn2_w_in", ffn2_w_in),
               ("ffn2_w_out", ffn2_w_out), ("a_w_qkv", a_w_qkv), ("a_w_o", a_w_o),
               ("kv_w_down", kv_w_down_p), ("kv_w_up", kv_w_up[None]), ("b_w_dq", b_w_dq),
               ("b_w_uq", b_w_uq_p), ("b_w_o", b_w_o)]
    names = [nm for nm, _ in sharded]
    shard_of = dict(sharded)
    W = {}

    gather_groups = [
        [("ffn1_w_in", 0)],
        [("ffn1_w_out", 0)],
        [("a_w_qkv", 0), ("a_w_o", 0)],
        [("ffn2_w_in", 0), ("ffn2_w_out", 0), ("kv_w_down", 0), ("kv_w_up", 0)],
        [("ffn1_w_in", 1), ("ffn1_w_out", 1), ("b_w_dq", 0), ("b_w_uq", 0), ("b_w_o", 0), ("ffn2_w_in", 1),
         ("ffn2_w_out", 1)]]

    own = {}

    def cast(g, after=None):
        keys = gather_groups[g]
        own.update(zip(keys, cast_group(f"cast_group_{g}", [shard_of[nm] for nm, _ in keys], [l for _, l in keys],
                                        idx, after=after)))

    def gather_start(g, after):
        keys = gather_groups[g]
        ss, rs, bufs, _, token = ici_start(f"gather_start_{g}", [halves(own[k]) for k in keys], [], after, True)
        return (g, ss, rs, bufs), token

    def gather_finish(state, after):
        g, ss, rs, bufs = state
        bufs, _ = ici_wait(f"gather_wait_{g}", ss, rs, bufs, [], after, True)
        full = gather_pair_pass(f"gather_pair_{g}", bufs)
        for k, w in zip(gather_groups[g], full):
            W[k] = whole(w)
        return full[0]

    def tied(a, token):
        return a + token[0, 0]

    def col(nm, l=0):
        return W[(nm, l)]

    def row(nm, l=0):
        w = W[(nm, l)]
        return w.reshape(N_CHIPS * w.shape[1], w.shape[2])

    bias = rel_bias_tile("rel_bias_tile", a_rel_bias[0])

    def ffn_fwd(tag, h, g, w_in, w_out):
        xn = rms_fwd(f"{tag}_norm", h, g)
        u, act = ffn_in_act(f"{tag}_in", xn, w_in)
        return mm_roww(f"{tag}_out", act, w_out, F32, res=h, alpha=0.5), (xn, u, act)

    h0 = x.reshape(T, D)
    for g in range(3):
        cast(g)
    st0, tok0 = gather_start(0, h0)
    st1, tok1 = gather_start(1, tok0)
    st2, tok2 = gather_start(2, tok1)
    for g in range(3, len(gather_groups)):
        cast(g, tok2)
    xn0 = rms_fwd("l0f1_norm", h0, tied(ffn1_norm[0], tok2))
    gather_finish(st0, xn0)
    u0, act0 = ffn_in_act("l0f1_in", xn0, col("ffn1_w_in", 0))
    gather_finish(st1, u0)
    h1 = mm_roww("l0f1_out", act0, row("ffn1_w_out", 0), F32, res=h0, alpha=0.5)
    sv_f1a = (xn0, u0, act0)
    done2 = gather_finish(st2, h1)
    st3, tok3 = gather_start(3, done2)
    st4, tok4 = gather_start(4, tok3)
    hn_a = rms_fwd("l0mix_norm", h1, tied(mix_norm[0], tok4))
    qkv = mm_colw("l0_qkv", hn_a, col("a_w_qkv"), BF16).reshape(B, S, 3 * D)
    o_a = attn_a_fwd("l0_attn", qkv, bias).reshape(T, D)
    h2 = mm_roww("l0_attn_out", o_a, row("a_w_o"), F32, res=h1)
    gather_finish(st3, h2)
    h3, sv_f2a = ffn_fwd("l0f2", h2, ffn2_norm[0], col("ffn2_w_in", 0), row("ffn2_w_out", 0))

    hkv = rms_fwd("kv_norm", h3, kv_norm)
    ckr = mm_roww("kv_down", hkv, row("kv_w_down"), F32)
    ckv, kr = kvprep_fwd("kv_prep", ckr, kv_latent_norm, tabs, B, S)
    kvb = mm_colw("kv_up", ckv, col("kv_w_up"), BF16).reshape(B, S, HB * 256)
    gather_finish(st4, kvb)

    h4, sv_f1b = ffn_fwd("l1f1", h3, ffn1_norm[1], col("ffn1_w_in", 1), row("ffn1_w_out", 1))
    hn_b = rms_fwd("l1mix_norm", h4, mix_norm[1])
    cqp = mm_roww("l1_dq", hn_b, row("b_w_dq"), F32)
    cq = rms_fwd("l1_q_norm", cqp, b_q_norm[0])
    qf = uq_rope("l1_uq", cq, col("b_w_uq"), tabs, S).reshape(B, S, HB * 256)
    o_b, lse = mla_fwd("l1_attn", qf, kvb, kr)
    h5 = mm_roww("l1_attn_out", o_b.reshape(T, HB * LANE), row("b_w_o"), F32, res=h4)
    h6, sv_f2b = ffn_fwd("l1f2", h5, ffn2_norm[1], col("ffn2_w_in", 1), row("ffn2_w_out", 1))

    dh, g_final, loss_part = loss_head("loss_head", h6, final_norm, loss_target.reshape(T, D))

    gw = {}
    gbufs = {nm: lax.empty(halves(w).shape, F32) for nm, w in sharded}

    def reduce_start(r, keys, after):
        dws = [halves(gw[k]) for k in keys]
        landed = pair_exchange(f"grad_pair_exchange_{r}", dws)
        parts = half_sum_group(f"half_sum_{r}", dws, landed, idx)
        lands = [lax.empty((3, *p.shape[1:]), p.dtype) for p in parts]
        ss, rs, parts, lands, token = ici_start(f"reduce_start_{r}", parts, lands, after, False)
        return (r, keys, ss, rs, parts, lands), token

    def reduce_finish(state, after):
        r, keys, ss, rs, parts, lands = state
        parts, lands = ici_wait(f"reduce_wait_{r}", ss, rs, parts, lands, after, False)
        done = chip_sum_group(f"chip_sum_{r}", parts, lands, [gbufs[nm] for nm, _ in keys], [l for _, l in keys], idx)
        gbufs.update(zip([nm for nm, _ in keys], done))
        return done[0]

    def ffn_bwd(tag, dh, h_in, g, w_in, w_out, saved, key_in, key_out, after=None, then=None):
        xn, u, act = saved
        du = ffn_dact(f"{tag}_dact", dh, w_out, u, after=after)
        dwo = mm_droww(f"{tag}_dwout", act, dh, alpha=0.5)
        gw[key_out] = dwo.reshape(N_CHIPS, dwo.shape[0] // N_CHIPS, dwo.shape[1])
        gw[key_in] = mm_dcolw(f"{tag}_dwin", xn, du, pair_layout=True)
        token = then(du) if then is not None else None
        return dx_norm_bwd(f"{tag}_dxn", du, w_in, h_in, g, dres=dh, pair_layout=True, after=token)

    def chip_major(dw):
        return dw.reshape(N_CHIPS, dw.shape[0] // N_CHIPS, dw.shape[1])

    dh, g_f2b = ffn_bwd("l1f2b", dh, h5, ffn2_norm[1], col("ffn2_w_in", 1), row("ffn2_w_out", 1), sv_f2b,
                        ("ffn2_w_in", 1), ("ffn2_w_out", 1))
    red0, rtok0 = reduce_start(0, [("ffn2_w_in", 1), ("ffn2_w_out", 1)], dh)
    do_b = mm_roww_t("l1_attn_do", dh, row("b_w_o"), BF16, after=rtok0).reshape(B, S, HB * LANE)
    gw[("b_w_o", 0)] = chip_major(mm_droww("l1_attn_dwo", o_b.reshape(T, HB * LANE), dh))
    dqpre, dkv, dkr = mla_bwd("l1_attn_bwd", qf, kvb, kr, do_b, o_b, lse, tabs)
    dqpre = dqpre.reshape(T, HB * 256)
    gw[("b_w_uq", 0)] = mm_dcolw("l1_dwuq", cq, dqpre)
    dcqp, g_qn = dx_norm_bwd("l1_dcq", dqpre, col("b_w_uq"), cqp, b_q_norm[0])
    gw[("b_w_dq", 0)] = chip_major(mm_droww("l1_dwdq", hn_b, dcqp))
    dhn = mm_roww_t("l1_dhn", dcqp, row("b_w_dq"), F32)
    dh, g_mixb = rms_bwd("l1_dmix", h4, mix_norm[1], dhn, dres=dh)
    dh, g_f1b = ffn_bwd("l1f1b", dh, h3, ffn1_norm[1], col("ffn1_w_in", 1), row("ffn1_w_out", 1), sv_f1b,
                        ("ffn1_w_in", 1), ("ffn1_w_out", 1))
    fin0 = reduce_finish(red0, dh)
    red1, rtok1 = reduce_start(1, [("b_w_o", 0), ("b_w_uq", 0), ("b_w_dq", 0), ("ffn1_w_in", 1), ("ffn1_w_out", 1)], fin0)
    dkv2 = dkv.reshape(T, HB * 256)
    gw[("kv_w_up", 0)] = mm_dcolw("kv_dwup", ckv, dkv2, after=rtok1)
    dckv = mm_colw_t("kv_dckv", dkv2, col("kv_w_up"), F32, after=rtok1)
    dckr, g_lat = kvprep_bwd("kv_prep_bwd", ckr, kv_latent_norm, dckv, dkr, tabs, B, S)
    gw[("kv_w_down", 0)] = chip_major(mm_droww("kv_dwdown", hkv, dckr))
    dhkv = mm_roww_t("kv_dhkv", dckr, row("kv_w_down"), F32)
    dh, g_kvn = rms_bwd("kv_dnorm", h3, kv_norm, dhkv, dres=dh)
    dh, g_f2a = ffn_bwd("l0f2b", dh, h2, ffn2_norm[0], col("ffn2_w_in", 0), row("ffn2_w_out", 0), sv_f2a,
                        ("ffn2_w_in", 0), ("ffn2_w_out", 0))
    do_a = mm_roww_t("l0_attn_do", dh, row("a_w_o"), BF16).reshape(B, S, D)
    gw[("a_w_o", 0)] = chip_major(mm_droww("l0_attn_dwo", o_a, dh))
    dqkv, dbias = attn_a_bwd("l0_attn_bwd", qkv, do_a, bias)
    dqkv = dqkv.reshape(T, 3 * D)
    gw[("a_w_qkv", 0)] = mm_dcolw("l0_dwqkv", hn_a, dqkv)
    dh, g_mixa = dx_norm_bwd("l0_dhn", dqkv, col("a_w_qkv"), h1, mix_norm[0], dres=dh)
    fin1 = reduce_finish(red1, dh)
    red2, rtok2 = reduce_start(2, [("kv_w_up", 0), ("kv_w_down", 0), ("ffn2_w_in", 0), ("ffn2_w_out", 0),
                                   ("a_w_o", 0), ("a_w_qkv", 0)], fin1)
    last = {}

    def last_group(du):
        fin2 = reduce_finish(red2, gw[("ffn1_w_in", 0)])
        last["red"], token = reduce_start(3, [("ffn1_w_in", 0), ("ffn1_w_out", 0)], fin2)
        return token

    dh, g_f1a = ffn_bwd("l0f1b", dh, h0, ffn1_norm[0], col("ffn1_w_in", 0), row("ffn1_w_out", 0), sv_f1a,
                        ("ffn1_w_in", 0), ("ffn1_w_out", 0), after=rtok2, then=last_group)
    grad_x = dh.reshape(B, S, D)
    g_rel = rel_bias_grad("rel_bias_grad", dbias)[:, :2 * MAX_REL + 1][None]
    reduce_finish(last["red"], dh)

    full = [whole(g) for g in pair_assemble([gbufs[nm] for nm in names])]
    G = {nm: g for (nm, _), g in zip(sharded, full)}
    G["kv_w_down"] = G["kv_w_down"][0, :, :KVL + ROPE]
    G["kv_w_up"] = G["kv_w_up"][0]
    G["b_w_uq"] = G["b_w_uq"].reshape(1, QL, hpc, 256)[..., :NOPE + ROPE].reshape(b_w_uq.shape)

    small = [("ffn1_norm", jnp.stack([g_f1a, g_f1b])), ("mix_norm", jnp.stack([g_mixa, g_mixb])),
             ("ffn2_norm", jnp.stack([g_f2a, g_f2b])), ("a_rel_bias", g_rel), ("kv_norm", g_kvn),
             ("kv_latent_norm", g_lat), ("b_q_norm", g_qn[None]), ("final_norm", g_final)]
    red = all_reduce_small(_pack([loss_part] + [g for _, g in small]))
    unpacked = _unpack(red, [loss_part] + [g for _, g in small])
    loss = unpacked[0][0, 0]
    for (nm, _), g in zip(small, unpacked[1:]):
        G[nm] = g

    given = dict(ffn1_norm=(ffn1_norm, m_ffn1_norm, v_ffn1_norm), ffn1_w_in=(ffn1_w_in, m_ffn1_w_in, v_ffn1_w_in),
                 ffn1_w_out=(ffn1_w_out, m_ffn1_w_out, v_ffn1_w_out), mix_norm=(mix_norm, m_mix_norm, v_mix_norm),
                 ffn2_norm=(ffn2_norm, m_ffn2_norm, v_ffn2_norm), ffn2_w_in=(ffn2_w_in, m_ffn2_w_in, v_ffn2_w_in),
                 ffn2_w_out=(ffn2_w_out, m_ffn2_w_out, v_ffn2_w_out), a_w_qkv=(a_w_qkv, m_a_w_qkv, v_a_w_qkv),
                 a_rel_bias=(a_rel_bias, m_a_rel_bias, v_a_rel_bias), a_w_o=(a_w_o, m_a_w_o, v_a_w_o),
                 kv_norm=(kv_norm, m_kv_norm, v_kv_norm), kv_w_down=(kv_w_down, m_kv_w_down, v_kv_w_down),
                 kv_latent_norm=(kv_latent_norm, m_kv_latent_norm, v_kv_latent_norm),
                 kv_w_up=(kv_w_up, m_kv_w_up, v_kv_w_up), b_w_dq=(b_w_dq, m_b_w_dq, v_b_w_dq),
                 b_q_norm=(b_q_norm, m_b_q_norm, v_b_q_norm), b_w_uq=(b_w_uq, m_b_w_uq, v_b_w_uq),
                 b_w_o=(b_w_o, m_b_w_o, v_b_w_o), final_norm=(final_norm, m_final_norm, v_final_norm))
    order = list(given)
    delta, new_m, new_v = {}, {}, {}
    small_names = [nm for nm, _ in small]
    packed = [_pack([given[nm][k] for nm in small_names]) for k in range(3)]
    outs = adamw("adamw_small", packed[0], _pack([G[nm] for nm in small_names]), packed[1], packed[2])
    for dst, buf in zip((delta, new_m, new_v), outs):
        for nm, a in zip(small_names, _unpack(buf, [given[nm][0] for nm in small_names])):
            dst[nm] = a
    for nm, _ in sharded:
        w, m, v = given[nm]
        g = G[nm].reshape(w.shape)
        G[nm] = g
        two = lambda a: a.reshape(-1, a.shape[-1])
        d_, m_, v_ = adamw(f"adamw_{nm}", two(w), two(g), two(m), two(v))
        delta[nm], new_m[nm], new_v[nm] = d_.reshape(w.shape), m_.reshape(w.shape), v_.reshape(w.shape)

    return (loss, grad_x, *[G[n] for n in order], *[delta[n] for n in order],
            *[new_m[n] for n in order], *[new_v[n] for n in order])
```

```python
import math

import jax
import jax.numpy as jnp
from jax import lax
from jax.experimental import pallas as pl
from jax.experimental.pallas import tpu as pltpu

F32 = jnp.float32
BF16 = jnp.bfloat16
I32 = jnp.int32

CHUNK = 64
CHUNK_SHIFT = 6
HEAD_DIM_A = 64
LEFT_CHUNKS = 8
MAX_REL = 128
REL_PAD = 384
QROWS = 2 * CHUNK
WIN = (LEFT_CHUNKS + 2) * CHUNK
PADR = LEFT_CHUNKS * CHUNK
NOPE = 128
ROPE = 64
EPS = 1e-6
NEG_INF = -1e30
ROPE_THETA = 10000.0
ADAM_LR, ADAM_B1, ADAM_B2, ADAM_EPS, ADAM_WD, ADAM_STEP = 0.001, 0.9, 0.999, 1e-08, 0.01, 10
N_CHIPS = 4
LANE = 128
MESH = pl.DeviceIdType.MESH
VMEM_CAP_MB = 60

NN = (((1,), (0,)), ((), ()))
NT = (((1,), (1,)), ((), ()))
TN = (((0,), (0,)), ((), ()))


def _tile(n, pref, mult):
    t = (min(pref, n) // mult) * mult
    while t >= mult:
        if n % t == 0:
            return t
        t -= mult
    return n


def _nbytes(shape, dtype):
    return math.prod(shape) * jnp.dtype(dtype).itemsize


def _params(block_bytes, extra_bytes=0):
    need = 2 * block_bytes + extra_bytes
    mb = min(VMEM_CAP_MB, max(16, int(need * 1.25 / 2**20) + 8))
    return pltpu.CompilerParams(vmem_limit_bytes=mb * 2**20)


def _mm(name, kind, a, b, grid, a_spec, b_spec, o_spec, out_shape, out_dtype, blocks,
        red_axis=None, nred=1, alpha=1.0, res=None, res_spec=None, after=None):
    dims = {"nn": NN, "nt": NT, "tn": TN}[kind]
    has_res = res is not None
    acc_in_out = nred > 1 and out_dtype == F32 and not has_res and alpha == 1.0
    n_in = 2 + has_res + (after is not None)

    def body(*refs):
        a_ref, b_ref = refs[0], refs[1]
        r_ref = refs[2] if has_res else None
        o_ref = refs[n_in]
        p = lax.dot_general(a_ref[...].astype(BF16), b_ref[...].astype(BF16), dims,
                            preferred_element_type=F32)

        def finish(acc):
            y = acc if alpha == 1.0 else acc * alpha
            if has_res:
                y = r_ref[...] + y
            o_ref[...] = y.astype(o_ref.dtype)

        if nred == 1:
            finish(p)
            return
        k = pl.program_id(red_axis)
        tgt = o_ref if acc_in_out else refs[-1]

        @pl.when(k == 0)
        def _():
            tgt[...] = p

        @pl.when(k > 0)
        def _():
            tgt[...] += p

        if not acc_in_out:
            @pl.when(k == nred - 1)
            def _():
                finish(tgt[...])

    a_blk, b_blk, o_blk = blocks
    scratch = []
    extra = 0
    if nred > 1 and not acc_in_out:
        scratch = [pltpu.VMEM(o_blk, F32)]
        extra = _nbytes(o_blk, F32)
    blk = _nbytes(a_blk, a.dtype) + _nbytes(b_blk, b.dtype) + _nbytes(o_blk, out_dtype)
    ins, specs = [a, b], [a_spec, b_spec]
    if has_res:
        ins.append(res)
        specs.append(res_spec)
        blk += _nbytes(o_blk, res.dtype)
    if after is not None:
        ins.append(after)
        specs.append(pl.BlockSpec(memory_space=pl.ANY))
    extra += _nbytes(a_blk, BF16) + _nbytes(b_blk, BF16) + 2 * _nbytes(o_blk, F32)
    return pl.pallas_call(
        body, name=name, grid=grid, in_specs=specs, out_specs=o_spec,
        out_shape=jax.ShapeDtypeStruct(out_shape, out_dtype), scratch_shapes=scratch,
        compiler_params=_params(blk, extra),
    )(*ins)


def mm_colw(name, x, w3, out_dtype):
    T, K = x.shape
    _, _, nl = w3.shape
    tm = _tile(T, 512, 8)
    return _mm(name, "nn", x, w3, (N_CHIPS, T // tm),
               pl.BlockSpec((tm, K), lambda j, i: (i, 0)),
               pl.BlockSpec((None, K, nl), lambda j, i: (j, 0, 0)),
               pl.BlockSpec((tm, nl), lambda j, i: (i, j)),
               (T, N_CHIPS * nl), out_dtype, ((tm, K), (K, nl), (tm, nl)))


def _pair_chip(j):
    return (j % 2) * 2 + j // 2


def mm_colw_t(name, dy, w3, out_dtype, res=None, after=None, pair_layout=False):
    T = dy.shape[0]
    _, K, nl = w3.shape
    tm = _tile(T, 1024, 8)
    chip = _pair_chip if pair_layout else (lambda j: j)
    return _mm(name, "nt", dy, w3, (T // tm, N_CHIPS),
               pl.BlockSpec((tm, nl), lambda i, j: (i, j)),
               pl.BlockSpec((None, K, nl), lambda i, j: (chip(j), 0, 0)),
               pl.BlockSpec((tm, K), lambda i, j: (i, 0)),
               (T, K), out_dtype, ((tm, nl), (K, nl), (tm, K)),
               red_axis=1, nred=N_CHIPS, res=res,
               res_spec=pl.BlockSpec((tm, K), lambda i, j: (i, 0)), after=after)


def mm_dcolw(name, x, dy, after=None, pair_layout=False):
    T, K = x.shape
    nl = dy.shape[1] // N_CHIPS
    tt = _tile(T, 2048, 8)
    chip = _pair_chip if pair_layout else (lambda j: j)
    return _mm(name, "tn", x, dy, (N_CHIPS, T // tt),
               pl.BlockSpec((tt, K), lambda j, t: (t, 0)),
               pl.BlockSpec((tt, nl), lambda j, t: (t, j)),
               pl.BlockSpec((None, K, nl), lambda j, t: (chip(j), 0, 0)),
               (N_CHIPS, K, nl), BF16, ((tt, K), (tt, nl), (K, nl)),
               red_axis=1, nred=T // tt, after=after)


def mm_roww(name, x, w2, out_dtype, res=None, alpha=1.0):
    T, Kt = x.shape
    N = w2.shape[1]
    tm = _tile(T, 512, 8)
    return _mm(name, "nn", x, w2, (T // tm,),
               pl.BlockSpec((tm, Kt), lambda i: (i, 0)),
               pl.BlockSpec((Kt, N), lambda i: (0, 0)),
               pl.BlockSpec((tm, N), lambda i: (i, 0)),
               (T, N), out_dtype, ((tm, Kt), (Kt, N), (tm, N)),
               alpha=alpha, res=res, res_spec=pl.BlockSpec((tm, N), lambda i: (i, 0)))


def mm_roww_t(name, dy, w2, out_dtype, alpha=1.0, after=None):
    T, N = dy.shape
    Kt = w2.shape[0]
    tm = _tile(T, 512, 8)
    tk = _tile(Kt, 1408, LANE)
    return _mm(name, "nt", dy, w2, (Kt // tk, T // tm),
               pl.BlockSpec((tm, N), lambda j, i: (i, 0)),
               pl.BlockSpec((tk, N), lambda j, i: (j, 0)),
               pl.BlockSpec((tm, tk), lambda j, i: (i, j)),
               (T, Kt), out_dtype, ((tm, N), (tk, N), (tm, tk)), alpha=alpha, after=after)


def mm_droww(name, x, dy, alpha=1.0):
    T, Kt = x.shape
    N = dy.shape[1]
    tt = _tile(T, 2048, 8)
    tk = _tile(Kt, 1408, LANE)
    return _mm(name, "tn", x, dy, (Kt // tk, T // tt),
               pl.BlockSpec((tt, tk), lambda j, t: (t, j)),
               pl.BlockSpec((tt, N), lambda j, t: (t, 0)),
               pl.BlockSpec((tk, N), lambda j, t: (j, 0)),
               (Kt, N), BF16, ((tt, tk), (tt, N), (tk, N)),
               red_axis=1, nred=T // tt, alpha=alpha)


def rms_fwd(name, x, g):
    T, D = x.shape
    tm = _tile(T, 512, 8)

    def body(x_ref, g_ref, o_ref):
        xv = x_ref[...]
        r = lax.rsqrt(jnp.mean(xv * xv, axis=-1, keepdims=True) + EPS)
        o_ref[...] = (xv * r * g_ref[...]).astype(o_ref.dtype)

    return pl.pallas_call(
        body, name=name, grid=(T // tm,),
        in_specs=[pl.BlockSpec((tm, D), lambda i: (i, 0)), pl.BlockSpec((1, D), lambda i: (0, 0))],
        out_specs=pl.BlockSpec((tm, D), lambda i: (i, 0)),
        out_shape=jax.ShapeDtypeStruct((T, D), BF16),
        compiler_params=_params(_nbytes((tm, D), F32) * 2, 4 * _nbytes((tm, D), F32)),
    )(x, g.reshape(1, D))


def _rms_bwd_math(xv, gv, dy):
    r = lax.rsqrt(jnp.mean(xv * xv, axis=-1, keepdims=True) + EPS)
    xh = xv * r
    dyg = dy * gv
    dx = r * (dyg - xh * jnp.mean(dyg * xh, axis=-1, keepdims=True))
    dg = jnp.sum(dy * xh, axis=0, keepdims=True)
    return dx, dg


def rms_bwd(name, x, g, dy, dres=None):
    T, D = x.shape
    tm = _tile(T, 256, 8)
    has_res = dres is not None

    def body(*refs):
        x_ref, g_ref, dy_ref = refs[:3]
        r_ref = refs[3] if has_res else None
        dx_ref, dg_ref = refs[-2:]
        dx, dg = _rms_bwd_math(x_ref[...], g_ref[...], dy_ref[...].astype(F32))
        if has_res:
            dx = r_ref[...] + dx
        dx_ref[...] = dx

        @pl.when(pl.program_id(0) == 0)
        def _():
            dg_ref[...] = dg

        @pl.when(pl.program_id(0) > 0)
        def _():
            dg_ref[...] += dg

    row = pl.BlockSpec((tm, D), lambda i: (i, 0))
    vec = pl.BlockSpec((1, D), lambda i: (0, 0))
    ins, specs = [x, g.reshape(1, D), dy], [row, vec, row]
    if has_res:
        ins.append(dres)
        specs.append(row)
    dx, dg = pl.pallas_call(
        body, name=name, grid=(T // tm,), in_specs=specs, out_specs=[row, vec],
        out_shape=[jax.ShapeDtypeStruct((T, D), F32), jax.ShapeDtypeStruct((1, D), F32)],
        compiler_params=_params(_nbytes((tm, D), F32) * 4, 6 * _nbytes((tm, D), F32)),
    )(*ins)
    return dx, dg.reshape(D)


def dx_norm_bwd(name, dy, w3, x, g, dres=None, pair_layout=False, after=None):
    T = dy.shape[0]
    _, K, nl = w3.shape
    tm = _tile(T, 512, 8)
    chip = _pair_chip if pair_layout else (lambda j: j)
    has_res = dres is not None

    def body(*refs):
        dy_ref, w_ref, x_ref, g_ref = refs[:4]
        r_ref = refs[4] if has_res else None
        dx_ref, dg_ref, acc = refs[-3:]
        i, k = pl.program_id(0), pl.program_id(1)
        p = lax.dot_general(dy_ref[...].astype(BF16), w_ref[...], NT, preferred_element_type=F32)

        @pl.when(k == 0)
        def _():
            acc[...] = p

        @pl.when(k > 0)
        def _():
            acc[...] += p

        @pl.when(k == N_CHIPS - 1)
        def _():
            dx, dg = _rms_bwd_math(x_ref[...], g_ref[...], acc[...])
            dx_ref[...] = r_ref[...] + dx if has_res else dx

            @pl.when(i == 0)
            def _():
                dg_ref[...] = dg

            @pl.when(i > 0)
            def _():
                dg_ref[...] += dg

    row = pl.BlockSpec((tm, K), lambda i, j: (i, 0))
    vec = pl.BlockSpec((1, K), lambda i, j: (0, 0))
    ins = [dy, w3, x, g.reshape(1, K)]
    specs = [pl.BlockSpec((tm, nl), lambda i, j: (i, j)),
             pl.BlockSpec((None, K, nl), lambda i, j: (chip(j), 0, 0)), row, vec]
    if has_res:
        ins.append(dres)
        specs.append(row)
    if after is not None:
        ins.append(after)
        specs.append(pl.BlockSpec(memory_space=pl.ANY))
    blk = _nbytes((tm, nl), dy.dtype) + _nbytes((K, nl), BF16) + (2 + has_res) * _nbytes((tm, K), F32)
    dx, dg = pl.pallas_call(
        body, name=name, grid=(T // tm, N_CHIPS), in_specs=specs, out_specs=[row, vec],
        out_shape=[jax.ShapeDtypeStruct((T, K), F32), jax.ShapeDtypeStruct((1, K), F32)],
        scratch_shapes=[pltpu.VMEM((tm, K), F32)],
        compiler_params=_params(blk, 8 * _nbytes((tm, K), F32)),
    )(*ins)
    return dx, dg.reshape(K)


def ffn_in_act(name, x, w3):
    T, K = x.shape
    _, _, nl = w3.shape
    tm = _tile(T, 512, 8)

    def body(*refs):
        x_ref, wg_ref, wu_ref = refs[:3]
        u_ref, a_ref = refs[-2:]
        xv = x_ref[...]
        g = jnp.dot(xv, wg_ref[...], preferred_element_type=F32)
        up = jnp.dot(xv, wu_ref[...], preferred_element_type=F32)
        u_ref[:, :nl] = g.astype(u_ref.dtype)
        u_ref[:, nl:] = up.astype(u_ref.dtype)
        a_ref[...] = (g * jax.nn.sigmoid(g) * up).astype(a_ref.dtype)

    blk = _nbytes((tm, K), BF16) + 2 * _nbytes((K, nl), BF16) + _nbytes((tm, 3 * nl), BF16)
    return pl.pallas_call(
        body, name=name, grid=(2, T // tm),
        in_specs=[pl.BlockSpec((tm, K), lambda p, i: (i, 0)),
                  pl.BlockSpec((None, K, nl), lambda p, i: (p, 0, 0)),
                  pl.BlockSpec((None, K, nl), lambda p, i: (p + 2, 0, 0))],
        out_specs=[pl.BlockSpec((tm, 2 * nl), lambda p, i: (i, p)), pl.BlockSpec((tm, nl), lambda p, i: (i, p))],
        out_shape=[jax.ShapeDtypeStruct((T, 4 * nl), BF16), jax.ShapeDtypeStruct((T, 2 * nl), BF16)],
        compiler_params=_params(blk, 4 * _nbytes((tm, nl), F32)),
    )(x, w3, w3)


def ffn_dact(name, dh, w_out, u, after=None):
    T, N = dh.shape
    F = w_out.shape[0]
    nl = F // 2
    tm = _tile(T, 512, 8)

    def body(*refs):
        d_ref, w_ref, u_ref = refs[:3]
        o_ref = refs[-1]
        dact = 0.5 * lax.dot_general(d_ref[...].astype(BF16), w_ref[...], NT, preferred_element_type=F32)
        g = u_ref[:, :nl].astype(F32)
        up = u_ref[:, nl:].astype(F32)
        sig = jax.nn.sigmoid(g)
        o_ref[:, :nl] = (dact * up * (sig * (1.0 + g * (1.0 - sig)))).astype(o_ref.dtype)
        o_ref[:, nl:] = (dact * (g * sig)).astype(o_ref.dtype)

    ins = [dh, w_out, u]
    specs = [pl.BlockSpec((tm, N), lambda p, i: (i, 0)), pl.BlockSpec((nl, N), lambda p, i: (p, 0)),
             pl.BlockSpec((tm, 2 * nl), lambda p, i: (i, p))]
    if after is not None:
        ins.append(after)
        specs.append(pl.BlockSpec(memory_space=pl.ANY))
    blk = _nbytes((tm, N), F32) + _nbytes((nl, N), BF16) + 2 * _nbytes((tm, 2 * nl), BF16)
    return pl.pallas_call(
        body, name=name, grid=(2, T // tm), in_specs=specs,
        out_specs=pl.BlockSpec((tm, 2 * nl), lambda p, i: (i, p)),
        out_shape=jax.ShapeDtypeStruct((T, 2 * F), BF16),
        compiler_params=_params(blk, 6 * _nbytes((tm, nl), F32)),
    )(*ins)


def loss_head(name, h, g, target):
    T, D = h.shape
    tm = _tile(T, 256, 8)

    def body(h_ref, g_ref, t_ref, dh_ref, dg_ref, loss_ref):
        xv = h_ref[...]
        gv = g_ref[...]
        r = lax.rsqrt(jnp.mean(xv * xv, axis=-1, keepdims=True) + EPS)
        err = xv * r * gv - t_ref[...]
        part = 0.5 * jnp.sum(jnp.mean(err * err, axis=-1, keepdims=True), axis=0, keepdims=True)
        dx, dg = _rms_bwd_math(xv, gv, err * (1.0 / D))
        dh_ref[...] = dx
        part = jnp.broadcast_to(part, (1, LANE))

        @pl.when(pl.program_id(0) == 0)
        def _():
            dg_ref[...] = dg
            loss_ref[...] = part

        @pl.when(pl.program_id(0) > 0)
        def _():
            dg_ref[...] += dg
            loss_ref[...] += part

    row = pl.BlockSpec((tm, D), lambda i: (i, 0))
    vec = pl.BlockSpec((1, D), lambda i: (0, 0))
    dh, dg, loss = pl.pallas_call(
        body, name=name, grid=(T // tm,), in_specs=[row, vec, row],
        out_specs=[row, vec, pl.BlockSpec((1, LANE), lambda i: (0, 0))],
        out_shape=[jax.ShapeDtypeStruct((T, D), F32), jax.ShapeDtypeStruct((1, D), F32),
                   jax.ShapeDtypeStruct((1, LANE), F32)],
        compiler_params=_params(_nbytes((tm, D), F32) * 3, 6 * _nbytes((tm, D), F32)),
    )(h, g.reshape(1, D), target)
    return dh, dg.reshape(D), loss


def rope_tables(S):
    half = ROPE // 2
    freqs = ROPE_THETA ** (-jnp.arange(half, dtype=F32) / half)
    ang = jnp.arange(S, dtype=F32)[:, None] * freqs[None, :]
    cos, sin = jnp.cos(ang), jnp.sin(ang)
    z = jnp.zeros_like(cos)
    ct = jnp.concatenate([cos, cos, z, z], axis=1)
    s1 = jnp.concatenate([-sin, z, z, z], axis=1)
    s2 = jnp.concatenate([z, sin, z, z], axis=1)
    return ct, s1, s2


def _rope_tile(t, ct, s1, s2):
    return t * ct + pltpu.roll(t, 96, 1) * s1 + pltpu.roll(t, 32, 1) * s2


def _rope_tile_bwd(d, ct, s1, s2):
    return d * ct + pltpu.roll(d * s1, 32, 1) + pltpu.roll(d * s2, 96, 1)


def uq_rope(name, x, w3, tabs, S):
    T, K = x.shape
    _, _, nl = w3.shape
    tm = _tile(S, 512, 8)
    nt = S // tm

    def body(x_ref, w_ref, ct_ref, s1_ref, s2_ref, o_ref):
        q = jnp.dot(x_ref[...], w_ref[...], preferred_element_type=F32)
        ct, s1, s2 = ct_ref[...], s1_ref[...], s2_ref[...]
        for h in range(nl // 256):
            o_ref[:, 256 * h:256 * h + 128] = q[:, 256 * h:256 * h + 128].astype(o_ref.dtype)
            o_ref[:, 256 * h + 128:256 * h + 256] = _rope_tile(q[:, 256 * h + 128:256 * h + 256],
                                                               ct, s1, s2).astype(o_ref.dtype)

    tab = pl.BlockSpec((tm, LANE), lambda j, i: (i % nt, 0))
    blk = _nbytes((tm, K), BF16) + _nbytes((K, nl), BF16) + _nbytes((tm, nl), BF16) + 3 * _nbytes((tm, LANE), F32)
    return pl.pallas_call(
        body, name=name, grid=(N_CHIPS, T // tm),
        in_specs=[pl.BlockSpec((tm, K), lambda j, i: (i, 0)), pl.BlockSpec((None, K, nl), lambda j, i: (j, 0, 0)),
                  tab, tab, tab],
        out_specs=pl.BlockSpec((tm, nl), lambda j, i: (i, j)),
        out_shape=jax.ShapeDtypeStruct((T, N_CHIPS * nl), BF16),
        compiler_params=_params(blk, 4 * _nbytes((tm, nl), F32)),
    )(x, w3, *tabs)


def kvprep_fwd(name, ckr, g, tabs, B, S):
    T, W = ckr.shape
    KVL = W - LANE
    ts = _tile(S, 256, 8)

    def body(x_ref, g_ref, ct_ref, s1_ref, s2_ref, c_ref, k_ref):
        xv = x_ref[0, :, :KVL]
        r = lax.rsqrt(jnp.mean(xv * xv, axis=-1, keepdims=True) + EPS)
        c_ref[0] = (xv * r * g_ref[...]).astype(c_ref.dtype)
        k_ref[0] = _rope_tile(x_ref[0, :, KVL:], ct_ref[...], s1_ref[...], s2_ref[...]).astype(k_ref.dtype)

    tab = pl.BlockSpec((ts, LANE), lambda b, s: (s, 0))
    c, k = pl.pallas_call(
        body, name=name, grid=(B, S // ts),
        in_specs=[pl.BlockSpec((1, ts, W), lambda b, s: (b, s, 0)), pl.BlockSpec((1, KVL), lambda b, s: (0, 0)),
                  tab, tab, tab],
        out_specs=[pl.BlockSpec((1, ts, KVL), lambda b, s: (b, s, 0)),
                   pl.BlockSpec((1, ts, LANE), lambda b, s: (b, s, 0))],
        out_shape=[jax.ShapeDtypeStruct((B, S, KVL), BF16), jax.ShapeDtypeStruct((B, S, LANE), BF16)],
        compiler_params=_params(_nbytes((ts, W), F32) * 2, _nbytes((ts, W), F32) * 2),
    )(ckr.reshape(B, S, W), g.reshape(1, KVL), *tabs)
    return c.reshape(T, KVL), k


def kvprep_bwd(name, ckr, g, dc, dkr, tabs, B, S):
    T, W = ckr.shape
    KVL = W - LANE
    ts = _tile(S, 256, 8)

    def body(x_ref, g_ref, dc_ref, dk_ref, ct_ref, s1_ref, s2_ref, o_ref, dg_ref):
        dx, dg = _rms_bwd_math(x_ref[0, :, :KVL], g_ref[...], dc_ref[0])
        o_ref[0, :, :KVL] = dx
        o_ref[0, :, KVL:] = _rope_tile_bwd(dk_ref[0], ct_ref[...], s1_ref[...], s2_ref[...])
        first = (pl.program_id(0) == 0) & (pl.program_id(1) == 0)

        @pl.when(first)
        def _():
            dg_ref[...] = dg

        @pl.when(jnp.logical_not(first))
        def _():
            dg_ref[...] += dg

    tab = pl.BlockSpec((ts, LANE), lambda b, s: (s, 0))
    vec = pl.BlockSpec((1, KVL), lambda b, s: (0, 0))
    o, dg = pl.pallas_call(
        body, name=name, grid=(B, S // ts),
        in_specs=[pl.BlockSpec((1, ts, W), lambda b, s: (b, s, 0)), vec,
                  pl.BlockSpec((1, ts, KVL), lambda b, s: (b, s, 0)),
                  pl.BlockSpec((1, ts, LANE), lambda b, s: (b, s, 0)), tab, tab, tab],
        out_specs=[pl.BlockSpec((1, ts, W), lambda b, s: (b, s, 0)), vec],
        out_shape=[jax.ShapeDtypeStruct((B, S, W), F32), jax.ShapeDtypeStruct((1, KVL), F32)],
        compiler_params=_params(_nbytes((ts, W), F32) * 4, _nbytes((ts, W), F32) * 4),
    )(ckr.reshape(B, S, W), g.reshape(1, KVL), dc.reshape(B, S, KVL), dkr, *tabs)
    return o.reshape(T, W), dg.reshape(KVL)


DIAGS = 768


def _diag_onehot():
    col = lax.broadcasted_iota(I32, (REL_PAD, DIAGS), 1)
    row = lax.broadcasted_iota(I32, (REL_PAD, DIAGS), 0)
    idx = jnp.clip(PADR + QROWS - 1 - col, -MAX_REL, MAX_REL) + MAX_REL
    return (row == idx).astype(F32)


def rel_bias_tile(name, table):
    H = table.shape[0]
    tpad = jnp.pad(table, ((0, 0), (0, REL_PAD - table.shape[1])))

    def body(t_ref, o_ref):
        g = lax.dot_general(t_ref[...], _diag_onehot(), NN, precision=lax.Precision.HIGHEST,
                            preferred_element_type=F32)
        qc = jnp.right_shift(lax.broadcasted_iota(I32, (QROWS, WIN), 0), CHUNK_SHIFT)
        kc = jnp.right_shift(lax.broadcasted_iota(I32, (QROWS, WIN), 1), CHUNK_SHIFT)
        band = (kc >= qc) & (kc <= qc + LEFT_CHUNKS)
        for h in range(H):
            gb = jnp.broadcast_to(g[h:h + 1, :], (QROWS, DIAGS))
            tile = pltpu.roll(gb, DIAGS - (QROWS - 1), 1, stride=1, stride_axis=0)
            o_ref[h // 2, (h % 2) * QROWS:(h % 2 + 1) * QROWS, :] = jnp.where(band, tile[:, :WIN], NEG_INF)

    return pl.pallas_call(
        body, name=name, out_shape=jax.ShapeDtypeStruct((H // 2, 2 * QROWS, WIN), F32),
        compiler_params=_params(0, 2 * _nbytes((H // 2, 2 * QROWS, WIN), F32)),
    )(tpad)


def rel_bias_grad(name, dbias):
    H = 2 * dbias.shape[0]

    def body(d_ref, o_ref):
        flip = (lax.broadcasted_iota(I32, (QROWS, QROWS), 0) + lax.broadcasted_iota(I32, (QROWS, QROWS), 1)
                == QROWS - 1).astype(F32)
        rows = []
        for h in range(H):
            x = d_ref[h // 2, (h % 2) * QROWS:(h % 2 + 1) * QROWS, :]
            xr = lax.dot_general(flip, x, NN, precision=lax.Precision.HIGHEST, preferred_element_type=F32)
            xp = jnp.concatenate([xr, jnp.zeros((QROWS, DIAGS - WIN), F32)], axis=1)
            y = pltpu.roll(xp, 0, 1, stride=1, stride_axis=0)
            rows.append(jnp.sum(y, axis=0, keepdims=True))
        o_ref[...] = lax.dot_general(jnp.concatenate(rows, axis=0), _diag_onehot(), NT,
                                     precision=lax.Precision.HIGHEST, preferred_element_type=F32)

    return pl.pallas_call(
        body, name=name, out_shape=jax.ShapeDtypeStruct((H, REL_PAD), F32),
        compiler_params=_params(0, 2 * _nbytes(dbias.shape, F32)),
    )(dbias)


def _stack_pair(xp):
    lane = lax.broadcasted_iota(I32, xp.shape, 1)
    z = jnp.zeros_like(xp)
    return jnp.concatenate([jnp.where(lane < HEAD_DIM_A, xp, z), jnp.where(lane >= HEAD_DIM_A, xp, z)], axis=0)


def _unstack_pair(y):
    lane = lax.broadcasted_iota(I32, (QROWS, LANE), 1)
    return jnp.where(lane < HEAD_DIM_A, y[:QROWS], y[QROWS:])


def _attn_a_rowpen(j):
    w = lax.broadcasted_iota(I32, (1, WIN), 1)
    return jnp.where(w >= PADR - QROWS * j, 0.0, NEG_INF).astype(F32)


def _attn_a_load_bias(bias_hbm, bias_v, sem):
    cp = pltpu.make_async_copy(bias_hbm, bias_v, sem)
    cp.start()
    cp.wait()


def _attn_a_load_kv(qkv_hbm, b, kpad, vpad, sem, S, D):
    kpad[0:PADR, :] = jnp.zeros((PADR, D), BF16)
    vpad[0:PADR, :] = jnp.zeros((PADR, D), BF16)
    ck = pltpu.make_async_copy(qkv_hbm.at[b, :, pl.ds(D, D)], kpad.at[pl.ds(PADR, S), :], sem.at[0])
    cv = pltpu.make_async_copy(qkv_hbm.at[b, :, pl.ds(2 * D, D)], vpad.at[pl.ds(PADR, S), :], sem.at[1])
    ck.start()
    cv.start()
    ck.wait()
    cv.wait()


def _attn_a_exp(q2s, kp, bias, pen):
    s = lax.dot_general(q2s, kp, NT, preferred_element_type=F32) + bias + pen
    e = jnp.exp(s - jnp.max(s, axis=-1, keepdims=True))
    return e, 1.0 / jnp.sum(e, axis=-1, keepdims=True)


def attn_a_fwd(name, qkv, bias):
    B, S, D3 = qkv.shape
    D = D3 // 3
    H = D // HEAD_DIM_A
    nb = S // QROWS
    scale = HEAD_DIM_A ** -0.5

    def body(q_ref, bias_hbm, qkv_hbm, o_ref, kpad, vpad, bias_v, sem):
        b, j = pl.program_id(0), pl.program_id(1)

        @pl.when((b == 0) & (j == 0))
        def _():
            _attn_a_load_bias(bias_hbm, bias_v, sem.at[2])

        @pl.when(j == 0)
        def _():
            _attn_a_load_kv(qkv_hbm, b, kpad, vpad, sem, S, D)

        pen = _attn_a_rowpen(j)
        w0 = pl.multiple_of(j * QROWS, QROWS)
        for p in range(H // 2):
            ls = slice(p * LANE, (p + 1) * LANE)
            e, rl = _attn_a_exp(_stack_pair(q_ref[0, :, ls] * scale), kpad[pl.ds(w0, WIN), ls], bias_v[p], pen)
            o2 = jnp.dot(e.astype(BF16), vpad[pl.ds(w0, WIN), ls], preferred_element_type=F32) * rl
            o_ref[0, :, ls] = _unstack_pair(o2).astype(o_ref.dtype)

    scr = 2 * _nbytes((PADR + S, D), BF16) + _nbytes(bias.shape, F32) + 8 * _nbytes((2 * QROWS, WIN), F32)
    return pl.pallas_call(
        body, name=name, grid=(B, nb),
        in_specs=[pl.BlockSpec((1, QROWS, D), lambda b, j: (b, j, 0)),
                  pl.BlockSpec(memory_space=pl.ANY), pl.BlockSpec(memory_space=pl.ANY)],
        out_specs=pl.BlockSpec((1, QROWS, D), lambda b, j: (b, j, 0)),
        out_shape=jax.ShapeDtypeStruct((B, S, D), BF16),
        scratch_shapes=[pltpu.VMEM((PADR + S, D), BF16), pltpu.VMEM((PADR + S, D), BF16),
                        pltpu.VMEM(bias.shape, F32), pltpu.SemaphoreType.DMA((3,))],
        compiler_params=_params(2 * _nbytes((QROWS, D), BF16), scr),
    )(qkv, bias, qkv)


def attn_a_bwd(name, qkv, do, bias):
    B, S, D3 = qkv.shape
    D = D3 // 3
    H = D // HEAD_DIM_A
    nb = S // QROWS
    scale = HEAD_DIM_A ** -0.5

    def body(q_ref, do_ref, bias_hbm, qkv_hbm, dqkv_hbm, dbias_hbm, kpad, vpad, dkacc, dvacc, bias_v, dbias_v,
             dq_stage, sem):
        b, j = pl.program_id(0), pl.program_id(1)
        step = b * nb + j
        slot = lax.rem(step, 2)

        def dq_out(s):
            return pltpu.make_async_copy(dq_stage.at[s], dqkv_hbm.at[b, pl.ds(j * QROWS, QROWS), pl.ds(0, D)],
                                         sem.at[3 + s])

        @pl.when(step >= 2)
        def _():
            dq_out(slot).wait()

        @pl.when((b == 0) & (j == 0))
        def _():
            _attn_a_load_bias(bias_hbm, bias_v, sem.at[2])
            dbias_v[...] = jnp.zeros_like(dbias_v)

        @pl.when(j == 0)
        def _():
            _attn_a_load_kv(qkv_hbm, b, kpad, vpad, sem, S, D)
            dkacc[...] = jnp.zeros_like(dkacc)
            dvacc[...] = jnp.zeros_like(dvacc)

        pen = _attn_a_rowpen(j)
        w0 = pl.multiple_of(j * QROWS, QROWS)
        for p in range(H // 2):
            ls = slice(p * LANE, (p + 1) * LANE)
            q2s = _stack_pair(q_ref[0, :, ls] * scale)
            do2 = _stack_pair(do_ref[0, :, ls])
            kp = kpad[pl.ds(w0, WIN), ls]
            vp = vpad[pl.ds(w0, WIN), ls]
            e, rl = _attn_a_exp(q2s, kp, bias_v[p], pen)
            pr = e * rl
            dp = lax.dot_general(do2, vp, NT, preferred_element_type=F32)
            ds = pr * (dp - jnp.sum(pr * dp, axis=-1, keepdims=True))
            dbias_v[p] += ds
            dsb = ds.astype(BF16)
            dq_stage[slot, :, ls] = (_unstack_pair(jnp.dot(dsb, kp, preferred_element_type=F32))
                                     * scale).astype(dq_stage.dtype)
            dkacc[pl.ds(w0, WIN), ls] += lax.dot_general(dsb, q2s, TN, preferred_element_type=F32)
            dvacc[pl.ds(w0, WIN), ls] += lax.dot_general(pr.astype(BF16), do2, TN, preferred_element_type=F32)

        dq_out(slot).start()

        @pl.when(j == nb - 1)
        def _():
            kpad[pl.ds(PADR, S), :] = dkacc[pl.ds(PADR, S), :].astype(BF16)
            vpad[pl.ds(PADR, S), :] = dvacc[pl.ds(PADR, S), :].astype(BF16)
            ck = pltpu.make_async_copy(kpad.at[pl.ds(PADR, S), :], dqkv_hbm.at[b, :, pl.ds(D, D)], sem.at[0])
            cv = pltpu.make_async_copy(vpad.at[pl.ds(PADR, S), :], dqkv_hbm.at[b, :, pl.ds(2 * D, D)], sem.at[1])
            ck.start()
            cv.start()
            ck.wait()
            cv.wait()

        @pl.when((b == B - 1) & (j == nb - 1))
        def _():
            cb = pltpu.make_async_copy(dbias_v, dbias_hbm, sem.at[2])
            cb.start()
            dq_out(0).wait()
            dq_out(1).wait()
            cb.wait()

    blk = _nbytes((QROWS, D), BF16) * 2
    scr = (2 * _nbytes((PADR + S, D), BF16) + 2 * _nbytes((PADR + S, D), F32) + 2 * _nbytes(bias.shape, F32)
           + 8 * _nbytes((2 * QROWS, WIN), F32) + 2 * _nbytes((QROWS, D), F32))
    return pl.pallas_call(
        body, name=name, grid=(B, nb),
        in_specs=[pl.BlockSpec((1, QROWS, D), lambda b, j: (b, j, 0)),
                  pl.BlockSpec((1, QROWS, D), lambda b, j: (b, j, 0)),
                  pl.BlockSpec(memory_space=pl.ANY), pl.BlockSpec(memory_space=pl.ANY)],
        out_specs=[pl.BlockSpec(memory_space=pl.ANY), pl.BlockSpec(memory_space=pl.ANY)],
        out_shape=[jax.ShapeDtypeStruct((B, S, 3 * D), BF16), jax.ShapeDtypeStruct(bias.shape, F32)],
        scratch_shapes=[pltpu.VMEM((PADR + S, D), BF16), pltpu.VMEM((PADR + S, D), BF16),
                        pltpu.VMEM((PADR + S, D), F32), pltpu.VMEM((PADR + S, D), F32),
                        pltpu.VMEM(bias.shape, F32), pltpu.VMEM(bias.shape, F32),
                        pltpu.VMEM((2, QROWS, D), BF16), pltpu.SemaphoreType.DMA((5,))],
        compiler_params=_params(blk, scr),
    )(qkv, do, bias, qkv)


def _mla_raw_t(k2, kj, q, QB):
    return lax.dot_general(k2[_blk(kj, QB), :], q, NT, preferred_element_type=F32)


def _blk(kj, QB):
    return pl.ds(kj * QB, QB) if isinstance(kj, int) else pl.ds(pl.multiple_of(kj * QB, QB), QB)


def _mla_diag_pen(QB):
    kc = jnp.right_shift(lax.broadcasted_iota(I32, (QB, QB), 0), CHUNK_SHIFT)
    qc = jnp.right_shift(lax.broadcasted_iota(I32, (QB, QB), 1), CHUNK_SHIFT)
    return jnp.where(kc <= qc, 0.0, NEG_INF).astype(F32)


def _mla_fill_keys(kv_ref, kr_ref, k2):
    k2[:, :NOPE] = kv_ref[0, :, :NOPE]
    k2[:, NOPE:] = kr_ref[0]


def _t(x):
    return x.astype(F32).T


def mla_fwd(name, qf, kv, kr):
    B, S, W = qf.shape
    HB = W // 256
    QB = _tile(S, 256, CHUNK)
    nq = S // QB
    scale = (NOPE + ROPE) ** -0.5

    def body(q_ref, kv_ref, kr_ref, o_ref, lse_ref, k2, vt, st_buf, pen):
        qi = pl.program_id(2)

        @pl.when(qi == 0)
        def _():
            pen[...] = _mla_diag_pen(QB)
            _mla_fill_keys(kv_ref, kr_ref, k2)
            for kj in range(nq):
                vt[kj] = _t(kv_ref[0, kj * QB:(kj + 1) * QB, NOPE:]).astype(BF16)

        q = q_ref[0]
        st_buf[0] = _mla_raw_t(k2, 0, q, QB)

        def step(kj, carry):
            m, l, acc = carry
            cur = lax.rem(kj, 2)
            st_raw = st_buf[cur]
            st_buf[1 - cur] = _mla_raw_t(k2, jnp.minimum(kj + 1, qi), q, QB)
            st = st_raw * scale + jnp.where(kj == qi, pen[...], 0.0)
            m_new = jnp.maximum(m, jnp.max(st, axis=0, keepdims=True))
            a = jnp.exp(m - m_new)
            pt = jnp.exp(st - m_new)
            l = a * l + jnp.sum(pt, axis=0, keepdims=True)
            acc = a * acc + jnp.dot(vt[kj], pt.astype(BF16), preferred_element_type=F32)
            return m_new, l, acc

        init = (jnp.full((1, QB), NEG_INF, F32), jnp.zeros((1, QB), F32), jnp.zeros((NOPE, QB), F32))
        m, l, acc = lax.fori_loop(0, qi + 1, step, init)
        o_ref[0] = (acc * (1.0 / l)).T
        lse_ref[0, 0] = m + jnp.log(l)

    blk = (_nbytes((QB, 256), BF16) + _nbytes((S, 256), BF16) + _nbytes((S, LANE), BF16)
           + _nbytes((QB, LANE), F32))
    return pl.pallas_call(
        body, name=name, grid=(B, HB, nq),
        in_specs=[pl.BlockSpec((1, QB, 256), lambda b, h, i: (b, i, h)),
                  pl.BlockSpec((1, S, 256), lambda b, h, i: (b, 0, h)),
                  pl.BlockSpec((1, S, LANE), lambda b, h, i: (b, 0, 0))],
        out_specs=[pl.BlockSpec((1, QB, LANE), lambda b, h, i: (b, i, h)),
                   pl.BlockSpec((1, 1, 1, QB), lambda b, h, i: (b, h, 0, i))],
        out_shape=[jax.ShapeDtypeStruct((B, S, HB * LANE), F32), jax.ShapeDtypeStruct((B, HB, 1, S), F32)],
        scratch_shapes=[pltpu.VMEM((S, 256), BF16), pltpu.VMEM((nq, NOPE, QB), BF16),
                        pltpu.VMEM((2, QB, QB), F32), pltpu.VMEM((QB, QB), F32)],
        compiler_params=_params(blk, 2 * _nbytes((S, 256), BF16) + 10 * _nbytes((QB, QB), F32)),
    )(qf, kv, kr)


def mla_bwd(name, qf, kv, kr, do, o, lse, tabs):
    B, S, W = qf.shape
    HB = W // 256
    QB = _tile(S, 256, CHUNK)
    nq = S // QB
    scale = (NOPE + ROPE) ** -0.5

    def body(q_ref, kv_ref, kr_ref, do_ref, o_ref, lse_ref, ct_ref, s1_ref, s2_ref, dq_ref, dkv_ref, dkr_ref,
             k2, kt, dot_, delta, dqt, st_buf, dp_buf, pen, dkv_acc):
        h = pl.program_id(1)
        pen[...] = _mla_diag_pen(QB)
        dkv_acc[...] = jnp.zeros_like(dkv_acc)

        @pl.when(h == 0)
        def _():
            dkr_ref[...] = jnp.zeros_like(dkr_ref)

        _mla_fill_keys(kv_ref, kr_ref, k2)
        for i in range(nq):
            rows = slice(i * QB, (i + 1) * QB)
            kt[i] = _t(k2[rows, :]).astype(BF16)
            dot32 = _t(do_ref[0, rows, :])
            delta[i] = jnp.sum(dot32 * o_ref[0, rows, :].T, axis=0, keepdims=True)
            dot_[i] = dot32.astype(BF16)

        for qi in range(nq):
            rows = slice(qi * QB, (qi + 1) * QB)
            q = q_ref[0, rows, :]
            dob = do_ref[0, rows, :]
            lse_q = lse_ref[0, 0, :, rows]
            delta_q = delta[qi]
            dqt[...] = jnp.zeros_like(dqt)

            def raw(kj, slot, q=q, qi=qi):
                st_buf[slot] = _mla_raw_t(k2, kj, q, QB)
                dp_buf[slot] = jnp.dot(kv_ref[0, _blk(kj, QB), NOPE:], dot_[qi], preferred_element_type=F32)

            raw(0, 0)

            def step(kj, carry, q=q, dob=dob, lse_q=lse_q, delta_q=delta_q, qi=qi, raw=raw):
                ks = pl.ds(pl.multiple_of(kj * QB, QB), QB)
                cur = lax.rem(kj, 2)
                st_raw, dp_raw = st_buf[cur], dp_buf[cur]
                raw(jnp.minimum(kj + 1, qi), 1 - cur)
                pt = jnp.exp(st_raw * scale + jnp.where(kj == qi, pen[...], 0.0) - lse_q)
                dst = (pt * (dp_raw - delta_q) * scale).astype(BF16)
                dkv_acc[ks, NOPE:] += jnp.dot(pt.astype(BF16), dob, preferred_element_type=F32)
                dk2 = jnp.dot(dst, q, preferred_element_type=F32)
                dkv_acc[ks, :NOPE] += dk2[:, :NOPE]
                dkr_ref[0, ks, :] += dk2[:, NOPE:]
                dqt[...] += jnp.dot(kt[kj], dst, preferred_element_type=F32)
                return carry

            lax.fori_loop(0, qi + 1, step, 0)
            dq = dqt[...].T
            dq_ref[0, rows, :NOPE] = dq[:, :NOPE].astype(dq_ref.dtype)
            dq_ref[0, rows, NOPE:] = _rope_tile_bwd(dq[:, NOPE:], ct_ref[rows, :], s1_ref[rows, :],
                                                    s2_ref[rows, :]).astype(dq_ref.dtype)

        dkv_ref[0] = dkv_acc[...].astype(dkv_ref.dtype)

    head = lambda w: pl.BlockSpec((1, S, w), lambda b, h: (b, 0, h))
    shared = pl.BlockSpec((1, S, LANE), lambda b, h: (b, 0, 0))
    blk = (2 * _nbytes((S, 256), BF16) + 2 * _nbytes((S, LANE), BF16) + _nbytes((S, LANE), F32)
           + 2 * _nbytes((S, 256), F32) + _nbytes((S, LANE), F32))
    scr = 3 * _nbytes((S, 256), BF16) + 14 * _nbytes((QB, QB), F32)
    return pl.pallas_call(
        body, name=name, grid=(B, HB),
        in_specs=[head(256), head(256), shared, head(LANE), head(LANE),
                  pl.BlockSpec((1, 1, 1, S), lambda b, h: (b, h, 0, 0))]
        + [pl.BlockSpec((S, LANE), lambda b, h: (0, 0))] * 3,
        out_specs=[head(256), head(256), shared],
        out_shape=[jax.ShapeDtypeStruct((B, S, W), BF16), jax.ShapeDtypeStruct((B, S, W), BF16),
                   jax.ShapeDtypeStruct((B, S, LANE), F32)],
        scratch_shapes=[pltpu.VMEM((S, 256), BF16), pltpu.VMEM((nq, 256, QB), BF16),
                        pltpu.VMEM((nq, NOPE, QB), BF16), pltpu.VMEM((nq, 1, QB), F32),
                        pltpu.VMEM((256, QB), F32), pltpu.VMEM((2, QB, QB), F32), pltpu.VMEM((2, QB, QB), F32),
                        pltpu.VMEM((QB, QB), F32), pltpu.VMEM((S, 256), F32)],
        compiler_params=_params(blk, scr),
    )(qf, kv, kr, do, o, lse, *tabs)


GROUP_STEPS = 4


def cast_group(name, ws, layers, idx, after=None):
    n = len(ws)
    n_in = n + (after is not None)

    def body(k_ref, *refs):
        for i in range(n):
            refs[n_in + i][...] = refs[i][...].astype(BF16)

    def spec_in(w, layer):
        return pl.BlockSpec((None, w.shape[1] // GROUP_STEPS, w.shape[2]), lambda r, k_ref: (layer, r, 0))

    def spec_out(w):
        return pl.BlockSpec((None, w.shape[1] // GROUP_STEPS, w.shape[2]), lambda r, k_ref: (k_ref[0], r, 0))

    return pl.pallas_call(
        body, name=name,
        grid_spec=pltpu.PrefetchScalarGridSpec(
            num_scalar_prefetch=1, grid=(GROUP_STEPS,),
            in_specs=([spec_in(w, l) for w, l in zip(ws, layers)]
                      + [pl.BlockSpec(memory_space=pl.ANY)] * (after is not None)),
            out_specs=[spec_out(w) for w in ws]),
        out_shape=[jax.ShapeDtypeStruct((N_CHIPS, *w.shape[1:]), BF16) for w in ws],
        compiler_params=_params(sum(_nbytes(w.shape[1:], F32) * 3 // 2 for w in ws) // GROUP_STEPS),
    )(idx, *ws, *([] if after is None else [after]))


def adamw(name, w, g, m, v):
    R, C = w.shape
    tr = _tile(R, max(8, (1 << 18) // C // 8 * 8), 8)
    c1 = 1.0 - ADAM_B1 ** ADAM_STEP
    c2 = 1.0 - ADAM_B2 ** ADAM_STEP

    def body(w_ref, g_ref, m_ref, v_ref, d_ref, mo_ref, vo_ref):
        gv = g_ref[...]
        mn = ADAM_B1 * m_ref[...] + (1.0 - ADAM_B1) * gv
        vn = ADAM_B2 * v_ref[...] + (1.0 - ADAM_B2) * (gv * gv)
        mo_ref[...] = mn
        vo_ref[...] = vn
        d_ref[...] = -ADAM_LR * ((mn / c1) / (jnp.sqrt(vn / c2) + ADAM_EPS) + ADAM_WD * w_ref[...])

    spec = pl.BlockSpec((tr, C), lambda r: (r, 0))
    return pl.pallas_call(
        body, name=name, grid=(R // tr,), in_specs=[spec] * 4, out_specs=[spec] * 3,
        out_shape=[jax.ShapeDtypeStruct((R, C), F32)] * 3,
        compiler_params=_params(7 * _nbytes((tr, C), F32), 4 * _nbytes((tr, C), F32)),
    )(w, g, m, v)


def half_sum_group(name, dws, landed, idx):
    n = len(dws)
    steps = GROUP_STEPS // 2

    def body(i_ref, *refs):
        for i in range(n):
            refs[2 * n + i][...] = (refs[i][...].astype(F32) + refs[n + i][...].astype(F32)).astype(BF16)

    def own(d):
        return pl.BlockSpec((None, None, d.shape[2] // steps, d.shape[3]), lambda k, r, i_ref: (k, i_ref[1], r, 0))

    def flat(d):
        return pl.BlockSpec((None, d.shape[2] // steps, d.shape[3]), lambda k, r, i_ref: (k, r, 0))

    return pl.pallas_call(
        body, name=name,
        grid_spec=pltpu.PrefetchScalarGridSpec(
            num_scalar_prefetch=1, grid=(N_CHIPS, steps),
            in_specs=[own(d) for d in dws] + [flat(d) for d in dws], out_specs=[flat(d) for d in dws]),
        out_shape=[jax.ShapeDtypeStruct((N_CHIPS, *d.shape[2:]), BF16) for d in dws],
        compiler_params=_params(sum(3 * _nbytes(d.shape[2:], BF16) for d in dws) // steps),
    )(idx, *dws, *landed)


def chip_sum_group(name, parts, landed, gbufs, layers, idx):
    n = len(parts)
    steps = GROUP_STEPS // 2

    def body(i_ref, *refs):
        for i in range(n):
            a, b = refs[i], refs[n + i]
            refs[3 * n + i][...] = ((a[...].astype(F32) + b[0].astype(F32)) + b[1].astype(F32)) + b[2].astype(F32)

    def mine(p):
        return pl.BlockSpec((None, p.shape[1] // steps, p.shape[2]), lambda r, i_ref: (i_ref[0], r, 0))

    def three(p):
        return pl.BlockSpec((3, p.shape[1] // steps, p.shape[2]), lambda r, i_ref: (0, r, 0))

    def out(p, layer):
        return pl.BlockSpec((None, None, p.shape[1] // steps, p.shape[2]), lambda r, i_ref: (layer, i_ref[1], r, 0))

    return pl.pallas_call(
        body, name=name,
        grid_spec=pltpu.PrefetchScalarGridSpec(
            num_scalar_prefetch=1, grid=(steps,),
            in_specs=[mine(p) for p in parts] + [three(p) for p in parts] + [pl.BlockSpec(memory_space=pl.ANY)] * n,
            out_specs=[out(p, l) for p, l in zip(parts, layers)]),
        out_shape=[jax.ShapeDtypeStruct(g.shape, F32) for g in gbufs],
        input_output_aliases={1 + 2 * n + i: i for i in range(n)},
        compiler_params=_params(sum(6 * _nbytes(p.shape[1:], BF16) for p in parts) // steps),
    )(idx, *parts, *landed, *gbufs)


ANY = pl.BlockSpec(memory_space=pl.ANY)


def _place():
    x, y, c = lax.axis_index("x"), lax.axis_index("y"), lax.axis_index("c")
    chips = [(1 - x, y), (x, 1 - y), (1 - x, 1 - y)]
    return x, y, c, chips


HBM = pl.BlockSpec(memory_space=pltpu.HBM)
SEM = pl.BlockSpec(memory_space=pltpu.SEMAPHORE)
EFFECT = pltpu.SideEffectType.DATAFLOW_SIDE_EFFECTING


def _in_hbm(a):
    return pltpu.with_memory_space_constraint(a, pltpu.HBM)


def _ici_copy(src, dst, send_sems, recv_sems, k, peer):
    return pltpu.make_async_remote_copy(src_ref=src, dst_ref=dst, send_sem=send_sems.at[k], recv_sem=recv_sems.at[k],
                                        device_id=peer, device_id_type=MESH)


def ici_start(name, bufs, lands, after, gather):
    n, nl = len(bufs), len(lands)

    def body(*refs):
        b_in = refs[:n]
        send_sems, recv_sems = refs[n + nl + 1], refs[n + nl + 2]
        b_out = refs[n + nl + 3:2 * n + nl + 3]
        l_out = refs[2 * n + nl + 3:2 * n + 2 * nl + 3]
        token = refs[-1]
        x, y, c, chips = _place()
        kme = 2 * x + y
        for i in range(n):
            for j in range(3):
                peer = (*chips[j], c)
                if gather:
                    _ici_copy(b_out[i].at[kme, c], b_out[i].at[kme, c], send_sems, recv_sems, 3 * i + j, peer).start()
                else:
                    kd = 2 * chips[j][0] + chips[j][1]
                    _ici_copy(b_out[i].at[kd], l_out[i].at[j], send_sems, recv_sems, 3 * i + j, peer).start()
        token[...] = jnp.zeros_like(token)

    arrays = [*bufs, *lands]
    outs = pl.pallas_call(
        body, name=name,
        in_specs=[HBM] * (n + nl) + [ANY],
        out_specs=(SEM, SEM, *[HBM] * (n + nl), pl.BlockSpec(memory_space=pltpu.VMEM)),
        out_shape=(pltpu.SemaphoreType.DMA((3 * n,)), pltpu.SemaphoreType.DMA((3 * n,)),
                   *[pltpu.HBM(a.shape, a.dtype) for a in arrays], jax.ShapeDtypeStruct((8, LANE), F32)),
        input_output_aliases={i: 2 + i for i in range(n + nl)},
        compiler_params=pltpu.CompilerParams(has_side_effects=EFFECT),
    )(*[_in_hbm(a) for a in arrays], after)
    return outs[0], outs[1], list(outs[2:2 + n]), list(outs[2 + n:2 + n + nl]), outs[-1]


def ici_wait(name, send_sems, recv_sems, bufs, lands, after, gather):
    n, nl = len(bufs), len(lands)

    def body(*refs):
        b_in, l_in = refs[:n], refs[n:n + nl]
        send_sems, recv_sems = refs[n + nl], refs[n + nl + 1]
        x, y, c, chips = _place()
        kme = 2 * x + y
        for i in range(n):
            for j in range(3):
                peer = (*chips[j], c)
                kj = 2 * chips[j][0] + chips[j][1]
                if gather:
                    _ici_copy(b_in[i].at[kme, c], b_in[i].at[kme, c], send_sems, recv_sems, 3 * i + j, peer).wait_send()
                    _ici_copy(b_in[i].at[kj, c], b_in[i].at[kj, c], send_sems, recv_sems, 3 * i + j, peer).wait_recv()
                else:
                    _ici_copy(b_in[i].at[kj], l_in[i].at[j], send_sems, recv_sems, 3 * i + j, peer).wait_send()
                    _ici_copy(b_in[i].at[kj], l_in[i].at[j], send_sems, recv_sems, 3 * i + j, peer).wait_recv()

    arrays = [*bufs, *lands]
    outs = pl.pallas_call(
        body, name=name,
        in_specs=[HBM] * (n + nl) + [SEM, SEM, ANY],
        out_specs=tuple([HBM] * (n + nl)),
        out_shape=tuple(pltpu.HBM(a.shape, a.dtype) for a in arrays),
        input_output_aliases={i: i for i in range(n + nl)},
        compiler_params=pltpu.CompilerParams(has_side_effects=EFFECT),
    )(*arrays, send_sems, recv_sems, after)
    return list(outs[:n]), list(outs[n:])


def gather_pair_pass(name, bufs):
    n = len(bufs)

    def body(*refs):
        b = refs[n:2 * n]
        send_sems, recv_sems = refs[2 * n:]
        x, y, c, chips = _place()
        sib = (x, y, 1 - c)

        def d2d(i, j, which):
            kj = 2 * chips[j][0] + chips[j][1]
            return _ici_copy(b[i].at[kj, which], b[i].at[kj, which], send_sems, recv_sems, 3 * i + j, sib)

        for i in range(n):
            for j in range(3):
                d2d(i, j, c).start()
        for i in range(n):
            for j in range(3):
                d2d(i, j, 1 - c).wait_recv()
        for i in range(n):
            for j in range(3):
                d2d(i, j, c).wait_send()

    return pl.pallas_call(
        body, name=name, in_specs=[ANY] * n, out_specs=[ANY] * n,
        out_shape=[jax.ShapeDtypeStruct(a.shape, a.dtype) for a in bufs],
        input_output_aliases={i: i for i in range(n)},
        scratch_shapes=[pltpu.SemaphoreType.DMA((3 * n,)), pltpu.SemaphoreType.DMA((3 * n,))],
    )(*bufs)


def pair_exchange(name, dws):
    n = len(dws)

    def body(*refs):
        ins, outs = refs[:n], refs[n:2 * n]
        send_sems, recv_sems = refs[2 * n:]
        x, y, c, _ = _place()
        copies = []
        for i in range(n):
            copies.append(pltpu.make_async_remote_copy(
                src_ref=ins[i].at[:, 1 - c], dst_ref=outs[i],
                send_sem=send_sems.at[i], recv_sem=recv_sems.at[i],
                device_id=(x, y, 1 - c), device_id_type=MESH))
            copies[i].start()
        for cp in copies:
            cp.wait_recv()
        for cp in copies:
            cp.wait_send()

    return pl.pallas_call(
        body, name=name, in_specs=[ANY] * n, out_specs=[ANY] * n,
        out_shape=[jax.ShapeDtypeStruct((N_CHIPS, *d.shape[2:]), d.dtype) for d in dws],
        scratch_shapes=[pltpu.SemaphoreType.DMA((n,)), pltpu.SemaphoreType.DMA((n,))],
    )(*dws)


def pair_assemble(gbufs):
    n = len(gbufs)

    def body(*refs):
        bufs = refs[n:2 * n]
        send_sems, recv_sems = refs[2 * n:]
        x, y, c, _ = _place()
        copies = []
        for i in range(n):
            copies.append(pltpu.make_async_remote_copy(
                src_ref=bufs[i].at[:, c], dst_ref=bufs[i].at[:, c],
                send_sem=send_sems.at[i], recv_sem=recv_sems.at[i],
                device_id=(x, y, 1 - c), device_id_type=MESH))
            copies[i].start()
        for i in range(n):
            pltpu.make_async_remote_copy(
                src_ref=bufs[i].at[:, 1 - c], dst_ref=bufs[i].at[:, 1 - c],
                send_sem=send_sems.at[i], recv_sem=recv_sems.at[i],
                device_id=(x, y, 1 - c), device_id_type=MESH).wait_recv()
        for cp in copies:
            cp.wait_send()

    return pl.pallas_call(
        body, name="grad_pair_assemble", in_specs=[ANY] * n, out_specs=[ANY] * n,
        out_shape=[jax.ShapeDtypeStruct(g.shape, g.dtype) for g in gbufs],
        input_output_aliases={i: i for i in range(n)},
        scratch_shapes=[pltpu.SemaphoreType.DMA((n,)), pltpu.SemaphoreType.DMA((n,))],
    )(*gbufs)


def all_reduce_small(vec):
    NR = vec.shape[0]
    flips = [(fx, fy, fc) for fx in (0, 1) for fy in (0, 1) for fc in (0, 1)][1:]

    def body(v_ref, o_ref, buf, send_sems, recv_sems):
        x, y, c, _ = _place()
        me = 4 * x + 2 * y + c
        buf[me] = v_ref[...]
        copies = []
        for j, (fx, fy, fc) in enumerate(flips):
            peer = (1 - x if fx else x, 1 - y if fy else y, 1 - c if fc else c)
            copies.append(pltpu.make_async_remote_copy(
                src_ref=v_ref, dst_ref=buf.at[me], send_sem=send_sems.at[j], recv_sem=recv_sems.at[j],
                device_id=peer, device_id_type=MESH))
            copies[j].start()
        for cp in copies:
            cp.wait_recv()
        for cp in copies:
            cp.wait_send()
        acc = buf[0]
        for d in range(1, 8):
            acc = acc + buf[d]
        o_ref[...] = acc

    return pl.pallas_call(
        body, name="all_reduce_small",
        in_specs=[pl.BlockSpec(memory_space=pltpu.VMEM)], out_specs=pl.BlockSpec(memory_space=pltpu.VMEM),
        out_shape=jax.ShapeDtypeStruct((NR, LANE), F32),
        scratch_shapes=[pltpu.VMEM((8, NR, LANE), F32), pltpu.SemaphoreType.DMA((7,)),
                        pltpu.SemaphoreType.DMA((7,))],
    )(vec)


def _pack(arrays):
    flat = jnp.concatenate([a.reshape(-1).astype(F32) for a in arrays])
    n = flat.shape[0]
    npad = -(-n // (8 * LANE)) * (8 * LANE)
    return jnp.pad(flat, (0, npad - n)).reshape(npad // LANE, LANE)


def _unpack(buf, like):
    flat = buf.reshape(-1)
    out, off = [], 0
    for a in like:
        out.append(flat[off:off + a.size].reshape(a.shape))
        off += a.size
    return out


def kernel(x, ffn1_norm, ffn1_w_in, ffn1_w_out, mix_norm, ffn2_norm, ffn2_w_in, ffn2_w_out, a_w_qkv, a_rel_bias, a_w_o, kv_norm, kv_w_down, kv_latent_norm, kv_w_up, b_w_dq, b_q_norm, b_w_uq, b_w_o, final_norm, loss_target, m_ffn1_norm, m_ffn1_w_in, m_ffn1_w_out, m_mix_norm, m_ffn2_norm, m_ffn2_w_in, m_ffn2_w_out, m_a_w_qkv, m_a_rel_bias, m_a_w_o, m_kv_norm, m_kv_w_down, m_kv_latent_norm, m_kv_w_up, m_b_w_dq, m_b_q_norm, m_b_w_uq, m_b_w_o, m_final_norm, v_ffn1_norm, v_ffn1_w_in, v_ffn1_w_out, v_mix_norm, v_ffn2_norm, v_ffn2_w_in, v_ffn2_w_out, v_a_w_qkv, v_a_rel_bias, v_a_w_o, v_kv_norm, v_kv_w_down, v_kv_latent_norm, v_kv_w_up, v_b_w_dq, v_b_q_norm, v_b_w_uq, v_b_w_o, v_final_norm):
    B, S, D = x.shape
    T = B * S
    HB = D // 128
    QL = b_q_norm.shape[-1]
    KVL = kv_latent_norm.shape[0]
    hpc = HB // N_CHIPS
    tabs = rope_tables(S)
    idx = jnp.stack([2 * lax.axis_index("x") + lax.axis_index("y"), lax.axis_index("c")]).astype(I32)

    def halves(a):
        return a.reshape(*a.shape[:-2], 2, a.shape[-2] // 2, a.shape[-1])

    def whole(a):
        return a.reshape(*a.shape[:-3], 2 * a.shape[-2], a.shape[-1])

    kv_w_down_p = jnp.pad(kv_w_down, ((0, 0), (0, LANE - ROPE)))[None]
    b_w_uq_p = jnp.pad(b_w_uq.reshape(1, QL, hpc, NOPE + ROPE),
                       ((0, 0), (0, 0), (0, 0), (0, LANE - ROPE))).reshape(1, QL, hpc * 256)
    sharded = [("ffn1_w_in", ffn1_w_in), ("ffn1_w_out", ffn1_w_out), ("ffn2_w_in", ffn2_w_in),
               ("ffn2_w_out", ffn2_w_out), ("a_w_qkv", a_w_qkv), ("a_w_o", a_w_o),
               ("kv_w_down", kv_w_down_p), ("kv_w_up", kv_w_up[None]), ("b_w_dq", b_w_dq),
               ("b_w_uq", b_w_uq_p), ("b_w_o", b_w_o)]
    names = [nm for nm, _ in sharded]
    shard_of = dict(sharded)
    W = {}

    gather_groups = [
        [("ffn1_w_in", 0)],
        [("ffn1_w_out", 0)],
        [("a_w_qkv", 0), ("a_w_o", 0)],
        [("ffn2_w_in", 0), ("ffn2_w_out", 0), ("kv_w_down", 0), ("kv_w_up", 0)],
        [("ffn1_w_in", 1), ("ffn1_w_out", 1), ("b_w_dq", 0), ("b_w_uq", 0), ("b_w_o", 0), ("ffn2_w_in", 1),
         ("ffn2_w_out", 1)]]

    own = {}

    def cast(g, after=None):
        keys = gather_groups[g]
        own.update(zip(keys, cast_group(f"cast_group_{g}", [shard_of[nm] for nm, _ in keys], [l for _, l in keys],
                                        idx, after=after)))

    def gather_start(g, after):
        keys = gather_groups[g]
        ss, rs, bufs, _, token = ici_start(f"gather_start_{g}", [halves(own[k]) for k in keys], [], after, True)
        return (g, ss, rs, bufs), token

    def gather_finish(state, after):
        g, ss, rs, bufs = state
        bufs, _ = ici_wait(f"gather_wait_{g}", ss, rs, bufs, [], after, True)
        full = gather_pair_pass(f"gather_pair_{g}", bufs)
        for k, w in zip(gather_groups[g], full):
            W[k] = whole(w)
        return full[0]

    def tied(a, token):
        return a + token[0, 0]

    def col(nm, l=0):
        return W[(nm, l)]

    def row(nm, l=0):
        w = W[(nm, l)]
        return w.reshape(N_CHIPS * w.shape[1], w.shape[2])

    bias = rel_bias_tile("rel_bias_tile", a_rel_bias[0])

    def ffn_fwd(tag, h, g, w_in, w_out):
        xn = rms_fwd(f"{tag}_norm", h, g)
        u, act = ffn_in_act(f"{tag}_in", xn, w_in)
        return mm_roww(f"{tag}_out", act, w_out, F32, res=h, alpha=0.5), (xn, u, act)

    h0 = x.reshape(T, D)
    for g in range(3):
        cast(g)
    st0, tok0 = gather_start(0, h0)
    st1, tok1 = gather_start(1, tok0)
    st2, tok2 = gather_start(2, tok1)
    for g in range(3, len(gather_groups)):
        cast(g, tok2)
    xn0 = rms_fwd("l0f1_norm", h0, tied(ffn1_norm[0], tok2))
    gather_finish(st0, xn0)
    u0, act0 = ffn_in_act("l0f1_in", xn0, col("ffn1_w_in", 0))
    gather_finish(st1, u0)
    h1 = mm_roww("l0f1_out", act0, row("ffn1_w_out", 0), F32, res=h0, alpha=0.5)
    sv_f1a = (xn0, u0, act0)
    done2 = gather_finish(st2, h1)
    st3, tok3 = gather_start(3, done2)
    st4, tok4 = gather_start(4, tok3)
    hn_a = rms_fwd("l0mix_norm", h1, tied(mix_norm[0], tok4))
    qkv = mm_colw("l0_qkv", hn_a, col("a_w_qkv"), BF16).reshape(B, S, 3 * D)
    o_a = attn_a_fwd("l0_attn", qkv, bias).reshape(T, D)
    h2 = mm_roww("l0_attn_out", o_a, row("a_w_o"), F32, res=h1)
    gather_finish(st3, h2)
    h3, sv_f2a = ffn_fwd("l0f2", h2, ffn2_norm[0], col("ffn2_w_in", 0), row("ffn2_w_out", 0))

    hkv = rms_fwd("kv_norm", h3, kv_norm)
    ckr = mm_roww("kv_down", hkv, row("kv_w_down"), F32)
    ckv, kr = kvprep_fwd("kv_prep", ckr, kv_latent_norm, tabs, B, S)
    kvb = mm_colw("kv_up", ckv, col("kv_w_up"), BF16).reshape(B, S, HB * 256)
    gather_finish(st4, kvb)

    h4, sv_f1b = ffn_fwd("l1f1", h3, ffn1_norm[1], col("ffn1_w_in", 1), row("ffn1_w_out", 1))
    hn_b = rms_fwd("l1mix_norm", h4, mix_norm[1])
    cqp = mm_roww("l1_dq", hn_b, row("b_w_dq"), F32)
    cq = rms_fwd("l1_q_norm", cqp, b_q_norm[0])
    qf = uq_rope("l1_uq", cq, col("b_w_uq"), tabs, S).reshape(B, S, HB * 256)
    o_b, lse = mla_fwd("l1_attn", qf, kvb, kr)
    h5 = mm_roww("l1_attn_out", o_b.reshape(T, HB * LANE), row("b_w_o"), F32, res=h4)
    h6, sv_f2b = ffn_fwd("l1f2", h5, ffn2_norm[1], col("ffn2_w_in", 1), row("ffn2_w_out", 1))

    dh, g_final, loss_part = loss_head("loss_head", h6, final_norm, loss_target.reshape(T, D))

    gw = {}
    gbufs = {nm: lax.empty(halves(w).shape, F32) for nm, w in sharded}

    def reduce_start(r, keys, after):
        dws = [halves(gw[k]) for k in keys]
        landed = pair_exchange(f"grad_pair_exchange_{r}", dws)
        parts = half_sum_group(f"half_sum_{r}", dws, landed, idx)
        lands = [lax.empty((3, *p.shape[1:]), p.dtype) for p in parts]
        ss, rs, parts, lands, token = ici_start(f"reduce_start_{r}", parts, lands, after, False)
        return (r, keys, ss, rs, parts, lands), token

    def reduce_finish(state, after):
        r, keys, ss, rs, parts, lands = state
        parts, lands = ici_wait(f"reduce_wait_{r}", ss, rs, parts, lands, after, False)
        done = chip_sum_group(f"chip_sum_{r}", parts, lands, [gbufs[nm] for nm, _ in keys], [l for _, l in keys], idx)
        gbufs.update(zip([nm for nm, _ in keys], done))
        return done[0]

    def ffn_bwd(tag, dh, h_in, g, w_in, w_out, saved, key_in, key_out, after=None, then=None):
        xn, u, act = saved
        du = ffn_dact(f"{tag}_dact", dh, w_out, u, after=after)
        dwo = mm_droww(f"{tag}_dwout", act, dh, alpha=0.5)
        gw[key_out] = dwo.reshape(N_CHIPS, dwo.shape[0] // N_CHIPS, dwo.shape[1])
        gw[key_in] = mm_dcolw(f"{tag}_dwin", xn, du, pair_layout=True)
        token = then(du) if then is not None else None
        return dx_norm_bwd(f"{tag}_dxn", du, w_in, h_in, g, dres=dh, pair_layout=True, after=token)

    def chip_major(dw):
        return dw.reshape(N_CHIPS, dw.shape[0] // N_CHIPS, dw.shape[1])

    dh, g_f2b = ffn_bwd("l1f2b", dh, h5, ffn2_norm[1], col("ffn2_w_in", 1), row("ffn2_w_out", 1), sv_f2b,
                        ("ffn2_w_in", 1), ("ffn2_w_out", 1))
    red0, rtok0 = reduce_start(0, [("ffn2_w_in", 1), ("ffn2_w_out", 1)], dh)
    do_b = mm_roww_t("l1_attn_do", dh, row("b_w_o"), BF16, after=rtok0).reshape(B, S, HB * LANE)
    gw[("b_w_o", 0)] = chip_major(mm_droww("l1_attn_dwo", o_b.reshape(T, HB * LANE), dh))
    dqpre, dkv, dkr = mla_bwd("l1_attn_bwd", qf, kvb, kr, do_b, o_b, lse, tabs)
    dqpre = dqpre.reshape(T, HB * 256)
    gw[("b_w_uq", 0)] = mm_dcolw("l1_dwuq", cq, dqpre)
    dcqp, g_qn = dx_norm_bwd("l1_dcq", dqpre, col("b_w_uq"), cqp, b_q_norm[0])
    gw[("b_w_dq", 0)] = chip_major(mm_droww("l1_dwdq", hn_b, dcqp))
    dhn = mm_roww_t("l1_dhn", dcqp, row("b_w_dq"), F32)
    dh, g_mixb = rms_bwd("l1_dmix", h4, mix_norm[1], dhn, dres=dh)
    dh, g_f1b = ffn_bwd("l1f1b", dh, h3, ffn1_norm[1], col("ffn1_w_in", 1), row("ffn1_w_out", 1), sv_f1b,
                        ("ffn1_w_in", 1), ("ffn1_w_out", 1))
    fin0 = reduce_finish(red0, dh)
    red1, rtok1 = reduce_start(1, [("b_w_o", 0), ("b_w_uq", 0), ("b_w_dq", 0), ("ffn1_w_in", 1), ("ffn1_w_out", 1)], fin0)
    dkv2 = dkv.reshape(T, HB * 256)
    gw[("kv_w_up", 0)] = mm_dcolw("kv_dwup", ckv, dkv2, after=rtok1)
    dckv = mm_colw_t("kv_dckv", dkv2, col("kv_w_up"), F32, after=rtok1)
    dckr, g_lat = kvprep_bwd("kv_prep_bwd", ckr, kv_latent_norm, dckv, dkr, tabs, B, S)
    gw[("kv_w_down", 0)] = chip_major(mm_droww("kv_dwdown", hkv, dckr))
    dhkv = mm_roww_t("kv_dhkv", dckr, row("kv_w_down"), F32)
    dh, g_kvn = rms_bwd("kv_dnorm", h3, kv_norm, dhkv, dres=dh)
    dh, g_f2a = ffn_bwd("l0f2b", dh, h2, ffn2_norm[0], col("ffn2_w_in", 0), row("ffn2_w_out", 0), sv_f2a,
                        ("ffn2_w_in", 0), ("ffn2_w_out", 0))
    do_a = mm_roww_t("l0_attn_do", dh, row("a_w_o"), BF16).reshape(B, S, D)
    gw[("a_w_o", 0)] = chip_major(mm_droww("l0_attn_dwo", o_a, dh))
    dqkv, dbias = attn_a_bwd("l0_attn_bwd", qkv, do_a, bias)
    dqkv = dqkv.reshape(T, 3 * D)
    gw[("a_w_qkv", 0)] = mm_dcolw("l0_dwqkv", hn_a, dqkv)
    dh, g_mixa = dx_norm_bwd("l0_dhn", dqkv, col("a_w_qkv"), h1, mix_norm[0], dres=dh)
    fin1 = reduce_finish(red1, dh)
    red2, rtok2 = reduce_start(2, [("kv_w_up", 0), ("kv_w_down", 0), ("ffn2_w_in", 0), ("ffn2_w_out", 0),
                                   ("a_w_o", 0), ("a_w_qkv", 0)], fin1)
    last = {}

    def last_group(du):
        fin2 = reduce_finish(red2, gw[("ffn1_w_in", 0)])
        last["red"], token = reduce_start(3, [("ffn1_w_in", 0), ("ffn1_w_out", 0)], fin2)
        return token

    dh, g_f1a = ffn_bwd("l0f1b", dh, h0, ffn1_norm[0], col("ffn1_w_in", 0), row("ffn1_w_out", 0), sv_f1a,
                        ("ffn1_w_in", 0), ("ffn1_w_out", 0), after=rtok2, then=last_group)
    grad_x = dh.reshape(B, S, D)
    g_rel = rel_bias_grad("rel_bias_grad", dbias)[:, :2 * MAX_REL + 1][None]
    reduce_finish(last["red"], dh)

    full = [whole(g) for g in pair_assemble([gbufs[nm] for nm in names])]
    G = {nm: g for (nm, _), g in zip(sharded, full)}
    G["kv_w_down"] = G["kv_w_down"][0, :, :KVL + ROPE]
    G["kv_w_up"] = G["kv_w_up"][0]
    G["b_w_uq"] = G["b_w_uq"].reshape(1, QL, hpc, 256)[..., :NOPE + ROPE].reshape(b_w_uq.shape)

    small = [("ffn1_norm", jnp.stack([g_f1a, g_f1b])), ("mix_norm", jnp.stack([g_mixa, g_mixb])),
             ("ffn2_norm", jnp.stack([g_f2a, g_f2b])), ("a_rel_bias", g_rel), ("kv_norm", g_kvn),
             ("kv_latent_norm", g_lat), ("b_q_norm", g_qn[None]), ("final_norm", g_final)]
    red = all_reduce_small(_pack([loss_part] + [g for _, g in small]))
    unpacked = _unpack(red, [loss_part] + [g for _, g in small])
    loss = unpacked[0][0, 0]
    for (nm, _), g in zip(small, unpacked[1:]):
        G[nm] = g

    given = dict(ffn1_norm=(ffn1_norm, m_ffn1_norm, v_ffn1_norm), ffn1_w_in=(ffn1_w_in, m_ffn1_w_in, v_ffn1_w_in),
                 ffn1_w_out=(ffn1_w_out, m_ffn1_w_out, v_ffn1_w_out), mix_norm=(mix_norm, m_mix_norm, v_mix_norm),
                 ffn2_norm=(ffn2_norm, m_ffn2_norm, v_ffn2_norm), ffn2_w_in=(ffn2_w_in, m_ffn2_w_in, v_ffn2_w_in),
                 ffn2_w_out=(ffn2_w_out, m_ffn2_w_out, v_ffn2_w_out), a_w_qkv=(a_w_qkv, m_a_w_qkv, v_a_w_qkv),
                 a_rel_bias=(a_rel_bias, m_a_rel_bias, v_a_rel_bias), a_w_o=(a_w_o, m_a_w_o, v_a_w_o),
                 kv_norm=(kv_norm, m_kv_norm, v_kv_norm), kv_w_down=(kv_w_down, m_kv_w_down, v_kv_w_down),
                 kv_latent_norm=(kv_latent_norm, m_kv_latent_norm, v_kv_latent_norm),
                 kv_w_up=(kv_w_up, m_kv_w_up, v_kv_w_up), b_w_dq=(b_w_dq, m_b_w_dq, v_b_w_dq),
                 b_q_norm=(b_q_norm, m_b_q_norm, v_b_q_norm), b_w_uq=(b_w_uq, m_b_w_uq, v_b_w_uq),
                 b_w_o=(b_w_o, m_b_w_o, v_b_w_o), final_norm=(final_norm, m_final_norm, v_final_norm))
    order = list(given)
    delta, new_m, new_v = {}, {}, {}
    small_names = [nm for nm, _ in small]
    packed = [_pack([given[nm][k] for nm in small_names]) for k in range(3)]
    outs = adamw("adamw_small", packed[0], _pack([G[nm] for nm in small_names]), packed[1], packed[2])
    for dst, buf in zip((delta, new_m, new_v), outs):
        for nm, a in zip(small_names, _unpack(buf, [given[nm][0] for nm in small_names])):
            dst[nm] = a
    for nm, _ in sharded:
        w, m, v = given[nm]
        g = G[nm].reshape(w.shape)
        G[nm] = g
        two = lambda a: a.reshape(-1, a.shape[-1])
        d_, m_, v_ = adamw(f"adamw_{nm}", two(w), two(g), two(m), two(v))
        delta[nm], new_m[nm], new_v[nm] = d_.reshape(w.shape), m_.reshape(w.shape), v_.reshape(w.shape)

    return (loss, grad_x, *[G[n] for n in order], *[delta[n] for n in order],
            *[new_m[n] for n in order], *[new_v[n] for n in order])
```

```python
import math

import jax
import jax.numpy as jnp
from jax import lax
from jax.experimental import pallas as pl
from jax.experimental.pallas import tpu as pltpu

F32 = jnp.float32
BF16 = jnp.bfloat16
I32 = jnp.int32

CHUNK = 64
CHUNK_SHIFT = 6
HEAD_DIM_A = 64
LEFT_CHUNKS = 8
MAX_REL = 128
REL_PAD = 384
QROWS = 2 * CHUNK
WIN = (LEFT_CHUNKS + 2) * CHUNK
PADR = LEFT_CHUNKS * CHUNK
NOPE = 128
ROPE = 64
EPS = 1e-6
NEG_INF = -1e30
ROPE_THETA = 10000.0
ADAM_LR, ADAM_B1, ADAM_B2, ADAM_EPS, ADAM_WD, ADAM_STEP = 0.001, 0.9, 0.999, 1e-08, 0.01, 10
N_CHIPS = 4
LANE = 128
MESH = pl.DeviceIdType.MESH
VMEM_CAP_MB = 60

NN = (((1,), (0,)), ((), ()))
NT = (((1,), (1,)), ((), ()))
TN = (((0,), (0,)), ((), ()))


def _tile(n, pref, mult):
    t = (min(pref, n) // mult) * mult
    while t >= mult:
        if n % t == 0:
            return t
        t -= mult
    return n


def _nbytes(shape, dtype):
    return math.prod(shape) * jnp.dtype(dtype).itemsize


def _params(block_bytes, extra_bytes=0):
    need = 2 * block_bytes + extra_bytes
    mb = min(VMEM_CAP_MB, max(48, int(need * 1.25 / 2**20) + 8))
    return pltpu.CompilerParams(vmem_limit_bytes=mb * 2**20)


def _mm(name, kind, a, b, grid, a_spec, b_spec, o_spec, out_shape, out_dtype, blocks,
        red_axis=None, nred=1, alpha=1.0, res=None, res_spec=None, after=None):
    dims = {"nn": NN, "nt": NT, "tn": TN}[kind]
    has_res = res is not None
    acc_in_out = nred > 1 and out_dtype == F32 and not has_res and alpha == 1.0
    n_in = 2 + has_res + (after is not None)

    def body(*refs):
        a_ref, b_ref = refs[0], refs[1]
        r_ref = refs[2] if has_res else None
        o_ref = refs[n_in]
        p = lax.dot_general(a_ref[...].astype(BF16), b_ref[...].astype(BF16), dims,
                            preferred_element_type=F32)

        def finish(acc):
            y = acc if alpha == 1.0 else acc * alpha
            if has_res:
                y = r_ref[...] + y
            o_ref[...] = y.astype(o_ref.dtype)

        if nred == 1:
            finish(p)
            return
        k = pl.program_id(red_axis)
        tgt = o_ref if acc_in_out else refs[-1]

        @pl.when(k == 0)
        def _():
            tgt[...] = p

        @pl.when(k > 0)
        def _():
            tgt[...] += p

        if not acc_in_out:
            @pl.when(k == nred - 1)
            def _():
                finish(tgt[...])

    a_blk, b_blk, o_blk = blocks
    scratch = []
    extra = 0
    if nred > 1 and not acc_in_out:
        scratch = [pltpu.VMEM(o_blk, F32)]
        extra = _nbytes(o_blk, F32)
    blk = _nbytes(a_blk, a.dtype) + _nbytes(b_blk, b.dtype) + _nbytes(o_blk, out_dtype)
    ins, specs = [a, b], [a_spec, b_spec]
    if has_res:
        ins.append(res)
        specs.append(res_spec)
        blk += _nbytes(o_blk, res.dtype)
    if after is not None:
        ins.append(after)
        specs.append(pl.BlockSpec(memory_space=pl.ANY))
    extra += _nbytes(a_blk, BF16) + _nbytes(b_blk, BF16) + 2 * _nbytes(o_blk, F32)
    return pl.pallas_call(
        body, name=name, grid=grid, in_specs=specs, out_specs=o_spec,
        out_shape=jax.ShapeDtypeStruct(out_shape, out_dtype), scratch_shapes=scratch,
        compiler_params=_params(blk, extra),
    )(*ins)


def mm_colw(name, x, w3, out_dtype):
    T, K = x.shape
    _, _, nl = w3.shape
    tm = _tile(T, 512, 8)
    return _mm(name, "nn", x, w3, (N_CHIPS, T // tm),
               pl.BlockSpec((tm, K), lambda j, i: (i, 0)),
               pl.BlockSpec((None, K, nl), lambda j, i: (j, 0, 0)),
               pl.BlockSpec((tm, nl), lambda j, i: (i, j)),
               (T, N_CHIPS * nl), out_dtype, ((tm, K), (K, nl), (tm, nl)))


def _pair_chip(j):
    return (j % 2) * 2 + j // 2


def mm_colw_t(name, dy, w3, out_dtype, res=None, after=None, pair_layout=False):
    T = dy.shape[0]
    _, K, nl = w3.shape
    tm = _tile(T, 1024, 8)
    chip = _pair_chip if pair_layout else (lambda j: j)
    return _mm(name, "nt", dy, w3, (T // tm, N_CHIPS),
               pl.BlockSpec((tm, nl), lambda i, j: (i, j)),
               pl.BlockSpec((None, K, nl), lambda i, j: (chip(j), 0, 0)),
               pl.BlockSpec((tm, K), lambda i, j: (i, 0)),
               (T, K), out_dtype, ((tm, nl), (K, nl), (tm, K)),
               red_axis=1, nred=N_CHIPS, res=res,
               res_spec=pl.BlockSpec((tm, K), lambda i, j: (i, 0)), after=after)


def mm_dcolw(name, x, dy, after=None, pair_layout=False):
    T, K = x.shape
    nl = dy.shape[1] // N_CHIPS
    tt = _tile(T, 2048, 8)
    chip = _pair_chip if pair_layout else (lambda j: j)
    return _mm(name, "tn", x, dy, (N_CHIPS, T // tt),
               pl.BlockSpec((tt, K), lambda j, t: (t, 0)),
               pl.BlockSpec((tt, nl), lambda j, t: (t, j)),
               pl.BlockSpec((None, K, nl), lambda j, t: (chip(j), 0, 0)),
               (N_CHIPS, K, nl), BF16, ((tt, K), (tt, nl), (K, nl)),
               red_axis=1, nred=T // tt, after=after)


def mm_roww(name, x, w2, out_dtype, res=None, alpha=1.0):
    T, Kt = x.shape
    N = w2.shape[1]
    tm = _tile(T, 512, 8)
    return _mm(name, "nn", x, w2, (T // tm,),
               pl.BlockSpec((tm, Kt), lambda i: (i, 0)),
               pl.BlockSpec((Kt, N), lambda i: (0, 0)),
               pl.BlockSpec((tm, N), lambda i: (i, 0)),
               (T, N), out_dtype, ((tm, Kt), (Kt, N), (tm, N)),
               alpha=alpha, res=res, res_spec=pl.BlockSpec((tm, N), lambda i: (i, 0)))


def mm_roww_t(name, dy, w2, out_dtype, alpha=1.0, after=None):
    T, N = dy.shape
    Kt = w2.shape[0]
    tm = _tile(T, 512, 8)
    tk = _tile(Kt, 1408, LANE)
    return _mm(name, "nt", dy, w2, (Kt // tk, T // tm),
               pl.BlockSpec((tm, N), lambda j, i: (i, 0)),
               pl.BlockSpec((tk, N), lambda j, i: (j, 0)),
               pl.BlockSpec((tm, tk), lambda j, i: (i, j)),
               (T, Kt), out_dtype, ((tm, N), (tk, N), (tm, tk)), alpha=alpha, after=after)


def mm_droww(name, x, dy, alpha=1.0):
    T, Kt = x.shape
    N = dy.shape[1]
    tt = _tile(T, 2048, 8)
    tk = _tile(Kt, 1408, LANE)
    return _mm(name, "tn", x, dy, (Kt // tk, T // tt),
               pl.BlockSpec((tt, tk), lambda j, t: (t, j)),
               pl.BlockSpec((tt, N), lambda j, t: (t, 0)),
               pl.BlockSpec((tk, N), lambda j, t: (j, 0)),
               (Kt, N), BF16, ((tt, tk), (tt, N), (tk, N)),
               red_axis=1, nred=T // tt, alpha=alpha)


def rms_fwd(name, x, g):
    T, D = x.shape
    tm = _tile(T, 512, 8)

    def body(x_ref, g_ref, o_ref):
        xv = x_ref[...]
        r = lax.rsqrt(jnp.mean(xv * xv, axis=-1, keepdims=True) + EPS)
        o_ref[...] = (xv * r * g_ref[...]).astype(o_ref.dtype)

    return pl.pallas_call(
        body, name=name, grid=(T // tm,),
        in_specs=[pl.BlockSpec((tm, D), lambda i: (i, 0)), pl.BlockSpec((1, D), lambda i: (0, 0))],
        out_specs=pl.BlockSpec((tm, D), lambda i: (i, 0)),
        out_shape=jax.ShapeDtypeStruct((T, D), BF16),
        compiler_params=_params(_nbytes((tm, D), F32) * 2, 4 * _nbytes((tm, D), F32)),
    )(x, g.reshape(1, D))


def _rms_bwd_math(xv, gv, dy):
    r = lax.rsqrt(jnp.mean(xv * xv, axis=-1, keepdims=True) + EPS)
    xh = xv * r
    dyg = dy * gv
    dx = r * (dyg - xh * jnp.mean(dyg * xh, axis=-1, keepdims=True))
    dg = jnp.sum(dy * xh, axis=0, keepdims=True)
    return dx, dg


def rms_bwd(name, x, g, dy, dres=None):
    T, D = x.shape
    tm = _tile(T, 256, 8)
    has_res = dres is not None

    def body(*refs):
        x_ref, g_ref, dy_ref = refs[:3]
        r_ref = refs[3] if has_res else None
        dx_ref, dg_ref = refs[-2:]
        dx, dg = _rms_bwd_math(x_ref[...], g_ref[...], dy_ref[...].astype(F32))
        if has_res:
            dx = r_ref[...] + dx
        dx_ref[...] = dx

        @pl.when(pl.program_id(0) == 0)
        def _():
            dg_ref[...] = dg

        @pl.when(pl.program_id(0) > 0)
        def _():
            dg_ref[...] += dg

    row = pl.BlockSpec((tm, D), lambda i: (i, 0))
    vec = pl.BlockSpec((1, D), lambda i: (0, 0))
    ins, specs = [x, g.reshape(1, D), dy], [row, vec, row]
    if has_res:
        ins.append(dres)
        specs.append(row)
    dx, dg = pl.pallas_call(
        body, name=name, grid=(T // tm,), in_specs=specs, out_specs=[row, vec],
        out_shape=[jax.ShapeDtypeStruct((T, D), F32), jax.ShapeDtypeStruct((1, D), F32)],
        compiler_params=_params(_nbytes((tm, D), F32) * 4, 6 * _nbytes((tm, D), F32)),
    )(*ins)
    return dx, dg.reshape(D)


def dx_norm_bwd(name, dy, w3, x, g, dres=None, pair_layout=False, after=None):
    T = dy.shape[0]
    _, K, nl = w3.shape
    tm = _tile(T, 512, 8)
    chip = _pair_chip if pair_layout else (lambda j: j)
    has_res = dres is not None

    def body(*refs):
        dy_ref, w_ref, x_ref, g_ref = refs[:4]
        r_ref = refs[4] if has_res else None
        dx_ref, dg_ref, acc = refs[-3:]
        i, k = pl.program_id(0), pl.program_id(1)
        p = lax.dot_general(dy_ref[...].astype(BF16), w_ref[...], NT, preferred_element_type=F32)

        @pl.when(k == 0)
        def _():
            acc[...] = p

        @pl.when(k > 0)
        def _():
            acc[...] += p

        @pl.when(k == N_CHIPS - 1)
        def _():
            dx, dg = _rms_bwd_math(x_ref[...], g_ref[...], acc[...])
            dx_ref[...] = r_ref[...] + dx if has_res else dx

            @pl.when(i == 0)
            def _():
                dg_ref[...] = dg

            @pl.when(i > 0)
            def _():
                dg_ref[...] += dg

    row = pl.BlockSpec((tm, K), lambda i, j: (i, 0))
    vec = pl.BlockSpec((1, K), lambda i, j: (0, 0))
    ins = [dy, w3, x, g.reshape(1, K)]
    specs = [pl.BlockSpec((tm, nl), lambda i, j: (i, j)),
             pl.BlockSpec((None, K, nl), lambda i, j: (chip(j), 0, 0)), row, vec]
    if has_res:
        ins.append(dres)
        specs.append(row)
    if after is not None:
        ins.append(after)
        specs.append(pl.BlockSpec(memory_space=pl.ANY))
    blk = _nbytes((tm, nl), dy.dtype) + _nbytes((K, nl), BF16) + (2 + has_res) * _nbytes((tm, K), F32)
    dx, dg = pl.pallas_call(
        body, name=name, grid=(T // tm, N_CHIPS), in_specs=specs, out_specs=[row, vec],
        out_shape=[jax.ShapeDtypeStruct((T, K), F32), jax.ShapeDtypeStruct((1, K), F32)],
        scratch_shapes=[pltpu.VMEM((tm, K), F32)],
        compiler_params=_params(blk, 8 * _nbytes((tm, K), F32)),
    )(*ins)
    return dx, dg.reshape(K)


def ffn_in_act(name, x, w3):
    T, K = x.shape
    _, _, nl = w3.shape
    tm = _tile(T, 512, 8)

    def body(*refs):
        x_ref, wg_ref, wu_ref = refs[:3]
        u_ref, a_ref = refs[-2:]
        xv = x_ref[...]
        g = jnp.dot(xv, wg_ref[...], preferred_element_type=F32)
        up = jnp.dot(xv, wu_ref[...], preferred_element_type=F32)
        u_ref[:, :nl] = g.astype(u_ref.dtype)
        u_ref[:, nl:] = up.astype(u_ref.dtype)
        a_ref[...] = (g * jax.nn.sigmoid(g) * up).astype(a_ref.dtype)

    blk = _nbytes((tm, K), BF16) + 2 * _nbytes((K, nl), BF16) + _nbytes((tm, 3 * nl), BF16)
    return pl.pallas_call(
        body, name=name, grid=(2, T // tm),
        in_specs=[pl.BlockSpec((tm, K), lambda p, i: (i, 0)),
                  pl.BlockSpec((None, K, nl), lambda p, i: (p, 0, 0)),
                  pl.BlockSpec((None, K, nl), lambda p, i: (p + 2, 0, 0))],
        out_specs=[pl.BlockSpec((tm, 2 * nl), lambda p, i: (i, p)), pl.BlockSpec((tm, nl), lambda p, i: (i, p))],
        out_shape=[jax.ShapeDtypeStruct((T, 4 * nl), BF16), jax.ShapeDtypeStruct((T, 2 * nl), BF16)],
        compiler_params=_params(blk, 4 * _nbytes((tm, nl), F32)),
    )(x, w3, w3)


def ffn_dact(name, dh, w_out, u, after=None):
    T, N = dh.shape
    F = w_out.shape[0]
    nl = F // 2
    tm = _tile(T, 512, 8)

    def body(*refs):
        d_ref, w_ref, u_ref = refs[:3]
        o_ref = refs[-1]
        dact = 0.5 * lax.dot_general(d_ref[...].astype(BF16), w_ref[...], NT, preferred_element_type=F32)
        g = u_ref[:, :nl].astype(F32)
        up = u_ref[:, nl:].astype(F32)
        sig = jax.nn.sigmoid(g)
        o_ref[:, :nl] = (dact * up * (sig * (1.0 + g * (1.0 - sig)))).astype(o_ref.dtype)
        o_ref[:, nl:] = (dact * (g * sig)).astype(o_ref.dtype)

    ins = [dh, w_out, u]
    specs = [pl.BlockSpec((tm, N), lambda p, i: (i, 0)), pl.BlockSpec((nl, N), lambda p, i: (p, 0)),
             pl.BlockSpec((tm, 2 * nl), lambda p, i: (i, p))]
    if after is not None:
        ins.append(after)
        specs.append(pl.BlockSpec(memory_space=pl.ANY))
    blk = _nbytes((tm, N), F32) + _nbytes((nl, N), BF16) + 2 * _nbytes((tm, 2 * nl), BF16)
    return pl.pallas_call(
        body, name=name, grid=(2, T // tm), in_specs=specs,
        out_specs=pl.BlockSpec((tm, 2 * nl), lambda p, i: (i, p)),
        out_shape=jax.ShapeDtypeStruct((T, 2 * F), BF16),
        compiler_params=_params(blk, 6 * _nbytes((tm, nl), F32)),
    )(*ins)


def loss_head(name, h, g, target):
    T, D = h.shape
    tm = _tile(T, 256, 8)

    def body(h_ref, g_ref, t_ref, dh_ref, dg_ref, loss_ref):
        xv = h_ref[...]
        gv = g_ref[...]
        r = lax.rsqrt(jnp.mean(xv * xv, axis=-1, keepdims=True) + EPS)
        err = xv * r * gv - t_ref[...]
        part = 0.5 * jnp.sum(jnp.mean(err * err, axis=-1, keepdims=True), axis=0, keepdims=True)
        dx, dg = _rms_bwd_math(xv, gv, err * (1.0 / D))
        dh_ref[...] = dx
        part = jnp.broadcast_to(part, (1, LANE))

        @pl.when(pl.program_id(0) == 0)
        def _():
            dg_ref[...] = dg
            loss_ref[...] = part

        @pl.when(pl.program_id(0) > 0)
        def _():
            dg_ref[...] += dg
            loss_ref[...] += part

    row = pl.BlockSpec((tm, D), lambda i: (i, 0))
    vec = pl.BlockSpec((1, D), lambda i: (0, 0))
    dh, dg, loss = pl.pallas_call(
        body, name=name, grid=(T // tm,), in_specs=[row, vec, row],
        out_specs=[row, vec, pl.BlockSpec((1, LANE), lambda i: (0, 0))],
        out_shape=[jax.ShapeDtypeStruct((T, D), F32), jax.ShapeDtypeStruct((1, D), F32),
                   jax.ShapeDtypeStruct((1, LANE), F32)],
        compiler_params=_params(_nbytes((tm, D), F32) * 3, 6 * _nbytes((tm, D), F32)),
    )(h, g.reshape(1, D), target)
    return dh, dg.reshape(D), loss


def rope_tables(S):
    half = ROPE // 2
    freqs = ROPE_THETA ** (-jnp.arange(half, dtype=F32) / half)
    ang = jnp.arange(S, dtype=F32)[:, None] * freqs[None, :]
    cos, sin = jnp.cos(ang), jnp.sin(ang)
    z = jnp.zeros_like(cos)
    ct = jnp.concatenate([cos, cos, z, z], axis=1)
    s1 = jnp.concatenate([-sin, z, z, z], axis=1)
    s2 = jnp.concatenate([z, sin, z, z], axis=1)
    return ct, s1, s2


def _rope_tile(t, ct, s1, s2):
    return t * ct + pltpu.roll(t, 96, 1) * s1 + pltpu.roll(t, 32, 1) * s2


def _rope_tile_bwd(d, ct, s1, s2):
    return d * ct + pltpu.roll(d * s1, 32, 1) + pltpu.roll(d * s2, 96, 1)


def uq_rope(name, x, w3, tabs, S):
    T, K = x.shape
    _, _, nl = w3.shape
    tm = _tile(S, 512, 8)
    nt = S // tm

    def body(x_ref, w_ref, ct_ref, s1_ref, s2_ref, o_ref):
        q = jnp.dot(x_ref[...], w_ref[...], preferred_element_type=F32)
        ct, s1, s2 = ct_ref[...], s1_ref[...], s2_ref[...]
        for h in range(nl // 256):
            o_ref[:, 256 * h:256 * h + 128] = q[:, 256 * h:256 * h + 128].astype(o_ref.dtype)
            o_ref[:, 256 * h + 128:256 * h + 256] = _rope_tile(q[:, 256 * h + 128:256 * h + 256],
                                                               ct, s1, s2).astype(o_ref.dtype)

    tab = pl.BlockSpec((tm, LANE), lambda j, i: (i % nt, 0))
    blk = _nbytes((tm, K), BF16) + _nbytes((K, nl), BF16) + _nbytes((tm, nl), BF16) + 3 * _nbytes((tm, LANE), F32)
    return pl.pallas_call(
        body, name=name, grid=(N_CHIPS, T // tm),
        in_specs=[pl.BlockSpec((tm, K), lambda j, i: (i, 0)), pl.BlockSpec((None, K, nl), lambda j, i: (j, 0, 0)),
                  tab, tab, tab],
        out_specs=pl.BlockSpec((tm, nl), lambda j, i: (i, j)),
        out_shape=jax.ShapeDtypeStruct((T, N_CHIPS * nl), BF16),
        compiler_params=_params(blk, 4 * _nbytes((tm, nl), F32)),
    )(x, w3, *tabs)


def kvprep_fwd(name, ckr, g, tabs, B, S):
    T, W = ckr.shape
    KVL = W - LANE
    ts = _tile(S, 256, 8)

    def body(x_ref, g_ref, ct_ref, s1_ref, s2_ref, c_ref, k_ref):
        xv = x_ref[0, :, :KVL]
        r = lax.rsqrt(jnp.mean(xv * xv, axis=-1, keepdims=True) + EPS)
        c_ref[0] = (xv * r * g_ref[...]).astype(c_ref.dtype)
        k_ref[0] = _rope_tile(x_ref[0, :, KVL:], ct_ref[...], s1_ref[...], s2_ref[...]).astype(k_ref.dtype)

    tab = pl.BlockSpec((ts, LANE), lambda b, s: (s, 0))
    c, k = pl.pallas_call(
        body, name=name, grid=(B, S // ts),
        in_specs=[pl.BlockSpec((1, ts, W), lambda b, s: (b, s, 0)), pl.BlockSpec((1, KVL), lambda b, s: (0, 0)),
                  tab, tab, tab],
        out_specs=[pl.BlockSpec((1, ts, KVL), lambda b, s: (b, s, 0)),
                   pl.BlockSpec((1, ts, LANE), lambda b, s: (b, s, 0))],
        out_shape=[jax.ShapeDtypeStruct((B, S, KVL), BF16), jax.ShapeDtypeStruct((B, S, LANE), BF16)],
        compiler_params=_params(_nbytes((ts, W), F32) * 2, _nbytes((ts, W), F32) * 2),
    )(ckr.reshape(B, S, W), g.reshape(1, KVL), *tabs)
    return c.reshape(T, KVL), k


def kvprep_bwd(name, ckr, g, dc, dkr, tabs, B, S):
    T, W = ckr.shape
    KVL = W - LANE
    ts = _tile(S, 256, 8)

    def body(x_ref, g_ref, dc_ref, dk_ref, ct_ref, s1_ref, s2_ref, o_ref, dg_ref):
        dx, dg = _rms_bwd_math(x_ref[0, :, :KVL], g_ref[...], dc_ref[0])
        o_ref[0, :, :KVL] = dx
        o_ref[0, :, KVL:] = _rope_tile_bwd(dk_ref[0], ct_ref[...], s1_ref[...], s2_ref[...])
        first = (pl.program_id(0) == 0) & (pl.program_id(1) == 0)

        @pl.when(first)
        def _():
            dg_ref[...] = dg

        @pl.when(jnp.logical_not(first))
        def _():
            dg_ref[...] += dg

    tab = pl.BlockSpec((ts, LANE), lambda b, s: (s, 0))
    vec = pl.BlockSpec((1, KVL), lambda b, s: (0, 0))
    o, dg = pl.pallas_call(
        body, name=name, grid=(B, S // ts),
        in_specs=[pl.BlockSpec((1, ts, W), lambda b, s: (b, s, 0)), vec,
                  pl.BlockSpec((1, ts, KVL), lambda b, s: (b, s, 0)),
                  pl.BlockSpec((1, ts, LANE), lambda b, s: (b, s, 0)), tab, tab, tab],
        out_specs=[pl.BlockSpec((1, ts, W), lambda b, s: (b, s, 0)), vec],
        out_shape=[jax.ShapeDtypeStruct((B, S, W), F32), jax.ShapeDtypeStruct((1, KVL), F32)],
        compiler_params=_params(_nbytes((ts, W), F32) * 4, _nbytes((ts, W), F32) * 4),
    )(ckr.reshape(B, S, W), g.reshape(1, KVL), dc.reshape(B, S, KVL), dkr, *tabs)
    return o.reshape(T, W), dg.reshape(KVL)


DIAGS = 768


def _diag_onehot():
    col = lax.broadcasted_iota(I32, (REL_PAD, DIAGS), 1)
    row = lax.broadcasted_iota(I32, (REL_PAD, DIAGS), 0)
    idx = jnp.clip(PADR + QROWS - 1 - col, -MAX_REL, MAX_REL) + MAX_REL
    return (row == idx).astype(F32)


def rel_bias_tile(name, table):
    H = table.shape[0]
    tpad = jnp.pad(table, ((0, 0), (0, REL_PAD - table.shape[1])))

    def body(t_ref, o_ref):
        g = lax.dot_general(t_ref[...], _diag_onehot(), NN, precision=lax.Precision.HIGHEST,
                            preferred_element_type=F32)
        qc = jnp.right_shift(lax.broadcasted_iota(I32, (QROWS, WIN), 0), CHUNK_SHIFT)
        kc = jnp.right_shift(lax.broadcasted_iota(I32, (QROWS, WIN), 1), CHUNK_SHIFT)
        band = (kc >= qc) & (kc <= qc + LEFT_CHUNKS)
        for h in range(H):
            gb = jnp.broadcast_to(g[h:h + 1, :], (QROWS, DIAGS))
            tile = pltpu.roll(gb, DIAGS - (QROWS - 1), 1, stride=1, stride_axis=0)
            o_ref[h // 2, (h % 2) * QROWS:(h % 2 + 1) * QROWS, :] = jnp.where(band, tile[:, :WIN], NEG_INF)

    return pl.pallas_call(
        body, name=name, out_shape=jax.ShapeDtypeStruct((H // 2, 2 * QROWS, WIN), F32),
        compiler_params=_params(0, 2 * _nbytes((H // 2, 2 * QROWS, WIN), F32)),
    )(tpad)


def rel_bias_grad(name, dbias):
    H = 2 * dbias.shape[0]

    def body(d_ref, o_ref):
        flip = (lax.broadcasted_iota(I32, (QROWS, QROWS), 0) + lax.broadcasted_iota(I32, (QROWS, QROWS), 1)
                == QROWS - 1).astype(F32)
        rows = []
        for h in range(H):
            x = d_ref[h // 2, (h % 2) * QROWS:(h % 2 + 1) * QROWS, :]
            xr = lax.dot_general(flip, x, NN, precision=lax.Precision.HIGHEST, preferred_element_type=F32)
            xp = jnp.concatenate([xr, jnp.zeros((QROWS, DIAGS - WIN), F32)], axis=1)
            y = pltpu.roll(xp, 0, 1, stride=1, stride_axis=0)
            rows.append(jnp.sum(y, axis=0, keepdims=True))
        o_ref[...] = lax.dot_general(jnp.concatenate(rows, axis=0), _diag_onehot(), NT,
                                     precision=lax.Precision.HIGHEST, preferred_element_type=F32)

    return pl.pallas_call(
        body, name=name, out_shape=jax.ShapeDtypeStruct((H, REL_PAD), F32),
        compiler_params=_params(0, 2 * _nbytes(dbias.shape, F32)),
    )(dbias)


def _stack_pair(xp):
    lane = lax.broadcasted_iota(I32, xp.shape, 1)
    z = jnp.zeros_like(xp)
    return jnp.concatenate([jnp.where(lane < HEAD_DIM_A, xp, z), jnp.where(lane >= HEAD_DIM_A, xp, z)], axis=0)


def _unstack_pair(y):
    lane = lax.broadcasted_iota(I32, (QROWS, LANE), 1)
    return jnp.where(lane < HEAD_DIM_A, y[:QROWS], y[QROWS:])


def _attn_a_rowpen(j):
    w = lax.broadcasted_iota(I32, (1, WIN), 1)
    return jnp.where(w >= PADR - QROWS * j, 0.0, NEG_INF).astype(F32)


def _attn_a_load_bias(bias_hbm, bias_v, sem):
    cp = pltpu.make_async_copy(bias_hbm, bias_v, sem)
    cp.start()
    cp.wait()


def _attn_a_load_kv(qkv_hbm, b, kpad, vpad, sem, S, D):
    kpad[0:PADR, :] = jnp.zeros((PADR, D), BF16)
    vpad[0:PADR, :] = jnp.zeros((PADR, D), BF16)
    ck = pltpu.make_async_copy(qkv_hbm.at[b, :, pl.ds(D, D)], kpad.at[pl.ds(PADR, S), :], sem.at[0])
    cv = pltpu.make_async_copy(qkv_hbm.at[b, :, pl.ds(2 * D, D)], vpad.at[pl.ds(PADR, S), :], sem.at[1])
    ck.start()
    cv.start()
    ck.wait()
    cv.wait()


def _attn_a_exp(q2s, kp, bias, pen):
    s = lax.dot_general(q2s, kp, NT, preferred_element_type=F32) + bias + pen
    e = jnp.exp(s - jnp.max(s, axis=-1, keepdims=True))
    return e, 1.0 / jnp.sum(e, axis=-1, keepdims=True)


def attn_a_fwd(name, qkv, bias):
    B, S, D3 = qkv.shape
    D = D3 // 3
    H = D // HEAD_DIM_A
    nb = S // QROWS
    scale = HEAD_DIM_A ** -0.5

    def body(q_ref, bias_hbm, qkv_hbm, o_ref, kpad, vpad, bias_v, sem):
        b, j = pl.program_id(0), pl.program_id(1)

        @pl.when((b == 0) & (j == 0))
        def _():
            _attn_a_load_bias(bias_hbm, bias_v, sem.at[2])

        @pl.when(j == 0)
        def _():
            _attn_a_load_kv(qkv_hbm, b, kpad, vpad, sem, S, D)

        pen = _attn_a_rowpen(j)
        w0 = pl.multiple_of(j * QROWS, QROWS)
        for p in range(H // 2):
            ls = slice(p * LANE, (p + 1) * LANE)
            e, rl = _attn_a_exp(_stack_pair(q_ref[0, :, ls] * scale), kpad[pl.ds(w0, WIN), ls], bias_v[p], pen)
            o2 = jnp.dot(e.astype(BF16), vpad[pl.ds(w0, WIN), ls], preferred_element_type=F32) * rl
            o_ref[0, :, ls] = _unstack_pair(o2).astype(o_ref.dtype)

    scr = 2 * _nbytes((PADR + S, D), BF16) + _nbytes(bias.shape, F32) + 8 * _nbytes((2 * QROWS, WIN), F32)
    return pl.pallas_call(
        body, name=name, grid=(B, nb),
        in_specs=[pl.BlockSpec((1, QROWS, D), lambda b, j: (b, j, 0)),
                  pl.BlockSpec(memory_space=pl.ANY), pl.BlockSpec(memory_space=pl.ANY)],
        out_specs=pl.BlockSpec((1, QROWS, D), lambda b, j: (b, j, 0)),
        out_shape=jax.ShapeDtypeStruct((B, S, D), BF16),
        scratch_shapes=[pltpu.VMEM((PADR + S, D), BF16), pltpu.VMEM((PADR + S, D), BF16),
                        pltpu.VMEM(bias.shape, F32), pltpu.SemaphoreType.DMA((3,))],
        compiler_params=_params(2 * _nbytes((QROWS, D), BF16), scr),
    )(qkv, bias, qkv)


def attn_a_bwd(name, qkv, do, bias):
    B, S, D3 = qkv.shape
    D = D3 // 3
    H = D // HEAD_DIM_A
    nb = S // QROWS
    scale = HEAD_DIM_A ** -0.5

    def body(q_ref, do_ref, bias_hbm, qkv_hbm, dqkv_hbm, dbias_hbm, kpad, vpad, dkacc, dvacc, bias_v, dbias_v,
             dq_stage, sem):
        b, j = pl.program_id(0), pl.program_id(1)
        step = b * nb + j
        slot = lax.rem(step, 2)

        def dq_out(s):
            return pltpu.make_async_copy(dq_stage.at[s], dqkv_hbm.at[b, pl.ds(j * QROWS, QROWS), pl.ds(0, D)],
                                         sem.at[3 + s])

        @pl.when(step >= 2)
        def _():
            dq_out(slot).wait()

        @pl.when((b == 0) & (j == 0))
        def _():
            _attn_a_load_bias(bias_hbm, bias_v, sem.at[2])
            dbias_v[...] = jnp.zeros_like(dbias_v)

        @pl.when(j == 0)
        def _():
            _attn_a_load_kv(qkv_hbm, b, kpad, vpad, sem, S, D)
            dkacc[...] = jnp.zeros_like(dkacc)
            dvacc[...] = jnp.zeros_like(dvacc)

        pen = _attn_a_rowpen(j)
        w0 = pl.multiple_of(j * QROWS, QROWS)
        for p in range(H // 2):
            ls = slice(p * LANE, (p + 1) * LANE)
            q2s = _stack_pair(q_ref[0, :, ls] * scale)
            do2 = _stack_pair(do_ref[0, :, ls])
            kp = kpad[pl.ds(w0, WIN), ls]
            vp = vpad[pl.ds(w0, WIN), ls]
            e, rl = _attn_a_exp(q2s, kp, bias_v[p], pen)
            pr = e * rl
            dp = lax.dot_general(do2, vp, NT, preferred_element_type=F32)
            ds = pr * (dp - jnp.sum(pr * dp, axis=-1, keepdims=True))
            dbias_v[p] += ds
            dsb = ds.astype(BF16)
            dq_stage[slot, :, ls] = (_unstack_pair(jnp.dot(dsb, kp, preferred_element_type=F32))
                                     * scale).astype(dq_stage.dtype)
            dkacc[pl.ds(w0, WIN), ls] += lax.dot_general(dsb, q2s, TN, preferred_element_type=F32)
            dvacc[pl.ds(w0, WIN), ls] += lax.dot_general(pr.astype(BF16), do2, TN, preferred_element_type=F32)

        dq_out(slot).start()

        @pl.when(j == nb - 1)
        def _():
            kpad[pl.ds(PADR, S), :] = dkacc[pl.ds(PADR, S), :].astype(BF16)
            vpad[pl.ds(PADR, S), :] = dvacc[pl.ds(PADR, S), :].astype(BF16)
            ck = pltpu.make_async_copy(kpad.at[pl.ds(PADR, S), :], dqkv_hbm.at[b, :, pl.ds(D, D)], sem.at[0])
            cv = pltpu.make_async_copy(vpad.at[pl.ds(PADR, S), :], dqkv_hbm.at[b, :, pl.ds(2 * D, D)], sem.at[1])
            ck.start()
            cv.start()
            ck.wait()
            cv.wait()

        @pl.when((b == B - 1) & (j == nb - 1))
        def _():
            cb = pltpu.make_async_copy(dbias_v, dbias_hbm, sem.at[2])
            cb.start()
            dq_out(0).wait()
            dq_out(1).wait()
            cb.wait()

    blk = _nbytes((QROWS, D), BF16) * 2
    scr = (2 * _nbytes((PADR + S, D), BF16) + 2 * _nbytes((PADR + S, D), F32) + 2 * _nbytes(bias.shape, F32)
           + 8 * _nbytes((2 * QROWS, WIN), F32) + 2 * _nbytes((QROWS, D), F32))
    return pl.pallas_call(
        body, name=name, grid=(B, nb),
        in_specs=[pl.BlockSpec((1, QROWS, D), lambda b, j: (b, j, 0)),
                  pl.BlockSpec((1, QROWS, D), lambda b, j: (b, j, 0)),
                  pl.BlockSpec(memory_space=pl.ANY), pl.BlockSpec(memory_space=pl.ANY)],
        out_specs=[pl.BlockSpec(memory_space=pl.ANY), pl.BlockSpec(memory_space=pl.ANY)],
        out_shape=[jax.ShapeDtypeStruct((B, S, 3 * D), BF16), jax.ShapeDtypeStruct(bias.shape, F32)],
        scratch_shapes=[pltpu.VMEM((PADR + S, D), BF16), pltpu.VMEM((PADR + S, D), BF16),
                        pltpu.VMEM((PADR + S, D), F32), pltpu.VMEM((PADR + S, D), F32),
                        pltpu.VMEM(bias.shape, F32), pltpu.VMEM(bias.shape, F32),
                        pltpu.VMEM((2, QROWS, D), BF16), pltpu.SemaphoreType.DMA((5,))],
        compiler_params=_params(blk, scr),
    )(qkv, do, bias, qkv)


def _mla_raw_t(k2, kj, q, QB):
    return lax.dot_general(k2[_blk(kj, QB), :], q, NT, preferred_element_type=F32)


def _blk(kj, QB):
    return pl.ds(kj * QB, QB) if isinstance(kj, int) else pl.ds(pl.multiple_of(kj * QB, QB), QB)


def _mla_diag_pen(QB):
    kc = jnp.right_shift(lax.broadcasted_iota(I32, (QB, QB), 0), CHUNK_SHIFT)
    qc = jnp.right_shift(lax.broadcasted_iota(I32, (QB, QB), 1), CHUNK_SHIFT)
    return jnp.where(kc <= qc, 0.0, NEG_INF).astype(F32)


def _mla_fill_keys(kv_ref, kr_ref, k2):
    k2[:, :NOPE] = kv_ref[0, :, :NOPE]
    k2[:, NOPE:] = kr_ref[0]


def _t(x):
    return x.astype(F32).T


def mla_fwd(name, qf, kv, kr):
    B, S, W = qf.shape
    HB = W // 256
    QB = _tile(S, 256, CHUNK)
    nq = S // QB
    scale = (NOPE + ROPE) ** -0.5

    def body(q_ref, kv_ref, kr_ref, o_ref, lse_ref, k2, vt, st_buf, pen):
        qi = pl.program_id(2)

        @pl.when(qi == 0)
        def _():
            pen[...] = _mla_diag_pen(QB)
            _mla_fill_keys(kv_ref, kr_ref, k2)
            for kj in range(nq):
                vt[kj] = _t(kv_ref[0, kj * QB:(kj + 1) * QB, NOPE:]).astype(BF16)

        q = q_ref[0]
        st_buf[0] = _mla_raw_t(k2, 0, q, QB)

        def step(kj, carry):
            m, l, acc = carry
            cur = lax.rem(kj, 2)
            st_raw = st_buf[cur]
            st_buf[1 - cur] = _mla_raw_t(k2, jnp.minimum(kj + 1, qi), q, QB)
            st = st_raw * scale + jnp.where(kj == qi, pen[...], 0.0)
            m_new = jnp.maximum(m, jnp.max(st, axis=0, keepdims=True))
            a = jnp.exp(m - m_new)
            pt = jnp.exp(st - m_new)
            l = a * l + jnp.sum(pt, axis=0, keepdims=True)
            acc = a * acc + jnp.dot(vt[kj], pt.astype(BF16), preferred_element_type=F32)
            return m_new, l, acc

        init = (jnp.full((1, QB), NEG_INF, F32), jnp.zeros((1, QB), F32), jnp.zeros((NOPE, QB), F32))
        m, l, acc = lax.fori_loop(0, qi + 1, step, init)
        o_ref[0] = (acc * (1.0 / l)).T
        lse_ref[0, 0] = m + jnp.log(l)

    blk = (_nbytes((QB, 256), BF16) + _nbytes((S, 256), BF16) + _nbytes((S, LANE), BF16)
           + _nbytes((QB, LANE), F32))
    return pl.pallas_call(
        body, name=name, grid=(B, HB, nq),
        in_specs=[pl.BlockSpec((1, QB, 256), lambda b, h, i: (b, i, h)),
                  pl.BlockSpec((1, S, 256), lambda b, h, i: (b, 0, h)),
                  pl.BlockSpec((1, S, LANE), lambda b, h, i: (b, 0, 0))],
        out_specs=[pl.BlockSpec((1, QB, LANE), lambda b, h, i: (b, i, h)),
                   pl.BlockSpec((1, 1, 1, QB), lambda b, h, i: (b, h, 0, i))],
        out_shape=[jax.ShapeDtypeStruct((B, S, HB * LANE), F32), jax.ShapeDtypeStruct((B, HB, 1, S), F32)],
        scratch_shapes=[pltpu.VMEM((S, 256), BF16), pltpu.VMEM((nq, NOPE, QB), BF16),
                        pltpu.VMEM((2, QB, QB), F32), pltpu.VMEM((QB, QB), F32)],
        compiler_params=_params(blk, 2 * _nbytes((S, 256), BF16) + 10 * _nbytes((QB, QB), F32)),
    )(qf, kv, kr)


def mla_bwd(name, qf, kv, kr, do, o, lse, tabs):
    B, S, W = qf.shape
    HB = W // 256
    QB = _tile(S, 256, CHUNK)
    nq = S // QB
    scale = (NOPE + ROPE) ** -0.5

    def body(q_ref, kv_ref, kr_ref, do_ref, o_ref, lse_ref, ct_ref, s1_ref, s2_ref, dq_ref, dkv_ref, dkr_ref,
             k2, kt, dot_, delta, dqt, st_buf, dp_buf, pen, dkv_acc):
        h = pl.program_id(1)
        pen[...] = _mla_diag_pen(QB)
        dkv_acc[...] = jnp.zeros_like(dkv_acc)

        @pl.when(h == 0)
        def _():
            dkr_ref[...] = jnp.zeros_like(dkr_ref)

        _mla_fill_keys(kv_ref, kr_ref, k2)
        for i in range(nq):
            rows = slice(i * QB, (i + 1) * QB)
            kt[i] = _t(k2[rows, :]).astype(BF16)
            dot32 = _t(do_ref[0, rows, :])
            delta[i] = jnp.sum(dot32 * o_ref[0, rows, :].T, axis=0, keepdims=True)
            dot_[i] = dot32.astype(BF16)

        for qi in range(nq):
            rows = slice(qi * QB, (qi + 1) * QB)
            q = q_ref[0, rows, :]
            dob = do_ref[0, rows, :]
            lse_q = lse_ref[0, 0, :, rows]
            delta_q = delta[qi]
            dqt[...] = jnp.zeros_like(dqt)

            def raw(kj, slot, q=q, qi=qi):
                st_buf[slot] = _mla_raw_t(k2, kj, q, QB)
                dp_buf[slot] = jnp.dot(kv_ref[0, _blk(kj, QB), NOPE:], dot_[qi], preferred_element_type=F32)

            raw(0, 0)

            def step(kj, carry, q=q, dob=dob, lse_q=lse_q, delta_q=delta_q, qi=qi, raw=raw):
                ks = pl.ds(pl.multiple_of(kj * QB, QB), QB)
                cur = lax.rem(kj, 2)
                st_raw, dp_raw = st_buf[cur], dp_buf[cur]
                raw(jnp.minimum(kj + 1, qi), 1 - cur)
                pt = jnp.exp(st_raw * scale + jnp.where(kj == qi, pen[...], 0.0) - lse_q)
                dst = (pt * (dp_raw - delta_q) * scale).astype(BF16)
                dkv_acc[ks, NOPE:] += jnp.dot(pt.astype(BF16), dob, preferred_element_type=F32)
                dk2 = jnp.dot(dst, q, preferred_element_type=F32)
                dkv_acc[ks, :NOPE] += dk2[:, :NOPE]
                dkr_ref[0, ks, :] += dk2[:, NOPE:]
                dqt[...] += jnp.dot(kt[kj], dst, preferred_element_type=F32)
                return carry

            lax.fori_loop(0, qi + 1, step, 0)
            dq = dqt[...].T
            dq_ref[0, rows, :NOPE] = dq[:, :NOPE].astype(dq_ref.dtype)
            dq_ref[0, rows, NOPE:] = _rope_tile_bwd(dq[:, NOPE:], ct_ref[rows, :], s1_ref[rows, :],
                                                    s2_ref[rows, :]).astype(dq_ref.dtype)

        dkv_ref[0] = dkv_acc[...].astype(dkv_ref.dtype)

    head = lambda w: pl.BlockSpec((1, S, w), lambda b, h: (b, 0, h))
    shared = pl.BlockSpec((1, S, LANE), lambda b, h: (b, 0, 0))
    blk = (2 * _nbytes((S, 256), BF16) + 2 * _nbytes((S, LANE), BF16) + _nbytes((S, LANE), F32)
           + 2 * _nbytes((S, 256), F32) + _nbytes((S, LANE), F32))
    scr = 3 * _nbytes((S, 256), BF16) + 14 * _nbytes((QB, QB), F32)
    return pl.pallas_call(
        body, name=name, grid=(B, HB),
        in_specs=[head(256), head(256), shared, head(LANE), head(LANE),
                  pl.BlockSpec((1, 1, 1, S), lambda b, h: (b, h, 0, 0))]
        + [pl.BlockSpec((S, LANE), lambda b, h: (0, 0))] * 3,
        out_specs=[head(256), head(256), shared],
        out_shape=[jax.ShapeDtypeStruct((B, S, W), BF16), jax.ShapeDtypeStruct((B, S, W), BF16),
                   jax.ShapeDtypeStruct((B, S, LANE), F32)],
        scratch_shapes=[pltpu.VMEM((S, 256), BF16), pltpu.VMEM((nq, 256, QB), BF16),
                        pltpu.VMEM((nq, NOPE, QB), BF16), pltpu.VMEM((nq, 1, QB), F32),
                        pltpu.VMEM((256, QB), F32), pltpu.VMEM((2, QB, QB), F32), pltpu.VMEM((2, QB, QB), F32),
                        pltpu.VMEM((QB, QB), F32), pltpu.VMEM((S, 256), F32)],
        compiler_params=_params(blk, scr),
    )(qf, kv, kr, do, o, lse, *tabs)


GROUP_STEPS = 4


def cast_group(name, ws, layers, idx, after=None):
    n = len(ws)
    n_in = n + (after is not None)

    def body(k_ref, *refs):
        for i in range(n):
            refs[n_in + i][...] = refs[i][...].astype(BF16)

    def spec_in(w, layer):
        return pl.BlockSpec((None, w.shape[1] // GROUP_STEPS, w.shape[2]), lambda r, k_ref: (layer, r, 0))

    def spec_out(w):
        return pl.BlockSpec((None, w.shape[1] // GROUP_STEPS, w.shape[2]), lambda r, k_ref: (k_ref[0], r, 0))

    return pl.pallas_call(
        body, name=name,
        grid_spec=pltpu.PrefetchScalarGridSpec(
            num_scalar_prefetch=1, grid=(GROUP_STEPS,),
            in_specs=([spec_in(w, l) for w, l in zip(ws, layers)]
                      + [pl.BlockSpec(memory_space=pl.ANY)] * (after is not None)),
            out_specs=[spec_out(w) for w in ws]),
        out_shape=[jax.ShapeDtypeStruct((N_CHIPS, *w.shape[1:]), BF16) for w in ws],
        compiler_params=_params(sum(_nbytes(w.shape[1:], F32) * 3 // 2 for w in ws) // GROUP_STEPS),
    )(idx, *ws, *([] if after is None else [after]))


def adamw(name, w, g, m, v):
    R, C = w.shape
    tr = _tile(R, max(8, (1 << 18) // C // 8 * 8), 8)
    c1 = 1.0 - ADAM_B1 ** ADAM_STEP
    c2 = 1.0 - ADAM_B2 ** ADAM_STEP

    def body(w_ref, g_ref, m_ref, v_ref, d_ref, mo_ref, vo_ref):
        gv = g_ref[...]
        mn = ADAM_B1 * m_ref[...] + (1.0 - ADAM_B1) * gv
        vn = ADAM_B2 * v_ref[...] + (1.0 - ADAM_B2) * (gv * gv)
        mo_ref[...] = mn
        vo_ref[...] = vn
        d_ref[...] = -ADAM_LR * ((mn / c1) / (jnp.sqrt(vn / c2) + ADAM_EPS) + ADAM_WD * w_ref[...])

    spec = pl.BlockSpec((tr, C), lambda r: (r, 0))
    return pl.pallas_call(
        body, name=name, grid=(R // tr,), in_specs=[spec] * 4, out_specs=[spec] * 3,
        out_shape=[jax.ShapeDtypeStruct((R, C), F32)] * 3,
        compiler_params=_params(7 * _nbytes((tr, C), F32), 4 * _nbytes((tr, C), F32)),
    )(w, g, m, v)


def half_sum_group(name, dws, landed, idx):
    n = len(dws)
    steps = GROUP_STEPS // 2

    def body(i_ref, *refs):
        for i in range(n):
            refs[2 * n + i][...] = (refs[i][...].astype(F32) + refs[n + i][...].astype(F32)).astype(BF16)

    def own(d):
        return pl.BlockSpec((None, None, d.shape[2] // steps, d.shape[3]), lambda k, r, i_ref: (k, i_ref[1], r, 0))

    def flat(d):
        return pl.BlockSpec((None, d.shape[2] // steps, d.shape[3]), lambda k, r, i_ref: (k, r, 0))

    return pl.pallas_call(
        body, name=name,
        grid_spec=pltpu.PrefetchScalarGridSpec(
            num_scalar_prefetch=1, grid=(N_CHIPS, steps),
            in_specs=[own(d) for d in dws] + [flat(d) for d in dws], out_specs=[flat(d) for d in dws]),
        out_shape=[jax.ShapeDtypeStruct((N_CHIPS, *d.shape[2:]), BF16) for d in dws],
        compiler_params=_params(sum(3 * _nbytes(d.shape[2:], BF16) for d in dws) // steps),
    )(idx, *dws, *landed)


def chip_sum_group(name, parts, landed, gbufs, layers, idx):
    n = len(parts)
    steps = GROUP_STEPS // 2

    def body(i_ref, *refs):
        for i in range(n):
            a, b = refs[i], refs[n + i]
            refs[3 * n + i][...] = ((a[...].astype(F32) + b[0].astype(F32)) + b[1].astype(F32)) + b[2].astype(F32)

    def mine(p):
        return pl.BlockSpec((None, p.shape[1] // steps, p.shape[2]), lambda r, i_ref: (i_ref[0], r, 0))

    def three(p):
        return pl.BlockSpec((3, p.shape[1] // steps, p.shape[2]), lambda r, i_ref: (0, r, 0))

    def out(p, layer):
        return pl.BlockSpec((None, None, p.shape[1] // steps, p.shape[2]), lambda r, i_ref: (layer, i_ref[1], r, 0))

    return pl.pallas_call(
        body, name=name,
        grid_spec=pltpu.PrefetchScalarGridSpec(
            num_scalar_prefetch=1, grid=(steps,),
            in_specs=[mine(p) for p in parts] + [three(p) for p in parts] + [pl.BlockSpec(memory_space=pl.ANY)] * n,
            out_specs=[out(p, l) for p, l in zip(parts, layers)]),
        out_shape=[jax.ShapeDtypeStruct(g.shape, F32) for g in gbufs],
        input_output_aliases={1 + 2 * n + i: i for i in range(n)},
        compiler_params=_params(sum(6 * _nbytes(p.shape[1:], BF16) for p in parts) // steps),
    )(idx, *parts, *landed, *gbufs)


ANY = pl.BlockSpec(memory_space=pl.ANY)


def _place():
    x, y, c = lax.axis_index("x"), lax.axis_index("y"), lax.axis_index("c")
    chips = [(1 - x, y), (x, 1 - y), (1 - x, 1 - y)]
    return x, y, c, chips


HBM = pl.BlockSpec(memory_space=pltpu.HBM)
SEM = pl.BlockSpec(memory_space=pltpu.SEMAPHORE)
EFFECT = pltpu.SideEffectType.DATAFLOW_SIDE_EFFECTING


def _in_hbm(a):
    return pltpu.with_memory_space_constraint(a, pltpu.HBM)


def _ici_copy(src, dst, send_sems, recv_sems, k, peer):
    return pltpu.make_async_remote_copy(src_ref=src, dst_ref=dst, send_sem=send_sems.at[k], recv_sem=recv_sems.at[k],
                                        device_id=peer, device_id_type=MESH)


def ici_start(name, bufs, lands, after, gather):
    n, nl = len(bufs), len(lands)

    def body(*refs):
        b_in = refs[:n]
        send_sems, recv_sems = refs[n + nl + 1], refs[n + nl + 2]
        b_out = refs[n + nl + 3:2 * n + nl + 3]
        l_out = refs[2 * n + nl + 3:2 * n + 2 * nl + 3]
        token = refs[-1]
        x, y, c, chips = _place()
        kme = 2 * x + y
        for i in range(n):
            for j in range(3):
                peer = (*chips[j], c)
                if gather:
                    _ici_copy(b_out[i].at[kme, c], b_out[i].at[kme, c], send_sems, recv_sems, 3 * i + j, peer).start()
                else:
                    kd = 2 * chips[j][0] + chips[j][1]
                    _ici_copy(b_out[i].at[kd], l_out[i].at[j], send_sems, recv_sems, 3 * i + j, peer).start()
        token[...] = jnp.zeros_like(token)

    arrays = [*bufs, *lands]
    outs = pl.pallas_call(
        body, name=name,
        in_specs=[HBM] * (n + nl) + [ANY],
        out_specs=(SEM, SEM, *[HBM] * (n + nl), pl.BlockSpec(memory_space=pltpu.VMEM)),
        out_shape=(pltpu.SemaphoreType.DMA((3 * n,)), pltpu.SemaphoreType.DMA((3 * n,)),
                   *[pltpu.HBM(a.shape, a.dtype) for a in arrays], jax.ShapeDtypeStruct((8, LANE), F32)),
        input_output_aliases={i: 2 + i for i in range(n + nl)},
        compiler_params=pltpu.CompilerParams(has_side_effects=EFFECT),
    )(*[_in_hbm(a) for a in arrays], after)
    return outs[0], outs[1], list(outs[2:2 + n]), list(outs[2 + n:2 + n + nl]), outs[-1]


def ici_wait(name, send_sems, recv_sems, bufs, lands, after, gather):
    n, nl = len(bufs), len(lands)

    def body(*refs):
        b_in, l_in = refs[:n], refs[n:n + nl]
        send_sems, recv_sems = refs[n + nl], refs[n + nl + 1]
        x, y, c, chips = _place()
        kme = 2 * x + y
        for i in range(n):
            for j in range(3):
                peer = (*chips[j], c)
                kj = 2 * chips[j][0] + chips[j][1]
                if gather:
                    _ici_copy(b_in[i].at[kme, c], b_in[i].at[kme, c], send_sems, recv_sems, 3 * i + j, peer).wait_send()
                    _ici_copy(b_in[i].at[kj, c], b_in[i].at[kj, c], send_sems, recv_sems, 3 * i + j, peer).wait_recv()
                else:
                    _ici_copy(b_in[i].at[kj], l_in[i].at[j], send_sems, recv_sems, 3 * i + j, peer).wait_send()
                    _ici_copy(b_in[i].at[kj], l_in[i].at[j], send_sems, recv_sems, 3 * i + j, peer).wait_recv()

    arrays = [*bufs, *lands]
    outs = pl.pallas_call(
        body, name=name,
        in_specs=[HBM] * (n + nl) + [SEM, SEM, ANY],
        out_specs=tuple([HBM] * (n + nl)),
        out_shape=tuple(pltpu.HBM(a.shape, a.dtype) for a in arrays),
        input_output_aliases={i: i for i in range(n + nl)},
        compiler_params=pltpu.CompilerParams(has_side_effects=EFFECT),
    )(*arrays, send_sems, recv_sems, after)
    return list(outs[:n]), list(outs[n:])


def gather_pair_pass(name, bufs):
    n = len(bufs)

    def body(*refs):
        b = refs[n:2 * n]
        send_sems, recv_sems = refs[2 * n:]
        x, y, c, chips = _place()
        sib = (x, y, 1 - c)

        def d2d(i, j, which):
            kj = 2 * chips[j][0] + chips[j][1]
            return _ici_copy(b[i].at[kj, which], b[i].at[kj, which], send_sems, recv_sems, 3 * i + j, sib)

        for i in range(n):
            for j in range(3):
                d2d(i, j, c).start()
        for i in range(n):
            for j in range(3):
                d2d(i, j, 1 - c).wait_recv()
        for i in range(n):
            for j in range(3):
                d2d(i, j, c).wait_send()

    return pl.pallas_call(
        body, name=name, in_specs=[ANY] * n, out_specs=[ANY] * n,
        out_shape=[jax.ShapeDtypeStruct(a.shape, a.dtype) for a in bufs],
        input_output_aliases={i: i for i in range(n)},
        scratch_shapes=[pltpu.SemaphoreType.DMA((3 * n,)), pltpu.SemaphoreType.DMA((3 * n,))],
    )(*bufs)


def pair_exchange(name, dws):
    n = len(dws)

    def body(*refs):
        ins, outs = refs[:n], refs[n:2 * n]
        send_sems, recv_sems = refs[2 * n:]
        x, y, c, _ = _place()
        copies = []
        for i in range(n):
            copies.append(pltpu.make_async_remote_copy(
                src_ref=ins[i].at[:, 1 - c], dst_ref=outs[i],
                send_sem=send_sems.at[i], recv_sem=recv_sems.at[i],
                device_id=(x, y, 1 - c), device_id_type=MESH))
            copies[i].start()
        for cp in copies:
            cp.wait_recv()
        for cp in copies:
            cp.wait_send()

    return pl.pallas_call(
        body, name=name, in_specs=[ANY] * n, out_specs=[ANY] * n,
        out_shape=[jax.ShapeDtypeStruct((N_CHIPS, *d.shape[2:]), d.dtype) for d in dws],
        scratch_shapes=[pltpu.SemaphoreType.DMA((n,)), pltpu.SemaphoreType.DMA((n,))],
    )(*dws)


def pair_assemble(gbufs):
    n = len(gbufs)

    def body(*refs):
        bufs = refs[n:2 * n]
        send_sems, recv_sems = refs[2 * n:]
        x, y, c, _ = _place()
        copies = []
        for i in range(n):
            copies.append(pltpu.make_async_remote_copy(
                src_ref=bufs[i].at[:, c], dst_ref=bufs[i].at[:, c],
                send_sem=send_sems.at[i], recv_sem=recv_sems.at[i],
                device_id=(x, y, 1 - c), device_id_type=MESH))
            copies[i].start()
        for i in range(n):
            pltpu.make_async_remote_copy(
                src_ref=bufs[i].at[:, 1 - c], dst_ref=bufs[i].at[:, 1 - c],
                send_sem=send_sems.at[i], recv_sem=recv_sems.at[i],
                device_id=(x, y, 1 - c), device_id_type=MESH).wait_recv()
        for cp in copies:
            cp.wait_send()

    return pl.pallas_call(
        body, name="grad_pair_assemble", in_specs=[ANY] * n, out_specs=[ANY] * n,
        out_shape=[jax.ShapeDtypeStruct(g.shape, g.dtype) for g in gbufs],
        input_output_aliases={i: i for i in range(n)},
        scratch_shapes=[pltpu.SemaphoreType.DMA((n,)), pltpu.SemaphoreType.DMA((n,))],
    )(*gbufs)


def all_reduce_small(vec):
    NR = vec.shape[0]
    flips = [(fx, fy, fc) for fx in (0, 1) for fy in (0, 1) for fc in (0, 1)][1:]

    def body(v_ref, o_ref, buf, send_sems, recv_sems):
        x, y, c, _ = _place()
        me = 4 * x + 2 * y + c
        buf[me] = v_ref[...]
        copies = []
        for j, (fx, fy, fc) in enumerate(flips):
            peer = (1 - x if fx else x, 1 - y if fy else y, 1 - c if fc else c)
            copies.append(pltpu.make_async_remote_copy(
                src_ref=v_ref, dst_ref=buf.at[me], send_sem=send_sems.at[j], recv_sem=recv_sems.at[j],
                device_id=peer, device_id_type=MESH))
            copies[j].start()
        for cp in copies:
            cp.wait_recv()
        for cp in copies:
            cp.wait_send()
        acc = buf[0]
        for d in range(1, 8):
            acc = acc + buf[d]
        o_ref[...] = acc

    return pl.pallas_call(
        body, name="all_reduce_small",
        in_specs=[pl.BlockSpec(memory_space=pltpu.VMEM)], out_specs=pl.BlockSpec(memory_space=pltpu.VMEM),
        out_shape=jax.ShapeDtypeStruct((NR, LANE), F32),
        scratch_shapes=[pltpu.VMEM((8, NR, LANE), F32), pltpu.SemaphoreType.DMA((7,)),
                        pltpu.SemaphoreType.DMA((7,))],
    )(vec)


def _pack(arrays):
    flat = jnp.concatenate([a.reshape(-1).astype(F32) for a in arrays])
    n = flat.shape[0]
    npad = -(-n // (8 * LANE)) * (8 * LANE)
    return jnp.pad(flat, (0, npad - n)).reshape(npad // LANE, LANE)


def _unpack(buf, like):
    flat = buf.reshape(-1)
    out, off = [], 0
    for a in like:
        out.append(flat[off:off + a.size].reshape(a.shape))
        off += a.size
    return out


def kernel(x, ffn1_norm, ffn1_w_in, ffn1_w_out, mix_norm, ffn2_norm, ffn2_w_in, ffn2_w_out, a_w_qkv, a_rel_bias, a_w_o, kv_norm, kv_w_down, kv_latent_norm, kv_w_up, b_w_dq, b_q_norm, b_w_uq, b_w_o, final_norm, loss_target, m_ffn1_norm, m_ffn1_w_in, m_ffn1_w_out, m_mix_norm, m_ffn2_norm, m_ffn2_w_in, m_ffn2_w_out, m_a_w_qkv, m_a_rel_bias, m_a_w_o, m_kv_norm, m_kv_w_down, m_kv_latent_norm, m_kv_w_up, m_b_w_dq, m_b_q_norm, m_b_w_uq, m_b_w_o, m_final_norm, v_ffn1_norm, v_ffn1_w_in, v_ffn1_w_out, v_mix_norm, v_ffn2_norm, v_ffn2_w_in, v_ffn2_w_out, v_a_w_qkv, v_a_rel_bias, v_a_w_o, v_kv_norm, v_kv_w_down, v_kv_latent_norm, v_kv_w_up, v_b_w_dq, v_b_q_norm, v_b_w_uq, v_b_w_o, v_final_norm):
    B, S, D = x.shape
    T = B * S
    HB = D // 128
    QL = b_q_norm.shape[-1]
    KVL = kv_latent_norm.shape[0]
    hpc = HB // N_CHIPS
    tabs = rope_tables(S)
    idx = jnp.stack([2 * lax.axis_index("x") + lax.axis_index("y"), lax.axis_index("c")]).astype(I32)

    def halves(a):
        return a.reshape(*a.shape[:-2], 2, a.shape[-2] // 2, a.shape[-1])

    def whole(a):
        return a.reshape(*a.shape[:-3], 2 * a.shape[-2], a.shape[-1])

    kv_w_down_p = jnp.pad(kv_w_down, ((0, 0), (0, LANE - ROPE)))[None]
    b_w_uq_p = jnp.pad(b_w_uq.reshape(1, QL, hpc, NOPE + ROPE),
                       ((0, 0), (0, 0), (0, 0), (0, LANE - ROPE))).reshape(1, QL, hpc * 256)
    sharded = [("ffn1_w_in", ffn1_w_in), ("ffn1_w_out", ffn1_w_out), ("ffn2_w_in", ffn2_w_in),
               ("ffn2_w_out", ffn2_w_out), ("a_w_qkv", a_w_qkv), ("a_w_o", a_w_o),
               ("kv_w_down", kv_w_down_p), ("kv_w_up", kv_w_up[None]), ("b_w_dq", b_w_dq),
               ("b_w_uq", b_w_uq_p), ("b_w_o", b_w_o)]
    names = [nm for nm, _ in sharded]
    shard_of = dict(sharded)
    W = {}

    gather_groups = [
        [("ffn1_w_in", 0)],
        [("ffn1_w_out", 0)],
        [("a_w_qkv", 0), ("a_w_o", 0)],
        [("ffn2_w_in", 0), ("ffn2_w_out", 0), ("kv_w_down", 0), ("kv_w_up", 0)],
        [("ffn1_w_in", 1), ("ffn1_w_out", 1), ("b_w_dq", 0), ("b_w_uq", 0), ("b_w_o", 0), ("ffn2_w_in", 1),
         ("ffn2_w_out", 1)]]

    own = {}

    def cast(g, after=None):
        keys = gather_groups[g]
        own.update(zip(keys, cast_group(f"cast_group_{g}", [shard_of[nm] for nm, _ in keys], [l for _, l in keys],
                                        idx, after=after)))

    def gather_start(g, after):
        keys = gather_groups[g]
        ss, rs, bufs, _, token = ici_start(f"gather_start_{g}", [halves(own[k]) for k in keys], [], after, True)
        return (g, ss, rs, bufs), token

    def gather_finish(state, after):
        g, ss, rs, bufs = state
        bufs, _ = ici_wait(f"gather_wait_{g}", ss, rs, bufs, [], after, True)
        full = gather_pair_pass(f"gather_pair_{g}", bufs)
        for k, w in zip(gather_groups[g], full):
            W[k] = whole(w)
        return full[0]

    def tied(a, token):
        return a + token[0, 0]

    def col(nm, l=0):
        return W[(nm, l)]

    def row(nm, l=0):
        w = W[(nm, l)]
        return w.reshape(N_CHIPS * w.shape[1], w.shape[2])

    bias = rel_bias_tile("rel_bias_tile", a_rel_bias[0])

    def ffn_fwd(tag, h, g, w_in, w_out):
        xn = rms_fwd(f"{tag}_norm", h, g)
        u, act = ffn_in_act(f"{tag}_in", xn, w_in)
        return mm_roww(f"{tag}_out", act, w_out, F32, res=h, alpha=0.5), (xn, u, act)

    h0 = x.reshape(T, D)
    for g in range(3):
        cast(g)
    st0, tok0 = gather_start(0, h0)
    st1, tok1 = gather_start(1, tok0)
    st2, tok2 = gather_start(2, tok1)
    for g in range(3, len(gather_groups)):
        cast(g, tok2)
    xn0 = rms_fwd("l0f1_norm", h0, tied(ffn1_norm[0], tok2))
    gather_finish(st0, xn0)
    u0, act0 = ffn_in_act("l0f1_in", xn0, col("ffn1_w_in", 0))
    gather_finish(st1, u0)
    h1 = mm_roww("l0f1_out", act0, row("ffn1_w_out", 0), F32, res=h0, alpha=0.5)
    sv_f1a = (xn0, u0, act0)
    done2 = gather_finish(st2, h1)
    st3, tok3 = gather_start(3, done2)
    st4, tok4 = gather_start(4, tok3)
    hn_a = rms_fwd("l0mix_norm", h1, tied(mix_norm[0], tok4))
    qkv = mm_colw("l0_qkv", hn_a, col("a_w_qkv"), BF16).reshape(B, S, 3 * D)
    o_a = attn_a_fwd("l0_attn", qkv, bias).reshape(T, D)
    h2 = mm_roww("l0_attn_out", o_a, row("a_w_o"), F32, res=h1)
    gather_finish(st3, h2)
    h3, sv_f2a = ffn_fwd("l0f2", h2, ffn2_norm[0], col("ffn2_w_in", 0), row("ffn2_w_out", 0))

    hkv = rms_fwd("kv_norm", h3, kv_norm)
    ckr = mm_roww("kv_down", hkv, row("kv_w_down"), F32)
    ckv, kr = kvprep_fwd("kv_prep", ckr, kv_latent_norm, tabs, B, S)
    kvb = mm_colw("kv_up", ckv, col("kv_w_up"), BF16).reshape(B, S, HB * 256)
    gather_finish(st4, kvb)

    h4, sv_f1b = ffn_fwd("l1f1", h3, ffn1_norm[1], col("ffn1_w_in", 1), row("ffn1_w_out", 1))
    hn_b = rms_fwd("l1mix_norm", h4, mix_norm[1])
    cqp = mm_roww("l1_dq", hn_b, row("b_w_dq"), F32)
    cq = rms_fwd("l1_q_norm", cqp, b_q_norm[0])
    qf = uq_rope("l1_uq", cq, col("b_w_uq"), tabs, S).reshape(B, S, HB * 256)
    o_b, lse = mla_fwd("l1_attn", qf, kvb, kr)
    h5 = mm_roww("l1_attn_out", o_b.reshape(T, HB * LANE), row("b_w_o"), F32, res=h4)
    h6, sv_f2b = ffn_fwd("l1f2", h5, ffn2_norm[1], col("ffn2_w_in", 1), row("ffn2_w_out", 1))

    dh, g_final, loss_part = loss_head("loss_head", h6, final_norm, loss_target.reshape(T, D))

    gw = {}
    gbufs = {nm: lax.empty(halves(w).shape, F32) for nm, w in sharded}

    def reduce_start(r, keys, after):
        dws = [halves(gw[k]) for k in keys]
        landed = pair_exchange(f"grad_pair_exchange_{r}", dws)
        parts = half_sum_group(f"half_sum_{r}", dws, landed, idx)
        lands = [lax.empty((3, *p.shape[1:]), p.dtype) for p in parts]
        ss, rs, parts, lands, token = ici_start(f"reduce_start_{r}", parts, lands, after, False)
        return (r, keys, ss, rs, parts, lands), token

    def reduce_finish(state, after):
        r, keys, ss, rs, parts, lands = state
        parts, lands = ici_wait(f"reduce_wait_{r}", ss, rs, parts, lands, after, False)
        done = chip_sum_group(f"chip_sum_{r}", parts, lands, [gbufs[nm] for nm, _ in keys], [l for _, l in keys], idx)
        gbufs.update(zip([nm for nm, _ in keys], done))
        return done[0]

    def ffn_bwd(tag, dh, h_in, g, w_in, w_out, saved, key_in, key_out, after=None, then=None):
        xn, u, act = saved
        du = ffn_dact(f"{tag}_dact", dh, w_out, u, after=after)
        dwo = mm_droww(f"{tag}_dwout", act, dh, alpha=0.5)
        gw[key_out] = dwo.reshape(N_CHIPS, dwo.shape[0] // N_CHIPS, dwo.shape[1])
        gw[key_in] = mm_dcolw(f"{tag}_dwin", xn, du, pair_layout=True)
        token = then(du) if then is not None else None
        return dx_norm_bwd(f"{tag}_dxn", du, w_in, h_in, g, dres=dh, pair_layout=True, after=token)

    def chip_major(dw):
        return dw.reshape(N_CHIPS, dw.shape[0] // N_CHIPS, dw.shape[1])

    dh, g_f2b = ffn_bwd("l1f2b", dh, h5, ffn2_norm[1], col("ffn2_w_in", 1), row("ffn2_w_out", 1), sv_f2b,
                        ("ffn2_w_in", 1), ("ffn2_w_out", 1))
    red0, rtok0 = reduce_start(0, [("ffn2_w_in", 1), ("ffn2_w_out", 1)], dh)
    do_b = mm_roww_t("l1_attn_do", dh, row("b_w_o"), BF16, after=rtok0).reshape(B, S, HB * LANE)
    gw[("b_w_o", 0)] = chip_major(mm_droww("l1_attn_dwo", o_b.reshape(T, HB * LANE), dh))
    dqpre, dkv, dkr = mla_bwd("l1_attn_bwd", qf, kvb, kr, do_b, o_b, lse, tabs)
    dqpre = dqpre.reshape(T, HB * 256)
    gw[("b_w_uq", 0)] = mm_dcolw("l1_dwuq", cq, dqpre)
    dcqp, g_qn = dx_norm_bwd("l1_dcq", dqpre, col("b_w_uq"), cqp, b_q_norm[0])
    gw[("b_w_dq", 0)] = chip_major(mm_droww("l1_dwdq", hn_b, dcqp))
    dhn = mm_roww_t("l1_dhn", dcqp, row("b_w_dq"), F32)
    dh, g_mixb = rms_bwd("l1_dmix", h4, mix_norm[1], dhn, dres=dh)
    dh, g_f1b = ffn_bwd("l1f1b", dh, h3, ffn1_norm[1], col("ffn1_w_in", 1), row("ffn1_w_out", 1), sv_f1b,
                        ("ffn1_w_in", 1), ("ffn1_w_out", 1))
    fin0 = reduce_finish(red0, dh)
    red1, rtok1 = reduce_start(1, [("b_w_o", 0), ("b_w_uq", 0), ("b_w_dq", 0), ("ffn1_w_in", 1), ("ffn1_w_out", 1)], fin0)
    dkv2 = dkv.reshape(T, HB * 256)
    gw[("kv_w_up", 0)] = mm_dcolw("kv_dwup", ckv, dkv2, after=rtok1)
    dckv = mm_colw_t("kv_dckv", dkv2, col("kv_w_up"), F32, after=rtok1)
    dckr, g_lat = kvprep_bwd("kv_prep_bwd", ckr, kv_latent_norm, dckv, dkr, tabs, B, S)
    gw[("kv_w_down", 0)] = chip_major(mm_droww("kv_dwdown", hkv, dckr))
    dhkv = mm_roww_t("kv_dhkv", dckr, row("kv_w_down"), F32)
    dh, g_kvn = rms_bwd("kv_dnorm", h3, kv_norm, dhkv, dres=dh)
    dh, g_f2a = ffn_bwd("l0f2b", dh, h2, ffn2_norm[0], col("ffn2_w_in", 0), row("ffn2_w_out", 0), sv_f2a,
                        ("ffn2_w_in", 0), ("ffn2_w_out", 0))
    do_a = mm_roww_t("l0_attn_do", dh, row("a_w_o"), BF16).reshape(B, S, D)
    gw[("a_w_o", 0)] = chip_major(mm_droww("l0_attn_dwo", o_a, dh))
    dqkv, dbias = attn_a_bwd("l0_attn_bwd", qkv, do_a, bias)
    dqkv = dqkv.reshape(T, 3 * D)
    gw[("a_w_qkv", 0)] = mm_dcolw("l0_dwqkv", hn_a, dqkv)
    dh, g_mixa = dx_norm_bwd("l0_dhn", dqkv, col("a_w_qkv"), h1, mix_norm[0], dres=dh)
    fin1 = reduce_finish(red1, dh)
    red2, rtok2 = reduce_start(2, [("kv_w_up", 0), ("kv_w_down", 0), ("ffn2_w_in", 0), ("ffn2_w_out", 0),
                                   ("a_w_o", 0), ("a_w_qkv", 0)], fin1)
    last = {}

    def last_group(du):
        fin2 = reduce_finish(red2, gw[("ffn1_w_in", 0)])
        last["red"], token = reduce_start(3, [("ffn1_w_in", 0), ("ffn1_w_out", 0)], fin2)
        return token

    dh, g_f1a = ffn_bwd("l0f1b", dh, h0, ffn1_norm[0], col("ffn1_w_in", 0), row("ffn1_w_out", 0), sv_f1a,
                        ("ffn1_w_in", 0), ("ffn1_w_out", 0), after=rtok2, then=last_group)
    grad_x = dh.reshape(B, S, D)
    g_rel = rel_bias_grad("rel_bias_grad", dbias)[:, :2 * MAX_REL + 1][None]
    reduce_finish(last["red"], dh)

    full = [whole(g) for g in pair_assemble([gbufs[nm] for nm in names])]
    G = {nm: g for (nm, _), g in zip(sharded, full)}
    G["kv_w_down"] = G["kv_w_down"][0, :, :KVL + ROPE]
    G["kv_w_up"] = G["kv_w_up"][0]
    G["b_w_uq"] = G["b_w_uq"].reshape(1, QL, hpc, 256)[..., :NOPE + ROPE].reshape(b_w_uq.shape)

    small = [("ffn1_norm", jnp.stack([g_f1a, g_f1b])), ("mix_norm", jnp.stack([g_mixa, g_mixb])),
             ("ffn2_norm", jnp.stack([g_f2a, g_f2b])), ("a_rel_bias", g_rel), ("kv_norm", g_kvn),
             ("kv_latent_norm", g_lat), ("b_q_norm", g_qn[None]), ("final_norm", g_final)]
    red = all_reduce_small(_pack([loss_part] + [g for _, g in small]))
    unpacked = _unpack(red, [loss_part] + [g for _, g in small])
    loss = unpacked[0][0, 0]
    for (nm, _), g in zip(small, unpacked[1:]):
        G[nm] = g

    given = dict(ffn1_norm=(ffn1_norm, m_ffn1_norm, v_ffn1_norm), ffn1_w_in=(ffn1_w_in, m_ffn1_w_in, v_ffn1_w_in),
                 ffn1_w_out=(ffn1_w_out, m_ffn1_w_out, v_ffn1_w_out), mix_norm=(mix_norm, m_mix_norm, v_mix_norm),
                 ffn2_norm=(ffn2_norm, m_ffn2_norm, v_ffn2_norm), ffn2_w_in=(ffn2_w_in, m_ffn2_w_in, v_ffn2_w_in),
                 ffn2_w_out=(ffn2_w_out, m_ffn2_w_out, v_ffn2_w_out), a_w_qkv=(a_w_qkv, m_a_w_qkv, v_a_w_qkv),
                 a_rel_bias=(a_rel_bias, m_a_rel_bias, v_a_rel_bias), a_w_o=(a_w_o, m_a_w_o, v_a_w_o),
                 kv_norm=(kv_norm, m_kv_norm, v_kv_norm), kv_w_down=(kv_w_down, m_kv_w_down, v_kv_w_down),
                 kv_latent_norm=(kv_latent_norm, m_kv_latent_norm, v_kv_latent_norm),
                 kv_w_up=(kv_w_up, m_kv_w_up, v_kv_w_up), b_w_dq=(b_w_dq, m_b_w_dq, v_b_w_dq),
                 b_q_norm=(b_q_norm, m_b_q_norm, v_b_q_norm), b_w_uq=(b_w_uq, m_b_w_uq, v_b_w_uq),
                 b_w_o=(b_w_o, m_b_w_o, v_b_w_o), final_norm=(final_norm, m_final_norm, v_final_norm))
    order = list(given)
    delta, new_m, new_v = {}, {}, {}
    small_names = [nm for nm, _ in small]
    packed = [_pack([given[nm][k] for nm in small_names]) for k in range(3)]
    outs = adamw("adamw_small", packed[0], _pack([G[nm] for nm in small_names]), packed[1], packed[2])
    for dst, buf in zip((delta, new_m, new_v), outs):
        for nm, a in zip(small_names, _unpack(buf, [given[nm][0] for nm in small_names])):
            dst[nm] = a
    for nm, _ in sharded:
        w, m, v = given[nm]
        g = G[nm].reshape(w.shape)
        G[nm] = g
        two = lambda a: a.reshape(-1, a.shape[-1])
        d_, m_, v_ = adamw(f"adamw_{nm}", two(w), two(g), two(m), two(v))
        delta[nm], new_m[nm], new_v[nm] = d_.reshape(w.shape), m_.reshape(w.shape), v_.reshape(w.shape)

    return (loss, grad_x, *[G[n] for n in order], *[delta[n] for n in order],
            *[new_m[n] for n in order], *[new_v[n] for n in order])
```

```python
import math

import jax
import jax.numpy as jnp
from jax import lax
from jax.experimental import pallas as pl
from jax.experimental.pallas import tpu as pltpu

F32 = jnp.float32
BF16 = jnp.bfloat16
I32 = jnp.int32

CHUNK = 64
CHUNK_SHIFT = 6
HEAD_DIM_A = 64
LEFT_CHUNKS = 8
MAX_REL = 128
REL_PAD = 384
QROWS = 2 * CHUNK
WIN = (LEFT_CHUNKS + 2) * CHUNK
PADR = LEFT_CHUNKS * CHUNK
NOPE = 128
ROPE = 64
EPS = 1e-6
NEG_INF = -1e30
ROPE_THETA = 10000.0
ADAM_LR, ADAM_B1, ADAM_B2, ADAM_EPS, ADAM_WD, ADAM_STEP = 0.001, 0.9, 0.999, 1e-08, 0.01, 10
N_CHIPS = 4
LANE = 128
MESH = pl.DeviceIdType.MESH
VMEM_CAP_MB = 60

NN = (((1,), (0,)), ((), ()))
NT = (((1,), (1,)), ((), ()))
TN = (((0,), (0,)), ((), ()))


def _tile(n, pref, mult):
    t = (min(pref, n) // mult) * mult
    while t >= mult:
        if n % t == 0:
            return t
        t -= mult
    return n


def _nbytes(shape, dtype):
    return math.prod(shape) * jnp.dtype(dtype).itemsize


def _params(block_bytes, extra_bytes=0):
    need = 2 * block_bytes + extra_bytes
    mb = min(VMEM_CAP_MB, max(VMEM_CAP_MB, int(need * 1.25 / 2**20) + 8))
    return pltpu.CompilerParams(vmem_limit_bytes=mb * 2**20)


def _mm(name, kind, a, b, grid, a_spec, b_spec, o_spec, out_shape, out_dtype, blocks,
        red_axis=None, nred=1, alpha=1.0, res=None, res_spec=None, after=None):
    dims = {"nn": NN, "nt": NT, "tn": TN}[kind]
    has_res = res is not None
    acc_in_out = nred > 1 and out_dtype == F32 and not has_res and alpha == 1.0
    n_in = 2 + has_res + (after is not None)

    def body(*refs):
        a_ref, b_ref = refs[0], refs[1]
        r_ref = refs[2] if has_res else None
        o_ref = refs[n_in]
        p = lax.dot_general(a_ref[...].astype(BF16), b_ref[...].astype(BF16), dims,
                            preferred_element_type=F32)

        def finish(acc):
            y = acc if alpha == 1.0 else acc * alpha
            if has_res:
                y = r_ref[...] + y
            o_ref[...] = y.astype(o_ref.dtype)

        if nred == 1:
            finish(p)
            return
        k = pl.program_id(red_axis)
        tgt = o_ref if acc_in_out else refs[-1]

        @pl.when(k == 0)
        def _():
            tgt[...] = p

        @pl.when(k > 0)
        def _():
            tgt[...] += p

        if not acc_in_out:
            @pl.when(k == nred - 1)
            def _():
                finish(tgt[...])

    a_blk, b_blk, o_blk = blocks
    scratch = []
    extra = 0
    if nred > 1 and not acc_in_out:
        scratch = [pltpu.VMEM(o_blk, F32)]
        extra = _nbytes(o_blk, F32)
    blk = _nbytes(a_blk, a.dtype) + _nbytes(b_blk, b.dtype) + _nbytes(o_blk, out_dtype)
    ins, specs = [a, b], [a_spec, b_spec]
    if has_res:
        ins.append(res)
        specs.append(res_spec)
        blk += _nbytes(o_blk, res.dtype)
    if after is not None:
        ins.append(after)
        specs.append(pl.BlockSpec(memory_space=pl.ANY))
    extra += _nbytes(a_blk, BF16) + _nbytes(b_blk, BF16) + 2 * _nbytes(o_blk, F32)
    return pl.pallas_call(
        body, name=name, grid=grid, in_specs=specs, out_specs=o_spec,
        out_shape=jax.ShapeDtypeStruct(out_shape, out_dtype), scratch_shapes=scratch,
        compiler_params=_params(blk, extra),
    )(*ins)


def mm_colw(name, x, w3, out_dtype):
    T, K = x.shape
    _, _, nl = w3.shape
    tm = _tile(T, 512, 8)
    return _mm(name, "nn", x, w3, (N_CHIPS, T // tm),
               pl.BlockSpec((tm, K), lambda j, i: (i, 0)),
               pl.BlockSpec((None, K, nl), lambda j, i: (j, 0, 0)),
               pl.BlockSpec((tm, nl), lambda j, i: (i, j)),
               (T, N_CHIPS * nl), out_dtype, ((tm, K), (K, nl), (tm, nl)))


def _pair_chip(j):
    return (j % 2) * 2 + j // 2


def mm_colw_t(name, dy, w3, out_dtype, res=None, after=None, pair_layout=False):
    T = dy.shape[0]
    _, K, nl = w3.shape
    tm = _tile(T, 1024, 8)
    chip = _pair_chip if pair_layout else (lambda j: j)
    return _mm(name, "nt", dy, w3, (T // tm, N_CHIPS),
               pl.BlockSpec((tm, nl), lambda i, j: (i, j)),
               pl.BlockSpec((None, K, nl), lambda i, j: (chip(j), 0, 0)),
               pl.BlockSpec((tm, K), lambda i, j: (i, 0)),
               (T, K), out_dtype, ((tm, nl), (K, nl), (tm, K)),
               red_axis=1, nred=N_CHIPS, res=res,
               res_spec=pl.BlockSpec((tm, K), lambda i, j: (i, 0)), after=after)


def mm_dcolw(name, x, dy, after=None, pair_layout=False):
    T, K = x.shape
    nl = dy.shape[1] // N_CHIPS
    tt = _tile(T, 2048, 8)
    chip = _pair_chip if pair_layout else (lambda j: j)
    return _mm(name, "tn", x, dy, (N_CHIPS, T // tt),
               pl.BlockSpec((tt, K), lambda j, t: (t, 0)),
               pl.BlockSpec((tt, nl), lambda j, t: (t, j)),
               pl.BlockSpec((None, K, nl), lambda j, t: (chip(j), 0, 0)),
               (N_CHIPS, K, nl), BF16, ((tt, K), (tt, nl), (K, nl)),
               red_axis=1, nred=T // tt, after=after)


def mm_roww(name, x, w2, out_dtype, res=None, alpha=1.0):
    T, Kt = x.shape
    N = w2.shape[1]
    tm = _tile(T, 512, 8)
    return _mm(name, "nn", x, w2, (T // tm,),
               pl.BlockSpec((tm, Kt), lambda i: (i, 0)),
               pl.BlockSpec((Kt, N), lambda i: (0, 0)),
               pl.BlockSpec((tm, N), lambda i: (i, 0)),
               (T, N), out_dtype, ((tm, Kt), (Kt, N), (tm, N)),
               alpha=alpha, res=res, res_spec=pl.BlockSpec((tm, N), lambda i: (i, 0)))


def mm_roww_t(name, dy, w2, out_dtype, alpha=1.0, after=None):
    T, N = dy.shape
    Kt = w2.shape[0]
    tm = _tile(T, 512, 8)
    tk = _tile(Kt, 1408, LANE)
    return _mm(name, "nt", dy, w2, (Kt // tk, T // tm),
               pl.BlockSpec((tm, N), lambda j, i: (i, 0)),
               pl.BlockSpec((tk, N), lambda j, i: (j, 0)),
               pl.BlockSpec((tm, tk), lambda j, i: (i, j)),
               (T, Kt), out_dtype, ((tm, N), (tk, N), (tm, tk)), alpha=alpha, after=after)


def mm_droww(name, x, dy, alpha=1.0):
    T, Kt = x.shape
    N = dy.shape[1]
    tt = _tile(T, 2048, 8)
    tk = _tile(Kt, 1408, LANE)
    return _mm(name, "tn", x, dy, (Kt // tk, T // tt),
               pl.BlockSpec((tt, tk), lambda j, t: (t, j)),
               pl.BlockSpec((tt, N), lambda j, t: (t, 0)),
               pl.BlockSpec((tk, N), lambda j, t: (j, 0)),
               (Kt, N), BF16, ((tt, tk), (tt, N), (tk, N)),
               red_axis=1, nred=T // tt, alpha=alpha)


def rms_fwd(name, x, g):
    T, D = x.shape
    tm = _tile(T, 512, 8)

    def body(x_ref, g_ref, o_ref):
        xv = x_ref[...]
        r = lax.rsqrt(jnp.mean(xv * xv, axis=-1, keepdims=True) + EPS)
        o_ref[...] = (xv * r * g_ref[...]).astype(o_ref.dtype)

    return pl.pallas_call(
        body, name=name, grid=(T // tm,),
        in_specs=[pl.BlockSpec((tm, D), lambda i: (i, 0)), pl.BlockSpec((1, D), lambda i: (0, 0))],
        out_specs=pl.BlockSpec((tm, D), lambda i: (i, 0)),
        out_shape=jax.ShapeDtypeStruct((T, D), BF16),
        compiler_params=_params(_nbytes((tm, D), F32) * 2, 4 * _nbytes((tm, D), F32)),
    )(x, g.reshape(1, D))


def _rms_bwd_math(xv, gv, dy):
    r = lax.rsqrt(jnp.mean(xv * xv, axis=-1, keepdims=True) + EPS)
    xh = xv * r
    dyg = dy * gv
    dx = r * (dyg - xh * jnp.mean(dyg * xh, axis=-1, keepdims=True))
    dg = jnp.sum(dy * xh, axis=0, keepdims=True)
    return dx, dg


def rms_bwd(name, x, g, dy, dres=None):
    T, D = x.shape
    tm = _tile(T, 256, 8)
    has_res = dres is not None

    def body(*refs):
        x_ref, g_ref, dy_ref = refs[:3]
        r_ref = refs[3] if has_res else None
        dx_ref, dg_ref = refs[-2:]
        dx, dg = _rms_bwd_math(x_ref[...], g_ref[...], dy_ref[...].astype(F32))
        if has_res:
            dx = r_ref[...] + dx
        dx_ref[...] = dx

        @pl.when(pl.program_id(0) == 0)
        def _():
            dg_ref[...] = dg

        @pl.when(pl.program_id(0) > 0)
        def _():
            dg_ref[...] += dg

    row = pl.BlockSpec((tm, D), lambda i: (i, 0))
    vec = pl.BlockSpec((1, D), lambda i: (0, 0))
    ins, specs = [x, g.reshape(1, D), dy], [row, vec, row]
    if has_res:
        ins.append(dres)
        specs.append(row)
    dx, dg = pl.pallas_call(
        body, name=name, grid=(T // tm,), in_specs=specs, out_specs=[row, vec],
        out_shape=[jax.ShapeDtypeStruct((T, D), F32), jax.ShapeDtypeStruct((1, D), F32)],
        compiler_params=_params(_nbytes((tm, D), F32) * 4, 6 * _nbytes((tm, D), F32)),
    )(*ins)
    return dx, dg.reshape(D)


def dx_norm_bwd(name, dy, w3, x, g, dres=None, pair_layout=False, after=None):
    T = dy.shape[0]
    _, K, nl = w3.shape
    tm = _tile(T, 512, 8)
    chip = _pair_chip if pair_layout else (lambda j: j)
    has_res = dres is not None

    def body(*refs):
        dy_ref, w_ref, x_ref, g_ref = refs[:4]
        r_ref = refs[4] if has_res else None
        dx_ref, dg_ref, acc = refs[-3:]
        i, k = pl.program_id(0), pl.program_id(1)
        p = lax.dot_general(dy_ref[...].astype(BF16), w_ref[...], NT, preferred_element_type=F32)

        @pl.when(k == 0)
        def _():
            acc[...] = p

        @pl.when(k > 0)
        def _():
            acc[...] += p

        @pl.when(k == N_CHIPS - 1)
        def _():
            dx, dg = _rms_bwd_math(x_ref[...], g_ref[...], acc[...])
            dx_ref[...] = r_ref[...] + dx if has_res else dx

            @pl.when(i == 0)
            def _():
                dg_ref[...] = dg

            @pl.when(i > 0)
            def _():
                dg_ref[...] += dg

    row = pl.BlockSpec((tm, K), lambda i, j: (i, 0))
    vec = pl.BlockSpec((1, K), lambda i, j: (0, 0))
    ins = [dy, w3, x, g.reshape(1, K)]
    specs = [pl.BlockSpec((tm, nl), lambda i, j: (i, j)),
             pl.BlockSpec((None, K, nl), lambda i, j: (chip(j), 0, 0)), row, vec]
    if has_res:
        ins.append(dres)
        specs.append(row)
    if after is not None:
        ins.append(after)
        specs.append(pl.BlockSpec(memory_space=pl.ANY))
    blk = _nbytes((tm, nl), dy.dtype) + _nbytes((K, nl), BF16) + (2 + has_res) * _nbytes((tm, K), F32)
    dx, dg = pl.pallas_call(
        body, name=name, grid=(T // tm, N_CHIPS), in_specs=specs, out_specs=[row, vec],
        out_shape=[jax.ShapeDtypeStruct((T, K), F32), jax.ShapeDtypeStruct((1, K), F32)],
        scratch_shapes=[pltpu.VMEM((tm, K), F32)],
        compiler_params=_params(blk, 8 * _nbytes((tm, K), F32)),
    )(*ins)
    return dx, dg.reshape(K)


def ffn_in_act(name, x, w3):
    T, K = x.shape
    _, _, nl = w3.shape
    tm = _tile(T, 512, 8)

    def body(*refs):
        x_ref, wg_ref, wu_ref = refs[:3]
        u_ref, a_ref = refs[-2:]
        xv = x_ref[...]
        g = jnp.dot(xv, wg_ref[...], preferred_element_type=F32)
        up = jnp.dot(xv, wu_ref[...], preferred_element_type=F32)
        u_ref[:, :nl] = g.astype(u_ref.dtype)
        u_ref[:, nl:] = up.astype(u_ref.dtype)
        a_ref[...] = (g * jax.nn.sigmoid(g) * up).astype(a_ref.dtype)

    blk = _nbytes((tm, K), BF16) + 2 * _nbytes((K, nl), BF16) + _nbytes((tm, 3 * nl), BF16)
    return pl.pallas_call(
        body, name=name, grid=(2, T // tm),
        in_specs=[pl.BlockSpec((tm, K), lambda p, i: (i, 0)),
                  pl.BlockSpec((None, K, nl), lambda p, i: (p, 0, 0)),
                  pl.BlockSpec((None, K, nl), lambda p, i: (p + 2, 0, 0))],
        out_specs=[pl.BlockSpec((tm, 2 * nl), lambda p, i: (i, p)), pl.BlockSpec((tm, nl), lambda p, i: (i, p))],
        out_shape=[jax.ShapeDtypeStruct((T, 4 * nl), BF16), jax.ShapeDtypeStruct((T, 2 * nl), BF16)],
        compiler_params=_params(blk, 4 * _nbytes((tm, nl), F32)),
    )(x, w3, w3)


def ffn_dact(name, dh, w_out, u, after=None):
    T, N = dh.shape
    F = w_out.shape[0]
    nl = F // 2
    tm = _tile(T, 512, 8)

    def body(*refs):
        d_ref, w_ref, u_ref = refs[:3]
        o_ref = refs[-1]
        dact = 0.5 * lax.dot_general(d_ref[...].astype(BF16), w_ref[...], NT, preferred_element_type=F32)
        g = u_ref[:, :nl].astype(F32)
        up = u_ref[:, nl:].astype(F32)
        sig = jax.nn.sigmoid(g)
        o_ref[:, :nl] = (dact * up * (sig * (1.0 + g * (1.0 - sig)))).astype(o_ref.dtype)
        o_ref[:, nl:] = (dact * (g * sig)).astype(o_ref.dtype)

    ins = [dh, w_out, u]
    specs = [pl.BlockSpec((tm, N), lambda p, i: (i, 0)), pl.BlockSpec((nl, N), lambda p, i: (p, 0)),
             pl.BlockSpec((tm, 2 * nl), lambda p, i: (i, p))]
    if after is not None:
        ins.append(after)
        specs.append(pl.BlockSpec(memory_space=pl.ANY))
    blk = _nbytes((tm, N), F32) + _nbytes((nl, N), BF16) + 2 * _nbytes((tm, 2 * nl), BF16)
    return pl.pallas_call(
        body, name=name, grid=(2, T // tm), in_specs=specs,
        out_specs=pl.BlockSpec((tm, 2 * nl), lambda p, i: (i, p)),
        out_shape=jax.ShapeDtypeStruct((T, 2 * F), BF16),
        compiler_params=_params(blk, 6 * _nbytes((tm, nl), F32)),
    )(*ins)


def loss_head(name, h, g, target):
    T, D = h.shape
    tm = _tile(T, 256, 8)

    def body(h_ref, g_ref, t_ref, dh_ref, dg_ref, loss_ref):
        xv = h_ref[...]
        gv = g_ref[...]
        r = lax.rsqrt(jnp.mean(xv * xv, axis=-1, keepdims=True) + EPS)
        err = xv * r * gv - t_ref[...]
        part = 0.5 * jnp.sum(jnp.mean(err * err, axis=-1, keepdims=True), axis=0, keepdims=True)
        dx, dg = _rms_bwd_math(xv, gv, err * (1.0 / D))
        dh_ref[...] = dx
        part = jnp.broadcast_to(part, (1, LANE))

        @pl.when(pl.program_id(0) == 0)
        def _():
            dg_ref[...] = dg
            loss_ref[...] = part

        @pl.when(pl.program_id(0) > 0)
        def _():
            dg_ref[...] += dg
            loss_ref[...] += part

    row = pl.BlockSpec((tm, D), lambda i: (i, 0))
    vec = pl.BlockSpec((1, D), lambda i: (0, 0))
    dh, dg, loss = pl.pallas_call(
        body, name=name, grid=(T // tm,), in_specs=[row, vec, row],
        out_specs=[row, vec, pl.BlockSpec((1, LANE), lambda i: (0, 0))],
        out_shape=[jax.ShapeDtypeStruct((T, D), F32), jax.ShapeDtypeStruct((1, D), F32),
                   jax.ShapeDtypeStruct((1, LANE), F32)],
        compiler_params=_params(_nbytes((tm, D), F32) * 3, 6 * _nbytes((tm, D), F32)),
    )(h, g.reshape(1, D), target)
    return dh, dg.reshape(D), loss


def rope_tables(S):
    half = ROPE // 2
    freqs = ROPE_THETA ** (-jnp.arange(half, dtype=F32) / half)
    ang = jnp.arange(S, dtype=F32)[:, None] * freqs[None, :]
    cos, sin = jnp.cos(ang), jnp.sin(ang)
    z = jnp.zeros_like(cos)
    ct = jnp.concatenate([cos, cos, z, z], axis=1)
    s1 = jnp.concatenate([-sin, z, z, z], axis=1)
    s2 = jnp.concatenate([z, sin, z, z], axis=1)
    return ct, s1, s2


def _rope_tile(t, ct, s1, s2):
    return t * ct + pltpu.roll(t, 96, 1) * s1 + pltpu.roll(t, 32, 1) * s2


def _rope_tile_bwd(d, ct, s1, s2):
    return d * ct + pltpu.roll(d * s1, 32, 1) + pltpu.roll(d * s2, 96, 1)


def uq_rope(name, x, w3, tabs, S):
    T, K = x.shape
    _, _, nl = w3.shape
    tm = _tile(S, 512, 8)
    nt = S // tm

    def body(x_ref, w_ref, ct_ref, s1_ref, s2_ref, o_ref):
        q = jnp.dot(x_ref[...], w_ref[...], preferred_element_type=F32)
        ct, s1, s2 = ct_ref[...], s1_ref[...], s2_ref[...]
        for h in range(nl // 256):
            o_ref[:, 256 * h:256 * h + 128] = q[:, 256 * h:256 * h + 128].astype(o_ref.dtype)
            o_ref[:, 256 * h + 128:256 * h + 256] = _rope_tile(q[:, 256 * h + 128:256 * h + 256],
                                                               ct, s1, s2).astype(o_ref.dtype)

    tab = pl.BlockSpec((tm, LANE), lambda j, i: (i % nt, 0))
    blk = _nbytes((tm, K), BF16) + _nbytes((K, nl), BF16) + _nbytes((tm, nl), BF16) + 3 * _nbytes((tm, LANE), F32)
    return pl.pallas_call(
        body, name=name, grid=(N_CHIPS, T // tm),
        in_specs=[pl.BlockSpec((tm, K), lambda j, i: (i, 0)), pl.BlockSpec((None, K, nl), lambda j, i: (j, 0, 0)),
                  tab, tab, tab],
        out_specs=pl.BlockSpec((tm, nl), lambda j, i: (i, j)),
        out_shape=jax.ShapeDtypeStruct((T, N_CHIPS * nl), BF16),
        compiler_params=_params(blk, 4 * _nbytes((tm, nl), F32)),
    )(x, w3, *tabs)


def kvprep_fwd(name, ckr, g, tabs, B, S):
    T, W = ckr.shape
    KVL = W - LANE
    ts = _tile(S, 256, 8)

    def body(x_ref, g_ref, ct_ref, s1_ref, s2_ref, c_ref, k_ref):
        xv = x_ref[0, :, :KVL]
        r = lax.rsqrt(jnp.mean(xv * xv, axis=-1, keepdims=True) + EPS)
        c_ref[0] = (xv * r * g_ref[...]).astype(c_ref.dtype)
        k_ref[0] = _rope_tile(x_ref[0, :, KVL:], ct_ref[...], s1_ref[...], s2_ref[...]).astype(k_ref.dtype)

    tab = pl.BlockSpec((ts, LANE), lambda b, s: (s, 0))
    c, k = pl.pallas_call(
        body, name=name, grid=(B, S // ts),
        in_specs=[pl.BlockSpec((1, ts, W), lambda b, s: (b, s, 0)), pl.BlockSpec((1, KVL), lambda b, s: (0, 0)),
                  tab, tab, tab],
        out_specs=[pl.BlockSpec((1, ts, KVL), lambda b, s: (b, s, 0)),
                   pl.BlockSpec((1, ts, LANE), lambda b, s: (b, s, 0))],
        out_shape=[jax.ShapeDtypeStruct((B, S, KVL), BF16), jax.ShapeDtypeStruct((B, S, LANE), BF16)],
        compiler_params=_params(_nbytes((ts, W), F32) * 2, _nbytes((ts, W), F32) * 2),
    )(ckr.reshape(B, S, W), g.reshape(1, KVL), *tabs)
    return c.reshape(T, KVL), k


def kvprep_bwd(name, ckr, g, dc, dkr, tabs, B, S):
    T, W = ckr.shape
    KVL = W - LANE
    ts = _tile(S, 256, 8)

    def body(x_ref, g_ref, dc_ref, dk_ref, ct_ref, s1_ref, s2_ref, o_ref, dg_ref):
        dx, dg = _rms_bwd_math(x_ref[0, :, :KVL], g_ref[...], dc_ref[0])
        o_ref[0, :, :KVL] = dx
        o_ref[0, :, KVL:] = _rope_tile_bwd(dk_ref[0], ct_ref[...], s1_ref[...], s2_ref[...])
        first = (pl.program_id(0) == 0) & (pl.program_id(1) == 0)

        @pl.when(first)
        def _():
            dg_ref[...] = dg

        @pl.when(jnp.logical_not(first))
        def _():
            dg_ref[...] += dg

    tab = pl.BlockSpec((ts, LANE), lambda b, s: (s, 0))
    vec = pl.BlockSpec((1, KVL), lambda b, s: (0, 0))
    o, dg = pl.pallas_call(
        body, name=name, grid=(B, S // ts),
        in_specs=[pl.BlockSpec((1, ts, W), lambda b, s: (b, s, 0)), vec,
                  pl.BlockSpec((1, ts, KVL), lambda b, s: (b, s, 0)),
                  pl.BlockSpec((1, ts, LANE), lambda b, s: (b, s, 0)), tab, tab, tab],
        out_specs=[pl.BlockSpec((1, ts, W), lambda b, s: (b, s, 0)), vec],
        out_shape=[jax.ShapeDtypeStruct((B, S, W), F32), jax.ShapeDtypeStruct((1, KVL), F32)],
        compiler_params=_params(_nbytes((ts, W), F32) * 4, _nbytes((ts, W), F32) * 4),
    )(ckr.reshape(B, S, W), g.reshape(1, KVL), dc.reshape(B, S, KVL), dkr, *tabs)
    return o.reshape(T, W), dg.reshape(KVL)


DIAGS = 768


def _diag_onehot():
    col = lax.broadcasted_iota(I32, (REL_PAD, DIAGS), 1)
    row = lax.broadcasted_iota(I32, (REL_PAD, DIAGS), 0)
    idx = jnp.clip(PADR + QROWS - 1 - col, -MAX_REL, MAX_REL) + MAX_REL
    return (row == idx).astype(F32)


def rel_bias_tile(name, table):
    H = table.shape[0]
    tpad = jnp.pad(table, ((0, 0), (0, REL_PAD - table.shape[1])))

    def body(t_ref, o_ref):
        g = lax.dot_general(t_ref[...], _diag_onehot(), NN, precision=lax.Precision.HIGHEST,
                            preferred_element_type=F32)
        qc = jnp.right_shift(lax.broadcasted_iota(I32, (QROWS, WIN), 0), CHUNK_SHIFT)
        kc = jnp.right_shift(lax.broadcasted_iota(I32, (QROWS, WIN), 1), CHUNK_SHIFT)
        band = (kc >= qc) & (kc <= qc + LEFT_CHUNKS)
        for h in range(H):
            gb = jnp.broadcast_to(g[h:h + 1, :], (QROWS, DIAGS))
            tile = pltpu.roll(gb, DIAGS - (QROWS - 1), 1, stride=1, stride_axis=0)
            o_ref[h // 2, (h % 2) * QROWS:(h % 2 + 1) * QROWS, :] = jnp.where(band, tile[:, :WIN], NEG_INF)

    return pl.pallas_call(
        body, name=name, out_shape=jax.ShapeDtypeStruct((H // 2, 2 * QROWS, WIN), F32),
        compiler_params=_params(0, 2 * _nbytes((H // 2, 2 * QROWS, WIN), F32)),
    )(tpad)


def rel_bias_grad(name, dbias):
    H = 2 * dbias.shape[0]

    def body(d_ref, o_ref):
        flip = (lax.broadcasted_iota(I32, (QROWS, QROWS), 0) + lax.broadcasted_iota(I32, (QROWS, QROWS), 1)
                == QROWS - 1).astype(F32)
        rows = []
        for h in range(H):
            x = d_ref[h // 2, (h % 2) * QROWS:(h % 2 + 1) * QROWS, :]
            xr = lax.dot_general(flip, x, NN, precision=lax.Precision.HIGHEST, preferred_element_type=F32)
            xp = jnp.concatenate([xr, jnp.zeros((QROWS, DIAGS - WIN), F32)], axis=1)
            y = pltpu.roll(xp, 0, 1, stride=1, stride_axis=0)
            rows.append(jnp.sum(y, axis=0, keepdims=True))
        o_ref[...] = lax.dot_general(jnp.concatenate(rows, axis=0), _diag_onehot(), NT,
                                     precision=lax.Precision.HIGHEST, preferred_element_type=F32)

    return pl.pallas_call(
        body, name=name, out_shape=jax.ShapeDtypeStruct((H, REL_PAD), F32),
        compiler_params=_params(0, 2 * _nbytes(dbias.shape, F32)),
    )(dbias)


def _stack_pair(xp):
    lane = lax.broadcasted_iota(I32, xp.shape, 1)
    z = jnp.zeros_like(xp)
    return jnp.concatenate([jnp.where(lane < HEAD_DIM_A, xp, z), jnp.where(lane >= HEAD_DIM_A, xp, z)], axis=0)


def _unstack_pair(y):
    lane = lax.broadcasted_iota(I32, (QROWS, LANE), 1)
    return jnp.where(lane < HEAD_DIM_A, y[:QROWS], y[QROWS:])


def _attn_a_rowpen(j):
    w = lax.broadcasted_iota(I32, (1, WIN), 1)
    return jnp.where(w >= PADR - QROWS * j, 0.0, NEG_INF).astype(F32)


def _attn_a_load_bias(bias_hbm, bias_v, sem):
    cp = pltpu.make_async_copy(bias_hbm, bias_v, sem)
    cp.start()
    cp.wait()


def _attn_a_load_kv(qkv_hbm, b, kpad, vpad, sem, S, D):
    kpad[0:PADR, :] = jnp.zeros((PADR, D), BF16)
    vpad[0:PADR, :] = jnp.zeros((PADR, D), BF16)
    ck = pltpu.make_async_copy(qkv_hbm.at[b, :, pl.ds(D, D)], kpad.at[pl.ds(PADR, S), :], sem.at[0])
    cv = pltpu.make_async_copy(qkv_hbm.at[b, :, pl.ds(2 * D, D)], vpad.at[pl.ds(PADR, S), :], sem.at[1])
    ck.start()
    cv.start()
    ck.wait()
    cv.wait()


def _attn_a_exp(q2s, kp, bias, pen):
    s = lax.dot_general(q2s, kp, NT, preferred_element_type=F32) + bias + pen
    e = jnp.exp(s - jnp.max(s, axis=-1, keepdims=True))
    return e, 1.0 / jnp.sum(e, axis=-1, keepdims=True)


def attn_a_fwd(name, qkv, bias):
    B, S, D3 = qkv.shape
    D = D3 // 3
    H = D // HEAD_DIM_A
    nb = S // QROWS
    scale = HEAD_DIM_A ** -0.5

    def body(q_ref, bias_hbm, qkv_hbm, o_ref, kpad, vpad, bias_v, sem):
        b, j = pl.program_id(0), pl.program_id(1)

        @pl.when((b == 0) & (j == 0))
        def _():
            _attn_a_load_bias(bias_hbm, bias_v, sem.at[2])

        @pl.when(j == 0)
        def _():
            _attn_a_load_kv(qkv_hbm, b, kpad, vpad, sem, S, D)

        pen = _attn_a_rowpen(j)
        w0 = pl.multiple_of(j * QROWS, QROWS)
        for p in range(H // 2):
            ls = slice(p * LANE, (p + 1) * LANE)
            e, rl = _attn_a_exp(_stack_pair(q_ref[0, :, ls] * scale), kpad[pl.ds(w0, WIN), ls], bias_v[p], pen)
            o2 = jnp.dot(e.astype(BF16), vpad[pl.ds(w0, WIN), ls], preferred_element_type=F32) * rl
            o_ref[0, :, ls] = _unstack_pair(o2).astype(o_ref.dtype)

    scr = 2 * _nbytes((PADR + S, D), BF16) + _nbytes(bias.shape, F32) + 8 * _nbytes((2 * QROWS, WIN), F32)
    return pl.pallas_call(
        body, name=name, grid=(B, nb),
        in_specs=[pl.BlockSpec((1, QROWS, D), lambda b, j: (b, j, 0)),
                  pl.BlockSpec(memory_space=pl.ANY), pl.BlockSpec(memory_space=pl.ANY)],
        out_specs=pl.BlockSpec((1, QROWS, D), lambda b, j: (b, j, 0)),
        out_shape=jax.ShapeDtypeStruct((B, S, D), BF16),
        scratch_shapes=[pltpu.VMEM((PADR + S, D), BF16), pltpu.VMEM((PADR + S, D), BF16),
                        pltpu.VMEM(bias.shape, F32), pltpu.SemaphoreType.DMA((3,))],
        compiler_params=_params(2 * _nbytes((QROWS, D), BF16), scr),
    )(qkv, bias, qkv)


def attn_a_bwd(name, qkv, do, bias):
    B, S, D3 = qkv.shape
    D = D3 // 3
    H = D // HEAD_DIM_A
    nb = S // QROWS
    scale = HEAD_DIM_A ** -0.5

    def body(q_ref, do_ref, bias_hbm, qkv_hbm, dqkv_hbm, dbias_hbm, kpad, vpad, dkacc, dvacc, bias_v, dbias_v,
             dq_stage, sem):
        b, j = pl.program_id(0), pl.program_id(1)
        step = b * nb + j
        slot = lax.rem(step, 2)

        def dq_out(s):
            return pltpu.make_async_copy(dq_stage.at[s], dqkv_hbm.at[b, pl.ds(j * QROWS, QROWS), pl.ds(0, D)],
                                         sem.at[3 + s])

        @pl.when(step >= 2)
        def _():
            dq_out(slot).wait()

        @pl.when((b == 0) & (j == 0))
        def _():
            _attn_a_load_bias(bias_hbm, bias_v, sem.at[2])
            dbias_v[...] = jnp.zeros_like(dbias_v)

        @pl.when(j == 0)
        def _():
            _attn_a_load_kv(qkv_hbm, b, kpad, vpad, sem, S, D)
            dkacc[...] = jnp.zeros_like(dkacc)
            dvacc[...] = jnp.zeros_like(dvacc)

        pen = _attn_a_rowpen(j)
        w0 = pl.multiple_of(j * QROWS, QROWS)
        for p in range(H // 2):
            ls = slice(p * LANE, (p + 1) * LANE)
            q2s = _stack_pair(q_ref[0, :, ls] * scale)
            do2 = _stack_pair(do_ref[0, :, ls])
            kp = kpad[pl.ds(w0, WIN), ls]
            vp = vpad[pl.ds(w0, WIN), ls]
            e, rl = _attn_a_exp(q2s, kp, bias_v[p], pen)
            pr = e * rl
            dp = lax.dot_general(do2, vp, NT, preferred_element_type=F32)
            ds = pr * (dp - jnp.sum(pr * dp, axis=-1, keepdims=True))
            dbias_v[p] += ds
            dsb = ds.astype(BF16)
            dq_stage[slot, :, ls] = (_unstack_pair(jnp.dot(dsb, kp, preferred_element_type=F32))
                                     * scale).astype(dq_stage.dtype)
            dkacc[pl.ds(w0, WIN), ls] += lax.dot_general(dsb, q2s, TN, preferred_element_type=F32)
            dvacc[pl.ds(w0, WIN), ls] += lax.dot_general(pr.astype(BF16), do2, TN, preferred_element_type=F32)

        dq_out(slot).start()

        @pl.when(j == nb - 1)
        def _():
            kpad[pl.ds(PADR, S), :] = dkacc[pl.ds(PADR, S), :].astype(BF16)
            vpad[pl.ds(PADR, S), :] = dvacc[pl.ds(PADR, S), :].astype(BF16)
            ck = pltpu.make_async_copy(kpad.at[pl.ds(PADR, S), :], dqkv_hbm.at[b, :, pl.ds(D, D)], sem.at[0])
            cv = pltpu.make_async_copy(vpad.at[pl.ds(PADR, S), :], dqkv_hbm.at[b, :, pl.ds(2 * D, D)], sem.at[1])
            ck.start()
            cv.start()
            ck.wait()
            cv.wait()

        @pl.when((b == B - 1) & (j == nb - 1))
        def _():
            cb = pltpu.make_async_copy(dbias_v, dbias_hbm, sem.at[2])
            cb.start()
            dq_out(0).wait()
            dq_out(1).wait()
            cb.wait()

    blk = _nbytes((QROWS, D), BF16) * 2
    scr = (2 * _nbytes((PADR + S, D), BF16) + 2 * _nbytes((PADR + S, D), F32) + 2 * _nbytes(bias.shape, F32)
           + 8 * _nbytes((2 * QROWS, WIN), F32) + 2 * _nbytes((QROWS, D), F32))
    return pl.pallas_call(
        body, name=name, grid=(B, nb),
        in_specs=[pl.BlockSpec((1, QROWS, D), lambda b, j: (b, j, 0)),
                  pl.BlockSpec((1, QROWS, D), lambda b, j: (b, j, 0)),
                  pl.BlockSpec(memory_space=pl.ANY), pl.BlockSpec(memory_space=pl.ANY)],
        out_specs=[pl.BlockSpec(memory_space=pl.ANY), pl.BlockSpec(memory_space=pl.ANY)],
        out_shape=[jax.ShapeDtypeStruct((B, S, 3 * D), BF16), jax.ShapeDtypeStruct(bias.shape, F32)],
        scratch_shapes=[pltpu.VMEM((PADR + S, D), BF16), pltpu.VMEM((PADR + S, D), BF16),
                        pltpu.VMEM((PADR + S, D), F32), pltpu.VMEM((PADR + S, D), F32),
                        pltpu.VMEM(bias.shape, F32), pltpu.VMEM(bias.shape, F32),
                        pltpu.VMEM((2, QROWS, D), BF16), pltpu.SemaphoreType.DMA((5,))],
        compiler_params=_params(blk, scr),
    )(qkv, do, bias, qkv)


def _mla_raw_t(k2, kj, q, QB):
    return lax.dot_general(k2[_blk(kj, QB), :], q, NT, preferred_element_type=F32)


def _blk(kj, QB):
    return pl.ds(kj * QB, QB) if isinstance(kj, int) else pl.ds(pl.multiple_of(kj * QB, QB), QB)


def _mla_diag_pen(QB):
    kc = jnp.right_shift(lax.broadcasted_iota(I32, (QB, QB), 0), CHUNK_SHIFT)
    qc = jnp.right_shift(lax.broadcasted_iota(I32, (QB, QB), 1), CHUNK_SHIFT)
    return jnp.where(kc <= qc, 0.0, NEG_INF).astype(F32)


def _mla_fill_keys(kv_ref, kr_ref, k2):
    k2[:, :NOPE] = kv_ref[0, :, :NOPE]
    k2[:, NOPE:] = kr_ref[0]


def _t(x):
    return x.astype(F32).T


def mla_fwd(name, qf, kv, kr):
    B, S, W = qf.shape
    HB = W // 256
    QB = _tile(S, 256, CHUNK)
    nq = S // QB
    scale = (NOPE + ROPE) ** -0.5

    def body(q_ref, kv_ref, kr_ref, o_ref, lse_ref, k2, vt, st_buf, pen):
        qi = pl.program_id(2)

        @pl.when(qi == 0)
        def _():
            pen[...] = _mla_diag_pen(QB)
            _mla_fill_keys(kv_ref, kr_ref, k2)
            for kj in range(nq):
                vt[kj] = _t(kv_ref[0, kj * QB:(kj + 1) * QB, NOPE:]).astype(BF16)

        q = q_ref[0]
        st_buf[0] = _mla_raw_t(k2, 0, q, QB)

        def step(kj, carry):
            m, l, acc = carry
            cur = lax.rem(kj, 2)
            st_raw = st_buf[cur]
            st_buf[1 - cur] = _mla_raw_t(k2, jnp.minimum(kj + 1, qi), q, QB)
            st = st_raw * scale + jnp.where(kj == qi, pen[...], 0.0)
            m_new = jnp.maximum(m, jnp.max(st, axis=0, keepdims=True))
            a = jnp.exp(m - m_new)
            pt = jnp.exp(st - m_new)
            l = a * l + jnp.sum(pt, axis=0, keepdims=True)
            acc = a * acc + jnp.dot(vt[kj], pt.astype(BF16), preferred_element_type=F32)
            return m_new, l, acc

        init = (jnp.full((1, QB), NEG_INF, F32), jnp.zeros((1, QB), F32), jnp.zeros((NOPE, QB), F32))
        m, l, acc = lax.fori_loop(0, qi + 1, step, init)
        o_ref[0] = (acc * (1.0 / l)).T
        lse_ref[0, 0] = m + jnp.log(l)

    blk = (_nbytes((QB, 256), BF16) + _nbytes((S, 256), BF16) + _nbytes((S, LANE), BF16)
           + _nbytes((QB, LANE), F32))
    return pl.pallas_call(
        body, name=name, grid=(B, HB, nq),
        in_specs=[pl.BlockSpec((1, QB, 256), lambda b, h, i: (b, i, h)),
                  pl.BlockSpec((1, S, 256), lambda b, h, i: (b, 0, h)),
                  pl.BlockSpec((1, S, LANE), lambda b, h, i: (b, 0, 0))],
        out_specs=[pl.BlockSpec((1, QB, LANE), lambda b, h, i: (b, i, h)),
                   pl.BlockSpec((1, 1, 1, QB), lambda b, h, i: (b, h, 0, i))],
        out_shape=[jax.ShapeDtypeStruct((B, S, HB * LANE), F32), jax.ShapeDtypeStruct((B, HB, 1, S), F32)],
        scratch_shapes=[pltpu.VMEM((S, 256), BF16), pltpu.VMEM((nq, NOPE, QB), BF16),
                        pltpu.VMEM((2, QB, QB), F32), pltpu.VMEM((QB, QB), F32)],
        compiler_params=_params(blk, 2 * _nbytes((S, 256), BF16) + 10 * _nbytes((QB, QB), F32)),
    )(qf, kv, kr)


def mla_bwd(name, qf, kv, kr, do, o, lse, tabs):
    B, S, W = qf.shape
    HB = W // 256
    QB = _tile(S, 256, CHUNK)
    nq = S // QB
    scale = (NOPE + ROPE) ** -0.5

    def body(q_ref, kv_ref, kr_ref, do_ref, o_ref, lse_ref, ct_ref, s1_ref, s2_ref, dq_ref, dkv_ref, dkr_ref,
             k2, kt, dot_, delta, dqt, st_buf, dp_buf, pen, dkv_acc):
        h = pl.program_id(1)
        pen[...] = _mla_diag_pen(QB)
        dkv_acc[...] = jnp.zeros_like(dkv_acc)

        @pl.when(h == 0)
        def _():
            dkr_ref[...] = jnp.zeros_like(dkr_ref)

        _mla_fill_keys(kv_ref, kr_ref, k2)
        for i in range(nq):
            rows = slice(i * QB, (i + 1) * QB)
            kt[i] = _t(k2[rows, :]).astype(BF16)
            dot32 = _t(do_ref[0, rows, :])
            delta[i] = jnp.sum(dot32 * o_ref[0, rows, :].T, axis=0, keepdims=True)
            dot_[i] = dot32.astype(BF16)

        for qi in range(nq):
            rows = slice(qi * QB, (qi + 1) * QB)
            q = q_ref[0, rows, :]
            dob = do_ref[0, rows, :]
            lse_q = lse_ref[0, 0, :, rows]
            delta_q = delta[qi]
            dqt[...] = jnp.zeros_like(dqt)

            def raw(kj, slot, q=q, qi=qi):
                st_buf[slot] = _mla_raw_t(k2, kj, q, QB)
                dp_buf[slot] = jnp.dot(kv_ref[0, _blk(kj, QB), NOPE:], dot_[qi], preferred_element_type=F32)

            raw(0, 0)

            def step(kj, carry, q=q, dob=dob, lse_q=lse_q, delta_q=delta_q, qi=qi, raw=raw):
                ks = pl.ds(pl.multiple_of(kj * QB, QB), QB)
                cur = lax.rem(kj, 2)
                st_raw, dp_raw = st_buf[cur], dp_buf[cur]
                raw(jnp.minimum(kj + 1, qi), 1 - cur)
                pt = jnp.exp(st_raw * scale + jnp.where(kj == qi, pen[...], 0.0) - lse_q)
                dst = (pt * (dp_raw - delta_q) * scale).astype(BF16)
                dkv_acc[ks, NOPE:] += jnp.dot(pt.astype(BF16), dob, preferred_element_type=F32)
                dk2 = jnp.dot(dst, q, preferred_element_type=F32)
                dkv_acc[ks, :NOPE] += dk2[:, :NOPE]
                dkr_ref[0, ks, :] += dk2[:, NOPE:]
                dqt[...] += jnp.dot(kt[kj], dst, preferred_element_type=F32)
                return carry

            lax.fori_loop(0, qi + 1, step, 0)
            dq = dqt[...].T
            dq_ref[0, rows, :NOPE] = dq[:, :NOPE].astype(dq_ref.dtype)
            dq_ref[0, rows, NOPE:] = _rope_tile_bwd(dq[:, NOPE:], ct_ref[rows, :], s1_ref[rows, :],
                                                    s2_ref[rows, :]).astype(dq_ref.dtype)

        dkv_ref[0] = dkv_acc[...].astype(dkv_ref.dtype)

    head = lambda w: pl.BlockSpec((1, S, w), lambda b, h: (b, 0, h))
    shared = pl.BlockSpec((1, S, LANE), lambda b, h: (b, 0, 0))
    blk = (2 * _nbytes((S, 256), BF16) + 2 * _nbytes((S, LANE), BF16) + _nbytes((S, LANE), F32)
           + 2 * _nbytes((S, 256), F32) + _nbytes((S, LANE), F32))
    scr = 3 * _nbytes((S, 256), BF16) + 14 * _nbytes((QB, QB), F32)
    return pl.pallas_call(
        body, name=name, grid=(B, HB),
        in_specs=[head(256), head(256), shared, head(LANE), head(LANE),
                  pl.BlockSpec((1, 1, 1, S), lambda b, h: (b, h, 0, 0))]
        + [pl.BlockSpec((S, LANE), lambda b, h: (0, 0))] * 3,
        out_specs=[head(256), head(256), shared],
        out_shape=[jax.ShapeDtypeStruct((B, S, W), BF16), jax.ShapeDtypeStruct((B, S, W), BF16),
                   jax.ShapeDtypeStruct((B, S, LANE), F32)],
        scratch_shapes=[pltpu.VMEM((S, 256), BF16), pltpu.VMEM((nq, 256, QB), BF16),
                        pltpu.VMEM((nq, NOPE, QB), BF16), pltpu.VMEM((nq, 1, QB), F32),
                        pltpu.VMEM((256, QB), F32), pltpu.VMEM((2, QB, QB), F32), pltpu.VMEM((2, QB, QB), F32),
                        pltpu.VMEM((QB, QB), F32), pltpu.VMEM((S, 256), F32)],
        compiler_params=_params(blk, scr),
    )(qf, kv, kr, do, o, lse, *tabs)


GROUP_STEPS = 4


def cast_group(name, ws, layers, idx, after=None):
    n = len(ws)
    n_in = n + (after is not None)

    def body(k_ref, *refs):
        for i in range(n):
            refs[n_in + i][...] = refs[i][...].astype(BF16)

    def spec_in(w, layer):
        return pl.BlockSpec((None, w.shape[1] // GROUP_STEPS, w.shape[2]), lambda r, k_ref: (layer, r, 0))

    def spec_out(w):
        return pl.BlockSpec((None, w.shape[1] // GROUP_STEPS, w.shape[2]), lambda r, k_ref: (k_ref[0], r, 0))

    return pl.pallas_call(
        body, name=name,
        grid_spec=pltpu.PrefetchScalarGridSpec(
            num_scalar_prefetch=1, grid=(GROUP_STEPS,),
            in_specs=([spec_in(w, l) for w, l in zip(ws, layers)]
                      + [pl.BlockSpec(memory_space=pl.ANY)] * (after is not None)),
            out_specs=[spec_out(w) for w in ws]),
        out_shape=[jax.ShapeDtypeStruct((N_CHIPS, *w.shape[1:]), BF16) for w in ws],
        compiler_params=_params(sum(_nbytes(w.shape[1:], F32) * 3 // 2 for w in ws) // GROUP_STEPS),
    )(idx, *ws, *([] if after is None else [after]))


def adamw(name, w, g, m, v):
    R, C = w.shape
    tr = _tile(R, max(8, (1 << 18) // C // 8 * 8), 8)
    c1 = 1.0 - ADAM_B1 ** ADAM_STEP
    c2 = 1.0 - ADAM_B2 ** ADAM_STEP

    def body(w_ref, g_ref, m_ref, v_ref, d_ref, mo_ref, vo_ref):
        gv = g_ref[...]
        mn = ADAM_B1 * m_ref[...] + (1.0 - ADAM_B1) * gv
        vn = ADAM_B2 * v_ref[...] + (1.0 - ADAM_B2) * (gv * gv)
        mo_ref[...] = mn
        vo_ref[...] = vn
        d_ref[...] = -ADAM_LR * ((mn / c1) / (jnp.sqrt(vn / c2) + ADAM_EPS) + ADAM_WD * w_ref[...])

    spec = pl.BlockSpec((tr, C), lambda r: (r, 0))
    return pl.pallas_call(
        body, name=name, grid=(R // tr,), in_specs=[spec] * 4, out_specs=[spec] * 3,
        out_shape=[jax.ShapeDtypeStruct((R, C), F32)] * 3,
        compiler_params=_params(7 * _nbytes((tr, C), F32), 4 * _nbytes((tr, C), F32)),
    )(w, g, m, v)


def half_sum_group(name, dws, landed, idx):
    n = len(dws)
    steps = GROUP_STEPS // 2

    def body(i_ref, *refs):
        for i in range(n):
            refs[2 * n + i][...] = (refs[i][...].astype(F32) + refs[n + i][...].astype(F32)).astype(BF16)

    def own(d):
        return pl.BlockSpec((None, None, d.shape[2] // steps, d.shape[3]), lambda k, r, i_ref: (k, i_ref[1], r, 0))

    def flat(d):
        return pl.BlockSpec((None, d.shape[2] // steps, d.shape[3]), lambda k, r, i_ref: (k, r, 0))

    return pl.pallas_call(
        body, name=name,
        grid_spec=pltpu.PrefetchScalarGridSpec(
            num_scalar_prefetch=1, grid=(N_CHIPS, steps),
            in_specs=[own(d) for d in dws] + [flat(d) for d in dws], out_specs=[flat(d) for d in dws]),
        out_shape=[jax.ShapeDtypeStruct((N_CHIPS, *d.shape[2:]), BF16) for d in dws],
        compiler_params=_params(sum(3 * _nbytes(d.shape[2:], BF16) for d in dws) // steps),
    )(idx, *dws, *landed)


def chip_sum_group(name, parts, landed, gbufs, layers, idx):
    n = len(parts)
    steps = GROUP_STEPS // 2

    def body(i_ref, *refs):
        for i in range(n):
            a, b = refs[i], refs[n + i]
            refs[3 * n + i][...] = ((a[...].astype(F32) + b[0].astype(F32)) + b[1].astype(F32)) + b[2].astype(F32)

    def mine(p):
        return pl.BlockSpec((None, p.shape[1] // steps, p.shape[2]), lambda r, i_ref: (i_ref[0], r, 0))

    def three(p):
        return pl.BlockSpec((3, p.shape[1] // steps, p.shape[2]), lambda r, i_ref: (0, r, 0))

    def out(p, layer):
        return pl.BlockSpec((None, None, p.shape[1] // steps, p.shape[2]), lambda r, i_ref: (layer, i_ref[1], r, 0))

    return pl.pallas_call(
        body, name=name,
        grid_spec=pltpu.PrefetchScalarGridSpec(
            num_scalar_prefetch=1, grid=(steps,),
            in_specs=[mine(p) for p in parts] + [three(p) for p in parts] + [pl.BlockSpec(memory_space=pl.ANY)] * n,
            out_specs=[out(p, l) for p, l in zip(parts, layers)]),
        out_shape=[jax.ShapeDtypeStruct(g.shape, F32) for g in gbufs],
        input_output_aliases={1 + 2 * n + i: i for i in range(n)},
        compiler_params=_params(sum(6 * _nbytes(p.shape[1:], BF16) for p in parts) // steps),
    )(idx, *parts, *landed, *gbufs)


ANY = pl.BlockSpec(memory_space=pl.ANY)


def _place():
    x, y, c = lax.axis_index("x"), lax.axis_index("y"), lax.axis_index("c")
    chips = [(1 - x, y), (x, 1 - y), (1 - x, 1 - y)]
    return x, y, c, chips


HBM = pl.BlockSpec(memory_space=pltpu.HBM)
SEM = pl.BlockSpec(memory_space=pltpu.SEMAPHORE)
EFFECT = pltpu.SideEffectType.DATAFLOW_SIDE_EFFECTING


def _in_hbm(a):
    return pltpu.with_memory_space_constraint(a, pltpu.HBM)


def _ici_copy(src, dst, send_sems, recv_sems, k, peer):
    return pltpu.make_async_remote_copy(src_ref=src, dst_ref=dst, send_sem=send_sems.at[k], recv_sem=recv_sems.at[k],
                                        device_id=peer, device_id_type=MESH)


def ici_start(name, bufs, lands, after, gather):
    n, nl = len(bufs), len(lands)

    def body(*refs):
        b_in = refs[:n]
        send_sems, recv_sems = refs[n + nl + 1], refs[n + nl + 2]
        b_out = refs[n + nl + 3:2 * n + nl + 3]
        l_out = refs[2 * n + nl + 3:2 * n + 2 * nl + 3]
        token = refs[-1]
        x, y, c, chips = _place()
        kme = 2 * x + y
        for i in range(n):
            for j in range(3):
                peer = (*chips[j], c)
                if gather:
                    _ici_copy(b_out[i].at[kme, c], b_out[i].at[kme, c], send_sems, recv_sems, 3 * i + j, peer).start()
                else:
                    kd = 2 * chips[j][0] + chips[j][1]
                    _ici_copy(b_out[i].at[kd], l_out[i].at[j], send_sems, recv_sems, 3 * i + j, peer).start()
        token[...] = jnp.zeros_like(token)

    arrays = [*bufs, *lands]
    outs = pl.pallas_call(
        body, name=name,
        in_specs=[HBM] * (n + nl) + [ANY],
        out_specs=(SEM, SEM, *[HBM] * (n + nl), pl.BlockSpec(memory_space=pltpu.VMEM)),
        out_shape=(pltpu.SemaphoreType.DMA((3 * n,)), pltpu.SemaphoreType.DMA((3 * n,)),
                   *[pltpu.HBM(a.shape, a.dtype) for a in arrays], jax.ShapeDtypeStruct((8, LANE), F32)),
        input_output_aliases={i: 2 + i for i in range(n + nl)},
        compiler_params=pltpu.CompilerParams(has_side_effects=EFFECT),
    )(*[_in_hbm(a) for a in arrays], after)
    return outs[0], outs[1], list(outs[2:2 + n]), list(outs[2 + n:2 + n + nl]), outs[-1]


def ici_wait(name, send_sems, recv_sems, bufs, lands, after, gather):
    n, nl = len(bufs), len(lands)

    def body(*refs):
        b_in, l_in = refs[:n], refs[n:n + nl]
        send_sems, recv_sems = refs[n + nl], refs[n + nl + 1]
        x, y, c, chips = _place()
        kme = 2 * x + y
        for i in range(n):
            for j in range(3):
                peer = (*chips[j], c)
                kj = 2 * chips[j][0] + chips[j][1]
                if gather:
                    _ici_copy(b_in[i].at[kme, c], b_in[i].at[kme, c], send_sems, recv_sems, 3 * i + j, peer).wait_send()
                    _ici_copy(b_in[i].at[kj, c], b_in[i].at[kj, c], send_sems, recv_sems, 3 * i + j, peer).wait_recv()
                else:
                    _ici_copy(b_in[i].at[kj], l_in[i].at[j], send_sems, recv_sems, 3 * i + j, peer).wait_send()
                    _ici_copy(b_in[i].at[kj], l_in[i].at[j], send_sems, recv_sems, 3 * i + j, peer).wait_recv()

    arrays = [*bufs, *lands]
    outs = pl.pallas_call(
        body, name=name,
        in_specs=[HBM] * (n + nl) + [SEM, SEM, ANY],
        out_specs=tuple([HBM] * (n + nl)),
        out_shape=tuple(pltpu.HBM(a.shape, a.dtype) for a in arrays),
        input_output_aliases={i: i for i in range(n + nl)},
        compiler_params=pltpu.CompilerParams(has_side_effects=EFFECT),
    )(*arrays, send_sems, recv_sems, after)
    return list(outs[:n]), list(outs[n:])


def gather_pair_pass(name, bufs):
    n = len(bufs)

    def body(*refs):
        b = refs[n:2 * n]
        send_sems, recv_sems = refs[2 * n:]
        x, y, c, chips = _place()
        sib = (x, y, 1 - c)

        def d2d(i, j, which):
            kj = 2 * chips[j][0] + chips[j][1]
            return _ici_copy(b[i].at[kj, which], b[i].at[kj, which], send_sems, recv_sems, 3 * i + j, sib)

        for i in range(n):
            for j in range(3):
                d2d(i, j, c).start()
        for i in range(n):
            for j in range(3):
                d2d(i, j, 1 - c).wait_recv()
        for i in range(n):
            for j in range(3):
                d2d(i, j, c).wait_send()

    return pl.pallas_call(
        body, name=name, in_specs=[ANY] * n, out_specs=[ANY] * n,
        out_shape=[jax.ShapeDtypeStruct(a.shape, a.dtype) for a in bufs],
        input_output_aliases={i: i for i in range(n)},
        scratch_shapes=[pltpu.SemaphoreType.DMA((3 * n,)), pltpu.SemaphoreType.DMA((3 * n,))],
    )(*bufs)


def pair_exchange(name, dws):
    n = len(dws)

    def body(*refs):
        ins, outs = refs[:n], refs[n:2 * n]
        send_sems, recv_sems = refs[2 * n:]
        x, y, c, _ = _place()
        copies = []
        for i in range(n):
            copies.append(pltpu.make_async_remote_copy(
                src_ref=ins[i].at[:, 1 - c], dst_ref=outs[i],
                send_sem=send_sems.at[i], recv_sem=recv_sems.at[i],
                device_id=(x, y, 1 - c), device_id_type=MESH))
            copies[i].start()
        for cp in copies:
            cp.wait_recv()
        for cp in copies:
            cp.wait_send()

    return pl.pallas_call(
        body, name=name, in_specs=[ANY] * n, out_specs=[ANY] * n,
        out_shape=[jax.ShapeDtypeStruct((N_CHIPS, *d.shape[2:]), d.dtype) for d in dws],
        scratch_shapes=[pltpu.SemaphoreType.DMA((n,)), pltpu.SemaphoreType.DMA((n,))],
    )(*dws)


def pair_assemble(gbufs):
    n = len(gbufs)

    def body(*refs):
        bufs = refs[n:2 * n]
        send_sems, recv_sems = refs[2 * n:]
        x, y, c, _ = _place()
        copies = []
        for i in range(n):
            copies.append(pltpu.make_async_remote_copy(
                src_ref=bufs[i].at[:, c], dst_ref=bufs[i].at[:, c],
                send_sem=send_sems.at[i], recv_sem=recv_sems.at[i],
                device_id=(x, y, 1 - c), device_id_type=MESH))
            copies[i].start()
        for i in range(n):
            pltpu.make_async_remote_copy(
                src_ref=bufs[i].at[:, 1 - c], dst_ref=bufs[i].at[:, 1 - c],
                send_sem=send_sems.at[i], recv_sem=recv_sems.at[i],
                device_id=(x, y, 1 - c), device_id_type=MESH).wait_recv()
        for cp in copies:
            cp.wait_send()

    return pl.pallas_call(
        body, name="grad_pair_assemble", in_specs=[ANY] * n, out_specs=[ANY] * n,
        out_shape=[jax.ShapeDtypeStruct(g.shape, g.dtype) for g in gbufs],
        input_output_aliases={i: i for i in range(n)},
        scratch_shapes=[pltpu.SemaphoreType.DMA((n,)), pltpu.SemaphoreType.DMA((n,))],
    )(*gbufs)


def all_reduce_small(vec):
    NR = vec.shape[0]
    flips = [(fx, fy, fc) for fx in (0, 1) for fy in (0, 1) for fc in (0, 1)][1:]

    def body(v_ref, o_ref, buf, send_sems, recv_sems):
        x, y, c, _ = _place()
        me = 4 * x + 2 * y + c
        buf[me] = v_ref[...]
        copies = []
        for j, (fx, fy, fc) in enumerate(flips):
            peer = (1 - x if fx else x, 1 - y if fy else y, 1 - c if fc else c)
            copies.append(pltpu.make_async_remote_copy(
                src_ref=v_ref, dst_ref=buf.at[me], send_sem=send_sems.at[j], recv_sem=recv_sems.at[j],
                device_id=peer, device_id_type=MESH))
            copies[j].start()
        for cp in copies:
            cp.wait_recv()
        for cp in copies:
            cp.wait_send()
        acc = buf[0]
        for d in range(1, 8):
            acc = acc + buf[d]
        o_ref[...] = acc

    return pl.pallas_call(
        body, name="all_reduce_small",
        in_specs=[pl.BlockSpec(memory_space=pltpu.VMEM)], out_specs=pl.BlockSpec(memory_space=pltpu.VMEM),
        out_shape=jax.ShapeDtypeStruct((NR, LANE), F32),
        scratch_shapes=[pltpu.VMEM((8, NR, LANE), F32), pltpu.SemaphoreType.DMA((7,)),
                        pltpu.SemaphoreType.DMA((7,))],
    )(vec)


def _pack(arrays):
    flat = jnp.concatenate([a.reshape(-1).astype(F32) for a in arrays])
    n = flat.shape[0]
    npad = -(-n // (8 * LANE)) * (8 * LANE)
    return jnp.pad(flat, (0, npad - n)).reshape(npad // LANE, LANE)


def _unpack(buf, like):
    flat = buf.reshape(-1)
    out, off = [], 0
    for a in like:
        out.append(flat[off:off + a.size].reshape(a.shape))
        off += a.size
    return out


def kernel(x, ffn1_norm, ffn1_w_in, ffn1_w_out, mix_norm, ffn2_norm, ffn2_w_in, ffn2_w_out, a_w_qkv, a_rel_bias, a_w_o, kv_norm, kv_w_down, kv_latent_norm, kv_w_up, b_w_dq, b_q_norm, b_w_uq, b_w_o, final_norm, loss_target, m_ffn1_norm, m_ffn1_w_in, m_ffn1_w_out, m_mix_norm, m_ffn2_norm, m_ffn2_w_in, m_ffn2_w_out, m_a_w_qkv, m_a_rel_bias, m_a_w_o, m_kv_norm, m_kv_w_down, m_kv_latent_norm, m_kv_w_up, m_b_w_dq, m_b_q_norm, m_b_w_uq, m_b_w_o, m_final_norm, v_ffn1_norm, v_ffn1_w_in, v_ffn1_w_out, v_mix_norm, v_ffn2_norm, v_ffn2_w_in, v_ffn2_w_out, v_a_w_qkv, v_a_rel_bias, v_a_w_o, v_kv_norm, v_kv_w_down, v_kv_latent_norm, v_kv_w_up, v_b_w_dq, v_b_q_norm, v_b_w_uq, v_b_w_o, v_final_norm):
    B, S, D = x.shape
    T = B * S
    HB = D // 128
    QL = b_q_norm.shape[-1]
    KVL = kv_latent_norm.shape[0]
    hpc = HB // N_CHIPS
    tabs = rope_tables(S)
    idx = jnp.stack([2 * lax.axis_index("x") + lax.axis_index("y"), lax.axis_index("c")]).astype(I32)

    def halves(a):
        return a.reshape(*a.shape[:-2], 2, a.shape[-2] // 2, a.shape[-1])

    def whole(a):
        return a.reshape(*a.shape[:-3], 2 * a.shape[-2], a.shape[-1])

    kv_w_down_p = jnp.pad(kv_w_down, ((0, 0), (0, LANE - ROPE)))[None]
    b_w_uq_p = jnp.pad(b_w_uq.reshape(1, QL, hpc, NOPE + ROPE),
                       ((0, 0), (0, 0), (0, 0), (0, LANE - ROPE))).reshape(1, QL, hpc * 256)
    sharded = [("ffn1_w_in", ffn1_w_in), ("ffn1_w_out", ffn1_w_out), ("ffn2_w_in", ffn2_w_in),
               ("ffn2_w_out", ffn2_w_out), ("a_w_qkv", a_w_qkv), ("a_w_o", a_w_o),
               ("kv_w_down", kv_w_down_p), ("kv_w_up", kv_w_up[None]), ("b_w_dq", b_w_dq),
               ("b_w_uq", b_w_uq_p), ("b_w_o", b_w_o)]
    names = [nm for nm, _ in sharded]
    shard_of = dict(sharded)
    W = {}

    gather_groups = [
        [("ffn1_w_in", 0)],
        [("ffn1_w_out", 0)],
        [("a_w_qkv", 0), ("a_w_o", 0)],
        [("ffn2_w_in", 0), ("ffn2_w_out", 0), ("kv_w_down", 0), ("kv_w_up", 0)],
        [("ffn1_w_in", 1), ("ffn1_w_out", 1), ("b_w_dq", 0), ("b_w_uq", 0), ("b_w_o", 0), ("ffn2_w_in", 1),
         ("ffn2_w_out", 1)]]

    own = {}

    def cast(g, after=None):
        keys = gather_groups[g]
        own.update(zip(keys, cast_group(f"cast_group_{g}", [shard_of[nm] for nm, _ in keys], [l for _, l in keys],
                                        idx, after=after)))

    def gather_start(g, after):
        keys = gather_groups[g]
        ss, rs, bufs, _, token = ici_start(f"gather_start_{g}", [halves(own[k]) for k in keys], [], after, True)
        return (g, ss, rs, bufs), token

    def gather_finish(state, after):
        g, ss, rs, bufs = state
        bufs, _ = ici_wait(f"gather_wait_{g}", ss, rs, bufs, [], after, True)
        full = gather_pair_pass(f"gather_pair_{g}", bufs)
        for k, w in zip(gather_groups[g], full):
            W[k] = whole(w)
        return full[0]

    def tied(a, token):
        return a + token[0, 0]

    def col(nm, l=0):
        return W[(nm, l)]

    def row(nm, l=0):
        w = W[(nm, l)]
        return w.reshape(N_CHIPS * w.shape[1], w.shape[2])

    bias = rel_bias_tile("rel_bias_tile", a_rel_bias[0])

    def ffn_fwd(tag, h, g, w_in, w_out):
        xn = rms_fwd(f"{tag}_norm", h, g)
        u, act = ffn_in_act(f"{tag}_in", xn, w_in)
        return mm_roww(f"{tag}_out", act, w_out, F32, res=h, alpha=0.5), (xn, u, act)

    h0 = x.reshape(T, D)
    for g in range(3):
        cast(g)
    st0, tok0 = gather_start(0, h0)
    st1, tok1 = gather_start(1, tok0)
    st2, tok2 = gather_start(2, tok1)
    for g in range(3, len(gather_groups)):
        cast(g, tok2)
    xn0 = rms_fwd("l0f1_norm", h0, tied(ffn1_norm[0], tok2))
    gather_finish(st0, xn0)
    u0, act0 = ffn_in_act("l0f1_in", xn0, col("ffn1_w_in", 0))
    gather_finish(st1, u0)
    h1 = mm_roww("l0f1_out", act0, row("ffn1_w_out", 0), F32, res=h0, alpha=0.5)
    sv_f1a = (xn0, u0, act0)
    done2 = gather_finish(st2, h1)
    st3, tok3 = gather_start(3, done2)
    st4, tok4 = gather_start(4, tok3)
    hn_a = rms_fwd("l0mix_norm", h1, tied(mix_norm[0], tok4))
    qkv = mm_colw("l0_qkv", hn_a, col("a_w_qkv"), BF16).reshape(B, S, 3 * D)
    o_a = attn_a_fwd("l0_attn", qkv, bias).reshape(T, D)
    h2 = mm_roww("l0_attn_out", o_a, row("a_w_o"), F32, res=h1)
    gather_finish(st3, h2)
    h3, sv_f2a = ffn_fwd("l0f2", h2, ffn2_norm[0], col("ffn2_w_in", 0), row("ffn2_w_out", 0))

    hkv = rms_fwd("kv_norm", h3, kv_norm)
    ckr = mm_roww("kv_down", hkv, row("kv_w_down"), F32)
    ckv, kr = kvprep_fwd("kv_prep", ckr, kv_latent_norm, tabs, B, S)
    kvb = mm_colw("kv_up", ckv, col("kv_w_up"), BF16).reshape(B, S, HB * 256)
    gather_finish(st4, kvb)

    h4, sv_f1b = ffn_fwd("l1f1", h3, ffn1_norm[1], col("ffn1_w_in", 1), row("ffn1_w_out", 1))
    hn_b = rms_fwd("l1mix_norm", h4, mix_norm[1])
    cqp = mm_roww("l1_dq", hn_b, row("b_w_dq"), F32)
    cq = rms_fwd("l1_q_norm", cqp, b_q_norm[0])
    qf = uq_rope("l1_uq", cq, col("b_w_uq"), tabs, S).reshape(B, S, HB * 256)
    o_b, lse = mla_fwd("l1_attn", qf, kvb, kr)
    h5 = mm_roww("l1_attn_out", o_b.reshape(T, HB * LANE), row("b_w_o"), F32, res=h4)
    h6, sv_f2b = ffn_fwd("l1f2", h5, ffn2_norm[1], col("ffn2_w_in", 1), row("ffn2_w_out", 1))

    dh, g_final, loss_part = loss_head("loss_head", h6, final_norm, loss_target.reshape(T, D))

    gw = {}
    gbufs = {nm: lax.empty(halves(w).shape, F32) for nm, w in sharded}

    def reduce_start(r, keys, after):
        dws = [halves(gw[k]) for k in keys]
        landed = pair_exchange(f"grad_pair_exchange_{r}", dws)
        parts = half_sum_group(f"half_sum_{r}", dws, landed, idx)
        lands = [lax.empty((3, *p.shape[1:]), p.dtype) for p in parts]
        ss, rs, parts, lands, token = ici_start(f"reduce_start_{r}", parts, lands, after, False)
        return (r, keys, ss, rs, parts, lands), token

    def reduce_finish(state, after):
        r, keys, ss, rs, parts, lands = state
        parts, lands = ici_wait(f"reduce_wait_{r}", ss, rs, parts, lands, after, False)
        done = chip_sum_group(f"chip_sum_{r}", parts, lands, [gbufs[nm] for nm, _ in keys], [l for _, l in keys], idx)
        gbufs.update(zip([nm for nm, _ in keys], done))
        return done[0]

    def ffn_bwd(tag, dh, h_in, g, w_in, w_out, saved, key_in, key_out, after=None, then=None):
        xn, u, act = saved
        du = ffn_dact(f"{tag}_dact", dh, w_out, u, after=after)
        dwo = mm_droww(f"{tag}_dwout", act, dh, alpha=0.5)
        gw[key_out] = dwo.reshape(N_CHIPS, dwo.shape[0] // N_CHIPS, dwo.shape[1])
        gw[key_in] = mm_dcolw(f"{tag}_dwin", xn, du, pair_layout=True)
        token = then(du) if then is not None else None
        return dx_norm_bwd(f"{tag}_dxn", du, w_in, h_in, g, dres=dh, pair_layout=True, after=token)

    def chip_major(dw):
        return dw.reshape(N_CHIPS, dw.shape[0] // N_CHIPS, dw.shape[1])

    dh, g_f2b = ffn_bwd("l1f2b", dh, h5, ffn2_norm[1], col("ffn2_w_in", 1), row("ffn2_w_out", 1), sv_f2b,
                        ("ffn2_w_in", 1), ("ffn2_w_out", 1))
    red0, rtok0 = reduce_start(0, [("ffn2_w_in", 1), ("ffn2_w_out", 1)], dh)
    do_b = mm_roww_t("l1_attn_do", dh, row("b_w_o"), BF16, after=rtok0).reshape(B, S, HB * LANE)
    gw[("b_w_o", 0)] = chip_major(mm_droww("l1_attn_dwo", o_b.reshape(T, HB * LANE), dh))
    dqpre, dkv, dkr = mla_bwd("l1_attn_bwd", qf, kvb, kr, do_b, o_b, lse, tabs)
    dqpre = dqpre.reshape(T, HB * 256)
    gw[("b_w_uq", 0)] = mm_dcolw("l1_dwuq", cq, dqpre)
    dcqp, g_qn = dx_norm_bwd("l1_dcq", dqpre, col("b_w_uq"), cqp, b_q_norm[0])
    gw[("b_w_dq", 0)] = chip_major(mm_droww("l1_dwdq", hn_b, dcqp))
    dhn = mm_roww_t("l1_dhn", dcqp, row("b_w_dq"), F32)
    dh, g_mixb = rms_bwd("l1_dmix", h4, mix_norm[1], dhn, dres=dh)
    dh, g_f1b = ffn_bwd("l1f1b", dh, h3, ffn1_norm[1], col("ffn1_w_in", 1), row("ffn1_w_out", 1), sv_f1b,
                        ("ffn1_w_in", 1), ("ffn1_w_out", 1))
    fin0 = reduce_finish(red0, dh)
    red1, rtok1 = reduce_start(1, [("b_w_o", 0), ("b_w_uq", 0), ("b_w_dq", 0), ("ffn1_w_in", 1), ("ffn1_w_out", 1)], fin0)
    dkv2 = dkv.reshape(T, HB * 256)
    gw[("kv_w_up", 0)] = mm_dcolw("kv_dwup", ckv, dkv2, after=rtok1)
    dckv = mm_colw_t("kv_dckv", dkv2, col("kv_w_up"), F32, after=rtok1)
    dckr, g_lat = kvprep_bwd("kv_prep_bwd", ckr, kv_latent_norm, dckv, dkr, tabs, B, S)
    gw[("kv_w_down", 0)] = chip_major(mm_droww("kv_dwdown", hkv, dckr))
    dhkv = mm_roww_t("kv_dhkv", dckr, row("kv_w_down"), F32)
    dh, g_kvn = rms_bwd("kv_dnorm", h3, kv_norm, dhkv, dres=dh)
    dh, g_f2a = ffn_bwd("l0f2b", dh, h2, ffn2_norm[0], col("ffn2_w_in", 0), row("ffn2_w_out", 0), sv_f2a,
                        ("ffn2_w_in", 0), ("ffn2_w_out", 0))
    do_a = mm_roww_t("l0_attn_do", dh, row("a_w_o"), BF16).reshape(B, S, D)
    gw[("a_w_o", 0)] = chip_major(mm_droww("l0_attn_dwo", o_a, dh))
    dqkv, dbias = attn_a_bwd("l0_attn_bwd", qkv, do_a, bias)
    dqkv = dqkv.reshape(T, 3 * D)
    gw[("a_w_qkv", 0)] = mm_dcolw("l0_dwqkv", hn_a, dqkv)
    dh, g_mixa = dx_norm_bwd("l0_dhn", dqkv, col("a_w_qkv"), h1, mix_norm[0], dres=dh)
    fin1 = reduce_finish(red1, dh)
    red2, rtok2 = reduce_start(2, [("kv_w_up", 0), ("kv_w_down", 0), ("ffn2_w_in", 0), ("ffn2_w_out", 0),
                                   ("a_w_o", 0), ("a_w_qkv", 0)], fin1)
    last = {}

    def last_group(du):
        fin2 = reduce_finish(red2, gw[("ffn1_w_in", 0)])
        last["red"], token = reduce_start(3, [("ffn1_w_in", 0), ("ffn1_w_out", 0)], fin2)
        return token

    dh, g_f1a = ffn_bwd("l0f1b", dh, h0, ffn1_norm[0], col("ffn1_w_in", 0), row("ffn1_w_out", 0), sv_f1a,
                        ("ffn1_w_in", 0), ("ffn1_w_out", 0), after=rtok2, then=last_group)
    grad_x = dh.reshape(B, S, D)
    g_rel = rel_bias_grad("rel_bias_grad", dbias)[:, :2 * MAX_REL + 1][None]
    reduce_finish(last["red"], dh)

    full = [whole(g) for g in pair_assemble([gbufs[nm] for nm in names])]
    G = {nm: g for (nm, _), g in zip(sharded, full)}
    G["kv_w_down"] = G["kv_w_down"][0, :, :KVL + ROPE]
    G["kv_w_up"] = G["kv_w_up"][0]
    G["b_w_uq"] = G["b_w_uq"].reshape(1, QL, hpc, 256)[..., :NOPE + ROPE].reshape(b_w_uq.shape)

    small = [("ffn1_norm", jnp.stack([g_f1a, g_f1b])), ("mix_norm", jnp.stack([g_mixa, g_mixb])),
             ("ffn2_norm", jnp.stack([g_f2a, g_f2b])), ("a_rel_bias", g_rel), ("kv_norm", g_kvn),
             ("kv_latent_norm", g_lat), ("b_q_norm", g_qn[None]), ("final_norm", g_final)]
    red = all_reduce_small(_pack([loss_part] + [g for _, g in small]))
    unpacked = _unpack(red, [loss_part] + [g for _, g in small])
    loss = unpacked[0][0, 0]
    for (nm, _), g in zip(small, unpacked[1:]):
        G[nm] = g

    given = dict(ffn1_norm=(ffn1_norm, m_ffn1_norm, v_ffn1_norm), ffn1_w_in=(ffn1_w_in, m_ffn1_w_in, v_ffn1_w_in),
                 ffn1_w_out=(ffn1_w_out, m_ffn1_w_out, v_ffn1_w_out), mix_norm=(mix_norm, m_mix_norm, v_mix_norm),
                 ffn2_norm=(ffn2_norm, m_ffn2_norm, v_ffn2_norm), ffn2_w_in=(ffn2_w_in, m_ffn2_w_in, v_ffn2_w_in),
                 ffn2_w_out=(ffn2_w_out, m_ffn2_w_out, v_ffn2_w_out), a_w_qkv=(a_w_qkv, m_a_w_qkv, v_a_w_qkv),
                 a_rel_bias=(a_rel_bias, m_a_rel_bias, v_a_rel_bias), a_w_o=(a_w_o, m_a_w_o, v_a_w_o),
                 kv_norm=(kv_norm, m_kv_norm, v_kv_norm), kv_w_down=(kv_w_down, m_kv_w_down, v_kv_w_down),
                 kv_latent_norm=(kv_latent_norm, m_kv_latent_norm, v_kv_latent_norm),
                 kv_w_up=(kv_w_up, m_kv_w_up, v_kv_w_up), b_w_dq=(b_w_dq, m_b_w_dq, v_b_w_dq),
                 b_q_norm=(b_q_norm, m_b_q_norm, v_b_q_norm), b_w_uq=(b_w_uq, m_b_w_uq, v_b_w_uq),
                 b_w_o=(b_w_o, m_b_w_o, v_b_w_o), final_norm=(final_norm, m_final_norm, v_final_norm))
    order = list(given)
    delta, new_m, new_v = {}, {}, {}
    small_names = [nm for nm, _ in small]
    packed = [_pack([given[nm][k] for nm in small_names]) for k in range(3)]
    outs = adamw("adamw_small", packed[0], _pack([G[nm] for nm in small_names]), packed[1], packed[2])
    for dst, buf in zip((delta, new_m, new_v), outs):
        for nm, a in zip(small_names, _unpack(buf, [given[nm][0] for nm in small_names])):
            dst[nm] = a
    for nm, _ in sharded:
        w, m, v = given[nm]
        g = G[nm].reshape(w.shape)
        G[nm] = g
        two = lambda a: a.reshape(-1, a.shape[-1])
        d_, m_, v_ = adamw(f"adamw_{nm}", two(w), two(g), two(m), two(v))
        delta[nm], new_m[nm], new_v[nm] = d_.reshape(w.shape), m_.reshape(w.shape), v_.reshape(w.shape)

    return (loss, grad_x, *[G[n] for n in order], *[delta[n] for n in order],
            *[new_m[n] for n in order], *[new_v[n] for n in order])
```

```python
import math

import jax
import jax.numpy as jnp
from jax import lax
from jax.experimental import pallas as pl
from jax.experimental.pallas import tpu as pltpu

F32 = jnp.float32
BF16 = jnp.bfloat16
I32 = jnp.int32

CHUNK = 64
CHUNK_SHIFT = 6
HEAD_DIM_A = 64
LEFT_CHUNKS = 8
MAX_REL = 128
REL_PAD = 384
QROWS = 2 * CHUNK
WIN = (LEFT_CHUNKS + 2) * CHUNK
PADR = LEFT_CHUNKS * CHUNK
NOPE = 128
ROPE = 64
EPS = 1e-6
NEG_INF = -1e30
ROPE_THETA = 10000.0
ADAM_LR, ADAM_B1, ADAM_B2, ADAM_EPS, ADAM_WD, ADAM_STEP = 0.001, 0.9, 0.999, 1e-08, 0.01, 10
N_CHIPS = 4
LANE = 128
MESH = pl.DeviceIdType.MESH
VMEM_CAP_MB = 60
VMEM_FLOOR_MB = 52

NN = (((1,), (0,)), ((), ()))
NT = (((1,), (1,)), ((), ()))
TN = (((0,), (0,)), ((), ()))


def _tile(n, pref, mult):
    t = (min(pref, n) // mult) * mult
    while t >= mult:
        if n % t == 0:
            return t
        t -= mult
    return n


def _nbytes(shape, dtype):
    return math.prod(shape) * jnp.dtype(dtype).itemsize


def _params(block_bytes, extra_bytes=0):
    need = 2 * block_bytes + extra_bytes
    mb = min(VMEM_CAP_MB, max(VMEM_FLOOR_MB, int(need * 1.25 / 2**20) + 8))
    return pltpu.CompilerParams(vmem_limit_bytes=mb * 2**20)


def _mm(name, kind, a, b, grid, a_spec, b_spec, o_spec, out_shape, out_dtype, blocks,
        red_axis=None, nred=1, alpha=1.0, res=None, res_spec=None, after=None):
    dims = {"nn": NN, "nt": NT, "tn": TN}[kind]
    has_res = res is not None
    acc_in_out = nred > 1 and out_dtype == F32 and not has_res and alpha == 1.0
    n_in = 2 + has_res + (after is not None)

    def body(*refs):
        a_ref, b_ref = refs[0], refs[1]
        r_ref = refs[2] if has_res else None
        o_ref = refs[n_in]
        p = lax.dot_general(a_ref[...].astype(BF16), b_ref[...].astype(BF16), dims,
                            preferred_element_type=F32)

        def finish(acc):
            y = acc if alpha == 1.0 else acc * alpha
            if has_res:
                y = r_ref[...] + y
            o_ref[...] = y.astype(o_ref.dtype)

        if nred == 1:
            finish(p)
            return
        k = pl.program_id(red_axis)
        tgt = o_ref if acc_in_out else refs[-1]

        @pl.when(k == 0)
        def _():
            tgt[...] = p

        @pl.when(k > 0)
        def _():
            tgt[...] += p

        if not acc_in_out:
            @pl.when(k == nred - 1)
            def _():
                finish(tgt[...])

    a_blk, b_blk, o_blk = blocks
    scratch = []
    extra = 0
    if nred > 1 and not acc_in_out:
        scratch = [pltpu.VMEM(o_blk, F32)]
        extra = _nbytes(o_blk, F32)
    blk = _nbytes(a_blk, a.dtype) + _nbytes(b_blk, b.dtype) + _nbytes(o_blk, out_dtype)
    ins, specs = [a, b], [a_spec, b_spec]
    if has_res:
        ins.append(res)
        specs.append(res_spec)
        blk += _nbytes(o_blk, res.dtype)
    if after is not None:
        ins.append(after)
        specs.append(pl.BlockSpec(memory_space=pl.ANY))
    extra += _nbytes(a_blk, BF16) + _nbytes(b_blk, BF16) + 2 * _nbytes(o_blk, F32)
    return pl.pallas_call(
        body, name=name, grid=grid, in_specs=specs, out_specs=o_spec,
        out_shape=jax.ShapeDtypeStruct(out_shape, out_dtype), scratch_shapes=scratch,
        compiler_params=_params(blk, extra),
    )(*ins)


def mm_colw(name, x, w3, out_dtype):
    T, K = x.shape
    _, _, nl = w3.shape
    tm = _tile(T, 512, 8)
    return _mm(name, "nn", x, w3, (N_CHIPS, T // tm),
               pl.BlockSpec((tm, K), lambda j, i: (i, 0)),
               pl.BlockSpec((None, K, nl), lambda j, i: (j, 0, 0)),
               pl.BlockSpec((tm, nl), lambda j, i: (i, j)),
               (T, N_CHIPS * nl), out_dtype, ((tm, K), (K, nl), (tm, nl)))


def _pair_chip(j):
    return (j % 2) * 2 + j // 2


def mm_colw_t(name, dy, w3, out_dtype, res=None, after=None, pair_layout=False):
    T = dy.shape[0]
    _, K, nl = w3.shape
    tm = _tile(T, 1024, 8)
    chip = _pair_chip if pair_layout else (lambda j: j)
    return _mm(name, "nt", dy, w3, (T // tm, N_CHIPS),
               pl.BlockSpec((tm, nl), lambda i, j: (i, j)),
               pl.BlockSpec((None, K, nl), lambda i, j: (chip(j), 0, 0)),
               pl.BlockSpec((tm, K), lambda i, j: (i, 0)),
               (T, K), out_dtype, ((tm, nl), (K, nl), (tm, K)),
               red_axis=1, nred=N_CHIPS, res=res,
               res_spec=pl.BlockSpec((tm, K), lambda i, j: (i, 0)), after=after)


def mm_dcolw(name, x, dy, after=None, pair_layout=False):
    T, K = x.shape
    nl = dy.shape[1] // N_CHIPS
    tt = _tile(T, 2048, 8)
    chip = _pair_chip if pair_layout else (lambda j: j)
    return _mm(name, "tn", x, dy, (N_CHIPS, T // tt),
               pl.BlockSpec((tt, K), lambda j, t: (t, 0)),
               pl.BlockSpec((tt, nl), lambda j, t: (t, j)),
               pl.BlockSpec((None, K, nl), lambda j, t: (chip(j), 0, 0)),
               (N_CHIPS, K, nl), BF16, ((tt, K), (tt, nl), (K, nl)),
               red_axis=1, nred=T // tt, after=after)


def mm_roww(name, x, w2, out_dtype, res=None, alpha=1.0):
    T, Kt = x.shape
    N = w2.shape[1]
    tm = _tile(T, 512, 8)
    return _mm(name, "nn", x, w2, (T // tm,),
               pl.BlockSpec((tm, Kt), lambda i: (i, 0)),
               pl.BlockSpec((Kt, N), lambda i: (0, 0)),
               pl.BlockSpec((tm, N), lambda i: (i, 0)),
               (T, N), out_dtype, ((tm, Kt), (Kt, N), (tm, N)),
               alpha=alpha, res=res, res_spec=pl.BlockSpec((tm, N), lambda i: (i, 0)))


def mm_roww_t(name, dy, w2, out_dtype, alpha=1.0, after=None):
    T, N = dy.shape
    Kt = w2.shape[0]
    tm = _tile(T, 512, 8)
    tk = _tile(Kt, 1408, LANE)
    return _mm(name, "nt", dy, w2, (Kt // tk, T // tm),
               pl.BlockSpec((tm, N), lambda j, i: (i, 0)),
               pl.BlockSpec((tk, N), lambda j, i: (j, 0)),
               pl.BlockSpec((tm, tk), lambda j, i: (i, j)),
               (T, Kt), out_dtype, ((tm, N), (tk, N), (tm, tk)), alpha=alpha, after=after)


def mm_droww(name, x, dy, alpha=1.0):
    T, Kt = x.shape
    N = dy.shape[1]
    tt = _tile(T, 2048, 8)
    tk = _tile(Kt, 1408, LANE)
    return _mm(name, "tn", x, dy, (Kt // tk, T // tt),
               pl.BlockSpec((tt, tk), lambda j, t: (t, j)),
               pl.BlockSpec((tt, N), lambda j, t: (t, 0)),
               pl.BlockSpec((tk, N), lambda j, t: (j, 0)),
               (Kt, N), BF16, ((tt, tk), (tt, N), (tk, N)),
               red_axis=1, nred=T // tt, alpha=alpha)


def rms_fwd(name, x, g):
    T, D = x.shape
    tm = _tile(T, 512, 8)

    def body(x_ref, g_ref, o_ref):
        xv = x_ref[...]
        r = lax.rsqrt(jnp.mean(xv * xv, axis=-1, keepdims=True) + EPS)
        o_ref[...] = (xv * r * g_ref[...]).astype(o_ref.dtype)

    return pl.pallas_call(
        body, name=name, grid=(T // tm,),
        in_specs=[pl.BlockSpec((tm, D), lambda i: (i, 0)), pl.BlockSpec((1, D), lambda i: (0, 0))],
        out_specs=pl.BlockSpec((tm, D), lambda i: (i, 0)),
        out_shape=jax.ShapeDtypeStruct((T, D), BF16),
        compiler_params=_params(_nbytes((tm, D), F32) * 2, 4 * _nbytes((tm, D), F32)),
    )(x, g.reshape(1, D))


def _rms_bwd_math(xv, gv, dy):
    r = lax.rsqrt(jnp.mean(xv * xv, axis=-1, keepdims=True) + EPS)
    xh = xv * r
    dyg = dy * gv
    dx = r * (dyg - xh * jnp.mean(dyg * xh, axis=-1, keepdims=True))
    dg = jnp.sum(dy * xh, axis=0, keepdims=True)
    return dx, dg


def rms_bwd(name, x, g, dy, dres=None):
    T, D = x.shape
    tm = _tile(T, 256, 8)
    has_res = dres is not None

    def body(*refs):
        x_ref, g_ref, dy_ref = refs[:3]
        r_ref = refs[3] if has_res else None
        dx_ref, dg_ref = refs[-2:]
        dx, dg = _rms_bwd_math(x_ref[...], g_ref[...], dy_ref[...].astype(F32))
        if has_res:
            dx = r_ref[...] + dx
        dx_ref[...] = dx

        @pl.when(pl.program_id(0) == 0)
        def _():
            dg_ref[...] = dg

        @pl.when(pl.program_id(0) > 0)
        def _():
            dg_ref[...] += dg

    row = pl.BlockSpec((tm, D), lambda i: (i, 0))
    vec = pl.BlockSpec((1, D), lambda i: (0, 0))
    ins, specs = [x, g.reshape(1, D), dy], [row, vec, row]
    if has_res:
        ins.append(dres)
        specs.append(row)
    dx, dg = pl.pallas_call(
        body, name=name, grid=(T // tm,), in_specs=specs, out_specs=[row, vec],
        out_shape=[jax.ShapeDtypeStruct((T, D), F32), jax.ShapeDtypeStruct((1, D), F32)],
        compiler_params=_params(_nbytes((tm, D), F32) * 4, 6 * _nbytes((tm, D), F32)),
    )(*ins)
    return dx, dg.reshape(D)


def dx_norm_bwd(name, dy, w3, x, g, dres=None, pair_layout=False, after=None):
    T = dy.shape[0]
    _, K, nl = w3.shape
    tm = _tile(T, 512, 8)
    chip = _pair_chip if pair_layout else (lambda j: j)
    has_res = dres is not None

    def body(*refs):
        dy_ref, w_ref, x_ref, g_ref = refs[:4]
        r_ref = refs[4] if has_res else None
        dx_ref, dg_ref, acc = refs[-3:]
        i, k = pl.program_id(0), pl.program_id(1)
        p = lax.dot_general(dy_ref[...].astype(BF16), w_ref[...], NT, preferred_element_type=F32)

        @pl.when(k == 0)
        def _():
            acc[...] = p

        @pl.when(k > 0)
        def _():
            acc[...] += p

        @pl.when(k == N_CHIPS - 1)
        def _():
            dx, dg = _rms_bwd_math(x_ref[...], g_ref[...], acc[...])
            dx_ref[...] = r_ref[...] + dx if has_res else dx

            @pl.when(i == 0)
            def _():
                dg_ref[...] = dg

            @pl.when(i > 0)
            def _():
                dg_ref[...] += dg

    row = pl.BlockSpec((tm, K), lambda i, j: (i, 0))
    vec = pl.BlockSpec((1, K), lambda i, j: (0, 0))
    ins = [dy, w3, x, g.reshape(1, K)]
    specs = [pl.BlockSpec((tm, nl), lambda i, j: (i, j)),
             pl.BlockSpec((None, K, nl), lambda i, j: (chip(j), 0, 0)), row, vec]
    if has_res:
        ins.append(dres)
        specs.append(row)
    if after is not None:
        ins.append(after)
        specs.append(pl.BlockSpec(memory_space=pl.ANY))
    blk = _nbytes((tm, nl), dy.dtype) + _nbytes((K, nl), BF16) + (2 + has_res) * _nbytes((tm, K), F32)
    dx, dg = pl.pallas_call(
        body, name=name, grid=(T // tm, N_CHIPS), in_specs=specs, out_specs=[row, vec],
        out_shape=[jax.ShapeDtypeStruct((T, K), F32), jax.ShapeDtypeStruct((1, K), F32)],
        scratch_shapes=[pltpu.VMEM((tm, K), F32)],
        compiler_params=_params(blk, 8 * _nbytes((tm, K), F32)),
    )(*ins)
    return dx, dg.reshape(K)


def ffn_in_act(name, x, w3):
    T, K = x.shape
    _, _, nl = w3.shape
    tm = _tile(T, 512, 8)

    def body(*refs):
        x_ref, wg_ref, wu_ref = refs[:3]
        u_ref, a_ref = refs[-2:]
        xv = x_ref[...]
        g = jnp.dot(xv, wg_ref[...], preferred_element_type=F32)
        up = jnp.dot(xv, wu_ref[...], preferred_element_type=F32)
        u_ref[:, :nl] = g.astype(u_ref.dtype)
        u_ref[:, nl:] = up.astype(u_ref.dtype)
        a_ref[...] = (g * jax.nn.sigmoid(g) * up).astype(a_ref.dtype)

    blk = _nbytes((tm, K), BF16) + 2 * _nbytes((K, nl), BF16) + _nbytes((tm, 3 * nl), BF16)
    return pl.pallas_call(
        body, name=name, grid=(2, T // tm),
        in_specs=[pl.BlockSpec((tm, K), lambda p, i: (i, 0)),
                  pl.BlockSpec((None, K, nl), lambda p, i: (p, 0, 0)),
                  pl.BlockSpec((None, K, nl), lambda p, i: (p + 2, 0, 0))],
        out_specs=[pl.BlockSpec((tm, 2 * nl), lambda p, i: (i, p)), pl.BlockSpec((tm, nl), lambda p, i: (i, p))],
        out_shape=[jax.ShapeDtypeStruct((T, 4 * nl), BF16), jax.ShapeDtypeStruct((T, 2 * nl), BF16)],
        compiler_params=_params(blk, 4 * _nbytes((tm, nl), F32)),
    )(x, w3, w3)


def ffn_dact(name, dh, w_out, u, after=None):
    T, N = dh.shape
    F = w_out.shape[0]
    nl = F // 2
    tm = _tile(T, 512, 8)

    def body(*refs):
        d_ref, w_ref, u_ref = refs[:3]
        o_ref = refs[-1]
        dact = 0.5 * lax.dot_general(d_ref[...].astype(BF16), w_ref[...], NT, preferred_element_type=F32)
        g = u_ref[:, :nl].astype(F32)
        up = u_ref[:, nl:].astype(F32)
        sig = jax.nn.sigmoid(g)
        o_ref[:, :nl] = (dact * up * (sig * (1.0 + g * (1.0 - sig)))).astype(o_ref.dtype)
        o_ref[:, nl:] = (dact * (g * sig)).astype(o_ref.dtype)

    ins = [dh, w_out, u]
    specs = [pl.BlockSpec((tm, N), lambda p, i: (i, 0)), pl.BlockSpec((nl, N), lambda p, i: (p, 0)),
             pl.BlockSpec((tm, 2 * nl), lambda p, i: (i, p))]
    if after is not None:
        ins.append(after)
        specs.append(pl.BlockSpec(memory_space=pl.ANY))
    blk = _nbytes((tm, N), F32) + _nbytes((nl, N), BF16) + 2 * _nbytes((tm, 2 * nl), BF16)
    return pl.pallas_call(
        body, name=name, grid=(2, T // tm), in_specs=specs,
        out_specs=pl.BlockSpec((tm, 2 * nl), lambda p, i: (i, p)),
        out_shape=jax.ShapeDtypeStruct((T, 2 * F), BF16),
        compiler_params=_params(blk, 6 * _nbytes((tm, nl), F32)),
    )(*ins)


def loss_head(name, h, g, target):
    T, D = h.shape
    tm = _tile(T, 256, 8)

    def body(h_ref, g_ref, t_ref, dh_ref, dg_ref, loss_ref):
        xv = h_ref[...]
        gv = g_ref[...]
        r = lax.rsqrt(jnp.mean(xv * xv, axis=-1, keepdims=True) + EPS)
        err = xv * r * gv - t_ref[...]
        part = 0.5 * jnp.sum(jnp.mean(err * err, axis=-1, keepdims=True), axis=0, keepdims=True)
        dx, dg = _rms_bwd_math(xv, gv, err * (1.0 / D))
        dh_ref[...] = dx
        part = jnp.broadcast_to(part, (1, LANE))

        @pl.when(pl.program_id(0) == 0)
        def _():
            dg_ref[...] = dg
            loss_ref[...] = part

        @pl.when(pl.program_id(0) > 0)
        def _():
            dg_ref[...] += dg
            loss_ref[...] += part

    row = pl.BlockSpec((tm, D), lambda i: (i, 0))
    vec = pl.BlockSpec((1, D), lambda i: (0, 0))
    dh, dg, loss = pl.pallas_call(
        body, name=name, grid=(T // tm,), in_specs=[row, vec, row],
        out_specs=[row, vec, pl.BlockSpec((1, LANE), lambda i: (0, 0))],
        out_shape=[jax.ShapeDtypeStruct((T, D), F32), jax.ShapeDtypeStruct((1, D), F32),
                   jax.ShapeDtypeStruct((1, LANE), F32)],
        compiler_params=_params(_nbytes((tm, D), F32) * 3, 6 * _nbytes((tm, D), F32)),
    )(h, g.reshape(1, D), target)
    return dh, dg.reshape(D), loss


def rope_tables(S):
    half = ROPE // 2
    freqs = ROPE_THETA ** (-jnp.arange(half, dtype=F32) / half)
    ang = jnp.arange(S, dtype=F32)[:, None] * freqs[None, :]
    cos, sin = jnp.cos(ang), jnp.sin(ang)
    z = jnp.zeros_like(cos)
    ct = jnp.concatenate([cos, cos, z, z], axis=1)
    s1 = jnp.concatenate([-sin, z, z, z], axis=1)
    s2 = jnp.concatenate([z, sin, z, z], axis=1)
    return ct, s1, s2


def _rope_tile(t, ct, s1, s2):
    return t * ct + pltpu.roll(t, 96, 1) * s1 + pltpu.roll(t, 32, 1) * s2


def _rope_tile_bwd(d, ct, s1, s2):
    return d * ct + pltpu.roll(d * s1, 32, 1) + pltpu.roll(d * s2, 96, 1)


def uq_rope(name, x, w3, tabs, S):
    T, K = x.shape
    _, _, nl = w3.shape
    tm = _tile(S, 512, 8)
    nt = S // tm

    def body(x_ref, w_ref, ct_ref, s1_ref, s2_ref, o_ref):
        q = jnp.dot(x_ref[...], w_ref[...], preferred_element_type=F32)
        ct, s1, s2 = ct_ref[...], s1_ref[...], s2_ref[...]
        for h in range(nl // 256):
            o_ref[:, 256 * h:256 * h + 128] = q[:, 256 * h:256 * h + 128].astype(o_ref.dtype)
            o_ref[:, 256 * h + 128:256 * h + 256] = _rope_tile(q[:, 256 * h + 128:256 * h + 256],
                                                               ct, s1, s2).astype(o_ref.dtype)

    tab = pl.BlockSpec((tm, LANE), lambda j, i: (i % nt, 0))
    blk = _nbytes((tm, K), BF16) + _nbytes((K, nl), BF16) + _nbytes((tm, nl), BF16) + 3 * _nbytes((tm, LANE), F32)
    return pl.pallas_call(
        body, name=name, grid=(N_CHIPS, T // tm),
        in_specs=[pl.BlockSpec((tm, K), lambda j, i: (i, 0)), pl.BlockSpec((None, K, nl), lambda j, i: (j, 0, 0)),
                  tab, tab, tab],
        out_specs=pl.BlockSpec((tm, nl), lambda j, i: (i, j)),
        out_shape=jax.ShapeDtypeStruct((T, N_CHIPS * nl), BF16),
        compiler_params=_params(blk, 4 * _nbytes((tm, nl), F32)),
    )(x, w3, *tabs)


def kvprep_fwd(name, ckr, g, tabs, B, S):
    T, W = ckr.shape
    KVL = W - LANE
    ts = _tile(S, 256, 8)

    def body(x_ref, g_ref, ct_ref, s1_ref, s2_ref, c_ref, k_ref):
        xv = x_ref[0, :, :KVL]
        r = lax.rsqrt(jnp.mean(xv * xv, axis=-1, keepdims=True) + EPS)
        c_ref[0] = (xv * r * g_ref[...]).astype(c_ref.dtype)
        k_ref[0] = _rope_tile(x_ref[0, :, KVL:], ct_ref[...], s1_ref[...], s2_ref[...]).astype(k_ref.dtype)

    tab = pl.BlockSpec((ts, LANE), lambda b, s: (s, 0))
    c, k = pl.pallas_call(
        body, name=name, grid=(B, S // ts),
        in_specs=[pl.BlockSpec((1, ts, W), lambda b, s: (b, s, 0)), pl.BlockSpec((1, KVL), lambda b, s: (0, 0)),
                  tab, tab, tab],
        out_specs=[pl.BlockSpec((1, ts, KVL), lambda b, s: (b, s, 0)),
                   pl.BlockSpec((1, ts, LANE), lambda b, s: (b, s, 0))],
        out_shape=[jax.ShapeDtypeStruct((B, S, KVL), BF16), jax.ShapeDtypeStruct((B, S, LANE), BF16)],
        compiler_params=_params(_nbytes((ts, W), F32) * 2, _nbytes((ts, W), F32) * 2),
    )(ckr.reshape(B, S, W), g.reshape(1, KVL), *tabs)
    return c.reshape(T, KVL), k


def kvprep_bwd(name, ckr, g, dc, dkr, tabs, B, S):
    T, W = ckr.shape
    KVL = W - LANE
    ts = _tile(S, 256, 8)

    def body(x_ref, g_ref, dc_ref, dk_ref, ct_ref, s1_ref, s2_ref, o_ref, dg_ref):
        dx, dg = _rms_bwd_math(x_ref[0, :, :KVL], g_ref[...], dc_ref[0])
        o_ref[0, :, :KVL] = dx
        o_ref[0, :, KVL:] = _rope_tile_bwd(dk_ref[0], ct_ref[...], s1_ref[...], s2_ref[...])
        first = (pl.program_id(0) == 0) & (pl.program_id(1) == 0)

        @pl.when(first)
        def _():
            dg_ref[...] = dg

        @pl.when(jnp.logical_not(first))
        def _():
            dg_ref[...] += dg

    tab = pl.BlockSpec((ts, LANE), lambda b, s: (s, 0))
    vec = pl.BlockSpec((1, KVL), lambda b, s: (0, 0))
    o, dg = pl.pallas_call(
        body, name=name, grid=(B, S // ts),
        in_specs=[pl.BlockSpec((1, ts, W), lambda b, s: (b, s, 0)), vec,
                  pl.BlockSpec((1, ts, KVL), lambda b, s: (b, s, 0)),
                  pl.BlockSpec((1, ts, LANE), lambda b, s: (b, s, 0)), tab, tab, tab],
        out_specs=[pl.BlockSpec((1, ts, W), lambda b, s: (b, s, 0)), vec],
        out_shape=[jax.ShapeDtypeStruct((B, S, W), F32), jax.ShapeDtypeStruct((1, KVL), F32)],
        compiler_params=_params(_nbytes((ts, W), F32) * 4, _nbytes((ts, W), F32) * 4),
    )(ckr.reshape(B, S, W), g.reshape(1, KVL), dc.reshape(B, S, KVL), dkr, *tabs)
    return o.reshape(T, W), dg.reshape(KVL)


DIAGS = 768


def _diag_onehot():
    col = lax.broadcasted_iota(I32, (REL_PAD, DIAGS), 1)
    row = lax.broadcasted_iota(I32, (REL_PAD, DIAGS), 0)
    idx = jnp.clip(PADR + QROWS - 1 - col, -MAX_REL, MAX_REL) + MAX_REL
    return (row == idx).astype(F32)


def rel_bias_tile(name, table):
    H = table.shape[0]
    tpad = jnp.pad(table, ((0, 0), (0, REL_PAD - table.shape[1])))

    def body(t_ref, o_ref):
        g = lax.dot_general(t_ref[...], _diag_onehot(), NN, precision=lax.Precision.HIGHEST,
                            preferred_element_type=F32)
        qc = jnp.right_shift(lax.broadcasted_iota(I32, (QROWS, WIN), 0), CHUNK_SHIFT)
        kc = jnp.right_shift(lax.broadcasted_iota(I32, (QROWS, WIN), 1), CHUNK_SHIFT)
        band = (kc >= qc) & (kc <= qc + LEFT_CHUNKS)
        for h in range(H):
            gb = jnp.broadcast_to(g[h:h + 1, :], (QROWS, DIAGS))
            tile = pltpu.roll(gb, DIAGS - (QROWS - 1), 1, stride=1, stride_axis=0)
            o_ref[h // 2, (h % 2) * QROWS:(h % 2 + 1) * QROWS, :] = jnp.where(band, tile[:, :WIN], NEG_INF)

    return pl.pallas_call(
        body, name=name, out_shape=jax.ShapeDtypeStruct((H // 2, 2 * QROWS, WIN), F32),
        compiler_params=_params(0, 2 * _nbytes((H // 2, 2 * QROWS, WIN), F32)),
    )(tpad)


def rel_bias_grad(name, dbias):
    H = 2 * dbias.shape[0]

    def body(d_ref, o_ref):
        flip = (lax.broadcasted_iota(I32, (QROWS, QROWS), 0) + lax.broadcasted_iota(I32, (QROWS, QROWS), 1)
                == QROWS - 1).astype(F32)
        rows = []
        for h in range(H):
            x = d_ref[h // 2, (h % 2) * QROWS:(h % 2 + 1) * QROWS, :]
            xr = lax.dot_general(flip, x, NN, precision=lax.Precision.HIGHEST, preferred_element_type=F32)
            xp = jnp.concatenate([xr, jnp.zeros((QROWS, DIAGS - WIN), F32)], axis=1)
            y = pltpu.roll(xp, 0, 1, stride=1, stride_axis=0)
            rows.append(jnp.sum(y, axis=0, keepdims=True))
        o_ref[...] = lax.dot_general(jnp.concatenate(rows, axis=0), _diag_onehot(), NT,
                                     precision=lax.Precision.HIGHEST, preferred_element_type=F32)

    return pl.pallas_call(
        body, name=name, out_shape=jax.ShapeDtypeStruct((H, REL_PAD), F32),
        compiler_params=_params(0, 2 * _nbytes(dbias.shape, F32)),
    )(dbias)


def _stack_pair(xp):
    lane = lax.broadcasted_iota(I32, xp.shape, 1)
    z = jnp.zeros_like(xp)
    return jnp.concatenate([jnp.where(lane < HEAD_DIM_A, xp, z), jnp.where(lane >= HEAD_DIM_A, xp, z)], axis=0)


def _unstack_pair(y):
    lane = lax.broadcasted_iota(I32, (QROWS, LANE), 1)
    return jnp.where(lane < HEAD_DIM_A, y[:QROWS], y[QROWS:])


def _attn_a_rowpen(j):
    w = lax.broadcasted_iota(I32, (1, WIN), 1)
    return jnp.where(w >= PADR - QROWS * j, 0.0, NEG_INF).astype(F32)


def _attn_a_load_bias(bias_hbm, bias_v, sem):
    cp = pltpu.make_async_copy(bias_hbm, bias_v, sem)
    cp.start()
    cp.wait()


def _attn_a_load_kv(qkv_hbm, b, kpad, vpad, sem, S, D):
    kpad[0:PADR, :] = jnp.zeros((PADR, D), BF16)
    vpad[0:PADR, :] = jnp.zeros((PADR, D), BF16)
    ck = pltpu.make_async_copy(qkv_hbm.at[b, :, pl.ds(D, D)], kpad.at[pl.ds(PADR, S), :], sem.at[0])
    cv = pltpu.make_async_copy(qkv_hbm.at[b, :, pl.ds(2 * D, D)], vpad.at[pl.ds(PADR, S), :], sem.at[1])
    ck.start()
    cv.start()
    ck.wait()
    cv.wait()


def _attn_a_exp(q2s, kp, bias, pen):
    s = lax.dot_general(q2s, kp, NT, preferred_element_type=F32) + bias + pen
    e = jnp.exp(s - jnp.max(s, axis=-1, keepdims=True))
    return e, 1.0 / jnp.sum(e, axis=-1, keepdims=True)


def attn_a_fwd(name, qkv, bias):
    B, S, D3 = qkv.shape
    D = D3 // 3
    H = D // HEAD_DIM_A
    nb = S // QROWS
    scale = HEAD_DIM_A ** -0.5

    def body(q_ref, bias_hbm, qkv_hbm, o_ref, kpad, vpad, bias_v, sem):
        b, j = pl.program_id(0), pl.program_id(1)

        @pl.when((b == 0) & (j == 0))
        def _():
            _attn_a_load_bias(bias_hbm, bias_v, sem.at[2])

        @pl.when(j == 0)
        def _():
            _attn_a_load_kv(qkv_hbm, b, kpad, vpad, sem, S, D)

        pen = _attn_a_rowpen(j)
        w0 = pl.multiple_of(j * QROWS, QROWS)
        for p in range(H // 2):
            ls = slice(p * LANE, (p + 1) * LANE)
            e, rl = _attn_a_exp(_stack_pair(q_ref[0, :, ls] * scale), kpad[pl.ds(w0, WIN), ls], bias_v[p], pen)
            o2 = jnp.dot(e.astype(BF16), vpad[pl.ds(w0, WIN), ls], preferred_element_type=F32) * rl
            o_ref[0, :, ls] = _unstack_pair(o2).astype(o_ref.dtype)

    scr = 2 * _nbytes((PADR + S, D), BF16) + _nbytes(bias.shape, F32) + 8 * _nbytes((2 * QROWS, WIN), F32)
    return pl.pallas_call(
        body, name=name, grid=(B, nb),
        in_specs=[pl.BlockSpec((1, QROWS, D), lambda b, j: (b, j, 0)),
                  pl.BlockSpec(memory_space=pl.ANY), pl.BlockSpec(memory_space=pl.ANY)],
        out_specs=pl.BlockSpec((1, QROWS, D), lambda b, j: (b, j, 0)),
        out_shape=jax.ShapeDtypeStruct((B, S, D), BF16),
        scratch_shapes=[pltpu.VMEM((PADR + S, D), BF16), pltpu.VMEM((PADR + S, D), BF16),
                        pltpu.VMEM(bias.shape, F32), pltpu.SemaphoreType.DMA((3,))],
        compiler_params=_params(2 * _nbytes((QROWS, D), BF16), scr),
    )(qkv, bias, qkv)


def attn_a_bwd(name, qkv, do, bias):
    B, S, D3 = qkv.shape
    D = D3 // 3
    H = D // HEAD_DIM_A
    nb = S // QROWS
    scale = HEAD_DIM_A ** -0.5

    def body(q_ref, do_ref, bias_hbm, qkv_hbm, dqkv_hbm, dbias_hbm, kpad, vpad, dkacc, dvacc, bias_v, dbias_v,
             dq_stage, sem):
        b, j = pl.program_id(0), pl.program_id(1)
        step = b * nb + j
        slot = lax.rem(step, 2)

        def dq_out(s):
            return pltpu.make_async_copy(dq_stage.at[s], dqkv_hbm.at[b, pl.ds(j * QROWS, QROWS), pl.ds(0, D)],
                                         sem.at[3 + s])

        @pl.when(step >= 2)
        def _():
            dq_out(slot).wait()

        @pl.when((b == 0) & (j == 0))
        def _():
            _attn_a_load_bias(bias_hbm, bias_v, sem.at[2])
            dbias_v[...] = jnp.zeros_like(dbias_v)

        @pl.when(j == 0)
        def _():
            _attn_a_load_kv(qkv_hbm, b, kpad, vpad, sem, S, D)
            dkacc[...] = jnp.zeros_like(dkacc)
            dvacc[...] = jnp.zeros_like(dvacc)

        pen = _attn_a_rowpen(j)
        w0 = pl.multiple_of(j * QROWS, QROWS)
        for p in range(H // 2):
            ls = slice(p * LANE, (p + 1) * LANE)
            q2s = _stack_pair(q_ref[0, :, ls] * scale)
            do2 = _stack_pair(do_ref[0, :, ls])
            kp = kpad[pl.ds(w0, WIN), ls]
            vp = vpad[pl.ds(w0, WIN), ls]
            e, rl = _attn_a_exp(q2s, kp, bias_v[p], pen)
            pr = e * rl
            dp = lax.dot_general(do2, vp, NT, preferred_element_type=F32)
            ds = pr * (dp - jnp.sum(pr * dp, axis=-1, keepdims=True))
            dbias_v[p] += ds
            dsb = ds.astype(BF16)
            dq_stage[slot, :, ls] = (_unstack_pair(jnp.dot(dsb, kp, preferred_element_type=F32))
                                     * scale).astype(dq_stage.dtype)
            dkacc[pl.ds(w0, WIN), ls] += lax.dot_general(dsb, q2s, TN, preferred_element_type=F32)
            dvacc[pl.ds(w0, WIN), ls] += lax.dot_general(pr.astype(BF16), do2, TN, preferred_element_type=F32)

        dq_out(slot).start()

        @pl.when(j == nb - 1)
        def _():
            kpad[pl.ds(PADR, S), :] = dkacc[pl.ds(PADR, S), :].astype(BF16)
            vpad[pl.ds(PADR, S), :] = dvacc[pl.ds(PADR, S), :].astype(BF16)
            ck = pltpu.make_async_copy(kpad.at[pl.ds(PADR, S), :], dqkv_hbm.at[b, :, pl.ds(D, D)], sem.at[0])
            cv = pltpu.make_async_copy(vpad.at[pl.ds(PADR, S), :], dqkv_hbm.at[b, :, pl.ds(2 * D, D)], sem.at[1])
            ck.start()
            cv.start()
            ck.wait()
            cv.wait()

        @pl.when((b == B - 1) & (j == nb - 1))
        def _():
            cb = pltpu.make_async_copy(dbias_v, dbias_hbm, sem.at[2])
            cb.start()
            dq_out(0).wait()
            dq_out(1).wait()
            cb.wait()

    blk = _nbytes((QROWS, D), BF16) * 2
    scr = (2 * _nbytes((PADR + S, D), BF16) + 2 * _nbytes((PADR + S, D), F32) + 2 * _nbytes(bias.shape, F32)
           + 8 * _nbytes((2 * QROWS, WIN), F32) + 2 * _nbytes((QROWS, D), F32))
    return pl.pallas_call(
        body, name=name, grid=(B, nb),
        in_specs=[pl.BlockSpec((1, QROWS, D), lambda b, j: (b, j, 0)),
                  pl.BlockSpec((1, QROWS, D), lambda b, j: (b, j, 0)),
                  pl.BlockSpec(memory_space=pl.ANY), pl.BlockSpec(memory_space=pl.ANY)],
        out_specs=[pl.BlockSpec(memory_space=pl.ANY), pl.BlockSpec(memory_space=pl.ANY)],
        out_shape=[jax.ShapeDtypeStruct((B, S, 3 * D), BF16), jax.ShapeDtypeStruct(bias.shape, F32)],
        scratch_shapes=[pltpu.VMEM((PADR + S, D), BF16), pltpu.VMEM((PADR + S, D), BF16),
                        pltpu.VMEM((PADR + S, D), F32), pltpu.VMEM((PADR + S, D), F32),
                        pltpu.VMEM(bias.shape, F32), pltpu.VMEM(bias.shape, F32),
                        pltpu.VMEM((2, QROWS, D), BF16), pltpu.SemaphoreType.DMA((5,))],
        compiler_params=_params(blk, scr),
    )(qkv, do, bias, qkv)


def _mla_raw_t(k2, kj, q, QB):
    return lax.dot_general(k2[_blk(kj, QB), :], q, NT, preferred_element_type=F32)


def _blk(kj, QB):
    return pl.ds(kj * QB, QB) if isinstance(kj, int) else pl.ds(pl.multiple_of(kj * QB, QB), QB)


def _mla_diag_pen(QB):
    kc = jnp.right_shift(lax.broadcasted_iota(I32, (QB, QB), 0), CHUNK_SHIFT)
    qc = jnp.right_shift(lax.broadcasted_iota(I32, (QB, QB), 1), CHUNK_SHIFT)
    return jnp.where(kc <= qc, 0.0, NEG_INF).astype(F32)


def _mla_fill_keys(kv_ref, kr_ref, k2):
    k2[:, :NOPE] = kv_ref[0, :, :NOPE]
    k2[:, NOPE:] = kr_ref[0]


def _t(x):
    return x.astype(F32).T


def mla_fwd(name, qf, kv, kr):
    B, S, W = qf.shape
    HB = W // 256
    QB = _tile(S, 256, CHUNK)
    nq = S // QB
    scale = (NOPE + ROPE) ** -0.5

    def body(q_ref, kv_ref, kr_ref, o_ref, lse_ref, k2, vt, st_buf, pen):
        qi = pl.program_id(2)

        @pl.when(qi == 0)
        def _():
            pen[...] = _mla_diag_pen(QB)
            _mla_fill_keys(kv_ref, kr_ref, k2)
            for kj in range(nq):
                vt[kj] = _t(kv_ref[0, kj * QB:(kj + 1) * QB, NOPE:]).astype(BF16)

        q = q_ref[0]
        st_buf[0] = _mla_raw_t(k2, 0, q, QB)

        def step(kj, carry):
            m, l, acc = carry
            cur = lax.rem(kj, 2)
            st_raw = st_buf[cur]
            st_buf[1 - cur] = _mla_raw_t(k2, jnp.minimum(kj + 1, qi), q, QB)
            st = st_raw * scale + jnp.where(kj == qi, pen[...], 0.0)
            m_new = jnp.maximum(m, jnp.max(st, axis=0, keepdims=True))
            a = jnp.exp(m - m_new)
            pt = jnp.exp(st - m_new)
            l = a * l + jnp.sum(pt, axis=0, keepdims=True)
            acc = a * acc + jnp.dot(vt[kj], pt.astype(BF16), preferred_element_type=F32)
            return m_new, l, acc

        init = (jnp.full((1, QB), NEG_INF, F32), jnp.zeros((1, QB), F32), jnp.zeros((NOPE, QB), F32))
        m, l, acc = lax.fori_loop(0, qi + 1, step, init)
        o_ref[0] = (acc * (1.0 / l)).T
        lse_ref[0, 0] = m + jnp.log(l)

    blk = (_nbytes((QB, 256), BF16) + _nbytes((S, 256), BF16) + _nbytes((S, LANE), BF16)
           + _nbytes((QB, LANE), F32))
    return pl.pallas_call(
        body, name=name, grid=(B, HB, nq),
        in_specs=[pl.BlockSpec((1, QB, 256), lambda b, h, i: (b, i, h)),
                  pl.BlockSpec((1, S, 256), lambda b, h, i: (b, 0, h)),
                  pl.BlockSpec((1, S, LANE), lambda b, h, i: (b, 0, 0))],
        out_specs=[pl.BlockSpec((1, QB, LANE), lambda b, h, i: (b, i, h)),
                   pl.BlockSpec((1, 1, 1, QB), lambda b, h, i: (b, h, 0, i))],
        out_shape=[jax.ShapeDtypeStruct((B, S, HB * LANE), F32), jax.ShapeDtypeStruct((B, HB, 1, S), F32)],
        scratch_shapes=[pltpu.VMEM((S, 256), BF16), pltpu.VMEM((nq, NOPE, QB), BF16),
                        pltpu.VMEM((2, QB, QB), F32), pltpu.VMEM((QB, QB), F32)],
        compiler_params=_params(blk, 2 * _nbytes((S, 256), BF16) + 10 * _nbytes((QB, QB), F32)),
    )(qf, kv, kr)


def mla_bwd(name, qf, kv, kr, do, o, lse, tabs):
    B, S, W = qf.shape
    HB = W // 256
    QB = _tile(S, 256, CHUNK)
    nq = S // QB
    scale = (NOPE + ROPE) ** -0.5

    def body(q_ref, kv_ref, kr_ref, do_ref, o_ref, lse_ref, ct_ref, s1_ref, s2_ref, dq_ref, dkv_ref, dkr_ref,
             k2, kt, dot_, delta, dqt, st_buf, dp_buf, pen, dkv_acc):
        h = pl.program_id(1)
        pen[...] = _mla_diag_pen(QB)
        dkv_acc[...] = jnp.zeros_like(dkv_acc)

        @pl.when(h == 0)
        def _():
            dkr_ref[...] = jnp.zeros_like(dkr_ref)

        _mla_fill_keys(kv_ref, kr_ref, k2)
        for i in range(nq):
            rows = slice(i * QB, (i + 1) * QB)
            kt[i] = _t(k2[rows, :]).astype(BF16)
            dot32 = _t(do_ref[0, rows, :])
            delta[i] = jnp.sum(dot32 * o_ref[0, rows, :].T, axis=0, keepdims=True)
            dot_[i] = dot32.astype(BF16)

        for qi in range(nq):
            rows = slice(qi * QB, (qi + 1) * QB)
            q = q_ref[0, rows, :]
            dob = do_ref[0, rows, :]
            lse_q = lse_ref[0, 0, :, rows]
            delta_q = delta[qi]
            dqt[...] = jnp.zeros_like(dqt)

            def raw(kj, slot, q=q, qi=qi):
                st_buf[slot] = _mla_raw_t(k2, kj, q, QB)
                dp_buf[slot] = jnp.dot(kv_ref[0, _blk(kj, QB), NOPE:], dot_[qi], preferred_element_type=F32)

            raw(0, 0)

            def step(kj, carry, q=q, dob=dob, lse_q=lse_q, delta_q=delta_q, qi=qi, raw=raw):
                ks = pl.ds(pl.multiple_of(kj * QB, QB), QB)
                cur = lax.rem(kj, 2)
                st_raw, dp_raw = st_buf[cur], dp_buf[cur]
                raw(jnp.minimum(kj + 1, qi), 1 - cur)
                pt = jnp.exp(st_raw * scale + jnp.where(kj == qi, pen[...], 0.0) - lse_q)
                dst = (pt * (dp_raw - delta_q) * scale).astype(BF16)
                dkv_acc[ks, NOPE:] += jnp.dot(pt.astype(BF16), dob, preferred_element_type=F32)
                dk2 = jnp.dot(dst, q, preferred_element_type=F32)
                dkv_acc[ks, :NOPE] += dk2[:, :NOPE]
                dkr_ref[0, ks, :] += dk2[:, NOPE:]
                dqt[...] += jnp.dot(kt[kj], dst, preferred_element_type=F32)
                return carry

            lax.fori_loop(0, qi + 1, step, 0)
            dq = dqt[...].T
            dq_ref[0, rows, :NOPE] = dq[:, :NOPE].astype(dq_ref.dtype)
            dq_ref[0, rows, NOPE:] = _rope_tile_bwd(dq[:, NOPE:], ct_ref[rows, :], s1_ref[rows, :],
                                                    s2_ref[rows, :]).astype(dq_ref.dtype)

        dkv_ref[0] = dkv_acc[...].astype(dkv_ref.dtype)

    head = lambda w: pl.BlockSpec((1, S, w), lambda b, h: (b, 0, h))
    shared = pl.BlockSpec((1, S, LANE), lambda b, h: (b, 0, 0))
    blk = (2 * _nbytes((S, 256), BF16) + 2 * _nbytes((S, LANE), BF16) + _nbytes((S, LANE), F32)
           + 2 * _nbytes((S, 256), F32) + _nbytes((S, LANE), F32))
    scr = 3 * _nbytes((S, 256), BF16) + 14 * _nbytes((QB, QB), F32)
    return pl.pallas_call(
        body, name=name, grid=(B, HB),
        in_specs=[head(256), head(256), shared, head(LANE), head(LANE),
                  pl.BlockSpec((1, 1, 1, S), lambda b, h: (b, h, 0, 0))]
        + [pl.BlockSpec((S, LANE), lambda b, h: (0, 0))] * 3,
        out_specs=[head(256), head(256), shared],
        out_shape=[jax.ShapeDtypeStruct((B, S, W), BF16), jax.ShapeDtypeStruct((B, S, W), BF16),
                   jax.ShapeDtypeStruct((B, S, LANE), F32)],
        scratch_shapes=[pltpu.VMEM((S, 256), BF16), pltpu.VMEM((nq, 256, QB), BF16),
                        pltpu.VMEM((nq, NOPE, QB), BF16), pltpu.VMEM((nq, 1, QB), F32),
                        pltpu.VMEM((256, QB), F32), pltpu.VMEM((2, QB, QB), F32), pltpu.VMEM((2, QB, QB), F32),
                        pltpu.VMEM((QB, QB), F32), pltpu.VMEM((S, 256), F32)],
        compiler_params=_params(blk, scr),
    )(qf, kv, kr, do, o, lse, *tabs)


GROUP_STEPS = 4


def cast_group(name, ws, layers, idx, after=None):
    n = len(ws)
    n_in = n + (after is not None)

    def body(k_ref, *refs):
        for i in range(n):
            refs[n_in + i][...] = refs[i][...].astype(BF16)

    def spec_in(w, layer):
        return pl.BlockSpec((None, w.shape[1] // GROUP_STEPS, w.shape[2]), lambda r, k_ref: (layer, r, 0))

    def spec_out(w):
        return pl.BlockSpec((None, w.shape[1] // GROUP_STEPS, w.shape[2]), lambda r, k_ref: (k_ref[0], r, 0))

    return pl.pallas_call(
        body, name=name,
        grid_spec=pltpu.PrefetchScalarGridSpec(
            num_scalar_prefetch=1, grid=(GROUP_STEPS,),
            in_specs=([spec_in(w, l) for w, l in zip(ws, layers)]
                      + [pl.BlockSpec(memory_space=pl.ANY)] * (after is not None)),
            out_specs=[spec_out(w) for w in ws]),
        out_shape=[jax.ShapeDtypeStruct((N_CHIPS, *w.shape[1:]), BF16) for w in ws],
        compiler_params=_params(sum(_nbytes(w.shape[1:], F32) * 3 // 2 for w in ws) // GROUP_STEPS),
    )(idx, *ws, *([] if after is None else [after]))


def adamw(name, w, g, m, v):
    R, C = w.shape
    tr = _tile(R, max(8, (1 << 18) // C // 8 * 8), 8)
    c1 = 1.0 - ADAM_B1 ** ADAM_STEP
    c2 = 1.0 - ADAM_B2 ** ADAM_STEP

    def body(w_ref, g_ref, m_ref, v_ref, d_ref, mo_ref, vo_ref):
        gv = g_ref[...]
        mn = ADAM_B1 * m_ref[...] + (1.0 - ADAM_B1) * gv
        vn = ADAM_B2 * v_ref[...] + (1.0 - ADAM_B2) * (gv * gv)
        mo_ref[...] = mn
        vo_ref[...] = vn
        d_ref[...] = -ADAM_LR * ((mn / c1) / (jnp.sqrt(vn / c2) + ADAM_EPS) + ADAM_WD * w_ref[...])

    spec = pl.BlockSpec((tr, C), lambda r: (r, 0))
    return pl.pallas_call(
        body, name=name, grid=(R // tr,), in_specs=[spec] * 4, out_specs=[spec] * 3,
        out_shape=[jax.ShapeDtypeStruct((R, C), F32)] * 3,
        compiler_params=_params(7 * _nbytes((tr, C), F32), 4 * _nbytes((tr, C), F32)),
    )(w, g, m, v)


def half_sum_group(name, dws, landed, idx):
    n = len(dws)
    steps = GROUP_STEPS // 2

    def body(i_ref, *refs):
        for i in range(n):
            refs[2 * n + i][...] = (refs[i][...].astype(F32) + refs[n + i][...].astype(F32)).astype(BF16)

    def own(d):
        return pl.BlockSpec((None, None, d.shape[2] // steps, d.shape[3]), lambda k, r, i_ref: (k, i_ref[1], r, 0))

    def flat(d):
        return pl.BlockSpec((None, d.shape[2] // steps, d.shape[3]), lambda k, r, i_ref: (k, r, 0))

    return pl.pallas_call(
        body, name=name,
        grid_spec=pltpu.PrefetchScalarGridSpec(
            num_scalar_prefetch=1, grid=(N_CHIPS, steps),
            in_specs=[own(d) for d in dws] + [flat(d) for d in dws], out_specs=[flat(d) for d in dws]),
        out_shape=[jax.ShapeDtypeStruct((N_CHIPS, *d.shape[2:]), BF16) for d in dws],
        compiler_params=_params(sum(3 * _nbytes(d.shape[2:], BF16) for d in dws) // steps),
    )(idx, *dws, *landed)


def chip_sum_group(name, parts, landed, gbufs, layers, idx):
    n = len(parts)
    steps = GROUP_STEPS // 2

    def body(i_ref, *refs):
        for i in range(n):
            a, b = refs[i], refs[n + i]
            refs[3 * n + i][...] = ((a[...].astype(F32) + b[0].astype(F32)) + b[1].astype(F32)) + b[2].astype(F32)

    def mine(p):
        return pl.BlockSpec((None, p.shape[1] // steps, p.shape[2]), lambda r, i_ref: (i_ref[0], r, 0))

    def three(p):
        return pl.BlockSpec((3, p.shape[1] // steps, p.shape[2]), lambda r, i_ref: (0, r, 0))

    def out(p, layer):
        return pl.BlockSpec((None, None, p.shape[1] // steps, p.shape[2]), lambda r, i_ref: (layer, i_ref[1], r, 0))

    return pl.pallas_call(
        body, name=name,
        grid_spec=pltpu.PrefetchScalarGridSpec(
            num_scalar_prefetch=1, grid=(steps,),
            in_specs=[mine(p) for p in parts] + [three(p) for p in parts] + [pl.BlockSpec(memory_space=pl.ANY)] * n,
            out_specs=[out(p, l) for p, l in zip(parts, layers)]),
        out_shape=[jax.ShapeDtypeStruct(g.shape, F32) for g in gbufs],
        input_output_aliases={1 + 2 * n + i: i for i in range(n)},
        compiler_params=_params(sum(6 * _nbytes(p.shape[1:], BF16) for p in parts) // steps),
    )(idx, *parts, *landed, *gbufs)


ANY = pl.BlockSpec(memory_space=pl.ANY)


def _place():
    x, y, c = lax.axis_index("x"), lax.axis_index("y"), lax.axis_index("c")
    chips = [(1 - x, y), (x, 1 - y), (1 - x, 1 - y)]
    return x, y, c, chips


HBM = pl.BlockSpec(memory_space=pltpu.HBM)
SEM = pl.BlockSpec(memory_space=pltpu.SEMAPHORE)
EFFECT = pltpu.SideEffectType.DATAFLOW_SIDE_EFFECTING


def _in_hbm(a):
    return pltpu.with_memory_space_constraint(a, pltpu.HBM)


def _ici_copy(src, dst, send_sems, recv_sems, k, peer):
    return pltpu.make_async_remote_copy(src_ref=src, dst_ref=dst, send_sem=send_sems.at[k], recv_sem=recv_sems.at[k],
                                        device_id=peer, device_id_type=MESH)


def ici_start(name, bufs, lands, after, gather):
    n, nl = len(bufs), len(lands)

    def body(*refs):
        b_in = refs[:n]
        send_sems, recv_sems = refs[n + nl + 1], refs[n + nl + 2]
        b_out = refs[n + nl + 3:2 * n + nl + 3]
        l_out = refs[2 * n + nl + 3:2 * n + 2 * nl + 3]
        token = refs[-1]
        x, y, c, chips = _place()
        kme = 2 * x + y
        for i in range(n):
            for j in range(3):
                peer = (*chips[j], c)
                if gather:
                    _ici_copy(b_out[i].at[kme, c], b_out[i].at[kme, c], send_sems, recv_sems, 3 * i + j, peer).start()
                else:
                    kd = 2 * chips[j][0] + chips[j][1]
                    _ici_copy(b_out[i].at[kd], l_out[i].at[j], send_sems, recv_sems, 3 * i + j, peer).start()
        token[...] = jnp.zeros_like(token)

    arrays = [*bufs, *lands]
    outs = pl.pallas_call(
        body, name=name,
        in_specs=[HBM] * (n + nl) + [ANY],
        out_specs=(SEM, SEM, *[HBM] * (n + nl), pl.BlockSpec(memory_space=pltpu.VMEM)),
        out_shape=(pltpu.SemaphoreType.DMA((3 * n,)), pltpu.SemaphoreType.DMA((3 * n,)),
                   *[pltpu.HBM(a.shape, a.dtype) for a in arrays], jax.ShapeDtypeStruct((8, LANE), F32)),
        input_output_aliases={i: 2 + i for i in range(n + nl)},
        compiler_params=pltpu.CompilerParams(has_side_effects=EFFECT),
    )(*[_in_hbm(a) for a in arrays], after)
    return outs[0], outs[1], list(outs[2:2 + n]), list(outs[2 + n:2 + n + nl]), outs[-1]


def ici_wait(name, send_sems, recv_sems, bufs, lands, after, gather):
    n, nl = len(bufs), len(lands)

    def body(*refs):
        b_in, l_in = refs[:n], refs[n:n + nl]
        send_sems, recv_sems = refs[n + nl], refs[n + nl + 1]
        x, y, c, chips = _place()
        kme = 2 * x + y
        for i in range(n):
            for j in range(3):
                peer = (*chips[j], c)
                kj = 2 * chips[j][0] + chips[j][1]
                if gather:
                    _ici_copy(b_in[i].at[kme, c], b_in[i].at[kme, c], send_sems, recv_sems, 3 * i + j, peer).wait_send()
                    _ici_copy(b_in[i].at[kj, c], b_in[i].at[kj, c], send_sems, recv_sems, 3 * i + j, peer).wait_recv()
                else:
                    _ici_copy(b_in[i].at[kj], l_in[i].at[j], send_sems, recv_sems, 3 * i + j, peer).wait_send()
                    _ici_copy(b_in[i].at[kj], l_in[i].at[j], send_sems, recv_sems, 3 * i + j, peer).wait_recv()

    arrays = [*bufs, *lands]
    outs = pl.pallas_call(
        body, name=name,
        in_specs=[HBM] * (n + nl) + [SEM, SEM, ANY],
        out_specs=tuple([HBM] * (n + nl)),
        out_shape=tuple(pltpu.HBM(a.shape, a.dtype) for a in arrays),
        input_output_aliases={i: i for i in range(n + nl)},
        compiler_params=pltpu.CompilerParams(has_side_effects=EFFECT),
    )(*arrays, send_sems, recv_sems, after)
    return list(outs[:n]), list(outs[n:])


def gather_pair_pass(name, bufs):
    n = len(bufs)

    def body(*refs):
        b = refs[n:2 * n]
        send_sems, recv_sems = refs[2 * n:]
        x, y, c, chips = _place()
        sib = (x, y, 1 - c)

        def d2d(i, j, which):
            kj = 2 * chips[j][0] + chips[j][1]
            return _ici_copy(b[i].at[kj, which], b[i].at[kj, which], send_sems, recv_sems, 3 * i + j, sib)

        for i in range(n):
            for j in range(3):
                d2d(i, j, c).start()
        for i in range(n):
            for j in range(3):
                d2d(i, j, 1 - c).wait_recv()
        for i in range(n):
            for j in range(3):
                d2d(i, j, c).wait_send()

    return pl.pallas_call(
        body, name=name, in_specs=[ANY] * n, out_specs=[ANY] * n,
        out_shape=[jax.ShapeDtypeStruct(a.shape, a.dtype) for a in bufs],
        input_output_aliases={i: i for i in range(n)},
        scratch_shapes=[pltpu.SemaphoreType.DMA((3 * n,)), pltpu.SemaphoreType.DMA((3 * n,))],
    )(*bufs)


def pair_exchange(name, dws):
    n = len(dws)

    def body(*refs):
        ins, outs = refs[:n], refs[n:2 * n]
        send_sems, recv_sems = refs[2 * n:]
        x, y, c, _ = _place()
        copies = []
        for i in range(n):
            copies.append(pltpu.make_async_remote_copy(
                src_ref=ins[i].at[:, 1 - c], dst_ref=outs[i],
                send_sem=send_sems.at[i], recv_sem=recv_sems.at[i],
                device_id=(x, y, 1 - c), device_id_type=MESH))
            copies[i].start()
        for cp in copies:
            cp.wait_recv()
        for cp in copies:
            cp.wait_send()

    return pl.pallas_call(
        body, name=name, in_specs=[ANY] * n, out_specs=[ANY] * n,
        out_shape=[jax.ShapeDtypeStruct((N_CHIPS, *d.shape[2:]), d.dtype) for d in dws],
        scratch_shapes=[pltpu.SemaphoreType.DMA((n,)), pltpu.SemaphoreType.DMA((n,))],
    )(*dws)


def pair_assemble(gbufs):
    n = len(gbufs)

    def body(*refs):
        bufs = refs[n:2 * n]
        send_sems, recv_sems = refs[2 * n:]
        x, y, c, _ = _place()
        copies = []
        for i in range(n):
            copies.append(pltpu.make_async_remote_copy(
                src_ref=bufs[i].at[:, c], dst_ref=bufs[i].at[:, c],
                send_sem=send_sems.at[i], recv_sem=recv_sems.at[i],
                device_id=(x, y, 1 - c), device_id_type=MESH))
            copies[i].start()
        for i in range(n):
            pltpu.make_async_remote_copy(
                src_ref=bufs[i].at[:, 1 - c], dst_ref=bufs[i].at[:, 1 - c],
                send_sem=send_sems.at[i], recv_sem=recv_sems.at[i],
                device_id=(x, y, 1 - c), device_id_type=MESH).wait_recv()
        for cp in copies:
            cp.wait_send()

    return pl.pallas_call(
        body, name="grad_pair_assemble", in_specs=[ANY] * n, out_specs=[ANY] * n,
        out_shape=[jax.ShapeDtypeStruct(g.shape, g.dtype) for g in gbufs],
        input_output_aliases={i: i for i in range(n)},
        scratch_shapes=[pltpu.SemaphoreType.DMA((n,)), pltpu.SemaphoreType.DMA((n,))],
    )(*gbufs)


def all_reduce_small(vec):
    NR = vec.shape[0]
    flips = [(fx, fy, fc) for fx in (0, 1) for fy in (0, 1) for fc in (0, 1)][1:]

    def body(v_ref, o_ref, buf, send_sems, recv_sems):
        x, y, c, _ = _place()
        me = 4 * x + 2 * y + c
        buf[me] = v_ref[...]
        copies = []
        for j, (fx, fy, fc) in enumerate(flips):
            peer = (1 - x if fx else x, 1 - y if fy else y, 1 - c if fc else c)
            copies.append(pltpu.make_async_remote_copy(
                src_ref=v_ref, dst_ref=buf.at[me], send_sem=send_sems.at[j], recv_sem=recv_sems.at[j],
                device_id=peer, device_id_type=MESH))
            copies[j].start()
        for cp in copies:
            cp.wait_recv()
        for cp in copies:
            cp.wait_send()
        acc = buf[0]
        for d in range(1, 8):
            acc = acc + buf[d]
        o_ref[...] = acc

    return pl.pallas_call(
        body, name="all_reduce_small",
        in_specs=[pl.BlockSpec(memory_space=pltpu.VMEM)], out_specs=pl.BlockSpec(memory_space=pltpu.VMEM),
        out_shape=jax.ShapeDtypeStruct((NR, LANE), F32),
        scratch_shapes=[pltpu.VMEM((8, NR, LANE), F32), pltpu.SemaphoreType.DMA((7,)),
                        pltpu.SemaphoreType.DMA((7,))],
    )(vec)


def _pack(arrays):
    flat = jnp.concatenate([a.reshape(-1).astype(F32) for a in arrays])
    n = flat.shape[0]
    npad = -(-n // (8 * LANE)) * (8 * LANE)
    return jnp.pad(flat, (0, npad - n)).reshape(npad // LANE, LANE)


def _unpack(buf, like):
    flat = buf.reshape(-1)
    out, off = [], 0
    for a in like:
        out.append(flat[off:off + a.size].reshape(a.shape))
        off += a.size
    return out


def kernel(x, ffn1_norm, ffn1_w_in, ffn1_w_out, mix_norm, ffn2_norm, ffn2_w_in, ffn2_w_out, a_w_qkv, a_rel_bias, a_w_o, kv_norm, kv_w_down, kv_latent_norm, kv_w_up, b_w_dq, b_q_norm, b_w_uq, b_w_o, final_norm, loss_target, m_ffn1_norm, m_ffn1_w_in, m_ffn1_w_out, m_mix_norm, m_ffn2_norm, m_ffn2_w_in, m_ffn2_w_out, m_a_w_qkv, m_a_rel_bias, m_a_w_o, m_kv_norm, m_kv_w_down, m_kv_latent_norm, m_kv_w_up, m_b_w_dq, m_b_q_norm, m_b_w_uq, m_b_w_o, m_final_norm, v_ffn1_norm, v_ffn1_w_in, v_ffn1_w_out, v_mix_norm, v_ffn2_norm, v_ffn2_w_in, v_ffn2_w_out, v_a_w_qkv, v_a_rel_bias, v_a_w_o, v_kv_norm, v_kv_w_down, v_kv_latent_norm, v_kv_w_up, v_b_w_dq, v_b_q_norm, v_b_w_uq, v_b_w_o, v_final_norm):
    B, S, D = x.shape
    T = B * S
    HB = D // 128
    QL = b_q_norm.shape[-1]
    KVL = kv_latent_norm.shape[0]
    hpc = HB // N_CHIPS
    tabs = rope_tables(S)
    idx = jnp.stack([2 * lax.axis_index("x") + lax.axis_index("y"), lax.axis_index("c")]).astype(I32)

    def halves(a):
        return a.reshape(*a.shape[:-2], 2, a.shape[-2] // 2, a.shape[-1])

    def whole(a):
        return a.reshape(*a.shape[:-3], 2 * a.shape[-2], a.shape[-1])

    kv_w_down_p = jnp.pad(kv_w_down, ((0, 0), (0, LANE - ROPE)))[None]
    b_w_uq_p = jnp.pad(b_w_uq.reshape(1, QL, hpc, NOPE + ROPE),
                       ((0, 0), (0, 0), (0, 0), (0, LANE - ROPE))).reshape(1, QL, hpc * 256)
    sharded = [("ffn1_w_in", ffn1_w_in), ("ffn1_w_out", ffn1_w_out), ("ffn2_w_in", ffn2_w_in),
               ("ffn2_w_out", ffn2_w_out), ("a_w_qkv", a_w_qkv), ("a_w_o", a_w_o),
               ("kv_w_down", kv_w_down_p), ("kv_w_up", kv_w_up[None]), ("b_w_dq", b_w_dq),
               ("b_w_uq", b_w_uq_p), ("b_w_o", b_w_o)]
    names = [nm for nm, _ in sharded]
    shard_of = dict(sharded)
    W = {}

    gather_groups = [
        [("ffn1_w_in", 0)],
        [("ffn1_w_out", 0)],
        [("a_w_qkv", 0), ("a_w_o", 0)],
        [("ffn2_w_in", 0), ("ffn2_w_out", 0), ("kv_w_down", 0), ("kv_w_up", 0)],
        [("ffn1_w_in", 1), ("ffn1_w_out", 1), ("b_w_dq", 0), ("b_w_uq", 0), ("b_w_o", 0), ("ffn2_w_in", 1),
         ("ffn2_w_out", 1)]]

    own = {}

    def cast(g, after=None):
        keys = gather_groups[g]
        own.update(zip(keys, cast_group(f"cast_group_{g}", [shard_of[nm] for nm, _ in keys], [l for _, l in keys],
                                        idx, after=after)))

    def gather_start(g, after):
        keys = gather_groups[g]
        ss, rs, bufs, _, token = ici_start(f"gather_start_{g}", [halves(own[k]) for k in keys], [], after, True)
        return (g, ss, rs, bufs), token

    def gather_finish(state, after):
        g, ss, rs, bufs = state
        bufs, _ = ici_wait(f"gather_wait_{g}", ss, rs, bufs, [], after, True)
        full = gather_pair_pass(f"gather_pair_{g}", bufs)
        for k, w in zip(gather_groups[g], full):
            W[k] = whole(w)
        return full[0]

    def tied(a, token):
        return a + token[0, 0]

    def col(nm, l=0):
        return W[(nm, l)]

    def row(nm, l=0):
        w = W[(nm, l)]
        return w.reshape(N_CHIPS * w.shape[1], w.shape[2])

    bias = rel_bias_tile("rel_bias_tile", a_rel_bias[0])

    def ffn_fwd(tag, h, g, w_in, w_out):
        xn = rms_fwd(f"{tag}_norm", h, g)
        u, act = ffn_in_act(f"{tag}_in", xn, w_in)
        return mm_roww(f"{tag}_out", act, w_out, F32, res=h, alpha=0.5), (xn, u, act)

    h0 = x.reshape(T, D)
    for g in range(3):
        cast(g)
    st0, tok0 = gather_start(0, h0)
    st1, tok1 = gather_start(1, tok0)
    st2, tok2 = gather_start(2, tok1)
    for g in range(3, len(gather_groups)):
        cast(g, tok2)
    xn0 = rms_fwd("l0f1_norm", h0, tied(ffn1_norm[0], tok2))
    gather_finish(st0, xn0)
    u0, act0 = ffn_in_act("l0f1_in", xn0, col("ffn1_w_in", 0))
    gather_finish(st1, u0)
    h1 = mm_roww("l0f1_out", act0, row("ffn1_w_out", 0), F32, res=h0, alpha=0.5)
    sv_f1a = (xn0, u0, act0)
    done2 = gather_finish(st2, h1)
    st3, tok3 = gather_start(3, done2)
    st4, tok4 = gather_start(4, tok3)
    hn_a = rms_fwd("l0mix_norm", h1, tied(mix_norm[0], tok4))
    qkv = mm_colw("l0_qkv", hn_a, col("a_w_qkv"), BF16).reshape(B, S, 3 * D)
    o_a = attn_a_fwd("l0_attn", qkv, bias).reshape(T, D)
    h2 = mm_roww("l0_attn_out", o_a, row("a_w_o"), F32, res=h1)
    gather_finish(st3, h2)
    h3, sv_f2a = ffn_fwd("l0f2", h2, ffn2_norm[0], col("ffn2_w_in", 0), row("ffn2_w_out", 0))

    hkv = rms_fwd("kv_norm", h3, kv_norm)
    ckr = mm_roww("kv_down", hkv, row("kv_w_down"), F32)
    ckv, kr = kvprep_fwd("kv_prep", ckr, kv_latent_norm, tabs, B, S)
    kvb = mm_colw("kv_up", ckv, col("kv_w_up"), BF16).reshape(B, S, HB * 256)
    gather_finish(st4, kvb)

    h4, sv_f1b = ffn_fwd("l1f1", h3, ffn1_norm[1], col("ffn1_w_in", 1), row("ffn1_w_out", 1))
    hn_b = rms_fwd("l1mix_norm", h4, mix_norm[1])
    cqp = mm_roww("l1_dq", hn_b, row("b_w_dq"), F32)
    cq = rms_fwd("l1_q_norm", cqp, b_q_norm[0])
    qf = uq_rope("l1_uq", cq, col("b_w_uq"), tabs, S).reshape(B, S, HB * 256)
    o_b, lse = mla_fwd("l1_attn", qf, kvb, kr)
    h5 = mm_roww("l1_attn_out", o_b.reshape(T, HB * LANE), row("b_w_o"), F32, res=h4)
    h6, sv_f2b = ffn_fwd("l1f2", h5, ffn2_norm[1], col("ffn2_w_in", 1), row("ffn2_w_out", 1))

    dh, g_final, loss_part = loss_head("loss_head", h6, final_norm, loss_target.reshape(T, D))

    gw = {}
    gbufs = {nm: lax.empty(halves(w).shape, F32) for nm, w in sharded}

    def reduce_start(r, keys, after):
        dws = [halves(gw[k]) for k in keys]
        landed = pair_exchange(f"grad_pair_exchange_{r}", dws)
        parts = half_sum_group(f"half_sum_{r}", dws, landed, idx)
        lands = [lax.empty((3, *p.shape[1:]), p.dtype) for p in parts]
        ss, rs, parts, lands, token = ici_start(f"reduce_start_{r}", parts, lands, after, False)
        return (r, keys, ss, rs, parts, lands), token

    def reduce_finish(state, after):
        r, keys, ss, rs, parts, lands = state
        parts, lands = ici_wait(f"reduce_wait_{r}", ss, rs, parts, lands, after, False)
        done = chip_sum_group(f"chip_sum_{r}", parts, lands, [gbufs[nm] for nm, _ in keys], [l for _, l in keys], idx)
        gbufs.update(zip([nm for nm, _ in keys], done))
        return done[0]

    def ffn_bwd(tag, dh, h_in, g, w_in, w_out, saved, key_in, key_out, after=None, then=None):
        xn, u, act = saved
        du = ffn_dact(f"{tag}_dact", dh, w_out, u, after=after)
        dwo = mm_droww(f"{tag}_dwout", act, dh, alpha=0.5)
        gw[key_out] = dwo.reshape(N_CHIPS, dwo.shape[0] // N_CHIPS, dwo.shape[1])
        gw[key_in] = mm_dcolw(f"{tag}_dwin", xn, du, pair_layout=True)
        token = then(du) if then is not None else None
        return dx_norm_bwd(f"{tag}_dxn", du, w_in, h_in, g, dres=dh, pair_layout=True, after=token)

    def chip_major(dw):
        return dw.reshape(N_CHIPS, dw.shape[0] // N_CHIPS, dw.shape[1])

    dh, g_f2b = ffn_bwd("l1f2b", dh, h5, ffn2_norm[1], col("ffn2_w_in", 1), row("ffn2_w_out", 1), sv_f2b,
                        ("ffn2_w_in", 1), ("ffn2_w_out", 1))
    red0, rtok0 = reduce_start(0, [("ffn2_w_in", 1), ("ffn2_w_out", 1)], dh)
    do_b = mm_roww_t("l1_attn_do", dh, row("b_w_o"), BF16, after=rtok0).reshape(B, S, HB * LANE)
    gw[("b_w_o", 0)] = chip_major(mm_droww("l1_attn_dwo", o_b.reshape(T, HB * LANE), dh))
    dqpre, dkv, dkr = mla_bwd("l1_attn_bwd", qf, kvb, kr, do_b, o_b, lse, tabs)
    dqpre = dqpre.reshape(T, HB * 256)
    gw[("b_w_uq", 0)] = mm_dcolw("l1_dwuq", cq, dqpre)
    dcqp, g_qn = dx_norm_bwd("l1_dcq", dqpre, col("b_w_uq"), cqp, b_q_norm[0])
    gw[("b_w_dq", 0)] = chip_major(mm_droww("l1_dwdq", hn_b, dcqp))
    dhn = mm_roww_t("l1_dhn", dcqp, row("b_w_dq"), F32)
    dh, g_mixb = rms_bwd("l1_dmix", h4, mix_norm[1], dhn, dres=dh)
    dh, g_f1b = ffn_bwd("l1f1b", dh, h3, ffn1_norm[1], col("ffn1_w_in", 1), row("ffn1_w_out", 1), sv_f1b,
                        ("ffn1_w_in", 1), ("ffn1_w_out", 1))
    fin0 = reduce_finish(red0, dh)
    red1, rtok1 = reduce_start(1, [("b_w_o", 0), ("b_w_uq", 0), ("b_w_dq", 0), ("ffn1_w_in", 1), ("ffn1_w_out", 1)], fin0)
    dkv2 = dkv.reshape(T, HB * 256)
    gw[("kv_w_up", 0)] = mm_dcolw("kv_dwup", ckv, dkv2, after=rtok1)
    dckv = mm_colw_t("kv_dckv", dkv2, col("kv_w_up"), F32, after=rtok1)
    dckr, g_lat = kvprep_bwd("kv_prep_bwd", ckr, kv_latent_norm, dckv, dkr, tabs, B, S)
    gw[("kv_w_down", 0)] = chip_major(mm_droww("kv_dwdown", hkv, dckr))
    dhkv = mm_roww_t("kv_dhkv", dckr, row("kv_w_down"), F32)
    dh, g_kvn = rms_bwd("kv_dnorm", h3, kv_norm, dhkv, dres=dh)
    dh, g_f2a = ffn_bwd("l0f2b", dh, h2, ffn2_norm[0], col("ffn2_w_in", 0), row("ffn2_w_out", 0), sv_f2a,
                        ("ffn2_w_in", 0), ("ffn2_w_out", 0))
    do_a = mm_roww_t("l0_attn_do", dh, row("a_w_o"), BF16).reshape(B, S, D)
    gw[("a_w_o", 0)] = chip_major(mm_droww("l0_attn_dwo", o_a, dh))
    dqkv, dbias = attn_a_bwd("l0_attn_bwd", qkv, do_a, bias)
    dqkv = dqkv.reshape(T, 3 * D)
    gw[("a_w_qkv", 0)] = mm_dcolw("l0_dwqkv", hn_a, dqkv)
    dh, g_mixa = dx_norm_bwd("l0_dhn", dqkv, col("a_w_qkv"), h1, mix_norm[0], dres=dh)
    fin1 = reduce_finish(red1, dh)
    red2, rtok2 = reduce_start(2, [("kv_w_up", 0), ("kv_w_down", 0), ("ffn2_w_in", 0), ("ffn2_w_out", 0),
                                   ("a_w_o", 0), ("a_w_qkv", 0)], fin1)
    last = {}

    def last_group(du):
        fin2 = reduce_finish(red2, gw[("ffn1_w_in", 0)])
        last["red"], token = reduce_start(3, [("ffn1_w_in", 0), ("ffn1_w_out", 0)], fin2)
        return token

    dh, g_f1a = ffn_bwd("l0f1b", dh, h0, ffn1_norm[0], col("ffn1_w_in", 0), row("ffn1_w_out", 0), sv_f1a,
                        ("ffn1_w_in", 0), ("ffn1_w_out", 0), after=rtok2, then=last_group)
    grad_x = dh.reshape(B, S, D)
    g_rel = rel_bias_grad("rel_bias_grad", dbias)[:, :2 * MAX_REL + 1][None]
    reduce_finish(last["red"], dh)

    full = [whole(g) for g in pair_assemble([gbufs[nm] for nm in names])]
    G = {nm: g for (nm, _), g in zip(sharded, full)}
    G["kv_w_down"] = G["kv_w_down"][0, :, :KVL + ROPE]
    G["kv_w_up"] = G["kv_w_up"][0]
    G["b_w_uq"] = G["b_w_uq"].reshape(1, QL, hpc, 256)[..., :NOPE + ROPE].reshape(b_w_uq.shape)

    small = [("ffn1_norm", jnp.stack([g_f1a, g_f1b])), ("mix_norm", jnp.stack([g_mixa, g_mixb])),
             ("ffn2_norm", jnp.stack([g_f2a, g_f2b])), ("a_rel_bias", g_rel), ("kv_norm", g_kvn),
             ("kv_latent_norm", g_lat), ("b_q_norm", g_qn[None]), ("final_norm", g_final)]
    red = all_reduce_small(_pack([loss_part] + [g for _, g in small]))
    unpacked = _unpack(red, [loss_part] + [g for _, g in small])
    loss = unpacked[0][0, 0]
    for (nm, _), g in zip(small, unpacked[1:]):
        G[nm] = g

    given = dict(ffn1_norm=(ffn1_norm, m_ffn1_norm, v_ffn1_norm), ffn1_w_in=(ffn1_w_in, m_ffn1_w_in, v_ffn1_w_in),
                 ffn1_w_out=(ffn1_w_out, m_ffn1_w_out, v_ffn1_w_out), mix_norm=(mix_norm, m_mix_norm, v_mix_norm),
                 ffn2_norm=(ffn2_norm, m_ffn2_norm, v_ffn2_norm), ffn2_w_in=(ffn2_w_in, m_ffn2_w_in, v_ffn2_w_in),
                 ffn2_w_out=(ffn2_w_out, m_ffn2_w_out, v_ffn2_w_out), a_w_qkv=(a_w_qkv, m_a_w_qkv, v_a_w_qkv),
                 a_rel_bias=(a_rel_bias, m_a_rel_bias, v_a_rel_bias), a_w_o=(a_w_o, m_a_w_o, v_a_w_o),
                 kv_norm=(kv_norm, m_kv_norm, v_kv_norm), kv_w_down=(kv_w_down, m_kv_w_down, v_kv_w_down),
                 kv_latent_norm=(kv_latent_norm, m_kv_latent_norm, v_kv_latent_norm),
                 kv_w_up=(kv_w_up, m_kv_w_up, v_kv_w_up), b_w_dq=(b_w_dq, m_b_w_dq, v_b_w_dq),
                 b_q_norm=(b_q_norm, m_b_q_norm, v_b_q_norm), b_w_uq=(b_w_uq, m_b_w_uq, v_b_w_uq),
                 b_w_o=(b_w_o, m_b_w_o, v_b_w_o), final_norm=(final_norm, m_final_norm, v_final_norm))
    order = list(given)
    delta, new_m, new_v = {}, {}, {}
    small_names = [nm for nm, _ in small]
    packed = [_pack([given[nm][k] for nm in small_names]) for k in range(3)]
    outs = adamw("adamw_small", packed[0], _pack([G[nm] for nm in small_names]), packed[1], packed[2])
    for dst, buf in zip((delta, new_m, new_v), outs):
        for nm, a in zip(small_names, _unpack(buf, [given[nm][0] for nm in small_names])):
            dst[nm] = a
    for nm, _ in sharded:
        w, m, v = given[nm]
        g = G[nm].reshape(w.shape)
        G[nm] = g
        two = lambda a: a.reshape(-1, a.shape[-1])
        d_, m_, v_ = adamw(f"adamw_{nm}", two(w), two(g), two(m), two(v))
        delta[nm], new_m[nm], new_v[nm] = d_.reshape(w.shape), m_.reshape(w.shape), v_.reshape(w.shape)

    return (loss, grad_x, *[G[n] for n in order], *[delta[n] for n in order],
            *[new_m[n] for n in order], *[new_v[n] for n in order])
```

```python
import math

import jax
import jax.numpy as jnp
from jax import lax
from jax.experimental import pallas as pl
from jax.experimental.pallas import tpu as pltpu

F32 = jnp.float32
BF16 = jnp.bfloat16
I32 = jnp.int32

CHUNK = 64
CHUNK_SHIFT = 6
HEAD_DIM_A = 64
LEFT_CHUNKS = 8
MAX_REL = 128
REL_PAD = 384
QROWS = 2 * CHUNK
WIN = (LEFT_CHUNKS + 2) * CHUNK
PADR = LEFT_CHUNKS * CHUNK
NOPE = 128
ROPE = 64
EPS = 1e-6
NEG_INF = -1e30
ROPE_THETA = 10000.0
ADAM_LR, ADAM_B1, ADAM_B2, ADAM_EPS, ADAM_WD, ADAM_STEP = 0.001, 0.9, 0.999, 1e-08, 0.01, 10
N_CHIPS = 4
LANE = 128
MESH = pl.DeviceIdType.MESH
VMEM_CAP_MB = 60
VMEM_FLOOR_MB = 42

NN = (((1,), (0,)), ((), ()))
NT = (((1,), (1,)), ((), ()))
TN = (((0,), (0,)), ((), ()))


def _tile(n, pref, mult):
    t = (min(pref, n) // mult) * mult
    while t >= mult:
        if n % t == 0:
            return t
        t -= mult
    return n


def _nbytes(shape, dtype):
    return math.prod(shape) * jnp.dtype(dtype).itemsize


def _params(block_bytes, extra_bytes=0):
    need = 2 * block_bytes + extra_bytes
    mb = min(VMEM_CAP_MB, max(VMEM_FLOOR_MB, int(need * 1.25 / 2**20) + 8))
    return pltpu.CompilerParams(vmem_limit_bytes=mb * 2**20)


def _mm(name, kind, a, b, grid, a_spec, b_spec, o_spec, out_shape, out_dtype, blocks,
        red_axis=None, nred=1, alpha=1.0, res=None, res_spec=None, after=None):
    dims = {"nn": NN, "nt": NT, "tn": TN}[kind]
    has_res = res is not None
    acc_in_out = nred > 1 and out_dtype == F32 and not has_res and alpha == 1.0
    n_in = 2 + has_res + (after is not None)

    def body(*refs):
        a_ref, b_ref = refs[0], refs[1]
        r_ref = refs[2] if has_res else None
        o_ref = refs[n_in]
        p = lax.dot_general(a_ref[...].astype(BF16), b_ref[...].astype(BF16), dims,
                            preferred_element_type=F32)

        def finish(acc):
            y = acc if alpha == 1.0 else acc * alpha
            if has_res:
                y = r_ref[...] + y
            o_ref[...] = y.astype(o_ref.dtype)

        if nred == 1:
            finish(p)
            return
        k = pl.program_id(red_axis)
        tgt = o_ref if acc_in_out else refs[-1]

        @pl.when(k == 0)
        def _():
            tgt[...] = p

        @pl.when(k > 0)
        def _():
            tgt[...] += p

        if not acc_in_out:
            @pl.when(k == nred - 1)
            def _():
                finish(tgt[...])

    a_blk, b_blk, o_blk = blocks
    scratch = []
    extra = 0
    if nred > 1 and not acc_in_out:
        scratch = [pltpu.VMEM(o_blk, F32)]
        extra = _nbytes(o_blk, F32)
    blk = _nbytes(a_blk, a.dtype) + _nbytes(b_blk, b.dtype) + _nbytes(o_blk, out_dtype)
    ins, specs = [a, b], [a_spec, b_spec]
    if has_res:
        ins.append(res)
        specs.append(res_spec)
        blk += _nbytes(o_blk, res.dtype)
    if after is not None:
        ins.append(after)
        specs.append(pl.BlockSpec(memory_space=pl.ANY))
    extra += _nbytes(a_blk, BF16) + _nbytes(b_blk, BF16) + 2 * _nbytes(o_blk, F32)
    return pl.pallas_call(
        body, name=name, grid=grid, in_specs=specs, out_specs=o_spec,
        out_shape=jax.ShapeDtypeStruct(out_shape, out_dtype), scratch_shapes=scratch,
        compiler_params=_params(blk, extra),
    )(*ins)


def mm_colw(name, x, w3, out_dtype):
    T, K = x.shape
    _, _, nl = w3.shape
    tm = _tile(T, 512, 8)
    return _mm(name, "nn", x, w3, (N_CHIPS, T // tm),
               pl.BlockSpec((tm, K), lambda j, i: (i, 0)),
               pl.BlockSpec((None, K, nl), lambda j, i: (j, 0, 0)),
               pl.BlockSpec((tm, nl), lambda j, i: (i, j)),
               (T, N_CHIPS * nl), out_dtype, ((tm, K), (K, nl), (tm, nl)))


def _pair_chip(j):
    return (j % 2) * 2 + j // 2


def mm_colw_t(name, dy, w3, out_dtype, res=None, after=None, pair_layout=False):
    T = dy.shape[0]
    _, K, nl = w3.shape
    tm = _tile(T, 1024, 8)
    chip = _pair_chip if pair_layout else (lambda j: j)
    return _mm(name, "nt", dy, w3, (T // tm, N_CHIPS),
               pl.BlockSpec((tm, nl), lambda i, j: (i, j)),
               pl.BlockSpec((None, K, nl), lambda i, j: (chip(j), 0, 0)),
               pl.BlockSpec((tm, K), lambda i, j: (i, 0)),
               (T, K), out_dtype, ((tm, nl), (K, nl), (tm, K)),
               red_axis=1, nred=N_CHIPS, res=res,
               res_spec=pl.BlockSpec((tm, K), lambda i, j: (i, 0)), after=after)


def mm_dcolw(name, x, dy, after=None, pair_layout=False):
    T, K = x.shape
    nl = dy.shape[1] // N_CHIPS
    tt = _tile(T, 2048, 8)
    chip = _pair_chip if pair_layout else (lambda j: j)
    return _mm(name, "tn", x, dy, (N_CHIPS, T // tt),
               pl.BlockSpec((tt, K), lambda j, t: (t, 0)),
               pl.BlockSpec((tt, nl), lambda j, t: (t, j)),
               pl.BlockSpec((None, K, nl), lambda j, t: (chip(j), 0, 0)),
               (N_CHIPS, K, nl), BF16, ((tt, K), (tt, nl), (K, nl)),
               red_axis=1, nred=T // tt, after=after)


def mm_roww(name, x, w2, out_dtype, res=None, alpha=1.0):
    T, Kt = x.shape
    N = w2.shape[1]
    tm = _tile(T, 512, 8)
    return _mm(name, "nn", x, w2, (T // tm,),
               pl.BlockSpec((tm, Kt), lambda i: (i, 0)),
               pl.BlockSpec((Kt, N), lambda i: (0, 0)),
               pl.BlockSpec((tm, N), lambda i: (i, 0)),
               (T, N), out_dtype, ((tm, Kt), (Kt, N), (tm, N)),
               alpha=alpha, res=res, res_spec=pl.BlockSpec((tm, N), lambda i: (i, 0)))


def mm_roww_t(name, dy, w2, out_dtype, alpha=1.0, after=None):
    T, N = dy.shape
    Kt = w2.shape[0]
    tm = _tile(T, 512, 8)
    tk = _tile(Kt, 1408, LANE)
    return _mm(name, "nt", dy, w2, (Kt // tk, T // tm),
               pl.BlockSpec((tm, N), lambda j, i: (i, 0)),
               pl.BlockSpec((tk, N), lambda j, i: (j, 0)),
               pl.BlockSpec((tm, tk), lambda j, i: (i, j)),
               (T, Kt), out_dtype, ((tm, N), (tk, N), (tm, tk)), alpha=alpha, after=after)


def mm_droww(name, x, dy, alpha=1.0):
    T, Kt = x.shape
    N = dy.shape[1]
    tt = _tile(T, 2048, 8)
    tk = _tile(Kt, 1408, LANE)
    return _mm(name, "tn", x, dy, (Kt // tk, T // tt),
               pl.BlockSpec((tt, tk), lambda j, t: (t, j)),
               pl.BlockSpec((tt, N), lambda j, t: (t, 0)),
               pl.BlockSpec((tk, N), lambda j, t: (j, 0)),
               (Kt, N), BF16, ((tt, tk), (tt, N), (tk, N)),
               red_axis=1, nred=T // tt, alpha=alpha)


def rms_fwd(name, x, g):
    T, D = x.shape
    tm = _tile(T, 512, 8)

    def body(x_ref, g_ref, o_ref):
        xv = x_ref[...]
        r = lax.rsqrt(jnp.mean(xv * xv, axis=-1, keepdims=True) + EPS)
        o_ref[...] = (xv * r * g_ref[...]).astype(o_ref.dtype)

    return pl.pallas_call(
        body, name=name, grid=(T // tm,),
        in_specs=[pl.BlockSpec((tm, D), lambda i: (i, 0)), pl.BlockSpec((1, D), lambda i: (0, 0))],
        out_specs=pl.BlockSpec((tm, D), lambda i: (i, 0)),
        out_shape=jax.ShapeDtypeStruct((T, D), BF16),
        compiler_params=_params(_nbytes((tm, D), F32) * 2, 4 * _nbytes((tm, D), F32)),
    )(x, g.reshape(1, D))


def _rms_bwd_math(xv, gv, dy):
    r = lax.rsqrt(jnp.mean(xv * xv, axis=-1, keepdims=True) + EPS)
    xh = xv * r
    dyg = dy * gv
    dx = r * (dyg - xh * jnp.mean(dyg * xh, axis=-1, keepdims=True))
    dg = jnp.sum(dy * xh, axis=0, keepdims=True)
    return dx, dg


def rms_bwd(name, x, g, dy, dres=None):
    T, D = x.shape
    tm = _tile(T, 256, 8)
    has_res = dres is not None

    def body(*refs):
        x_ref, g_ref, dy_ref = refs[:3]
        r_ref = refs[3] if has_res else None
        dx_ref, dg_ref = refs[-2:]
        dx, dg = _rms_bwd_math(x_ref[...], g_ref[...], dy_ref[...].astype(F32))
        if has_res:
            dx = r_ref[...] + dx
        dx_ref[...] = dx

        @pl.when(pl.program_id(0) == 0)
        def _():
            dg_ref[...] = dg

        @pl.when(pl.program_id(0) > 0)
        def _():
            dg_ref[...] += dg

    row = pl.BlockSpec((tm, D), lambda i: (i, 0))
    vec = pl.BlockSpec((1, D), lambda i: (0, 0))
    ins, specs = [x, g.reshape(1, D), dy], [row, vec, row]
    if has_res:
        ins.append(dres)
        specs.append(row)
    dx, dg = pl.pallas_call(
        body, name=name, grid=(T // tm,), in_specs=specs, out_specs=[row, vec],
        out_shape=[jax.ShapeDtypeStruct((T, D), F32), jax.ShapeDtypeStruct((1, D), F32)],
        compiler_params=_params(_nbytes((tm, D), F32) * 4, 6 * _nbytes((tm, D), F32)),
    )(*ins)
    return dx, dg.reshape(D)


def dx_norm_bwd(name, dy, w3, x, g, dres=None, pair_layout=False, after=None):
    T = dy.shape[0]
    _, K, nl = w3.shape
    tm = _tile(T, 512, 8)
    chip = _pair_chip if pair_layout else (lambda j: j)
    has_res = dres is not None

    def body(*refs):
        dy_ref, w_ref, x_ref, g_ref = refs[:4]
        r_ref = refs[4] if has_res else None
        dx_ref, dg_ref, acc = refs[-3:]
        i, k = pl.program_id(0), pl.program_id(1)
        p = lax.dot_general(dy_ref[...].astype(BF16), w_ref[...], NT, preferred_element_type=F32)

        @pl.when(k == 0)
        def _():
            acc[...] = p

        @pl.when(k > 0)
        def _():
            acc[...] += p

        @pl.when(k == N_CHIPS - 1)
        def _():
            dx, dg = _rms_bwd_math(x_ref[...], g_ref[...], acc[...])
            dx_ref[...] = r_ref[...] + dx if has_res else dx

            @pl.when(i == 0)
            def _():
                dg_ref[...] = dg

            @pl.when(i > 0)
            def _():
                dg_ref[...] += dg

    row = pl.BlockSpec((tm, K), lambda i, j: (i, 0))
    vec = pl.BlockSpec((1, K), lambda i, j: (0, 0))
    ins = [dy, w3, x, g.reshape(1, K)]
    specs = [pl.BlockSpec((tm, nl), lambda i, j: (i, j)),
             pl.BlockSpec((None, K, nl), lambda i, j: (chip(j), 0, 0)), row, vec]
    if has_res:
        ins.append(dres)
        specs.append(row)
    if after is not None:
        ins.append(after)
        specs.append(pl.BlockSpec(memory_space=pl.ANY))
    blk = _nbytes((tm, nl), dy.dtype) + _nbytes((K, nl), BF16) + (2 + has_res) * _nbytes((tm, K), F32)
    dx, dg = pl.pallas_call(
        body, name=name, grid=(T // tm, N_CHIPS), in_specs=specs, out_specs=[row, vec],
        out_shape=[jax.ShapeDtypeStruct((T, K), F32), jax.ShapeDtypeStruct((1, K), F32)],
        scratch_shapes=[pltpu.VMEM((tm, K), F32)],
        compiler_params=_params(blk, 8 * _nbytes((tm, K), F32)),
    )(*ins)
    return dx, dg.reshape(K)


def ffn_in_act(name, x, w3):
    T, K = x.shape
    _, _, nl = w3.shape
    tm = _tile(T, 512, 8)

    def body(*refs):
        x_ref, wg_ref, wu_ref = refs[:3]
        u_ref, a_ref = refs[-2:]
        xv = x_ref[...]
        g = jnp.dot(xv, wg_ref[...], preferred_element_type=F32)
        up = jnp.dot(xv, wu_ref[...], preferred_element_type=F32)
        u_ref[:, :nl] = g.astype(u_ref.dtype)
        u_ref[:, nl:] = up.astype(u_ref.dtype)
        a_ref[...] = (g * jax.nn.sigmoid(g) * up).astype(a_ref.dtype)

    blk = _nbytes((tm, K), BF16) + 2 * _nbytes((K, nl), BF16) + _nbytes((tm, 3 * nl), BF16)
    return pl.pallas_call(
        body, name=name, grid=(2, T // tm),
        in_specs=[pl.BlockSpec((tm, K), lambda p, i: (i, 0)),
                  pl.BlockSpec((None, K, nl), lambda p, i: (p, 0, 0)),
                  pl.BlockSpec((None, K, nl), lambda p, i: (p + 2, 0, 0))],
        out_specs=[pl.BlockSpec((tm, 2 * nl), lambda p, i: (i, p)), pl.BlockSpec((tm, nl), lambda p, i: (i, p))],
        out_shape=[jax.ShapeDtypeStruct((T, 4 * nl), BF16), jax.ShapeDtypeStruct((T, 2 * nl), BF16)],
        compiler_params=_params(blk, 4 * _nbytes((tm, nl), F32)),
    )(x, w3, w3)


def ffn_dact(name, dh, w_out, u, after=None):
    T, N = dh.shape
    F = w_out.shape[0]
    nl = F // 2
    tm = _tile(T, 512, 8)

    def body(*refs):
        d_ref, w_ref, u_ref = refs[:3]
        o_ref = refs[-1]
        dact = 0.5 * lax.dot_general(d_ref[...].astype(BF16), w_ref[...], NT, preferred_element_type=F32)
        g = u_ref[:, :nl].astype(F32)
        up = u_ref[:, nl:].astype(F32)
        sig = jax.nn.sigmoid(g)
        o_ref[:, :nl] = (dact * up * (sig * (1.0 + g * (1.0 - sig)))).astype(o_ref.dtype)
        o_ref[:, nl:] = (dact * (g * sig)).astype(o_ref.dtype)

    ins = [dh, w_out, u]
    specs = [pl.BlockSpec((tm, N), lambda p, i: (i, 0)), pl.BlockSpec((nl, N), lambda p, i: (p, 0)),
             pl.BlockSpec((tm, 2 * nl), lambda p, i: (i, p))]
    if after is not None:
        ins.append(after)
        specs.append(pl.BlockSpec(memory_space=pl.ANY))
    blk = _nbytes((tm, N), F32) + _nbytes((nl, N), BF16) + 2 * _nbytes((tm, 2 * nl), BF16)
    return pl.pallas_call(
        body, name=name, grid=(2, T // tm), in_specs=specs,
        out_specs=pl.BlockSpec((tm, 2 * nl), lambda p, i: (i, p)),
        out_shape=jax.ShapeDtypeStruct((T, 2 * F), BF16),
        compiler_params=_params(blk, 6 * _nbytes((tm, nl), F32)),
    )(*ins)


def loss_head(name, h, g, target):
    T, D = h.shape
    tm = _tile(T, 256, 8)

    def body(h_ref, g_ref, t_ref, dh_ref, dg_ref, loss_ref):
        xv = h_ref[...]
        gv = g_ref[...]
        r = lax.rsqrt(jnp.mean(xv * xv, axis=-1, keepdims=True) + EPS)
        err = xv * r * gv - t_ref[...]
        part = 0.5 * jnp.sum(jnp.mean(err * err, axis=-1, keepdims=True), axis=0, keepdims=True)
        dx, dg = _rms_bwd_math(xv, gv, err * (1.0 / D))
        dh_ref[...] = dx
        part = jnp.broadcast_to(part, (1, LANE))

        @pl.when(pl.program_id(0) == 0)
        def _():
            dg_ref[...] = dg
            loss_ref[...] = part

        @pl.when(pl.program_id(0) > 0)
        def _():
            dg_ref[...] += dg
            loss_ref[...] += part

    row = pl.BlockSpec((tm, D), lambda i: (i, 0))
    vec = pl.BlockSpec((1, D), lambda i: (0, 0))
    dh, dg, loss = pl.pallas_call(
        body, name=name, grid=(T // tm,), in_specs=[row, vec, row],
        out_specs=[row, vec, pl.BlockSpec((1, LANE), lambda i: (0, 0))],
        out_shape=[jax.ShapeDtypeStruct((T, D), F32), jax.ShapeDtypeStruct((1, D), F32),
                   jax.ShapeDtypeStruct((1, LANE), F32)],
        compiler_params=_params(_nbytes((tm, D), F32) * 3, 6 * _nbytes((tm, D), F32)),
    )(h, g.reshape(1, D), target)
    return dh, dg.reshape(D), loss


def rope_tables(S):
    half = ROPE // 2
    freqs = ROPE_THETA ** (-jnp.arange(half, dtype=F32) / half)
    ang = jnp.arange(S, dtype=F32)[:, None] * freqs[None, :]
    cos, sin = jnp.cos(ang), jnp.sin(ang)
    z = jnp.zeros_like(cos)
    ct = jnp.concatenate([cos, cos, z, z], axis=1)
    s1 = jnp.concatenate([-sin, z, z, z], axis=1)
    s2 = jnp.concatenate([z, sin, z, z], axis=1)
    return ct, s1, s2


def _rope_tile(t, ct, s1, s2):
    return t * ct + pltpu.roll(t, 96, 1) * s1 + pltpu.roll(t, 32, 1) * s2


def _rope_tile_bwd(d, ct, s1, s2):
    return d * ct + pltpu.roll(d * s1, 32, 1) + pltpu.roll(d * s2, 96, 1)


def uq_rope(name, x, w3, tabs, S):
    T, K = x.shape
    _, _, nl = w3.shape
    tm = _tile(S, 512, 8)
    nt = S // tm

    def body(x_ref, w_ref, ct_ref, s1_ref, s2_ref, o_ref):
        q = jnp.dot(x_ref[...], w_ref[...], preferred_element_type=F32)
        ct, s1, s2 = ct_ref[...], s1_ref[...], s2_ref[...]
        for h in range(nl // 256):
            o_ref[:, 256 * h:256 * h + 128] = q[:, 256 * h:256 * h + 128].astype(o_ref.dtype)
            o_ref[:, 256 * h + 128:256 * h + 256] = _rope_tile(q[:, 256 * h + 128:256 * h + 256],
                                                               ct, s1, s2).astype(o_ref.dtype)

    tab = pl.BlockSpec((tm, LANE), lambda j, i: (i % nt, 0))
    blk = _nbytes((tm, K), BF16) + _nbytes((K, nl), BF16) + _nbytes((tm, nl), BF16) + 3 * _nbytes((tm, LANE), F32)
    return pl.pallas_call(
        body, name=name, grid=(N_CHIPS, T // tm),
        in_specs=[pl.BlockSpec((tm, K), lambda j, i: (i, 0)), pl.BlockSpec((None, K, nl), lambda j, i: (j, 0, 0)),
                  tab, tab, tab],
        out_specs=pl.BlockSpec((tm, nl), lambda j, i: (i, j)),
        out_shape=jax.ShapeDtypeStruct((T, N_CHIPS * nl), BF16),
        compiler_params=_params(blk, 4 * _nbytes((tm, nl), F32)),
    )(x, w3, *tabs)


def kvprep_fwd(name, ckr, g, tabs, B, S):
    T, W = ckr.shape
    KVL = W - LANE
    ts = _tile(S, 256, 8)

    def body(x_ref, g_ref, ct_ref, s1_ref, s2_ref, c_ref, k_ref):
        xv = x_ref[0, :, :KVL]
        r = lax.rsqrt(jnp.mean(xv * xv, axis=-1, keepdims=True) + EPS)
        c_ref[0] = (xv * r * g_ref[...]).astype(c_ref.dtype)
        k_ref[0] = _rope_tile(x_ref[0, :, KVL:], ct_ref[...], s1_ref[...], s2_ref[...]).astype(k_ref.dtype)

    tab = pl.BlockSpec((ts, LANE), lambda b, s: (s, 0))
    c, k = pl.pallas_call(
        body, name=name, grid=(B, S // ts),
        in_specs=[pl.BlockSpec((1, ts, W), lambda b, s: (b, s, 0)), pl.BlockSpec((1, KVL), lambda b, s: (0, 0)),
                  tab, tab, tab],
        out_specs=[pl.BlockSpec((1, ts, KVL), lambda b, s: (b, s, 0)),
                   pl.BlockSpec((1, ts, LANE), lambda b, s: (b, s, 0))],
        out_shape=[jax.ShapeDtypeStruct((B, S, KVL), BF16), jax.ShapeDtypeStruct((B, S, LANE), BF16)],
        compiler_params=_params(_nbytes((ts, W), F32) * 2, _nbytes((ts, W), F32) * 2),
    )(ckr.reshape(B, S, W), g.reshape(1, KVL), *tabs)
    return c.reshape(T, KVL), k


def kvprep_bwd(name, ckr, g, dc, dkr, tabs, B, S):
    T, W = ckr.shape
    KVL = W - LANE
    ts = _tile(S, 256, 8)

    def body(x_ref, g_ref, dc_ref, dk_ref, ct_ref, s1_ref, s2_ref, o_ref, dg_ref):
        dx, dg = _rms_bwd_math(x_ref[0, :, :KVL], g_ref[...], dc_ref[0])
        o_ref[0, :, :KVL] = dx
        o_ref[0, :, KVL:] = _rope_tile_bwd(dk_ref[0], ct_ref[...], s1_ref[...], s2_ref[...])
        first = (pl.program_id(0) == 0) & (pl.program_id(1) == 0)

        @pl.when(first)
        def _():
            dg_ref[...] = dg

        @pl.when(jnp.logical_not(first))
        def _():
            dg_ref[...] += dg

    tab = pl.BlockSpec((ts, LANE), lambda b, s: (s, 0))
    vec = pl.BlockSpec((1, KVL), lambda b, s: (0, 0))
    o, dg = pl.pallas_call(
        body, name=name, grid=(B, S // ts),
        in_specs=[pl.BlockSpec((1, ts, W), lambda b, s: (b, s, 0)), vec,
                  pl.BlockSpec((1, ts, KVL), lambda b, s: (b, s, 0)),
                  pl.BlockSpec((1, ts, LANE), lambda b, s: (b, s, 0)), tab, tab, tab],
        out_specs=[pl.BlockSpec((1, ts, W), lambda b, s: (b, s, 0)), vec],
        out_shape=[jax.ShapeDtypeStruct((B, S, W), F32), jax.ShapeDtypeStruct((1, KVL), F32)],
        compiler_params=_params(_nbytes((ts, W), F32) * 4, _nbytes((ts, W), F32) * 4),
    )(ckr.reshape(B, S, W), g.reshape(1, KVL), dc.reshape(B, S, KVL), dkr, *tabs)
    return o.reshape(T, W), dg.reshape(KVL)


DIAGS = 768


def _diag_onehot():
    col = lax.broadcasted_iota(I32, (REL_PAD, DIAGS), 1)
    row = lax.broadcasted_iota(I32, (REL_PAD, DIAGS), 0)
    idx = jnp.clip(PADR + QROWS - 1 - col, -MAX_REL, MAX_REL) + MAX_REL
    return (row == idx).astype(F32)


def rel_bias_tile(name, table):
    H = table.shape[0]
    tpad = jnp.pad(table, ((0, 0), (0, REL_PAD - table.shape[1])))

    def body(t_ref, o_ref):
        g = lax.dot_general(t_ref[...], _diag_onehot(), NN, precision=lax.Precision.HIGHEST,
                            preferred_element_type=F32)
        qc = jnp.right_shift(lax.broadcasted_iota(I32, (QROWS, WIN), 0), CHUNK_SHIFT)
        kc = jnp.right_shift(lax.broadcasted_iota(I32, (QROWS, WIN), 1), CHUNK_SHIFT)
        band = (kc >= qc) & (kc <= qc + LEFT_CHUNKS)
        for h in range(H):
            gb = jnp.broadcast_to(g[h:h + 1, :], (QROWS, DIAGS))
            tile = pltpu.roll(gb, DIAGS - (QROWS - 1), 1, stride=1, stride_axis=0)
            o_ref[h // 2, (h % 2) * QROWS:(h % 2 + 1) * QROWS, :] = jnp.where(band, tile[:, :WIN], NEG_INF)

    return pl.pallas_call(
        body, name=name, out_shape=jax.ShapeDtypeStruct((H // 2, 2 * QROWS, WIN), F32),
        compiler_params=_params(0, 2 * _nbytes((H // 2, 2 * QROWS, WIN), F32)),
    )(tpad)


def rel_bias_grad(name, dbias):
    H = 2 * dbias.shape[0]

    def body(d_ref, o_ref):
        flip = (lax.broadcasted_iota(I32, (QROWS, QROWS), 0) + lax.broadcasted_iota(I32, (QROWS, QROWS), 1)
                == QROWS - 1).astype(F32)
        rows = []
        for h in range(H):
            x = d_ref[h // 2, (h % 2) * QROWS:(h % 2 + 1) * QROWS, :]
            xr = lax.dot_general(flip, x, NN, precision=lax.Precision.HIGHEST, preferred_element_type=F32)
            xp = jnp.concatenate([xr, jnp.zeros((QROWS, DIAGS - WIN), F32)], axis=1)
            y = pltpu.roll(xp, 0, 1, stride=1, stride_axis=0)
            rows.append(jnp.sum(y, axis=0, keepdims=True))
        o_ref[...] = lax.dot_general(jnp.concatenate(rows, axis=0), _diag_onehot(), NT,
                                     precision=lax.Precision.HIGHEST, preferred_element_type=F32)

    return pl.pallas_call(
        body, name=name, out_shape=jax.ShapeDtypeStruct((H, REL_PAD), F32),
        compiler_params=_params(0, 2 * _nbytes(dbias.shape, F32)),
    )(dbias)


def _stack_pair(xp):
    lane = lax.broadcasted_iota(I32, xp.shape, 1)
    z = jnp.zeros_like(xp)
    return jnp.concatenate([jnp.where(lane < HEAD_DIM_A, xp, z), jnp.where(lane >= HEAD_DIM_A, xp, z)], axis=0)


def _unstack_pair(y):
    lane = lax.broadcasted_iota(I32, (QROWS, LANE), 1)
    return jnp.where(lane < HEAD_DIM_A, y[:QROWS], y[QROWS:])


def _attn_a_rowpen(j):
    w = lax.broadcasted_iota(I32, (1, WIN), 1)
    return jnp.where(w >= PADR - QROWS * j, 0.0, NEG_INF).astype(F32)


def _attn_a_load_bias(bias_hbm, bias_v, sem):
    cp = pltpu.make_async_copy(bias_hbm, bias_v, sem)
    cp.start()
    cp.wait()


def _attn_a_load_kv(qkv_hbm, b, kpad, vpad, sem, S, D):
    kpad[0:PADR, :] = jnp.zeros((PADR, D), BF16)
    vpad[0:PADR, :] = jnp.zeros((PADR, D), BF16)
    ck = pltpu.make_async_copy(qkv_hbm.at[b, :, pl.ds(D, D)], kpad.at[pl.ds(PADR, S), :], sem.at[0])
    cv = pltpu.make_async_copy(qkv_hbm.at[b, :, pl.ds(2 * D, D)], vpad.at[pl.ds(PADR, S), :], sem.at[1])
    ck.start()
    cv.start()
    ck.wait()
    cv.wait()


def _attn_a_exp(q2s, kp, bias, pen):
    s = lax.dot_general(q2s, kp, NT, preferred_element_type=F32) + bias + pen
    e = jnp.exp(s - jnp.max(s, axis=-1, keepdims=True))
    return e, 1.0 / jnp.sum(e, axis=-1, keepdims=True)


def attn_a_fwd(name, qkv, bias):
    B, S, D3 = qkv.shape
    D = D3 // 3
    H = D // HEAD_DIM_A
    nb = S // QROWS
    scale = HEAD_DIM_A ** -0.5

    def body(q_ref, bias_hbm, qkv_hbm, o_ref, kpad, vpad, bias_v, sem):
        b, j = pl.program_id(0), pl.program_id(1)

        @pl.when((b == 0) & (j == 0))
        def _():
            _attn_a_load_bias(bias_hbm, bias_v, sem.at[2])

        @pl.when(j == 0)
        def _():
            _attn_a_load_kv(qkv_hbm, b, kpad, vpad, sem, S, D)

        pen = _attn_a_rowpen(j)
        w0 = pl.multiple_of(j * QROWS, QROWS)
        for p in range(H // 2):
            ls = slice(p * LANE, (p + 1) * LANE)
            e, rl = _attn_a_exp(_stack_pair(q_ref[0, :, ls] * scale), kpad[pl.ds(w0, WIN), ls], bias_v[p], pen)
            o2 = jnp.dot(e.astype(BF16), vpad[pl.ds(w0, WIN), ls], preferred_element_type=F32) * rl
            o_ref[0, :, ls] = _unstack_pair(o2).astype(o_ref.dtype)

    scr = 2 * _nbytes((PADR + S, D), BF16) + _nbytes(bias.shape, F32) + 8 * _nbytes((2 * QROWS, WIN), F32)
    return pl.pallas_call(
        body, name=name, grid=(B, nb),
        in_specs=[pl.BlockSpec((1, QROWS, D), lambda b, j: (b, j, 0)),
                  pl.BlockSpec(memory_space=pl.ANY), pl.BlockSpec(memory_space=pl.ANY)],
        out_specs=pl.BlockSpec((1, QROWS, D), lambda b, j: (b, j, 0)),
        out_shape=jax.ShapeDtypeStruct((B, S, D), BF16),
        scratch_shapes=[pltpu.VMEM((PADR + S, D), BF16), pltpu.VMEM((PADR + S, D), BF16),
                        pltpu.VMEM(bias.shape, F32), pltpu.SemaphoreType.DMA((3,))],
        compiler_params=_params(2 * _nbytes((QROWS, D), BF16), scr),
    )(qkv, bias, qkv)


def attn_a_bwd(name, qkv, do, bias):
    B, S, D3 = qkv.shape
    D = D3 // 3
    H = D // HEAD_DIM_A
    nb = S // QROWS
    scale = HEAD_DIM_A ** -0.5

    def body(q_ref, do_ref, bias_hbm, qkv_hbm, dqkv_hbm, dbias_hbm, kpad, vpad, dkacc, dvacc, bias_v, dbias_v,
             dq_stage, sem):
        b, j = pl.program_id(0), pl.program_id(1)
        step = b * nb + j
        slot = lax.rem(step, 2)

        def dq_out(s):
            return pltpu.make_async_copy(dq_stage.at[s], dqkv_hbm.at[b, pl.ds(j * QROWS, QROWS), pl.ds(0, D)],
                                         sem.at[3 + s])

        @pl.when(step >= 2)
        def _():
            dq_out(slot).wait()

        @pl.when((b == 0) & (j == 0))
        def _():
            _attn_a_load_bias(bias_hbm, bias_v, sem.at[2])
            dbias_v[...] = jnp.zeros_like(dbias_v)

        @pl.when(j == 0)
        def _():
            _attn_a_load_kv(qkv_hbm, b, kpad, vpad, sem, S, D)
            dkacc[...] = jnp.zeros_like(dkacc)
            dvacc[...] = jnp.zeros_like(dvacc)

        pen = _attn_a_rowpen(j)
        w0 = pl.multiple_of(j * QROWS, QROWS)
        for p in range(H // 2):
            ls = slice(p * LANE, (p + 1) * LANE)
            q2s = _stack_pair(q_ref[0, :, ls] * scale)
            do2 = _stack_pair(do_ref[0, :, ls])
            kp = kpad[pl.ds(w0, WIN), ls]
            vp = vpad[pl.ds(w0, WIN), ls]
            e, rl = _attn_a_exp(q2s, kp, bias_v[p], pen)
            pr = e * rl
            dp = lax.dot_general(do2, vp, NT, preferred_element_type=F32)
            ds = pr * (dp - jnp.sum(pr * dp, axis=-1, keepdims=True))
            dbias_v[p] += ds
            dsb = ds.astype(BF16)
            dq_stage[slot, :, ls] = (_unstack_pair(jnp.dot(dsb, kp, preferred_element_type=F32))
                                     * scale).astype(dq_stage.dtype)
            dkacc[pl.ds(w0, WIN), ls] += lax.dot_general(dsb, q2s, TN, preferred_element_type=F32)
            dvacc[pl.ds(w0, WIN), ls] += lax.dot_general(pr.astype(BF16), do2, TN, preferred_element_type=F32)

        dq_out(slot).start()

        @pl.when(j == nb - 1)
        def _():
            kpad[pl.ds(PADR, S), :] = dkacc[pl.ds(PADR, S), :].astype(BF16)
            vpad[pl.ds(PADR, S), :] = dvacc[pl.ds(PADR, S), :].astype(BF16)
            ck = pltpu.make_async_copy(kpad.at[pl.ds(PADR, S), :], dqkv_hbm.at[b, :, pl.ds(D, D)], sem.at[0])
            cv = pltpu.make_async_copy(vpad.at[pl.ds(PADR, S), :], dqkv_hbm.at[b, :, pl.ds(2 * D, D)], sem.at[1])
            ck.start()
            cv.start()
            ck.wait()
            cv.wait()

        @pl.when((b == B - 1) & (j == nb - 1))
        def _():
            cb = pltpu.make_async_copy(dbias_v, dbias_hbm, sem.at[2])
            cb.start()
            dq_out(0).wait()
            dq_out(1).wait()
            cb.wait()

    blk = _nbytes((QROWS, D), BF16) * 2
    scr = (2 * _nbytes((PADR + S, D), BF16) + 2 * _nbytes((PADR + S, D), F32) + 2 * _nbytes(bias.shape, F32)
           + 8 * _nbytes((2 * QROWS, WIN), F32) + 2 * _nbytes((QROWS, D), F32))
    return pl.pallas_call(
        body, name=name, grid=(B, nb),
        in_specs=[pl.BlockSpec((1, QROWS, D), lambda b, j: (b, j, 0)),
                  pl.BlockSpec((1, QROWS, D), lambda b, j: (b, j, 0)),
                  pl.BlockSpec(memory_space=pl.ANY), pl.BlockSpec(memory_space=pl.ANY)],
        out_specs=[pl.BlockSpec(memory_space=pl.ANY), pl.BlockSpec(memory_space=pl.ANY)],
        out_shape=[jax.ShapeDtypeStruct((B, S, 3 * D), BF16), jax.ShapeDtypeStruct(bias.shape, F32)],
        scratch_shapes=[pltpu.VMEM((PADR + S, D), BF16), pltpu.VMEM((PADR + S, D), BF16),
                        pltpu.VMEM((PADR + S, D), F32), pltpu.VMEM((PADR + S, D), F32),
                        pltpu.VMEM(bias.shape, F32), pltpu.VMEM(bias.shape, F32),
                        pltpu.VMEM((2, QROWS, D), BF16), pltpu.SemaphoreType.DMA((5,))],
        compiler_params=_params(blk, scr),
    )(qkv, do, bias, qkv)


def _mla_raw_t(k2, kj, q, QB):
    return lax.dot_general(k2[_blk(kj, QB), :], q, NT, preferred_element_type=F32)


def _blk(kj, QB):
    return pl.ds(kj * QB, QB) if isinstance(kj, int) else pl.ds(pl.multiple_of(kj * QB, QB), QB)


def _mla_diag_pen(QB):
    kc = jnp.right_shift(lax.broadcasted_iota(I32, (QB, QB), 0), CHUNK_SHIFT)
    qc = jnp.right_shift(lax.broadcasted_iota(I32, (QB, QB), 1), CHUNK_SHIFT)
    return jnp.where(kc <= qc, 0.0, NEG_INF).astype(F32)


def _mla_fill_keys(kv_ref, kr_ref, k2):
    k2[:, :NOPE] = kv_ref[0, :, :NOPE]
    k2[:, NOPE:] = kr_ref[0]


def _t(x):
    return x.astype(F32).T


def mla_fwd(name, qf, kv, kr):
    B, S, W = qf.shape
    HB = W // 256
    QB = _tile(S, 256, CHUNK)
    nq = S // QB
    scale = (NOPE + ROPE) ** -0.5

    def body(q_ref, kv_ref, kr_ref, o_ref, lse_ref, k2, vt, st_buf, pen):
        qi = pl.program_id(2)

        @pl.when(qi == 0)
        def _():
            pen[...] = _mla_diag_pen(QB)
            _mla_fill_keys(kv_ref, kr_ref, k2)
            for kj in range(nq):
                vt[kj] = _t(kv_ref[0, kj * QB:(kj + 1) * QB, NOPE:]).astype(BF16)

        q = q_ref[0]
        st_buf[0] = _mla_raw_t(k2, 0, q, QB)

        def step(kj, carry):
            m, l, acc = carry
            cur = lax.rem(kj, 2)
            st_raw = st_buf[cur]
            st_buf[1 - cur] = _mla_raw_t(k2, jnp.minimum(kj + 1, qi), q, QB)
            st = st_raw * scale + jnp.where(kj == qi, pen[...], 0.0)
            m_new = jnp.maximum(m, jnp.max(st, axis=0, keepdims=True))
            a = jnp.exp(m - m_new)
            pt = jnp.exp(st - m_new)
            l = a * l + jnp.sum(pt, axis=0, keepdims=True)
            acc = a * acc + jnp.dot(vt[kj], pt.astype(BF16), preferred_element_type=F32)
            return m_new, l, acc

        init = (jnp.full((1, QB), NEG_INF, F32), jnp.zeros((1, QB), F32), jnp.zeros((NOPE, QB), F32))
        m, l, acc = lax.fori_loop(0, qi + 1, step, init)
        o_ref[0] = (acc * (1.0 / l)).T
        lse_ref[0, 0] = m + jnp.log(l)

    blk = (_nbytes((QB, 256), BF16) + _nbytes((S, 256), BF16) + _nbytes((S, LANE), BF16)
           + _nbytes((QB, LANE), F32))
    return pl.pallas_call(
        body, name=name, grid=(B, HB, nq),
        in_specs=[pl.BlockSpec((1, QB, 256), lambda b, h, i: (b, i, h)),
                  pl.BlockSpec((1, S, 256), lambda b, h, i: (b, 0, h)),
                  pl.BlockSpec((1, S, LANE), lambda b, h, i: (b, 0, 0))],
        out_specs=[pl.BlockSpec((1, QB, LANE), lambda b, h, i: (b, i, h)),
                   pl.BlockSpec((1, 1, 1, QB), lambda b, h, i: (b, h, 0, i))],
        out_shape=[jax.ShapeDtypeStruct((B, S, HB * LANE), F32), jax.ShapeDtypeStruct((B, HB, 1, S), F32)],
        scratch_shapes=[pltpu.VMEM((S, 256), BF16), pltpu.VMEM((nq, NOPE, QB), BF16),
                        pltpu.VMEM((2, QB, QB), F32), pltpu.VMEM((QB, QB), F32)],
        compiler_params=_params(blk, 2 * _nbytes((S, 256), BF16) + 10 * _nbytes((QB, QB), F32)),
    )(qf, kv, kr)


def mla_bwd(name, qf, kv, kr, do, o, lse, tabs):
    B, S, W = qf.shape
    HB = W // 256
    QB = _tile(S, 256, CHUNK)
    nq = S // QB
    scale = (NOPE + ROPE) ** -0.5

    def body(q_ref, kv_ref, kr_ref, do_ref, o_ref, lse_ref, ct_ref, s1_ref, s2_ref, dq_ref, dkv_ref, dkr_ref,
             k2, kt, dot_, delta, dqt, st_buf, dp_buf, pen, dkv_acc):
        h = pl.program_id(1)
        pen[...] = _mla_diag_pen(QB)
        dkv_acc[...] = jnp.zeros_like(dkv_acc)

        @pl.when(h == 0)
        def _():
            dkr_ref[...] = jnp.zeros_like(dkr_ref)

        _mla_fill_keys(kv_ref, kr_ref, k2)
        for i in range(nq):
            rows = slice(i * QB, (i + 1) * QB)
            kt[i] = _t(k2[rows, :]).astype(BF16)
            dot32 = _t(do_ref[0, rows, :])
            delta[i] = jnp.sum(dot32 * o_ref[0, rows, :].T, axis=0, keepdims=True)
            dot_[i] = dot32.astype(BF16)

        for qi in range(nq):
            rows = slice(qi * QB, (qi + 1) * QB)
            q = q_ref[0, rows, :]
            dob = do_ref[0, rows, :]
            lse_q = lse_ref[0, 0, :, rows]
            delta_q = delta[qi]
            dqt[...] = jnp.zeros_like(dqt)

            def raw(kj, slot, q=q, qi=qi):
                st_buf[slot] = _mla_raw_t(k2, kj, q, QB)
                dp_buf[slot] = jnp.dot(kv_ref[0, _blk(kj, QB), NOPE:], dot_[qi], preferred_element_type=F32)

            raw(0, 0)

            def step(kj, carry, q=q, dob=dob, lse_q=lse_q, delta_q=delta_q, qi=qi, raw=raw):
                ks = pl.ds(pl.multiple_of(kj * QB, QB), QB)
                cur = lax.rem(kj, 2)
                st_raw, dp_raw = st_buf[cur], dp_buf[cur]
                raw(jnp.minimum(kj + 1, qi), 1 - cur)
                pt = jnp.exp(st_raw * scale + jnp.where(kj == qi, pen[...], 0.0) - lse_q)
                dst = (pt * (dp_raw - delta_q) * scale).astype(BF16)
                dkv_acc[ks, NOPE:] += jnp.dot(pt.astype(BF16), dob, preferred_element_type=F32)
                dk2 = jnp.dot(dst, q, preferred_element_type=F32)
                dkv_acc[ks, :NOPE] += dk2[:, :NOPE]
                dkr_ref[0, ks, :] += dk2[:, NOPE:]
                dqt[...] += jnp.dot(kt[kj], dst, preferred_element_type=F32)
                return carry

            lax.fori_loop(0, qi + 1, step, 0)
            dq = dqt[...].T
            dq_ref[0, rows, :NOPE] = dq[:, :NOPE].astype(dq_ref.dtype)
            dq_ref[0, rows, NOPE:] = _rope_tile_bwd(dq[:, NOPE:], ct_ref[rows, :], s1_ref[rows, :],
                                                    s2_ref[rows, :]).astype(dq_ref.dtype)

        dkv_ref[0] = dkv_acc[...].astype(dkv_ref.dtype)

    head = lambda w: pl.BlockSpec((1, S, w), lambda b, h: (b, 0, h))
    shared = pl.BlockSpec((1, S, LANE), lambda b, h: (b, 0, 0))
    blk = (2 * _nbytes((S, 256), BF16) + 2 * _nbytes((S, LANE), BF16) + _nbytes((S, LANE), F32)
           + 2 * _nbytes((S, 256), F32) + _nbytes((S, LANE), F32))
    scr = 3 * _nbytes((S, 256), BF16) + 14 * _nbytes((QB, QB), F32)
    return pl.pallas_call(
        body, name=name, grid=(B, HB),
        in_specs=[head(256), head(256), shared, head(LANE), head(LANE),
                  pl.BlockSpec((1, 1, 1, S), lambda b, h: (b, h, 0, 0))]
        + [pl.BlockSpec((S, LANE), lambda b, h: (0, 0))] * 3,
        out_specs=[head(256), head(256), shared],
        out_shape=[jax.ShapeDtypeStruct((B, S, W), BF16), jax.ShapeDtypeStruct((B, S, W), BF16),
                   jax.ShapeDtypeStruct((B, S, LANE), F32)],
        scratch_shapes=[pltpu.VMEM((S, 256), BF16), pltpu.VMEM((nq, 256, QB), BF16),
                        pltpu.VMEM((nq, NOPE, QB), BF16), pltpu.VMEM((nq, 1, QB), F32),
                        pltpu.VMEM((256, QB), F32), pltpu.VMEM((2, QB, QB), F32), pltpu.VMEM((2, QB, QB), F32),
                        pltpu.VMEM((QB, QB), F32), pltpu.VMEM((S, 256), F32)],
        compiler_params=_params(blk, scr),
    )(qf, kv, kr, do, o, lse, *tabs)


GROUP_STEPS = 4


def cast_group(name, ws, layers, idx, after=None):
    n = len(ws)
    n_in = n + (after is not None)

    def body(k_ref, *refs):
        for i in range(n):
            refs[n_in + i][...] = refs[i][...].astype(BF16)

    def spec_in(w, layer):
        return pl.BlockSpec((None, w.shape[1] // GROUP_STEPS, w.shape[2]), lambda r, k_ref: (layer, r, 0))

    def spec_out(w):
        return pl.BlockSpec((None, w.shape[1] // GROUP_STEPS, w.shape[2]), lambda r, k_ref: (k_ref[0], r, 0))

    return pl.pallas_call(
        body, name=name,
        grid_spec=pltpu.PrefetchScalarGridSpec(
            num_scalar_prefetch=1, grid=(GROUP_STEPS,),
            in_specs=([spec_in(w, l) for w, l in zip(ws, layers)]
                      + [pl.BlockSpec(memory_space=pl.ANY)] * (after is not None)),
            out_specs=[spec_out(w) for w in ws]),
        out_shape=[jax.ShapeDtypeStruct((N_CHIPS, *w.shape[1:]), BF16) for w in ws],
        compiler_params=_params(sum(_nbytes(w.shape[1:], F32) * 3 // 2 for w in ws) // GROUP_STEPS),
    )(idx, *ws, *([] if after is None else [after]))


def adamw(name, w, g, m, v):
    R, C = w.shape
    tr = _tile(R, max(8, (1 << 18) // C // 8 * 8), 8)
    c1 = 1.0 - ADAM_B1 ** ADAM_STEP
    c2 = 1.0 - ADAM_B2 ** ADAM_STEP

    def body(w_ref, g_ref, m_ref, v_ref, d_ref, mo_ref, vo_ref):
        gv = g_ref[...]
        mn = ADAM_B1 * m_ref[...] + (1.0 - ADAM_B1) * gv
        vn = ADAM_B2 * v_ref[...] + (1.0 - ADAM_B2) * (gv * gv)
        mo_ref[...] = mn
        vo_ref[...] = vn
        d_ref[...] = -ADAM_LR * ((mn / c1) / (jnp.sqrt(vn / c2) + ADAM_EPS) + ADAM_WD * w_ref[...])

    spec = pl.BlockSpec((tr, C), lambda r: (r, 0))
    return pl.pallas_call(
        body, name=name, grid=(R // tr,), in_specs=[spec] * 4, out_specs=[spec] * 3,
        out_shape=[jax.ShapeDtypeStruct((R, C), F32)] * 3,
        compiler_params=_params(7 * _nbytes((tr, C), F32), 4 * _nbytes((tr, C), F32)),
    )(w, g, m, v)


def half_sum_group(name, dws, landed, idx):
    n = len(dws)
    steps = GROUP_STEPS // 2

    def body(i_ref, *refs):
        for i in range(n):
            refs[2 * n + i][...] = (refs[i][...].astype(F32) + refs[n + i][...].astype(F32)).astype(BF16)

    def own(d):
        return pl.BlockSpec((None, None, d.shape[2] // steps, d.shape[3]), lambda k, r, i_ref: (k, i_ref[1], r, 0))

    def flat(d):
        return pl.BlockSpec((None, d.shape[2] // steps, d.shape[3]), lambda k, r, i_ref: (k, r, 0))

    return pl.pallas_call(
        body, name=name,
        grid_spec=pltpu.PrefetchScalarGridSpec(
            num_scalar_prefetch=1, grid=(N_CHIPS, steps),
            in_specs=[own(d) for d in dws] + [flat(d) for d in dws], out_specs=[flat(d) for d in dws]),
        out_shape=[jax.ShapeDtypeStruct((N_CHIPS, *d.shape[2:]), BF16) for d in dws],
        compiler_params=_params(sum(3 * _nbytes(d.shape[2:], BF16) for d in dws) // steps),
    )(idx, *dws, *landed)


def chip_sum_group(name, parts, landed, gbufs, layers, idx):
    n = len(parts)
    steps = GROUP_STEPS // 2

    def body(i_ref, *refs):
        for i in range(n):
            a, b = refs[i], refs[n + i]
            refs[3 * n + i][...] = ((a[...].astype(F32) + b[0].astype(F32)) + b[1].astype(F32)) + b[2].astype(F32)

    def mine(p):
        return pl.BlockSpec((None, p.shape[1] // steps, p.shape[2]), lambda r, i_ref: (i_ref[0], r, 0))

    def three(p):
        return pl.BlockSpec((3, p.shape[1] // steps, p.shape[2]), lambda r, i_ref: (0, r, 0))

    def out(p, layer):
        return pl.BlockSpec((None, None, p.shape[1] // steps, p.shape[2]), lambda r, i_ref: (layer, i_ref[1], r, 0))

    return pl.pallas_call(
        body, name=name,
        grid_spec=pltpu.PrefetchScalarGridSpec(
            num_scalar_prefetch=1, grid=(steps,),
            in_specs=[mine(p) for p in parts] + [three(p) for p in parts] + [pl.BlockSpec(memory_space=pl.ANY)] * n,
            out_specs=[out(p, l) for p, l in zip(parts, layers)]),
        out_shape=[jax.ShapeDtypeStruct(g.shape, F32) for g in gbufs],
        input_output_aliases={1 + 2 * n + i: i for i in range(n)},
        compiler_params=_params(sum(6 * _nbytes(p.shape[1:], BF16) for p in parts) // steps),
    )(idx, *parts, *landed, *gbufs)


ANY = pl.BlockSpec(memory_space=pl.ANY)


def _place():
    x, y, c = lax.axis_index("x"), lax.axis_index("y"), lax.axis_index("c")
    chips = [(1 - x, y), (x, 1 - y), (1 - x, 1 - y)]
    return x, y, c, chips


HBM = pl.BlockSpec(memory_space=pltpu.HBM)
SEM = pl.BlockSpec(memory_space=pltpu.SEMAPHORE)
EFFECT = pltpu.SideEffectType.DATAFLOW_SIDE_EFFECTING


def _in_hbm(a):
    return pltpu.with_memory_space_constraint(a, pltpu.HBM)


def _ici_copy(src, dst, send_sems, recv_sems, k, peer):
    return pltpu.make_async_remote_copy(src_ref=src, dst_ref=dst, send_sem=send_sems.at[k], recv_sem=recv_sems.at[k],
                                        device_id=peer, device_id_type=MESH)


def ici_start(name, bufs, lands, after, gather):
    n, nl = len(bufs), len(lands)

    def body(*refs):
        b_in = refs[:n]
        send_sems, recv_sems = refs[n + nl + 1], refs[n + nl + 2]
        b_out = refs[n + nl + 3:2 * n + nl + 3]
        l_out = refs[2 * n + nl + 3:2 * n + 2 * nl + 3]
        token = refs[-1]
        x, y, c, chips = _place()
        kme = 2 * x + y
        for i in range(n):
            for j in range(3):
                peer = (*chips[j], c)
                if gather:
                    _ici_copy(b_out[i].at[kme, c], b_out[i].at[kme, c], send_sems, recv_sems, 3 * i + j, peer).start()
                else:
                    kd = 2 * chips[j][0] + chips[j][1]
                    _ici_copy(b_out[i].at[kd], l_out[i].at[j], send_sems, recv_sems, 3 * i + j, peer).start()
        token[...] = jnp.zeros_like(token)

    arrays = [*bufs, *lands]
    outs = pl.pallas_call(
        body, name=name,
        in_specs=[HBM] * (n + nl) + [ANY],
        out_specs=(SEM, SEM, *[HBM] * (n + nl), pl.BlockSpec(memory_space=pltpu.VMEM)),
        out_shape=(pltpu.SemaphoreType.DMA((3 * n,)), pltpu.SemaphoreType.DMA((3 * n,)),
                   *[pltpu.HBM(a.shape, a.dtype) for a in arrays], jax.ShapeDtypeStruct((8, LANE), F32)),
        input_output_aliases={i: 2 + i for i in range(n + nl)},
        compiler_params=pltpu.CompilerParams(has_side_effects=EFFECT),
    )(*[_in_hbm(a) for a in arrays], after)
    return outs[0], outs[1], list(outs[2:2 + n]), list(outs[2 + n:2 + n + nl]), outs[-1]


def ici_wait(name, send_sems, recv_sems, bufs, lands, after, gather):
    n, nl = len(bufs), len(lands)

    def body(*refs):
        b_in, l_in = refs[:n], refs[n:n + nl]
        send_sems, recv_sems = refs[n + nl], refs[n + nl + 1]
        x, y, c, chips = _place()
        kme = 2 * x + y
        for i in range(n):
            for j in range(3):
                peer = (*chips[j], c)
                kj = 2 * chips[j][0] + chips[j][1]
                if gather:
                    _ici_copy(b_in[i].at[kme, c], b_in[i].at[kme, c], send_sems, recv_sems, 3 * i + j, peer).wait_send()
                    _ici_copy(b_in[i].at[kj, c], b_in[i].at[kj, c], send_sems, recv_sems, 3 * i + j, peer).wait_recv()
                else:
                    _ici_copy(b_in[i].at[kj], l_in[i].at[j], send_sems, recv_sems, 3 * i + j, peer).wait_send()
                    _ici_copy(b_in[i].at[kj], l_in[i].at[j], send_sems, recv_sems, 3 * i + j, peer).wait_recv()

    arrays = [*bufs, *lands]
    outs = pl.pallas_call(
        body, name=name,
        in_specs=[HBM] * (n + nl) + [SEM, SEM, ANY],
        out_specs=tuple([HBM] * (n + nl)),
        out_shape=tuple(pltpu.HBM(a.shape, a.dtype) for a in arrays),
        input_output_aliases={i: i for i in range(n + nl)},
        compiler_params=pltpu.CompilerParams(has_side_effects=EFFECT),
    )(*arrays, send_sems, recv_sems, after)
    return list(outs[:n]), list(outs[n:])


def gather_pair_pass(name, bufs):
    n = len(bufs)

    def body(*refs):
        b = refs[n:2 * n]
        send_sems, recv_sems = refs[2 * n:]
        x, y, c, chips = _place()
        sib = (x, y, 1 - c)

        def d2d(i, j, which):
            kj = 2 * chips[j][0] + chips[j][1]
            return _ici_copy(b[i].at[kj, which], b[i].at[kj, which], send_sems, recv_sems, 3 * i + j, sib)

        for i in range(n):
            for j in range(3):
                d2d(i, j, c).start()
        for i in range(n):
            for j in range(3):
                d2d(i, j, 1 - c).wait_recv()
        for i in range(n):
            for j in range(3):
                d2d(i, j, c).wait_send()

    return pl.pallas_call(
        body, name=name, in_specs=[ANY] * n, out_specs=[ANY] * n,
        out_shape=[jax.ShapeDtypeStruct(a.shape, a.dtype) for a in bufs],
        input_output_aliases={i: i for i in range(n)},
        scratch_shapes=[pltpu.SemaphoreType.DMA((3 * n,)), pltpu.SemaphoreType.DMA((3 * n,))],
    )(*bufs)


def pair_exchange(name, dws):
    n = len(dws)

    def body(*refs):
        ins, outs = refs[:n], refs[n:2 * n]
        send_sems, recv_sems = refs[2 * n:]
        x, y, c, _ = _place()
        copies = []
        for i in range(n):
            copies.append(pltpu.make_async_remote_copy(
                src_ref=ins[i].at[:, 1 - c], dst_ref=outs[i],
                send_sem=send_sems.at[i], recv_sem=recv_sems.at[i],
                device_id=(x, y, 1 - c), device_id_type=MESH))
            copies[i].start()
        for cp in copies:
            cp.wait_recv()
        for cp in copies:
            cp.wait_send()

    return pl.pallas_call(
        body, name=name, in_specs=[ANY] * n, out_specs=[ANY] * n,
        out_shape=[jax.ShapeDtypeStruct((N_CHIPS, *d.shape[2:]), d.dtype) for d in dws],
        scratch_shapes=[pltpu.SemaphoreType.DMA((n,)), pltpu.SemaphoreType.DMA((n,))],
    )(*dws)


def pair_assemble(gbufs):
    n = len(gbufs)

    def body(*refs):
        bufs = refs[n:2 * n]
        send_sems, recv_sems = refs[2 * n:]
        x, y, c, _ = _place()
        copies = []
        for i in range(n):
            copies.append(pltpu.make_async_remote_copy(
                src_ref=bufs[i].at[:, c], dst_ref=bufs[i].at[:, c],
                send_sem=send_sems.at[i], recv_sem=recv_sems.at[i],
                device_id=(x, y, 1 - c), device_id_type=MESH))
            copies[i].start()
        for i in range(n):
            pltpu.make_async_remote_copy(
                src_ref=bufs[i].at[:, 1 - c], dst_ref=bufs[i].at[:, 1 - c],
                send_sem=send_sems.at[i], recv_sem=recv_sems.at[i],
                device_id=(x, y, 1 - c), device_id_type=MESH).wait_recv()
        for cp in copies:
            cp.wait_send()

    return pl.pallas_call(
        body, name="grad_pair_assemble", in_specs=[ANY] * n, out_specs=[ANY] * n,
        out_shape=[jax.ShapeDtypeStruct(g.shape, g.dtype) for g in gbufs],
        input_output_aliases={i: i for i in range(n)},
        scratch_shapes=[pltpu.SemaphoreType.DMA((n,)), pltpu.SemaphoreType.DMA((n,))],
    )(*gbufs)


def all_reduce_small(vec):
    NR = vec.shape[0]
    flips = [(fx, fy, fc) for fx in (0, 1) for fy in (0, 1) for fc in (0, 1)][1:]

    def body(v_ref, o_ref, buf, send_sems, recv_sems):
        x, y, c, _ = _place()
        me = 4 * x + 2 * y + c
        buf[me] = v_ref[...]
        copies = []
        for j, (fx, fy, fc) in enumerate(flips):
            peer = (1 - x if fx else x, 1 - y if fy else y, 1 - c if fc else c)
            copies.append(pltpu.make_async_remote_copy(
                src_ref=v_ref, dst_ref=buf.at[me], send_sem=send_sems.at[j], recv_sem=recv_sems.at[j],
                device_id=peer, device_id_type=MESH))
            copies[j].start()
        for cp in copies:
            cp.wait_recv()
        for cp in copies:
            cp.wait_send()
        acc = buf[0]
        for d in range(1, 8):
            acc = acc + buf[d]
        o_ref[...] = acc

    return pl.pallas_call(
        body, name="all_reduce_small",
        in_specs=[pl.BlockSpec(memory_space=pltpu.VMEM)], out_specs=pl.BlockSpec(memory_space=pltpu.VMEM),
        out_shape=jax.ShapeDtypeStruct((NR, LANE), F32),
        scratch_shapes=[pltpu.VMEM((8, NR, LANE), F32), pltpu.SemaphoreType.DMA((7,)),
                        pltpu.SemaphoreType.DMA((7,))],
    )(vec)


def _pack(arrays):
    flat = jnp.concatenate([a.reshape(-1).astype(F32) for a in arrays])
    n = flat.shape[0]
    npad = -(-n // (8 * LANE)) * (8 * LANE)
    return jnp.pad(flat, (0, npad - n)).reshape(npad // LANE, LANE)


def _unpack(buf, like):
    flat = buf.reshape(-1)
    out, off = [], 0
    for a in like:
        out.append(flat[off:off + a.size].reshape(a.shape))
        off += a.size
    return out


def kernel(x, ffn1_norm, ffn1_w_in, ffn1_w_out, mix_norm, ffn2_norm, ffn2_w_in, ffn2_w_out, a_w_qkv, a_rel_bias, a_w_o, kv_norm, kv_w_down, kv_latent_norm, kv_w_up, b_w_dq, b_q_norm, b_w_uq, b_w_o, final_norm, loss_target, m_ffn1_norm, m_ffn1_w_in, m_ffn1_w_out, m_mix_norm, m_ffn2_norm, m_ffn2_w_in, m_ffn2_w_out, m_a_w_qkv, m_a_rel_bias, m_a_w_o, m_kv_norm, m_kv_w_down, m_kv_latent_norm, m_kv_w_up, m_b_w_dq, m_b_q_norm, m_b_w_uq, m_b_w_o, m_final_norm, v_ffn1_norm, v_ffn1_w_in, v_ffn1_w_out, v_mix_norm, v_ffn2_norm, v_ffn2_w_in, v_ffn2_w_out, v_a_w_qkv, v_a_rel_bias, v_a_w_o, v_kv_norm, v_kv_w_down, v_kv_latent_norm, v_kv_w_up, v_b_w_dq, v_b_q_norm, v_b_w_uq, v_b_w_o, v_final_norm):
    B, S, D = x.shape
    T = B * S
    HB = D // 128
    QL = b_q_norm.shape[-1]
    KVL = kv_latent_norm.shape[0]
    hpc = HB // N_CHIPS
    tabs = rope_tables(S)
    idx = jnp.stack([2 * lax.axis_index("x") + lax.axis_index("y"), lax.axis_index("c")]).astype(I32)

    def halves(a):
        return a.reshape(*a.shape[:-2], 2, a.shape[-2] // 2, a.shape[-1])

    def whole(a):
        return a.reshape(*a.shape[:-3], 2 * a.shape[-2], a.shape[-1])

    kv_w_down_p = jnp.pad(kv_w_down, ((0, 0), (0, LANE - ROPE)))[None]
    b_w_uq_p = jnp.pad(b_w_uq.reshape(1, QL, hpc, NOPE + ROPE),
                       ((0, 0), (0, 0), (0, 0), (0, LANE - ROPE))).reshape(1, QL, hpc * 256)
    sharded = [("ffn1_w_in", ffn1_w_in), ("ffn1_w_out", ffn1_w_out), ("ffn2_w_in", ffn2_w_in),
               ("ffn2_w_out", ffn2_w_out), ("a_w_qkv", a_w_qkv), ("a_w_o", a_w_o),
               ("kv_w_down", kv_w_down_p), ("kv_w_up", kv_w_up[None]), ("b_w_dq", b_w_dq),
               ("b_w_uq", b_w_uq_p), ("b_w_o", b_w_o)]
    names = [nm for nm, _ in sharded]
    shard_of = dict(sharded)
    W = {}

    gather_groups = [
        [("ffn1_w_in", 0)],
        [("ffn1_w_out", 0)],
        [("a_w_qkv", 0), ("a_w_o", 0)],
        [("ffn2_w_in", 0), ("ffn2_w_out", 0), ("kv_w_down", 0), ("kv_w_up", 0)],
        [("ffn1_w_in", 1), ("ffn1_w_out", 1), ("b_w_dq", 0), ("b_w_uq", 0), ("b_w_o", 0), ("ffn2_w_in", 1),
         ("ffn2_w_out", 1)]]

    own = {}

    def cast(g, after=None):
        keys = gather_groups[g]
        own.update(zip(keys, cast_group(f"cast_group_{g}", [shard_of[nm] for nm, _ in keys], [l for _, l in keys],
                                        idx, after=after)))

    def gather_start(g, after):
        keys = gather_groups[g]
        ss, rs, bufs, _, token = ici_start(f"gather_start_{g}", [halves(own[k]) for k in keys], [], after, True)
        return (g, ss, rs, bufs), token

    def gather_finish(state, after):
        g, ss, rs, bufs = state
        bufs, _ = ici_wait(f"gather_wait_{g}", ss, rs, bufs, [], after, True)
        full = gather_pair_pass(f"gather_pair_{g}", bufs)
        for k, w in zip(gather_groups[g], full):
            W[k] = whole(w)
        return full[0]

    def tied(a, token):
        return a + token[0, 0]

    def col(nm, l=0):
        return W[(nm, l)]

    def row(nm, l=0):
        w = W[(nm, l)]
        return w.reshape(N_CHIPS * w.shape[1], w.shape[2])

    bias = rel_bias_tile("rel_bias_tile", a_rel_bias[0])

    def ffn_fwd(tag, h, g, w_in, w_out):
        xn = rms_fwd(f"{tag}_norm", h, g)
        u, act = ffn_in_act(f"{tag}_in", xn, w_in)
        return mm_roww(f"{tag}_out", act, w_out, F32, res=h, alpha=0.5), (xn, u, act)

    h0 = x.reshape(T, D)
    for g in range(3):
        cast(g)
    st0, tok0 = gather_start(0, h0)
    st1, tok1 = gather_start(1, tok0)
    st2, tok2 = gather_start(2, tok1)
    for g in range(3, len(gather_groups)):
        cast(g, tok2)
    xn0 = rms_fwd("l0f1_norm", h0, tied(ffn1_norm[0], tok2))
    gather_finish(st0, xn0)
    u0, act0 = ffn_in_act("l0f1_in", xn0, col("ffn1_w_in", 0))
    gather_finish(st1, u0)
    h1 = mm_roww("l0f1_out", act0, row("ffn1_w_out", 0), F32, res=h0, alpha=0.5)
    sv_f1a = (xn0, u0, act0)
    done2 = gather_finish(st2, h1)
    st3, tok3 = gather_start(3, done2)
    st4, tok4 = gather_start(4, tok3)
    hn_a = rms_fwd("l0mix_norm", h1, tied(mix_norm[0], tok4))
    qkv = mm_colw("l0_qkv", hn_a, col("a_w_qkv"), BF16).reshape(B, S, 3 * D)
    o_a = attn_a_fwd("l0_attn", qkv, bias).reshape(T, D)
    h2 = mm_roww("l0_attn_out", o_a, row("a_w_o"), F32, res=h1)
    gather_finish(st3, h2)
    h3, sv_f2a = ffn_fwd("l0f2", h2, ffn2_norm[0], col("ffn2_w_in", 0), row("ffn2_w_out", 0))

    hkv = rms_fwd("kv_norm", h3, kv_norm)
    ckr = mm_roww("kv_down", hkv, row("kv_w_down"), F32)
    ckv, kr = kvprep_fwd("kv_prep", ckr, kv_latent_norm, tabs, B, S)
    kvb = mm_colw("kv_up", ckv, col("kv_w_up"), BF16).reshape(B, S, HB * 256)
    gather_finish(st4, kvb)

    h4, sv_f1b = ffn_fwd("l1f1", h3, ffn1_norm[1], col("ffn1_w_in", 1), row("ffn1_w_out", 1))
    hn_b = rms_fwd("l1mix_norm", h4, mix_norm[1])
    cqp = mm_roww("l1_dq", hn_b, row("b_w_dq"), F32)
    cq = rms_fwd("l1_q_norm", cqp, b_q_norm[0])
    qf = uq_rope("l1_uq", cq, col("b_w_uq"), tabs, S).reshape(B, S, HB * 256)
    o_b, lse = mla_fwd("l1_attn", qf, kvb, kr)
    h5 = mm_roww("l1_attn_out", o_b.reshape(T, HB * LANE), row("b_w_o"), F32, res=h4)
    h6, sv_f2b = ffn_fwd("l1f2", h5, ffn2_norm[1], col("ffn2_w_in", 1), row("ffn2_w_out", 1))

    dh, g_final, loss_part = loss_head("loss_head", h6, final_norm, loss_target.reshape(T, D))

    gw = {}
    gbufs = {nm: lax.empty(halves(w).shape, F32) for nm, w in sharded}

    def reduce_start(r, keys, after):
        dws = [halves(gw[k]) for k in keys]
        landed = pair_exchange(f"grad_pair_exchange_{r}", dws)
        parts = half_sum_group(f"half_sum_{r}", dws, landed, idx)
        lands = [lax.empty((3, *p.shape[1:]), p.dtype) for p in parts]
        ss, rs, parts, lands, token = ici_start(f"reduce_start_{r}", parts, lands, after, False)
        return (r, keys, ss, rs, parts, lands), token

    def reduce_finish(state, after):
        r, keys, ss, rs, parts, lands = state
        parts, lands = ici_wait(f"reduce_wait_{r}", ss, rs, parts, lands, after, False)
        done = chip_sum_group(f"chip_sum_{r}", parts, lands, [gbufs[nm] for nm, _ in keys], [l for _, l in keys], idx)
        gbufs.update(zip([nm for nm, _ in keys], done))
        return done[0]

    def ffn_bwd(tag, dh, h_in, g, w_in, w_out, saved, key_in, key_out, after=None, then=None):
        xn, u, act = saved
        du = ffn_dact(f"{tag}_dact", dh, w_out, u, after=after)
        dwo = mm_droww(f"{tag}_dwout", act, dh, alpha=0.5)
        gw[key_out] = dwo.reshape(N_CHIPS, dwo.shape[0] // N_CHIPS, dwo.shape[1])
        gw[key_in] = mm_dcolw(f"{tag}_dwin", xn, du, pair_layout=True)
        token = then(du) if then is not None else None
        return dx_norm_bwd(f"{tag}_dxn", du, w_in, h_in, g, dres=dh, pair_layout=True, after=token)

    def chip_major(dw):
        return dw.reshape(N_CHIPS, dw.shape[0] // N_CHIPS, dw.shape[1])

    dh, g_f2b = ffn_bwd("l1f2b", dh, h5, ffn2_norm[1], col("ffn2_w_in", 1), row("ffn2_w_out", 1), sv_f2b,
                        ("ffn2_w_in", 1), ("ffn2_w_out", 1))
    red0, rtok0 = reduce_start(0, [("ffn2_w_in", 1), ("ffn2_w_out", 1)], dh)
    do_b = mm_roww_t("l1_attn_do", dh, row("b_w_o"), BF16, after=rtok0).reshape(B, S, HB * LANE)
    gw[("b_w_o", 0)] = chip_major(mm_droww("l1_attn_dwo", o_b.reshape(T, HB * LANE), dh))
    dqpre, dkv, dkr = mla_bwd("l1_attn_bwd", qf, kvb, kr, do_b, o_b, lse, tabs)
    dqpre = dqpre.reshape(T, HB * 256)
    gw[("b_w_uq", 0)] = mm_dcolw("l1_dwuq", cq, dqpre)
    dcqp, g_qn = dx_norm_bwd("l1_dcq", dqpre, col("b_w_uq"), cqp, b_q_norm[0])
    gw[("b_w_dq", 0)] = chip_major(mm_droww("l1_dwdq", hn_b, dcqp))
    dhn = mm_roww_t("l1_dhn", dcqp, row("b_w_dq"), F32)
    dh, g_mixb = rms_bwd("l1_dmix", h4, mix_norm[1], dhn, dres=dh)
    dh, g_f1b = ffn_bwd("l1f1b", dh, h3, ffn1_norm[1], col("ffn1_w_in", 1), row("ffn1_w_out", 1), sv_f1b,
                        ("ffn1_w_in", 1), ("ffn1_w_out", 1))
    fin0 = reduce_finish(red0, dh)
    red1, rtok1 = reduce_start(1, [("b_w_o", 0), ("b_w_uq", 0), ("b_w_dq", 0), ("ffn1_w_in", 1), ("ffn1_w_out", 1)], fin0)
    dkv2 = dkv.reshape(T, HB * 256)
    gw[("kv_w_up", 0)] = mm_dcolw("kv_dwup", ckv, dkv2, after=rtok1)
    dckv = mm_colw_t("kv_dckv", dkv2, col("kv_w_up"), F32, after=rtok1)
    dckr, g_lat = kvprep_bwd("kv_prep_bwd", ckr, kv_latent_norm, dckv, dkr, tabs, B, S)
    gw[("kv_w_down", 0)] = chip_major(mm_droww("kv_dwdown", hkv, dckr))
    dhkv = mm_roww_t("kv_dhkv", dckr, row("kv_w_down"), F32)
    dh, g_kvn = rms_bwd("kv_dnorm", h3, kv_norm, dhkv, dres=dh)
    dh, g_f2a = ffn_bwd("l0f2b", dh, h2, ffn2_norm[0], col("ffn2_w_in", 0), row("ffn2_w_out", 0), sv_f2a,
                        ("ffn2_w_in", 0), ("ffn2_w_out", 0))
    do_a = mm_roww_t("l0_attn_do", dh, row("a_w_o"), BF16).reshape(B, S, D)
    gw[("a_w_o", 0)] = chip_major(mm_droww("l0_attn_dwo", o_a, dh))
    dqkv, dbias = attn_a_bwd("l0_attn_bwd", qkv, do_a, bias)
    dqkv = dqkv.reshape(T, 3 * D)
    gw[("a_w_qkv", 0)] = mm_dcolw("l0_dwqkv", hn_a, dqkv)
    dh, g_mixa = dx_norm_bwd("l0_dhn", dqkv, col("a_w_qkv"), h1, mix_norm[0], dres=dh)
    fin1 = reduce_finish(red1, dh)
    red2, rtok2 = reduce_start(2, [("kv_w_up", 0), ("kv_w_down", 0), ("ffn2_w_in", 0), ("ffn2_w_out", 0),
                                   ("a_w_o", 0), ("a_w_qkv", 0)], fin1)
    last = {}

    def last_group(du):
        fin2 = reduce_finish(red2, gw[("ffn1_w_in", 0)])
        last["red"], token = reduce_start(3, [("ffn1_w_in", 0), ("ffn1_w_out", 0)], fin2)
        return token

    dh, g_f1a = ffn_bwd("l0f1b", dh, h0, ffn1_norm[0], col("ffn1_w_in", 0), row("ffn1_w_out", 0), sv_f1a,
                        ("ffn1_w_in", 0), ("ffn1_w_out", 0), after=rtok2, then=last_group)
    grad_x = dh.reshape(B, S, D)
    g_rel = rel_bias_grad("rel_bias_grad", dbias)[:, :2 * MAX_REL + 1][None]
    reduce_finish(last["red"], dh)

    full = [whole(g) for g in pair_assemble([gbufs[nm] for nm in names])]
    G = {nm: g for (nm, _), g in zip(sharded, full)}
    G["kv_w_down"] = G["kv_w_down"][0, :, :KVL + ROPE]
    G["kv_w_up"] = G["kv_w_up"][0]
    G["b_w_uq"] = G["b_w_uq"].reshape(1, QL, hpc, 256)[..., :NOPE + ROPE].reshape(b_w_uq.shape)

    small = [("ffn1_norm", jnp.stack([g_f1a, g_f1b])), ("mix_norm", jnp.stack([g_mixa, g_mixb])),
             ("ffn2_norm", jnp.stack([g_f2a, g_f2b])), ("a_rel_bias", g_rel), ("kv_norm", g_kvn),
             ("kv_latent_norm", g_lat), ("b_q_norm", g_qn[None]), ("final_norm", g_final)]
    red = all_reduce_small(_pack([loss_part] + [g for _, g in small]))
    unpacked = _unpack(red, [loss_part] + [g for _, g in small])
    loss = unpacked[0][0, 0]
    for (nm, _), g in zip(small, unpacked[1:]):
        G[nm] = g

    given = dict(ffn1_norm=(ffn1_norm, m_ffn1_norm, v_ffn1_norm), ffn1_w_in=(ffn1_w_in, m_ffn1_w_in, v_ffn1_w_in),
                 ffn1_w_out=(ffn1_w_out, m_ffn1_w_out, v_ffn1_w_out), mix_norm=(mix_norm, m_mix_norm, v_mix_norm),
                 ffn2_norm=(ffn2_norm, m_ffn2_norm, v_ffn2_norm), ffn2_w_in=(ffn2_w_in, m_ffn2_w_in, v_ffn2_w_in),
                 ffn2_w_out=(ffn2_w_out, m_ffn2_w_out, v_ffn2_w_out), a_w_qkv=(a_w_qkv, m_a_w_qkv, v_a_w_qkv),
                 a_rel_bias=(a_rel_bias, m_a_rel_bias, v_a_rel_bias), a_w_o=(a_w_o, m_a_w_o, v_a_w_o),
                 kv_norm=(kv_norm, m_kv_norm, v_kv_norm), kv_w_down=(kv_w_down, m_kv_w_down, v_kv_w_down),
                 kv_latent_norm=(kv_latent_norm, m_kv_latent_norm, v_kv_latent_norm),
                 kv_w_up=(kv_w_up, m_kv_w_up, v_kv_w_up), b_w_dq=(b_w_dq, m_b_w_dq, v_b_w_dq),
                 b_q_norm=(b_q_norm, m_b_q_norm, v_b_q_norm), b_w_uq=(b_w_uq, m_b_w_uq, v_b_w_uq),
                 b_w_o=(b_w_o, m_b_w_o, v_b_w_o), final_norm=(final_norm, m_final_norm, v_final_norm))
    order = list(given)
    delta, new_m, new_v = {}, {}, {}
    small_names = [nm for nm, _ in small]
    packed = [_pack([given[nm][k] for nm in small_names]) for k in range(3)]
    outs = adamw("adamw_small", packed[0], _pack([G[nm] for nm in small_names]), packed[1], packed[2])
    for dst, buf in zip((delta, new_m, new_v), outs):
        for nm, a in zip(small_names, _unpack(buf, [given[nm][0] for nm in small_names])):
            dst[nm] = a
    for nm, _ in sharded:
        w, m, v = given[nm]
        g = G[nm].reshape(w.shape)
        G[nm] = g
        two = lambda a: a.reshape(-1, a.shape[-1])
        d_, m_, v_ = adamw(f"adamw_{nm}", two(w), two(g), two(m), two(v))
        delta[nm], new_m[nm], new_v[nm] = d_.reshape(w.shape), m_.reshape(w.shape), v_.reshape(w.shape)

    return (loss, grad_x, *[G[n] for n in order], *[delta[n] for n in order],
            *[new_m[n] for n in order], *[new_v[n] for n in order])
```

```python
import math

import jax
import jax.numpy as jnp
from jax import lax
from jax.experimental import pallas as pl
from jax.experimental.pallas import tpu as pltpu

F32 = jnp.float32
BF16 = jnp.bfloat16
I32 = jnp.int32

CHUNK = 64
CHUNK_SHIFT = 6
HEAD_DIM_A = 64
LEFT_CHUNKS = 8
MAX_REL = 128
REL_PAD = 384
QROWS = 2 * CHUNK
WIN = (LEFT_CHUNKS + 2) * CHUNK
PADR = LEFT_CHUNKS * CHUNK
NOPE = 128
ROPE = 64
EPS = 1e-6
NEG_INF = -1e30
ROPE_THETA = 10000.0
ADAM_LR, ADAM_B1, ADAM_B2, ADAM_EPS, ADAM_WD, ADAM_STEP = 0.001, 0.9, 0.999, 1e-08, 0.01, 10
N_CHIPS = 4
LANE = 128
MESH = pl.DeviceIdType.MESH
VMEM_CAP_MB = 60
VMEM_FLOOR_MB = 48

NN = (((1,), (0,)), ((), ()))
NT = (((1,), (1,)), ((), ()))
TN = (((0,), (0,)), ((), ()))


def _tile(n, pref, mult):
    t = (min(pref, n) // mult) * mult
    while t >= mult:
        if n % t == 0:
            return t
        t -= mult
    return n


def _nbytes(shape, dtype):
    return math.prod(shape) * jnp.dtype(dtype).itemsize


def _params(block_bytes, extra_bytes=0):
    need = 2 * block_bytes + extra_bytes
    mb = min(VMEM_CAP_MB, max(VMEM_FLOOR_MB, int(need * 1.25 / 2**20) + 8))
    return pltpu.CompilerParams(vmem_limit_bytes=mb * 2**20)


def _mm(name, kind, a, b, grid, a_spec, b_spec, o_spec, out_shape, out_dtype, blocks,
        red_axis=None, nred=1, alpha=1.0, res=None, res_spec=None, after=None):
    dims = {"nn": NN, "nt": NT, "tn": TN}[kind]
    has_res = res is not None
    acc_in_out = nred > 1 and out_dtype == F32 and not has_res and alpha == 1.0
    n_in = 2 + has_res + (after is not None)

    def body(*refs):
        a_ref, b_ref = refs[0], refs[1]
        r_ref = refs[2] if has_res else None
        o_ref = refs[n_in]
        p = lax.dot_general(a_ref[...].astype(BF16), b_ref[...].astype(BF16), dims,
                            preferred_element_type=F32)

        def finish(acc):
            y = acc if alpha == 1.0 else acc * alpha
            if has_res:
                y = r_ref[...] + y
            o_ref[...] = y.astype(o_ref.dtype)

        if nred == 1:
            finish(p)
            return
        k = pl.program_id(red_axis)
        tgt = o_ref if acc_in_out else refs[-1]

        @pl.when(k == 0)
        def _():
            tgt[...] = p

        @pl.when(k > 0)
        def _():
            tgt[...] += p

        if not acc_in_out:
            @pl.when(k == nred - 1)
            def _():
                finish(tgt[...])

    a_blk, b_blk, o_blk = blocks
    scratch = []
    extra = 0
    if nred > 1 and not acc_in_out:
        scratch = [pltpu.VMEM(o_blk, F32)]
        extra = _nbytes(o_blk, F32)
    blk = _nbytes(a_blk, a.dtype) + _nbytes(b_blk, b.dtype) + _nbytes(o_blk, out_dtype)
    ins, specs = [a, b], [a_spec, b_spec]
    if has_res:
        ins.append(res)
        specs.append(res_spec)
        blk += _nbytes(o_blk, res.dtype)
    if after is not None:
        ins.append(after)
        specs.append(pl.BlockSpec(memory_space=pl.ANY))
    extra += _nbytes(a_blk, BF16) + _nbytes(b_blk, BF16) + 2 * _nbytes(o_blk, F32)
    return pl.pallas_call(
        body, name=name, grid=grid, in_specs=specs, out_specs=o_spec,
        out_shape=jax.ShapeDtypeStruct(out_shape, out_dtype), scratch_shapes=scratch,
        compiler_params=_params(blk, extra),
    )(*ins)


def mm_colw(name, x, w3, out_dtype):
    T, K = x.shape
    _, _, nl = w3.shape
    tm = _tile(T, 512, 8)
    return _mm(name, "nn", x, w3, (N_CHIPS, T // tm),
               pl.BlockSpec((tm, K), lambda j, i: (i, 0)),
               pl.BlockSpec((None, K, nl), lambda j, i: (j, 0, 0)),
               pl.BlockSpec((tm, nl), lambda j, i: (i, j)),
               (T, N_CHIPS * nl), out_dtype, ((tm, K), (K, nl), (tm, nl)))


def _pair_chip(j):
    return (j % 2) * 2 + j // 2


def mm_colw_t(name, dy, w3, out_dtype, res=None, after=None, pair_layout=False):
    T = dy.shape[0]
    _, K, nl = w3.shape
    tm = _tile(T, 1024, 8)
    chip = _pair_chip if pair_layout else (lambda j: j)
    return _mm(name, "nt", dy, w3, (T // tm, N_CHIPS),
               pl.BlockSpec((tm, nl), lambda i, j: (i, j)),
               pl.BlockSpec((None, K, nl), lambda i, j: (chip(j), 0, 0)),
               pl.BlockSpec((tm, K), lambda i, j: (i, 0)),
               (T, K), out_dtype, ((tm, nl), (K, nl), (tm, K)),
               red_axis=1, nred=N_CHIPS, res=res,
               res_spec=pl.BlockSpec((tm, K), lambda i, j: (i, 0)), after=after)


def mm_dcolw(name, x, dy, after=None, pair_layout=False):
    T, K = x.shape
    nl = dy.shape[1] // N_CHIPS
    tt = _tile(T, 2048, 8)
    chip = _pair_chip if pair_layout else (lambda j: j)
    return _mm(name, "tn", x, dy, (N_CHIPS, T // tt),
               pl.BlockSpec((tt, K), lambda j, t: (t, 0)),
               pl.BlockSpec((tt, nl), lambda j, t: (t, j)),
               pl.BlockSpec((None, K, nl), lambda j, t: (chip(j), 0, 0)),
               (N_CHIPS, K, nl), BF16, ((tt, K), (tt, nl), (K, nl)),
               red_axis=1, nred=T // tt, after=after)


def mm_roww(name, x, w2, out_dtype, res=None, alpha=1.0):
    T, Kt = x.shape
    N = w2.shape[1]
    tm = _tile(T, 512, 8)
    return _mm(name, "nn", x, w2, (T // tm,),
               pl.BlockSpec((tm, Kt), lambda i: (i, 0)),
               pl.BlockSpec((Kt, N), lambda i: (0, 0)),
               pl.BlockSpec((tm, N), lambda i: (i, 0)),
               (T, N), out_dtype, ((tm, Kt), (Kt, N), (tm, N)),
               alpha=alpha, res=res, res_spec=pl.BlockSpec((tm, N), lambda i: (i, 0)))


def mm_roww_t(name, dy, w2, out_dtype, alpha=1.0, after=None):
    T, N = dy.shape
    Kt = w2.shape[0]
    tm = _tile(T, 512, 8)
    tk = _tile(Kt, 1408, LANE)
    return _mm(name, "nt", dy, w2, (Kt // tk, T // tm),
               pl.BlockSpec((tm, N), lambda j, i: (i, 0)),
               pl.BlockSpec((tk, N), lambda j, i: (j, 0)),
               pl.BlockSpec((tm, tk), lambda j, i: (i, j)),
               (T, Kt), out_dtype, ((tm, N), (tk, N), (tm, tk)), alpha=alpha, after=after)


def mm_droww(name, x, dy, alpha=1.0):
    T, Kt = x.shape
    N = dy.shape[1]
    tt = _tile(T, 2048, 8)
    tk = _tile(Kt, 1408, LANE)
    return _mm(name, "tn", x, dy, (Kt // tk, T // tt),
               pl.BlockSpec((tt, tk), lambda j, t: (t, j)),
               pl.BlockSpec((tt, N), lambda j, t: (t, 0)),
               pl.BlockSpec((tk, N), lambda j, t: (j, 0)),
               (Kt, N), BF16, ((tt, tk), (tt, N), (tk, N)),
               red_axis=1, nred=T // tt, alpha=alpha)


def rms_fwd(name, x, g):
    T, D = x.shape
    tm = _tile(T, 512, 8)

    def body(x_ref, g_ref, o_ref):
        xv = x_ref[...]
        r = lax.rsqrt(jnp.mean(xv * xv, axis=-1, keepdims=True) + EPS)
        o_ref[...] = (xv * r * g_ref[...]).astype(o_ref.dtype)

    return pl.pallas_call(
        body, name=name, grid=(T // tm,),
        in_specs=[pl.BlockSpec((tm, D), lambda i: (i, 0)), pl.BlockSpec((1, D), lambda i: (0, 0))],
        out_specs=pl.BlockSpec((tm, D), lambda i: (i, 0)),
        out_shape=jax.ShapeDtypeStruct((T, D), BF16),
        compiler_params=_params(_nbytes((tm, D), F32) * 2, 4 * _nbytes((tm, D), F32)),
    )(x, g.reshape(1, D))


def _rms_bwd_math(xv, gv, dy):
    r = lax.rsqrt(jnp.mean(xv * xv, axis=-1, keepdims=True) + EPS)
    xh = xv * r
    dyg = dy * gv
    dx = r * (dyg - xh * jnp.mean(dyg * xh, axis=-1, keepdims=True))
    dg = jnp.sum(dy * xh, axis=0, keepdims=True)
    return dx, dg


def rms_bwd(name, x, g, dy, dres=None):
    T, D = x.shape
    tm = _tile(T, 256, 8)
    has_res = dres is not None

    def body(*refs):
        x_ref, g_ref, dy_ref = refs[:3]
        r_ref = refs[3] if has_res else None
        dx_ref, dg_ref = refs[-2:]
        dx, dg = _rms_bwd_math(x_ref[...], g_ref[...], dy_ref[...].astype(F32))
        if has_res:
            dx = r_ref[...] + dx
        dx_ref[...] = dx

        @pl.when(pl.program_id(0) == 0)
        def _():
            dg_ref[...] = dg

        @pl.when(pl.program_id(0) > 0)
        def _():
            dg_ref[...] += dg

    row = pl.BlockSpec((tm, D), lambda i: (i, 0))
    vec = pl.BlockSpec((1, D), lambda i: (0, 0))
    ins, specs = [x, g.reshape(1, D), dy], [row, vec, row]
    if has_res:
        ins.append(dres)
        specs.append(row)
    dx, dg = pl.pallas_call(
        body, name=name, grid=(T // tm,), in_specs=specs, out_specs=[row, vec],
        out_shape=[jax.ShapeDtypeStruct((T, D), F32), jax.ShapeDtypeStruct((1, D), F32)],
        compiler_params=_params(_nbytes((tm, D), F32) * 4, 6 * _nbytes((tm, D), F32)),
    )(*ins)
    return dx, dg.reshape(D)


def dx_norm_bwd(name, dy, w3, x, g, dres=None, pair_layout=False, after=None):
    T = dy.shape[0]
    _, K, nl = w3.shape
    tm = _tile(T, 512, 8)
    chip = _pair_chip if pair_layout else (lambda j: j)
    has_res = dres is not None

    def body(*refs):
        dy_ref, w_ref, x_ref, g_ref = refs[:4]
        r_ref = refs[4] if has_res else None
        dx_ref, dg_ref, acc = refs[-3:]
        i, k = pl.program_id(0), pl.program_id(1)
        p = lax.dot_general(dy_ref[...].astype(BF16), w_ref[...], NT, preferred_element_type=F32)

        @pl.when(k == 0)
        def _():
            acc[...] = p

        @pl.when(k > 0)
        def _():
            acc[...] += p

        @pl.when(k == N_CHIPS - 1)
        def _():
            dx, dg = _rms_bwd_math(x_ref[...], g_ref[...], acc[...])
            dx_ref[...] = r_ref[...] + dx if has_res else dx

            @pl.when(i == 0)
            def _():
                dg_ref[...] = dg

            @pl.when(i > 0)
            def _():
                dg_ref[...] += dg

    row = pl.BlockSpec((tm, K), lambda i, j: (i, 0))
    vec = pl.BlockSpec((1, K), lambda i, j: (0, 0))
    ins = [dy, w3, x, g.reshape(1, K)]
    specs = [pl.BlockSpec((tm, nl), lambda i, j: (i, j)),
             pl.BlockSpec((None, K, nl), lambda i, j: (chip(j), 0, 0)), row, vec]
    if has_res:
        ins.append(dres)
        specs.append(row)
    if after is not None:
        ins.append(after)
        specs.append(pl.BlockSpec(memory_space=pl.ANY))
    blk = _nbytes((tm, nl), dy.dtype) + _nbytes((K, nl), BF16) + (2 + has_res) * _nbytes((tm, K), F32)
    dx, dg = pl.pallas_call(
        body, name=name, grid=(T // tm, N_CHIPS), in_specs=specs, out_specs=[row, vec],
        out_shape=[jax.ShapeDtypeStruct((T, K), F32), jax.ShapeDtypeStruct((1, K), F32)],
        scratch_shapes=[pltpu.VMEM((tm, K), F32)],
        compiler_params=_params(blk, 8 * _nbytes((tm, K), F32)),
    )(*ins)
    return dx, dg.reshape(K)


def ffn_in_act(name, x, w3):
    T, K = x.shape
    _, _, nl = w3.shape
    tm = _tile(T, 512, 8)

    def body(*refs):
        x_ref, wg_ref, wu_ref = refs[:3]
        u_ref, a_ref = refs[-2:]
        xv = x_ref[...]
        g = jnp.dot(xv, wg_ref[...], preferred_element_type=F32)
        up = jnp.dot(xv, wu_ref[...], preferred_element_type=F32)
        u_ref[:, :nl] = g.astype(u_ref.dtype)
        u_ref[:, nl:] = up.astype(u_ref.dtype)
        a_ref[...] = (g * jax.nn.sigmoid(g) * up).astype(a_ref.dtype)

    blk = _nbytes((tm, K), BF16) + 2 * _nbytes((K, nl), BF16) + _nbytes((tm, 3 * nl), BF16)
    return pl.pallas_call(
        body, name=name, grid=(2, T // tm),
        in_specs=[pl.BlockSpec((tm, K), lambda p, i: (i, 0)),
                  pl.BlockSpec((None, K, nl), lambda p, i: (p, 0, 0)),
                  pl.BlockSpec((None, K, nl), lambda p, i: (p + 2, 0, 0))],
        out_specs=[pl.BlockSpec((tm, 2 * nl), lambda p, i: (i, p)), pl.BlockSpec((tm, nl), lambda p, i: (i, p))],
        out_shape=[jax.ShapeDtypeStruct((T, 4 * nl), BF16), jax.ShapeDtypeStruct((T, 2 * nl), BF16)],
        compiler_params=_params(blk, 4 * _nbytes((tm, nl), F32)),
    )(x, w3, w3)


def ffn_dact(name, dh, w_out, u, after=None):
    T, N = dh.shape
    F = w_out.shape[0]
    nl = F // 2
    tm = _tile(T, 512, 8)

    def body(*refs):
        d_ref, w_ref, u_ref = refs[:3]
        o_ref = refs[-1]
        dact = 0.5 * lax.dot_general(d_ref[...].astype(BF16), w_ref[...], NT, preferred_element_type=F32)
        g = u_ref[:, :nl].astype(F32)
        up = u_ref[:, nl:].astype(F32)
        sig = jax.nn.sigmoid(g)
        o_ref[:, :nl] = (dact * up * (sig * (1.0 + g * (1.0 - sig)))).astype(o_ref.dtype)
        o_ref[:, nl:] = (dact * (g * sig)).astype(o_ref.dtype)

    ins = [dh, w_out, u]
    specs = [pl.BlockSpec((tm, N), lambda p, i: (i, 0)), pl.BlockSpec((nl, N), lambda p, i: (p, 0)),
             pl.BlockSpec((tm, 2 * nl), lambda p, i: (i, p))]
    if after is not None:
        ins.append(after)
        specs.append(pl.BlockSpec(memory_space=pl.ANY))
    blk = _nbytes((tm, N), F32) + _nbytes((nl, N), BF16) + 2 * _nbytes((tm, 2 * nl), BF16)
    return pl.pallas_call(
        body, name=name, grid=(2, T // tm), in_specs=specs,
        out_specs=pl.BlockSpec((tm, 2 * nl), lambda p, i: (i, p)),
        out_shape=jax.ShapeDtypeStruct((T, 2 * F), BF16),
        compiler_params=_params(blk, 6 * _nbytes((tm, nl), F32)),
    )(*ins)


def loss_head(name, h, g, target):
    T, D = h.shape
    tm = _tile(T, 256, 8)

    def body(h_ref, g_ref, t_ref, dh_ref, dg_ref, loss_ref):
        xv = h_ref[...]
        gv = g_ref[...]
        r = lax.rsqrt(jnp.mean(xv * xv, axis=-1, keepdims=True) + EPS)
        err = xv * r * gv - t_ref[...]
        part = 0.5 * jnp.sum(jnp.mean(err * err, axis=-1, keepdims=True), axis=0, keepdims=True)
        dx, dg = _rms_bwd_math(xv, gv, err * (1.0 / D))
        dh_ref[...] = dx
        part = jnp.broadcast_to(part, (1, LANE))

        @pl.when(pl.program_id(0) == 0)
        def _():
            dg_ref[...] = dg
            loss_ref[...] = part

        @pl.when(pl.program_id(0) > 0)
        def _():
            dg_ref[...] += dg
            loss_ref[...] += part

    row = pl.BlockSpec((tm, D), lambda i: (i, 0))
    vec = pl.BlockSpec((1, D), lambda i: (0, 0))
    dh, dg, loss = pl.pallas_call(
        body, name=name, grid=(T // tm,), in_specs=[row, vec, row],
        out_specs=[row, vec, pl.BlockSpec((1, LANE), lambda i: (0, 0))],
        out_shape=[jax.ShapeDtypeStruct((T, D), F32), jax.ShapeDtypeStruct((1, D), F32),
                   jax.ShapeDtypeStruct((1, LANE), F32)],
        compiler_params=_params(_nbytes((tm, D), F32) * 3, 6 * _nbytes((tm, D), F32)),
    )(h, g.reshape(1, D), target)
    return dh, dg.reshape(D), loss


def rope_tables(S):
    half = ROPE // 2
    freqs = ROPE_THETA ** (-jnp.arange(half, dtype=F32) / half)
    ang = jnp.arange(S, dtype=F32)[:, None] * freqs[None, :]
    cos, sin = jnp.cos(ang), jnp.sin(ang)
    z = jnp.zeros_like(cos)
    ct = jnp.concatenate([cos, cos, z, z], axis=1)
    s1 = jnp.concatenate([-sin, z, z, z], axis=1)
    s2 = jnp.concatenate([z, sin, z, z], axis=1)
    return ct, s1, s2


def _rope_tile(t, ct, s1, s2):
    return t * ct + pltpu.roll(t, 96, 1) * s1 + pltpu.roll(t, 32, 1) * s2


def _rope_tile_bwd(d, ct, s1, s2):
    return d * ct + pltpu.roll(d * s1, 32, 1) + pltpu.roll(d * s2, 96, 1)


def uq_rope(name, x, w3, tabs, S):
    T, K = x.shape
    _, _, nl = w3.shape
    tm = _tile(S, 512, 8)
    nt = S // tm

    def body(x_ref, w_ref, ct_ref, s1_ref, s2_ref, o_ref):
        q = jnp.dot(x_ref[...], w_ref[...], preferred_element_type=F32)
        ct, s1, s2 = ct_ref[...], s1_ref[...], s2_ref[...]
        for h in range(nl // 256):
            o_ref[:, 256 * h:256 * h + 128] = q[:, 256 * h:256 * h + 128].astype(o_ref.dtype)
            o_ref[:, 256 * h + 128:256 * h + 256] = _rope_tile(q[:, 256 * h + 128:256 * h + 256],
                                                               ct, s1, s2).astype(o_ref.dtype)

    tab = pl.BlockSpec((tm, LANE), lambda j, i: (i % nt, 0))
    blk = _nbytes((tm, K), BF16) + _nbytes((K, nl), BF16) + _nbytes((tm, nl), BF16) + 3 * _nbytes((tm, LANE), F32)
    return pl.pallas_call(
        body, name=name, grid=(N_CHIPS, T // tm),
        in_specs=[pl.BlockSpec((tm, K), lambda j, i: (i, 0)), pl.BlockSpec((None, K, nl), lambda j, i: (j, 0, 0)),
                  tab, tab, tab],
        out_specs=pl.BlockSpec((tm, nl), lambda j, i: (i, j)),
        out_shape=jax.ShapeDtypeStruct((T, N_CHIPS * nl), BF16),
        compiler_params=_params(blk, 4 * _nbytes((tm, nl), F32)),
    )(x, w3, *tabs)


def kvprep_fwd(name, ckr, g, tabs, B, S):
    T, W = ckr.shape
    KVL = W - LANE
    ts = _tile(S, 256, 8)

    def body(x_ref, g_ref, ct_ref, s1_ref, s2_ref, c_ref, k_ref):
        xv = x_ref[0, :, :KVL]
        r = lax.rsqrt(jnp.mean(xv * xv, axis=-1, keepdims=True) + EPS)
        c_ref[0] = (xv * r * g_ref[...]).astype(c_ref.dtype)
        k_ref[0] = _rope_tile(x_ref[0, :, KVL:], ct_ref[...], s1_ref[...], s2_ref[...]).astype(k_ref.dtype)

    tab = pl.BlockSpec((ts, LANE), lambda b, s: (s, 0))
    c, k = pl.pallas_call(
        body, name=name, grid=(B, S // ts),
        in_specs=[pl.BlockSpec((1, ts, W), lambda b, s: (b, s, 0)), pl.BlockSpec((1, KVL), lambda b, s: (0, 0)),
                  tab, tab, tab],
        out_specs=[pl.BlockSpec((1, ts, KVL), lambda b, s: (b, s, 0)),
                   pl.BlockSpec((1, ts, LANE), lambda b, s: (b, s, 0))],
        out_shape=[jax.ShapeDtypeStruct((B, S, KVL), BF16), jax.ShapeDtypeStruct((B, S, LANE), BF16)],
        compiler_params=_params(_nbytes((ts, W), F32) * 2, _nbytes((ts, W), F32) * 2),
    )(ckr.reshape(B, S, W), g.reshape(1, KVL), *tabs)
    return c.reshape(T, KVL), k


def kvprep_bwd(name, ckr, g, dc, dkr, tabs, B, S):
    T, W = ckr.shape
    KVL = W - LANE
    ts = _tile(S, 256, 8)

    def body(x_ref, g_ref, dc_ref, dk_ref, ct_ref, s1_ref, s2_ref, o_ref, dg_ref):
        dx, dg = _rms_bwd_math(x_ref[0, :, :KVL], g_ref[...], dc_ref[0])
        o_ref[0, :, :KVL] = dx
        o_ref[0, :, KVL:] = _rope_tile_bwd(dk_ref[0], ct_ref[...], s1_ref[...], s2_ref[...])
        first = (pl.program_id(0) == 0) & (pl.program_id(1) == 0)

        @pl.when(first)
        def _():
            dg_ref[...] = dg

        @pl.when(jnp.logical_not(first))
        def _():
            dg_ref[...] += dg

    tab = pl.BlockSpec((ts, LANE), lambda b, s: (s, 0))
    vec = pl.BlockSpec((1, KVL), lambda b, s: (0, 0))
    o, dg = pl.pallas_call(
        body, name=name, grid=(B, S // ts),
        in_specs=[pl.BlockSpec((1, ts, W), lambda b, s: (b, s, 0)), vec,
                  pl.BlockSpec((1, ts, KVL), lambda b, s: (b, s, 0)),
                  pl.BlockSpec((1, ts, LANE), lambda b, s: (b, s, 0)), tab, tab, tab],
        out_specs=[pl.BlockSpec((1, ts, W), lambda b, s: (b, s, 0)), vec],
        out_shape=[jax.ShapeDtypeStruct((B, S, W), F32), jax.ShapeDtypeStruct((1, KVL), F32)],
        compiler_params=_params(_nbytes((ts, W), F32) * 4, _nbytes((ts, W), F32) * 4),
    )(ckr.reshape(B, S, W), g.reshape(1, KVL), dc.reshape(B, S, KVL), dkr, *tabs)
    return o.reshape(T, W), dg.reshape(KVL)


DIAGS = 768


def _diag_onehot():
    col = lax.broadcasted_iota(I32, (REL_PAD, DIAGS), 1)
    row = lax.broadcasted_iota(I32, (REL_PAD, DIAGS), 0)
    idx = jnp.clip(PADR + QROWS - 1 - col, -MAX_REL, MAX_REL) + MAX_REL
    return (row == idx).astype(F32)


def rel_bias_tile(name, table):
    H = table.shape[0]
    tpad = jnp.pad(table, ((0, 0), (0, REL_PAD - table.shape[1])))

    def body(t_ref, o_ref):
        g = lax.dot_general(t_ref[...], _diag_onehot(), NN, precision=lax.Precision.HIGHEST,
                            preferred_element_type=F32)
        qc = jnp.right_shift(lax.broadcasted_iota(I32, (QROWS, WIN), 0), CHUNK_SHIFT)
        kc = jnp.right_shift(lax.broadcasted_iota(I32, (QROWS, WIN), 1), CHUNK_SHIFT)
        band = (kc >= qc) & (kc <= qc + LEFT_CHUNKS)
        for h in range(H):
            gb = jnp.broadcast_to(g[h:h + 1, :], (QROWS, DIAGS))
            tile = pltpu.roll(gb, DIAGS - (QROWS - 1), 1, stride=1, stride_axis=0)
            o_ref[h // 2, (h % 2) * QROWS:(h % 2 + 1) * QROWS, :] = jnp.where(band, tile[:, :WIN], NEG_INF)

    return pl.pallas_call(
        body, name=name, out_shape=jax.ShapeDtypeStruct((H // 2, 2 * QROWS, WIN), F32),
        compiler_params=_params(0, 2 * _nbytes((H // 2, 2 * QROWS, WIN), F32)),
    )(tpad)


def rel_bias_grad(name, dbias):
    H = 2 * dbias.shape[0]

    def body(d_ref, o_ref):
        flip = (lax.broadcasted_iota(I32, (QROWS, QROWS), 0) + lax.broadcasted_iota(I32, (QROWS, QROWS), 1)
                == QROWS - 1).astype(F32)
        rows = []
        for h in range(H):
            x = d_ref[h // 2, (h % 2) * QROWS:(h % 2 + 1) * QROWS, :]
            xr = lax.dot_general(flip, x, NN, precision=lax.Precision.HIGHEST, preferred_element_type=F32)
            xp = jnp.concatenate([xr, jnp.zeros((QROWS, DIAGS - WIN), F32)], axis=1)
            y = pltpu.roll(xp, 0, 1, stride=1, stride_axis=0)
            rows.append(jnp.sum(y, axis=0, keepdims=True))
        o_ref[...] = lax.dot_general(jnp.concatenate(rows, axis=0), _diag_onehot(), NT,
                                     precision=lax.Precision.HIGHEST, preferred_element_type=F32)

    return pl.pallas_call(
        body, name=name, out_shape=jax.ShapeDtypeStruct((H, REL_PAD), F32),
        compiler_params=_params(0, 2 * _nbytes(dbias.shape, F32)),
    )(dbias)


def _stack_pair(xp):
    lane = lax.broadcasted_iota(I32, xp.shape, 1)
    z = jnp.zeros_like(xp)
    return jnp.concatenate([jnp.where(lane < HEAD_DIM_A, xp, z), jnp.where(lane >= HEAD_DIM_A, xp, z)], axis=0)


def _unstack_pair(y):
    lane = lax.broadcasted_iota(I32, (QROWS, LANE), 1)
    return jnp.where(lane < HEAD_DIM_A, y[:QROWS], y[QROWS:])


def _attn_a_rowpen(j):
    w = lax.broadcasted_iota(I32, (1, WIN), 1)
    return jnp.where(w >= PADR - QROWS * j, 0.0, NEG_INF).astype(F32)


def _attn_a_load_bias(bias_hbm, bias_v, sem):
    cp = pltpu.make_async_copy(bias_hbm, bias_v, sem)
    cp.start()
    cp.wait()


def _attn_a_load_kv(qkv_hbm, b, kpad, vpad, sem, S, D):
    kpad[0:PADR, :] = jnp.zeros((PADR, D), BF16)
    vpad[0:PADR, :] = jnp.zeros((PADR, D), BF16)
    ck = pltpu.make_async_copy(qkv_hbm.at[b, :, pl.ds(D, D)], kpad.at[pl.ds(PADR, S), :], sem.at[0])
    cv = pltpu.make_async_copy(qkv_hbm.at[b, :, pl.ds(2 * D, D)], vpad.at[pl.ds(PADR, S), :], sem.at[1])
    ck.start()
    cv.start()
    ck.wait()
    cv.wait()


def _attn_a_exp(q2s, kp, bias, pen):
    s = lax.dot_general(q2s, kp, NT, preferred_element_type=F32) + bias + pen
    e = jnp.exp(s - jnp.max(s, axis=-1, keepdims=True))
    return e, 1.0 / jnp.sum(e, axis=-1, keepdims=True)


def attn_a_fwd(name, qkv, bias):
    B, S, D3 = qkv.shape
    D = D3 // 3
    H = D // HEAD_DIM_A
    nb = S // QROWS
    scale = HEAD_DIM_A ** -0.5

    def body(q_ref, bias_hbm, qkv_hbm, o_ref, kpad, vpad, bias_v, sem):
        b, j = pl.program_id(0), pl.program_id(1)

        @pl.when((b == 0) & (j == 0))
        def _():
            _attn_a_load_bias(bias_hbm, bias_v, sem.at[2])

        @pl.when(j == 0)
        def _():
            _attn_a_load_kv(qkv_hbm, b, kpad, vpad, sem, S, D)

        pen = _attn_a_rowpen(j)
        w0 = pl.multiple_of(j * QROWS, QROWS)
        for p in range(H // 2):
            ls = slice(p * LANE, (p + 1) * LANE)
            e, rl = _attn_a_exp(_stack_pair(q_ref[0, :, ls] * scale), kpad[pl.ds(w0, WIN), ls], bias_v[p], pen)
            o2 = jnp.dot(e.astype(BF16), vpad[pl.ds(w0, WIN), ls], preferred_element_type=F32) * rl
            o_ref[0, :, ls] = _unstack_pair(o2).astype(o_ref.dtype)

    scr = 2 * _nbytes((PADR + S, D), BF16) + _nbytes(bias.shape, F32) + 8 * _nbytes((2 * QROWS, WIN), F32)
    return pl.pallas_call(
        body, name=name, grid=(B, nb),
        in_specs=[pl.BlockSpec((1, QROWS, D), lambda b, j: (b, j, 0)),
                  pl.BlockSpec(memory_space=pl.ANY), pl.BlockSpec(memory_space=pl.ANY)],
        out_specs=pl.BlockSpec((1, QROWS, D), lambda b, j: (b, j, 0)),
        out_shape=jax.ShapeDtypeStruct((B, S, D), BF16),
        scratch_shapes=[pltpu.VMEM((PADR + S, D), BF16), pltpu.VMEM((PADR + S, D), BF16),
                        pltpu.VMEM(bias.shape, F32), pltpu.SemaphoreType.DMA((3,))],
        compiler_params=_params(2 * _nbytes((QROWS, D), BF16), scr),
    )(qkv, bias, qkv)


def attn_a_bwd(name, qkv, do, bias):
    B, S, D3 = qkv.shape
    D = D3 // 3
    H = D // HEAD_DIM_A
    nb = S // QROWS
    scale = HEAD_DIM_A ** -0.5

    def body(q_ref, do_ref, bias_hbm, qkv_hbm, dqkv_hbm, dbias_hbm, kpad, vpad, dkacc, dvacc, bias_v, dbias_v,
             dq_stage, sem):
        b, j = pl.program_id(0), pl.program_id(1)
        step = b * nb + j
        slot = lax.rem(step, 2)

        def dq_out(s):
            return pltpu.make_async_copy(dq_stage.at[s], dqkv_hbm.at[b, pl.ds(j * QROWS, QROWS), pl.ds(0, D)],
                                         sem.at[3 + s])

        @pl.when(step >= 2)
        def _():
            dq_out(slot).wait()

        @pl.when((b == 0) & (j == 0))
        def _():
            _attn_a_load_bias(bias_hbm, bias_v, sem.at[2])
            dbias_v[...] = jnp.zeros_like(dbias_v)

        @pl.when(j == 0)
        def _():
            _attn_a_load_kv(qkv_hbm, b, kpad, vpad, sem, S, D)
            dkacc[...] = jnp.zeros_like(dkacc)
            dvacc[...] = jnp.zeros_like(dvacc)

        pen = _attn_a_rowpen(j)
        w0 = pl.multiple_of(j * QROWS, QROWS)
        for p in range(H // 2):
            ls = slice(p * LANE, (p + 1) * LANE)
            q2s = _stack_pair(q_ref[0, :, ls] * scale)
            do2 = _stack_pair(do_ref[0, :, ls])
            kp = kpad[pl.ds(w0, WIN), ls]
            vp = vpad[pl.ds(w0, WIN), ls]
            e, rl = _attn_a_exp(q2s, kp, bias_v[p], pen)
            pr = e * rl
            dp = lax.dot_general(do2, vp, NT, preferred_element_type=F32)
            ds = pr * (dp - jnp.sum(pr * dp, axis=-1, keepdims=True))
            dbias_v[p] += ds
            dsb = ds.astype(BF16)
            dq_stage[slot, :, ls] = (_unstack_pair(jnp.dot(dsb, kp, preferred_element_type=F32))
                                     * scale).astype(dq_stage.dtype)
            dkacc[pl.ds(w0, WIN), ls] += lax.dot_general(dsb, q2s, TN, preferred_element_type=F32)
            dvacc[pl.ds(w0, WIN), ls] += lax.dot_general(pr.astype(BF16), do2, TN, preferred_element_type=F32)

        dq_out(slot).start()

        @pl.when(j == nb - 1)
        def _():
            kpad[pl.ds(PADR, S), :] = dkacc[pl.ds(PADR, S), :].astype(BF16)
            vpad[pl.ds(PADR, S), :] = dvacc[pl.ds(PADR, S), :].astype(BF16)
            ck = pltpu.make_async_copy(kpad.at[pl.ds(PADR, S), :], dqkv_hbm.at[b, :, pl.ds(D, D)], sem.at[0])
            cv = pltpu.make_async_copy(vpad.at[pl.ds(PADR, S), :], dqkv_hbm.at[b, :, pl.ds(2 * D, D)], sem.at[1])
            ck.start()
            cv.start()
            ck.wait()
            cv.wait()

        @pl.when((b == B - 1) & (j == nb - 1))
        def _():
            cb = pltpu.make_async_copy(dbias_v, dbias_hbm, sem.at[2])
            cb.start()
            dq_out(0).wait()
            dq_out(1).wait()
            cb.wait()

    blk = _nbytes((QROWS, D), BF16) * 2
    scr = (2 * _nbytes((PADR + S, D), BF16) + 2 * _nbytes((PADR + S, D), F32) + 2 * _nbytes(bias.shape, F32)
           + 8 * _nbytes((2 * QROWS, WIN), F32) + 2 * _nbytes((QROWS, D), F32))
    return pl.pallas_call(
        body, name=name, grid=(B, nb),
        in_specs=[pl.BlockSpec((1, QROWS, D), lambda b, j: (b, j, 0)),
                  pl.BlockSpec((1, QROWS, D), lambda b, j: (b, j, 0)),
                  pl.BlockSpec(memory_space=pl.ANY), pl.BlockSpec(memory_space=pl.ANY)],
        out_specs=[pl.BlockSpec(memory_space=pl.ANY), pl.BlockSpec(memory_space=pl.ANY)],
        out_shape=[jax.ShapeDtypeStruct((B, S, 3 * D), BF16), jax.ShapeDtypeStruct(bias.shape, F32)],
        scratch_shapes=[pltpu.VMEM((PADR + S, D), BF16), pltpu.VMEM((PADR + S, D), BF16),
                        pltpu.VMEM((PADR + S, D), F32), pltpu.VMEM((PADR + S, D), F32),
                        pltpu.VMEM(bias.shape, F32), pltpu.VMEM(bias.shape, F32),
                        pltpu.VMEM((2, QROWS, D), BF16), pltpu.SemaphoreType.DMA((5,))],
        compiler_params=_params(blk, scr),
    )(qkv, do, bias, qkv)


def _mla_raw_t(k2, kj, q, QB):
    return lax.dot_general(k2[_blk(kj, QB), :], q, NT, preferred_element_type=F32)


def _blk(kj, QB):
    return pl.ds(kj * QB, QB) if isinstance(kj, int) else pl.ds(pl.multiple_of(kj * QB, QB), QB)


def _mla_diag_pen(QB):
    kc = jnp.right_shift(lax.broadcasted_iota(I32, (QB, QB), 0), CHUNK_SHIFT)
    qc = jnp.right_shift(lax.broadcasted_iota(I32, (QB, QB), 1), CHUNK_SHIFT)
    return jnp.where(kc <= qc, 0.0, NEG_INF).astype(F32)


def _mla_fill_keys(kv_ref, kr_ref, k2):
    k2[:, :NOPE] = kv_ref[0, :, :NOPE]
    k2[:, NOPE:] = kr_ref[0]


def _t(x):
    return x.astype(F32).T


def mla_fwd(name, qf, kv, kr):
    B, S, W = qf.shape
    HB = W // 256
    QB = _tile(S, 256, CHUNK)
    nq = S // QB
    scale = (NOPE + ROPE) ** -0.5

    def body(q_ref, kv_ref, kr_ref, o_ref, lse_ref, k2, vt, st_buf, pen):
        qi = pl.program_id(2)

        @pl.when(qi == 0)
        def _():
            pen[...] = _mla_diag_pen(QB)
            _mla_fill_keys(kv_ref, kr_ref, k2)
            for kj in range(nq):
                vt[kj] = _t(kv_ref[0, kj * QB:(kj + 1) * QB, NOPE:]).astype(BF16)

        q = q_ref[0]
        st_buf[0] = _mla_raw_t(k2, 0, q, QB)

        def step(kj, carry):
            m, l, acc = carry
            cur = lax.rem(kj, 2)
            st_raw = st_buf[cur]
            st_buf[1 - cur] = _mla_raw_t(k2, jnp.minimum(kj + 1, qi), q, QB)
            st = st_raw * scale + jnp.where(kj == qi, pen[...], 0.0)
            m_new = jnp.maximum(m, jnp.max(st, axis=0, keepdims=True))
            a = jnp.exp(m - m_new)
            pt = jnp.exp(st - m_new)
            l = a * l + jnp.sum(pt, axis=0, keepdims=True)
            acc = a * acc + jnp.dot(vt[kj], pt.astype(BF16), preferred_element_type=F32)
            return m_new, l, acc

        init = (jnp.full((1, QB), NEG_INF, F32), jnp.zeros((1, QB), F32), jnp.zeros((NOPE, QB), F32))
        m, l, acc = lax.fori_loop(0, qi + 1, step, init)
        o_ref[0] = (acc * (1.0 / l)).T
        lse_ref[0, 0] = m + jnp.log(l)

    blk = (_nbytes((QB, 256), BF16) + _nbytes((S, 256), BF16) + _nbytes((S, LANE), BF16)
           + _nbytes((QB, LANE), F32))
    return pl.pallas_call(
        body, name=name, grid=(B, HB, nq),
        in_specs=[pl.BlockSpec((1, QB, 256), lambda b, h, i: (b, i, h)),
                  pl.BlockSpec((1, S, 256), lambda b, h, i: (b, 0, h)),
                  pl.BlockSpec((1, S, LANE), lambda b, h, i: (b, 0, 0))],
        out_specs=[pl.BlockSpec((1, QB, LANE), lambda b, h, i: (b, i, h)),
                   pl.BlockSpec((1, 1, 1, QB), lambda b, h, i: (b, h, 0, i))],
        out_shape=[jax.ShapeDtypeStruct((B, S, HB * LANE), F32), jax.ShapeDtypeStruct((B, HB, 1, S), F32)],
        scratch_shapes=[pltpu.VMEM((S, 256), BF16), pltpu.VMEM((nq, NOPE, QB), BF16),
                        pltpu.VMEM((2, QB, QB), F32), pltpu.VMEM((QB, QB), F32)],
        compiler_params=_params(blk, 2 * _nbytes((S, 256), BF16) + 10 * _nbytes((QB, QB), F32)),
    )(qf, kv, kr)


def mla_bwd(name, qf, kv, kr, do, o, lse, tabs):
    B, S, W = qf.shape
    HB = W // 256
    QB = _tile(S, 256, CHUNK)
    nq = S // QB
    scale = (NOPE + ROPE) ** -0.5

    def body(q_ref, kv_ref, kr_ref, do_ref, o_ref, lse_ref, ct_ref, s1_ref, s2_ref, dq_ref, dkv_ref, dkr_ref,
             k2, kt, dot_, delta, dqt, st_buf, dp_buf, pen, dkv_acc):
        h = pl.program_id(1)
        pen[...] = _mla_diag_pen(QB)
        dkv_acc[...] = jnp.zeros_like(dkv_acc)

        @pl.when(h == 0)
        def _():
            dkr_ref[...] = jnp.zeros_like(dkr_ref)

        _mla_fill_keys(kv_ref, kr_ref, k2)
        for i in range(nq):
            rows = slice(i * QB, (i + 1) * QB)
            kt[i] = _t(k2[rows, :]).astype(BF16)
            dot32 = _t(do_ref[0, rows, :])
            delta[i] = jnp.sum(dot32 * o_ref[0, rows, :].T, axis=0, keepdims=True)
            dot_[i] = dot32.astype(BF16)

        for qi in range(nq):
            rows = slice(qi * QB, (qi + 1) * QB)
            q = q_ref[0, rows, :]
            dob = do_ref[0, rows, :]
            lse_q = lse_ref[0, 0, :, rows]
            delta_q = delta[qi]
            dqt[...] = jnp.zeros_like(dqt)

            def raw(kj, slot, q=q, qi=qi):
                st_buf[slot] = _mla_raw_t(k2, kj, q, QB)
                dp_buf[slot] = jnp.dot(kv_ref[0, _blk(kj, QB), NOPE:], dot_[qi], preferred_element_type=F32)

            raw(0, 0)

            def step(kj, carry, q=q, dob=dob, lse_q=lse_q, delta_q=delta_q, qi=qi, raw=raw):
                ks = pl.ds(pl.multiple_of(kj * QB, QB), QB)
                cur = lax.rem(kj, 2)
                st_raw, dp_raw = st_buf[cur], dp_buf[cur]
                raw(jnp.minimum(kj + 1, qi), 1 - cur)
                pt = jnp.exp(st_raw * scale + jnp.where(kj == qi, pen[...], 0.0) - lse_q)
                dst = (pt * (dp_raw - delta_q) * scale).astype(BF16)
                dkv_acc[ks, NOPE:] += jnp.dot(pt.astype(BF16), dob, preferred_element_type=F32)
                dk2 = jnp.dot(dst, q, preferred_element_type=F32)
                dkv_acc[ks, :NOPE] += dk2[:, :NOPE]
                dkr_ref[0, ks, :] += dk2[:, NOPE:]
                dqt[...] += jnp.dot(kt[kj], dst, preferred_element_type=F32)
                return carry

            lax.fori_loop(0, qi + 1, step, 0)
            dq = dqt[...].T
            dq_ref[0, rows, :NOPE] = dq[:, :NOPE].astype(dq_ref.dtype)
            dq_ref[0, rows, NOPE:] = _rope_tile_bwd(dq[:, NOPE:], ct_ref[rows, :], s1_ref[rows, :],
                                                    s2_ref[rows, :]).astype(dq_ref.dtype)

        dkv_ref[0] = dkv_acc[...].astype(dkv_ref.dtype)

    head = lambda w: pl.BlockSpec((1, S, w), lambda b, h: (b, 0, h))
    shared = pl.BlockSpec((1, S, LANE), lambda b, h: (b, 0, 0))
    blk = (2 * _nbytes((S, 256), BF16) + 2 * _nbytes((S, LANE), BF16) + _nbytes((S, LANE), F32)
           + 2 * _nbytes((S, 256), F32) + _nbytes((S, LANE), F32))
    scr = 3 * _nbytes((S, 256), BF16) + 14 * _nbytes((QB, QB), F32)
    return pl.pallas_call(
        body, name=name, grid=(B, HB),
        in_specs=[head(256), head(256), shared, head(LANE), head(LANE),
                  pl.BlockSpec((1, 1, 1, S), lambda b, h: (b, h, 0, 0))]
        + [pl.BlockSpec((S, LANE), lambda b, h: (0, 0))] * 3,
        out_specs=[head(256), head(256), shared],
        out_shape=[jax.ShapeDtypeStruct((B, S, W), BF16), jax.ShapeDtypeStruct((B, S, W), BF16),
                   jax.ShapeDtypeStruct((B, S, LANE), F32)],
        scratch_shapes=[pltpu.VMEM((S, 256), BF16), pltpu.VMEM((nq, 256, QB), BF16),
                        pltpu.VMEM((nq, NOPE, QB), BF16), pltpu.VMEM((nq, 1, QB), F32),
                        pltpu.VMEM((256, QB), F32), pltpu.VMEM((2, QB, QB), F32), pltpu.VMEM((2, QB, QB), F32),
                        pltpu.VMEM((QB, QB), F32), pltpu.VMEM((S, 256), F32)],
        compiler_params=_params(blk, scr),
    )(qf, kv, kr, do, o, lse, *tabs)


GROUP_STEPS = 2


def cast_group(name, ws, layers, idx, after=None):
    n = len(ws)
    n_in = n + (after is not None)

    def body(k_ref, *refs):
        for i in range(n):
            refs[n_in + i][...] = refs[i][...].astype(BF16)

    def spec_in(w, layer):
        return pl.BlockSpec((None, w.shape[1] // GROUP_STEPS, w.shape[2]), lambda r, k_ref: (layer, r, 0))

    def spec_out(w):
        return pl.BlockSpec((None, w.shape[1] // GROUP_STEPS, w.shape[2]), lambda r, k_ref: (k_ref[0], r, 0))

    return pl.pallas_call(
        body, name=name,
        grid_spec=pltpu.PrefetchScalarGridSpec(
            num_scalar_prefetch=1, grid=(GROUP_STEPS,),
            in_specs=([spec_in(w, l) for w, l in zip(ws, layers)]
                      + [pl.BlockSpec(memory_space=pl.ANY)] * (after is not None)),
            out_specs=[spec_out(w) for w in ws]),
        out_shape=[jax.ShapeDtypeStruct((N_CHIPS, *w.shape[1:]), BF16) for w in ws],
        compiler_params=_params(sum(_nbytes(w.shape[1:], F32) * 3 // 2 for w in ws) // GROUP_STEPS),
    )(idx, *ws, *([] if after is None else [after]))


def adamw(name, w, g, m, v):
    R, C = w.shape
    tr = _tile(R, max(8, (1 << 18) // C // 8 * 8), 8)
    c1 = 1.0 - ADAM_B1 ** ADAM_STEP
    c2 = 1.0 - ADAM_B2 ** ADAM_STEP

    def body(w_ref, g_ref, m_ref, v_ref, d_ref, mo_ref, vo_ref):
        gv = g_ref[...]
        mn = ADAM_B1 * m_ref[...] + (1.0 - ADAM_B1) * gv
        vn = ADAM_B2 * v_ref[...] + (1.0 - ADAM_B2) * (gv * gv)
        mo_ref[...] = mn
        vo_ref[...] = vn
        d_ref[...] = -ADAM_LR * ((mn / c1) / (jnp.sqrt(vn / c2) + ADAM_EPS) + ADAM_WD * w_ref[...])

    spec = pl.BlockSpec((tr, C), lambda r: (r, 0))
    return pl.pallas_call(
        body, name=name, grid=(R // tr,), in_specs=[spec] * 4, out_specs=[spec] * 3,
        out_shape=[jax.ShapeDtypeStruct((R, C), F32)] * 3,
        compiler_params=_params(7 * _nbytes((tr, C), F32), 4 * _nbytes((tr, C), F32)),
    )(w, g, m, v)


def half_sum_group(name, dws, landed, idx):
    n = len(dws)
    steps = GROUP_STEPS // 2

    def body(i_ref, *refs):
        for i in range(n):
            refs[2 * n + i][...] = (refs[i][...].astype(F32) + refs[n + i][...].astype(F32)).astype(BF16)

    def own(d):
        return pl.BlockSpec((None, None, d.shape[2] // steps, d.shape[3]), lambda k, r, i_ref: (k, i_ref[1], r, 0))

    def flat(d):
        return pl.BlockSpec((None, d.shape[2] // steps, d.shape[3]), lambda k, r, i_ref: (k, r, 0))

    return pl.pallas_call(
        body, name=name,
        grid_spec=pltpu.PrefetchScalarGridSpec(
            num_scalar_prefetch=1, grid=(N_CHIPS, steps),
            in_specs=[own(d) for d in dws] + [flat(d) for d in dws], out_specs=[flat(d) for d in dws]),
        out_shape=[jax.ShapeDtypeStruct((N_CHIPS, *d.shape[2:]), BF16) for d in dws],
        compiler_params=_params(sum(3 * _nbytes(d.shape[2:], BF16) for d in dws) // steps),
    )(idx, *dws, *landed)


def chip_sum_group(name, parts, landed, gbufs, layers, idx):
    n = len(parts)
    steps = GROUP_STEPS // 2

    def body(i_ref, *refs):
        for i in range(n):
            a, b = refs[i], refs[n + i]
            refs[3 * n + i][...] = ((a[...].astype(F32) + b[0].astype(F32)) + b[1].astype(F32)) + b[2].astype(F32)

    def mine(p):
        return pl.BlockSpec((None, p.shape[1] // steps, p.shape[2]), lambda r, i_ref: (i_ref[0], r, 0))

    def three(p):
        return pl.BlockSpec((3, p.shape[1] // steps, p.shape[2]), lambda r, i_ref: (0, r, 0))

    def out(p, layer):
        return pl.BlockSpec((None, None, p.shape[1] // steps, p.shape[2]), lambda r, i_ref: (layer, i_ref[1], r, 0))

    return pl.pallas_call(
        body, name=name,
        grid_spec=pltpu.PrefetchScalarGridSpec(
            num_scalar_prefetch=1, grid=(steps,),
            in_specs=[mine(p) for p in parts] + [three(p) for p in parts] + [pl.BlockSpec(memory_space=pl.ANY)] * n,
            out_specs=[out(p, l) for p, l in zip(parts, layers)]),
        out_shape=[jax.ShapeDtypeStruct(g.shape, F32) for g in gbufs],
        input_output_aliases={1 + 2 * n + i: i for i in range(n)},
        compiler_params=_params(sum(6 * _nbytes(p.shape[1:], BF16) for p in parts) // steps),
    )(idx, *parts, *landed, *gbufs)


ANY = pl.BlockSpec(memory_space=pl.ANY)


def _place():
    x, y, c = lax.axis_index("x"), lax.axis_index("y"), lax.axis_index("c")
    chips = [(1 - x, y), (x, 1 - y), (1 - x, 1 - y)]
    return x, y, c, chips


HBM = pl.BlockSpec(memory_space=pltpu.HBM)
SEM = pl.BlockSpec(memory_space=pltpu.SEMAPHORE)
EFFECT = pltpu.SideEffectType.DATAFLOW_SIDE_EFFECTING


def _in_hbm(a):
    return pltpu.with_memory_space_constraint(a, pltpu.HBM)


def _ici_copy(src, dst, send_sems, recv_sems, k, peer):
    return pltpu.make_async_remote_copy(src_ref=src, dst_ref=dst, send_sem=send_sems.at[k], recv_sem=recv_sems.at[k],
                                        device_id=peer, device_id_type=MESH)


def ici_start(name, bufs, lands, after, gather):
    n, nl = len(bufs), len(lands)

    def body(*refs):
        b_in = refs[:n]
        send_sems, recv_sems = refs[n + nl + 1], refs[n + nl + 2]
        b_out = refs[n + nl + 3:2 * n + nl + 3]
        l_out = refs[2 * n + nl + 3:2 * n + 2 * nl + 3]
        token = refs[-1]
        x, y, c, chips = _place()
        kme = 2 * x + y
        for i in range(n):
            for j in range(3):
                peer = (*chips[j], c)
                if gather:
                    _ici_copy(b_out[i].at[kme, c], b_out[i].at[kme, c], send_sems, recv_sems, 3 * i + j, peer).start()
                else:
                    kd = 2 * chips[j][0] + chips[j][1]
                    _ici_copy(b_out[i].at[kd], l_out[i].at[j], send_sems, recv_sems, 3 * i + j, peer).start()
        token[...] = jnp.zeros_like(token)

    arrays = [*bufs, *lands]
    outs = pl.pallas_call(
        body, name=name,
        in_specs=[HBM] * (n + nl) + [ANY],
        out_specs=(SEM, SEM, *[HBM] * (n + nl), pl.BlockSpec(memory_space=pltpu.VMEM)),
        out_shape=(pltpu.SemaphoreType.DMA((3 * n,)), pltpu.SemaphoreType.DMA((3 * n,)),
                   *[pltpu.HBM(a.shape, a.dtype) for a in arrays], jax.ShapeDtypeStruct((8, LANE), F32)),
        input_output_aliases={i: 2 + i for i in range(n + nl)},
        compiler_params=pltpu.CompilerParams(has_side_effects=EFFECT),
    )(*[_in_hbm(a) for a in arrays], after)
    return outs[0], outs[1], list(outs[2:2 + n]), list(outs[2 + n:2 + n + nl]), outs[-1]


def ici_wait(name, send_sems, recv_sems, bufs, lands, after, gather):
    n, nl = len(bufs), len(lands)

    def body(*refs):
        b_in, l_in = refs[:n], refs[n:n + nl]
        send_sems, recv_sems = refs[n + nl], refs[n + nl + 1]
        x, y, c, chips = _place()
        kme = 2 * x + y
        for i in range(n):
            for j in range(3):
                peer = (*chips[j], c)
                kj = 2 * chips[j][0] + chips[j][1]
                if gather:
                    _ici_copy(b_in[i].at[kme, c], b_in[i].at[kme, c], send_sems, recv_sems, 3 * i + j, peer).wait_send()
                    _ici_copy(b_in[i].at[kj, c], b_in[i].at[kj, c], send_sems, recv_sems, 3 * i + j, peer).wait_recv()
                else:
                    _ici_copy(b_in[i].at[kj], l_in[i].at[j], send_sems, recv_sems, 3 * i + j, peer).wait_send()
                    _ici_copy(b_in[i].at[kj], l_in[i].at[j], send_sems, recv_sems, 3 * i + j, peer).wait_recv()

    arrays = [*bufs, *lands]
    outs = pl.pallas_call(
        body, name=name,
        in_specs=[HBM] * (n + nl) + [SEM, SEM, ANY],
        out_specs=tuple([HBM] * (n + nl)),
        out_shape=tuple(pltpu.HBM(a.shape, a.dtype) for a in arrays),
        input_output_aliases={i: i for i in range(n + nl)},
        compiler_params=pltpu.CompilerParams(has_side_effects=EFFECT),
    )(*arrays, send_sems, recv_sems, after)
    return list(outs[:n]), list(outs[n:])


def gather_pair_pass(name, bufs):
    n = len(bufs)

    def body(*refs):
        b = refs[n:2 * n]
        send_sems, recv_sems = refs[2 * n:]
        x, y, c, chips = _place()
        sib = (x, y, 1 - c)

        def d2d(i, j, which):
            kj = 2 * chips[j][0] + chips[j][1]
            return _ici_copy(b[i].at[kj, which], b[i].at[kj, which], send_sems, recv_sems, 3 * i + j, sib)

        for i in range(n):
            for j in range(3):
                d2d(i, j, c).start()
        for i in range(n):
            for j in range(3):
                d2d(i, j, 1 - c).wait_recv()
        for i in range(n):
            for j in range(3):
                d2d(i, j, c).wait_send()

    return pl.pallas_call(
        body, name=name, in_specs=[ANY] * n, out_specs=[ANY] * n,
        out_shape=[jax.ShapeDtypeStruct(a.shape, a.dtype) for a in bufs],
        input_output_aliases={i: i for i in range(n)},
        scratch_shapes=[pltpu.SemaphoreType.DMA((3 * n,)), pltpu.SemaphoreType.DMA((3 * n,))],
    )(*bufs)


def pair_exchange(name, dws):
    n = len(dws)

    def body(*refs):
        ins, outs = refs[:n], refs[n:2 * n]
        send_sems, recv_sems = refs[2 * n:]
        x, y, c, _ = _place()
        copies = []
        for i in range(n):
            copies.append(pltpu.make_async_remote_copy(
                src_ref=ins[i].at[:, 1 - c], dst_ref=outs[i],
                send_sem=send_sems.at[i], recv_sem=recv_sems.at[i],
                device_id=(x, y, 1 - c), device_id_type=MESH))
            copies[i].start()
        for cp in copies:
            cp.wait_recv()
        for cp in copies:
            cp.wait_send()

    return pl.pallas_call(
        body, name=name, in_specs=[ANY] * n, out_specs=[ANY] * n,
        out_shape=[jax.ShapeDtypeStruct((N_CHIPS, *d.shape[2:]), d.dtype) for d in dws],
        scratch_shapes=[pltpu.SemaphoreType.DMA((n,)), pltpu.SemaphoreType.DMA((n,))],
    )(*dws)


def pair_assemble(gbufs):
    n = len(gbufs)

    def body(*refs):
        bufs = refs[n:2 * n]
        send_sems, recv_sems = refs[2 * n:]
        x, y, c, _ = _place()
        copies = []
        for i in range(n):
            copies.append(pltpu.make_async_remote_copy(
                src_ref=bufs[i].at[:, c], dst_ref=bufs[i].at[:, c],
                send_sem=send_sems.at[i], recv_sem=recv_sems.at[i],
                device_id=(x, y, 1 - c), device_id_type=MESH))
            copies[i].start()
        for i in range(n):
            pltpu.make_async_remote_copy(
                src_ref=bufs[i].at[:, 1 - c], dst_ref=bufs[i].at[:, 1 - c],
                send_sem=send_sems.at[i], recv_sem=recv_sems.at[i],
                device_id=(x, y, 1 - c), device_id_type=MESH).wait_recv()
        for cp in copies:
            cp.wait_send()

    return pl.pallas_call(
        body, name="grad_pair_assemble", in_specs=[ANY] * n, out_specs=[ANY] * n,
        out_shape=[jax.ShapeDtypeStruct(g.shape, g.dtype) for g in gbufs],
        input_output_aliases={i: i for i in range(n)},
        scratch_shapes=[pltpu.SemaphoreType.DMA((n,)), pltpu.SemaphoreType.DMA((n,))],
    )(*gbufs)


def all_reduce_small(vec):
    NR = vec.shape[0]
    flips = [(fx, fy, fc) for fx in (0, 1) for fy in (0, 1) for fc in (0, 1)][1:]

    def body(v_ref, o_ref, buf, send_sems, recv_sems):
        x, y, c, _ = _place()
        me = 4 * x + 2 * y + c
        buf[me] = v_ref[...]
        copies = []
        for j, (fx, fy, fc) in enumerate(flips):
            peer = (1 - x if fx else x, 1 - y if fy else y, 1 - c if fc else c)
            copies.append(pltpu.make_async_remote_copy(
                src_ref=v_ref, dst_ref=buf.at[me], send_sem=send_sems.at[j], recv_sem=recv_sems.at[j],
                device_id=peer, device_id_type=MESH))
            copies[j].start()
        for cp in copies:
            cp.wait_recv()
        for cp in copies:
            cp.wait_send()
        acc = buf[0]
        for d in range(1, 8):
            acc = acc + buf[d]
        o_ref[...] = acc

    return pl.pallas_call(
        body, name="all_reduce_small",
        in_specs=[pl.BlockSpec(memory_space=pltpu.VMEM)], out_specs=pl.BlockSpec(memory_space=pltpu.VMEM),
        out_shape=jax.ShapeDtypeStruct((NR, LANE), F32),
        scratch_shapes=[pltpu.VMEM((8, NR, LANE), F32), pltpu.SemaphoreType.DMA((7,)),
                        pltpu.SemaphoreType.DMA((7,))],
    )(vec)


def _pack(arrays):
    flat = jnp.concatenate([a.reshape(-1).astype(F32) for a in arrays])
    n = flat.shape[0]
    npad = -(-n // (8 * LANE)) * (8 * LANE)
    return jnp.pad(flat, (0, npad - n)).reshape(npad // LANE, LANE)


def _unpack(buf, like):
    flat = buf.reshape(-1)
    out, off = [], 0
    for a in like:
        out.append(flat[off:off + a.size].reshape(a.shape))
        off += a.size
    return out


def kernel(x, ffn1_norm, ffn1_w_in, ffn1_w_out, mix_norm, ffn2_norm, ffn2_w_in, ffn2_w_out, a_w_qkv, a_rel_bias, a_w_o, kv_norm, kv_w_down, kv_latent_norm, kv_w_up, b_w_dq, b_q_norm, b_w_uq, b_w_o, final_norm, loss_target, m_ffn1_norm, m_ffn1_w_in, m_ffn1_w_out, m_mix_norm, m_ffn2_norm, m_ffn2_w_in, m_ffn2_w_out, m_a_w_qkv, m_a_rel_bias, m_a_w_o, m_kv_norm, m_kv_w_down, m_kv_latent_norm, m_kv_w_up, m_b_w_dq, m_b_q_norm, m_b_w_uq, m_b_w_o, m_final_norm, v_ffn1_norm, v_ffn1_w_in, v_ffn1_w_out, v_mix_norm, v_ffn2_norm, v_ffn2_w_in, v_ffn2_w_out, v_a_w_qkv, v_a_rel_bias, v_a_w_o, v_kv_norm, v_kv_w_down, v_kv_latent_norm, v_kv_w_up, v_b_w_dq, v_b_q_norm, v_b_w_uq, v_b_w_o, v_final_norm):
    B, S, D = x.shape
    T = B * S
    HB = D // 128
    QL = b_q_norm.shape[-1]
    KVL = kv_latent_norm.shape[0]
    hpc = HB // N_CHIPS
    tabs = rope_tables(S)
    idx = jnp.stack([2 * lax.axis_index("x") + lax.axis_index("y"), lax.axis_index("c")]).astype(I32)

    def halves(a):
        return a.reshape(*a.shape[:-2], 2, a.shape[-2] // 2, a.shape[-1])

    def whole(a):
        return a.reshape(*a.shape[:-3], 2 * a.shape[-2], a.shape[-1])

    kv_w_down_p = jnp.pad(kv_w_down, ((0, 0), (0, LANE - ROPE)))[None]
    b_w_uq_p = jnp.pad(b_w_uq.reshape(1, QL, hpc, NOPE + ROPE),
                       ((0, 0), (0, 0), (0, 0), (0, LANE - ROPE))).reshape(1, QL, hpc * 256)
    sharded = [("ffn1_w_in", ffn1_w_in), ("ffn1_w_out", ffn1_w_out), ("ffn2_w_in", ffn2_w_in),
               ("ffn2_w_out", ffn2_w_out), ("a_w_qkv", a_w_qkv), ("a_w_o", a_w_o),
               ("kv_w_down", kv_w_down_p), ("kv_w_up", kv_w_up[None]), ("b_w_dq", b_w_dq),
               ("b_w_uq", b_w_uq_p), ("b_w_o", b_w_o)]
    names = [nm for nm, _ in sharded]
    shard_of = dict(sharded)
    W = {}

    gather_groups = [
        [("ffn1_w_in", 0)],
        [("ffn1_w_out", 0)],
        [("a_w_qkv", 0), ("a_w_o", 0)],
        [("ffn2_w_in", 0), ("ffn2_w_out", 0), ("kv_w_down", 0), ("kv_w_up", 0)],
        [("ffn1_w_in", 1), ("ffn1_w_out", 1), ("b_w_dq", 0), ("b_w_uq", 0), ("b_w_o", 0), ("ffn2_w_in", 1),
         ("ffn2_w_out", 1)]]

    own = {}

    def cast(g, after=None):
        keys = gather_groups[g]
        own.update(zip(keys, cast_group(f"cast_group_{g}", [shard_of[nm] for nm, _ in keys], [l for _, l in keys],
                                        idx, after=after)))

    def gather_start(g, after):
        keys = gather_groups[g]
        ss, rs, bufs, _, token = ici_start(f"gather_start_{g}", [halves(own[k]) for k in keys], [], after, True)
        return (g, ss, rs, bufs), token

    def gather_finish(state, after):
        g, ss, rs, bufs = state
        bufs, _ = ici_wait(f"gather_wait_{g}", ss, rs, bufs, [], after, True)
        full = gather_pair_pass(f"gather_pair_{g}", bufs)
        for k, w in zip(gather_groups[g], full):
            W[k] = whole(w)
        return full[0]

    def tied(a, token):
        return a + token[0, 0]

    def col(nm, l=0):
        return W[(nm, l)]

    def row(nm, l=0):
        w = W[(nm, l)]
        return w.reshape(N_CHIPS * w.shape[1], w.shape[2])

    bias = rel_bias_tile("rel_bias_tile", a_rel_bias[0])

    def ffn_fwd(tag, h, g, w_in, w_out):
        xn = rms_fwd(f"{tag}_norm", h, g)
        u, act = ffn_in_act(f"{tag}_in", xn, w_in)
        return mm_roww(f"{tag}_out", act, w_out, F32, res=h, alpha=0.5), (xn, u, act)

    h0 = x.reshape(T, D)
    for g in range(3):
        cast(g)
    st0, tok0 = gather_start(0, h0)
    st1, tok1 = gather_start(1, tok0)
    st2, tok2 = gather_start(2, tok1)
    for g in range(3, len(gather_groups)):
        cast(g, tok2)
    xn0 = rms_fwd("l0f1_norm", h0, tied(ffn1_norm[0], tok2))
    gather_finish(st0, xn0)
    u0, act0 = ffn_in_act("l0f1_in", xn0, col("ffn1_w_in", 0))
    gather_finish(st1, u0)
    h1 = mm_roww("l0f1_out", act0, row("ffn1_w_out", 0), F32, res=h0, alpha=0.5)
    sv_f1a = (xn0, u0, act0)
    done2 = gather_finish(st2, h1)
    st3, tok3 = gather_start(3, done2)
    st4, tok4 = gather_start(4, tok3)
    hn_a = rms_fwd("l0mix_norm", h1, tied(mix_norm[0], tok4))
    qkv = mm_colw("l0_qkv", hn_a, col("a_w_qkv"), BF16).reshape(B, S, 3 * D)
    o_a = attn_a_fwd("l0_attn", qkv, bias).reshape(T, D)
    h2 = mm_roww("l0_attn_out", o_a, row("a_w_o"), F32, res=h1)
    gather_finish(st3, h2)
    h3, sv_f2a = ffn_fwd("l0f2", h2, ffn2_norm[0], col("ffn2_w_in", 0), row("ffn2_w_out", 0))

    hkv = rms_fwd("kv_norm", h3, kv_norm)
    ckr = mm_roww("kv_down", hkv, row("kv_w_down"), F32)
    ckv, kr = kvprep_fwd("kv_prep", ckr, kv_latent_norm, tabs, B, S)
    kvb = mm_colw("kv_up", ckv, col("kv_w_up"), BF16).reshape(B, S, HB * 256)
    gather_finish(st4, kvb)

    h4, sv_f1b = ffn_fwd("l1f1", h3, ffn1_norm[1], col("ffn1_w_in", 1), row("ffn1_w_out", 1))
    hn_b = rms_fwd("l1mix_norm", h4, mix_norm[1])
    cqp = mm_roww("l1_dq", hn_b, row("b_w_dq"), F32)
    cq = rms_fwd("l1_q_norm", cqp, b_q_norm[0])
    qf = uq_rope("l1_uq", cq, col("b_w_uq"), tabs, S).reshape(B, S, HB * 256)
    o_b, lse = mla_fwd("l1_attn", qf, kvb, kr)
    h5 = mm_roww("l1_attn_out", o_b.reshape(T, HB * LANE), row("b_w_o"), F32, res=h4)
    h6, sv_f2b = ffn_fwd("l1f2", h5, ffn2_norm[1], col("ffn2_w_in", 1), row("ffn2_w_out", 1))

    dh, g_final, loss_part = loss_head("loss_head", h6, final_norm, loss_target.reshape(T, D))

    gw = {}
    gbufs = {nm: lax.empty(halves(w).shape, F32) for nm, w in sharded}

    def reduce_start(r, keys, after):
        dws = [halves(gw[k]) for k in keys]
        landed = pair_exchange(f"grad_pair_exchange_{r}", dws)
        parts = half_sum_group(f"half_sum_{r}", dws, landed, idx)
        lands = [lax.empty((3, *p.shape[1:]), p.dtype) for p in parts]
        ss, rs, parts, lands, token = ici_start(f"reduce_start_{r}", parts, lands, after, False)
        return (r, keys, ss, rs, parts, lands), token

    def reduce_finish(state, after):
        r, keys, ss, rs, parts, lands = state
        parts, lands = ici_wait(f"reduce_wait_{r}", ss, rs, parts, lands, after, False)
        done = chip_sum_group(f"chip_sum_{r}", parts, lands, [gbufs[nm] for nm, _ in keys], [l for _, l in keys], idx)
        gbufs.update(zip([nm for nm, _ in keys], done))
        return done[0]

    def ffn_bwd(tag, dh, h_in, g, w_in, w_out, saved, key_in, key_out, after=None, then=None):
        xn, u, act = saved
        du = ffn_dact(f"{tag}_dact", dh, w_out, u, after=after)
        dwo = mm_droww(f"{tag}_dwout", act, dh, alpha=0.5)
        gw[key_out] = dwo.reshape(N_CHIPS, dwo.shape[0] // N_CHIPS, dwo.shape[1])
        gw[key_in] = mm_dcolw(f"{tag}_dwin", xn, du, pair_layout=True)
        token = then(du) if then is not None else None
        return dx_norm_bwd(f"{tag}_dxn", du, w_in, h_in, g, dres=dh, pair_layout=True, after=token)

    def chip_major(dw):
        return dw.reshape(N_CHIPS, dw.shape[0] // N_CHIPS, dw.shape[1])

    dh, g_f2b = ffn_bwd("l1f2b", dh, h5, ffn2_norm[1], col("ffn2_w_in", 1), row("ffn2_w_out", 1), sv_f2b,
                        ("ffn2_w_in", 1), ("ffn2_w_out", 1))
    red0, rtok0 = reduce_start(0, [("ffn2_w_in", 1), ("ffn2_w_out", 1)], dh)
    do_b = mm_roww_t("l1_attn_do", dh, row("b_w_o"), BF16, after=rtok0).reshape(B, S, HB * LANE)
    gw[("b_w_o", 0)] = chip_major(mm_droww("l1_attn_dwo", o_b.reshape(T, HB * LANE), dh))
    dqpre, dkv, dkr = mla_bwd("l1_attn_bwd", qf, kvb, kr, do_b, o_b, lse, tabs)
    dqpre = dqpre.reshape(T, HB * 256)
    gw[("b_w_uq", 0)] = mm_dcolw("l1_dwuq", cq, dqpre)
    dcqp, g_qn = dx_norm_bwd("l1_dcq", dqpre, col("b_w_uq"), cqp, b_q_norm[0])
    gw[("b_w_dq", 0)] = chip_major(mm_droww("l1_dwdq", hn_b, dcqp))
    dhn = mm_roww_t("l1_dhn", dcqp, row("b_w_dq"), F32)
    dh, g_mixb = rms_bwd("l1_dmix", h4, mix_norm[1], dhn, dres=dh)
    dh, g_f1b = ffn_bwd("l1f1b", dh, h3, ffn1_norm[1], col("ffn1_w_in", 1), row("ffn1_w_out", 1), sv_f1b,
                        ("ffn1_w_in", 1), ("ffn1_w_out", 1))
    fin0 = reduce_finish(red0, dh)
    red1, rtok1 = reduce_start(1, [("b_w_o", 0), ("b_w_uq", 0), ("b_w_dq", 0), ("ffn1_w_in", 1), ("ffn1_w_out", 1)], fin0)
    dkv2 = dkv.reshape(T, HB * 256)
    gw[("kv_w_up", 0)] = mm_dcolw("kv_dwup", ckv, dkv2, after=rtok1)
    dckv = mm_colw_t("kv_dckv", dkv2, col("kv_w_up"), F32, after=rtok1)
    dckr, g_lat = kvprep_bwd("kv_prep_bwd", ckr, kv_latent_norm, dckv, dkr, tabs, B, S)
    gw[("kv_w_down", 0)] = chip_major(mm_droww("kv_dwdown", hkv, dckr))
    dhkv = mm_roww_t("kv_dhkv", dckr, row("kv_w_down"), F32)
    dh, g_kvn = rms_bwd("kv_dnorm", h3, kv_norm, dhkv, dres=dh)
    dh, g_f2a = ffn_bwd("l0f2b", dh, h2, ffn2_norm[0], col("ffn2_w_in", 0), row("ffn2_w_out", 0), sv_f2a,
                        ("ffn2_w_in", 0), ("ffn2_w_out", 0))
    do_a = mm_roww_t("l0_attn_do", dh, row("a_w_o"), BF16).reshape(B, S, D)
    gw[("a_w_o", 0)] = chip_major(mm_droww("l0_attn_dwo", o_a, dh))
    dqkv, dbias = attn_a_bwd("l0_attn_bwd", qkv, do_a, bias)
    dqkv = dqkv.reshape(T, 3 * D)
    gw[("a_w_qkv", 0)] = mm_dcolw("l0_dwqkv", hn_a, dqkv)
    dh, g_mixa = dx_norm_bwd("l0_dhn", dqkv, col("a_w_qkv"), h1, mix_norm[0], dres=dh)
    fin1 = reduce_finish(red1, dh)
    red2, rtok2 = reduce_start(2, [("kv_w_up", 0), ("kv_w_down", 0), ("ffn2_w_in", 0), ("ffn2_w_out", 0),
                                   ("a_w_o", 0), ("a_w_qkv", 0)], fin1)
    last = {}

    def last_group(du):
        fin2 = reduce_finish(red2, gw[("ffn1_w_in", 0)])
        last["red"], token = reduce_start(3, [("ffn1_w_in", 0), ("ffn1_w_out", 0)], fin2)
        return token

    dh, g_f1a = ffn_bwd("l0f1b", dh, h0, ffn1_norm[0], col("ffn1_w_in", 0), row("ffn1_w_out", 0), sv_f1a,
                        ("ffn1_w_in", 0), ("ffn1_w_out", 0), after=rtok2, then=last_group)
    grad_x = dh.reshape(B, S, D)
    g_rel = rel_bias_grad("rel_bias_grad", dbias)[:, :2 * MAX_REL + 1][None]
    reduce_finish(last["red"], dh)

    full = [whole(g) for g in pair_assemble([gbufs[nm] for nm in names])]
    G = {nm: g for (nm, _), g in zip(sharded, full)}
    G["kv_w_down"] = G["kv_w_down"][0, :, :KVL + ROPE]
    G["kv_w_up"] = G["kv_w_up"][0]
    G["b_w_uq"] = G["b_w_uq"].reshape(1, QL, hpc, 256)[..., :NOPE + ROPE].reshape(b_w_uq.shape)

    small = [("ffn1_norm", jnp.stack([g_f1a, g_f1b])), ("mix_norm", jnp.stack([g_mixa, g_mixb])),
             ("ffn2_norm", jnp.stack([g_f2a, g_f2b])), ("a_rel_bias", g_rel), ("kv_norm", g_kvn),
             ("kv_latent_norm", g_lat), ("b_q_norm", g_qn[None]), ("final_norm", g_final)]
    red = all_reduce_small(_pack([loss_part] + [g for _, g in small]))
    unpacked = _unpack(red, [loss_part] + [g for _, g in small])
    loss = unpacked[0][0, 0]
    for (nm, _), g in zip(small, unpacked[1:]):
        G[nm] = g

    given = dict(ffn1_norm=(ffn1_norm, m_ffn1_norm, v_ffn1_norm), ffn1_w_in=(ffn1_w_in, m_ffn1_w_in, v_ffn1_w_in),
                 ffn1_w_out=(ffn1_w_out, m_ffn1_w_out, v_ffn1_w_out), mix_norm=(mix_norm, m_mix_norm, v_mix_norm),
                 ffn2_norm=(ffn2_norm, m_ffn2_norm, v_ffn2_norm), ffn2_w_in=(ffn2_w_in, m_ffn2_w_in, v_ffn2_w_in),
                 ffn2_w_out=(ffn2_w_out, m_ffn2_w_out, v_ffn2_w_out), a_w_qkv=(a_w_qkv, m_a_w_qkv, v_a_w_qkv),
                 a_rel_bias=(a_rel_bias, m_a_rel_bias, v_a_rel_bias), a_w_o=(a_w_o, m_a_w_o, v_a_w_o),
                 kv_norm=(kv_norm, m_kv_norm, v_kv_norm), kv_w_down=(kv_w_down, m_kv_w_down, v_kv_w_down),
                 kv_latent_norm=(kv_latent_norm, m_kv_latent_norm, v_kv_latent_norm),
                 kv_w_up=(kv_w_up, m_kv_w_up, v_kv_w_up), b_w_dq=(b_w_dq, m_b_w_dq, v_b_w_dq),
                 b_q_norm=(b_q_norm, m_b_q_norm, v_b_q_norm), b_w_uq=(b_w_uq, m_b_w_uq, v_b_w_uq),
                 b_w_o=(b_w_o, m_b_w_o, v_b_w_o), final_norm=(final_norm, m_final_norm, v_final_norm))
    order = list(given)
    delta, new_m, new_v = {}, {}, {}
    small_names = [nm for nm, _ in small]
    packed = [_pack([given[nm][k] for nm in small_names]) for k in range(3)]
    outs = adamw("adamw_small", packed[0], _pack([G[nm] for nm in small_names]), packed[1], packed[2])
    for dst, buf in zip((delta, new_m, new_v), outs):
        for nm, a in zip(small_names, _unpack(buf, [given[nm][0] for nm in small_names])):
            dst[nm] = a
    for nm, _ in sharded:
        w, m, v = given[nm]
        g = G[nm].reshape(w.shape)
        G[nm] = g
        two = lambda a: a.reshape(-1, a.shape[-1])
        d_, m_, v_ = adamw(f"adamw_{nm}", two(w), two(g), two(m), two(v))
        delta[nm], new_m[nm], new_v[nm] = d_.reshape(w.shape), m_.reshape(w.shape), v_.reshape(w.shape)

    return (loss, grad_x, *[G[n] for n in order], *[delta[n] for n in order],
            *[new_m[n] for n in order], *[new_v[n] for n in order])
```

```python
import math

import jax
import jax.numpy as jnp
from jax import lax
from jax.experimental import pallas as pl
from jax.experimental.pallas import tpu as pltpu

F32 = jnp.float32
BF16 = jnp.bfloat16
I32 = jnp.int32

CHUNK = 64
CHUNK_SHIFT = 6
HEAD_DIM_A = 64
LEFT_CHUNKS = 8
MAX_REL = 128
REL_PAD = 384
QROWS = 2 * CHUNK
WIN = (LEFT_CHUNKS + 2) * CHUNK
PADR = LEFT_CHUNKS * CHUNK
NOPE = 128
ROPE = 64
EPS = 1e-6
NEG_INF = -1e30
ROPE_THETA = 10000.0
ADAM_LR, ADAM_B1, ADAM_B2, ADAM_EPS, ADAM_WD, ADAM_STEP = 0.001, 0.9, 0.999, 1e-08, 0.01, 10
N_CHIPS = 4
LANE = 128
MESH = pl.DeviceIdType.MESH
VMEM_CAP_MB = 60
VMEM_FLOOR_MB = 48

NN = (((1,), (0,)), ((), ()))
NT = (((1,), (1,)), ((), ()))
TN = (((0,), (0,)), ((), ()))


def _tile(n, pref, mult):
    t = (min(pref, n) // mult) * mult
    while t >= mult:
        if n % t == 0:
            return t
        t -= mult
    return n


def _nbytes(shape, dtype):
    return math.prod(shape) * jnp.dtype(dtype).itemsize


def _params(block_bytes, extra_bytes=0):
    need = 2 * block_bytes + extra_bytes
    mb = min(VMEM_CAP_MB, max(VMEM_FLOOR_MB, int(need * 1.25 / 2**20) + 8))
    return pltpu.CompilerParams(vmem_limit_bytes=mb * 2**20)


def _mm(name, kind, a, b, grid, a_spec, b_spec, o_spec, out_shape, out_dtype, blocks,
        red_axis=None, nred=1, alpha=1.0, res=None, res_spec=None, after=None):
    dims = {"nn": NN, "nt": NT, "tn": TN}[kind]
    has_res = res is not None
    acc_in_out = nred > 1 and out_dtype == F32 and not has_res and alpha == 1.0
    n_in = 2 + has_res + (after is not None)

    def body(*refs):
        a_ref, b_ref = refs[0], refs[1]
        r_ref = refs[2] if has_res else None
        o_ref = refs[n_in]
        p = lax.dot_general(a_ref[...].astype(BF16), b_ref[...].astype(BF16), dims,
                            preferred_element_type=F32)

        def finish(acc):
            y = acc if alpha == 1.0 else acc * alpha
            if has_res:
                y = r_ref[...] + y
            o_ref[...] = y.astype(o_ref.dtype)

        if nred == 1:
            finish(p)
            return
        k = pl.program_id(red_axis)
        tgt = o_ref if acc_in_out else refs[-1]

        @pl.when(k == 0)
        def _():
            tgt[...] = p

        @pl.when(k > 0)
        def _():
            tgt[...] += p

        if not acc_in_out:
            @pl.when(k == nred - 1)
            def _():
                finish(tgt[...])

    a_blk, b_blk, o_blk = blocks
    scratch = []
    extra = 0
    if nred > 1 and not acc_in_out:
        scratch = [pltpu.VMEM(o_blk, F32)]
        extra = _nbytes(o_blk, F32)
    blk = _nbytes(a_blk, a.dtype) + _nbytes(b_blk, b.dtype) + _nbytes(o_blk, out_dtype)
    ins, specs = [a, b], [a_spec, b_spec]
    if has_res:
        ins.append(res)
        specs.append(res_spec)
        blk += _nbytes(o_blk, res.dtype)
    if after is not None:
        ins.append(after)
        specs.append(pl.BlockSpec(memory_space=pl.ANY))
    extra += _nbytes(a_blk, BF16) + _nbytes(b_blk, BF16) + 2 * _nbytes(o_blk, F32)
    return pl.pallas_call(
        body, name=name, grid=grid, in_specs=specs, out_specs=o_spec,
        out_shape=jax.ShapeDtypeStruct(out_shape, out_dtype), scratch_shapes=scratch,
        compiler_params=_params(blk, extra),
    )(*ins)


def mm_colw(name, x, w3, out_dtype):
    T, K = x.shape
    _, _, nl = w3.shape
    tm = _tile(T, 512, 8)
    return _mm(name, "nn", x, w3, (N_CHIPS, T // tm),
               pl.BlockSpec((tm, K), lambda j, i: (i, 0)),
               pl.BlockSpec((None, K, nl), lambda j, i: (j, 0, 0)),
               pl.BlockSpec((tm, nl), lambda j, i: (i, j)),
               (T, N_CHIPS * nl), out_dtype, ((tm, K), (K, nl), (tm, nl)))


def _pair_chip(j):
    return (j % 2) * 2 + j // 2


def mm_colw_t(name, dy, w3, out_dtype, res=None, after=None, pair_layout=False):
    T = dy.shape[0]
    _, K, nl = w3.shape
    tm = _tile(T, 1024, 8)
    chip = _pair_chip if pair_layout else (lambda j: j)
    return _mm(name, "nt", dy, w3, (T // tm, N_CHIPS),
               pl.BlockSpec((tm, nl), lambda i, j: (i, j)),
               pl.BlockSpec((None, K, nl), lambda i, j: (chip(j), 0, 0)),
               pl.BlockSpec((tm, K), lambda i, j: (i, 0)),
               (T, K), out_dtype, ((tm, nl), (K, nl), (tm, K)),
               red_axis=1, nred=N_CHIPS, res=res,
               res_spec=pl.BlockSpec((tm, K), lambda i, j: (i, 0)), after=after)


def mm_dcolw(name, x, dy, after=None, pair_layout=False):
    T, K = x.shape
    nl = dy.shape[1] // N_CHIPS
    tt = _tile(T, 2048, 8)
    chip = _pair_chip if pair_layout else (lambda j: j)
    return _mm(name, "tn", x, dy, (N_CHIPS, T // tt),
               pl.BlockSpec((tt, K), lambda j, t: (t, 0)),
               pl.BlockSpec((tt, nl), lambda j, t: (t, j)),
               pl.BlockSpec((None, K, nl), lambda j, t: (chip(j), 0, 0)),
               (N_CHIPS, K, nl), BF16, ((tt, K), (tt, nl), (K, nl)),
               red_axis=1, nred=T // tt, after=after)


def mm_roww(name, x, w2, out_dtype, res=None, alpha=1.0):
    T, Kt = x.shape
    N = w2.shape[1]
    tm = _tile(T, 512, 8)
    return _mm(name, "nn", x, w2, (T // tm,),
               pl.BlockSpec((tm, Kt), lambda i: (i, 0)),
               pl.BlockSpec((Kt, N), lambda i: (0, 0)),
               pl.BlockSpec((tm, N), lambda i: (i, 0)),
               (T, N), out_dtype, ((tm, Kt), (Kt, N), (tm, N)),
               alpha=alpha, res=res, res_spec=pl.BlockSpec((tm, N), lambda i: (i, 0)))


def mm_roww_t(name, dy, w2, out_dtype, alpha=1.0, after=None):
    T, N = dy.shape
    Kt = w2.shape[0]
    tm = _tile(T, 512, 8)
    tk = _tile(Kt, 1408, LANE)
    return _mm(name, "nt", dy, w2, (Kt // tk, T // tm),
               pl.BlockSpec((tm, N), lambda j, i: (i, 0)),
               pl.BlockSpec((tk, N), lambda j, i: (j, 0)),
               pl.BlockSpec((tm, tk), lambda j, i: (i, j)),
               (T, Kt), out_dtype, ((tm, N), (tk, N), (tm, tk)), alpha=alpha, after=after)


def mm_droww(name, x, dy, alpha=1.0):
    T, Kt = x.shape
    N = dy.shape[1]
    tt = _tile(T, 2048, 8)
    tk = _tile(Kt, 1408, LANE)
    return _mm(name, "tn", x, dy, (Kt // tk, T // tt),
               pl.BlockSpec((tt, tk), lambda j, t: (t, j)),
               pl.BlockSpec((tt, N), lambda j, t: (t, 0)),
               pl.BlockSpec((tk, N), lambda j, t: (j, 0)),
               (Kt, N), BF16, ((tt, tk), (tt, N), (tk, N)),
               red_axis=1, nred=T // tt, alpha=alpha)


def rms_fwd(name, x, g):
    T, D = x.shape
    tm = _tile(T, 512, 8)

    def body(x_ref, g_ref, o_ref):
        xv = x_ref[...]
        r = lax.rsqrt(jnp.mean(xv * xv, axis=-1, keepdims=True) + EPS)
        o_ref[...] = (xv * r * g_ref[...]).astype(o_ref.dtype)

    return pl.pallas_call(
        body, name=name, grid=(T // tm,),
        in_specs=[pl.BlockSpec((tm, D), lambda i: (i, 0)), pl.BlockSpec((1, D), lambda i: (0, 0))],
        out_specs=pl.BlockSpec((tm, D), lambda i: (i, 0)),
        out_shape=jax.ShapeDtypeStruct((T, D), BF16),
        compiler_params=_params(_nbytes((tm, D), F32) * 2, 4 * _nbytes((tm, D), F32)),
    )(x, g.reshape(1, D))


def _rms_bwd_math(xv, gv, dy):
    r = lax.rsqrt(jnp.mean(xv * xv, axis=-1, keepdims=True) + EPS)
    xh = xv * r
    dyg = dy * gv
    dx = r * (dyg - xh * jnp.mean(dyg * xh, axis=-1, keepdims=True))
    dg = jnp.sum(dy * xh, axis=0, keepdims=True)
    return dx, dg


def rms_bwd(name, x, g, dy, dres=None):
    T, D = x.shape
    tm = _tile(T, 256, 8)
    has_res = dres is not None

    def body(*refs):
        x_ref, g_ref, dy_ref = refs[:3]
        r_ref = refs[3] if has_res else None
        dx_ref, dg_ref = refs[-2:]
        dx, dg = _rms_bwd_math(x_ref[...], g_ref[...], dy_ref[...].astype(F32))
        if has_res:
            dx = r_ref[...] + dx
        dx_ref[...] = dx

        @pl.when(pl.program_id(0) == 0)
        def _():
            dg_ref[...] = dg

        @pl.when(pl.program_id(0) > 0)
        def _():
            dg_ref[...] += dg

    row = pl.BlockSpec((tm, D), lambda i: (i, 0))
    vec = pl.BlockSpec((1, D), lambda i: (0, 0))
    ins, specs = [x, g.reshape(1, D), dy], [row, vec, row]
    if has_res:
        ins.append(dres)
        specs.append(row)
    dx, dg = pl.pallas_call(
        body, name=name, grid=(T // tm,), in_specs=specs, out_specs=[row, vec],
        out_shape=[jax.ShapeDtypeStruct((T, D), F32), jax.ShapeDtypeStruct((1, D), F32)],
        compiler_params=_params(_nbytes((tm, D), F32) * 4, 6 * _nbytes((tm, D), F32)),
    )(*ins)
    return dx, dg.reshape(D)


def dx_norm_bwd(name, dy, w3, x, g, dres=None, pair_layout=False, after=None):
    T = dy.shape[0]
    _, K, nl = w3.shape
    tm = _tile(T, 512, 8)
    chip = _pair_chip if pair_layout else (lambda j: j)
    has_res = dres is not None

    def body(*refs):
        dy_ref, w_ref, x_ref, g_ref = refs[:4]
        r_ref = refs[4] if has_res else None
        dx_ref, dg_ref, acc = refs[-3:]
        i, k = pl.program_id(0), pl.program_id(1)
        p = lax.dot_general(dy_ref[...].astype(BF16), w_ref[...], NT, preferred_element_type=F32)

        @pl.when(k == 0)
        def _():
            acc[...] = p

        @pl.when(k > 0)
        def _():
            acc[...] += p

        @pl.when(k == N_CHIPS - 1)
        def _():
            dx, dg = _rms_bwd_math(x_ref[...], g_ref[...], acc[...])
            dx_ref[...] = r_ref[...] + dx if has_res else dx

            @pl.when(i == 0)
            def _():
                dg_ref[...] = dg

            @pl.when(i > 0)
            def _():
                dg_ref[...] += dg

    row = pl.BlockSpec((tm, K), lambda i, j: (i, 0))
    vec = pl.BlockSpec((1, K), lambda i, j: (0, 0))
    ins = [dy, w3, x, g.reshape(1, K)]
    specs = [pl.BlockSpec((tm, nl), lambda i, j: (i, j)),
             pl.BlockSpec((None, K, nl), lambda i, j: (chip(j), 0, 0)), row, vec]
    if has_res:
        ins.append(dres)
        specs.append(row)
    if after is not None:
        ins.append(after)
        specs.append(pl.BlockSpec(memory_space=pl.ANY))
    blk = _nbytes((tm, nl), dy.dtype) + _nbytes((K, nl), BF16) + (2 + has_res) * _nbytes((tm, K), F32)
    dx, dg = pl.pallas_call(
        body, name=name, grid=(T // tm, N_CHIPS), in_specs=specs, out_specs=[row, vec],
        out_shape=[jax.ShapeDtypeStruct((T, K), F32), jax.ShapeDtypeStruct((1, K), F32)],
        scratch_shapes=[pltpu.VMEM((tm, K), F32)],
        compiler_params=_params(blk, 8 * _nbytes((tm, K), F32)),
    )(*ins)
    return dx, dg.reshape(K)


def ffn_in_act(name, x, w3):
    T, K = x.shape
    _, _, nl = w3.shape
    tm = _tile(T, 1024, 8)

    def body(*refs):
        x_ref, wg_ref, wu_ref = refs[:3]
        u_ref, a_ref = refs[-2:]
        xv = x_ref[...]
        g = jnp.dot(xv, wg_ref[...], preferred_element_type=F32)
        up = jnp.dot(xv, wu_ref[...], preferred_element_type=F32)
        u_ref[:, :nl] = g.astype(u_ref.dtype)
        u_ref[:, nl:] = up.astype(u_ref.dtype)
        a_ref[...] = (g * jax.nn.sigmoid(g) * up).astype(a_ref.dtype)

    blk = _nbytes((tm, K), BF16) + 2 * _nbytes((K, nl), BF16) + _nbytes((tm, 3 * nl), BF16)
    return pl.pallas_call(
        body, name=name, grid=(2, T // tm),
        in_specs=[pl.BlockSpec((tm, K), lambda p, i: (i, 0)),
                  pl.BlockSpec((None, K, nl), lambda p, i: (p, 0, 0)),
                  pl.BlockSpec((None, K, nl), lambda p, i: (p + 2, 0, 0))],
        out_specs=[pl.BlockSpec((tm, 2 * nl), lambda p, i: (i, p)), pl.BlockSpec((tm, nl), lambda p, i: (i, p))],
        out_shape=[jax.ShapeDtypeStruct((T, 4 * nl), BF16), jax.ShapeDtypeStruct((T, 2 * nl), BF16)],
        compiler_params=_params(blk, 4 * _nbytes((tm, nl), F32)),
    )(x, w3, w3)


def ffn_dact(name, dh, w_out, u, after=None):
    T, N = dh.shape
    F = w_out.shape[0]
    nl = F // 2
    tm = _tile(T, 512, 8)

    def body(*refs):
        d_ref, w_ref, u_ref = refs[:3]
        o_ref = refs[-1]
        dact = 0.5 * lax.dot_general(d_ref[...].astype(BF16), w_ref[...], NT, preferred_element_type=F32)
        g = u_ref[:, :nl].astype(F32)
        up = u_ref[:, nl:].astype(F32)
        sig = jax.nn.sigmoid(g)
        o_ref[:, :nl] = (dact * up * (sig * (1.0 + g * (1.0 - sig)))).astype(o_ref.dtype)
        o_ref[:, nl:] = (dact * (g * sig)).astype(o_ref.dtype)

    ins = [dh, w_out, u]
    specs = [pl.BlockSpec((tm, N), lambda p, i: (i, 0)), pl.BlockSpec((nl, N), lambda p, i: (p, 0)),
             pl.BlockSpec((tm, 2 * nl), lambda p, i: (i, p))]
    if after is not None:
        ins.append(after)
        specs.append(pl.BlockSpec(memory_space=pl.ANY))
    blk = _nbytes((tm, N), F32) + _nbytes((nl, N), BF16) + 2 * _nbytes((tm, 2 * nl), BF16)
    return pl.pallas_call(
        body, name=name, grid=(2, T // tm), in_specs=specs,
        out_specs=pl.BlockSpec((tm, 2 * nl), lambda p, i: (i, p)),
        out_shape=jax.ShapeDtypeStruct((T, 2 * F), BF16),
        compiler_params=_params(blk, 6 * _nbytes((tm, nl), F32)),
    )(*ins)


def loss_head(name, h, g, target):
    T, D = h.shape
    tm = _tile(T, 256, 8)

    def body(h_ref, g_ref, t_ref, dh_ref, dg_ref, loss_ref):
        xv = h_ref[...]
        gv = g_ref[...]
        r = lax.rsqrt(jnp.mean(xv * xv, axis=-1, keepdims=True) + EPS)
        err = xv * r * gv - t_ref[...]
        part = 0.5 * jnp.sum(jnp.mean(err * err, axis=-1, keepdims=True), axis=0, keepdims=True)
        dx, dg = _rms_bwd_math(xv, gv, err * (1.0 / D))
        dh_ref[...] = dx
        part = jnp.broadcast_to(part, (1, LANE))

        @pl.when(pl.program_id(0) == 0)
        def _():
            dg_ref[...] = dg
            loss_ref[...] = part

        @pl.when(pl.program_id(0) > 0)
        def _():
            dg_ref[...] += dg
            loss_ref[...] += part

    row = pl.BlockSpec((tm, D), lambda i: (i, 0))
    vec = pl.BlockSpec((1, D), lambda i: (0, 0))
    dh, dg, loss = pl.pallas_call(
        body, name=name, grid=(T // tm,), in_specs=[row, vec, row],
        out_specs=[row, vec, pl.BlockSpec((1, LANE), lambda i: (0, 0))],
        out_shape=[jax.ShapeDtypeStruct((T, D), F32), jax.ShapeDtypeStruct((1, D), F32),
                   jax.ShapeDtypeStruct((1, LANE), F32)],
        compiler_params=_params(_nbytes((tm, D), F32) * 3, 6 * _nbytes((tm, D), F32)),
    )(h, g.reshape(1, D), target)
    return dh, dg.reshape(D), loss


def rope_tables(S):
    half = ROPE // 2
    freqs = ROPE_THETA ** (-jnp.arange(half, dtype=F32) / half)
    ang = jnp.arange(S, dtype=F32)[:, None] * freqs[None, :]
    cos, sin = jnp.cos(ang), jnp.sin(ang)
    z = jnp.zeros_like(cos)
    ct = jnp.concatenate([cos, cos, z, z], axis=1)
    s1 = jnp.concatenate([-sin, z, z, z], axis=1)
    s2 = jnp.concatenate([z, sin, z, z], axis=1)
    return ct, s1, s2


def _rope_tile(t, ct, s1, s2):
    return t * ct + pltpu.roll(t, 96, 1) * s1 + pltpu.roll(t, 32, 1) * s2


def _rope_tile_bwd(d, ct, s1, s2):
    return d * ct + pltpu.roll(d * s1, 32, 1) + pltpu.roll(d * s2, 96, 1)


def uq_rope(name, x, w3, tabs, S):
    T, K = x.shape
    _, _, nl = w3.shape
    tm = _tile(S, 512, 8)
    nt = S // tm

    def body(x_ref, w_ref, ct_ref, s1_ref, s2_ref, o_ref):
        q = jnp.dot(x_ref[...], w_ref[...], preferred_element_type=F32)
        ct, s1, s2 = ct_ref[...], s1_ref[...], s2_ref[...]
        for h in range(nl // 256):
            o_ref[:, 256 * h:256 * h + 128] = q[:, 256 * h:256 * h + 128].astype(o_ref.dtype)
            o_ref[:, 256 * h + 128:256 * h + 256] = _rope_tile(q[:, 256 * h + 128:256 * h + 256],
                                                               ct, s1, s2).astype(o_ref.dtype)

    tab = pl.BlockSpec((tm, LANE), lambda j, i: (i % nt, 0))
    blk = _nbytes((tm, K), BF16) + _nbytes((K, nl), BF16) + _nbytes((tm, nl), BF16) + 3 * _nbytes((tm, LANE), F32)
    return pl.pallas_call(
        body, name=name, grid=(N_CHIPS, T // tm),
        in_specs=[pl.BlockSpec((tm, K), lambda j, i: (i, 0)), pl.BlockSpec((None, K, nl), lambda j, i: (j, 0, 0)),
                  tab, tab, tab],
        out_specs=pl.BlockSpec((tm, nl), lambda j, i: (i, j)),
        out_shape=jax.ShapeDtypeStruct((T, N_CHIPS * nl), BF16),
        compiler_params=_params(blk, 4 * _nbytes((tm, nl), F32)),
    )(x, w3, *tabs)


def kvprep_fwd(name, ckr, g, tabs, B, S):
    T, W = ckr.shape
    KVL = W - LANE
    ts = _tile(S, 256, 8)

    def body(x_ref, g_ref, ct_ref, s1_ref, s2_ref, c_ref, k_ref):
        xv = x_ref[0, :, :KVL]
        r = lax.rsqrt(jnp.mean(xv * xv, axis=-1, keepdims=True) + EPS)
        c_ref[0] = (xv * r * g_ref[...]).astype(c_ref.dtype)
        k_ref[0] = _rope_tile(x_ref[0, :, KVL:], ct_ref[...], s1_ref[...], s2_ref[...]).astype(k_ref.dtype)

    tab = pl.BlockSpec((ts, LANE), lambda b, s: (s, 0))
    c, k = pl.pallas_call(
        body, name=name, grid=(B, S // ts),
        in_specs=[pl.BlockSpec((1, ts, W), lambda b, s: (b, s, 0)), pl.BlockSpec((1, KVL), lambda b, s: (0, 0)),
                  tab, tab, tab],
        out_specs=[pl.BlockSpec((1, ts, KVL), lambda b, s: (b, s, 0)),
                   pl.BlockSpec((1, ts, LANE), lambda b, s: (b, s, 0))],
        out_shape=[jax.ShapeDtypeStruct((B, S, KVL), BF16), jax.ShapeDtypeStruct((B, S, LANE), BF16)],
        compiler_params=_params(_nbytes((ts, W), F32) * 2, _nbytes((ts, W), F32) * 2),
    )(ckr.reshape(B, S, W), g.reshape(1, KVL), *tabs)
    return c.reshape(T, KVL), k


def kvprep_bwd(name, ckr, g, dc, dkr, tabs, B, S):
    T, W = ckr.shape
    KVL = W - LANE
    ts = _tile(S, 256, 8)

    def body(x_ref, g_ref, dc_ref, dk_ref, ct_ref, s1_ref, s2_ref, o_ref, dg_ref):
        dx, dg = _rms_bwd_math(x_ref[0, :, :KVL], g_ref[...], dc_ref[0])
        o_ref[0, :, :KVL] = dx
        o_ref[0, :, KVL:] = _rope_tile_bwd(dk_ref[0], ct_ref[...], s1_ref[...], s2_ref[...])
        first = (pl.program_id(0) == 0) & (pl.program_id(1) == 0)

        @pl.when(first)
        def _():
            dg_ref[...] = dg

        @pl.when(jnp.logical_not(first))
        def _():
            dg_ref[...] += dg

    tab = pl.BlockSpec((ts, LANE), lambda b, s: (s, 0))
    vec = pl.BlockSpec((1, KVL), lambda b, s: (0, 0))
    o, dg = pl.pallas_call(
        body, name=name, grid=(B, S // ts),
        in_specs=[pl.BlockSpec((1, ts, W), lambda b, s: (b, s, 0)), vec,
                  pl.BlockSpec((1, ts, KVL), lambda b, s: (b, s, 0)),
                  pl.BlockSpec((1, ts, LANE), lambda b, s: (b, s, 0)), tab, tab, tab],
        out_specs=[pl.BlockSpec((1, ts, W), lambda b, s: (b, s, 0)), vec],
        out_shape=[jax.ShapeDtypeStruct((B, S, W), F32), jax.ShapeDtypeStruct((1, KVL), F32)],
        compiler_params=_params(_nbytes((ts, W), F32) * 4, _nbytes((ts, W), F32) * 4),
    )(ckr.reshape(B, S, W), g.reshape(1, KVL), dc.reshape(B, S, KVL), dkr, *tabs)
    return o.reshape(T, W), dg.reshape(KVL)


DIAGS = 768


def _diag_onehot():
    col = lax.broadcasted_iota(I32, (REL_PAD, DIAGS), 1)
    row = lax.broadcasted_iota(I32, (REL_PAD, DIAGS), 0)
    idx = jnp.clip(PADR + QROWS - 1 - col, -MAX_REL, MAX_REL) + MAX_REL
    return (row == idx).astype(F32)


def rel_bias_tile(name, table):
    H = table.shape[0]
    tpad = jnp.pad(table, ((0, 0), (0, REL_PAD - table.shape[1])))

    def body(t_ref, o_ref):
        g = lax.dot_general(t_ref[...], _diag_onehot(), NN, precision=lax.Precision.HIGHEST,
                            preferred_element_type=F32)
        qc = jnp.right_shift(lax.broadcasted_iota(I32, (QROWS, WIN), 0), CHUNK_SHIFT)
        kc = jnp.right_shift(lax.broadcasted_iota(I32, (QROWS, WIN), 1), CHUNK_SHIFT)
        band = (kc >= qc) & (kc <= qc + LEFT_CHUNKS)
        for h in range(H):
            gb = jnp.broadcast_to(g[h:h + 1, :], (QROWS, DIAGS))
            tile = pltpu.roll(gb, DIAGS - (QROWS - 1), 1, stride=1, stride_axis=0)
            o_ref[h // 2, (h % 2) * QROWS:(h % 2 + 1) * QROWS, :] = jnp.where(band, tile[:, :WIN], NEG_INF)

    return pl.pallas_call(
        body, name=name, out_shape=jax.ShapeDtypeStruct((H // 2, 2 * QROWS, WIN), F32),
        compiler_params=_params(0, 2 * _nbytes((H // 2, 2 * QROWS, WIN), F32)),
    )(tpad)


def rel_bias_grad(name, dbias):
    H = 2 * dbias.shape[0]

    def body(d_ref, o_ref):
        flip = (lax.broadcasted_iota(I32, (QROWS, QROWS), 0) + lax.broadcasted_iota(I32, (QROWS, QROWS), 1)
                == QROWS - 1).astype(F32)
        rows = []
        for h in range(H):
            x = d_ref[h // 2, (h % 2) * QROWS:(h % 2 + 1) * QROWS, :]
            xr = lax.dot_general(flip, x, NN, precision=lax.Precision.HIGHEST, preferred_element_type=F32)
            xp = jnp.concatenate([xr, jnp.zeros((QROWS, DIAGS - WIN), F32)], axis=1)
            y = pltpu.roll(xp, 0, 1, stride=1, stride_axis=0)
            rows.append(jnp.sum(y, axis=0, keepdims=True))
        o_ref[...] = lax.dot_general(jnp.concatenate(rows, axis=0), _diag_onehot(), NT,
                                     precision=lax.Precision.HIGHEST, preferred_element_type=F32)

    return pl.pallas_call(
        body, name=name, out_shape=jax.ShapeDtypeStruct((H, REL_PAD), F32),
        compiler_params=_params(0, 2 * _nbytes(dbias.shape, F32)),
    )(dbias)


def _stack_pair(xp):
    lane = lax.broadcasted_iota(I32, xp.shape, 1)
    z = jnp.zeros_like(xp)
    return jnp.concatenate([jnp.where(lane < HEAD_DIM_A, xp, z), jnp.where(lane >= HEAD_DIM_A, xp, z)], axis=0)


def _unstack_pair(y):
    lane = lax.broadcasted_iota(I32, (QROWS, LANE), 1)
    return jnp.where(lane < HEAD_DIM_A, y[:QROWS], y[QROWS:])


def _attn_a_rowpen(j):
    w = lax.broadcasted_iota(I32, (1, WIN), 1)
    return jnp.where(w >= PADR - QROWS * j, 0.0, NEG_INF).astype(F32)


def _attn_a_load_bias(bias_hbm, bias_v, sem):
    cp = pltpu.make_async_copy(bias_hbm, bias_v, sem)
    cp.start()
    cp.wait()


def _attn_a_load_kv(qkv_hbm, b, kpad, vpad, sem, S, D):
    kpad[0:PADR, :] = jnp.zeros((PADR, D), BF16)
    vpad[0:PADR, :] = jnp.zeros((PADR, D), BF16)
    ck = pltpu.make_async_copy(qkv_hbm.at[b, :, pl.ds(D, D)], kpad.at[pl.ds(PADR, S), :], sem.at[0])
    cv = pltpu.make_async_copy(qkv_hbm.at[b, :, pl.ds(2 * D, D)], vpad.at[pl.ds(PADR, S), :], sem.at[1])
    ck.start()
    cv.start()
    ck.wait()
    cv.wait()


def _attn_a_exp(q2s, kp, bias, pen):
    s = lax.dot_general(q2s, kp, NT, preferred_element_type=F32) + bias + pen
    e = jnp.exp(s - jnp.max(s, axis=-1, keepdims=True))
    return e, 1.0 / jnp.sum(e, axis=-1, keepdims=True)


def attn_a_fwd(name, qkv, bias):
    B, S, D3 = qkv.shape
    D = D3 // 3
    H = D // HEAD_DIM_A
    nb = S // QROWS
    scale = HEAD_DIM_A ** -0.5

    def body(q_ref, bias_hbm, qkv_hbm, o_ref, kpad, vpad, bias_v, sem):
        b, j = pl.program_id(0), pl.program_id(1)

        @pl.when((b == 0) & (j == 0))
        def _():
            _attn_a_load_bias(bias_hbm, bias_v, sem.at[2])

        @pl.when(j == 0)
        def _():
            _attn_a_load_kv(qkv_hbm, b, kpad, vpad, sem, S, D)

        pen = _attn_a_rowpen(j)
        w0 = pl.multiple_of(j * QROWS, QROWS)
        for p in range(H // 2):
            ls = slice(p * LANE, (p + 1) * LANE)
            e, rl = _attn_a_exp(_stack_pair(q_ref[0, :, ls] * scale), kpad[pl.ds(w0, WIN), ls], bias_v[p], pen)
            o2 = jnp.dot(e.astype(BF16), vpad[pl.ds(w0, WIN), ls], preferred_element_type=F32) * rl
            o_ref[0, :, ls] = _unstack_pair(o2).astype(o_ref.dtype)

    scr = 2 * _nbytes((PADR + S, D), BF16) + _nbytes(bias.shape, F32) + 8 * _nbytes((2 * QROWS, WIN), F32)
    return pl.pallas_call(
        body, name=name, grid=(B, nb),
        in_specs=[pl.BlockSpec((1, QROWS, D), lambda b, j: (b, j, 0)),
                  pl.BlockSpec(memory_space=pl.ANY), pl.BlockSpec(memory_space=pl.ANY)],
        out_specs=pl.BlockSpec((1, QROWS, D), lambda b, j: (b, j, 0)),
        out_shape=jax.ShapeDtypeStruct((B, S, D), BF16),
        scratch_shapes=[pltpu.VMEM((PADR + S, D), BF16), pltpu.VMEM((PADR + S, D), BF16),
                        pltpu.VMEM(bias.shape, F32), pltpu.SemaphoreType.DMA((3,))],
        compiler_params=_params(2 * _nbytes((QROWS, D), BF16), scr),
    )(qkv, bias, qkv)


def attn_a_bwd(name, qkv, do, bias):
    B, S, D3 = qkv.shape
    D = D3 // 3
    H = D // HEAD_DIM_A
    nb = S // QROWS
    scale = HEAD_DIM_A ** -0.5

    def body(q_ref, do_ref, bias_hbm, qkv_hbm, dqkv_hbm, dbias_hbm, kpad, vpad, dkacc, dvacc, bias_v, dbias_v,
             dq_stage, sem):
        b, j = pl.program_id(0), pl.program_id(1)
        step = b * nb + j
        slot = lax.rem(step, 2)

        def dq_out(s):
            return pltpu.make_async_copy(dq_stage.at[s], dqkv_hbm.at[b, pl.ds(j * QROWS, QROWS), pl.ds(0, D)],
                                         sem.at[3 + s])

        @pl.when(step >= 2)
        def _():
            dq_out(slot).wait()

        @pl.when((b == 0) & (j == 0))
        def _():
            _attn_a_load_bias(bias_hbm, bias_v, sem.at[2])
            dbias_v[...] = jnp.zeros_like(dbias_v)

        @pl.when(j == 0)
        def _():
            _attn_a_load_kv(qkv_hbm, b, kpad, vpad, sem, S, D)
            dkacc[...] = jnp.zeros_like(dkacc)
            dvacc[...] = jnp.zeros_like(dvacc)

        pen = _attn_a_rowpen(j)
        w0 = pl.multiple_of(j * QROWS, QROWS)
        for p in range(H // 2):
            ls = slice(p * LANE, (p + 1) * LANE)
            q2s = _stack_pair(q_ref[0, :, ls] * scale)
            do2 = _stack_pair(do_ref[0, :, ls])
            kp = kpad[pl.ds(w0, WIN), ls]
            vp = vpad[pl.ds(w0, WIN), ls]
            e, rl = _attn_a_exp(q2s, kp, bias_v[p], pen)
            pr = e * rl
            dp = lax.dot_general(do2, vp, NT, preferred_element_type=F32)
            ds = pr * (dp - jnp.sum(pr * dp, axis=-1, keepdims=True))
            dbias_v[p] += ds
            dsb = ds.astype(BF16)
            dq_stage[slot, :, ls] = (_unstack_pair(jnp.dot(dsb, kp, preferred_element_type=F32))
                                     * scale).astype(dq_stage.dtype)
            dkacc[pl.ds(w0, WIN), ls] += lax.dot_general(dsb, q2s, TN, preferred_element_type=F32)
            dvacc[pl.ds(w0, WIN), ls] += lax.dot_general(pr.astype(BF16), do2, TN, preferred_element_type=F32)

        dq_out(slot).start()

        @pl.when(j == nb - 1)
        def _():
            kpad[pl.ds(PADR, S), :] = dkacc[pl.ds(PADR, S), :].astype(BF16)
            vpad[pl.ds(PADR, S), :] = dvacc[pl.ds(PADR, S), :].astype(BF16)
            ck = pltpu.make_async_copy(kpad.at[pl.ds(PADR, S), :], dqkv_hbm.at[b, :, pl.ds(D, D)], sem.at[0])
            cv = pltpu.make_async_copy(vpad.at[pl.ds(PADR, S), :], dqkv_hbm.at[b, :, pl.ds(2 * D, D)], sem.at[1])
            ck.start()
            cv.start()
            ck.wait()
            cv.wait()

        @pl.when((b == B - 1) & (j == nb - 1))
        def _():
            cb = pltpu.make_async_copy(dbias_v, dbias_hbm, sem.at[2])
            cb.start()
            dq_out(0).wait()
            dq_out(1).wait()
            cb.wait()

    blk = _nbytes((QROWS, D), BF16) * 2
    scr = (2 * _nbytes((PADR + S, D), BF16) + 2 * _nbytes((PADR + S, D), F32) + 2 * _nbytes(bias.shape, F32)
           + 8 * _nbytes((2 * QROWS, WIN), F32) + 2 * _nbytes((QROWS, D), F32))
    return pl.pallas_call(
        body, name=name, grid=(B, nb),
        in_specs=[pl.BlockSpec((1, QROWS, D), lambda b, j: (b, j, 0)),
                  pl.BlockSpec((1, QROWS, D), lambda b, j: (b, j, 0)),
                  pl.BlockSpec(memory_space=pl.ANY), pl.BlockSpec(memory_space=pl.ANY)],
        out_specs=[pl.BlockSpec(memory_space=pl.ANY), pl.BlockSpec(memory_space=pl.ANY)],
        out_shape=[jax.ShapeDtypeStruct((B, S, 3 * D), BF16), jax.ShapeDtypeStruct(bias.shape, F32)],
        scratch_shapes=[pltpu.VMEM((PADR + S, D), BF16), pltpu.VMEM((PADR + S, D), BF16),
                        pltpu.VMEM((PADR + S, D), F32), pltpu.VMEM((PADR + S, D), F32),
                        pltpu.VMEM(bias.shape, F32), pltpu.VMEM(bias.shape, F32),
                        pltpu.VMEM((2, QROWS, D), BF16), pltpu.SemaphoreType.DMA((5,))],
        compiler_params=_params(blk, scr),
    )(qkv, do, bias, qkv)


def _mla_raw_t(k2, kj, q, QB):
    return lax.dot_general(k2[_blk(kj, QB), :], q, NT, preferred_element_type=F32)


def _blk(kj, QB):
    return pl.ds(kj * QB, QB) if isinstance(kj, int) else pl.ds(pl.multiple_of(kj * QB, QB), QB)


def _mla_diag_pen(QB):
    kc = jnp.right_shift(lax.broadcasted_iota(I32, (QB, QB), 0), CHUNK_SHIFT)
    qc = jnp.right_shift(lax.broadcasted_iota(I32, (QB, QB), 1), CHUNK_SHIFT)
    return jnp.where(kc <= qc, 0.0, NEG_INF).astype(F32)


def _mla_fill_keys(kv_ref, kr_ref, k2):
    k2[:, :NOPE] = kv_ref[0, :, :NOPE]
    k2[:, NOPE:] = kr_ref[0]


def _t(x):
    return x.astype(F32).T


def mla_fwd(name, qf, kv, kr):
    B, S, W = qf.shape
    HB = W // 256
    QB = _tile(S, 256, CHUNK)
    nq = S // QB
    scale = (NOPE + ROPE) ** -0.5

    def body(q_ref, kv_ref, kr_ref, o_ref, lse_ref, k2, vt, st_buf, pen):
        qi = pl.program_id(2)

        @pl.when(qi == 0)
        def _():
            pen[...] = _mla_diag_pen(QB)
            _mla_fill_keys(kv_ref, kr_ref, k2)
            for kj in range(nq):
                vt[kj] = _t(kv_ref[0, kj * QB:(kj + 1) * QB, NOPE:]).astype(BF16)

        q = q_ref[0]
        st_buf[0] = _mla_raw_t(k2, 0, q, QB)

        def step(kj, carry):
            m, l, acc = carry
            cur = lax.rem(kj, 2)
            st_raw = st_buf[cur]
            st_buf[1 - cur] = _mla_raw_t(k2, jnp.minimum(kj + 1, qi), q, QB)
            st = st_raw * scale + jnp.where(kj == qi, pen[...], 0.0)
            m_new = jnp.maximum(m, jnp.max(st, axis=0, keepdims=True))
            a = jnp.exp(m - m_new)
            pt = jnp.exp(st - m_new)
            l = a * l + jnp.sum(pt, axis=0, keepdims=True)
            acc = a * acc + jnp.dot(vt[kj], pt.astype(BF16), preferred_element_type=F32)
            return m_new, l, acc

        init = (jnp.full((1, QB), NEG_INF, F32), jnp.zeros((1, QB), F32), jnp.zeros((NOPE, QB), F32))
        m, l, acc = lax.fori_loop(0, qi + 1, step, init)
        o_ref[0] = (acc * (1.0 / l)).T
        lse_ref[0, 0] = m + jnp.log(l)

    blk = (_nbytes((QB, 256), BF16) + _nbytes((S, 256), BF16) + _nbytes((S, LANE), BF16)
           + _nbytes((QB, LANE), F32))
    return pl.pallas_call(
        body, name=name, grid=(B, HB, nq),
        in_specs=[pl.BlockSpec((1, QB, 256), lambda b, h, i: (b, i, h)),
                  pl.BlockSpec((1, S, 256), lambda b, h, i: (b, 0, h)),
                  pl.BlockSpec((1, S, LANE), lambda b, h, i: (b, 0, 0))],
        out_specs=[pl.BlockSpec((1, QB, LANE), lambda b, h, i: (b, i, h)),
                   pl.BlockSpec((1, 1, 1, QB), lambda b, h, i: (b, h, 0, i))],
        out_shape=[jax.ShapeDtypeStruct((B, S, HB * LANE), F32), jax.ShapeDtypeStruct((B, HB, 1, S), F32)],
        scratch_shapes=[pltpu.VMEM((S, 256), BF16), pltpu.VMEM((nq, NOPE, QB), BF16),
                        pltpu.VMEM((2, QB, QB), F32), pltpu.VMEM((QB, QB), F32)],
        compiler_params=_params(blk, 2 * _nbytes((S, 256), BF16) + 10 * _nbytes((QB, QB), F32)),
    )(qf, kv, kr)


def mla_bwd(name, qf, kv, kr, do, o, lse, tabs):
    B, S, W = qf.shape
    HB = W // 256
    QB = _tile(S, 256, CHUNK)
    nq = S // QB
    scale = (NOPE + ROPE) ** -0.5

    def body(q_ref, kv_ref, kr_ref, do_ref, o_ref, lse_ref, ct_ref, s1_ref, s2_ref, dq_ref, dkv_ref, dkr_ref,
             k2, kt, dot_, delta, dqt, st_buf, dp_buf, pen, dkv_acc):
        h = pl.program_id(1)
        pen[...] = _mla_diag_pen(QB)
        dkv_acc[...] = jnp.zeros_like(dkv_acc)

        @pl.when(h == 0)
        def _():
            dkr_ref[...] = jnp.zeros_like(dkr_ref)

        _mla_fill_keys(kv_ref, kr_ref, k2)
        for i in range(nq):
            rows = slice(i * QB, (i + 1) * QB)
            kt[i] = _t(k2[rows, :]).astype(BF16)
            dot32 = _t(do_ref[0, rows, :])
            delta[i] = jnp.sum(dot32 * o_ref[0, rows, :].T, axis=0, keepdims=True)
            dot_[i] = dot32.astype(BF16)

        for qi in range(nq):
            rows = slice(qi * QB, (qi + 1) * QB)
            q = q_ref[0, rows, :]
            dob = do_ref[0, rows, :]
            lse_q = lse_ref[0, 0, :, rows]
            delta_q = delta[qi]
            dqt[...] = jnp.zeros_like(dqt)

            def raw(kj, slot, q=q, qi=qi):
                st_buf[slot] = _mla_raw_t(k2, kj, q, QB)
                dp_buf[slot] = jnp.dot(kv_ref[0, _blk(kj, QB), NOPE:], dot_[qi], preferred_element_type=F32)

            raw(0, 0)

            def step(kj, carry, q=q, dob=dob, lse_q=lse_q, delta_q=delta_q, qi=qi, raw=raw):
                ks = pl.ds(pl.multiple_of(kj * QB, QB), QB)
                cur = lax.rem(kj, 2)
                st_raw, dp_raw = st_buf[cur], dp_buf[cur]
                raw(jnp.minimum(kj + 1, qi), 1 - cur)
                pt = jnp.exp(st_raw * scale + jnp.where(kj == qi, pen[...], 0.0) - lse_q)
                dst = (pt * (dp_raw - delta_q) * scale).astype(BF16)
                dkv_acc[ks, NOPE:] += jnp.dot(pt.astype(BF16), dob, preferred_element_type=F32)
                dk2 = jnp.dot(dst, q, preferred_element_type=F32)
                dkv_acc[ks, :NOPE] += dk2[:, :NOPE]
                dkr_ref[0, ks, :] += dk2[:, NOPE:]
                dqt[...] += jnp.dot(kt[kj], dst, preferred_element_type=F32)
                return carry

            lax.fori_loop(0, qi + 1, step, 0)
            dq = dqt[...].T
            dq_ref[0, rows, :NOPE] = dq[:, :NOPE].astype(dq_ref.dtype)
            dq_ref[0, rows, NOPE:] = _rope_tile_bwd(dq[:, NOPE:], ct_ref[rows, :], s1_ref[rows, :],
                                                    s2_ref[rows, :]).astype(dq_ref.dtype)

        dkv_ref[0] = dkv_acc[...].astype(dkv_ref.dtype)

    head = lambda w: pl.BlockSpec((1, S, w), lambda b, h: (b, 0, h))
    shared = pl.BlockSpec((1, S, LANE), lambda b, h: (b, 0, 0))
    blk = (2 * _nbytes((S, 256), BF16) + 2 * _nbytes((S, LANE), BF16) + _nbytes((S, LANE), F32)
           + 2 * _nbytes((S, 256), F32) + _nbytes((S, LANE), F32))
    scr = 3 * _nbytes((S, 256), BF16) + 14 * _nbytes((QB, QB), F32)
    return pl.pallas_call(
        body, name=name, grid=(B, HB),
        in_specs=[head(256), head(256), shared, head(LANE), head(LANE),
                  pl.BlockSpec((1, 1, 1, S), lambda b, h: (b, h, 0, 0))]
        + [pl.BlockSpec((S, LANE), lambda b, h: (0, 0))] * 3,
        out_specs=[head(256), head(256), shared],
        out_shape=[jax.ShapeDtypeStruct((B, S, W), BF16), jax.ShapeDtypeStruct((B, S, W), BF16),
                   jax.ShapeDtypeStruct((B, S, LANE), F32)],
        scratch_shapes=[pltpu.VMEM((S, 256), BF16), pltpu.VMEM((nq, 256, QB), BF16),
                        pltpu.VMEM((nq, NOPE, QB), BF16), pltpu.VMEM((nq, 1, QB), F32),
                        pltpu.VMEM((256, QB), F32), pltpu.VMEM((2, QB, QB), F32), pltpu.VMEM((2, QB, QB), F32),
                        pltpu.VMEM((QB, QB), F32), pltpu.VMEM((S, 256), F32)],
        compiler_params=_params(blk, scr),
    )(qf, kv, kr, do, o, lse, *tabs)


GROUP_STEPS = 2


def cast_group(name, ws, layers, idx, after=None):
    n = len(ws)
    n_in = n + (after is not None)

    def body(k_ref, *refs):
        for i in range(n):
            refs[n_in + i][...] = refs[i][...].astype(BF16)

    def spec_in(w, layer):
        return pl.BlockSpec((None, w.shape[1] // GROUP_STEPS, w.shape[2]), lambda r, k_ref: (layer, r, 0))

    def spec_out(w):
        return pl.BlockSpec((None, w.shape[1] // GROUP_STEPS, w.shape[2]), lambda r, k_ref: (k_ref[0], r, 0))

    return pl.pallas_call(
        body, name=name,
        grid_spec=pltpu.PrefetchScalarGridSpec(
            num_scalar_prefetch=1, grid=(GROUP_STEPS,),
            in_specs=([spec_in(w, l) for w, l in zip(ws, layers)]
                      + [pl.BlockSpec(memory_space=pl.ANY)] * (after is not None)),
            out_specs=[spec_out(w) for w in ws]),
        out_shape=[jax.ShapeDtypeStruct((N_CHIPS, *w.shape[1:]), BF16) for w in ws],
        compiler_params=_params(sum(_nbytes(w.shape[1:], F32) * 3 // 2 for w in ws) // GROUP_STEPS),
    )(idx, *ws, *([] if after is None else [after]))


def adamw(name, w, g, m, v):
    R, C = w.shape
    tr = _tile(R, max(8, (1 << 18) // C // 8 * 8), 8)
    c1 = 1.0 - ADAM_B1 ** ADAM_STEP
    c2 = 1.0 - ADAM_B2 ** ADAM_STEP

    def body(w_ref, g_ref, m_ref, v_ref, d_ref, mo_ref, vo_ref):
        gv = g_ref[...]
        mn = ADAM_B1 * m_ref[...] + (1.0 - ADAM_B1) * gv
        vn = ADAM_B2 * v_ref[...] + (1.0 - ADAM_B2) * (gv * gv)
        mo_ref[...] = mn
        vo_ref[...] = vn
        d_ref[...] = -ADAM_LR * ((mn / c1) / (jnp.sqrt(vn / c2) + ADAM_EPS) + ADAM_WD * w_ref[...])

    spec = pl.BlockSpec((tr, C), lambda r: (r, 0))
    return pl.pallas_call(
        body, name=name, grid=(R // tr,), in_specs=[spec] * 4, out_specs=[spec] * 3,
        out_shape=[jax.ShapeDtypeStruct((R, C), F32)] * 3,
        compiler_params=_params(7 * _nbytes((tr, C), F32), 4 * _nbytes((tr, C), F32)),
    )(w, g, m, v)


def half_sum_group(name, dws, landed, idx):
    n = len(dws)
    steps = GROUP_STEPS // 2

    def body(i_ref, *refs):
        for i in range(n):
            refs[2 * n + i][...] = (refs[i][...].astype(F32) + refs[n + i][...].astype(F32)).astype(BF16)

    def own(d):
        return pl.BlockSpec((None, None, d.shape[2] // steps, d.shape[3]), lambda k, r, i_ref: (k, i_ref[1], r, 0))

    def flat(d):
        return pl.BlockSpec((None, d.shape[2] // steps, d.shape[3]), lambda k, r, i_ref: (k, r, 0))

    return pl.pallas_call(
        body, name=name,
        grid_spec=pltpu.PrefetchScalarGridSpec(
            num_scalar_prefetch=1, grid=(N_CHIPS, steps),
            in_specs=[own(d) for d in dws] + [flat(d) for d in dws], out_specs=[flat(d) for d in dws]),
        out_shape=[jax.ShapeDtypeStruct((N_CHIPS, *d.shape[2:]), BF16) for d in dws],
        compiler_params=_params(sum(3 * _nbytes(d.shape[2:], BF16) for d in dws) // steps),
    )(idx, *dws, *landed)


def chip_sum_group(name, parts, landed, gbufs, layers, idx):
    n = len(parts)
    steps = GROUP_STEPS // 2

    def body(i_ref, *refs):
        for i in range(n):
            a, b = refs[i], refs[n + i]
            refs[3 * n + i][...] = ((a[...].astype(F32) + b[0].astype(F32)) + b[1].astype(F32)) + b[2].astype(F32)

    def mine(p):
        return pl.BlockSpec((None, p.shape[1] // steps, p.shape[2]), lambda r, i_ref: (i_ref[0], r, 0))

    def three(p):
        return pl.BlockSpec((3, p.shape[1] // steps, p.shape[2]), lambda r, i_ref: (0, r, 0))

    def out(p, layer):
        return pl.BlockSpec((None, None, p.shape[1] // steps, p.shape[2]), lambda r, i_ref: (layer, i_ref[1], r, 0))

    return pl.pallas_call(
        body, name=name,
        grid_spec=pltpu.PrefetchScalarGridSpec(
            num_scalar_prefetch=1, grid=(steps,),
            in_specs=[mine(p) for p in parts] + [three(p) for p in parts] + [pl.BlockSpec(memory_space=pl.ANY)] * n,
            out_specs=[out(p, l) for p, l in zip(parts, layers)]),
        out_shape=[jax.ShapeDtypeStruct(g.shape, F32) for g in gbufs],
        input_output_aliases={1 + 2 * n + i: i for i in range(n)},
        compiler_params=_params(sum(6 * _nbytes(p.shape[1:], BF16) for p in parts) // steps),
    )(idx, *parts, *landed, *gbufs)


ANY = pl.BlockSpec(memory_space=pl.ANY)


def _place():
    x, y, c = lax.axis_index("x"), lax.axis_index("y"), lax.axis_index("c")
    chips = [(1 - x, y), (x, 1 - y), (1 - x, 1 - y)]
    return x, y, c, chips


HBM = pl.BlockSpec(memory_space=pltpu.HBM)
SEM = pl.BlockSpec(memory_space=pltpu.SEMAPHORE)
EFFECT = pltpu.SideEffectType.DATAFLOW_SIDE_EFFECTING


def _in_hbm(a):
    return pltpu.with_memory_space_constraint(a, pltpu.HBM)


def _ici_copy(src, dst, send_sems, recv_sems, k, peer):
    return pltpu.make_async_remote_copy(src_ref=src, dst_ref=dst, send_sem=send_sems.at[k], recv_sem=recv_sems.at[k],
                                        device_id=peer, device_id_type=MESH)


def ici_start(name, bufs, lands, after, gather):
    n, nl = len(bufs), len(lands)

    def body(*refs):
        b_in = refs[:n]
        send_sems, recv_sems = refs[n + nl + 1], refs[n + nl + 2]
        b_out = refs[n + nl + 3:2 * n + nl + 3]
        l_out = refs[2 * n + nl + 3:2 * n + 2 * nl + 3]
        token = refs[-1]
        x, y, c, chips = _place()
        kme = 2 * x + y
        for i in range(n):
            for j in range(3):
                peer = (*chips[j], c)
                if gather:
                    _ici_copy(b_out[i].at[kme, c], b_out[i].at[kme, c], send_sems, recv_sems, 3 * i + j, peer).start()
                else:
                    kd = 2 * chips[j][0] + chips[j][1]
                    _ici_copy(b_out[i].at[kd], l_out[i].at[j], send_sems, recv_sems, 3 * i + j, peer).start()
        token[...] = jnp.zeros_like(token)

    arrays = [*bufs, *lands]
    outs = pl.pallas_call(
        body, name=name,
        in_specs=[HBM] * (n + nl) + [ANY],
        out_specs=(SEM, SEM, *[HBM] * (n + nl), pl.BlockSpec(memory_space=pltpu.VMEM)),
        out_shape=(pltpu.SemaphoreType.DMA((3 * n,)), pltpu.SemaphoreType.DMA((3 * n,)),
                   *[pltpu.HBM(a.shape, a.dtype) for a in arrays], jax.ShapeDtypeStruct((8, LANE), F32)),
        input_output_aliases={i: 2 + i for i in range(n + nl)},
        compiler_params=pltpu.CompilerParams(has_side_effects=EFFECT),
    )(*[_in_hbm(a) for a in arrays], after)
    return outs[0], outs[1], list(outs[2:2 + n]), list(outs[2 + n:2 + n + nl]), outs[-1]


def ici_wait(name, send_sems, recv_sems, bufs, lands, after, gather):
    n, nl = len(bufs), len(lands)

    def body(*refs):
        b_in, l_in = refs[:n], refs[n:n + nl]
        send_sems, recv_sems = refs[n + nl], refs[n + nl + 1]
        x, y, c, chips = _place()
        kme = 2 * x + y
        for i in range(n):
            for j in range(3):
                peer = (*chips[j], c)
                kj = 2 * chips[j][0] + chips[j][1]
                if gather:
                    _ici_copy(b_in[i].at[kme, c], b_in[i].at[kme, c], send_sems, recv_sems, 3 * i + j, peer).wait_send()
                    _ici_copy(b_in[i].at[kj, c], b_in[i].at[kj, c], send_sems, recv_sems, 3 * i + j, peer).wait_recv()
                else:
                    _ici_copy(b_in[i].at[kj], l_in[i].at[j], send_sems, recv_sems, 3 * i + j, peer).wait_send()
                    _ici_copy(b_in[i].at[kj], l_in[i].at[j], send_sems, recv_sems, 3 * i + j, peer).wait_recv()

    arrays = [*bufs, *lands]
    outs = pl.pallas_call(
        body, name=name,
        in_specs=[HBM] * (n + nl) + [SEM, SEM, ANY],
        out_specs=tuple([HBM] * (n + nl)),
        out_shape=tuple(pltpu.HBM(a.shape, a.dtype) for a in arrays),
        input_output_aliases={i: i for i in range(n + nl)},
        compiler_params=pltpu.CompilerParams(has_side_effects=EFFECT),
    )(*arrays, send_sems, recv_sems, after)
    return list(outs[:n]), list(outs[n:])


def gather_pair_pass(name, bufs):
    n = len(bufs)

    def body(*refs):
        b = refs[n:2 * n]
        send_sems, recv_sems = refs[2 * n:]
        x, y, c, chips = _place()
        sib = (x, y, 1 - c)

        def d2d(i, j, which):
            kj = 2 * chips[j][0] + chips[j][1]
            return _ici_copy(b[i].at[kj, which], b[i].at[kj, which], send_sems, recv_sems, 3 * i + j, sib)

        for i in range(n):
            for j in range(3):
                d2d(i, j, c).start()
        for i in range(n):
            for j in range(3):
                d2d(i, j, 1 - c).wait_recv()
        for i in range(n):
            for j in range(3):
                d2d(i, j, c).wait_send()

    return pl.pallas_call(
        body, name=name, in_specs=[ANY] * n, out_specs=[ANY] * n,
        out_shape=[jax.ShapeDtypeStruct(a.shape, a.dtype) for a in bufs],
        input_output_aliases={i: i for i in range(n)},
        scratch_shapes=[pltpu.SemaphoreType.DMA((3 * n,)), pltpu.SemaphoreType.DMA((3 * n,))],
    )(*bufs)


def pair_exchange(name, dws):
    n = len(dws)

    def body(*refs):
        ins, outs = refs[:n], refs[n:2 * n]
        send_sems, recv_sems = refs[2 * n:]
        x, y, c, _ = _place()
        copies = []
        for i in range(n):
            copies.append(pltpu.make_async_remote_copy(
                src_ref=ins[i].at[:, 1 - c], dst_ref=outs[i],
                send_sem=send_sems.at[i], recv_sem=recv_sems.at[i],
                device_id=(x, y, 1 - c), device_id_type=MESH))
            copies[i].start()
        for cp in copies:
            cp.wait_recv()
        for cp in copies:
            cp.wait_send()

    return pl.pallas_call(
        body, name=name, in_specs=[ANY] * n, out_specs=[ANY] * n,
        out_shape=[jax.ShapeDtypeStruct((N_CHIPS, *d.shape[2:]), d.dtype) for d in dws],
        scratch_shapes=[pltpu.SemaphoreType.DMA((n,)), pltpu.SemaphoreType.DMA((n,))],
    )(*dws)


def pair_assemble(gbufs):
    n = len(gbufs)

    def body(*refs):
        bufs = refs[n:2 * n]
        send_sems, recv_sems = refs[2 * n:]
        x, y, c, _ = _place()
        copies = []
        for i in range(n):
            copies.append(pltpu.make_async_remote_copy(
                src_ref=bufs[i].at[:, c], dst_ref=bufs[i].at[:, c],
                send_sem=send_sems.at[i], recv_sem=recv_sems.at[i],
                device_id=(x, y, 1 - c), device_id_type=MESH))
            copies[i].start()
        for i in range(n):
            pltpu.make_async_remote_copy(
                src_ref=bufs[i].at[:, 1 - c], dst_ref=bufs[i].at[:, 1 - c],
                send_sem=send_sems.at[i], recv_sem=recv_sems.at[i],
                device_id=(x, y, 1 - c), device_id_type=MESH).wait_recv()
        for cp in copies:
            cp.wait_send()

    return pl.pallas_call(
        body, name="grad_pair_assemble", in_specs=[ANY] * n, out_specs=[ANY] * n,
        out_shape=[jax.ShapeDtypeStruct(g.shape, g.dtype) for g in gbufs],
        input_output_aliases={i: i for i in range(n)},
        scratch_shapes=[pltpu.SemaphoreType.DMA((n,)), pltpu.SemaphoreType.DMA((n,))],
    )(*gbufs)


def all_reduce_small(vec):
    NR = vec.shape[0]
    flips = [(fx, fy, fc) for fx in (0, 1) for fy in (0, 1) for fc in (0, 1)][1:]

    def body(v_ref, o_ref, buf, send_sems, recv_sems):
        x, y, c, _ = _place()
        me = 4 * x + 2 * y + c
        buf[me] = v_ref[...]
        copies = []
        for j, (fx, fy, fc) in enumerate(flips):
            peer = (1 - x if fx else x, 1 - y if fy else y, 1 - c if fc else c)
            copies.append(pltpu.make_async_remote_copy(
                src_ref=v_ref, dst_ref=buf.at[me], send_sem=send_sems.at[j], recv_sem=recv_sems.at[j],
                device_id=peer, device_id_type=MESH))
            copies[j].start()
        for cp in copies:
            cp.wait_recv()
        for cp in copies:
            cp.wait_send()
        acc = buf[0]
        for d in range(1, 8):
            acc = acc + buf[d]
        o_ref[...] = acc

    return pl.pallas_call(
        body, name="all_reduce_small",
        in_specs=[pl.BlockSpec(memory_space=pltpu.VMEM)], out_specs=pl.BlockSpec(memory_space=pltpu.VMEM),
        out_shape=jax.ShapeDtypeStruct((NR, LANE), F32),
        scratch_shapes=[pltpu.VMEM((8, NR, LANE), F32), pltpu.SemaphoreType.DMA((7,)),
                        pltpu.SemaphoreType.DMA((7,))],
    )(vec)


def _pack(arrays):
    flat = jnp.concatenate([a.reshape(-1).astype(F32) for a in arrays])
    n = flat.shape[0]
    npad = -(-n // (8 * LANE)) * (8 * LANE)
    return jnp.pad(flat, (0, npad - n)).reshape(npad // LANE, LANE)


def _unpack(buf, like):
    flat = buf.reshape(-1)
    out, off = [], 0
    for a in like:
        out.append(flat[off:off + a.size].reshape(a.shape))
        off += a.size
    return out


def kernel(x, ffn1_norm, ffn1_w_in, ffn1_w_out, mix_norm, ffn2_norm, ffn2_w_in, ffn2_w_out, a_w_qkv, a_rel_bias, a_w_o, kv_norm, kv_w_down, kv_latent_norm, kv_w_up, b_w_dq, b_q_norm, b_w_uq, b_w_o, final_norm, loss_target, m_ffn1_norm, m_ffn1_w_in, m_ffn1_w_out, m_mix_norm, m_ffn2_norm, m_ffn2_w_in, m_ffn2_w_out, m_a_w_qkv, m_a_rel_bias, m_a_w_o, m_kv_norm, m_kv_w_down, m_kv_latent_norm, m_kv_w_up, m_b_w_dq, m_b_q_norm, m_b_w_uq, m_b_w_o, m_final_norm, v_ffn1_norm, v_ffn1_w_in, v_ffn1_w_out, v_mix_norm, v_ffn2_norm, v_ffn2_w_in, v_ffn2_w_out, v_a_w_qkv, v_a_rel_bias, v_a_w_o, v_kv_norm, v_kv_w_down, v_kv_latent_norm, v_kv_w_up, v_b_w_dq, v_b_q_norm, v_b_w_uq, v_b_w_o, v_final_norm):
    B, S, D = x.shape
    T = B * S
    HB = D // 128
    QL = b_q_norm.shape[-1]
    KVL = kv_latent_norm.shape[0]
    hpc = HB // N_CHIPS
    tabs = rope_tables(S)
    idx = jnp.stack([2 * lax.axis_index("x") + lax.axis_index("y"), lax.axis_index("c")]).astype(I32)

    def halves(a):
        return a.reshape(*a.shape[:-2], 2, a.shape[-2] // 2, a.shape[-1])

    def whole(a):
        return a.reshape(*a.shape[:-3], 2 * a.shape[-2], a.shape[-1])

    kv_w_down_p = jnp.pad(kv_w_down, ((0, 0), (0, LANE - ROPE)))[None]
    b_w_uq_p = jnp.pad(b_w_uq.reshape(1, QL, hpc, NOPE + ROPE),
                       ((0, 0), (0, 0), (0, 0), (0, LANE - ROPE))).reshape(1, QL, hpc * 256)
    sharded = [("ffn1_w_in", ffn1_w_in), ("ffn1_w_out", ffn1_w_out), ("ffn2_w_in", ffn2_w_in),
               ("ffn2_w_out", ffn2_w_out), ("a_w_qkv", a_w_qkv), ("a_w_o", a_w_o),
               ("kv_w_down", kv_w_down_p), ("kv_w_up", kv_w_up[None]), ("b_w_dq", b_w_dq),
               ("b_w_uq", b_w_uq_p), ("b_w_o", b_w_o)]
    names = [nm for nm, _ in sharded]
    shard_of = dict(sharded)
    W = {}

    gather_groups = [
        [("ffn1_w_in", 0)],
        [("ffn1_w_out", 0)],
        [("a_w_qkv", 0), ("a_w_o", 0)],
        [("ffn2_w_in", 0), ("ffn2_w_out", 0), ("kv_w_down", 0), ("kv_w_up", 0)],
        [("ffn1_w_in", 1), ("ffn1_w_out", 1), ("b_w_dq", 0), ("b_w_uq", 0), ("b_w_o", 0), ("ffn2_w_in", 1),
         ("ffn2_w_out", 1)]]

    own = {}

    def cast(g, after=None):
        keys = gather_groups[g]
        own.update(zip(keys, cast_group(f"cast_group_{g}", [shard_of[nm] for nm, _ in keys], [l for _, l in keys],
                                        idx, after=after)))

    def gather_start(g, after):
        keys = gather_groups[g]
        ss, rs, bufs, _, token = ici_start(f"gather_start_{g}", [halves(own[k]) for k in keys], [], after, True)
        return (g, ss, rs, bufs), token

    def gather_finish(state, after):
        g, ss, rs, bufs = state
        bufs, _ = ici_wait(f"gather_wait_{g}", ss, rs, bufs, [], after, True)
        full = gather_pair_pass(f"gather_pair_{g}", bufs)
        for k, w in zip(gather_groups[g], full):
            W[k] = whole(w)
        return full[0]

    def tied(a, token):
        return a + token[0, 0]

    def col(nm, l=0):
        return W[(nm, l)]

    def row(nm, l=0):
        w = W[(nm, l)]
        return w.reshape(N_CHIPS * w.shape[1], w.shape[2])

    bias = rel_bias_tile("rel_bias_tile", a_rel_bias[0])

    def ffn_fwd(tag, h, g, w_in, w_out):
        xn = rms_fwd(f"{tag}_norm", h, g)
        u, act = ffn_in_act(f"{tag}_in", xn, w_in)
        return mm_roww(f"{tag}_out", act, w_out, F32, res=h, alpha=0.5), (xn, u, act)

    h0 = x.reshape(T, D)
    for g in range(3):
        cast(g)
    st0, tok0 = gather_start(0, h0)
    st1, tok1 = gather_start(1, tok0)
    st2, tok2 = gather_start(2, tok1)
    for g in range(3, len(gather_groups)):
        cast(g, tok2)
    xn0 = rms_fwd("l0f1_norm", h0, tied(ffn1_norm[0], tok2))
    gather_finish(st0, xn0)
    u0, act0 = ffn_in_act("l0f1_in", xn0, col("ffn1_w_in", 0))
    gather_finish(st1, u0)
    h1 = mm_roww("l0f1_out", act0, row("ffn1_w_out", 0), F32, res=h0, alpha=0.5)
    sv_f1a = (xn0, u0, act0)
    done2 = gather_finish(st2, h1)
    st3, tok3 = gather_start(3, done2)
    st4, tok4 = gather_start(4, tok3)
    hn_a = rms_fwd("l0mix_norm", h1, tied(mix_norm[0], tok4))
    qkv = mm_colw("l0_qkv", hn_a, col("a_w_qkv"), BF16).reshape(B, S, 3 * D)
    o_a = attn_a_fwd("l0_attn", qkv, bias).reshape(T, D)
    h2 = mm_roww("l0_attn_out", o_a, row("a_w_o"), F32, res=h1)
    gather_finish(st3, h2)
    h3, sv_f2a = ffn_fwd("l0f2", h2, ffn2_norm[0], col("ffn2_w_in", 0), row("ffn2_w_out", 0))

    hkv = rms_fwd("kv_norm", h3, kv_norm)
    ckr = mm_roww("kv_down", hkv, row("kv_w_down"), F32)
    ckv, kr = kvprep_fwd("kv_prep", ckr, kv_latent_norm, tabs, B, S)
    kvb = mm_colw("kv_up", ckv, col("kv_w_up"), BF16).reshape(B, S, HB * 256)
    gather_finish(st4, kvb)

    h4, sv_f1b = ffn_fwd("l1f1", h3, ffn1_norm[1], col("ffn1_w_in", 1), row("ffn1_w_out", 1))
    hn_b = rms_fwd("l1mix_norm", h4, mix_norm[1])
    cqp = mm_roww("l1_dq", hn_b, row("b_w_dq"), F32)
    cq = rms_fwd("l1_q_norm", cqp, b_q_norm[0])
    qf = uq_rope("l1_uq", cq, col("b_w_uq"), tabs, S).reshape(B, S, HB * 256)
    o_b, lse = mla_fwd("l1_attn", qf, kvb, kr)
    h5 = mm_roww("l1_attn_out", o_b.reshape(T, HB * LANE), row("b_w_o"), F32, res=h4)
    h6, sv_f2b = ffn_fwd("l1f2", h5, ffn2_norm[1], col("ffn2_w_in", 1), row("ffn2_w_out", 1))

    dh, g_final, loss_part = loss_head("loss_head", h6, final_norm, loss_target.reshape(T, D))

    gw = {}
    gbufs = {nm: lax.empty(halves(w).shape, F32) for nm, w in sharded}

    def reduce_start(r, keys, after):
        dws = [halves(gw[k]) for k in keys]
        landed = pair_exchange(f"grad_pair_exchange_{r}", dws)
        parts = half_sum_group(f"half_sum_{r}", dws, landed, idx)
        lands = [lax.empty((3, *p.shape[1:]), p.dtype) for p in parts]
        ss, rs, parts, lands, token = ici_start(f"reduce_start_{r}", parts, lands, after, False)
        return (r, keys, ss, rs, parts, lands), token

    def reduce_finish(state, after):
        r, keys, ss, rs, parts, lands = state
        parts, lands = ici_wait(f"reduce_wait_{r}", ss, rs, parts, lands, after, False)
        done = chip_sum_group(f"chip_sum_{r}", parts, lands, [gbufs[nm] for nm, _ in keys], [l for _, l in keys], idx)
        gbufs.update(zip([nm for nm, _ in keys], done))
        return done[0]

    def ffn_bwd(tag, dh, h_in, g, w_in, w_out, saved, key_in, key_out, after=None, then=None):
        xn, u, act = saved
        du = ffn_dact(f"{tag}_dact", dh, w_out, u, after=after)
        dwo = mm_droww(f"{tag}_dwout", act, dh, alpha=0.5)
        gw[key_out] = dwo.reshape(N_CHIPS, dwo.shape[0] // N_CHIPS, dwo.shape[1])
        gw[key_in] = mm_dcolw(f"{tag}_dwin", xn, du, pair_layout=True)
        token = then(du) if then is not None else None
        return dx_norm_bwd(f"{tag}_dxn", du, w_in, h_in, g, dres=dh, pair_layout=True, after=token)

    def chip_major(dw):
        return dw.reshape(N_CHIPS, dw.shape[0] // N_CHIPS, dw.shape[1])

    dh, g_f2b = ffn_bwd("l1f2b", dh, h5, ffn2_norm[1], col("ffn2_w_in", 1), row("ffn2_w_out", 1), sv_f2b,
                        ("ffn2_w_in", 1), ("ffn2_w_out", 1))
    red0, rtok0 = reduce_start(0, [("ffn2_w_in", 1), ("ffn2_w_out", 1)], dh)
    do_b = mm_roww_t("l1_attn_do", dh, row("b_w_o"), BF16, after=rtok0).reshape(B, S, HB * LANE)
    gw[("b_w_o", 0)] = chip_major(mm_droww("l1_attn_dwo", o_b.reshape(T, HB * LANE), dh))
    dqpre, dkv, dkr = mla_bwd("l1_attn_bwd", qf, kvb, kr, do_b, o_b, lse, tabs)
    dqpre = dqpre.reshape(T, HB * 256)
    gw[("b_w_uq", 0)] = mm_dcolw("l1_dwuq", cq, dqpre)
    dcqp, g_qn = dx_norm_bwd("l1_dcq", dqpre, col("b_w_uq"), cqp, b_q_norm[0])
    gw[("b_w_dq", 0)] = chip_major(mm_droww("l1_dwdq", hn_b, dcqp))
    dhn = mm_roww_t("l1_dhn", dcqp, row("b_w_dq"), F32)
    dh, g_mixb = rms_bwd("l1_dmix", h4, mix_norm[1], dhn, dres=dh)
    dh, g_f1b = ffn_bwd("l1f1b", dh, h3, ffn1_norm[1], col("ffn1_w_in", 1), row("ffn1_w_out", 1), sv_f1b,
                        ("ffn1_w_in", 1), ("ffn1_w_out", 1))
    fin0 = reduce_finish(red0, dh)
    red1, rtok1 = reduce_start(1, [("b_w_o", 0), ("b_w_uq", 0), ("b_w_dq", 0), ("ffn1_w_in", 1), ("ffn1_w_out", 1)], fin0)
    dkv2 = dkv.reshape(T, HB * 256)
    gw[("kv_w_up", 0)] = mm_dcolw("kv_dwup", ckv, dkv2, after=rtok1)
    dckv = mm_colw_t("kv_dckv", dkv2, col("kv_w_up"), F32, after=rtok1)
    dckr, g_lat = kvprep_bwd("kv_prep_bwd", ckr, kv_latent_norm, dckv, dkr, tabs, B, S)
    gw[("kv_w_down", 0)] = chip_major(mm_droww("kv_dwdown", hkv, dckr))
    dhkv = mm_roww_t("kv_dhkv", dckr, row("kv_w_down"), F32)
    dh, g_kvn = rms_bwd("kv_dnorm", h3, kv_norm, dhkv, dres=dh)
    dh, g_f2a = ffn_bwd("l0f2b", dh, h2, ffn2_norm[0], col("ffn2_w_in", 0), row("ffn2_w_out", 0), sv_f2a,
                        ("ffn2_w_in", 0), ("ffn2_w_out", 0))
    do_a = mm_roww_t("l0_attn_do", dh, row("a_w_o"), BF16).reshape(B, S, D)
    gw[("a_w_o", 0)] = chip_major(mm_droww("l0_attn_dwo", o_a, dh))
    dqkv, dbias = attn_a_bwd("l0_attn_bwd", qkv, do_a, bias)
    dqkv = dqkv.reshape(T, 3 * D)
    gw[("a_w_qkv", 0)] = mm_dcolw("l0_dwqkv", hn_a, dqkv)
    dh, g_mixa = dx_norm_bwd("l0_dhn", dqkv, col("a_w_qkv"), h1, mix_norm[0], dres=dh)
    fin1 = reduce_finish(red1, dh)
    red2, rtok2 = reduce_start(2, [("kv_w_up", 0), ("kv_w_down", 0), ("ffn2_w_in", 0), ("ffn2_w_out", 0),
                                   ("a_w_o", 0), ("a_w_qkv", 0)], fin1)
    last = {}

    def last_group(du):
        fin2 = reduce_finish(red2, gw[("ffn1_w_in", 0)])
        last["red"], token = reduce_start(3, [("ffn1_w_in", 0), ("ffn1_w_out", 0)], fin2)
        return token

    dh, g_f1a = ffn_bwd("l0f1b", dh, h0, ffn1_norm[0], col("ffn1_w_in", 0), row("ffn1_w_out", 0), sv_f1a,
                        ("ffn1_w_in", 0), ("ffn1_w_out", 0), after=rtok2, then=last_group)
    grad_x = dh.reshape(B, S, D)
    g_rel = rel_bias_grad("rel_bias_grad", dbias)[:, :2 * MAX_REL + 1][None]
    reduce_finish(last["red"], dh)

    full = [whole(g) for g in pair_assemble([gbufs[nm] for nm in names])]
    G = {nm: g for (nm, _), g in zip(sharded, full)}
    G["kv_w_down"] = G["kv_w_down"][0, :, :KVL + ROPE]
    G["kv_w_up"] = G["kv_w_up"][0]
    G["b_w_uq"] = G["b_w_uq"].reshape(1, QL, hpc, 256)[..., :NOPE + ROPE].reshape(b_w_uq.shape)

    small = [("ffn1_norm", jnp.stack([g_f1a, g_f1b])), ("mix_norm", jnp.stack([g_mixa, g_mixb])),
             ("ffn2_norm", jnp.stack([g_f2a, g_f2b])), ("a_rel_bias", g_rel), ("kv_norm", g_kvn),
             ("kv_latent_norm", g_lat), ("b_q_norm", g_qn[None]), ("final_norm", g_final)]
    red = all_reduce_small(_pack([loss_part] + [g for _, g in small]))
    unpacked = _unpack(red, [loss_part] + [g for _, g in small])
    loss = unpacked[0][0, 0]
    for (nm, _), g in zip(small, unpacked[1:]):
        G[nm] = g

    given = dict(ffn1_norm=(ffn1_norm, m_ffn1_norm, v_ffn1_norm), ffn1_w_in=(ffn1_w_in, m_ffn1_w_in, v_ffn1_w_in),
                 ffn1_w_out=(ffn1_w_out, m_ffn1_w_out, v_ffn1_w_out), mix_norm=(mix_norm, m_mix_norm, v_mix_norm),
                 ffn2_norm=(ffn2_norm, m_ffn2_norm, v_ffn2_norm), ffn2_w_in=(ffn2_w_in, m_ffn2_w_in, v_ffn2_w_in),
                 ffn2_w_out=(ffn2_w_out, m_ffn2_w_out, v_ffn2_w_out), a_w_qkv=(a_w_qkv, m_a_w_qkv, v_a_w_qkv),
                 a_rel_bias=(a_rel_bias, m_a_rel_bias, v_a_rel_bias), a_w_o=(a_w_o, m_a_w_o, v_a_w_o),
                 kv_norm=(kv_norm, m_kv_norm, v_kv_norm), kv_w_down=(kv_w_down, m_kv_w_down, v_kv_w_down),
                 kv_latent_norm=(kv_latent_norm, m_kv_latent_norm, v_kv_latent_norm),
                 kv_w_up=(kv_w_up, m_kv_w_up, v_kv_w_up), b_w_dq=(b_w_dq, m_b_w_dq, v_b_w_dq),
                 b_q_norm=(b_q_norm, m_b_q_norm, v_b_q_norm), b_w_uq=(b_w_uq, m_b_w_uq, v_b_w_uq),
                 b_w_o=(b_w_o, m_b_w_o, v_b_w_o), final_norm=(final_norm, m_final_norm, v_final_norm))
    order = list(given)
    delta, new_m, new_v = {}, {}, {}
    small_names = [nm for nm, _ in small]
    packed = [_pack([given[nm][k] for nm in small_names]) for k in range(3)]
    outs = adamw("adamw_small", packed[0], _pack([G[nm] for nm in small_names]), packed[1], packed[2])
    for dst, buf in zip((delta, new_m, new_v), outs):
        for nm, a in zip(small_names, _unpack(buf, [given[nm][0] for nm in small_names])):
            dst[nm] = a
    for nm, _ in sharded:
        w, m, v = given[nm]
        g = G[nm].reshape(w.shape)
        G[nm] = g
        two = lambda a: a.reshape(-1, a.shape[-1])
        d_, m_, v_ = adamw(f"adamw_{nm}", two(w), two(g), two(m), two(v))
        delta[nm], new_m[nm], new_v[nm] = d_.reshape(w.shape), m_.reshape(w.shape), v_.reshape(w.shape)

    return (loss, grad_x, *[G[n] for n in order], *[delta[n] for n in order],
            *[new_m[n] for n in order], *[new_v[n] for n in order])
```

```python
import math

import jax
import jax.numpy as jnp
from jax import lax
from jax.experimental import pallas as pl
from jax.experimental.pallas import tpu as pltpu

F32 = jnp.float32
BF16 = jnp.bfloat16
I32 = jnp.int32

CHUNK = 64
CHUNK_SHIFT = 6
HEAD_DIM_A = 64
LEFT_CHUNKS = 8
MAX_REL = 128
REL_PAD = 384
QROWS = 2 * CHUNK
WIN = (LEFT_CHUNKS + 2) * CHUNK
PADR = LEFT_CHUNKS * CHUNK
NOPE = 128
ROPE = 64
EPS = 1e-6
NEG_INF = -1e30
ROPE_THETA = 10000.0
ADAM_LR, ADAM_B1, ADAM_B2, ADAM_EPS, ADAM_WD, ADAM_STEP = 0.001, 0.9, 0.999, 1e-08, 0.01, 10
N_CHIPS = 4
LANE = 128
MESH = pl.DeviceIdType.MESH
VMEM_CAP_MB = 60
VMEM_FLOOR_MB = 48

NN = (((1,), (0,)), ((), ()))
NT = (((1,), (1,)), ((), ()))
TN = (((0,), (0,)), ((), ()))


def _tile(n, pref, mult):
    t = (min(pref, n) // mult) * mult
    while t >= mult:
        if n % t == 0:
            return t
        t -= mult
    return n


def _nbytes(shape, dtype):
    return math.prod(shape) * jnp.dtype(dtype).itemsize


def _params(block_bytes, extra_bytes=0):
    need = 2 * block_bytes + extra_bytes
    mb = min(VMEM_CAP_MB, max(VMEM_FLOOR_MB, int(need * 1.25 / 2**20) + 8))
    return pltpu.CompilerParams(vmem_limit_bytes=mb * 2**20)


def _mm(name, kind, a, b, grid, a_spec, b_spec, o_spec, out_shape, out_dtype, blocks,
        red_axis=None, nred=1, alpha=1.0, res=None, res_spec=None, after=None):
    dims = {"nn": NN, "nt": NT, "tn": TN}[kind]
    has_res = res is not None
    acc_in_out = nred > 1 and out_dtype == F32 and not has_res and alpha == 1.0
    n_in = 2 + has_res + (after is not None)

    def body(*refs):
        a_ref, b_ref = refs[0], refs[1]
        r_ref = refs[2] if has_res else None
        o_ref = refs[n_in]
        p = lax.dot_general(a_ref[...].astype(BF16), b_ref[...].astype(BF16), dims,
                            preferred_element_type=F32)

        def finish(acc):
            y = acc if alpha == 1.0 else acc * alpha
            if has_res:
                y = r_ref[...] + y
            o_ref[...] = y.astype(o_ref.dtype)

        if nred == 1:
            finish(p)
            return
        k = pl.program_id(red_axis)
        tgt = o_ref if acc_in_out else refs[-1]

        @pl.when(k == 0)
        def _():
            tgt[...] = p

        @pl.when(k > 0)
        def _():
            tgt[...] += p

        if not acc_in_out:
            @pl.when(k == nred - 1)
            def _():
                finish(tgt[...])

    a_blk, b_blk, o_blk = blocks
    scratch = []
    extra = 0
    if nred > 1 and not acc_in_out:
        scratch = [pltpu.VMEM(o_blk, F32)]
        extra = _nbytes(o_blk, F32)
    blk = _nbytes(a_blk, a.dtype) + _nbytes(b_blk, b.dtype) + _nbytes(o_blk, out_dtype)
    ins, specs = [a, b], [a_spec, b_spec]
    if has_res:
        ins.append(res)
        specs.append(res_spec)
        blk += _nbytes(o_blk, res.dtype)
    if after is not None:
        ins.append(after)
        specs.append(pl.BlockSpec(memory_space=pl.ANY))
    extra += _nbytes(a_blk, BF16) + _nbytes(b_blk, BF16) + 2 * _nbytes(o_blk, F32)
    return pl.pallas_call(
        body, name=name, grid=grid, in_specs=specs, out_specs=o_spec,
        out_shape=jax.ShapeDtypeStruct(out_shape, out_dtype), scratch_shapes=scratch,
        compiler_params=_params(blk, extra),
    )(*ins)


def mm_colw(name, x, w3, out_dtype):
    T, K = x.shape
    _, _, nl = w3.shape
    tm = _tile(T, 1024, 8)
    return _mm(name, "nn", x, w3, (N_CHIPS, T // tm),
               pl.BlockSpec((tm, K), lambda j, i: (i, 0)),
               pl.BlockSpec((None, K, nl), lambda j, i: (j, 0, 0)),
               pl.BlockSpec((tm, nl), lambda j, i: (i, j)),
               (T, N_CHIPS * nl), out_dtype, ((tm, K), (K, nl), (tm, nl)))


def _pair_chip(j):
    return (j % 2) * 2 + j // 2


def mm_colw_t(name, dy, w3, out_dtype, res=None, after=None, pair_layout=False):
    T = dy.shape[0]
    _, K, nl = w3.shape
    tm = _tile(T, 1024, 8)
    chip = _pair_chip if pair_layout else (lambda j: j)
    return _mm(name, "nt", dy, w3, (T // tm, N_CHIPS),
               pl.BlockSpec((tm, nl), lambda i, j: (i, j)),
               pl.BlockSpec((None, K, nl), lambda i, j: (chip(j), 0, 0)),
               pl.BlockSpec((tm, K), lambda i, j: (i, 0)),
               (T, K), out_dtype, ((tm, nl), (K, nl), (tm, K)),
               red_axis=1, nred=N_CHIPS, res=res,
               res_spec=pl.BlockSpec((tm, K), lambda i, j: (i, 0)), after=after)


def mm_dcolw(name, x, dy, after=None, pair_layout=False):
    T, K = x.shape
    nl = dy.shape[1] // N_CHIPS
    tt = _tile(T, 2048, 8)
    chip = _pair_chip if pair_layout else (lambda j: j)
    return _mm(name, "tn", x, dy, (N_CHIPS, T // tt),
               pl.BlockSpec((tt, K), lambda j, t: (t, 0)),
               pl.BlockSpec((tt, nl), lambda j, t: (t, j)),
               pl.BlockSpec((None, K, nl), lambda j, t: (chip(j), 0, 0)),
               (N_CHIPS, K, nl), BF16, ((tt, K), (tt, nl), (K, nl)),
               red_axis=1, nred=T // tt, after=after)


def mm_roww(name, x, w2, out_dtype, res=None, alpha=1.0):
    T, Kt = x.shape
    N = w2.shape[1]
    tm = _tile(T, 512, 8)
    return _mm(name, "nn", x, w2, (T // tm,),
               pl.BlockSpec((tm, Kt), lambda i: (i, 0)),
               pl.BlockSpec((Kt, N), lambda i: (0, 0)),
               pl.BlockSpec((tm, N), lambda i: (i, 0)),
               (T, N), out_dtype, ((tm, Kt), (Kt, N), (tm, N)),
               alpha=alpha, res=res, res_spec=pl.BlockSpec((tm, N), lambda i: (i, 0)))


def mm_roww_t(name, dy, w2, out_dtype, alpha=1.0, after=None):
    T, N = dy.shape
    Kt = w2.shape[0]
    tm = _tile(T, 512, 8)
    tk = _tile(Kt, 1408, LANE)
    return _mm(name, "nt", dy, w2, (Kt // tk, T // tm),
               pl.BlockSpec((tm, N), lambda j, i: (i, 0)),
               pl.BlockSpec((tk, N), lambda j, i: (j, 0)),
               pl.BlockSpec((tm, tk), lambda j, i: (i, j)),
               (T, Kt), out_dtype, ((tm, N), (tk, N), (tm, tk)), alpha=alpha, after=after)


def mm_droww(name, x, dy, alpha=1.0):
    T, Kt = x.shape
    N = dy.shape[1]
    tt = _tile(T, 2048, 8)
    tk = _tile(Kt, 1408, LANE)
    return _mm(name, "tn", x, dy, (Kt // tk, T // tt),
               pl.BlockSpec((tt, tk), lambda j, t: (t, j)),
               pl.BlockSpec((tt, N), lambda j, t: (t, 0)),
               pl.BlockSpec((tk, N), lambda j, t: (j, 0)),
               (Kt, N), BF16, ((tt, tk), (tt, N), (tk, N)),
               red_axis=1, nred=T // tt, alpha=alpha)


def rms_fwd(name, x, g):
    T, D = x.shape
    tm = _tile(T, 512, 8)

    def body(x_ref, g_ref, o_ref):
        xv = x_ref[...]
        r = lax.rsqrt(jnp.mean(xv * xv, axis=-1, keepdims=True) + EPS)
        o_ref[...] = (xv * r * g_ref[...]).astype(o_ref.dtype)

    return pl.pallas_call(
        body, name=name, grid=(T // tm,),
        in_specs=[pl.BlockSpec((tm, D), lambda i: (i, 0)), pl.BlockSpec((1, D), lambda i: (0, 0))],
        out_specs=pl.BlockSpec((tm, D), lambda i: (i, 0)),
        out_shape=jax.ShapeDtypeStruct((T, D), BF16),
        compiler_params=_params(_nbytes((tm, D), F32) * 2, 4 * _nbytes((tm, D), F32)),
    )(x, g.reshape(1, D))


def _rms_bwd_math(xv, gv, dy):
    r = lax.rsqrt(jnp.mean(xv * xv, axis=-1, keepdims=True) + EPS)
    xh = xv * r
    dyg = dy * gv
    dx = r * (dyg - xh * jnp.mean(dyg * xh, axis=-1, keepdims=True))
    dg = jnp.sum(dy * xh, axis=0, keepdims=True)
    return dx, dg


def rms_bwd(name, x, g, dy, dres=None):
    T, D = x.shape
    tm = _tile(T, 256, 8)
    has_res = dres is not None

    def body(*refs):
        x_ref, g_ref, dy_ref = refs[:3]
        r_ref = refs[3] if has_res else None
        dx_ref, dg_ref = refs[-2:]
        dx, dg = _rms_bwd_math(x_ref[...], g_ref[...], dy_ref[...].astype(F32))
        if has_res:
            dx = r_ref[...] + dx
        dx_ref[...] = dx

        @pl.when(pl.program_id(0) == 0)
        def _():
            dg_ref[...] = dg

        @pl.when(pl.program_id(0) > 0)
        def _():
            dg_ref[...] += dg

    row = pl.BlockSpec((tm, D), lambda i: (i, 0))
    vec = pl.BlockSpec((1, D), lambda i: (0, 0))
    ins, specs = [x, g.reshape(1, D), dy], [row, vec, row]
    if has_res:
        ins.append(dres)
        specs.append(row)
    dx, dg = pl.pallas_call(
        body, name=name, grid=(T // tm,), in_specs=specs, out_specs=[row, vec],
        out_shape=[jax.ShapeDtypeStruct((T, D), F32), jax.ShapeDtypeStruct((1, D), F32)],
        compiler_params=_params(_nbytes((tm, D), F32) * 4, 6 * _nbytes((tm, D), F32)),
    )(*ins)
    return dx, dg.reshape(D)


def dx_norm_bwd(name, dy, w3, x, g, dres=None, pair_layout=False, after=None):
    T = dy.shape[0]
    _, K, nl = w3.shape
    tm = _tile(T, 512, 8)
    chip = _pair_chip if pair_layout else (lambda j: j)
    has_res = dres is not None

    def body(*refs):
        dy_ref, w_ref, x_ref, g_ref = refs[:4]
        r_ref = refs[4] if has_res else None
        dx_ref, dg_ref, acc = refs[-3:]
        i, k = pl.program_id(0), pl.program_id(1)
        p = lax.dot_general(dy_ref[...].astype(BF16), w_ref[...], NT, preferred_element_type=F32)

        @pl.when(k == 0)
        def _():
            acc[...] = p

        @pl.when(k > 0)
        def _():
            acc[...] += p

        @pl.when(k == N_CHIPS - 1)
        def _():
            dx, dg = _rms_bwd_math(x_ref[...], g_ref[...], acc[...])
            dx_ref[...] = r_ref[...] + dx if has_res else dx

            @pl.when(i == 0)
            def _():
                dg_ref[...] = dg

            @pl.when(i > 0)
            def _():
                dg_ref[...] += dg

    row = pl.BlockSpec((tm, K), lambda i, j: (i, 0))
    vec = pl.BlockSpec((1, K), lambda i, j: (0, 0))
    ins = [dy, w3, x, g.reshape(1, K)]
    specs = [pl.BlockSpec((tm, nl), lambda i, j: (i, j)),
             pl.BlockSpec((None, K, nl), lambda i, j: (chip(j), 0, 0)), row, vec]
    if has_res:
        ins.append(dres)
        specs.append(row)
    if after is not None:
        ins.append(after)
        specs.append(pl.BlockSpec(memory_space=pl.ANY))
    blk = _nbytes((tm, nl), dy.dtype) + _nbytes((K, nl), BF16) + (2 + has_res) * _nbytes((tm, K), F32)
    dx, dg = pl.pallas_call(
        body, name=name, grid=(T // tm, N_CHIPS), in_specs=specs, out_specs=[row, vec],
        out_shape=[jax.ShapeDtypeStruct((T, K), F32), jax.ShapeDtypeStruct((1, K), F32)],
        scratch_shapes=[pltpu.VMEM((tm, K), F32)],
        compiler_params=_params(blk, 8 * _nbytes((tm, K), F32)),
    )(*ins)
    return dx, dg.reshape(K)


def ffn_in_act(name, x, w3):
    T, K = x.shape
    _, _, nl = w3.shape
    tm = _tile(T, 1024, 8)

    def body(*refs):
        x_ref, wg_ref, wu_ref = refs[:3]
        u_ref, a_ref = refs[-2:]
        xv = x_ref[...]
        g = jnp.dot(xv, wg_ref[...], preferred_element_type=F32)
        up = jnp.dot(xv, wu_ref[...], preferred_element_type=F32)
        u_ref[:, :nl] = g.astype(u_ref.dtype)
        u_ref[:, nl:] = up.astype(u_ref.dtype)
        a_ref[...] = (g * jax.nn.sigmoid(g) * up).astype(a_ref.dtype)

    blk = _nbytes((tm, K), BF16) + 2 * _nbytes((K, nl), BF16) + _nbytes((tm, 3 * nl), BF16)
    return pl.pallas_call(
        body, name=name, grid=(2, T // tm),
        in_specs=[pl.BlockSpec((tm, K), lambda p, i: (i, 0)),
                  pl.BlockSpec((None, K, nl), lambda p, i: (p, 0, 0)),
                  pl.BlockSpec((None, K, nl), lambda p, i: (p + 2, 0, 0))],
        out_specs=[pl.BlockSpec((tm, 2 * nl), lambda p, i: (i, p)), pl.BlockSpec((tm, nl), lambda p, i: (i, p))],
        out_shape=[jax.ShapeDtypeStruct((T, 4 * nl), BF16), jax.ShapeDtypeStruct((T, 2 * nl), BF16)],
        compiler_params=_params(blk, 4 * _nbytes((tm, nl), F32)),
    )(x, w3, w3)


def ffn_dact(name, dh, w_out, u, after=None):
    T, N = dh.shape
    F = w_out.shape[0]
    nl = F // 2
    tm = _tile(T, 512, 8)

    def body(*refs):
        d_ref, w_ref, u_ref = refs[:3]
        o_ref = refs[-1]
        dact = 0.5 * lax.dot_general(d_ref[...].astype(BF16), w_ref[...], NT, preferred_element_type=F32)
        g = u_ref[:, :nl].astype(F32)
        up = u_ref[:, nl:].astype(F32)
        sig = jax.nn.sigmoid(g)
        o_ref[:, :nl] = (dact * up * (sig * (1.0 + g * (1.0 - sig)))).astype(o_ref.dtype)
        o_ref[:, nl:] = (dact * (g * sig)).astype(o_ref.dtype)

    ins = [dh, w_out, u]
    specs = [pl.BlockSpec((tm, N), lambda p, i: (i, 0)), pl.BlockSpec((nl, N), lambda p, i: (p, 0)),
             pl.BlockSpec((tm, 2 * nl), lambda p, i: (i, p))]
    if after is not None:
        ins.append(after)
        specs.append(pl.BlockSpec(memory_space=pl.ANY))
    blk = _nbytes((tm, N), F32) + _nbytes((nl, N), BF16) + 2 * _nbytes((tm, 2 * nl), BF16)
    return pl.pallas_call(
        body, name=name, grid=(2, T // tm), in_specs=specs,
        out_specs=pl.BlockSpec((tm, 2 * nl), lambda p, i: (i, p)),
        out_shape=jax.ShapeDtypeStruct((T, 2 * F), BF16),
        compiler_params=_params(blk, 6 * _nbytes((tm, nl), F32)),
    )(*ins)


def loss_head(name, h, g, target):
    T, D = h.shape
    tm = _tile(T, 256, 8)

    def body(h_ref, g_ref, t_ref, dh_ref, dg_ref, loss_ref):
        xv = h_ref[...]
        gv = g_ref[...]
        r = lax.rsqrt(jnp.mean(xv * xv, axis=-1, keepdims=True) + EPS)
        err = xv * r * gv - t_ref[...]
        part = 0.5 * jnp.sum(jnp.mean(err * err, axis=-1, keepdims=True), axis=0, keepdims=True)
        dx, dg = _rms_bwd_math(xv, gv, err * (1.0 / D))
        dh_ref[...] = dx
        part = jnp.broadcast_to(part, (1, LANE))

        @pl.when(pl.program_id(0) == 0)
        def _():
            dg_ref[...] = dg
            loss_ref[...] = part

        @pl.when(pl.program_id(0) > 0)
        def _():
            dg_ref[...] += dg
            loss_ref[...] += part

    row = pl.BlockSpec((tm, D), lambda i: (i, 0))
    vec = pl.BlockSpec((1, D), lambda i: (0, 0))
    dh, dg, loss = pl.pallas_call(
        body, name=name, grid=(T // tm,), in_specs=[row, vec, row],
        out_specs=[row, vec, pl.BlockSpec((1, LANE), lambda i: (0, 0))],
        out_shape=[jax.ShapeDtypeStruct((T, D), F32), jax.ShapeDtypeStruct((1, D), F32),
                   jax.ShapeDtypeStruct((1, LANE), F32)],
        compiler_params=_params(_nbytes((tm, D), F32) * 3, 6 * _nbytes((tm, D), F32)),
    )(h, g.reshape(1, D), target)
    return dh, dg.reshape(D), loss


def rope_tables(S):
    half = ROPE // 2
    freqs = ROPE_THETA ** (-jnp.arange(half, dtype=F32) / half)
    ang = jnp.arange(S, dtype=F32)[:, None] * freqs[None, :]
    cos, sin = jnp.cos(ang), jnp.sin(ang)
    z = jnp.zeros_like(cos)
    ct = jnp.concatenate([cos, cos, z, z], axis=1)
    s1 = jnp.concatenate([-sin, z, z, z], axis=1)
    s2 = jnp.concatenate([z, sin, z, z], axis=1)
    return ct, s1, s2


def _rope_tile(t, ct, s1, s2):
    return t * ct + pltpu.roll(t, 96, 1) * s1 + pltpu.roll(t, 32, 1) * s2


def _rope_tile_bwd(d, ct, s1, s2):
    return d * ct + pltpu.roll(d * s1, 32, 1) + pltpu.roll(d * s2, 96, 1)


def uq_rope(name, x, w3, tabs, S):
    T, K = x.shape
    _, _, nl = w3.shape
    tm = _tile(S, 512, 8)
    nt = S // tm

    def body(x_ref, w_ref, ct_ref, s1_ref, s2_ref, o_ref):
        q = jnp.dot(x_ref[...], w_ref[...], preferred_element_type=F32)
        ct, s1, s2 = ct_ref[...], s1_ref[...], s2_ref[...]
        for h in range(nl // 256):
            o_ref[:, 256 * h:256 * h + 128] = q[:, 256 * h:256 * h + 128].astype(o_ref.dtype)
            o_ref[:, 256 * h + 128:256 * h + 256] = _rope_tile(q[:, 256 * h + 128:256 * h + 256],
                                                               ct, s1, s2).astype(o_ref.dtype)

    tab = pl.BlockSpec((tm, LANE), lambda j, i: (i % nt, 0))
    blk = _nbytes((tm, K), BF16) + _nbytes((K, nl), BF16) + _nbytes((tm, nl), BF16) + 3 * _nbytes((tm, LANE), F32)
    return pl.pallas_call(
        body, name=name, grid=(N_CHIPS, T // tm),
        in_specs=[pl.BlockSpec((tm, K), lambda j, i: (i, 0)), pl.BlockSpec((None, K, nl), lambda j, i: (j, 0, 0)),
                  tab, tab, tab],
        out_specs=pl.BlockSpec((tm, nl), lambda j, i: (i, j)),
        out_shape=jax.ShapeDtypeStruct((T, N_CHIPS * nl), BF16),
        compiler_params=_params(blk, 4 * _nbytes((tm, nl), F32)),
    )(x, w3, *tabs)


def kvprep_fwd(name, ckr, g, tabs, B, S):
    T, W = ckr.shape
    KVL = W - LANE
    ts = _tile(S, 256, 8)

    def body(x_ref, g_ref, ct_ref, s1_ref, s2_ref, c_ref, k_ref):
        xv = x_ref[0, :, :KVL]
        r = lax.rsqrt(jnp.mean(xv * xv, axis=-1, keepdims=True) + EPS)
        c_ref[0] = (xv * r * g_ref[...]).astype(c_ref.dtype)
        k_ref[0] = _rope_tile(x_ref[0, :, KVL:], ct_ref[...], s1_ref[...], s2_ref[...]).astype(k_ref.dtype)

    tab = pl.BlockSpec((ts, LANE), lambda b, s: (s, 0))
    c, k = pl.pallas_call(
        body, name=name, grid=(B, S // ts),
        in_specs=[pl.BlockSpec((1, ts, W), lambda b, s: (b, s, 0)), pl.BlockSpec((1, KVL), lambda b, s: (0, 0)),
                  tab, tab, tab],
        out_specs=[pl.BlockSpec((1, ts, KVL), lambda b, s: (b, s, 0)),
                   pl.BlockSpec((1, ts, LANE), lambda b, s: (b, s, 0))],
        out_shape=[jax.ShapeDtypeStruct((B, S, KVL), BF16), jax.ShapeDtypeStruct((B, S, LANE), BF16)],
        compiler_params=_params(_nbytes((ts, W), F32) * 2, _nbytes((ts, W), F32) * 2),
    )(ckr.reshape(B, S, W), g.reshape(1, KVL), *tabs)
    return c.reshape(T, KVL), k


def kvprep_bwd(name, ckr, g, dc, dkr, tabs, B, S):
    T, W = ckr.shape
    KVL = W - LANE
    ts = _tile(S, 256, 8)

    def body(x_ref, g_ref, dc_ref, dk_ref, ct_ref, s1_ref, s2_ref, o_ref, dg_ref):
        dx, dg = _rms_bwd_math(x_ref[0, :, :KVL], g_ref[...], dc_ref[0])
        o_ref[0, :, :KVL] = dx
        o_ref[0, :, KVL:] = _rope_tile_bwd(dk_ref[0], ct_ref[...], s1_ref[...], s2_ref[...])
        first = (pl.program_id(0) == 0) & (pl.program_id(1) == 0)

        @pl.when(first)
        def _():
            dg_ref[...] = dg

        @pl.when(jnp.logical_not(first))
        def _():
            dg_ref[...] += dg

    tab = pl.BlockSpec((ts, LANE), lambda b, s: (s, 0))
    vec = pl.BlockSpec((1, KVL), lambda b, s: (0, 0))
    o, dg = pl.pallas_call(
        body, name=name, grid=(B, S // ts),
        in_specs=[pl.BlockSpec((1, ts, W), lambda b, s: (b, s, 0)), vec,
                  pl.BlockSpec((1, ts, KVL), lambda b, s: (b, s, 0)),
                  pl.BlockSpec((1, ts, LANE), lambda b, s: (b, s, 0)), tab, tab, tab],
        out_specs=[pl.BlockSpec((1, ts, W), lambda b, s: (b, s, 0)), vec],
        out_shape=[jax.ShapeDtypeStruct((B, S, W), F32), jax.ShapeDtypeStruct((1, KVL), F32)],
        compiler_params=_params(_nbytes((ts, W), F32) * 4, _nbytes((ts, W), F32) * 4),
    )(ckr.reshape(B, S, W), g.reshape(1, KVL), dc.reshape(B, S, KVL), dkr, *tabs)
    return o.reshape(T, W), dg.reshape(KVL)


DIAGS = 768


def _diag_onehot():
    col = lax.broadcasted_iota(I32, (REL_PAD, DIAGS), 1)
    row = lax.broadcasted_iota(I32, (REL_PAD, DIAGS), 0)
    idx = jnp.clip(PADR + QROWS - 1 - col, -MAX_REL, MAX_REL) + MAX_REL
    return (row == idx).astype(F32)


def rel_bias_tile(name, table):
    H = table.shape[0]
    tpad = jnp.pad(table, ((0, 0), (0, REL_PAD - table.shape[1])))

    def body(t_ref, o_ref):
        g = lax.dot_general(t_ref[...], _diag_onehot(), NN, precision=lax.Precision.HIGHEST,
                            preferred_element_type=F32)
        qc = jnp.right_shift(lax.broadcasted_iota(I32, (QROWS, WIN), 0), CHUNK_SHIFT)
        kc = jnp.right_shift(lax.broadcasted_iota(I32, (QROWS, WIN), 1), CHUNK_SHIFT)
        band = (kc >= qc) & (kc <= qc + LEFT_CHUNKS)
        for h in range(H):
            gb = jnp.broadcast_to(g[h:h + 1, :], (QROWS, DIAGS))
            tile = pltpu.roll(gb, DIAGS - (QROWS - 1), 1, stride=1, stride_axis=0)
            o_ref[h // 2, (h % 2) * QROWS:(h % 2 + 1) * QROWS, :] = jnp.where(band, tile[:, :WIN], NEG_INF)

    return pl.pallas_call(
        body, name=name, out_shape=jax.ShapeDtypeStruct((H // 2, 2 * QROWS, WIN), F32),
        compiler_params=_params(0, 2 * _nbytes((H // 2, 2 * QROWS, WIN), F32)),
    )(tpad)


def rel_bias_grad(name, dbias):
    H = 2 * dbias.shape[0]

    def body(d_ref, o_ref):
        flip = (lax.broadcasted_iota(I32, (QROWS, QROWS), 0) + lax.broadcasted_iota(I32, (QROWS, QROWS), 1)
                == QROWS - 1).astype(F32)
        rows = []
        for h in range(H):
            x = d_ref[h // 2, (h % 2) * QROWS:(h % 2 + 1) * QROWS, :]
            xr = lax.dot_general(flip, x, NN, precision=lax.Precision.HIGHEST, preferred_element_type=F32)
            xp = jnp.concatenate([xr, jnp.zeros((QROWS, DIAGS - WIN), F32)], axis=1)
            y = pltpu.roll(xp, 0, 1, stride=1, stride_axis=0)
            rows.append(jnp.sum(y, axis=0, keepdims=True))
        o_ref[...] = lax.dot_general(jnp.concatenate(rows, axis=0), _diag_onehot(), NT,
                                     precision=lax.Precision.HIGHEST, preferred_element_type=F32)

    return pl.pallas_call(
        body, name=name, out_shape=jax.ShapeDtypeStruct((H, REL_PAD), F32),
        compiler_params=_params(0, 2 * _nbytes(dbias.shape, F32)),
    )(dbias)


def _stack_pair(xp):
    lane = lax.broadcasted_iota(I32, xp.shape, 1)
    z = jnp.zeros_like(xp)
    return jnp.concatenate([jnp.where(lane < HEAD_DIM_A, xp, z), jnp.where(lane >= HEAD_DIM_A, xp, z)], axis=0)


def _unstack_pair(y):
    lane = lax.broadcasted_iota(I32, (QROWS, LANE), 1)
    return jnp.where(lane < HEAD_DIM_A, y[:QROWS], y[QROWS:])


def _attn_a_rowpen(j):
    w = lax.broadcasted_iota(I32, (1, WIN), 1)
    return jnp.where(w >= PADR - QROWS * j, 0.0, NEG_INF).astype(F32)


def _attn_a_load_bias(bias_hbm, bias_v, sem):
    cp = pltpu.make_async_copy(bias_hbm, bias_v, sem)
    cp.start()
    cp.wait()


def _attn_a_load_kv(qkv_hbm, b, kpad, vpad, sem, S, D):
    kpad[0:PADR, :] = jnp.zeros((PADR, D), BF16)
    vpad[0:PADR, :] = jnp.zeros((PADR, D), BF16)
    ck = pltpu.make_async_copy(qkv_hbm.at[b, :, pl.ds(D, D)], kpad.at[pl.ds(PADR, S), :], sem.at[0])
    cv = pltpu.make_async_copy(qkv_hbm.at[b, :, pl.ds(2 * D, D)], vpad.at[pl.ds(PADR, S), :], sem.at[1])
    ck.start()
    cv.start()
    ck.wait()
    cv.wait()


def _attn_a_exp(q2s, kp, bias, pen):
    s = lax.dot_general(q2s, kp, NT, preferred_element_type=F32) + bias + pen
    e = jnp.exp(s - jnp.max(s, axis=-1, keepdims=True))
    return e, 1.0 / jnp.sum(e, axis=-1, keepdims=True)


def attn_a_fwd(name, qkv, bias):
    B, S, D3 = qkv.shape
    D = D3 // 3
    H = D // HEAD_DIM_A
    nb = S // QROWS
    scale = HEAD_DIM_A ** -0.5

    def body(q_ref, bias_hbm, qkv_hbm, o_ref, kpad, vpad, bias_v, sem):
        b, j = pl.program_id(0), pl.program_id(1)

        @pl.when((b == 0) & (j == 0))
        def _():
            _attn_a_load_bias(bias_hbm, bias_v, sem.at[2])

        @pl.when(j == 0)
        def _():
            _attn_a_load_kv(qkv_hbm, b, kpad, vpad, sem, S, D)

        pen = _attn_a_rowpen(j)
        w0 = pl.multiple_of(j * QROWS, QROWS)
        for p in range(H // 2):
            ls = slice(p * LANE, (p + 1) * LANE)
            e, rl = _attn_a_exp(_stack_pair(q_ref[0, :, ls] * scale), kpad[pl.ds(w0, WIN), ls], bias_v[p], pen)
            o2 = jnp.dot(e.astype(BF16), vpad[pl.ds(w0, WIN), ls], preferred_element_type=F32) * rl
            o_ref[0, :, ls] = _unstack_pair(o2).astype(o_ref.dtype)

    scr = 2 * _nbytes((PADR + S, D), BF16) + _nbytes(bias.shape, F32) + 8 * _nbytes((2 * QROWS, WIN), F32)
    return pl.pallas_call(
        body, name=name, grid=(B, nb),
        in_specs=[pl.BlockSpec((1, QROWS, D), lambda b, j: (b, j, 0)),
                  pl.BlockSpec(memory_space=pl.ANY), pl.BlockSpec(memory_space=pl.ANY)],
        out_specs=pl.BlockSpec((1, QROWS, D), lambda b, j: (b, j, 0)),
        out_shape=jax.ShapeDtypeStruct((B, S, D), BF16),
        scratch_shapes=[pltpu.VMEM((PADR + S, D), BF16), pltpu.VMEM((PADR + S, D), BF16),
                        pltpu.VMEM(bias.shape, F32), pltpu.SemaphoreType.DMA((3,))],
        compiler_params=_params(2 * _nbytes((QROWS, D), BF16), scr),
    )(qkv, bias, qkv)


def attn_a_bwd(name, qkv, do, bias):
    B, S, D3 = qkv.shape
    D = D3 // 3
    H = D // HEAD_DIM_A
    nb = S // QROWS
    scale = HEAD_DIM_A ** -0.5

    def body(q_ref, do_ref, bias_hbm, qkv_hbm, dqkv_hbm, dbias_hbm, kpad, vpad, dkacc, dvacc, bias_v, dbias_v,
             dq_stage, sem):
        b, j = pl.program_id(0), pl.program_id(1)
        step = b * nb + j
        slot = lax.rem(step, 2)

        def dq_out(s):
            return pltpu.make_async_copy(dq_stage.at[s], dqkv_hbm.at[b, pl.ds(j * QROWS, QROWS), pl.ds(0, D)],
                                         sem.at[3 + s])

        @pl.when(step >= 2)
        def _():
            dq_out(slot).wait()

        @pl.when((b == 0) & (j == 0))
        def _():
            _attn_a_load_bias(bias_hbm, bias_v, sem.at[2])
            dbias_v[...] = jnp.zeros_like(dbias_v)

        @pl.when(j == 0)
        def _():
            _attn_a_load_kv(qkv_hbm, b, kpad, vpad, sem, S, D)
            dkacc[...] = jnp.zeros_like(dkacc)
            dvacc[...] = jnp.zeros_like(dvacc)

        pen = _attn_a_rowpen(j)
        w0 = pl.multiple_of(j * QROWS, QROWS)
        for p in range(H // 2):
            ls = slice(p * LANE, (p + 1) * LANE)
            q2s = _stack_pair(q_ref[0, :, ls] * scale)
            do2 = _stack_pair(do_ref[0, :, ls])
            kp = kpad[pl.ds(w0, WIN), ls]
            vp = vpad[pl.ds(w0, WIN), ls]
            e, rl = _attn_a_exp(q2s, kp, bias_v[p], pen)
            pr = e * rl
            dp = lax.dot_general(do2, vp, NT, preferred_element_type=F32)
            ds = pr * (dp - jnp.sum(pr * dp, axis=-1, keepdims=True))
            dbias_v[p] += ds
            dsb = ds.astype(BF16)
            dq_stage[slot, :, ls] = (_unstack_pair(jnp.dot(dsb, kp, preferred_element_type=F32))
                                     * scale).astype(dq_stage.dtype)
            dkacc[pl.ds(w0, WIN), ls] += lax.dot_general(dsb, q2s, TN, preferred_element_type=F32)
            dvacc[pl.ds(w0, WIN), ls] += lax.dot_general(pr.astype(BF16), do2, TN, preferred_element_type=F32)

        dq_out(slot).start()

        @pl.when(j == nb - 1)
        def _():
            kpad[pl.ds(PADR, S), :] = dkacc[pl.ds(PADR, S), :].astype(BF16)
            vpad[pl.ds(PADR, S), :] = dvacc[pl.ds(PADR, S), :].astype(BF16)
            ck = pltpu.make_async_copy(kpad.at[pl.ds(PADR, S), :], dqkv_hbm.at[b, :, pl.ds(D, D)], sem.at[0])
            cv = pltpu.make_async_copy(vpad.at[pl.ds(PADR, S), :], dqkv_hbm.at[b, :, pl.ds(2 * D, D)], sem.at[1])
            ck.start()
            cv.start()
            ck.wait()
            cv.wait()

        @pl.when((b == B - 1) & (j == nb - 1))
        def _():
            cb = pltpu.make_async_copy(dbias_v, dbias_hbm, sem.at[2])
            cb.start()
            dq_out(0).wait()
            dq_out(1).wait()
            cb.wait()

    blk = _nbytes((QROWS, D), BF16) * 2
    scr = (2 * _nbytes((PADR + S, D), BF16) + 2 * _nbytes((PADR + S, D), F32) + 2 * _nbytes(bias.shape, F32)
           + 8 * _nbytes((2 * QROWS, WIN), F32) + 2 * _nbytes((QROWS, D), F32))
    return pl.pallas_call(
        body, name=name, grid=(B, nb),
        in_specs=[pl.BlockSpec((1, QROWS, D), lambda b, j: (b, j, 0)),
                  pl.BlockSpec((1, QROWS, D), lambda b, j: (b, j, 0)),
                  pl.BlockSpec(memory_space=pl.ANY), pl.BlockSpec(memory_space=pl.ANY)],
        out_specs=[pl.BlockSpec(memory_space=pl.ANY), pl.BlockSpec(memory_space=pl.ANY)],
        out_shape=[jax.ShapeDtypeStruct((B, S, 3 * D), BF16), jax.ShapeDtypeStruct(bias.shape, F32)],
        scratch_shapes=[pltpu.VMEM((PADR + S, D), BF16), pltpu.VMEM((PADR + S, D), BF16),
                        pltpu.VMEM((PADR + S, D), F32), pltpu.VMEM((PADR + S, D), F32),
                        pltpu.VMEM(bias.shape, F32), pltpu.VMEM(bias.shape, F32),
                        pltpu.VMEM((2, QROWS, D), BF16), pltpu.SemaphoreType.DMA((5,))],
        compiler_params=_params(blk, scr),
    )(qkv, do, bias, qkv)


def _mla_raw_t(k2, kj, q, QB):
    return lax.dot_general(k2[_blk(kj, QB), :], q, NT, preferred_element_type=F32)


def _blk(kj, QB):
    return pl.ds(kj * QB, QB) if isinstance(kj, int) else pl.ds(pl.multiple_of(kj * QB, QB), QB)


def _mla_diag_pen(QB):
    kc = jnp.right_shift(lax.broadcasted_iota(I32, (QB, QB), 0), CHUNK_SHIFT)
    qc = jnp.right_shift(lax.broadcasted_iota(I32, (QB, QB), 1), CHUNK_SHIFT)
    return jnp.where(kc <= qc, 0.0, NEG_INF).astype(F32)


def _mla_fill_keys(kv_ref, kr_ref, k2):
    k2[:, :NOPE] = kv_ref[0, :, :NOPE]
    k2[:, NOPE:] = kr_ref[0]


def _t(x):
    return x.astype(F32).T


def mla_fwd(name, qf, kv, kr):
    B, S, W = qf.shape
    HB = W // 256
    QB = _tile(S, 256, CHUNK)
    nq = S // QB
    scale = (NOPE + ROPE) ** -0.5

    def body(q_ref, kv_ref, kr_ref, o_ref, lse_ref, k2, vt, st_buf, pen):
        qi = pl.program_id(2)

        @pl.when(qi == 0)
        def _():
            pen[...] = _mla_diag_pen(QB)
            _mla_fill_keys(kv_ref, kr_ref, k2)
            for kj in range(nq):
                vt[kj] = _t(kv_ref[0, kj * QB:(kj + 1) * QB, NOPE:]).astype(BF16)

        q = q_ref[0]
        st_buf[0] = _mla_raw_t(k2, 0, q, QB)

        def step(kj, carry):
            m, l, acc = carry
            cur = lax.rem(kj, 2)
            st_raw = st_buf[cur]
            st_buf[1 - cur] = _mla_raw_t(k2, jnp.minimum(kj + 1, qi), q, QB)
            st = st_raw * scale + jnp.where(kj == qi, pen[...], 0.0)
            m_new = jnp.maximum(m, jnp.max(st, axis=0, keepdims=True))
            a = jnp.exp(m - m_new)
            pt = jnp.exp(st - m_new)
            l = a * l + jnp.sum(pt, axis=0, keepdims=True)
            acc = a * acc + jnp.dot(vt[kj], pt.astype(BF16), preferred_element_type=F32)
            return m_new, l, acc

        init = (jnp.full((1, QB), NEG_INF, F32), jnp.zeros((1, QB), F32), jnp.zeros((NOPE, QB), F32))
        m, l, acc = lax.fori_loop(0, qi + 1, step, init)
        o_ref[0] = (acc * (1.0 / l)).T
        lse_ref[0, 0] = m + jnp.log(l)

    blk = (_nbytes((QB, 256), BF16) + _nbytes((S, 256), BF16) + _nbytes((S, LANE), BF16)
           + _nbytes((QB, LANE), F32))
    return pl.pallas_call(
        body, name=name, grid=(B, HB, nq),
        in_specs=[pl.BlockSpec((1, QB, 256), lambda b, h, i: (b, i, h)),
                  pl.BlockSpec((1, S, 256), lambda b, h, i: (b, 0, h)),
                  pl.BlockSpec((1, S, LANE), lambda b, h, i: (b, 0, 0))],
        out_specs=[pl.BlockSpec((1, QB, LANE), lambda b, h, i: (b, i, h)),
                   pl.BlockSpec((1, 1, 1, QB), lambda b, h, i: (b, h, 0, i))],
        out_shape=[jax.ShapeDtypeStruct((B, S, HB * LANE), F32), jax.ShapeDtypeStruct((B, HB, 1, S), F32)],
        scratch_shapes=[pltpu.VMEM((S, 256), BF16), pltpu.VMEM((nq, NOPE, QB), BF16),
                        pltpu.VMEM((2, QB, QB), F32), pltpu.VMEM((QB, QB), F32)],
        compiler_params=_params(blk, 2 * _nbytes((S, 256), BF16) + 10 * _nbytes((QB, QB), F32)),
    )(qf, kv, kr)


def mla_bwd(name, qf, kv, kr, do, o, lse, tabs):
    B, S, W = qf.shape
    HB = W // 256
    QB = _tile(S, 256, CHUNK)
    nq = S // QB
    scale = (NOPE + ROPE) ** -0.5

    def body(q_ref, kv_ref, kr_ref, do_ref, o_ref, lse_ref, ct_ref, s1_ref, s2_ref, dq_ref, dkv_ref, dkr_ref,
             k2, kt, dot_, delta, dqt, st_buf, dp_buf, pen, dkv_acc):
        h = pl.program_id(1)
        pen[...] = _mla_diag_pen(QB)
        dkv_acc[...] = jnp.zeros_like(dkv_acc)

        @pl.when(h == 0)
        def _():
            dkr_ref[...] = jnp.zeros_like(dkr_ref)

        _mla_fill_keys(kv_ref, kr_ref, k2)
        for i in range(nq):
            rows = slice(i * QB, (i + 1) * QB)
            kt[i] = _t(k2[rows, :]).astype(BF16)
            dot32 = _t(do_ref[0, rows, :])
            delta[i] = jnp.sum(dot32 * o_ref[0, rows, :].T, axis=0, keepdims=True)
            dot_[i] = dot32.astype(BF16)

        for qi in range(nq):
            rows = slice(qi * QB, (qi + 1) * QB)
            q = q_ref[0, rows, :]
            dob = do_ref[0, rows, :]
            lse_q = lse_ref[0, 0, :, rows]
            delta_q = delta[qi]
            dqt[...] = jnp.zeros_like(dqt)

            def raw(kj, slot, q=q, qi=qi):
                st_buf[slot] = _mla_raw_t(k2, kj, q, QB)
                dp_buf[slot] = jnp.dot(kv_ref[0, _blk(kj, QB), NOPE:], dot_[qi], preferred_element_type=F32)

            raw(0, 0)

            def step(kj, carry, q=q, dob=dob, lse_q=lse_q, delta_q=delta_q, qi=qi, raw=raw):
                ks = pl.ds(pl.multiple_of(kj * QB, QB), QB)
                cur = lax.rem(kj, 2)
                st_raw, dp_raw = st_buf[cur], dp_buf[cur]
                raw(jnp.minimum(kj + 1, qi), 1 - cur)
                pt = jnp.exp(st_raw * scale + jnp.where(kj == qi, pen[...], 0.0) - lse_q)
                dst = (pt * (dp_raw - delta_q) * scale).astype(BF16)
                dkv_acc[ks, NOPE:] += jnp.dot(pt.astype(BF16), dob, preferred_element_type=F32)
                dk2 = jnp.dot(dst, q, preferred_element_type=F32)
                dkv_acc[ks, :NOPE] += dk2[:, :NOPE]
                dkr_ref[0, ks, :] += dk2[:, NOPE:]
                dqt[...] += jnp.dot(kt[kj], dst, preferred_element_type=F32)
                return carry

            lax.fori_loop(0, qi + 1, step, 0)
            dq = dqt[...].T
            dq_ref[0, rows, :NOPE] = dq[:, :NOPE].astype(dq_ref.dtype)
            dq_ref[0, rows, NOPE:] = _rope_tile_bwd(dq[:, NOPE:], ct_ref[rows, :], s1_ref[rows, :],
                                                    s2_ref[rows, :]).astype(dq_ref.dtype)

        dkv_ref[0] = dkv_acc[...].astype(dkv_ref.dtype)

    head = lambda w: pl.BlockSpec((1, S, w), lambda b, h: (b, 0, h))
    shared = pl.BlockSpec((1, S, LANE), lambda b, h: (b, 0, 0))
    blk = (2 * _nbytes((S, 256), BF16) + 2 * _nbytes((S, LANE), BF16) + _nbytes((S, LANE), F32)
           + 2 * _nbytes((S, 256), F32) + _nbytes((S, LANE), F32))
    scr = 3 * _nbytes((S, 256), BF16) + 14 * _nbytes((QB, QB), F32)
    return pl.pallas_call(
        body, name=name, grid=(B, HB),
        in_specs=[head(256), head(256), shared, head(LANE), head(LANE),
                  pl.BlockSpec((1, 1, 1, S), lambda b, h: (b, h, 0, 0))]
        + [pl.BlockSpec((S, LANE), lambda b, h: (0, 0))] * 3,
        out_specs=[head(256), head(256), shared],
        out_shape=[jax.ShapeDtypeStruct((B, S, W), BF16), jax.ShapeDtypeStruct((B, S, W), BF16),
                   jax.ShapeDtypeStruct((B, S, LANE), F32)],
        scratch_shapes=[pltpu.VMEM((S, 256), BF16), pltpu.VMEM((nq, 256, QB), BF16),
                        pltpu.VMEM((nq, NOPE, QB), BF16), pltpu.VMEM((nq, 1, QB), F32),
                        pltpu.VMEM((256, QB), F32), pltpu.VMEM((2, QB, QB), F32), pltpu.VMEM((2, QB, QB), F32),
                        pltpu.VMEM((QB, QB), F32), pltpu.VMEM((S, 256), F32)],
        compiler_params=_params(blk, scr),
    )(qf, kv, kr, do, o, lse, *tabs)


GROUP_STEPS = 2


def cast_group(name, ws, layers, idx, after=None):
    n = len(ws)
    n_in = n + (after is not None)

    def body(k_ref, *refs):
        for i in range(n):
            refs[n_in + i][...] = refs[i][...].astype(BF16)

    def spec_in(w, layer):
        return pl.BlockSpec((None, w.shape[1] // GROUP_STEPS, w.shape[2]), lambda r, k_ref: (layer, r, 0))

    def spec_out(w):
        return pl.BlockSpec((None, w.shape[1] // GROUP_STEPS, w.shape[2]), lambda r, k_ref: (k_ref[0], r, 0))

    return pl.pallas_call(
        body, name=name,
        grid_spec=pltpu.PrefetchScalarGridSpec(
            num_scalar_prefetch=1, grid=(GROUP_STEPS,),
            in_specs=([spec_in(w, l) for w, l in zip(ws, layers)]
                      + [pl.BlockSpec(memory_space=pl.ANY)] * (after is not None)),
            out_specs=[spec_out(w) for w in ws]),
        out_shape=[jax.ShapeDtypeStruct((N_CHIPS, *w.shape[1:]), BF16) for w in ws],
        compiler_params=_params(sum(_nbytes(w.shape[1:], F32) * 3 // 2 for w in ws) // GROUP_STEPS),
    )(idx, *ws, *([] if after is None else [after]))


def adamw(name, w, g, m, v):
    R, C = w.shape
    tr = _tile(R, max(8, (1 << 18) // C // 8 * 8), 8)
    c1 = 1.0 - ADAM_B1 ** ADAM_STEP
    c2 = 1.0 - ADAM_B2 ** ADAM_STEP

    def body(w_ref, g_ref, m_ref, v_ref, d_ref, mo_ref, vo_ref):
        gv = g_ref[...]
        mn = ADAM_B1 * m_ref[...] + (1.0 - ADAM_B1) * gv
        vn = ADAM_B2 * v_ref[...] + (1.0 - ADAM_B2) * (gv * gv)
        mo_ref[...] = mn
        vo_ref[...] = vn
        d_ref[...] = -ADAM_LR * ((mn / c1) / (jnp.sqrt(vn / c2) + ADAM_EPS) + ADAM_WD * w_ref[...])

    spec = pl.BlockSpec((tr, C), lambda r: (r, 0))
    return pl.pallas_call(
        body, name=name, grid=(R // tr,), in_specs=[spec] * 4, out_specs=[spec] * 3,
        out_shape=[jax.ShapeDtypeStruct((R, C), F32)] * 3,
        compiler_params=_params(7 * _nbytes((tr, C), F32), 4 * _nbytes((tr, C), F32)),
    )(w, g, m, v)


def half_sum_group(name, dws, landed, idx):
    n = len(dws)
    steps = GROUP_STEPS // 2

    def body(i_ref, *refs):
        for i in range(n):
            refs[2 * n + i][...] = (refs[i][...].astype(F32) + refs[n + i][...].astype(F32)).astype(BF16)

    def own(d):
        return pl.BlockSpec((None, None, d.shape[2] // steps, d.shape[3]), lambda k, r, i_ref: (k, i_ref[1], r, 0))

    def flat(d):
        return pl.BlockSpec((None, d.shape[2] // steps, d.shape[3]), lambda k, r, i_ref: (k, r, 0))

    return pl.pallas_call(
        body, name=name,
        grid_spec=pltpu.PrefetchScalarGridSpec(
            num_scalar_prefetch=1, grid=(N_CHIPS, steps),
            in_specs=[own(d) for d in dws] + [flat(d) for d in dws], out_specs=[flat(d) for d in dws]),
        out_shape=[jax.ShapeDtypeStruct((N_CHIPS, *d.shape[2:]), BF16) for d in dws],
        compiler_params=_params(sum(3 * _nbytes(d.shape[2:], BF16) for d in dws) // steps),
    )(idx, *dws, *landed)


def chip_sum_group(name, parts, landed, gbufs, layers, idx):
    n = len(parts)
    steps = GROUP_STEPS // 2

    def body(i_ref, *refs):
        for i in range(n):
            a, b = refs[i], refs[n + i]
            refs[3 * n + i][...] = ((a[...].astype(F32) + b[0].astype(F32)) + b[1].astype(F32)) + b[2].astype(F32)

    def mine(p):
        return pl.BlockSpec((None, p.shape[1] // steps, p.shape[2]), lambda r, i_ref: (i_ref[0], r, 0))

    def three(p):
        return pl.BlockSpec((3, p.shape[1] // steps, p.shape[2]), lambda r, i_ref: (0, r, 0))

    def out(p, layer):
        return pl.BlockSpec((None, None, p.shape[1] // steps, p.shape[2]), lambda r, i_ref: (layer, i_ref[1], r, 0))

    return pl.pallas_call(
        body, name=name,
        grid_spec=pltpu.PrefetchScalarGridSpec(
            num_scalar_prefetch=1, grid=(steps,),
            in_specs=[mine(p) for p in parts] + [three(p) for p in parts] + [pl.BlockSpec(memory_space=pl.ANY)] * n,
            out_specs=[out(p, l) for p, l in zip(parts, layers)]),
        out_shape=[jax.ShapeDtypeStruct(g.shape, F32) for g in gbufs],
        input_output_aliases={1 + 2 * n + i: i for i in range(n)},
        compiler_params=_params(sum(6 * _nbytes(p.shape[1:], BF16) for p in parts) // steps),
    )(idx, *parts, *landed, *gbufs)


ANY = pl.BlockSpec(memory_space=pl.ANY)


def _place():
    x, y, c = lax.axis_index("x"), lax.axis_index("y"), lax.axis_index("c")
    chips = [(1 - x, y), (x, 1 - y), (1 - x, 1 - y)]
    return x, y, c, chips


HBM = pl.BlockSpec(memory_space=pltpu.HBM)
SEM = pl.BlockSpec(memory_space=pltpu.SEMAPHORE)
EFFECT = pltpu.SideEffectType.DATAFLOW_SIDE_EFFECTING


def _in_hbm(a):
    return pltpu.with_memory_space_constraint(a, pltpu.HBM)


def _ici_copy(src, dst, send_sems, recv_sems, k, peer):
    return pltpu.make_async_remote_copy(src_ref=src, dst_ref=dst, send_sem=send_sems.at[k], recv_sem=recv_sems.at[k],
                                        device_id=peer, device_id_type=MESH)


def ici_start(name, bufs, lands, after, gather):
    n, nl = len(bufs), len(lands)

    def body(*refs):
        b_in = refs[:n]
        send_sems, recv_sems = refs[n + nl + 1], refs[n + nl + 2]
        b_out = refs[n + nl + 3:2 * n + nl + 3]
        l_out = refs[2 * n + nl + 3:2 * n + 2 * nl + 3]
        token = refs[-1]
        x, y, c, chips = _place()
        kme = 2 * x + y
        for i in range(n):
            for j in range(3):
                peer = (*chips[j], c)
                if gather:
                    _ici_copy(b_out[i].at[kme, c], b_out[i].at[kme, c], send_sems, recv_sems, 3 * i + j, peer).start()
                else:
                    kd = 2 * chips[j][0] + chips[j][1]
                    _ici_copy(b_out[i].at[kd], l_out[i].at[j], send_sems, recv_sems, 3 * i + j, peer).start()
        token[...] = jnp.zeros_like(token)

    arrays = [*bufs, *lands]
    outs = pl.pallas_call(
        body, name=name,
        in_specs=[HBM] * (n + nl) + [ANY],
        out_specs=(SEM, SEM, *[HBM] * (n + nl), pl.BlockSpec(memory_space=pltpu.VMEM)),
        out_shape=(pltpu.SemaphoreType.DMA((3 * n,)), pltpu.SemaphoreType.DMA((3 * n,)),
                   *[pltpu.HBM(a.shape, a.dtype) for a in arrays], jax.ShapeDtypeStruct((8, LANE), F32)),
        input_output_aliases={i: 2 + i for i in range(n + nl)},
        compiler_params=pltpu.CompilerParams(has_side_effects=EFFECT),
    )(*[_in_hbm(a) for a in arrays], after)
    return outs[0], outs[1], list(outs[2:2 + n]), list(outs[2 + n:2 + n + nl]), outs[-1]


def ici_wait(name, send_sems, recv_sems, bufs, lands, after, gather):
    n, nl = len(bufs), len(lands)

    def body(*refs):
        b_in, l_in = refs[:n], refs[n:n + nl]
        send_sems, recv_sems = refs[n + nl], refs[n + nl + 1]
        x, y, c, chips = _place()
        kme = 2 * x + y
        for i in range(n):
            for j in range(3):
                peer = (*chips[j], c)
                kj = 2 * chips[j][0] + chips[j][1]
                if gather:
                    _ici_copy(b_in[i].at[kme, c], b_in[i].at[kme, c], send_sems, recv_sems, 3 * i + j, peer).wait_send()
                    _ici_copy(b_in[i].at[kj, c], b_in[i].at[kj, c], send_sems, recv_sems, 3 * i + j, peer).wait_recv()
                else:
                    _ici_copy(b_in[i].at[kj], l_in[i].at[j], send_sems, recv_sems, 3 * i + j, peer).wait_send()
                    _ici_copy(b_in[i].at[kj], l_in[i].at[j], send_sems, recv_sems, 3 * i + j, peer).wait_recv()

    arrays = [*bufs, *lands]
    outs = pl.pallas_call(
        body, name=name,
        in_specs=[HBM] * (n + nl) + [SEM, SEM, ANY],
        out_specs=tuple([HBM] * (n + nl)),
        out_shape=tuple(pltpu.HBM(a.shape, a.dtype) for a in arrays),
        input_output_aliases={i: i for i in range(n + nl)},
        compiler_params=pltpu.CompilerParams(has_side_effects=EFFECT),
    )(*arrays, send_sems, recv_sems, after)
    return list(outs[:n]), list(outs[n:])


def gather_pair_pass(name, bufs):
    n = len(bufs)

    def body(*refs):
        b = refs[n:2 * n]
        send_sems, recv_sems = refs[2 * n:]
        x, y, c, chips = _place()
        sib = (x, y, 1 - c)

        def d2d(i, j, which):
            kj = 2 * chips[j][0] + chips[j][1]
            return _ici_copy(b[i].at[kj, which], b[i].at[kj, which], send_sems, recv_sems, 3 * i + j, sib)

        for i in range(n):
            for j in range(3):
                d2d(i, j, c).start()
        for i in range(n):
            for j in range(3):
                d2d(i, j, 1 - c).wait_recv()
        for i in range(n):
            for j in range(3):
                d2d(i, j, c).wait_send()

    return pl.pallas_call(
        body, name=name, in_specs=[ANY] * n, out_specs=[ANY] * n,
        out_shape=[jax.ShapeDtypeStruct(a.shape, a.dtype) for a in bufs],
        input_output_aliases={i: i for i in range(n)},
        scratch_shapes=[pltpu.SemaphoreType.DMA((3 * n,)), pltpu.SemaphoreType.DMA((3 * n,))],
    )(*bufs)


def pair_exchange(name, dws):
    n = len(dws)

    def body(*refs):
        ins, outs = refs[:n], refs[n:2 * n]
        send_sems, recv_sems = refs[2 * n:]
        x, y, c, _ = _place()
        copies = []
        for i in range(n):
            copies.append(pltpu.make_async_remote_copy(
                src_ref=ins[i].at[:, 1 - c], dst_ref=outs[i],
                send_sem=send_sems.at[i], recv_sem=recv_sems.at[i],
                device_id=(x, y, 1 - c), device_id_type=MESH))
            copies[i].start()
        for cp in copies:
            cp.wait_recv()
        for cp in copies:
            cp.wait_send()

    return pl.pallas_call(
        body, name=name, in_specs=[ANY] * n, out_specs=[ANY] * n,
        out_shape=[jax.ShapeDtypeStruct((N_CHIPS, *d.shape[2:]), d.dtype) for d in dws],
        scratch_shapes=[pltpu.SemaphoreType.DMA((n,)), pltpu.SemaphoreType.DMA((n,))],
    )(*dws)


def pair_assemble(gbufs):
    n = len(gbufs)

    def body(*refs):
        bufs = refs[n:2 * n]
        send_sems, recv_sems = refs[2 * n:]
        x, y, c, _ = _place()
        copies = []
        for i in range(n):
            copies.append(pltpu.make_async_remote_copy(
                src_ref=bufs[i].at[:, c], dst_ref=bufs[i].at[:, c],
                send_sem=send_sems.at[i], recv_sem=recv_sems.at[i],
                device_id=(x, y, 1 - c), device_id_type=MESH))
            copies[i].start()
        for i in range(n):
            pltpu.make_async_remote_copy(
                src_ref=bufs[i].at[:, 1 - c], dst_ref=bufs[i].at[:, 1 - c],
                send_sem=send_sems.at[i], recv_sem=recv_sems.at[i],
                device_id=(x, y, 1 - c), device_id_type=MESH).wait_recv()
        for cp in copies:
            cp.wait_send()

    return pl.pallas_call(
        body, name="grad_pair_assemble", in_specs=[ANY] * n, out_specs=[ANY] * n,
        out_shape=[jax.ShapeDtypeStruct(g.shape, g.dtype) for g in gbufs],
        input_output_aliases={i: i for i in range(n)},
        scratch_shapes=[pltpu.SemaphoreType.DMA((n,)), pltpu.SemaphoreType.DMA((n,))],
    )(*gbufs)


def all_reduce_small(vec):
    NR = vec.shape[0]
    flips = [(fx, fy, fc) for fx in (0, 1) for fy in (0, 1) for fc in (0, 1)][1:]

    def body(v_ref, o_ref, buf, send_sems, recv_sems):
        x, y, c, _ = _place()
        me = 4 * x + 2 * y + c
        buf[me] = v_ref[...]
        copies = []
        for j, (fx, fy, fc) in enumerate(flips):
            peer = (1 - x if fx else x, 1 - y if fy else y, 1 - c if fc else c)
            copies.append(pltpu.make_async_remote_copy(
                src_ref=v_ref, dst_ref=buf.at[me], send_sem=send_sems.at[j], recv_sem=recv_sems.at[j],
                device_id=peer, device_id_type=MESH))
            copies[j].start()
        for cp in copies:
            cp.wait_recv()
        for cp in copies:
            cp.wait_send()
        acc = buf[0]
        for d in range(1, 8):
            acc = acc + buf[d]
        o_ref[...] = acc

    return pl.pallas_call(
        body, name="all_reduce_small",
        in_specs=[pl.BlockSpec(memory_space=pltpu.VMEM)], out_specs=pl.BlockSpec(memory_space=pltpu.VMEM),
        out_shape=jax.ShapeDtypeStruct((NR, LANE), F32),
        scratch_shapes=[pltpu.VMEM((8, NR, LANE), F32), pltpu.SemaphoreType.DMA((7,)),
                        pltpu.SemaphoreType.DMA((7,))],
    )(vec)


def _pack(arrays):
    flat = jnp.concatenate([a.reshape(-1).astype(F32) for a in arrays])
    n = flat.shape[0]
    npad = -(-n // (8 * LANE)) * (8 * LANE)
    return jnp.pad(flat, (0, npad - n)).reshape(npad // LANE, LANE)


def _unpack(buf, like):
    flat = buf.reshape(-1)
    out, off = [], 0
    for a in like:
        out.append(flat[off:off + a.size].reshape(a.shape))
        off += a.size
    return out


def kernel(x, ffn1_norm, ffn1_w_in, ffn1_w_out, mix_norm, ffn2_norm, ffn2_w_in, ffn2_w_out, a_w_qkv, a_rel_bias, a_w_o, kv_norm, kv_w_down, kv_latent_norm, kv_w_up, b_w_dq, b_q_norm, b_w_uq, b_w_o, final_norm, loss_target, m_ffn1_norm, m_ffn1_w_in, m_ffn1_w_out, m_mix_norm, m_ffn2_norm, m_ffn2_w_in, m_ffn2_w_out, m_a_w_qkv, m_a_rel_bias, m_a_w_o, m_kv_norm, m_kv_w_down, m_kv_latent_norm, m_kv_w_up, m_b_w_dq, m_b_q_norm, m_b_w_uq, m_b_w_o, m_final_norm, v_ffn1_norm, v_ffn1_w_in, v_ffn1_w_out, v_mix_norm, v_ffn2_norm, v_ffn2_w_in, v_ffn2_w_out, v_a_w_qkv, v_a_rel_bias, v_a_w_o, v_kv_norm, v_kv_w_down, v_kv_latent_norm, v_kv_w_up, v_b_w_dq, v_b_q_norm, v_b_w_uq, v_b_w_o, v_final_norm):
    B, S, D = x.shape
    T = B * S
    HB = D // 128
    QL = b_q_norm.shape[-1]
    KVL = kv_latent_norm.shape[0]
    hpc = HB // N_CHIPS
    tabs = rope_tables(S)
    idx = jnp.stack([2 * lax.axis_index("x") + lax.axis_index("y"), lax.axis_index("c")]).astype(I32)

    def halves(a):
        return a.reshape(*a.shape[:-2], 2, a.shape[-2] // 2, a.shape[-1])

    def whole(a):
        return a.reshape(*a.shape[:-3], 2 * a.shape[-2], a.shape[-1])

    kv_w_down_p = jnp.pad(kv_w_down, ((0, 0), (0, LANE - ROPE)))[None]
    b_w_uq_p = jnp.pad(b_w_uq.reshape(1, QL, hpc, NOPE + ROPE),
                       ((0, 0), (0, 0), (0, 0), (0, LANE - ROPE))).reshape(1, QL, hpc * 256)
    sharded = [("ffn1_w_in", ffn1_w_in), ("ffn1_w_out", ffn1_w_out), ("ffn2_w_in", ffn2_w_in),
               ("ffn2_w_out", ffn2_w_out), ("a_w_qkv", a_w_qkv), ("a_w_o", a_w_o),
               ("kv_w_down", kv_w_down_p), ("kv_w_up", kv_w_up[None]), ("b_w_dq", b_w_dq),
               ("b_w_uq", b_w_uq_p), ("b_w_o", b_w_o)]
    names = [nm for nm, _ in sharded]
    shard_of = dict(sharded)
    W = {}

    gather_groups = [
        [("ffn1_w_in", 0)],
        [("ffn1_w_out", 0)],
        [("a_w_qkv", 0), ("a_w_o", 0)],
        [("ffn2_w_in", 0), ("ffn2_w_out", 0), ("kv_w_down", 0), ("kv_w_up", 0)],
        [("ffn1_w_in", 1), ("ffn1_w_out", 1), ("b_w_dq", 0), ("b_w_uq", 0), ("b_w_o", 0), ("ffn2_w_in", 1),
         ("ffn2_w_out", 1)]]

    own = {}

    def cast(g, after=None):
        keys = gather_groups[g]
        own.update(zip(keys, cast_group(f"cast_group_{g}", [shard_of[nm] for nm, _ in keys], [l for _, l in keys],
                                        idx, after=after)))

    def gather_start(g, after):
        keys = gather_groups[g]
        ss, rs, bufs, _, token = ici_start(f"gather_start_{g}", [halves(own[k]) for k in keys], [], after, True)
        return (g, ss, rs, bufs), token

    def gather_finish(state, after):
        g, ss, rs, bufs = state
        bufs, _ = ici_wait(f"gather_wait_{g}", ss, rs, bufs, [], after, True)
        full = gather_pair_pass(f"gather_pair_{g}", bufs)
        for k, w in zip(gather_groups[g], full):
            W[k] = whole(w)
        return full[0]

    def tied(a, token):
        return a + token[0, 0]

    def col(nm, l=0):
        return W[(nm, l)]

    def row(nm, l=0):
        w = W[(nm, l)]
        return w.reshape(N_CHIPS * w.shape[1], w.shape[2])

    bias = rel_bias_tile("rel_bias_tile", a_rel_bias[0])

    def ffn_fwd(tag, h, g, w_in, w_out):
        xn = rms_fwd(f"{tag}_norm", h, g)
        u, act = ffn_in_act(f"{tag}_in", xn, w_in)
        return mm_roww(f"{tag}_out", act, w_out, F32, res=h, alpha=0.5), (xn, u, act)

    h0 = x.reshape(T, D)
    for g in range(3):
        cast(g)
    st0, tok0 = gather_start(0, h0)
    st1, tok1 = gather_start(1, tok0)
    st2, tok2 = gather_start(2, tok1)
    for g in range(3, len(gather_groups)):
        cast(g, tok2)
    xn0 = rms_fwd("l0f1_norm", h0, tied(ffn1_norm[0], tok2))
    gather_finish(st0, xn0)
    u0, act0 = ffn_in_act("l0f1_in", xn0, col("ffn1_w_in", 0))
    gather_finish(st1, u0)
    h1 = mm_roww("l0f1_out", act0, row("ffn1_w_out", 0), F32, res=h0, alpha=0.5)
    sv_f1a = (xn0, u0, act0)
    done2 = gather_finish(st2, h1)
    st3, tok3 = gather_start(3, done2)
    st4, tok4 = gather_start(4, tok3)
    hn_a = rms_fwd("l0mix_norm", h1, tied(mix_norm[0], tok4))
    qkv = mm_colw("l0_qkv", hn_a, col("a_w_qkv"), BF16).reshape(B, S, 3 * D)
    o_a = attn_a_fwd("l0_attn", qkv, bias).reshape(T, D)
    h2 = mm_roww("l0_attn_out", o_a, row("a_w_o"), F32, res=h1)
    gather_finish(st3, h2)
    h3, sv_f2a = ffn_fwd("l0f2", h2, ffn2_norm[0], col("ffn2_w_in", 0), row("ffn2_w_out", 0))

    hkv = rms_fwd("kv_norm", h3, kv_norm)
    ckr = mm_roww("kv_down", hkv, row("kv_w_down"), F32)
    ckv, kr = kvprep_fwd("kv_prep", ckr, kv_latent_norm, tabs, B, S)
    kvb = mm_colw("kv_up", ckv, col("kv_w_up"), BF16).reshape(B, S, HB * 256)
    gather_finish(st4, kvb)

    h4, sv_f1b = ffn_fwd("l1f1", h3, ffn1_norm[1], col("ffn1_w_in", 1), row("ffn1_w_out", 1))
    hn_b = rms_fwd("l1mix_norm", h4, mix_norm[1])
    cqp = mm_roww("l1_dq", hn_b, row("b_w_dq"), F32)
    cq = rms_fwd("l1_q_norm", cqp, b_q_norm[0])
    qf = uq_rope("l1_uq", cq, col("b_w_uq"), tabs, S).reshape(B, S, HB * 256)
    o_b, lse = mla_fwd("l1_attn", qf, kvb, kr)
    h5 = mm_roww("l1_attn_out", o_b.reshape(T, HB * LANE), row("b_w_o"), F32, res=h4)
    h6, sv_f2b = ffn_fwd("l1f2", h5, ffn2_norm[1], col("ffn2_w_in", 1), row("ffn2_w_out", 1))

    dh, g_final, loss_part = loss_head("loss_head", h6, final_norm, loss_target.reshape(T, D))

    gw = {}
    gbufs = {nm: lax.empty(halves(w).shape, F32) for nm, w in sharded}

    def reduce_start(r, keys, after):
        dws = [halves(gw[k]) for k in keys]
        landed = pair_exchange(f"grad_pair_exchange_{r}", dws)
        parts = half_sum_group(f"half_sum_{r}", dws, landed, idx)
        lands = [lax.empty((3, *p.shape[1:]), p.dtype) for p in parts]
        ss, rs, parts, lands, token = ici_start(f"reduce_start_{r}", parts, lands, after, False)
        return (r, keys, ss, rs, parts, lands), token

    def reduce_finish(state, after):
        r, keys, ss, rs, parts, lands = state
        parts, lands = ici_wait(f"reduce_wait_{r}", ss, rs, parts, lands, after, False)
        done = chip_sum_group(f"chip_sum_{r}", parts, lands, [gbufs[nm] for nm, _ in keys], [l for _, l in keys], idx)
        gbufs.update(zip([nm for nm, _ in keys], done))
        return done[0]

    def ffn_bwd(tag, dh, h_in, g, w_in, w_out, saved, key_in, key_out, after=None, then=None):
        xn, u, act = saved
        du = ffn_dact(f"{tag}_dact", dh, w_out, u, after=after)
        dwo = mm_droww(f"{tag}_dwout", act, dh, alpha=0.5)
        gw[key_out] = dwo.reshape(N_CHIPS, dwo.shape[0] // N_CHIPS, dwo.shape[1])
        gw[key_in] = mm_dcolw(f"{tag}_dwin", xn, du, pair_layout=True)
        token = then(du) if then is not None else None
        return dx_norm_bwd(f"{tag}_dxn", du, w_in, h_in, g, dres=dh, pair_layout=True, after=token)

    def chip_major(dw):
        return dw.reshape(N_CHIPS, dw.shape[0] // N_CHIPS, dw.shape[1])

    dh, g_f2b = ffn_bwd("l1f2b", dh, h5, ffn2_norm[1], col("ffn2_w_in", 1), row("ffn2_w_out", 1), sv_f2b,
                        ("ffn2_w_in", 1), ("ffn2_w_out", 1))
    red0, rtok0 = reduce_start(0, [("ffn2_w_in", 1), ("ffn2_w_out", 1)], dh)
    do_b = mm_roww_t("l1_attn_do", dh, row("b_w_o"), BF16, after=rtok0).reshape(B, S, HB * LANE)
    gw[("b_w_o", 0)] = chip_major(mm_droww("l1_attn_dwo", o_b.reshape(T, HB * LANE), dh))
    dqpre, dkv, dkr = mla_bwd("l1_attn_bwd", qf, kvb, kr, do_b, o_b, lse, tabs)
    dqpre = dqpre.reshape(T, HB * 256)
    gw[("b_w_uq", 0)] = mm_dcolw("l1_dwuq", cq, dqpre)
    dcqp, g_qn = dx_norm_bwd("l1_dcq", dqpre, col("b_w_uq"), cqp, b_q_norm[0])
    gw[("b_w_dq", 0)] = chip_major(mm_droww("l1_dwdq", hn_b, dcqp))
    dhn = mm_roww_t("l1_dhn", dcqp, row("b_w_dq"), F32)
    dh, g_mixb = rms_bwd("l1_dmix", h4, mix_norm[1], dhn, dres=dh)
    dh, g_f1b = ffn_bwd("l1f1b", dh, h3, ffn1_norm[1], col("ffn1_w_in", 1), row("ffn1_w_out", 1), sv_f1b,
                        ("ffn1_w_in", 1), ("ffn1_w_out", 1))
    fin0 = reduce_finish(red0, dh)
    red1, rtok1 = reduce_start(1, [("b_w_o", 0), ("b_w_uq", 0), ("b_w_dq", 0), ("ffn1_w_in", 1), ("ffn1_w_out", 1)], fin0)
    dkv2 = dkv.reshape(T, HB * 256)
    gw[("kv_w_up", 0)] = mm_dcolw("kv_dwup", ckv, dkv2, after=rtok1)
    dckv = mm_colw_t("kv_dckv", dkv2, col("kv_w_up"), F32, after=rtok1)
    dckr, g_lat = kvprep_bwd("kv_prep_bwd", ckr, kv_latent_norm, dckv, dkr, tabs, B, S)
    gw[("kv_w_down", 0)] = chip_major(mm_droww("kv_dwdown", hkv, dckr))
    dhkv = mm_roww_t("kv_dhkv", dckr, row("kv_w_down"), F32)
    dh, g_kvn = rms_bwd("kv_dnorm", h3, kv_norm, dhkv, dres=dh)
    dh, g_f2a = ffn_bwd("l0f2b", dh, h2, ffn2_norm[0], col("ffn2_w_in", 0), row("ffn2_w_out", 0), sv_f2a,
                        ("ffn2_w_in", 0), ("ffn2_w_out", 0))
    do_a = mm_roww_t("l0_attn_do", dh, row("a_w_o"), BF16).reshape(B, S, D)
    gw[("a_w_o", 0)] = chip_major(mm_droww("l0_attn_dwo", o_a, dh))
    dqkv, dbias = attn_a_bwd("l0_attn_bwd", qkv, do_a, bias)
    dqkv = dqkv.reshape(T, 3 * D)
    gw[("a_w_qkv", 0)] = mm_dcolw("l0_dwqkv", hn_a, dqkv)
    dh, g_mixa = dx_norm_bwd("l0_dhn", dqkv, col("a_w_qkv"), h1, mix_norm[0], dres=dh)
    fin1 = reduce_finish(red1, dh)
    red2, rtok2 = reduce_start(2, [("kv_w_up", 0), ("kv_w_down", 0), ("ffn2_w_in", 0), ("ffn2_w_out", 0),
                                   ("a_w_o", 0), ("a_w_qkv", 0)], fin1)
    last = {}

    def last_group(du):
        fin2 = reduce_finish(red2, gw[("ffn1_w_in", 0)])
        last["red"], token = reduce_start(3, [("ffn1_w_in", 0), ("ffn1_w_out", 0)], fin2)
        return token

    dh, g_f1a = ffn_bwd("l0f1b", dh, h0, ffn1_norm[0], col("ffn1_w_in", 0), row("ffn1_w_out", 0), sv_f1a,
                        ("ffn1_w_in", 0), ("ffn1_w_out", 0), after=rtok2, then=last_group)
    grad_x = dh.reshape(B, S, D)
    g_rel = rel_bias_grad("rel_bias_grad", dbias)[:, :2 * MAX_REL + 1][None]
    reduce_finish(last["red"], dh)

    full = [whole(g) for g in pair_assemble([gbufs[nm] for nm in names])]
    G = {nm: g for (nm, _), g in zip(sharded, full)}
    G["kv_w_down"] = G["kv_w_down"][0, :, :KVL + ROPE]
    G["kv_w_up"] = G["kv_w_up"][0]
    G["b_w_uq"] = G["b_w_uq"].reshape(1, QL, hpc, 256)[..., :NOPE + ROPE].reshape(b_w_uq.shape)

    small = [("ffn1_norm", jnp.stack([g_f1a, g_f1b])), ("mix_norm", jnp.stack([g_mixa, g_mixb])),
             ("ffn2_norm", jnp.stack([g_f2a, g_f2b])), ("a_rel_bias", g_rel), ("kv_norm", g_kvn),
             ("kv_latent_norm", g_lat), ("b_q_norm", g_qn[None]), ("final_norm", g_final)]
    red = all_reduce_small(_pack([loss_part] + [g for _, g in small]))
    unpacked = _unpack(red, [loss_part] + [g for _, g in small])
    loss = unpacked[0][0, 0]
    for (nm, _), g in zip(small, unpacked[1:]):
        G[nm] = g

    given = dict(ffn1_norm=(ffn1_norm, m_ffn1_norm, v_ffn1_norm), ffn1_w_in=(ffn1_w_in, m_ffn1_w_in, v_ffn1_w_in),
                 ffn1_w_out=(ffn1_w_out, m_ffn1_w_out, v_ffn1_w_out), mix_norm=(mix_norm, m_mix_norm, v_mix_norm),
                 ffn2_norm=(ffn2_norm, m_ffn2_norm, v_ffn2_norm), ffn2_w_in=(ffn2_w_in, m_ffn2_w_in, v_ffn2_w_in),
                 ffn2_w_out=(ffn2_w_out, m_ffn2_w_out, v_ffn2_w_out), a_w_qkv=(a_w_qkv, m_a_w_qkv, v_a_w_qkv),
                 a_rel_bias=(a_rel_bias, m_a_rel_bias, v_a_rel_bias), a_w_o=(a_w_o, m_a_w_o, v_a_w_o),
                 kv_norm=(kv_norm, m_kv_norm, v_kv_norm), kv_w_down=(kv_w_down, m_kv_w_down, v_kv_w_down),
                 kv_latent_norm=(kv_latent_norm, m_kv_latent_norm, v_kv_latent_norm),
                 kv_w_up=(kv_w_up, m_kv_w_up, v_kv_w_up), b_w_dq=(b_w_dq, m_b_w_dq, v_b_w_dq),
                 b_q_norm=(b_q_norm, m_b_q_norm, v_b_q_norm), b_w_uq=(b_w_uq, m_b_w_uq, v_b_w_uq),
                 b_w_o=(b_w_o, m_b_w_o, v_b_w_o), final_norm=(final_norm, m_final_norm, v_final_norm))
    order = list(given)
    delta, new_m, new_v = {}, {}, {}
    small_names = [nm for nm, _ in small]
    packed = [_pack([given[nm][k] for nm in small_names]) for k in range(3)]
    outs = adamw("adamw_small", packed[0], _pack([G[nm] for nm in small_names]), packed[1], packed[2])
    for dst, buf in zip((delta, new_m, new_v), outs):
        for nm, a in zip(small_names, _unpack(buf, [given[nm][0] for nm in small_names])):
            dst[nm] = a
    for nm, _ in sharded:
        w, m, v = given[nm]
        g = G[nm].reshape(w.shape)
        G[nm] = g
        two = lambda a: a.reshape(-1, a.shape[-1])
        d_, m_, v_ = adamw(f"adamw_{nm}", two(w), two(g), two(m), two(v))
        delta[nm], new_m[nm], new_v[nm] = d_.reshape(w.shape), m_.reshape(w.shape), v_.reshape(w.shape)

    return (loss, grad_x, *[G[n] for n in order], *[delta[n] for n in order],
            *[new_m[n] for n in order], *[new_v[n] for n in order])
```

```python
import math

import jax
import jax.numpy as jnp
from jax import lax
from jax.experimental import pallas as pl
from jax.experimental.pallas import tpu as pltpu

F32 = jnp.float32
BF16 = jnp.bfloat16
I32 = jnp.int32

CHUNK = 64
CHUNK_SHIFT = 6
HEAD_DIM_A = 64
LEFT_CHUNKS = 8
MAX_REL = 128
REL_PAD = 384
QROWS = 2 * CHUNK
WIN = (LEFT_CHUNKS + 2) * CHUNK
PADR = LEFT_CHUNKS * CHUNK
NOPE = 128
ROPE = 64
EPS = 1e-6
NEG_INF = -1e30
ROPE_THETA = 10000.0
ADAM_LR, ADAM_B1, ADAM_B2, ADAM_EPS, ADAM_WD, ADAM_STEP = 0.001, 0.9, 0.999, 1e-08, 0.01, 10
N_CHIPS = 4
LANE = 128
MESH = pl.DeviceIdType.MESH
VMEM_CAP_MB = 60
VMEM_FLOOR_MB = 48

NN = (((1,), (0,)), ((), ()))
NT = (((1,), (1,)), ((), ()))
TN = (((0,), (0,)), ((), ()))


def _tile(n, pref, mult):
    t = (min(pref, n) // mult) * mult
    while t >= mult:
        if n % t == 0:
            return t
        t -= mult
    return n


def _nbytes(shape, dtype):
    return math.prod(shape) * jnp.dtype(dtype).itemsize


def _params(block_bytes, extra_bytes=0):
    need = 2 * block_bytes + extra_bytes
    mb = min(VMEM_CAP_MB, max(VMEM_FLOOR_MB, int(need * 1.25 / 2**20) + 8))
    return pltpu.CompilerParams(vmem_limit_bytes=mb * 2**20)


def _mm(name, kind, a, b, grid, a_spec, b_spec, o_spec, out_shape, out_dtype, blocks,
        red_axis=None, nred=1, alpha=1.0, res=None, res_spec=None, after=None):
    dims = {"nn": NN, "nt": NT, "tn": TN}[kind]
    has_res = res is not None
    acc_in_out = nred > 1 and out_dtype == F32 and not has_res and alpha == 1.0
    n_in = 2 + has_res + (after is not None)

    def body(*refs):
        a_ref, b_ref = refs[0], refs[1]
        r_ref = refs[2] if has_res else None
        o_ref = refs[n_in]
        p = lax.dot_general(a_ref[...].astype(BF16), b_ref[...].astype(BF16), dims,
                            preferred_element_type=F32)

        def finish(acc):
            y = acc if alpha == 1.0 else acc * alpha
            if has_res:
                y = r_ref[...] + y
            o_ref[...] = y.astype(o_ref.dtype)

        if nred == 1:
            finish(p)
            return
        k = pl.program_id(red_axis)
        tgt = o_ref if acc_in_out else refs[-1]

        @pl.when(k == 0)
        def _():
            tgt[...] = p

        @pl.when(k > 0)
        def _():
            tgt[...] += p

        if not acc_in_out:
            @pl.when(k == nred - 1)
            def _():
                finish(tgt[...])

    a_blk, b_blk, o_blk = blocks
    scratch = []
    extra = 0
    if nred > 1 and not acc_in_out:
        scratch = [pltpu.VMEM(o_blk, F32)]
        extra = _nbytes(o_blk, F32)
    blk = _nbytes(a_blk, a.dtype) + _nbytes(b_blk, b.dtype) + _nbytes(o_blk, out_dtype)
    ins, specs = [a, b], [a_spec, b_spec]
    if has_res:
        ins.append(res)
        specs.append(res_spec)
        blk += _nbytes(o_blk, res.dtype)
    if after is not None:
        ins.append(after)
        specs.append(pl.BlockSpec(memory_space=pl.ANY))
    extra += _nbytes(a_blk, BF16) + _nbytes(b_blk, BF16) + 2 * _nbytes(o_blk, F32)
    return pl.pallas_call(
        body, name=name, grid=grid, in_specs=specs, out_specs=o_spec,
        out_shape=jax.ShapeDtypeStruct(out_shape, out_dtype), scratch_shapes=scratch,
        compiler_params=_params(blk, extra),
    )(*ins)


def mm_colw(name, x, w3, out_dtype):
    T, K = x.shape
    _, _, nl = w3.shape
    tm = _tile(T, 1024, 8)
    return _mm(name, "nn", x, w3, (N_CHIPS, T // tm),
               pl.BlockSpec((tm, K), lambda j, i: (i, 0)),
               pl.BlockSpec((None, K, nl), lambda j, i: (j, 0, 0)),
               pl.BlockSpec((tm, nl), lambda j, i: (i, j)),
               (T, N_CHIPS * nl), out_dtype, ((tm, K), (K, nl), (tm, nl)))


def _pair_chip(j):
    return (j % 2) * 2 + j // 2


def mm_colw_t(name, dy, w3, out_dtype, res=None, after=None, pair_layout=False):
    T = dy.shape[0]
    _, K, nl = w3.shape
    tm = _tile(T, 1024, 8)
    chip = _pair_chip if pair_layout else (lambda j: j)
    return _mm(name, "nt", dy, w3, (T // tm, N_CHIPS),
               pl.BlockSpec((tm, nl), lambda i, j: (i, j)),
               pl.BlockSpec((None, K, nl), lambda i, j: (chip(j), 0, 0)),
               pl.BlockSpec((tm, K), lambda i, j: (i, 0)),
               (T, K), out_dtype, ((tm, nl), (K, nl), (tm, K)),
               red_axis=1, nred=N_CHIPS, res=res,
               res_spec=pl.BlockSpec((tm, K), lambda i, j: (i, 0)), after=after)


def mm_dcolw(name, x, dy, after=None, pair_layout=False):
    T, K = x.shape
    nl = dy.shape[1] // N_CHIPS
    tt = _tile(T, 2048, 8)
    chip = _pair_chip if pair_layout else (lambda j: j)
    return _mm(name, "tn", x, dy, (N_CHIPS, T // tt),
               pl.BlockSpec((tt, K), lambda j, t: (t, 0)),
               pl.BlockSpec((tt, nl), lambda j, t: (t, j)),
               pl.BlockSpec((None, K, nl), lambda j, t: (chip(j), 0, 0)),
               (N_CHIPS, K, nl), BF16, ((tt, K), (tt, nl), (K, nl)),
               red_axis=1, nred=T // tt, after=after)


def mm_roww(name, x, w2, out_dtype, res=None, alpha=1.0):
    T, Kt = x.shape
    N = w2.shape[1]
    tm = _tile(T, 512, 8)
    return _mm(name, "nn", x, w2, (T // tm,),
               pl.BlockSpec((tm, Kt), lambda i: (i, 0)),
               pl.BlockSpec((Kt, N), lambda i: (0, 0)),
               pl.BlockSpec((tm, N), lambda i: (i, 0)),
               (T, N), out_dtype, ((tm, Kt), (Kt, N), (tm, N)),
               alpha=alpha, res=res, res_spec=pl.BlockSpec((tm, N), lambda i: (i, 0)))


def mm_roww_t(name, dy, w2, out_dtype, alpha=1.0, after=None):
    T, N = dy.shape
    Kt = w2.shape[0]
    tm = _tile(T, 1024, 8)
    tk = _tile(Kt, 1408, LANE)
    return _mm(name, "nt", dy, w2, (Kt // tk, T // tm),
               pl.BlockSpec((tm, N), lambda j, i: (i, 0)),
               pl.BlockSpec((tk, N), lambda j, i: (j, 0)),
               pl.BlockSpec((tm, tk), lambda j, i: (i, j)),
               (T, Kt), out_dtype, ((tm, N), (tk, N), (tm, tk)), alpha=alpha, after=after)


def mm_droww(name, x, dy, alpha=1.0):
    T, Kt = x.shape
    N = dy.shape[1]
    tt = _tile(T, 2048, 8)
    tk = _tile(Kt, 1408, LANE)
    return _mm(name, "tn", x, dy, (Kt // tk, T // tt),
               pl.BlockSpec((tt, tk), lambda j, t: (t, j)),
               pl.BlockSpec((tt, N), lambda j, t: (t, 0)),
               pl.BlockSpec((tk, N), lambda j, t: (j, 0)),
               (Kt, N), BF16, ((tt, tk), (tt, N), (tk, N)),
               red_axis=1, nred=T // tt, alpha=alpha)


def rms_fwd(name, x, g):
    T, D = x.shape
    tm = _tile(T, 512, 8)

    def body(x_ref, g_ref, o_ref):
        xv = x_ref[...]
        r = lax.rsqrt(jnp.mean(xv * xv, axis=-1, keepdims=True) + EPS)
        o_ref[...] = (xv * r * g_ref[...]).astype(o_ref.dtype)

    return pl.pallas_call(
        body, name=name, grid=(T // tm,),
        in_specs=[pl.BlockSpec((tm, D), lambda i: (i, 0)), pl.BlockSpec((1, D), lambda i: (0, 0))],
        out_specs=pl.BlockSpec((tm, D), lambda i: (i, 0)),
        out_shape=jax.ShapeDtypeStruct((T, D), BF16),
        compiler_params=_params(_nbytes((tm, D), F32) * 2, 4 * _nbytes((tm, D), F32)),
    )(x, g.reshape(1, D))


def _rms_bwd_math(xv, gv, dy):
    r = lax.rsqrt(jnp.mean(xv * xv, axis=-1, keepdims=True) + EPS)
    xh = xv * r
    dyg = dy * gv
    dx = r * (dyg - xh * jnp.mean(dyg * xh, axis=-1, keepdims=True))
    dg = jnp.sum(dy * xh, axis=0, keepdims=True)
    return dx, dg


def rms_bwd(name, x, g, dy, dres=None):
    T, D = x.shape
    tm = _tile(T, 256, 8)
    has_res = dres is not None

    def body(*refs):
        x_ref, g_ref, dy_ref = refs[:3]
        r_ref = refs[3] if has_res else None
        dx_ref, dg_ref = refs[-2:]
        dx, dg = _rms_bwd_math(x_ref[...], g_ref[...], dy_ref[...].astype(F32))
        if has_res:
            dx = r_ref[...] + dx
        dx_ref[...] = dx

        @pl.when(pl.program_id(0) == 0)
        def _():
            dg_ref[...] = dg

        @pl.when(pl.program_id(0) > 0)
        def _():
            dg_ref[...] += dg

    row = pl.BlockSpec((tm, D), lambda i: (i, 0))
    vec = pl.BlockSpec((1, D), lambda i: (0, 0))
    ins, specs = [x, g.reshape(1, D), dy], [row, vec, row]
    if has_res:
        ins.append(dres)
        specs.append(row)
    dx, dg = pl.pallas_call(
        body, name=name, grid=(T // tm,), in_specs=specs, out_specs=[row, vec],
        out_shape=[jax.ShapeDtypeStruct((T, D), F32), jax.ShapeDtypeStruct((1, D), F32)],
        compiler_params=_params(_nbytes((tm, D), F32) * 4, 6 * _nbytes((tm, D), F32)),
    )(*ins)
    return dx, dg.reshape(D)


def dx_norm_bwd(name, dy, w3, x, g, dres=None, pair_layout=False, after=None):
    T = dy.shape[0]
    _, K, nl = w3.shape
    tm = _tile(T, 512, 8)
    chip = _pair_chip if pair_layout else (lambda j: j)
    has_res = dres is not None

    def body(*refs):
        dy_ref, w_ref, x_ref, g_ref = refs[:4]
        r_ref = refs[4] if has_res else None
        dx_ref, dg_ref, acc = refs[-3:]
        i, k = pl.program_id(0), pl.program_id(1)
        p = lax.dot_general(dy_ref[...].astype(BF16), w_ref[...], NT, preferred_element_type=F32)

        @pl.when(k == 0)
        def _():
            acc[...] = p

        @pl.when(k > 0)
        def _():
            acc[...] += p

        @pl.when(k == N_CHIPS - 1)
        def _():
            dx, dg = _rms_bwd_math(x_ref[...], g_ref[...], acc[...])
            dx_ref[...] = r_ref[...] + dx if has_res else dx

            @pl.when(i == 0)
            def _():
                dg_ref[...] = dg

            @pl.when(i > 0)
            def _():
                dg_ref[...] += dg

    row = pl.BlockSpec((tm, K), lambda i, j: (i, 0))
    vec = pl.BlockSpec((1, K), lambda i, j: (0, 0))
    ins = [dy, w3, x, g.reshape(1, K)]
    specs = [pl.BlockSpec((tm, nl), lambda i, j: (i, j)),
             pl.BlockSpec((None, K, nl), lambda i, j: (chip(j), 0, 0)), row, vec]
    if has_res:
        ins.append(dres)
        specs.append(row)
    if after is not None:
        ins.append(after)
        specs.append(pl.BlockSpec(memory_space=pl.ANY))
    blk = _nbytes((tm, nl), dy.dtype) + _nbytes((K, nl), BF16) + (2 + has_res) * _nbytes((tm, K), F32)
    dx, dg = pl.pallas_call(
        body, name=name, grid=(T // tm, N_CHIPS), in_specs=specs, out_specs=[row, vec],
        out_shape=[jax.ShapeDtypeStruct((T, K), F32), jax.ShapeDtypeStruct((1, K), F32)],
        scratch_shapes=[pltpu.VMEM((tm, K), F32)],
        compiler_params=_params(blk, 8 * _nbytes((tm, K), F32)),
    )(*ins)
    return dx, dg.reshape(K)


def ffn_in_act(name, x, w3):
    T, K = x.shape
    _, _, nl = w3.shape
    tm = _tile(T, 1024, 8)

    def body(*refs):
        x_ref, wg_ref, wu_ref = refs[:3]
        u_ref, a_ref = refs[-2:]
        xv = x_ref[...]
        g = jnp.dot(xv, wg_ref[...], preferred_element_type=F32)
        up = jnp.dot(xv, wu_ref[...], preferred_element_type=F32)
        u_ref[:, :nl] = g.astype(u_ref.dtype)
        u_ref[:, nl:] = up.astype(u_ref.dtype)
        a_ref[...] = (g * jax.nn.sigmoid(g) * up).astype(a_ref.dtype)

    blk = _nbytes((tm, K), BF16) + 2 * _nbytes((K, nl), BF16) + _nbytes((tm, 3 * nl), BF16)
    return pl.pallas_call(
        body, name=name, grid=(2, T // tm),
        in_specs=[pl.BlockSpec((tm, K), lambda p, i: (i, 0)),
                  pl.BlockSpec((None, K, nl), lambda p, i: (p, 0, 0)),
                  pl.BlockSpec((None, K, nl), lambda p, i: (p + 2, 0, 0))],
        out_specs=[pl.BlockSpec((tm, 2 * nl), lambda p, i: (i, p)), pl.BlockSpec((tm, nl), lambda p, i: (i, p))],
        out_shape=[jax.ShapeDtypeStruct((T, 4 * nl), BF16), jax.ShapeDtypeStruct((T, 2 * nl), BF16)],
        compiler_params=_params(blk, 4 * _nbytes((tm, nl), F32)),
    )(x, w3, w3)


def ffn_dact(name, dh, w_out, u, after=None):
    T, N = dh.shape
    F = w_out.shape[0]
    nl = F // 2
    tm = _tile(T, 512, 8)

    def body(*refs):
        d_ref, w_ref, u_ref = refs[:3]
        o_ref = refs[-1]
        dact = 0.5 * lax.dot_general(d_ref[...].astype(BF16), w_ref[...], NT, preferred_element_type=F32)
        g = u_ref[:, :nl].astype(F32)
        up = u_ref[:, nl:].astype(F32)
        sig = jax.nn.sigmoid(g)
        o_ref[:, :nl] = (dact * up * (sig * (1.0 + g * (1.0 - sig)))).astype(o_ref.dtype)
        o_ref[:, nl:] = (dact * (g * sig)).astype(o_ref.dtype)

    ins = [dh, w_out, u]
    specs = [pl.BlockSpec((tm, N), lambda p, i: (i, 0)), pl.BlockSpec((nl, N), lambda p, i: (p, 0)),
             pl.BlockSpec((tm, 2 * nl), lambda p, i: (i, p))]
    if after is not None:
        ins.append(after)
        specs.append(pl.BlockSpec(memory_space=pl.ANY))
    blk = _nbytes((tm, N), F32) + _nbytes((nl, N), BF16) + 2 * _nbytes((tm, 2 * nl), BF16)
    return pl.pallas_call(
        body, name=name, grid=(2, T // tm), in_specs=specs,
        out_specs=pl.BlockSpec((tm, 2 * nl), lambda p, i: (i, p)),
        out_shape=jax.ShapeDtypeStruct((T, 2 * F), BF16),
        compiler_params=_params(blk, 6 * _nbytes((tm, nl), F32)),
    )(*ins)


def loss_head(name, h, g, target):
    T, D = h.shape
    tm = _tile(T, 256, 8)

    def body(h_ref, g_ref, t_ref, dh_ref, dg_ref, loss_ref):
        xv = h_ref[...]
        gv = g_ref[...]
        r = lax.rsqrt(jnp.mean(xv * xv, axis=-1, keepdims=True) + EPS)
        err = xv * r * gv - t_ref[...]
        part = 0.5 * jnp.sum(jnp.mean(err * err, axis=-1, keepdims=True), axis=0, keepdims=True)
        dx, dg = _rms_bwd_math(xv, gv, err * (1.0 / D))
        dh_ref[...] = dx
        part = jnp.broadcast_to(part, (1, LANE))

        @pl.when(pl.program_id(0) == 0)
        def _():
            dg_ref[...] = dg
            loss_ref[...] = part

        @pl.when(pl.program_id(0) > 0)
        def _():
            dg_ref[...] += dg
            loss_ref[...] += part

    row = pl.BlockSpec((tm, D), lambda i: (i, 0))
    vec = pl.BlockSpec((1, D), lambda i: (0, 0))
    dh, dg, loss = pl.pallas_call(
        body, name=name, grid=(T // tm,), in_specs=[row, vec, row],
        out_specs=[row, vec, pl.BlockSpec((1, LANE), lambda i: (0, 0))],
        out_shape=[jax.ShapeDtypeStruct((T, D), F32), jax.ShapeDtypeStruct((1, D), F32),
                   jax.ShapeDtypeStruct((1, LANE), F32)],
        compiler_params=_params(_nbytes((tm, D), F32) * 3, 6 * _nbytes((tm, D), F32)),
    )(h, g.reshape(1, D), target)
    return dh, dg.reshape(D), loss


def rope_tables(S):
    half = ROPE // 2
    freqs = ROPE_THETA ** (-jnp.arange(half, dtype=F32) / half)
    ang = jnp.arange(S, dtype=F32)[:, None] * freqs[None, :]
    cos, sin = jnp.cos(ang), jnp.sin(ang)
    z = jnp.zeros_like(cos)
    ct = jnp.concatenate([cos, cos, z, z], axis=1)
    s1 = jnp.concatenate([-sin, z, z, z], axis=1)
    s2 = jnp.concatenate([z, sin, z, z], axis=1)
    return ct, s1, s2


def _rope_tile(t, ct, s1, s2):
    return t * ct + pltpu.roll(t, 96, 1) * s1 + pltpu.roll(t, 32, 1) * s2


def _rope_tile_bwd(d, ct, s1, s2):
    return d * ct + pltpu.roll(d * s1, 32, 1) + pltpu.roll(d * s2, 96, 1)


def uq_rope(name, x, w3, tabs, S):
    T, K = x.shape
    _, _, nl = w3.shape
    tm = _tile(S, 512, 8)
    nt = S // tm

    def body(x_ref, w_ref, ct_ref, s1_ref, s2_ref, o_ref):
        q = jnp.dot(x_ref[...], w_ref[...], preferred_element_type=F32)
        ct, s1, s2 = ct_ref[...], s1_ref[...], s2_ref[...]
        for h in range(nl // 256):
            o_ref[:, 256 * h:256 * h + 128] = q[:, 256 * h:256 * h + 128].astype(o_ref.dtype)
            o_ref[:, 256 * h + 128:256 * h + 256] = _rope_tile(q[:, 256 * h + 128:256 * h + 256],
                                                               ct, s1, s2).astype(o_ref.dtype)

    tab = pl.BlockSpec((tm, LANE), lambda j, i: (i % nt, 0))
    blk = _nbytes((tm, K), BF16) + _nbytes((K, nl), BF16) + _nbytes((tm, nl), BF16) + 3 * _nbytes((tm, LANE), F32)
    return pl.pallas_call(
        body, name=name, grid=(N_CHIPS, T // tm),
        in_specs=[pl.BlockSpec((tm, K), lambda j, i: (i, 0)), pl.BlockSpec((None, K, nl), lambda j, i: (j, 0, 0)),
                  tab, tab, tab],
        out_specs=pl.BlockSpec((tm, nl), lambda j, i: (i, j)),
        out_shape=jax.ShapeDtypeStruct((T, N_CHIPS * nl), BF16),
        compiler_params=_params(blk, 4 * _nbytes((tm, nl), F32)),
    )(x, w3, *tabs)


def kvprep_fwd(name, ckr, g, tabs, B, S):
    T, W = ckr.shape
    KVL = W - LANE
    ts = _tile(S, 256, 8)

    def body(x_ref, g_ref, ct_ref, s1_ref, s2_ref, c_ref, k_ref):
        xv = x_ref[0, :, :KVL]
        r = lax.rsqrt(jnp.mean(xv * xv, axis=-1, keepdims=True) + EPS)
        c_ref[0] = (xv * r * g_ref[...]).astype(c_ref.dtype)
        k_ref[0] = _rope_tile(x_ref[0, :, KVL:], ct_ref[...], s1_ref[...], s2_ref[...]).astype(k_ref.dtype)

    tab = pl.BlockSpec((ts, LANE), lambda b, s: (s, 0))
    c, k = pl.pallas_call(
        body, name=name, grid=(B, S // ts),
        in_specs=[pl.BlockSpec((1, ts, W), lambda b, s: (b, s, 0)), pl.BlockSpec((1, KVL), lambda b, s: (0, 0)),
                  tab, tab, tab],
        out_specs=[pl.BlockSpec((1, ts, KVL), lambda b, s: (b, s, 0)),
                   pl.BlockSpec((1, ts, LANE), lambda b, s: (b, s, 0))],
        out_shape=[jax.ShapeDtypeStruct((B, S, KVL), BF16), jax.ShapeDtypeStruct((B, S, LANE), BF16)],
        compiler_params=_params(_nbytes((ts, W), F32) * 2, _nbytes((ts, W), F32) * 2),
    )(ckr.reshape(B, S, W), g.reshape(1, KVL), *tabs)
    return c.reshape(T, KVL), k


def kvprep_bwd(name, ckr, g, dc, dkr, tabs, B, S):
    T, W = ckr.shape
    KVL = W - LANE
    ts = _tile(S, 256, 8)

    def body(x_ref, g_ref, dc_ref, dk_ref, ct_ref, s1_ref, s2_ref, o_ref, dg_ref):
        dx, dg = _rms_bwd_math(x_ref[0, :, :KVL], g_ref[...], dc_ref[0])
        o_ref[0, :, :KVL] = dx
        o_ref[0, :, KVL:] = _rope_tile_bwd(dk_ref[0], ct_ref[...], s1_ref[...], s2_ref[...])
        first = (pl.program_id(0) == 0) & (pl.program_id(1) == 0)

        @pl.when(first)
        def _():
            dg_ref[...] = dg

        @pl.when(jnp.logical_not(first))
        def _():
            dg_ref[...] += dg

    tab = pl.BlockSpec((ts, LANE), lambda b, s: (s, 0))
    vec = pl.BlockSpec((1, KVL), lambda b, s: (0, 0))
    o, dg = pl.pallas_call(
        body, name=name, grid=(B, S // ts),
        in_specs=[pl.BlockSpec((1, ts, W), lambda b, s: (b, s, 0)), vec,
                  pl.BlockSpec((1, ts, KVL), lambda b, s: (b, s, 0)),
                  pl.BlockSpec((1, ts, LANE), lambda b, s: (b, s, 0)), tab, tab, tab],
        out_specs=[pl.BlockSpec((1, ts, W), lambda b, s: (b, s, 0)), vec],
        out_shape=[jax.ShapeDtypeStruct((B, S, W), F32), jax.ShapeDtypeStruct((1, KVL), F32)],
        compiler_params=_params(_nbytes((ts, W), F32) * 4, _nbytes((ts, W), F32) * 4),
    )(ckr.reshape(B, S, W), g.reshape(1, KVL), dc.reshape(B, S, KVL), dkr, *tabs)
    return o.reshape(T, W), dg.reshape(KVL)


DIAGS = 768


def _diag_onehot():
    col = lax.broadcasted_iota(I32, (REL_PAD, DIAGS), 1)
    row = lax.broadcasted_iota(I32, (REL_PAD, DIAGS), 0)
    idx = jnp.clip(PADR + QROWS - 1 - col, -MAX_REL, MAX_REL) + MAX_REL
    return (row == idx).astype(F32)


def rel_bias_tile(name, table):
    H = table.shape[0]
    tpad = jnp.pad(table, ((0, 0), (0, REL_PAD - table.shape[1])))

    def body(t_ref, o_ref):
        g = lax.dot_general(t_ref[...], _diag_onehot(), NN, precision=lax.Precision.HIGHEST,
                            preferred_element_type=F32)
        qc = jnp.right_shift(lax.broadcasted_iota(I32, (QROWS, WIN), 0), CHUNK_SHIFT)
        kc = jnp.right_shift(lax.broadcasted_iota(I32, (QROWS, WIN), 1), CHUNK_SHIFT)
        band = (kc >= qc) & (kc <= qc + LEFT_CHUNKS)
        for h in range(H):
            gb = jnp.broadcast_to(g[h:h + 1, :], (QROWS, DIAGS))
            tile = pltpu.roll(gb, DIAGS - (QROWS - 1), 1, stride=1, stride_axis=0)
            o_ref[h // 2, (h % 2) * QROWS:(h % 2 + 1) * QROWS, :] = jnp.where(band, tile[:, :WIN], NEG_INF)

    return pl.pallas_call(
        body, name=name, out_shape=jax.ShapeDtypeStruct((H // 2, 2 * QROWS, WIN), F32),
        compiler_params=_params(0, 2 * _nbytes((H // 2, 2 * QROWS, WIN), F32)),
    )(tpad)


def rel_bias_grad(name, dbias):
    H = 2 * dbias.shape[0]

    def body(d_ref, o_ref):
        flip = (lax.broadcasted_iota(I32, (QROWS, QROWS), 0) + lax.broadcasted_iota(I32, (QROWS, QROWS), 1)
                == QROWS - 1).astype(F32)
        rows = []
        for h in range(H):
            x = d_ref[h // 2, (h % 2) * QROWS:(h % 2 + 1) * QROWS, :]
            xr = lax.dot_general(flip, x, NN, precision=lax.Precision.HIGHEST, preferred_element_type=F32)
            xp = jnp.concatenate([xr, jnp.zeros((QROWS, DIAGS - WIN), F32)], axis=1)
            y = pltpu.roll(xp, 0, 1, stride=1, stride_axis=0)
            rows.append(jnp.sum(y, axis=0, keepdims=True))
        o_ref[...] = lax.dot_general(jnp.concatenate(rows, axis=0), _diag_onehot(), NT,
                                     precision=lax.Precision.HIGHEST, preferred_element_type=F32)

    return pl.pallas_call(
        body, name=name, out_shape=jax.ShapeDtypeStruct((H, REL_PAD), F32),
        compiler_params=_params(0, 2 * _nbytes(dbias.shape, F32)),
    )(dbias)


def _stack_pair(xp):
    lane = lax.broadcasted_iota(I32, xp.shape, 1)
    z = jnp.zeros_like(xp)
    return jnp.concatenate([jnp.where(lane < HEAD_DIM_A, xp, z), jnp.where(lane >= HEAD_DIM_A, xp, z)], axis=0)


def _unstack_pair(y):
    lane = lax.broadcasted_iota(I32, (QROWS, LANE), 1)
    return jnp.where(lane < HEAD_DIM_A, y[:QROWS], y[QROWS:])


def _attn_a_rowpen(j):
    w = lax.broadcasted_iota(I32, (1, WIN), 1)
    return jnp.where(w >= PADR - QROWS * j, 0.0, NEG_INF).astype(F32)


def _attn_a_load_bias(bias_hbm, bias_v, sem):
    cp = pltpu.make_async_copy(bias_hbm, bias_v, sem)
    cp.start()
    cp.wait()


def _attn_a_load_kv(qkv_hbm, b, kpad, vpad, sem, S, D):
    kpad[0:PADR, :] = jnp.zeros((PADR, D), BF16)
    vpad[0:PADR, :] = jnp.zeros((PADR, D), BF16)
    ck = pltpu.make_async_copy(qkv_hbm.at[b, :, pl.ds(D, D)], kpad.at[pl.ds(PADR, S), :], sem.at[0])
    cv = pltpu.make_async_copy(qkv_hbm.at[b, :, pl.ds(2 * D, D)], vpad.at[pl.ds(PADR, S), :], sem.at[1])
    ck.start()
    cv.start()
    ck.wait()
    cv.wait()


def _attn_a_exp(q2s, kp, bias, pen):
    s = lax.dot_general(q2s, kp, NT, preferred_element_type=F32) + bias + pen
    e = jnp.exp(s - jnp.max(s, axis=-1, keepdims=True))
    return e, 1.0 / jnp.sum(e, axis=-1, keepdims=True)


def attn_a_fwd(name, qkv, bias):
    B, S, D3 = qkv.shape
    D = D3 // 3
    H = D // HEAD_DIM_A
    nb = S // QROWS
    scale = HEAD_DIM_A ** -0.5

    def body(q_ref, bias_hbm, qkv_hbm, o_ref, kpad, vpad, bias_v, sem):
        b, j = pl.program_id(0), pl.program_id(1)

        @pl.when((b == 0) & (j == 0))
        def _():
            _attn_a_load_bias(bias_hbm, bias_v, sem.at[2])

        @pl.when(j == 0)
        def _():
            _attn_a_load_kv(qkv_hbm, b, kpad, vpad, sem, S, D)

        pen = _attn_a_rowpen(j)
        w0 = pl.multiple_of(j * QROWS, QROWS)
        for p in range(H // 2):
            ls = slice(p * LANE, (p + 1) * LANE)
            e, rl = _attn_a_exp(_stack_pair(q_ref[0, :, ls] * scale), kpad[pl.ds(w0, WIN), ls], bias_v[p], pen)
            o2 = jnp.dot(e.astype(BF16), vpad[pl.ds(w0, WIN), ls], preferred_element_type=F32) * rl
            o_ref[0, :, ls] = _unstack_pair(o2).astype(o_ref.dtype)

    scr = 2 * _nbytes((PADR + S, D), BF16) + _nbytes(bias.shape, F32) + 8 * _nbytes((2 * QROWS, WIN), F32)
    return pl.pallas_call(
        body, name=name, grid=(B, nb),
        in_specs=[pl.BlockSpec((1, QROWS, D), lambda b, j: (b, j, 0)),
                  pl.BlockSpec(memory_space=pl.ANY), pl.BlockSpec(memory_space=pl.ANY)],
        out_specs=pl.BlockSpec((1, QROWS, D), lambda b, j: (b, j, 0)),
        out_shape=jax.ShapeDtypeStruct((B, S, D), BF16),
        scratch_shapes=[pltpu.VMEM((PADR + S, D), BF16), pltpu.VMEM((PADR + S, D), BF16),
                        pltpu.VMEM(bias.shape, F32), pltpu.SemaphoreType.DMA((3,))],
        compiler_params=_params(2 * _nbytes((QROWS, D), BF16), scr),
    )(qkv, bias, qkv)


def attn_a_bwd(name, qkv, do, bias):
    B, S, D3 = qkv.shape
    D = D3 // 3
    H = D // HEAD_DIM_A
    nb = S // QROWS
    scale = HEAD_DIM_A ** -0.5

    def body(q_ref, do_ref, bias_hbm, qkv_hbm, dqkv_hbm, dbias_hbm, kpad, vpad, dkacc, dvacc, bias_v, dbias_v,
             dq_stage, sem):
        b, j = pl.program_id(0), pl.program_id(1)
        step = b * nb + j
        slot = lax.rem(step, 2)

        def dq_out(s):
            return pltpu.make_async_copy(dq_stage.at[s], dqkv_hbm.at[b, pl.ds(j * QROWS, QROWS), pl.ds(0, D)],
                                         sem.at[3 + s])

        @pl.when(step >= 2)
        def _():
            dq_out(slot).wait()

        @pl.when((b == 0) & (j == 0))
        def _():
            _attn_a_load_bias(bias_hbm, bias_v, sem.at[2])
            dbias_v[...] = jnp.zeros_like(dbias_v)

        @pl.when(j == 0)
        def _():
            _attn_a_load_kv(qkv_hbm, b, kpad, vpad, sem, S, D)
            dkacc[...] = jnp.zeros_like(dkacc)
            dvacc[...] = jnp.zeros_like(dvacc)

        pen = _attn_a_rowpen(j)
        w0 = pl.multiple_of(j * QROWS, QROWS)
        for p in range(H // 2):
            ls = slice(p * LANE, (p + 1) * LANE)
            q2s = _stack_pair(q_ref[0, :, ls] * scale)
            do2 = _stack_pair(do_ref[0, :, ls])
            kp = kpad[pl.ds(w0, WIN), ls]
            vp = vpad[pl.ds(w0, WIN), ls]
            e, rl = _attn_a_exp(q2s, kp, bias_v[p], pen)
            pr = e * rl
            dp = lax.dot_general(do2, vp, NT, preferred_element_type=F32)
            ds = pr * (dp - jnp.sum(pr * dp, axis=-1, keepdims=True))
            dbias_v[p] += ds
            dsb = ds.astype(BF16)
            dq_stage[slot, :, ls] = (_unstack_pair(jnp.dot(dsb, kp, preferred_element_type=F32))
                                     * scale).astype(dq_stage.dtype)
            dkacc[pl.ds(w0, WIN), ls] += lax.dot_general(dsb, q2s, TN, preferred_element_type=F32)
            dvacc[pl.ds(w0, WIN), ls] += lax.dot_general(pr.astype(BF16), do2, TN, preferred_element_type=F32)

        dq_out(slot).start()

        @pl.when(j == nb - 1)
        def _():
            kpad[pl.ds(PADR, S), :] = dkacc[pl.ds(PADR, S), :].astype(BF16)
            vpad[pl.ds(PADR, S), :] = dvacc[pl.ds(PADR, S), :].astype(BF16)
            ck = pltpu.make_async_copy(kpad.at[pl.ds(PADR, S), :], dqkv_hbm.at[b, :, pl.ds(D, D)], sem.at[0])
            cv = pltpu.make_async_copy(vpad.at[pl.ds(PADR, S), :], dqkv_hbm.at[b, :, pl.ds(2 * D, D)], sem.at[1])
            ck.start()
            cv.start()
            ck.wait()
            cv.wait()

        @pl.when((b == B - 1) & (j == nb - 1))
        def _():
            cb = pltpu.make_async_copy(dbias_v, dbias_hbm, sem.at[2])
            cb.start()
            dq_out(0).wait()
            dq_out(1).wait()
            cb.wait()

    blk = _nbytes((QROWS, D), BF16) * 2
    scr = (2 * _nbytes((PADR + S, D), BF16) + 2 * _nbytes((PADR + S, D), F32) + 2 * _nbytes(bias.shape, F32)
           + 8 * _nbytes((2 * QROWS, WIN), F32) + 2 * _nbytes((QROWS, D), F32))
    return pl.pallas_call(
        body, name=name, grid=(B, nb),
        in_specs=[pl.BlockSpec((1, QROWS, D), lambda b, j: (b, j, 0)),
                  pl.BlockSpec((1, QROWS, D), lambda b, j: (b, j, 0)),
                  pl.BlockSpec(memory_space=pl.ANY), pl.BlockSpec(memory_space=pl.ANY)],
        out_specs=[pl.BlockSpec(memory_space=pl.ANY), pl.BlockSpec(memory_space=pl.ANY)],
        out_shape=[jax.ShapeDtypeStruct((B, S, 3 * D), BF16), jax.ShapeDtypeStruct(bias.shape, F32)],
        scratch_shapes=[pltpu.VMEM((PADR + S, D), BF16), pltpu.VMEM((PADR + S, D), BF16),
                        pltpu.VMEM((PADR + S, D), F32), pltpu.VMEM((PADR + S, D), F32),
                        pltpu.VMEM(bias.shape, F32), pltpu.VMEM(bias.shape, F32),
                        pltpu.VMEM((2, QROWS, D), BF16), pltpu.SemaphoreType.DMA((5,))],
        compiler_params=_params(blk, scr),
    )(qkv, do, bias, qkv)


def _mla_raw_t(k2, kj, q, QB):
    return lax.dot_general(k2[_blk(kj, QB), :], q, NT, preferred_element_type=F32)


def _blk(kj, QB):
    return pl.ds(kj * QB, QB) if isinstance(kj, int) else pl.ds(pl.multiple_of(kj * QB, QB), QB)


def _mla_diag_pen(QB):
    kc = jnp.right_shift(lax.broadcasted_iota(I32, (QB, QB), 0), CHUNK_SHIFT)
    qc = jnp.right_shift(lax.broadcasted_iota(I32, (QB, QB), 1), CHUNK_SHIFT)
    return jnp.where(kc <= qc, 0.0, NEG_INF).astype(F32)


def _mla_fill_keys(kv_ref, kr_ref, k2):
    k2[:, :NOPE] = kv_ref[0, :, :NOPE]
    k2[:, NOPE:] = kr_ref[0]


def _t(x):
    return x.astype(F32).T


def mla_fwd(name, qf, kv, kr):
    B, S, W = qf.shape
    HB = W // 256
    QB = _tile(S, 256, CHUNK)
    nq = S // QB
    scale = (NOPE + ROPE) ** -0.5

    def body(q_ref, kv_ref, kr_ref, o_ref, lse_ref, k2, vt, st_buf, pen):
        qi = pl.program_id(2)

        @pl.when(qi == 0)
        def _():
            pen[...] = _mla_diag_pen(QB)
            _mla_fill_keys(kv_ref, kr_ref, k2)
            for kj in range(nq):
                vt[kj] = _t(kv_ref[0, kj * QB:(kj + 1) * QB, NOPE:]).astype(BF16)

        q = q_ref[0]
        st_buf[0] = _mla_raw_t(k2, 0, q, QB)

        def step(kj, carry):
            m, l, acc = carry
            cur = lax.rem(kj, 2)
            st_raw = st_buf[cur]
            st_buf[1 - cur] = _mla_raw_t(k2, jnp.minimum(kj + 1, qi), q, QB)
            st = st_raw * scale + jnp.where(kj == qi, pen[...], 0.0)
            m_new = jnp.maximum(m, jnp.max(st, axis=0, keepdims=True))
            a = jnp.exp(m - m_new)
            pt = jnp.exp(st - m_new)
            l = a * l + jnp.sum(pt, axis=0, keepdims=True)
            acc = a * acc + jnp.dot(vt[kj], pt.astype(BF16), preferred_element_type=F32)
            return m_new, l, acc

        init = (jnp.full((1, QB), NEG_INF, F32), jnp.zeros((1, QB), F32), jnp.zeros((NOPE, QB), F32))
        m, l, acc = lax.fori_loop(0, qi + 1, step, init)
        o_ref[0] = (acc * (1.0 / l)).T
        lse_ref[0, 0] = m + jnp.log(l)

    blk = (_nbytes((QB, 256), BF16) + _nbytes((S, 256), BF16) + _nbytes((S, LANE), BF16)
           + _nbytes((QB, LANE), F32))
    return pl.pallas_call(
        body, name=name, grid=(B, HB, nq),
        in_specs=[pl.BlockSpec((1, QB, 256), lambda b, h, i: (b, i, h)),
                  pl.BlockSpec((1, S, 256), lambda b, h, i: (b, 0, h)),
                  pl.BlockSpec((1, S, LANE), lambda b, h, i: (b, 0, 0))],
        out_specs=[pl.BlockSpec((1, QB, LANE), lambda b, h, i: (b, i, h)),
                   pl.BlockSpec((1, 1, 1, QB), lambda b, h, i: (b, h, 0, i))],
        out_shape=[jax.ShapeDtypeStruct((B, S, HB * LANE), F32), jax.ShapeDtypeStruct((B, HB, 1, S), F32)],
        scratch_shapes=[pltpu.VMEM((S, 256), BF16), pltpu.VMEM((nq, NOPE, QB), BF16),
                        pltpu.VMEM((2, QB, QB), F32), pltpu.VMEM((QB, QB), F32)],
        compiler_params=_params(blk, 2 * _nbytes((S, 256), BF16) + 10 * _nbytes((QB, QB), F32)),
    )(qf, kv, kr)


def mla_bwd(name, qf, kv, kr, do, o, lse, tabs):
    B, S, W = qf.shape
    HB = W // 256
    QB = _tile(S, 256, CHUNK)
    nq = S // QB
    scale = (NOPE + ROPE) ** -0.5

    def body(q_ref, kv_ref, kr_ref, do_ref, o_ref, lse_ref, ct_ref, s1_ref, s2_ref, dq_ref, dkv_ref, dkr_ref,
             k2, kt, dot_, delta, dqt, st_buf, dp_buf, pen, dkv_acc):
        h = pl.program_id(1)
        pen[...] = _mla_diag_pen(QB)
        dkv_acc[...] = jnp.zeros_like(dkv_acc)

        @pl.when(h == 0)
        def _():
            dkr_ref[...] = jnp.zeros_like(dkr_ref)

        _mla_fill_keys(kv_ref, kr_ref, k2)
        for i in range(nq):
            rows = slice(i * QB, (i + 1) * QB)
            kt[i] = _t(k2[rows, :]).astype(BF16)
            dot32 = _t(do_ref[0, rows, :])
            delta[i] = jnp.sum(dot32 * o_ref[0, rows, :].T, axis=0, keepdims=True)
            dot_[i] = dot32.astype(BF16)

        for qi in range(nq):
            rows = slice(qi * QB, (qi + 1) * QB)
            q = q_ref[0, rows, :]
            dob = do_ref[0, rows, :]
            lse_q = lse_ref[0, 0, :, rows]
            delta_q = delta[qi]
            dqt[...] = jnp.zeros_like(dqt)

            def raw(kj, slot, q=q, qi=qi):
                st_buf[slot] = _mla_raw_t(k2, kj, q, QB)
                dp_buf[slot] = jnp.dot(kv_ref[0, _blk(kj, QB), NOPE:], dot_[qi], preferred_element_type=F32)

            raw(0, 0)

            def step(kj, carry, q=q, dob=dob, lse_q=lse_q, delta_q=delta_q, qi=qi, raw=raw):
                ks = pl.ds(pl.multiple_of(kj * QB, QB), QB)
                cur = lax.rem(kj, 2)
                st_raw, dp_raw = st_buf[cur], dp_buf[cur]
                raw(jnp.minimum(kj + 1, qi), 1 - cur)
                pt = jnp.exp(st_raw * scale + jnp.where(kj == qi, pen[...], 0.0) - lse_q)
                dst = (pt * (dp_raw - delta_q) * scale).astype(BF16)
                dkv_acc[ks, NOPE:] += jnp.dot(pt.astype(BF16), dob, preferred_element_type=F32)
                dk2 = jnp.dot(dst, q, preferred_element_type=F32)
                dkv_acc[ks, :NOPE] += dk2[:, :NOPE]
                dkr_ref[0, ks, :] += dk2[:, NOPE:]
                dqt[...] += jnp.dot(kt[kj], dst, preferred_element_type=F32)
                return carry

            lax.fori_loop(0, qi + 1, step, 0)
            dq = dqt[...].T
            dq_ref[0, rows, :NOPE] = dq[:, :NOPE].astype(dq_ref.dtype)
            dq_ref[0, rows, NOPE:] = _rope_tile_bwd(dq[:, NOPE:], ct_ref[rows, :], s1_ref[rows, :],
                                                    s2_ref[rows, :]).astype(dq_ref.dtype)

        dkv_ref[0] = dkv_acc[...].astype(dkv_ref.dtype)

    head = lambda w: pl.BlockSpec((1, S, w), lambda b, h: (b, 0, h))
    shared = pl.BlockSpec((1, S, LANE), lambda b, h: (b, 0, 0))
    blk = (2 * _nbytes((S, 256), BF16) + 2 * _nbytes((S, LANE), BF16) + _nbytes((S, LANE), F32)
           + 2 * _nbytes((S, 256), F32) + _nbytes((S, LANE), F32))
    scr = 3 * _nbytes((S, 256), BF16) + 14 * _nbytes((QB, QB), F32)
    return pl.pallas_call(
        body, name=name, grid=(B, HB),
        in_specs=[head(256), head(256), shared, head(LANE), head(LANE),
                  pl.BlockSpec((1, 1, 1, S), lambda b, h: (b, h, 0, 0))]
        + [pl.BlockSpec((S, LANE), lambda b, h: (0, 0))] * 3,
        out_specs=[head(256), head(256), shared],
        out_shape=[jax.ShapeDtypeStruct((B, S, W), BF16), jax.ShapeDtypeStruct((B, S, W), BF16),
                   jax.ShapeDtypeStruct((B, S, LANE), F32)],
        scratch_shapes=[pltpu.VMEM((S, 256), BF16), pltpu.VMEM((nq, 256, QB), BF16),
                        pltpu.VMEM((nq, NOPE, QB), BF16), pltpu.VMEM((nq, 1, QB), F32),
                        pltpu.VMEM((256, QB), F32), pltpu.VMEM((2, QB, QB), F32), pltpu.VMEM((2, QB, QB), F32),
                        pltpu.VMEM((QB, QB), F32), pltpu.VMEM((S, 256), F32)],
        compiler_params=_params(blk, scr),
    )(qf, kv, kr, do, o, lse, *tabs)


GROUP_STEPS = 2


def cast_group(name, ws, layers, idx, after=None):
    n = len(ws)
    n_in = n + (after is not None)

    def body(k_ref, *refs):
        for i in range(n):
            refs[n_in + i][...] = refs[i][...].astype(BF16)

    def spec_in(w, layer):
        return pl.BlockSpec((None, w.shape[1] // GROUP_STEPS, w.shape[2]), lambda r, k_ref: (layer, r, 0))

    def spec_out(w):
        return pl.BlockSpec((None, w.shape[1] // GROUP_STEPS, w.shape[2]), lambda r, k_ref: (k_ref[0], r, 0))

    return pl.pallas_call(
        body, name=name,
        grid_spec=pltpu.PrefetchScalarGridSpec(
            num_scalar_prefetch=1, grid=(GROUP_STEPS,),
            in_specs=([spec_in(w, l) for w, l in zip(ws, layers)]
                      + [pl.BlockSpec(memory_space=pl.ANY)] * (after is not None)),
            out_specs=[spec_out(w) for w in ws]),
        out_shape=[jax.ShapeDtypeStruct((N_CHIPS, *w.shape[1:]), BF16) for w in ws],
        compiler_params=_params(sum(_nbytes(w.shape[1:], F32) * 3 // 2 for w in ws) // GROUP_STEPS),
    )(idx, *ws, *([] if after is None else [after]))


def adamw(name, w, g, m, v):
    R, C = w.shape
    tr = _tile(R, max(8, (1 << 18) // C // 8 * 8), 8)
    c1 = 1.0 - ADAM_B1 ** ADAM_STEP
    c2 = 1.0 - ADAM_B2 ** ADAM_STEP

    def body(w_ref, g_ref, m_ref, v_ref, d_ref, mo_ref, vo_ref):
        gv = g_ref[...]
        mn = ADAM_B1 * m_ref[...] + (1.0 - ADAM_B1) * gv
        vn = ADAM_B2 * v_ref[...] + (1.0 - ADAM_B2) * (gv * gv)
        mo_ref[...] = mn
        vo_ref[...] = vn
        d_ref[...] = -ADAM_LR * ((mn / c1) / (jnp.sqrt(vn / c2) + ADAM_EPS) + ADAM_WD * w_ref[...])

    spec = pl.BlockSpec((tr, C), lambda r: (r, 0))
    return pl.pallas_call(
        body, name=name, grid=(R // tr,), in_specs=[spec] * 4, out_specs=[spec] * 3,
        out_shape=[jax.ShapeDtypeStruct((R, C), F32)] * 3,
        compiler_params=_params(7 * _nbytes((tr, C), F32), 4 * _nbytes((tr, C), F32)),
    )(w, g, m, v)


def half_sum_group(name, dws, landed, idx):
    n = len(dws)
    steps = GROUP_STEPS // 2

    def body(i_ref, *refs):
        for i in range(n):
            refs[2 * n + i][...] = (refs[i][...].astype(F32) + refs[n + i][...].astype(F32)).astype(BF16)

    def own(d):
        return pl.BlockSpec((None, None, d.shape[2] // steps, d.shape[3]), lambda k, r, i_ref: (k, i_ref[1], r, 0))

    def flat(d):
        return pl.BlockSpec((None, d.shape[2] // steps, d.shape[3]), lambda k, r, i_ref: (k, r, 0))

    return pl.pallas_call(
        body, name=name,
        grid_spec=pltpu.PrefetchScalarGridSpec(
            num_scalar_prefetch=1, grid=(N_CHIPS, steps),
            in_specs=[own(d) for d in dws] + [flat(d) for d in dws], out_specs=[flat(d) for d in dws]),
        out_shape=[jax.ShapeDtypeStruct((N_CHIPS, *d.shape[2:]), BF16) for d in dws],
        compiler_params=_params(sum(3 * _nbytes(d.shape[2:], BF16) for d in dws) // steps),
    )(idx, *dws, *landed)


def chip_sum_group(name, parts, landed, gbufs, layers, idx):
    n = len(parts)
    steps = GROUP_STEPS // 2

    def body(i_ref, *refs):
        for i in range(n):
            a, b = refs[i], refs[n + i]
            refs[3 * n + i][...] = ((a[...].astype(F32) + b[0].astype(F32)) + b[1].astype(F32)) + b[2].astype(F32)

    def mine(p):
        return pl.BlockSpec((None, p.shape[1] // steps, p.shape[2]), lambda r, i_ref: (i_ref[0], r, 0))

    def three(p):
        return pl.BlockSpec((3, p.shape[1] // steps, p.shape[2]), lambda r, i_ref: (0, r, 0))

    def out(p, layer):
        return pl.BlockSpec((None, None, p.shape[1] // steps, p.shape[2]), lambda r, i_ref: (layer, i_ref[1], r, 0))

    return pl.pallas_call(
        body, name=name,
        grid_spec=pltpu.PrefetchScalarGridSpec(
            num_scalar_prefetch=1, grid=(steps,),
            in_specs=[mine(p) for p in parts] + [three(p) for p in parts] + [pl.BlockSpec(memory_space=pl.ANY)] * n,
            out_specs=[out(p, l) for p, l in zip(parts, layers)]),
        out_shape=[jax.ShapeDtypeStruct(g.shape, F32) for g in gbufs],
        input_output_aliases={1 + 2 * n + i: i for i in range(n)},
        compiler_params=_params(sum(6 * _nbytes(p.shape[1:], BF16) for p in parts) // steps),
    )(idx, *parts, *landed, *gbufs)


ANY = pl.BlockSpec(memory_space=pl.ANY)


def _place():
    x, y, c = lax.axis_index("x"), lax.axis_index("y"), lax.axis_index("c")
    chips = [(1 - x, y), (x, 1 - y), (1 - x, 1 - y)]
    return x, y, c, chips


HBM = pl.BlockSpec(memory_space=pltpu.HBM)
SEM = pl.BlockSpec(memory_space=pltpu.SEMAPHORE)
EFFECT = pltpu.SideEffectType.DATAFLOW_SIDE_EFFECTING


def _in_hbm(a):
    return pltpu.with_memory_space_constraint(a, pltpu.HBM)


def _ici_copy(src, dst, send_sems, recv_sems, k, peer):
    return pltpu.make_async_remote_copy(src_ref=src, dst_ref=dst, send_sem=send_sems.at[k], recv_sem=recv_sems.at[k],
                                        device_id=peer, device_id_type=MESH)


def ici_start(name, bufs, lands, after, gather):
    n, nl = len(bufs), len(lands)

    def body(*refs):
        b_in = refs[:n]
        send_sems, recv_sems = refs[n + nl + 1], refs[n + nl + 2]
        b_out = refs[n + nl + 3:2 * n + nl + 3]
        l_out = refs[2 * n + nl + 3:2 * n + 2 * nl + 3]
        token = refs[-1]
        x, y, c, chips = _place()
        kme = 2 * x + y
        for i in range(n):
            for j in range(3):
                peer = (*chips[j], c)
                if gather:
                    _ici_copy(b_out[i].at[kme, c], b_out[i].at[kme, c], send_sems, recv_sems, 3 * i + j, peer).start()
                else:
                    kd = 2 * chips[j][0] + chips[j][1]
                    _ici_copy(b_out[i].at[kd], l_out[i].at[j], send_sems, recv_sems, 3 * i + j, peer).start()
        token[...] = jnp.zeros_like(token)

    arrays = [*bufs, *lands]
    outs = pl.pallas_call(
        body, name=name,
        in_specs=[HBM] * (n + nl) + [ANY],
        out_specs=(SEM, SEM, *[HBM] * (n + nl), pl.BlockSpec(memory_space=pltpu.VMEM)),
        out_shape=(pltpu.SemaphoreType.DMA((3 * n,)), pltpu.SemaphoreType.DMA((3 * n,)),
                   *[pltpu.HBM(a.shape, a.dtype) for a in arrays], jax.ShapeDtypeStruct((8, LANE), F32)),
        input_output_aliases={i: 2 + i for i in range(n + nl)},
        compiler_params=pltpu.CompilerParams(has_side_effects=EFFECT),
    )(*[_in_hbm(a) for a in arrays], after)
    return outs[0], outs[1], list(outs[2:2 + n]), list(outs[2 + n:2 + n + nl]), outs[-1]


def ici_wait(name, send_sems, recv_sems, bufs, lands, after, gather):
    n, nl = len(bufs), len(lands)

    def body(*refs):
        b_in, l_in = refs[:n], refs[n:n + nl]
        send_sems, recv_sems = refs[n + nl], refs[n + nl + 1]
        x, y, c, chips = _place()
        kme = 2 * x + y
        for i in range(n):
            for j in range(3):
                peer = (*chips[j], c)
                kj = 2 * chips[j][0] + chips[j][1]
                if gather:
                    _ici_copy(b_in[i].at[kme, c], b_in[i].at[kme, c], send_sems, recv_sems, 3 * i + j, peer).wait_send()
                    _ici_copy(b_in[i].at[kj, c], b_in[i].at[kj, c], send_sems, recv_sems, 3 * i + j, peer).wait_recv()
                else:
                    _ici_copy(b_in[i].at[kj], l_in[i].at[j], send_sems, recv_sems, 3 * i + j, peer).wait_send()
                    _ici_copy(b_in[i].at[kj], l_in[i].at[j], send_sems, recv_sems, 3 * i + j, peer).wait_recv()

    arrays = [*bufs, *lands]
    outs = pl.pallas_call(
        body, name=name,
        in_specs=[HBM] * (n + nl) + [SEM, SEM, ANY],
        out_specs=tuple([HBM] * (n + nl)),
        out_shape=tuple(pltpu.HBM(a.shape, a.dtype) for a in arrays),
        input_output_aliases={i: i for i in range(n + nl)},
        compiler_params=pltpu.CompilerParams(has_side_effects=EFFECT),
    )(*arrays, send_sems, recv_sems, after)
    return list(outs[:n]), list(outs[n:])


def gather_pair_pass(name, bufs):
    n = len(bufs)

    def body(*refs):
        b = refs[n:2 * n]
        send_sems, recv_sems = refs[2 * n:]
        x, y, c, chips = _place()
        sib = (x, y, 1 - c)

        def d2d(i, j, which):
            kj = 2 * chips[j][0] + chips[j][1]
            return _ici_copy(b[i].at[kj, which], b[i].at[kj, which], send_sems, recv_sems, 3 * i + j, sib)

        for i in range(n):
            for j in range(3):
                d2d(i, j, c).start()
        for i in range(n):
            for j in range(3):
                d2d(i, j, 1 - c).wait_recv()
        for i in range(n):
            for j in range(3):
                d2d(i, j, c).wait_send()

    return pl.pallas_call(
        body, name=name, in_specs=[ANY] * n, out_specs=[ANY] * n,
        out_shape=[jax.ShapeDtypeStruct(a.shape, a.dtype) for a in bufs],
        input_output_aliases={i: i for i in range(n)},
        scratch_shapes=[pltpu.SemaphoreType.DMA((3 * n,)), pltpu.SemaphoreType.DMA((3 * n,))],
    )(*bufs)


def pair_exchange(name, dws):
    n = len(dws)

    def body(*refs):
        ins, outs = refs[:n], refs[n:2 * n]
        send_sems, recv_sems = refs[2 * n:]
        x, y, c, _ = _place()
        copies = []
        for i in range(n):
            copies.append(pltpu.make_async_remote_copy(
                src_ref=ins[i].at[:, 1 - c], dst_ref=outs[i],
                send_sem=send_sems.at[i], recv_sem=recv_sems.at[i],
                device_id=(x, y, 1 - c), device_id_type=MESH))
            copies[i].start()
        for cp in copies:
            cp.wait_recv()
        for cp in copies:
            cp.wait_send()

    return pl.pallas_call(
        body, name=name, in_specs=[ANY] * n, out_specs=[ANY] * n,
        out_shape=[jax.ShapeDtypeStruct((N_CHIPS, *d.shape[2:]), d.dtype) for d in dws],
        scratch_shapes=[pltpu.SemaphoreType.DMA((n,)), pltpu.SemaphoreType.DMA((n,))],
    )(*dws)


def pair_assemble(gbufs):
    n = len(gbufs)

    def body(*refs):
        bufs = refs[n:2 * n]
        send_sems, recv_sems = refs[2 * n:]
        x, y, c, _ = _place()
        copies = []
        for i in range(n):
            copies.append(pltpu.make_async_remote_copy(
                src_ref=bufs[i].at[:, c], dst_ref=bufs[i].at[:, c],
                send_sem=send_sems.at[i], recv_sem=recv_sems.at[i],
                device_id=(x, y, 1 - c), device_id_type=MESH))
            copies[i].start()
        for i in range(n):
            pltpu.make_async_remote_copy(
                src_ref=bufs[i].at[:, 1 - c], dst_ref=bufs[i].at[:, 1 - c],
                send_sem=send_sems.at[i], recv_sem=recv_sems.at[i],
                device_id=(x, y, 1 - c), device_id_type=MESH).wait_recv()
        for cp in copies:
            cp.wait_send()

    return pl.pallas_call(
        body, name="grad_pair_assemble", in_specs=[ANY] * n, out_specs=[ANY] * n,
        out_shape=[jax.ShapeDtypeStruct(g.shape, g.dtype) for g in gbufs],
        input_output_aliases={i: i for i in range(n)},
        scratch_shapes=[pltpu.SemaphoreType.DMA((n,)), pltpu.SemaphoreType.DMA((n,))],
    )(*gbufs)


def all_reduce_small(vec):
    NR = vec.shape[0]
    flips = [(fx, fy, fc) for fx in (0, 1) for fy in (0, 1) for fc in (0, 1)][1:]

    def body(v_ref, o_ref, buf, send_sems, recv_sems):
        x, y, c, _ = _place()
        me = 4 * x + 2 * y + c
        buf[me] = v_ref[...]
        copies = []
        for j, (fx, fy, fc) in enumerate(flips):
            peer = (1 - x if fx else x, 1 - y if fy else y, 1 - c if fc else c)
            copies.append(pltpu.make_async_remote_copy(
                src_ref=v_ref, dst_ref=buf.at[me], send_sem=send_sems.at[j], recv_sem=recv_sems.at[j],
                device_id=peer, device_id_type=MESH))
            copies[j].start()
        for cp in copies:
            cp.wait_recv()
        for cp in copies:
            cp.wait_send()
        acc = buf[0]
        for d in range(1, 8):
            acc = acc + buf[d]
        o_ref[...] = acc

    return pl.pallas_call(
        body, name="all_reduce_small",
        in_specs=[pl.BlockSpec(memory_space=pltpu.VMEM)], out_specs=pl.BlockSpec(memory_space=pltpu.VMEM),
        out_shape=jax.ShapeDtypeStruct((NR, LANE), F32),
        scratch_shapes=[pltpu.VMEM((8, NR, LANE), F32), pltpu.SemaphoreType.DMA((7,)),
                        pltpu.SemaphoreType.DMA((7,))],
    )(vec)


def _pack(arrays):
    flat = jnp.concatenate([a.reshape(-1).astype(F32) for a in arrays])
    n = flat.shape[0]
    npad = -(-n // (8 * LANE)) * (8 * LANE)
    return jnp.pad(flat, (0, npad - n)).reshape(npad // LANE, LANE)


def _unpack(buf, like):
    flat = buf.reshape(-1)
    out, off = [], 0
    for a in like:
        out.append(flat[off:off + a.size].reshape(a.shape))
        off += a.size
    return out


def kernel(x, ffn1_norm, ffn1_w_in, ffn1_w_out, mix_norm, ffn2_norm, ffn2_w_in, ffn2_w_out, a_w_qkv, a_rel_bias, a_w_o, kv_norm, kv_w_down, kv_latent_norm, kv_w_up, b_w_dq, b_q_norm, b_w_uq, b_w_o, final_norm, loss_target, m_ffn1_norm, m_ffn1_w_in, m_ffn1_w_out, m_mix_norm, m_ffn2_norm, m_ffn2_w_in, m_ffn2_w_out, m_a_w_qkv, m_a_rel_bias, m_a_w_o, m_kv_norm, m_kv_w_down, m_kv_latent_norm, m_kv_w_up, m_b_w_dq, m_b_q_norm, m_b_w_uq, m_b_w_o, m_final_norm, v_ffn1_norm, v_ffn1_w_in, v_ffn1_w_out, v_mix_norm, v_ffn2_norm, v_ffn2_w_in, v_ffn2_w_out, v_a_w_qkv, v_a_rel_bias, v_a_w_o, v_kv_norm, v_kv_w_down, v_kv_latent_norm, v_kv_w_up, v_b_w_dq, v_b_q_norm, v_b_w_uq, v_b_w_o, v_final_norm):
    B, S, D = x.shape
    T = B * S
    HB = D // 128
    QL = b_q_norm.shape[-1]
    KVL = kv_latent_norm.shape[0]
    hpc = HB // N_CHIPS
    tabs = rope_tables(S)
    idx = jnp.stack([2 * lax.axis_index("x") + lax.axis_index("y"), lax.axis_index("c")]).astype(I32)

    def halves(a):
        return a.reshape(*a.shape[:-2], 2, a.shape[-2] // 2, a.shape[-1])

    def whole(a):
        return a.reshape(*a.shape[:-3], 2 * a.shape[-2], a.shape[-1])

    kv_w_down_p = jnp.pad(kv_w_down, ((0, 0), (0, LANE - ROPE)))[None]
    b_w_uq_p = jnp.pad(b_w_uq.reshape(1, QL, hpc, NOPE + ROPE),
                       ((0, 0), (0, 0), (0, 0), (0, LANE - ROPE))).reshape(1, QL, hpc * 256)
    sharded = [("ffn1_w_in", ffn1_w_in), ("ffn1_w_out", ffn1_w_out), ("ffn2_w_in", ffn2_w_in),
               ("ffn2_w_out", ffn2_w_out), ("a_w_qkv", a_w_qkv), ("a_w_o", a_w_o),
               ("kv_w_down", kv_w_down_p), ("kv_w_up", kv_w_up[None]), ("b_w_dq", b_w_dq),
               ("b_w_uq", b_w_uq_p), ("b_w_o", b_w_o)]
    names = [nm for nm, _ in sharded]
    shard_of = dict(sharded)
    W = {}

    gather_groups = [
        [("ffn1_w_in", 0)],
        [("ffn1_w_out", 0)],
        [("a_w_qkv", 0), ("a_w_o", 0)],
        [("ffn2_w_in", 0), ("ffn2_w_out", 0), ("kv_w_down", 0), ("kv_w_up", 0)],
        [("ffn1_w_in", 1), ("ffn1_w_out", 1), ("b_w_dq", 0), ("b_w_uq", 0), ("b_w_o", 0), ("ffn2_w_in", 1),
         ("ffn2_w_out", 1)]]

    own = {}

    def cast(g, after=None):
        keys = gather_groups[g]
        own.update(zip(keys, cast_group(f"cast_group_{g}", [shard_of[nm] for nm, _ in keys], [l for _, l in keys],
                                        idx, after=after)))

    def gather_start(g, after):
        keys = gather_groups[g]
        ss, rs, bufs, _, token = ici_start(f"gather_start_{g}", [halves(own[k]) for k in keys], [], after, True)
        return (g, ss, rs, bufs), token

    def gather_finish(state, after):
        g, ss, rs, bufs = state
        bufs, _ = ici_wait(f"gather_wait_{g}", ss, rs, bufs, [], after, True)
        full = gather_pair_pass(f"gather_pair_{g}", bufs)
        for k, w in zip(gather_groups[g], full):
            W[k] = whole(w)
        return full[0]

    def tied(a, token):
        return a + token[0, 0]

    def col(nm, l=0):
        return W[(nm, l)]

    def row(nm, l=0):
        w = W[(nm, l)]
        return w.reshape(N_CHIPS * w.shape[1], w.shape[2])

    bias = rel_bias_tile("rel_bias_tile", a_rel_bias[0])

    def ffn_fwd(tag, h, g, w_in, w_out):
        xn = rms_fwd(f"{tag}_norm", h, g)
        u, act = ffn_in_act(f"{tag}_in", xn, w_in)
        return mm_roww(f"{tag}_out", act, w_out, F32, res=h, alpha=0.5), (xn, u, act)

    h0 = x.reshape(T, D)
    for g in range(3):
        cast(g)
    st0, tok0 = gather_start(0, h0)
    st1, tok1 = gather_start(1, tok0)
    st2, tok2 = gather_start(2, tok1)
    for g in range(3, len(gather_groups)):
        cast(g, tok2)
    xn0 = rms_fwd("l0f1_norm", h0, tied(ffn1_norm[0], tok2))
    gather_finish(st0, xn0)
    u0, act0 = ffn_in_act("l0f1_in", xn0, col("ffn1_w_in", 0))
    gather_finish(st1, u0)
    h1 = mm_roww("l0f1_out", act0, row("ffn1_w_out", 0), F32, res=h0, alpha=0.5)
    sv_f1a = (xn0, u0, act0)
    done2 = gather_finish(st2, h1)
    st3, tok3 = gather_start(3, done2)
    st4, tok4 = gather_start(4, tok3)
    hn_a = rms_fwd("l0mix_norm", h1, tied(mix_norm[0], tok4))
    qkv = mm_colw("l0_qkv", hn_a, col("a_w_qkv"), BF16).reshape(B, S, 3 * D)
    o_a = attn_a_fwd("l0_attn", qkv, bias).reshape(T, D)
    h2 = mm_roww("l0_attn_out", o_a, row("a_w_o"), F32, res=h1)
    gather_finish(st3, h2)
    h3, sv_f2a = ffn_fwd("l0f2", h2, ffn2_norm[0], col("ffn2_w_in", 0), row("ffn2_w_out", 0))

    hkv = rms_fwd("kv_norm", h3, kv_norm)
    ckr = mm_roww("kv_down", hkv, row("kv_w_down"), F32)
    ckv, kr = kvprep_fwd("kv_prep", ckr, kv_latent_norm, tabs, B, S)
    kvb = mm_colw("kv_up", ckv, col("kv_w_up"), BF16).reshape(B, S, HB * 256)
    gather_finish(st4, kvb)

    h4, sv_f1b = ffn_fwd("l1f1", h3, ffn1_norm[1], col("ffn1_w_in", 1), row("ffn1_w_out", 1))
    hn_b = rms_fwd("l1mix_norm", h4, mix_norm[1])
    cqp = mm_roww("l1_dq", hn_b, row("b_w_dq"), F32)
    cq = rms_fwd("l1_q_norm", cqp, b_q_norm[0])
    qf = uq_rope("l1_uq", cq, col("b_w_uq"), tabs, S).reshape(B, S, HB * 256)
    o_b, lse = mla_fwd("l1_attn", qf, kvb, kr)
    h5 = mm_roww("l1_attn_out", o_b.reshape(T, HB * LANE), row("b_w_o"), F32, res=h4)
    h6, sv_f2b = ffn_fwd("l1f2", h5, ffn2_norm[1], col("ffn2_w_in", 1), row("ffn2_w_out", 1))

    dh, g_final, loss_part = loss_head("loss_head", h6, final_norm, loss_target.reshape(T, D))

    gw = {}
    gbufs = {nm: lax.empty(halves(w).shape, F32) for nm, w in sharded}

    def reduce_start(r, keys, after):
        dws = [halves(gw[k]) for k in keys]
        landed = pair_exchange(f"grad_pair_exchange_{r}", dws)
        parts = half_sum_group(f"half_sum_{r}", dws, landed, idx)
        lands = [lax.empty((3, *p.shape[1:]), p.dtype) for p in parts]
        ss, rs, parts, lands, token = ici_start(f"reduce_start_{r}", parts, lands, after, False)
        return (r, keys, ss, rs, parts, lands), token

    def reduce_finish(state, after):
        r, keys, ss, rs, parts, lands = state
        parts, lands = ici_wait(f"reduce_wait_{r}", ss, rs, parts, lands, after, False)
        done = chip_sum_group(f"chip_sum_{r}", parts, lands, [gbufs[nm] for nm, _ in keys], [l for _, l in keys], idx)
        gbufs.update(zip([nm for nm, _ in keys], done))
        return done[0]

    def ffn_bwd(tag, dh, h_in, g, w_in, w_out, saved, key_in, key_out, after=None, then=None):
        xn, u, act = saved
        du = ffn_dact(f"{tag}_dact", dh, w_out, u, after=after)
        dwo = mm_droww(f"{tag}_dwout", act, dh, alpha=0.5)
        gw[key_out] = dwo.reshape(N_CHIPS, dwo.shape[0] // N_CHIPS, dwo.shape[1])
        gw[key_in] = mm_dcolw(f"{tag}_dwin", xn, du, pair_layout=True)
        token = then(du) if then is not None else None
        return dx_norm_bwd(f"{tag}_dxn", du, w_in, h_in, g, dres=dh, pair_layout=True, after=token)

    def chip_major(dw):
        return dw.reshape(N_CHIPS, dw.shape[0] // N_CHIPS, dw.shape[1])

    dh, g_f2b = ffn_bwd("l1f2b", dh, h5, ffn2_norm[1], col("ffn2_w_in", 1), row("ffn2_w_out", 1), sv_f2b,
                        ("ffn2_w_in", 1), ("ffn2_w_out", 1))
    red0, rtok0 = reduce_start(0, [("ffn2_w_in", 1), ("ffn2_w_out", 1)], dh)
    do_b = mm_roww_t("l1_attn_do", dh, row("b_w_o"), BF16, after=rtok0).reshape(B, S, HB * LANE)
    gw[("b_w_o", 0)] = chip_major(mm_droww("l1_attn_dwo", o_b.reshape(T, HB * LANE), dh))
    dqpre, dkv, dkr = mla_bwd("l1_attn_bwd", qf, kvb, kr, do_b, o_b, lse, tabs)
    dqpre = dqpre.reshape(T, HB * 256)
    gw[("b_w_uq", 0)] = mm_dcolw("l1_dwuq", cq, dqpre)
    dcqp, g_qn = dx_norm_bwd("l1_dcq", dqpre, col("b_w_uq"), cqp, b_q_norm[0])
    gw[("b_w_dq", 0)] = chip_major(mm_droww("l1_dwdq", hn_b, dcqp))
    dhn = mm_roww_t("l1_dhn", dcqp, row("b_w_dq"), F32)
    dh, g_mixb = rms_bwd("l1_dmix", h4, mix_norm[1], dhn, dres=dh)
    dh, g_f1b = ffn_bwd("l1f1b", dh, h3, ffn1_norm[1], col("ffn1_w_in", 1), row("ffn1_w_out", 1), sv_f1b,
                        ("ffn1_w_in", 1), ("ffn1_w_out", 1))
    fin0 = reduce_finish(red0, dh)
    red1, rtok1 = reduce_start(1, [("b_w_o", 0), ("b_w_uq", 0), ("b_w_dq", 0), ("ffn1_w_in", 1), ("ffn1_w_out", 1)], fin0)
    dkv2 = dkv.reshape(T, HB * 256)
    gw[("kv_w_up", 0)] = mm_dcolw("kv_dwup", ckv, dkv2, after=rtok1)
    dckv = mm_colw_t("kv_dckv", dkv2, col("kv_w_up"), F32, after=rtok1)
    dckr, g_lat = kvprep_bwd("kv_prep_bwd", ckr, kv_latent_norm, dckv, dkr, tabs, B, S)
    gw[("kv_w_down", 0)] = chip_major(mm_droww("kv_dwdown", hkv, dckr))
    dhkv = mm_roww_t("kv_dhkv", dckr, row("kv_w_down"), F32)
    dh, g_kvn = rms_bwd("kv_dnorm", h3, kv_norm, dhkv, dres=dh)
    dh, g_f2a = ffn_bwd("l0f2b", dh, h2, ffn2_norm[0], col("ffn2_w_in", 0), row("ffn2_w_out", 0), sv_f2a,
                        ("ffn2_w_in", 0), ("ffn2_w_out", 0))
    do_a = mm_roww_t("l0_attn_do", dh, row("a_w_o"), BF16).reshape(B, S, D)
    gw[("a_w_o", 0)] = chip_major(mm_droww("l0_attn_dwo", o_a, dh))
    dqkv, dbias = attn_a_bwd("l0_attn_bwd", qkv, do_a, bias)
    dqkv = dqkv.reshape(T, 3 * D)
    gw[("a_w_qkv", 0)] = mm_dcolw("l0_dwqkv", hn_a, dqkv)
    dh, g_mixa = dx_norm_bwd("l0_dhn", dqkv, col("a_w_qkv"), h1, mix_norm[0], dres=dh)
    fin1 = reduce_finish(red1, dh)
    red2, rtok2 = reduce_start(2, [("kv_w_up", 0), ("kv_w_down", 0), ("ffn2_w_in", 0), ("ffn2_w_out", 0),
                                   ("a_w_o", 0), ("a_w_qkv", 0)], fin1)
    last = {}

    def last_group(du):
        fin2 = reduce_finish(red2, gw[("ffn1_w_in", 0)])
        last["red"], token = reduce_start(3, [("ffn1_w_in", 0), ("ffn1_w_out", 0)], fin2)
        return token

    dh, g_f1a = ffn_bwd("l0f1b", dh, h0, ffn1_norm[0], col("ffn1_w_in", 0), row("ffn1_w_out", 0), sv_f1a,
                        ("ffn1_w_in", 0), ("ffn1_w_out", 0), after=rtok2, then=last_group)
    grad_x = dh.reshape(B, S, D)
    g_rel = rel_bias_grad("rel_bias_grad", dbias)[:, :2 * MAX_REL + 1][None]
    reduce_finish(last["red"], dh)

    full = [whole(g) for g in pair_assemble([gbufs[nm] for nm in names])]
    G = {nm: g for (nm, _), g in zip(sharded, full)}
    G["kv_w_down"] = G["kv_w_down"][0, :, :KVL + ROPE]
    G["kv_w_up"] = G["kv_w_up"][0]
    G["b_w_uq"] = G["b_w_uq"].reshape(1, QL, hpc, 256)[..., :NOPE + ROPE].reshape(b_w_uq.shape)

    small = [("ffn1_norm", jnp.stack([g_f1a, g_f1b])), ("mix_norm", jnp.stack([g_mixa, g_mixb])),
             ("ffn2_norm", jnp.stack([g_f2a, g_f2b])), ("a_rel_bias", g_rel), ("kv_norm", g_kvn),
             ("kv_latent_norm", g_lat), ("b_q_norm", g_qn[None]), ("final_norm", g_final)]
    red = all_reduce_small(_pack([loss_part] + [g for _, g in small]))
    unpacked = _unpack(red, [loss_part] + [g for _, g in small])
    loss = unpacked[0][0, 0]
    for (nm, _), g in zip(small, unpacked[1:]):
        G[nm] = g

    given = dict(ffn1_norm=(ffn1_norm, m_ffn1_norm, v_ffn1_norm), ffn1_w_in=(ffn1_w_in, m_ffn1_w_in, v_ffn1_w_in),
                 ffn1_w_out=(ffn1_w_out, m_ffn1_w_out, v_ffn1_w_out), mix_norm=(mix_norm, m_mix_norm, v_mix_norm),
                 ffn2_norm=(ffn2_norm, m_ffn2_norm, v_ffn2_norm), ffn2_w_in=(ffn2_w_in, m_ffn2_w_in, v_ffn2_w_in),
                 ffn2_w_out=(ffn2_w_out, m_ffn2_w_out, v_ffn2_w_out), a_w_qkv=(a_w_qkv, m_a_w_qkv, v_a_w_qkv),
                 a_rel_bias=(a_rel_bias, m_a_rel_bias, v_a_rel_bias), a_w_o=(a_w_o, m_a_w_o, v_a_w_o),
                 kv_norm=(kv_norm, m_kv_norm, v_kv_norm), kv_w_down=(kv_w_down, m_kv_w_down, v_kv_w_down),
                 kv_latent_norm=(kv_latent_norm, m_kv_latent_norm, v_kv_latent_norm),
                 kv_w_up=(kv_w_up, m_kv_w_up, v_kv_w_up), b_w_dq=(b_w_dq, m_b_w_dq, v_b_w_dq),
                 b_q_norm=(b_q_norm, m_b_q_norm, v_b_q_norm), b_w_uq=(b_w_uq, m_b_w_uq, v_b_w_uq),
                 b_w_o=(b_w_o, m_b_w_o, v_b_w_o), final_norm=(final_norm, m_final_norm, v_final_norm))
    order = list(given)
    delta, new_m, new_v = {}, {}, {}
    small_names = [nm for nm, _ in small]
    packed = [_pack([given[nm][k] for nm in small_names]) for k in range(3)]
    outs = adamw("adamw_small", packed[0], _pack([G[nm] for nm in small_names]), packed[1], packed[2])
    for dst, buf in zip((delta, new_m, new_v), outs):
        for nm, a in zip(small_names, _unpack(buf, [given[nm][0] for nm in small_names])):
            dst[nm] = a
    for nm, _ in sharded:
        w, m, v = given[nm]
        g = G[nm].reshape(w.shape)
        G[nm] = g
        two = lambda a: a.reshape(-1, a.shape[-1])
        d_, m_, v_ = adamw(f"adamw_{nm}", two(w), two(g), two(m), two(v))
        delta[nm], new_m[nm], new_v[nm] = d_.reshape(w.shape), m_.reshape(w.shape), v_.reshape(w.shape)

    return (loss, grad_x, *[G[n] for n in order], *[delta[n] for n in order],
            *[new_m[n] for n in order], *[new_v[n] for n in order])
```

```python
import math

import jax
import jax.numpy as jnp
from jax import lax
from jax.experimental import pallas as pl
from jax.experimental.pallas import tpu as pltpu

F32 = jnp.float32
BF16 = jnp.bfloat16
I32 = jnp.int32

CHUNK = 64
CHUNK_SHIFT = 6
HEAD_DIM_A = 64
LEFT_CHUNKS = 8
MAX_REL = 128
REL_PAD = 384
QROWS = 2 * CHUNK
WIN = (LEFT_CHUNKS + 2) * CHUNK
PADR = LEFT_CHUNKS * CHUNK
NOPE = 128
ROPE = 64
EPS = 1e-6
NEG_INF = -1e30
ROPE_THETA = 10000.0
ADAM_LR, ADAM_B1, ADAM_B2, ADAM_EPS, ADAM_WD, ADAM_STEP = 0.001, 0.9, 0.999, 1e-08, 0.01, 10
N_CHIPS = 4
LANE = 128
MESH = pl.DeviceIdType.MESH
VMEM_CAP_MB = 60
VMEM_FLOOR_MB = 48

NN = (((1,), (0,)), ((), ()))
NT = (((1,), (1,)), ((), ()))
TN = (((0,), (0,)), ((), ()))


def _tile(n, pref, mult):
    t = (min(pref, n) // mult) * mult
    while t >= mult:
        if n % t == 0:
            return t
        t -= mult
    return n


def _nbytes(shape, dtype):
    return math.prod(shape) * jnp.dtype(dtype).itemsize


def _params(block_bytes, extra_bytes=0):
    need = 2 * block_bytes + extra_bytes
    mb = min(VMEM_CAP_MB, max(VMEM_FLOOR_MB, int(need * 1.25 / 2**20) + 8))
    return pltpu.CompilerParams(vmem_limit_bytes=mb * 2**20)


def _mm(name, kind, a, b, grid, a_spec, b_spec, o_spec, out_shape, out_dtype, blocks,
        red_axis=None, nred=1, alpha=1.0, res=None, res_spec=None, after=None):
    dims = {"nn": NN, "nt": NT, "tn": TN}[kind]
    has_res = res is not None
    acc_in_out = nred > 1 and out_dtype == F32 and not has_res and alpha == 1.0
    n_in = 2 + has_res + (after is not None)

    def body(*refs):
        a_ref, b_ref = refs[0], refs[1]
        r_ref = refs[2] if has_res else None
        o_ref = refs[n_in]
        p = lax.dot_general(a_ref[...].astype(BF16), b_ref[...].astype(BF16), dims,
                            preferred_element_type=F32)

        def finish(acc):
            y = acc if alpha == 1.0 else acc * alpha
            if has_res:
                y = r_ref[...] + y
            o_ref[...] = y.astype(o_ref.dtype)

        if nred == 1:
            finish(p)
            return
        k = pl.program_id(red_axis)
        tgt = o_ref if acc_in_out else refs[-1]

        @pl.when(k == 0)
        def _():
            tgt[...] = p

        @pl.when(k > 0)
        def _():
            tgt[...] += p

        if not acc_in_out:
            @pl.when(k == nred - 1)
            def _():
                finish(tgt[...])

    a_blk, b_blk, o_blk = blocks
    scratch = []
    extra = 0
    if nred > 1 and not acc_in_out:
        scratch = [pltpu.VMEM(o_blk, F32)]
        extra = _nbytes(o_blk, F32)
    blk = _nbytes(a_blk, a.dtype) + _nbytes(b_blk, b.dtype) + _nbytes(o_blk, out_dtype)
    ins, specs = [a, b], [a_spec, b_spec]
    if has_res:
        ins.append(res)
        specs.append(res_spec)
        blk += _nbytes(o_blk, res.dtype)
    if after is not None:
        ins.append(after)
        specs.append(pl.BlockSpec(memory_space=pl.ANY))
    extra += _nbytes(a_blk, BF16) + _nbytes(b_blk, BF16) + 2 * _nbytes(o_blk, F32)
    return pl.pallas_call(
        body, name=name, grid=grid, in_specs=specs, out_specs=o_spec,
        out_shape=jax.ShapeDtypeStruct(out_shape, out_dtype), scratch_shapes=scratch,
        compiler_params=_params(blk, extra),
    )(*ins)


def mm_colw(name, x, w3, out_dtype):
    T, K = x.shape
    _, _, nl = w3.shape
    tm = _tile(T, 1024, 8)
    return _mm(name, "nn", x, w3, (N_CHIPS, T // tm),
               pl.BlockSpec((tm, K), lambda j, i: (i, 0)),
               pl.BlockSpec((None, K, nl), lambda j, i: (j, 0, 0)),
               pl.BlockSpec((tm, nl), lambda j, i: (i, j)),
               (T, N_CHIPS * nl), out_dtype, ((tm, K), (K, nl), (tm, nl)))


def _pair_chip(j):
    return (j % 2) * 2 + j // 2


def mm_colw_t(name, dy, w3, out_dtype, res=None, after=None, pair_layout=False):
    T = dy.shape[0]
    _, K, nl = w3.shape
    tm = _tile(T, 1024, 8)
    chip = _pair_chip if pair_layout else (lambda j: j)
    return _mm(name, "nt", dy, w3, (T // tm, N_CHIPS),
               pl.BlockSpec((tm, nl), lambda i, j: (i, j)),
               pl.BlockSpec((None, K, nl), lambda i, j: (chip(j), 0, 0)),
               pl.BlockSpec((tm, K), lambda i, j: (i, 0)),
               (T, K), out_dtype, ((tm, nl), (K, nl), (tm, K)),
               red_axis=1, nred=N_CHIPS, res=res,
               res_spec=pl.BlockSpec((tm, K), lambda i, j: (i, 0)), after=after)


def mm_dcolw(name, x, dy, after=None, pair_layout=False):
    T, K = x.shape
    nl = dy.shape[1] // N_CHIPS
    tt = _tile(T, 2048, 8)
    chip = _pair_chip if pair_layout else (lambda j: j)
    return _mm(name, "tn", x, dy, (N_CHIPS, T // tt),
               pl.BlockSpec((tt, K), lambda j, t: (t, 0)),
               pl.BlockSpec((tt, nl), lambda j, t: (t, j)),
               pl.BlockSpec((None, K, nl), lambda j, t: (chip(j), 0, 0)),
               (N_CHIPS, K, nl), BF16, ((tt, K), (tt, nl), (K, nl)),
               red_axis=1, nred=T // tt, after=after)


def mm_roww(name, x, w2, out_dtype, res=None, alpha=1.0):
    T, Kt = x.shape
    N = w2.shape[1]
    tm = _tile(T, 1024, 8)
    return _mm(name, "nn", x, w2, (T // tm,),
               pl.BlockSpec((tm, Kt), lambda i: (i, 0)),
               pl.BlockSpec((Kt, N), lambda i: (0, 0)),
               pl.BlockSpec((tm, N), lambda i: (i, 0)),
               (T, N), out_dtype, ((tm, Kt), (Kt, N), (tm, N)),
               alpha=alpha, res=res, res_spec=pl.BlockSpec((tm, N), lambda i: (i, 0)))


def mm_roww_t(name, dy, w2, out_dtype, alpha=1.0, after=None):
    T, N = dy.shape
    Kt = w2.shape[0]
    tm = _tile(T, 512, 8)
    tk = _tile(Kt, 1408, LANE)
    return _mm(name, "nt", dy, w2, (Kt // tk, T // tm),
               pl.BlockSpec((tm, N), lambda j, i: (i, 0)),
               pl.BlockSpec((tk, N), lambda j, i: (j, 0)),
               pl.BlockSpec((tm, tk), lambda j, i: (i, j)),
               (T, Kt), out_dtype, ((tm, N), (tk, N), (tm, tk)), alpha=alpha, after=after)


def mm_droww(name, x, dy, alpha=1.0):
    T, Kt = x.shape
    N = dy.shape[1]
    tt = _tile(T, 2048, 8)
    tk = _tile(Kt, 1408, LANE)
    return _mm(name, "tn", x, dy, (Kt // tk, T // tt),
               pl.BlockSpec((tt, tk), lambda j, t: (t, j)),
               pl.BlockSpec((tt, N), lambda j, t: (t, 0)),
               pl.BlockSpec((tk, N), lambda j, t: (j, 0)),
               (Kt, N), BF16, ((tt, tk), (tt, N), (tk, N)),
               red_axis=1, nred=T // tt, alpha=alpha)


def rms_fwd(name, x, g):
    T, D = x.shape
    tm = _tile(T, 512, 8)

    def body(x_ref, g_ref, o_ref):
        xv = x_ref[...]
        r = lax.rsqrt(jnp.mean(xv * xv, axis=-1, keepdims=True) + EPS)
        o_ref[...] = (xv * r * g_ref[...]).astype(o_ref.dtype)

    return pl.pallas_call(
        body, name=name, grid=(T // tm,),
        in_specs=[pl.BlockSpec((tm, D), lambda i: (i, 0)), pl.BlockSpec((1, D), lambda i: (0, 0))],
        out_specs=pl.BlockSpec((tm, D), lambda i: (i, 0)),
        out_shape=jax.ShapeDtypeStruct((T, D), BF16),
        compiler_params=_params(_nbytes((tm, D), F32) * 2, 4 * _nbytes((tm, D), F32)),
    )(x, g.reshape(1, D))


def _rms_bwd_math(xv, gv, dy):
    r = lax.rsqrt(jnp.mean(xv * xv, axis=-1, keepdims=True) + EPS)
    xh = xv * r
    dyg = dy * gv
    dx = r * (dyg - xh * jnp.mean(dyg * xh, axis=-1, keepdims=True))
    dg = jnp.sum(dy * xh, axis=0, keepdims=True)
    return dx, dg


def rms_bwd(name, x, g, dy, dres=None):
    T, D = x.shape
    tm = _tile(T, 256, 8)
    has_res = dres is not None

    def body(*refs):
        x_ref, g_ref, dy_ref = refs[:3]
        r_ref = refs[3] if has_res else None
        dx_ref, dg_ref = refs[-2:]
        dx, dg = _rms_bwd_math(x_ref[...], g_ref[...], dy_ref[...].astype(F32))
        if has_res:
            dx = r_ref[...] + dx
        dx_ref[...] = dx

        @pl.when(pl.program_id(0) == 0)
        def _():
            dg_ref[...] = dg

        @pl.when(pl.program_id(0) > 0)
        def _():
            dg_ref[...] += dg

    row = pl.BlockSpec((tm, D), lambda i: (i, 0))
    vec = pl.BlockSpec((1, D), lambda i: (0, 0))
    ins, specs = [x, g.reshape(1, D), dy], [row, vec, row]
    if has_res:
        ins.append(dres)
        specs.append(row)
    dx, dg = pl.pallas_call(
        body, name=name, grid=(T // tm,), in_specs=specs, out_specs=[row, vec],
        out_shape=[jax.ShapeDtypeStruct((T, D), F32), jax.ShapeDtypeStruct((1, D), F32)],
        compiler_params=_params(_nbytes((tm, D), F32) * 4, 6 * _nbytes((tm, D), F32)),
    )(*ins)
    return dx, dg.reshape(D)


def dx_norm_bwd(name, dy, w3, x, g, dres=None, pair_layout=False, after=None):
    T = dy.shape[0]
    _, K, nl = w3.shape
    tm = _tile(T, 512, 8)
    chip = _pair_chip if pair_layout else (lambda j: j)
    has_res = dres is not None

    def body(*refs):
        dy_ref, w_ref, x_ref, g_ref = refs[:4]
        r_ref = refs[4] if has_res else None
        dx_ref, dg_ref, acc = refs[-3:]
        i, k = pl.program_id(0), pl.program_id(1)
        p = lax.dot_general(dy_ref[...].astype(BF16), w_ref[...], NT, preferred_element_type=F32)

        @pl.when(k == 0)
        def _():
            acc[...] = p

        @pl.when(k > 0)
        def _():
            acc[...] += p

        @pl.when(k == N_CHIPS - 1)
        def _():
            dx, dg = _rms_bwd_math(x_ref[...], g_ref[...], acc[...])
            dx_ref[...] = r_ref[...] + dx if has_res else dx

            @pl.when(i == 0)
            def _():
                dg_ref[...] = dg

            @pl.when(i > 0)
            def _():
                dg_ref[...] += dg

    row = pl.BlockSpec((tm, K), lambda i, j: (i, 0))
    vec = pl.BlockSpec((1, K), lambda i, j: (0, 0))
    ins = [dy, w3, x, g.reshape(1, K)]
    specs = [pl.BlockSpec((tm, nl), lambda i, j: (i, j)),
             pl.BlockSpec((None, K, nl), lambda i, j: (chip(j), 0, 0)), row, vec]
    if has_res:
        ins.append(dres)
        specs.append(row)
    if after is not None:
        ins.append(after)
        specs.append(pl.BlockSpec(memory_space=pl.ANY))
    blk = _nbytes((tm, nl), dy.dtype) + _nbytes((K, nl), BF16) + (2 + has_res) * _nbytes((tm, K), F32)
    dx, dg = pl.pallas_call(
        body, name=name, grid=(T // tm, N_CHIPS), in_specs=specs, out_specs=[row, vec],
        out_shape=[jax.ShapeDtypeStruct((T, K), F32), jax.ShapeDtypeStruct((1, K), F32)],
        scratch_shapes=[pltpu.VMEM((tm, K), F32)],
        compiler_params=_params(blk, 8 * _nbytes((tm, K), F32)),
    )(*ins)
    return dx, dg.reshape(K)


def ffn_in_act(name, x, w3):
    T, K = x.shape
    _, _, nl = w3.shape
    tm = _tile(T, 1024, 8)

    def body(*refs):
        x_ref, wg_ref, wu_ref = refs[:3]
        u_ref, a_ref = refs[-2:]
        xv = x_ref[...]
        g = jnp.dot(xv, wg_ref[...], preferred_element_type=F32)
        up = jnp.dot(xv, wu_ref[...], preferred_element_type=F32)
        u_ref[:, :nl] = g.astype(u_ref.dtype)
        u_ref[:, nl:] = up.astype(u_ref.dtype)
        a_ref[...] = (g * jax.nn.sigmoid(g) * up).astype(a_ref.dtype)

    blk = _nbytes((tm, K), BF16) + 2 * _nbytes((K, nl), BF16) + _nbytes((tm, 3 * nl), BF16)
    return pl.pallas_call(
        body, name=name, grid=(2, T // tm),
        in_specs=[pl.BlockSpec((tm, K), lambda p, i: (i, 0)),
                  pl.BlockSpec((None, K, nl), lambda p, i: (p, 0, 0)),
                  pl.BlockSpec((None, K, nl), lambda p, i: (p + 2, 0, 0))],
        out_specs=[pl.BlockSpec((tm, 2 * nl), lambda p, i: (i, p)), pl.BlockSpec((tm, nl), lambda p, i: (i, p))],
        out_shape=[jax.ShapeDtypeStruct((T, 4 * nl), BF16), jax.ShapeDtypeStruct((T, 2 * nl), BF16)],
        compiler_params=_params(blk, 4 * _nbytes((tm, nl), F32)),
    )(x, w3, w3)


def ffn_dact(name, dh, w_out, u, after=None):
    T, N = dh.shape
    F = w_out.shape[0]
    nl = F // 2
    tm = _tile(T, 512, 8)

    def body(*refs):
        d_ref, w_ref, u_ref = refs[:3]
        o_ref = refs[-1]
        dact = 0.5 * lax.dot_general(d_ref[...].astype(BF16), w_ref[...], NT, preferred_element_type=F32)
        g = u_ref[:, :nl].astype(F32)
        up = u_ref[:, nl:].astype(F32)
        sig = jax.nn.sigmoid(g)
        o_ref[:, :nl] = (dact * up * (sig * (1.0 + g * (1.0 - sig)))).astype(o_ref.dtype)
        o_ref[:, nl:] = (dact * (g * sig)).astype(o_ref.dtype)

    ins = [dh, w_out, u]
    specs = [pl.BlockSpec((tm, N), lambda p, i: (i, 0)), pl.BlockSpec((nl, N), lambda p, i: (p, 0)),
             pl.BlockSpec((tm, 2 * nl), lambda p, i: (i, p))]
    if after is not None:
        ins.append(after)
        specs.append(pl.BlockSpec(memory_space=pl.ANY))
    blk = _nbytes((tm, N), F32) + _nbytes((nl, N), BF16) + 2 * _nbytes((tm, 2 * nl), BF16)
    return pl.pallas_call(
        body, name=name, grid=(2, T // tm), in_specs=specs,
        out_specs=pl.BlockSpec((tm, 2 * nl), lambda p, i: (i, p)),
        out_shape=jax.ShapeDtypeStruct((T, 2 * F), BF16),
        compiler_params=_params(blk, 6 * _nbytes((tm, nl), F32)),
    )(*ins)


def loss_head(name, h, g, target):
    T, D = h.shape
    tm = _tile(T, 256, 8)

    def body(h_ref, g_ref, t_ref, dh_ref, dg_ref, loss_ref):
        xv = h_ref[...]
        gv = g_ref[...]
        r = lax.rsqrt(jnp.mean(xv * xv, axis=-1, keepdims=True) + EPS)
        err = xv * r * gv - t_ref[...]
        part = 0.5 * jnp.sum(jnp.mean(err * err, axis=-1, keepdims=True), axis=0, keepdims=True)
        dx, dg = _rms_bwd_math(xv, gv, err * (1.0 / D))
        dh_ref[...] = dx
        part = jnp.broadcast_to(part, (1, LANE))

        @pl.when(pl.program_id(0) == 0)
        def _():
            dg_ref[...] = dg
            loss_ref[...] = part

        @pl.when(pl.program_id(0) > 0)
        def _():
            dg_ref[...] += dg
            loss_ref[...] += part

    row = pl.BlockSpec((tm, D), lambda i: (i, 0))
    vec = pl.BlockSpec((1, D), lambda i: (0, 0))
    dh, dg, loss = pl.pallas_call(
        body, name=name, grid=(T // tm,), in_specs=[row, vec, row],
        out_specs=[row, vec, pl.BlockSpec((1, LANE), lambda i: (0, 0))],
        out_shape=[jax.ShapeDtypeStruct((T, D), F32), jax.ShapeDtypeStruct((1, D), F32),
                   jax.ShapeDtypeStruct((1, LANE), F32)],
        compiler_params=_params(_nbytes((tm, D), F32) * 3, 6 * _nbytes((tm, D), F32)),
    )(h, g.reshape(1, D), target)
    return dh, dg.reshape(D), loss


def rope_tables(S):
    half = ROPE // 2
    freqs = ROPE_THETA ** (-jnp.arange(half, dtype=F32) / half)
    ang = jnp.arange(S, dtype=F32)[:, None] * freqs[None, :]
    cos, sin = jnp.cos(ang), jnp.sin(ang)
    z = jnp.zeros_like(cos)
    ct = jnp.concatenate([cos, cos, z, z], axis=1)
    s1 = jnp.concatenate([-sin, z, z, z], axis=1)
    s2 = jnp.concatenate([z, sin, z, z], axis=1)
    return ct, s1, s2


def _rope_tile(t, ct, s1, s2):
    return t * ct + pltpu.roll(t, 96, 1) * s1 + pltpu.roll(t, 32, 1) * s2


def _rope_tile_bwd(d, ct, s1, s2):
    return d * ct + pltpu.roll(d * s1, 32, 1) + pltpu.roll(d * s2, 96, 1)


def uq_rope(name, x, w3, tabs, S):
    T, K = x.shape
    _, _, nl = w3.shape
    tm = _tile(S, 512, 8)
    nt = S // tm

    def body(x_ref, w_ref, ct_ref, s1_ref, s2_ref, o_ref):
        q = jnp.dot(x_ref[...], w_ref[...], preferred_element_type=F32)
        ct, s1, s2 = ct_ref[...], s1_ref[...], s2_ref[...]
        for h in range(nl // 256):
            o_ref[:, 256 * h:256 * h + 128] = q[:, 256 * h:256 * h + 128].astype(o_ref.dtype)
            o_ref[:, 256 * h + 128:256 * h + 256] = _rope_tile(q[:, 256 * h + 128:256 * h + 256],
                                                               ct, s1, s2).astype(o_ref.dtype)

    tab = pl.BlockSpec((tm, LANE), lambda j, i: (i % nt, 0))
    blk = _nbytes((tm, K), BF16) + _nbytes((K, nl), BF16) + _nbytes((tm, nl), BF16) + 3 * _nbytes((tm, LANE), F32)
    return pl.pallas_call(
        body, name=name, grid=(N_CHIPS, T // tm),
        in_specs=[pl.BlockSpec((tm, K), lambda j, i: (i, 0)), pl.BlockSpec((None, K, nl), lambda j, i: (j, 0, 0)),
                  tab, tab, tab],
        out_specs=pl.BlockSpec((tm, nl), lambda j, i: (i, j)),
        out_shape=jax.ShapeDtypeStruct((T, N_CHIPS * nl), BF16),
        compiler_params=_params(blk, 4 * _nbytes((tm, nl), F32)),
    )(x, w3, *tabs)


def kvprep_fwd(name, ckr, g, tabs, B, S):
    T, W = ckr.shape
    KVL = W - LANE
    ts = _tile(S, 256, 8)

    def body(x_ref, g_ref, ct_ref, s1_ref, s2_ref, c_ref, k_ref):
        xv = x_ref[0, :, :KVL]
        r = lax.rsqrt(jnp.mean(xv * xv, axis=-1, keepdims=True) + EPS)
        c_ref[0] = (xv * r * g_ref[...]).astype(c_ref.dtype)
        k_ref[0] = _rope_tile(x_ref[0, :, KVL:], ct_ref[...], s1_ref[...], s2_ref[...]).astype(k_ref.dtype)

    tab = pl.BlockSpec((ts, LANE), lambda b, s: (s, 0))
    c, k = pl.pallas_call(
        body, name=name, grid=(B, S // ts),
        in_specs=[pl.BlockSpec((1, ts, W), lambda b, s: (b, s, 0)), pl.BlockSpec((1, KVL), lambda b, s: (0, 0)),
                  tab, tab, tab],
        out_specs=[pl.BlockSpec((1, ts, KVL), lambda b, s: (b, s, 0)),
                   pl.BlockSpec((1, ts, LANE), lambda b, s: (b, s, 0))],
        out_shape=[jax.ShapeDtypeStruct((B, S, KVL), BF16), jax.ShapeDtypeStruct((B, S, LANE), BF16)],
        compiler_params=_params(_nbytes((ts, W), F32) * 2, _nbytes((ts, W), F32) * 2),
    )(ckr.reshape(B, S, W), g.reshape(1, KVL), *tabs)
    return c.reshape(T, KVL), k


def kvprep_bwd(name, ckr, g, dc, dkr, tabs, B, S):
    T, W = ckr.shape
    KVL = W - LANE
    ts = _tile(S, 256, 8)

    def body(x_ref, g_ref, dc_ref, dk_ref, ct_ref, s1_ref, s2_ref, o_ref, dg_ref):
        dx, dg = _rms_bwd_math(x_ref[0, :, :KVL], g_ref[...], dc_ref[0])
        o_ref[0, :, :KVL] = dx
        o_ref[0, :, KVL:] = _rope_tile_bwd(dk_ref[0], ct_ref[...], s1_ref[...], s2_ref[...])
        first = (pl.program_id(0) == 0) & (pl.program_id(1) == 0)

        @pl.when(first)
        def _():
            dg_ref[...] = dg

        @pl.when(jnp.logical_not(first))
        def _():
            dg_ref[...] += dg

    tab = pl.BlockSpec((ts, LANE), lambda b, s: (s, 0))
    vec = pl.BlockSpec((1, KVL), lambda b, s: (0, 0))
    o, dg = pl.pallas_call(
        body, name=name, grid=(B, S // ts),
        in_specs=[pl.BlockSpec((1, ts, W), lambda b, s: (b, s, 0)), vec,
                  pl.BlockSpec((1, ts, KVL), lambda b, s: (b, s, 0)),
                  pl.BlockSpec((1, ts, LANE), lambda b, s: (b, s, 0)), tab, tab, tab],
        out_specs=[pl.BlockSpec((1, ts, W), lambda b, s: (b, s, 0)), vec],
        out_shape=[jax.ShapeDtypeStruct((B, S, W), F32), jax.ShapeDtypeStruct((1, KVL), F32)],
        compiler_params=_params(_nbytes((ts, W), F32) * 4, _nbytes((ts, W), F32) * 4),
    )(ckr.reshape(B, S, W), g.reshape(1, KVL), dc.reshape(B, S, KVL), dkr, *tabs)
    return o.reshape(T, W), dg.reshape(KVL)


DIAGS = 768


def _diag_onehot():
    col = lax.broadcasted_iota(I32, (REL_PAD, DIAGS), 1)
    row = lax.broadcasted_iota(I32, (REL_PAD, DIAGS), 0)
    idx = jnp.clip(PADR + QROWS - 1 - col, -MAX_REL, MAX_REL) + MAX_REL
    return (row == idx).astype(F32)


def rel_bias_tile(name, table):
    H = table.shape[0]
    tpad = jnp.pad(table, ((0, 0), (0, REL_PAD - table.shape[1])))

    def body(t_ref, o_ref):
        g = lax.dot_general(t_ref[...], _diag_onehot(), NN, precision=lax.Precision.HIGHEST,
                            preferred_element_type=F32)
        qc = jnp.right_shift(lax.broadcasted_iota(I32, (QROWS, WIN), 0), CHUNK_SHIFT)
        kc = jnp.right_shift(lax.broadcasted_iota(I32, (QROWS, WIN), 1), CHUNK_SHIFT)
        band = (kc >= qc) & (kc <= qc + LEFT_CHUNKS)
        for h in range(H):
            gb = jnp.broadcast_to(g[h:h + 1, :], (QROWS, DIAGS))
            tile = pltpu.roll(gb, DIAGS - (QROWS - 1), 1, stride=1, stride_axis=0)
            o_ref[h // 2, (h % 2) * QROWS:(h % 2 + 1) * QROWS, :] = jnp.where(band, tile[:, :WIN], NEG_INF)

    return pl.pallas_call(
        body, name=name, out_shape=jax.ShapeDtypeStruct((H // 2, 2 * QROWS, WIN), F32),
        compiler_params=_params(0, 2 * _nbytes((H // 2, 2 * QROWS, WIN), F32)),
    )(tpad)


def rel_bias_grad(name, dbias):
    H = 2 * dbias.shape[0]

    def body(d_ref, o_ref):
        flip = (lax.broadcasted_iota(I32, (QROWS, QROWS), 0) + lax.broadcasted_iota(I32, (QROWS, QROWS), 1)
                == QROWS - 1).astype(F32)
        rows = []
        for h in range(H):
            x = d_ref[h // 2, (h % 2) * QROWS:(h % 2 + 1) * QROWS, :]
            xr = lax.dot_general(flip, x, NN, precision=lax.Precision.HIGHEST, preferred_element_type=F32)
            xp = jnp.concatenate([xr, jnp.zeros((QROWS, DIAGS - WIN), F32)], axis=1)
            y = pltpu.roll(xp, 0, 1, stride=1, stride_axis=0)
            rows.append(jnp.sum(y, axis=0, keepdims=True))
        o_ref[...] = lax.dot_general(jnp.concatenate(rows, axis=0), _diag_onehot(), NT,
                                     precision=lax.Precision.HIGHEST, preferred_element_type=F32)

    return pl.pallas_call(
        body, name=name, out_shape=jax.ShapeDtypeStruct((H, REL_PAD), F32),
        compiler_params=_params(0, 2 * _nbytes(dbias.shape, F32)),
    )(dbias)


def _stack_pair(xp):
    lane = lax.broadcasted_iota(I32, xp.shape, 1)
    z = jnp.zeros_like(xp)
    return jnp.concatenate([jnp.where(lane < HEAD_DIM_A, xp, z), jnp.where(lane >= HEAD_DIM_A, xp, z)], axis=0)


def _unstack_pair(y):
    lane = lax.broadcasted_iota(I32, (QROWS, LANE), 1)
    return jnp.where(lane < HEAD_DIM_A, y[:QROWS], y[QROWS:])


def _attn_a_rowpen(j):
    w = lax.broadcasted_iota(I32, (1, WIN), 1)
    return jnp.where(w >= PADR - QROWS * j, 0.0, NEG_INF).astype(F32)


def _attn_a_load_bias(bias_hbm, bias_v, sem):
    cp = pltpu.make_async_copy(bias_hbm, bias_v, sem)
    cp.start()
    cp.wait()


def _attn_a_load_kv(qkv_hbm, b, kpad, vpad, sem, S, D):
    kpad[0:PADR, :] = jnp.zeros((PADR, D), BF16)
    vpad[0:PADR, :] = jnp.zeros((PADR, D), BF16)
    ck = pltpu.make_async_copy(qkv_hbm.at[b, :, pl.ds(D, D)], kpad.at[pl.ds(PADR, S), :], sem.at[0])
    cv = pltpu.make_async_copy(qkv_hbm.at[b, :, pl.ds(2 * D, D)], vpad.at[pl.ds(PADR, S), :], sem.at[1])
    ck.start()
    cv.start()
    ck.wait()
    cv.wait()


def _attn_a_exp(q2s, kp, bias, pen):
    s = lax.dot_general(q2s, kp, NT, preferred_element_type=F32) + bias + pen
    e = jnp.exp(s - jnp.max(s, axis=-1, keepdims=True))
    return e, 1.0 / jnp.sum(e, axis=-1, keepdims=True)


def attn_a_fwd(name, qkv, bias):
    B, S, D3 = qkv.shape
    D = D3 // 3
    H = D // HEAD_DIM_A
    nb = S // QROWS
    scale = HEAD_DIM_A ** -0.5

    def body(q_ref, bias_hbm, qkv_hbm, o_ref, kpad, vpad, bias_v, sem):
        b, j = pl.program_id(0), pl.program_id(1)

        @pl.when((b == 0) & (j == 0))
        def _():
            _attn_a_load_bias(bias_hbm, bias_v, sem.at[2])

        @pl.when(j == 0)
        def _():
            _attn_a_load_kv(qkv_hbm, b, kpad, vpad, sem, S, D)

        pen = _attn_a_rowpen(j)
        w0 = pl.multiple_of(j * QROWS, QROWS)
        for p in range(H // 2):
            ls = slice(p * LANE, (p + 1) * LANE)
            e, rl = _attn_a_exp(_stack_pair(q_ref[0, :, ls] * scale), kpad[pl.ds(w0, WIN), ls], bias_v[p], pen)
            o2 = jnp.dot(e.astype(BF16), vpad[pl.ds(w0, WIN), ls], preferred_element_type=F32) * rl
            o_ref[0, :, ls] = _unstack_pair(o2).astype(o_ref.dtype)

    scr = 2 * _nbytes((PADR + S, D), BF16) + _nbytes(bias.shape, F32) + 8 * _nbytes((2 * QROWS, WIN), F32)
    return pl.pallas_call(
        body, name=name, grid=(B, nb),
        in_specs=[pl.BlockSpec((1, QROWS, D), lambda b, j: (b, j, 0)),
                  pl.BlockSpec(memory_space=pl.ANY), pl.BlockSpec(memory_space=pl.ANY)],
        out_specs=pl.BlockSpec((1, QROWS, D), lambda b, j: (b, j, 0)),
        out_shape=jax.ShapeDtypeStruct((B, S, D), BF16),
        scratch_shapes=[pltpu.VMEM((PADR + S, D), BF16), pltpu.VMEM((PADR + S, D), BF16),
                        pltpu.VMEM(bias.shape, F32), pltpu.SemaphoreType.DMA((3,))],
        compiler_params=_params(2 * _nbytes((QROWS, D), BF16), scr),
    )(qkv, bias, qkv)


def attn_a_bwd(name, qkv, do, bias):
    B, S, D3 = qkv.shape
    D = D3 // 3
    H = D // HEAD_DIM_A
    nb = S // QROWS
    scale = HEAD_DIM_A ** -0.5

    def body(q_ref, do_ref, bias_hbm, qkv_hbm, dqkv_hbm, dbias_hbm, kpad, vpad, dkacc, dvacc, bias_v, dbias_v,
             dq_stage, sem):
        b, j = pl.program_id(0), pl.program_id(1)
        step = b * nb + j
        slot = lax.rem(step, 2)

        def dq_out(s):
            return pltpu.make_async_copy(dq_stage.at[s], dqkv_hbm.at[b, pl.ds(j * QROWS, QROWS), pl.ds(0, D)],
                                         sem.at[3 + s])

        @pl.when(step >= 2)
        def _():
            dq_out(slot).wait()

        @pl.when((b == 0) & (j == 0))
        def _():
            _attn_a_load_bias(bias_hbm, bias_v, sem.at[2])
            dbias_v[...] = jnp.zeros_like(dbias_v)

        @pl.when(j == 0)
        def _():
            _attn_a_load_kv(qkv_hbm, b, kpad, vpad, sem, S, D)
            dkacc[...] = jnp.zeros_like(dkacc)
            dvacc[...] = jnp.zeros_like(dvacc)

        pen = _attn_a_rowpen(j)
        w0 = pl.multiple_of(j * QROWS, QROWS)
        for p in range(H // 2):
            ls = slice(p * LANE, (p + 1) * LANE)
            q2s = _stack_pair(q_ref[0, :, ls] * scale)
            do2 = _stack_pair(do_ref[0, :, ls])
            kp = kpad[pl.ds(w0, WIN), ls]
            vp = vpad[pl.ds(w0, WIN), ls]
            e, rl = _attn_a_exp(q2s, kp, bias_v[p], pen)
            pr = e * rl
            dp = lax.dot_general(do2, vp, NT, preferred_element_type=F32)
            ds = pr * (dp - jnp.sum(pr * dp, axis=-1, keepdims=True))
            dbias_v[p] += ds
            dsb = ds.astype(BF16)
            dq_stage[slot, :, ls] = (_unstack_pair(jnp.dot(dsb, kp, preferred_element_type=F32))
                                     * scale).astype(dq_stage.dtype)
            dkacc[pl.ds(w0, WIN), ls] += lax.dot_general(dsb, q2s, TN, preferred_element_type=F32)
            dvacc[pl.ds(w0, WIN), ls] += lax.dot_general(pr.astype(BF16), do2, TN, preferred_element_type=F32)

        dq_out(slot).start()

        @pl.when(j == nb - 1)
        def _():
            kpad[pl.ds(PADR, S), :] = dkacc[pl.ds(PADR, S), :].astype(BF16)
            vpad[pl.ds(PADR, S), :] = dvacc[pl.ds(PADR, S), :].astype(BF16)
            ck = pltpu.make_async_copy(kpad.at[pl.ds(PADR, S), :], dqkv_hbm.at[b, :, pl.ds(D, D)], sem.at[0])
            cv = pltpu.make_async_copy(vpad.at[pl.ds(PADR, S), :], dqkv_hbm.at[b, :, pl.ds(2 * D, D)], sem.at[1])
            ck.start()
            cv.start()
            ck.wait()
            cv.wait()

        @pl.when((b == B - 1) & (j == nb - 1))
        def _():
            cb = pltpu.make_async_copy(dbias_v, dbias_hbm, sem.at[2])
            cb.start()
            dq_out(0).wait()
            dq_out(1).wait()
            cb.wait()

    blk = _nbytes((QROWS, D), BF16) * 2
    scr = (2 * _nbytes((PADR + S, D), BF16) + 2 * _nbytes((PADR + S, D), F32) + 2 * _nbytes(bias.shape, F32)
           + 8 * _nbytes((2 * QROWS, WIN), F32) + 2 * _nbytes((QROWS, D), F32))
    return pl.pallas_call(
        body, name=name, grid=(B, nb),
        in_specs=[pl.BlockSpec((1, QROWS, D), lambda b, j: (b, j, 0)),
                  pl.BlockSpec((1, QROWS, D), lambda b, j: (b, j, 0)),
                  pl.BlockSpec(memory_space=pl.ANY), pl.BlockSpec(memory_space=pl.ANY)],
        out_specs=[pl.BlockSpec(memory_space=pl.ANY), pl.BlockSpec(memory_space=pl.ANY)],
        out_shape=[jax.ShapeDtypeStruct((B, S, 3 * D), BF16), jax.ShapeDtypeStruct(bias.shape, F32)],
        scratch_shapes=[pltpu.VMEM((PADR + S, D), BF16), pltpu.VMEM((PADR + S, D), BF16),
                        pltpu.VMEM((PADR + S, D), F32), pltpu.VMEM((PADR + S, D), F32),
                        pltpu.VMEM(bias.shape, F32), pltpu.VMEM(bias.shape, F32),
                        pltpu.VMEM((2, QROWS, D), BF16), pltpu.SemaphoreType.DMA((5,))],
        compiler_params=_params(blk, scr),
    )(qkv, do, bias, qkv)


def _mla_raw_t(k2, kj, q, QB):
    return lax.dot_general(k2[_blk(kj, QB), :], q, NT, preferred_element_type=F32)


def _blk(kj, QB):
    return pl.ds(kj * QB, QB) if isinstance(kj, int) else pl.ds(pl.multiple_of(kj * QB, QB), QB)


def _mla_diag_pen(QB):
    kc = jnp.right_shift(lax.broadcasted_iota(I32, (QB, QB), 0), CHUNK_SHIFT)
    qc = jnp.right_shift(lax.broadcasted_iota(I32, (QB, QB), 1), CHUNK_SHIFT)
    return jnp.where(kc <= qc, 0.0, NEG_INF).astype(F32)


def _mla_fill_keys(kv_ref, kr_ref, k2):
    k2[:, :NOPE] = kv_ref[0, :, :NOPE]
    k2[:, NOPE:] = kr_ref[0]


def _t(x):
    return x.astype(F32).T


def mla_fwd(name, qf, kv, kr):
    B, S, W = qf.shape
    HB = W // 256
    QB = _tile(S, 256, CHUNK)
    nq = S // QB
    scale = (NOPE + ROPE) ** -0.5

    def body(q_ref, kv_ref, kr_ref, o_ref, lse_ref, k2, vt, st_buf, pen):
        qi = pl.program_id(2)

        @pl.when(qi == 0)
        def _():
            pen[...] = _mla_diag_pen(QB)
            _mla_fill_keys(kv_ref, kr_ref, k2)
            for kj in range(nq):
                vt[kj] = _t(kv_ref[0, kj * QB:(kj + 1) * QB, NOPE:]).astype(BF16)

        q = q_ref[0]
        st_buf[0] = _mla_raw_t(k2, 0, q, QB)

        def step(kj, carry):
            m, l, acc = carry
            cur = lax.rem(kj, 2)
            st_raw = st_buf[cur]
            st_buf[1 - cur] = _mla_raw_t(k2, jnp.minimum(kj + 1, qi), q, QB)
            st = st_raw * scale + jnp.where(kj == qi, pen[...], 0.0)
            m_new = jnp.maximum(m, jnp.max(st, axis=0, keepdims=True))
            a = jnp.exp(m - m_new)
            pt = jnp.exp(st - m_new)
            l = a * l + jnp.sum(pt, axis=0, keepdims=True)
            acc = a * acc + jnp.dot(vt[kj], pt.astype(BF16), preferred_element_type=F32)
            return m_new, l, acc

        init = (jnp.full((1, QB), NEG_INF, F32), jnp.zeros((1, QB), F32), jnp.zeros((NOPE, QB), F32))
        m, l, acc = lax.fori_loop(0, qi + 1, step, init)
        o_ref[0] = (acc * (1.0 / l)).T
        lse_ref[0, 0] = m + jnp.log(l)

    blk = (_nbytes((QB, 256), BF16) + _nbytes((S, 256), BF16) + _nbytes((S, LANE), BF16)
           + _nbytes((QB, LANE), F32))
    return pl.pallas_call(
        body, name=name, grid=(B, HB, nq),
        in_specs=[pl.BlockSpec((1, QB, 256), lambda b, h, i: (b, i, h)),
                  pl.BlockSpec((1, S, 256), lambda b, h, i: (b, 0, h)),
                  pl.BlockSpec((1, S, LANE), lambda b, h, i: (b, 0, 0))],
        out_specs=[pl.BlockSpec((1, QB, LANE), lambda b, h, i: (b, i, h)),
                   pl.BlockSpec((1, 1, 1, QB), lambda b, h, i: (b, h, 0, i))],
        out_shape=[jax.ShapeDtypeStruct((B, S, HB * LANE), F32), jax.ShapeDtypeStruct((B, HB, 1, S), F32)],
        scratch_shapes=[pltpu.VMEM((S, 256), BF16), pltpu.VMEM((nq, NOPE, QB), BF16),
                        pltpu.VMEM((2, QB, QB), F32), pltpu.VMEM((QB, QB), F32)],
        compiler_params=_params(blk, 2 * _nbytes((S, 256), BF16) + 10 * _nbytes((QB, QB), F32)),
    )(qf, kv, kr)


def mla_bwd(name, qf, kv, kr, do, o, lse, tabs):
    B, S, W = qf.shape
    HB = W // 256
    QB = _tile(S, 256, CHUNK)
    nq = S // QB
    scale = (NOPE + ROPE) ** -0.5

    def body(q_ref, kv_ref, kr_ref, do_ref, o_ref, lse_ref, ct_ref, s1_ref, s2_ref, dq_ref, dkv_ref, dkr_ref,
             k2, kt, dot_, delta, dqt, st_buf, dp_buf, pen, dkv_acc):
        h = pl.program_id(1)
        pen[...] = _mla_diag_pen(QB)
        dkv_acc[...] = jnp.zeros_like(dkv_acc)

        @pl.when(h == 0)
        def _():
            dkr_ref[...] = jnp.zeros_like(dkr_ref)

        _mla_fill_keys(kv_ref, kr_ref, k2)
        for i in range(nq):
            rows = slice(i * QB, (i + 1) * QB)
            kt[i] = _t(k2[rows, :]).astype(BF16)
            dot32 = _t(do_ref[0, rows, :])
            delta[i] = jnp.sum(dot32 * o_ref[0, rows, :].T, axis=0, keepdims=True)
            dot_[i] = dot32.astype(BF16)

        for qi in range(nq):
            rows = slice(qi * QB, (qi + 1) * QB)
            q = q_ref[0, rows, :]
            dob = do_ref[0, rows, :]
            lse_q = lse_ref[0, 0, :, rows]
            delta_q = delta[qi]
            dqt[...] = jnp.zeros_like(dqt)

            def raw(kj, slot, q=q, qi=qi):
                st_buf[slot] = _mla_raw_t(k2, kj, q, QB)
                dp_buf[slot] = jnp.dot(kv_ref[0, _blk(kj, QB), NOPE:], dot_[qi], preferred_element_type=F32)

            raw(0, 0)

            def step(kj, carry, q=q, dob=dob, lse_q=lse_q, delta_q=delta_q, qi=qi, raw=raw):
                ks = pl.ds(pl.multiple_of(kj * QB, QB), QB)
                cur = lax.rem(kj, 2)
                st_raw, dp_raw = st_buf[cur], dp_buf[cur]
                raw(jnp.minimum(kj + 1, qi), 1 - cur)
                pt = jnp.exp(st_raw * scale + jnp.where(kj == qi, pen[...], 0.0) - lse_q)
                dst = (pt * (dp_raw - delta_q) * scale).astype(BF16)
                dkv_acc[ks, NOPE:] += jnp.dot(pt.astype(BF16), dob, preferred_element_type=F32)
                dk2 = jnp.dot(dst, q, preferred_element_type=F32)
                dkv_acc[ks, :NOPE] += dk2[:, :NOPE]
                dkr_ref[0, ks, :] += dk2[:, NOPE:]
                dqt[...] += jnp.dot(kt[kj], dst, preferred_element_type=F32)
                return carry

            lax.fori_loop(0, qi + 1, step, 0)
            dq = dqt[...].T
            dq_ref[0, rows, :NOPE] = dq[:, :NOPE].astype(dq_ref.dtype)
            dq_ref[0, rows, NOPE:] = _rope_tile_bwd(dq[:, NOPE:], ct_ref[rows, :], s1_ref[rows, :],
                                                    s2_ref[rows, :]).astype(dq_ref.dtype)

        dkv_ref[0] = dkv_acc[...].astype(dkv_ref.dtype)

    head = lambda w: pl.BlockSpec((1, S, w), lambda b, h: (b, 0, h))
    shared = pl.BlockSpec((1, S, LANE), lambda b, h: (b, 0, 0))
    blk = (2 * _nbytes((S, 256), BF16) + 2 * _nbytes((S, LANE), BF16) + _nbytes((S, LANE), F32)
           + 2 * _nbytes((S, 256), F32) + _nbytes((S, LANE), F32))
    scr = 3 * _nbytes((S, 256), BF16) + 14 * _nbytes((QB, QB), F32)
    return pl.pallas_call(
        body, name=name, grid=(B, HB),
        in_specs=[head(256), head(256), shared, head(LANE), head(LANE),
                  pl.BlockSpec((1, 1, 1, S), lambda b, h: (b, h, 0, 0))]
        + [pl.BlockSpec((S, LANE), lambda b, h: (0, 0))] * 3,
        out_specs=[head(256), head(256), shared],
        out_shape=[jax.ShapeDtypeStruct((B, S, W), BF16), jax.ShapeDtypeStruct((B, S, W), BF16),
                   jax.ShapeDtypeStruct((B, S, LANE), F32)],
        scratch_shapes=[pltpu.VMEM((S, 256), BF16), pltpu.VMEM((nq, 256, QB), BF16),
                        pltpu.VMEM((nq, NOPE, QB), BF16), pltpu.VMEM((nq, 1, QB), F32),
                        pltpu.VMEM((256, QB), F32), pltpu.VMEM((2, QB, QB), F32), pltpu.VMEM((2, QB, QB), F32),
                        pltpu.VMEM((QB, QB), F32), pltpu.VMEM((S, 256), F32)],
        compiler_params=_params(blk, scr),
    )(qf, kv, kr, do, o, lse, *tabs)


GROUP_STEPS = 2


def cast_group(name, ws, layers, idx, after=None):
    n = len(ws)
    n_in = n + (after is not None)

    def body(k_ref, *refs):
        for i in range(n):
            refs[n_in + i][...] = refs[i][...].astype(BF16)

    def spec_in(w, layer):
        return pl.BlockSpec((None, w.shape[1] // GROUP_STEPS, w.shape[2]), lambda r, k_ref: (layer, r, 0))

    def spec_out(w):
        return pl.BlockSpec((None, w.shape[1] // GROUP_STEPS, w.shape[2]), lambda r, k_ref: (k_ref[0], r, 0))

    return pl.pallas_call(
        body, name=name,
        grid_spec=pltpu.PrefetchScalarGridSpec(
            num_scalar_prefetch=1, grid=(GROUP_STEPS,),
            in_specs=([spec_in(w, l) for w, l in zip(ws, layers)]
                      + [pl.BlockSpec(memory_space=pl.ANY)] * (after is not None)),
            out_specs=[spec_out(w) for w in ws]),
        out_shape=[jax.ShapeDtypeStruct((N_CHIPS, *w.shape[1:]), BF16) for w in ws],
        compiler_params=_params(sum(_nbytes(w.shape[1:], F32) * 3 // 2 for w in ws) // GROUP_STEPS),
    )(idx, *ws, *([] if after is None else [after]))


def adamw(name, w, g, m, v):
    R, C = w.shape
    tr = _tile(R, max(8, (1 << 18) // C // 8 * 8), 8)
    c1 = 1.0 - ADAM_B1 ** ADAM_STEP
    c2 = 1.0 - ADAM_B2 ** ADAM_STEP

    def body(w_ref, g_ref, m_ref, v_ref, d_ref, mo_ref, vo_ref):
        gv = g_ref[...]
        mn = ADAM_B1 * m_ref[...] + (1.0 - ADAM_B1) * gv
        vn = ADAM_B2 * v_ref[...] + (1.0 - ADAM_B2) * (gv * gv)
        mo_ref[...] = mn
        vo_ref[...] = vn
        d_ref[...] = -ADAM_LR * ((mn / c1) / (jnp.sqrt(vn / c2) + ADAM_EPS) + ADAM_WD * w_ref[...])

    spec = pl.BlockSpec((tr, C), lambda r: (r, 0))
    return pl.pallas_call(
        body, name=name, grid=(R // tr,), in_specs=[spec] * 4, out_specs=[spec] * 3,
        out_shape=[jax.ShapeDtypeStruct((R, C), F32)] * 3,
        compiler_params=_params(7 * _nbytes((tr, C), F32), 4 * _nbytes((tr, C), F32)),
    )(w, g, m, v)


def half_sum_group(name, dws, landed, idx):
    n = len(dws)
    steps = GROUP_STEPS // 2

    def body(i_ref, *refs):
        for i in range(n):
            refs[2 * n + i][...] = (refs[i][...].astype(F32) + refs[n + i][...].astype(F32)).astype(BF16)

    def own(d):
        return pl.BlockSpec((None, None, d.shape[2] // steps, d.shape[3]), lambda k, r, i_ref: (k, i_ref[1], r, 0))

    def flat(d):
        return pl.BlockSpec((None, d.shape[2] // steps, d.shape[3]), lambda k, r, i_ref: (k, r, 0))

    return pl.pallas_call(
        body, name=name,
        grid_spec=pltpu.PrefetchScalarGridSpec(
            num_scalar_prefetch=1, grid=(N_CHIPS, steps),
            in_specs=[own(d) for d in dws] + [flat(d) for d in dws], out_specs=[flat(d) for d in dws]),
        out_shape=[jax.ShapeDtypeStruct((N_CHIPS, *d.shape[2:]), BF16) for d in dws],
        compiler_params=_params(sum(3 * _nbytes(d.shape[2:], BF16) for d in dws) // steps),
    )(idx, *dws, *landed)


def chip_sum_group(name, parts, landed, gbufs, layers, idx):
    n = len(parts)
    steps = GROUP_STEPS // 2

    def body(i_ref, *refs):
        for i in range(n):
            a, b = refs[i], refs[n + i]
            refs[3 * n + i][...] = ((a[...].astype(F32) + b[0].astype(F32)) + b[1].astype(F32)) + b[2].astype(F32)

    def mine(p):
        return pl.BlockSpec((None, p.shape[1] // steps, p.shape[2]), lambda r, i_ref: (i_ref[0], r, 0))

    def three(p):
        return pl.BlockSpec((3, p.shape[1] // steps, p.shape[2]), lambda r, i_ref: (0, r, 0))

    def out(p, layer):
        return pl.BlockSpec((None, None, p.shape[1] // steps, p.shape[2]), lambda r, i_ref: (layer, i_ref[1], r, 0))

    return pl.pallas_call(
        body, name=name,
        grid_spec=pltpu.PrefetchScalarGridSpec(
            num_scalar_prefetch=1, grid=(steps,),
            in_specs=[mine(p) for p in parts] + [three(p) for p in parts] + [pl.BlockSpec(memory_space=pl.ANY)] * n,
            out_specs=[out(p, l) for p, l in zip(parts, layers)]),
        out_shape=[jax.ShapeDtypeStruct(g.shape, F32) for g in gbufs],
        input_output_aliases={1 + 2 * n + i: i for i in range(n)},
        compiler_params=_params(sum(6 * _nbytes(p.shape[1:], BF16) for p in parts) // steps),
    )(idx, *parts, *landed, *gbufs)


ANY = pl.BlockSpec(memory_space=pl.ANY)


def _place():
    x, y, c = lax.axis_index("x"), lax.axis_index("y"), lax.axis_index("c")
    chips = [(1 - x, y), (x, 1 - y), (1 - x, 1 - y)]
    return x, y, c, chips


HBM = pl.BlockSpec(memory_space=pltpu.HBM)
SEM = pl.BlockSpec(memory_space=pltpu.SEMAPHORE)
EFFECT = pltpu.SideEffectType.DATAFLOW_SIDE_EFFECTING


def _in_hbm(a):
    return pltpu.with_memory_space_constraint(a, pltpu.HBM)


def _ici_copy(src, dst, send_sems, recv_sems, k, peer):
    return pltpu.make_async_remote_copy(src_ref=src, dst_ref=dst, send_sem=send_sems.at[k], recv_sem=recv_sems.at[k],
                                        device_id=peer, device_id_type=MESH)


def ici_start(name, bufs, lands, after, gather):
    n, nl = len(bufs), len(lands)

    def body(*refs):
        b_in = refs[:n]
        send_sems, recv_sems = refs[n + nl + 1], refs[n + nl + 2]
        b_out = refs[n + nl + 3:2 * n + nl + 3]
        l_out = refs[2 * n + nl + 3:2 * n + 2 * nl + 3]
        token = refs[-1]
        x, y, c, chips = _place()
        kme = 2 * x + y
        for i in range(n):
            for j in range(3):
                peer = (*chips[j], c)
                if gather:
                    _ici_copy(b_out[i].at[kme, c], b_out[i].at[kme, c], send_sems, recv_sems, 3 * i + j, peer).start()
                else:
                    kd = 2 * chips[j][0] + chips[j][1]
                    _ici_copy(b_out[i].at[kd], l_out[i].at[j], send_sems, recv_sems, 3 * i + j, peer).start()
        token[...] = jnp.zeros_like(token)

    arrays = [*bufs, *lands]
    outs = pl.pallas_call(
        body, name=name,
        in_specs=[HBM] * (n + nl) + [ANY],
        out_specs=(SEM, SEM, *[HBM] * (n + nl), pl.BlockSpec(memory_space=pltpu.VMEM)),
        out_shape=(pltpu.SemaphoreType.DMA((3 * n,)), pltpu.SemaphoreType.DMA((3 * n,)),
                   *[pltpu.HBM(a.shape, a.dtype) for a in arrays], jax.ShapeDtypeStruct((8, LANE), F32)),
        input_output_aliases={i: 2 + i for i in range(n + nl)},
        compiler_params=pltpu.CompilerParams(has_side_effects=EFFECT),
    )(*[_in_hbm(a) for a in arrays], after)
    return outs[0], outs[1], list(outs[2:2 + n]), list(outs[2 + n:2 + n + nl]), outs[-1]


def ici_wait(name, send_sems, recv_sems, bufs, lands, after, gather):
    n, nl = len(bufs), len(lands)

    def body(*refs):
        b_in, l_in = refs[:n], refs[n:n + nl]
        send_sems, recv_sems = refs[n + nl], refs[n + nl + 1]
        x, y, c, chips = _place()
        kme = 2 * x + y
        for i in range(n):
            for j in range(3):
                peer = (*chips[j], c)
                kj = 2 * chips[j][0] + chips[j][1]
                if gather:
                    _ici_copy(b_in[i].at[kme, c], b_in[i].at[kme, c], send_sems, recv_sems, 3 * i + j, peer).wait_send()
                    _ici_copy(b_in[i].at[kj, c], b_in[i].at[kj, c], send_sems, recv_sems, 3 * i + j, peer).wait_recv()
                else:
                    _ici_copy(b_in[i].at[kj], l_in[i].at[j], send_sems, recv_sems, 3 * i + j, peer).wait_send()
                    _ici_copy(b_in[i].at[kj], l_in[i].at[j], send_sems, recv_sems, 3 * i + j, peer).wait_recv()

    arrays = [*bufs, *lands]
    outs = pl.pallas_call(
        body, name=name,
        in_specs=[HBM] * (n + nl) + [SEM, SEM, ANY],
        out_specs=tuple([HBM] * (n + nl)),
        out_shape=tuple(pltpu.HBM(a.shape, a.dtype) for a in arrays),
        input_output_aliases={i: i for i in range(n + nl)},
        compiler_params=pltpu.CompilerParams(has_side_effects=EFFECT),
    )(*arrays, send_sems, recv_sems, after)
    return list(outs[:n]), list(outs[n:])


def gather_pair_pass(name, bufs):
    n = len(bufs)

    def body(*refs):
        b = refs[n:2 * n]
        send_sems, recv_sems = refs[2 * n:]
        x, y, c, chips = _place()
        sib = (x, y, 1 - c)

        def d2d(i, j, which):
            kj = 2 * chips[j][0] + chips[j][1]
            return _ici_copy(b[i].at[kj, which], b[i].at[kj, which], send_sems, recv_sems, 3 * i + j, sib)

        for i in range(n):
            for j in range(3):
                d2d(i, j, c).start()
        for i in range(n):
            for j in range(3):
                d2d(i, j, 1 - c).wait_recv()
        for i in range(n):
            for j in range(3):
                d2d(i, j, c).wait_send()

    return pl.pallas_call(
        body, name=name, in_specs=[ANY] * n, out_specs=[ANY] * n,
        out_shape=[jax.ShapeDtypeStruct(a.shape, a.dtype) for a in bufs],
        input_output_aliases={i: i for i in range(n)},
        scratch_shapes=[pltpu.SemaphoreType.DMA((3 * n,)), pltpu.SemaphoreType.DMA((3 * n,))],
    )(*bufs)


def pair_exchange(name, dws):
    n = len(dws)

    def body(*refs):
        ins, outs = refs[:n], refs[n:2 * n]
        send_sems, recv_sems = refs[2 * n:]
        x, y, c, _ = _place()
        copies = []
        for i in range(n):
            copies.append(pltpu.make_async_remote_copy(
                src_ref=ins[i].at[:, 1 - c], dst_ref=outs[i],
                send_sem=send_sems.at[i], recv_sem=recv_sems.at[i],
                device_id=(x, y, 1 - c), device_id_type=MESH))
            copies[i].start()
        for cp in copies:
            cp.wait_recv()
        for cp in copies:
            cp.wait_send()

    return pl.pallas_call(
        body, name=name, in_specs=[ANY] * n, out_specs=[ANY] * n,
        out_shape=[jax.ShapeDtypeStruct((N_CHIPS, *d.shape[2:]), d.dtype) for d in dws],
        scratch_shapes=[pltpu.SemaphoreType.DMA((n,)), pltpu.SemaphoreType.DMA((n,))],
    )(*dws)


def pair_assemble(gbufs):
    n = len(gbufs)

    def body(*refs):
        bufs = refs[n:2 * n]
        send_sems, recv_sems = refs[2 * n:]
        x, y, c, _ = _place()
        copies = []
        for i in range(n):
            copies.append(pltpu.make_async_remote_copy(
                src_ref=bufs[i].at[:, c], dst_ref=bufs[i].at[:, c],
                send_sem=send_sems.at[i], recv_sem=recv_sems.at[i],
                device_id=(x, y, 1 - c), device_id_type=MESH))
            copies[i].start()
        for i in range(n):
            pltpu.make_async_remote_copy(
                src_ref=bufs[i].at[:, 1 - c], dst_ref=bufs[i].at[:, 1 - c],
                send_sem=send_sems.at[i], recv_sem=recv_sems.at[i],
                device_id=(x, y, 1 - c), device_id_type=MESH).wait_recv()
        for cp in copies:
            cp.wait_send()

    return pl.pallas_call(
        body, name="grad_pair_assemble", in_specs=[ANY] * n, out_specs=[ANY] * n,
        out_shape=[jax.ShapeDtypeStruct(g.shape, g.dtype) for g in gbufs],
        input_output_aliases={i: i for i in range(n)},
        scratch_shapes=[pltpu.SemaphoreType.DMA((n,)), pltpu.SemaphoreType.DMA((n,))],
    )(*gbufs)


def all_reduce_small(vec):
    NR = vec.shape[0]
    flips = [(fx, fy, fc) for fx in (0, 1) for fy in (0, 1) for fc in (0, 1)][1:]

    def body(v_ref, o_ref, buf, send_sems, recv_sems):
        x, y, c, _ = _place()
        me = 4 * x + 2 * y + c
        buf[me] = v_ref[...]
        copies = []
        for j, (fx, fy, fc) in enumerate(flips):
            peer = (1 - x if fx else x, 1 - y if fy else y, 1 - c if fc else c)
            copies.append(pltpu.make_async_remote_copy(
                src_ref=v_ref, dst_ref=buf.at[me], send_sem=send_sems.at[j], recv_sem=recv_sems.at[j],
                device_id=peer, device_id_type=MESH))
            copies[j].start()
        for cp in copies:
            cp.wait_recv()
        for cp in copies:
            cp.wait_send()
        acc = buf[0]
        for d in range(1, 8):
            acc = acc + buf[d]
        o_ref[...] = acc

    return pl.pallas_call(
        body, name="all_reduce_small",
        in_specs=[pl.BlockSpec(memory_space=pltpu.VMEM)], out_specs=pl.BlockSpec(memory_space=pltpu.VMEM),
        out_shape=jax.ShapeDtypeStruct((NR, LANE), F32),
        scratch_shapes=[pltpu.VMEM((8, NR, LANE), F32), pltpu.SemaphoreType.DMA((7,)),
                        pltpu.SemaphoreType.DMA((7,))],
    )(vec)


def _pack(arrays):
    flat = jnp.concatenate([a.reshape(-1).astype(F32) for a in arrays])
    n = flat.shape[0]
    npad = -(-n // (8 * LANE)) * (8 * LANE)
    return jnp.pad(flat, (0, npad - n)).reshape(npad // LANE, LANE)


def _unpack(buf, like):
    flat = buf.reshape(-1)
    out, off = [], 0
    for a in like:
        out.append(flat[off:off + a.size].reshape(a.shape))
        off += a.size
    return out


def kernel(x, ffn1_norm, ffn1_w_in, ffn1_w_out, mix_norm, ffn2_norm, ffn2_w_in, ffn2_w_out, a_w_qkv, a_rel_bias, a_w_o, kv_norm, kv_w_down, kv_latent_norm, kv_w_up, b_w_dq, b_q_norm, b_w_uq, b_w_o, final_norm, loss_target, m_ffn1_norm, m_ffn1_w_in, m_ffn1_w_out, m_mix_norm, m_ffn2_norm, m_ffn2_w_in, m_ffn2_w_out, m_a_w_qkv, m_a_rel_bias, m_a_w_o, m_kv_norm, m_kv_w_down, m_kv_latent_norm, m_kv_w_up, m_b_w_dq, m_b_q_norm, m_b_w_uq, m_b_w_o, m_final_norm, v_ffn1_norm, v_ffn1_w_in, v_ffn1_w_out, v_mix_norm, v_ffn2_norm, v_ffn2_w_in, v_ffn2_w_out, v_a_w_qkv, v_a_rel_bias, v_a_w_o, v_kv_norm, v_kv_w_down, v_kv_latent_norm, v_kv_w_up, v_b_w_dq, v_b_q_norm, v_b_w_uq, v_b_w_o, v_final_norm):
    B, S, D = x.shape
    T = B * S
    HB = D // 128
    QL = b_q_norm.shape[-1]
    KVL = kv_latent_norm.shape[0]
    hpc = HB // N_CHIPS
    tabs = rope_tables(S)
    idx = jnp.stack([2 * lax.axis_index("x") + lax.axis_index("y"), lax.axis_index("c")]).astype(I32)

    def halves(a):
        return a.reshape(*a.shape[:-2], 2, a.shape[-2] // 2, a.shape[-1])

    def whole(a):
        return a.reshape(*a.shape[:-3], 2 * a.shape[-2], a.shape[-1])

    kv_w_down_p = jnp.pad(kv_w_down, ((0, 0), (0, LANE - ROPE)))[None]
    b_w_uq_p = jnp.pad(b_w_uq.reshape(1, QL, hpc, NOPE + ROPE),
                       ((0, 0), (0, 0), (0, 0), (0, LANE - ROPE))).reshape(1, QL, hpc * 256)
    sharded = [("ffn1_w_in", ffn1_w_in), ("ffn1_w_out", ffn1_w_out), ("ffn2_w_in", ffn2_w_in),
               ("ffn2_w_out", ffn2_w_out), ("a_w_qkv", a_w_qkv), ("a_w_o", a_w_o),
               ("kv_w_down", kv_w_down_p), ("kv_w_up", kv_w_up[None]), ("b_w_dq", b_w_dq),
               ("b_w_uq", b_w_uq_p), ("b_w_o", b_w_o)]
    names = [nm for nm, _ in sharded]
    shard_of = dict(sharded)
    W = {}

    gather_groups = [
        [("ffn1_w_in", 0)],
        [("ffn1_w_out", 0)],
        [("a_w_qkv", 0), ("a_w_o", 0)],
        [("ffn2_w_in", 0), ("ffn2_w_out", 0), ("kv_w_down", 0), ("kv_w_up", 0)],
        [("ffn1_w_in", 1), ("ffn1_w_out", 1), ("b_w_dq", 0), ("b_w_uq", 0), ("b_w_o", 0), ("ffn2_w_in", 1),
         ("ffn2_w_out", 1)]]

    own = {}

    def cast(g, after=None):
        keys = gather_groups[g]
        own.update(zip(keys, cast_group(f"cast_group_{g}", [shard_of[nm] for nm, _ in keys], [l for _, l in keys],
                                        idx, after=after)))

    def gather_start(g, after):
        keys = gather_groups[g]
        ss, rs, bufs, _, token = ici_start(f"gather_start_{g}", [halves(own[k]) for k in keys], [], after, True)
        return (g, ss, rs, bufs), token

    def gather_finish(state, after):
        g, ss, rs, bufs = state
        bufs, _ = ici_wait(f"gather_wait_{g}", ss, rs, bufs, [], after, True)
        full = gather_pair_pass(f"gather_pair_{g}", bufs)
        for k, w in zip(gather_groups[g], full):
            W[k] = whole(w)
        return full[0]

    def tied(a, token):
        return a + token[0, 0]

    def col(nm, l=0):
        return W[(nm, l)]

    def row(nm, l=0):
        w = W[(nm, l)]
        return w.reshape(N_CHIPS * w.shape[1], w.shape[2])

    bias = rel_bias_tile("rel_bias_tile", a_rel_bias[0])

    def ffn_fwd(tag, h, g, w_in, w_out):
        xn = rms_fwd(f"{tag}_norm", h, g)
        u, act = ffn_in_act(f"{tag}_in", xn, w_in)
        return mm_roww(f"{tag}_out", act, w_out, F32, res=h, alpha=0.5), (xn, u, act)

    h0 = x.reshape(T, D)
    for g in range(3):
        cast(g)
    st0, tok0 = gather_start(0, h0)
    st1, tok1 = gather_start(1, tok0)
    st2, tok2 = gather_start(2, tok1)
    for g in range(3, len(gather_groups)):
        cast(g, tok2)
    xn0 = rms_fwd("l0f1_norm", h0, tied(ffn1_norm[0], tok2))
    gather_finish(st0, xn0)
    u0, act0 = ffn_in_act("l0f1_in", xn0, col("ffn1_w_in", 0))
    gather_finish(st1, u0)
    h1 = mm_roww("l0f1_out", act0, row("ffn1_w_out", 0), F32, res=h0, alpha=0.5)
    sv_f1a = (xn0, u0, act0)
    done2 = gather_finish(st2, h1)
    st3, tok3 = gather_start(3, done2)
    st4, tok4 = gather_start(4, tok3)
    hn_a = rms_fwd("l0mix_norm", h1, tied(mix_norm[0], tok4))
    qkv = mm_colw("l0_qkv", hn_a, col("a_w_qkv"), BF16).reshape(B, S, 3 * D)
    o_a = attn_a_fwd("l0_attn", qkv, bias).reshape(T, D)
    h2 = mm_roww("l0_attn_out", o_a, row("a_w_o"), F32, res=h1)
    gather_finish(st3, h2)
    h3, sv_f2a = ffn_fwd("l0f2", h2, ffn2_norm[0], col("ffn2_w_in", 0), row("ffn2_w_out", 0))

    hkv = rms_fwd("kv_norm", h3, kv_norm)
    ckr = mm_roww("kv_down", hkv, row("kv_w_down"), F32)
    ckv, kr = kvprep_fwd("kv_prep", ckr, kv_latent_norm, tabs, B, S)
    kvb = mm_colw("kv_up", ckv, col("kv_w_up"), BF16).reshape(B, S, HB * 256)
    gather_finish(st4, kvb)

    h4, sv_f1b = ffn_fwd("l1f1", h3, ffn1_norm[1], col("ffn1_w_in", 1), row("ffn1_w_out", 1))
    hn_b = rms_fwd("l1mix_norm", h4, mix_norm[1])
    cqp = mm_roww("l1_dq", hn_b, row("b_w_dq"), F32)
    cq = rms_fwd("l1_q_norm", cqp, b_q_norm[0])
    qf = uq_rope("l1_uq", cq, col("b_w_uq"), tabs, S).reshape(B, S, HB * 256)
    o_b, lse = mla_fwd("l1_attn", qf, kvb, kr)
    h5 = mm_roww("l1_attn_out", o_b.reshape(T, HB * LANE), row("b_w_o"), F32, res=h4)
    h6, sv_f2b = ffn_fwd("l1f2", h5, ffn2_norm[1], col("ffn2_w_in", 1), row("ffn2_w_out", 1))

    dh, g_final, loss_part = loss_head("loss_head", h6, final_norm, loss_target.reshape(T, D))

    gw = {}
    gbufs = {nm: lax.empty(halves(w).shape, F32) for nm, w in sharded}

    def reduce_start(r, keys, after):
        dws = [halves(gw[k]) for k in keys]
        landed = pair_exchange(f"grad_pair_exchange_{r}", dws)
        parts = half_sum_group(f"half_sum_{r}", dws, landed, idx)
        lands = [lax.empty((3, *p.shape[1:]), p.dtype) for p in parts]
        ss, rs, parts, lands, token = ici_start(f"reduce_start_{r}", parts, lands, after, False)
        return (r, keys, ss, rs, parts, lands), token

    def reduce_finish(state, after):
        r, keys, ss, rs, parts, lands = state
        parts, lands = ici_wait(f"reduce_wait_{r}", ss, rs, parts, lands, after, False)
        done = chip_sum_group(f"chip_sum_{r}", parts, lands, [gbufs[nm] for nm, _ in keys], [l for _, l in keys], idx)
        gbufs.update(zip([nm for nm, _ in keys], done))
        return done[0]

    def ffn_bwd(tag, dh, h_in, g, w_in, w_out, saved, key_in, key_out, after=None, then=None):
        xn, u, act = saved
        du = ffn_dact(f"{tag}_dact", dh, w_out, u, after=after)
        dwo = mm_droww(f"{tag}_dwout", act, dh, alpha=0.5)
        gw[key_out] = dwo.reshape(N_CHIPS, dwo.shape[0] // N_CHIPS, dwo.shape[1])
        gw[key_in] = mm_dcolw(f"{tag}_dwin", xn, du, pair_layout=True)
        token = then(du) if then is not None else None
        return dx_norm_bwd(f"{tag}_dxn", du, w_in, h_in, g, dres=dh, pair_layout=True, after=token)

    def chip_major(dw):
        return dw.reshape(N_CHIPS, dw.shape[0] // N_CHIPS, dw.shape[1])

    dh, g_f2b = ffn_bwd("l1f2b", dh, h5, ffn2_norm[1], col("ffn2_w_in", 1), row("ffn2_w_out", 1), sv_f2b,
                        ("ffn2_w_in", 1), ("ffn2_w_out", 1))
    red0, rtok0 = reduce_start(0, [("ffn2_w_in", 1), ("ffn2_w_out", 1)], dh)
    do_b = mm_roww_t("l1_attn_do", dh, row("b_w_o"), BF16, after=rtok0).reshape(B, S, HB * LANE)
    gw[("b_w_o", 0)] = chip_major(mm_droww("l1_attn_dwo", o_b.reshape(T, HB * LANE), dh))
    dqpre, dkv, dkr = mla_bwd("l1_attn_bwd", qf, kvb, kr, do_b, o_b, lse, tabs)
    dqpre = dqpre.reshape(T, HB * 256)
    gw[("b_w_uq", 0)] = mm_dcolw("l1_dwuq", cq, dqpre)
    dcqp, g_qn = dx_norm_bwd("l1_dcq", dqpre, col("b_w_uq"), cqp, b_q_norm[0])
    gw[("b_w_dq", 0)] = chip_major(mm_droww("l1_dwdq", hn_b, dcqp))
    dhn = mm_roww_t("l1_dhn", dcqp, row("b_w_dq"), F32)
    dh, g_mixb = rms_bwd("l1_dmix", h4, mix_norm[1], dhn, dres=dh)
    dh, g_f1b = ffn_bwd("l1f1b", dh, h3, ffn1_norm[1], col("ffn1_w_in", 1), row("ffn1_w_out", 1), sv_f1b,
                        ("ffn1_w_in", 1), ("ffn1_w_out", 1))
    fin0 = reduce_finish(red0, dh)
    red1, rtok1 = reduce_start(1, [("b_w_o", 0), ("b_w_uq", 0), ("b_w_dq", 0), ("ffn1_w_in", 1), ("ffn1_w_out", 1)], fin0)
    dkv2 = dkv.reshape(T, HB * 256)
    gw[("kv_w_up", 0)] = mm_dcolw("kv_dwup", ckv, dkv2, after=rtok1)
    dckv = mm_colw_t("kv_dckv", dkv2, col("kv_w_up"), F32, after=rtok1)
    dckr, g_lat = kvprep_bwd("kv_prep_bwd", ckr, kv_latent_norm, dckv, dkr, tabs, B, S)
    gw[("kv_w_down", 0)] = chip_major(mm_droww("kv_dwdown", hkv, dckr))
    dhkv = mm_roww_t("kv_dhkv", dckr, row("kv_w_down"), F32)
    dh, g_kvn = rms_bwd("kv_dnorm", h3, kv_norm, dhkv, dres=dh)
    dh, g_f2a = ffn_bwd("l0f2b", dh, h2, ffn2_norm[0], col("ffn2_w_in", 0), row("ffn2_w_out", 0), sv_f2a,
                        ("ffn2_w_in", 0), ("ffn2_w_out", 0))
    do_a = mm_roww_t("l0_attn_do", dh, row("a_w_o"), BF16).reshape(B, S, D)
    gw[("a_w_o", 0)] = chip_major(mm_droww("l0_attn_dwo", o_a, dh))
    dqkv, dbias = attn_a_bwd("l0_attn_bwd", qkv, do_a, bias)
    dqkv = dqkv.reshape(T, 3 * D)
    gw[("a_w_qkv", 0)] = mm_dcolw("l0_dwqkv", hn_a, dqkv)
    dh, g_mixa = dx_norm_bwd("l0_dhn", dqkv, col("a_w_qkv"), h1, mix_norm[0], dres=dh)
    fin1 = reduce_finish(red1, dh)
    red2, rtok2 = reduce_start(2, [("kv_w_up", 0), ("kv_w_down", 0), ("ffn2_w_in", 0), ("ffn2_w_out", 0),
                                   ("a_w_o", 0), ("a_w_qkv", 0)], fin1)
    last = {}

    def last_group(du):
        fin2 = reduce_finish(red2, gw[("ffn1_w_in", 0)])
        last["red"], token = reduce_start(3, [("ffn1_w_in", 0), ("ffn1_w_out", 0)], fin2)
        return token

    dh, g_f1a = ffn_bwd("l0f1b", dh, h0, ffn1_norm[0], col("ffn1_w_in", 0), row("ffn1_w_out", 0), sv_f1a,
                        ("ffn1_w_in", 0), ("ffn1_w_out", 0), after=rtok2, then=last_group)
    grad_x = dh.reshape(B, S, D)
    g_rel = rel_bias_grad("rel_bias_grad", dbias)[:, :2 * MAX_REL + 1][None]
    reduce_finish(last["red"], dh)

    full = [whole(g) for g in pair_assemble([gbufs[nm] for nm in names])]
    G = {nm: g for (nm, _), g in zip(sharded, full)}
    G["kv_w_down"] = G["kv_w_down"][0, :, :KVL + ROPE]
    G["kv_w_up"] = G["kv_w_up"][0]
    G["b_w_uq"] = G["b_w_uq"].reshape(1, QL, hpc, 256)[..., :NOPE + ROPE].reshape(b_w_uq.shape)

    small = [("ffn1_norm", jnp.stack([g_f1a, g_f1b])), ("mix_norm", jnp.stack([g_mixa, g_mixb])),
             ("ffn2_norm", jnp.stack([g_f2a, g_f2b])), ("a_rel_bias", g_rel), ("kv_norm", g_kvn),
             ("kv_latent_norm", g_lat), ("b_q_norm", g_qn[None]), ("final_norm", g_final)]
    red = all_reduce_small(_pack([loss_part] + [g for _, g in small]))
    unpacked = _unpack(red, [loss_part] + [g for _, g in small])
    loss = unpacked[0][0, 0]
    for (nm, _), g in zip(small, unpacked[1:]):
        G[nm] = g

    given = dict(ffn1_norm=(ffn1_norm, m_ffn1_norm, v_ffn1_norm), ffn1_w_in=(ffn1_w_in, m_ffn1_w_in, v_ffn1_w_in),
                 ffn1_w_out=(ffn1_w_out, m_ffn1_w_out, v_ffn1_w_out), mix_norm=(mix_norm, m_mix_norm, v_mix_norm),
                 ffn2_norm=(ffn2_norm, m_ffn2_norm, v_ffn2_norm), ffn2_w_in=(ffn2_w_in, m_ffn2_w_in, v_ffn2_w_in),
                 ffn2_w_out=(ffn2_w_out, m_ffn2_w_out, v_ffn2_w_out), a_w_qkv=(a_w_qkv, m_a_w_qkv, v_a_w_qkv),
                 a_rel_bias=(a_rel_bias, m_a_rel_bias, v_a_rel_bias), a_w_o=(a_w_o, m_a_w_o, v_a_w_o),
                 kv_norm=(kv_norm, m_kv_norm, v_kv_norm), kv_w_down=(kv_w_down, m_kv_w_down, v_kv_w_down),
                 kv_latent_norm=(kv_latent_norm, m_kv_latent_norm, v_kv_latent_norm),
                 kv_w_up=(kv_w_up, m_kv_w_up, v_kv_w_up), b_w_dq=(b_w_dq, m_b_w_dq, v_b_w_dq),
                 b_q_norm=(b_q_norm, m_b_q_norm, v_b_q_norm), b_w_uq=(b_w_uq, m_b_w_uq, v_b_w_uq),
                 b_w_o=(b_w_o, m_b_w_o, v_b_w_o), final_norm=(final_norm, m_final_norm, v_final_norm))
    order = list(given)
    delta, new_m, new_v = {}, {}, {}
    small_names = [nm for nm, _ in small]
    packed = [_pack([given[nm][k] for nm in small_names]) for k in range(3)]
    outs = adamw("adamw_small", packed[0], _pack([G[nm] for nm in small_names]), packed[1], packed[2])
    for dst, buf in zip((delta, new_m, new_v), outs):
        for nm, a in zip(small_names, _unpack(buf, [given[nm][0] for nm in small_names])):
            dst[nm] = a
    for nm, _ in sharded:
        w, m, v = given[nm]
        g = G[nm].reshape(w.shape)
        G[nm] = g
        two = lambda a: a.reshape(-1, a.shape[-1])
        d_, m_, v_ = adamw(f"adamw_{nm}", two(w), two(g), two(m), two(v))
        delta[nm], new_m[nm], new_v[nm] = d_.reshape(w.shape), m_.reshape(w.shape), v_.reshape(w.shape)

    return (loss, grad_x, *[G[n] for n in order], *[delta[n] for n in order],
            *[new_m[n] for n in order], *[new_v[n] for n in order])
```
